```python
import jax, jax.numpy as jnp
from jax import lax
import numpy as np

D_MODEL = 1024
BATCH = 8
SEQ = 8192
DEPTH = 2

GRID_W = 64
CTX_LEN = 256
N_MIXERS = 2
EXPAND = 2
D_INNER = EXPAND * D_MODEL
LRU_BLOCKS = 16
LRU_BLOCK = D_INNER // LRU_BLOCKS
CONV_W = 4
CONV_LEFT = 2
LRU_C = 8.0
POOL_WINDOWS = (2, 4, 8, 16)
N_POOL_GROUPS = len(POOL_WINDOWS)
POOL_GROUP = D_INNER // N_POOL_GROUPS
N_A_LAYERS = (DEPTH + 1) // 2
N_B_LAYERS = DEPTH // 2
ALPHA = float((2 * DEPTH) ** 0.25)
BETA = float((8 * DEPTH) ** -0.25)
LN_EPS = 1e-5

kernel_name = "hybrid_rglru_pool_deepnorm_prefix"


def _layer_norm(v, g, b):
    vf = v.astype(jnp.float32)
    mu = jnp.mean(vf, axis=-1, keepdims=True)
    var = jnp.mean(jnp.square(vf - mu), axis=-1, keepdims=True)
    y = (vf - mu) * lax.rsqrt(var + LN_EPS) * g.astype(jnp.float32) + b.astype(jnp.float32)
    return y.astype(v.dtype)


def _adaln(cvec, w_mod, b_mod):
    m = jax.nn.silu(cvec) @ w_mod + b_mod
    shift, scale, gate = jnp.split(m, 3, axis=-1)
    return shift, scale, gate


def _centred_dwconv(u, w, b):
    L = u.shape[1]
    up = jnp.pad(u, ((0, 0), (CONV_LEFT, CONV_W - 1 - CONV_LEFT), (0, 0)))
    out = up[:, 0:L] * w[0]
    for k in range(1, CONV_W):
        out = out + up[:, k:k + L] * w[k]
    return out + b


def _lru_coeffs(uf, wa, ba, wx, bx, lam):
    bn, L, _ = uf.shape
    ub = uf.reshape(bn, L, LRU_BLOCKS, LRU_BLOCK)
    r = jax.nn.sigmoid(jnp.einsum('blnh,nhk->blnk', ub, wa.astype(jnp.float32)).reshape(bn, L, D_INNER) + ba.astype(jnp.float32))
    i = jax.nn.sigmoid(jnp.einsum('blnh,nhk->blnk', ub, wx.astype(jnp.float32)).reshape(bn, L, D_INNER) + bx.astype(jnp.float32))
    log_a = LRU_C * r * jax.nn.log_sigmoid(lam.astype(jnp.float32))
    a = jnp.exp(log_a)
    drive = jnp.sqrt(-jnp.expm1(2.0 * log_a)) * (i * uf)
    return a, drive


def _linear_scan(a, b, h0, reverse):
    if h0 is not None:
        if reverse:
            b = b.at[:, -1].add(a[:, -1] * h0)
        else:
            b = b.at[:, 0].add(a[:, 0] * h0)

    def combine(left, right):
        a1, b1 = left
        a2, b2 = right
        return a1 * a2, a2 * b1 + b2

    _, h = lax.associative_scan(combine, (a, b), reverse=reverse, axis=1)
    return h


def _rglru(u, wa, ba, wx, bx, lam, h0_f, h0_b):
    uf = u.astype(jnp.float32)
    af, df = _lru_coeffs(uf, wa[0], ba[0], wx[0], bx[0], lam[0])
    ab, db = _lru_coeffs(uf, wa[1], ba[1], wx[1], bx[1], lam[1])
    h_fwd = _linear_scan(af, df, h0_f, False)
    h_bwd = _linear_scan(ab, db, h0_b, True)
    return h_fwd, h_bwd


def _window_mean(v, w, axis):
    n = v.shape[axis]
    cs = jnp.cumsum(v.astype(jnp.float32), axis=axis)
    pad = [(0, 0)] * v.ndim
    pad[axis] = (1, 0)
    cs = jnp.pad(cs, pad)
    t = np.arange(n)
    lo = np.clip(t - w // 2, 0, n)
    hi = np.clip(t + w // 2, 0, n)
    s = jnp.take(cs, hi, axis=axis) - jnp.take(cs, lo, axis=axis)
    cnt_shape = [1] * v.ndim
    cnt_shape[axis] = n
    cnt = jnp.asarray((hi - lo).astype(np.float32)).reshape(cnt_shape)
    return s / cnt


def _pool_grid(u, w_p, scale, rows):
    bn = u.shape[0]
    ug = u.reshape(bn, rows, GRID_W, D_INNER)
    outs = []
    for k, w in enumerate(POOL_WINDOWS):
        seg = ug[..., k * POOL_GROUP:(k + 1) * POOL_GROUP]
        m = _window_mean(_window_mean(seg, w, 2), w, 1)
        d = (m - seg.astype(jnp.float32)).astype(u.dtype)
        outs.append(jnp.einsum('brcg,gh->brch', d, w_p[k]))
    y = jnp.concatenate(outs, axis=-1).reshape(bn, rows * GRID_W, D_INNER)
    return y * scale


def _pool_seq(u, w_p, scale):
    outs = []
    for k, w in enumerate(POOL_WINDOWS):
        seg = u[..., k * POOL_GROUP:(k + 1) * POOL_GROUP]
        d = (_window_mean(seg, w, 1) - seg.astype(jnp.float32)).astype(u.dtype)
        outs.append(jnp.einsum('blg,gh->blh', d, w_p[k]))
    return jnp.concatenate(outs, axis=-1) * scale


def _ctx_needed_after(i):
    return any(j % N_MIXERS == 0 for j in range(i + 1, DEPTH))


def _fwd_setup_inputs(seed: int = 0) -> dict:
    key = jax.random.key(seed)
    ks = jax.random.split(key, 20)
    f32 = jnp.float32
    x = jax.random.normal(ks[0], (BATCH, SEQ, D_MODEL), f32)
    c = jax.random.normal(ks[1], (BATCH, D_MODEL), f32)
    ctx = jax.random.normal(ks[2], (BATCH, CTX_LEN, D_MODEL), f32)
    c_ctx = jax.random.normal(ks[3], (D_MODEL,), f32)
    w_mod = jax.random.normal(ks[4], (DEPTH, D_MODEL, 3 * D_MODEL), f32) * (0.5 * D_MODEL ** -0.5)
    b_mod = jax.random.normal(ks[5], (DEPTH, 3 * D_MODEL), f32) * 0.02
    w_in = jax.random.normal(ks[6], (DEPTH, D_MODEL, 2 * D_INNER), f32) * D_MODEL ** -0.5
    w_out = jax.random.normal(ks[7], (DEPTH, D_INNER, D_MODEL), f32) * (D_INNER ** -0.5 * BETA)
    ln_g = 1.0 + 0.02 * jax.random.normal(ks[8], (DEPTH, D_MODEL), f32)
    ln_b = 0.02 * jax.random.normal(ks[9], (DEPTH, D_MODEL), f32)
    conv_w = jax.random.normal(ks[10], (N_A_LAYERS, CONV_W, D_INNER), f32) * CONV_W ** -0.5
    conv_b = 0.02 * jax.random.normal(ks[11], (N_A_LAYERS, D_INNER), f32)
    lru_wa = jax.random.normal(ks[12], (N_A_LAYERS, 2, LRU_BLOCKS, LRU_BLOCK, LRU_BLOCK), f32) * LRU_BLOCK ** -0.5
    lru_ba = 0.02 * jax.random.normal(ks[13], (N_A_LAYERS, 2, D_INNER), f32)
    lru_wx = jax.random.normal(ks[14], (N_A_LAYERS, 2, LRU_BLOCKS, LRU_BLOCK, LRU_BLOCK), f32) * LRU_BLOCK ** -0.5
    lru_bx = 0.02 * jax.random.normal(ks[15], (N_A_LAYERS, 2, D_INNER), f32)
    a_pow_c = jax.random.uniform(ks[16], (N_A_LAYERS, 2, D_INNER), f32, minval=0.9, maxval=0.999)
    a0 = a_pow_c ** (1.0 / LRU_C)
    lru_lam = jnp.log(a0) - jnp.log1p(-a0)
    pool_w = jax.random.normal(ks[17], (N_B_LAYERS, N_POOL_GROUPS, POOL_GROUP, POOL_GROUP), f32) * POOL_GROUP ** -0.5
    pool_scale = 1.0 + 0.02 * jax.random.normal(ks[18], (N_B_LAYERS, D_INNER), f32)
    return {"x": x, "c": c, "ctx": ctx, "c_ctx": c_ctx, "w_mod": w_mod, "b_mod": b_mod,
            "w_in": w_in, "w_out": w_out, "ln_g": ln_g, "ln_b": ln_b,
            "conv_w": conv_w, "conv_b": conv_b, "lru_wa": lru_wa, "lru_ba": lru_ba,
            "lru_wx": lru_wx, "lru_bx": lru_bx, "lru_lam": lru_lam,
            "pool_w": pool_w, "pool_scale": pool_scale}


def _fwd_reference(x, c, ctx, c_ctx, w_mod, b_mod, w_in, w_out, ln_g, ln_b,
              conv_w, conv_b, lru_wa, lru_ba, lru_wx, lru_bx, lru_lam,
              pool_w, pool_scale):
    rows = x.shape[1] // GRID_W
    xc = ctx
    for i in range(DEPTH):
        kind = i % N_MIXERS
        j = i // N_MIXERS
        ctx_out = _ctx_needed_after(i)
        sh, sc, gt = _adaln(c, w_mod[i], b_mod[i])
        h = x * (1.0 + sc[:, None]) + sh[:, None]
        u, g = jnp.split(h @ w_in[i], 2, axis=-1)
        if kind == 0 or ctx_out:
            shc, scc, gtc = _adaln(c_ctx, w_mod[i], b_mod[i])
            hc = xc * (1.0 + scc) + shc
            if ctx_out:
                uc, gc = jnp.split(hc @ w_in[i], 2, axis=-1)
            else:
                uc = hc @ w_in[i][:, :D_INNER]
        if kind == 0:
            uc = _centred_dwconv(uc, conv_w[j], conv_b[j])
            hcf, hcb = _rglru(uc, lru_wa[j], lru_ba[j], lru_wx[j], lru_bx[j], lru_lam[j], None, None)
            u = _centred_dwconv(u, conv_w[j], conv_b[j])
            hf, hb = _rglru(u, lru_wa[j], lru_ba[j], lru_wx[j], lru_bx[j], lru_lam[j],
                            hcf[:, -1], hcb[:, 0])
            y = (hf + hb).astype(x.dtype)
            if ctx_out:
                yc = (hcf + hcb).astype(xc.dtype)
        else:
            y = _pool_grid(u, pool_w[j], pool_scale[j], rows)
            if ctx_out:
                yc = _pool_seq(uc, pool_w[j], pool_scale[j])
        branch = (y * jax.nn.silu(g)) @ w_out[i]
        x = _layer_norm(ALPHA * x + gt[:, None] * branch, ln_g[i], ln_b[i])
        if ctx_out:
            branch_c = (yc * jax.nn.silu(gc)) @ w_out[i]
            xc = _layer_norm(ALPHA * xc + gtc * branch_c, ln_g[i], ln_b[i])
    return x


import jax as _jax
import jax.numpy as _jnp

TWIN_FORMAT = 'train_step'
FWD_PARAMS = ['x', 'c', 'ctx', 'c_ctx', 'w_mod', 'b_mod', 'w_in', 'w_out', 'ln_g', 'ln_b', 'conv_w', 'conv_b', 'lru_wa', 'lru_ba', 'lru_wx', 'lru_bx', 'lru_lam', 'pool_w', 'pool_scale']
TWIN_WEIGHTS = ['c_ctx', 'w_mod', 'b_mod', 'w_in', 'w_out', 'ln_g', 'ln_b', 'conv_w', 'conv_b', 'lru_wa', 'lru_ba', 'lru_wx', 'lru_bx', 'lru_lam', 'pool_w', 'pool_scale']
TWIN_DIFF_INPUT = 'x'
TWIN_INPUTS = ['x', 'c', 'ctx', 'c_ctx', 'w_mod', 'b_mod', 'w_in', 'w_out', 'ln_g', 'ln_b', 'conv_w', 'conv_b', 'lru_wa', 'lru_ba', 'lru_wx', 'lru_bx', 'lru_lam', 'pool_w', 'pool_scale', 'loss_target', 'm_c_ctx', 'm_w_mod', 'm_b_mod', 'm_w_in', 'm_w_out', 'm_ln_g', 'm_ln_b', 'm_conv_w', 'm_conv_b', 'm_lru_wa', 'm_lru_ba', 'm_lru_wx', 'm_lru_bx', 'm_lru_lam', 'm_pool_w', 'm_pool_scale', 'v_c_ctx', 'v_w_mod', 'v_b_mod', 'v_w_in', 'v_w_out', 'v_ln_g', 'v_ln_b', 'v_conv_w', 'v_conv_b', 'v_lru_wa', 'v_lru_ba', 'v_lru_wx', 'v_lru_bx', 'v_lru_lam', 'v_pool_w', 'v_pool_scale']
TWIN_OUTPUTS = ['loss', 'grad_x', 'grad_c_ctx', 'grad_w_mod', 'grad_b_mod', 'grad_w_in', 'grad_w_out', 'grad_ln_g', 'grad_ln_b', 'grad_conv_w', 'grad_conv_b', 'grad_lru_wa', 'grad_lru_ba', 'grad_lru_wx', 'grad_lru_bx', 'grad_lru_lam', 'grad_pool_w', 'grad_pool_scale', 'delta_c_ctx', 'delta_w_mod', 'delta_b_mod', 'delta_w_in', 'delta_w_out', 'delta_ln_g', 'delta_ln_b', 'delta_conv_w', 'delta_conv_b', 'delta_lru_wa', 'delta_lru_ba', 'delta_lru_wx', 'delta_lru_bx', 'delta_lru_lam', 'delta_pool_w', 'delta_pool_scale', 'new_m_c_ctx', 'new_m_w_mod', 'new_m_b_mod', 'new_m_w_in', 'new_m_w_out', 'new_m_ln_g', 'new_m_ln_b', 'new_m_conv_w', 'new_m_conv_b', 'new_m_lru_wa', 'new_m_lru_ba', 'new_m_lru_wx', 'new_m_lru_bx', 'new_m_lru_lam', 'new_m_pool_w', 'new_m_pool_scale', 'new_v_c_ctx', 'new_v_w_mod', 'new_v_b_mod', 'new_v_w_in', 'new_v_w_out', 'new_v_ln_g', 'new_v_ln_b', 'new_v_conv_w', 'new_v_conv_b', 'new_v_lru_wa', 'new_v_lru_ba', 'new_v_lru_wx', 'new_v_lru_bx', 'new_v_lru_lam', 'new_v_pool_w', 'new_v_pool_scale']
TWIN_LEAF_KINDS = {'loss': 'loss', 'grad_x': 'grad_x', 'grad_c_ctx': 'grad_w', 'grad_w_mod': 'grad_w', 'grad_b_mod': 'grad_w', 'grad_w_in': 'grad_w', 'grad_w_out': 'grad_w', 'grad_ln_g': 'grad_w', 'grad_ln_b': 'grad_w', 'grad_conv_w': 'grad_w', 'grad_conv_b': 'grad_w', 'grad_lru_wa': 'grad_w', 'grad_lru_ba': 'grad_w', 'grad_lru_wx': 'grad_w', 'grad_lru_bx': 'grad_w', 'grad_lru_lam': 'grad_w', 'grad_pool_w': 'grad_w', 'grad_pool_scale': 'grad_w', 'delta_c_ctx': 'delta_w', 'delta_w_mod': 'delta_w', 'delta_b_mod': 'delta_w', 'delta_w_in': 'delta_w', 'delta_w_out': 'delta_w', 'delta_ln_g': 'delta_w', 'delta_ln_b': 'delta_w', 'delta_conv_w': 'delta_w', 'delta_conv_b': 'delta_w', 'delta_lru_wa': 'delta_w', 'delta_lru_ba': 'delta_w', 'delta_lru_wx': 'delta_w', 'delta_lru_bx': 'delta_w', 'delta_lru_lam': 'delta_w', 'delta_pool_w': 'delta_w', 'delta_pool_scale': 'delta_w', 'new_m_c_ctx': 'new_m', 'new_m_w_mod': 'new_m', 'new_m_b_mod': 'new_m', 'new_m_w_in': 'new_m', 'new_m_w_out': 'new_m', 'new_m_ln_g': 'new_m', 'new_m_ln_b': 'new_m', 'new_m_conv_w': 'new_m', 'new_m_conv_b': 'new_m', 'new_m_lru_wa': 'new_m', 'new_m_lru_ba': 'new_m', 'new_m_lru_wx': 'new_m', 'new_m_lru_bx': 'new_m', 'new_m_lru_lam': 'new_m', 'new_m_pool_w': 'new_m', 'new_m_pool_scale': 'new_m', 'new_v_c_ctx': 'new_v', 'new_v_w_mod': 'new_v', 'new_v_b_mod': 'new_v', 'new_v_w_in': 'new_v', 'new_v_w_out': 'new_v', 'new_v_ln_g': 'new_v', 'new_v_ln_b': 'new_v', 'new_v_conv_w': 'new_v', 'new_v_conv_b': 'new_v', 'new_v_lru_wa': 'new_v', 'new_v_lru_ba': 'new_v', 'new_v_lru_wx': 'new_v', 'new_v_lru_bx': 'new_v', 'new_v_lru_lam': 'new_v', 'new_v_pool_w': 'new_v', 'new_v_pool_scale': 'new_v'}


def _forward(args):
    return _fwd_reference(*[args[k] for k in FWD_PARAMS])


def _output_shape():
    def fwd():
        inp = _fwd_setup_inputs(0)
        return _fwd_reference(*[inp[k] for k in FWD_PARAMS])
    out = _jax.eval_shape(fwd)
    return out.shape, out.dtype

N_MICROBATCH = 1
ADAM_LR = 0.001
ADAM_B1 = 0.9
ADAM_B2 = 0.999
ADAM_EPS = 1e-08
ADAM_WD = 0.01
ADAM_STEP = 10
PER_EXAMPLE_BATCH_AXIS = {'x': 0, 'c': 0, 'ctx': 0, 'loss_target': 0}
SHARED_INPUTS = []
_WEIGHT_DTYPES = {'c_ctx': _jnp.float32, 'w_mod': _jnp.float32, 'b_mod': _jnp.float32, 'w_in': _jnp.float32, 'w_out': _jnp.float32, 'ln_g': _jnp.float32, 'ln_b': _jnp.float32, 'conv_w': _jnp.float32, 'conv_b': _jnp.float32, 'lru_wa': _jnp.float32, 'lru_ba': _jnp.float32, 'lru_wx': _jnp.float32, 'lru_bx': _jnp.float32, 'lru_lam': _jnp.float32, 'pool_w': _jnp.float32, 'pool_scale': _jnp.float32}
MOMENT_SCALE = {'c_ctx': 1.796835e-02, 'w_mod': 9.167257e-02, 'b_mod': 1.462061e-01, 'w_in': 3.742374e-02, 'w_out': 1.067183e-01, 'ln_g': 4.533816e+01, 'ln_b': 2.015673e+00, 'conv_w': 5.671447e-02, 'conv_b': 1.645311e-01, 'lru_wa': 3.090405e-03, 'lru_ba': 4.884096e-03, 'lru_wx': 6.092276e-03, 'lru_bx': 1.143255e-02, 'lru_lam': 1.287638e-02, 'pool_w': 1.254150e-02, 'pool_scale': 1.349908e-02}


def _to_microbatches(a, axis):
    t = _jnp.moveaxis(a, axis, 0)
    t = t.reshape((N_MICROBATCH, t.shape[0] // N_MICROBATCH) + t.shape[1:])
    return _jnp.moveaxis(t, 1, axis + 1)


def setup_inputs(seed: int = 0) -> dict:
    inp = _fwd_setup_inputs(seed)
    key = _jax.random.fold_in(_jax.random.key(seed), 7919)
    shape, _ = _output_shape()
    out = dict(inp)
    out["loss_target"] = _jax.random.normal(_jax.random.fold_in(key, 0), shape, _jnp.float32)
    for i, name in enumerate(TWIN_WEIGHTS):
        w = inp[name].astype(_jnp.float32)
        if MOMENT_SCALE is None:
            s = _jnp.sqrt(_jnp.mean(_jnp.square(w)) + 1e-30)
        else:
            s = MOMENT_SCALE[name]
        km, kv = _jax.random.split(_jax.random.fold_in(key, i + 1))
        out[name] = w
        out["m_" + name] = s * _jax.random.normal(km, w.shape, _jnp.float32)
        out["v_" + name] = (s * s) * _jax.random.uniform(kv, w.shape, _jnp.float32, 0.5, 1.5)
    if N_MICROBATCH > 1:
        for name, axis in PER_EXAMPLE_BATCH_AXIS.items():
            out[name] = _to_microbatches(out[name], axis)
    return {'x': out['x'], 'c': out['c'], 'ctx': out['ctx'], 'c_ctx': out['c_ctx'], 'w_mod': out['w_mod'], 'b_mod': out['b_mod'], 'w_in': out['w_in'], 'w_out': out['w_out'], 'ln_g': out['ln_g'], 'ln_b': out['ln_b'], 'conv_w': out['conv_w'], 'conv_b': out['conv_b'], 'lru_wa': out['lru_wa'], 'lru_ba': out['lru_ba'], 'lru_wx': out['lru_wx'], 'lru_bx': out['lru_bx'], 'lru_lam': out['lru_lam'], 'pool_w': out['pool_w'], 'pool_scale': out['pool_scale'], 'loss_target': out['loss_target'], 'm_c_ctx': out['m_c_ctx'], 'm_w_mod': out['m_w_mod'], 'm_b_mod': out['m_b_mod'], 'm_w_in': out['m_w_in'], 'm_w_out': out['m_w_out'], 'm_ln_g': out['m_ln_g'], 'm_ln_b': out['m_ln_b'], 'm_conv_w': out['m_conv_w'], 'm_conv_b': out['m_conv_b'], 'm_lru_wa': out['m_lru_wa'], 'm_lru_ba': out['m_lru_ba'], 'm_lru_wx': out['m_lru_wx'], 'm_lru_bx': out['m_lru_bx'], 'm_lru_lam': out['m_lru_lam'], 'm_pool_w': out['m_pool_w'], 'm_pool_scale': out['m_pool_scale'], 'v_c_ctx': out['v_c_ctx'], 'v_w_mod': out['v_w_mod'], 'v_b_mod': out['v_b_mod'], 'v_w_in': out['v_w_in'], 'v_w_out': out['v_w_out'], 'v_ln_g': out['v_ln_g'], 'v_ln_b': out['v_ln_b'], 'v_conv_w': out['v_conv_w'], 'v_conv_b': out['v_conv_b'], 'v_lru_wa': out['v_lru_wa'], 'v_lru_ba': out['v_lru_ba'], 'v_lru_wx': out['v_lru_wx'], 'v_lru_bx': out['v_lru_bx'], 'v_lru_lam': out['v_lru_lam'], 'v_pool_w': out['v_pool_w'], 'v_pool_scale': out['v_pool_scale']}


def _loss(weights, diff, rest, loss_target):
    with _jax.named_scope("forward"):
        args = {**rest, TWIN_DIFF_INPUT: diff, **{k: w.astype(_WEIGHT_DTYPES[k]) for k, w in weights.items()}}
        y = _forward(args)
    with _jax.named_scope("loss_head"):
        err = _jnp.square(y.astype(_jnp.float32) - loss_target)
        return 0.5 * _jnp.sum(_jnp.mean(err, axis=-1)) if err.ndim else 0.5 * err


def _adamw(w, g, m, v):
    m = ADAM_B1 * m + (1.0 - ADAM_B1) * g
    v = ADAM_B2 * v + (1.0 - ADAM_B2) * _jnp.square(g)
    m_hat = m / (1.0 - ADAM_B1 ** ADAM_STEP)
    v_hat = v / (1.0 - ADAM_B2 ** ADAM_STEP)
    delta = -ADAM_LR * (m_hat / (_jnp.sqrt(v_hat) + ADAM_EPS) + ADAM_WD * w)
    return delta, m, v


def reference(x, c, ctx, c_ctx, w_mod, b_mod, w_in, w_out, ln_g, ln_b, conv_w, conv_b, lru_wa, lru_ba, lru_wx, lru_bx, lru_lam, pool_w, pool_scale, loss_target, m_c_ctx, m_w_mod, m_b_mod, m_w_in, m_w_out, m_ln_g, m_ln_b, m_conv_w, m_conv_b, m_lru_wa, m_lru_ba, m_lru_wx, m_lru_bx, m_lru_lam, m_pool_w, m_pool_scale, v_c_ctx, v_w_mod, v_b_mod, v_w_in, v_w_out, v_ln_g, v_ln_b, v_conv_w, v_conv_b, v_lru_wa, v_lru_ba, v_lru_wx, v_lru_bx, v_lru_lam, v_pool_w, v_pool_scale):
    given = dict(x=x, c=c, ctx=ctx, c_ctx=c_ctx, w_mod=w_mod, b_mod=b_mod, w_in=w_in, w_out=w_out, ln_g=ln_g, ln_b=ln_b, conv_w=conv_w, conv_b=conv_b, lru_wa=lru_wa, lru_ba=lru_ba, lru_wx=lru_wx, lru_bx=lru_bx, lru_lam=lru_lam, pool_w=pool_w, pool_scale=pool_scale, loss_target=loss_target, m_c_ctx=m_c_ctx, m_w_mod=m_w_mod, m_b_mod=m_b_mod, m_w_in=m_w_in, m_w_out=m_w_out, m_ln_g=m_ln_g, m_ln_b=m_ln_b, m_conv_w=m_conv_w, m_conv_b=m_conv_b, m_lru_wa=m_lru_wa, m_lru_ba=m_lru_ba, m_lru_wx=m_lru_wx, m_lru_bx=m_lru_bx, m_lru_lam=m_lru_lam, m_pool_w=m_pool_w, m_pool_scale=m_pool_scale, v_c_ctx=v_c_ctx, v_w_mod=v_w_mod, v_b_mod=v_b_mod, v_w_in=v_w_in, v_w_out=v_w_out, v_ln_g=v_ln_g, v_ln_b=v_ln_b, v_conv_w=v_conv_w, v_conv_b=v_conv_b, v_lru_wa=v_lru_wa, v_lru_ba=v_lru_ba, v_lru_wx=v_lru_wx, v_lru_bx=v_lru_bx, v_lru_lam=v_lru_lam, v_pool_w=v_pool_w, v_pool_scale=v_pool_scale)
    weights = {n: given[n] for n in TWIN_WEIGHTS}
    shared = {n: given[n] for n in SHARED_INPUTS}
    per_example = {n: given[n] for n in ['x', 'c', 'ctx']}
    grad_fn = _jax.value_and_grad(_loss, argnums=(0, 1))

    def one_microbatch(ex, loss_target):
        ex = dict(ex)
        diff = ex.pop(TWIN_DIFF_INPUT)
        return grad_fn(weights, diff, {**shared, **ex}, loss_target)

    if N_MICROBATCH == 1:
        loss, (grad_w, grad_x) = one_microbatch(per_example, given["loss_target"])
    else:
        def body(carry, xs):
            loss_sum, grad_sum = carry
            l_k, (gw_k, gx_k) = one_microbatch(xs[0], xs[1])
            with _jax.named_scope("update"):
                return (loss_sum + l_k, _jax.tree.map(_jnp.add, grad_sum, gw_k)), gx_k

        init = (_jnp.zeros((), _jnp.float32), _jax.tree.map(_jnp.zeros_like, weights))
        (loss, grad_w), grad_x = _jax.lax.scan(body, init, (per_example, given["loss_target"]))
    with _jax.named_scope("update"):
        delta_w, new_m, new_v = {}, {}, {}
        for n in TWIN_WEIGHTS:
            delta_w[n], new_m[n], new_v[n] = _adamw(weights[n], grad_w[n], given["m_" + n], given["v_" + n])
    return (loss, grad_x, *[grad_w[n] for n in TWIN_WEIGHTS], *[delta_w[n] for n in TWIN_WEIGHTS],
            *[new_m[n] for n in TWIN_WEIGHTS], *[new_v[n] for n in TWIN_WEIGHTS])
```

```python
import functools

import numpy as np
import jax
import jax.numpy as jnp
from jax import lax
from jax.experimental import pallas as pl
from jax.experimental.pallas import tpu as pltpu

F32 = jnp.float32
BF16 = jnp.bfloat16
MXU_DTYPE = BF16

D_MODEL = 1024
D_INNER = 2048
LRU_BLOCK = 128
GRID_W = 64
POOL_WINDOWS = (2, 4, 8, 16)
POOL_GROUP = 512
ALPHA = float(4 ** 0.25)
LN_EPS = 1e-5
LRU_C = 8.0
N_DEV = 8
N_WBLK = 8
WBLK = 512

ADAM_LR = 0.001
ADAM_B1 = 0.9
ADAM_B2 = 0.999
ADAM_EPS = 1e-08
ADAM_WD = 0.01
ADAM_STEP = 10

LANES = 128
SUBLANES = 8
V7X_VMEM_BYTES = 64 * 1024 * 1024
VMEM_LIMIT = V7X_VMEM_BYTES - 8 * 1024 * 1024
MESH = pl.DeviceIdType.MESH
ANY = pl.BlockSpec(memory_space=pl.ANY)

TM_MM = 512
TM_BWD = 256
TM_LRU = 512
CB_LRU = 512
N_SEG = 8
FLAT_TR = 256


def _cparams(**kw):
    return pltpu.CompilerParams(vmem_limit_bytes=VMEM_LIMIT, **kw)


def _my_pos():
    return lax.axis_index("x"), lax.axis_index("y"), lax.axis_index("c")


def _dot(a, b):
    return jnp.dot(a.astype(MXU_DTYPE), b.astype(MXU_DTYPE), preferred_element_type=F32)


def _dot_tn(a, b):
    return lax.dot_general(a.astype(MXU_DTYPE), b.astype(MXU_DTYPE), (((0,), (0,)), ((), ())),
                           preferred_element_type=F32)


def _dot_nt(a, b):
    return lax.dot_general(a.astype(MXU_DTYPE), b.astype(MXU_DTYPE), (((1,), (1,)), ((), ())),
                           preferred_element_type=F32)


def _sigmoid(z):
    return 1.0 / (1.0 + jnp.exp(-z))


def _neg_expm1(x):
    series = x * (1.0 + x * (0.5 + x * (1.0 / 6.0 + x * (1.0 / 24.0))))
    return -jnp.where(jnp.abs(x) < 0.03, series, jnp.exp(x) - 1.0)


def _log_sigmoid(x):
    y = jnp.exp(-jnp.abs(x))
    u = 1.0 + y
    l1p = jnp.where(u == 1.0, y, jnp.log(u) * (y / jnp.where(u == 1.0, 1.0, u - 1.0)))
    return jnp.minimum(x, 0.0) - l1p


def _rowsum(v):
    return jnp.sum(v, axis=0, keepdims=True)


def _layer_norm_stats(z):
    mu = jnp.mean(z, axis=-1, keepdims=True)
    zc = z - mu
    var = jnp.mean(zc * zc, axis=-1, keepdims=True)
    rstd = lax.rsqrt(var + LN_EPS)
    return zc * rstd, rstd


def _layer_norm_bwd(dy, xhat, rstd, g):
    dxh = dy * g
    m1 = jnp.mean(dxh, axis=-1, keepdims=True)
    m2 = jnp.mean(dxh * xhat, axis=-1, keepdims=True)
    return rstd * (dxh - m1 - xhat * m2)


def _shift_down(v, first_row):
    row = lax.broadcasted_iota(jnp.int32, v.shape, 0)
    return jnp.where(row == 0, first_row, pltpu.roll(v, 1, 0))


def _shift_up(v, last_row):
    n = v.shape[0]
    row = lax.broadcasted_iota(jnp.int32, v.shape, 0)
    return jnp.where(row == n - 1, last_row, pltpu.roll(v, n - 1, 0))


def _all_gather(blocks, name):
    n = len(blocks)

    def body(*refs):
        x_refs, out_refs = refs[:n], refs[n:2 * n]
        send_sems, recv_sems, local_sems = refs[2 * n:]
        x, y, c = _my_pos()
        me, sibling = (x, y, c), (x, y, 1 - c)
        chips = [(1 - x, y), (x, 1 - y), (1 - x, 1 - y)]

        def slot(a, px, py, pc):
            return out_refs[a].at[4 * px + 2 * py + pc]

        def copy(a, k, block, to, src=None):
            return pltpu.make_async_remote_copy(
                src_ref=slot(a, *block) if src is None else src, dst_ref=slot(a, *block),
                send_sem=send_sems.at[a, k], recv_sem=recv_sems.at[a, k], device_id=to, device_id_type=MESH)

        mine = [pltpu.make_async_copy(x_refs[a], slot(a, *me), local_sems.at[a]) for a in range(n)]
        for cp in mine:
            cp.start()
        first = []
        for a in range(n):
            first.append(copy(a, 0, me, sibling, src=x_refs[a]))
            first += [copy(a, 1 + j, me, (*chip, c), src=x_refs[a]) for j, chip in enumerate(chips)]
        for cp in first:
            cp.start()
        passed = []
        for j, chip in enumerate(chips):
            for a in range(n):
                copy(a, 1 + j, (*chip, c), me).wait_recv()
                fwd = copy(a, 4 + j, (*chip, c), sibling)
                fwd.start()
                passed.append(fwd)
        for a in range(n):
            copy(a, 0, sibling, me).wait_recv()
            for j, chip in enumerate(chips):
                copy(a, 4 + j, (*chip, 1 - c), me).wait_recv()
        for cp in first + passed:
            cp.wait_send()
        for cp in mine:
            cp.wait()

    outs = pl.pallas_call(
        body, name=name,
        out_shape=[jax.ShapeDtypeStruct((N_DEV,) + b.shape, b.dtype) for b in blocks],
        in_specs=[ANY] * n, out_specs=[ANY] * n,
        scratch_shapes=[pltpu.SemaphoreType.DMA((n, 7)), pltpu.SemaphoreType.DMA((n, 7)),
                        pltpu.SemaphoreType.DMA((n,))],
    )(*blocks)
    return list(outs)


def _sibling_exchange(buf, name):
    _, r, l = buf.shape

    def body(src, out, send_sems, recv_sems):
        x, y, c = _my_pos()
        copies = [pltpu.make_async_remote_copy(
            src_ref=src.at[2 * j + (1 - c)], dst_ref=out.at[j], send_sem=send_sems.at[j], recv_sem=recv_sems.at[j],
            device_id=(x, y, 1 - c), device_id_type=MESH) for j in range(4)]
        for cp in copies:
            cp.start()
        for cp in copies:
            cp.wait()

    return pl.pallas_call(
        body, name=name, out_shape=jax.ShapeDtypeStruct((4, r, l), buf.dtype),
        in_specs=[ANY], out_specs=ANY,
        scratch_shapes=[pltpu.SemaphoreType.DMA((4,)), pltpu.SemaphoreType.DMA((4,))],
    )(buf)


def _chip_exchange(part, name):
    def body(src, out, send_sems, recv_sems, local_sem):
        x, y, c = _my_pos()
        jme = 2 * x + y
        peers = [(1 - x, y), (x, 1 - y), (1 - x, 1 - y)]
        loc = pltpu.make_async_copy(src.at[jme], out.at[jme], local_sem)
        loc.start()

        def copy(k, px, py, dst_slot):
            return pltpu.make_async_remote_copy(
                src_ref=src.at[2 * px + py], dst_ref=out.at[dst_slot], send_sem=send_sems.at[k],
                recv_sem=recv_sems.at[k], device_id=(px, py, c), device_id_type=MESH)

        sends = [copy(k, px, py, jme) for k, (px, py) in enumerate(peers)]
        for cp in sends:
            cp.start()
        for k, (px, py) in enumerate(peers):
            copy(k, px, py, 2 * px + py).wait_recv()
        for cp in sends:
            cp.wait_send()
        loc.wait()

    return pl.pallas_call(
        body, name=name, out_shape=jax.ShapeDtypeStruct(part.shape, part.dtype),
        in_specs=[ANY], out_specs=ANY,
        scratch_shapes=[pltpu.SemaphoreType.DMA((3,)), pltpu.SemaphoreType.DMA((3,)), pltpu.SemaphoreType.DMA(())],
    )(part)


def _pair_sum(buf, recv, core, name):
    _, r, l = buf.shape

    def body(core_ref, a_ref, b_ref, o_ref):
        o_ref[...] = a_ref[...] + b_ref[...]

    return pl.pallas_call(
        body, name=name, out_shape=jax.ShapeDtypeStruct((4, r, l), F32),
        grid_spec=pltpu.PrefetchScalarGridSpec(
            num_scalar_prefetch=1, grid=(4, r // FLAT_TR),
            in_specs=[pl.BlockSpec((None, FLAT_TR, l), lambda j, i, cr: (2 * j + cr[0], i, 0)),
                      pl.BlockSpec((None, FLAT_TR, l), lambda j, i, cr: (j, i, 0))],
            out_specs=pl.BlockSpec((None, FLAT_TR, l), lambda j, i, cr: (j, i, 0))),
        compiler_params=_cparams(dimension_semantics=("arbitrary", "arbitrary")),
    )(core, buf, recv)


def _sum4(parts, name):
    _, r, l = parts.shape

    def body(p_ref, o_ref):
        o_ref[...] = (p_ref[0] + p_ref[1]) + (p_ref[2] + p_ref[3])

    return pl.pallas_call(
        body, name=name, out_shape=jax.ShapeDtypeStruct((r, l), F32), grid=(r // FLAT_TR,),
        in_specs=[pl.BlockSpec((4, FLAT_TR, l), lambda i: (0, i, 0))],
        out_specs=pl.BlockSpec((FLAT_TR, l), lambda i: (i, 0)),
        compiler_params=_cparams(dimension_semantics=("arbitrary",)),
    )(parts)


def _adamw(w, g, m, v, name):
    r, l = w.shape

    def body(w_ref, g_ref, m_ref, v_ref, d_ref, nm_ref, nv_ref):
        gg = g_ref[...]
        nm = ADAM_B1 * m_ref[...] + (1.0 - ADAM_B1) * gg
        nv = ADAM_B2 * v_ref[...] + (1.0 - ADAM_B2) * (gg * gg)
        m_hat = nm / (1.0 - ADAM_B1 ** ADAM_STEP)
        v_hat = nv / (1.0 - ADAM_B2 ** ADAM_STEP)
        d_ref[...] = -ADAM_LR * (m_hat / (jnp.sqrt(v_hat) + ADAM_EPS) + ADAM_WD * w_ref[...])
        nm_ref[...] = nm
        nv_ref[...] = nv

    spec = pl.BlockSpec((FLAT_TR, l), lambda i: (i, 0))
    return pl.pallas_call(
        body, name=name, out_shape=[jax.ShapeDtypeStruct((r, l), F32)] * 3, grid=(r // FLAT_TR,),
        in_specs=[spec] * 4, out_specs=[spec] * 3,
        compiler_params=_cparams(dimension_semantics=("arbitrary",)),
    )(w, g, m, v)


def _to_rows(pieces, row_multiple):
    flat = jnp.concatenate([p.reshape(-1) for p in pieces])
    rows = -(-flat.shape[0] // LANES)
    rows = -(-rows // row_multiple) * row_multiple
    flat = jnp.pad(flat, (0, rows * LANES - flat.shape[0]))
    return flat.reshape(rows, LANES)


def _split_rows(rows, shapes):
    flat = rows.reshape(-1)
    out, off = [], 0
    for s in shapes:
        n = int(np.prod(s))
        out.append(flat[off:off + n].reshape(s))
        off += n
    return out


def _mod_fwd(cond, w_mod, b_my, name):
    nl, _, ncol = w_mod.shape

    def body(a_ref, w_ref, b_ref, o_ref):
        a = a_ref[...]
        s = a * _sigmoid(a)
        for i in range(nl):
            o_ref[i] = _dot(s, w_ref[i]) + b_ref[i]

    return pl.pallas_call(
        body, name=name, out_shape=jax.ShapeDtypeStruct((nl, 16, ncol), F32),
        compiler_params=_cparams(),
    )(cond, w_mod, b_my)


def _mod_bwd(cond, dm_all, dm_my, w_mod, name):
    nl, _, ncol = w_mod.shape

    def body(a_ref, dma_ref, dmm_ref, w_ref, gw_ref, gb_ref, gc_ref):
        a = a_ref[...]
        sg = _sigmoid(a)
        s = a * sg
        for i in range(nl):
            gw_ref[i] = _dot_tn(s, dmm_ref[i])
            gb_ref[i] = jnp.sum(dma_ref[i], axis=0, keepdims=True)
        back = _dot_nt(dmm_ref[0], w_ref[0])
        dsilu = sg * (1.0 + a * (1.0 - sg))
        gc_ref[...] = jnp.sum(back[8:16] * dsilu[8:16], axis=0, keepdims=True)

    return pl.pallas_call(
        body, name=name,
        out_shape=[jax.ShapeDtypeStruct((nl, D_MODEL, ncol), F32), jax.ShapeDtypeStruct((nl, 1, 3 * D_MODEL), F32),
                   jax.ShapeDtypeStruct((1, D_MODEL), F32)],
        compiler_params=_cparams(),
    )(cond, dm_all, dm_my, w_mod)


def _in_proj(xt, sc, sh, wg, name):
    t = xt.shape[0]
    tm = min(TM_MM, t)

    def body(x_ref, sc_ref, sh_ref, w_ref, u_ref, g_ref):
        h = (x_ref[...] * (1.0 + sc_ref[...]) + sh_ref[...]).astype(MXU_DTYPE)
        for k in range(N_WBLK):
            o = jnp.dot(h, w_ref[k], preferred_element_type=F32)
            if k < N_WBLK // 2:
                u_ref[:, k * WBLK:(k + 1) * WBLK] = o
            else:
                kk = k - N_WBLK // 2
                g_ref[:, kk * WBLK:(kk + 1) * WBLK] = o

    row = pl.BlockSpec((1, D_MODEL), lambda i: (0, 0))
    return pl.pallas_call(
        body, name=name, out_shape=[jax.ShapeDtypeStruct((t, D_INNER), F32)] * 2, grid=(t // tm,),
        in_specs=[pl.BlockSpec((tm, D_MODEL), lambda i: (i, 0)), row, row,
                  pl.BlockSpec((N_WBLK, D_MODEL, WBLK), lambda i: (0, 0, 0), pipeline_mode=pl.Buffered(1))],
        out_specs=[pl.BlockSpec((tm, D_INNER), lambda i: (i, 0))] * 2,
        compiler_params=_cparams(dimension_semantics=("arbitrary",)),
    )(xt, sc, sh, wg)


def _halo_maps(nt, tm, n_rows8, pos):
    per = tm // SUBLANES
    prev = lambda cb, i: (jnp.maximum(pos(i) * per - 1, 0), cb)
    nxt = lambda cb, i: (jnp.minimum((pos(i) + 1) * per, n_rows8 - 1), cb)
    return prev, nxt


def _conv_taps(u, prev8, next8, is_first, is_last):
    pz = jnp.where(is_first, 0.0, 1.0)
    nz = jnp.where(is_last, 0.0, 1.0)
    p7, p6, n0 = prev8[7:8] * pz, prev8[6:7] * pz, next8[0:1] * nz
    um1 = _shift_down(u, p7)
    um2 = _shift_down(um1, p6)
    up1 = _shift_up(u, n0)
    return um2, um1, up1


def _lru_gates(uv, wa_ref, wx_ref, ba, bx, cl, g):
    sl = slice(g * LANES, (g + 1) * LANES)
    uvg = uv[:, sl]
    r = _sigmoid(_dot(uvg, wa_ref[g]) + ba[:, sl])
    ii = _sigmoid(_dot(uvg, wx_ref[g]) + bx[:, sl])
    la = cl[:, sl] * r
    a = jnp.exp(la)
    s = jnp.sqrt(_neg_expm1(2.0 * la))
    return uvg, r, ii, a, s


def _scan_tile(a_s, b_s, carry_ref, write_out, seg, reverse):
    n_g = a_s.shape[0]
    stride = a_s.shape[1] // N_SEG

    def step(k, state):
        t = (seg - 1 - k) if reverse else k
        hs, cs = state
        nh, nc = [], []
        for g in range(n_g):
            a = a_s[g, pl.ds(t, N_SEG, stride=stride), :]
            b = b_s[g, pl.ds(t, N_SEG, stride=stride), :]
            h = a * hs[g] + b
            cum = a * cs[g]
            b_s[g, pl.ds(t, N_SEG, stride=stride), :] = h
            a_s[g, pl.ds(t, N_SEG, stride=stride), :] = cum
            nh.append(h)
            nc.append(cum)
        return tuple(nh), tuple(nc)

    zeros = tuple(jnp.zeros((N_SEG, LANES), F32) for _ in range(n_g))
    ones = tuple(jnp.ones((N_SEG, LANES), F32) for _ in range(n_g))
    h_fin, a_fin = lax.fori_loop(0, seg, step, (zeros, ones))

    order = list(range(N_SEG - 1, -1, -1)) if reverse else list(range(N_SEG))
    for g in range(n_g):
        carry = carry_ref[:, g * LANES:(g + 1) * LANES]
        for j in order:
            rows = pl.ds(j * stride, seg)
            write_out(j, g, b_s[g, rows, :] + a_s[g, rows, :] * carry)
            carry = a_fin[g][j:j + 1] * carry + h_fin[g][j:j + 1]
        carry_ref[:, g * LANES:(g + 1) * LANES] = carry


def _lru_specs(s, tm, cb, direction_pos, nt):
    n_rows8 = s // SUBLANES
    prev, nxt = _halo_maps(nt, tm, n_rows8, direction_pos)
    tile = pl.BlockSpec((tm, cb), lambda c, i: (direction_pos(i), c))
    return tile, pl.BlockSpec((SUBLANES, cb), prev), pl.BlockSpec((SUBLANES, cb), nxt)


def _lru_param_specs(cb, d):
    n_g = cb // LANES
    vec = pl.BlockSpec((1, cb), lambda c, i: (0, c))
    dvec = pl.BlockSpec((None, 1, cb), lambda c, i: (d, 0, c))
    wmat = pl.BlockSpec((None, n_g, LRU_BLOCK, LRU_BLOCK), lambda c, i: (d, c, 0, 0))
    return vec, dvec, wmat


def _lru_fwd(u, h0, p, d, name):
    s = u.shape[0]
    tm = min(TM_LRU, s)
    cb = CB_LRU
    n_g = cb // LANES
    nt = s // tm
    seg = tm // N_SEG
    stride = seg + SUBLANES
    pos = (lambda i: i) if d == 0 else (lambda i: nt - 1 - i)

    def body(u_ref, up_ref, un_ref, cw_ref, cbias_ref, wa_ref, wx_ref, ba_ref, bx_ref, lam_ref, h0_ref,
             h_ref, hc_ref, a_s, b_s):
        i = pl.program_id(1)
        tp = pos(i)

        @pl.when(i == 0)
        def _():
            hc_ref[...] = h0_ref[...]

        u_t = u_ref[...]
        um2, um1, up1 = _conv_taps(u_t, up_ref[...], un_ref[...], tp == 0, tp == nt - 1)
        cw = cw_ref[...]
        uv = um2 * cw[0:1] + um1 * cw[1:2] + u_t * cw[2:3] + up1 * cw[3:4] + cbias_ref[...]
        cl = LRU_C * _log_sigmoid(lam_ref[...])
        ba, bx = ba_ref[...], bx_ref[...]
        for g in range(n_g):
            uvg, r, ii, a, sq = _lru_gates(uv, wa_ref, wx_ref, ba, bx, cl, g)
            b = sq * (ii * uvg)
            for j in range(N_SEG):
                a_s[g, pl.ds(j * stride, seg), :] = a[j * seg:(j + 1) * seg]
                b_s[g, pl.ds(j * stride, seg), :] = b[j * seg:(j + 1) * seg]

        def write_out(j, g, h):
            h_ref[pl.ds(j * seg, seg), pl.ds(g * LANES, LANES)] = h

        _scan_tile(a_s, b_s, hc_ref, write_out, seg, reverse=(d == 1))

    tile, prev, nxt = _lru_specs(s, tm, cb, pos, nt)
    vec, dvec, wmat = _lru_param_specs(cb, d)
    return pl.pallas_call(
        body, name=name,
        out_shape=[jax.ShapeDtypeStruct((s, D_INNER), F32), jax.ShapeDtypeStruct((1, D_INNER), F32)],
        grid=(D_INNER // cb, nt),
        in_specs=[tile, prev, nxt, pl.BlockSpec((4, cb), lambda c, i: (0, c)), vec, wmat, wmat, dvec, dvec, dvec, vec],
        out_specs=[tile, vec],
        scratch_shapes=[pltpu.VMEM((n_g, N_SEG * stride, LANES), F32)] * 2,
        compiler_params=_cparams(dimension_semantics=("arbitrary", "arbitrary")),
    )(u, u, u, p["conv_w"], p["conv_b"], p["wa"], p["wx"], p["ba"], p["bx"], p["lam"], h0)


def _lru_bwd(u, dh, h, h0, lam_in, p, d, name):
    s = u.shape[0]
    tm = min(TM_LRU, s)
    cb = CB_LRU
    n_g = cb // LANES
    nt = s // tm
    seg = tm // N_SEG
    stride = seg + SUBLANES
    pos = (lambda i: nt - 1 - i) if d == 0 else (lambda i: i)

    def body(u_ref, up_ref, un_ref, dh_ref, h_ref, hh_ref, cw_ref, cbias_ref, wa_ref, wx_ref, ba_ref, bx_ref,
             lam_ref, h0_ref, lin_ref, duv_ref, gwa_ref, gwx_ref, gv_ref, lc_ref, a_s, b_s, lp_s):
        i = pl.program_id(1)
        tp = pos(i)

        @pl.when(i == 0)
        def _():
            lc_ref[...] = lin_ref[...]
            gwa_ref[...] = jnp.zeros_like(gwa_ref)
            gwx_ref[...] = jnp.zeros_like(gwx_ref)
            gv_ref[...] = jnp.zeros_like(gv_ref)

        u_t = u_ref[...]
        um2, um1, up1 = _conv_taps(u_t, up_ref[...], un_ref[...], tp == 0, tp == nt - 1)
        cw = cw_ref[...]
        uv = um2 * cw[0:1] + um1 * cw[1:2] + u_t * cw[2:3] + up1 * cw[3:4] + cbias_ref[...]
        lam = lam_ref[...]
        cl = LRU_C * _log_sigmoid(lam)
        ba, bx = ba_ref[...], bx_ref[...]
        dh_t = dh_ref[...]
        carry_in = lc_ref[...]
        for g in range(n_g):
            _, _, _, a, _ = _lru_gates(uv, wa_ref, wx_ref, ba, bx, cl, g)
            b = a * dh_t[:, g * LANES:(g + 1) * LANES]
            for j in range(N_SEG):
                a_s[g, pl.ds(j * stride, seg), :] = a[j * seg:(j + 1) * seg]
                b_s[g, pl.ds(j * stride, seg), :] = b[j * seg:(j + 1) * seg]

        def write_out(j, g, v):
            lp_s[pl.ds(j * seg, seg), pl.ds(g * LANES, LANES)] = v

        _scan_tile(a_s, b_s, lc_ref, write_out, seg, reverse=(d == 0))

        h_t = h_ref[...]
        hh = hh_ref[...]
        if d == 0:
            edge = jnp.where(tp == 0, h0_ref[...], hh[7:8])
            h_prev = _shift_down(h_t, edge)
            lam_t = dh_t + _shift_up(lp_s[...], carry_in)
        else:
            edge = jnp.where(tp == nt - 1, h0_ref[...], hh[0:1])
            h_prev = _shift_up(h_t, edge)
            lam_t = dh_t + _shift_down(lp_s[...], carry_in)

        dsig = LRU_C * _sigmoid(-lam)
        for g in range(n_g):
            sl = slice(g * LANES, (g + 1) * LANES)
            uvg, r, ii, a, sq = _lru_gates(uv, wa_ref, wx_ref, ba, bx, cl, g)
            lt = lam_t[:, sl]
            iu = ii * uvg
            dla = lt * h_prev[:, sl] * a - (lt * iu) * (a * a / sq)
            dzr = (dla * cl[:, sl]) * r * (1.0 - r)
            dzi = (lt * sq * uvg) * ii * (1.0 - ii)
            duv_ref[:, sl] = lt * sq * ii + _dot_nt(dzr, wa_ref[g]) + _dot_nt(dzi, wx_ref[g])
            gwa_ref[g] += _dot_tn(uvg, dzr)
            gwx_ref[g] += _dot_tn(uvg, dzi)
            gv_ref[0:1, sl] += _rowsum(dzr)
            gv_ref[1:2, sl] += _rowsum(dzi)
            gv_ref[2:3, sl] += _rowsum(dla * r) * dsig[:, sl]

    tile, prev, nxt = _lru_specs(s, tm, cb, pos, nt)
    vec, dvec, wmat = _lru_param_specs(cb, d)
    hh_spec = prev if d == 0 else nxt
    gw_spec = pl.BlockSpec((n_g, LRU_BLOCK, LRU_BLOCK), lambda c, i: (c, 0, 0))
    n_blk = D_INNER // LRU_BLOCK
    return pl.pallas_call(
        body, name=name,
        out_shape=[jax.ShapeDtypeStruct((s, D_INNER), F32),
                   jax.ShapeDtypeStruct((n_blk, LRU_BLOCK, LRU_BLOCK), F32),
                   jax.ShapeDtypeStruct((n_blk, LRU_BLOCK, LRU_BLOCK), F32),
                   jax.ShapeDtypeStruct((SUBLANES, D_INNER), F32),
                   jax.ShapeDtypeStruct((1, D_INNER), F32)],
        grid=(D_INNER // cb, nt),
        in_specs=[tile, prev, nxt, tile, tile, hh_spec, pl.BlockSpec((4, cb), lambda c, i: (0, c)), vec,
                  wmat, wmat, dvec, dvec, dvec, vec, vec],
        out_specs=[tile, gw_spec, gw_spec, pl.BlockSpec((SUBLANES, cb), lambda c, i: (0, c)), vec],
        scratch_shapes=[pltpu.VMEM((n_g, N_SEG * stride, LANES), F32)] * 2 + [pltpu.VMEM((tm, cb), F32)],
        compiler_params=_cparams(dimension_semantics=("arbitrary", "arbitrary")),
    )(u, u, u, dh, h, h, p["conv_w"], p["conv_b"], p["wa"], p["wx"], p["ba"], p["bx"], p["lam"], h0, lam_in)


def _out0(hf, hb, g, xt, gt, wo, lg, lb, name):
    t = xt.shape[0]
    tm = min(TM_MM, t)

    def body(hf_ref, hb_ref, g_ref, x_ref, gt_ref, w_ref, lg_ref, lb_ref, x1_ref, br_ref):
        gg = g_ref[...]
        p = (hf_ref[...] + hb_ref[...]) * (gg * _sigmoid(gg))
        br = _dot(p, w_ref[...])
        z = ALPHA * x_ref[...] + gt_ref[...] * br
        xhat, _ = _layer_norm_stats(z)
        x1_ref[...] = xhat * lg_ref[...] + lb_ref[...]
        br_ref[...] = br

    wide = pl.BlockSpec((tm, D_INNER), lambda i: (i, 0))
    nar = pl.BlockSpec((tm, D_MODEL), lambda i: (i, 0))
    row = pl.BlockSpec((1, D_MODEL), lambda i: (0, 0))
    return pl.pallas_call(
        body, name=name, out_shape=[jax.ShapeDtypeStruct((t, D_MODEL), F32)] * 2, grid=(t // tm,),
        in_specs=[wide, wide, wide, nar, row,
                  pl.BlockSpec((D_INNER, D_MODEL), lambda i: (0, 0), pipeline_mode=pl.Buffered(1)), row, row],
        out_specs=[nar, nar],
        compiler_params=_cparams(dimension_semantics=("arbitrary",)),
    )(hf, hb, g, xt, gt, wo, lg, lb)


def _col_mean_matrices(transpose):
    mats = []
    for w in POOL_WINDOWS:
        t = np.arange(GRID_W)
        lo = np.clip(t - w // 2, 0, GRID_W)
        hi = np.clip(t + w // 2, 0, GRID_W)
        m = np.zeros((GRID_W, GRID_W), np.float64)
        for r in range(GRID_W):
            m[r, lo[r]:hi[r]] = 1.0 / (hi[r] - lo[r])
        m2 = np.kron(np.eye(LANES // GRID_W), m)
        mats.append(m2.T if transpose else m2)
    return jnp.asarray(np.stack(mats), F32)


def _pool_mix(xin, transpose, out_dtype, name):
    s = xin.shape[0]
    n_rows = s // GRID_W
    pad_t = SUBLANES * GRID_W
    ch = 512
    n_slab = D_INNER // LANES
    slabs_per_group = POOL_GROUP // LANES
    mats = _col_mean_matrices(transpose)

    def body(mc_ref, x_ref, o_ref, pad_s, tmp_s):
        k = pl.program_id(0) // slabs_per_group
        pad_s[pl.ds(0, pad_t), :] = jnp.zeros((pad_t, LANES), F32)
        pad_s[pl.ds(pad_t + s, pad_t), :] = jnp.zeros((pad_t, LANES), F32)

        for kk, w in enumerate(POOL_WINDOWS):
            half = w // 2

            def count(c):
                tok = c * ch + lax.broadcasted_iota(jnp.int32, (ch, LANES), 0)
                r = lax.shift_right_logical(tok, 6)
                return (jnp.minimum(r + half, n_rows) - jnp.maximum(r - half, 0)).astype(F32)

            def col_mix(src_ref, src_off, dst):
                def blk(b, carry):
                    st = pl.multiple_of(b * LANES, LANES)
                    xb = src_ref[pl.ds(src_off + st, LANES), :]
                    dst(st, jnp.dot(mc_ref[kk], xb, precision=lax.Precision.HIGHEST, preferred_element_type=F32))
                    return carry
                lax.fori_loop(0, s // LANES, blk, 0)

            def row_sum(c, offsets):
                st = pl.multiple_of(c * ch, ch)
                acc = pad_s[pl.ds(pad_t + st + offsets[0] * GRID_W, ch), :]
                for o in offsets[1:]:
                    acc = acc + pad_s[pl.ds(pad_t + st + o * GRID_W, ch), :]
                return st, acc

            @pl.when(k == kk)
            def _():
                if not transpose:
                    def put(st, v):
                        pad_s[pl.ds(pad_t + st, LANES), :] = v
                    col_mix(x_ref, 0, put)

                    def rows(c, carry):
                        st, acc = row_sum(c, list(range(-half, half)))
                        o_ref[pl.ds(st, ch), :] = (acc / count(c) - x_ref[pl.ds(st, ch), :]).astype(out_dtype)
                        return carry
                    lax.fori_loop(0, s // ch, rows, 0)
                else:
                    def scale(c, carry):
                        st = pl.multiple_of(c * ch, ch)
                        pad_s[pl.ds(pad_t + st, ch), :] = x_ref[pl.ds(st, ch), :] / count(c)
                        return carry
                    lax.fori_loop(0, s // ch, scale, 0)

                    def rows(c, carry):
                        st, acc = row_sum(c, list(range(-(half - 1), half + 1)))
                        tmp_s[pl.ds(st, ch), :] = acc
                        return carry
                    lax.fori_loop(0, s // ch, rows, 0)

                    def put(st, v):
                        o_ref[pl.ds(st, LANES), :] = (v - x_ref[pl.ds(st, LANES), :]).astype(out_dtype)
                    col_mix(tmp_s, 0, put)

    slab = pl.BlockSpec((s, LANES), lambda i: (0, i))
    return pl.pallas_call(
        body, name=name, out_shape=jax.ShapeDtypeStruct((s, D_INNER), out_dtype), grid=(n_slab,),
        in_specs=[pl.BlockSpec((len(POOL_WINDOWS), LANES, LANES), lambda i: (0, 0, 0)), slab],
        out_specs=slab,
        scratch_shapes=[pltpu.VMEM((s + 2 * pad_t, LANES), F32), pltpu.VMEM((s, LANES), F32)],
        compiler_params=_cparams(dimension_semantics=("arbitrary",)),
    )(mats, xin)


def _out1(dmix, pw, ps, g, x1, gt, wo, lg, lb, tgt, name):
    t = x1.shape[0]
    tm = min(TM_MM, t)
    n_grp = len(POOL_WINDOWS)

    def body(d_ref, pw_ref, ps_ref, g_ref, x1_ref, gt_ref, w_ref, lg_ref, lb_ref, tgt_ref, dz_ref, st_ref):
        @pl.when(pl.program_id(0) == 0)
        def _():
            st_ref[...] = jnp.zeros_like(st_ref)

        br = jnp.zeros((tm, D_MODEL), F32)
        for k in range(n_grp):
            sl = slice(k * POOL_GROUP, (k + 1) * POOL_GROUP)
            y = jnp.dot(d_ref[:, sl], pw_ref[k], preferred_element_type=F32) * ps_ref[:, sl]
            gg = g_ref[:, sl]
            br = br + _dot(y * (gg * _sigmoid(gg)), w_ref[sl, :])
        z = ALPHA * x1_ref[...] + gt_ref[...] * br
        xhat, rstd = _layer_norm_stats(z)
        lg_v = lg_ref[...]
        err = xhat * lg_v + lb_ref[...] - tgt_ref[...]
        dy = err * (1.0 / D_MODEL)
        dz = _layer_norm_bwd(dy, xhat, rstd, lg_v)
        dz_ref[...] = dz
        st_ref[0:1, :] += _rowsum(dy * xhat)
        st_ref[1:2, :] += _rowsum(dy)
        st_ref[2:3, :] += _rowsum(dz * br)
        st_ref[3:4, :] += _rowsum(err * err)

    wide = pl.BlockSpec((tm, D_INNER), lambda i: (i, 0))
    nar = pl.BlockSpec((tm, D_MODEL), lambda i: (i, 0))
    row = pl.BlockSpec((1, D_MODEL), lambda i: (0, 0))
    return pl.pallas_call(
        body, name=name,
        out_shape=[jax.ShapeDtypeStruct((t, D_MODEL), F32), jax.ShapeDtypeStruct((SUBLANES, D_MODEL), F32)],
        grid=(t // tm,),
        in_specs=[wide, pl.BlockSpec((n_grp, POOL_GROUP, POOL_GROUP), lambda i: (0, 0, 0)),
                  pl.BlockSpec((1, D_INNER), lambda i: (0, 0)), wide, nar, row,
                  pl.BlockSpec((D_INNER, D_MODEL), lambda i: (0, 0), pipeline_mode=pl.Buffered(1)), row, row, nar],
        out_specs=[nar, pl.BlockSpec((SUBLANES, D_MODEL), lambda i: (0, 0))],
        compiler_params=_cparams(dimension_semantics=("arbitrary",)),
    )(dmix, pw, ps, g, x1, gt, wo, lg, lb, tgt)


def _flush(acc, out_hbm, sem):
    cp = pltpu.make_async_copy(acc, out_hbm, sem)
    cp.start()
    cp.wait()


def _bout1(dz, dmix, g, pw, ps, gt, wo, name):
    t = dz.shape[0]
    tm = min(TM_BWD, t)
    nt = t // tm
    n_grp = len(POOL_WINDOWS)

    def body(dz_ref, d_ref, g_ref, pw_ref, ps_ref, gt_ref, w_ref, dd_ref, dg_ref, gwo_hbm, gpw_hbm, gps_ref,
             gwo_acc, gpw_acc, sems):
        i = pl.program_id(0)

        @pl.when(i == 0)
        def _():
            gwo_acc[...] = jnp.zeros_like(gwo_acc)
            gpw_acc[...] = jnp.zeros_like(gpw_acc)
            gps_ref[...] = jnp.zeros_like(gps_ref)

        db = (gt_ref[...] * dz_ref[...]).astype(MXU_DTYPE)
        for k in range(n_grp):
            sl = slice(k * POOL_GROUP, (k + 1) * POOL_GROUP)
            dk = d_ref[:, sl]
            po = jnp.dot(dk, pw_ref[k], preferred_element_type=F32)
            psk = ps_ref[:, sl]
            y = po * psk
            gg = g_ref[:, sl]
            sg = _sigmoid(gg)
            silu = gg * sg
            gwo_acc[sl, :] += _dot_tn(y * silu, db)
            dp = _dot_nt(db, w_ref[sl, :])
            dy = dp * silu
            dg_ref[:, sl] = dp * y * (sg * (1.0 + gg * (1.0 - sg)))
            gps_ref[0:1, sl] += _rowsum(dy * po)
            dpo = (dy * psk).astype(MXU_DTYPE)
            gpw_acc[k] += _dot_tn(dk, dpo)
            dd_ref[:, sl] = _dot_nt(dpo, pw_ref[k])

        @pl.when(i == nt - 1)
        def _():
            _flush(gwo_acc, gwo_hbm, sems.at[0])
            _flush(gpw_acc, gpw_hbm, sems.at[1])

    wide = pl.BlockSpec((tm, D_INNER), lambda i: (i, 0))
    nar = pl.BlockSpec((tm, D_MODEL), lambda i: (i, 0))
    return pl.pallas_call(
        body, name=name,
        out_shape=[jax.ShapeDtypeStruct((t, D_INNER), F32), jax.ShapeDtypeStruct((t, D_INNER), F32),
                   jax.ShapeDtypeStruct((D_INNER, D_MODEL), F32),
                   jax.ShapeDtypeStruct((n_grp, POOL_GROUP, POOL_GROUP), F32),
                   jax.ShapeDtypeStruct((SUBLANES, D_INNER), F32)],
        grid=(nt,),
        in_specs=[nar, wide, wide, pl.BlockSpec((n_grp, POOL_GROUP, POOL_GROUP), lambda i: (0, 0, 0)),
                  pl.BlockSpec((1, D_INNER), lambda i: (0, 0)), pl.BlockSpec((1, D_MODEL), lambda i: (0, 0)),
                  pl.BlockSpec((D_INNER, D_MODEL), lambda i: (0, 0), pipeline_mode=pl.Buffered(1))],
        out_specs=[wide, wide, ANY, ANY, pl.BlockSpec((SUBLANES, D_INNER), lambda i: (0, 0))],
        scratch_shapes=[pltpu.VMEM((D_INNER, D_MODEL), F32), pltpu.VMEM((n_grp, POOL_GROUP, POOL_GROUP), F32),
                        pltpu.SemaphoreType.DMA((2,))],
        compiler_params=_cparams(dimension_semantics=("arbitrary",)),
    )(dz, dmix, g, pw, ps, gt, wo)


def _bout0(dz, hf, hb, g, gt, wo, name):
    t = dz.shape[0]
    tm = min(TM_BWD, t)
    nt = t // tm

    def body(dz_ref, hf_ref, hb_ref, g_ref, gt_ref, w_ref, dy_ref, dg_ref, gwo_hbm, gwo_acc, sem):
        i = pl.program_id(0)

        @pl.when(i == 0)
        def _():
            gwo_acc[...] = jnp.zeros_like(gwo_acc)

        db = (gt_ref[...] * dz_ref[...]).astype(MXU_DTYPE)
        y = hf_ref[...] + hb_ref[...]
        gg = g_ref[...]
        sg = _sigmoid(gg)
        silu = gg * sg
        gwo_acc[...] += _dot_tn(y * silu, db)
        dp = _dot_nt(db, w_ref[...])
        dy_ref[...] = dp * silu
        dg_ref[...] = dp * y * (sg * (1.0 + gg * (1.0 - sg)))

        @pl.when(i == nt - 1)
        def _():
            _flush(gwo_acc, gwo_hbm, sem)

    wide = pl.BlockSpec((tm, D_INNER), lambda i: (i, 0))
    nar = pl.BlockSpec((tm, D_MODEL), lambda i: (i, 0))
    return pl.pallas_call(
        body, name=name,
        out_shape=[jax.ShapeDtypeStruct((t, D_INNER), F32), jax.ShapeDtypeStruct((t, D_INNER), F32),
                   jax.ShapeDtypeStruct((D_INNER, D_MODEL), F32)],
        grid=(nt,),
        in_specs=[nar, wide, wide, wide, pl.BlockSpec((1, D_MODEL), lambda i: (0, 0)),
                  pl.BlockSpec((D_INNER, D_MODEL), lambda i: (0, 0), pipeline_mode=pl.Buffered(1))],
        out_specs=[wide, wide, ANY],
        scratch_shapes=[pltpu.VMEM((D_INNER, D_MODEL), F32), pltpu.SemaphoreType.DMA(())],
        compiler_params=_cparams(dimension_semantics=("arbitrary",)),
    )(dz, hf, hb, g, gt, wo)


def _in_proj_bwd_core(h, parts, w_ref, gw_acc):
    dh = None
    k = 0
    for part in parts:
        for kk in range(part.shape[1] // WBLK):
            dk = part[:, kk * WBLK:(kk + 1) * WBLK].astype(MXU_DTYPE)
            gw_acc[k] += _dot_tn(h, dk)
            contrib = _dot_nt(dk, w_ref[k])
            dh = contrib if dh is None else dh + contrib
            k += 1
    return dh


def _bin1(du, dg, x1, dz1, sc, sh, wg, xt, br0, gt0, lg0, name):
    t = x1.shape[0]
    tm = min(TM_BWD, t)
    nt = t // tm

    def body(du_ref, dg_ref, x1_ref, dz1_ref, sc_ref, sh_ref, w_ref, x_ref, br_ref, gt_ref, lg_ref,
             dz0_ref, gw_hbm, st_ref, gw_acc, sem):
        i = pl.program_id(0)

        @pl.when(i == 0)
        def _():
            gw_acc[...] = jnp.zeros_like(gw_acc)
            st_ref[...] = jnp.zeros_like(st_ref)

        x1v = x1_ref[...]
        scale = 1.0 + sc_ref[...]
        h = (x1v * scale + sh_ref[...]).astype(MXU_DTYPE)
        dh = _in_proj_bwd_core(h, [du_ref[...], dg_ref[...]], w_ref, gw_acc)
        dx1 = ALPHA * dz1_ref[...] + dh * scale
        st_ref[0:1, :] += _rowsum(dh * x1v)
        st_ref[1:2, :] += _rowsum(dh)
        br = br_ref[...]
        z0 = ALPHA * x_ref[...] + gt_ref[...] * br
        xhat, rstd = _layer_norm_stats(z0)
        dz0 = _layer_norm_bwd(dx1, xhat, rstd, lg_ref[...])
        dz0_ref[...] = dz0
        st_ref[2:3, :] += _rowsum(dx1 * xhat)
        st_ref[3:4, :] += _rowsum(dx1)
        st_ref[4:5, :] += _rowsum(dz0 * br)

        @pl.when(i == nt - 1)
        def _():
            _flush(gw_acc, gw_hbm, sem)

    wide = pl.BlockSpec((tm, D_INNER), lambda i: (i, 0))
    nar = pl.BlockSpec((tm, D_MODEL), lambda i: (i, 0))
    row = pl.BlockSpec((1, D_MODEL), lambda i: (0, 0))
    return pl.pallas_call(
        body, name=name,
        out_shape=[jax.ShapeDtypeStruct((t, D_MODEL), F32), jax.ShapeDtypeStruct((N_WBLK, D_MODEL, WBLK), F32),
                   jax.ShapeDtypeStruct((SUBLANES, D_MODEL), F32)],
        grid=(nt,),
        in_specs=[wide, wide, nar, nar, row, row,
                  pl.BlockSpec((N_WBLK, D_MODEL, WBLK), lambda i: (0, 0, 0), pipeline_mode=pl.Buffered(1)),
                  nar, nar, row, row],
        out_specs=[nar, ANY, pl.BlockSpec((SUBLANES, D_MODEL), lambda i: (0, 0))],
        scratch_shapes=[pltpu.VMEM((N_WBLK, D_MODEL, WBLK), F32), pltpu.SemaphoreType.DMA(())],
        compiler_params=_cparams(dimension_semantics=("arbitrary",)),
    )(du, dg, x1, dz1, sc, sh, wg, xt, br0, gt0, lg0)


def _bin0(duvf, duvb, u, dg, xt, dz0, sc, sh, wg, conv_w, name):
    t = xt.shape[0]
    tm = min(TM_BWD, t)
    nt = t // tm
    latent = dg is not None
    n_blk = N_WBLK if latent else N_WBLK // 2
    per = tm // SUBLANES
    n_rows8 = t // SUBLANES

    def body(*refs):
        if latent:
            (df_ref, dfp_ref, dfn_ref, db_ref, dbp_ref, dbn_ref, u_ref, up_ref, un_ref, dg_ref, x_ref, dz_ref,
             sc_ref, sh_ref, w_ref, cw_ref, gx_ref, gw_hbm, st_ref, cst_ref, gw_acc, sem) = refs
        else:
            (df_ref, dfp_ref, dfn_ref, db_ref, dbp_ref, dbn_ref, u_ref, up_ref, un_ref, x_ref,
             sc_ref, sh_ref, w_ref, cw_ref, gw_hbm, st_ref, cst_ref, gw_acc, sem) = refs
        i = pl.program_id(0)

        @pl.when(i == 0)
        def _():
            gw_acc[...] = jnp.zeros_like(gw_acc)
            st_ref[...] = jnp.zeros_like(st_ref)
            cst_ref[...] = jnp.zeros_like(cst_ref)

        first, last = i == 0, i == nt - 1
        nz = jnp.where(last, 0.0, 1.0)
        xv = x_ref[...]
        scale = 1.0 + sc_ref[...]
        h = (xv * scale + sh_ref[...]).astype(MXU_DTYPE)
        cw = cw_ref[...]
        dh = None
        for k in range(N_WBLK // 2):
            sl = slice(k * WBLK, (k + 1) * WBLK)
            dout = df_ref[:, sl] + db_ref[:, sl]
            dnext = dfn_ref[:, sl] + dbn_ref[:, sl]
            _, dm1, dp1 = _conv_taps(dout, dfp_ref[:, sl] + dbp_ref[:, sl], dnext, first, last)
            dp2 = _shift_up(dp1, dnext[1:2] * nz)
            du = dp2 * cw[0:1, sl] + dp1 * cw[1:2, sl] + dout * cw[2:3, sl] + dm1 * cw[3:4, sl]
            u_t = u_ref[:, sl]
            um2, um1, up1 = _conv_taps(u_t, up_ref[:, sl], un_ref[:, sl], first, last)
            cst_ref[0:1, sl] += _rowsum(dout * um2)
            cst_ref[1:2, sl] += _rowsum(dout * um1)
            cst_ref[2:3, sl] += _rowsum(dout * u_t)
            cst_ref[3:4, sl] += _rowsum(dout * up1)
            cst_ref[4:5, sl] += _rowsum(dout)
            dk = du.astype(MXU_DTYPE)
            gw_acc[k] += _dot_tn(h, dk)
            contrib = _dot_nt(dk, w_ref[k])
            dh = contrib if dh is None else dh + contrib
        if latent:
            for k in range(N_WBLK // 2, N_WBLK):
                kk = k - N_WBLK // 2
                dk = dg_ref[:, kk * WBLK:(kk + 1) * WBLK].astype(MXU_DTYPE)
                gw_acc[k] += _dot_tn(h, dk)
                dh = dh + _dot_nt(dk, w_ref[k])
        st_ref[0:1, :] += _rowsum(dh * xv)
        st_ref[1:2, :] += _rowsum(dh)
        if latent:
            gx_ref[...] = ALPHA * dz_ref[...] + dh * scale

        @pl.when(i == nt - 1)
        def _():
            _flush(gw_acc, gw_hbm, sem)

    wide = pl.BlockSpec((tm, D_INNER), lambda i: (i, 0))
    prev = pl.BlockSpec((SUBLANES, D_INNER), lambda i: (jnp.maximum(i * per - 1, 0), 0))
    nxt = pl.BlockSpec((SUBLANES, D_INNER), lambda i: (jnp.minimum((i + 1) * per, n_rows8 - 1), 0))
    nar = pl.BlockSpec((tm, D_MODEL), lambda i: (i, 0))
    row = pl.BlockSpec((1, D_MODEL), lambda i: (0, 0))
    wspec = pl.BlockSpec((n_blk, D_MODEL, WBLK), lambda i: (0, 0, 0), pipeline_mode=pl.Buffered(1))
    cwspec = pl.BlockSpec((4, D_INNER), lambda i: (0, 0))
    stat = pl.BlockSpec((SUBLANES, D_MODEL), lambda i: (0, 0))
    cstat = pl.BlockSpec((SUBLANES, D_INNER), lambda i: (0, 0))
    halo3 = [wide, prev, nxt]
    if latent:
        in_specs = halo3 * 3 + [wide, nar, nar, row, row, wspec, cwspec]
        args = (duvf, duvf, duvf, duvb, duvb, duvb, u, u, u, dg, xt, dz0, sc, sh, wg, conv_w)
        out_shape = [jax.ShapeDtypeStruct((t, D_MODEL), F32)]
        out_specs = [nar]
    else:
        in_specs = halo3 * 3 + [nar, row, row, wspec, cwspec]
        args = (duvf, duvf, duvf, duvb, duvb, duvb, u, u, u, xt, sc, sh, wg, conv_w)
        out_shape, out_specs = [], []
    out_shape += [jax.ShapeDtypeStruct((n_blk, D_MODEL, WBLK), F32), jax.ShapeDtypeStruct((SUBLANES, D_MODEL), F32),
                  jax.ShapeDtypeStruct((SUBLANES, D_INNER), F32)]
    out_specs += [ANY, stat, cstat]
    return pl.pallas_call(
        body, name=name, out_shape=out_shape, grid=(nt,), in_specs=in_specs, out_specs=out_specs,
        scratch_shapes=[pltpu.VMEM((n_blk, D_MODEL, WBLK), F32), pltpu.SemaphoreType.DMA(())],
        compiler_params=_cparams(dimension_semantics=("arbitrary",)),
    )(*args)


def _blocks_by_device(a, axis):
    shape = a.shape
    a = a.reshape(shape[:axis] + (N_DEV, shape[axis] // N_DEV) + shape[axis + 1:])
    return jnp.moveaxis(a, axis, 0)


def _unblock(a, axis):
    a = jnp.moveaxis(a, 0, axis)
    shape = a.shape
    return a.reshape(shape[:axis] + (shape[axis] * shape[axis + 1],) + shape[axis + 2:])


def kernel(x, c, ctx, c_ctx, w_mod, b_mod, w_in, w_out, ln_g, ln_b, conv_w, conv_b, lru_wa, lru_ba, lru_wx, lru_bx, lru_lam, pool_w, pool_scale, loss_target, m_c_ctx, m_w_mod, m_b_mod, m_w_in, m_w_out, m_ln_g, m_ln_b, m_conv_w, m_conv_b, m_lru_wa, m_lru_ba, m_lru_wx, m_lru_bx, m_lru_lam, m_pool_w, m_pool_scale, v_c_ctx, v_w_mod, v_b_mod, v_w_in, v_w_out, v_ln_g, v_ln_b, v_conv_w, v_conv_b, v_lru_wa, v_lru_ba, v_lru_wx, v_lru_bx, v_lru_lam, v_pool_w, v_pool_scale):
    xi, yi, ci = _my_pos()
    dev = 4 * xi + 2 * yi + ci
    xt, ctxt, tgt = x[0], ctx[0], loss_target[0]
    n_mod = w_mod.shape[2]

    small_shapes = [(D_MODEL,), conv_w.shape[1:], lru_ba.shape[1:], lru_bx.shape[1:], lru_lam.shape[1:],
                    pool_scale.shape[1:]]
    small = _to_rows([c[0], conv_w[0], lru_ba[0], lru_bx[0], lru_lam[0], pool_scale[0]], SUBLANES)
    small_all, = _all_gather([small], "gather_small")
    pieces = [_split_rows(small_all[k], small_shapes) for k in range(N_DEV)]
    c_all = jnp.stack([p[0] for p in pieces])
    conv_w_f = jnp.concatenate([p[1] for p in pieces], axis=-1)
    lru_ba_f = jnp.concatenate([p[2] for p in pieces], axis=-1)[:, None, :]
    lru_bx_f = jnp.concatenate([p[3] for p in pieces], axis=-1)[:, None, :]
    lru_lam_f = jnp.concatenate([p[4] for p in pieces], axis=-1)[:, None, :]
    pool_scale_f = jnp.concatenate([p[5] for p in pieces], axis=-1)[None, :]

    cond = jnp.concatenate([c_all, jnp.broadcast_to(c_ctx[None, :], (N_DEV, D_MODEL))], axis=0)
    b_my = lax.dynamic_slice(b_mod, (0, dev * n_mod), (2, n_mod))[:, None, :]
    mod_part = _mod_fwd(cond, w_mod, b_my, "mod_fwd")
    mod_all, = _all_gather([mod_part], "gather_mod")
    mod = jnp.transpose(mod_all, (1, 2, 0, 3)).reshape(2, 16, 3 * D_MODEL)
    mod_me = lax.dynamic_slice(mod, (0, dev, 0), (2, 1, 3 * D_MODEL))
    sh = [mod_me[i, :, 0:D_MODEL] for i in range(2)]
    sc = [mod_me[i, :, D_MODEL:2 * D_MODEL] for i in range(2)]
    gt = [mod_me[i, :, 2 * D_MODEL:] for i in range(2)]
    shc, scc = mod[0, 8:9, 0:D_MODEL], mod[0, 8:9, D_MODEL:2 * D_MODEL]

    w_in_g, w_out_g, pool_w_g = _all_gather(
        [w_in.astype(MXU_DTYPE), w_out.astype(MXU_DTYPE), pool_w[0].astype(MXU_DTYPE)], "gather_weights")
    w_in_l = [w_in_g[:, i] for i in range(2)]
    w_out_l = [w_out_g[:, i].reshape(D_INNER, D_MODEL) for i in range(2)]
    pool_w_f = jnp.transpose(pool_w_g, (1, 0, 2, 3)).reshape(len(POOL_WINDOWS), POOL_GROUP, POOL_GROUP)
    lg = [ln_g[i][None, :] for i in range(2)]
    lb = [ln_b[i][None, :] for i in range(2)]
    lru_p = dict(conv_w=conv_w_f, conv_b=conv_b, wa=lru_wa[0].astype(MXU_DTYPE), wx=lru_wx[0].astype(MXU_DTYPE),
                 ba=lru_ba_f, bx=lru_bx_f, lam=lru_lam_f)
    zero_state = jnp.zeros((1, D_INNER), F32)

    u0, g0 = _in_proj(xt, sc[0], sh[0], w_in_l[0], "in_proj0")
    uc, _ = _in_proj(ctxt, scc, shc, w_in_l[0], "in_proj0_ctx")
    hcf, cf = _lru_fwd(uc, zero_state, lru_p, 0, "lru_fwd_ctx_f")
    hcb, cbk = _lru_fwd(uc, zero_state, lru_p, 1, "lru_fwd_ctx_b")
    hf, _ = _lru_fwd(u0, cf, lru_p, 0, "lru_fwd_f")
    hb, _ = _lru_fwd(u0, cbk, lru_p, 1, "lru_fwd_b")
    x1, br0 = _out0(hf, hb, g0, xt, gt[0], w_out_l[0], lg[0], lb[0], "out0")
    u1, g1 = _in_proj(x1, sc[1], sh[1], w_in_l[1], "in_proj1")
    dmix = _pool_mix(u1, False, MXU_DTYPE, "pool_fwd")
    dz1, st1 = _out1(dmix, pool_w_f, pool_scale_f, g1, x1, gt[1], w_out_l[1], lg[1], lb[1], tgt, "out1")
    loss = lax.psum((0.5 / D_MODEL) * jnp.sum(st1[3]), ("x", "y", "c"))

    dd, dg1, gwo1, gpw, gps = _bout1(dz1, dmix, g1, pool_w_f, pool_scale_f, gt[1], w_out_l[1], "bwd_out1")
    du1 = _pool_mix(dd, True, F32, "pool_bwd")
    dz0, gwi1, stb1 = _bin1(du1, dg1, x1, dz1, sc[1], sh[1], w_in_l[1], xt, br0, gt[0], lg[0], "bwd_in1")
    dy0, dg0, gwo0 = _bout0(dz0, hf, hb, g0, gt[0], w_out_l[0], "bwd_out0")
    duvf, gwa_f, gwx_f, gv_f, dh0f = _lru_bwd(u0, dy0, hf, cf, zero_state, lru_p, 0, "lru_bwd_f")
    duvb, gwa_b, gwx_b, gv_b, dh0b = _lru_bwd(u0, dy0, hb, cbk, zero_state, lru_p, 1, "lru_bwd_b")
    zero_dh = jnp.zeros_like(uc)
    ducf, gwa_cf, gwx_cf, gv_cf, _ = _lru_bwd(uc, zero_dh, hcf, zero_state, dh0f, lru_p, 0, "lru_bwd_ctx_f")
    ducb, gwa_cb, gwx_cb, gv_cb, _ = _lru_bwd(uc, zero_dh, hcb, zero_state, dh0b, lru_p, 1, "lru_bwd_ctx_b")
    gx, gwi0, stb0, cst0 = _bin0(duvf, duvb, u0, dg0, xt, dz0, sc[0], sh[0], w_in_l[0], conv_w_f, "bwd_in0")
    gwic, stc, cstc = _bin0(ducf, ducb, uc, None, ctxt, None, scc, shc, w_in_l[0][:N_WBLK // 2], conv_w_f,
                            "bwd_in0_ctx")
    gwi0 = gwi0.at[:N_WBLK // 2].add(gwic)

    zero_row = jnp.zeros((1, D_MODEL), F32)
    dm_me = jnp.stack([
        jnp.concatenate([jnp.concatenate([stb0[1:2], stb0[0:1], stb1[4:5]], axis=1),
                         jnp.concatenate([stc[1:2], stc[0:1], zero_row], axis=1)], axis=0),
        jnp.concatenate([jnp.concatenate([stb1[1:2], stb1[0:1], st1[2:3]], axis=1),
                         jnp.zeros((1, 3 * D_MODEL), F32)], axis=0)])
    dm_g, = _all_gather([dm_me], "gather_dmod")
    dm_all = jnp.concatenate([jnp.transpose(dm_g[:, :, 0], (1, 0, 2)), jnp.transpose(dm_g[:, :, 1], (1, 0, 2))],
                             axis=1)
    dm_my = lax.dynamic_slice(dm_all, (0, 0, dev * n_mod), (2, 16, n_mod))
    g_w_mod, g_b_mod, gcc_part = _mod_bwd(cond, dm_all, dm_my, w_mod, "mod_bwd")
    g_b_mod = g_b_mod.reshape(b_mod.shape)

    gwa = jnp.stack([gwa_f + gwa_cf, gwa_b + gwa_cb])
    gwx = jnp.stack([gwx_f + gwx_cf, gwx_b + gwx_cb])
    gv = jnp.stack([gv_f + gv_cf, gv_b + gv_cb])
    cst = cst0 + cstc
    g_ln_g = jnp.stack([stb1[2], st1[0]])
    g_ln_b = jnp.stack([stb1[3], st1[1]])
    sharded = [
        jnp.stack([gwi0, gwi1], axis=1),
        _blocks_by_device(jnp.stack([gwo0, gwo1]), 1),
        _blocks_by_device(gpw, 1),
        _blocks_by_device(cst[0:4], 1),
        _blocks_by_device(gv[:, 0], 1), _blocks_by_device(gv[:, 1], 1), _blocks_by_device(gv[:, 2], 1),
        _blocks_by_device(gps[0], 0),
    ]
    replicated = [gwa.reshape(-1), gwx.reshape(-1), g_ln_g.reshape(-1), g_ln_b.reshape(-1), cst[4],
                  gcc_part.reshape(-1)]
    sh_sizes = [int(np.prod(a.shape[1:])) for a in sharded]
    rep_sizes = [a.shape[0] // N_DEV for a in replicated]
    n_flat = sum(sh_sizes) + sum(rep_sizes)
    rows = -(-n_flat // LANES)
    rows = -(-rows // FLAT_TR) * FLAT_TR
    buf = jnp.concatenate([a.reshape(N_DEV, -1) for a in sharded] +
                          [a.reshape(N_DEV, -1) for a in replicated], axis=1)
    buf = jnp.pad(buf, ((0, 0), (0, rows * LANES - n_flat))).reshape(N_DEV, rows, LANES)
    recv = _sibling_exchange(buf, "reduce_sibling")
    part = _pair_sum(buf, recv, jnp.reshape(ci, (1,)).astype(jnp.int32), "reduce_pair_sum")
    parts = _chip_exchange(part, "reduce_chips")
    g_flat = _sum4(parts, "reduce_sum4").reshape(-1)

    offs = np.cumsum([0] + sh_sizes + rep_sizes)
    n_sh = len(sh_sizes)
    sh_shapes = [w_in.shape, w_out.shape, pool_w.shape, conv_w.shape, lru_ba.shape, lru_bx.shape, lru_lam.shape,
                 pool_scale.shape]
    g_sh = [g_flat[offs[k]:offs[k + 1]].reshape(sh_shapes[k]) for k in range(n_sh)]
    g_w_in, g_w_out, g_pool_w, g_conv_w, g_lru_ba, g_lru_bx, g_lru_lam, g_pool_scale = g_sh
    rep_block = _to_rows([g_flat[offs[n_sh]:offs[-1]]], SUBLANES)
    rep_all, = _all_gather([rep_block], "gather_replicated")
    rep_flat = rep_all.reshape(N_DEV, -1)
    rep_full, off = [], 0
    for n in rep_sizes:
        rep_full.append(rep_flat[:, off:off + n].reshape(-1))
        off += n
    g_lru_wa = rep_full[0].reshape(lru_wa.shape)
    g_lru_wx = rep_full[1].reshape(lru_wx.shape)
    g_ln_g = rep_full[2].reshape(ln_g.shape)
    g_ln_b = rep_full[3].reshape(ln_b.shape)
    g_conv_b = rep_full[4].reshape(conv_b.shape)
    g_c_ctx = rep_full[5].reshape(c_ctx.shape)

    names = ["c_ctx", "w_mod", "b_mod", "w_in", "w_out", "ln_g", "ln_b", "conv_w", "conv_b", "lru_wa", "lru_ba",
             "lru_wx", "lru_bx", "lru_lam", "pool_w", "pool_scale"]
    weights = dict(c_ctx=c_ctx, w_mod=w_mod, b_mod=b_mod, w_in=w_in, w_out=w_out, ln_g=ln_g, ln_b=ln_b,
                   conv_w=conv_w, conv_b=conv_b, lru_wa=lru_wa, lru_ba=lru_ba, lru_wx=lru_wx, lru_bx=lru_bx,
                   lru_lam=lru_lam, pool_w=pool_w, pool_scale=pool_scale)
    mom_m = dict(c_ctx=m_c_ctx, w_mod=m_w_mod, b_mod=m_b_mod, w_in=m_w_in, w_out=m_w_out, ln_g=m_ln_g, ln_b=m_ln_b,
                 conv_w=m_conv_w, conv_b=m_conv_b, lru_wa=m_lru_wa, lru_ba=m_lru_ba, lru_wx=m_lru_wx,
                 lru_bx=m_lru_bx, lru_lam=m_lru_lam, pool_w=m_pool_w, pool_scale=m_pool_scale)
    mom_v = dict(c_ctx=v_c_ctx, w_mod=v_w_mod, b_mod=v_b_mod, w_in=v_w_in, w_out=v_w_out, ln_g=v_ln_g, ln_b=v_ln_b,
                 conv_w=v_conv_w, conv_b=v_conv_b, lru_wa=v_lru_wa, lru_ba=v_lru_ba, lru_wx=v_lru_wx,
                 lru_bx=v_lru_bx, lru_lam=v_lru_lam, pool_w=v_pool_w, pool_scale=v_pool_scale)
    grads = dict(c_ctx=g_c_ctx, w_mod=g_w_mod, b_mod=g_b_mod, w_in=g_w_in, w_out=g_w_out, ln_g=g_ln_g, ln_b=g_ln_b,
                 conv_w=g_conv_w, conv_b=g_conv_b, lru_wa=g_lru_wa, lru_ba=g_lru_ba, lru_wx=g_lru_wx,
                 lru_bx=g_lru_bx, lru_lam=g_lru_lam, pool_w=g_pool_w, pool_scale=g_pool_scale)
    shapes = [weights[n].shape for n in names]
    flat = lambda d: _to_rows([d[n] for n in names], FLAT_TR)
    delta_r, new_m_r, new_v_r = _adamw(flat(weights), flat(grads), flat(mom_m), flat(mom_v), "adamw")
    delta = _split_rows(delta_r, shapes)
    new_m = _split_rows(new_m_r, shapes)
    new_v = _split_rows(new_v_r, shapes)

    return (loss, gx[None], *[grads[n] for n in names], *delta, *new_m, *new_v)
```

```python
import functools

import numpy as np
import jax
import jax.numpy as jnp
from jax import lax
from jax.experimental import pallas as pl
from jax.experimental.pallas import tpu as pltpu

F32 = jnp.float32
BF16 = jnp.bfloat16
MXU_DTYPE = BF16

D_MODEL = 1024
D_INNER = 2048
LRU_BLOCK = 128
GRID_W = 64
POOL_WINDOWS = (2, 4, 8, 16)
POOL_GROUP = 512
ALPHA = float(4 ** 0.25)
LN_EPS = 1e-5
LRU_C = 8.0
N_DEV = 8
N_WBLK = 8
WBLK = 512

ADAM_LR = 0.001
ADAM_B1 = 0.9
ADAM_B2 = 0.999
ADAM_EPS = 1e-08
ADAM_WD = 0.01
ADAM_STEP = 10

LANES = 128
SUBLANES = 8
V7X_VMEM_BYTES = 64 * 1024 * 1024
VMEM_LIMIT = V7X_VMEM_BYTES - 8 * 1024 * 1024
MESH = pl.DeviceIdType.MESH
ANY = pl.BlockSpec(memory_space=pl.ANY)

TM_MM = 512
TM_BWD = 256
TM_LRU = 512
CB_LRU = 512
N_SEG = 8
FLAT_ROWS = 16
ELEMENTWISE_TILE_BYTES = 1 << 20
POOL_TOK = 256
WIRE_DTYPE = BF16


def _cparams(**kw):
    return pltpu.CompilerParams(vmem_limit_bytes=VMEM_LIMIT, **kw)


def _my_pos():
    return lax.axis_index("x"), lax.axis_index("y"), lax.axis_index("c")


def _dot(a, b):
    return jnp.dot(a.astype(MXU_DTYPE), b.astype(MXU_DTYPE), preferred_element_type=F32)


def _dot_tn(a, b):
    return lax.dot_general(a.astype(MXU_DTYPE), b.astype(MXU_DTYPE), (((0,), (0,)), ((), ())),
                           preferred_element_type=F32)


def _dot_nt(a, b):
    return lax.dot_general(a.astype(MXU_DTYPE), b.astype(MXU_DTYPE), (((1,), (1,)), ((), ())),
                           preferred_element_type=F32)


def _sigmoid(z):
    return 0.5 * jnp.tanh(0.5 * z) + 0.5


def _one_minus_sq(la, a):
    x = -2.0 * la
    series = x * (1.0 - x * (0.5 - x * (1.0 / 6.0)))
    return jnp.where(x < 0.01, series, 1.0 - a * a)


def _log_sigmoid(x):
    y = jnp.exp(-jnp.abs(x))
    u = 1.0 + y
    l1p = jnp.where(u == 1.0, y, jnp.log(u) * (y / jnp.where(u == 1.0, 1.0, u - 1.0)))
    return jnp.minimum(x, 0.0) - l1p


def _rowsum(v):
    return jnp.sum(v, axis=0, keepdims=True)


def _layer_norm_stats(z):
    mu = jnp.mean(z, axis=-1, keepdims=True)
    zc = z - mu
    var = jnp.mean(zc * zc, axis=-1, keepdims=True)
    rstd = lax.rsqrt(var + LN_EPS)
    return zc * rstd, rstd


def _layer_norm_bwd(dy, xhat, rstd, g):
    dxh = dy * g
    m1 = jnp.mean(dxh, axis=-1, keepdims=True)
    m2 = jnp.mean(dxh * xhat, axis=-1, keepdims=True)
    return rstd * (dxh - m1 - xhat * m2)


def _shifted(v, before8, after8, offsets):
    n = v.shape[0]
    ext = jnp.concatenate([before8, v, after8], axis=0)
    total = n + 2 * SUBLANES
    return [pltpu.roll(ext, (-k) % total, 0)[SUBLANES:SUBLANES + n] for k in offsets]


def _rows8(row):
    return jnp.broadcast_to(row, (SUBLANES, row.shape[1]))


def _shift_down(v, first_row):
    return _shifted(v, _rows8(first_row), _rows8(first_row), [-1])[0]


def _shift_up(v, last_row):
    return _shifted(v, _rows8(last_row), _rows8(last_row), [1])[0]


def _all_gather(blocks, name):
    n = len(blocks)

    def body(*refs):
        x_refs, out_refs = refs[:n], refs[n:2 * n]
        send_sems, recv_sems, local_sems = refs[2 * n:]
        x, y, c = _my_pos()
        me, sibling = (x, y, c), (x, y, 1 - c)
        chips = [(1 - x, y), (x, 1 - y), (1 - x, 1 - y)]

        def slot(a, px, py, pc):
            return out_refs[a].at[4 * px + 2 * py + pc]

        def copy(a, k, block, to, src=None):
            return pltpu.make_async_remote_copy(
                src_ref=slot(a, *block) if src is None else src, dst_ref=slot(a, *block),
                send_sem=send_sems.at[a, k], recv_sem=recv_sems.at[a, k], device_id=to, device_id_type=MESH)

        mine = [pltpu.make_async_copy(x_refs[a], slot(a, *me), local_sems.at[a]) for a in range(n)]
        for cp in mine:
            cp.start()
        first = []
        for a in range(n):
            first.append(copy(a, 0, me, sibling, src=x_refs[a]))
            first += [copy(a, 1 + j, me, (*chip, c), src=x_refs[a]) for j, chip in enumerate(chips)]
        for cp in first:
            cp.start()
        passed = []
        for j, chip in enumerate(chips):
            for a in range(n):
                copy(a, 1 + j, (*chip, c), me).wait_recv()
                fwd = copy(a, 4 + j, (*chip, c), sibling)
                fwd.start()
                passed.append(fwd)
        for a in range(n):
            copy(a, 0, sibling, me).wait_recv()
            for j, chip in enumerate(chips):
                copy(a, 4 + j, (*chip, 1 - c), me).wait_recv()
        for cp in first + passed:
            cp.wait_send()
        for cp in mine:
            cp.wait()

    outs = pl.pallas_call(
        body, name=name,
        out_shape=[jax.ShapeDtypeStruct((N_DEV,) + b.shape, b.dtype) for b in blocks],
        in_specs=[ANY] * n, out_specs=[ANY] * n,
        scratch_shapes=[pltpu.SemaphoreType.DMA((n, 7)), pltpu.SemaphoreType.DMA((n, 7)),
                        pltpu.SemaphoreType.DMA((n,))],
    )(*blocks)
    return list(outs)


def _sibling_exchange(bufs, name):
    n = len(bufs)

    def body(*refs):
        srcs, outs = refs[:n], refs[n:2 * n]
        send_sems, recv_sems = refs[2 * n:]
        x, y, c = _my_pos()
        copies = [pltpu.make_async_remote_copy(
            src_ref=srcs[a].at[2 * j + (1 - c)], dst_ref=outs[a].at[j], send_sem=send_sems.at[a, j],
            recv_sem=recv_sems.at[a, j], device_id=(x, y, 1 - c), device_id_type=MESH)
            for a in range(n) for j in range(4)]
        for cp in copies:
            cp.start()
        for cp in copies:
            cp.wait()

    outs = pl.pallas_call(
        body, name=name, out_shape=[jax.ShapeDtypeStruct((4,) + b.shape[1:], b.dtype) for b in bufs],
        in_specs=[ANY] * n, out_specs=[ANY] * n,
        scratch_shapes=[pltpu.SemaphoreType.DMA((n, 4)), pltpu.SemaphoreType.DMA((n, 4))],
    )(*bufs)
    return list(outs)


def _chip_exchange(parts, name):
    n = len(parts)

    def body(*refs):
        srcs, outs = refs[:n], refs[n:2 * n]
        send_sems, recv_sems, local_sems = refs[2 * n:]
        x, y, c = _my_pos()
        jme = 2 * x + y
        peers = [(1 - x, y), (x, 1 - y), (1 - x, 1 - y)]
        local = [pltpu.make_async_copy(srcs[a].at[jme], outs[a].at[jme], local_sems.at[a]) for a in range(n)]
        for cp in local:
            cp.start()

        def copy(a, k, px, py, dst_slot):
            return pltpu.make_async_remote_copy(
                src_ref=srcs[a].at[2 * px + py], dst_ref=outs[a].at[dst_slot], send_sem=send_sems.at[a, k],
                recv_sem=recv_sems.at[a, k], device_id=(px, py, c), device_id_type=MESH)

        sends = [copy(a, k, px, py, jme) for a in range(n) for k, (px, py) in enumerate(peers)]
        for cp in sends:
            cp.start()
        for a in range(n):
            for k, (px, py) in enumerate(peers):
                copy(a, k, px, py, 2 * px + py).wait_recv()
        for cp in sends:
            cp.wait_send()
        for cp in local:
            cp.wait()

    outs = pl.pallas_call(
        body, name=name, out_shape=[jax.ShapeDtypeStruct(p.shape, p.dtype) for p in parts],
        in_specs=[ANY] * n, out_specs=[ANY] * n,
        scratch_shapes=[pltpu.SemaphoreType.DMA((n, 3)), pltpu.SemaphoreType.DMA((n, 3)),
                        pltpu.SemaphoreType.DMA((n,))],
    )(*parts)
    return list(outs)


def _row_tile(r, l):
    t = min(r, max(16, ELEMENTWISE_TILE_BYTES // (4 * l) // 16 * 16))
    while r % t:
        t -= 16
    return t


def _pair_sum(buf, recv, core, name):
    _, r, l = buf.shape
    tr = _row_tile(r, l)

    def body(core_ref, a_ref, b_ref, o_ref):
        o_ref[...] = (a_ref[...] + b_ref[...]).astype(WIRE_DTYPE)

    return pl.pallas_call(
        body, name=name, out_shape=jax.ShapeDtypeStruct((4, r, l), WIRE_DTYPE),
        grid_spec=pltpu.PrefetchScalarGridSpec(
            num_scalar_prefetch=1, grid=(4, r // tr),
            in_specs=[pl.BlockSpec((None, tr, l), lambda j, i, cr: (2 * j + cr[0], i, 0)),
                      pl.BlockSpec((None, tr, l), lambda j, i, cr: (j, i, 0))],
            out_specs=pl.BlockSpec((None, tr, l), lambda j, i, cr: (j, i, 0))),
        compiler_params=_cparams(dimension_semantics=("arbitrary", "arbitrary")),
    )(core, buf, recv)


def _sum_parts(p_ref):
    return ((p_ref[0].astype(F32) + p_ref[1].astype(F32)) + (p_ref[2].astype(F32) + p_ref[3].astype(F32)))


def _sum4(parts, name):
    _, r, l = parts.shape
    tr = _row_tile(r, l)

    def body(p_ref, o_ref):
        o_ref[...] = _sum_parts(p_ref)

    return pl.pallas_call(
        body, name=name, out_shape=jax.ShapeDtypeStruct((r, l), F32), grid=(r // tr,),
        in_specs=[pl.BlockSpec((4, tr, l), lambda i: (0, i, 0))],
        out_specs=pl.BlockSpec((tr, l), lambda i: (i, 0)),
        compiler_params=_cparams(dimension_semantics=("arbitrary",)),
    )(parts)


def _adamw_update(w, gg, m, v):
    nm = ADAM_B1 * m + (1.0 - ADAM_B1) * gg
    nv = ADAM_B2 * v + (1.0 - ADAM_B2) * (gg * gg)
    m_hat = nm / (1.0 - ADAM_B1 ** ADAM_STEP)
    v_hat = nv / (1.0 - ADAM_B2 ** ADAM_STEP)
    return -ADAM_LR * (m_hat / (jnp.sqrt(v_hat) + ADAM_EPS) + ADAM_WD * w), nm, nv


def _adamw(w, g, m, v, name):
    r, l = w.shape
    tr = _row_tile(r, l)

    def body(w_ref, g_ref, m_ref, v_ref, d_ref, nm_ref, nv_ref):
        d_ref[...], nm_ref[...], nv_ref[...] = _adamw_update(w_ref[...], g_ref[...], m_ref[...], v_ref[...])

    spec = pl.BlockSpec((tr, l), lambda i: (i, 0))
    return pl.pallas_call(
        body, name=name, out_shape=[jax.ShapeDtypeStruct((r, l), F32)] * 3, grid=(r // tr,),
        in_specs=[spec] * 4, out_specs=[spec] * 3,
        compiler_params=_cparams(dimension_semantics=("arbitrary",)),
    )(w, g, m, v)


def _adamw_parts(w, parts, m, v, name):
    nl, r, l = w.shape
    tr = _row_tile(r, l)

    def body(*refs):
        w_ref, p_refs, (m_ref, v_ref, g_ref, d_ref, nm_ref, nv_ref) = refs[0], refs[1:1 + nl], refs[1 + nl:]
        layer = pl.program_id(0)
        gg = _sum_parts(p_refs[0])
        for q in range(1, nl):
            gg = jnp.where(layer == q, _sum_parts(p_refs[q]), gg)
        g_ref[...] = gg
        d_ref[...], nm_ref[...], nv_ref[...] = _adamw_update(w_ref[...], gg, m_ref[...], v_ref[...])

    spec = pl.BlockSpec((None, tr, l), lambda q, i: (q, i, 0))
    pspecs = [pl.BlockSpec((4, tr, l), lambda q, i, k=k: (0, jnp.where(q == k, i, 0), 0)) for k in range(nl)]
    return pl.pallas_call(
        body, name=name, out_shape=[jax.ShapeDtypeStruct((nl, r, l), F32)] * 4, grid=(nl, r // tr),
        in_specs=[spec] + pspecs + [spec, spec], out_specs=[spec] * 4,
        compiler_params=_cparams(dimension_semantics=("arbitrary", "arbitrary")),
    )(w, *parts, m, v)


def _to_rows(pieces, row_multiple):
    flat = jnp.concatenate([p.reshape(-1) for p in pieces])
    rows = -(-flat.shape[0] // LANES)
    rows = -(-rows // row_multiple) * row_multiple
    flat = jnp.pad(flat, (0, rows * LANES - flat.shape[0]))
    return flat.reshape(rows, LANES)


def _split_rows(rows, shapes):
    flat = rows.reshape(-1)
    out, off = [], 0
    for s in shapes:
        n = int(np.prod(s))
        out.append(flat[off:off + n].reshape(s))
        off += n
    return out


def _mod_fwd(cond, w_mod, b_my, name):
    nl, _, ncol = w_mod.shape

    def body(a_ref, w_ref, b_ref, o_ref):
        a = a_ref[...]
        s = a * _sigmoid(a)
        for i in range(nl):
            o_ref[i] = _dot(s, w_ref[i]) + b_ref[i]

    return pl.pallas_call(
        body, name=name, out_shape=jax.ShapeDtypeStruct((nl, 16, ncol), F32),
        compiler_params=_cparams(),
    )(cond, w_mod, b_my)


def _mod_bwd(cond, dm_all, dm_my, w_mod, name):
    nl, _, ncol = w_mod.shape

    def body(a_ref, dma_ref, dmm_ref, w_ref, gw_ref, gb_ref, gc_ref):
        a = a_ref[...]
        sg = _sigmoid(a)
        s = a * sg
        for i in range(nl):
            gw_ref[i] = _dot_tn(s, dmm_ref[i])
            gb_ref[i] = jnp.sum(dma_ref[i], axis=0, keepdims=True)
        back = _dot_nt(dmm_ref[0], w_ref[0])
        dsilu = sg * (1.0 + a * (1.0 - sg))
        gc_ref[...] = jnp.sum(back[8:16] * dsilu[8:16], axis=0, keepdims=True)

    return pl.pallas_call(
        body, name=name,
        out_shape=[jax.ShapeDtypeStruct((nl, D_MODEL, ncol), F32), jax.ShapeDtypeStruct((nl, 1, 3 * D_MODEL), F32),
                   jax.ShapeDtypeStruct((1, D_MODEL), F32)],
        compiler_params=_cparams(),
    )(cond, dm_all, dm_my, w_mod)


def _in_proj(xt, sc, sh, wg, name):
    t = xt.shape[0]
    tm = min(TM_MM, t)

    def body(x_ref, sc_ref, sh_ref, w_ref, u_ref, g_ref):
        h = (x_ref[...] * (1.0 + sc_ref[...]) + sh_ref[...]).astype(MXU_DTYPE)
        for k in range(N_WBLK):
            o = jnp.dot(h, w_ref[k], preferred_element_type=F32)
            if k < N_WBLK // 2:
                u_ref[:, k * WBLK:(k + 1) * WBLK] = o
            else:
                kk = k - N_WBLK // 2
                g_ref[:, kk * WBLK:(kk + 1) * WBLK] = o

    row = pl.BlockSpec((1, D_MODEL), lambda i: (0, 0))
    return pl.pallas_call(
        body, name=name, out_shape=[jax.ShapeDtypeStruct((t, D_INNER), F32)] * 2, grid=(t // tm,),
        in_specs=[pl.BlockSpec((tm, D_MODEL), lambda i: (i, 0)), row, row,
                  pl.BlockSpec((N_WBLK, D_MODEL, WBLK), lambda i: (0, 0, 0), pipeline_mode=pl.Buffered(1))],
        out_specs=[pl.BlockSpec((tm, D_INNER), lambda i: (i, 0))] * 2,
        compiler_params=_cparams(dimension_semantics=("arbitrary",)),
    )(xt, sc, sh, wg)


def _halo_maps(nt, tm, n_rows8, pos):
    per = tm // SUBLANES
    prev = lambda cb, i: (jnp.maximum(pos(i) * per - 1, 0), cb)
    nxt = lambda cb, i: (jnp.minimum((pos(i) + 1) * per, n_rows8 - 1), cb)
    return prev, nxt


def _conv_taps(u, prev8, next8, is_first, is_last):
    pz = jnp.where(is_first, 0.0, 1.0)
    nz = jnp.where(is_last, 0.0, 1.0)
    return _shifted(u, prev8 * pz, next8 * nz, [-2, -1, 1])


def _lru_gates(uv, wa_ref, wx_ref, ba, bx, cl, g):
    sl = slice(g * LANES, (g + 1) * LANES)
    uvg = uv[:, sl]
    r = _sigmoid(_dot(uvg, wa_ref[g]) + ba[:, sl])
    ii = _sigmoid(_dot(uvg, wx_ref[g]) + bx[:, sl])
    la = cl[:, sl] * r
    a = jnp.exp(la)
    s = jnp.sqrt(_one_minus_sq(la, a))
    return uvg, r, ii, a, s


def _scan_tile(a_s, b_s, carry_ref, write_out, seg, reverse):
    n_g = a_s.shape[0]
    stride = a_s.shape[1] // N_SEG

    def step(k, state):
        t = (seg - 1 - k) if reverse else k
        hs, cs = state
        nh, nc = [], []
        for g in range(n_g):
            a = a_s[g, pl.ds(t, N_SEG, stride=stride), :]
            b = b_s[g, pl.ds(t, N_SEG, stride=stride), :]
            h = a * hs[g] + b
            cum = a * cs[g]
            b_s[g, pl.ds(t, N_SEG, stride=stride), :] = h
            a_s[g, pl.ds(t, N_SEG, stride=stride), :] = cum
            nh.append(h)
            nc.append(cum)
        return tuple(nh), tuple(nc)

    zeros = tuple(jnp.zeros((N_SEG, LANES), F32) for _ in range(n_g))
    ones = tuple(jnp.ones((N_SEG, LANES), F32) for _ in range(n_g))
    h_fin, a_fin = lax.fori_loop(0, seg, step, (zeros, ones))

    order = list(range(N_SEG - 1, -1, -1)) if reverse else list(range(N_SEG))
    for g in range(n_g):
        carry = carry_ref[:, g * LANES:(g + 1) * LANES]
        for j in order:
            rows = pl.ds(j * stride, seg)
            write_out(j, g, b_s[g, rows, :] + a_s[g, rows, :] * carry)
            carry = a_fin[g][j:j + 1] * carry + h_fin[g][j:j + 1]
        carry_ref[:, g * LANES:(g + 1) * LANES] = carry


def _lru_specs(s, tm, cb, direction_pos, nt):
    n_rows8 = s // SUBLANES
    prev, nxt = _halo_maps(nt, tm, n_rows8, direction_pos)
    tile = pl.BlockSpec((tm, cb), lambda c, i: (direction_pos(i), c))
    return tile, pl.BlockSpec((SUBLANES, cb), prev), pl.BlockSpec((SUBLANES, cb), nxt)


def _lru_param_specs(cb, d):
    n_g = cb // LANES
    vec = pl.BlockSpec((1, cb), lambda c, i: (0, c))
    dvec = pl.BlockSpec((None, 1, cb), lambda c, i: (d, 0, c))
    wmat = pl.BlockSpec((None, n_g, LRU_BLOCK, LRU_BLOCK), lambda c, i: (d, c, 0, 0))
    return vec, dvec, wmat


def _lru_fwd(u, h0, p, d, name):
    s = u.shape[0]
    tm = min(TM_LRU, s)
    cb = CB_LRU
    n_g = cb // LANES
    nt = s // tm
    seg = tm // N_SEG
    stride = seg + SUBLANES
    pos = (lambda i: i) if d == 0 else (lambda i: nt - 1 - i)

    def body(u_ref, up_ref, un_ref, cw_ref, cbias_ref, wa_ref, wx_ref, ba_ref, bx_ref, lam_ref, h0_ref,
             h_ref, hc_ref, a_s, b_s):
        i = pl.program_id(1)
        tp = pos(i)

        @pl.when(i == 0)
        def _():
            hc_ref[...] = h0_ref[...]

        u_t = u_ref[...]
        um2, um1, up1 = _conv_taps(u_t, up_ref[...], un_ref[...], tp == 0, tp == nt - 1)
        cw = cw_ref[...]
        uv = um2 * cw[0:1] + um1 * cw[1:2] + u_t * cw[2:3] + up1 * cw[3:4] + cbias_ref[...]
        cl = LRU_C * _log_sigmoid(lam_ref[...])
        ba, bx = ba_ref[...], bx_ref[...]
        for g in range(n_g):
            uvg, r, ii, a, sq = _lru_gates(uv, wa_ref, wx_ref, ba, bx, cl, g)
            b = sq * (ii * uvg)
            for j in range(N_SEG):
                a_s[g, pl.ds(j * stride, seg), :] = a[j * seg:(j + 1) * seg]
                b_s[g, pl.ds(j * stride, seg), :] = b[j * seg:(j + 1) * seg]

        def write_out(j, g, h):
            h_ref[pl.ds(j * seg, seg), pl.ds(g * LANES, LANES)] = h

        _scan_tile(a_s, b_s, hc_ref, write_out, seg, reverse=(d == 1))

    tile, prev, nxt = _lru_specs(s, tm, cb, pos, nt)
    vec, dvec, wmat = _lru_param_specs(cb, d)
    return pl.pallas_call(
        body, name=name,
        out_shape=[jax.ShapeDtypeStruct((s, D_INNER), F32), jax.ShapeDtypeStruct((1, D_INNER), F32)],
        grid=(D_INNER // cb, nt),
        in_specs=[tile, prev, nxt, pl.BlockSpec((4, cb), lambda c, i: (0, c)), vec, wmat, wmat, dvec, dvec, dvec, vec],
        out_specs=[tile, vec],
        scratch_shapes=[pltpu.VMEM((n_g, N_SEG * stride, LANES), F32)] * 2,
        compiler_params=_cparams(dimension_semantics=("arbitrary", "arbitrary")),
    )(u, u, u, p["conv_w"], p["conv_b"], p["wa"], p["wx"], p["ba"], p["bx"], p["lam"], h0)


def _lru_bwd(u, dh, h, h0, lam_in, p, d, name):
    s = u.shape[0]
    tm = min(TM_LRU, s)
    cb = CB_LRU
    n_g = cb // LANES
    nt = s // tm
    seg = tm // N_SEG
    stride = seg + SUBLANES
    pos = (lambda i: nt - 1 - i) if d == 0 else (lambda i: i)

    def body(u_ref, up_ref, un_ref, dh_ref, h_ref, hh_ref, cw_ref, cbias_ref, wa_ref, wx_ref, ba_ref, bx_ref,
             lam_ref, h0_ref, lin_ref, duv_ref, gwa_ref, gwx_ref, gv_ref, lc_ref, a_s, b_s, lp_s,
             r_s, i_s, q_s, a_keep):
        i = pl.program_id(1)
        tp = pos(i)

        @pl.when(i == 0)
        def _():
            lc_ref[...] = lin_ref[...]
            gwa_ref[...] = jnp.zeros_like(gwa_ref)
            gwx_ref[...] = jnp.zeros_like(gwx_ref)
            gv_ref[...] = jnp.zeros_like(gv_ref)

        u_t = u_ref[...]
        um2, um1, up1 = _conv_taps(u_t, up_ref[...], un_ref[...], tp == 0, tp == nt - 1)
        cw = cw_ref[...]
        uv = um2 * cw[0:1] + um1 * cw[1:2] + u_t * cw[2:3] + up1 * cw[3:4] + cbias_ref[...]
        lam = lam_ref[...]
        cl = LRU_C * _log_sigmoid(lam)
        ba, bx = ba_ref[...], bx_ref[...]
        dh_t = dh_ref[...]
        carry_in = lc_ref[...]
        for g in range(n_g):
            sl = slice(g * LANES, (g + 1) * LANES)
            _, r, ii, a, sq = _lru_gates(uv, wa_ref, wx_ref, ba, bx, cl, g)
            r_s[:, sl], i_s[:, sl], q_s[:, sl], a_keep[:, sl] = r, ii, sq, a
            b = a * dh_t[:, sl]
            for j in range(N_SEG):
                a_s[g, pl.ds(j * stride, seg), :] = a[j * seg:(j + 1) * seg]
                b_s[g, pl.ds(j * stride, seg), :] = b[j * seg:(j + 1) * seg]

        def write_out(j, g, v):
            lp_s[pl.ds(j * seg, seg), pl.ds(g * LANES, LANES)] = v

        _scan_tile(a_s, b_s, lc_ref, write_out, seg, reverse=(d == 0))

        h_t = h_ref[...]
        hh = hh_ref[...]
        if d == 0:
            edge = jnp.where(tp == 0, h0_ref[...], hh[7:8])
            h_prev = _shift_down(h_t, edge)
            lam_t = dh_t + _shift_up(lp_s[...], carry_in)
        else:
            edge = jnp.where(tp == nt - 1, h0_ref[...], hh[0:1])
            h_prev = _shift_up(h_t, edge)
            lam_t = dh_t + _shift_down(lp_s[...], carry_in)

        dsig = LRU_C * _sigmoid(-lam)
        for g in range(n_g):
            sl = slice(g * LANES, (g + 1) * LANES)
            uvg, r, ii, a, sq = uv[:, sl], r_s[:, sl], i_s[:, sl], a_keep[:, sl], q_s[:, sl]
            lt = lam_t[:, sl]
            iu = ii * uvg
            dla = lt * h_prev[:, sl] * a - (lt * iu) * (a * a / sq)
            dzr = (dla * cl[:, sl]) * r * (1.0 - r)
            dzi = (lt * sq * uvg) * ii * (1.0 - ii)
            duv_ref[:, sl] = lt * sq * ii + _dot_nt(dzr, wa_ref[g]) + _dot_nt(dzi, wx_ref[g])
            gwa_ref[g] += _dot_tn(uvg, dzr)
            gwx_ref[g] += _dot_tn(uvg, dzi)
            gv_ref[0:1, sl] += _rowsum(dzr)
            gv_ref[1:2, sl] += _rowsum(dzi)
            gv_ref[2:3, sl] += _rowsum(dla * r) * dsig[:, sl]

    tile, prev, nxt = _lru_specs(s, tm, cb, pos, nt)
    vec, dvec, wmat = _lru_param_specs(cb, d)
    hh_spec = prev if d == 0 else nxt
    gw_spec = pl.BlockSpec((n_g, LRU_BLOCK, LRU_BLOCK), lambda c, i: (c, 0, 0))
    n_blk = D_INNER // LRU_BLOCK
    return pl.pallas_call(
        body, name=name,
        out_shape=[jax.ShapeDtypeStruct((s, D_INNER), F32),
                   jax.ShapeDtypeStruct((n_blk, LRU_BLOCK, LRU_BLOCK), F32),
                   jax.ShapeDtypeStruct((n_blk, LRU_BLOCK, LRU_BLOCK), F32),
                   jax.ShapeDtypeStruct((SUBLANES, D_INNER), F32),
                   jax.ShapeDtypeStruct((1, D_INNER), F32)],
        grid=(D_INNER // cb, nt),
        in_specs=[tile, prev, nxt, tile, tile, hh_spec, pl.BlockSpec((4, cb), lambda c, i: (0, c)), vec,
                  wmat, wmat, dvec, dvec, dvec, vec, vec],
        out_specs=[tile, gw_spec, gw_spec, pl.BlockSpec((SUBLANES, cb), lambda c, i: (0, c)), vec],
        scratch_shapes=[pltpu.VMEM((n_g, N_SEG * stride, LANES), F32)] * 2 + [pltpu.VMEM((tm, cb), F32)] * 5,
        compiler_params=_cparams(dimension_semantics=("arbitrary", "arbitrary")),
    )(u, u, u, dh, h, h, p["conv_w"], p["conv_b"], p["wa"], p["wx"], p["ba"], p["bx"], p["lam"], h0, lam_in)


def _out0(hf, hb, g, xt, gt, wo, lg, lb, name):
    t = xt.shape[0]
    tm = min(TM_MM, t)

    def body(hf_ref, hb_ref, g_ref, x_ref, gt_ref, w_ref, lg_ref, lb_ref, x1_ref, br_ref):
        gg = g_ref[...]
        p = (hf_ref[...] + hb_ref[...]) * (gg * _sigmoid(gg))
        br = _dot(p, w_ref[...])
        z = ALPHA * x_ref[...] + gt_ref[...] * br
        xhat, _ = _layer_norm_stats(z)
        x1_ref[...] = xhat * lg_ref[...] + lb_ref[...]
        br_ref[...] = br

    wide = pl.BlockSpec((tm, D_INNER), lambda i: (i, 0))
    nar = pl.BlockSpec((tm, D_MODEL), lambda i: (i, 0))
    row = pl.BlockSpec((1, D_MODEL), lambda i: (0, 0))
    return pl.pallas_call(
        body, name=name, out_shape=[jax.ShapeDtypeStruct((t, D_MODEL), F32)] * 2, grid=(t // tm,),
        in_specs=[wide, wide, wide, nar, row,
                  pl.BlockSpec((D_INNER, D_MODEL), lambda i: (0, 0), pipeline_mode=pl.Buffered(1)), row, row],
        out_specs=[nar, nar],
        compiler_params=_cparams(dimension_semantics=("arbitrary",)),
    )(hf, hb, g, xt, gt, wo, lg, lb)


def _window(n, w):
    t = np.arange(n)
    return np.clip(t - w // 2, 0, n), np.clip(t + w // 2, 0, n)


def _pool_tables(n_rows, transpose):
    boxes, inv_c, inv_r = [], [], []
    for w in POOL_WINDOWS:
        lo, hi = _window(GRID_W, w)
        m = np.zeros((GRID_W, GRID_W), np.float32)
        for r in range(GRID_W):
            m[r, lo[r]:hi[r]] = 1.0
        m = np.kron(np.eye(POOL_TOK // GRID_W, dtype=np.float32), m)
        boxes.append(m.T if transpose else m)
        inv_c.append(np.broadcast_to((1.0 / (hi - lo).astype(np.float32))[:, None], (GRID_W, LANES)))
        lo_r, hi_r = _window(n_rows, w)
        inv_r.append(1.0 / (hi_r - lo_r).astype(np.float32))
    return (jnp.asarray(np.stack(boxes), MXU_DTYPE), jnp.asarray(np.stack(inv_c), F32),
            jnp.asarray(np.stack(inv_r), F32))


def _pool_mix(xin, transpose, out_dtype, name):
    s = xin.shape[0]
    n_rows = s // GRID_W
    pad_t = SUBLANES * GRID_W
    rows_per_blk = POOL_TOK // GRID_W
    n_slab = D_INNER // LANES
    slabs_per_group = POOL_GROUP // LANES
    n_win = len(POOL_WINDOWS)
    boxes, inv_c, inv_r = _pool_tables(n_rows, transpose)

    def body(invr_ref, box_ref, invc_ref, x_ref, o_ref, pad_s):
        k = pl.program_id(0) // slabs_per_group
        pad_s[pl.ds(0, pad_t), :] = jnp.zeros((pad_t, LANES), F32)
        pad_s[pl.ds(pad_t + s, pad_t), :] = jnp.zeros((pad_t, LANES), F32)

        for kk, w in enumerate(POOL_WINDOWS):
            half = w // 2
            offsets = list(range(-(half - 1), half + 1)) if transpose else list(range(-half, half))

            @pl.when(k == kk)
            def _():
                inv_col = invc_ref[kk]

                def col_box(b, carry):
                    st = pl.multiple_of(b * POOL_TOK, POOL_TOK)
                    xb = x_ref[pl.ds(st, POOL_TOK), :]
                    if transpose:
                        xb = xb * jnp.concatenate(
                            [inv_col * invr_ref[kk, b * rows_per_blk + q] for q in range(rows_per_blk)], axis=0)
                    hi = xb.astype(MXU_DTYPE)
                    lo = (xb - hi.astype(F32)).astype(MXU_DTYPE)
                    both = jnp.dot(box_ref[kk], jnp.concatenate([hi, lo], axis=1), preferred_element_type=F32)
                    pad_s[pl.ds(pad_t + st, POOL_TOK), :] = both[:, :LANES] + both[:, LANES:]
                    return carry
                lax.fori_loop(0, s // POOL_TOK, col_box, 0)

                def row_box(r, carry):
                    st = pl.multiple_of(r * GRID_W, GRID_W)
                    acc = pad_s[pl.ds(pad_t + st + offsets[0] * GRID_W, GRID_W), :]
                    for o in offsets[1:]:
                        acc = acc + pad_s[pl.ds(pad_t + st + o * GRID_W, GRID_W), :]
                    if not transpose:
                        acc = acc * (inv_col * invr_ref[kk, r])
                    o_ref[pl.ds(st, GRID_W), :] = (acc - x_ref[pl.ds(st, GRID_W), :]).astype(out_dtype)
                    return carry
                lax.fori_loop(0, n_rows, row_box, 0)

    slab = pl.BlockSpec((s, LANES), lambda i: (0, i))
    return pl.pallas_call(
        body, name=name, out_shape=jax.ShapeDtypeStruct((s, D_INNER), out_dtype), grid=(n_slab,),
        in_specs=[pl.BlockSpec(memory_space=pltpu.SMEM),
                  pl.BlockSpec((n_win, POOL_TOK, POOL_TOK), lambda i: (0, 0, 0)),
                  pl.BlockSpec((n_win, GRID_W, LANES), lambda i: (0, 0, 0)), slab],
        out_specs=slab,
        scratch_shapes=[pltpu.VMEM((s + 2 * pad_t, LANES), F32)],
        compiler_params=_cparams(dimension_semantics=("arbitrary",)),
    )(inv_r, boxes, inv_c, xin)


def _out1(dmix, pw, ps, g, x1, gt, wo, lg, lb, tgt, name):
    t = x1.shape[0]
    tm = min(TM_MM, t)
    n_grp = len(POOL_WINDOWS)

    def body(d_ref, pw_ref, ps_ref, g_ref, x1_ref, gt_ref, w_ref, lg_ref, lb_ref, tgt_ref, dz_ref, st_ref):
        @pl.when(pl.program_id(0) == 0)
        def _():
            st_ref[...] = jnp.zeros_like(st_ref)

        br = jnp.zeros((tm, D_MODEL), F32)
        for k in range(n_grp):
            sl = slice(k * POOL_GROUP, (k + 1) * POOL_GROUP)
            y = jnp.dot(d_ref[:, sl], pw_ref[k], preferred_element_type=F32) * ps_ref[:, sl]
            gg = g_ref[:, sl]
            br = br + _dot(y * (gg * _sigmoid(gg)), w_ref[sl, :])
        z = ALPHA * x1_ref[...] + gt_ref[...] * br
        xhat, rstd = _layer_norm_stats(z)
        lg_v = lg_ref[...]
        err = xhat * lg_v + lb_ref[...] - tgt_ref[...]
        dy = err * (1.0 / D_MODEL)
        dz = _layer_norm_bwd(dy, xhat, rstd, lg_v)
        dz_ref[...] = dz
        st_ref[0:1, :] += _rowsum(dy * xhat)
        st_ref[1:2, :] += _rowsum(dy)
        st_ref[2:3, :] += _rowsum(dz * br)
        st_ref[3:4, :] += _rowsum(err * err)

    wide = pl.BlockSpec((tm, D_INNER), lambda i: (i, 0))
    nar = pl.BlockSpec((tm, D_MODEL), lambda i: (i, 0))
    row = pl.BlockSpec((1, D_MODEL), lambda i: (0, 0))
    return pl.pallas_call(
        body, name=name,
        out_shape=[jax.ShapeDtypeStruct((t, D_MODEL), F32), jax.ShapeDtypeStruct((SUBLANES, D_MODEL), F32)],
        grid=(t // tm,),
        in_specs=[wide, pl.BlockSpec((n_grp, POOL_GROUP, POOL_GROUP), lambda i: (0, 0, 0)),
                  pl.BlockSpec((1, D_INNER), lambda i: (0, 0)), wide, nar, row,
                  pl.BlockSpec((D_INNER, D_MODEL), lambda i: (0, 0), pipeline_mode=pl.Buffered(1)), row, row, nar],
        out_specs=[nar, pl.BlockSpec((SUBLANES, D_MODEL), lambda i: (0, 0))],
        compiler_params=_cparams(dimension_semantics=("arbitrary",)),
    )(dmix, pw, ps, g, x1, gt, wo, lg, lb, tgt)


def _flush(acc, out_hbm, sem):
    cp = pltpu.make_async_copy(acc, out_hbm, sem)
    cp.start()
    cp.wait()


def _bout1(dz, dmix, g, pw, ps, gt, wo, name):
    t = dz.shape[0]
    tm = min(TM_BWD, t)
    nt = t // tm
    n_grp = len(POOL_WINDOWS)

    def body(dz_ref, d_ref, g_ref, pw_ref, ps_ref, gt_ref, w_ref, dd_ref, dg_ref, gwo_hbm, gpw_hbm, gps_ref,
             gwo_acc, gpw_acc, sems):
        i = pl.program_id(0)

        @pl.when(i == 0)
        def _():
            gwo_acc[...] = jnp.zeros_like(gwo_acc)
            gpw_acc[...] = jnp.zeros_like(gpw_acc)
            gps_ref[...] = jnp.zeros_like(gps_ref)

        db = (gt_ref[...] * dz_ref[...]).astype(MXU_DTYPE)
        for k in range(n_grp):
            sl = slice(k * POOL_GROUP, (k + 1) * POOL_GROUP)
            dk = d_ref[:, sl]
            po = jnp.dot(dk, pw_ref[k], preferred_element_type=F32)
            psk = ps_ref[:, sl]
            y = po * psk
            gg = g_ref[:, sl]
            sg = _sigmoid(gg)
            silu = gg * sg
            gwo_acc[sl, :] += _dot_tn(y * silu, db)
            dp = _dot_nt(db, w_ref[sl, :])
            dy = dp * silu
            dg_ref[:, sl] = dp * y * (sg * (1.0 + gg * (1.0 - sg)))
            gps_ref[0:1, sl] += _rowsum(dy * po)
            dpo = (dy * psk).astype(MXU_DTYPE)
            gpw_acc[k] += _dot_tn(dk, dpo)
            dd_ref[:, sl] = _dot_nt(dpo, pw_ref[k])

        @pl.when(i == nt - 1)
        def _():
            _flush(gwo_acc, gwo_hbm, sems.at[0])
            _flush(gpw_acc, gpw_hbm, sems.at[1])

    wide = pl.BlockSpec((tm, D_INNER), lambda i: (i, 0))
    nar = pl.BlockSpec((tm, D_MODEL), lambda i: (i, 0))
    return pl.pallas_call(
        body, name=name,
        out_shape=[jax.ShapeDtypeStruct((t, D_INNER), F32), jax.ShapeDtypeStruct((t, D_INNER), F32),
                   jax.ShapeDtypeStruct((D_INNER, D_MODEL), F32),
                   jax.ShapeDtypeStruct((n_grp, POOL_GROUP, POOL_GROUP), F32),
                   jax.ShapeDtypeStruct((SUBLANES, D_INNER), F32)],
        grid=(nt,),
        in_specs=[nar, wide, wide, pl.BlockSpec((n_grp, POOL_GROUP, POOL_GROUP), lambda i: (0, 0, 0)),
                  pl.BlockSpec((1, D_INNER), lambda i: (0, 0)), pl.BlockSpec((1, D_MODEL), lambda i: (0, 0)),
                  pl.BlockSpec((D_INNER, D_MODEL), lambda i: (0, 0), pipeline_mode=pl.Buffered(1))],
        out_specs=[wide, wide, ANY, ANY, pl.BlockSpec((SUBLANES, D_INNER), lambda i: (0, 0))],
        scratch_shapes=[pltpu.VMEM((D_INNER, D_MODEL), F32), pltpu.VMEM((n_grp, POOL_GROUP, POOL_GROUP), F32),
                        pltpu.SemaphoreType.DMA((2,))],
        compiler_params=_cparams(dimension_semantics=("arbitrary",)),
    )(dz, dmix, g, pw, ps, gt, wo)


def _bout0(dz, hf, hb, g, gt, wo, name):
    t = dz.shape[0]
    tm = min(TM_BWD, t)
    nt = t // tm

    def body(dz_ref, hf_ref, hb_ref, g_ref, gt_ref, w_ref, dy_ref, dg_ref, gwo_hbm, gwo_acc, sem):
        i = pl.program_id(0)

        @pl.when(i == 0)
        def _():
            gwo_acc[...] = jnp.zeros_like(gwo_acc)

        db = (gt_ref[...] * dz_ref[...]).astype(MXU_DTYPE)
        y = hf_ref[...] + hb_ref[...]
        gg = g_ref[...]
        sg = _sigmoid(gg)
        silu = gg * sg
        gwo_acc[...] += _dot_tn(y * silu, db)
        dp = _dot_nt(db, w_ref[...])
        dy_ref[...] = dp * silu
        dg_ref[...] = dp * y * (sg * (1.0 + gg * (1.0 - sg)))

        @pl.when(i == nt - 1)
        def _():
            _flush(gwo_acc, gwo_hbm, sem)

    wide = pl.BlockSpec((tm, D_INNER), lambda i: (i, 0))
    nar = pl.BlockSpec((tm, D_MODEL), lambda i: (i, 0))
    return pl.pallas_call(
        body, name=name,
        out_shape=[jax.ShapeDtypeStruct((t, D_INNER), F32), jax.ShapeDtypeStruct((t, D_INNER), F32),
                   jax.ShapeDtypeStruct((D_INNER, D_MODEL), F32)],
        grid=(nt,),
        in_specs=[nar, wide, wide, wide, pl.BlockSpec((1, D_MODEL), lambda i: (0, 0)),
                  pl.BlockSpec((D_INNER, D_MODEL), lambda i: (0, 0), pipeline_mode=pl.Buffered(1))],
        out_specs=[wide, wide, ANY],
        scratch_shapes=[pltpu.VMEM((D_INNER, D_MODEL), F32), pltpu.SemaphoreType.DMA(())],
        compiler_params=_cparams(dimension_semantics=("arbitrary",)),
    )(dz, hf, hb, g, gt, wo)


def _in_proj_bwd_core(h, parts, w_ref, gw_acc):
    dh = None
    k = 0
    for part in parts:
        for kk in range(part.shape[1] // WBLK):
            dk = part[:, kk * WBLK:(kk + 1) * WBLK].astype(MXU_DTYPE)
            gw_acc[k] += _dot_tn(h, dk)
            contrib = _dot_nt(dk, w_ref[k])
            dh = contrib if dh is None else dh + contrib
            k += 1
    return dh


def _bin1(du, dg, x1, dz1, sc, sh, wg, xt, br0, gt0, lg0, name):
    t = x1.shape[0]
    tm = min(TM_BWD, t)
    nt = t // tm

    def body(du_ref, dg_ref, x1_ref, dz1_ref, sc_ref, sh_ref, w_ref, x_ref, br_ref, gt_ref, lg_ref,
             dz0_ref, gw_hbm, st_ref, gw_acc, sem):
        i = pl.program_id(0)

        @pl.when(i == 0)
        def _():
            gw_acc[...] = jnp.zeros_like(gw_acc)
            st_ref[...] = jnp.zeros_like(st_ref)

        x1v = x1_ref[...]
        scale = 1.0 + sc_ref[...]
        h = (x1v * scale + sh_ref[...]).astype(MXU_DTYPE)
        dh = _in_proj_bwd_core(h, [du_ref[...], dg_ref[...]], w_ref, gw_acc)
        dx1 = ALPHA * dz1_ref[...] + dh * scale
        st_ref[0:1, :] += _rowsum(dh * x1v)
        st_ref[1:2, :] += _rowsum(dh)
        br = br_ref[...]
        z0 = ALPHA * x_ref[...] + gt_ref[...] * br
        xhat, rstd = _layer_norm_stats(z0)
        dz0 = _layer_norm_bwd(dx1, xhat, rstd, lg_ref[...])
        dz0_ref[...] = dz0
        st_ref[2:3, :] += _rowsum(dx1 * xhat)
        st_ref[3:4, :] += _rowsum(dx1)
        st_ref[4:5, :] += _rowsum(dz0 * br)

        @pl.when(i == nt - 1)
        def _():
            _flush(gw_acc, gw_hbm, sem)

    wide = pl.BlockSpec((tm, D_INNER), lambda i: (i, 0))
    nar = pl.BlockSpec((tm, D_MODEL), lambda i: (i, 0))
    row = pl.BlockSpec((1, D_MODEL), lambda i: (0, 0))
    return pl.pallas_call(
        body, name=name,
        out_shape=[jax.ShapeDtypeStruct((t, D_MODEL), F32), jax.ShapeDtypeStruct((N_WBLK, D_MODEL, WBLK), F32),
                   jax.ShapeDtypeStruct((SUBLANES, D_MODEL), F32)],
        grid=(nt,),
        in_specs=[wide, wide, nar, nar, row, row,
                  pl.BlockSpec((N_WBLK, D_MODEL, WBLK), lambda i: (0, 0, 0), pipeline_mode=pl.Buffered(1)),
                  nar, nar, row, row],
        out_specs=[nar, ANY, pl.BlockSpec((SUBLANES, D_MODEL), lambda i: (0, 0))],
        scratch_shapes=[pltpu.VMEM((N_WBLK, D_MODEL, WBLK), F32), pltpu.SemaphoreType.DMA(())],
        compiler_params=_cparams(dimension_semantics=("arbitrary",)),
    )(du, dg, x1, dz1, sc, sh, wg, xt, br0, gt0, lg0)


def _bin0(duvf, duvb, u, dg, xt, dz0, sc, sh, wg, conv_w, name):
    t = xt.shape[0]
    tm = min(TM_BWD, t)
    nt = t // tm
    latent = dg is not None
    n_blk = N_WBLK if latent else N_WBLK // 2
    per = tm // SUBLANES
    n_rows8 = t // SUBLANES

    def body(*refs):
        if latent:
            (df_ref, dfp_ref, dfn_ref, db_ref, dbp_ref, dbn_ref, u_ref, up_ref, un_ref, dg_ref, x_ref, dz_ref,
             sc_ref, sh_ref, w_ref, cw_ref, gx_ref, gw_hbm, st_ref, cst_ref, gw_acc, sem) = refs
        else:
            (df_ref, dfp_ref, dfn_ref, db_ref, dbp_ref, dbn_ref, u_ref, up_ref, un_ref, x_ref,
             sc_ref, sh_ref, w_ref, cw_ref, gw_hbm, st_ref, cst_ref, gw_acc, sem) = refs
        i = pl.program_id(0)

        @pl.when(i == 0)
        def _():
            gw_acc[...] = jnp.zeros_like(gw_acc)
            st_ref[...] = jnp.zeros_like(st_ref)
            cst_ref[...] = jnp.zeros_like(cst_ref)

        first, last = i == 0, i == nt - 1
        pz = jnp.where(first, 0.0, 1.0)
        nz = jnp.where(last, 0.0, 1.0)
        xv = x_ref[...]
        scale = 1.0 + sc_ref[...]
        h = (xv * scale + sh_ref[...]).astype(MXU_DTYPE)
        cw = cw_ref[...]
        dh = None
        for k in range(N_WBLK // 2):
            sl = slice(k * WBLK, (k + 1) * WBLK)
            dout = df_ref[:, sl] + db_ref[:, sl]
            dm1, dp1, dp2 = _shifted(dout, (dfp_ref[:, sl] + dbp_ref[:, sl]) * pz,
                                     (dfn_ref[:, sl] + dbn_ref[:, sl]) * nz, [-1, 1, 2])
            du = dp2 * cw[0:1, sl] + dp1 * cw[1:2, sl] + dout * cw[2:3, sl] + dm1 * cw[3:4, sl]
            u_t = u_ref[:, sl]
            um2, um1, up1 = _conv_taps(u_t, up_ref[:, sl], un_ref[:, sl], first, last)
            cst_ref[0:1, sl] += _rowsum(dout * um2)
            cst_ref[1:2, sl] += _rowsum(dout * um1)
            cst_ref[2:3, sl] += _rowsum(dout * u_t)
            cst_ref[3:4, sl] += _rowsum(dout * up1)
            cst_ref[4:5, sl] += _rowsum(dout)
            dk = du.astype(MXU_DTYPE)
            gw_acc[k] += _dot_tn(h, dk)
            contrib = _dot_nt(dk, w_ref[k])
            dh = contrib if dh is None else dh + contrib
        if latent:
            for k in range(N_WBLK // 2, N_WBLK):
                kk = k - N_WBLK // 2
                dk = dg_ref[:, kk * WBLK:(kk + 1) * WBLK].astype(MXU_DTYPE)
                gw_acc[k] += _dot_tn(h, dk)
                dh = dh + _dot_nt(dk, w_ref[k])
        st_ref[0:1, :] += _rowsum(dh * xv)
        st_ref[1:2, :] += _rowsum(dh)
        if latent:
            gx_ref[...] = ALPHA * dz_ref[...] + dh * scale

        @pl.when(i == nt - 1)
        def _():
            _flush(gw_acc, gw_hbm, sem)

    wide = pl.BlockSpec((tm, D_INNER), lambda i: (i, 0))
    prev = pl.BlockSpec((SUBLANES, D_INNER), lambda i: (jnp.maximum(i * per - 1, 0), 0))
    nxt = pl.BlockSpec((SUBLANES, D_INNER), lambda i: (jnp.minimum((i + 1) * per, n_rows8 - 1), 0))
    nar = pl.BlockSpec((tm, D_MODEL), lambda i: (i, 0))
    row = pl.BlockSpec((1, D_MODEL), lambda i: (0, 0))
    wspec = pl.BlockSpec((n_blk, D_MODEL, WBLK), lambda i: (0, 0, 0), pipeline_mode=pl.Buffered(1))
    cwspec = pl.BlockSpec((4, D_INNER), lambda i: (0, 0))
    stat = pl.BlockSpec((SUBLANES, D_MODEL), lambda i: (0, 0))
    cstat = pl.BlockSpec((SUBLANES, D_INNER), lambda i: (0, 0))
    halo3 = [wide, prev, nxt]
    if latent:
        in_specs = halo3 * 3 + [wide, nar, nar, row, row, wspec, cwspec]
        args = (duvf, duvf, duvf, duvb, duvb, duvb, u, u, u, dg, xt, dz0, sc, sh, wg, conv_w)
        out_shape = [jax.ShapeDtypeStruct((t, D_MODEL), F32)]
        out_specs = [nar]
    else:
        in_specs = halo3 * 3 + [nar, row, row, wspec, cwspec]
        args = (duvf, duvf, duvf, duvb, duvb, duvb, u, u, u, xt, sc, sh, wg, conv_w)
        out_shape, out_specs = [], []
    out_shape += [jax.ShapeDtypeStruct((n_blk, D_MODEL, WBLK), F32), jax.ShapeDtypeStruct((SUBLANES, D_MODEL), F32),
                  jax.ShapeDtypeStruct((SUBLANES, D_INNER), F32)]
    out_specs += [ANY, stat, cstat]
    return pl.pallas_call(
        body, name=name, out_shape=out_shape, grid=(nt,), in_specs=in_specs, out_specs=out_specs,
        scratch_shapes=[pltpu.VMEM((n_blk, D_MODEL, WBLK), F32), pltpu.SemaphoreType.DMA(())],
        compiler_params=_cparams(dimension_semantics=("arbitrary",)),
    )(*args)


def _blocks_by_device(a, axis):
    shape = a.shape
    a = a.reshape(shape[:axis] + (N_DEV, shape[axis] // N_DEV) + shape[axis + 1:])
    return jnp.moveaxis(a, axis, 0)


def kernel(x, c, ctx, c_ctx, w_mod, b_mod, w_in, w_out, ln_g, ln_b, conv_w, conv_b, lru_wa, lru_ba, lru_wx, lru_bx, lru_lam, pool_w, pool_scale, loss_target, m_c_ctx, m_w_mod, m_b_mod, m_w_in, m_w_out, m_ln_g, m_ln_b, m_conv_w, m_conv_b, m_lru_wa, m_lru_ba, m_lru_wx, m_lru_bx, m_lru_lam, m_pool_w, m_pool_scale, v_c_ctx, v_w_mod, v_b_mod, v_w_in, v_w_out, v_ln_g, v_ln_b, v_conv_w, v_conv_b, v_lru_wa, v_lru_ba, v_lru_wx, v_lru_bx, v_lru_lam, v_pool_w, v_pool_scale):
    xi, yi, ci = _my_pos()
    dev = 4 * xi + 2 * yi + ci
    xt, ctxt, tgt = x[0], ctx[0], loss_target[0]
    n_mod = w_mod.shape[2]

    small_shapes = [(D_MODEL,), conv_w.shape[1:], lru_ba.shape[1:], lru_bx.shape[1:], lru_lam.shape[1:],
                    pool_scale.shape[1:]]
    small = _to_rows([c[0], conv_w[0], lru_ba[0], lru_bx[0], lru_lam[0], pool_scale[0]], SUBLANES)
    small_all, = _all_gather([small], "gather_small")
    pieces = [_split_rows(small_all[k], small_shapes) for k in range(N_DEV)]
    c_all = jnp.stack([p[0] for p in pieces])
    conv_w_f = jnp.concatenate([p[1] for p in pieces], axis=-1)
    lru_ba_f = jnp.concatenate([p[2] for p in pieces], axis=-1)[:, None, :]
    lru_bx_f = jnp.concatenate([p[3] for p in pieces], axis=-1)[:, None, :]
    lru_lam_f = jnp.concatenate([p[4] for p in pieces], axis=-1)[:, None, :]
    pool_scale_f = jnp.concatenate([p[5] for p in pieces], axis=-1)[None, :]

    cond = jnp.concatenate([c_all, jnp.broadcast_to(c_ctx[None, :], (N_DEV, D_MODEL))], axis=0)
    b_my = lax.dynamic_slice(b_mod, (0, dev * n_mod), (2, n_mod))[:, None, :]
    mod_part = _mod_fwd(cond, w_mod, b_my, "mod_fwd")
    mod_all, = _all_gather([mod_part], "gather_mod")
    mod = jnp.transpose(mod_all, (1, 2, 0, 3)).reshape(2, 16, 3 * D_MODEL)
    mod_me = lax.dynamic_slice(mod, (0, dev, 0), (2, 1, 3 * D_MODEL))
    sh = [mod_me[i, :, 0:D_MODEL] for i in range(2)]
    sc = [mod_me[i, :, D_MODEL:2 * D_MODEL] for i in range(2)]
    gt = [mod_me[i, :, 2 * D_MODEL:] for i in range(2)]
    shc, scc = mod[0, 8:9, 0:D_MODEL], mod[0, 8:9, D_MODEL:2 * D_MODEL]

    wi0, wi1, wo0, wo1, pool_w_g = _all_gather(
        [w_in[0].astype(MXU_DTYPE), w_in[1].astype(MXU_DTYPE), w_out[0].astype(MXU_DTYPE),
         w_out[1].astype(MXU_DTYPE), pool_w[0].astype(MXU_DTYPE)], "gather_weights")
    w_in_l = [wi0, wi1]
    w_out_l = [wo0.reshape(D_INNER, D_MODEL), wo1.reshape(D_INNER, D_MODEL)]
    pool_w_f = jnp.transpose(pool_w_g, (1, 0, 2, 3)).reshape(len(POOL_WINDOWS), POOL_GROUP, POOL_GROUP)
    lg = [ln_g[i][None, :] for i in range(2)]
    lb = [ln_b[i][None, :] for i in range(2)]
    lru_p = dict(conv_w=conv_w_f, conv_b=conv_b, wa=lru_wa[0].astype(MXU_DTYPE), wx=lru_wx[0].astype(MXU_DTYPE),
                 ba=lru_ba_f, bx=lru_bx_f, lam=lru_lam_f)
    zero_state = jnp.zeros((1, D_INNER), F32)

    u0, g0 = _in_proj(xt, sc[0], sh[0], w_in_l[0], "in_proj0")
    uc, _ = _in_proj(ctxt, scc, shc, w_in_l[0], "in_proj0_ctx")
    hcf, cf = _lru_fwd(uc, zero_state, lru_p, 0, "lru_fwd_ctx_f")
    hcb, cbk = _lru_fwd(uc, zero_state, lru_p, 1, "lru_fwd_ctx_b")
    hf, _ = _lru_fwd(u0, cf, lru_p, 0, "lru_fwd_f")
    hb, _ = _lru_fwd(u0, cbk, lru_p, 1, "lru_fwd_b")
    x1, br0 = _out0(hf, hb, g0, xt, gt[0], w_out_l[0], lg[0], lb[0], "out0")
    u1, g1 = _in_proj(x1, sc[1], sh[1], w_in_l[1], "in_proj1")
    dmix = _pool_mix(u1, False, MXU_DTYPE, "pool_fwd")
    dz1, st1 = _out1(dmix, pool_w_f, pool_scale_f, g1, x1, gt[1], w_out_l[1], lg[1], lb[1], tgt, "out1")
    loss = lax.psum((0.5 / D_MODEL) * jnp.sum(st1[3]), ("x", "y", "c"))

    dd, dg1, gwo1, gpw, gps = _bout1(dz1, dmix, g1, pool_w_f, pool_scale_f, gt[1], w_out_l[1], "bwd_out1")
    du1 = _pool_mix(dd, True, F32, "pool_bwd")
    dz0, gwi1, stb1 = _bin1(du1, dg1, x1, dz1, sc[1], sh[1], w_in_l[1], xt, br0, gt[0], lg[0], "bwd_in1")
    dy0, dg0, gwo0 = _bout0(dz0, hf, hb, g0, gt[0], w_out_l[0], "bwd_out0")
    duvf, gwa_f, gwx_f, gv_f, dh0f = _lru_bwd(u0, dy0, hf, cf, zero_state, lru_p, 0, "lru_bwd_f")
    duvb, gwa_b, gwx_b, gv_b, dh0b = _lru_bwd(u0, dy0, hb, cbk, zero_state, lru_p, 1, "lru_bwd_b")
    zero_dh = jnp.zeros_like(uc)
    ducf, gwa_cf, gwx_cf, gv_cf, _ = _lru_bwd(uc, zero_dh, hcf, zero_state, dh0f, lru_p, 0, "lru_bwd_ctx_f")
    ducb, gwa_cb, gwx_cb, gv_cb, _ = _lru_bwd(uc, zero_dh, hcb, zero_state, dh0b, lru_p, 1, "lru_bwd_ctx_b")
    gx, gwi0, stb0, cst0 = _bin0(duvf, duvb, u0, dg0, xt, dz0, sc[0], sh[0], w_in_l[0], conv_w_f, "bwd_in0")
    gwic, stc, cstc = _bin0(ducf, ducb, uc, None, ctxt, None, scc, shc, w_in_l[0][:N_WBLK // 2], conv_w_f,
                            "bwd_in0_ctx")
    gwi0 = gwi0.at[:N_WBLK // 2].add(gwic)

    zero_row = jnp.zeros((1, D_MODEL), F32)
    dm_me = jnp.stack([
        jnp.concatenate([jnp.concatenate([stb0[1:2], stb0[0:1], stb1[4:5]], axis=1),
                         jnp.concatenate([stc[1:2], stc[0:1], zero_row], axis=1)], axis=0),
        jnp.concatenate([jnp.concatenate([stb1[1:2], stb1[0:1], st1[2:3]], axis=1),
                         jnp.zeros((1, 3 * D_MODEL), F32)], axis=0)])
    dm_g, = _all_gather([dm_me], "gather_dmod")
    dm_all = jnp.concatenate([jnp.transpose(dm_g[:, :, 0], (1, 0, 2)), jnp.transpose(dm_g[:, :, 1], (1, 0, 2))],
                             axis=1)
    dm_my = lax.dynamic_slice(dm_all, (0, 0, dev * n_mod), (2, 16, n_mod))
    g_w_mod, g_b_mod, gcc_part = _mod_bwd(cond, dm_all, dm_my, w_mod, "mod_bwd")
    g_b_mod = g_b_mod.reshape(b_mod.shape)

    gwa = jnp.stack([gwa_f + gwa_cf, gwa_b + gwa_cb])
    gwx = jnp.stack([gwx_f + gwx_cf, gwx_b + gwx_cb])
    gv = jnp.stack([gv_f + gv_cf, gv_b + gv_cb])
    cst = cst0 + cstc
    g_ln_g = jnp.stack([stb1[2], st1[0]])
    g_ln_b = jnp.stack([stb1[3], st1[1]])
    sharded = [
        _blocks_by_device(cst[0:4], 1),
        _blocks_by_device(gv[:, 0], 1), _blocks_by_device(gv[:, 1], 1), _blocks_by_device(gv[:, 2], 1),
        _blocks_by_device(gps[0], 0),
    ]
    replicated = [gwa.reshape(-1), gwx.reshape(-1), g_ln_g.reshape(-1), g_ln_b.reshape(-1), cst[4],
                  gcc_part.reshape(-1)]
    sh_sizes = [int(np.prod(a.shape[1:])) for a in sharded]
    rep_sizes = [a.shape[0] // N_DEV for a in replicated]
    n_flat = sum(sh_sizes) + sum(rep_sizes)
    rows = -(-n_flat // LANES)
    rows = -(-rows // FLAT_ROWS) * FLAT_ROWS
    misc = jnp.concatenate([a.reshape(N_DEV, -1) for a in sharded] +
                           [a.reshape(N_DEV, -1) for a in replicated], axis=1)
    misc = jnp.pad(misc, ((0, 0), (0, rows * LANES - n_flat))).reshape(N_DEV, rows, LANES)
    bufs = [gwi0, gwi1, gwo0.reshape(N_DEV, D_INNER // N_DEV, D_MODEL), gwo1.reshape(N_DEV, D_INNER // N_DEV, D_MODEL),
            _blocks_by_device(gpw, 1).reshape(N_DEV, POOL_GROUP // N_DEV * len(POOL_WINDOWS), POOL_GROUP), misc]
    names_r = ["w_in0", "w_in1", "w_out0", "w_out1", "pool_w", "misc"]
    recvs = _sibling_exchange(bufs, "reduce_sibling")
    core = jnp.reshape(ci, (1,)).astype(jnp.int32)
    pairs = [_pair_sum(b, r, core, "reduce_pair_" + n) for b, r, n in zip(bufs, recvs, names_r)]
    p_wi0, p_wi1, p_wo0, p_wo1, p_pw, p_misc = _chip_exchange(pairs, "reduce_chips")
    g_flat = _sum4(p_misc, "reduce_sum_misc").reshape(-1)

    offs = np.cumsum([0] + sh_sizes + rep_sizes)
    n_sh = len(sh_sizes)
    sh_shapes = [conv_w.shape, lru_ba.shape, lru_bx.shape, lru_lam.shape, pool_scale.shape]
    g_sh = [g_flat[offs[k]:offs[k + 1]].reshape(sh_shapes[k]) for k in range(n_sh)]
    g_conv_w, g_lru_ba, g_lru_bx, g_lru_lam, g_pool_scale = g_sh
    rep_block = _to_rows([g_flat[offs[n_sh]:offs[-1]]], SUBLANES)
    rep_all, = _all_gather([rep_block], "gather_replicated")
    rep_flat = rep_all.reshape(N_DEV, -1)
    rep_full, off = [], 0
    for n in rep_sizes:
        rep_full.append(rep_flat[:, off:off + n].reshape(-1))
        off += n
    g_lru_wa = rep_full[0].reshape(lru_wa.shape)
    g_lru_wx = rep_full[1].reshape(lru_wx.shape)
    g_ln_g = rep_full[2].reshape(ln_g.shape)
    g_ln_b = rep_full[3].reshape(ln_b.shape)
    g_conv_b = rep_full[4].reshape(conv_b.shape)
    g_c_ctx = rep_full[5].reshape(c_ctx.shape)

    names = ["c_ctx", "w_mod", "b_mod", "w_in", "w_out", "ln_g", "ln_b", "conv_w", "conv_b", "lru_wa", "lru_ba",
             "lru_wx", "lru_bx", "lru_lam", "pool_w", "pool_scale"]
    weights = dict(c_ctx=c_ctx, w_mod=w_mod, b_mod=b_mod, w_in=w_in, w_out=w_out, ln_g=ln_g, ln_b=ln_b,
                   conv_w=conv_w, conv_b=conv_b, lru_wa=lru_wa, lru_ba=lru_ba, lru_wx=lru_wx, lru_bx=lru_bx,
                   lru_lam=lru_lam, pool_w=pool_w, pool_scale=pool_scale)
    mom_m = dict(c_ctx=m_c_ctx, w_mod=m_w_mod, b_mod=m_b_mod, w_in=m_w_in, w_out=m_w_out, ln_g=m_ln_g, ln_b=m_ln_b,
                 conv_w=m_conv_w, conv_b=m_conv_b, lru_wa=m_lru_wa, lru_ba=m_lru_ba, lru_wx=m_lru_wx,
                 lru_bx=m_lru_bx, lru_lam=m_lru_lam, pool_w=m_pool_w, pool_scale=m_pool_scale)
    mom_v = dict(c_ctx=v_c_ctx, w_mod=v_w_mod, b_mod=v_b_mod, w_in=v_w_in, w_out=v_w_out, ln_g=v_ln_g, ln_b=v_ln_b,
                 conv_w=v_conv_w, conv_b=v_conv_b, lru_wa=v_lru_wa, lru_ba=v_lru_ba, lru_wx=v_lru_wx,
                 lru_bx=v_lru_bx, lru_lam=v_lru_lam, pool_w=v_pool_w, pool_scale=v_pool_scale)
    grads = dict(c_ctx=g_c_ctx, w_mod=g_w_mod, b_mod=g_b_mod, ln_g=g_ln_g, ln_b=g_ln_b,
                 conv_w=g_conv_w, conv_b=g_conv_b, lru_wa=g_lru_wa, lru_ba=g_lru_ba, lru_wx=g_lru_wx,
                 lru_bx=g_lru_bx, lru_lam=g_lru_lam)
    grads["pool_scale"] = g_pool_scale
    delta, new_m, new_v = {}, {}, {}

    def update_parts(n, parts, view):
        res = _adamw_parts(weights[n].reshape(view), parts, mom_m[n].reshape(view), mom_v[n].reshape(view),
                           "adamw_" + n)
        grads[n], delta[n], new_m[n], new_v[n] = [r.reshape(weights[n].shape) for r in res]

    update_parts("w_in", [p_wi0, p_wi1], w_in.shape)
    update_parts("w_out", [p_wo0, p_wo1], w_out.shape)
    update_parts("pool_w", [p_pw], (1,) + p_pw.shape[1:])
    view = (w_mod.shape[0] * w_mod.shape[1], w_mod.shape[2])
    res = _adamw(w_mod.reshape(view), g_w_mod.reshape(view), m_w_mod.reshape(view), v_w_mod.reshape(view),
                 "adamw_w_mod")
    delta["w_mod"], new_m["w_mod"], new_v["w_mod"] = [r.reshape(w_mod.shape) for r in res]

    small = [n for n in names if n not in delta]
    shapes = [weights[n].shape for n in small]
    flat = lambda d: _to_rows([d[n] for n in small], FLAT_ROWS)
    res = _adamw(flat(weights), flat(grads), flat(mom_m), flat(mom_v), "adamw_small")
    for d, r in zip((delta, new_m, new_v), res):
        d.update(zip(small, _split_rows(r, shapes)))

    return (loss, gx[None], *[grads[n] for n in names], *[delta[n] for n in names],
            *[new_m[n] for n in names], *[new_v[n] for n in names])
```

```python
import functools

import numpy as np
import jax
import jax.numpy as jnp
from jax import lax
from jax.experimental import pallas as pl
from jax.experimental.pallas import tpu as pltpu

F32 = jnp.float32
BF16 = jnp.bfloat16
MXU_DTYPE = BF16

D_MODEL = 1024
D_INNER = 2048
LRU_BLOCK = 128
GRID_W = 64
POOL_WINDOWS = (2, 4, 8, 16)
POOL_GROUP = 512
ALPHA = float(4 ** 0.25)
LN_EPS = 1e-5
LRU_C = 8.0
N_DEV = 8
N_WBLK = 8
WBLK = 512

ADAM_LR = 0.001
ADAM_B1 = 0.9
ADAM_B2 = 0.999
ADAM_EPS = 1e-08
ADAM_WD = 0.01
ADAM_STEP = 10

LANES = 128
SUBLANES = 8
V7X_VMEM_BYTES = 64 * 1024 * 1024
VMEM_LIMIT = V7X_VMEM_BYTES - 8 * 1024 * 1024
MESH = pl.DeviceIdType.MESH
ANY = pl.BlockSpec(memory_space=pl.ANY)

TM_MM = 512
TM_BWD = 256
TM_LRU = 512
CB_LRU = 512
N_SEG = 8
SCAN_UNROLL = 4
SQRT_FLOOR = 1e-30
FLAT_ROWS = 16
ELEMENTWISE_TILE_BYTES = 1 << 20
POOL_TOK = 256
WIRE_DTYPE = BF16


def _cparams(**kw):
    return pltpu.CompilerParams(vmem_limit_bytes=VMEM_LIMIT, **kw)


def _my_pos():
    return lax.axis_index("x"), lax.axis_index("y"), lax.axis_index("c")


def _dot(a, b):
    return jnp.dot(a.astype(MXU_DTYPE), b.astype(MXU_DTYPE), preferred_element_type=F32)


def _dot_tn(a, b):
    return lax.dot_general(a.astype(MXU_DTYPE), b.astype(MXU_DTYPE), (((0,), (0,)), ((), ())),
                           preferred_element_type=F32)


def _dot_nt(a, b):
    return lax.dot_general(a.astype(MXU_DTYPE), b.astype(MXU_DTYPE), (((1,), (1,)), ((), ())),
                           preferred_element_type=F32)


def _sigmoid(z):
    return 0.5 * jnp.tanh(0.5 * z) + 0.5


def _one_minus_sq(la, a):
    x = -2.0 * la
    series = x * (1.0 - x * (0.5 - x * (1.0 / 6.0)))
    return jnp.where(x < 0.01, series, 1.0 - a * a)


def _log_sigmoid(x):
    y = jnp.exp(-jnp.abs(x))
    u = 1.0 + y
    l1p = jnp.where(u == 1.0, y, jnp.log(u) * (y / jnp.where(u == 1.0, 1.0, u - 1.0)))
    return jnp.minimum(x, 0.0) - l1p


def _rowsum(v):
    return jnp.sum(v, axis=0, keepdims=True)


def _layer_norm_stats(z):
    mu = jnp.mean(z, axis=-1, keepdims=True)
    zc = z - mu
    var = jnp.mean(zc * zc, axis=-1, keepdims=True)
    rstd = lax.rsqrt(var + LN_EPS)
    return zc * rstd, rstd


def _layer_norm_bwd(dy, xhat, rstd, g):
    dxh = dy * g
    m1 = jnp.mean(dxh, axis=-1, keepdims=True)
    m2 = jnp.mean(dxh * xhat, axis=-1, keepdims=True)
    return rstd * (dxh - m1 - xhat * m2)


def _shifted(v, before8, after8, offsets):
    n = v.shape[0]
    ext = jnp.concatenate([before8, v, after8], axis=0)
    total = n + 2 * SUBLANES
    return [pltpu.roll(ext, (-k) % total, 0)[SUBLANES:SUBLANES + n] for k in offsets]


def _rows8(row):
    return jnp.broadcast_to(row, (SUBLANES, row.shape[1]))


def _shift_down(v, first_row):
    return _shifted(v, _rows8(first_row), _rows8(first_row), [-1])[0]


def _shift_up(v, last_row):
    return _shifted(v, _rows8(last_row), _rows8(last_row), [1])[0]


def _all_gather(blocks, name):
    n = len(blocks)

    def body(*refs):
        x_refs, out_refs = refs[:n], refs[n:2 * n]
        send_sems, recv_sems, local_sems = refs[2 * n:]
        x, y, c = _my_pos()
        me, sibling = (x, y, c), (x, y, 1 - c)
        chips = [(1 - x, y), (x, 1 - y), (1 - x, 1 - y)]

        def slot(a, px, py, pc):
            return out_refs[a].at[4 * px + 2 * py + pc]

        def copy(a, k, block, to, src=None):
            return pltpu.make_async_remote_copy(
                src_ref=slot(a, *block) if src is None else src, dst_ref=slot(a, *block),
                send_sem=send_sems.at[a, k], recv_sem=recv_sems.at[a, k], device_id=to, device_id_type=MESH)

        mine = [pltpu.make_async_copy(x_refs[a], slot(a, *me), local_sems.at[a]) for a in range(n)]
        for cp in mine:
            cp.start()
        first = []
        for a in range(n):
            first.append(copy(a, 0, me, sibling, src=x_refs[a]))
            first += [copy(a, 1 + j, me, (*chip, c), src=x_refs[a]) for j, chip in enumerate(chips)]
        for cp in first:
            cp.start()
        passed = []
        for j, chip in enumerate(chips):
            for a in range(n):
                copy(a, 1 + j, (*chip, c), me).wait_recv()
                fwd = copy(a, 4 + j, (*chip, c), sibling)
                fwd.start()
                passed.append(fwd)
        for a in range(n):
            copy(a, 0, sibling, me).wait_recv()
            for j, chip in enumerate(chips):
                copy(a, 4 + j, (*chip, 1 - c), me).wait_recv()
        for cp in first + passed:
            cp.wait_send()
        for cp in mine:
            cp.wait()

    outs = pl.pallas_call(
        body, name=name,
        out_shape=[jax.ShapeDtypeStruct((N_DEV,) + b.shape, b.dtype) for b in blocks],
        in_specs=[ANY] * n, out_specs=[ANY] * n,
        scratch_shapes=[pltpu.SemaphoreType.DMA((n, 7)), pltpu.SemaphoreType.DMA((n, 7)),
                        pltpu.SemaphoreType.DMA((n,))],
    )(*blocks)
    return list(outs)


def _sibling_exchange(bufs, name):
    n = len(bufs)

    def body(*refs):
        srcs, outs = refs[:n], refs[n:2 * n]
        send_sems, recv_sems = refs[2 * n:]
        x, y, c = _my_pos()
        copies = [pltpu.make_async_remote_copy(
            src_ref=srcs[a].at[2 * j + (1 - c)], dst_ref=outs[a].at[j], send_sem=send_sems.at[a, j],
            recv_sem=recv_sems.at[a, j], device_id=(x, y, 1 - c), device_id_type=MESH)
            for a in range(n) for j in range(4)]
        for cp in copies:
            cp.start()
        for cp in copies:
            cp.wait()

    outs = pl.pallas_call(
        body, name=name, out_shape=[jax.ShapeDtypeStruct((4,) + b.shape[1:], b.dtype) for b in bufs],
        in_specs=[ANY] * n, out_specs=[ANY] * n,
        scratch_shapes=[pltpu.SemaphoreType.DMA((n, 4)), pltpu.SemaphoreType.DMA((n, 4))],
    )(*bufs)
    return list(outs)


def _chip_exchange(parts, name):
    n = len(parts)

    def body(*refs):
        srcs, outs = refs[:n], refs[n:2 * n]
        send_sems, recv_sems, local_sems = refs[2 * n:]
        x, y, c = _my_pos()
        jme = 2 * x + y
        peers = [(1 - x, y), (x, 1 - y), (1 - x, 1 - y)]
        local = [pltpu.make_async_copy(srcs[a].at[jme], outs[a].at[jme], local_sems.at[a]) for a in range(n)]
        for cp in local:
            cp.start()

        def copy(a, k, px, py, dst_slot):
            return pltpu.make_async_remote_copy(
                src_ref=srcs[a].at[2 * px + py], dst_ref=outs[a].at[dst_slot], send_sem=send_sems.at[a, k],
                recv_sem=recv_sems.at[a, k], device_id=(px, py, c), device_id_type=MESH)

        sends = [copy(a, k, px, py, jme) for a in range(n) for k, (px, py) in enumerate(peers)]
        for cp in sends:
            cp.start()
        for a in range(n):
            for k, (px, py) in enumerate(peers):
                copy(a, k, px, py, 2 * px + py).wait_recv()
        for cp in sends:
            cp.wait_send()
        for cp in local:
            cp.wait()

    outs = pl.pallas_call(
        body, name=name, out_shape=[jax.ShapeDtypeStruct(p.shape, p.dtype) for p in parts],
        in_specs=[ANY] * n, out_specs=[ANY] * n,
        scratch_shapes=[pltpu.SemaphoreType.DMA((n, 3)), pltpu.SemaphoreType.DMA((n, 3)),
                        pltpu.SemaphoreType.DMA((n,))],
    )(*parts)
    return list(outs)


def _row_tile(r, l):
    t = min(r, max(16, ELEMENTWISE_TILE_BYTES // (4 * l) // 16 * 16))
    while r % t:
        t -= 16
    return t


def _pair_sum(buf, recv, core, name):
    _, r, l = buf.shape
    tr = _row_tile(r, l)

    def body(core_ref, a_ref, b_ref, o_ref):
        o_ref[...] = (a_ref[...] + b_ref[...]).astype(WIRE_DTYPE)

    return pl.pallas_call(
        body, name=name, out_shape=jax.ShapeDtypeStruct((4, r, l), WIRE_DTYPE),
        grid_spec=pltpu.PrefetchScalarGridSpec(
            num_scalar_prefetch=1, grid=(4, r // tr),
            in_specs=[pl.BlockSpec((None, tr, l), lambda j, i, cr: (2 * j + cr[0], i, 0)),
                      pl.BlockSpec((None, tr, l), lambda j, i, cr: (j, i, 0))],
            out_specs=pl.BlockSpec((None, tr, l), lambda j, i, cr: (j, i, 0))),
        compiler_params=_cparams(dimension_semantics=("arbitrary", "arbitrary")),
    )(core, buf, recv)


def _sum_parts(p_ref):
    return ((p_ref[0].astype(F32) + p_ref[1].astype(F32)) + (p_ref[2].astype(F32) + p_ref[3].astype(F32)))


def _sum4(parts, name):
    _, r, l = parts.shape
    tr = _row_tile(r, l)

    def body(p_ref, o_ref):
        o_ref[...] = _sum_parts(p_ref)

    return pl.pallas_call(
        body, name=name, out_shape=jax.ShapeDtypeStruct((r, l), F32), grid=(r // tr,),
        in_specs=[pl.BlockSpec((4, tr, l), lambda i: (0, i, 0))],
        out_specs=pl.BlockSpec((tr, l), lambda i: (i, 0)),
        compiler_params=_cparams(dimension_semantics=("arbitrary",)),
    )(parts)


def _adamw_update(w, gg, m, v):
    nm = ADAM_B1 * m + (1.0 - ADAM_B1) * gg
    nv = ADAM_B2 * v + (1.0 - ADAM_B2) * (gg * gg)
    m_hat = nm / (1.0 - ADAM_B1 ** ADAM_STEP)
    v_hat = nv / (1.0 - ADAM_B2 ** ADAM_STEP)
    return -ADAM_LR * (m_hat / (jnp.sqrt(v_hat) + ADAM_EPS) + ADAM_WD * w), nm, nv


def _adamw(w, g, m, v, name):
    r, l = w.shape
    tr = _row_tile(r, l)

    def body(w_ref, g_ref, m_ref, v_ref, d_ref, nm_ref, nv_ref):
        d_ref[...], nm_ref[...], nv_ref[...] = _adamw_update(w_ref[...], g_ref[...], m_ref[...], v_ref[...])

    spec = pl.BlockSpec((tr, l), lambda i: (i, 0))
    return pl.pallas_call(
        body, name=name, out_shape=[jax.ShapeDtypeStruct((r, l), F32)] * 3, grid=(r // tr,),
        in_specs=[spec] * 4, out_specs=[spec] * 3,
        compiler_params=_cparams(dimension_semantics=("arbitrary",)),
    )(w, g, m, v)


def _adamw_parts(w, parts, m, v, name):
    nl, r, l = w.shape
    tr = _row_tile(r, l)

    def body(*refs):
        w_ref, p_refs, (m_ref, v_ref, g_ref, d_ref, nm_ref, nv_ref) = refs[0], refs[1:1 + nl], refs[1 + nl:]
        layer = pl.program_id(0)
        gg = _sum_parts(p_refs[0])
        for q in range(1, nl):
            gg = jnp.where(layer == q, _sum_parts(p_refs[q]), gg)
        g_ref[...] = gg
        d_ref[...], nm_ref[...], nv_ref[...] = _adamw_update(w_ref[...], gg, m_ref[...], v_ref[...])

    spec = pl.BlockSpec((None, tr, l), lambda q, i: (q, i, 0))
    pspecs = [pl.BlockSpec((4, tr, l), lambda q, i, k=k: (0, jnp.where(q == k, i, 0), 0)) for k in range(nl)]
    return pl.pallas_call(
        body, name=name, out_shape=[jax.ShapeDtypeStruct((nl, r, l), F32)] * 4, grid=(nl, r // tr),
        in_specs=[spec] + pspecs + [spec, spec], out_specs=[spec] * 4,
        compiler_params=_cparams(dimension_semantics=("arbitrary", "arbitrary")),
    )(w, *parts, m, v)


def _to_rows(pieces, row_multiple):
    flat = jnp.concatenate([p.reshape(-1) for p in pieces])
    rows = -(-flat.shape[0] // LANES)
    rows = -(-rows // row_multiple) * row_multiple
    flat = jnp.pad(flat, (0, rows * LANES - flat.shape[0]))
    return flat.reshape(rows, LANES)


def _split_rows(rows, shapes):
    flat = rows.reshape(-1)
    out, off = [], 0
    for s in shapes:
        n = int(np.prod(s))
        out.append(flat[off:off + n].reshape(s))
        off += n
    return out


def _mod_fwd(cond, w_mod, b_my, name):
    nl, _, ncol = w_mod.shape

    def body(a_ref, w_ref, b_ref, o_ref):
        a = a_ref[...]
        s = a * _sigmoid(a)
        for i in range(nl):
            o_ref[i] = _dot(s, w_ref[i]) + b_ref[i]

    return pl.pallas_call(
        body, name=name, out_shape=jax.ShapeDtypeStruct((nl, 16, ncol), F32),
        compiler_params=_cparams(),
    )(cond, w_mod, b_my)


def _mod_bwd(cond, dm_all, dm_my, w_mod, name):
    nl, _, ncol = w_mod.shape

    def body(a_ref, dma_ref, dmm_ref, w_ref, gw_ref, gb_ref, gc_ref):
        a = a_ref[...]
        sg = _sigmoid(a)
        s = a * sg
        for i in range(nl):
            gw_ref[i] = _dot_tn(s, dmm_ref[i])
            gb_ref[i] = jnp.sum(dma_ref[i], axis=0, keepdims=True)
        back = _dot_nt(dmm_ref[0], w_ref[0])
        dsilu = sg * (1.0 + a * (1.0 - sg))
        gc_ref[...] = jnp.sum(back[8:16] * dsilu[8:16], axis=0, keepdims=True)

    return pl.pallas_call(
        body, name=name,
        out_shape=[jax.ShapeDtypeStruct((nl, D_MODEL, ncol), F32), jax.ShapeDtypeStruct((nl, 1, 3 * D_MODEL), F32),
                   jax.ShapeDtypeStruct((1, D_MODEL), F32)],
        compiler_params=_cparams(),
    )(cond, dm_all, dm_my, w_mod)


def _in_proj(xt, sc, sh, wg, name):
    t = xt.shape[0]
    tm = min(TM_MM, t)

    def body(x_ref, sc_ref, sh_ref, w_ref, u_ref, g_ref):
        h = (x_ref[...] * (1.0 + sc_ref[...]) + sh_ref[...]).astype(MXU_DTYPE)
        for k in range(N_WBLK):
            o = jnp.dot(h, w_ref[k], preferred_element_type=F32)
            if k < N_WBLK // 2:
                u_ref[:, k * WBLK:(k + 1) * WBLK] = o
            else:
                kk = k - N_WBLK // 2
                g_ref[:, kk * WBLK:(kk + 1) * WBLK] = o

    row = pl.BlockSpec((1, D_MODEL), lambda i: (0, 0))
    return pl.pallas_call(
        body, name=name, out_shape=[jax.ShapeDtypeStruct((t, D_INNER), F32)] * 2, grid=(t // tm,),
        in_specs=[pl.BlockSpec((tm, D_MODEL), lambda i: (i, 0)), row, row,
                  pl.BlockSpec((N_WBLK, D_MODEL, WBLK), lambda i: (0, 0, 0), pipeline_mode=pl.Buffered(1))],
        out_specs=[pl.BlockSpec((tm, D_INNER), lambda i: (i, 0))] * 2,
        compiler_params=_cparams(dimension_semantics=("arbitrary",)),
    )(xt, sc, sh, wg)


def _halo_maps(nt, tm, n_rows8, pos):
    per = tm // SUBLANES
    prev = lambda cb, i: (jnp.maximum(pos(i) * per - 1, 0), cb)
    nxt = lambda cb, i: (jnp.minimum((pos(i) + 1) * per, n_rows8 - 1), cb)
    return prev, nxt


def _conv_taps(u, prev8, next8, is_first, is_last):
    pz = jnp.where(is_first, 0.0, 1.0)
    nz = jnp.where(is_last, 0.0, 1.0)
    return _shifted(u, prev8 * pz, next8 * nz, [-2, -1, 1])


def _lru_gates(uv, wa_ref, wx_ref, ba, bx, cl, g):
    sl = slice(g * LANES, (g + 1) * LANES)
    uvg = uv[:, sl]
    r = _sigmoid(_dot(uvg, wa_ref[g]) + ba[:, sl])
    ii = _sigmoid(_dot(uvg, wx_ref[g]) + bx[:, sl])
    la = cl[:, sl] * r
    a = jnp.exp(la)
    q = _one_minus_sq(la, a)
    s = q * lax.rsqrt(jnp.maximum(q, SQRT_FLOOR))
    return uvg, r, ii, a, s


def _scan_tile(a_s, b_s, carry_ref, write_out, seg, reverse):
    n_g = a_s.shape[0]
    stride = a_s.shape[1] // N_SEG

    def step(k, state):
        t = (seg - 1 - k) if reverse else k
        hs, cs = state
        nh, nc = [], []
        for g in range(n_g):
            a = a_s[g, pl.ds(t, N_SEG, stride=stride), :]
            b = b_s[g, pl.ds(t, N_SEG, stride=stride), :]
            h = a * hs[g] + b
            cum = a * cs[g]
            b_s[g, pl.ds(t, N_SEG, stride=stride), :] = h
            a_s[g, pl.ds(t, N_SEG, stride=stride), :] = cum
            nh.append(h)
            nc.append(cum)
        return tuple(nh), tuple(nc)

    zeros = tuple(jnp.zeros((N_SEG, LANES), F32) for _ in range(n_g))
    ones = tuple(jnp.ones((N_SEG, LANES), F32) for _ in range(n_g))
    h_fin, a_fin = lax.fori_loop(0, seg, step, (zeros, ones), unroll=SCAN_UNROLL if seg % SCAN_UNROLL == 0 else 1)

    order = list(range(N_SEG - 1, -1, -1)) if reverse else list(range(N_SEG))
    for g in range(n_g):
        carry = carry_ref[:, g * LANES:(g + 1) * LANES]
        for j in order:
            rows = pl.ds(j * stride, seg)
            write_out(j, g, b_s[g, rows, :] + a_s[g, rows, :] * carry)
            carry = a_fin[g][j:j + 1] * carry + h_fin[g][j:j + 1]
        carry_ref[:, g * LANES:(g + 1) * LANES] = carry


def _lru_specs(s, tm, cb, direction_pos, nt):
    n_rows8 = s // SUBLANES
    prev, nxt = _halo_maps(nt, tm, n_rows8, direction_pos)
    tile = pl.BlockSpec((tm, cb), lambda c, i: (direction_pos(i), c))
    return tile, pl.BlockSpec((SUBLANES, cb), prev), pl.BlockSpec((SUBLANES, cb), nxt)


def _lru_param_specs(cb, d):
    n_g = cb // LANES
    vec = pl.BlockSpec((1, cb), lambda c, i: (0, c))
    dvec = pl.BlockSpec((None, 1, cb), lambda c, i: (d, 0, c))
    wmat = pl.BlockSpec((None, n_g, LRU_BLOCK, LRU_BLOCK), lambda c, i: (d, c, 0, 0))
    return vec, dvec, wmat


def _lru_fwd(u, h0, p, d, name):
    s = u.shape[0]
    tm = min(TM_LRU, s)
    cb = CB_LRU
    n_g = cb // LANES
    nt = s // tm
    seg = tm // N_SEG
    stride = seg + SUBLANES
    pos = (lambda i: i) if d == 0 else (lambda i: nt - 1 - i)

    def body(u_ref, up_ref, un_ref, cw_ref, cbias_ref, wa_ref, wx_ref, ba_ref, bx_ref, lam_ref, h0_ref,
             h_ref, hc_ref, a_s, b_s):
        i = pl.program_id(1)
        tp = pos(i)

        @pl.when(i == 0)
        def _():
            hc_ref[...] = h0_ref[...]

        u_t = u_ref[...]
        um2, um1, up1 = _conv_taps(u_t, up_ref[...], un_ref[...], tp == 0, tp == nt - 1)
        cw = cw_ref[...]
        uv = um2 * cw[0:1] + um1 * cw[1:2] + u_t * cw[2:3] + up1 * cw[3:4] + cbias_ref[...]
        cl = LRU_C * _log_sigmoid(lam_ref[...])
        ba, bx = ba_ref[...], bx_ref[...]
        for g in range(n_g):
            uvg, r, ii, a, sq = _lru_gates(uv, wa_ref, wx_ref, ba, bx, cl, g)
            b = sq * (ii * uvg)
            for j in range(N_SEG):
                a_s[g, pl.ds(j * stride, seg), :] = a[j * seg:(j + 1) * seg]
                b_s[g, pl.ds(j * stride, seg), :] = b[j * seg:(j + 1) * seg]

        def write_out(j, g, h):
            h_ref[pl.ds(j * seg, seg), pl.ds(g * LANES, LANES)] = h

        _scan_tile(a_s, b_s, hc_ref, write_out, seg, reverse=(d == 1))

    tile, prev, nxt = _lru_specs(s, tm, cb, pos, nt)
    vec, dvec, wmat = _lru_param_specs(cb, d)
    return pl.pallas_call(
        body, name=name,
        out_shape=[jax.ShapeDtypeStruct((s, D_INNER), F32), jax.ShapeDtypeStruct((1, D_INNER), F32)],
        grid=(D_INNER // cb, nt),
        in_specs=[tile, prev, nxt, pl.BlockSpec((4, cb), lambda c, i: (0, c)), vec, wmat, wmat, dvec, dvec, dvec, vec],
        out_specs=[tile, vec],
        scratch_shapes=[pltpu.VMEM((n_g, N_SEG * stride, LANES), F32)] * 2,
        compiler_params=_cparams(dimension_semantics=("arbitrary", "arbitrary")),
    )(u, u, u, p["conv_w"], p["conv_b"], p["wa"], p["wx"], p["ba"], p["bx"], p["lam"], h0)


def _lru_bwd(u, dh, h, h0, lam_in, p, d, name):
    s = u.shape[0]
    tm = min(TM_LRU, s)
    cb = CB_LRU
    n_g = cb // LANES
    nt = s // tm
    seg = tm // N_SEG
    stride = seg + SUBLANES
    pos = (lambda i: nt - 1 - i) if d == 0 else (lambda i: i)

    def body(u_ref, up_ref, un_ref, dh_ref, h_ref, hh_ref, cw_ref, cbias_ref, wa_ref, wx_ref, ba_ref, bx_ref,
             lam_ref, h0_ref, lin_ref, duv_ref, gwa_ref, gwx_ref, gv_ref, lc_ref, a_s, b_s, lp_s,
             r_s, i_s, q_s, a_keep):
        i = pl.program_id(1)
        tp = pos(i)

        @pl.when(i == 0)
        def _():
            lc_ref[...] = lin_ref[...]
            gwa_ref[...] = jnp.zeros_like(gwa_ref)
            gwx_ref[...] = jnp.zeros_like(gwx_ref)
            gv_ref[...] = jnp.zeros_like(gv_ref)

        u_t = u_ref[...]
        um2, um1, up1 = _conv_taps(u_t, up_ref[...], un_ref[...], tp == 0, tp == nt - 1)
        cw = cw_ref[...]
        uv = um2 * cw[0:1] + um1 * cw[1:2] + u_t * cw[2:3] + up1 * cw[3:4] + cbias_ref[...]
        lam = lam_ref[...]
        cl = LRU_C * _log_sigmoid(lam)
        ba, bx = ba_ref[...], bx_ref[...]
        dh_t = dh_ref[...]
        carry_in = lc_ref[...]
        for g in range(n_g):
            sl = slice(g * LANES, (g + 1) * LANES)
            _, r, ii, a, sq = _lru_gates(uv, wa_ref, wx_ref, ba, bx, cl, g)
            r_s[:, sl], i_s[:, sl], q_s[:, sl], a_keep[:, sl] = r, ii, sq, a
            b = a * dh_t[:, sl]
            for j in range(N_SEG):
                a_s[g, pl.ds(j * stride, seg), :] = a[j * seg:(j + 1) * seg]
                b_s[g, pl.ds(j * stride, seg), :] = b[j * seg:(j + 1) * seg]

        def write_out(j, g, v):
            lp_s[pl.ds(j * seg, seg), pl.ds(g * LANES, LANES)] = v

        _scan_tile(a_s, b_s, lc_ref, write_out, seg, reverse=(d == 0))

        h_t = h_ref[...]
        hh = hh_ref[...]
        if d == 0:
            edge = jnp.where(tp == 0, h0_ref[...], hh[7:8])
            h_prev = _shift_down(h_t, edge)
            lam_t = dh_t + _shift_up(lp_s[...], carry_in)
        else:
            edge = jnp.where(tp == nt - 1, h0_ref[...], hh[0:1])
            h_prev = _shift_up(h_t, edge)
            lam_t = dh_t + _shift_down(lp_s[...], carry_in)

        dsig = LRU_C * _sigmoid(-lam)
        for g in range(n_g):
            sl = slice(g * LANES, (g + 1) * LANES)
            uvg, r, ii, a, sq = uv[:, sl], r_s[:, sl], i_s[:, sl], a_keep[:, sl], q_s[:, sl]
            lt = lam_t[:, sl]
            iu = ii * uvg
            dla = lt * h_prev[:, sl] * a - (lt * iu) * (a * a / sq)
            dzr = (dla * cl[:, sl]) * r * (1.0 - r)
            dzi = (lt * sq * uvg) * ii * (1.0 - ii)
            duv_ref[:, sl] = lt * sq * ii + _dot_nt(dzr, wa_ref[g]) + _dot_nt(dzi, wx_ref[g])
            gwa_ref[g] += _dot_tn(uvg, dzr)
            gwx_ref[g] += _dot_tn(uvg, dzi)
            gv_ref[0:1, sl] += _rowsum(dzr)
            gv_ref[1:2, sl] += _rowsum(dzi)
            gv_ref[2:3, sl] += _rowsum(dla * r) * dsig[:, sl]

    tile, prev, nxt = _lru_specs(s, tm, cb, pos, nt)
    vec, dvec, wmat = _lru_param_specs(cb, d)
    hh_spec = prev if d == 0 else nxt
    gw_spec = pl.BlockSpec((n_g, LRU_BLOCK, LRU_BLOCK), lambda c, i: (c, 0, 0))
    n_blk = D_INNER // LRU_BLOCK
    return pl.pallas_call(
        body, name=name,
        out_shape=[jax.ShapeDtypeStruct((s, D_INNER), F32),
                   jax.ShapeDtypeStruct((n_blk, LRU_BLOCK, LRU_BLOCK), F32),
                   jax.ShapeDtypeStruct((n_blk, LRU_BLOCK, LRU_BLOCK), F32),
                   jax.ShapeDtypeStruct((SUBLANES, D_INNER), F32),
                   jax.ShapeDtypeStruct((1, D_INNER), F32)],
        grid=(D_INNER // cb, nt),
        in_specs=[tile, prev, nxt, tile, tile, hh_spec, pl.BlockSpec((4, cb), lambda c, i: (0, c)), vec,
                  wmat, wmat, dvec, dvec, dvec, vec, vec],
        out_specs=[tile, gw_spec, gw_spec, pl.BlockSpec((SUBLANES, cb), lambda c, i: (0, c)), vec],
        scratch_shapes=[pltpu.VMEM((n_g, N_SEG * stride, LANES), F32)] * 2 + [pltpu.VMEM((tm, cb), F32)] * 5,
        compiler_params=_cparams(dimension_semantics=("arbitrary", "arbitrary")),
    )(u, u, u, dh, h, h, p["conv_w"], p["conv_b"], p["wa"], p["wx"], p["ba"], p["bx"], p["lam"], h0, lam_in)


def _out0(hf, hb, g, xt, gt, wo, lg, lb, name):
    t = xt.shape[0]
    tm = min(TM_MM, t)

    def body(hf_ref, hb_ref, g_ref, x_ref, gt_ref, w_ref, lg_ref, lb_ref, x1_ref, br_ref):
        gg = g_ref[...]
        p = (hf_ref[...] + hb_ref[...]) * (gg * _sigmoid(gg))
        br = _dot(p, w_ref[...])
        z = ALPHA * x_ref[...] + gt_ref[...] * br
        xhat, _ = _layer_norm_stats(z)
        x1_ref[...] = xhat * lg_ref[...] + lb_ref[...]
        br_ref[...] = br

    wide = pl.BlockSpec((tm, D_INNER), lambda i: (i, 0))
    nar = pl.BlockSpec((tm, D_MODEL), lambda i: (i, 0))
    row = pl.BlockSpec((1, D_MODEL), lambda i: (0, 0))
    return pl.pallas_call(
        body, name=name, out_shape=[jax.ShapeDtypeStruct((t, D_MODEL), F32)] * 2, grid=(t // tm,),
        in_specs=[wide, wide, wide, nar, row,
                  pl.BlockSpec((D_INNER, D_MODEL), lambda i: (0, 0), pipeline_mode=pl.Buffered(1)), row, row],
        out_specs=[nar, nar],
        compiler_params=_cparams(dimension_semantics=("arbitrary",)),
    )(hf, hb, g, xt, gt, wo, lg, lb)


def _window(n, w):
    t = np.arange(n)
    return np.clip(t - w // 2, 0, n), np.clip(t + w // 2, 0, n)


def _pool_tables(n_rows, transpose):
    boxes, inv_c, inv_r = [], [], []
    for w in POOL_WINDOWS:
        lo, hi = _window(GRID_W, w)
        m = np.zeros((GRID_W, GRID_W), np.float32)
        for r in range(GRID_W):
            m[r, lo[r]:hi[r]] = 1.0
        m = np.kron(np.eye(POOL_TOK // GRID_W, dtype=np.float32), m)
        boxes.append(m.T if transpose else m)
        inv_c.append(np.broadcast_to((1.0 / (hi - lo).astype(np.float32))[:, None], (GRID_W, LANES)))
        lo_r, hi_r = _window(n_rows, w)
        inv_r.append(1.0 / (hi_r - lo_r).astype(np.float32))
    return (jnp.asarray(np.stack(boxes), MXU_DTYPE), jnp.asarray(np.stack(inv_c), F32),
            jnp.asarray(np.stack(inv_r), F32))


def _pool_mix(xin, transpose, out_dtype, name):
    s = xin.shape[0]
    n_rows = s // GRID_W
    pad_t = SUBLANES * GRID_W
    rows_per_blk = POOL_TOK // GRID_W
    n_slab = D_INNER // LANES
    slabs_per_group = POOL_GROUP // LANES
    n_win = len(POOL_WINDOWS)
    boxes, inv_c, inv_r = _pool_tables(n_rows, transpose)

    def body(invr_ref, box_ref, invc_ref, x_ref, o_ref, pad_s):
        k = pl.program_id(0) // slabs_per_group
        pad_s[pl.ds(0, pad_t), :] = jnp.zeros((pad_t, LANES), F32)
        pad_s[pl.ds(pad_t + s, pad_t), :] = jnp.zeros((pad_t, LANES), F32)

        for kk, w in enumerate(POOL_WINDOWS):
            half = w // 2
            offsets = list(range(-(half - 1), half + 1)) if transpose else list(range(-half, half))

            @pl.when(k == kk)
            def _():
                inv_col = invc_ref[kk]

                def col_box(b, carry):
                    st = pl.multiple_of(b * POOL_TOK, POOL_TOK)
                    xb = x_ref[pl.ds(st, POOL_TOK), :]
                    if transpose:
                        xb = xb * jnp.concatenate(
                            [inv_col * invr_ref[kk, b * rows_per_blk + q] for q in range(rows_per_blk)], axis=0)
                    hi = xb.astype(MXU_DTYPE)
                    lo = (xb - hi.astype(F32)).astype(MXU_DTYPE)
                    both = jnp.dot(box_ref[kk], jnp.concatenate([hi, lo], axis=1), preferred_element_type=F32)
                    pad_s[pl.ds(pad_t + st, POOL_TOK), :] = both[:, :LANES] + both[:, LANES:]
                    return carry
                lax.fori_loop(0, s // POOL_TOK, col_box, 0)

                def row_box(r, carry):
                    st = pl.multiple_of(r * GRID_W, GRID_W)
                    acc = pad_s[pl.ds(pad_t + st + offsets[0] * GRID_W, GRID_W), :]
                    for o in offsets[1:]:
                        acc = acc + pad_s[pl.ds(pad_t + st + o * GRID_W, GRID_W), :]
                    if not transpose:
                        acc = acc * (inv_col * invr_ref[kk, r])
                    o_ref[pl.ds(st, GRID_W), :] = (acc - x_ref[pl.ds(st, GRID_W), :]).astype(out_dtype)
                    return carry
                lax.fori_loop(0, n_rows, row_box, 0)

    slab = pl.BlockSpec((s, LANES), lambda i: (0, i))
    return pl.pallas_call(
        body, name=name, out_shape=jax.ShapeDtypeStruct((s, D_INNER), out_dtype), grid=(n_slab,),
        in_specs=[pl.BlockSpec(memory_space=pltpu.SMEM),
                  pl.BlockSpec((n_win, POOL_TOK, POOL_TOK), lambda i: (0, 0, 0)),
                  pl.BlockSpec((n_win, GRID_W, LANES), lambda i: (0, 0, 0)), slab],
        out_specs=slab,
        scratch_shapes=[pltpu.VMEM((s + 2 * pad_t, LANES), F32)],
        compiler_params=_cparams(dimension_semantics=("arbitrary",)),
    )(inv_r, boxes, inv_c, xin)


def _out1(dmix, pw, ps, g, x1, gt, wo, lg, lb, tgt, name):
    t = x1.shape[0]
    tm = min(TM_MM, t)
    n_grp = len(POOL_WINDOWS)

    def body(d_ref, pw_ref, ps_ref, g_ref, x1_ref, gt_ref, w_ref, lg_ref, lb_ref, tgt_ref, dz_ref, st_ref):
        @pl.when(pl.program_id(0) == 0)
        def _():
            st_ref[...] = jnp.zeros_like(st_ref)

        br = jnp.zeros((tm, D_MODEL), F32)
        for k in range(n_grp):
            sl = slice(k * POOL_GROUP, (k + 1) * POOL_GROUP)
            y = jnp.dot(d_ref[:, sl], pw_ref[k], preferred_element_type=F32) * ps_ref[:, sl]
            gg = g_ref[:, sl]
            br = br + _dot(y * (gg * _sigmoid(gg)), w_ref[sl, :])
        z = ALPHA * x1_ref[...] + gt_ref[...] * br
        xhat, rstd = _layer_norm_stats(z)
        lg_v = lg_ref[...]
        err = xhat * lg_v + lb_ref[...] - tgt_ref[...]
        dy = err * (1.0 / D_MODEL)
        dz = _layer_norm_bwd(dy, xhat, rstd, lg_v)
        dz_ref[...] = dz
        st_ref[0:1, :] += _rowsum(dy * xhat)
        st_ref[1:2, :] += _rowsum(dy)
        st_ref[2:3, :] += _rowsum(dz * br)
        st_ref[3:4, :] += _rowsum(err * err)

    wide = pl.BlockSpec((tm, D_INNER), lambda i: (i, 0))
    nar = pl.BlockSpec((tm, D_MODEL), lambda i: (i, 0))
    row = pl.BlockSpec((1, D_MODEL), lambda i: (0, 0))
    return pl.pallas_call(
        body, name=name,
        out_shape=[jax.ShapeDtypeStruct((t, D_MODEL), F32), jax.ShapeDtypeStruct((SUBLANES, D_MODEL), F32)],
        grid=(t // tm,),
        in_specs=[wide, pl.BlockSpec((n_grp, POOL_GROUP, POOL_GROUP), lambda i: (0, 0, 0)),
                  pl.BlockSpec((1, D_INNER), lambda i: (0, 0)), wide, nar, row,
                  pl.BlockSpec((D_INNER, D_MODEL), lambda i: (0, 0), pipeline_mode=pl.Buffered(1)), row, row, nar],
        out_specs=[nar, pl.BlockSpec((SUBLANES, D_MODEL), lambda i: (0, 0))],
        compiler_params=_cparams(dimension_semantics=("arbitrary",)),
    )(dmix, pw, ps, g, x1, gt, wo, lg, lb, tgt)


def _flush(acc, out_hbm, sem):
    cp = pltpu.make_async_copy(acc, out_hbm, sem)
    cp.start()
    cp.wait()


def _bout1(dz, dmix, g, pw, ps, gt, wo, name):
    t = dz.shape[0]
    tm = min(TM_BWD, t)
    nt = t // tm
    n_grp = len(POOL_WINDOWS)

    def body(dz_ref, d_ref, g_ref, pw_ref, ps_ref, gt_ref, w_ref, dd_ref, dg_ref, gwo_hbm, gpw_hbm, gps_ref,
             gwo_acc, gpw_acc, sems):
        i = pl.program_id(0)

        @pl.when(i == 0)
        def _():
            gwo_acc[...] = jnp.zeros_like(gwo_acc)
            gpw_acc[...] = jnp.zeros_like(gpw_acc)
            gps_ref[...] = jnp.zeros_like(gps_ref)

        db = (gt_ref[...] * dz_ref[...]).astype(MXU_DTYPE)
        for k in range(n_grp):
            sl = slice(k * POOL_GROUP, (k + 1) * POOL_GROUP)
            dk = d_ref[:, sl]
            po = jnp.dot(dk, pw_ref[k], preferred_element_type=F32)
            psk = ps_ref[:, sl]
            y = po * psk
            gg = g_ref[:, sl]
            sg = _sigmoid(gg)
            silu = gg * sg
            gwo_acc[sl, :] += _dot_tn(y * silu, db)
            dp = _dot_nt(db, w_ref[sl, :])
            dy = dp * silu
            dg_ref[:, sl] = (dp * y * (sg * (1.0 + gg * (1.0 - sg)))).astype(MXU_DTYPE)
            gps_ref[0:1, sl] += _rowsum(dy * po)
            dpo = (dy * psk).astype(MXU_DTYPE)
            gpw_acc[k] += _dot_tn(dk, dpo)
            dd_ref[:, sl] = _dot_nt(dpo, pw_ref[k])

        @pl.when(i == nt - 1)
        def _():
            _flush(gwo_acc, gwo_hbm, sems.at[0])
            _flush(gpw_acc, gpw_hbm, sems.at[1])

    wide = pl.BlockSpec((tm, D_INNER), lambda i: (i, 0))
    nar = pl.BlockSpec((tm, D_MODEL), lambda i: (i, 0))
    return pl.pallas_call(
        body, name=name,
        out_shape=[jax.ShapeDtypeStruct((t, D_INNER), F32), jax.ShapeDtypeStruct((t, D_INNER), MXU_DTYPE),
                   jax.ShapeDtypeStruct((D_INNER, D_MODEL), F32),
                   jax.ShapeDtypeStruct((n_grp, POOL_GROUP, POOL_GROUP), F32),
                   jax.ShapeDtypeStruct((SUBLANES, D_INNER), F32)],
        grid=(nt,),
        in_specs=[nar, wide, wide, pl.BlockSpec((n_grp, POOL_GROUP, POOL_GROUP), lambda i: (0, 0, 0)),
                  pl.BlockSpec((1, D_INNER), lambda i: (0, 0)), pl.BlockSpec((1, D_MODEL), lambda i: (0, 0)),
                  pl.BlockSpec((D_INNER, D_MODEL), lambda i: (0, 0), pipeline_mode=pl.Buffered(1))],
        out_specs=[wide, wide, ANY, ANY, pl.BlockSpec((SUBLANES, D_INNER), lambda i: (0, 0))],
        scratch_shapes=[pltpu.VMEM((D_INNER, D_MODEL), F32), pltpu.VMEM((n_grp, POOL_GROUP, POOL_GROUP), F32),
                        pltpu.SemaphoreType.DMA((2,))],
        compiler_params=_cparams(dimension_semantics=("arbitrary",)),
    )(dz, dmix, g, pw, ps, gt, wo)


def _bout0(dx1, xt, br0, lg, hf, hb, g, gt, wo, name):
    t = dx1.shape[0]
    tm = min(TM_BWD, t)
    nt = t // tm

    def body(dx_ref, x_ref, br_ref, lg_ref, hf_ref, hb_ref, g_ref, gt_ref, w_ref,
             dz_ref, dy_ref, dg_ref, gwo_hbm, st_ref, gwo_acc, sem):
        i = pl.program_id(0)

        @pl.when(i == 0)
        def _():
            gwo_acc[...] = jnp.zeros_like(gwo_acc)
            st_ref[...] = jnp.zeros_like(st_ref)

        dx = dx_ref[...]
        br = br_ref[...]
        gate = gt_ref[...]
        xhat, rstd = _layer_norm_stats(ALPHA * x_ref[...] + gate * br)
        dz = _layer_norm_bwd(dx, xhat, rstd, lg_ref[...])
        dz_ref[...] = dz
        st_ref[0:1, :] += _rowsum(dx * xhat)
        st_ref[1:2, :] += _rowsum(dx)
        st_ref[2:3, :] += _rowsum(dz * br)
        db = (gate * dz).astype(MXU_DTYPE)
        y = hf_ref[...] + hb_ref[...]
        gg = g_ref[...]
        sg = _sigmoid(gg)
        silu = gg * sg
        gwo_acc[...] += _dot_tn(y * silu, db)
        dp = _dot_nt(db, w_ref[...])
        dy_ref[...] = dp * silu
        dg_ref[...] = (dp * y * (sg * (1.0 + gg * (1.0 - sg)))).astype(MXU_DTYPE)

        @pl.when(i == nt - 1)
        def _():
            _flush(gwo_acc, gwo_hbm, sem)

    wide = pl.BlockSpec((tm, D_INNER), lambda i: (i, 0))
    nar = pl.BlockSpec((tm, D_MODEL), lambda i: (i, 0))
    row = pl.BlockSpec((1, D_MODEL), lambda i: (0, 0))
    return pl.pallas_call(
        body, name=name,
        out_shape=[jax.ShapeDtypeStruct((t, D_MODEL), F32), jax.ShapeDtypeStruct((t, D_INNER), F32),
                   jax.ShapeDtypeStruct((t, D_INNER), MXU_DTYPE), jax.ShapeDtypeStruct((D_INNER, D_MODEL), F32),
                   jax.ShapeDtypeStruct((SUBLANES, D_MODEL), F32)],
        grid=(nt,),
        in_specs=[nar, nar, nar, row, wide, wide, wide, row,
                  pl.BlockSpec((D_INNER, D_MODEL), lambda i: (0, 0), pipeline_mode=pl.Buffered(1))],
        out_specs=[nar, wide, wide, ANY, pl.BlockSpec((SUBLANES, D_MODEL), lambda i: (0, 0))],
        scratch_shapes=[pltpu.VMEM((D_INNER, D_MODEL), F32), pltpu.SemaphoreType.DMA(())],
        compiler_params=_cparams(dimension_semantics=("arbitrary",)),
    )(dx1, xt, br0, lg, hf, hb, g, gt, wo)


def _conv_bwd(duvf, duvb, u, conv_w, name):
    s = u.shape[0]
    tm = min(TM_LRU, s)
    cb = CB_LRU
    nt = s // tm

    def body(df_ref, dfp_ref, dfn_ref, db_ref, dbp_ref, dbn_ref, u_ref, up_ref, un_ref, cw_ref, du_ref, cst_ref):
        i = pl.program_id(1)

        @pl.when(i == 0)
        def _():
            cst_ref[...] = jnp.zeros_like(cst_ref)

        first, last = i == 0, i == nt - 1
        pz = jnp.where(first, 0.0, 1.0)
        nz = jnp.where(last, 0.0, 1.0)
        dout = df_ref[...] + db_ref[...]
        dm1, dp1, dp2 = _shifted(dout, (dfp_ref[...] + dbp_ref[...]) * pz, (dfn_ref[...] + dbn_ref[...]) * nz,
                                 [-1, 1, 2])
        cw = cw_ref[...]
        du_ref[...] = (dp2 * cw[0:1] + dp1 * cw[1:2] + dout * cw[2:3] + dm1 * cw[3:4]).astype(MXU_DTYPE)
        u_t = u_ref[...]
        um2, um1, up1 = _conv_taps(u_t, up_ref[...], un_ref[...], first, last)
        cst_ref[0:1, :] += _rowsum(dout * um2)
        cst_ref[1:2, :] += _rowsum(dout * um1)
        cst_ref[2:3, :] += _rowsum(dout * u_t)
        cst_ref[3:4, :] += _rowsum(dout * up1)
        cst_ref[4:5, :] += _rowsum(dout)

    tile, prev, nxt = _lru_specs(s, tm, cb, lambda i: i, nt)
    return pl.pallas_call(
        body, name=name,
        out_shape=[jax.ShapeDtypeStruct((s, D_INNER), MXU_DTYPE), jax.ShapeDtypeStruct((SUBLANES, D_INNER), F32)],
        grid=(D_INNER // cb, nt),
        in_specs=[tile, prev, nxt] * 3 + [pl.BlockSpec((4, cb), lambda c, i: (0, c))],
        out_specs=[tile, pl.BlockSpec((SUBLANES, cb), lambda c, i: (0, c))],
        compiler_params=_cparams(dimension_semantics=("arbitrary", "arbitrary")),
    )(duvf, duvf, duvf, duvb, duvb, duvb, u, u, u, conv_w)


def _bin(du, dg, xin, dzin, sc, sh, wg, name):
    t = xin.shape[0]
    tm = min(TM_MM, t)
    nt = t // tm
    has_g, has_dx = dg is not None, dzin is not None
    half = N_WBLK // 2
    n_blk = N_WBLK if has_g else half

    def body(*refs):
        refs = list(refs)
        du_ref = refs.pop(0)
        dg_ref = refs.pop(0) if has_g else None
        x_ref = refs.pop(0)
        dz_ref = refs.pop(0) if has_dx else None
        sc_ref, sh_ref, w_ref = refs.pop(0), refs.pop(0), refs.pop(0)
        dx_ref = refs.pop(0) if has_dx else None
        gw_hbm, st_ref, gw_acc, sem = refs
        i = pl.program_id(0)

        @pl.when(i == 0)
        def _():
            gw_acc[...] = jnp.zeros_like(gw_acc)
            st_ref[...] = jnp.zeros_like(st_ref)

        xv = x_ref[...]
        scale = 1.0 + sc_ref[...]
        h = (xv * scale + sh_ref[...]).astype(MXU_DTYPE)
        dh = None
        for k in range(n_blk):
            src = du_ref if k < half else dg_ref
            kk = k % half
            dk = src[:, kk * WBLK:(kk + 1) * WBLK]
            gw_acc[k] += _dot_tn(h, dk)
            contrib = _dot_nt(dk, w_ref[k])
            dh = contrib if dh is None else dh + contrib
        st_ref[0:1, :] += _rowsum(dh * xv)
        st_ref[1:2, :] += _rowsum(dh)
        if has_dx:
            dx_ref[...] = ALPHA * dz_ref[...] + dh * scale

        @pl.when(i == nt - 1)
        def _():
            _flush(gw_acc, gw_hbm, sem)

    wide = pl.BlockSpec((tm, D_INNER), lambda i: (i, 0))
    nar = pl.BlockSpec((tm, D_MODEL), lambda i: (i, 0))
    row = pl.BlockSpec((1, D_MODEL), lambda i: (0, 0))
    wspec = pl.BlockSpec((n_blk, D_MODEL, WBLK), lambda i: (0, 0, 0), pipeline_mode=pl.Buffered(1))
    in_specs = [wide] + ([wide] if has_g else []) + [nar] + ([nar] if has_dx else []) + [row, row, wspec]
    args = [du] + ([dg] if has_g else []) + [xin] + ([dzin] if has_dx else []) + [sc, sh, wg]
    out_shape = ([jax.ShapeDtypeStruct((t, D_MODEL), F32)] if has_dx else []) + [
        jax.ShapeDtypeStruct((n_blk, D_MODEL, WBLK), F32), jax.ShapeDtypeStruct((SUBLANES, D_MODEL), F32)]
    out_specs = ([nar] if has_dx else []) + [ANY, pl.BlockSpec((SUBLANES, D_MODEL), lambda i: (0, 0))]
    return pl.pallas_call(
        body, name=name, out_shape=out_shape, grid=(nt,), in_specs=in_specs, out_specs=out_specs,
        scratch_shapes=[pltpu.VMEM((n_blk, D_MODEL, WBLK), F32), pltpu.SemaphoreType.DMA(())],
        compiler_params=_cparams(dimension_semantics=("arbitrary",)),
    )(*args)


def _blocks_by_device(a, axis):
    shape = a.shape
    a = a.reshape(shape[:axis] + (N_DEV, shape[axis] // N_DEV) + shape[axis + 1:])
    return jnp.moveaxis(a, axis, 0)


def kernel(x, c, ctx, c_ctx, w_mod, b_mod, w_in, w_out, ln_g, ln_b, conv_w, conv_b, lru_wa, lru_ba, lru_wx, lru_bx, lru_lam, pool_w, pool_scale, loss_target, m_c_ctx, m_w_mod, m_b_mod, m_w_in, m_w_out, m_ln_g, m_ln_b, m_conv_w, m_conv_b, m_lru_wa, m_lru_ba, m_lru_wx, m_lru_bx, m_lru_lam, m_pool_w, m_pool_scale, v_c_ctx, v_w_mod, v_b_mod, v_w_in, v_w_out, v_ln_g, v_ln_b, v_conv_w, v_conv_b, v_lru_wa, v_lru_ba, v_lru_wx, v_lru_bx, v_lru_lam, v_pool_w, v_pool_scale):
    xi, yi, ci = _my_pos()
    dev = 4 * xi + 2 * yi + ci
    xt, ctxt, tgt = x[0], ctx[0], loss_target[0]
    n_mod = w_mod.shape[2]

    small_shapes = [(D_MODEL,), conv_w.shape[1:], lru_ba.shape[1:], lru_bx.shape[1:], lru_lam.shape[1:],
                    pool_scale.shape[1:]]
    small = _to_rows([c[0], conv_w[0], lru_ba[0], lru_bx[0], lru_lam[0], pool_scale[0]], SUBLANES)
    small_all, = _all_gather([small], "gather_small")
    pieces = [_split_rows(small_all[k], small_shapes) for k in range(N_DEV)]
    c_all = jnp.stack([p[0] for p in pieces])
    conv_w_f = jnp.concatenate([p[1] for p in pieces], axis=-1)
    lru_ba_f = jnp.concatenate([p[2] for p in pieces], axis=-1)[:, None, :]
    lru_bx_f = jnp.concatenate([p[3] for p in pieces], axis=-1)[:, None, :]
    lru_lam_f = jnp.concatenate([p[4] for p in pieces], axis=-1)[:, None, :]
    pool_scale_f = jnp.concatenate([p[5] for p in pieces], axis=-1)[None, :]

    cond = jnp.concatenate([c_all, jnp.broadcast_to(c_ctx[None, :], (N_DEV, D_MODEL))], axis=0)
    b_my = lax.dynamic_slice(b_mod, (0, dev * n_mod), (2, n_mod))[:, None, :]
    mod_part = _mod_fwd(cond, w_mod, b_my, "mod_fwd")
    mod_all, = _all_gather([mod_part], "gather_mod")
    mod = jnp.transpose(mod_all, (1, 2, 0, 3)).reshape(2, 16, 3 * D_MODEL)
    mod_me = lax.dynamic_slice(mod, (0, dev, 0), (2, 1, 3 * D_MODEL))
    sh = [mod_me[i, :, 0:D_MODEL] for i in range(2)]
    sc = [mod_me[i, :, D_MODEL:2 * D_MODEL] for i in range(2)]
    gt = [mod_me[i, :, 2 * D_MODEL:] for i in range(2)]
    shc, scc = mod[0, 8:9, 0:D_MODEL], mod[0, 8:9, D_MODEL:2 * D_MODEL]

    wi0, wi1, wo0, wo1, pool_w_g = _all_gather(
        [w_in[0].astype(MXU_DTYPE), w_in[1].astype(MXU_DTYPE), w_out[0].astype(MXU_DTYPE),
         w_out[1].astype(MXU_DTYPE), pool_w[0].astype(MXU_DTYPE)], "gather_weights")
    w_in_l = [wi0, wi1]
    w_out_l = [wo0.reshape(D_INNER, D_MODEL), wo1.reshape(D_INNER, D_MODEL)]
    pool_w_f = jnp.transpose(pool_w_g, (1, 0, 2, 3)).reshape(len(POOL_WINDOWS), POOL_GROUP, POOL_GROUP)
    lg = [ln_g[i][None, :] for i in range(2)]
    lb = [ln_b[i][None, :] for i in range(2)]
    lru_p = dict(conv_w=conv_w_f, conv_b=conv_b, wa=lru_wa[0].astype(MXU_DTYPE), wx=lru_wx[0].astype(MXU_DTYPE),
                 ba=lru_ba_f, bx=lru_bx_f, lam=lru_lam_f)
    zero_state = jnp.zeros((1, D_INNER), F32)

    u0, g0 = _in_proj(xt, sc[0], sh[0], w_in_l[0], "in_proj0")
    uc, _ = _in_proj(ctxt, scc, shc, w_in_l[0], "in_proj0_ctx")
    hcf, cf = _lru_fwd(uc, zero_state, lru_p, 0, "lru_fwd_ctx_f")
    hcb, cbk = _lru_fwd(uc, zero_state, lru_p, 1, "lru_fwd_ctx_b")
    hf, _ = _lru_fwd(u0, cf, lru_p, 0, "lru_fwd_f")
    hb, _ = _lru_fwd(u0, cbk, lru_p, 1, "lru_fwd_b")
    x1, br0 = _out0(hf, hb, g0, xt, gt[0], w_out_l[0], lg[0], lb[0], "out0")
    u1, g1 = _in_proj(x1, sc[1], sh[1], w_in_l[1], "in_proj1")
    dmix = _pool_mix(u1, False, MXU_DTYPE, "pool_fwd")
    dz1, st1 = _out1(dmix, pool_w_f, pool_scale_f, g1, x1, gt[1], w_out_l[1], lg[1], lb[1], tgt, "out1")
    loss = lax.psum((0.5 / D_MODEL) * jnp.sum(st1[3]), ("x", "y", "c"))

    dd, dg1, gwo1, gpw, gps = _bout1(dz1, dmix, g1, pool_w_f, pool_scale_f, gt[1], w_out_l[1], "bwd_out1")
    du1 = _pool_mix(dd, True, MXU_DTYPE, "pool_bwd")
    dx1, gwi1, stb1 = _bin(du1, dg1, x1, dz1, sc[1], sh[1], w_in_l[1], "bwd_in1")
    dz0, dy0, dg0, gwo0, stl0 = _bout0(dx1, xt, br0, lg[0], hf, hb, g0, gt[0], w_out_l[0], "bwd_out0")
    duvf, gwa_f, gwx_f, gv_f, dh0f = _lru_bwd(u0, dy0, hf, cf, zero_state, lru_p, 0, "lru_bwd_f")
    duvb, gwa_b, gwx_b, gv_b, dh0b = _lru_bwd(u0, dy0, hb, cbk, zero_state, lru_p, 1, "lru_bwd_b")
    zero_dh = jnp.zeros_like(uc)
    ducf, gwa_cf, gwx_cf, gv_cf, _ = _lru_bwd(uc, zero_dh, hcf, zero_state, dh0f, lru_p, 0, "lru_bwd_ctx_f")
    ducb, gwa_cb, gwx_cb, gv_cb, _ = _lru_bwd(uc, zero_dh, hcb, zero_state, dh0b, lru_p, 1, "lru_bwd_ctx_b")
    du0, cst0 = _conv_bwd(duvf, duvb, u0, conv_w_f, "conv_bwd")
    duc, cstc = _conv_bwd(ducf, ducb, uc, conv_w_f, "conv_bwd_ctx")
    gx, gwi0, stb0 = _bin(du0, dg0, xt, dz0, sc[0], sh[0], w_in_l[0], "bwd_in0")
    gwic, stc = _bin(duc, None, ctxt, None, scc, shc, w_in_l[0][:N_WBLK // 2], "bwd_in0_ctx")
    gwi0 = gwi0.at[:N_WBLK // 2].add(gwic)

    zero_row = jnp.zeros((1, D_MODEL), F32)
    dm_me = jnp.stack([
        jnp.concatenate([jnp.concatenate([stb0[1:2], stb0[0:1], stl0[2:3]], axis=1),
                         jnp.concatenate([stc[1:2], stc[0:1], zero_row], axis=1)], axis=0),
        jnp.concatenate([jnp.concatenate([stb1[1:2], stb1[0:1], st1[2:3]], axis=1),
                         jnp.zeros((1, 3 * D_MODEL), F32)], axis=0)])
    dm_g, = _all_gather([dm_me], "gather_dmod")
    dm_all = jnp.concatenate([jnp.transpose(dm_g[:, :, 0], (1, 0, 2)), jnp.transpose(dm_g[:, :, 1], (1, 0, 2))],
                             axis=1)
    dm_my = lax.dynamic_slice(dm_all, (0, 0, dev * n_mod), (2, 16, n_mod))
    g_w_mod, g_b_mod, gcc_part = _mod_bwd(cond, dm_all, dm_my, w_mod, "mod_bwd")
    g_b_mod = g_b_mod.reshape(b_mod.shape)

    gwa = jnp.stack([gwa_f + gwa_cf, gwa_b + gwa_cb])
    gwx = jnp.stack([gwx_f + gwx_cf, gwx_b + gwx_cb])
    gv = jnp.stack([gv_f + gv_cf, gv_b + gv_cb])
    cst = cst0 + cstc
    g_ln_g = jnp.stack([stl0[0], st1[0]])
    g_ln_b = jnp.stack([stl0[1], st1[1]])
    sharded = [
        _blocks_by_device(cst[0:4], 1),
        _blocks_by_device(gv[:, 0], 1), _blocks_by_device(gv[:, 1], 1), _blocks_by_device(gv[:, 2], 1),
        _blocks_by_device(gps[0], 0),
    ]
    replicated = [gwa.reshape(-1), gwx.reshape(-1), g_ln_g.reshape(-1), g_ln_b.reshape(-1), cst[4],
                  gcc_part.reshape(-1)]
    sh_sizes = [int(np.prod(a.shape[1:])) for a in sharded]
    rep_sizes = [a.shape[0] // N_DEV for a in replicated]
    n_flat = sum(sh_sizes) + sum(rep_sizes)
    rows = -(-n_flat // LANES)
    rows = -(-rows // FLAT_ROWS) * FLAT_ROWS
    misc = jnp.concatenate([a.reshape(N_DEV, -1) for a in sharded] +
                           [a.reshape(N_DEV, -1) for a in replicated], axis=1)
    misc = jnp.pad(misc, ((0, 0), (0, rows * LANES - n_flat))).reshape(N_DEV, rows, LANES)
    bufs = [gwi0, gwi1, gwo0.reshape(N_DEV, D_INNER // N_DEV, D_MODEL), gwo1.reshape(N_DEV, D_INNER // N_DEV, D_MODEL),
            _blocks_by_device(gpw, 1).reshape(N_DEV, POOL_GROUP // N_DEV * len(POOL_WINDOWS), POOL_GROUP), misc]
    names_r = ["w_in0", "w_in1", "w_out0", "w_out1", "pool_w", "misc"]
    recvs = _sibling_exchange(bufs, "reduce_sibling")
    core = jnp.reshape(ci, (1,)).astype(jnp.int32)
    pairs = [_pair_sum(b, r, core, "reduce_pair_" + n) for b, r, n in zip(bufs, recvs, names_r)]
    p_wi0, p_wi1, p_wo0, p_wo1, p_pw, p_misc = _chip_exchange(pairs, "reduce_chips")
    g_flat = _sum4(p_misc, "reduce_sum_misc").reshape(-1)

    offs = np.cumsum([0] + sh_sizes + rep_sizes)
    n_sh = len(sh_sizes)
    sh_shapes = [conv_w.shape, lru_ba.shape, lru_bx.shape, lru_lam.shape, pool_scale.shape]
    g_sh = [g_flat[offs[k]:offs[k + 1]].reshape(sh_shapes[k]) for k in range(n_sh)]
    g_conv_w, g_lru_ba, g_lru_bx, g_lru_lam, g_pool_scale = g_sh
    rep_block = _to_rows([g_flat[offs[n_sh]:offs[-1]]], SUBLANES)
    rep_all, = _all_gather([rep_block], "gather_replicated")
    rep_flat = rep_all.reshape(N_DEV, -1)
    rep_full, off = [], 0
    for n in rep_sizes:
        rep_full.append(rep_flat[:, off:off + n].reshape(-1))
        off += n
    g_lru_wa = rep_full[0].reshape(lru_wa.shape)
    g_lru_wx = rep_full[1].reshape(lru_wx.shape)
    g_ln_g = rep_full[2].reshape(ln_g.shape)
    g_ln_b = rep_full[3].reshape(ln_b.shape)
    g_conv_b = rep_full[4].reshape(conv_b.shape)
    g_c_ctx = rep_full[5].reshape(c_ctx.shape)

    names = ["c_ctx", "w_mod", "b_mod", "w_in", "w_out", "ln_g", "ln_b", "conv_w", "conv_b", "lru_wa", "lru_ba",
             "lru_wx", "lru_bx", "lru_lam", "pool_w", "pool_scale"]
    weights = dict(c_ctx=c_ctx, w_mod=w_mod, b_mod=b_mod, w_in=w_in, w_out=w_out, ln_g=ln_g, ln_b=ln_b,
                   conv_w=conv_w, conv_b=conv_b, lru_wa=lru_wa, lru_ba=lru_ba, lru_wx=lru_wx, lru_bx=lru_bx,
                   lru_lam=lru_lam, pool_w=pool_w, pool_scale=pool_scale)
    mom_m = dict(c_ctx=m_c_ctx, w_mod=m_w_mod, b_mod=m_b_mod, w_in=m_w_in, w_out=m_w_out, ln_g=m_ln_g, ln_b=m_ln_b,
                 conv_w=m_conv_w, conv_b=m_conv_b, lru_wa=m_lru_wa, lru_ba=m_lru_ba, lru_wx=m_lru_wx,
                 lru_bx=m_lru_bx, lru_lam=m_lru_lam, pool_w=m_pool_w, pool_scale=m_pool_scale)
    mom_v = dict(c_ctx=v_c_ctx, w_mod=v_w_mod, b_mod=v_b_mod, w_in=v_w_in, w_out=v_w_out, ln_g=v_ln_g, ln_b=v_ln_b,
                 conv_w=v_conv_w, conv_b=v_conv_b, lru_wa=v_lru_wa, lru_ba=v_lru_ba, lru_wx=v_lru_wx,
                 lru_bx=v_lru_bx, lru_lam=v_lru_lam, pool_w=v_pool_w, pool_scale=v_pool_scale)
    grads = dict(c_ctx=g_c_ctx, w_mod=g_w_mod, b_mod=g_b_mod, ln_g=g_ln_g, ln_b=g_ln_b,
                 conv_w=g_conv_w, conv_b=g_conv_b, lru_wa=g_lru_wa, lru_ba=g_lru_ba, lru_wx=g_lru_wx,
                 lru_bx=g_lru_bx, lru_lam=g_lru_lam)
    grads["pool_scale"] = g_pool_scale
    delta, new_m, new_v = {}, {}, {}

    def update_parts(n, parts, view):
        res = _adamw_parts(weights[n].reshape(view), parts, mom_m[n].reshape(view), mom_v[n].reshape(view),
                           "adamw_" + n)
        grads[n], delta[n], new_m[n], new_v[n] = [r.reshape(weights[n].shape) for r in res]

    update_parts("w_in", [p_wi0, p_wi1], w_in.shape)
    update_parts("w_out", [p_wo0, p_wo1], w_out.shape)
    update_parts("pool_w", [p_pw], (1,) + p_pw.shape[1:])
    for n in ("w_mod", "lru_wa", "lru_wx"):
        shape = weights[n].shape
        view = (int(np.prod(shape[:-1])), shape[-1])
        res = _adamw(weights[n].reshape(view), grads[n].reshape(view), mom_m[n].reshape(view),
                     mom_v[n].reshape(view), "adamw_" + n)
        delta[n], new_m[n], new_v[n] = [r.reshape(shape) for r in res]

    small = [n for n in names if n not in delta]
    shapes = [weights[n].shape for n in small]
    flat = lambda d: _to_rows([d[n] for n in small], FLAT_ROWS)
    res = _adamw(flat(weights), flat(grads), flat(mom_m), flat(mom_v), "adamw_small")
    for d, r in zip((delta, new_m, new_v), res):
        d.update(zip(small, _split_rows(r, shapes)))

    return (loss, gx[None], *[grads[n] for n in names], *[delta[n] for n in names],
            *[new_m[n] for n in names], *[new_v[n] for n in names])
```

```python
import functools

import numpy as np
import jax
import jax.numpy as jnp
from jax import lax
from jax.experimental import pallas as pl
from jax.experimental.pallas import tpu as pltpu

F32 = jnp.float32
BF16 = jnp.bfloat16
MXU_DTYPE = BF16

D_MODEL = 1024
D_INNER = 2048
LRU_BLOCK = 128
GRID_W = 64
POOL_WINDOWS = (2, 4, 8, 16)
POOL_GROUP = 512
ALPHA = float(4 ** 0.25)
LN_EPS = 1e-5
LRU_C = 8.0
N_DEV = 8
N_WBLK = 8
WBLK = 512

ADAM_LR = 0.001
ADAM_B1 = 0.9
ADAM_B2 = 0.999
ADAM_EPS = 1e-08
ADAM_WD = 0.01
ADAM_STEP = 10

LANES = 128
SUBLANES = 8
V7X_VMEM_BYTES = 64 * 1024 * 1024
VMEM_LIMIT = V7X_VMEM_BYTES - 8 * 1024 * 1024
MESH = pl.DeviceIdType.MESH
ANY = pl.BlockSpec(memory_space=pl.ANY)

TM_MM = 512
TM_BWD = 256
TM_LRU = 512
CB_LRU = 512
N_SEG = 8
SCAN_UNROLL = 4
SQRT_FLOOR = 1e-30
FLAT_ROWS = 16
ELEMENTWISE_TILE_BYTES = 1 << 20
POOL_TOK = 256
WIRE_DTYPE = BF16


def _cparams(**kw):
    return pltpu.CompilerParams(vmem_limit_bytes=VMEM_LIMIT, **kw)


def _my_pos():
    return lax.axis_index("x"), lax.axis_index("y"), lax.axis_index("c")


def _dot(a, b):
    return jnp.dot(a.astype(MXU_DTYPE), b.astype(MXU_DTYPE), preferred_element_type=F32)


def _dot_tn(a, b):
    return lax.dot_general(a.astype(MXU_DTYPE), b.astype(MXU_DTYPE), (((0,), (0,)), ((), ())),
                           preferred_element_type=F32)


def _dot_nt(a, b):
    return lax.dot_general(a.astype(MXU_DTYPE), b.astype(MXU_DTYPE), (((1,), (1,)), ((), ())),
                           preferred_element_type=F32)


def _sigmoid(z):
    return 0.5 * jnp.tanh(0.5 * z) + 0.5


def _one_minus_sq(la, a):
    x = -2.0 * la
    series = x * (1.0 - x * (0.5 - x * (1.0 / 6.0)))
    return jnp.where(x < 0.01, series, 1.0 - a * a)


def _log_sigmoid(x):
    y = jnp.exp(-jnp.abs(x))
    u = 1.0 + y
    l1p = jnp.where(u == 1.0, y, jnp.log(u) * (y / jnp.where(u == 1.0, 1.0, u - 1.0)))
    return jnp.minimum(x, 0.0) - l1p


def _rowsum(v):
    return jnp.sum(v, axis=0, keepdims=True)


def _layer_norm_stats(z):
    mu = jnp.mean(z, axis=-1, keepdims=True)
    zc = z - mu
    var = jnp.mean(zc * zc, axis=-1, keepdims=True)
    rstd = lax.rsqrt(var + LN_EPS)
    return zc * rstd, rstd


def _layer_norm_bwd(dy, xhat, rstd, g):
    dxh = dy * g
    m1 = jnp.mean(dxh, axis=-1, keepdims=True)
    m2 = jnp.mean(dxh * xhat, axis=-1, keepdims=True)
    return rstd * (dxh - m1 - xhat * m2)


def _shifted(v, before8, after8, offsets):
    n = v.shape[0]
    ext = jnp.concatenate([before8, v, after8], axis=0)
    total = n + 2 * SUBLANES
    return [pltpu.roll(ext, (-k) % total, 0)[SUBLANES:SUBLANES + n] for k in offsets]


def _rows8(row):
    return jnp.broadcast_to(row, (SUBLANES, row.shape[1]))


def _shift_down(v, first_row):
    return _shifted(v, _rows8(first_row), _rows8(first_row), [-1])[0]


def _shift_up(v, last_row):
    return _shifted(v, _rows8(last_row), _rows8(last_row), [1])[0]


def _all_gather(blocks, name):
    n = len(blocks)

    def body(*refs):
        x_refs, out_refs = refs[:n], refs[n:2 * n]
        send_sems, recv_sems, local_sems = refs[2 * n:]
        x, y, c = _my_pos()
        me, sibling = (x, y, c), (x, y, 1 - c)
        chips = [(1 - x, y), (x, 1 - y), (1 - x, 1 - y)]

        def slot(a, px, py, pc):
            return out_refs[a].at[4 * px + 2 * py + pc]

        def copy(a, k, block, to, src=None):
            return pltpu.make_async_remote_copy(
                src_ref=slot(a, *block) if src is None else src, dst_ref=slot(a, *block),
                send_sem=send_sems.at[a, k], recv_sem=recv_sems.at[a, k], device_id=to, device_id_type=MESH)

        mine = [pltpu.make_async_copy(x_refs[a], slot(a, *me), local_sems.at[a]) for a in range(n)]
        for cp in mine:
            cp.start()
        first = []
        for a in range(n):
            first.append(copy(a, 0, me, sibling, src=x_refs[a]))
            first += [copy(a, 1 + j, me, (*chip, c), src=x_refs[a]) for j, chip in enumerate(chips)]
        for cp in first:
            cp.start()
        passed = []
        for j, chip in enumerate(chips):
            for a in range(n):
                copy(a, 1 + j, (*chip, c), me).wait_recv()
                fwd = copy(a, 4 + j, (*chip, c), sibling)
                fwd.start()
                passed.append(fwd)
        for a in range(n):
            copy(a, 0, sibling, me).wait_recv()
            for j, chip in enumerate(chips):
                copy(a, 4 + j, (*chip, 1 - c), me).wait_recv()
        for cp in first + passed:
            cp.wait_send()
        for cp in mine:
            cp.wait()

    outs = pl.pallas_call(
        body, name=name,
        out_shape=[jax.ShapeDtypeStruct((N_DEV,) + b.shape, b.dtype) for b in blocks],
        in_specs=[ANY] * n, out_specs=[ANY] * n,
        scratch_shapes=[pltpu.SemaphoreType.DMA((n, 7)), pltpu.SemaphoreType.DMA((n, 7)),
                        pltpu.SemaphoreType.DMA((n,))],
    )(*blocks)
    return list(outs)


def _sibling_exchange(bufs, name):
    n = len(bufs)

    def body(*refs):
        srcs, outs = refs[:n], refs[n:2 * n]
        send_sems, recv_sems = refs[2 * n:]
        x, y, c = _my_pos()
        copies = [pltpu.make_async_remote_copy(
            src_ref=srcs[a].at[2 * j + (1 - c)], dst_ref=outs[a].at[j], send_sem=send_sems.at[a, j],
            recv_sem=recv_sems.at[a, j], device_id=(x, y, 1 - c), device_id_type=MESH)
            for a in range(n) for j in range(4)]
        for cp in copies:
            cp.start()
        for cp in copies:
            cp.wait()

    outs = pl.pallas_call(
        body, name=name, out_shape=[jax.ShapeDtypeStruct((4,) + b.shape[1:], b.dtype) for b in bufs],
        in_specs=[ANY] * n, out_specs=[ANY] * n,
        scratch_shapes=[pltpu.SemaphoreType.DMA((n, 4)), pltpu.SemaphoreType.DMA((n, 4))],
    )(*bufs)
    return list(outs)


def _chip_exchange(parts, name):
    n = len(parts)

    def body(*refs):
        srcs, outs = refs[:n], refs[n:2 * n]
        send_sems, recv_sems, local_sems = refs[2 * n:]
        x, y, c = _my_pos()
        jme = 2 * x + y
        peers = [(1 - x, y), (x, 1 - y), (1 - x, 1 - y)]
        local = [pltpu.make_async_copy(srcs[a].at[jme], outs[a].at[jme], local_sems.at[a]) for a in range(n)]
        for cp in local:
            cp.start()

        def copy(a, k, px, py, dst_slot):
            return pltpu.make_async_remote_copy(
                src_ref=srcs[a].at[2 * px + py], dst_ref=outs[a].at[dst_slot], send_sem=send_sems.at[a, k],
                recv_sem=recv_sems.at[a, k], device_id=(px, py, c), device_id_type=MESH)

        sends = [copy(a, k, px, py, jme) for a in range(n) for k, (px, py) in enumerate(peers)]
        for cp in sends:
            cp.start()
        for a in range(n):
            for k, (px, py) in enumerate(peers):
                copy(a, k, px, py, 2 * px + py).wait_recv()
        for cp in sends:
            cp.wait_send()
        for cp in local:
            cp.wait()

    outs = pl.pallas_call(
        body, name=name, out_shape=[jax.ShapeDtypeStruct(p.shape, p.dtype) for p in parts],
        in_specs=[ANY] * n, out_specs=[ANY] * n,
        scratch_shapes=[pltpu.SemaphoreType.DMA((n, 3)), pltpu.SemaphoreType.DMA((n, 3)),
                        pltpu.SemaphoreType.DMA((n,))],
    )(*parts)
    return list(outs)


def _row_tile(r, l):
    t = min(r, max(16, ELEMENTWISE_TILE_BYTES // (4 * l) // 16 * 16))
    while r % t:
        t -= 16
    return t


def _pair_sum(buf, recv, core, name):
    _, r, l = buf.shape
    tr = _row_tile(r, l)

    def body(core_ref, a_ref, b_ref, o_ref):
        o_ref[...] = (a_ref[...] + b_ref[...]).astype(WIRE_DTYPE)

    return pl.pallas_call(
        body, name=name, out_shape=jax.ShapeDtypeStruct((4, r, l), WIRE_DTYPE),
        grid_spec=pltpu.PrefetchScalarGridSpec(
            num_scalar_prefetch=1, grid=(4, r // tr),
            in_specs=[pl.BlockSpec((None, tr, l), lambda j, i, cr: (2 * j + cr[0], i, 0)),
                      pl.BlockSpec((None, tr, l), lambda j, i, cr: (j, i, 0))],
            out_specs=pl.BlockSpec((None, tr, l), lambda j, i, cr: (j, i, 0))),
        compiler_params=_cparams(dimension_semantics=("arbitrary", "arbitrary")),
    )(core, buf, recv)


def _sum_parts(p_ref):
    return ((p_ref[0].astype(F32) + p_ref[1].astype(F32)) + (p_ref[2].astype(F32) + p_ref[3].astype(F32)))


def _sum4(parts, name):
    _, r, l = parts.shape
    tr = _row_tile(r, l)

    def body(p_ref, o_ref):
        o_ref[...] = _sum_parts(p_ref)

    return pl.pallas_call(
        body, name=name, out_shape=jax.ShapeDtypeStruct((r, l), F32), grid=(r // tr,),
        in_specs=[pl.BlockSpec((4, tr, l), lambda i: (0, i, 0))],
        out_specs=pl.BlockSpec((tr, l), lambda i: (i, 0)),
        compiler_params=_cparams(dimension_semantics=("arbitrary",)),
    )(parts)


def _adamw_update(w, gg, m, v):
    nm = ADAM_B1 * m + (1.0 - ADAM_B1) * gg
    nv = ADAM_B2 * v + (1.0 - ADAM_B2) * (gg * gg)
    m_hat = nm / (1.0 - ADAM_B1 ** ADAM_STEP)
    v_hat = nv / (1.0 - ADAM_B2 ** ADAM_STEP)
    return -ADAM_LR * (m_hat / (jnp.sqrt(v_hat) + ADAM_EPS) + ADAM_WD * w), nm, nv


def _adamw(w, g, m, v, name):
    r, l = w.shape
    tr = _row_tile(r, l)

    def body(w_ref, g_ref, m_ref, v_ref, d_ref, nm_ref, nv_ref):
        d_ref[...], nm_ref[...], nv_ref[...] = _adamw_update(w_ref[...], g_ref[...], m_ref[...], v_ref[...])

    spec = pl.BlockSpec((tr, l), lambda i: (i, 0))
    return pl.pallas_call(
        body, name=name, out_shape=[jax.ShapeDtypeStruct((r, l), F32)] * 3, grid=(r // tr,),
        in_specs=[spec] * 4, out_specs=[spec] * 3,
        compiler_params=_cparams(dimension_semantics=("arbitrary",)),
    )(w, g, m, v)


def _adamw_parts(w, parts, m, v, name):
    nl, r, l = w.shape
    tr = _row_tile(r, l)

    def body(*refs):
        w_ref, p_refs, (m_ref, v_ref, g_ref, d_ref, nm_ref, nv_ref) = refs[0], refs[1:1 + nl], refs[1 + nl:]
        layer = pl.program_id(0)
        gg = _sum_parts(p_refs[0])
        for q in range(1, nl):
            gg = jnp.where(layer == q, _sum_parts(p_refs[q]), gg)
        g_ref[...] = gg
        d_ref[...], nm_ref[...], nv_ref[...] = _adamw_update(w_ref[...], gg, m_ref[...], v_ref[...])

    spec = pl.BlockSpec((None, tr, l), lambda q, i: (q, i, 0))
    pspecs = [pl.BlockSpec((4, tr, l), lambda q, i, k=k: (0, jnp.where(q == k, i, 0), 0)) for k in range(nl)]
    return pl.pallas_call(
        body, name=name, out_shape=[jax.ShapeDtypeStruct((nl, r, l), F32)] * 4, grid=(nl, r // tr),
        in_specs=[spec] + pspecs + [spec, spec], out_specs=[spec] * 4,
        compiler_params=_cparams(dimension_semantics=("arbitrary", "arbitrary")),
    )(w, *parts, m, v)


def _to_rows(pieces, row_multiple):
    flat = jnp.concatenate([p.reshape(-1) for p in pieces])
    rows = -(-flat.shape[0] // LANES)
    rows = -(-rows // row_multiple) * row_multiple
    flat = jnp.pad(flat, (0, rows * LANES - flat.shape[0]))
    return flat.reshape(rows, LANES)


def _split_rows(rows, shapes):
    flat = rows.reshape(-1)
    out, off = [], 0
    for s in shapes:
        n = int(np.prod(s))
        out.append(flat[off:off + n].reshape(s))
        off += n
    return out


def _mod_fwd(cond, w_mod, b_my, name):
    nl, _, ncol = w_mod.shape

    def body(a_ref, w_ref, b_ref, o_ref):
        a = a_ref[...]
        s = a * _sigmoid(a)
        for i in range(nl):
            o_ref[i] = _dot(s, w_ref[i]) + b_ref[i]

    return pl.pallas_call(
        body, name=name, out_shape=jax.ShapeDtypeStruct((nl, 16, ncol), F32),
        compiler_params=_cparams(),
    )(cond, w_mod, b_my)


def _mod_bwd(cond, dm_all, dm_my, w_mod, name):
    nl, _, ncol = w_mod.shape

    def body(a_ref, dma_ref, dmm_ref, w_ref, gw_ref, gb_ref, gc_ref):
        a = a_ref[...]
        sg = _sigmoid(a)
        s = a * sg
        for i in range(nl):
            gw_ref[i] = _dot_tn(s, dmm_ref[i])
            gb_ref[i] = jnp.sum(dma_ref[i], axis=0, keepdims=True)
        back = _dot_nt(dmm_ref[0], w_ref[0])
        dsilu = sg * (1.0 + a * (1.0 - sg))
        gc_ref[...] = jnp.sum(back[8:16] * dsilu[8:16], axis=0, keepdims=True)

    return pl.pallas_call(
        body, name=name,
        out_shape=[jax.ShapeDtypeStruct((nl, D_MODEL, ncol), F32), jax.ShapeDtypeStruct((nl, 1, 3 * D_MODEL), F32),
                   jax.ShapeDtypeStruct((1, D_MODEL), F32)],
        compiler_params=_cparams(),
    )(cond, dm_all, dm_my, w_mod)


def _in_proj(xt, sc, sh, wg, name):
    t = xt.shape[0]
    tm = min(TM_MM, t)

    def body(x_ref, sc_ref, sh_ref, w_ref, u_ref, g_ref):
        h = (x_ref[...] * (1.0 + sc_ref[...]) + sh_ref[...]).astype(MXU_DTYPE)
        for k in range(N_WBLK):
            o = jnp.dot(h, w_ref[k], preferred_element_type=F32)
            if k < N_WBLK // 2:
                u_ref[:, k * WBLK:(k + 1) * WBLK] = o
            else:
                kk = k - N_WBLK // 2
                g_ref[:, kk * WBLK:(kk + 1) * WBLK] = o

    row = pl.BlockSpec((1, D_MODEL), lambda i: (0, 0))
    return pl.pallas_call(
        body, name=name, out_shape=[jax.ShapeDtypeStruct((t, D_INNER), F32)] * 2, grid=(t // tm,),
        in_specs=[pl.BlockSpec((tm, D_MODEL), lambda i: (i, 0)), row, row,
                  pl.BlockSpec((N_WBLK, D_MODEL, WBLK), lambda i: (0, 0, 0), pipeline_mode=pl.Buffered(1))],
        out_specs=[pl.BlockSpec((tm, D_INNER), lambda i: (i, 0))] * 2,
        compiler_params=_cparams(dimension_semantics=("arbitrary",)),
    )(xt, sc, sh, wg)


def _halo_maps(nt, tm, n_rows8, pos):
    per = tm // SUBLANES
    prev = lambda cb, i: (jnp.maximum(pos(i) * per - 1, 0), cb)
    nxt = lambda cb, i: (jnp.minimum((pos(i) + 1) * per, n_rows8 - 1), cb)
    return prev, nxt


def _conv_taps(u, prev8, next8, is_first, is_last):
    pz = jnp.where(is_first, 0.0, 1.0)
    nz = jnp.where(is_last, 0.0, 1.0)
    return _shifted(u, prev8 * pz, next8 * nz, [-2, -1, 1])


def _lru_gates(uv, wa_ref, wx_ref, ba, bx, cl, g):
    sl = slice(g * LANES, (g + 1) * LANES)
    uvg = uv[:, sl]
    r = _sigmoid(_dot(uvg, wa_ref[g]) + ba[:, sl])
    ii = _sigmoid(_dot(uvg, wx_ref[g]) + bx[:, sl])
    la = cl[:, sl] * r
    a = jnp.exp(la)
    q = _one_minus_sq(la, a)
    rs = lax.rsqrt(jnp.maximum(q, SQRT_FLOOR))
    return uvg, r, ii, a, q * rs, rs


def _scan_tile(a_s, b_s, carry_ref, write_out, seg, reverse):
    n_g = a_s.shape[0]
    stride = a_s.shape[1] // N_SEG

    def step(k, state):
        t = (seg - 1 - k) if reverse else k
        hs, cs = state
        nh, nc = [], []
        for g in range(n_g):
            a = a_s[g, pl.ds(t, N_SEG, stride=stride), :]
            b = b_s[g, pl.ds(t, N_SEG, stride=stride), :]
            h = a * hs[g] + b
            cum = a * cs[g]
            b_s[g, pl.ds(t, N_SEG, stride=stride), :] = h
            a_s[g, pl.ds(t, N_SEG, stride=stride), :] = cum
            nh.append(h)
            nc.append(cum)
        return tuple(nh), tuple(nc)

    zeros = tuple(jnp.zeros((N_SEG, LANES), F32) for _ in range(n_g))
    ones = tuple(jnp.ones((N_SEG, LANES), F32) for _ in range(n_g))
    h_fin, a_fin = lax.fori_loop(0, seg, step, (zeros, ones), unroll=SCAN_UNROLL if seg % SCAN_UNROLL == 0 else 1)

    order = list(range(N_SEG - 1, -1, -1)) if reverse else list(range(N_SEG))
    for g in range(n_g):
        carry = carry_ref[:, g * LANES:(g + 1) * LANES]
        for j in order:
            rows = pl.ds(j * stride, seg)
            write_out(j, g, b_s[g, rows, :] + a_s[g, rows, :] * carry)
            carry = a_fin[g][j:j + 1] * carry + h_fin[g][j:j + 1]
        carry_ref[:, g * LANES:(g + 1) * LANES] = carry


def _lru_specs(s, tm, cb, direction_pos, nt):
    n_rows8 = s // SUBLANES
    prev, nxt = _halo_maps(nt, tm, n_rows8, direction_pos)
    tile = pl.BlockSpec((tm, cb), lambda c, i: (direction_pos(i), c))
    return tile, pl.BlockSpec((SUBLANES, cb), prev), pl.BlockSpec((SUBLANES, cb), nxt)


def _lru_param_specs(cb, d):
    n_g = cb // LANES
    vec = pl.BlockSpec((1, cb), lambda c, i: (0, c))
    dvec = pl.BlockSpec((None, 1, cb), lambda c, i: (d, 0, c))
    wmat = pl.BlockSpec((None, n_g, LRU_BLOCK, LRU_BLOCK), lambda c, i: (d, c, 0, 0))
    return vec, dvec, wmat


def _lru_fwd(src, h0, p, d, name, conv):
    s = src.shape[0]
    tm = min(TM_LRU, s)
    cb = CB_LRU
    n_g = cb // LANES
    nt = s // tm
    seg = tm // N_SEG
    stride = seg + SUBLANES
    pos = (lambda i: i) if d == 0 else (lambda i: nt - 1 - i)

    def body(*refs):
        refs = list(refs)
        u_ref = refs.pop(0)
        if conv:
            up_ref, un_ref, cw_ref, cbias_ref = [refs.pop(0) for _ in range(4)]
        wa_ref, wx_ref, ba_ref, bx_ref, lam_ref, h0_ref, h_ref, hc_ref = [refs.pop(0) for _ in range(8)]
        uv_ref = refs.pop(0) if conv else None
        a_s, b_s = refs
        i = pl.program_id(1)
        tp = pos(i)

        @pl.when(i == 0)
        def _():
            hc_ref[...] = h0_ref[...]

        uv = u_ref[...]
        if conv:
            um2, um1, up1 = _conv_taps(uv, up_ref[...], un_ref[...], tp == 0, tp == nt - 1)
            cw = cw_ref[...]
            uv = um2 * cw[0:1] + um1 * cw[1:2] + uv * cw[2:3] + up1 * cw[3:4] + cbias_ref[...]
            uv_ref[...] = uv
        cl = LRU_C * _log_sigmoid(lam_ref[...])
        ba, bx = ba_ref[...], bx_ref[...]
        for g in range(n_g):
            uvg, r, ii, a, sq, _ = _lru_gates(uv, wa_ref, wx_ref, ba, bx, cl, g)
            b = sq * (ii * uvg)
            for j in range(N_SEG):
                a_s[g, pl.ds(j * stride, seg), :] = a[j * seg:(j + 1) * seg]
                b_s[g, pl.ds(j * stride, seg), :] = b[j * seg:(j + 1) * seg]

        def write_out(j, g, h):
            h_ref[pl.ds(j * seg, seg), pl.ds(g * LANES, LANES)] = h

        _scan_tile(a_s, b_s, hc_ref, write_out, seg, reverse=(d == 1))

    tile, prev, nxt = _lru_specs(s, tm, cb, pos, nt)
    vec, dvec, wmat = _lru_param_specs(cb, d)
    wide = jax.ShapeDtypeStruct((s, D_INNER), F32)
    conv_specs = [prev, nxt, pl.BlockSpec((4, cb), lambda c, i: (0, c)), vec] if conv else []
    conv_args = [src, src, p["conv_w"], p["conv_b"]] if conv else []
    return pl.pallas_call(
        body, name=name,
        out_shape=[wide, jax.ShapeDtypeStruct((1, D_INNER), F32)] + ([wide] if conv else []),
        grid=(D_INNER // cb, nt),
        in_specs=[tile] + conv_specs + [wmat, wmat, dvec, dvec, dvec, vec],
        out_specs=[tile, vec] + ([tile] if conv else []),
        scratch_shapes=[pltpu.VMEM((n_g, N_SEG * stride, LANES), F32)] * 2,
        compiler_params=_cparams(dimension_semantics=("arbitrary", "arbitrary")),
    )(src, *conv_args, p["wa"], p["wx"], p["ba"], p["bx"], p["lam"], h0)


def _lru_bwd(uv, dh, h, h0, lam_in, p, d, name):
    s = uv.shape[0]
    tm = min(TM_LRU, s)
    cb = CB_LRU
    n_g = cb // LANES
    nt = s // tm
    seg = tm // N_SEG
    stride = seg + SUBLANES
    pos = (lambda i: nt - 1 - i) if d == 0 else (lambda i: i)

    def body(uv_ref, dh_ref, h_ref, hh_ref, wa_ref, wx_ref, ba_ref, bx_ref,
             lam_ref, h0_ref, lin_ref, duv_ref, gwa_ref, gwx_ref, gv_ref, lc_ref, a_s, b_s, lp_s,
             r_s, i_s, q_s, rq_s, a_keep):
        i = pl.program_id(1)
        tp = pos(i)

        @pl.when(i == 0)
        def _():
            lc_ref[...] = lin_ref[...]
            gwa_ref[...] = jnp.zeros_like(gwa_ref)
            gwx_ref[...] = jnp.zeros_like(gwx_ref)
            gv_ref[...] = jnp.zeros_like(gv_ref)

        uv = uv_ref[...]
        lam = lam_ref[...]
        cl = LRU_C * _log_sigmoid(lam)
        ba, bx = ba_ref[...], bx_ref[...]
        dh_t = dh_ref[...]
        carry_in = lc_ref[...]
        for g in range(n_g):
            sl = slice(g * LANES, (g + 1) * LANES)
            _, r, ii, a, sq, rs = _lru_gates(uv, wa_ref, wx_ref, ba, bx, cl, g)
            r_s[:, sl], i_s[:, sl], q_s[:, sl], rq_s[:, sl], a_keep[:, sl] = r, ii, sq, rs, a
            b = a * dh_t[:, sl]
            for j in range(N_SEG):
                a_s[g, pl.ds(j * stride, seg), :] = a[j * seg:(j + 1) * seg]
                b_s[g, pl.ds(j * stride, seg), :] = b[j * seg:(j + 1) * seg]

        def write_out(j, g, v):
            lp_s[pl.ds(j * seg, seg), pl.ds(g * LANES, LANES)] = v

        _scan_tile(a_s, b_s, lc_ref, write_out, seg, reverse=(d == 0))

        h_t = h_ref[...]
        hh = hh_ref[...]
        if d == 0:
            edge = jnp.where(tp == 0, h0_ref[...], hh[7:8])
            h_prev = _shift_down(h_t, edge)
            lam_t = dh_t + _shift_up(lp_s[...], carry_in)
        else:
            edge = jnp.where(tp == nt - 1, h0_ref[...], hh[0:1])
            h_prev = _shift_up(h_t, edge)
            lam_t = dh_t + _shift_down(lp_s[...], carry_in)

        dsig = LRU_C * _sigmoid(-lam)
        for g in range(n_g):
            sl = slice(g * LANES, (g + 1) * LANES)
            uvg, r, ii, a, sq = uv[:, sl], r_s[:, sl], i_s[:, sl], a_keep[:, sl], q_s[:, sl]
            lt = lam_t[:, sl]
            ls = lt * sq
            dla = (lt * a) * (h_prev[:, sl] - (ii * uvg) * (a * rq_s[:, sl]))
            dzr = (dla * cl[:, sl]) * r * (1.0 - r)
            dzi = (ls * uvg) * ii * (1.0 - ii)
            duv_ref[:, sl] = ls * ii + _dot_nt(dzr, wa_ref[g]) + _dot_nt(dzi, wx_ref[g])
            gwa_ref[g] += _dot_tn(uvg, dzr)
            gwx_ref[g] += _dot_tn(uvg, dzi)
            gv_ref[0:1, sl] += _rowsum(dzr)
            gv_ref[1:2, sl] += _rowsum(dzi)
            gv_ref[2:3, sl] += _rowsum(dla * r) * dsig[:, sl]

    tile, prev, nxt = _lru_specs(s, tm, cb, pos, nt)
    vec, dvec, wmat = _lru_param_specs(cb, d)
    hh_spec = prev if d == 0 else nxt
    gw_spec = pl.BlockSpec((n_g, LRU_BLOCK, LRU_BLOCK), lambda c, i: (c, 0, 0))
    n_blk = D_INNER // LRU_BLOCK
    return pl.pallas_call(
        body, name=name,
        out_shape=[jax.ShapeDtypeStruct((s, D_INNER), F32),
                   jax.ShapeDtypeStruct((n_blk, LRU_BLOCK, LRU_BLOCK), F32),
                   jax.ShapeDtypeStruct((n_blk, LRU_BLOCK, LRU_BLOCK), F32),
                   jax.ShapeDtypeStruct((SUBLANES, D_INNER), F32),
                   jax.ShapeDtypeStruct((1, D_INNER), F32)],
        grid=(D_INNER // cb, nt),
        in_specs=[tile, tile, tile, hh_spec, wmat, wmat, dvec, dvec, dvec, vec, vec],
        out_specs=[tile, gw_spec, gw_spec, pl.BlockSpec((SUBLANES, cb), lambda c, i: (0, c)), vec],
        scratch_shapes=[pltpu.VMEM((n_g, N_SEG * stride, LANES), F32)] * 2 + [pltpu.VMEM((tm, cb), F32)] * 6,
        compiler_params=_cparams(dimension_semantics=("arbitrary", "arbitrary")),
    )(uv, dh, h, h, p["wa"], p["wx"], p["ba"], p["bx"], p["lam"], h0, lam_in)


def _out0(hf, hb, g, xt, gt, wo, lg, lb, name):
    t = xt.shape[0]
    tm = min(TM_MM, t)

    def body(hf_ref, hb_ref, g_ref, x_ref, gt_ref, w_ref, lg_ref, lb_ref, x1_ref, br_ref):
        gg = g_ref[...]
        p = (hf_ref[...] + hb_ref[...]) * (gg * _sigmoid(gg))
        br = _dot(p, w_ref[...])
        z = ALPHA * x_ref[...] + gt_ref[...] * br
        xhat, _ = _layer_norm_stats(z)
        x1_ref[...] = xhat * lg_ref[...] + lb_ref[...]
        br_ref[...] = br

    wide = pl.BlockSpec((tm, D_INNER), lambda i: (i, 0))
    nar = pl.BlockSpec((tm, D_MODEL), lambda i: (i, 0))
    row = pl.BlockSpec((1, D_MODEL), lambda i: (0, 0))
    return pl.pallas_call(
        body, name=name, out_shape=[jax.ShapeDtypeStruct((t, D_MODEL), F32)] * 2, grid=(t // tm,),
        in_specs=[wide, wide, wide, nar, row,
                  pl.BlockSpec((D_INNER, D_MODEL), lambda i: (0, 0), pipeline_mode=pl.Buffered(1)), row, row],
        out_specs=[nar, nar],
        compiler_params=_cparams(dimension_semantics=("arbitrary",)),
    )(hf, hb, g, xt, gt, wo, lg, lb)


def _window(n, w):
    t = np.arange(n)
    return np.clip(t - w // 2, 0, n), np.clip(t + w // 2, 0, n)


def _pool_tables(n_rows, transpose):
    boxes, inv_c, inv_r = [], [], []
    for w in POOL_WINDOWS:
        lo, hi = _window(GRID_W, w)
        m = np.zeros((GRID_W, GRID_W), np.float32)
        for r in range(GRID_W):
            m[r, lo[r]:hi[r]] = 1.0
        m = np.kron(np.eye(POOL_TOK // GRID_W, dtype=np.float32), m)
        boxes.append(m.T if transpose else m)
        inv_c.append(np.broadcast_to((1.0 / (hi - lo).astype(np.float32))[:, None], (GRID_W, LANES)))
        lo_r, hi_r = _window(n_rows, w)
        inv_r.append(1.0 / (hi_r - lo_r).astype(np.float32))
    return (jnp.asarray(np.stack(boxes), MXU_DTYPE), jnp.asarray(np.stack(inv_c), F32),
            jnp.asarray(np.stack(inv_r), F32))


def _pool_mix(xin, transpose, out_dtype, name):
    s = xin.shape[0]
    n_rows = s // GRID_W
    pad_t = SUBLANES * GRID_W
    rows_per_blk = POOL_TOK // GRID_W
    n_slab = D_INNER // LANES
    slabs_per_group = POOL_GROUP // LANES
    n_win = len(POOL_WINDOWS)
    boxes, inv_c, inv_r = _pool_tables(n_rows, transpose)

    def body(invr_ref, box_ref, invc_ref, x_ref, o_ref, pad_s):
        k = pl.program_id(0) // slabs_per_group
        pad_s[pl.ds(0, pad_t), :] = jnp.zeros((pad_t, LANES), F32)
        pad_s[pl.ds(pad_t + s, pad_t), :] = jnp.zeros((pad_t, LANES), F32)

        for kk, w in enumerate(POOL_WINDOWS):
            half = w // 2
            offsets = list(range(-(half - 1), half + 1)) if transpose else list(range(-half, half))

            @pl.when(k == kk)
            def _():
                inv_col = invc_ref[kk]

                def col_box(b, carry):
                    st = pl.multiple_of(b * POOL_TOK, POOL_TOK)
                    xb = x_ref[pl.ds(st, POOL_TOK), :]
                    if transpose:
                        xb = xb * jnp.concatenate(
                            [inv_col * invr_ref[kk, b * rows_per_blk + q] for q in range(rows_per_blk)], axis=0)
                    hi = xb.astype(MXU_DTYPE)
                    lo = (xb - hi.astype(F32)).astype(MXU_DTYPE)
                    both = jnp.dot(box_ref[kk], jnp.concatenate([hi, lo], axis=1), preferred_element_type=F32)
                    pad_s[pl.ds(pad_t + st, POOL_TOK), :] = both[:, :LANES] + both[:, LANES:]
                    return carry
                lax.fori_loop(0, s // POOL_TOK, col_box, 0)

                def row_box(r, carry):
                    st = pl.multiple_of(r * GRID_W, GRID_W)
                    acc = pad_s[pl.ds(pad_t + st + offsets[0] * GRID_W, GRID_W), :]
                    for o in offsets[1:]:
                        acc = acc + pad_s[pl.ds(pad_t + st + o * GRID_W, GRID_W), :]
                    if not transpose:
                        acc = acc * (inv_col * invr_ref[kk, r])
                    o_ref[pl.ds(st, GRID_W), :] = (acc - x_ref[pl.ds(st, GRID_W), :]).astype(out_dtype)
                    return carry
                lax.fori_loop(0, n_rows, row_box, 0)

    slab = pl.BlockSpec((s, LANES), lambda i: (0, i))
    return pl.pallas_call(
        body, name=name, out_shape=jax.ShapeDtypeStruct((s, D_INNER), out_dtype), grid=(n_slab,),
        in_specs=[pl.BlockSpec(memory_space=pltpu.SMEM),
                  pl.BlockSpec((n_win, POOL_TOK, POOL_TOK), lambda i: (0, 0, 0)),
                  pl.BlockSpec((n_win, GRID_W, LANES), lambda i: (0, 0, 0)), slab],
        out_specs=slab,
        scratch_shapes=[pltpu.VMEM((s + 2 * pad_t, LANES), F32)],
        compiler_params=_cparams(dimension_semantics=("arbitrary",)),
    )(inv_r, boxes, inv_c, xin)


def _out1(dmix, pw, ps, g, x1, gt, wo, lg, lb, tgt, name):
    t = x1.shape[0]
    tm = min(TM_MM, t)
    n_grp = len(POOL_WINDOWS)

    def body(d_ref, pw_ref, ps_ref, g_ref, x1_ref, gt_ref, w_ref, lg_ref, lb_ref, tgt_ref, dz_ref, st_ref):
        @pl.when(pl.program_id(0) == 0)
        def _():
            st_ref[...] = jnp.zeros_like(st_ref)

        br = jnp.zeros((tm, D_MODEL), F32)
        for k in range(n_grp):
            sl = slice(k * POOL_GROUP, (k + 1) * POOL_GROUP)
            y = jnp.dot(d_ref[:, sl], pw_ref[k], preferred_element_type=F32) * ps_ref[:, sl]
            gg = g_ref[:, sl]
            br = br + _dot(y * (gg * _sigmoid(gg)), w_ref[sl, :])
        z = ALPHA * x1_ref[...] + gt_ref[...] * br
        xhat, rstd = _layer_norm_stats(z)
        lg_v = lg_ref[...]
        err = xhat * lg_v + lb_ref[...] - tgt_ref[...]
        dy = err * (1.0 / D_MODEL)
        dz = _layer_norm_bwd(dy, xhat, rstd, lg_v)
        dz_ref[...] = dz
        st_ref[0:1, :] += _rowsum(dy * xhat)
        st_ref[1:2, :] += _rowsum(dy)
        st_ref[2:3, :] += _rowsum(dz * br)
        st_ref[3:4, :] += _rowsum(err * err)

    wide = pl.BlockSpec((tm, D_INNER), lambda i: (i, 0))
    nar = pl.BlockSpec((tm, D_MODEL), lambda i: (i, 0))
    row = pl.BlockSpec((1, D_MODEL), lambda i: (0, 0))
    return pl.pallas_call(
        body, name=name,
        out_shape=[jax.ShapeDtypeStruct((t, D_MODEL), F32), jax.ShapeDtypeStruct((SUBLANES, D_MODEL), F32)],
        grid=(t // tm,),
        in_specs=[wide, pl.BlockSpec((n_grp, POOL_GROUP, POOL_GROUP), lambda i: (0, 0, 0)),
                  pl.BlockSpec((1, D_INNER), lambda i: (0, 0)), wide, nar, row,
                  pl.BlockSpec((D_INNER, D_MODEL), lambda i: (0, 0), pipeline_mode=pl.Buffered(1)), row, row, nar],
        out_specs=[nar, pl.BlockSpec((SUBLANES, D_MODEL), lambda i: (0, 0))],
        compiler_params=_cparams(dimension_semantics=("arbitrary",)),
    )(dmix, pw, ps, g, x1, gt, wo, lg, lb, tgt)


def _flush(acc, out_hbm, sem):
    cp = pltpu.make_async_copy(acc, out_hbm, sem)
    cp.start()
    cp.wait()


def _bout1(dz, dmix, g, pw, ps, gt, wo, name):
    t = dz.shape[0]
    tm = min(TM_BWD, t)
    nt = t // tm
    n_grp = len(POOL_WINDOWS)

    def body(dz_ref, d_ref, g_ref, pw_ref, ps_ref, gt_ref, w_ref, dd_ref, dg_ref, gwo_hbm, gpw_hbm, gps_ref,
             gwo_acc, gpw_acc, sems):
        i = pl.program_id(0)

        @pl.when(i == 0)
        def _():
            gwo_acc[...] = jnp.zeros_like(gwo_acc)
            gpw_acc[...] = jnp.zeros_like(gpw_acc)
            gps_ref[...] = jnp.zeros_like(gps_ref)

        db = (gt_ref[...] * dz_ref[...]).astype(MXU_DTYPE)
        for k in range(n_grp):
            sl = slice(k * POOL_GROUP, (k + 1) * POOL_GROUP)
            dk = d_ref[:, sl]
            po = jnp.dot(dk, pw_ref[k], preferred_element_type=F32)
            psk = ps_ref[:, sl]
            y = po * psk
            gg = g_ref[:, sl]
            sg = _sigmoid(gg)
            silu = gg * sg
            gwo_acc[sl, :] += _dot_tn(y * silu, db)
            dp = _dot_nt(db, w_ref[sl, :])
            dy = dp * silu
            dg_ref[:, sl] = (dp * y * (sg * (1.0 + gg * (1.0 - sg)))).astype(MXU_DTYPE)
            gps_ref[0:1, sl] += _rowsum(dy * po)
            dpo = (dy * psk).astype(MXU_DTYPE)
            gpw_acc[k] += _dot_tn(dk, dpo)
            dd_ref[:, sl] = _dot_nt(dpo, pw_ref[k])

        @pl.when(i == nt - 1)
        def _():
            _flush(gwo_acc, gwo_hbm, sems.at[0])
            _flush(gpw_acc, gpw_hbm, sems.at[1])

    wide = pl.BlockSpec((tm, D_INNER), lambda i: (i, 0))
    nar = pl.BlockSpec((tm, D_MODEL), lambda i: (i, 0))
    return pl.pallas_call(
        body, name=name,
        out_shape=[jax.ShapeDtypeStruct((t, D_INNER), F32), jax.ShapeDtypeStruct((t, D_INNER), MXU_DTYPE),
                   jax.ShapeDtypeStruct((D_INNER, D_MODEL), F32),
                   jax.ShapeDtypeStruct((n_grp, POOL_GROUP, POOL_GROUP), F32),
                   jax.ShapeDtypeStruct((SUBLANES, D_INNER), F32)],
        grid=(nt,),
        in_specs=[nar, wide, wide, pl.BlockSpec((n_grp, POOL_GROUP, POOL_GROUP), lambda i: (0, 0, 0)),
                  pl.BlockSpec((1, D_INNER), lambda i: (0, 0)), pl.BlockSpec((1, D_MODEL), lambda i: (0, 0)),
                  pl.BlockSpec((D_INNER, D_MODEL), lambda i: (0, 0), pipeline_mode=pl.Buffered(1))],
        out_specs=[wide, wide, ANY, ANY, pl.BlockSpec((SUBLANES, D_INNER), lambda i: (0, 0))],
        scratch_shapes=[pltpu.VMEM((D_INNER, D_MODEL), F32), pltpu.VMEM((n_grp, POOL_GROUP, POOL_GROUP), F32),
                        pltpu.SemaphoreType.DMA((2,))],
        compiler_params=_cparams(dimension_semantics=("arbitrary",)),
    )(dz, dmix, g, pw, ps, gt, wo)


def _bout0(dx1, xt, br0, lg, hf, hb, g, gt, wo, name):
    t = dx1.shape[0]
    tm = min(TM_BWD, t)
    nt = t // tm

    def body(dx_ref, x_ref, br_ref, lg_ref, hf_ref, hb_ref, g_ref, gt_ref, w_ref,
             dz_ref, dy_ref, dg_ref, gwo_hbm, st_ref, gwo_acc, sem):
        i = pl.program_id(0)

        @pl.when(i == 0)
        def _():
            gwo_acc[...] = jnp.zeros_like(gwo_acc)
            st_ref[...] = jnp.zeros_like(st_ref)

        dx = dx_ref[...]
        br = br_ref[...]
        gate = gt_ref[...]
        xhat, rstd = _layer_norm_stats(ALPHA * x_ref[...] + gate * br)
        dz = _layer_norm_bwd(dx, xhat, rstd, lg_ref[...])
        dz_ref[...] = dz
        st_ref[0:1, :] += _rowsum(dx * xhat)
        st_ref[1:2, :] += _rowsum(dx)
        st_ref[2:3, :] += _rowsum(dz * br)
        db = (gate * dz).astype(MXU_DTYPE)
        y = hf_ref[...] + hb_ref[...]
        gg = g_ref[...]
        sg = _sigmoid(gg)
        silu = gg * sg
        gwo_acc[...] += _dot_tn(y * silu, db)
        dp = _dot_nt(db, w_ref[...])
        dy_ref[...] = dp * silu
        dg_ref[...] = (dp * y * (sg * (1.0 + gg * (1.0 - sg)))).astype(MXU_DTYPE)

        @pl.when(i == nt - 1)
        def _():
            _flush(gwo_acc, gwo_hbm, sem)

    wide = pl.BlockSpec((tm, D_INNER), lambda i: (i, 0))
    nar = pl.BlockSpec((tm, D_MODEL), lambda i: (i, 0))
    row = pl.BlockSpec((1, D_MODEL), lambda i: (0, 0))
    return pl.pallas_call(
        body, name=name,
        out_shape=[jax.ShapeDtypeStruct((t, D_MODEL), F32), jax.ShapeDtypeStruct((t, D_INNER), F32),
                   jax.ShapeDtypeStruct((t, D_INNER), MXU_DTYPE), jax.ShapeDtypeStruct((D_INNER, D_MODEL), F32),
                   jax.ShapeDtypeStruct((SUBLANES, D_MODEL), F32)],
        grid=(nt,),
        in_specs=[nar, nar, nar, row, wide, wide, wide, row,
                  pl.BlockSpec((D_INNER, D_MODEL), lambda i: (0, 0), pipeline_mode=pl.Buffered(1))],
        out_specs=[nar, wide, wide, ANY, pl.BlockSpec((SUBLANES, D_MODEL), lambda i: (0, 0))],
        scratch_shapes=[pltpu.VMEM((D_INNER, D_MODEL), F32), pltpu.SemaphoreType.DMA(())],
        compiler_params=_cparams(dimension_semantics=("arbitrary",)),
    )(dx1, xt, br0, lg, hf, hb, g, gt, wo)


def _conv_bwd(duvf, duvb, u, conv_w, name):
    s = u.shape[0]
    tm = min(TM_LRU, s)
    cb = CB_LRU
    nt = s // tm

    def body(df_ref, dfp_ref, dfn_ref, db_ref, dbp_ref, dbn_ref, u_ref, cw_ref, du_ref, cst_ref):
        i = pl.program_id(1)

        @pl.when(i == 0)
        def _():
            cst_ref[...] = jnp.zeros_like(cst_ref)

        first, last = i == 0, i == nt - 1
        pz = jnp.where(first, 0.0, 1.0)
        nz = jnp.where(last, 0.0, 1.0)
        dout = df_ref[...] + db_ref[...]
        dm1, dp1, dp2 = _shifted(dout, (dfp_ref[...] + dbp_ref[...]) * pz, (dfn_ref[...] + dbn_ref[...]) * nz,
                                 [-1, 1, 2])
        cw = cw_ref[...]
        du_ref[...] = (dp2 * cw[0:1] + dp1 * cw[1:2] + dout * cw[2:3] + dm1 * cw[3:4]).astype(MXU_DTYPE)
        u_t = u_ref[...]
        cst_ref[0:1, :] += _rowsum(dp2 * u_t)
        cst_ref[1:2, :] += _rowsum(dp1 * u_t)
        cst_ref[2:3, :] += _rowsum(dout * u_t)
        cst_ref[3:4, :] += _rowsum(dm1 * u_t)
        cst_ref[4:5, :] += _rowsum(dout)

    tile, prev, nxt = _lru_specs(s, tm, cb, lambda i: i, nt)
    return pl.pallas_call(
        body, name=name,
        out_shape=[jax.ShapeDtypeStruct((s, D_INNER), MXU_DTYPE), jax.ShapeDtypeStruct((SUBLANES, D_INNER), F32)],
        grid=(D_INNER // cb, nt),
        in_specs=[tile, prev, nxt] * 2 + [tile, pl.BlockSpec((4, cb), lambda c, i: (0, c))],
        out_specs=[tile, pl.BlockSpec((SUBLANES, cb), lambda c, i: (0, c))],
        compiler_params=_cparams(dimension_semantics=("arbitrary", "arbitrary")),
    )(duvf, duvf, duvf, duvb, duvb, duvb, u, conv_w)


def _bin(du, dg, xin, dzin, sc, sh, wg, name):
    t = xin.shape[0]
    tm = min(TM_MM, t)
    nt = t // tm
    has_g, has_dx = dg is not None, dzin is not None
    half = N_WBLK // 2
    n_blk = N_WBLK if has_g else half

    def body(*refs):
        refs = list(refs)
        du_ref = refs.pop(0)
        dg_ref = refs.pop(0) if has_g else None
        x_ref = refs.pop(0)
        dz_ref = refs.pop(0) if has_dx else None
        sc_ref, sh_ref, w_ref = refs.pop(0), refs.pop(0), refs.pop(0)
        dx_ref = refs.pop(0) if has_dx else None
        gw_hbm, st_ref, gw_acc, sem = refs
        i = pl.program_id(0)

        @pl.when(i == 0)
        def _():
            gw_acc[...] = jnp.zeros_like(gw_acc)
            st_ref[...] = jnp.zeros_like(st_ref)

        xv = x_ref[...]
        scale = 1.0 + sc_ref[...]
        h = (xv * scale + sh_ref[...]).astype(MXU_DTYPE)
        dh = None
        for k in range(n_blk):
            src = du_ref if k < half else dg_ref
            kk = k % half
            dk = src[:, kk * WBLK:(kk + 1) * WBLK]
            gw_acc[k] += _dot_tn(h, dk)
            contrib = _dot_nt(dk, w_ref[k])
            dh = contrib if dh is None else dh + contrib
        st_ref[0:1, :] += _rowsum(dh * xv)
        st_ref[1:2, :] += _rowsum(dh)
        if has_dx:
            dx_ref[...] = ALPHA * dz_ref[...] + dh * scale

        @pl.when(i == nt - 1)
        def _():
            _flush(gw_acc, gw_hbm, sem)

    wide = pl.BlockSpec((tm, D_INNER), lambda i: (i, 0))
    nar = pl.BlockSpec((tm, D_MODEL), lambda i: (i, 0))
    row = pl.BlockSpec((1, D_MODEL), lambda i: (0, 0))
    wspec = pl.BlockSpec((n_blk, D_MODEL, WBLK), lambda i: (0, 0, 0), pipeline_mode=pl.Buffered(1))
    in_specs = [wide] + ([wide] if has_g else []) + [nar] + ([nar] if has_dx else []) + [row, row, wspec]
    args = [du] + ([dg] if has_g else []) + [xin] + ([dzin] if has_dx else []) + [sc, sh, wg]
    out_shape = ([jax.ShapeDtypeStruct((t, D_MODEL), F32)] if has_dx else []) + [
        jax.ShapeDtypeStruct((n_blk, D_MODEL, WBLK), F32), jax.ShapeDtypeStruct((SUBLANES, D_MODEL), F32)]
    out_specs = ([nar] if has_dx else []) + [ANY, pl.BlockSpec((SUBLANES, D_MODEL), lambda i: (0, 0))]
    return pl.pallas_call(
        body, name=name, out_shape=out_shape, grid=(nt,), in_specs=in_specs, out_specs=out_specs,
        scratch_shapes=[pltpu.VMEM((n_blk, D_MODEL, WBLK), F32), pltpu.SemaphoreType.DMA(())],
        compiler_params=_cparams(dimension_semantics=("arbitrary",)),
    )(*args)


def _blocks_by_device(a, axis):
    shape = a.shape
    a = a.reshape(shape[:axis] + (N_DEV, shape[axis] // N_DEV) + shape[axis + 1:])
    return jnp.moveaxis(a, axis, 0)


def kernel(x, c, ctx, c_ctx, w_mod, b_mod, w_in, w_out, ln_g, ln_b, conv_w, conv_b, lru_wa, lru_ba, lru_wx, lru_bx, lru_lam, pool_w, pool_scale, loss_target, m_c_ctx, m_w_mod, m_b_mod, m_w_in, m_w_out, m_ln_g, m_ln_b, m_conv_w, m_conv_b, m_lru_wa, m_lru_ba, m_lru_wx, m_lru_bx, m_lru_lam, m_pool_w, m_pool_scale, v_c_ctx, v_w_mod, v_b_mod, v_w_in, v_w_out, v_ln_g, v_ln_b, v_conv_w, v_conv_b, v_lru_wa, v_lru_ba, v_lru_wx, v_lru_bx, v_lru_lam, v_pool_w, v_pool_scale):
    xi, yi, ci = _my_pos()
    dev = 4 * xi + 2 * yi + ci
    xt, ctxt, tgt = x[0], ctx[0], loss_target[0]
    n_mod = w_mod.shape[2]

    small_shapes = [(D_MODEL,), conv_w.shape[1:], lru_ba.shape[1:], lru_bx.shape[1:], lru_lam.shape[1:],
                    pool_scale.shape[1:]]
    small = _to_rows([c[0], conv_w[0], lru_ba[0], lru_bx[0], lru_lam[0], pool_scale[0]], SUBLANES)
    small_all, = _all_gather([small], "gather_small")
    pieces = [_split_rows(small_all[k], small_shapes) for k in range(N_DEV)]
    c_all = jnp.stack([p[0] for p in pieces])
    conv_w_f = jnp.concatenate([p[1] for p in pieces], axis=-1)
    lru_ba_f = jnp.concatenate([p[2] for p in pieces], axis=-1)[:, None, :]
    lru_bx_f = jnp.concatenate([p[3] for p in pieces], axis=-1)[:, None, :]
    lru_lam_f = jnp.concatenate([p[4] for p in pieces], axis=-1)[:, None, :]
    pool_scale_f = jnp.concatenate([p[5] for p in pieces], axis=-1)[None, :]

    cond = jnp.concatenate([c_all, jnp.broadcast_to(c_ctx[None, :], (N_DEV, D_MODEL))], axis=0)
    b_my = lax.dynamic_slice(b_mod, (0, dev * n_mod), (2, n_mod))[:, None, :]
    mod_part = _mod_fwd(cond, w_mod, b_my, "mod_fwd")
    mod_all, = _all_gather([mod_part], "gather_mod")
    mod = jnp.transpose(mod_all, (1, 2, 0, 3)).reshape(2, 16, 3 * D_MODEL)
    mod_me = lax.dynamic_slice(mod, (0, dev, 0), (2, 1, 3 * D_MODEL))
    sh = [mod_me[i, :, 0:D_MODEL] for i in range(2)]
    sc = [mod_me[i, :, D_MODEL:2 * D_MODEL] for i in range(2)]
    gt = [mod_me[i, :, 2 * D_MODEL:] for i in range(2)]
    shc, scc = mod[0, 8:9, 0:D_MODEL], mod[0, 8:9, D_MODEL:2 * D_MODEL]

    wi0, wi1, wo0, wo1, pool_w_g = _all_gather(
        [w_in[0].astype(MXU_DTYPE), w_in[1].astype(MXU_DTYPE), w_out[0].astype(MXU_DTYPE),
         w_out[1].astype(MXU_DTYPE), pool_w[0].astype(MXU_DTYPE)], "gather_weights")
    w_in_l = [wi0, wi1]
    w_out_l = [wo0.reshape(D_INNER, D_MODEL), wo1.reshape(D_INNER, D_MODEL)]
    pool_w_f = jnp.transpose(pool_w_g, (1, 0, 2, 3)).reshape(len(POOL_WINDOWS), POOL_GROUP, POOL_GROUP)
    lg = [ln_g[i][None, :] for i in range(2)]
    lb = [ln_b[i][None, :] for i in range(2)]
    lru_p = dict(conv_w=conv_w_f, conv_b=conv_b, wa=lru_wa[0].astype(MXU_DTYPE), wx=lru_wx[0].astype(MXU_DTYPE),
                 ba=lru_ba_f, bx=lru_bx_f, lam=lru_lam_f)
    zero_state = jnp.zeros((1, D_INNER), F32)

    u0, g0 = _in_proj(xt, sc[0], sh[0], w_in_l[0], "in_proj0")
    uc, _ = _in_proj(ctxt, scc, shc, w_in_l[0], "in_proj0_ctx")
    hcf, cf, uvc = _lru_fwd(uc, zero_state, lru_p, 0, "lru_fwd_ctx_f", conv=True)
    hcb, cbk = _lru_fwd(uvc, zero_state, lru_p, 1, "lru_fwd_ctx_b", conv=False)
    hf, _, uv0 = _lru_fwd(u0, cf, lru_p, 0, "lru_fwd_f", conv=True)
    hb, _ = _lru_fwd(uv0, cbk, lru_p, 1, "lru_fwd_b", conv=False)
    x1, br0 = _out0(hf, hb, g0, xt, gt[0], w_out_l[0], lg[0], lb[0], "out0")
    u1, g1 = _in_proj(x1, sc[1], sh[1], w_in_l[1], "in_proj1")
    dmix = _pool_mix(u1, False, MXU_DTYPE, "pool_fwd")
    dz1, st1 = _out1(dmix, pool_w_f, pool_scale_f, g1, x1, gt[1], w_out_l[1], lg[1], lb[1], tgt, "out1")
    loss = lax.psum((0.5 / D_MODEL) * jnp.sum(st1[3]), ("x", "y", "c"))

    dd, dg1, gwo1, gpw, gps = _bout1(dz1, dmix, g1, pool_w_f, pool_scale_f, gt[1], w_out_l[1], "bwd_out1")
    du1 = _pool_mix(dd, True, MXU_DTYPE, "pool_bwd")
    dx1, gwi1, stb1 = _bin(du1, dg1, x1, dz1, sc[1], sh[1], w_in_l[1], "bwd_in1")
    dz0, dy0, dg0, gwo0, stl0 = _bout0(dx1, xt, br0, lg[0], hf, hb, g0, gt[0], w_out_l[0], "bwd_out0")
    duvf, gwa_f, gwx_f, gv_f, dh0f = _lru_bwd(uv0, dy0, hf, cf, zero_state, lru_p, 0, "lru_bwd_f")
    duvb, gwa_b, gwx_b, gv_b, dh0b = _lru_bwd(uv0, dy0, hb, cbk, zero_state, lru_p, 1, "lru_bwd_b")
    zero_dh = jnp.zeros_like(uc)
    ducf, gwa_cf, gwx_cf, gv_cf, _ = _lru_bwd(uvc, zero_dh, hcf, zero_state, dh0f, lru_p, 0, "lru_bwd_ctx_f")
    ducb, gwa_cb, gwx_cb, gv_cb, _ = _lru_bwd(uvc, zero_dh, hcb, zero_state, dh0b, lru_p, 1, "lru_bwd_ctx_b")
    du0, cst0 = _conv_bwd(duvf, duvb, u0, conv_w_f, "conv_bwd")
    duc, cstc = _conv_bwd(ducf, ducb, uc, conv_w_f, "conv_bwd_ctx")
    gx, gwi0, stb0 = _bin(du0, dg0, xt, dz0, sc[0], sh[0], w_in_l[0], "bwd_in0")
    gwic, stc = _bin(duc, None, ctxt, None, scc, shc, w_in_l[0][:N_WBLK // 2], "bwd_in0_ctx")
    gwi0 = gwi0.at[:N_WBLK // 2].add(gwic)

    zero_row = jnp.zeros((1, D_MODEL), F32)
    dm_me = jnp.stack([
        jnp.concatenate([jnp.concatenate([stb0[1:2], stb0[0:1], stl0[2:3]], axis=1),
                         jnp.concatenate([stc[1:2], stc[0:1], zero_row], axis=1)], axis=0),
        jnp.concatenate([jnp.concatenate([stb1[1:2], stb1[0:1], st1[2:3]], axis=1),
                         jnp.zeros((1, 3 * D_MODEL), F32)], axis=0)])
    dm_g, = _all_gather([dm_me], "gather_dmod")
    dm_all = jnp.concatenate([jnp.transpose(dm_g[:, :, 0], (1, 0, 2)), jnp.transpose(dm_g[:, :, 1], (1, 0, 2))],
                             axis=1)
    dm_my = lax.dynamic_slice(dm_all, (0, 0, dev * n_mod), (2, 16, n_mod))
    g_w_mod, g_b_mod, gcc_part = _mod_bwd(cond, dm_all, dm_my, w_mod, "mod_bwd")
    g_b_mod = g_b_mod.reshape(b_mod.shape)

    gwa = jnp.stack([gwa_f + gwa_cf, gwa_b + gwa_cb])
    gwx = jnp.stack([gwx_f + gwx_cf, gwx_b + gwx_cb])
    gv = jnp.stack([gv_f + gv_cf, gv_b + gv_cb])
    cst = cst0 + cstc
    g_ln_g = jnp.stack([stl0[0], st1[0]])
    g_ln_b = jnp.stack([stl0[1], st1[1]])
    sharded = [
        _blocks_by_device(cst[0:4], 1),
        _blocks_by_device(gv[:, 0], 1), _blocks_by_device(gv[:, 1], 1), _blocks_by_device(gv[:, 2], 1),
        _blocks_by_device(gps[0], 0),
    ]
    replicated = [gwa.reshape(-1), gwx.reshape(-1), g_ln_g.reshape(-1), g_ln_b.reshape(-1), cst[4],
                  gcc_part.reshape(-1)]
    sh_sizes = [int(np.prod(a.shape[1:])) for a in sharded]
    rep_sizes = [a.shape[0] // N_DEV for a in replicated]
    n_flat = sum(sh_sizes) + sum(rep_sizes)
    rows = -(-n_flat // LANES)
    rows = -(-rows // FLAT_ROWS) * FLAT_ROWS
    misc = jnp.concatenate([a.reshape(N_DEV, -1) for a in sharded] +
                           [a.reshape(N_DEV, -1) for a in replicated], axis=1)
    misc = jnp.pad(misc, ((0, 0), (0, rows * LANES - n_flat))).reshape(N_DEV, rows, LANES)
    bufs = [gwi0, gwi1, gwo0.reshape(N_DEV, D_INNER // N_DEV, D_MODEL), gwo1.reshape(N_DEV, D_INNER // N_DEV, D_MODEL),
            _blocks_by_device(gpw, 1).reshape(N_DEV, POOL_GROUP // N_DEV * len(POOL_WINDOWS), POOL_GROUP), misc]
    names_r = ["w_in0", "w_in1", "w_out0", "w_out1", "pool_w", "misc"]
    recvs = _sibling_exchange(bufs, "reduce_sibling")
    core = jnp.reshape(ci, (1,)).astype(jnp.int32)
    pairs = [_pair_sum(b, r, core, "reduce_pair_" + n) for b, r, n in zip(bufs, recvs, names_r)]
    p_wi0, p_wi1, p_wo0, p_wo1, p_pw, p_misc = _chip_exchange(pairs, "reduce_chips")
    g_flat = _sum4(p_misc, "reduce_sum_misc").reshape(-1)

    offs = np.cumsum([0] + sh_sizes + rep_sizes)
    n_sh = len(sh_sizes)
    sh_shapes = [conv_w.shape, lru_ba.shape, lru_bx.shape, lru_lam.shape, pool_scale.shape]
    g_sh = [g_flat[offs[k]:offs[k + 1]].reshape(sh_shapes[k]) for k in range(n_sh)]
    g_conv_w, g_lru_ba, g_lru_bx, g_lru_lam, g_pool_scale = g_sh
    rep_block = _to_rows([g_flat[offs[n_sh]:offs[-1]]], SUBLANES)
    rep_all, = _all_gather([rep_block], "gather_replicated")
    rep_flat = rep_all.reshape(N_DEV, -1)
    rep_full, off = [], 0
    for n in rep_sizes:
        rep_full.append(rep_flat[:, off:off + n].reshape(-1))
        off += n
    g_lru_wa = rep_full[0].reshape(lru_wa.shape)
    g_lru_wx = rep_full[1].reshape(lru_wx.shape)
    g_ln_g = rep_full[2].reshape(ln_g.shape)
    g_ln_b = rep_full[3].reshape(ln_b.shape)
    g_conv_b = rep_full[4].reshape(conv_b.shape)
    g_c_ctx = rep_full[5].reshape(c_ctx.shape)

    names = ["c_ctx", "w_mod", "b_mod", "w_in", "w_out", "ln_g", "ln_b", "conv_w", "conv_b", "lru_wa", "lru_ba",
             "lru_wx", "lru_bx", "lru_lam", "pool_w", "pool_scale"]
    weights = dict(c_ctx=c_ctx, w_mod=w_mod, b_mod=b_mod, w_in=w_in, w_out=w_out, ln_g=ln_g, ln_b=ln_b,
                   conv_w=conv_w, conv_b=conv_b, lru_wa=lru_wa, lru_ba=lru_ba, lru_wx=lru_wx, lru_bx=lru_bx,
                   lru_lam=lru_lam, pool_w=pool_w, pool_scale=pool_scale)
    mom_m = dict(c_ctx=m_c_ctx, w_mod=m_w_mod, b_mod=m_b_mod, w_in=m_w_in, w_out=m_w_out, ln_g=m_ln_g, ln_b=m_ln_b,
                 conv_w=m_conv_w, conv_b=m_conv_b, lru_wa=m_lru_wa, lru_ba=m_lru_ba, lru_wx=m_lru_wx,
                 lru_bx=m_lru_bx, lru_lam=m_lru_lam, pool_w=m_pool_w, pool_scale=m_pool_scale)
    mom_v = dict(c_ctx=v_c_ctx, w_mod=v_w_mod, b_mod=v_b_mod, w_in=v_w_in, w_out=v_w_out, ln_g=v_ln_g, ln_b=v_ln_b,
                 conv_w=v_conv_w, conv_b=v_conv_b, lru_wa=v_lru_wa, lru_ba=v_lru_ba, lru_wx=v_lru_wx,
                 lru_bx=v_lru_bx, lru_lam=v_lru_lam, pool_w=v_pool_w, pool_scale=v_pool_scale)
    grads = dict(c_ctx=g_c_ctx, w_mod=g_w_mod, b_mod=g_b_mod, ln_g=g_ln_g, ln_b=g_ln_b,
                 conv_w=g_conv_w, conv_b=g_conv_b, lru_wa=g_lru_wa, lru_ba=g_lru_ba, lru_wx=g_lru_wx,
                 lru_bx=g_lru_bx, lru_lam=g_lru_lam)
    grads["pool_scale"] = g_pool_scale
    delta, new_m, new_v = {}, {}, {}

    def update_parts(n, parts, view):
        res = _adamw_parts(weights[n].reshape(view), parts, mom_m[n].reshape(view), mom_v[n].reshape(view),
                           "adamw_" + n)
        grads[n], delta[n], new_m[n], new_v[n] = [r.reshape(weights[n].shape) for r in res]

    update_parts("w_in", [p_wi0, p_wi1], w_in.shape)
    update_parts("w_out", [p_wo0, p_wo1], w_out.shape)
    update_parts("pool_w", [p_pw], (1,) + p_pw.shape[1:])
    for n in ("w_mod", "lru_wa", "lru_wx"):
        shape = weights[n].shape
        view = (int(np.prod(shape[:-1])), shape[-1])
        res = _adamw(weights[n].reshape(view), grads[n].reshape(view), mom_m[n].reshape(view),
                     mom_v[n].reshape(view), "adamw_" + n)
        delta[n], new_m[n], new_v[n] = [r.reshape(shape) for r in res]

    small = [n for n in names if n not in delta]
    shapes = [weights[n].shape for n in small]
    flat = lambda d: _to_rows([d[n] for n in small], FLAT_ROWS)
    res = _adamw(flat(weights), flat(grads), flat(mom_m), flat(mom_v), "adamw_small")
    for d, r in zip((delta, new_m, new_v), res):
        d.update(zip(small, _split_rows(r, shapes)))

    return (loss, gx[None], *[grads[n] for n in names], *[delta[n] for n in names],
            *[new_m[n] for n in names], *[new_v[n] for n in names])
```

```python
import functools

import numpy as np
import jax
import jax.numpy as jnp
from jax import lax
from jax.experimental import pallas as pl
from jax.experimental.pallas import tpu as pltpu

F32 = jnp.float32
BF16 = jnp.bfloat16
MXU_DTYPE = BF16

D_MODEL = 1024
D_INNER = 2048
LRU_BLOCK = 128
GRID_W = 64
POOL_WINDOWS = (2, 4, 8, 16)
POOL_GROUP = 512
ALPHA = float(4 ** 0.25)
LN_EPS = 1e-5
LRU_C = 8.0
N_DEV = 8
N_WBLK = 8
WBLK = 512

ADAM_LR = 0.001
ADAM_B1 = 0.9
ADAM_B2 = 0.999
ADAM_EPS = 1e-08
ADAM_WD = 0.01
ADAM_STEP = 10

LANES = 128
SUBLANES = 8
V7X_VMEM_BYTES = 64 * 1024 * 1024
VMEM_LIMIT = V7X_VMEM_BYTES - 8 * 1024 * 1024
MESH = pl.DeviceIdType.MESH
ANY = pl.BlockSpec(memory_space=pl.ANY)

TM_MM = 512
TM_BWD = 256
TM_LRU = 512
CB_LRU = 512
N_SEG = 8
SCAN_UNROLL = 4
SQRT_FLOOR = 1e-30
FLAT_ROWS = 16
ELEMENTWISE_TILE_BYTES = 1 << 20
POOL_TOK = 256
WIRE_DTYPE = BF16


def _cparams(**kw):
    return pltpu.CompilerParams(vmem_limit_bytes=VMEM_LIMIT, **kw)


def _my_pos():
    return lax.axis_index("x"), lax.axis_index("y"), lax.axis_index("c")


def _dot(a, b):
    return jnp.dot(a.astype(MXU_DTYPE), b.astype(MXU_DTYPE), preferred_element_type=F32)


def _dot_tn(a, b):
    return lax.dot_general(a.astype(MXU_DTYPE), b.astype(MXU_DTYPE), (((0,), (0,)), ((), ())),
                           preferred_element_type=F32)


def _dot_nt(a, b):
    return lax.dot_general(a.astype(MXU_DTYPE), b.astype(MXU_DTYPE), (((1,), (1,)), ((), ())),
                           preferred_element_type=F32)


def _sigmoid(z):
    return 0.5 * jnp.tanh(0.5 * z) + 0.5


def _log_sigmoid(x):
    y = jnp.exp(-jnp.abs(x))
    u = 1.0 + y
    l1p = jnp.where(u == 1.0, y, jnp.log(u) * (y / jnp.where(u == 1.0, 1.0, u - 1.0)))
    return jnp.minimum(x, 0.0) - l1p


def _rowsum(v):
    return jnp.sum(v, axis=0, keepdims=True)


def _layer_norm_stats(z):
    mu = jnp.mean(z, axis=-1, keepdims=True)
    zc = z - mu
    var = jnp.mean(zc * zc, axis=-1, keepdims=True)
    rstd = lax.rsqrt(var + LN_EPS)
    return zc * rstd, rstd


def _layer_norm_bwd(dy, xhat, rstd, g):
    dxh = dy * g
    m1 = jnp.mean(dxh, axis=-1, keepdims=True)
    m2 = jnp.mean(dxh * xhat, axis=-1, keepdims=True)
    return rstd * (dxh - m1 - xhat * m2)


def _shifted(v, before8, after8, offsets):
    n = v.shape[0]
    ext = jnp.concatenate([before8, v, after8], axis=0)
    total = n + 2 * SUBLANES
    return [pltpu.roll(ext, (-k) % total, 0)[SUBLANES:SUBLANES + n] for k in offsets]


def _rows8(row):
    return jnp.broadcast_to(row, (SUBLANES, row.shape[1]))


def _shift_down(v, first_row):
    return _shifted(v, _rows8(first_row), _rows8(first_row), [-1])[0]


def _shift_up(v, last_row):
    return _shifted(v, _rows8(last_row), _rows8(last_row), [1])[0]


def _all_gather(blocks, name):
    n = len(blocks)

    def body(*refs):
        x_refs, out_refs = refs[:n], refs[n:2 * n]
        send_sems, recv_sems, local_sems = refs[2 * n:]
        x, y, c = _my_pos()
        me, sibling = (x, y, c), (x, y, 1 - c)
        chips = [(1 - x, y), (x, 1 - y), (1 - x, 1 - y)]

        def slot(a, px, py, pc):
            return out_refs[a].at[4 * px + 2 * py + pc]

        def copy(a, k, block, to, src=None):
            return pltpu.make_async_remote_copy(
                src_ref=slot(a, *block) if src is None else src, dst_ref=slot(a, *block),
                send_sem=send_sems.at[a, k], recv_sem=recv_sems.at[a, k], device_id=to, device_id_type=MESH)

        mine = [pltpu.make_async_copy(x_refs[a], slot(a, *me), local_sems.at[a]) for a in range(n)]
        for cp in mine:
            cp.start()
        first = []
        for a in range(n):
            first.append(copy(a, 0, me, sibling, src=x_refs[a]))
            first += [copy(a, 1 + j, me, (*chip, c), src=x_refs[a]) for j, chip in enumerate(chips)]
        for cp in first:
            cp.start()
        passed = []
        for j, chip in enumerate(chips):
            for a in range(n):
                copy(a, 1 + j, (*chip, c), me).wait_recv()
                fwd = copy(a, 4 + j, (*chip, c), sibling)
                fwd.start()
                passed.append(fwd)
        for a in range(n):
            copy(a, 0, sibling, me).wait_recv()
            for j, chip in enumerate(chips):
                copy(a, 4 + j, (*chip, 1 - c), me).wait_recv()
        for cp in first + passed:
            cp.wait_send()
        for cp in mine:
            cp.wait()

    outs = pl.pallas_call(
        body, name=name,
        out_shape=[jax.ShapeDtypeStruct((N_DEV,) + b.shape, b.dtype) for b in blocks],
        in_specs=[ANY] * n, out_specs=[ANY] * n,
        scratch_shapes=[pltpu.SemaphoreType.DMA((n, 7)), pltpu.SemaphoreType.DMA((n, 7)),
                        pltpu.SemaphoreType.DMA((n,))],
    )(*blocks)
    return list(outs)


def _sibling_exchange(bufs, name):
    n = len(bufs)

    def body(*refs):
        srcs, outs = refs[:n], refs[n:2 * n]
        send_sems, recv_sems = refs[2 * n:]
        x, y, c = _my_pos()
        copies = [pltpu.make_async_remote_copy(
            src_ref=srcs[a].at[2 * j + (1 - c)], dst_ref=outs[a].at[j], send_sem=send_sems.at[a, j],
            recv_sem=recv_sems.at[a, j], device_id=(x, y, 1 - c), device_id_type=MESH)
            for a in range(n) for j in range(4)]
        for cp in copies:
            cp.start()
        for cp in copies:
            cp.wait()

    outs = pl.pallas_call(
        body, name=name, out_shape=[jax.ShapeDtypeStruct((4,) + b.shape[1:], b.dtype) for b in bufs],
        in_specs=[ANY] * n, out_specs=[ANY] * n,
        scratch_shapes=[pltpu.SemaphoreType.DMA((n, 4)), pltpu.SemaphoreType.DMA((n, 4))],
    )(*bufs)
    return list(outs)


def _chip_exchange(parts, name):
    n = len(parts)

    def body(*refs):
        srcs, outs = refs[:n], refs[n:2 * n]
        send_sems, recv_sems, local_sems = refs[2 * n:]
        x, y, c = _my_pos()
        jme = 2 * x + y
        peers = [(1 - x, y), (x, 1 - y), (1 - x, 1 - y)]
        local = [pltpu.make_async_copy(srcs[a].at[jme], outs[a].at[jme], local_sems.at[a]) for a in range(n)]
        for cp in local:
            cp.start()

        def copy(a, k, px, py, dst_slot):
            return pltpu.make_async_remote_copy(
                src_ref=srcs[a].at[2 * px + py], dst_ref=outs[a].at[dst_slot], send_sem=send_sems.at[a, k],
                recv_sem=recv_sems.at[a, k], device_id=(px, py, c), device_id_type=MESH)

        sends = [copy(a, k, px, py, jme) for a in range(n) for k, (px, py) in enumerate(peers)]
        for cp in sends:
            cp.start()
        for a in range(n):
            for k, (px, py) in enumerate(peers):
                copy(a, k, px, py, 2 * px + py).wait_recv()
        for cp in sends:
            cp.wait_send()
        for cp in local:
            cp.wait()

    outs = pl.pallas_call(
        body, name=name, out_shape=[jax.ShapeDtypeStruct(p.shape, p.dtype) for p in parts],
        in_specs=[ANY] * n, out_specs=[ANY] * n,
        scratch_shapes=[pltpu.SemaphoreType.DMA((n, 3)), pltpu.SemaphoreType.DMA((n, 3)),
                        pltpu.SemaphoreType.DMA((n,))],
    )(*parts)
    return list(outs)


def _row_tile(r, l):
    t = min(r, max(16, ELEMENTWISE_TILE_BYTES // (4 * l) // 16 * 16))
    while r % t:
        t -= 16
    return t


def _pair_sum(buf, recv, core, name):
    _, r, l = buf.shape
    tr = _row_tile(r, l)

    def body(core_ref, a_ref, b_ref, o_ref):
        o_ref[...] = (a_ref[...] + b_ref[...]).astype(WIRE_DTYPE)

    return pl.pallas_call(
        body, name=name, out_shape=jax.ShapeDtypeStruct((4, r, l), WIRE_DTYPE),
        grid_spec=pltpu.PrefetchScalarGridSpec(
            num_scalar_prefetch=1, grid=(4, r // tr),
            in_specs=[pl.BlockSpec((None, tr, l), lambda j, i, cr: (2 * j + cr[0], i, 0)),
                      pl.BlockSpec((None, tr, l), lambda j, i, cr: (j, i, 0))],
            out_specs=pl.BlockSpec((None, tr, l), lambda j, i, cr: (j, i, 0))),
        compiler_params=_cparams(dimension_semantics=("arbitrary", "arbitrary")),
    )(core, buf, recv)


def _sum_parts(p_ref):
    return ((p_ref[0].astype(F32) + p_ref[1].astype(F32)) + (p_ref[2].astype(F32) + p_ref[3].astype(F32)))


def _sum4(parts, name):
    _, r, l = parts.shape
    tr = _row_tile(r, l)

    def body(p_ref, o_ref):
        o_ref[...] = _sum_parts(p_ref)

    return pl.pallas_call(
        body, name=name, out_shape=jax.ShapeDtypeStruct((r, l), F32), grid=(r // tr,),
        in_specs=[pl.BlockSpec((4, tr, l), lambda i: (0, i, 0))],
        out_specs=pl.BlockSpec((tr, l), lambda i: (i, 0)),
        compiler_params=_cparams(dimension_semantics=("arbitrary",)),
    )(parts)


def _adamw_update(w, gg, m, v):
    nm = ADAM_B1 * m + (1.0 - ADAM_B1) * gg
    nv = ADAM_B2 * v + (1.0 - ADAM_B2) * (gg * gg)
    m_hat = nm / (1.0 - ADAM_B1 ** ADAM_STEP)
    v_hat = nv / (1.0 - ADAM_B2 ** ADAM_STEP)
    return -ADAM_LR * (m_hat / (jnp.sqrt(v_hat) + ADAM_EPS) + ADAM_WD * w), nm, nv


def _adamw(w, g, m, v, name):
    r, l = w.shape
    tr = _row_tile(r, l)

    def body(w_ref, g_ref, m_ref, v_ref, d_ref, nm_ref, nv_ref):
        d_ref[...], nm_ref[...], nv_ref[...] = _adamw_update(w_ref[...], g_ref[...], m_ref[...], v_ref[...])

    spec = pl.BlockSpec((tr, l), lambda i: (i, 0))
    return pl.pallas_call(
        body, name=name, out_shape=[jax.ShapeDtypeStruct((r, l), F32)] * 3, grid=(r // tr,),
        in_specs=[spec] * 4, out_specs=[spec] * 3,
        compiler_params=_cparams(dimension_semantics=("arbitrary",)),
    )(w, g, m, v)


def _adamw_parts(w, parts, m, v, name):
    nl, r, l = w.shape
    tr = _row_tile(r, l)

    def body(*refs):
        w_ref, p_refs, (m_ref, v_ref, g_ref, d_ref, nm_ref, nv_ref) = refs[0], refs[1:1 + nl], refs[1 + nl:]
        layer = pl.program_id(0)
        gg = _sum_parts(p_refs[0])
        for q in range(1, nl):
            gg = jnp.where(layer == q, _sum_parts(p_refs[q]), gg)
        g_ref[...] = gg
        d_ref[...], nm_ref[...], nv_ref[...] = _adamw_update(w_ref[...], gg, m_ref[...], v_ref[...])

    spec = pl.BlockSpec((None, tr, l), lambda q, i: (q, i, 0))
    pspecs = [pl.BlockSpec((4, tr, l), lambda q, i, k=k: (0, jnp.where(q == k, i, 0), 0)) for k in range(nl)]
    return pl.pallas_call(
        body, name=name, out_shape=[jax.ShapeDtypeStruct((nl, r, l), F32)] * 4, grid=(nl, r // tr),
        in_specs=[spec] + pspecs + [spec, spec], out_specs=[spec] * 4,
        compiler_params=_cparams(dimension_semantics=("arbitrary", "arbitrary")),
    )(w, *parts, m, v)


def _to_rows(pieces, row_multiple):
    flat = jnp.concatenate([p.reshape(-1) for p in pieces])
    rows = -(-flat.shape[0] // LANES)
    rows = -(-rows // row_multiple) * row_multiple
    flat = jnp.pad(flat, (0, rows * LANES - flat.shape[0]))
    return flat.reshape(rows, LANES)


def _split_rows(rows, shapes):
    flat = rows.reshape(-1)
    out, off = [], 0
    for s in shapes:
        n = int(np.prod(s))
        out.append(flat[off:off + n].reshape(s))
        off += n
    return out


def _mod_fwd(cond, w_mod, b_my, name):
    nl, _, ncol = w_mod.shape

    def body(a_ref, w_ref, b_ref, o_ref):
        a = a_ref[...]
        s = a * _sigmoid(a)
        for i in range(nl):
            o_ref[i] = _dot(s, w_ref[i]) + b_ref[i]

    return pl.pallas_call(
        body, name=name, out_shape=jax.ShapeDtypeStruct((nl, 16, ncol), F32),
        compiler_params=_cparams(),
    )(cond, w_mod, b_my)


def _mod_bwd(cond, dm_all, dm_my, w_mod, name):
    nl, _, ncol = w_mod.shape

    def body(a_ref, dma_ref, dmm_ref, w_ref, gw_ref, gb_ref, gc_ref):
        a = a_ref[...]
        sg = _sigmoid(a)
        s = a * sg
        for i in range(nl):
            gw_ref[i] = _dot_tn(s, dmm_ref[i])
            gb_ref[i] = jnp.sum(dma_ref[i], axis=0, keepdims=True)
        back = _dot_nt(dmm_ref[0], w_ref[0])
        dsilu = sg * (1.0 + a * (1.0 - sg))
        gc_ref[...] = jnp.sum(back[8:16] * dsilu[8:16], axis=0, keepdims=True)

    return pl.pallas_call(
        body, name=name,
        out_shape=[jax.ShapeDtypeStruct((nl, D_MODEL, ncol), F32), jax.ShapeDtypeStruct((nl, 1, 3 * D_MODEL), F32),
                   jax.ShapeDtypeStruct((1, D_MODEL), F32)],
        compiler_params=_cparams(),
    )(cond, dm_all, dm_my, w_mod)


def _in_proj(xt, sc, sh, wg, name):
    t = xt.shape[0]
    tm = min(TM_MM, t)

    def body(x_ref, sc_ref, sh_ref, w_ref, u_ref, g_ref):
        h = (x_ref[...] * (1.0 + sc_ref[...]) + sh_ref[...]).astype(MXU_DTYPE)
        for k in range(N_WBLK):
            o = jnp.dot(h, w_ref[k], preferred_element_type=F32)
            if k < N_WBLK // 2:
                u_ref[:, k * WBLK:(k + 1) * WBLK] = o
            else:
                kk = k - N_WBLK // 2
                g_ref[:, kk * WBLK:(kk + 1) * WBLK] = o

    row = pl.BlockSpec((1, D_MODEL), lambda i: (0, 0))
    return pl.pallas_call(
        body, name=name, out_shape=[jax.ShapeDtypeStruct((t, D_INNER), F32)] * 2, grid=(t // tm,),
        in_specs=[pl.BlockSpec((tm, D_MODEL), lambda i: (i, 0)), row, row,
                  pl.BlockSpec((N_WBLK, D_MODEL, WBLK), lambda i: (0, 0, 0), pipeline_mode=pl.Buffered(1))],
        out_specs=[pl.BlockSpec((tm, D_INNER), lambda i: (i, 0))] * 2,
        compiler_params=_cparams(dimension_semantics=("arbitrary",)),
    )(xt, sc, sh, wg)


def _halo_maps(nt, tm, n_rows8, pos):
    per = tm // SUBLANES
    prev = lambda cb, i: (jnp.maximum(pos(i) * per - 1, 0), cb)
    nxt = lambda cb, i: (jnp.minimum((pos(i) + 1) * per, n_rows8 - 1), cb)
    return prev, nxt


def _conv_taps(u, prev8, next8, is_first, is_last):
    pz = jnp.where(is_first, 0.0, 1.0)
    nz = jnp.where(is_last, 0.0, 1.0)
    return _shifted(u, prev8 * pz, next8 * nz, [-2, -1, 1])


def _lru_gates(uv, wa_ref, wx_ref, ba, bx, cl, g):
    sl = slice(g * LANES, (g + 1) * LANES)
    uvg = uv[:, sl]
    r = _sigmoid(_dot(uvg, wa_ref[g]) + ba[:, sl])
    ii = _sigmoid(_dot(uvg, wx_ref[g]) + bx[:, sl])
    la = cl[:, sl] * r
    a = jnp.exp(la)
    q = jnp.tanh(-la) * (1.0 + a * a)
    rs = lax.rsqrt(jnp.maximum(q, SQRT_FLOOR))
    return uvg, r, ii, a, q * rs, rs


def _scan_tile(a_s, b_s, carry_ref, write_out, seg, reverse):
    n_g = a_s.shape[0]
    stride = a_s.shape[1] // N_SEG

    unroll = SCAN_UNROLL if seg % SCAN_UNROLL == 0 else 1

    def steps(k, state):
        hs, cs = list(state[0]), list(state[1])
        for q in range(unroll):
            t = k * unroll + q
            if reverse:
                t = seg - 1 - t
            for g in range(n_g):
                a = a_s[g, pl.ds(t, N_SEG, stride=stride), :]
                b = b_s[g, pl.ds(t, N_SEG, stride=stride), :]
                hs[g] = a * hs[g] + b
                cs[g] = a * cs[g]
                b_s[g, pl.ds(t, N_SEG, stride=stride), :] = hs[g]
                a_s[g, pl.ds(t, N_SEG, stride=stride), :] = cs[g]
        return tuple(hs), tuple(cs)

    zeros = tuple(jnp.zeros((N_SEG, LANES), F32) for _ in range(n_g))
    ones = tuple(jnp.ones((N_SEG, LANES), F32) for _ in range(n_g))
    h_fin, a_fin = lax.fori_loop(0, seg // unroll, steps, (zeros, ones))

    order = list(range(N_SEG - 1, -1, -1)) if reverse else list(range(N_SEG))
    for g in range(n_g):
        carry = carry_ref[:, g * LANES:(g + 1) * LANES]
        for j in order:
            rows = pl.ds(j * stride, seg)
            write_out(j, g, b_s[g, rows, :] + a_s[g, rows, :] * carry)
            carry = a_fin[g][j:j + 1] * carry + h_fin[g][j:j + 1]
        carry_ref[:, g * LANES:(g + 1) * LANES] = carry


def _lru_specs(s, tm, cb, direction_pos, nt):
    n_rows8 = s // SUBLANES
    prev, nxt = _halo_maps(nt, tm, n_rows8, direction_pos)
    tile = pl.BlockSpec((tm, cb), lambda c, i: (direction_pos(i), c))
    return tile, pl.BlockSpec((SUBLANES, cb), prev), pl.BlockSpec((SUBLANES, cb), nxt)


def _lru_param_specs(cb, d):
    n_g = cb // LANES
    vec = pl.BlockSpec((1, cb), lambda c, i: (0, c))
    dvec = pl.BlockSpec((None, 1, cb), lambda c, i: (d, 0, c))
    wmat = pl.BlockSpec((None, n_g, LRU_BLOCK, LRU_BLOCK), lambda c, i: (d, c, 0, 0))
    return vec, dvec, wmat


def _lru_fwd(src, h0, p, d, name, conv):
    s = src.shape[0]
    tm = min(TM_LRU, s)
    cb = CB_LRU
    n_g = cb // LANES
    nt = s // tm
    seg = tm // N_SEG
    stride = seg + SUBLANES
    pos = (lambda i: i) if d == 0 else (lambda i: nt - 1 - i)

    def body(*refs):
        refs = list(refs)
        u_ref = refs.pop(0)
        if conv:
            up_ref, un_ref, cw_ref, cbias_ref = [refs.pop(0) for _ in range(4)]
        wa_ref, wx_ref, ba_ref, bx_ref, lam_ref, h0_ref, h_ref, hc_ref = [refs.pop(0) for _ in range(8)]
        uv_ref = refs.pop(0) if conv else None
        a_s, b_s = refs
        i = pl.program_id(1)
        tp = pos(i)

        @pl.when(i == 0)
        def _():
            hc_ref[...] = h0_ref[...]

        uv = u_ref[...]
        if conv:
            um2, um1, up1 = _conv_taps(uv, up_ref[...], un_ref[...], tp == 0, tp == nt - 1)
            cw = cw_ref[...]
            uv = um2 * cw[0:1] + um1 * cw[1:2] + uv * cw[2:3] + up1 * cw[3:4] + cbias_ref[...]
            uv_ref[...] = uv
        cl = LRU_C * _log_sigmoid(lam_ref[...])
        ba, bx = ba_ref[...], bx_ref[...]
        for g in range(n_g):
            uvg, r, ii, a, sq, _ = _lru_gates(uv, wa_ref, wx_ref, ba, bx, cl, g)
            b = sq * (ii * uvg)
            for j in range(N_SEG):
                a_s[g, pl.ds(j * stride, seg), :] = a[j * seg:(j + 1) * seg]
                b_s[g, pl.ds(j * stride, seg), :] = b[j * seg:(j + 1) * seg]

        def write_out(j, g, h):
            h_ref[pl.ds(j * seg, seg), pl.ds(g * LANES, LANES)] = h

        _scan_tile(a_s, b_s, hc_ref, write_out, seg, reverse=(d == 1))

    tile, prev, nxt = _lru_specs(s, tm, cb, pos, nt)
    vec, dvec, wmat = _lru_param_specs(cb, d)
    wide = jax.ShapeDtypeStruct((s, D_INNER), F32)
    conv_specs = [prev, nxt, pl.BlockSpec((4, cb), lambda c, i: (0, c)), vec] if conv else []
    conv_args = [src, src, p["conv_w"], p["conv_b"]] if conv else []
    return pl.pallas_call(
        body, name=name,
        out_shape=[wide, jax.ShapeDtypeStruct((1, D_INNER), F32)] + ([wide] if conv else []),
        grid=(D_INNER // cb, nt),
        in_specs=[tile] + conv_specs + [wmat, wmat, dvec, dvec, dvec, vec],
        out_specs=[tile, vec] + ([tile] if conv else []),
        scratch_shapes=[pltpu.VMEM((n_g, N_SEG * stride, LANES), F32)] * 2,
        compiler_params=_cparams(dimension_semantics=("arbitrary", "arbitrary")),
    )(src, *conv_args, p["wa"], p["wx"], p["ba"], p["bx"], p["lam"], h0)


def _lru_bwd(uv, dh, h, h0, lam_in, p, d, name):
    s = uv.shape[0]
    tm = min(TM_LRU, s)
    cb = CB_LRU
    n_g = cb // LANES
    nt = s // tm
    seg = tm // N_SEG
    stride = seg + SUBLANES
    pos = (lambda i: nt - 1 - i) if d == 0 else (lambda i: i)

    def body(uv_ref, dh_ref, h_ref, hh_ref, wa_ref, wx_ref, ba_ref, bx_ref,
             lam_ref, h0_ref, lin_ref, duv_ref, gwa_ref, gwx_ref, gv_ref, lc_ref, a_s, b_s, lp_s,
             r_s, i_s, q_s, rq_s, a_keep):
        i = pl.program_id(1)
        tp = pos(i)

        @pl.when(i == 0)
        def _():
            lc_ref[...] = lin_ref[...]
            gwa_ref[...] = jnp.zeros_like(gwa_ref)
            gwx_ref[...] = jnp.zeros_like(gwx_ref)
            gv_ref[...] = jnp.zeros_like(gv_ref)

        uv = uv_ref[...]
        lam = lam_ref[...]
        cl = LRU_C * _log_sigmoid(lam)
        ba, bx = ba_ref[...], bx_ref[...]
        dh_t = dh_ref[...]
        carry_in = lc_ref[...]
        for g in range(n_g):
            sl = slice(g * LANES, (g + 1) * LANES)
            _, r, ii, a, sq, rs = _lru_gates(uv, wa_ref, wx_ref, ba, bx, cl, g)
            r_s[:, sl], i_s[:, sl], q_s[:, sl], rq_s[:, sl], a_keep[:, sl] = r, ii, sq, rs, a
            b = a * dh_t[:, sl]
            for j in range(N_SEG):
                a_s[g, pl.ds(j * stride, seg), :] = a[j * seg:(j + 1) * seg]
                b_s[g, pl.ds(j * stride, seg), :] = b[j * seg:(j + 1) * seg]

        def write_out(j, g, v):
            lp_s[pl.ds(j * seg, seg), pl.ds(g * LANES, LANES)] = v

        _scan_tile(a_s, b_s, lc_ref, write_out, seg, reverse=(d == 0))

        h_t = h_ref[...]
        hh = hh_ref[...]
        if d == 0:
            edge = jnp.where(tp == 0, h0_ref[...], hh[7:8])
            h_prev = _shift_down(h_t, edge)
            lam_t = dh_t + _shift_up(lp_s[...], carry_in)
        else:
            edge = jnp.where(tp == nt - 1, h0_ref[...], hh[0:1])
            h_prev = _shift_up(h_t, edge)
            lam_t = dh_t + _shift_down(lp_s[...], carry_in)

        dsig = LRU_C * _sigmoid(-lam)
        for g in range(n_g):
            sl = slice(g * LANES, (g + 1) * LANES)
            uvg, r, ii, a, sq = uv[:, sl], r_s[:, sl], i_s[:, sl], a_keep[:, sl], q_s[:, sl]
            lt = lam_t[:, sl]
            ls = lt * sq
            dla = (lt * a) * (h_prev[:, sl] - (ii * uvg) * (a * rq_s[:, sl]))
            dzr = (dla * cl[:, sl]) * r * (1.0 - r)
            dzi = (ls * uvg) * ii * (1.0 - ii)
            duv_ref[:, sl] = ls * ii + _dot_nt(dzr, wa_ref[g]) + _dot_nt(dzi, wx_ref[g])
            gwa_ref[g] += _dot_tn(uvg, dzr)
            gwx_ref[g] += _dot_tn(uvg, dzi)
            gv_ref[0:1, sl] += _rowsum(dzr)
            gv_ref[1:2, sl] += _rowsum(dzi)
            gv_ref[2:3, sl] += _rowsum(dla * r) * dsig[:, sl]

    tile, prev, nxt = _lru_specs(s, tm, cb, pos, nt)
    vec, dvec, wmat = _lru_param_specs(cb, d)
    hh_spec = prev if d == 0 else nxt
    gw_spec = pl.BlockSpec((n_g, LRU_BLOCK, LRU_BLOCK), lambda c, i: (c, 0, 0))
    n_blk = D_INNER // LRU_BLOCK
    return pl.pallas_call(
        body, name=name,
        out_shape=[jax.ShapeDtypeStruct((s, D_INNER), F32),
                   jax.ShapeDtypeStruct((n_blk, LRU_BLOCK, LRU_BLOCK), F32),
                   jax.ShapeDtypeStruct((n_blk, LRU_BLOCK, LRU_BLOCK), F32),
                   jax.ShapeDtypeStruct((SUBLANES, D_INNER), F32),
                   jax.ShapeDtypeStruct((1, D_INNER), F32)],
        grid=(D_INNER // cb, nt),
        in_specs=[tile, tile, tile, hh_spec, wmat, wmat, dvec, dvec, dvec, vec, vec],
        out_specs=[tile, gw_spec, gw_spec, pl.BlockSpec((SUBLANES, cb), lambda c, i: (0, c)), vec],
        scratch_shapes=[pltpu.VMEM((n_g, N_SEG * stride, LANES), F32)] * 2 + [pltpu.VMEM((tm, cb), F32)] * 6,
        compiler_params=_cparams(dimension_semantics=("arbitrary", "arbitrary")),
    )(uv, dh, h, h, p["wa"], p["wx"], p["ba"], p["bx"], p["lam"], h0, lam_in)


def _out0(hf, hb, g, xt, gt, wo, lg, lb, name):
    t = xt.shape[0]
    tm = min(TM_MM, t)

    def body(hf_ref, hb_ref, g_ref, x_ref, gt_ref, w_ref, lg_ref, lb_ref, x1_ref, br_ref):
        gg = g_ref[...]
        p = (hf_ref[...] + hb_ref[...]) * (gg * _sigmoid(gg))
        br = _dot(p, w_ref[...])
        z = ALPHA * x_ref[...] + gt_ref[...] * br
        xhat, _ = _layer_norm_stats(z)
        x1_ref[...] = xhat * lg_ref[...] + lb_ref[...]
        br_ref[...] = br

    wide = pl.BlockSpec((tm, D_INNER), lambda i: (i, 0))
    nar = pl.BlockSpec((tm, D_MODEL), lambda i: (i, 0))
    row = pl.BlockSpec((1, D_MODEL), lambda i: (0, 0))
    return pl.pallas_call(
        body, name=name, out_shape=[jax.ShapeDtypeStruct((t, D_MODEL), F32)] * 2, grid=(t // tm,),
        in_specs=[wide, wide, wide, nar, row,
                  pl.BlockSpec((D_INNER, D_MODEL), lambda i: (0, 0), pipeline_mode=pl.Buffered(1)), row, row],
        out_specs=[nar, nar],
        compiler_params=_cparams(dimension_semantics=("arbitrary",)),
    )(hf, hb, g, xt, gt, wo, lg, lb)


def _unrolled_loop(n, fn, unroll=4):
    while n % unroll:
        unroll //= 2

    def trip(k, carry):
        for q in range(unroll):
            fn(k * unroll + q)
        return carry
    lax.fori_loop(0, n // unroll, trip, 0)


def _window(n, w):
    t = np.arange(n)
    return np.clip(t - w // 2, 0, n), np.clip(t + w // 2, 0, n)


def _pool_tables(n_rows, transpose):
    boxes, inv_c, inv_r = [], [], []
    for w in POOL_WINDOWS:
        lo, hi = _window(GRID_W, w)
        m = np.zeros((GRID_W, GRID_W), np.float32)
        for r in range(GRID_W):
            m[r, lo[r]:hi[r]] = 1.0
        m = np.kron(np.eye(POOL_TOK // GRID_W, dtype=np.float32), m)
        boxes.append(m.T if transpose else m)
        inv_c.append(np.broadcast_to((1.0 / (hi - lo).astype(np.float32))[:, None], (GRID_W, LANES)))
        lo_r, hi_r = _window(n_rows, w)
        inv_r.append(1.0 / (hi_r - lo_r).astype(np.float32))
    return (jnp.asarray(np.stack(boxes), MXU_DTYPE), jnp.asarray(np.stack(inv_c), F32),
            jnp.asarray(np.stack(inv_r), F32))


def _pool_mix(xin, transpose, out_dtype, name):
    s = xin.shape[0]
    n_rows = s // GRID_W
    pad_t = SUBLANES * GRID_W
    rows_per_blk = POOL_TOK // GRID_W
    n_slab = D_INNER // LANES
    slabs_per_group = POOL_GROUP // LANES
    n_win = len(POOL_WINDOWS)
    boxes, inv_c, inv_r = _pool_tables(n_rows, transpose)

    def body(invr_ref, box_ref, invc_ref, x_ref, o_ref, pad_s):
        k = pl.program_id(0) // slabs_per_group
        pad_s[pl.ds(0, pad_t), :] = jnp.zeros((pad_t, LANES), F32)
        pad_s[pl.ds(pad_t + s, pad_t), :] = jnp.zeros((pad_t, LANES), F32)

        for kk, w in enumerate(POOL_WINDOWS):
            half = w // 2
            offsets = list(range(-(half - 1), half + 1)) if transpose else list(range(-half, half))

            @pl.when(k == kk)
            def _():
                inv_col = invc_ref[kk]

                def col_box(b):
                    st = pl.multiple_of(b * POOL_TOK, POOL_TOK)
                    xb = x_ref[pl.ds(st, POOL_TOK), :]
                    if transpose:
                        xb = xb * jnp.concatenate(
                            [inv_col * invr_ref[kk, b * rows_per_blk + q] for q in range(rows_per_blk)], axis=0)
                    hi = xb.astype(MXU_DTYPE)
                    lo = (xb - hi.astype(F32)).astype(MXU_DTYPE)
                    both = jnp.dot(box_ref[kk], jnp.concatenate([hi, lo], axis=1), preferred_element_type=F32)
                    pad_s[pl.ds(pad_t + st, POOL_TOK), :] = both[:, :LANES] + both[:, LANES:]
                _unrolled_loop(s // POOL_TOK, col_box)

                def row_box(r):
                    st = pl.multiple_of(r * GRID_W, GRID_W)
                    acc = pad_s[pl.ds(pad_t + st + offsets[0] * GRID_W, GRID_W), :]
                    for o in offsets[1:]:
                        acc = acc + pad_s[pl.ds(pad_t + st + o * GRID_W, GRID_W), :]
                    if not transpose:
                        acc = acc * (inv_col * invr_ref[kk, r])
                    o_ref[pl.ds(st, GRID_W), :] = (acc - x_ref[pl.ds(st, GRID_W), :]).astype(out_dtype)
                _unrolled_loop(n_rows, row_box)

    slab = pl.BlockSpec((s, LANES), lambda i: (0, i))
    return pl.pallas_call(
        body, name=name, out_shape=jax.ShapeDtypeStruct((s, D_INNER), out_dtype), grid=(n_slab,),
        in_specs=[pl.BlockSpec(memory_space=pltpu.SMEM),
                  pl.BlockSpec((n_win, POOL_TOK, POOL_TOK), lambda i: (0, 0, 0)),
                  pl.BlockSpec((n_win, GRID_W, LANES), lambda i: (0, 0, 0)), slab],
        out_specs=slab,
        scratch_shapes=[pltpu.VMEM((s + 2 * pad_t, LANES), F32)],
        compiler_params=_cparams(dimension_semantics=("arbitrary",)),
    )(inv_r, boxes, inv_c, xin)


def _out1(dmix, pw, ps, g, x1, gt, wo, lg, lb, tgt, name):
    t = x1.shape[0]
    tm = min(TM_MM, t)
    n_grp = len(POOL_WINDOWS)

    def body(d_ref, pw_ref, ps_ref, g_ref, x1_ref, gt_ref, w_ref, lg_ref, lb_ref, tgt_ref, dz_ref, st_ref):
        @pl.when(pl.program_id(0) == 0)
        def _():
            st_ref[...] = jnp.zeros_like(st_ref)

        br = jnp.zeros((tm, D_MODEL), F32)
        for k in range(n_grp):
            sl = slice(k * POOL_GROUP, (k + 1) * POOL_GROUP)
            y = jnp.dot(d_ref[:, sl], pw_ref[k], preferred_element_type=F32) * ps_ref[:, sl]
            gg = g_ref[:, sl]
            br = br + _dot(y * (gg * _sigmoid(gg)), w_ref[sl, :])
        z = ALPHA * x1_ref[...] + gt_ref[...] * br
        xhat, rstd = _layer_norm_stats(z)
        lg_v = lg_ref[...]
        err = xhat * lg_v + lb_ref[...] - tgt_ref[...]
        dy = err * (1.0 / D_MODEL)
        dz = _layer_norm_bwd(dy, xhat, rstd, lg_v)
        dz_ref[...] = dz
        st_ref[0:1, :] += _rowsum(dy * xhat)
        st_ref[1:2, :] += _rowsum(dy)
        st_ref[2:3, :] += _rowsum(dz * br)
        st_ref[3:4, :] += _rowsum(err * err)

    wide = pl.BlockSpec((tm, D_INNER), lambda i: (i, 0))
    nar = pl.BlockSpec((tm, D_MODEL), lambda i: (i, 0))
    row = pl.BlockSpec((1, D_MODEL), lambda i: (0, 0))
    return pl.pallas_call(
        body, name=name,
        out_shape=[jax.ShapeDtypeStruct((t, D_MODEL), F32), jax.ShapeDtypeStruct((SUBLANES, D_MODEL), F32)],
        grid=(t // tm,),
        in_specs=[wide, pl.BlockSpec((n_grp, POOL_GROUP, POOL_GROUP), lambda i: (0, 0, 0)),
                  pl.BlockSpec((1, D_INNER), lambda i: (0, 0)), wide, nar, row,
                  pl.BlockSpec((D_INNER, D_MODEL), lambda i: (0, 0), pipeline_mode=pl.Buffered(1)), row, row, nar],
        out_specs=[nar, pl.BlockSpec((SUBLANES, D_MODEL), lambda i: (0, 0))],
        compiler_params=_cparams(dimension_semantics=("arbitrary",)),
    )(dmix, pw, ps, g, x1, gt, wo, lg, lb, tgt)


def _flush(acc, out_hbm, sem):
    cp = pltpu.make_async_copy(acc, out_hbm, sem)
    cp.start()
    cp.wait()


def _bout1(dz, dmix, g, pw, ps, gt, wo, name):
    t = dz.shape[0]
    tm = min(TM_BWD, t)
    nt = t // tm
    n_grp = len(POOL_WINDOWS)

    def body(dz_ref, d_ref, g_ref, pw_ref, ps_ref, gt_ref, w_ref, dd_ref, dg_ref, gwo_hbm, gpw_hbm, gps_ref,
             gwo_acc, gpw_acc, sems):
        i = pl.program_id(0)

        @pl.when(i == 0)
        def _():
            gwo_acc[...] = jnp.zeros_like(gwo_acc)
            gpw_acc[...] = jnp.zeros_like(gpw_acc)
            gps_ref[...] = jnp.zeros_like(gps_ref)

        db = (gt_ref[...] * dz_ref[...]).astype(MXU_DTYPE)
        for k in range(n_grp):
            sl = slice(k * POOL_GROUP, (k + 1) * POOL_GROUP)
            dk = d_ref[:, sl]
            po = jnp.dot(dk, pw_ref[k], preferred_element_type=F32)
            psk = ps_ref[:, sl]
            y = po * psk
            gg = g_ref[:, sl]
            sg = _sigmoid(gg)
            silu = gg * sg
            gwo_acc[sl, :] += _dot_tn(y * silu, db)
            dp = _dot_nt(db, w_ref[sl, :])
            dy = dp * silu
            dg_ref[:, sl] = (dp * y * (sg * (1.0 + gg * (1.0 - sg)))).astype(MXU_DTYPE)
            gps_ref[0:1, sl] += _rowsum(dy * po)
            dpo = (dy * psk).astype(MXU_DTYPE)
            gpw_acc[k] += _dot_tn(dk, dpo)
            dd_ref[:, sl] = _dot_nt(dpo, pw_ref[k])

        @pl.when(i == nt - 1)
        def _():
            _flush(gwo_acc, gwo_hbm, sems.at[0])
            _flush(gpw_acc, gpw_hbm, sems.at[1])

    wide = pl.BlockSpec((tm, D_INNER), lambda i: (i, 0))
    nar = pl.BlockSpec((tm, D_MODEL), lambda i: (i, 0))
    return pl.pallas_call(
        body, name=name,
        out_shape=[jax.ShapeDtypeStruct((t, D_INNER), F32), jax.ShapeDtypeStruct((t, D_INNER), MXU_DTYPE),
                   jax.ShapeDtypeStruct((D_INNER, D_MODEL), F32),
                   jax.ShapeDtypeStruct((n_grp, POOL_GROUP, POOL_GROUP), F32),
                   jax.ShapeDtypeStruct((SUBLANES, D_INNER), F32)],
        grid=(nt,),
        in_specs=[nar, wide, wide, pl.BlockSpec((n_grp, POOL_GROUP, POOL_GROUP), lambda i: (0, 0, 0)),
                  pl.BlockSpec((1, D_INNER), lambda i: (0, 0)), pl.BlockSpec((1, D_MODEL), lambda i: (0, 0)),
                  pl.BlockSpec((D_INNER, D_MODEL), lambda i: (0, 0), pipeline_mode=pl.Buffered(1))],
        out_specs=[wide, wide, ANY, ANY, pl.BlockSpec((SUBLANES, D_INNER), lambda i: (0, 0))],
        scratch_shapes=[pltpu.VMEM((D_INNER, D_MODEL), F32), pltpu.VMEM((n_grp, POOL_GROUP, POOL_GROUP), F32),
                        pltpu.SemaphoreType.DMA((2,))],
        compiler_params=_cparams(dimension_semantics=("arbitrary",)),
    )(dz, dmix, g, pw, ps, gt, wo)


def _bout0(dx1, xt, br0, lg, hf, hb, g, gt, wo, name):
    t = dx1.shape[0]
    tm = min(TM_BWD, t)
    nt = t // tm

    def body(dx_ref, x_ref, br_ref, lg_ref, hf_ref, hb_ref, g_ref, gt_ref, w_ref,
             dz_ref, dy_ref, dg_ref, gwo_hbm, st_ref, gwo_acc, sem):
        i = pl.program_id(0)

        @pl.when(i == 0)
        def _():
            gwo_acc[...] = jnp.zeros_like(gwo_acc)
            st_ref[...] = jnp.zeros_like(st_ref)

        dx = dx_ref[...]
        br = br_ref[...]
        gate = gt_ref[...]
        xhat, rstd = _layer_norm_stats(ALPHA * x_ref[...] + gate * br)
        dz = _layer_norm_bwd(dx, xhat, rstd, lg_ref[...])
        dz_ref[...] = dz
        st_ref[0:1, :] += _rowsum(dx * xhat)
        st_ref[1:2, :] += _rowsum(dx)
        st_ref[2:3, :] += _rowsum(dz * br)
        db = (gate * dz).astype(MXU_DTYPE)
        y = hf_ref[...] + hb_ref[...]
        gg = g_ref[...]
        sg = _sigmoid(gg)
        silu = gg * sg
        gwo_acc[...] += _dot_tn(y * silu, db)
        dp = _dot_nt(db, w_ref[...])
        dy_ref[...] = dp * silu
        dg_ref[...] = (dp * y * (sg * (1.0 + gg * (1.0 - sg)))).astype(MXU_DTYPE)

        @pl.when(i == nt - 1)
        def _():
            _flush(gwo_acc, gwo_hbm, sem)

    wide = pl.BlockSpec((tm, D_INNER), lambda i: (i, 0))
    nar = pl.BlockSpec((tm, D_MODEL), lambda i: (i, 0))
    row = pl.BlockSpec((1, D_MODEL), lambda i: (0, 0))
    return pl.pallas_call(
        body, name=name,
        out_shape=[jax.ShapeDtypeStruct((t, D_MODEL), F32), jax.ShapeDtypeStruct((t, D_INNER), F32),
                   jax.ShapeDtypeStruct((t, D_INNER), MXU_DTYPE), jax.ShapeDtypeStruct((D_INNER, D_MODEL), F32),
                   jax.ShapeDtypeStruct((SUBLANES, D_MODEL), F32)],
        grid=(nt,),
        in_specs=[nar, nar, nar, row, wide, wide, wide, row,
                  pl.BlockSpec((D_INNER, D_MODEL), lambda i: (0, 0), pipeline_mode=pl.Buffered(1))],
        out_specs=[nar, wide, wide, ANY, pl.BlockSpec((SUBLANES, D_MODEL), lambda i: (0, 0))],
        scratch_shapes=[pltpu.VMEM((D_INNER, D_MODEL), F32), pltpu.SemaphoreType.DMA(())],
        compiler_params=_cparams(dimension_semantics=("arbitrary",)),
    )(dx1, xt, br0, lg, hf, hb, g, gt, wo)


def _conv_bwd(duvf, duvb, u, conv_w, name):
    s = u.shape[0]
    tm = min(TM_LRU, s)
    cb = CB_LRU
    nt = s // tm

    def body(df_ref, dfp_ref, dfn_ref, db_ref, dbp_ref, dbn_ref, u_ref, cw_ref, du_ref, cst_ref):
        i = pl.program_id(1)

        @pl.when(i == 0)
        def _():
            cst_ref[...] = jnp.zeros_like(cst_ref)

        first, last = i == 0, i == nt - 1
        pz = jnp.where(first, 0.0, 1.0)
        nz = jnp.where(last, 0.0, 1.0)
        dout = df_ref[...] + db_ref[...]
        dm1, dp1, dp2 = _shifted(dout, (dfp_ref[...] + dbp_ref[...]) * pz, (dfn_ref[...] + dbn_ref[...]) * nz,
                                 [-1, 1, 2])
        cw = cw_ref[...]
        du_ref[...] = (dp2 * cw[0:1] + dp1 * cw[1:2] + dout * cw[2:3] + dm1 * cw[3:4]).astype(MXU_DTYPE)
        u_t = u_ref[...]
        cst_ref[0:1, :] += _rowsum(dp2 * u_t)
        cst_ref[1:2, :] += _rowsum(dp1 * u_t)
        cst_ref[2:3, :] += _rowsum(dout * u_t)
        cst_ref[3:4, :] += _rowsum(dm1 * u_t)
        cst_ref[4:5, :] += _rowsum(dout)

    tile, prev, nxt = _lru_specs(s, tm, cb, lambda i: i, nt)
    return pl.pallas_call(
        body, name=name,
        out_shape=[jax.ShapeDtypeStruct((s, D_INNER), MXU_DTYPE), jax.ShapeDtypeStruct((SUBLANES, D_INNER), F32)],
        grid=(D_INNER // cb, nt),
        in_specs=[tile, prev, nxt] * 2 + [tile, pl.BlockSpec((4, cb), lambda c, i: (0, c))],
        out_specs=[tile, pl.BlockSpec((SUBLANES, cb), lambda c, i: (0, c))],
        compiler_params=_cparams(dimension_semantics=("arbitrary", "arbitrary")),
    )(duvf, duvf, duvf, duvb, duvb, duvb, u, conv_w)


def _bin(du, dg, xin, dzin, sc, sh, wg, name):
    t = xin.shape[0]
    tm = min(TM_MM, t)
    nt = t // tm
    has_g, has_dx = dg is not None, dzin is not None
    half = N_WBLK // 2
    n_blk = N_WBLK if has_g else half

    def body(*refs):
        refs = list(refs)
        du_ref = refs.pop(0)
        dg_ref = refs.pop(0) if has_g else None
        x_ref = refs.pop(0)
        dz_ref = refs.pop(0) if has_dx else None
        sc_ref, sh_ref, w_ref = refs.pop(0), refs.pop(0), refs.pop(0)
        dx_ref = refs.pop(0) if has_dx else None
        gw_hbm, st_ref, gw_acc, sem = refs
        i = pl.program_id(0)

        @pl.when(i == 0)
        def _():
            gw_acc[...] = jnp.zeros_like(gw_acc)
            st_ref[...] = jnp.zeros_like(st_ref)

        xv = x_ref[...]
        scale = 1.0 + sc_ref[...]
        h = (xv * scale + sh_ref[...]).astype(MXU_DTYPE)
        dh = None
        for k in range(n_blk):
            src = du_ref if k < half else dg_ref
            kk = k % half
            dk = src[:, kk * WBLK:(kk + 1) * WBLK]
            gw_acc[k] += _dot_tn(h, dk)
            contrib = _dot_nt(dk, w_ref[k])
            dh = contrib if dh is None else dh + contrib
        st_ref[0:1, :] += _rowsum(dh * xv)
        st_ref[1:2, :] += _rowsum(dh)
        if has_dx:
            dx_ref[...] = ALPHA * dz_ref[...] + dh * scale

        @pl.when(i == nt - 1)
        def _():
            _flush(gw_acc, gw_hbm, sem)

    wide = pl.BlockSpec((tm, D_INNER), lambda i: (i, 0))
    nar = pl.BlockSpec((tm, D_MODEL), lambda i: (i, 0))
    row = pl.BlockSpec((1, D_MODEL), lambda i: (0, 0))
    wspec = pl.BlockSpec((n_blk, D_MODEL, WBLK), lambda i: (0, 0, 0), pipeline_mode=pl.Buffered(1))
    in_specs = [wide] + ([wide] if has_g else []) + [nar] + ([nar] if has_dx else []) + [row, row, wspec]
    args = [du] + ([dg] if has_g else []) + [xin] + ([dzin] if has_dx else []) + [sc, sh, wg]
    out_shape = ([jax.ShapeDtypeStruct((t, D_MODEL), F32)] if has_dx else []) + [
        jax.ShapeDtypeStruct((n_blk, D_MODEL, WBLK), F32), jax.ShapeDtypeStruct((SUBLANES, D_MODEL), F32)]
    out_specs = ([nar] if has_dx else []) + [ANY, pl.BlockSpec((SUBLANES, D_MODEL), lambda i: (0, 0))]
    return pl.pallas_call(
        body, name=name, out_shape=out_shape, grid=(nt,), in_specs=in_specs, out_specs=out_specs,
        scratch_shapes=[pltpu.VMEM((n_blk, D_MODEL, WBLK), F32), pltpu.SemaphoreType.DMA(())],
        compiler_params=_cparams(dimension_semantics=("arbitrary",)),
    )(*args)


def _blocks_by_device(a, axis):
    shape = a.shape
    a = a.reshape(shape[:axis] + (N_DEV, shape[axis] // N_DEV) + shape[axis + 1:])
    return jnp.moveaxis(a, axis, 0)


def kernel(x, c, ctx, c_ctx, w_mod, b_mod, w_in, w_out, ln_g, ln_b, conv_w, conv_b, lru_wa, lru_ba, lru_wx, lru_bx, lru_lam, pool_w, pool_scale, loss_target, m_c_ctx, m_w_mod, m_b_mod, m_w_in, m_w_out, m_ln_g, m_ln_b, m_conv_w, m_conv_b, m_lru_wa, m_lru_ba, m_lru_wx, m_lru_bx, m_lru_lam, m_pool_w, m_pool_scale, v_c_ctx, v_w_mod, v_b_mod, v_w_in, v_w_out, v_ln_g, v_ln_b, v_conv_w, v_conv_b, v_lru_wa, v_lru_ba, v_lru_wx, v_lru_bx, v_lru_lam, v_pool_w, v_pool_scale):
    xi, yi, ci = _my_pos()
    dev = 4 * xi + 2 * yi + ci
    xt, ctxt, tgt = x[0], ctx[0], loss_target[0]
    n_mod = w_mod.shape[2]

    small_shapes = [(D_MODEL,), conv_w.shape[1:], lru_ba.shape[1:], lru_bx.shape[1:], lru_lam.shape[1:],
                    pool_scale.shape[1:]]
    small = _to_rows([c[0], conv_w[0], lru_ba[0], lru_bx[0], lru_lam[0], pool_scale[0]], SUBLANES)
    small_all, = _all_gather([small], "gather_small")
    pieces = [_split_rows(small_all[k], small_shapes) for k in range(N_DEV)]
    c_all = jnp.stack([p[0] for p in pieces])
    conv_w_f = jnp.concatenate([p[1] for p in pieces], axis=-1)
    lru_ba_f = jnp.concatenate([p[2] for p in pieces], axis=-1)[:, None, :]
    lru_bx_f = jnp.concatenate([p[3] for p in pieces], axis=-1)[:, None, :]
    lru_lam_f = jnp.concatenate([p[4] for p in pieces], axis=-1)[:, None, :]
    pool_scale_f = jnp.concatenate([p[5] for p in pieces], axis=-1)[None, :]

    cond = jnp.concatenate([c_all, jnp.broadcast_to(c_ctx[None, :], (N_DEV, D_MODEL))], axis=0)
    b_my = lax.dynamic_slice(b_mod, (0, dev * n_mod), (2, n_mod))[:, None, :]
    mod_part = _mod_fwd(cond, w_mod, b_my, "mod_fwd")
    mod_all, = _all_gather([mod_part], "gather_mod")
    mod = jnp.transpose(mod_all, (1, 2, 0, 3)).reshape(2, 16, 3 * D_MODEL)
    mod_me = lax.dynamic_slice(mod, (0, dev, 0), (2, 1, 3 * D_MODEL))
    sh = [mod_me[i, :, 0:D_MODEL] for i in range(2)]
    sc = [mod_me[i, :, D_MODEL:2 * D_MODEL] for i in range(2)]
    gt = [mod_me[i, :, 2 * D_MODEL:] for i in range(2)]
    shc, scc = mod[0, 8:9, 0:D_MODEL], mod[0, 8:9, D_MODEL:2 * D_MODEL]

    wi0, wi1, wo0, wo1, pool_w_g = _all_gather(
        [w_in[0].astype(MXU_DTYPE), w_in[1].astype(MXU_DTYPE), w_out[0].astype(MXU_DTYPE),
         w_out[1].astype(MXU_DTYPE), pool_w[0].astype(MXU_DTYPE)], "gather_weights")
    w_in_l = [wi0, wi1]
    w_out_l = [wo0.reshape(D_INNER, D_MODEL), wo1.reshape(D_INNER, D_MODEL)]
    pool_w_f = jnp.transpose(pool_w_g, (1, 0, 2, 3)).reshape(len(POOL_WINDOWS), POOL_GROUP, POOL_GROUP)
    lg = [ln_g[i][None, :] for i in range(2)]
    lb = [ln_b[i][None, :] for i in range(2)]
    lru_p = dict(conv_w=conv_w_f, conv_b=conv_b, wa=lru_wa[0].astype(MXU_DTYPE), wx=lru_wx[0].astype(MXU_DTYPE),
                 ba=lru_ba_f, bx=lru_bx_f, lam=lru_lam_f)
    zero_state = jnp.zeros((1, D_INNER), F32)

    u0, g0 = _in_proj(xt, sc[0], sh[0], w_in_l[0], "in_proj0")
    uc, _ = _in_proj(ctxt, scc, shc, w_in_l[0], "in_proj0_ctx")
    hcf, cf, uvc = _lru_fwd(uc, zero_state, lru_p, 0, "lru_fwd_ctx_f", conv=True)
    hcb, cbk = _lru_fwd(uvc, zero_state, lru_p, 1, "lru_fwd_ctx_b", conv=False)
    hf, _, uv0 = _lru_fwd(u0, cf, lru_p, 0, "lru_fwd_f", conv=True)
    hb, _ = _lru_fwd(uv0, cbk, lru_p, 1, "lru_fwd_b", conv=False)
    x1, br0 = _out0(hf, hb, g0, xt, gt[0], w_out_l[0], lg[0], lb[0], "out0")
    u1, g1 = _in_proj(x1, sc[1], sh[1], w_in_l[1], "in_proj1")
    dmix = _pool_mix(u1, False, MXU_DTYPE, "pool_fwd")
    dz1, st1 = _out1(dmix, pool_w_f, pool_scale_f, g1, x1, gt[1], w_out_l[1], lg[1], lb[1], tgt, "out1")
    loss_me = jnp.full((1, LANES), (0.5 / D_MODEL) * jnp.sum(st1[3]), F32)

    dd, dg1, gwo1, gpw, gps = _bout1(dz1, dmix, g1, pool_w_f, pool_scale_f, gt[1], w_out_l[1], "bwd_out1")
    du1 = _pool_mix(dd, True, MXU_DTYPE, "pool_bwd")
    dx1, gwi1, stb1 = _bin(du1, dg1, x1, dz1, sc[1], sh[1], w_in_l[1], "bwd_in1")
    dz0, dy0, dg0, gwo0, stl0 = _bout0(dx1, xt, br0, lg[0], hf, hb, g0, gt[0], w_out_l[0], "bwd_out0")
    duvf, gwa_f, gwx_f, gv_f, dh0f = _lru_bwd(uv0, dy0, hf, cf, zero_state, lru_p, 0, "lru_bwd_f")
    duvb, gwa_b, gwx_b, gv_b, dh0b = _lru_bwd(uv0, dy0, hb, cbk, zero_state, lru_p, 1, "lru_bwd_b")
    zero_dh = jnp.zeros_like(uc)
    ducf, gwa_cf, gwx_cf, gv_cf, _ = _lru_bwd(uvc, zero_dh, hcf, zero_state, dh0f, lru_p, 0, "lru_bwd_ctx_f")
    ducb, gwa_cb, gwx_cb, gv_cb, _ = _lru_bwd(uvc, zero_dh, hcb, zero_state, dh0b, lru_p, 1, "lru_bwd_ctx_b")
    du0, cst0 = _conv_bwd(duvf, duvb, u0, conv_w_f, "conv_bwd")
    duc, cstc = _conv_bwd(ducf, ducb, uc, conv_w_f, "conv_bwd_ctx")
    gx, gwi0, stb0 = _bin(du0, dg0, xt, dz0, sc[0], sh[0], w_in_l[0], "bwd_in0")
    gwic, stc = _bin(duc, None, ctxt, None, scc, shc, w_in_l[0][:N_WBLK // 2], "bwd_in0_ctx")
    gwi0 = gwi0.at[:N_WBLK // 2].add(gwic)

    zero_row = jnp.zeros((1, D_MODEL), F32)
    dm_me = jnp.stack([
        jnp.concatenate([jnp.concatenate([stb0[1:2], stb0[0:1], stl0[2:3]], axis=1),
                         jnp.concatenate([stc[1:2], stc[0:1], zero_row], axis=1)], axis=0),
        jnp.concatenate([jnp.concatenate([stb1[1:2], stb1[0:1], st1[2:3]], axis=1),
                         jnp.zeros((1, 3 * D_MODEL), F32)], axis=0)])
    dm_g, loss_g = _all_gather([dm_me, loss_me], "gather_dmod")
    loss = jnp.sum(loss_g[:, 0, 0])
    dm_all = jnp.concatenate([jnp.transpose(dm_g[:, :, 0], (1, 0, 2)), jnp.transpose(dm_g[:, :, 1], (1, 0, 2))],
                             axis=1)
    dm_my = lax.dynamic_slice(dm_all, (0, 0, dev * n_mod), (2, 16, n_mod))
    g_w_mod, g_b_mod, gcc_part = _mod_bwd(cond, dm_all, dm_my, w_mod, "mod_bwd")
    g_b_mod = g_b_mod.reshape(b_mod.shape)

    gwa = jnp.stack([gwa_f + gwa_cf, gwa_b + gwa_cb])
    gwx = jnp.stack([gwx_f + gwx_cf, gwx_b + gwx_cb])
    gv = jnp.stack([gv_f + gv_cf, gv_b + gv_cb])
    cst = cst0 + cstc
    g_ln_g = jnp.stack([stl0[0], st1[0]])
    g_ln_b = jnp.stack([stl0[1], st1[1]])
    sharded = [
        _blocks_by_device(cst[0:4], 1),
        _blocks_by_device(gv[:, 0], 1), _blocks_by_device(gv[:, 1], 1), _blocks_by_device(gv[:, 2], 1),
        _blocks_by_device(gps[0], 0),
    ]
    replicated = [gwa.reshape(-1), gwx.reshape(-1), g_ln_g.reshape(-1), g_ln_b.reshape(-1), cst[4],
                  gcc_part.reshape(-1)]
    sh_sizes = [int(np.prod(a.shape[1:])) for a in sharded]
    rep_sizes = [a.shape[0] // N_DEV for a in replicated]
    n_flat = sum(sh_sizes) + sum(rep_sizes)
    rows = -(-n_flat // LANES)
    rows = -(-rows // FLAT_ROWS) * FLAT_ROWS
    misc = jnp.concatenate([a.reshape(N_DEV, -1) for a in sharded] +
                           [a.reshape(N_DEV, -1) for a in replicated], axis=1)
    misc = jnp.pad(misc, ((0, 0), (0, rows * LANES - n_flat))).reshape(N_DEV, rows, LANES)
    bufs = [gwi0, gwi1, gwo0.reshape(N_DEV, D_INNER // N_DEV, D_MODEL), gwo1.reshape(N_DEV, D_INNER // N_DEV, D_MODEL),
            _blocks_by_device(gpw, 1).reshape(N_DEV, POOL_GROUP // N_DEV * len(POOL_WINDOWS), POOL_GROUP), misc]
    names_r = ["w_in0", "w_in1", "w_out0", "w_out1", "pool_w", "misc"]
    recvs = _sibling_exchange(bufs, "reduce_sibling")
    core = jnp.reshape(ci, (1,)).astype(jnp.int32)
    pairs = [_pair_sum(b, r, core, "reduce_pair_" + n) for b, r, n in zip(bufs, recvs, names_r)]
    p_wi0, p_wi1, p_wo0, p_wo1, p_pw, p_misc = _chip_exchange(pairs, "reduce_chips")
    g_flat = _sum4(p_misc, "reduce_sum_misc").reshape(-1)

    offs = np.cumsum([0] + sh_sizes + rep_sizes)
    n_sh = len(sh_sizes)
    sh_shapes = [conv_w.shape, lru_ba.shape, lru_bx.shape, lru_lam.shape, pool_scale.shape]
    g_sh = [g_flat[offs[k]:offs[k + 1]].reshape(sh_shapes[k]) for k in range(n_sh)]
    g_conv_w, g_lru_ba, g_lru_bx, g_lru_lam, g_pool_scale = g_sh
    rep_block = _to_rows([g_flat[offs[n_sh]:offs[-1]]], SUBLANES)
    rep_all, = _all_gather([rep_block], "gather_replicated")
    rep_flat = rep_all.reshape(N_DEV, -1)
    rep_full, off = [], 0
    for n in rep_sizes:
        rep_full.append(rep_flat[:, off:off + n].reshape(-1))
        off += n
    g_lru_wa = rep_full[0].reshape(lru_wa.shape)
    g_lru_wx = rep_full[1].reshape(lru_wx.shape)
    g_ln_g = rep_full[2].reshape(ln_g.shape)
    g_ln_b = rep_full[3].reshape(ln_b.shape)
    g_conv_b = rep_full[4].reshape(conv_b.shape)
    g_c_ctx = rep_full[5].reshape(c_ctx.shape)

    names = ["c_ctx", "w_mod", "b_mod", "w_in", "w_out", "ln_g", "ln_b", "conv_w", "conv_b", "lru_wa", "lru_ba",
             "lru_wx", "lru_bx", "lru_lam", "pool_w", "pool_scale"]
    weights = dict(c_ctx=c_ctx, w_mod=w_mod, b_mod=b_mod, w_in=w_in, w_out=w_out, ln_g=ln_g, ln_b=ln_b,
                   conv_w=conv_w, conv_b=conv_b, lru_wa=lru_wa, lru_ba=lru_ba, lru_wx=lru_wx, lru_bx=lru_bx,
                   lru_lam=lru_lam, pool_w=pool_w, pool_scale=pool_scale)
    mom_m = dict(c_ctx=m_c_ctx, w_mod=m_w_mod, b_mod=m_b_mod, w_in=m_w_in, w_out=m_w_out, ln_g=m_ln_g, ln_b=m_ln_b,
                 conv_w=m_conv_w, conv_b=m_conv_b, lru_wa=m_lru_wa, lru_ba=m_lru_ba, lru_wx=m_lru_wx,
                 lru_bx=m_lru_bx, lru_lam=m_lru_lam, pool_w=m_pool_w, pool_scale=m_pool_scale)
    mom_v = dict(c_ctx=v_c_ctx, w_mod=v_w_mod, b_mod=v_b_mod, w_in=v_w_in, w_out=v_w_out, ln_g=v_ln_g, ln_b=v_ln_b,
                 conv_w=v_conv_w, conv_b=v_conv_b, lru_wa=v_lru_wa, lru_ba=v_lru_ba, lru_wx=v_lru_wx,
                 lru_bx=v_lru_bx, lru_lam=v_lru_lam, pool_w=v_pool_w, pool_scale=v_pool_scale)
    grads = dict(c_ctx=g_c_ctx, w_mod=g_w_mod, b_mod=g_b_mod, ln_g=g_ln_g, ln_b=g_ln_b,
                 conv_w=g_conv_w, conv_b=g_conv_b, lru_wa=g_lru_wa, lru_ba=g_lru_ba, lru_wx=g_lru_wx,
                 lru_bx=g_lru_bx, lru_lam=g_lru_lam)
    grads["pool_scale"] = g_pool_scale
    delta, new_m, new_v = {}, {}, {}

    def update_parts(n, parts, view):
        res = _adamw_parts(weights[n].reshape(view), parts, mom_m[n].reshape(view), mom_v[n].reshape(view),
                           "adamw_" + n)
        grads[n], delta[n], new_m[n], new_v[n] = [r.reshape(weights[n].shape) for r in res]

    update_parts("w_in", [p_wi0, p_wi1], w_in.shape)
    update_parts("w_out", [p_wo0, p_wo1], w_out.shape)
    update_parts("pool_w", [p_pw], (1,) + p_pw.shape[1:])
    for n in ("w_mod", "lru_wa", "lru_wx"):
        shape = weights[n].shape
        view = (int(np.prod(shape[:-1])), shape[-1])
        res = _adamw(weights[n].reshape(view), grads[n].reshape(view), mom_m[n].reshape(view),
                     mom_v[n].reshape(view), "adamw_" + n)
        delta[n], new_m[n], new_v[n] = [r.reshape(shape) for r in res]

    small = [n for n in names if n not in delta]
    shapes = [weights[n].shape for n in small]
    flat = lambda d: _to_rows([d[n] for n in small], FLAT_ROWS)
    res = _adamw(flat(weights), flat(grads), flat(mom_m), flat(mom_v), "adamw_small")
    for d, r in zip((delta, new_m, new_v), res):
        d.update(zip(small, _split_rows(r, shapes)))

    return (loss, gx[None], *[grads[n] for n in names], *[delta[n] for n in names],
            *[new_m[n] for n in names], *[new_v[n] for n in names])
```

```python
import functools

import numpy as np
import jax
import jax.numpy as jnp
from jax import lax
from jax.experimental import pallas as pl
from jax.experimental.pallas import tpu as pltpu

F32 = jnp.float32
BF16 = jnp.bfloat16
MXU_DTYPE = BF16

D_MODEL = 1024
D_INNER = 2048
LRU_BLOCK = 128
GRID_W = 64
POOL_WINDOWS = (2, 4, 8, 16)
POOL_GROUP = 512
ALPHA = float(4 ** 0.25)
LN_EPS = 1e-5
LRU_C = 8.0
N_DEV = 8
N_WBLK = 8
WBLK = 512

ADAM_LR = 0.001
ADAM_B1 = 0.9
ADAM_B2 = 0.999
ADAM_EPS = 1e-08
ADAM_WD = 0.01
ADAM_STEP = 10

LANES = 128
SUBLANES = 8
V7X_VMEM_BYTES = 64 * 1024 * 1024
VMEM_LIMIT = V7X_VMEM_BYTES - 8 * 1024 * 1024
MESH = pl.DeviceIdType.MESH
ANY = pl.BlockSpec(memory_space=pl.ANY)

TM_MM = 512
TM_BWD = 256
TM_LRU = 512
CB_LRU = 512
N_SEG = 8
SCAN_UNROLL = 4
SQRT_FLOOR = 1e-30
FLAT_ROWS = 16
ELEMENTWISE_TILE_BYTES = 1 << 20
POOL_TOK = 256
WIRE_DTYPE = BF16


def _cparams(**kw):
    return pltpu.CompilerParams(vmem_limit_bytes=VMEM_LIMIT, **kw)


def _my_pos():
    return lax.axis_index("x"), lax.axis_index("y"), lax.axis_index("c")


def _dot(a, b):
    return jnp.dot(a.astype(MXU_DTYPE), b.astype(MXU_DTYPE), preferred_element_type=F32)


def _dot_tn(a, b):
    return lax.dot_general(a.astype(MXU_DTYPE), b.astype(MXU_DTYPE), (((0,), (0,)), ((), ())),
                           preferred_element_type=F32)


def _dot_nt(a, b):
    return lax.dot_general(a.astype(MXU_DTYPE), b.astype(MXU_DTYPE), (((1,), (1,)), ((), ())),
                           preferred_element_type=F32)


def _sigmoid(z):
    return 0.5 * jnp.tanh(0.5 * z) + 0.5


def _log_sigmoid(x):
    y = jnp.exp(-jnp.abs(x))
    u = 1.0 + y
    l1p = jnp.where(u == 1.0, y, jnp.log(u) * (y / jnp.where(u == 1.0, 1.0, u - 1.0)))
    return jnp.minimum(x, 0.0) - l1p


def _rowsum(v):
    return jnp.sum(v, axis=0, keepdims=True)


def _layer_norm_stats(z):
    mu = jnp.mean(z, axis=-1, keepdims=True)
    zc = z - mu
    var = jnp.mean(zc * zc, axis=-1, keepdims=True)
    rstd = lax.rsqrt(var + LN_EPS)
    return zc * rstd, rstd


def _layer_norm_bwd(dy, xhat, rstd, g):
    dxh = dy * g
    m1 = jnp.mean(dxh, axis=-1, keepdims=True)
    m2 = jnp.mean(dxh * xhat, axis=-1, keepdims=True)
    return rstd * (dxh - m1 - xhat * m2)


def _shifted(v, before8, after8, offsets):
    n = v.shape[0]
    ext = jnp.concatenate([before8, v, after8], axis=0)
    total = n + 2 * SUBLANES
    return [pltpu.roll(ext, (-k) % total, 0)[SUBLANES:SUBLANES + n] for k in offsets]


def _rows8(row):
    return jnp.broadcast_to(row, (SUBLANES, row.shape[1]))


def _shift_down(v, first_row):
    return _shifted(v, _rows8(first_row), _rows8(first_row), [-1])[0]


def _shift_up(v, last_row):
    return _shifted(v, _rows8(last_row), _rows8(last_row), [1])[0]


def _all_gather(blocks, name):
    n = len(blocks)

    def body(*refs):
        x_refs, out_refs = refs[:n], refs[n:2 * n]
        send_sems, recv_sems, local_sems = refs[2 * n:]
        x, y, c = _my_pos()
        me, sibling = (x, y, c), (x, y, 1 - c)
        chips = [(1 - x, y), (x, 1 - y), (1 - x, 1 - y)]

        def slot(a, px, py, pc):
            return out_refs[a].at[4 * px + 2 * py + pc]

        def copy(a, k, block, to, src=None):
            return pltpu.make_async_remote_copy(
                src_ref=slot(a, *block) if src is None else src, dst_ref=slot(a, *block),
                send_sem=send_sems.at[a, k], recv_sem=recv_sems.at[a, k], device_id=to, device_id_type=MESH)

        mine = [pltpu.make_async_copy(x_refs[a], slot(a, *me), local_sems.at[a]) for a in range(n)]
        for cp in mine:
            cp.start()
        first = []
        for a in range(n):
            first.append(copy(a, 0, me, sibling, src=x_refs[a]))
            first += [copy(a, 1 + j, me, (*chip, c), src=x_refs[a]) for j, chip in enumerate(chips)]
        for cp in first:
            cp.start()
        passed = []
        for j, chip in enumerate(chips):
            for a in range(n):
                copy(a, 1 + j, (*chip, c), me).wait_recv()
                fwd = copy(a, 4 + j, (*chip, c), sibling)
                fwd.start()
                passed.append(fwd)
        for a in range(n):
            copy(a, 0, sibling, me).wait_recv()
            for j, chip in enumerate(chips):
                copy(a, 4 + j, (*chip, 1 - c), me).wait_recv()
        for cp in first + passed:
            cp.wait_send()
        for cp in mine:
            cp.wait()

    outs = pl.pallas_call(
        body, name=name,
        out_shape=[jax.ShapeDtypeStruct((N_DEV,) + b.shape, b.dtype) for b in blocks],
        in_specs=[ANY] * n, out_specs=[ANY] * n,
        scratch_shapes=[pltpu.SemaphoreType.DMA((n, 7)), pltpu.SemaphoreType.DMA((n, 7)),
                        pltpu.SemaphoreType.DMA((n,))],
    )(*blocks)
    return list(outs)


def _sibling_exchange(bufs, name):
    n = len(bufs)

    def body(*refs):
        srcs, outs = refs[:n], refs[n:2 * n]
        send_sems, recv_sems = refs[2 * n:]
        x, y, c = _my_pos()
        copies = [pltpu.make_async_remote_copy(
            src_ref=srcs[a].at[2 * j + (1 - c)], dst_ref=outs[a].at[j], send_sem=send_sems.at[a, j],
            recv_sem=recv_sems.at[a, j], device_id=(x, y, 1 - c), device_id_type=MESH)
            for a in range(n) for j in range(4)]
        for cp in copies:
            cp.start()
        for cp in copies:
            cp.wait()

    outs = pl.pallas_call(
        body, name=name, out_shape=[jax.ShapeDtypeStruct((4,) + b.shape[1:], b.dtype) for b in bufs],
        in_specs=[ANY] * n, out_specs=[ANY] * n,
        scratch_shapes=[pltpu.SemaphoreType.DMA((n, 4)), pltpu.SemaphoreType.DMA((n, 4))],
    )(*bufs)
    return list(outs)


def _chip_exchange(parts, name):
    n = len(parts)

    def body(*refs):
        srcs, outs = refs[:n], refs[n:2 * n]
        send_sems, recv_sems, local_sems = refs[2 * n:]
        x, y, c = _my_pos()
        jme = 2 * x + y
        peers = [(1 - x, y), (x, 1 - y), (1 - x, 1 - y)]
        local = [pltpu.make_async_copy(srcs[a].at[jme], outs[a].at[jme], local_sems.at[a]) for a in range(n)]
        for cp in local:
            cp.start()

        def copy(a, k, px, py, dst_slot):
            return pltpu.make_async_remote_copy(
                src_ref=srcs[a].at[2 * px + py], dst_ref=outs[a].at[dst_slot], send_sem=send_sems.at[a, k],
                recv_sem=recv_sems.at[a, k], device_id=(px, py, c), device_id_type=MESH)

        sends = [copy(a, k, px, py, jme) for a in range(n) for k, (px, py) in enumerate(peers)]
        for cp in sends:
            cp.start()
        for a in range(n):
            for k, (px, py) in enumerate(peers):
                copy(a, k, px, py, 2 * px + py).wait_recv()
        for cp in sends:
            cp.wait_send()
        for cp in local:
            cp.wait()

    outs = pl.pallas_call(
        body, name=name, out_shape=[jax.ShapeDtypeStruct(p.shape, p.dtype) for p in parts],
        in_specs=[ANY] * n, out_specs=[ANY] * n,
        scratch_shapes=[pltpu.SemaphoreType.DMA((n, 3)), pltpu.SemaphoreType.DMA((n, 3)),
                        pltpu.SemaphoreType.DMA((n,))],
    )(*parts)
    return list(outs)


_SIDE_REMOTE = {"gather": 7, "sibling": 4, "chips": 3}
_FLIPS = [(0, 0, 1), (1, 0, 0), (0, 1, 0), (1, 1, 0), (1, 0, 1), (0, 1, 1), (1, 1, 1)]


def _side_plan(sides):
    inputs, out_shapes, scratch = [], [], []
    for kind, arrays in sides:
        n = len(arrays)
        for a in arrays:
            inputs.append(a)
            shape = {"gather": (N_DEV,) + a.shape, "sibling": (4,) + a.shape[1:], "chips": a.shape}[kind]
            out_shapes.append(jax.ShapeDtypeStruct(shape, a.dtype))
        scratch += [pltpu.SemaphoreType.DMA((n, _SIDE_REMOTE[kind])), pltpu.SemaphoreType.DMA((n, _SIDE_REMOTE[kind])),
                    pltpu.SemaphoreType.DMA((n,))]
    return inputs, out_shapes, scratch


def _side_copies(sides, in_refs, out_refs, sem_refs):
    x, y, c = _my_pos()
    starts, waits = [], []
    pos = 0
    for s, (kind, arrays) in enumerate(sides):
        send_sems, recv_sems, local_sems = sem_refs[3 * s:3 * s + 3]
        for a in range(len(arrays)):
            src, out = in_refs[pos], out_refs[pos]
            pos += 1

            def remote(k, src_ref, dst_ref, to):
                return pltpu.make_async_remote_copy(src_ref=src_ref, dst_ref=dst_ref, send_sem=send_sems.at[a, k],
                                                    recv_sem=recv_sems.at[a, k], device_id=to, device_id_type=MESH)

            def local(src_ref, dst_ref):
                cp = pltpu.make_async_copy(src_ref, dst_ref, local_sems.at[a])
                starts.append(cp.start)
                waits.append(cp.wait)

            if kind == "gather":
                me = 4 * x + 2 * y + c
                local(src, out.at[me])
                for k, (fx, fy, fc) in enumerate(_FLIPS):
                    px, py, pc = (1 - x if fx else x), (1 - y if fy else y), (1 - c if fc else c)
                    send = remote(k, src, out.at[me], (px, py, pc))
                    starts.append(send.start)
                    waits += [remote(k, src, out.at[4 * px + 2 * py + pc], (px, py, pc)).wait_recv, send.wait_send]
            elif kind == "sibling":
                for j in range(4):
                    cp = remote(j, src.at[2 * j + (1 - c)], out.at[j], (x, y, 1 - c))
                    starts.append(cp.start)
                    waits.append(cp.wait)
            else:
                jme = 2 * x + y
                local(src.at[jme], out.at[jme])
                for k, (px, py) in enumerate([(1 - x, y), (x, 1 - y), (1 - x, 1 - y)]):
                    send = remote(k, src.at[2 * px + py], out.at[jme], (px, py, c))
                    starts.append(send.start)
                    waits += [remote(k, src.at[2 * px + py], out.at[2 * px + py], (px, py, c)).wait_recv,
                              send.wait_send]
    return starts, waits


def _call_with_sides(body, sides, *, name, grid, in_specs, out_specs, out_shape, scratch_shapes, compiler_params, args):
    if not sides:
        res = pl.pallas_call(body, name=name, grid=grid, in_specs=in_specs, out_specs=out_specs, out_shape=out_shape,
                             scratch_shapes=scratch_shapes, compiler_params=compiler_params)(*args)
        return list(res), []
    s_in, s_out, s_scr = _side_plan(sides)
    n_in, n_out, n_scr, n_side = len(in_specs), len(out_specs), len(scratch_shapes), len(s_in)

    def wrapped(*refs):
        refs = list(refs)
        ins, side_in = refs[:n_in], refs[n_in:n_in + n_side]
        outs = refs[n_in + n_side:n_in + n_side + n_out]
        side_out = refs[n_in + n_side + n_out:n_in + 2 * n_side + n_out]
        rest = refs[n_in + 2 * n_side + n_out:]
        starts, waits = _side_copies(sides, side_in, side_out, rest[n_scr:])
        first = functools.reduce(jnp.logical_and, [pl.program_id(d) == 0 for d in range(len(grid))])
        last = functools.reduce(jnp.logical_and, [pl.program_id(d) == grid[d] - 1 for d in range(len(grid))])

        @pl.when(first)
        def _():
            for start in starts:
                start()

        body(*ins, *outs, *rest[:n_scr])

        @pl.when(last)
        def _():
            for wait in waits:
                wait()

    res = pl.pallas_call(
        wrapped, name=name, grid=grid, in_specs=list(in_specs) + [ANY] * n_side,
        out_specs=list(out_specs) + [ANY] * n_side, out_shape=list(out_shape) + s_out,
        scratch_shapes=list(scratch_shapes) + s_scr, compiler_params=compiler_params,
    )(*args, *s_in)
    return list(res[:n_out]), list(res[n_out:])


def _row_tile(r, l):
    t = min(r, max(16, ELEMENTWISE_TILE_BYTES // (4 * l) // 16 * 16))
    while r % t:
        t -= 16
    return t


def _pair_sum(buf, recv, core, name):
    _, r, l = buf.shape
    tr = _row_tile(r, l)

    def body(core_ref, a_ref, b_ref, o_ref):
        o_ref[...] = (a_ref[...] + b_ref[...]).astype(WIRE_DTYPE)

    return pl.pallas_call(
        body, name=name, out_shape=jax.ShapeDtypeStruct((4, r, l), WIRE_DTYPE),
        grid_spec=pltpu.PrefetchScalarGridSpec(
            num_scalar_prefetch=1, grid=(4, r // tr),
            in_specs=[pl.BlockSpec((None, tr, l), lambda j, i, cr: (2 * j + cr[0], i, 0)),
                      pl.BlockSpec((None, tr, l), lambda j, i, cr: (j, i, 0))],
            out_specs=pl.BlockSpec((None, tr, l), lambda j, i, cr: (j, i, 0))),
        compiler_params=_cparams(dimension_semantics=("arbitrary", "arbitrary")),
    )(core, buf, recv)


def _sum_parts(p_ref):
    return ((p_ref[0].astype(F32) + p_ref[1].astype(F32)) + (p_ref[2].astype(F32) + p_ref[3].astype(F32)))


def _sum4(parts, name):
    _, r, l = parts.shape
    tr = _row_tile(r, l)

    def body(p_ref, o_ref):
        o_ref[...] = _sum_parts(p_ref)

    return pl.pallas_call(
        body, name=name, out_shape=jax.ShapeDtypeStruct((r, l), F32), grid=(r // tr,),
        in_specs=[pl.BlockSpec((4, tr, l), lambda i: (0, i, 0))],
        out_specs=pl.BlockSpec((tr, l), lambda i: (i, 0)),
        compiler_params=_cparams(dimension_semantics=("arbitrary",)),
    )(parts)


def _adamw_update(w, gg, m, v):
    nm = ADAM_B1 * m + (1.0 - ADAM_B1) * gg
    nv = ADAM_B2 * v + (1.0 - ADAM_B2) * (gg * gg)
    m_hat = nm / (1.0 - ADAM_B1 ** ADAM_STEP)
    v_hat = nv / (1.0 - ADAM_B2 ** ADAM_STEP)
    return -ADAM_LR * (m_hat / (jnp.sqrt(v_hat) + ADAM_EPS) + ADAM_WD * w), nm, nv


def _adamw(w, g, m, v, name):
    r, l = w.shape
    tr = _row_tile(r, l)

    def body(w_ref, g_ref, m_ref, v_ref, d_ref, nm_ref, nv_ref):
        d_ref[...], nm_ref[...], nv_ref[...] = _adamw_update(w_ref[...], g_ref[...], m_ref[...], v_ref[...])

    spec = pl.BlockSpec((tr, l), lambda i: (i, 0))
    return pl.pallas_call(
        body, name=name, out_shape=[jax.ShapeDtypeStruct((r, l), F32)] * 3, grid=(r // tr,),
        in_specs=[spec] * 4, out_specs=[spec] * 3,
        compiler_params=_cparams(dimension_semantics=("arbitrary",)),
    )(w, g, m, v)


def _adamw_parts(w, parts, m, v, name):
    nl, r, l = w.shape
    tr = _row_tile(r, l)

    def body(*refs):
        w_ref, p_refs, (m_ref, v_ref, g_ref, d_ref, nm_ref, nv_ref) = refs[0], refs[1:1 + nl], refs[1 + nl:]
        layer = pl.program_id(0)
        gg = _sum_parts(p_refs[0])
        for q in range(1, nl):
            gg = jnp.where(layer == q, _sum_parts(p_refs[q]), gg)
        g_ref[...] = gg
        d_ref[...], nm_ref[...], nv_ref[...] = _adamw_update(w_ref[...], gg, m_ref[...], v_ref[...])

    spec = pl.BlockSpec((None, tr, l), lambda q, i: (q, i, 0))
    pspecs = [pl.BlockSpec((4, tr, l), lambda q, i, k=k: (0, jnp.where(q == k, i, 0), 0)) for k in range(nl)]
    return pl.pallas_call(
        body, name=name, out_shape=[jax.ShapeDtypeStruct((nl, r, l), F32)] * 4, grid=(nl, r // tr),
        in_specs=[spec] + pspecs + [spec, spec], out_specs=[spec] * 4,
        compiler_params=_cparams(dimension_semantics=("arbitrary", "arbitrary")),
    )(w, *parts, m, v)


def _to_rows(pieces, row_multiple):
    flat = jnp.concatenate([p.reshape(-1) for p in pieces])
    rows = -(-flat.shape[0] // LANES)
    rows = -(-rows // row_multiple) * row_multiple
    flat = jnp.pad(flat, (0, rows * LANES - flat.shape[0]))
    return flat.reshape(rows, LANES)


def _split_rows(rows, shapes):
    flat = rows.reshape(-1)
    out, off = [], 0
    for s in shapes:
        n = int(np.prod(s))
        out.append(flat[off:off + n].reshape(s))
        off += n
    return out


def _mod_fwd(cond, w_mod, b_my, name):
    nl, _, ncol = w_mod.shape

    def body(a_ref, w_ref, b_ref, o_ref):
        a = a_ref[...]
        s = a * _sigmoid(a)
        for i in range(nl):
            o_ref[i] = _dot(s, w_ref[i]) + b_ref[i]

    return pl.pallas_call(
        body, name=name, out_shape=jax.ShapeDtypeStruct((nl, 16, ncol), F32),
        compiler_params=_cparams(),
    )(cond, w_mod, b_my)


def _mod_bwd(cond, dm_all, dm_my, w_mod, name):
    nl, _, ncol = w_mod.shape

    def body(a_ref, dma_ref, dmm_ref, w_ref, gw_ref, gb_ref, gc_ref):
        a = a_ref[...]
        sg = _sigmoid(a)
        s = a * sg
        for i in range(nl):
            gw_ref[i] = _dot_tn(s, dmm_ref[i])
            gb_ref[i] = jnp.sum(dma_ref[i], axis=0, keepdims=True)
        back = _dot_nt(dmm_ref[0], w_ref[0])
        dsilu = sg * (1.0 + a * (1.0 - sg))
        gc_ref[...] = jnp.sum(back[8:16] * dsilu[8:16], axis=0, keepdims=True)

    return pl.pallas_call(
        body, name=name,
        out_shape=[jax.ShapeDtypeStruct((nl, D_MODEL, ncol), F32), jax.ShapeDtypeStruct((nl, 1, 3 * D_MODEL), F32),
                   jax.ShapeDtypeStruct((1, D_MODEL), F32)],
        compiler_params=_cparams(),
    )(cond, dm_all, dm_my, w_mod)


def _in_proj(xt, sc, sh, wg, name):
    t = xt.shape[0]
    tm = min(TM_MM, t)

    def body(x_ref, sc_ref, sh_ref, w_ref, u_ref, g_ref):
        h = (x_ref[...] * (1.0 + sc_ref[...]) + sh_ref[...]).astype(MXU_DTYPE)
        for k in range(N_WBLK):
            o = jnp.dot(h, w_ref[k], preferred_element_type=F32)
            if k < N_WBLK // 2:
                u_ref[:, k * WBLK:(k + 1) * WBLK] = o
            else:
                kk = k - N_WBLK // 2
                g_ref[:, kk * WBLK:(kk + 1) * WBLK] = o

    row = pl.BlockSpec((1, D_MODEL), lambda i: (0, 0))
    return pl.pallas_call(
        body, name=name, out_shape=[jax.ShapeDtypeStruct((t, D_INNER), F32)] * 2, grid=(t // tm,),
        in_specs=[pl.BlockSpec((tm, D_MODEL), lambda i: (i, 0)), row, row,
                  pl.BlockSpec((N_WBLK, D_MODEL, WBLK), lambda i: (0, 0, 0), pipeline_mode=pl.Buffered(1))],
        out_specs=[pl.BlockSpec((tm, D_INNER), lambda i: (i, 0))] * 2,
        compiler_params=_cparams(dimension_semantics=("arbitrary",)),
    )(xt, sc, sh, wg)


def _halo_maps(nt, tm, n_rows8, pos):
    per = tm // SUBLANES
    prev = lambda cb, i: (jnp.maximum(pos(i) * per - 1, 0), cb)
    nxt = lambda cb, i: (jnp.minimum((pos(i) + 1) * per, n_rows8 - 1), cb)
    return prev, nxt


def _conv_taps(u, prev8, next8, is_first, is_last):
    pz = jnp.where(is_first, 0.0, 1.0)
    nz = jnp.where(is_last, 0.0, 1.0)
    return _shifted(u, prev8 * pz, next8 * nz, [-2, -1, 1])


def _lru_gates(uv, wa_ref, wx_ref, ba, bx, cl, g):
    sl = slice(g * LANES, (g + 1) * LANES)
    uvg = uv[:, sl]
    r = _sigmoid(_dot(uvg, wa_ref[g]) + ba[:, sl])
    ii = _sigmoid(_dot(uvg, wx_ref[g]) + bx[:, sl])
    la = cl[:, sl] * r
    a = jnp.exp(la)
    q = jnp.tanh(-la) * (1.0 + a * a)
    rs = lax.rsqrt(jnp.maximum(q, SQRT_FLOOR))
    return uvg, r, ii, a, q * rs, rs


def _scan_tile(a_s, b_s, carry_ref, write_out, seg, reverse):
    n_g = a_s.shape[0]
    stride = a_s.shape[1] // N_SEG

    unroll = SCAN_UNROLL if seg % SCAN_UNROLL == 0 else 1

    def steps(k, state):
        hs, cs = list(state[0]), list(state[1])
        for q in range(unroll):
            t = k * unroll + q
            if reverse:
                t = seg - 1 - t
            for g in range(n_g):
                a = a_s[g, pl.ds(t, N_SEG, stride=stride), :]
                b = b_s[g, pl.ds(t, N_SEG, stride=stride), :]
                hs[g] = a * hs[g] + b
                cs[g] = a * cs[g]
                b_s[g, pl.ds(t, N_SEG, stride=stride), :] = hs[g]
                a_s[g, pl.ds(t, N_SEG, stride=stride), :] = cs[g]
        return tuple(hs), tuple(cs)

    zeros = tuple(jnp.zeros((N_SEG, LANES), F32) for _ in range(n_g))
    ones = tuple(jnp.ones((N_SEG, LANES), F32) for _ in range(n_g))
    h_fin, a_fin = lax.fori_loop(0, seg // unroll, steps, (zeros, ones))

    order = list(range(N_SEG - 1, -1, -1)) if reverse else list(range(N_SEG))
    for g in range(n_g):
        carry = carry_ref[:, g * LANES:(g + 1) * LANES]
        for j in order:
            rows = pl.ds(j * stride, seg)
            write_out(j, g, b_s[g, rows, :] + a_s[g, rows, :] * carry)
            carry = a_fin[g][j:j + 1] * carry + h_fin[g][j:j + 1]
        carry_ref[:, g * LANES:(g + 1) * LANES] = carry


def _lru_specs(s, tm, cb, direction_pos, nt):
    n_rows8 = s // SUBLANES
    prev, nxt = _halo_maps(nt, tm, n_rows8, direction_pos)
    tile = pl.BlockSpec((tm, cb), lambda c, i: (direction_pos(i), c))
    return tile, pl.BlockSpec((SUBLANES, cb), prev), pl.BlockSpec((SUBLANES, cb), nxt)


def _lru_param_specs(cb, d):
    n_g = cb // LANES
    vec = pl.BlockSpec((1, cb), lambda c, i: (0, c))
    dvec = pl.BlockSpec((None, 1, cb), lambda c, i: (d, 0, c))
    wmat = pl.BlockSpec((None, n_g, LRU_BLOCK, LRU_BLOCK), lambda c, i: (d, c, 0, 0))
    return vec, dvec, wmat


def _lru_fwd(src, h0, p, d, name, conv, sides=()):
    s = src.shape[0]
    tm = min(TM_LRU, s)
    cb = CB_LRU
    n_g = cb // LANES
    nt = s // tm
    seg = tm // N_SEG
    stride = seg + SUBLANES
    pos = (lambda i: i) if d == 0 else (lambda i: nt - 1 - i)

    def body(*refs):
        refs = list(refs)
        u_ref = refs.pop(0)
        if conv:
            up_ref, un_ref, cw_ref, cbias_ref = [refs.pop(0) for _ in range(4)]
        wa_ref, wx_ref, ba_ref, bx_ref, lam_ref, h0_ref, h_ref, hc_ref = [refs.pop(0) for _ in range(8)]
        uv_ref = refs.pop(0) if conv else None
        a_s, b_s = refs
        i = pl.program_id(1)
        tp = pos(i)

        @pl.when(i == 0)
        def _():
            hc_ref[...] = h0_ref[...]

        uv = u_ref[...]
        if conv:
            um2, um1, up1 = _conv_taps(uv, up_ref[...], un_ref[...], tp == 0, tp == nt - 1)
            cw = cw_ref[...]
            uv = um2 * cw[0:1] + um1 * cw[1:2] + uv * cw[2:3] + up1 * cw[3:4] + cbias_ref[...]
            uv_ref[...] = uv
        cl = LRU_C * _log_sigmoid(lam_ref[...])
        ba, bx = ba_ref[...], bx_ref[...]
        for g in range(n_g):
            uvg, r, ii, a, sq, _ = _lru_gates(uv, wa_ref, wx_ref, ba, bx, cl, g)
            b = sq * (ii * uvg)
            for j in range(N_SEG):
                a_s[g, pl.ds(j * stride, seg), :] = a[j * seg:(j + 1) * seg]
                b_s[g, pl.ds(j * stride, seg), :] = b[j * seg:(j + 1) * seg]

        def write_out(j, g, h):
            h_ref[pl.ds(j * seg, seg), pl.ds(g * LANES, LANES)] = h

        _scan_tile(a_s, b_s, hc_ref, write_out, seg, reverse=(d == 1))

    tile, prev, nxt = _lru_specs(s, tm, cb, pos, nt)
    vec, dvec, wmat = _lru_param_specs(cb, d)
    wide = jax.ShapeDtypeStruct((s, D_INNER), F32)
    conv_specs = [prev, nxt, pl.BlockSpec((4, cb), lambda c, i: (0, c)), vec] if conv else []
    conv_args = [src, src, p["conv_w"], p["conv_b"]] if conv else []
    return _call_with_sides(
        body, sides, name=name,
        out_shape=[wide, jax.ShapeDtypeStruct((1, D_INNER), F32)] + ([wide] if conv else []),
        grid=(D_INNER // cb, nt),
        in_specs=[tile] + conv_specs + [wmat, wmat, dvec, dvec, dvec, vec],
        out_specs=[tile, vec] + ([tile] if conv else []),
        scratch_shapes=[pltpu.VMEM((n_g, N_SEG * stride, LANES), F32)] * 2,
        compiler_params=_cparams(dimension_semantics=("arbitrary", "arbitrary")),
        args=[src, *conv_args, p["wa"], p["wx"], p["ba"], p["bx"], p["lam"], h0])


def _lru_bwd(uv, dh, h, h0, lam_in, p, d, name, sides=()):
    s = uv.shape[0]
    tm = min(TM_LRU, s)
    cb = CB_LRU
    n_g = cb // LANES
    nt = s // tm
    seg = tm // N_SEG
    stride = seg + SUBLANES
    pos = (lambda i: nt - 1 - i) if d == 0 else (lambda i: i)

    def body(uv_ref, dh_ref, h_ref, hh_ref, wa_ref, wx_ref, ba_ref, bx_ref,
             lam_ref, h0_ref, lin_ref, duv_ref, gwa_ref, gwx_ref, gv_ref, lc_ref, a_s, b_s, lp_s,
             r_s, i_s, q_s, rq_s, a_keep):
        i = pl.program_id(1)
        tp = pos(i)

        @pl.when(i == 0)
        def _():
            lc_ref[...] = lin_ref[...]
            gwa_ref[...] = jnp.zeros_like(gwa_ref)
            gwx_ref[...] = jnp.zeros_like(gwx_ref)
            gv_ref[...] = jnp.zeros_like(gv_ref)

        uv = uv_ref[...]
        lam = lam_ref[...]
        cl = LRU_C * _log_sigmoid(lam)
        ba, bx = ba_ref[...], bx_ref[...]
        dh_t = dh_ref[...]
        carry_in = lc_ref[...]
        for g in range(n_g):
            sl = slice(g * LANES, (g + 1) * LANES)
            _, r, ii, a, sq, rs = _lru_gates(uv, wa_ref, wx_ref, ba, bx, cl, g)
            r_s[:, sl], i_s[:, sl], q_s[:, sl], rq_s[:, sl], a_keep[:, sl] = r, ii, sq, rs, a
            b = a * dh_t[:, sl]
            for j in range(N_SEG):
                a_s[g, pl.ds(j * stride, seg), :] = a[j * seg:(j + 1) * seg]
                b_s[g, pl.ds(j * stride, seg), :] = b[j * seg:(j + 1) * seg]

        def write_out(j, g, v):
            lp_s[pl.ds(j * seg, seg), pl.ds(g * LANES, LANES)] = v

        _scan_tile(a_s, b_s, lc_ref, write_out, seg, reverse=(d == 0))

        h_t = h_ref[...]
        hh = hh_ref[...]
        if d == 0:
            edge = jnp.where(tp == 0, h0_ref[...], hh[7:8])
            h_prev = _shift_down(h_t, edge)
            lam_t = dh_t + _shift_up(lp_s[...], carry_in)
        else:
            edge = jnp.where(tp == nt - 1, h0_ref[...], hh[0:1])
            h_prev = _shift_up(h_t, edge)
            lam_t = dh_t + _shift_down(lp_s[...], carry_in)

        dsig = LRU_C * _sigmoid(-lam)
        for g in range(n_g):
            sl = slice(g * LANES, (g + 1) * LANES)
            uvg, r, ii, a, sq = uv[:, sl], r_s[:, sl], i_s[:, sl], a_keep[:, sl], q_s[:, sl]
            lt = lam_t[:, sl]
            ls = lt * sq
            dla = (lt * a) * (h_prev[:, sl] - (ii * uvg) * (a * rq_s[:, sl]))
            dzr = (dla * cl[:, sl]) * r * (1.0 - r)
            dzi = (ls * uvg) * ii * (1.0 - ii)
            duv_ref[:, sl] = ls * ii + _dot_nt(dzr, wa_ref[g]) + _dot_nt(dzi, wx_ref[g])
            gwa_ref[g] += _dot_tn(uvg, dzr)
            gwx_ref[g] += _dot_tn(uvg, dzi)
            gv_ref[0:1, sl] += _rowsum(dzr)
            gv_ref[1:2, sl] += _rowsum(dzi)
            gv_ref[2:3, sl] += _rowsum(dla * r) * dsig[:, sl]

    tile, prev, nxt = _lru_specs(s, tm, cb, pos, nt)
    vec, dvec, wmat = _lru_param_specs(cb, d)
    hh_spec = prev if d == 0 else nxt
    gw_spec = pl.BlockSpec((n_g, LRU_BLOCK, LRU_BLOCK), lambda c, i: (c, 0, 0))
    n_blk = D_INNER // LRU_BLOCK
    return _call_with_sides(
        body, sides, name=name,
        out_shape=[jax.ShapeDtypeStruct((s, D_INNER), F32),
                   jax.ShapeDtypeStruct((n_blk, LRU_BLOCK, LRU_BLOCK), F32),
                   jax.ShapeDtypeStruct((n_blk, LRU_BLOCK, LRU_BLOCK), F32),
                   jax.ShapeDtypeStruct((SUBLANES, D_INNER), F32),
                   jax.ShapeDtypeStruct((1, D_INNER), F32)],
        grid=(D_INNER // cb, nt),
        in_specs=[tile, tile, tile, hh_spec, wmat, wmat, dvec, dvec, dvec, vec, vec],
        out_specs=[tile, gw_spec, gw_spec, pl.BlockSpec((SUBLANES, cb), lambda c, i: (0, c)), vec],
        scratch_shapes=[pltpu.VMEM((n_g, N_SEG * stride, LANES), F32)] * 2 + [pltpu.VMEM((tm, cb), F32)] * 6,
        compiler_params=_cparams(dimension_semantics=("arbitrary", "arbitrary")),
        args=[uv, dh, h, h, p["wa"], p["wx"], p["ba"], p["bx"], p["lam"], h0, lam_in])


def _out0(hf, hb, g, xt, gt, wo, lg, lb, name):
    t = xt.shape[0]
    tm = min(TM_MM, t)

    def body(hf_ref, hb_ref, g_ref, x_ref, gt_ref, w_ref, lg_ref, lb_ref, x1_ref, br_ref):
        gg = g_ref[...]
        p = (hf_ref[...] + hb_ref[...]) * (gg * _sigmoid(gg))
        br = _dot(p, w_ref[...])
        z = ALPHA * x_ref[...] + gt_ref[...] * br
        xhat, _ = _layer_norm_stats(z)
        x1_ref[...] = xhat * lg_ref[...] + lb_ref[...]
        br_ref[...] = br

    wide = pl.BlockSpec((tm, D_INNER), lambda i: (i, 0))
    nar = pl.BlockSpec((tm, D_MODEL), lambda i: (i, 0))
    row = pl.BlockSpec((1, D_MODEL), lambda i: (0, 0))
    return pl.pallas_call(
        body, name=name, out_shape=[jax.ShapeDtypeStruct((t, D_MODEL), F32)] * 2, grid=(t // tm,),
        in_specs=[wide, wide, wide, nar, row,
                  pl.BlockSpec((D_INNER, D_MODEL), lambda i: (0, 0), pipeline_mode=pl.Buffered(1)), row, row],
        out_specs=[nar, nar],
        compiler_params=_cparams(dimension_semantics=("arbitrary",)),
    )(hf, hb, g, xt, gt, wo, lg, lb)


def _unrolled_loop(n, fn, unroll=4):
    while n % unroll:
        unroll //= 2

    def trip(k, carry):
        for q in range(unroll):
            fn(k * unroll + q)
        return carry
    lax.fori_loop(0, n // unroll, trip, 0)


def _window(n, w):
    t = np.arange(n)
    return np.clip(t - w // 2, 0, n), np.clip(t + w // 2, 0, n)


def _pool_tables(n_rows, transpose):
    boxes, inv_c, inv_r = [], [], []
    for w in POOL_WINDOWS:
        lo, hi = _window(GRID_W, w)
        m = np.zeros((GRID_W, GRID_W), np.float32)
        for r in range(GRID_W):
            m[r, lo[r]:hi[r]] = 1.0
        m = np.kron(np.eye(POOL_TOK // GRID_W, dtype=np.float32), m)
        boxes.append(m.T if transpose else m)
        inv_c.append(np.broadcast_to((1.0 / (hi - lo).astype(np.float32))[:, None], (GRID_W, LANES)))
        lo_r, hi_r = _window(n_rows, w)
        inv_r.append(1.0 / (hi_r - lo_r).astype(np.float32))
    return (jnp.asarray(np.stack(boxes), MXU_DTYPE), jnp.asarray(np.stack(inv_c), F32),
            jnp.asarray(np.stack(inv_r), F32))


def _pool_mix(xin, transpose, out_dtype, name):
    s = xin.shape[0]
    n_rows = s // GRID_W
    pad_t = SUBLANES * GRID_W
    rows_per_blk = POOL_TOK // GRID_W
    n_slab = D_INNER // LANES
    slabs_per_group = POOL_GROUP // LANES
    n_win = len(POOL_WINDOWS)
    boxes, inv_c, inv_r = _pool_tables(n_rows, transpose)

    def body(invr_ref, box_ref, invc_ref, x_ref, o_ref, pad_s):
        k = pl.program_id(0) // slabs_per_group
        pad_s[pl.ds(0, pad_t), :] = jnp.zeros((pad_t, LANES), F32)
        pad_s[pl.ds(pad_t + s, pad_t), :] = jnp.zeros((pad_t, LANES), F32)

        for kk, w in enumerate(POOL_WINDOWS):
            half = w // 2
            offsets = list(range(-(half - 1), half + 1)) if transpose else list(range(-half, half))

            @pl.when(k == kk)
            def _():
                inv_col = invc_ref[kk]

                def col_box(b):
                    st = pl.multiple_of(b * POOL_TOK, POOL_TOK)
                    xb = x_ref[pl.ds(st, POOL_TOK), :]
                    if transpose:
                        xb = xb * jnp.concatenate(
                            [inv_col * invr_ref[kk, b * rows_per_blk + q] for q in range(rows_per_blk)], axis=0)
                    hi = xb.astype(MXU_DTYPE)
                    lo = (xb - hi.astype(F32)).astype(MXU_DTYPE)
                    both = jnp.dot(box_ref[kk], jnp.concatenate([hi, lo], axis=1), preferred_element_type=F32)
                    pad_s[pl.ds(pad_t + st, POOL_TOK), :] = both[:, :LANES] + both[:, LANES:]
                _unrolled_loop(s // POOL_TOK, col_box)

                def row_box(r):
                    st = pl.multiple_of(r * GRID_W, GRID_W)
                    acc = pad_s[pl.ds(pad_t + st + offsets[0] * GRID_W, GRID_W), :]
                    for o in offsets[1:]:
                        acc = acc + pad_s[pl.ds(pad_t + st + o * GRID_W, GRID_W), :]
                    if not transpose:
                        acc = acc * (inv_col * invr_ref[kk, r])
                    o_ref[pl.ds(st, GRID_W), :] = (acc - x_ref[pl.ds(st, GRID_W), :]).astype(out_dtype)
                _unrolled_loop(n_rows, row_box)

    slab = pl.BlockSpec((s, LANES), lambda i: (0, i))
    return pl.pallas_call(
        body, name=name, out_shape=jax.ShapeDtypeStruct((s, D_INNER), out_dtype), grid=(n_slab,),
        in_specs=[pl.BlockSpec(memory_space=pltpu.SMEM),
                  pl.BlockSpec((n_win, POOL_TOK, POOL_TOK), lambda i: (0, 0, 0)),
                  pl.BlockSpec((n_win, GRID_W, LANES), lambda i: (0, 0, 0)), slab],
        out_specs=slab,
        scratch_shapes=[pltpu.VMEM((s + 2 * pad_t, LANES), F32)],
        compiler_params=_cparams(dimension_semantics=("arbitrary",)),
    )(inv_r, boxes, inv_c, xin)


def _out1(dmix, pw, ps, g, x1, gt, wo, lg, lb, tgt, name):
    t = x1.shape[0]
    tm = min(TM_MM, t)
    n_grp = len(POOL_WINDOWS)

    def body(d_ref, pw_ref, ps_ref, g_ref, x1_ref, gt_ref, w_ref, lg_ref, lb_ref, tgt_ref, dz_ref, st_ref):
        @pl.when(pl.program_id(0) == 0)
        def _():
            st_ref[...] = jnp.zeros_like(st_ref)

        br = jnp.zeros((tm, D_MODEL), F32)
        for k in range(n_grp):
            sl = slice(k * POOL_GROUP, (k + 1) * POOL_GROUP)
            y = jnp.dot(d_ref[:, sl], pw_ref[k], preferred_element_type=F32) * ps_ref[:, sl]
            gg = g_ref[:, sl]
            br = br + _dot(y * (gg * _sigmoid(gg)), w_ref[sl, :])
        z = ALPHA * x1_ref[...] + gt_ref[...] * br
        xhat, rstd = _layer_norm_stats(z)
        lg_v = lg_ref[...]
        err = xhat * lg_v + lb_ref[...] - tgt_ref[...]
        dy = err * (1.0 / D_MODEL)
        dz = _layer_norm_bwd(dy, xhat, rstd, lg_v)
        dz_ref[...] = dz
        st_ref[0:1, :] += _rowsum(dy * xhat)
        st_ref[1:2, :] += _rowsum(dy)
        st_ref[2:3, :] += _rowsum(dz * br)
        st_ref[3:4, :] += _rowsum(err * err)

    wide = pl.BlockSpec((tm, D_INNER), lambda i: (i, 0))
    nar = pl.BlockSpec((tm, D_MODEL), lambda i: (i, 0))
    row = pl.BlockSpec((1, D_MODEL), lambda i: (0, 0))
    return pl.pallas_call(
        body, name=name,
        out_shape=[jax.ShapeDtypeStruct((t, D_MODEL), F32), jax.ShapeDtypeStruct((SUBLANES, D_MODEL), F32)],
        grid=(t // tm,),
        in_specs=[wide, pl.BlockSpec((n_grp, POOL_GROUP, POOL_GROUP), lambda i: (0, 0, 0)),
                  pl.BlockSpec((1, D_INNER), lambda i: (0, 0)), wide, nar, row,
                  pl.BlockSpec((D_INNER, D_MODEL), lambda i: (0, 0), pipeline_mode=pl.Buffered(1)), row, row, nar],
        out_specs=[nar, pl.BlockSpec((SUBLANES, D_MODEL), lambda i: (0, 0))],
        compiler_params=_cparams(dimension_semantics=("arbitrary",)),
    )(dmix, pw, ps, g, x1, gt, wo, lg, lb, tgt)


def _flush(acc, out_hbm, sem):
    cp = pltpu.make_async_copy(acc, out_hbm, sem)
    cp.start()
    cp.wait()


def _bout1(dz, dmix, g, pw, ps, gt, wo, name):
    t = dz.shape[0]
    tm = min(TM_BWD, t)
    nt = t // tm
    n_grp = len(POOL_WINDOWS)

    def body(dz_ref, d_ref, g_ref, pw_ref, ps_ref, gt_ref, w_ref, dd_ref, dg_ref, gwo_hbm, gpw_hbm, gps_ref,
             gwo_acc, gpw_acc, sems):
        i = pl.program_id(0)

        @pl.when(i == 0)
        def _():
            gwo_acc[...] = jnp.zeros_like(gwo_acc)
            gpw_acc[...] = jnp.zeros_like(gpw_acc)
            gps_ref[...] = jnp.zeros_like(gps_ref)

        db = (gt_ref[...] * dz_ref[...]).astype(MXU_DTYPE)
        for k in range(n_grp):
            sl = slice(k * POOL_GROUP, (k + 1) * POOL_GROUP)
            dk = d_ref[:, sl]
            po = jnp.dot(dk, pw_ref[k], preferred_element_type=F32)
            psk = ps_ref[:, sl]
            y = po * psk
            gg = g_ref[:, sl]
            sg = _sigmoid(gg)
            silu = gg * sg
            gwo_acc[sl, :] += _dot_tn(y * silu, db)
            dp = _dot_nt(db, w_ref[sl, :])
            dy = dp * silu
            dg_ref[:, sl] = (dp * y * (sg * (1.0 + gg * (1.0 - sg)))).astype(MXU_DTYPE)
            gps_ref[0:1, sl] += _rowsum(dy * po)
            dpo = (dy * psk).astype(MXU_DTYPE)
            gpw_acc[k] += _dot_tn(dk, dpo)
            dd_ref[:, sl] = _dot_nt(dpo, pw_ref[k])

        @pl.when(i == nt - 1)
        def _():
            _flush(gwo_acc, gwo_hbm, sems.at[0])
            _flush(gpw_acc, gpw_hbm, sems.at[1])

    wide = pl.BlockSpec((tm, D_INNER), lambda i: (i, 0))
    nar = pl.BlockSpec((tm, D_MODEL), lambda i: (i, 0))
    return pl.pallas_call(
        body, name=name,
        out_shape=[jax.ShapeDtypeStruct((t, D_INNER), F32), jax.ShapeDtypeStruct((t, D_INNER), MXU_DTYPE),
                   jax.ShapeDtypeStruct((D_INNER, D_MODEL), F32),
                   jax.ShapeDtypeStruct((n_grp, POOL_GROUP, POOL_GROUP), F32),
                   jax.ShapeDtypeStruct((SUBLANES, D_INNER), F32)],
        grid=(nt,),
        in_specs=[nar, wide, wide, pl.BlockSpec((n_grp, POOL_GROUP, POOL_GROUP), lambda i: (0, 0, 0)),
                  pl.BlockSpec((1, D_INNER), lambda i: (0, 0)), pl.BlockSpec((1, D_MODEL), lambda i: (0, 0)),
                  pl.BlockSpec((D_INNER, D_MODEL), lambda i: (0, 0), pipeline_mode=pl.Buffered(1))],
        out_specs=[wide, wide, ANY, ANY, pl.BlockSpec((SUBLANES, D_INNER), lambda i: (0, 0))],
        scratch_shapes=[pltpu.VMEM((D_INNER, D_MODEL), F32), pltpu.VMEM((n_grp, POOL_GROUP, POOL_GROUP), F32),
                        pltpu.SemaphoreType.DMA((2,))],
        compiler_params=_cparams(dimension_semantics=("arbitrary",)),
    )(dz, dmix, g, pw, ps, gt, wo)


def _bout0(dx1, xt, br0, lg, hf, hb, g, gt, wo, name, sides=()):
    t = dx1.shape[0]
    tm = min(TM_BWD, t)
    nt = t // tm

    def body(dx_ref, x_ref, br_ref, lg_ref, hf_ref, hb_ref, g_ref, gt_ref, w_ref,
             dz_ref, dy_ref, dg_ref, gwo_hbm, st_ref, gwo_acc, sem):
        i = pl.program_id(0)

        @pl.when(i == 0)
        def _():
            gwo_acc[...] = jnp.zeros_like(gwo_acc)
            st_ref[...] = jnp.zeros_like(st_ref)

        dx = dx_ref[...]
        br = br_ref[...]
        gate = gt_ref[...]
        xhat, rstd = _layer_norm_stats(ALPHA * x_ref[...] + gate * br)
        dz = _layer_norm_bwd(dx, xhat, rstd, lg_ref[...])
        dz_ref[...] = dz
        st_ref[0:1, :] += _rowsum(dx * xhat)
        st_ref[1:2, :] += _rowsum(dx)
        st_ref[2:3, :] += _rowsum(dz * br)
        db = (gate * dz).astype(MXU_DTYPE)
        y = hf_ref[...] + hb_ref[...]
        gg = g_ref[...]
        sg = _sigmoid(gg)
        silu = gg * sg
        gwo_acc[...] += _dot_tn(y * silu, db)
        dp = _dot_nt(db, w_ref[...])
        dy_ref[...] = dp * silu
        dg_ref[...] = (dp * y * (sg * (1.0 + gg * (1.0 - sg)))).astype(MXU_DTYPE)

        @pl.when(i == nt - 1)
        def _():
            _flush(gwo_acc, gwo_hbm, sem)

    wide = pl.BlockSpec((tm, D_INNER), lambda i: (i, 0))
    nar = pl.BlockSpec((tm, D_MODEL), lambda i: (i, 0))
    row = pl.BlockSpec((1, D_MODEL), lambda i: (0, 0))
    return _call_with_sides(
        body, sides, name=name,
        out_shape=[jax.ShapeDtypeStruct((t, D_MODEL), F32), jax.ShapeDtypeStruct((t, D_INNER), F32),
                   jax.ShapeDtypeStruct((t, D_INNER), MXU_DTYPE), jax.ShapeDtypeStruct((D_INNER, D_MODEL), F32),
                   jax.ShapeDtypeStruct((SUBLANES, D_MODEL), F32)],
        grid=(nt,),
        in_specs=[nar, nar, nar, row, wide, wide, wide, row,
                  pl.BlockSpec((D_INNER, D_MODEL), lambda i: (0, 0), pipeline_mode=pl.Buffered(1))],
        out_specs=[nar, wide, wide, ANY, pl.BlockSpec((SUBLANES, D_MODEL), lambda i: (0, 0))],
        scratch_shapes=[pltpu.VMEM((D_INNER, D_MODEL), F32), pltpu.SemaphoreType.DMA(())],
        compiler_params=_cparams(dimension_semantics=("arbitrary",)),
        args=[dx1, xt, br0, lg, hf, hb, g, gt, wo])


def _conv_bwd(duvf, duvb, u, conv_w, name):
    s = u.shape[0]
    tm = min(TM_LRU, s)
    cb = CB_LRU
    nt = s // tm

    def body(df_ref, dfp_ref, dfn_ref, db_ref, dbp_ref, dbn_ref, u_ref, cw_ref, du_ref, cst_ref):
        i = pl.program_id(1)

        @pl.when(i == 0)
        def _():
            cst_ref[...] = jnp.zeros_like(cst_ref)

        first, last = i == 0, i == nt - 1
        pz = jnp.where(first, 0.0, 1.0)
        nz = jnp.where(last, 0.0, 1.0)
        dout = df_ref[...] + db_ref[...]
        dm1, dp1, dp2 = _shifted(dout, (dfp_ref[...] + dbp_ref[...]) * pz, (dfn_ref[...] + dbn_ref[...]) * nz,
                                 [-1, 1, 2])
        cw = cw_ref[...]
        du_ref[...] = (dp2 * cw[0:1] + dp1 * cw[1:2] + dout * cw[2:3] + dm1 * cw[3:4]).astype(MXU_DTYPE)
        u_t = u_ref[...]
        cst_ref[0:1, :] += _rowsum(dp2 * u_t)
        cst_ref[1:2, :] += _rowsum(dp1 * u_t)
        cst_ref[2:3, :] += _rowsum(dout * u_t)
        cst_ref[3:4, :] += _rowsum(dm1 * u_t)
        cst_ref[4:5, :] += _rowsum(dout)

    tile, prev, nxt = _lru_specs(s, tm, cb, lambda i: i, nt)
    return pl.pallas_call(
        body, name=name,
        out_shape=[jax.ShapeDtypeStruct((s, D_INNER), MXU_DTYPE), jax.ShapeDtypeStruct((SUBLANES, D_INNER), F32)],
        grid=(D_INNER // cb, nt),
        in_specs=[tile, prev, nxt] * 2 + [tile, pl.BlockSpec((4, cb), lambda c, i: (0, c))],
        out_specs=[tile, pl.BlockSpec((SUBLANES, cb), lambda c, i: (0, c))],
        compiler_params=_cparams(dimension_semantics=("arbitrary", "arbitrary")),
    )(duvf, duvf, duvf, duvb, duvb, duvb, u, conv_w)


def _bin(du, dg, xin, dzin, sc, sh, wg, name):
    t = xin.shape[0]
    tm = min(TM_MM, t)
    nt = t // tm
    has_g, has_dx = dg is not None, dzin is not None
    half = N_WBLK // 2
    n_blk = N_WBLK if has_g else half

    def body(*refs):
        refs = list(refs)
        du_ref = refs.pop(0)
        dg_ref = refs.pop(0) if has_g else None
        x_ref = refs.pop(0)
        dz_ref = refs.pop(0) if has_dx else None
        sc_ref, sh_ref, w_ref = refs.pop(0), refs.pop(0), refs.pop(0)
        dx_ref = refs.pop(0) if has_dx else None
        gw_hbm, st_ref, gw_acc, sem = refs
        i = pl.program_id(0)

        @pl.when(i == 0)
        def _():
            gw_acc[...] = jnp.zeros_like(gw_acc)
            st_ref[...] = jnp.zeros_like(st_ref)

        xv = x_ref[...]
        scale = 1.0 + sc_ref[...]
        h = (xv * scale + sh_ref[...]).astype(MXU_DTYPE)
        dh = None
        for k in range(n_blk):
            src = du_ref if k < half else dg_ref
            kk = k % half
            dk = src[:, kk * WBLK:(kk + 1) * WBLK]
            gw_acc[k] += _dot_tn(h, dk)
            contrib = _dot_nt(dk, w_ref[k])
            dh = contrib if dh is None else dh + contrib
        st_ref[0:1, :] += _rowsum(dh * xv)
        st_ref[1:2, :] += _rowsum(dh)
        if has_dx:
            dx_ref[...] = ALPHA * dz_ref[...] + dh * scale

        @pl.when(i == nt - 1)
        def _():
            _flush(gw_acc, gw_hbm, sem)

    wide = pl.BlockSpec((tm, D_INNER), lambda i: (i, 0))
    nar = pl.BlockSpec((tm, D_MODEL), lambda i: (i, 0))
    row = pl.BlockSpec((1, D_MODEL), lambda i: (0, 0))
    wspec = pl.BlockSpec((n_blk, D_MODEL, WBLK), lambda i: (0, 0, 0), pipeline_mode=pl.Buffered(1))
    in_specs = [wide] + ([wide] if has_g else []) + [nar] + ([nar] if has_dx else []) + [row, row, wspec]
    args = [du] + ([dg] if has_g else []) + [xin] + ([dzin] if has_dx else []) + [sc, sh, wg]
    out_shape = ([jax.ShapeDtypeStruct((t, D_MODEL), F32)] if has_dx else []) + [
        jax.ShapeDtypeStruct((n_blk, D_MODEL, WBLK), F32), jax.ShapeDtypeStruct((SUBLANES, D_MODEL), F32)]
    out_specs = ([nar] if has_dx else []) + [ANY, pl.BlockSpec((SUBLANES, D_MODEL), lambda i: (0, 0))]
    return pl.pallas_call(
        body, name=name, out_shape=out_shape, grid=(nt,), in_specs=in_specs, out_specs=out_specs,
        scratch_shapes=[pltpu.VMEM((n_blk, D_MODEL, WBLK), F32), pltpu.SemaphoreType.DMA(())],
        compiler_params=_cparams(dimension_semantics=("arbitrary",)),
    )(*args)


def _blocks_by_device(a, axis):
    shape = a.shape
    a = a.reshape(shape[:axis] + (N_DEV, shape[axis] // N_DEV) + shape[axis + 1:])
    return jnp.moveaxis(a, axis, 0)


def kernel(x, c, ctx, c_ctx, w_mod, b_mod, w_in, w_out, ln_g, ln_b, conv_w, conv_b, lru_wa, lru_ba, lru_wx, lru_bx, lru_lam, pool_w, pool_scale, loss_target, m_c_ctx, m_w_mod, m_b_mod, m_w_in, m_w_out, m_ln_g, m_ln_b, m_conv_w, m_conv_b, m_lru_wa, m_lru_ba, m_lru_wx, m_lru_bx, m_lru_lam, m_pool_w, m_pool_scale, v_c_ctx, v_w_mod, v_b_mod, v_w_in, v_w_out, v_ln_g, v_ln_b, v_conv_w, v_conv_b, v_lru_wa, v_lru_ba, v_lru_wx, v_lru_bx, v_lru_lam, v_pool_w, v_pool_scale):
    xi, yi, ci = _my_pos()
    dev = 4 * xi + 2 * yi + ci
    xt, ctxt, tgt = x[0], ctx[0], loss_target[0]
    n_mod = w_mod.shape[2]

    small_shapes = [(D_MODEL,), conv_w.shape[1:], lru_ba.shape[1:], lru_bx.shape[1:], lru_lam.shape[1:],
                    pool_scale.shape[1:]]
    small = _to_rows([c[0], conv_w[0], lru_ba[0], lru_bx[0], lru_lam[0], pool_scale[0]], SUBLANES)
    small_all, = _all_gather([small], "gather_small")
    pieces = [_split_rows(small_all[k], small_shapes) for k in range(N_DEV)]
    c_all = jnp.stack([p[0] for p in pieces])
    conv_w_f = jnp.concatenate([p[1] for p in pieces], axis=-1)
    lru_ba_f = jnp.concatenate([p[2] for p in pieces], axis=-1)[:, None, :]
    lru_bx_f = jnp.concatenate([p[3] for p in pieces], axis=-1)[:, None, :]
    lru_lam_f = jnp.concatenate([p[4] for p in pieces], axis=-1)[:, None, :]
    pool_scale_f = jnp.concatenate([p[5] for p in pieces], axis=-1)[None, :]

    cond = jnp.concatenate([c_all, jnp.broadcast_to(c_ctx[None, :], (N_DEV, D_MODEL))], axis=0)
    b_my = lax.dynamic_slice(b_mod, (0, dev * n_mod), (2, n_mod))[:, None, :]
    mod_part = _mod_fwd(cond, w_mod, b_my, "mod_fwd")
    mod_all, = _all_gather([mod_part], "gather_mod")
    mod = jnp.transpose(mod_all, (1, 2, 0, 3)).reshape(2, 16, 3 * D_MODEL)
    mod_me = lax.dynamic_slice(mod, (0, dev, 0), (2, 1, 3 * D_MODEL))
    sh = [mod_me[i, :, 0:D_MODEL] for i in range(2)]
    sc = [mod_me[i, :, D_MODEL:2 * D_MODEL] for i in range(2)]
    gt = [mod_me[i, :, 2 * D_MODEL:] for i in range(2)]
    shc, scc = mod[0, 8:9, 0:D_MODEL], mod[0, 8:9, D_MODEL:2 * D_MODEL]

    wi0, wo0 = _all_gather([w_in[0].astype(MXU_DTYPE), w_out[0].astype(MXU_DTYPE)], "gather_weights0")
    lg = [ln_g[i][None, :] for i in range(2)]
    lb = [ln_b[i][None, :] for i in range(2)]
    lru_p = dict(conv_w=conv_w_f, conv_b=conv_b, wa=lru_wa[0].astype(MXU_DTYPE), wx=lru_wx[0].astype(MXU_DTYPE),
                 ba=lru_ba_f, bx=lru_bx_f, lam=lru_lam_f)
    zero_state = jnp.zeros((1, D_INNER), F32)

    u0, g0 = _in_proj(xt, sc[0], sh[0], wi0, "in_proj0")
    uc, _ = _in_proj(ctxt, scc, shc, wi0, "in_proj0_ctx")
    (hcf, cf, uvc), _ = _lru_fwd(uc, zero_state, lru_p, 0, "lru_fwd_ctx_f", conv=True)
    (hcb, cbk), _ = _lru_fwd(uvc, zero_state, lru_p, 1, "lru_fwd_ctx_b", conv=False)
    (hf, _, uv0), (wi1,) = _lru_fwd(u0, cf, lru_p, 0, "lru_fwd_f", conv=True,
                                    sides=[("gather", [w_in[1].astype(MXU_DTYPE)])])
    (hb, _), (wo1, pool_w_g) = _lru_fwd(
        uv0, cbk, lru_p, 1, "lru_fwd_b", conv=False,
        sides=[("gather", [w_out[1].astype(MXU_DTYPE), pool_w[0].astype(MXU_DTYPE)])])
    w_in_l = [wi0, wi1]
    w_out_l = [wo0.reshape(D_INNER, D_MODEL), wo1.reshape(D_INNER, D_MODEL)]
    pool_w_f = jnp.transpose(pool_w_g, (1, 0, 2, 3)).reshape(len(POOL_WINDOWS), POOL_GROUP, POOL_GROUP)
    x1, br0 = _out0(hf, hb, g0, xt, gt[0], w_out_l[0], lg[0], lb[0], "out0")
    u1, g1 = _in_proj(x1, sc[1], sh[1], w_in_l[1], "in_proj1")
    dmix = _pool_mix(u1, False, MXU_DTYPE, "pool_fwd")
    dz1, st1 = _out1(dmix, pool_w_f, pool_scale_f, g1, x1, gt[1], w_out_l[1], lg[1], lb[1], tgt, "out1")
    loss_me = jnp.full((1, LANES), (0.5 / D_MODEL) * jnp.sum(st1[3]), F32)

    core = jnp.reshape(ci, (1,)).astype(jnp.int32)
    wo_view = lambda a: a.reshape(N_DEV, D_INNER // N_DEV, D_MODEL)
    pw_view = lambda a: _blocks_by_device(a, 1).reshape(N_DEV, POOL_GROUP // N_DEV * len(POOL_WINDOWS), POOL_GROUP)
    dd, dg1, gwo1, gpw, gps = _bout1(dz1, dmix, g1, pool_w_f, pool_scale_f, gt[1], w_out_l[1], "bwd_out1")
    du1 = _pool_mix(dd, True, MXU_DTYPE, "pool_bwd")
    dx1, gwi1, stb1 = _bin(du1, dg1, x1, dz1, sc[1], sh[1], w_in_l[1], "bwd_in1")
    bufs1 = [gwi1, wo_view(gwo1), pw_view(gpw)]
    (dz0, dy0, dg0, gwo0, stl0), recv1 = _bout0(dx1, xt, br0, lg[0], hf, hb, g0, gt[0], w_out_l[0], "bwd_out0",
                                                sides=[("sibling", bufs1)])
    pairs1 = [_pair_sum(b, r, core, "reduce_pair_" + n)
              for b, r, n in zip(bufs1, recv1, ["w_in1", "w_out1", "pool_w"])]
    (duvf, gwa_f, gwx_f, gv_f, dh0f), (p_wi1, p_wo1, p_pw, recv_wo0) = _lru_bwd(
        uv0, dy0, hf, cf, zero_state, lru_p, 0, "lru_bwd_f", sides=[("chips", pairs1), ("sibling", [wo_view(gwo0)])])
    pair_wo0 = _pair_sum(wo_view(gwo0), recv_wo0, core, "reduce_pair_w_out0")
    (duvb, gwa_b, gwx_b, gv_b, dh0b), (p_wo0,) = _lru_bwd(
        uv0, dy0, hb, cbk, zero_state, lru_p, 1, "lru_bwd_b", sides=[("chips", [pair_wo0])])
    zero_dh = jnp.zeros_like(uc)
    (ducf, gwa_cf, gwx_cf, gv_cf, _), _ = _lru_bwd(uvc, zero_dh, hcf, zero_state, dh0f, lru_p, 0, "lru_bwd_ctx_f")
    (ducb, gwa_cb, gwx_cb, gv_cb, _), _ = _lru_bwd(uvc, zero_dh, hcb, zero_state, dh0b, lru_p, 1, "lru_bwd_ctx_b")
    du0, cst0 = _conv_bwd(duvf, duvb, u0, conv_w_f, "conv_bwd")
    duc, cstc = _conv_bwd(ducf, ducb, uc, conv_w_f, "conv_bwd_ctx")
    gx, gwi0, stb0 = _bin(du0, dg0, xt, dz0, sc[0], sh[0], w_in_l[0], "bwd_in0")
    gwic, stc = _bin(duc, None, ctxt, None, scc, shc, w_in_l[0][:N_WBLK // 2], "bwd_in0_ctx")
    gwi0 = gwi0.at[:N_WBLK // 2].add(gwic)

    zero_row = jnp.zeros((1, D_MODEL), F32)
    dm_me = jnp.stack([
        jnp.concatenate([jnp.concatenate([stb0[1:2], stb0[0:1], stl0[2:3]], axis=1),
                         jnp.concatenate([stc[1:2], stc[0:1], zero_row], axis=1)], axis=0),
        jnp.concatenate([jnp.concatenate([stb1[1:2], stb1[0:1], st1[2:3]], axis=1),
                         jnp.zeros((1, 3 * D_MODEL), F32)], axis=0)])
    dm_g, loss_g = _all_gather([dm_me, loss_me], "gather_dmod")
    loss = jnp.sum(loss_g[:, 0, 0])
    dm_all = jnp.concatenate([jnp.transpose(dm_g[:, :, 0], (1, 0, 2)), jnp.transpose(dm_g[:, :, 1], (1, 0, 2))],
                             axis=1)
    dm_my = lax.dynamic_slice(dm_all, (0, 0, dev * n_mod), (2, 16, n_mod))
    g_w_mod, g_b_mod, gcc_part = _mod_bwd(cond, dm_all, dm_my, w_mod, "mod_bwd")
    g_b_mod = g_b_mod.reshape(b_mod.shape)

    gwa = jnp.stack([gwa_f + gwa_cf, gwa_b + gwa_cb])
    gwx = jnp.stack([gwx_f + gwx_cf, gwx_b + gwx_cb])
    gv = jnp.stack([gv_f + gv_cf, gv_b + gv_cb])
    cst = cst0 + cstc
    g_ln_g = jnp.stack([stl0[0], st1[0]])
    g_ln_b = jnp.stack([stl0[1], st1[1]])
    sharded = [
        _blocks_by_device(cst[0:4], 1),
        _blocks_by_device(gv[:, 0], 1), _blocks_by_device(gv[:, 1], 1), _blocks_by_device(gv[:, 2], 1),
        _blocks_by_device(gps[0], 0),
    ]
    replicated = [gwa.reshape(-1), gwx.reshape(-1), g_ln_g.reshape(-1), g_ln_b.reshape(-1), cst[4],
                  gcc_part.reshape(-1)]
    sh_sizes = [int(np.prod(a.shape[1:])) for a in sharded]
    rep_sizes = [a.shape[0] // N_DEV for a in replicated]
    n_flat = sum(sh_sizes) + sum(rep_sizes)
    rows = -(-n_flat // LANES)
    rows = -(-rows // FLAT_ROWS) * FLAT_ROWS
    misc = jnp.concatenate([a.reshape(N_DEV, -1) for a in sharded] +
                           [a.reshape(N_DEV, -1) for a in replicated], axis=1)
    misc = jnp.pad(misc, ((0, 0), (0, rows * LANES - n_flat))).reshape(N_DEV, rows, LANES)
    bufs = [gwi0, misc]
    recvs = _sibling_exchange(bufs, "reduce_sibling")
    pairs = [_pair_sum(b, r, core, "reduce_pair_" + n) for b, r, n in zip(bufs, recvs, ["w_in0", "misc"])]
    p_wi0, p_misc = _chip_exchange(pairs, "reduce_chips")
    g_flat = _sum4(p_misc, "reduce_sum_misc").reshape(-1)

    offs = np.cumsum([0] + sh_sizes + rep_sizes)
    n_sh = len(sh_sizes)
    sh_shapes = [conv_w.shape, lru_ba.shape, lru_bx.shape, lru_lam.shape, pool_scale.shape]
    g_sh = [g_flat[offs[k]:offs[k + 1]].reshape(sh_shapes[k]) for k in range(n_sh)]
    g_conv_w, g_lru_ba, g_lru_bx, g_lru_lam, g_pool_scale = g_sh
    rep_block = _to_rows([g_flat[offs[n_sh]:offs[-1]]], SUBLANES)
    rep_all, = _all_gather([rep_block], "gather_replicated")
    rep_flat = rep_all.reshape(N_DEV, -1)
    rep_full, off = [], 0
    for n in rep_sizes:
        rep_full.append(rep_flat[:, off:off + n].reshape(-1))
        off += n
    g_lru_wa = rep_full[0].reshape(lru_wa.shape)
    g_lru_wx = rep_full[1].reshape(lru_wx.shape)
    g_ln_g = rep_full[2].reshape(ln_g.shape)
    g_ln_b = rep_full[3].reshape(ln_b.shape)
    g_conv_b = rep_full[4].reshape(conv_b.shape)
    g_c_ctx = rep_full[5].reshape(c_ctx.shape)

    names = ["c_ctx", "w_mod", "b_mod", "w_in", "w_out", "ln_g", "ln_b", "conv_w", "conv_b", "lru_wa", "lru_ba",
             "lru_wx", "lru_bx", "lru_lam", "pool_w", "pool_scale"]
    weights = dict(c_ctx=c_ctx, w_mod=w_mod, b_mod=b_mod, w_in=w_in, w_out=w_out, ln_g=ln_g, ln_b=ln_b,
                   conv_w=conv_w, conv_b=conv_b, lru_wa=lru_wa, lru_ba=lru_ba, lru_wx=lru_wx, lru_bx=lru_bx,
                   lru_lam=lru_lam, pool_w=pool_w, pool_scale=pool_scale)
    mom_m = dict(c_ctx=m_c_ctx, w_mod=m_w_mod, b_mod=m_b_mod, w_in=m_w_in, w_out=m_w_out, ln_g=m_ln_g, ln_b=m_ln_b,
                 conv_w=m_conv_w, conv_b=m_conv_b, lru_wa=m_lru_wa, lru_ba=m_lru_ba, lru_wx=m_lru_wx,
                 lru_bx=m_lru_bx, lru_lam=m_lru_lam, pool_w=m_pool_w, pool_scale=m_pool_scale)
    mom_v = dict(c_ctx=v_c_ctx, w_mod=v_w_mod, b_mod=v_b_mod, w_in=v_w_in, w_out=v_w_out, ln_g=v_ln_g, ln_b=v_ln_b,
                 conv_w=v_conv_w, conv_b=v_conv_b, lru_wa=v_lru_wa, lru_ba=v_lru_ba, lru_wx=v_lru_wx,
                 lru_bx=v_lru_bx, lru_lam=v_lru_lam, pool_w=v_pool_w, pool_scale=v_pool_scale)
    grads = dict(c_ctx=g_c_ctx, w_mod=g_w_mod, b_mod=g_b_mod, ln_g=g_ln_g, ln_b=g_ln_b,
                 conv_w=g_conv_w, conv_b=g_conv_b, lru_wa=g_lru_wa, lru_ba=g_lru_ba, lru_wx=g_lru_wx,
                 lru_bx=g_lru_bx, lru_lam=g_lru_lam)
    grads["pool_scale"] = g_pool_scale
    delta, new_m, new_v = {}, {}, {}

    def update_parts(n, parts, view):
        res = _adamw_parts(weights[n].reshape(view), parts, mom_m[n].reshape(view), mom_v[n].reshape(view),
                           "adamw_" + n)
        grads[n], delta[n], new_m[n], new_v[n] = [r.reshape(weights[n].shape) for r in res]

    update_parts("w_in", [p_wi0, p_wi1], w_in.shape)
    update_parts("w_out", [p_wo0, p_wo1], w_out.shape)
    update_parts("pool_w", [p_pw], (1,) + p_pw.shape[1:])
    for n in ("w_mod", "lru_wa", "lru_wx"):
        shape = weights[n].shape
        view = (int(np.prod(shape[:-1])), shape[-1])
        res = _adamw(weights[n].reshape(view), grads[n].reshape(view), mom_m[n].reshape(view),
                     mom_v[n].reshape(view), "adamw_" + n)
        delta[n], new_m[n], new_v[n] = [r.reshape(shape) for r in res]

    small = [n for n in names if n not in delta]
    shapes = [weights[n].shape for n in small]
    flat = lambda d: _to_rows([d[n] for n in small], FLAT_ROWS)
    res = _adamw(flat(weights), flat(grads), flat(mom_m), flat(mom_v), "adamw_small")
    for d, r in zip((delta, new_m, new_v), res):
        d.update(zip(small, _split_rows(r, shapes)))

    return (loss, gx[None], *[grads[n] for n in names], *[delta[n] for n in names],
            *[new_m[n] for n in names], *[new_v[n] for n in names])
```

```python
import functools

import numpy as np
import jax
import jax.numpy as jnp
from jax import lax
from jax.experimental import pallas as pl
from jax.experimental.pallas import tpu as pltpu

F32 = jnp.float32
BF16 = jnp.bfloat16
MXU_DTYPE = BF16

D_MODEL = 1024
D_INNER = 2048
LRU_BLOCK = 128
GRID_W = 64
POOL_WINDOWS = (2, 4, 8, 16)
POOL_GROUP = 512
ALPHA = float(4 ** 0.25)
LN_EPS = 1e-5
LRU_C = 8.0
N_DEV = 8
N_WBLK = 8
WBLK = 512

ADAM_LR = 0.001
ADAM_B1 = 0.9
ADAM_B2 = 0.999
ADAM_EPS = 1e-08
ADAM_WD = 0.01
ADAM_STEP = 10

LANES = 128
SUBLANES = 8
V7X_VMEM_BYTES = 64 * 1024 * 1024
VMEM_LIMIT = V7X_VMEM_BYTES - 8 * 1024 * 1024
MESH = pl.DeviceIdType.MESH
ANY = pl.BlockSpec(memory_space=pl.ANY)

TM_MM = 512
TM_BWD = 256
TM_LRU = 512
CB_LRU = 512
N_SEG = 8
SCAN_UNROLL = 4
SCAN_ROW_T = 33
SCAN_ROW_J = 4
SQRT_FLOOR = 1e-30
FLAT_ROWS = 16
ELEMENTWISE_TILE_BYTES = 1 << 20
POOL_TOK = 256
WIRE_DTYPE = BF16


def _cparams(**kw):
    return pltpu.CompilerParams(vmem_limit_bytes=VMEM_LIMIT, **kw)


def _my_pos():
    return lax.axis_index("x"), lax.axis_index("y"), lax.axis_index("c")


def _dot(a, b):
    return jnp.dot(a.astype(MXU_DTYPE), b.astype(MXU_DTYPE), preferred_element_type=F32)


def _dot_tn(a, b):
    return lax.dot_general(a.astype(MXU_DTYPE), b.astype(MXU_DTYPE), (((0,), (0,)), ((), ())),
                           preferred_element_type=F32)


def _dot_nt(a, b):
    return lax.dot_general(a.astype(MXU_DTYPE), b.astype(MXU_DTYPE), (((1,), (1,)), ((), ())),
                           preferred_element_type=F32)


def _sigmoid(z):
    return 0.5 * jnp.tanh(0.5 * z) + 0.5


def _log_sigmoid(x):
    y = jnp.exp(-jnp.abs(x))
    u = 1.0 + y
    l1p = jnp.where(u == 1.0, y, jnp.log(u) * (y / jnp.where(u == 1.0, 1.0, u - 1.0)))
    return jnp.minimum(x, 0.0) - l1p


def _rowsum(v):
    return jnp.sum(v, axis=0, keepdims=True)


def _layer_norm_stats(z):
    mu = jnp.mean(z, axis=-1, keepdims=True)
    zc = z - mu
    var = jnp.mean(zc * zc, axis=-1, keepdims=True)
    rstd = lax.rsqrt(var + LN_EPS)
    return zc * rstd, rstd


def _layer_norm_bwd(dy, xhat, rstd, g):
    dxh = dy * g
    m1 = jnp.mean(dxh, axis=-1, keepdims=True)
    m2 = jnp.mean(dxh * xhat, axis=-1, keepdims=True)
    return rstd * (dxh - m1 - xhat * m2)


def _shifted(v, before8, after8, offsets):
    n = v.shape[0]
    ext = jnp.concatenate([before8, v, after8], axis=0)
    total = n + 2 * SUBLANES
    return [pltpu.roll(ext, (-k) % total, 0)[SUBLANES:SUBLANES + n] for k in offsets]


def _rows8(row):
    return jnp.broadcast_to(row, (SUBLANES, row.shape[1]))


def _shift_down(v, first_row):
    return _shifted(v, _rows8(first_row), _rows8(first_row), [-1])[0]


def _shift_up(v, last_row):
    return _shifted(v, _rows8(last_row), _rows8(last_row), [1])[0]


def _all_gather(blocks, name):
    n = len(blocks)

    def body(*refs):
        x_refs, out_refs = refs[:n], refs[n:2 * n]
        send_sems, recv_sems, local_sems = refs[2 * n:]
        x, y, c = _my_pos()
        me, sibling = (x, y, c), (x, y, 1 - c)
        chips = [(1 - x, y), (x, 1 - y), (1 - x, 1 - y)]

        def slot(a, px, py, pc):
            return out_refs[a].at[4 * px + 2 * py + pc]

        def copy(a, k, block, to, src=None):
            return pltpu.make_async_remote_copy(
                src_ref=slot(a, *block) if src is None else src, dst_ref=slot(a, *block),
                send_sem=send_sems.at[a, k], recv_sem=recv_sems.at[a, k], device_id=to, device_id_type=MESH)

        mine = [pltpu.make_async_copy(x_refs[a], slot(a, *me), local_sems.at[a]) for a in range(n)]
        for cp in mine:
            cp.start()
        first = []
        for a in range(n):
            first.append(copy(a, 0, me, sibling, src=x_refs[a]))
            first += [copy(a, 1 + j, me, (*chip, c), src=x_refs[a]) for j, chip in enumerate(chips)]
        for cp in first:
            cp.start()
        passed = []
        for j, chip in enumerate(chips):
            for a in range(n):
                copy(a, 1 + j, (*chip, c), me).wait_recv()
                fwd = copy(a, 4 + j, (*chip, c), sibling)
                fwd.start()
                passed.append(fwd)
        for a in range(n):
            copy(a, 0, sibling, me).wait_recv()
            for j, chip in enumerate(chips):
                copy(a, 4 + j, (*chip, 1 - c), me).wait_recv()
        for cp in first + passed:
            cp.wait_send()
        for cp in mine:
            cp.wait()

    outs = pl.pallas_call(
        body, name=name,
        out_shape=[jax.ShapeDtypeStruct((N_DEV,) + b.shape, b.dtype) for b in blocks],
        in_specs=[ANY] * n, out_specs=[ANY] * n,
        scratch_shapes=[pltpu.SemaphoreType.DMA((n, 7)), pltpu.SemaphoreType.DMA((n, 7)),
                        pltpu.SemaphoreType.DMA((n,))],
    )(*blocks)
    return list(outs)


def _sibling_exchange(bufs, name):
    n = len(bufs)

    def body(*refs):
        srcs, outs = refs[:n], refs[n:2 * n]
        send_sems, recv_sems = refs[2 * n:]
        x, y, c = _my_pos()
        copies = [pltpu.make_async_remote_copy(
            src_ref=srcs[a].at[2 * j + (1 - c)], dst_ref=outs[a].at[j], send_sem=send_sems.at[a, j],
            recv_sem=recv_sems.at[a, j], device_id=(x, y, 1 - c), device_id_type=MESH)
            for a in range(n) for j in range(4)]
        for cp in copies:
            cp.start()
        for cp in copies:
            cp.wait()

    outs = pl.pallas_call(
        body, name=name, out_shape=[jax.ShapeDtypeStruct((4,) + b.shape[1:], b.dtype) for b in bufs],
        in_specs=[ANY] * n, out_specs=[ANY] * n,
        scratch_shapes=[pltpu.SemaphoreType.DMA((n, 4)), pltpu.SemaphoreType.DMA((n, 4))],
    )(*bufs)
    return list(outs)


def _chip_exchange(parts, name):
    n = len(parts)

    def body(*refs):
        srcs, outs = refs[:n], refs[n:2 * n]
        send_sems, recv_sems, local_sems = refs[2 * n:]
        x, y, c = _my_pos()
        jme = 2 * x + y
        peers = [(1 - x, y), (x, 1 - y), (1 - x, 1 - y)]
        local = [pltpu.make_async_copy(srcs[a].at[jme], outs[a].at[jme], local_sems.at[a]) for a in range(n)]
        for cp in local:
            cp.start()

        def copy(a, k, px, py, dst_slot):
            return pltpu.make_async_remote_copy(
                src_ref=srcs[a].at[2 * px + py], dst_ref=outs[a].at[dst_slot], send_sem=send_sems.at[a, k],
                recv_sem=recv_sems.at[a, k], device_id=(px, py, c), device_id_type=MESH)

        sends = [copy(a, k, px, py, jme) for a in range(n) for k, (px, py) in enumerate(peers)]
        for cp in sends:
            cp.start()
        for a in range(n):
            for k, (px, py) in enumerate(peers):
                copy(a, k, px, py, 2 * px + py).wait_recv()
        for cp in sends:
            cp.wait_send()
        for cp in local:
            cp.wait()

    outs = pl.pallas_call(
        body, name=name, out_shape=[jax.ShapeDtypeStruct(p.shape, p.dtype) for p in parts],
        in_specs=[ANY] * n, out_specs=[ANY] * n,
        scratch_shapes=[pltpu.SemaphoreType.DMA((n, 3)), pltpu.SemaphoreType.DMA((n, 3)),
                        pltpu.SemaphoreType.DMA((n,))],
    )(*parts)
    return list(outs)


_SIDE_REMOTE = {"gather": 7, "sibling": 4, "chips": 3}
_FLIPS = [(0, 0, 1), (1, 0, 0), (0, 1, 0), (1, 1, 0), (1, 0, 1), (0, 1, 1), (1, 1, 1)]


def _side_plan(sides):
    inputs, out_shapes, scratch = [], [], []
    for kind, arrays in sides:
        n = len(arrays)
        for a in arrays:
            inputs.append(a)
            shape = {"gather": (N_DEV,) + a.shape, "sibling": (4,) + a.shape[1:], "chips": a.shape}[kind]
            out_shapes.append(jax.ShapeDtypeStruct(shape, a.dtype))
        scratch += [pltpu.SemaphoreType.DMA((n, _SIDE_REMOTE[kind])), pltpu.SemaphoreType.DMA((n, _SIDE_REMOTE[kind])),
                    pltpu.SemaphoreType.DMA((n,))]
    return inputs, out_shapes, scratch


def _side_copies(sides, in_refs, out_refs, sem_refs):
    x, y, c = _my_pos()
    starts, waits = [], []
    pos = 0
    for s, (kind, arrays) in enumerate(sides):
        send_sems, recv_sems, local_sems = sem_refs[3 * s:3 * s + 3]
        for a in range(len(arrays)):
            src, out = in_refs[pos], out_refs[pos]
            pos += 1

            def remote(k, src_ref, dst_ref, to):
                return pltpu.make_async_remote_copy(src_ref=src_ref, dst_ref=dst_ref, send_sem=send_sems.at[a, k],
                                                    recv_sem=recv_sems.at[a, k], device_id=to, device_id_type=MESH)

            def local(src_ref, dst_ref):
                cp = pltpu.make_async_copy(src_ref, dst_ref, local_sems.at[a])
                starts.append(cp.start)
                waits.append(cp.wait)

            if kind == "gather":
                me = 4 * x + 2 * y + c
                local(src, out.at[me])
                for k, (fx, fy, fc) in enumerate(_FLIPS):
                    px, py, pc = (1 - x if fx else x), (1 - y if fy else y), (1 - c if fc else c)
                    send = remote(k, src, out.at[me], (px, py, pc))
                    starts.append(send.start)
                    waits += [remote(k, src, out.at[4 * px + 2 * py + pc], (px, py, pc)).wait_recv, send.wait_send]
            elif kind == "sibling":
                for j in range(4):
                    cp = remote(j, src.at[2 * j + (1 - c)], out.at[j], (x, y, 1 - c))
                    starts.append(cp.start)
                    waits.append(cp.wait)
            else:
                jme = 2 * x + y
                local(src.at[jme], out.at[jme])
                for k, (px, py) in enumerate([(1 - x, y), (x, 1 - y), (1 - x, 1 - y)]):
                    send = remote(k, src.at[2 * px + py], out.at[jme], (px, py, c))
                    starts.append(send.start)
                    waits += [remote(k, src.at[2 * px + py], out.at[2 * px + py], (px, py, c)).wait_recv,
                              send.wait_send]
    return starts, waits


def _call_with_sides(body, sides, *, name, grid, in_specs, out_specs, out_shape, scratch_shapes, compiler_params, args):
    if not sides:
        res = pl.pallas_call(body, name=name, grid=grid, in_specs=in_specs, out_specs=out_specs, out_shape=out_shape,
                             scratch_shapes=scratch_shapes, compiler_params=compiler_params)(*args)
        return list(res), []
    s_in, s_out, s_scr = _side_plan(sides)
    n_in, n_out, n_scr, n_side = len(in_specs), len(out_specs), len(scratch_shapes), len(s_in)

    def wrapped(*refs):
        refs = list(refs)
        ins, side_in = refs[:n_in], refs[n_in:n_in + n_side]
        outs = refs[n_in + n_side:n_in + n_side + n_out]
        side_out = refs[n_in + n_side + n_out:n_in + 2 * n_side + n_out]
        rest = refs[n_in + 2 * n_side + n_out:]
        starts, waits = _side_copies(sides, side_in, side_out, rest[n_scr:])
        first = functools.reduce(jnp.logical_and, [pl.program_id(d) == 0 for d in range(len(grid))])
        last = functools.reduce(jnp.logical_and, [pl.program_id(d) == grid[d] - 1 for d in range(len(grid))])

        @pl.when(first)
        def _():
            for start in starts:
                start()

        body(*ins, *outs, *rest[:n_scr])

        @pl.when(last)
        def _():
            for wait in waits:
                wait()

    res = pl.pallas_call(
        wrapped, name=name, grid=grid, in_specs=list(in_specs) + [ANY] * n_side,
        out_specs=list(out_specs) + [ANY] * n_side, out_shape=list(out_shape) + s_out,
        scratch_shapes=list(scratch_shapes) + s_scr, compiler_params=compiler_params,
    )(*args, *s_in)
    return list(res[:n_out]), list(res[n_out:])


def _row_tile(r, l):
    t = min(r, max(16, ELEMENTWISE_TILE_BYTES // (4 * l) // 16 * 16))
    while r % t:
        t -= 16
    return t


def _pair_sum(buf, recv, core, name):
    _, r, l = buf.shape
    tr = _row_tile(r, l)

    def body(core_ref, a_ref, b_ref, o_ref):
        o_ref[...] = (a_ref[...] + b_ref[...]).astype(WIRE_DTYPE)

    return pl.pallas_call(
        body, name=name, out_shape=jax.ShapeDtypeStruct((4, r, l), WIRE_DTYPE),
        grid_spec=pltpu.PrefetchScalarGridSpec(
            num_scalar_prefetch=1, grid=(4, r // tr),
            in_specs=[pl.BlockSpec((None, tr, l), lambda j, i, cr: (2 * j + cr[0], i, 0)),
                      pl.BlockSpec((None, tr, l), lambda j, i, cr: (j, i, 0))],
            out_specs=pl.BlockSpec((None, tr, l), lambda j, i, cr: (j, i, 0))),
        compiler_params=_cparams(dimension_semantics=("arbitrary", "arbitrary")),
    )(core, buf, recv)


def _sum_parts(p_ref):
    return ((p_ref[0].astype(F32) + p_ref[1].astype(F32)) + (p_ref[2].astype(F32) + p_ref[3].astype(F32)))


def _sum4(parts, name):
    _, r, l = parts.shape
    tr = _row_tile(r, l)

    def body(p_ref, o_ref):
        o_ref[...] = _sum_parts(p_ref)

    return pl.pallas_call(
        body, name=name, out_shape=jax.ShapeDtypeStruct((r, l), F32), grid=(r // tr,),
        in_specs=[pl.BlockSpec((4, tr, l), lambda i: (0, i, 0))],
        out_specs=pl.BlockSpec((tr, l), lambda i: (i, 0)),
        compiler_params=_cparams(dimension_semantics=("arbitrary",)),
    )(parts)


def _adamw_update(w, gg, m, v):
    nm = ADAM_B1 * m + (1.0 - ADAM_B1) * gg
    nv = ADAM_B2 * v + (1.0 - ADAM_B2) * (gg * gg)
    m_hat = nm / (1.0 - ADAM_B1 ** ADAM_STEP)
    v_hat = nv / (1.0 - ADAM_B2 ** ADAM_STEP)
    return -ADAM_LR * (m_hat / (jnp.sqrt(v_hat) + ADAM_EPS) + ADAM_WD * w), nm, nv


def _adamw(w, g, m, v, name):
    r, l = w.shape
    tr = _row_tile(r, l)

    def body(w_ref, g_ref, m_ref, v_ref, d_ref, nm_ref, nv_ref):
        d_ref[...], nm_ref[...], nv_ref[...] = _adamw_update(w_ref[...], g_ref[...], m_ref[...], v_ref[...])

    spec = pl.BlockSpec((tr, l), lambda i: (i, 0))
    return pl.pallas_call(
        body, name=name, out_shape=[jax.ShapeDtypeStruct((r, l), F32)] * 3, grid=(r // tr,),
        in_specs=[spec] * 4, out_specs=[spec] * 3,
        compiler_params=_cparams(dimension_semantics=("arbitrary",)),
    )(w, g, m, v)


def _adamw_parts(w, parts, m, v, name):
    nl, r, l = w.shape
    tr = _row_tile(r, l)

    def body(*refs):
        w_ref, p_refs, (m_ref, v_ref, g_ref, d_ref, nm_ref, nv_ref) = refs[0], refs[1:1 + nl], refs[1 + nl:]
        layer = pl.program_id(0)
        gg = _sum_parts(p_refs[0])
        for q in range(1, nl):
            gg = jnp.where(layer == q, _sum_parts(p_refs[q]), gg)
        g_ref[...] = gg
        d_ref[...], nm_ref[...], nv_ref[...] = _adamw_update(w_ref[...], gg, m_ref[...], v_ref[...])

    spec = pl.BlockSpec((None, tr, l), lambda q, i: (q, i, 0))
    pspecs = [pl.BlockSpec((4, tr, l), lambda q, i, k=k: (0, jnp.where(q == k, i, 0), 0)) for k in range(nl)]
    return pl.pallas_call(
        body, name=name, out_shape=[jax.ShapeDtypeStruct((nl, r, l), F32)] * 4, grid=(nl, r // tr),
        in_specs=[spec] + pspecs + [spec, spec], out_specs=[spec] * 4,
        compiler_params=_cparams(dimension_semantics=("arbitrary", "arbitrary")),
    )(w, *parts, m, v)


def _to_rows(pieces, row_multiple):
    flat = jnp.concatenate([p.reshape(-1) for p in pieces])
    rows = -(-flat.shape[0] // LANES)
    rows = -(-rows // row_multiple) * row_multiple
    flat = jnp.pad(flat, (0, rows * LANES - flat.shape[0]))
    return flat.reshape(rows, LANES)


def _split_rows(rows, shapes):
    flat = rows.reshape(-1)
    out, off = [], 0
    for s in shapes:
        n = int(np.prod(s))
        out.append(flat[off:off + n].reshape(s))
        off += n
    return out


def _mod_fwd(cond, w_mod, b_my, name):
    nl, _, ncol = w_mod.shape

    def body(a_ref, w_ref, b_ref, o_ref):
        a = a_ref[...]
        s = a * _sigmoid(a)
        for i in range(nl):
            o_ref[i] = _dot(s, w_ref[i]) + b_ref[i]

    return pl.pallas_call(
        body, name=name, out_shape=jax.ShapeDtypeStruct((nl, 16, ncol), F32),
        compiler_params=_cparams(),
    )(cond, w_mod, b_my)


def _mod_bwd(cond, dm_all, dm_my, w_mod, name):
    nl, _, ncol = w_mod.shape

    def body(a_ref, dma_ref, dmm_ref, w_ref, gw_ref, gb_ref, gc_ref):
        a = a_ref[...]
        sg = _sigmoid(a)
        s = a * sg
        for i in range(nl):
            gw_ref[i] = _dot_tn(s, dmm_ref[i])
            gb_ref[i] = jnp.sum(dma_ref[i], axis=0, keepdims=True)
        back = _dot_nt(dmm_ref[0], w_ref[0])
        dsilu = sg * (1.0 + a * (1.0 - sg))
        gc_ref[...] = jnp.sum(back[8:16] * dsilu[8:16], axis=0, keepdims=True)

    return pl.pallas_call(
        body, name=name,
        out_shape=[jax.ShapeDtypeStruct((nl, D_MODEL, ncol), F32), jax.ShapeDtypeStruct((nl, 1, 3 * D_MODEL), F32),
                   jax.ShapeDtypeStruct((1, D_MODEL), F32)],
        compiler_params=_cparams(),
    )(cond, dm_all, dm_my, w_mod)


def _in_proj(xt, sc, sh, wg, name, sides=()):
    t = xt.shape[0]
    tm = min(TM_MM, t)

    def body(x_ref, sc_ref, sh_ref, w_ref, u_ref, g_ref):
        h = (x_ref[...] * (1.0 + sc_ref[...]) + sh_ref[...]).astype(MXU_DTYPE)
        for k in range(N_WBLK):
            o = jnp.dot(h, w_ref[k], preferred_element_type=F32)
            if k < N_WBLK // 2:
                u_ref[:, k * WBLK:(k + 1) * WBLK] = o
            else:
                kk = k - N_WBLK // 2
                g_ref[:, kk * WBLK:(kk + 1) * WBLK] = o

    row = pl.BlockSpec((1, D_MODEL), lambda i: (0, 0))
    return _call_with_sides(
        body, sides, name=name, out_shape=[jax.ShapeDtypeStruct((t, D_INNER), F32)] * 2, grid=(t // tm,),
        in_specs=[pl.BlockSpec((tm, D_MODEL), lambda i: (i, 0)), row, row,
                  pl.BlockSpec((N_WBLK, D_MODEL, WBLK), lambda i: (0, 0, 0), pipeline_mode=pl.Buffered(1))],
        out_specs=[pl.BlockSpec((tm, D_INNER), lambda i: (i, 0))] * 2, scratch_shapes=[],
        compiler_params=_cparams(dimension_semantics=("arbitrary",)), args=[xt, sc, sh, wg])


def _halo_maps(nt, tm, n_rows8, pos):
    per = tm // SUBLANES
    prev = lambda cb, i: (jnp.maximum(pos(i) * per - 1, 0), cb)
    nxt = lambda cb, i: (jnp.minimum((pos(i) + 1) * per, n_rows8 - 1), cb)
    return prev, nxt


def _conv_taps(u, prev8, next8, is_first, is_last):
    pz = jnp.where(is_first, 0.0, 1.0)
    nz = jnp.where(is_last, 0.0, 1.0)
    return _shifted(u, prev8 * pz, next8 * nz, [-2, -1, 1])


def _lru_gates(uv, wa_ref, wx_ref, ba, bx, cl, g):
    sl = slice(g * LANES, (g + 1) * LANES)
    uvg = uv[:, sl]
    r = _sigmoid(_dot(uvg, wa_ref[g]) + ba[:, sl])
    ii = _sigmoid(_dot(uvg, wx_ref[g]) + bx[:, sl])
    la = cl[:, sl] * r
    a = jnp.exp(la)
    q = jnp.tanh(-la) * (1.0 + a * a)
    rs = lax.rsqrt(jnp.maximum(q, SQRT_FLOOR))
    return uvg, r, ii, a, q * rs, rs


def _scan_rows(seg):
    return -(-(SCAN_ROW_T * (seg - 1) + SCAN_ROW_J * (N_SEG - 1) + 1) // SUBLANES) * SUBLANES


def _seg_chunk(j, c):
    return pl.ds(SCAN_ROW_T * SUBLANES * c + SCAN_ROW_J * j, SUBLANES, stride=SCAN_ROW_T)


def _seg_scatter(ref, g, seg, value):
    for j in range(N_SEG):
        for c in range(seg // SUBLANES):
            r0 = j * seg + SUBLANES * c
            ref[g, _seg_chunk(j, c), :] = value[r0:r0 + SUBLANES]


def _scan_tile(a_s, b_s, carry_ref, write_out, seg, reverse):
    n_g = a_s.shape[0]
    unroll = SCAN_UNROLL if seg % SCAN_UNROLL == 0 else 1

    def steps(k, state):
        hs, cs = list(state[0]), list(state[1])
        for q in range(unroll):
            t = k * unroll + q
            if reverse:
                t = seg - 1 - t
            rows = pl.ds(t * SCAN_ROW_T, N_SEG, stride=SCAN_ROW_J)
            for g in range(n_g):
                a = a_s[g, rows, :]
                b = b_s[g, rows, :]
                hs[g] = a * hs[g] + b
                cs[g] = a * cs[g]
                b_s[g, rows, :] = hs[g]
                a_s[g, rows, :] = cs[g]
        return tuple(hs), tuple(cs)

    zeros = tuple(jnp.zeros((N_SEG, LANES), F32) for _ in range(n_g))
    ones = tuple(jnp.ones((N_SEG, LANES), F32) for _ in range(n_g))
    h_fin, a_fin = lax.fori_loop(0, seg // unroll, steps, (zeros, ones))

    order = list(range(N_SEG - 1, -1, -1)) if reverse else list(range(N_SEG))
    for g in range(n_g):
        carry = carry_ref[:, g * LANES:(g + 1) * LANES]
        for j in order:
            for c in range(seg // SUBLANES):
                rows = _seg_chunk(j, c)
                write_out(j, c, g, b_s[g, rows, :] + a_s[g, rows, :] * carry)
            carry = a_fin[g][j:j + 1] * carry + h_fin[g][j:j + 1]
        carry_ref[:, g * LANES:(g + 1) * LANES] = carry


def _lru_specs(s, tm, cb, direction_pos, nt):
    n_rows8 = s // SUBLANES
    prev, nxt = _halo_maps(nt, tm, n_rows8, direction_pos)
    tile = pl.BlockSpec((tm, cb), lambda c, i: (direction_pos(i), c))
    return tile, pl.BlockSpec((SUBLANES, cb), prev), pl.BlockSpec((SUBLANES, cb), nxt)


def _lru_param_specs(cb, d):
    n_g = cb // LANES
    vec = pl.BlockSpec((1, cb), lambda c, i: (0, c))
    dvec = pl.BlockSpec((None, 1, cb), lambda c, i: (d, 0, c))
    wmat = pl.BlockSpec((None, n_g, LRU_BLOCK, LRU_BLOCK), lambda c, i: (d, c, 0, 0))
    return vec, dvec, wmat


def _lru_fwd(src, h0, p, d, name, conv, sides=()):
    s = src.shape[0]
    tm = min(TM_LRU, s)
    cb = CB_LRU
    n_g = cb // LANES
    nt = s // tm
    seg = tm // N_SEG
    pos = (lambda i: i) if d == 0 else (lambda i: nt - 1 - i)

    def body(*refs):
        refs = list(refs)
        u_ref = refs.pop(0)
        if conv:
            up_ref, un_ref, cw_ref, cbias_ref = [refs.pop(0) for _ in range(4)]
        wa_ref, wx_ref, ba_ref, bx_ref, lam_ref, h0_ref, h_ref, hc_ref = [refs.pop(0) for _ in range(8)]
        uv_ref = refs.pop(0) if conv else None
        a_s, b_s = refs
        i = pl.program_id(1)
        tp = pos(i)

        @pl.when(i == 0)
        def _():
            hc_ref[...] = h0_ref[...]

        uv = u_ref[...]
        if conv:
            um2, um1, up1 = _conv_taps(uv, up_ref[...], un_ref[...], tp == 0, tp == nt - 1)
            cw = cw_ref[...]
            uv = um2 * cw[0:1] + um1 * cw[1:2] + uv * cw[2:3] + up1 * cw[3:4] + cbias_ref[...]
            uv_ref[...] = uv
        cl = LRU_C * _log_sigmoid(lam_ref[...])
        ba, bx = ba_ref[...], bx_ref[...]
        for g in range(n_g):
            uvg, r, ii, a, sq, _ = _lru_gates(uv, wa_ref, wx_ref, ba, bx, cl, g)
            b = sq * (ii * uvg)
            _seg_scatter(a_s, g, seg, a)
            _seg_scatter(b_s, g, seg, b)

        def write_out(j, c, g, h):
            h_ref[pl.ds(j * seg + SUBLANES * c, SUBLANES), pl.ds(g * LANES, LANES)] = h

        _scan_tile(a_s, b_s, hc_ref, write_out, seg, reverse=(d == 1))

    tile, prev, nxt = _lru_specs(s, tm, cb, pos, nt)
    vec, dvec, wmat = _lru_param_specs(cb, d)
    wide = jax.ShapeDtypeStruct((s, D_INNER), F32)
    conv_specs = [prev, nxt, pl.BlockSpec((4, cb), lambda c, i: (0, c)), vec] if conv else []
    conv_args = [src, src, p["conv_w"], p["conv_b"]] if conv else []
    return _call_with_sides(
        body, sides, name=name,
        out_shape=[wide, jax.ShapeDtypeStruct((1, D_INNER), F32)] + ([wide] if conv else []),
        grid=(D_INNER // cb, nt),
        in_specs=[tile] + conv_specs + [wmat, wmat, dvec, dvec, dvec, vec],
        out_specs=[tile, vec] + ([tile] if conv else []),
        scratch_shapes=[pltpu.VMEM((n_g, _scan_rows(seg), LANES), F32)] * 2,
        compiler_params=_cparams(dimension_semantics=("arbitrary", "arbitrary")),
        args=[src, *conv_args, p["wa"], p["wx"], p["ba"], p["bx"], p["lam"], h0])


def _lru_bwd(uv, dh, h, h0, lam_in, p, d, name, sides=()):
    s = uv.shape[0]
    tm = min(TM_LRU, s)
    cb = CB_LRU
    n_g = cb // LANES
    nt = s // tm
    seg = tm // N_SEG
    pos = (lambda i: nt - 1 - i) if d == 0 else (lambda i: i)

    def body(uv_ref, dh_ref, h_ref, hh_ref, wa_ref, wx_ref, ba_ref, bx_ref,
             lam_ref, h0_ref, lin_ref, duv_ref, gwa_ref, gwx_ref, gv_ref, lc_ref, a_s, b_s, lp_s,
             r_s, i_s, q_s, rq_s, a_keep):
        i = pl.program_id(1)
        tp = pos(i)

        @pl.when(i == 0)
        def _():
            lc_ref[...] = lin_ref[...]
            gwa_ref[...] = jnp.zeros_like(gwa_ref)
            gwx_ref[...] = jnp.zeros_like(gwx_ref)
            gv_ref[...] = jnp.zeros_like(gv_ref)

        uv = uv_ref[...]
        lam = lam_ref[...]
        cl = LRU_C * _log_sigmoid(lam)
        ba, bx = ba_ref[...], bx_ref[...]
        dh_t = dh_ref[...]
        carry_in = lc_ref[...]
        for g in range(n_g):
            sl = slice(g * LANES, (g + 1) * LANES)
            _, r, ii, a, sq, rs = _lru_gates(uv, wa_ref, wx_ref, ba, bx, cl, g)
            r_s[:, sl], i_s[:, sl], q_s[:, sl], rq_s[:, sl], a_keep[:, sl] = r, ii, sq, rs, a
            b = a * dh_t[:, sl]
            _seg_scatter(a_s, g, seg, a)
            _seg_scatter(b_s, g, seg, b)

        def write_out(j, c, g, v):
            lp_s[pl.ds(j * seg + SUBLANES * c, SUBLANES), pl.ds(g * LANES, LANES)] = v

        _scan_tile(a_s, b_s, lc_ref, write_out, seg, reverse=(d == 0))

        h_t = h_ref[...]
        hh = hh_ref[...]
        if d == 0:
            edge = jnp.where(tp == 0, h0_ref[...], hh[7:8])
            h_prev = _shift_down(h_t, edge)
            lam_t = dh_t + _shift_up(lp_s[...], carry_in)
        else:
            edge = jnp.where(tp == nt - 1, h0_ref[...], hh[0:1])
            h_prev = _shift_up(h_t, edge)
            lam_t = dh_t + _shift_down(lp_s[...], carry_in)

        dsig = LRU_C * _sigmoid(-lam)
        for g in range(n_g):
            sl = slice(g * LANES, (g + 1) * LANES)
            uvg, r, ii, a, sq = uv[:, sl], r_s[:, sl], i_s[:, sl], a_keep[:, sl], q_s[:, sl]
            lt = lam_t[:, sl]
            ls = lt * sq
            dla = (lt * a) * (h_prev[:, sl] - (ii * uvg) * (a * rq_s[:, sl]))
            dzr = (dla * cl[:, sl]) * r * (1.0 - r)
            dzi = (ls * uvg) * ii * (1.0 - ii)
            duv_ref[:, sl] = ls * ii + _dot_nt(dzr, wa_ref[g]) + _dot_nt(dzi, wx_ref[g])
            gwa_ref[g] += _dot_tn(uvg, dzr)
            gwx_ref[g] += _dot_tn(uvg, dzi)
            gv_ref[0:1, sl] += _rowsum(dzr)
            gv_ref[1:2, sl] += _rowsum(dzi)
            gv_ref[2:3, sl] += _rowsum(dla * r) * dsig[:, sl]

    tile, prev, nxt = _lru_specs(s, tm, cb, pos, nt)
    vec, dvec, wmat = _lru_param_specs(cb, d)
    hh_spec = prev if d == 0 else nxt
    gw_spec = pl.BlockSpec((n_g, LRU_BLOCK, LRU_BLOCK), lambda c, i: (c, 0, 0))
    n_blk = D_INNER // LRU_BLOCK
    return _call_with_sides(
        body, sides, name=name,
        out_shape=[jax.ShapeDtypeStruct((s, D_INNER), F32),
                   jax.ShapeDtypeStruct((n_blk, LRU_BLOCK, LRU_BLOCK), F32),
                   jax.ShapeDtypeStruct((n_blk, LRU_BLOCK, LRU_BLOCK), F32),
                   jax.ShapeDtypeStruct((SUBLANES, D_INNER), F32),
                   jax.ShapeDtypeStruct((1, D_INNER), F32)],
        grid=(D_INNER // cb, nt),
        in_specs=[tile, tile, tile, hh_spec, wmat, wmat, dvec, dvec, dvec, vec, vec],
        out_specs=[tile, gw_spec, gw_spec, pl.BlockSpec((SUBLANES, cb), lambda c, i: (0, c)), vec],
        scratch_shapes=[pltpu.VMEM((n_g, _scan_rows(seg), LANES), F32)] * 2 + [pltpu.VMEM((tm, cb), F32)] * 6,
        compiler_params=_cparams(dimension_semantics=("arbitrary", "arbitrary")),
        args=[uv, dh, h, h, p["wa"], p["wx"], p["ba"], p["bx"], p["lam"], h0, lam_in])


def _out0(hf, hb, g, xt, gt, wo, lg, lb, name):
    t = xt.shape[0]
    tm = min(TM_MM, t)

    def body(hf_ref, hb_ref, g_ref, x_ref, gt_ref, w_ref, lg_ref, lb_ref, x1_ref, br_ref):
        gg = g_ref[...]
        p = (hf_ref[...] + hb_ref[...]) * (gg * _sigmoid(gg))
        br = _dot(p, w_ref[...])
        z = ALPHA * x_ref[...] + gt_ref[...] * br
        xhat, _ = _layer_norm_stats(z)
        x1_ref[...] = xhat * lg_ref[...] + lb_ref[...]
        br_ref[...] = br

    wide = pl.BlockSpec((tm, D_INNER), lambda i: (i, 0))
    nar = pl.BlockSpec((tm, D_MODEL), lambda i: (i, 0))
    row = pl.BlockSpec((1, D_MODEL), lambda i: (0, 0))
    return pl.pallas_call(
        body, name=name, out_shape=[jax.ShapeDtypeStruct((t, D_MODEL), F32)] * 2, grid=(t // tm,),
        in_specs=[wide, wide, wide, nar, row,
                  pl.BlockSpec((D_INNER, D_MODEL), lambda i: (0, 0), pipeline_mode=pl.Buffered(1)), row, row],
        out_specs=[nar, nar],
        compiler_params=_cparams(dimension_semantics=("arbitrary",)),
    )(hf, hb, g, xt, gt, wo, lg, lb)


def _unrolled_loop(n, fn, unroll=4):
    while n % unroll:
        unroll //= 2

    def trip(k, carry):
        for q in range(unroll):
            fn(k * unroll + q)
        return carry
    lax.fori_loop(0, n // unroll, trip, 0)


def _window(n, w):
    t = np.arange(n)
    return np.clip(t - w // 2, 0, n), np.clip(t + w // 2, 0, n)


def _pool_tables(n_rows, transpose):
    boxes, inv_c, inv_r = [], [], []
    for w in POOL_WINDOWS:
        lo, hi = _window(GRID_W, w)
        m = np.zeros((GRID_W, GRID_W), np.float32)
        for r in range(GRID_W):
            m[r, lo[r]:hi[r]] = 1.0
        m = np.kron(np.eye(POOL_TOK // GRID_W, dtype=np.float32), m)
        boxes.append(m.T if transpose else m)
        inv_c.append(np.broadcast_to((1.0 / (hi - lo).astype(np.float32))[:, None], (GRID_W, LANES)))
        lo_r, hi_r = _window(n_rows, w)
        inv_r.append(1.0 / (hi_r - lo_r).astype(np.float32))
    return (jnp.asarray(np.stack(boxes), MXU_DTYPE), jnp.asarray(np.stack(inv_c), F32),
            jnp.asarray(np.stack(inv_r), F32))


def _pool_mix(xin, transpose, out_dtype, name):
    s = xin.shape[0]
    n_rows = s // GRID_W
    pad_t = SUBLANES * GRID_W
    rows_per_blk = POOL_TOK // GRID_W
    n_slab = D_INNER // LANES
    slabs_per_group = POOL_GROUP // LANES
    n_win = len(POOL_WINDOWS)
    boxes, inv_c, inv_r = _pool_tables(n_rows, transpose)

    def body(invr_ref, box_ref, invc_ref, x_ref, o_ref, pad_s):
        k = pl.program_id(0) // slabs_per_group
        pad_s[pl.ds(0, pad_t), :] = jnp.zeros((pad_t, LANES), F32)
        pad_s[pl.ds(pad_t + s, pad_t), :] = jnp.zeros((pad_t, LANES), F32)

        for kk, w in enumerate(POOL_WINDOWS):
            half = w // 2
            offsets = list(range(-(half - 1), half + 1)) if transpose else list(range(-half, half))

            @pl.when(k == kk)
            def _():
                inv_col = invc_ref[kk]

                def col_box(b):
                    st = pl.multiple_of(b * POOL_TOK, POOL_TOK)
                    xb = x_ref[pl.ds(st, POOL_TOK), :]
                    if transpose:
                        xb = xb * jnp.concatenate(
                            [inv_col * invr_ref[kk, b * rows_per_blk + q] for q in range(rows_per_blk)], axis=0)
                    hi = xb.astype(MXU_DTYPE)
                    lo = (xb - hi.astype(F32)).astype(MXU_DTYPE)
                    both = jnp.dot(box_ref[kk], jnp.concatenate([hi, lo], axis=1), preferred_element_type=F32)
                    pad_s[pl.ds(pad_t + st, POOL_TOK), :] = both[:, :LANES] + both[:, LANES:]
                _unrolled_loop(s // POOL_TOK, col_box)

                def row_box(r):
                    st = pl.multiple_of(r * GRID_W, GRID_W)
                    acc = pad_s[pl.ds(pad_t + st + offsets[0] * GRID_W, GRID_W), :]
                    for o in offsets[1:]:
                        acc = acc + pad_s[pl.ds(pad_t + st + o * GRID_W, GRID_W), :]
                    if not transpose:
                        acc = acc * (inv_col * invr_ref[kk, r])
                    o_ref[pl.ds(st, GRID_W), :] = (acc - x_ref[pl.ds(st, GRID_W), :]).astype(out_dtype)
                _unrolled_loop(n_rows, row_box)

    slab = pl.BlockSpec((s, LANES), lambda i: (0, i))
    return pl.pallas_call(
        body, name=name, out_shape=jax.ShapeDtypeStruct((s, D_INNER), out_dtype), grid=(n_slab,),
        in_specs=[pl.BlockSpec(memory_space=pltpu.SMEM),
                  pl.BlockSpec((n_win, POOL_TOK, POOL_TOK), lambda i: (0, 0, 0)),
                  pl.BlockSpec((n_win, GRID_W, LANES), lambda i: (0, 0, 0)), slab],
        out_specs=slab,
        scratch_shapes=[pltpu.VMEM((s + 2 * pad_t, LANES), F32)],
        compiler_params=_cparams(dimension_semantics=("arbitrary",)),
    )(inv_r, boxes, inv_c, xin)


def _out1(dmix, pw, ps, g, x1, gt, wo, lg, lb, tgt, name):
    t = x1.shape[0]
    tm = min(TM_MM, t)
    n_grp = len(POOL_WINDOWS)

    def body(d_ref, pw_ref, ps_ref, g_ref, x1_ref, gt_ref, w_ref, lg_ref, lb_ref, tgt_ref, dz_ref, st_ref):
        @pl.when(pl.program_id(0) == 0)
        def _():
            st_ref[...] = jnp.zeros_like(st_ref)

        br = jnp.zeros((tm, D_MODEL), F32)
        for k in range(n_grp):
            sl = slice(k * POOL_GROUP, (k + 1) * POOL_GROUP)
            y = jnp.dot(d_ref[:, sl], pw_ref[k], preferred_element_type=F32) * ps_ref[:, sl]
            gg = g_ref[:, sl]
            br = br + _dot(y * (gg * _sigmoid(gg)), w_ref[sl, :])
        z = ALPHA * x1_ref[...] + gt_ref[...] * br
        xhat, rstd = _layer_norm_stats(z)
        lg_v = lg_ref[...]
        err = xhat * lg_v + lb_ref[...] - tgt_ref[...]
        dy = err * (1.0 / D_MODEL)
        dz = _layer_norm_bwd(dy, xhat, rstd, lg_v)
        dz_ref[...] = dz
        st_ref[0:1, :] += _rowsum(dy * xhat)
        st_ref[1:2, :] += _rowsum(dy)
        st_ref[2:3, :] += _rowsum(dz * br)
        st_ref[3:4, :] += _rowsum(err * err)

    wide = pl.BlockSpec((tm, D_INNER), lambda i: (i, 0))
    nar = pl.BlockSpec((tm, D_MODEL), lambda i: (i, 0))
    row = pl.BlockSpec((1, D_MODEL), lambda i: (0, 0))
    return pl.pallas_call(
        body, name=name,
        out_shape=[jax.ShapeDtypeStruct((t, D_MODEL), F32), jax.ShapeDtypeStruct((SUBLANES, D_MODEL), F32)],
        grid=(t // tm,),
        in_specs=[wide, pl.BlockSpec((n_grp, POOL_GROUP, POOL_GROUP), lambda i: (0, 0, 0)),
                  pl.BlockSpec((1, D_INNER), lambda i: (0, 0)), wide, nar, row,
                  pl.BlockSpec((D_INNER, D_MODEL), lambda i: (0, 0), pipeline_mode=pl.Buffered(1)), row, row, nar],
        out_specs=[nar, pl.BlockSpec((SUBLANES, D_MODEL), lambda i: (0, 0))],
        compiler_params=_cparams(dimension_semantics=("arbitrary",)),
    )(dmix, pw, ps, g, x1, gt, wo, lg, lb, tgt)


def _flush(acc, out_hbm, sem):
    cp = pltpu.make_async_copy(acc, out_hbm, sem)
    cp.start()
    cp.wait()


def _bout1(dz, dmix, g, pw, ps, gt, wo, name):
    t = dz.shape[0]
    tm = min(TM_MM, t)
    nt = t // tm
    n_grp = len(POOL_WINDOWS)

    def body(dz_ref, d_ref, g_ref, pw_ref, ps_ref, gt_ref, w_ref, dd_ref, dg_ref, gwo_hbm, gpw_hbm, gps_ref,
             gwo_acc, gpw_acc, sems):
        i = pl.program_id(0)

        @pl.when(i == 0)
        def _():
            gwo_acc[...] = jnp.zeros_like(gwo_acc)
            gpw_acc[...] = jnp.zeros_like(gpw_acc)
            gps_ref[...] = jnp.zeros_like(gps_ref)

        db = (gt_ref[...] * dz_ref[...]).astype(MXU_DTYPE)
        for k in range(n_grp):
            sl = slice(k * POOL_GROUP, (k + 1) * POOL_GROUP)
            dk = d_ref[:, sl]
            po = jnp.dot(dk, pw_ref[k], preferred_element_type=F32)
            psk = ps_ref[:, sl]
            y = po * psk
            gg = g_ref[:, sl]
            sg = _sigmoid(gg)
            silu = gg * sg
            gwo_acc[sl, :] += _dot_tn(y * silu, db)
            dp = _dot_nt(db, w_ref[sl, :])
            dy = dp * silu
            dg_ref[:, sl] = (dp * y * (sg * (1.0 + gg * (1.0 - sg)))).astype(MXU_DTYPE)
            gps_ref[0:1, sl] += _rowsum(dy * po)
            dpo = (dy * psk).astype(MXU_DTYPE)
            gpw_acc[k] += _dot_tn(dk, dpo)
            dd_ref[:, sl] = _dot_nt(dpo, pw_ref[k])

        @pl.when(i == nt - 1)
        def _():
            _flush(gwo_acc, gwo_hbm, sems.at[0])
            _flush(gpw_acc, gpw_hbm, sems.at[1])

    wide = pl.BlockSpec((tm, D_INNER), lambda i: (i, 0))
    nar = pl.BlockSpec((tm, D_MODEL), lambda i: (i, 0))
    return pl.pallas_call(
        body, name=name,
        out_shape=[jax.ShapeDtypeStruct((t, D_INNER), F32), jax.ShapeDtypeStruct((t, D_INNER), MXU_DTYPE),
                   jax.ShapeDtypeStruct((D_INNER, D_MODEL), F32),
                   jax.ShapeDtypeStruct((n_grp, POOL_GROUP, POOL_GROUP), F32),
                   jax.ShapeDtypeStruct((SUBLANES, D_INNER), F32)],
        grid=(nt,),
        in_specs=[nar, wide, wide,
                  pl.BlockSpec((n_grp, POOL_GROUP, POOL_GROUP), lambda i: (0, 0, 0), pipeline_mode=pl.Buffered(1)),
                  pl.BlockSpec((1, D_INNER), lambda i: (0, 0)), pl.BlockSpec((1, D_MODEL), lambda i: (0, 0)),
                  pl.BlockSpec((D_INNER, D_MODEL), lambda i: (0, 0), pipeline_mode=pl.Buffered(1))],
        out_specs=[wide, wide, ANY, ANY, pl.BlockSpec((SUBLANES, D_INNER), lambda i: (0, 0))],
        scratch_shapes=[pltpu.VMEM((D_INNER, D_MODEL), F32), pltpu.VMEM((n_grp, POOL_GROUP, POOL_GROUP), F32),
                        pltpu.SemaphoreType.DMA((2,))],
        compiler_params=_cparams(dimension_semantics=("arbitrary",)),
    )(dz, dmix, g, pw, ps, gt, wo)


def _bout0(dx1, xt, br0, lg, hf, hb, g, gt, wo, name, sides=()):
    t = dx1.shape[0]
    tm = min(TM_BWD, t)
    nt = t // tm

    def body(dx_ref, x_ref, br_ref, lg_ref, hf_ref, hb_ref, g_ref, gt_ref, w_ref,
             dz_ref, dy_ref, dg_ref, gwo_hbm, st_ref, gwo_acc, sem):
        i = pl.program_id(0)

        @pl.when(i == 0)
        def _():
            gwo_acc[...] = jnp.zeros_like(gwo_acc)
            st_ref[...] = jnp.zeros_like(st_ref)

        dx = dx_ref[...]
        br = br_ref[...]
        gate = gt_ref[...]
        xhat, rstd = _layer_norm_stats(ALPHA * x_ref[...] + gate * br)
        dz = _layer_norm_bwd(dx, xhat, rstd, lg_ref[...])
        dz_ref[...] = dz
        st_ref[0:1, :] += _rowsum(dx * xhat)
        st_ref[1:2, :] += _rowsum(dx)
        st_ref[2:3, :] += _rowsum(dz * br)
        db = (gate * dz).astype(MXU_DTYPE)
        y = hf_ref[...] + hb_ref[...]
        gg = g_ref[...]
        sg = _sigmoid(gg)
        silu = gg * sg
        gwo_acc[...] += _dot_tn(y * silu, db)
        dp = _dot_nt(db, w_ref[...])
        dy_ref[...] = dp * silu
        dg_ref[...] = (dp * y * (sg * (1.0 + gg * (1.0 - sg)))).astype(MXU_DTYPE)

        @pl.when(i == nt - 1)
        def _():
            _flush(gwo_acc, gwo_hbm, sem)

    wide = pl.BlockSpec((tm, D_INNER), lambda i: (i, 0))
    nar = pl.BlockSpec((tm, D_MODEL), lambda i: (i, 0))
    row = pl.BlockSpec((1, D_MODEL), lambda i: (0, 0))
    return _call_with_sides(
        body, sides, name=name,
        out_shape=[jax.ShapeDtypeStruct((t, D_MODEL), F32), jax.ShapeDtypeStruct((t, D_INNER), F32),
                   jax.ShapeDtypeStruct((t, D_INNER), MXU_DTYPE), jax.ShapeDtypeStruct((D_INNER, D_MODEL), F32),
                   jax.ShapeDtypeStruct((SUBLANES, D_MODEL), F32)],
        grid=(nt,),
        in_specs=[nar, nar, nar, row, wide, wide, wide, row,
                  pl.BlockSpec((D_INNER, D_MODEL), lambda i: (0, 0), pipeline_mode=pl.Buffered(1))],
        out_specs=[nar, wide, wide, ANY, pl.BlockSpec((SUBLANES, D_MODEL), lambda i: (0, 0))],
        scratch_shapes=[pltpu.VMEM((D_INNER, D_MODEL), F32), pltpu.SemaphoreType.DMA(())],
        compiler_params=_cparams(dimension_semantics=("arbitrary",)),
        args=[dx1, xt, br0, lg, hf, hb, g, gt, wo])


def _conv_bwd(duvf, duvb, u, conv_w, name):
    s = u.shape[0]
    tm = min(TM_LRU, s)
    cb = CB_LRU
    nt = s // tm

    def body(df_ref, dfp_ref, dfn_ref, db_ref, dbp_ref, dbn_ref, u_ref, cw_ref, du_ref, cst_ref):
        i = pl.program_id(1)

        @pl.when(i == 0)
        def _():
            cst_ref[...] = jnp.zeros_like(cst_ref)

        first, last = i == 0, i == nt - 1
        pz = jnp.where(first, 0.0, 1.0)
        nz = jnp.where(last, 0.0, 1.0)
        dout = df_ref[...] + db_ref[...]
        dm1, dp1, dp2 = _shifted(dout, (dfp_ref[...] + dbp_ref[...]) * pz, (dfn_ref[...] + dbn_ref[...]) * nz,
                                 [-1, 1, 2])
        cw = cw_ref[...]
        du_ref[...] = (dp2 * cw[0:1] + dp1 * cw[1:2] + dout * cw[2:3] + dm1 * cw[3:4]).astype(MXU_DTYPE)
        u_t = u_ref[...]
        cst_ref[0:1, :] += _rowsum(dp2 * u_t)
        cst_ref[1:2, :] += _rowsum(dp1 * u_t)
        cst_ref[2:3, :] += _rowsum(dout * u_t)
        cst_ref[3:4, :] += _rowsum(dm1 * u_t)
        cst_ref[4:5, :] += _rowsum(dout)

    tile, prev, nxt = _lru_specs(s, tm, cb, lambda i: i, nt)
    return pl.pallas_call(
        body, name=name,
        out_shape=[jax.ShapeDtypeStruct((s, D_INNER), MXU_DTYPE), jax.ShapeDtypeStruct((SUBLANES, D_INNER), F32)],
        grid=(D_INNER // cb, nt),
        in_specs=[tile, prev, nxt] * 2 + [tile, pl.BlockSpec((4, cb), lambda c, i: (0, c))],
        out_specs=[tile, pl.BlockSpec((SUBLANES, cb), lambda c, i: (0, c))],
        compiler_params=_cparams(dimension_semantics=("arbitrary", "arbitrary")),
    )(duvf, duvf, duvf, duvb, duvb, duvb, u, conv_w)


def _bin(du, dg, xin, dzin, sc, sh, wg, name, gw_init=None):
    t = xin.shape[0]
    tm = min(TM_MM, t)
    nt = t // tm
    has_g, has_dx, has_init = dg is not None, dzin is not None, gw_init is not None
    half = N_WBLK // 2
    n_blk = N_WBLK if has_g else half

    def body(*refs):
        refs = list(refs)
        du_ref = refs.pop(0)
        dg_ref = refs.pop(0) if has_g else None
        x_ref = refs.pop(0)
        dz_ref = refs.pop(0) if has_dx else None
        sc_ref, sh_ref, w_ref = refs.pop(0), refs.pop(0), refs.pop(0)
        init_hbm = refs.pop(0) if has_init else None
        dx_ref = refs.pop(0) if has_dx else None
        gw_hbm, st_ref, gw_acc, sem = refs
        i = pl.program_id(0)

        @pl.when(i == 0)
        def _():
            st_ref[...] = jnp.zeros_like(st_ref)
            first_zero = 0
            if has_init:
                _flush(init_hbm, gw_acc.at[pl.ds(0, half)], sem)
                first_zero = half
            for k in range(first_zero, n_blk):
                gw_acc[k] = jnp.zeros((D_MODEL, WBLK), F32)

        xv = x_ref[...]
        scale = 1.0 + sc_ref[...]
        h = (xv * scale + sh_ref[...]).astype(MXU_DTYPE)
        dh = None
        for k in range(n_blk):
            src = du_ref if k < half else dg_ref
            kk = k % half
            dk = src[:, kk * WBLK:(kk + 1) * WBLK]
            gw_acc[k] += _dot_tn(h, dk)
            contrib = _dot_nt(dk, w_ref[k])
            dh = contrib if dh is None else dh + contrib
        st_ref[0:1, :] += _rowsum(dh * xv)
        st_ref[1:2, :] += _rowsum(dh)
        if has_dx:
            dx_ref[...] = ALPHA * dz_ref[...] + dh * scale

        @pl.when(i == nt - 1)
        def _():
            _flush(gw_acc, gw_hbm, sem)

    wide = pl.BlockSpec((tm, D_INNER), lambda i: (i, 0))
    nar = pl.BlockSpec((tm, D_MODEL), lambda i: (i, 0))
    row = pl.BlockSpec((1, D_MODEL), lambda i: (0, 0))
    wspec = pl.BlockSpec((n_blk, D_MODEL, WBLK), lambda i: (0, 0, 0), pipeline_mode=pl.Buffered(1))
    in_specs = ([wide] + ([wide] if has_g else []) + [nar] + ([nar] if has_dx else []) + [row, row, wspec]
                + ([ANY] if has_init else []))
    args = ([du] + ([dg] if has_g else []) + [xin] + ([dzin] if has_dx else []) + [sc, sh, wg]
            + ([gw_init] if has_init else []))
    out_shape = ([jax.ShapeDtypeStruct((t, D_MODEL), F32)] if has_dx else []) + [
        jax.ShapeDtypeStruct((n_blk, D_MODEL, WBLK), F32), jax.ShapeDtypeStruct((SUBLANES, D_MODEL), F32)]
    out_specs = ([nar] if has_dx else []) + [ANY, pl.BlockSpec((SUBLANES, D_MODEL), lambda i: (0, 0))]
    return pl.pallas_call(
        body, name=name, out_shape=out_shape, grid=(nt,), in_specs=in_specs, out_specs=out_specs,
        scratch_shapes=[pltpu.VMEM((n_blk, D_MODEL, WBLK), F32), pltpu.SemaphoreType.DMA(())],
        compiler_params=_cparams(dimension_semantics=("arbitrary",)),
    )(*args)


def _blocks_by_device(a, axis):
    shape = a.shape
    a = a.reshape(shape[:axis] + (N_DEV, shape[axis] // N_DEV) + shape[axis + 1:])
    return jnp.moveaxis(a, axis, 0)


def kernel(x, c, ctx, c_ctx, w_mod, b_mod, w_in, w_out, ln_g, ln_b, conv_w, conv_b, lru_wa, lru_ba, lru_wx, lru_bx, lru_lam, pool_w, pool_scale, loss_target, m_c_ctx, m_w_mod, m_b_mod, m_w_in, m_w_out, m_ln_g, m_ln_b, m_conv_w, m_conv_b, m_lru_wa, m_lru_ba, m_lru_wx, m_lru_bx, m_lru_lam, m_pool_w, m_pool_scale, v_c_ctx, v_w_mod, v_b_mod, v_w_in, v_w_out, v_ln_g, v_ln_b, v_conv_w, v_conv_b, v_lru_wa, v_lru_ba, v_lru_wx, v_lru_bx, v_lru_lam, v_pool_w, v_pool_scale):
    xi, yi, ci = _my_pos()
    dev = 4 * xi + 2 * yi + ci
    xt, ctxt, tgt = x[0], ctx[0], loss_target[0]
    n_mod = w_mod.shape[2]

    small_shapes = [(D_MODEL,), conv_w.shape[1:], lru_ba.shape[1:], lru_bx.shape[1:], lru_lam.shape[1:],
                    pool_scale.shape[1:]]
    small = _to_rows([c[0], conv_w[0], lru_ba[0], lru_bx[0], lru_lam[0], pool_scale[0]], SUBLANES)
    small_all, = _all_gather([small], "gather_small")
    pieces = [_split_rows(small_all[k], small_shapes) for k in range(N_DEV)]
    c_all = jnp.stack([p[0] for p in pieces])
    conv_w_f = jnp.concatenate([p[1] for p in pieces], axis=-1)
    lru_ba_f = jnp.concatenate([p[2] for p in pieces], axis=-1)[:, None, :]
    lru_bx_f = jnp.concatenate([p[3] for p in pieces], axis=-1)[:, None, :]
    lru_lam_f = jnp.concatenate([p[4] for p in pieces], axis=-1)[:, None, :]
    pool_scale_f = jnp.concatenate([p[5] for p in pieces], axis=-1)[None, :]

    cond = jnp.concatenate([c_all, jnp.broadcast_to(c_ctx[None, :], (N_DEV, D_MODEL))], axis=0)
    b_my = lax.dynamic_slice(b_mod, (0, dev * n_mod), (2, n_mod))[:, None, :]
    mod_part = _mod_fwd(cond, w_mod, b_my, "mod_fwd")
    mod_all, = _all_gather([mod_part], "gather_mod")
    mod = jnp.transpose(mod_all, (1, 2, 0, 3)).reshape(2, 16, 3 * D_MODEL)
    mod_me = lax.dynamic_slice(mod, (0, dev, 0), (2, 1, 3 * D_MODEL))
    sh = [mod_me[i, :, 0:D_MODEL] for i in range(2)]
    sc = [mod_me[i, :, D_MODEL:2 * D_MODEL] for i in range(2)]
    gt = [mod_me[i, :, 2 * D_MODEL:] for i in range(2)]
    shc, scc = mod[0, 8:9, 0:D_MODEL], mod[0, 8:9, D_MODEL:2 * D_MODEL]

    wi0, = _all_gather([w_in[0].astype(MXU_DTYPE)], "gather_weights0")
    lg = [ln_g[i][None, :] for i in range(2)]
    lb = [ln_b[i][None, :] for i in range(2)]
    lru_p = dict(conv_w=conv_w_f, conv_b=conv_b, wa=lru_wa[0].astype(MXU_DTYPE), wx=lru_wx[0].astype(MXU_DTYPE),
                 ba=lru_ba_f, bx=lru_bx_f, lam=lru_lam_f)
    zero_state = jnp.zeros((1, D_INNER), F32)

    (u0, g0), (wo0,) = _in_proj(xt, sc[0], sh[0], wi0, "in_proj0", sides=[("gather", [w_out[0].astype(MXU_DTYPE)])])
    (uc, _), _ = _in_proj(ctxt, scc, shc, wi0, "in_proj0_ctx")
    (hcf, cf, uvc), _ = _lru_fwd(uc, zero_state, lru_p, 0, "lru_fwd_ctx_f", conv=True)
    (hcb, cbk), _ = _lru_fwd(uvc, zero_state, lru_p, 1, "lru_fwd_ctx_b", conv=False)
    (hf, _, uv0), (wi1,) = _lru_fwd(u0, cf, lru_p, 0, "lru_fwd_f", conv=True,
                                    sides=[("gather", [w_in[1].astype(MXU_DTYPE)])])
    (hb, _), (wo1, pool_w_g) = _lru_fwd(
        uv0, cbk, lru_p, 1, "lru_fwd_b", conv=False,
        sides=[("gather", [w_out[1].astype(MXU_DTYPE), pool_w[0].astype(MXU_DTYPE)])])
    w_in_l = [wi0, wi1]
    w_out_l = [wo0.reshape(D_INNER, D_MODEL), wo1.reshape(D_INNER, D_MODEL)]
    pool_w_f = jnp.transpose(pool_w_g, (1, 0, 2, 3)).reshape(len(POOL_WINDOWS), POOL_GROUP, POOL_GROUP)
    x1, br0 = _out0(hf, hb, g0, xt, gt[0], w_out_l[0], lg[0], lb[0], "out0")
    (u1, g1), _ = _in_proj(x1, sc[1], sh[1], w_in_l[1], "in_proj1")
    dmix = _pool_mix(u1, False, MXU_DTYPE, "pool_fwd")
    dz1, st1 = _out1(dmix, pool_w_f, pool_scale_f, g1, x1, gt[1], w_out_l[1], lg[1], lb[1], tgt, "out1")
    loss_me = jnp.full((1, LANES), (0.5 / D_MODEL) * jnp.sum(st1[3]), F32)

    core = jnp.reshape(ci, (1,)).astype(jnp.int32)
    wo_view = lambda a: a.reshape(N_DEV, D_INNER // N_DEV, D_MODEL)
    pw_view = lambda a: _blocks_by_device(a, 1).reshape(N_DEV, POOL_GROUP // N_DEV * len(POOL_WINDOWS), POOL_GROUP)
    dd, dg1, gwo1, gpw, gps = _bout1(dz1, dmix, g1, pool_w_f, pool_scale_f, gt[1], w_out_l[1], "bwd_out1")
    du1 = _pool_mix(dd, True, MXU_DTYPE, "pool_bwd")
    dx1, gwi1, stb1 = _bin(du1, dg1, x1, dz1, sc[1], sh[1], w_in_l[1], "bwd_in1")
    bufs1 = [gwi1, wo_view(gwo1), pw_view(gpw)]
    (dz0, dy0, dg0, gwo0, stl0), recv1 = _bout0(dx1, xt, br0, lg[0], hf, hb, g0, gt[0], w_out_l[0], "bwd_out0",
                                                sides=[("sibling", bufs1)])
    pairs1 = [_pair_sum(b, r, core, "reduce_pair_" + n)
              for b, r, n in zip(bufs1, recv1, ["w_in1", "w_out1", "pool_w"])]
    (duvf, gwa_f, gwx_f, gv_f, dh0f), (p_wi1, p_wo1, p_pw, recv_wo0) = _lru_bwd(
        uv0, dy0, hf, cf, zero_state, lru_p, 0, "lru_bwd_f", sides=[("chips", pairs1), ("sibling", [wo_view(gwo0)])])
    pair_wo0 = _pair_sum(wo_view(gwo0), recv_wo0, core, "reduce_pair_w_out0")
    (duvb, gwa_b, gwx_b, gv_b, dh0b), (p_wo0,) = _lru_bwd(
        uv0, dy0, hb, cbk, zero_state, lru_p, 1, "lru_bwd_b", sides=[("chips", [pair_wo0])])
    zero_dh = jnp.zeros_like(uc)
    (ducf, gwa_cf, gwx_cf, gv_cf, _), _ = _lru_bwd(uvc, zero_dh, hcf, zero_state, dh0f, lru_p, 0, "lru_bwd_ctx_f")
    (ducb, gwa_cb, gwx_cb, gv_cb, _), _ = _lru_bwd(uvc, zero_dh, hcb, zero_state, dh0b, lru_p, 1, "lru_bwd_ctx_b")
    du0, cst0 = _conv_bwd(duvf, duvb, u0, conv_w_f, "conv_bwd")
    duc, cstc = _conv_bwd(ducf, ducb, uc, conv_w_f, "conv_bwd_ctx")
    gwic, stc = _bin(duc, None, ctxt, None, scc, shc, w_in_l[0][:N_WBLK // 2], "bwd_in0_ctx")
    gx, gwi0, stb0 = _bin(du0, dg0, xt, dz0, sc[0], sh[0], w_in_l[0], "bwd_in0", gw_init=gwic)

    zero_row = jnp.zeros((1, D_MODEL), F32)
    dm_me = jnp.stack([
        jnp.concatenate([jnp.concatenate([stb0[1:2], stb0[0:1], stl0[2:3]], axis=1),
                         jnp.concatenate([stc[1:2], stc[0:1], zero_row], axis=1)], axis=0),
        jnp.concatenate([jnp.concatenate([stb1[1:2], stb1[0:1], st1[2:3]], axis=1),
                         jnp.zeros((1, 3 * D_MODEL), F32)], axis=0)])
    dm_g, loss_g = _all_gather([dm_me, loss_me], "gather_dmod")
    loss = jnp.sum(loss_g[:, 0, 0])
    dm_all = jnp.concatenate([jnp.transpose(dm_g[:, :, 0], (1, 0, 2)), jnp.transpose(dm_g[:, :, 1], (1, 0, 2))],
                             axis=1)
    dm_my = lax.dynamic_slice(dm_all, (0, 0, dev * n_mod), (2, 16, n_mod))
    g_w_mod, g_b_mod, gcc_part = _mod_bwd(cond, dm_all, dm_my, w_mod, "mod_bwd")
    g_b_mod = g_b_mod.reshape(b_mod.shape)

    gwa = jnp.stack([gwa_f + gwa_cf, gwa_b + gwa_cb])
    gwx = jnp.stack([gwx_f + gwx_cf, gwx_b + gwx_cb])
    gv = jnp.stack([gv_f + gv_cf, gv_b + gv_cb])
    cst = cst0 + cstc
    g_ln_g = jnp.stack([stl0[0], st1[0]])
    g_ln_b = jnp.stack([stl0[1], st1[1]])
    sharded = [
        _blocks_by_device(cst[0:4], 1),
        _blocks_by_device(gv[:, 0], 1), _blocks_by_device(gv[:, 1], 1), _blocks_by_device(gv[:, 2], 1),
        _blocks_by_device(gps[0], 0),
    ]
    replicated = [gwa.reshape(-1), gwx.reshape(-1), g_ln_g.reshape(-1), g_ln_b.reshape(-1), cst[4],
                  gcc_part.reshape(-1)]
    sh_sizes = [int(np.prod(a.shape[1:])) for a in sharded]
    rep_sizes = [a.shape[0] // N_DEV for a in replicated]
    n_flat = sum(sh_sizes) + sum(rep_sizes)
    rows = -(-n_flat // LANES)
    rows = -(-rows // FLAT_ROWS) * FLAT_ROWS
    misc = jnp.concatenate([a.reshape(N_DEV, -1) for a in sharded] +
                           [a.reshape(N_DEV, -1) for a in replicated], axis=1)
    misc = jnp.pad(misc, ((0, 0), (0, rows * LANES - n_flat))).reshape(N_DEV, rows, LANES)
    bufs = [gwi0, misc]
    recvs = _sibling_exchange(bufs, "reduce_sibling")
    pairs = [_pair_sum(b, r, core, "reduce_pair_" + n) for b, r, n in zip(bufs, recvs, ["w_in0", "misc"])]
    p_wi0, p_misc = _chip_exchange(pairs, "reduce_chips")
    g_flat = _sum4(p_misc, "reduce_sum_misc").reshape(-1)

    offs = np.cumsum([0] + sh_sizes + rep_sizes)
    n_sh = len(sh_sizes)
    sh_shapes = [conv_w.shape, lru_ba.shape, lru_bx.shape, lru_lam.shape, pool_scale.shape]
    g_sh = [g_flat[offs[k]:offs[k + 1]].reshape(sh_shapes[k]) for k in range(n_sh)]
    g_conv_w, g_lru_ba, g_lru_bx, g_lru_lam, g_pool_scale = g_sh
    rep_block = _to_rows([g_flat[offs[n_sh]:offs[-1]]], SUBLANES)
    rep_all, = _all_gather([rep_block], "gather_replicated")
    rep_flat = rep_all.reshape(N_DEV, -1)
    rep_full, off = [], 0
    for n in rep_sizes:
        rep_full.append(rep_flat[:, off:off + n].reshape(-1))
        off += n
    g_lru_wa = rep_full[0].reshape(lru_wa.shape)
    g_lru_wx = rep_full[1].reshape(lru_wx.shape)
    g_ln_g = rep_full[2].reshape(ln_g.shape)
    g_ln_b = rep_full[3].reshape(ln_b.shape)
    g_conv_b = rep_full[4].reshape(conv_b.shape)
    g_c_ctx = rep_full[5].reshape(c_ctx.shape)

    names = ["c_ctx", "w_mod", "b_mod", "w_in", "w_out", "ln_g", "ln_b", "conv_w", "conv_b", "lru_wa", "lru_ba",
             "lru_wx", "lru_bx", "lru_lam", "pool_w", "pool_scale"]
    weights = dict(c_ctx=c_ctx, w_mod=w_mod, b_mod=b_mod, w_in=w_in, w_out=w_out, ln_g=ln_g, ln_b=ln_b,
                   conv_w=conv_w, conv_b=conv_b, lru_wa=lru_wa, lru_ba=lru_ba, lru_wx=lru_wx, lru_bx=lru_bx,
                   lru_lam=lru_lam, pool_w=pool_w, pool_scale=pool_scale)
    mom_m = dict(c_ctx=m_c_ctx, w_mod=m_w_mod, b_mod=m_b_mod, w_in=m_w_in, w_out=m_w_out, ln_g=m_ln_g, ln_b=m_ln_b,
                 conv_w=m_conv_w, conv_b=m_conv_b, lru_wa=m_lru_wa, lru_ba=m_lru_ba, lru_wx=m_lru_wx,
                 lru_bx=m_lru_bx, lru_lam=m_lru_lam, pool_w=m_pool_w, pool_scale=m_pool_scale)
    mom_v = dict(c_ctx=v_c_ctx, w_mod=v_w_mod, b_mod=v_b_mod, w_in=v_w_in, w_out=v_w_out, ln_g=v_ln_g, ln_b=v_ln_b,
                 conv_w=v_conv_w, conv_b=v_conv_b, lru_wa=v_lru_wa, lru_ba=v_lru_ba, lru_wx=v_lru_wx,
                 lru_bx=v_lru_bx, lru_lam=v_lru_lam, pool_w=v_pool_w, pool_scale=v_pool_scale)
    grads = dict(c_ctx=g_c_ctx, w_mod=g_w_mod, b_mod=g_b_mod, ln_g=g_ln_g, ln_b=g_ln_b,
                 conv_w=g_conv_w, conv_b=g_conv_b, lru_wa=g_lru_wa, lru_ba=g_lru_ba, lru_wx=g_lru_wx,
                 lru_bx=g_lru_bx, lru_lam=g_lru_lam)
    grads["pool_scale"] = g_pool_scale
    delta, new_m, new_v = {}, {}, {}

    def update_parts(n, parts, view):
        res = _adamw_parts(weights[n].reshape(view), parts, mom_m[n].reshape(view), mom_v[n].reshape(view),
                           "adamw_" + n)
        grads[n], delta[n], new_m[n], new_v[n] = [r.reshape(weights[n].shape) for r in res]

    update_parts("w_in", [p_wi0, p_wi1], w_in.shape)
    update_parts("w_out", [p_wo0, p_wo1], w_out.shape)
    update_parts("pool_w", [p_pw], (1,) + p_pw.shape[1:])
    for n in ("w_mod", "lru_wa", "lru_wx"):
        shape = weights[n].shape
        view = (int(np.prod(shape[:-1])), shape[-1])
        res = _adamw(weights[n].reshape(view), grads[n].reshape(view), mom_m[n].reshape(view),
                     mom_v[n].reshape(view), "adamw_" + n)
        delta[n], new_m[n], new_v[n] = [r.reshape(shape) for r in res]

    small = [n for n in names if n not in delta]
    shapes = [weights[n].shape for n in small]
    flat = lambda d: _to_rows([d[n] for n in small], FLAT_ROWS)
    res = _adamw(flat(weights), flat(grads), flat(mom_m), flat(mom_v), "adamw_small")
    for d, r in zip((delta, new_m, new_v), res):
        d.update(zip(small, _split_rows(r, shapes)))

    return (loss, gx[None], *[grads[n] for n in names], *[delta[n] for n in names],
            *[new_m[n] for n in names], *[new_v[n] for n in names])
```

```python
import functools

import numpy as np
import jax
import jax.numpy as jnp
from jax import lax
from jax.experimental import pallas as pl
from jax.experimental.pallas import tpu as pltpu

F32 = jnp.float32
BF16 = jnp.bfloat16
MXU_DTYPE = BF16

D_MODEL = 1024
D_INNER = 2048
LRU_BLOCK = 128
GRID_W = 64
POOL_WINDOWS = (2, 4, 8, 16)
POOL_GROUP = 512
ALPHA = float(4 ** 0.25)
LN_EPS = 1e-5
LRU_C = 8.0
N_DEV = 8
N_WBLK = 8
WBLK = 512

ADAM_LR = 0.001
ADAM_B1 = 0.9
ADAM_B2 = 0.999
ADAM_EPS = 1e-08
ADAM_WD = 0.01
ADAM_STEP = 10

LANES = 128
SUBLANES = 8
V7X_VMEM_BYTES = 64 * 1024 * 1024
VMEM_LIMIT = V7X_VMEM_BYTES - 8 * 1024 * 1024
MESH = pl.DeviceIdType.MESH
ANY = pl.BlockSpec(memory_space=pl.ANY)

TM_MM = 512
TM_BWD = 256
TM_LRU = 512
CB_LRU = 512
N_SEG = 8
SCAN_UNROLL = 4
SCAN_ROW_T = 33
SCAN_ROW_J = 4
SQRT_FLOOR = 1e-30
FLAT_ROWS = 16
ELEMENTWISE_TILE_BYTES = 1 << 20
POOL_TOK = 256
WIRE_DTYPE = BF16


def _cparams(**kw):
    return pltpu.CompilerParams(vmem_limit_bytes=VMEM_LIMIT, **kw)


def _my_pos():
    return lax.axis_index("x"), lax.axis_index("y"), lax.axis_index("c")


def _dot(a, b):
    return jnp.dot(a.astype(MXU_DTYPE), b.astype(MXU_DTYPE), preferred_element_type=F32)


def _dot_tn(a, b):
    return lax.dot_general(a.astype(MXU_DTYPE), b.astype(MXU_DTYPE), (((0,), (0,)), ((), ())),
                           preferred_element_type=F32)


def _dot_nt(a, b):
    return lax.dot_general(a.astype(MXU_DTYPE), b.astype(MXU_DTYPE), (((1,), (1,)), ((), ())),
                           preferred_element_type=F32)


def _sigmoid(z):
    return 0.5 * jnp.tanh(0.5 * z) + 0.5


def _log_sigmoid(x):
    y = jnp.exp(-jnp.abs(x))
    u = 1.0 + y
    l1p = jnp.where(u == 1.0, y, jnp.log(u) * (y / jnp.where(u == 1.0, 1.0, u - 1.0)))
    return jnp.minimum(x, 0.0) - l1p


def _rowsum(v):
    return jnp.sum(v, axis=0, keepdims=True)


def _layer_norm_stats(z):
    mu = jnp.mean(z, axis=-1, keepdims=True)
    zc = z - mu
    var = jnp.mean(zc * zc, axis=-1, keepdims=True)
    rstd = lax.rsqrt(var + LN_EPS)
    return zc * rstd, rstd


def _layer_norm_bwd(dy, xhat, rstd, g):
    dxh = dy * g
    m1 = jnp.mean(dxh, axis=-1, keepdims=True)
    m2 = jnp.mean(dxh * xhat, axis=-1, keepdims=True)
    return rstd * (dxh - m1 - xhat * m2)


def _shifted(v, before8, after8, offsets):
    n = v.shape[0]
    ext = jnp.concatenate([before8, v, after8], axis=0)
    total = n + 2 * SUBLANES
    return [pltpu.roll(ext, (-k) % total, 0)[SUBLANES:SUBLANES + n] for k in offsets]


def _rows8(row):
    return jnp.broadcast_to(row, (SUBLANES, row.shape[1]))


def _shift_down(v, first_row):
    return _shifted(v, _rows8(first_row), _rows8(first_row), [-1])[0]


def _shift_up(v, last_row):
    return _shifted(v, _rows8(last_row), _rows8(last_row), [1])[0]


def _all_gather(blocks, name):
    n = len(blocks)

    def body(*refs):
        x_refs, out_refs = refs[:n], refs[n:2 * n]
        send_sems, recv_sems, local_sems = refs[2 * n:]
        x, y, c = _my_pos()
        me, sibling = (x, y, c), (x, y, 1 - c)
        chips = [(1 - x, y), (x, 1 - y), (1 - x, 1 - y)]

        def slot(a, px, py, pc):
            return out_refs[a].at[4 * px + 2 * py + pc]

        def copy(a, k, block, to, src=None):
            return pltpu.make_async_remote_copy(
                src_ref=slot(a, *block) if src is None else src, dst_ref=slot(a, *block),
                send_sem=send_sems.at[a, k], recv_sem=recv_sems.at[a, k], device_id=to, device_id_type=MESH)

        mine = [pltpu.make_async_copy(x_refs[a], slot(a, *me), local_sems.at[a]) for a in range(n)]
        for cp in mine:
            cp.start()
        first = []
        for a in range(n):
            first.append(copy(a, 0, me, sibling, src=x_refs[a]))
            first += [copy(a, 1 + j, me, (*chip, c), src=x_refs[a]) for j, chip in enumerate(chips)]
        for cp in first:
            cp.start()
        passed = []
        for j, chip in enumerate(chips):
            for a in range(n):
                copy(a, 1 + j, (*chip, c), me).wait_recv()
                fwd = copy(a, 4 + j, (*chip, c), sibling)
                fwd.start()
                passed.append(fwd)
        for a in range(n):
            copy(a, 0, sibling, me).wait_recv()
            for j, chip in enumerate(chips):
                copy(a, 4 + j, (*chip, 1 - c), me).wait_recv()
        for cp in first + passed:
            cp.wait_send()
        for cp in mine:
            cp.wait()

    outs = pl.pallas_call(
        body, name=name,
        out_shape=[jax.ShapeDtypeStruct((N_DEV,) + b.shape, b.dtype) for b in blocks],
        in_specs=[ANY] * n, out_specs=[ANY] * n,
        scratch_shapes=[pltpu.SemaphoreType.DMA((n, 7)), pltpu.SemaphoreType.DMA((n, 7)),
                        pltpu.SemaphoreType.DMA((n,))],
    )(*blocks)
    return list(outs)


def _sibling_exchange(bufs, name):
    n = len(bufs)

    def body(*refs):
        srcs, outs = refs[:n], refs[n:2 * n]
        send_sems, recv_sems = refs[2 * n:]
        x, y, c = _my_pos()
        copies = [pltpu.make_async_remote_copy(
            src_ref=srcs[a].at[2 * j + (1 - c)], dst_ref=outs[a].at[j], send_sem=send_sems.at[a, j],
            recv_sem=recv_sems.at[a, j], device_id=(x, y, 1 - c), device_id_type=MESH)
            for a in range(n) for j in range(4)]
        for cp in copies:
            cp.start()
        for cp in copies:
            cp.wait()

    outs = pl.pallas_call(
        body, name=name, out_shape=[jax.ShapeDtypeStruct((4,) + b.shape[1:], b.dtype) for b in bufs],
        in_specs=[ANY] * n, out_specs=[ANY] * n,
        scratch_shapes=[pltpu.SemaphoreType.DMA((n, 4)), pltpu.SemaphoreType.DMA((n, 4))],
    )(*bufs)
    return list(outs)


def _chip_exchange(parts, name):
    n = len(parts)

    def body(*refs):
        srcs, outs = refs[:n], refs[n:2 * n]
        send_sems, recv_sems, local_sems = refs[2 * n:]
        x, y, c = _my_pos()
        jme = 2 * x + y
        peers = [(1 - x, y), (x, 1 - y), (1 - x, 1 - y)]
        local = [pltpu.make_async_copy(srcs[a].at[jme], outs[a].at[jme], local_sems.at[a]) for a in range(n)]
        for cp in local:
            cp.start()

        def copy(a, k, px, py, dst_slot):
            return pltpu.make_async_remote_copy(
                src_ref=srcs[a].at[2 * px + py], dst_ref=outs[a].at[dst_slot], send_sem=send_sems.at[a, k],
                recv_sem=recv_sems.at[a, k], device_id=(px, py, c), device_id_type=MESH)

        sends = [copy(a, k, px, py, jme) for a in range(n) for k, (px, py) in enumerate(peers)]
        for cp in sends:
            cp.start()
        for a in range(n):
            for k, (px, py) in enumerate(peers):
                copy(a, k, px, py, 2 * px + py).wait_recv()
        for cp in sends:
            cp.wait_send()
        for cp in local:
            cp.wait()

    outs = pl.pallas_call(
        body, name=name, out_shape=[jax.ShapeDtypeStruct(p.shape, p.dtype) for p in parts],
        in_specs=[ANY] * n, out_specs=[ANY] * n,
        scratch_shapes=[pltpu.SemaphoreType.DMA((n, 3)), pltpu.SemaphoreType.DMA((n, 3)),
                        pltpu.SemaphoreType.DMA((n,))],
    )(*parts)
    return list(outs)


_SIDE_REMOTE = {"gather": 7, "sibling": 4, "chips": 3}
_FLIPS = [(0, 0, 1), (1, 0, 0), (0, 1, 0), (1, 1, 0), (1, 0, 1), (0, 1, 1), (1, 1, 1)]


def _side_plan(sides):
    inputs, out_shapes, scratch = [], [], []
    for kind, arrays in sides:
        n = len(arrays)
        for a in arrays:
            inputs.append(a)
            shape = {"gather": (N_DEV,) + a.shape, "sibling": (4,) + a.shape[1:], "chips": a.shape}[kind]
            out_shapes.append(jax.ShapeDtypeStruct(shape, a.dtype))
        scratch += [pltpu.SemaphoreType.DMA((n, _SIDE_REMOTE[kind])), pltpu.SemaphoreType.DMA((n, _SIDE_REMOTE[kind])),
                    pltpu.SemaphoreType.DMA((n,))]
    return inputs, out_shapes, scratch


def _side_copies(sides, in_refs, out_refs, sem_refs):
    x, y, c = _my_pos()
    starts, waits = [], []
    pos = 0
    for s, (kind, arrays) in enumerate(sides):
        send_sems, recv_sems, local_sems = sem_refs[3 * s:3 * s + 3]
        for a in range(len(arrays)):
            src, out = in_refs[pos], out_refs[pos]
            pos += 1

            def remote(k, src_ref, dst_ref, to):
                return pltpu.make_async_remote_copy(src_ref=src_ref, dst_ref=dst_ref, send_sem=send_sems.at[a, k],
                                                    recv_sem=recv_sems.at[a, k], device_id=to, device_id_type=MESH)

            def local(src_ref, dst_ref):
                cp = pltpu.make_async_copy(src_ref, dst_ref, local_sems.at[a])
                starts.append(cp.start)
                waits.append(cp.wait)

            if kind == "gather":
                me = 4 * x + 2 * y + c
                local(src, out.at[me])
                for k, (fx, fy, fc) in enumerate(_FLIPS):
                    px, py, pc = (1 - x if fx else x), (1 - y if fy else y), (1 - c if fc else c)
                    send = remote(k, src, out.at[me], (px, py, pc))
                    starts.append(send.start)
                    waits += [remote(k, src, out.at[4 * px + 2 * py + pc], (px, py, pc)).wait_recv, send.wait_send]
            elif kind == "sibling":
                for j in range(4):
                    cp = remote(j, src.at[2 * j + (1 - c)], out.at[j], (x, y, 1 - c))
                    starts.append(cp.start)
                    waits.append(cp.wait)
            else:
                jme = 2 * x + y
                local(src.at[jme], out.at[jme])
                for k, (px, py) in enumerate([(1 - x, y), (x, 1 - y), (1 - x, 1 - y)]):
                    send = remote(k, src.at[2 * px + py], out.at[jme], (px, py, c))
                    starts.append(send.start)
                    waits += [remote(k, src.at[2 * px + py], out.at[2 * px + py], (px, py, c)).wait_recv,
                              send.wait_send]
    return starts, waits


def _call_with_sides(body, sides, *, name, grid, in_specs, out_specs, out_shape, scratch_shapes, compiler_params, args):
    if not sides:
        res = pl.pallas_call(body, name=name, grid=grid, in_specs=in_specs, out_specs=out_specs, out_shape=out_shape,
                             scratch_shapes=scratch_shapes, compiler_params=compiler_params)(*args)
        return list(res), []
    s_in, s_out, s_scr = _side_plan(sides)
    n_in, n_out, n_scr, n_side = len(in_specs), len(out_specs), len(scratch_shapes), len(s_in)

    def wrapped(*refs):
        refs = list(refs)
        ins, side_in = refs[:n_in], refs[n_in:n_in + n_side]
        outs = refs[n_in + n_side:n_in + n_side + n_out]
        side_out = refs[n_in + n_side + n_out:n_in + 2 * n_side + n_out]
        rest = refs[n_in + 2 * n_side + n_out:]
        starts, waits = _side_copies(sides, side_in, side_out, rest[n_scr:])
        first = functools.reduce(jnp.logical_and, [pl.program_id(d) == 0 for d in range(len(grid))])
        last = functools.reduce(jnp.logical_and, [pl.program_id(d) == grid[d] - 1 for d in range(len(grid))])

        @pl.when(first)
        def _():
            for start in starts:
                start()

        body(*ins, *outs, *rest[:n_scr])

        @pl.when(last)
        def _():
            for wait in waits:
                wait()

    res = pl.pallas_call(
        wrapped, name=name, grid=grid, in_specs=list(in_specs) + [ANY] * n_side,
        out_specs=list(out_specs) + [ANY] * n_side, out_shape=list(out_shape) + s_out,
        scratch_shapes=list(scratch_shapes) + s_scr, compiler_params=compiler_params,
    )(*args, *s_in)
    return list(res[:n_out]), list(res[n_out:])


def _row_tile(r, l):
    t = min(r, max(16, ELEMENTWISE_TILE_BYTES // (4 * l) // 16 * 16))
    while r % t:
        t -= 16
    return t


def _pair_sum(buf, recv, core, name):
    _, r, l = buf.shape
    tr = _row_tile(r, l)

    def body(core_ref, a_ref, b_ref, o_ref):
        o_ref[...] = (a_ref[...] + b_ref[...]).astype(WIRE_DTYPE)

    return pl.pallas_call(
        body, name=name, out_shape=jax.ShapeDtypeStruct((4, r, l), WIRE_DTYPE),
        grid_spec=pltpu.PrefetchScalarGridSpec(
            num_scalar_prefetch=1, grid=(4, r // tr),
            in_specs=[pl.BlockSpec((None, tr, l), lambda j, i, cr: (2 * j + cr[0], i, 0)),
                      pl.BlockSpec((None, tr, l), lambda j, i, cr: (j, i, 0))],
            out_specs=pl.BlockSpec((None, tr, l), lambda j, i, cr: (j, i, 0))),
        compiler_params=_cparams(dimension_semantics=("arbitrary", "arbitrary")),
    )(core, buf, recv)


def _sum_parts(p_ref):
    return ((p_ref[0].astype(F32) + p_ref[1].astype(F32)) + (p_ref[2].astype(F32) + p_ref[3].astype(F32)))


def _sum4(parts, name):
    _, r, l = parts.shape
    tr = _row_tile(r, l)

    def body(p_ref, o_ref):
        o_ref[...] = _sum_parts(p_ref)

    return pl.pallas_call(
        body, name=name, out_shape=jax.ShapeDtypeStruct((r, l), F32), grid=(r // tr,),
        in_specs=[pl.BlockSpec((4, tr, l), lambda i: (0, i, 0))],
        out_specs=pl.BlockSpec((tr, l), lambda i: (i, 0)),
        compiler_params=_cparams(dimension_semantics=("arbitrary",)),
    )(parts)


def _adamw_update(w, gg, m, v):
    nm = ADAM_B1 * m + (1.0 - ADAM_B1) * gg
    nv = ADAM_B2 * v + (1.0 - ADAM_B2) * (gg * gg)
    m_hat = nm / (1.0 - ADAM_B1 ** ADAM_STEP)
    v_hat = nv / (1.0 - ADAM_B2 ** ADAM_STEP)
    return -ADAM_LR * (m_hat / (jnp.sqrt(v_hat) + ADAM_EPS) + ADAM_WD * w), nm, nv


def _adamw(w, g, m, v, name):
    r, l = w.shape
    tr = _row_tile(r, l)

    def body(w_ref, g_ref, m_ref, v_ref, d_ref, nm_ref, nv_ref):
        d_ref[...], nm_ref[...], nv_ref[...] = _adamw_update(w_ref[...], g_ref[...], m_ref[...], v_ref[...])

    spec = pl.BlockSpec((tr, l), lambda i: (i, 0))
    return pl.pallas_call(
        body, name=name, out_shape=[jax.ShapeDtypeStruct((r, l), F32)] * 3, grid=(r // tr,),
        in_specs=[spec] * 4, out_specs=[spec] * 3,
        compiler_params=_cparams(dimension_semantics=("arbitrary",)),
    )(w, g, m, v)


def _adamw_parts(w, parts, m, v, name):
    nl, r, l = w.shape
    tr = _row_tile(r, l)

    def body(*refs):
        w_ref, p_refs, (m_ref, v_ref, g_ref, d_ref, nm_ref, nv_ref) = refs[0], refs[1:1 + nl], refs[1 + nl:]
        layer = pl.program_id(0)
        gg = _sum_parts(p_refs[0])
        for q in range(1, nl):
            gg = jnp.where(layer == q, _sum_parts(p_refs[q]), gg)
        g_ref[...] = gg
        d_ref[...], nm_ref[...], nv_ref[...] = _adamw_update(w_ref[...], gg, m_ref[...], v_ref[...])

    spec = pl.BlockSpec((None, tr, l), lambda q, i: (q, i, 0))
    pspecs = [pl.BlockSpec((4, tr, l), lambda q, i, k=k: (0, jnp.where(q == k, i, 0), 0)) for k in range(nl)]
    return pl.pallas_call(
        body, name=name, out_shape=[jax.ShapeDtypeStruct((nl, r, l), F32)] * 4, grid=(nl, r // tr),
        in_specs=[spec] + pspecs + [spec, spec], out_specs=[spec] * 4,
        compiler_params=_cparams(dimension_semantics=("arbitrary", "arbitrary")),
    )(w, *parts, m, v)


def _to_rows(pieces, row_multiple):
    flat = jnp.concatenate([p.reshape(-1) for p in pieces])
    rows = -(-flat.shape[0] // LANES)
    rows = -(-rows // row_multiple) * row_multiple
    flat = jnp.pad(flat, (0, rows * LANES - flat.shape[0]))
    return flat.reshape(rows, LANES)


def _split_rows(rows, shapes):
    flat = rows.reshape(-1)
    out, off = [], 0
    for s in shapes:
        n = int(np.prod(s))
        out.append(flat[off:off + n].reshape(s))
        off += n
    return out


def _mod_fwd(cond, w_mod, b_my, name):
    nl, _, ncol = w_mod.shape

    def body(a_ref, w_ref, b_ref, o_ref):
        a = a_ref[...]
        s = a * _sigmoid(a)
        for i in range(nl):
            o_ref[i] = _dot(s, w_ref[i]) + b_ref[i]

    return pl.pallas_call(
        body, name=name, out_shape=jax.ShapeDtypeStruct((nl, 16, ncol), F32),
        compiler_params=_cparams(),
    )(cond, w_mod, b_my)


def _mod_bwd(cond, dm_all, dm_my, w_mod, name):
    nl, _, ncol = w_mod.shape

    def body(a_ref, dma_ref, dmm_ref, w_ref, gw_ref, gb_ref, gc_ref):
        a = a_ref[...]
        sg = _sigmoid(a)
        s = a * sg
        for i in range(nl):
            gw_ref[i] = _dot_tn(s, dmm_ref[i])
            gb_ref[i] = jnp.sum(dma_ref[i], axis=0, keepdims=True)
        back = _dot_nt(dmm_ref[0], w_ref[0])
        dsilu = sg * (1.0 + a * (1.0 - sg))
        gc_ref[...] = jnp.sum(back[8:16] * dsilu[8:16], axis=0, keepdims=True)

    return pl.pallas_call(
        body, name=name,
        out_shape=[jax.ShapeDtypeStruct((nl, D_MODEL, ncol), F32), jax.ShapeDtypeStruct((nl, 1, 3 * D_MODEL), F32),
                   jax.ShapeDtypeStruct((1, D_MODEL), F32)],
        compiler_params=_cparams(),
    )(cond, dm_all, dm_my, w_mod)


def _in_proj(xt, sc, sh, wg, name, sides=()):
    t = xt.shape[0]
    tm = min(TM_MM, t)

    def body(x_ref, sc_ref, sh_ref, w_ref, u_ref, g_ref):
        h = (x_ref[...] * (1.0 + sc_ref[...]) + sh_ref[...]).astype(MXU_DTYPE)
        for k in range(N_WBLK):
            o = jnp.dot(h, w_ref[k], preferred_element_type=F32)
            if k < N_WBLK // 2:
                u_ref[:, k * WBLK:(k + 1) * WBLK] = o
            else:
                kk = k - N_WBLK // 2
                g_ref[:, kk * WBLK:(kk + 1) * WBLK] = o

    row = pl.BlockSpec((1, D_MODEL), lambda i: (0, 0))
    return _call_with_sides(
        body, sides, name=name, out_shape=[jax.ShapeDtypeStruct((t, D_INNER), F32)] * 2, grid=(t // tm,),
        in_specs=[pl.BlockSpec((tm, D_MODEL), lambda i: (i, 0)), row, row,
                  pl.BlockSpec((N_WBLK, D_MODEL, WBLK), lambda i: (0, 0, 0), pipeline_mode=pl.Buffered(1))],
        out_specs=[pl.BlockSpec((tm, D_INNER), lambda i: (i, 0))] * 2, scratch_shapes=[],
        compiler_params=_cparams(dimension_semantics=("arbitrary",)), args=[xt, sc, sh, wg])


def _halo_maps(nt, tm, n_rows8, pos):
    per = tm // SUBLANES
    prev = lambda cb, i: (jnp.maximum(pos(i) * per - 1, 0), cb)
    nxt = lambda cb, i: (jnp.minimum((pos(i) + 1) * per, n_rows8 - 1), cb)
    return prev, nxt


def _conv_taps(u, prev8, next8, is_first, is_last):
    pz = jnp.where(is_first, 0.0, 1.0)
    nz = jnp.where(is_last, 0.0, 1.0)
    return _shifted(u, prev8 * pz, next8 * nz, [-2, -1, 1])


def _lru_gates(uv, wa_ref, wx_ref, ba, bx, cl, g):
    sl = slice(g * LANES, (g + 1) * LANES)
    uvg = uv[:, sl]
    r = _sigmoid(_dot(uvg, wa_ref[g]) + ba[:, sl])
    ii = _sigmoid(_dot(uvg, wx_ref[g]) + bx[:, sl])
    la = cl[:, sl] * r
    a = jnp.exp(la)
    q = jnp.tanh(-la) * (1.0 + a * a)
    rs = lax.rsqrt(jnp.maximum(q, SQRT_FLOOR))
    return uvg, r, ii, a, q * rs, rs


def _scan_rows(seg):
    return -(-(SCAN_ROW_T * (seg - 1) + SCAN_ROW_J * (N_SEG - 1) + 1) // SUBLANES) * SUBLANES


def _seg_chunk(j, c):
    return pl.ds(SCAN_ROW_T * SUBLANES * c + SCAN_ROW_J * j, SUBLANES, stride=SCAN_ROW_T)


def _seg_scatter(ref, g, seg, value):
    for j in range(N_SEG):
        for c in range(seg // SUBLANES):
            r0 = j * seg + SUBLANES * c
            ref[g, _seg_chunk(j, c), :] = value[r0:r0 + SUBLANES]


def _scan_tile(a_s, b_s, carry_ref, write_out, seg, reverse):
    n_g = a_s.shape[0]
    unroll = SCAN_UNROLL if seg % SCAN_UNROLL == 0 else 1

    n_trips = seg // unroll

    def steps(k, state):
        hs, cs = list(state[0]), list(state[1])
        base = ((n_trips - 1 - k) if reverse else k) * unroll
        for q in (range(unroll - 1, -1, -1) if reverse else range(unroll)):
            t = base + q
            rows = pl.ds(t * SCAN_ROW_T, N_SEG, stride=SCAN_ROW_J)
            for g in range(n_g):
                a = a_s[g, rows, :]
                b = b_s[g, rows, :]
                hs[g] = a * hs[g] + b
                cs[g] = a * cs[g]
                b_s[g, rows, :] = hs[g]
                a_s[g, rows, :] = cs[g]
        return tuple(hs), tuple(cs)

    zeros = tuple(jnp.zeros((N_SEG, LANES), F32) for _ in range(n_g))
    ones = tuple(jnp.ones((N_SEG, LANES), F32) for _ in range(n_g))
    h_fin, a_fin = lax.fori_loop(0, seg // unroll, steps, (zeros, ones))

    order = list(range(N_SEG - 1, -1, -1)) if reverse else list(range(N_SEG))
    for g in range(n_g):
        carry = carry_ref[:, g * LANES:(g + 1) * LANES]
        for j in order:
            for c in range(seg // SUBLANES):
                rows = _seg_chunk(j, c)
                write_out(j, c, g, b_s[g, rows, :] + a_s[g, rows, :] * carry)
            carry = a_fin[g][j:j + 1] * carry + h_fin[g][j:j + 1]
        carry_ref[:, g * LANES:(g + 1) * LANES] = carry


def _lru_specs(s, tm, cb, direction_pos, nt):
    n_rows8 = s // SUBLANES
    prev, nxt = _halo_maps(nt, tm, n_rows8, direction_pos)
    tile = pl.BlockSpec((tm, cb), lambda c, i: (direction_pos(i), c))
    return tile, pl.BlockSpec((SUBLANES, cb), prev), pl.BlockSpec((SUBLANES, cb), nxt)


def _lru_param_specs(cb, d):
    n_g = cb // LANES
    vec = pl.BlockSpec((1, cb), lambda c, i: (0, c))
    dvec = pl.BlockSpec((None, 1, cb), lambda c, i: (d, 0, c))
    wmat = pl.BlockSpec((None, n_g, LRU_BLOCK, LRU_BLOCK), lambda c, i: (d, c, 0, 0))
    return vec, dvec, wmat


def _lru_fwd(src, h0, p, d, name, conv, sides=()):
    s = src.shape[0]
    tm = min(TM_LRU, s)
    cb = CB_LRU
    n_g = cb // LANES
    nt = s // tm
    seg = tm // N_SEG
    pos = (lambda i: i) if d == 0 else (lambda i: nt - 1 - i)

    def body(*refs):
        refs = list(refs)
        u_ref = refs.pop(0)
        if conv:
            up_ref, un_ref, cw_ref, cbias_ref = [refs.pop(0) for _ in range(4)]
        wa_ref, wx_ref, ba_ref, bx_ref, lam_ref, h0_ref, h_ref, hc_ref = [refs.pop(0) for _ in range(8)]
        uv_ref = refs.pop(0) if conv else None
        a_s, b_s = refs
        i = pl.program_id(1)
        tp = pos(i)

        @pl.when(i == 0)
        def _():
            hc_ref[...] = h0_ref[...]

        if conv:
            u_t = u_ref[...]
            um2, um1, up1 = _conv_taps(u_t, up_ref[...], un_ref[...], tp == 0, tp == nt - 1)
            cw = cw_ref[...]
            uv_ref[...] = um2 * cw[0:1] + um1 * cw[1:2] + u_t * cw[2:3] + up1 * cw[3:4] + cbias_ref[...]
        src_ref = uv_ref if conv else u_ref
        cl = LRU_C * _log_sigmoid(lam_ref[...])
        ba, bx = ba_ref[...], bx_ref[...]
        for g in range(n_g):
            uvg, r, ii, a, sq, _ = _lru_gates(src_ref, wa_ref, wx_ref, ba, bx, cl, g)
            b = sq * (ii * uvg)
            _seg_scatter(a_s, g, seg, a)
            _seg_scatter(b_s, g, seg, b)

        def write_out(j, c, g, h):
            h_ref[pl.ds(j * seg + SUBLANES * c, SUBLANES), pl.ds(g * LANES, LANES)] = h

        _scan_tile(a_s, b_s, hc_ref, write_out, seg, reverse=(d == 1))

    tile, prev, nxt = _lru_specs(s, tm, cb, pos, nt)
    vec, dvec, wmat = _lru_param_specs(cb, d)
    wide = jax.ShapeDtypeStruct((s, D_INNER), F32)
    conv_specs = [prev, nxt, pl.BlockSpec((4, cb), lambda c, i: (0, c)), vec] if conv else []
    conv_args = [src, src, p["conv_w"], p["conv_b"]] if conv else []
    return _call_with_sides(
        body, sides, name=name,
        out_shape=[wide, jax.ShapeDtypeStruct((1, D_INNER), F32)] + ([wide] if conv else []),
        grid=(D_INNER // cb, nt),
        in_specs=[tile] + conv_specs + [wmat, wmat, dvec, dvec, dvec, vec],
        out_specs=[tile, vec] + ([tile] if conv else []),
        scratch_shapes=[pltpu.VMEM((n_g, _scan_rows(seg), LANES), F32)] * 2,
        compiler_params=_cparams(dimension_semantics=("arbitrary", "arbitrary")),
        args=[src, *conv_args, p["wa"], p["wx"], p["ba"], p["bx"], p["lam"], h0])


def _lru_bwd(uv, dh, h, h0, lam_in, p, d, name, sides=()):
    s = uv.shape[0]
    tm = min(TM_LRU, s)
    cb = CB_LRU
    n_g = cb // LANES
    nt = s // tm
    seg = tm // N_SEG
    pos = (lambda i: nt - 1 - i) if d == 0 else (lambda i: i)

    def body(uv_ref, dh_ref, h_ref, hh_ref, wa_ref, wx_ref, ba_ref, bx_ref,
             lam_ref, h0_ref, lin_ref, duv_ref, gwa_ref, gwx_ref, gv_ref, lc_ref, a_s, b_s, lp_s,
             r_s, i_s, q_s, rq_s, a_keep):
        i = pl.program_id(1)
        tp = pos(i)

        @pl.when(i == 0)
        def _():
            lc_ref[...] = lin_ref[...]
            gwa_ref[...] = jnp.zeros_like(gwa_ref)
            gwx_ref[...] = jnp.zeros_like(gwx_ref)
            gv_ref[...] = jnp.zeros_like(gv_ref)

        uv = uv_ref[...]
        lam = lam_ref[...]
        cl = LRU_C * _log_sigmoid(lam)
        ba, bx = ba_ref[...], bx_ref[...]
        dh_t = dh_ref[...]
        carry_in = lc_ref[...]
        for g in range(n_g):
            sl = slice(g * LANES, (g + 1) * LANES)
            _, r, ii, a, sq, rs = _lru_gates(uv, wa_ref, wx_ref, ba, bx, cl, g)
            r_s[:, sl], i_s[:, sl], q_s[:, sl], rq_s[:, sl], a_keep[:, sl] = r, ii, sq, rs, a
            b = a * dh_t[:, sl]
            _seg_scatter(a_s, g, seg, a)
            _seg_scatter(b_s, g, seg, b)

        def write_out(j, c, g, v):
            lp_s[pl.ds(j * seg + SUBLANES * c, SUBLANES), pl.ds(g * LANES, LANES)] = v

        _scan_tile(a_s, b_s, lc_ref, write_out, seg, reverse=(d == 0))

        h_t = h_ref[...]
        hh = hh_ref[...]
        if d == 0:
            edge = jnp.where(tp == 0, h0_ref[...], hh[7:8])
            h_prev = _shift_down(h_t, edge)
            lam_t = dh_t + _shift_up(lp_s[...], carry_in)
        else:
            edge = jnp.where(tp == nt - 1, h0_ref[...], hh[0:1])
            h_prev = _shift_up(h_t, edge)
            lam_t = dh_t + _shift_down(lp_s[...], carry_in)

        dsig = LRU_C * _sigmoid(-lam)
        for g in range(n_g):
            sl = slice(g * LANES, (g + 1) * LANES)
            uvg, r, ii, a, sq = uv[:, sl], r_s[:, sl], i_s[:, sl], a_keep[:, sl], q_s[:, sl]
            lt = lam_t[:, sl]
            ls = lt * sq
            dla = (lt * a) * (h_prev[:, sl] - (ii * uvg) * (a * rq_s[:, sl]))
            dzr = (dla * cl[:, sl]) * r * (1.0 - r)
            dzi = (ls * uvg) * ii * (1.0 - ii)
            duv_ref[:, sl] = ls * ii + _dot_nt(dzr, wa_ref[g]) + _dot_nt(dzi, wx_ref[g])
            gwa_ref[g] += _dot_tn(uvg, dzr)
            gwx_ref[g] += _dot_tn(uvg, dzi)
            gv_ref[0:1, sl] += _rowsum(dzr)
            gv_ref[1:2, sl] += _rowsum(dzi)
            gv_ref[2:3, sl] += _rowsum(dla * r) * dsig[:, sl]

    tile, prev, nxt = _lru_specs(s, tm, cb, pos, nt)
    vec, dvec, wmat = _lru_param_specs(cb, d)
    hh_spec = prev if d == 0 else nxt
    gw_spec = pl.BlockSpec((n_g, LRU_BLOCK, LRU_BLOCK), lambda c, i: (c, 0, 0))
    n_blk = D_INNER // LRU_BLOCK
    return _call_with_sides(
        body, sides, name=name,
        out_shape=[jax.ShapeDtypeStruct((s, D_INNER), F32),
                   jax.ShapeDtypeStruct((n_blk, LRU_BLOCK, LRU_BLOCK), F32),
                   jax.ShapeDtypeStruct((n_blk, LRU_BLOCK, LRU_BLOCK), F32),
                   jax.ShapeDtypeStruct((SUBLANES, D_INNER), F32),
                   jax.ShapeDtypeStruct((1, D_INNER), F32)],
        grid=(D_INNER // cb, nt),
        in_specs=[tile, tile, tile, hh_spec, wmat, wmat, dvec, dvec, dvec, vec, vec],
        out_specs=[tile, gw_spec, gw_spec, pl.BlockSpec((SUBLANES, cb), lambda c, i: (0, c)), vec],
        scratch_shapes=[pltpu.VMEM((n_g, _scan_rows(seg), LANES), F32)] * 2 + [pltpu.VMEM((tm, cb), F32)] * 6,
        compiler_params=_cparams(dimension_semantics=("arbitrary", "arbitrary")),
        args=[uv, dh, h, h, p["wa"], p["wx"], p["ba"], p["bx"], p["lam"], h0, lam_in])


def _out0(hf, hb, g, xt, gt, wo, lg, lb, name):
    t = xt.shape[0]
    tm = min(TM_MM, t)

    def body(hf_ref, hb_ref, g_ref, x_ref, gt_ref, w_ref, lg_ref, lb_ref, x1_ref, br_ref):
        br = None
        for k in range(D_INNER // WBLK):
            sl = slice(k * WBLK, (k + 1) * WBLK)
            gg = g_ref[:, sl]
            p = (hf_ref[:, sl] + hb_ref[:, sl]) * (gg * _sigmoid(gg))
            part = _dot(p, w_ref[sl, :])
            br = part if br is None else br + part
        z = ALPHA * x_ref[...] + gt_ref[...] * br
        xhat, _ = _layer_norm_stats(z)
        x1_ref[...] = xhat * lg_ref[...] + lb_ref[...]
        br_ref[...] = br

    wide = pl.BlockSpec((tm, D_INNER), lambda i: (i, 0))
    nar = pl.BlockSpec((tm, D_MODEL), lambda i: (i, 0))
    row = pl.BlockSpec((1, D_MODEL), lambda i: (0, 0))
    return pl.pallas_call(
        body, name=name, out_shape=[jax.ShapeDtypeStruct((t, D_MODEL), F32)] * 2, grid=(t // tm,),
        in_specs=[wide, wide, wide, nar, row,
                  pl.BlockSpec((D_INNER, D_MODEL), lambda i: (0, 0), pipeline_mode=pl.Buffered(1)), row, row],
        out_specs=[nar, nar],
        compiler_params=_cparams(dimension_semantics=("arbitrary",)),
    )(hf, hb, g, xt, gt, wo, lg, lb)


def _unrolled_loop(n, fn, unroll=4):
    while n % unroll:
        unroll //= 2

    def trip(k, carry):
        for q in range(unroll):
            fn(k * unroll + q)
        return carry
    lax.fori_loop(0, n // unroll, trip, 0)


def _window(n, w):
    t = np.arange(n)
    return np.clip(t - w // 2, 0, n), np.clip(t + w // 2, 0, n)


def _pool_tables(n_rows, transpose):
    boxes, inv_c, inv_r = [], [], []
    for w in POOL_WINDOWS:
        lo, hi = _window(GRID_W, w)
        m = np.zeros((GRID_W, GRID_W), np.float32)
        for r in range(GRID_W):
            m[r, lo[r]:hi[r]] = 1.0
        m = np.kron(np.eye(POOL_TOK // GRID_W, dtype=np.float32), m)
        boxes.append(m.T if transpose else m)
        inv_c.append(np.broadcast_to((1.0 / (hi - lo).astype(np.float32))[:, None], (GRID_W, LANES)))
        lo_r, hi_r = _window(n_rows, w)
        inv_r.append(1.0 / (hi_r - lo_r).astype(np.float32))
    return (jnp.asarray(np.stack(boxes), MXU_DTYPE), jnp.asarray(np.stack(inv_c), F32),
            jnp.asarray(np.stack(inv_r), F32))


def _pool_mix(xin, transpose, out_dtype, name):
    s = xin.shape[0]
    n_rows = s // GRID_W
    pad_t = SUBLANES * GRID_W
    rows_per_blk = POOL_TOK // GRID_W
    n_slab = D_INNER // LANES
    slabs_per_group = POOL_GROUP // LANES
    n_win = len(POOL_WINDOWS)
    boxes, inv_c, inv_r = _pool_tables(n_rows, transpose)

    def body(invr_ref, box_ref, invc_ref, x_ref, o_ref, pad_s):
        k = pl.program_id(0) // slabs_per_group
        pad_s[pl.ds(0, pad_t), :] = jnp.zeros((pad_t, LANES), F32)
        pad_s[pl.ds(pad_t + s, pad_t), :] = jnp.zeros((pad_t, LANES), F32)

        for kk, w in enumerate(POOL_WINDOWS):
            half = w // 2
            offsets = list(range(-(half - 1), half + 1)) if transpose else list(range(-half, half))

            @pl.when(k == kk)
            def _():
                inv_col = invc_ref[kk]

                def col_box(b):
                    st = pl.multiple_of(b * POOL_TOK, POOL_TOK)
                    xb = x_ref[pl.ds(st, POOL_TOK), :]
                    if transpose:
                        xb = xb * jnp.concatenate(
                            [inv_col * invr_ref[kk, b * rows_per_blk + q] for q in range(rows_per_blk)], axis=0)
                    hi = xb.astype(MXU_DTYPE)
                    lo = (xb - hi.astype(F32)).astype(MXU_DTYPE)
                    both = jnp.dot(box_ref[kk], jnp.concatenate([hi, lo], axis=1), preferred_element_type=F32)
                    pad_s[pl.ds(pad_t + st, POOL_TOK), :] = both[:, :LANES] + both[:, LANES:]
                _unrolled_loop(s // POOL_TOK, col_box)

                def row_box(r):
                    st = pl.multiple_of(r * GRID_W, GRID_W)
                    acc = pad_s[pl.ds(pad_t + st + offsets[0] * GRID_W, GRID_W), :]
                    for o in offsets[1:]:
                        acc = acc + pad_s[pl.ds(pad_t + st + o * GRID_W, GRID_W), :]
                    if not transpose:
                        acc = acc * (inv_col * invr_ref[kk, r])
                    o_ref[pl.ds(st, GRID_W), :] = (acc - x_ref[pl.ds(st, GRID_W), :]).astype(out_dtype)
                _unrolled_loop(n_rows, row_box)

    slab = pl.BlockSpec((s, LANES), lambda i: (0, i))
    return pl.pallas_call(
        body, name=name, out_shape=jax.ShapeDtypeStruct((s, D_INNER), out_dtype), grid=(n_slab,),
        in_specs=[pl.BlockSpec(memory_space=pltpu.SMEM),
                  pl.BlockSpec((n_win, POOL_TOK, POOL_TOK), lambda i: (0, 0, 0)),
                  pl.BlockSpec((n_win, GRID_W, LANES), lambda i: (0, 0, 0)), slab],
        out_specs=slab,
        scratch_shapes=[pltpu.VMEM((s + 2 * pad_t, LANES), F32)],
        compiler_params=_cparams(dimension_semantics=("arbitrary",)),
    )(inv_r, boxes, inv_c, xin)


def _out1(dmix, pw, ps, g, x1, gt, wo, lg, lb, tgt, name):
    t = x1.shape[0]
    tm = min(TM_MM, t)
    n_grp = len(POOL_WINDOWS)

    def body(d_ref, pw_ref, ps_ref, g_ref, x1_ref, gt_ref, w_ref, lg_ref, lb_ref, tgt_ref, dz_ref, st_ref):
        @pl.when(pl.program_id(0) == 0)
        def _():
            st_ref[...] = jnp.zeros_like(st_ref)

        br = jnp.zeros((tm, D_MODEL), F32)
        for k in range(n_grp):
            sl = slice(k * POOL_GROUP, (k + 1) * POOL_GROUP)
            y = jnp.dot(d_ref[:, sl], pw_ref[k], preferred_element_type=F32) * ps_ref[:, sl]
            gg = g_ref[:, sl]
            br = br + _dot(y * (gg * _sigmoid(gg)), w_ref[sl, :])
        z = ALPHA * x1_ref[...] + gt_ref[...] * br
        xhat, rstd = _layer_norm_stats(z)
        lg_v = lg_ref[...]
        err = xhat * lg_v + lb_ref[...] - tgt_ref[...]
        dy = err * (1.0 / D_MODEL)
        dz = _layer_norm_bwd(dy, xhat, rstd, lg_v)
        dz_ref[...] = dz
        st_ref[0:1, :] += _rowsum(dy * xhat)
        st_ref[1:2, :] += _rowsum(dy)
        st_ref[2:3, :] += _rowsum(dz * br)
        st_ref[3:4, :] += _rowsum(err * err)

    wide = pl.BlockSpec((tm, D_INNER), lambda i: (i, 0))
    nar = pl.BlockSpec((tm, D_MODEL), lambda i: (i, 0))
    row = pl.BlockSpec((1, D_MODEL), lambda i: (0, 0))
    return pl.pallas_call(
        body, name=name,
        out_shape=[jax.ShapeDtypeStruct((t, D_MODEL), F32), jax.ShapeDtypeStruct((SUBLANES, D_MODEL), F32)],
        grid=(t // tm,),
        in_specs=[wide, pl.BlockSpec((n_grp, POOL_GROUP, POOL_GROUP), lambda i: (0, 0, 0)),
                  pl.BlockSpec((1, D_INNER), lambda i: (0, 0)), wide, nar, row,
                  pl.BlockSpec((D_INNER, D_MODEL), lambda i: (0, 0), pipeline_mode=pl.Buffered(1)), row, row, nar],
        out_specs=[nar, pl.BlockSpec((SUBLANES, D_MODEL), lambda i: (0, 0))],
        compiler_params=_cparams(dimension_semantics=("arbitrary",)),
    )(dmix, pw, ps, g, x1, gt, wo, lg, lb, tgt)


def _flush(acc, out_hbm, sem):
    cp = pltpu.make_async_copy(acc, out_hbm, sem)
    cp.start()
    cp.wait()


def _bout1(dz, dmix, g, pw, ps, gt, wo, name):
    t = dz.shape[0]
    tm = min(TM_MM, t)
    nt = t // tm
    n_grp = len(POOL_WINDOWS)

    def body(dz_ref, d_ref, g_ref, pw_ref, ps_ref, gt_ref, w_ref, dd_ref, dg_ref, gwo_hbm, gpw_hbm, gps_ref,
             gwo_acc, gpw_acc, sems):
        i = pl.program_id(0)

        @pl.when(i == 0)
        def _():
            gwo_acc[...] = jnp.zeros_like(gwo_acc)
            gpw_acc[...] = jnp.zeros_like(gpw_acc)
            gps_ref[...] = jnp.zeros_like(gps_ref)

        db = (gt_ref[...] * dz_ref[...]).astype(MXU_DTYPE)
        for k in range(n_grp):
            sl = slice(k * POOL_GROUP, (k + 1) * POOL_GROUP)
            dk = d_ref[:, sl]
            po = jnp.dot(dk, pw_ref[k], preferred_element_type=F32)
            psk = ps_ref[:, sl]
            y = po * psk
            gg = g_ref[:, sl]
            sg = _sigmoid(gg)
            silu = gg * sg
            gwo_acc[sl, :] += _dot_tn(y * silu, db)
            dp = _dot_nt(db, w_ref[sl, :])
            dy = dp * silu
            dg_ref[:, sl] = (dp * y * (sg * (1.0 + gg * (1.0 - sg)))).astype(MXU_DTYPE)
            gps_ref[0:1, sl] += _rowsum(dy * po)
            dpo = (dy * psk).astype(MXU_DTYPE)
            gpw_acc[k] += _dot_tn(dk, dpo)
            dd_ref[:, sl] = _dot_nt(dpo, pw_ref[k])

        @pl.when(i == nt - 1)
        def _():
            _flush(gwo_acc, gwo_hbm, sems.at[0])
            _flush(gpw_acc, gpw_hbm, sems.at[1])

    wide = pl.BlockSpec((tm, D_INNER), lambda i: (i, 0))
    nar = pl.BlockSpec((tm, D_MODEL), lambda i: (i, 0))
    return pl.pallas_call(
        body, name=name,
        out_shape=[jax.ShapeDtypeStruct((t, D_INNER), F32), jax.ShapeDtypeStruct((t, D_INNER), MXU_DTYPE),
                   jax.ShapeDtypeStruct((D_INNER, D_MODEL), F32),
                   jax.ShapeDtypeStruct((n_grp, POOL_GROUP, POOL_GROUP), F32),
                   jax.ShapeDtypeStruct((SUBLANES, D_INNER), F32)],
        grid=(nt,),
        in_specs=[nar, wide, wide,
                  pl.BlockSpec((n_grp, POOL_GROUP, POOL_GROUP), lambda i: (0, 0, 0), pipeline_mode=pl.Buffered(1)),
                  pl.BlockSpec((1, D_INNER), lambda i: (0, 0)), pl.BlockSpec((1, D_MODEL), lambda i: (0, 0)),
                  pl.BlockSpec((D_INNER, D_MODEL), lambda i: (0, 0), pipeline_mode=pl.Buffered(1))],
        out_specs=[wide, wide, ANY, ANY, pl.BlockSpec((SUBLANES, D_INNER), lambda i: (0, 0))],
        scratch_shapes=[pltpu.VMEM((D_INNER, D_MODEL), F32), pltpu.VMEM((n_grp, POOL_GROUP, POOL_GROUP), F32),
                        pltpu.SemaphoreType.DMA((2,))],
        compiler_params=_cparams(dimension_semantics=("arbitrary",)),
    )(dz, dmix, g, pw, ps, gt, wo)


def _bout0(dx1, xt, br0, lg, hf, hb, g, gt, wo, name, sides=()):
    t = dx1.shape[0]
    tm = min(TM_BWD, t)
    nt = t // tm

    def body(dx_ref, x_ref, br_ref, lg_ref, hf_ref, hb_ref, g_ref, gt_ref, w_ref,
             dz_ref, dy_ref, dg_ref, gwo_hbm, st_ref, gwo_acc, sem):
        i = pl.program_id(0)

        @pl.when(i == 0)
        def _():
            gwo_acc[...] = jnp.zeros_like(gwo_acc)
            st_ref[...] = jnp.zeros_like(st_ref)

        dx = dx_ref[...]
        br = br_ref[...]
        gate = gt_ref[...]
        xhat, rstd = _layer_norm_stats(ALPHA * x_ref[...] + gate * br)
        dz = _layer_norm_bwd(dx, xhat, rstd, lg_ref[...])
        dz_ref[...] = dz
        st_ref[0:1, :] += _rowsum(dx * xhat)
        st_ref[1:2, :] += _rowsum(dx)
        st_ref[2:3, :] += _rowsum(dz * br)
        db = (gate * dz).astype(MXU_DTYPE)
        for k in range(D_INNER // WBLK):
            sl = slice(k * WBLK, (k + 1) * WBLK)
            y = hf_ref[:, sl] + hb_ref[:, sl]
            gg = g_ref[:, sl]
            sg = _sigmoid(gg)
            silu = gg * sg
            gwo_acc[sl, :] += _dot_tn(y * silu, db)
            dp = _dot_nt(db, w_ref[sl, :])
            dy_ref[:, sl] = dp * silu
            dg_ref[:, sl] = (dp * y * (sg * (1.0 + gg * (1.0 - sg)))).astype(MXU_DTYPE)

        @pl.when(i == nt - 1)
        def _():
            _flush(gwo_acc, gwo_hbm, sem)

    wide = pl.BlockSpec((tm, D_INNER), lambda i: (i, 0))
    nar = pl.BlockSpec((tm, D_MODEL), lambda i: (i, 0))
    row = pl.BlockSpec((1, D_MODEL), lambda i: (0, 0))
    return _call_with_sides(
        body, sides, name=name,
        out_shape=[jax.ShapeDtypeStruct((t, D_MODEL), F32), jax.ShapeDtypeStruct((t, D_INNER), F32),
                   jax.ShapeDtypeStruct((t, D_INNER), MXU_DTYPE), jax.ShapeDtypeStruct((D_INNER, D_MODEL), F32),
                   jax.ShapeDtypeStruct((SUBLANES, D_MODEL), F32)],
        grid=(nt,),
        in_specs=[nar, nar, nar, row, wide, wide, wide, row,
                  pl.BlockSpec((D_INNER, D_MODEL), lambda i: (0, 0), pipeline_mode=pl.Buffered(1))],
        out_specs=[nar, wide, wide, ANY, pl.BlockSpec((SUBLANES, D_MODEL), lambda i: (0, 0))],
        scratch_shapes=[pltpu.VMEM((D_INNER, D_MODEL), F32), pltpu.SemaphoreType.DMA(())],
        compiler_params=_cparams(dimension_semantics=("arbitrary",)),
        args=[dx1, xt, br0, lg, hf, hb, g, gt, wo])


def _conv_bwd(duvf, duvb, u, conv_w, name):
    s = u.shape[0]
    tm = min(TM_LRU, s)
    cb = CB_LRU
    nt = s // tm

    def body(df_ref, dfp_ref, dfn_ref, db_ref, dbp_ref, dbn_ref, u_ref, cw_ref, du_ref, cst_ref):
        i = pl.program_id(1)

        @pl.when(i == 0)
        def _():
            cst_ref[...] = jnp.zeros_like(cst_ref)

        first, last = i == 0, i == nt - 1
        pz = jnp.where(first, 0.0, 1.0)
        nz = jnp.where(last, 0.0, 1.0)
        dout = df_ref[...] + db_ref[...]
        dm1, dp1, dp2 = _shifted(dout, (dfp_ref[...] + dbp_ref[...]) * pz, (dfn_ref[...] + dbn_ref[...]) * nz,
                                 [-1, 1, 2])
        cw = cw_ref[...]
        du_ref[...] = (dp2 * cw[0:1] + dp1 * cw[1:2] + dout * cw[2:3] + dm1 * cw[3:4]).astype(MXU_DTYPE)
        u_t = u_ref[...]
        cst_ref[0:1, :] += _rowsum(dp2 * u_t)
        cst_ref[1:2, :] += _rowsum(dp1 * u_t)
        cst_ref[2:3, :] += _rowsum(dout * u_t)
        cst_ref[3:4, :] += _rowsum(dm1 * u_t)
        cst_ref[4:5, :] += _rowsum(dout)

    tile, prev, nxt = _lru_specs(s, tm, cb, lambda i: i, nt)
    return pl.pallas_call(
        body, name=name,
        out_shape=[jax.ShapeDtypeStruct((s, D_INNER), MXU_DTYPE), jax.ShapeDtypeStruct((SUBLANES, D_INNER), F32)],
        grid=(D_INNER // cb, nt),
        in_specs=[tile, prev, nxt] * 2 + [tile, pl.BlockSpec((4, cb), lambda c, i: (0, c))],
        out_specs=[tile, pl.BlockSpec((SUBLANES, cb), lambda c, i: (0, c))],
        compiler_params=_cparams(dimension_semantics=("arbitrary", "arbitrary")),
    )(duvf, duvf, duvf, duvb, duvb, duvb, u, conv_w)


def _bin(du, dg, xin, dzin, sc, sh, wg, name, gw_init=None):
    t = xin.shape[0]
    tm = min(TM_MM, t)
    nt = t // tm
    has_g, has_dx, has_init = dg is not None, dzin is not None, gw_init is not None
    half = N_WBLK // 2
    n_blk = N_WBLK if has_g else half

    def body(*refs):
        refs = list(refs)
        du_ref = refs.pop(0)
        dg_ref = refs.pop(0) if has_g else None
        x_ref = refs.pop(0)
        dz_ref = refs.pop(0) if has_dx else None
        sc_ref, sh_ref, w_ref = refs.pop(0), refs.pop(0), refs.pop(0)
        init_hbm = refs.pop(0) if has_init else None
        dx_ref = refs.pop(0) if has_dx else None
        gw_hbm, st_ref, gw_acc, sem = refs
        i = pl.program_id(0)

        @pl.when(i == 0)
        def _():
            st_ref[...] = jnp.zeros_like(st_ref)
            first_zero = 0
            if has_init:
                _flush(init_hbm, gw_acc.at[pl.ds(0, half)], sem)
                first_zero = half
            for k in range(first_zero, n_blk):
                gw_acc[k] = jnp.zeros((D_MODEL, WBLK), F32)

        xv = x_ref[...]
        scale = 1.0 + sc_ref[...]
        h = (xv * scale + sh_ref[...]).astype(MXU_DTYPE)
        dh = None
        for k in range(n_blk):
            src = du_ref if k < half else dg_ref
            kk = k % half
            dk = src[:, kk * WBLK:(kk + 1) * WBLK]
            gw_acc[k] += _dot_tn(h, dk)
            contrib = _dot_nt(dk, w_ref[k])
            dh = contrib if dh is None else dh + contrib
        st_ref[0:1, :] += _rowsum(dh * xv)
        st_ref[1:2, :] += _rowsum(dh)
        if has_dx:
            dx_ref[...] = ALPHA * dz_ref[...] + dh * scale

        @pl.when(i == nt - 1)
        def _():
            _flush(gw_acc, gw_hbm, sem)

    wide = pl.BlockSpec((tm, D_INNER), lambda i: (i, 0))
    nar = pl.BlockSpec((tm, D_MODEL), lambda i: (i, 0))
    row = pl.BlockSpec((1, D_MODEL), lambda i: (0, 0))
    wspec = pl.BlockSpec((n_blk, D_MODEL, WBLK), lambda i: (0, 0, 0), pipeline_mode=pl.Buffered(1))
    in_specs = ([wide] + ([wide] if has_g else []) + [nar] + ([nar] if has_dx else []) + [row, row, wspec]
                + ([ANY] if has_init else []))
    args = ([du] + ([dg] if has_g else []) + [xin] + ([dzin] if has_dx else []) + [sc, sh, wg]
            + ([gw_init] if has_init else []))
    out_shape = ([jax.ShapeDtypeStruct((t, D_MODEL), F32)] if has_dx else []) + [
        jax.ShapeDtypeStruct((n_blk, D_MODEL, WBLK), F32), jax.ShapeDtypeStruct((SUBLANES, D_MODEL), F32)]
    out_specs = ([nar] if has_dx else []) + [ANY, pl.BlockSpec((SUBLANES, D_MODEL), lambda i: (0, 0))]
    return pl.pallas_call(
        body, name=name, out_shape=out_shape, grid=(nt,), in_specs=in_specs, out_specs=out_specs,
        scratch_shapes=[pltpu.VMEM((n_blk, D_MODEL, WBLK), F32), pltpu.SemaphoreType.DMA(())],
        compiler_params=_cparams(dimension_semantics=("arbitrary",)),
    )(*args)


def _blocks_by_device(a, axis):
    shape = a.shape
    a = a.reshape(shape[:axis] + (N_DEV, shape[axis] // N_DEV) + shape[axis + 1:])
    return jnp.moveaxis(a, axis, 0)


def kernel(x, c, ctx, c_ctx, w_mod, b_mod, w_in, w_out, ln_g, ln_b, conv_w, conv_b, lru_wa, lru_ba, lru_wx, lru_bx, lru_lam, pool_w, pool_scale, loss_target, m_c_ctx, m_w_mod, m_b_mod, m_w_in, m_w_out, m_ln_g, m_ln_b, m_conv_w, m_conv_b, m_lru_wa, m_lru_ba, m_lru_wx, m_lru_bx, m_lru_lam, m_pool_w, m_pool_scale, v_c_ctx, v_w_mod, v_b_mod, v_w_in, v_w_out, v_ln_g, v_ln_b, v_conv_w, v_conv_b, v_lru_wa, v_lru_ba, v_lru_wx, v_lru_bx, v_lru_lam, v_pool_w, v_pool_scale):
    xi, yi, ci = _my_pos()
    dev = 4 * xi + 2 * yi + ci
    xt, ctxt, tgt = x[0], ctx[0], loss_target[0]
    n_mod = w_mod.shape[2]

    small_shapes = [(D_MODEL,), conv_w.shape[1:], lru_ba.shape[1:], lru_bx.shape[1:], lru_lam.shape[1:],
                    pool_scale.shape[1:]]
    small = _to_rows([c[0], conv_w[0], lru_ba[0], lru_bx[0], lru_lam[0], pool_scale[0]], SUBLANES)
    small_all, = _all_gather([small], "gather_small")
    pieces = [_split_rows(small_all[k], small_shapes) for k in range(N_DEV)]
    c_all = jnp.stack([p[0] for p in pieces])
    conv_w_f = jnp.concatenate([p[1] for p in pieces], axis=-1)
    lru_ba_f = jnp.concatenate([p[2] for p in pieces], axis=-1)[:, None, :]
    lru_bx_f = jnp.concatenate([p[3] for p in pieces], axis=-1)[:, None, :]
    lru_lam_f = jnp.concatenate([p[4] for p in pieces], axis=-1)[:, None, :]
    pool_scale_f = jnp.concatenate([p[5] for p in pieces], axis=-1)[None, :]

    cond = jnp.concatenate([c_all, jnp.broadcast_to(c_ctx[None, :], (N_DEV, D_MODEL))], axis=0)
    b_my = lax.dynamic_slice(b_mod, (0, dev * n_mod), (2, n_mod))[:, None, :]
    mod_part = _mod_fwd(cond, w_mod, b_my, "mod_fwd")
    mod_all, = _all_gather([mod_part], "gather_mod")
    mod = jnp.transpose(mod_all, (1, 2, 0, 3)).reshape(2, 16, 3 * D_MODEL)
    mod_me = lax.dynamic_slice(mod, (0, dev, 0), (2, 1, 3 * D_MODEL))
    sh = [mod_me[i, :, 0:D_MODEL] for i in range(2)]
    sc = [mod_me[i, :, D_MODEL:2 * D_MODEL] for i in range(2)]
    gt = [mod_me[i, :, 2 * D_MODEL:] for i in range(2)]
    shc, scc = mod[0, 8:9, 0:D_MODEL], mod[0, 8:9, D_MODEL:2 * D_MODEL]

    wi0, = _all_gather([w_in[0].astype(MXU_DTYPE)], "gather_weights0")
    lg = [ln_g[i][None, :] for i in range(2)]
    lb = [ln_b[i][None, :] for i in range(2)]
    lru_p = dict(conv_w=conv_w_f, conv_b=conv_b, wa=lru_wa[0].astype(MXU_DTYPE), wx=lru_wx[0].astype(MXU_DTYPE),
                 ba=lru_ba_f, bx=lru_bx_f, lam=lru_lam_f)
    zero_state = jnp.zeros((1, D_INNER), F32)

    (u0, g0), (wo0,) = _in_proj(xt, sc[0], sh[0], wi0, "in_proj0", sides=[("gather", [w_out[0].astype(MXU_DTYPE)])])
    (uc, _), _ = _in_proj(ctxt, scc, shc, wi0, "in_proj0_ctx")
    (hcf, cf, uvc), _ = _lru_fwd(uc, zero_state, lru_p, 0, "lru_fwd_ctx_f", conv=True)
    (hcb, cbk), _ = _lru_fwd(uvc, zero_state, lru_p, 1, "lru_fwd_ctx_b", conv=False)
    (hf, _, uv0), (wi1,) = _lru_fwd(u0, cf, lru_p, 0, "lru_fwd_f", conv=True,
                                    sides=[("gather", [w_in[1].astype(MXU_DTYPE)])])
    (hb, _), (wo1, pool_w_g) = _lru_fwd(
        uv0, cbk, lru_p, 1, "lru_fwd_b", conv=False,
        sides=[("gather", [w_out[1].astype(MXU_DTYPE), pool_w[0].astype(MXU_DTYPE)])])
    w_in_l = [wi0, wi1]
    w_out_l = [wo0.reshape(D_INNER, D_MODEL), wo1.reshape(D_INNER, D_MODEL)]
    pool_w_f = jnp.transpose(pool_w_g, (1, 0, 2, 3)).reshape(len(POOL_WINDOWS), POOL_GROUP, POOL_GROUP)
    x1, br0 = _out0(hf, hb, g0, xt, gt[0], w_out_l[0], lg[0], lb[0], "out0")
    (u1, g1), _ = _in_proj(x1, sc[1], sh[1], w_in_l[1], "in_proj1")
    dmix = _pool_mix(u1, False, MXU_DTYPE, "pool_fwd")
    dz1, st1 = _out1(dmix, pool_w_f, pool_scale_f, g1, x1, gt[1], w_out_l[1], lg[1], lb[1], tgt, "out1")
    loss_me = jnp.full((1, LANES), (0.5 / D_MODEL) * jnp.sum(st1[3]), F32)

    core = jnp.reshape(ci, (1,)).astype(jnp.int32)
    wo_view = lambda a: a.reshape(N_DEV, D_INNER // N_DEV, D_MODEL)
    pw_view = lambda a: _blocks_by_device(a, 1).reshape(N_DEV, POOL_GROUP // N_DEV * len(POOL_WINDOWS), POOL_GROUP)
    dd, dg1, gwo1, gpw, gps = _bout1(dz1, dmix, g1, pool_w_f, pool_scale_f, gt[1], w_out_l[1], "bwd_out1")
    du1 = _pool_mix(dd, True, MXU_DTYPE, "pool_bwd")
    dx1, gwi1, stb1 = _bin(du1, dg1, x1, dz1, sc[1], sh[1], w_in_l[1], "bwd_in1")
    bufs1 = [gwi1, wo_view(gwo1), pw_view(gpw)]
    (dz0, dy0, dg0, gwo0, stl0), recv1 = _bout0(dx1, xt, br0, lg[0], hf, hb, g0, gt[0], w_out_l[0], "bwd_out0",
                                                sides=[("sibling", bufs1)])
    pairs1 = [_pair_sum(b, r, core, "reduce_pair_" + n)
              for b, r, n in zip(bufs1, recv1, ["w_in1", "w_out1", "pool_w"])]
    (duvf, gwa_f, gwx_f, gv_f, dh0f), (p_wi1, p_wo1, p_pw, recv_wo0) = _lru_bwd(
        uv0, dy0, hf, cf, zero_state, lru_p, 0, "lru_bwd_f", sides=[("chips", pairs1), ("sibling", [wo_view(gwo0)])])
    pair_wo0 = _pair_sum(wo_view(gwo0), recv_wo0, core, "reduce_pair_w_out0")
    (duvb, gwa_b, gwx_b, gv_b, dh0b), (p_wo0,) = _lru_bwd(
        uv0, dy0, hb, cbk, zero_state, lru_p, 1, "lru_bwd_b", sides=[("chips", [pair_wo0])])
    zero_dh = jnp.zeros_like(uc)
    (ducf, gwa_cf, gwx_cf, gv_cf, _), _ = _lru_bwd(uvc, zero_dh, hcf, zero_state, dh0f, lru_p, 0, "lru_bwd_ctx_f")
    (ducb, gwa_cb, gwx_cb, gv_cb, _), _ = _lru_bwd(uvc, zero_dh, hcb, zero_state, dh0b, lru_p, 1, "lru_bwd_ctx_b")
    du0, cst0 = _conv_bwd(duvf, duvb, u0, conv_w_f, "conv_bwd")
    duc, cstc = _conv_bwd(ducf, ducb, uc, conv_w_f, "conv_bwd_ctx")
    gwic, stc = _bin(duc, None, ctxt, None, scc, shc, w_in_l[0][:N_WBLK // 2], "bwd_in0_ctx")
    gx, gwi0, stb0 = _bin(du0, dg0, xt, dz0, sc[0], sh[0], w_in_l[0], "bwd_in0", gw_init=gwic)

    zero_row = jnp.zeros((1, D_MODEL), F32)
    dm_me = jnp.stack([
        jnp.concatenate([jnp.concatenate([stb0[1:2], stb0[0:1], stl0[2:3]], axis=1),
                         jnp.concatenate([stc[1:2], stc[0:1], zero_row], axis=1)], axis=0),
        jnp.concatenate([jnp.concatenate([stb1[1:2], stb1[0:1], st1[2:3]], axis=1),
                         jnp.zeros((1, 3 * D_MODEL), F32)], axis=0)])
    dm_g, loss_g = _all_gather([dm_me, loss_me], "gather_dmod")
    loss = jnp.sum(loss_g[:, 0, 0])
    dm_all = jnp.concatenate([jnp.transpose(dm_g[:, :, 0], (1, 0, 2)), jnp.transpose(dm_g[:, :, 1], (1, 0, 2))],
                             axis=1)
    dm_my = lax.dynamic_slice(dm_all, (0, 0, dev * n_mod), (2, 16, n_mod))
    g_w_mod, g_b_mod, gcc_part = _mod_bwd(cond, dm_all, dm_my, w_mod, "mod_bwd")
    g_b_mod = g_b_mod.reshape(b_mod.shape)

    gwa = jnp.stack([gwa_f + gwa_cf, gwa_b + gwa_cb])
    gwx = jnp.stack([gwx_f + gwx_cf, gwx_b + gwx_cb])
    gv = jnp.stack([gv_f + gv_cf, gv_b + gv_cb])
    cst = cst0 + cstc
    g_ln_g = jnp.stack([stl0[0], st1[0]])
    g_ln_b = jnp.stack([stl0[1], st1[1]])
    sharded = [
        _blocks_by_device(cst[0:4], 1),
        _blocks_by_device(gv[:, 0], 1), _blocks_by_device(gv[:, 1], 1), _blocks_by_device(gv[:, 2], 1),
        _blocks_by_device(gps[0], 0),
    ]
    replicated = [gwa.reshape(-1), gwx.reshape(-1), g_ln_g.reshape(-1), g_ln_b.reshape(-1), cst[4],
                  gcc_part.reshape(-1)]
    sh_sizes = [int(np.prod(a.shape[1:])) for a in sharded]
    rep_sizes = [a.shape[0] // N_DEV for a in replicated]
    n_flat = sum(sh_sizes) + sum(rep_sizes)
    rows = -(-n_flat // LANES)
    rows = -(-rows // FLAT_ROWS) * FLAT_ROWS
    misc = jnp.concatenate([a.reshape(N_DEV, -1) for a in sharded] +
                           [a.reshape(N_DEV, -1) for a in replicated], axis=1)
    misc = jnp.pad(misc, ((0, 0), (0, rows * LANES - n_flat))).reshape(N_DEV, rows, LANES)
    bufs = [gwi0, misc]
    recvs = _sibling_exchange(bufs, "reduce_sibling")
    pairs = [_pair_sum(b, r, core, "reduce_pair_" + n) for b, r, n in zip(bufs, recvs, ["w_in0", "misc"])]
    p_wi0, p_misc = _chip_exchange(pairs, "reduce_chips")
    g_flat = _sum4(p_misc, "reduce_sum_misc").reshape(-1)

    offs = np.cumsum([0] + sh_sizes + rep_sizes)
    n_sh = len(sh_sizes)
    sh_shapes = [conv_w.shape, lru_ba.shape, lru_bx.shape, lru_lam.shape, pool_scale.shape]
    g_sh = [g_flat[offs[k]:offs[k + 1]].reshape(sh_shapes[k]) for k in range(n_sh)]
    g_conv_w, g_lru_ba, g_lru_bx, g_lru_lam, g_pool_scale = g_sh
    rep_block = _to_rows([g_flat[offs[n_sh]:offs[-1]]], SUBLANES)
    rep_all, = _all_gather([rep_block], "gather_replicated")
    rep_flat = rep_all.reshape(N_DEV, -1)
    rep_full, off = [], 0
    for n in rep_sizes:
        rep_full.append(rep_flat[:, off:off + n].reshape(-1))
        off += n
    g_lru_wa = rep_full[0].reshape(lru_wa.shape)
    g_lru_wx = rep_full[1].reshape(lru_wx.shape)
    g_ln_g = rep_full[2].reshape(ln_g.shape)
    g_ln_b = rep_full[3].reshape(ln_b.shape)
    g_conv_b = rep_full[4].reshape(conv_b.shape)
    g_c_ctx = rep_full[5].reshape(c_ctx.shape)

    names = ["c_ctx", "w_mod", "b_mod", "w_in", "w_out", "ln_g", "ln_b", "conv_w", "conv_b", "lru_wa", "lru_ba",
             "lru_wx", "lru_bx", "lru_lam", "pool_w", "pool_scale"]
    weights = dict(c_ctx=c_ctx, w_mod=w_mod, b_mod=b_mod, w_in=w_in, w_out=w_out, ln_g=ln_g, ln_b=ln_b,
                   conv_w=conv_w, conv_b=conv_b, lru_wa=lru_wa, lru_ba=lru_ba, lru_wx=lru_wx, lru_bx=lru_bx,
                   lru_lam=lru_lam, pool_w=pool_w, pool_scale=pool_scale)
    mom_m = dict(c_ctx=m_c_ctx, w_mod=m_w_mod, b_mod=m_b_mod, w_in=m_w_in, w_out=m_w_out, ln_g=m_ln_g, ln_b=m_ln_b,
                 conv_w=m_conv_w, conv_b=m_conv_b, lru_wa=m_lru_wa, lru_ba=m_lru_ba, lru_wx=m_lru_wx,
                 lru_bx=m_lru_bx, lru_lam=m_lru_lam, pool_w=m_pool_w, pool_scale=m_pool_scale)
    mom_v = dict(c_ctx=v_c_ctx, w_mod=v_w_mod, b_mod=v_b_mod, w_in=v_w_in, w_out=v_w_out, ln_g=v_ln_g, ln_b=v_ln_b,
                 conv_w=v_conv_w, conv_b=v_conv_b, lru_wa=v_lru_wa, lru_ba=v_lru_ba, lru_wx=v_lru_wx,
                 lru_bx=v_lru_bx, lru_lam=v_lru_lam, pool_w=v_pool_w, pool_scale=v_pool_scale)
    grads = dict(c_ctx=g_c_ctx, w_mod=g_w_mod, b_mod=g_b_mod, ln_g=g_ln_g, ln_b=g_ln_b,
                 conv_w=g_conv_w, conv_b=g_conv_b, lru_wa=g_lru_wa, lru_ba=g_lru_ba, lru_wx=g_lru_wx,
                 lru_bx=g_lru_bx, lru_lam=g_lru_lam)
    grads["pool_scale"] = g_pool_scale
    delta, new_m, new_v = {}, {}, {}

    def update_parts(n, parts, view):
        res = _adamw_parts(weights[n].reshape(view), parts, mom_m[n].reshape(view), mom_v[n].reshape(view),
                           "adamw_" + n)
        grads[n], delta[n], new_m[n], new_v[n] = [r.reshape(weights[n].shape) for r in res]

    update_parts("w_in", [p_wi0, p_wi1], w_in.shape)
    update_parts("w_out", [p_wo0, p_wo1], w_out.shape)
    update_parts("pool_w", [p_pw], (1,) + p_pw.shape[1:])
    for n in ("w_mod", "lru_wa", "lru_wx"):
        shape = weights[n].shape
        view = (int(np.prod(shape[:-1])), shape[-1])
        res = _adamw(weights[n].reshape(view), grads[n].reshape(view), mom_m[n].reshape(view),
                     mom_v[n].reshape(view), "adamw_" + n)
        delta[n], new_m[n], new_v[n] = [r.reshape(shape) for r in res]

    small = [n for n in names if n not in delta]
    shapes = [weights[n].shape for n in small]
    flat = lambda d: _to_rows([d[n] for n in small], FLAT_ROWS)
    res = _adamw(flat(weights), flat(grads), flat(mom_m), flat(mom_v), "adamw_small")
    for d, r in zip((delta, new_m, new_v), res):
        d.update(zip(small, _split_rows(r, shapes)))

    return (loss, gx[None], *[grads[n] for n in names], *[delta[n] for n in names],
            *[new_m[n] for n in names], *[new_v[n] for n in names])
```

```python
import functools

import numpy as np
import jax
import jax.numpy as jnp
from jax import lax
from jax.experimental import pallas as pl
from jax.experimental.pallas import tpu as pltpu

F32 = jnp.float32
BF16 = jnp.bfloat16
MXU_DTYPE = BF16

D_MODEL = 1024
D_INNER = 2048
LRU_BLOCK = 128
GRID_W = 64
POOL_WINDOWS = (2, 4, 8, 16)
POOL_GROUP = 512
ALPHA = float(4 ** 0.25)
LN_EPS = 1e-5
LRU_C = 8.0
N_DEV = 8
N_WBLK = 8
WBLK = 512

ADAM_LR = 0.001
ADAM_B1 = 0.9
ADAM_B2 = 0.999
ADAM_EPS = 1e-08
ADAM_WD = 0.01
ADAM_STEP = 10

LANES = 128
SUBLANES = 8
V7X_VMEM_BYTES = 64 * 1024 * 1024
VMEM_LIMIT = V7X_VMEM_BYTES - 8 * 1024 * 1024
MESH = pl.DeviceIdType.MESH
ANY = pl.BlockSpec(memory_space=pl.ANY)

TM_MM = 512
TM_BWD = 256
TM_LRU = 512
TM_LRU_FWD = 1024
CB_LRU = 512
N_SEG = 8
SCAN_UNROLL = 4
SCAN_ROW_T = 33
SCAN_ROW_J = 4
SQRT_FLOOR = 1e-30
FLAT_ROWS = 16
ELEMENTWISE_TILE_BYTES = 1 << 20
POOL_TOK = 256
WIRE_DTYPE = BF16


def _cparams(**kw):
    return pltpu.CompilerParams(vmem_limit_bytes=VMEM_LIMIT, **kw)


def _my_pos():
    return lax.axis_index("x"), lax.axis_index("y"), lax.axis_index("c")


def _dot(a, b):
    return jnp.dot(a.astype(MXU_DTYPE), b.astype(MXU_DTYPE), preferred_element_type=F32)


def _dot_tn(a, b):
    return lax.dot_general(a.astype(MXU_DTYPE), b.astype(MXU_DTYPE), (((0,), (0,)), ((), ())),
                           preferred_element_type=F32)


def _dot_nt(a, b):
    return lax.dot_general(a.astype(MXU_DTYPE), b.astype(MXU_DTYPE), (((1,), (1,)), ((), ())),
                           preferred_element_type=F32)


def _sigmoid(z):
    return 0.5 * jnp.tanh(0.5 * z) + 0.5


def _log_sigmoid(x):
    y = jnp.exp(-jnp.abs(x))
    u = 1.0 + y
    l1p = jnp.where(u == 1.0, y, jnp.log(u) * (y / jnp.where(u == 1.0, 1.0, u - 1.0)))
    return jnp.minimum(x, 0.0) - l1p


def _rowsum(v):
    return jnp.sum(v, axis=0, keepdims=True)


def _layer_norm_stats(z):
    mu = jnp.mean(z, axis=-1, keepdims=True)
    zc = z - mu
    var = jnp.mean(zc * zc, axis=-1, keepdims=True)
    rstd = lax.rsqrt(var + LN_EPS)
    return zc * rstd, rstd


def _layer_norm_bwd(dy, xhat, rstd, g):
    dxh = dy * g
    m1 = jnp.mean(dxh, axis=-1, keepdims=True)
    m2 = jnp.mean(dxh * xhat, axis=-1, keepdims=True)
    return rstd * (dxh - m1 - xhat * m2)


def _shifted(v, before8, after8, offsets):
    n = v.shape[0]
    ext = jnp.concatenate([before8, v, after8], axis=0)
    total = n + 2 * SUBLANES
    return [pltpu.roll(ext, (-k) % total, 0)[SUBLANES:SUBLANES + n] for k in offsets]


def _rows8(row):
    return jnp.broadcast_to(row, (SUBLANES, row.shape[1]))


def _shift_down(v, first_row):
    return _shifted(v, _rows8(first_row), _rows8(first_row), [-1])[0]


def _shift_up(v, last_row):
    return _shifted(v, _rows8(last_row), _rows8(last_row), [1])[0]


def _all_gather(blocks, name):
    n = len(blocks)

    def body(*refs):
        x_refs, out_refs = refs[:n], refs[n:2 * n]
        send_sems, recv_sems, local_sems = refs[2 * n:]
        x, y, c = _my_pos()
        me, sibling = (x, y, c), (x, y, 1 - c)
        chips = [(1 - x, y), (x, 1 - y), (1 - x, 1 - y)]

        def slot(a, px, py, pc):
            return out_refs[a].at[4 * px + 2 * py + pc]

        def copy(a, k, block, to, src=None):
            return pltpu.make_async_remote_copy(
                src_ref=slot(a, *block) if src is None else src, dst_ref=slot(a, *block),
                send_sem=send_sems.at[a, k], recv_sem=recv_sems.at[a, k], device_id=to, device_id_type=MESH)

        mine = [pltpu.make_async_copy(x_refs[a], slot(a, *me), local_sems.at[a]) for a in range(n)]
        for cp in mine:
            cp.start()
        first = []
        for a in range(n):
            first.append(copy(a, 0, me, sibling, src=x_refs[a]))
            first += [copy(a, 1 + j, me, (*chip, c), src=x_refs[a]) for j, chip in enumerate(chips)]
        for cp in first:
            cp.start()
        passed = []
        for j, chip in enumerate(chips):
            for a in range(n):
                copy(a, 1 + j, (*chip, c), me).wait_recv()
                fwd = copy(a, 4 + j, (*chip, c), sibling)
                fwd.start()
                passed.append(fwd)
        for a in range(n):
            copy(a, 0, sibling, me).wait_recv()
            for j, chip in enumerate(chips):
                copy(a, 4 + j, (*chip, 1 - c), me).wait_recv()
        for cp in first + passed:
            cp.wait_send()
        for cp in mine:
            cp.wait()

    outs = pl.pallas_call(
        body, name=name,
        out_shape=[jax.ShapeDtypeStruct((N_DEV,) + b.shape, b.dtype) for b in blocks],
        in_specs=[ANY] * n, out_specs=[ANY] * n,
        scratch_shapes=[pltpu.SemaphoreType.DMA((n, 7)), pltpu.SemaphoreType.DMA((n, 7)),
                        pltpu.SemaphoreType.DMA((n,))],
    )(*blocks)
    return list(outs)


def _sibling_exchange(bufs, name):
    n = len(bufs)

    def body(*refs):
        srcs, outs = refs[:n], refs[n:2 * n]
        send_sems, recv_sems = refs[2 * n:]
        x, y, c = _my_pos()
        copies = [pltpu.make_async_remote_copy(
            src_ref=srcs[a].at[2 * j + (1 - c)], dst_ref=outs[a].at[j], send_sem=send_sems.at[a, j],
            recv_sem=recv_sems.at[a, j], device_id=(x, y, 1 - c), device_id_type=MESH)
            for a in range(n) for j in range(4)]
        for cp in copies:
            cp.start()
        for cp in copies:
            cp.wait()

    outs = pl.pallas_call(
        body, name=name, out_shape=[jax.ShapeDtypeStruct((4,) + b.shape[1:], b.dtype) for b in bufs],
        in_specs=[ANY] * n, out_specs=[ANY] * n,
        scratch_shapes=[pltpu.SemaphoreType.DMA((n, 4)), pltpu.SemaphoreType.DMA((n, 4))],
    )(*bufs)
    return list(outs)


def _chip_exchange(parts, name):
    n = len(parts)

    def body(*refs):
        srcs, outs = refs[:n], refs[n:2 * n]
        send_sems, recv_sems, local_sems = refs[2 * n:]
        x, y, c = _my_pos()
        jme = 2 * x + y
        peers = [(1 - x, y), (x, 1 - y), (1 - x, 1 - y)]
        local = [pltpu.make_async_copy(srcs[a].at[jme], outs[a].at[jme], local_sems.at[a]) for a in range(n)]
        for cp in local:
            cp.start()

        def copy(a, k, px, py, dst_slot):
            return pltpu.make_async_remote_copy(
                src_ref=srcs[a].at[2 * px + py], dst_ref=outs[a].at[dst_slot], send_sem=send_sems.at[a, k],
                recv_sem=recv_sems.at[a, k], device_id=(px, py, c), device_id_type=MESH)

        sends = [copy(a, k, px, py, jme) for a in range(n) for k, (px, py) in enumerate(peers)]
        for cp in sends:
            cp.start()
        for a in range(n):
            for k, (px, py) in enumerate(peers):
                copy(a, k, px, py, 2 * px + py).wait_recv()
        for cp in sends:
            cp.wait_send()
        for cp in local:
            cp.wait()

    outs = pl.pallas_call(
        body, name=name, out_shape=[jax.ShapeDtypeStruct(p.shape, p.dtype) for p in parts],
        in_specs=[ANY] * n, out_specs=[ANY] * n,
        scratch_shapes=[pltpu.SemaphoreType.DMA((n, 3)), pltpu.SemaphoreType.DMA((n, 3)),
                        pltpu.SemaphoreType.DMA((n,))],
    )(*parts)
    return list(outs)


_SIDE_REMOTE = {"gather": 7, "sibling": 4, "chips": 3}
_FLIPS = [(0, 0, 1), (1, 0, 0), (0, 1, 0), (1, 1, 0), (1, 0, 1), (0, 1, 1), (1, 1, 1)]


def _side_plan(sides):
    inputs, out_shapes, scratch = [], [], []
    for kind, arrays in sides:
        n = len(arrays)
        for a in arrays:
            inputs.append(a)
            shape = {"gather": (N_DEV,) + a.shape, "sibling": (4,) + a.shape[1:], "chips": a.shape}[kind]
            out_shapes.append(jax.ShapeDtypeStruct(shape, a.dtype))
        scratch += [pltpu.SemaphoreType.DMA((n, _SIDE_REMOTE[kind])), pltpu.SemaphoreType.DMA((n, _SIDE_REMOTE[kind])),
                    pltpu.SemaphoreType.DMA((n,))]
    return inputs, out_shapes, scratch


def _side_copies(sides, in_refs, out_refs, sem_refs):
    x, y, c = _my_pos()
    starts, waits = [], []
    pos = 0
    for s, (kind, arrays) in enumerate(sides):
        send_sems, recv_sems, local_sems = sem_refs[3 * s:3 * s + 3]
        for a in range(len(arrays)):
            src, out = in_refs[pos], out_refs[pos]
            pos += 1

            def remote(k, src_ref, dst_ref, to):
                return pltpu.make_async_remote_copy(src_ref=src_ref, dst_ref=dst_ref, send_sem=send_sems.at[a, k],
                                                    recv_sem=recv_sems.at[a, k], device_id=to, device_id_type=MESH)

            def local(src_ref, dst_ref):
                cp = pltpu.make_async_copy(src_ref, dst_ref, local_sems.at[a])
                starts.append(cp.start)
                waits.append(cp.wait)

            if kind == "gather":
                me = 4 * x + 2 * y + c
                local(src, out.at[me])
                for k, (fx, fy, fc) in enumerate(_FLIPS):
                    px, py, pc = (1 - x if fx else x), (1 - y if fy else y), (1 - c if fc else c)
                    send = remote(k, src, out.at[me], (px, py, pc))
                    starts.append(send.start)
                    waits += [remote(k, src, out.at[4 * px + 2 * py + pc], (px, py, pc)).wait_recv, send.wait_send]
            elif kind == "sibling":
                for j in range(4):
                    cp = remote(j, src.at[2 * j + (1 - c)], out.at[j], (x, y, 1 - c))
                    starts.append(cp.start)
                    waits.append(cp.wait)
            else:
                jme = 2 * x + y
                local(src.at[jme], out.at[jme])
                for k, (px, py) in enumerate([(1 - x, y), (x, 1 - y), (1 - x, 1 - y)]):
                    send = remote(k, src.at[2 * px + py], out.at[jme], (px, py, c))
                    starts.append(send.start)
                    waits += [remote(k, src.at[2 * px + py], out.at[2 * px + py], (px, py, c)).wait_recv,
                              send.wait_send]
    return starts, waits


def _call_with_sides(body, sides, *, name, grid, in_specs, out_specs, out_shape, scratch_shapes, compiler_params, args):
    if not sides:
        res = pl.pallas_call(body, name=name, grid=grid, in_specs=in_specs, out_specs=out_specs, out_shape=out_shape,
                             scratch_shapes=scratch_shapes, compiler_params=compiler_params)(*args)
        return list(res), []
    s_in, s_out, s_scr = _side_plan(sides)
    n_in, n_out, n_scr, n_side = len(in_specs), len(out_specs), len(scratch_shapes), len(s_in)

    def wrapped(*refs):
        refs = list(refs)
        ins, side_in = refs[:n_in], refs[n_in:n_in + n_side]
        outs = refs[n_in + n_side:n_in + n_side + n_out]
        side_out = refs[n_in + n_side + n_out:n_in + 2 * n_side + n_out]
        rest = refs[n_in + 2 * n_side + n_out:]
        starts, waits = _side_copies(sides, side_in, side_out, rest[n_scr:])
        first = functools.reduce(jnp.logical_and, [pl.program_id(d) == 0 for d in range(len(grid))])
        last = functools.reduce(jnp.logical_and, [pl.program_id(d) == grid[d] - 1 for d in range(len(grid))])

        @pl.when(first)
        def _():
            for start in starts:
                start()

        body(*ins, *outs, *rest[:n_scr])

        @pl.when(last)
        def _():
            for wait in waits:
                wait()

    res = pl.pallas_call(
        wrapped, name=name, grid=grid, in_specs=list(in_specs) + [ANY] * n_side,
        out_specs=list(out_specs) + [ANY] * n_side, out_shape=list(out_shape) + s_out,
        scratch_shapes=list(scratch_shapes) + s_scr, compiler_params=compiler_params,
    )(*args, *s_in)
    return list(res[:n_out]), list(res[n_out:])


def _row_tile(r, l):
    t = min(r, max(16, ELEMENTWISE_TILE_BYTES // (4 * l) // 16 * 16))
    while r % t:
        t -= 16
    return t


def _pair_sum(buf, recv, core, name):
    _, r, l = buf.shape
    tr = _row_tile(r, l)

    def body(core_ref, a_ref, b_ref, o_ref):
        o_ref[...] = (a_ref[...] + b_ref[...]).astype(WIRE_DTYPE)

    return pl.pallas_call(
        body, name=name, out_shape=jax.ShapeDtypeStruct((4, r, l), WIRE_DTYPE),
        grid_spec=pltpu.PrefetchScalarGridSpec(
            num_scalar_prefetch=1, grid=(4, r // tr),
            in_specs=[pl.BlockSpec((None, tr, l), lambda j, i, cr: (2 * j + cr[0], i, 0)),
                      pl.BlockSpec((None, tr, l), lambda j, i, cr: (j, i, 0))],
            out_specs=pl.BlockSpec((None, tr, l), lambda j, i, cr: (j, i, 0))),
        compiler_params=_cparams(dimension_semantics=("arbitrary", "arbitrary")),
    )(core, buf, recv)


def _sum_parts(p_ref):
    return ((p_ref[0].astype(F32) + p_ref[1].astype(F32)) + (p_ref[2].astype(F32) + p_ref[3].astype(F32)))


def _sum4(parts, name):
    _, r, l = parts.shape
    tr = _row_tile(r, l)

    def body(p_ref, o_ref):
        o_ref[...] = _sum_parts(p_ref)

    return pl.pallas_call(
        body, name=name, out_shape=jax.ShapeDtypeStruct((r, l), F32), grid=(r // tr,),
        in_specs=[pl.BlockSpec((4, tr, l), lambda i: (0, i, 0))],
        out_specs=pl.BlockSpec((tr, l), lambda i: (i, 0)),
        compiler_params=_cparams(dimension_semantics=("arbitrary",)),
    )(parts)


def _adamw_update(w, gg, m, v):
    nm = ADAM_B1 * m + (1.0 - ADAM_B1) * gg
    nv = ADAM_B2 * v + (1.0 - ADAM_B2) * (gg * gg)
    m_hat = nm / (1.0 - ADAM_B1 ** ADAM_STEP)
    v_hat = nv / (1.0 - ADAM_B2 ** ADAM_STEP)
    return -ADAM_LR * (m_hat / (jnp.sqrt(v_hat) + ADAM_EPS) + ADAM_WD * w), nm, nv


def _adamw(w, g, m, v, name):
    r, l = w.shape
    tr = _row_tile(r, l)

    def body(w_ref, g_ref, m_ref, v_ref, d_ref, nm_ref, nv_ref):
        d_ref[...], nm_ref[...], nv_ref[...] = _adamw_update(w_ref[...], g_ref[...], m_ref[...], v_ref[...])

    spec = pl.BlockSpec((tr, l), lambda i: (i, 0))
    return pl.pallas_call(
        body, name=name, out_shape=[jax.ShapeDtypeStruct((r, l), F32)] * 3, grid=(r // tr,),
        in_specs=[spec] * 4, out_specs=[spec] * 3,
        compiler_params=_cparams(dimension_semantics=("arbitrary",)),
    )(w, g, m, v)


def _adamw_parts(w, parts, m, v, name):
    nl, r, l = w.shape
    tr = _row_tile(r, l)

    def body(*refs):
        w_ref, p_refs, (m_ref, v_ref, g_ref, d_ref, nm_ref, nv_ref) = refs[0], refs[1:1 + nl], refs[1 + nl:]
        layer = pl.program_id(0)
        gg = _sum_parts(p_refs[0])
        for q in range(1, nl):
            gg = jnp.where(layer == q, _sum_parts(p_refs[q]), gg)
        g_ref[...] = gg
        d_ref[...], nm_ref[...], nv_ref[...] = _adamw_update(w_ref[...], gg, m_ref[...], v_ref[...])

    spec = pl.BlockSpec((None, tr, l), lambda q, i: (q, i, 0))
    pspecs = [pl.BlockSpec((4, tr, l), lambda q, i, k=k: (0, jnp.where(q == k, i, 0), 0)) for k in range(nl)]
    return pl.pallas_call(
        body, name=name, out_shape=[jax.ShapeDtypeStruct((nl, r, l), F32)] * 4, grid=(nl, r // tr),
        in_specs=[spec] + pspecs + [spec, spec], out_specs=[spec] * 4,
        compiler_params=_cparams(dimension_semantics=("arbitrary", "arbitrary")),
    )(w, *parts, m, v)


def _to_rows(pieces, row_multiple):
    flat = jnp.concatenate([p.reshape(-1) for p in pieces])
    rows = -(-flat.shape[0] // LANES)
    rows = -(-rows // row_multiple) * row_multiple
    flat = jnp.pad(flat, (0, rows * LANES - flat.shape[0]))
    return flat.reshape(rows, LANES)


def _split_rows(rows, shapes):
    flat = rows.reshape(-1)
    out, off = [], 0
    for s in shapes:
        n = int(np.prod(s))
        out.append(flat[off:off + n].reshape(s))
        off += n
    return out


def _mod_fwd(cond, w_mod, b_my, name):
    nl, _, ncol = w_mod.shape

    def body(a_ref, w_ref, b_ref, o_ref):
        a = a_ref[...]
        s = a * _sigmoid(a)
        for i in range(nl):
            o_ref[i] = _dot(s, w_ref[i]) + b_ref[i]

    return pl.pallas_call(
        body, name=name, out_shape=jax.ShapeDtypeStruct((nl, 16, ncol), F32),
        compiler_params=_cparams(),
    )(cond, w_mod, b_my)


def _mod_bwd(cond, dm_all, dm_my, w_mod, name):
    nl, _, ncol = w_mod.shape

    def body(a_ref, dma_ref, dmm_ref, w_ref, gw_ref, gb_ref, gc_ref):
        a = a_ref[...]
        sg = _sigmoid(a)
        s = a * sg
        for i in range(nl):
            gw_ref[i] = _dot_tn(s, dmm_ref[i])
            gb_ref[i] = jnp.sum(dma_ref[i], axis=0, keepdims=True)
        back = _dot_nt(dmm_ref[0], w_ref[0])
        dsilu = sg * (1.0 + a * (1.0 - sg))
        gc_ref[...] = jnp.sum(back[8:16] * dsilu[8:16], axis=0, keepdims=True)

    return pl.pallas_call(
        body, name=name,
        out_shape=[jax.ShapeDtypeStruct((nl, D_MODEL, ncol), F32), jax.ShapeDtypeStruct((nl, 1, 3 * D_MODEL), F32),
                   jax.ShapeDtypeStruct((1, D_MODEL), F32)],
        compiler_params=_cparams(),
    )(cond, dm_all, dm_my, w_mod)


def _in_proj(xt, sc, sh, wg, name, sides=()):
    t = xt.shape[0]
    tm = min(TM_MM, t)

    def body(x_ref, sc_ref, sh_ref, w_ref, u_ref, g_ref):
        h = (x_ref[...] * (1.0 + sc_ref[...]) + sh_ref[...]).astype(MXU_DTYPE)
        for k in range(N_WBLK):
            o = jnp.dot(h, w_ref[k], preferred_element_type=F32)
            if k < N_WBLK // 2:
                u_ref[:, k * WBLK:(k + 1) * WBLK] = o
            else:
                kk = k - N_WBLK // 2
                g_ref[:, kk * WBLK:(kk + 1) * WBLK] = o

    row = pl.BlockSpec((1, D_MODEL), lambda i: (0, 0))
    return _call_with_sides(
        body, sides, name=name, out_shape=[jax.ShapeDtypeStruct((t, D_INNER), F32)] * 2, grid=(t // tm,),
        in_specs=[pl.BlockSpec((tm, D_MODEL), lambda i: (i, 0)), row, row,
                  pl.BlockSpec((N_WBLK, D_MODEL, WBLK), lambda i: (0, 0, 0), pipeline_mode=pl.Buffered(1))],
        out_specs=[pl.BlockSpec((tm, D_INNER), lambda i: (i, 0))] * 2, scratch_shapes=[],
        compiler_params=_cparams(dimension_semantics=("arbitrary",)), args=[xt, sc, sh, wg])


def _halo_maps(nt, tm, n_rows8, pos):
    per = tm // SUBLANES
    prev = lambda cb, i: (jnp.maximum(pos(i) * per - 1, 0), cb)
    nxt = lambda cb, i: (jnp.minimum((pos(i) + 1) * per, n_rows8 - 1), cb)
    return prev, nxt


def _conv_taps(u, prev8, next8, is_first, is_last):
    pz = jnp.where(is_first, 0.0, 1.0)
    nz = jnp.where(is_last, 0.0, 1.0)
    return _shifted(u, prev8 * pz, next8 * nz, [-2, -1, 1])


def _lru_gates(uv, wa_ref, wx_ref, ba, bx, cl, g):
    sl = slice(g * LANES, (g + 1) * LANES)
    uvg = uv[:, sl]
    r = _sigmoid(_dot(uvg, wa_ref[g]) + ba[:, sl])
    ii = _sigmoid(_dot(uvg, wx_ref[g]) + bx[:, sl])
    la = cl[:, sl] * r
    a = jnp.exp(la)
    q = jnp.tanh(-la) * (1.0 + a * a)
    rs = lax.rsqrt(jnp.maximum(q, SQRT_FLOOR))
    return uvg, r, ii, a, q * rs, rs


def _scan_rows(seg):
    return -(-(SCAN_ROW_T * (seg - 1) + SCAN_ROW_J * (N_SEG - 1) + 1) // SUBLANES) * SUBLANES


def _seg_chunk(j, c):
    return pl.ds(SCAN_ROW_T * SUBLANES * c + SCAN_ROW_J * j, SUBLANES, stride=SCAN_ROW_T)


def _seg_scatter(ref, g, seg, value):
    for j in range(N_SEG):
        for c in range(seg // SUBLANES):
            r0 = j * seg + SUBLANES * c
            ref[g, _seg_chunk(j, c), :] = value[r0:r0 + SUBLANES]


def _scan_tile(a_s, b_s, carry_ref, write_out, seg, reverse):
    n_g = a_s.shape[0]
    unroll = SCAN_UNROLL if seg % SCAN_UNROLL == 0 else 1

    n_trips = seg // unroll

    def steps(k, state):
        hs, cs = list(state[0]), list(state[1])
        base = ((n_trips - 1 - k) if reverse else k) * unroll
        for q in (range(unroll - 1, -1, -1) if reverse else range(unroll)):
            t = base + q
            rows = pl.ds(t * SCAN_ROW_T, N_SEG, stride=SCAN_ROW_J)
            for g in range(n_g):
                a = a_s[g, rows, :]
                b = b_s[g, rows, :]
                hs[g] = a * hs[g] + b
                cs[g] = a * cs[g]
                b_s[g, rows, :] = hs[g]
                a_s[g, rows, :] = cs[g]
        return tuple(hs), tuple(cs)

    zeros = tuple(jnp.zeros((N_SEG, LANES), F32) for _ in range(n_g))
    ones = tuple(jnp.ones((N_SEG, LANES), F32) for _ in range(n_g))
    h_fin, a_fin = lax.fori_loop(0, seg // unroll, steps, (zeros, ones))

    order = list(range(N_SEG - 1, -1, -1)) if reverse else list(range(N_SEG))
    for g in range(n_g):
        carry = carry_ref[:, g * LANES:(g + 1) * LANES]
        for j in order:
            for c in range(seg // SUBLANES):
                rows = _seg_chunk(j, c)
                write_out(j, c, g, b_s[g, rows, :] + a_s[g, rows, :] * carry)
            carry = a_fin[g][j:j + 1] * carry + h_fin[g][j:j + 1]
        carry_ref[:, g * LANES:(g + 1) * LANES] = carry


def _lru_specs(s, tm, cb, direction_pos, nt):
    n_rows8 = s // SUBLANES
    prev, nxt = _halo_maps(nt, tm, n_rows8, direction_pos)
    tile = pl.BlockSpec((tm, cb), lambda c, i: (direction_pos(i), c))
    return tile, pl.BlockSpec((SUBLANES, cb), prev), pl.BlockSpec((SUBLANES, cb), nxt)


def _lru_param_specs(cb, d):
    n_g = cb // LANES
    vec = pl.BlockSpec((1, cb), lambda c, i: (0, c))
    dvec = pl.BlockSpec((None, 1, cb), lambda c, i: (d, 0, c))
    wmat = pl.BlockSpec((None, n_g, LRU_BLOCK, LRU_BLOCK), lambda c, i: (d, c, 0, 0))
    return vec, dvec, wmat


def _lru_fwd(src, h0, p, d, name, conv, sides=()):
    s = src.shape[0]
    tm = min(TM_LRU_FWD, s)
    cb = CB_LRU
    n_g = cb // LANES
    nt = s // tm
    seg = tm // N_SEG
    pos = (lambda i: i) if d == 0 else (lambda i: nt - 1 - i)

    def body(*refs):
        refs = list(refs)
        u_ref = refs.pop(0)
        if conv:
            up_ref, un_ref, cw_ref, cbias_ref = [refs.pop(0) for _ in range(4)]
        wa_ref, wx_ref, ba_ref, bx_ref, lam_ref, h0_ref, h_ref, hc_ref = [refs.pop(0) for _ in range(8)]
        uv_ref = refs.pop(0) if conv else None
        a_s, b_s = refs
        i = pl.program_id(1)
        tp = pos(i)

        @pl.when(i == 0)
        def _():
            hc_ref[...] = h0_ref[...]

        if conv:
            u_t = u_ref[...]
            um2, um1, up1 = _conv_taps(u_t, up_ref[...], un_ref[...], tp == 0, tp == nt - 1)
            cw = cw_ref[...]
            uv_ref[...] = um2 * cw[0:1] + um1 * cw[1:2] + u_t * cw[2:3] + up1 * cw[3:4] + cbias_ref[...]
        src_ref = uv_ref if conv else u_ref
        cl = LRU_C * _log_sigmoid(lam_ref[...])
        ba, bx = ba_ref[...], bx_ref[...]
        for g in range(n_g):
            uvg, r, ii, a, sq, _ = _lru_gates(src_ref, wa_ref, wx_ref, ba, bx, cl, g)
            b = sq * (ii * uvg)
            _seg_scatter(a_s, g, seg, a)
            _seg_scatter(b_s, g, seg, b)

        def write_out(j, c, g, h):
            h_ref[pl.ds(j * seg + SUBLANES * c, SUBLANES), pl.ds(g * LANES, LANES)] = h

        _scan_tile(a_s, b_s, hc_ref, write_out, seg, reverse=(d == 1))

    tile, prev, nxt = _lru_specs(s, tm, cb, pos, nt)
    vec, dvec, wmat = _lru_param_specs(cb, d)
    wide = jax.ShapeDtypeStruct((s, D_INNER), F32)
    conv_specs = [prev, nxt, pl.BlockSpec((4, cb), lambda c, i: (0, c)), vec] if conv else []
    conv_args = [src, src, p["conv_w"], p["conv_b"]] if conv else []
    return _call_with_sides(
        body, sides, name=name,
        out_shape=[wide, jax.ShapeDtypeStruct((1, D_INNER), F32)] + ([wide] if conv else []),
        grid=(D_INNER // cb, nt),
        in_specs=[tile] + conv_specs + [wmat, wmat, dvec, dvec, dvec, vec],
        out_specs=[tile, vec] + ([tile] if conv else []),
        scratch_shapes=[pltpu.VMEM((n_g, _scan_rows(seg), LANES), F32)] * 2,
        compiler_params=_cparams(dimension_semantics=("arbitrary", "arbitrary")),
        args=[src, *conv_args, p["wa"], p["wx"], p["ba"], p["bx"], p["lam"], h0])


def _lru_bwd(uv, dh, h, h0, lam_in, p, d, name, sides=()):
    s = uv.shape[0]
    tm = min(TM_LRU, s)
    cb = CB_LRU
    n_g = cb // LANES
    nt = s // tm
    seg = tm // N_SEG
    pos = (lambda i: nt - 1 - i) if d == 0 else (lambda i: i)

    def body(uv_ref, dh_ref, h_ref, hh_ref, wa_ref, wx_ref, ba_ref, bx_ref,
             lam_ref, h0_ref, lin_ref, duv_ref, gwa_ref, gwx_ref, gv_ref, lc_ref, a_s, b_s, lp_s,
             r_s, i_s, q_s, rq_s, a_keep):
        i = pl.program_id(1)
        tp = pos(i)

        @pl.when(i == 0)
        def _():
            lc_ref[...] = lin_ref[...]
            gwa_ref[...] = jnp.zeros_like(gwa_ref)
            gwx_ref[...] = jnp.zeros_like(gwx_ref)
            gv_ref[...] = jnp.zeros_like(gv_ref)

        uv = uv_ref[...]
        lam = lam_ref[...]
        cl = LRU_C * _log_sigmoid(lam)
        ba, bx = ba_ref[...], bx_ref[...]
        dh_t = dh_ref[...]
        carry_in = lc_ref[...]
        for g in range(n_g):
            sl = slice(g * LANES, (g + 1) * LANES)
            _, r, ii, a, sq, rs = _lru_gates(uv, wa_ref, wx_ref, ba, bx, cl, g)
            r_s[:, sl], i_s[:, sl], q_s[:, sl], rq_s[:, sl], a_keep[:, sl] = r, ii, sq, rs, a
            b = a * dh_t[:, sl]
            _seg_scatter(a_s, g, seg, a)
            _seg_scatter(b_s, g, seg, b)

        def write_out(j, c, g, v):
            lp_s[pl.ds(j * seg + SUBLANES * c, SUBLANES), pl.ds(g * LANES, LANES)] = v

        _scan_tile(a_s, b_s, lc_ref, write_out, seg, reverse=(d == 0))

        h_t = h_ref[...]
        hh = hh_ref[...]
        if d == 0:
            edge = jnp.where(tp == 0, h0_ref[...], hh[7:8])
            h_prev = _shift_down(h_t, edge)
            lam_t = dh_t + _shift_up(lp_s[...], carry_in)
        else:
            edge = jnp.where(tp == nt - 1, h0_ref[...], hh[0:1])
            h_prev = _shift_up(h_t, edge)
            lam_t = dh_t + _shift_down(lp_s[...], carry_in)

        dsig = LRU_C * _sigmoid(-lam)
        for g in range(n_g):
            sl = slice(g * LANES, (g + 1) * LANES)
            uvg, r, ii, a, sq = uv[:, sl], r_s[:, sl], i_s[:, sl], a_keep[:, sl], q_s[:, sl]
            lt = lam_t[:, sl]
            ls = lt * sq
            dla = (lt * a) * (h_prev[:, sl] - (ii * uvg) * (a * rq_s[:, sl]))
            dzr = (dla * cl[:, sl]) * r * (1.0 - r)
            dzi = (ls * uvg) * ii * (1.0 - ii)
            duv_ref[:, sl] = ls * ii + _dot_nt(dzr, wa_ref[g]) + _dot_nt(dzi, wx_ref[g])
            gwa_ref[g] += _dot_tn(uvg, dzr)
            gwx_ref[g] += _dot_tn(uvg, dzi)
            gv_ref[0:1, sl] += _rowsum(dzr)
            gv_ref[1:2, sl] += _rowsum(dzi)
            gv_ref[2:3, sl] += _rowsum(dla * r) * dsig[:, sl]

    tile, prev, nxt = _lru_specs(s, tm, cb, pos, nt)
    vec, dvec, wmat = _lru_param_specs(cb, d)
    hh_spec = prev if d == 0 else nxt
    gw_spec = pl.BlockSpec((n_g, LRU_BLOCK, LRU_BLOCK), lambda c, i: (c, 0, 0))
    n_blk = D_INNER // LRU_BLOCK
    return _call_with_sides(
        body, sides, name=name,
        out_shape=[jax.ShapeDtypeStruct((s, D_INNER), F32),
                   jax.ShapeDtypeStruct((n_blk, LRU_BLOCK, LRU_BLOCK), F32),
                   jax.ShapeDtypeStruct((n_blk, LRU_BLOCK, LRU_BLOCK), F32),
                   jax.ShapeDtypeStruct((SUBLANES, D_INNER), F32),
                   jax.ShapeDtypeStruct((1, D_INNER), F32)],
        grid=(D_INNER // cb, nt),
        in_specs=[tile, tile, tile, hh_spec, wmat, wmat, dvec, dvec, dvec, vec, vec],
        out_specs=[tile, gw_spec, gw_spec, pl.BlockSpec((SUBLANES, cb), lambda c, i: (0, c)), vec],
        scratch_shapes=[pltpu.VMEM((n_g, _scan_rows(seg), LANES), F32)] * 2 + [pltpu.VMEM((tm, cb), F32)] * 6,
        compiler_params=_cparams(dimension_semantics=("arbitrary", "arbitrary")),
        args=[uv, dh, h, h, p["wa"], p["wx"], p["ba"], p["bx"], p["lam"], h0, lam_in])


def _out0(hf, hb, g, xt, gt, wo, lg, lb, name):
    t = xt.shape[0]
    tm = min(TM_MM, t)

    def body(hf_ref, hb_ref, g_ref, x_ref, gt_ref, w_ref, lg_ref, lb_ref, x1_ref, br_ref):
        br = None
        for k in range(D_INNER // WBLK):
            sl = slice(k * WBLK, (k + 1) * WBLK)
            gg = g_ref[:, sl]
            p = (hf_ref[:, sl] + hb_ref[:, sl]) * (gg * _sigmoid(gg))
            part = _dot(p, w_ref[sl, :])
            br = part if br is None else br + part
        z = ALPHA * x_ref[...] + gt_ref[...] * br
        xhat, _ = _layer_norm_stats(z)
        x1_ref[...] = xhat * lg_ref[...] + lb_ref[...]
        br_ref[...] = br

    wide = pl.BlockSpec((tm, D_INNER), lambda i: (i, 0))
    nar = pl.BlockSpec((tm, D_MODEL), lambda i: (i, 0))
    row = pl.BlockSpec((1, D_MODEL), lambda i: (0, 0))
    return pl.pallas_call(
        body, name=name, out_shape=[jax.ShapeDtypeStruct((t, D_MODEL), F32)] * 2, grid=(t // tm,),
        in_specs=[wide, wide, wide, nar, row,
                  pl.BlockSpec((D_INNER, D_MODEL), lambda i: (0, 0), pipeline_mode=pl.Buffered(1)), row, row],
        out_specs=[nar, nar],
        compiler_params=_cparams(dimension_semantics=("arbitrary",)),
    )(hf, hb, g, xt, gt, wo, lg, lb)


def _unrolled_loop(n, fn, unroll=4):
    while n % unroll:
        unroll //= 2

    def trip(k, carry):
        for q in range(unroll):
            fn(k * unroll + q)
        return carry
    lax.fori_loop(0, n // unroll, trip, 0)


def _window(n, w):
    t = np.arange(n)
    return np.clip(t - w // 2, 0, n), np.clip(t + w // 2, 0, n)


def _pool_tables(n_rows, transpose):
    boxes, inv_c, inv_r = [], [], []
    for w in POOL_WINDOWS:
        lo, hi = _window(GRID_W, w)
        m = np.zeros((GRID_W, GRID_W), np.float32)
        for r in range(GRID_W):
            m[r, lo[r]:hi[r]] = 1.0
        m = np.kron(np.eye(POOL_TOK // GRID_W, dtype=np.float32), m)
        boxes.append(m.T if transpose else m)
        inv_c.append(np.broadcast_to((1.0 / (hi - lo).astype(np.float32))[:, None], (GRID_W, LANES)))
        lo_r, hi_r = _window(n_rows, w)
        inv_r.append(1.0 / (hi_r - lo_r).astype(np.float32))
    return (jnp.asarray(np.stack(boxes), MXU_DTYPE), jnp.asarray(np.stack(inv_c), F32),
            jnp.asarray(np.stack(inv_r), F32))


def _pool_mix(xin, transpose, out_dtype, name):
    s = xin.shape[0]
    n_rows = s // GRID_W
    pad_t = SUBLANES * GRID_W
    rows_per_blk = POOL_TOK // GRID_W
    n_slab = D_INNER // LANES
    slabs_per_group = POOL_GROUP // LANES
    n_win = len(POOL_WINDOWS)
    boxes, inv_c, inv_r = _pool_tables(n_rows, transpose)

    def body(invr_ref, box_ref, invc_ref, x_ref, o_ref, pad_s):
        k = pl.program_id(0) // slabs_per_group
        pad_s[pl.ds(0, pad_t), :] = jnp.zeros((pad_t, LANES), F32)
        pad_s[pl.ds(pad_t + s, pad_t), :] = jnp.zeros((pad_t, LANES), F32)

        for kk, w in enumerate(POOL_WINDOWS):
            half = w // 2
            offsets = list(range(-(half - 1), half + 1)) if transpose else list(range(-half, half))

            @pl.when(k == kk)
            def _():
                inv_col = invc_ref[kk]

                def col_box(b):
                    st = pl.multiple_of(b * POOL_TOK, POOL_TOK)
                    xb = x_ref[pl.ds(st, POOL_TOK), :]
                    if transpose:
                        xb = xb * jnp.concatenate(
                            [inv_col * invr_ref[kk, b * rows_per_blk + q] for q in range(rows_per_blk)], axis=0)
                    hi = xb.astype(MXU_DTYPE)
                    lo = (xb - hi.astype(F32)).astype(MXU_DTYPE)
                    both = jnp.dot(box_ref[kk], jnp.concatenate([hi, lo], axis=1), preferred_element_type=F32)
                    pad_s[pl.ds(pad_t + st, POOL_TOK), :] = both[:, :LANES] + both[:, LANES:]
                _unrolled_loop(s // POOL_TOK, col_box)

                def row_box(r):
                    st = pl.multiple_of(r * GRID_W, GRID_W)
                    acc = pad_s[pl.ds(pad_t + st + offsets[0] * GRID_W, GRID_W), :]
                    for o in offsets[1:]:
                        acc = acc + pad_s[pl.ds(pad_t + st + o * GRID_W, GRID_W), :]
                    if not transpose:
                        acc = acc * (inv_col * invr_ref[kk, r])
                    o_ref[pl.ds(st, GRID_W), :] = (acc - x_ref[pl.ds(st, GRID_W), :]).astype(out_dtype)
                _unrolled_loop(n_rows, row_box)

    slab = pl.BlockSpec((s, LANES), lambda i: (0, i))
    return pl.pallas_call(
        body, name=name, out_shape=jax.ShapeDtypeStruct((s, D_INNER), out_dtype), grid=(n_slab,),
        in_specs=[pl.BlockSpec(memory_space=pltpu.SMEM),
                  pl.BlockSpec((n_win, POOL_TOK, POOL_TOK), lambda i: (0, 0, 0)),
                  pl.BlockSpec((n_win, GRID_W, LANES), lambda i: (0, 0, 0)), slab],
        out_specs=slab,
        scratch_shapes=[pltpu.VMEM((s + 2 * pad_t, LANES), F32)],
        compiler_params=_cparams(dimension_semantics=("arbitrary",)),
    )(inv_r, boxes, inv_c, xin)


def _out1(dmix, pw, ps, g, x1, gt, wo, lg, lb, tgt, name):
    t = x1.shape[0]
    tm = min(TM_MM, t)
    n_grp = len(POOL_WINDOWS)

    def body(d_ref, pw_ref, ps_ref, g_ref, x1_ref, gt_ref, w_ref, lg_ref, lb_ref, tgt_ref, dz_ref, st_ref):
        @pl.when(pl.program_id(0) == 0)
        def _():
            st_ref[...] = jnp.zeros_like(st_ref)

        br = jnp.zeros((tm, D_MODEL), F32)
        for k in range(n_grp):
            sl = slice(k * POOL_GROUP, (k + 1) * POOL_GROUP)
            y = jnp.dot(d_ref[:, sl], pw_ref[k], preferred_element_type=F32) * ps_ref[:, sl]
            gg = g_ref[:, sl]
            br = br + _dot(y * (gg * _sigmoid(gg)), w_ref[sl, :])
        z = ALPHA * x1_ref[...] + gt_ref[...] * br
        xhat, rstd = _layer_norm_stats(z)
        lg_v = lg_ref[...]
        err = xhat * lg_v + lb_ref[...] - tgt_ref[...]
        dy = err * (1.0 / D_MODEL)
        dz = _layer_norm_bwd(dy, xhat, rstd, lg_v)
        dz_ref[...] = dz
        st_ref[0:1, :] += _rowsum(dy * xhat)
        st_ref[1:2, :] += _rowsum(dy)
        st_ref[2:3, :] += _rowsum(dz * br)
        st_ref[3:4, :] += _rowsum(err * err)

    wide = pl.BlockSpec((tm, D_INNER), lambda i: (i, 0))
    nar = pl.BlockSpec((tm, D_MODEL), lambda i: (i, 0))
    row = pl.BlockSpec((1, D_MODEL), lambda i: (0, 0))
    return pl.pallas_call(
        body, name=name,
        out_shape=[jax.ShapeDtypeStruct((t, D_MODEL), F32), jax.ShapeDtypeStruct((SUBLANES, D_MODEL), F32)],
        grid=(t // tm,),
        in_specs=[wide, pl.BlockSpec((n_grp, POOL_GROUP, POOL_GROUP), lambda i: (0, 0, 0)),
                  pl.BlockSpec((1, D_INNER), lambda i: (0, 0)), wide, nar, row,
                  pl.BlockSpec((D_INNER, D_MODEL), lambda i: (0, 0), pipeline_mode=pl.Buffered(1)), row, row, nar],
        out_specs=[nar, pl.BlockSpec((SUBLANES, D_MODEL), lambda i: (0, 0))],
        compiler_params=_cparams(dimension_semantics=("arbitrary",)),
    )(dmix, pw, ps, g, x1, gt, wo, lg, lb, tgt)


def _flush(acc, out_hbm, sem):
    cp = pltpu.make_async_copy(acc, out_hbm, sem)
    cp.start()
    cp.wait()


def _bout1(dz, dmix, g, pw, ps, gt, wo, name):
    t = dz.shape[0]
    tm = min(TM_MM, t)
    nt = t // tm
    n_grp = len(POOL_WINDOWS)

    def body(dz_ref, d_ref, g_ref, pw_ref, ps_ref, gt_ref, w_ref, dd_ref, dg_ref, gwo_hbm, gpw_hbm, gps_ref,
             gwo_acc, gpw_acc, sems):
        i = pl.program_id(0)

        @pl.when(i == 0)
        def _():
            gwo_acc[...] = jnp.zeros_like(gwo_acc)
            gpw_acc[...] = jnp.zeros_like(gpw_acc)
            gps_ref[...] = jnp.zeros_like(gps_ref)

        db = (gt_ref[...] * dz_ref[...]).astype(MXU_DTYPE)
        for k in range(n_grp):
            sl = slice(k * POOL_GROUP, (k + 1) * POOL_GROUP)
            dk = d_ref[:, sl]
            po = jnp.dot(dk, pw_ref[k], preferred_element_type=F32)
            psk = ps_ref[:, sl]
            y = po * psk
            gg = g_ref[:, sl]
            sg = _sigmoid(gg)
            silu = gg * sg
            gwo_acc[sl, :] += _dot_tn(y * silu, db)
            dp = _dot_nt(db, w_ref[sl, :])
            dy = dp * silu
            dg_ref[:, sl] = (dp * y * (sg * (1.0 + gg * (1.0 - sg)))).astype(MXU_DTYPE)
            gps_ref[0:1, sl] += _rowsum(dy * po)
            dpo = (dy * psk).astype(MXU_DTYPE)
            gpw_acc[k] += _dot_tn(dk, dpo)
            dd_ref[:, sl] = _dot_nt(dpo, pw_ref[k])

        @pl.when(i == nt - 1)
        def _():
            _flush(gwo_acc, gwo_hbm, sems.at[0])
            _flush(gpw_acc, gpw_hbm, sems.at[1])

    wide = pl.BlockSpec((tm, D_INNER), lambda i: (i, 0))
    nar = pl.BlockSpec((tm, D_MODEL), lambda i: (i, 0))
    return pl.pallas_call(
        body, name=name,
        out_shape=[jax.ShapeDtypeStruct((t, D_INNER), F32), jax.ShapeDtypeStruct((t, D_INNER), MXU_DTYPE),
                   jax.ShapeDtypeStruct((D_INNER, D_MODEL), F32),
                   jax.ShapeDtypeStruct((n_grp, POOL_GROUP, POOL_GROUP), F32),
                   jax.ShapeDtypeStruct((SUBLANES, D_INNER), F32)],
        grid=(nt,),
        in_specs=[nar, wide, wide,
                  pl.BlockSpec((n_grp, POOL_GROUP, POOL_GROUP), lambda i: (0, 0, 0), pipeline_mode=pl.Buffered(1)),
                  pl.BlockSpec((1, D_INNER), lambda i: (0, 0)), pl.BlockSpec((1, D_MODEL), lambda i: (0, 0)),
                  pl.BlockSpec((D_INNER, D_MODEL), lambda i: (0, 0), pipeline_mode=pl.Buffered(1))],
        out_specs=[wide, wide, ANY, ANY, pl.BlockSpec((SUBLANES, D_INNER), lambda i: (0, 0))],
        scratch_shapes=[pltpu.VMEM((D_INNER, D_MODEL), F32), pltpu.VMEM((n_grp, POOL_GROUP, POOL_GROUP), F32),
                        pltpu.SemaphoreType.DMA((2,))],
        compiler_params=_cparams(dimension_semantics=("arbitrary",)),
    )(dz, dmix, g, pw, ps, gt, wo)


def _bout0(dx1, xt, br0, lg, hf, hb, g, gt, wo, name, sides=()):
    t = dx1.shape[0]
    tm = min(TM_BWD, t)
    nt = t // tm

    def body(dx_ref, x_ref, br_ref, lg_ref, hf_ref, hb_ref, g_ref, gt_ref, w_ref,
             dz_ref, dy_ref, dg_ref, gwo_hbm, st_ref, gwo_acc, sem):
        i = pl.program_id(0)

        @pl.when(i == 0)
        def _():
            gwo_acc[...] = jnp.zeros_like(gwo_acc)
            st_ref[...] = jnp.zeros_like(st_ref)

        dx = dx_ref[...]
        br = br_ref[...]
        gate = gt_ref[...]
        xhat, rstd = _layer_norm_stats(ALPHA * x_ref[...] + gate * br)
        dz = _layer_norm_bwd(dx, xhat, rstd, lg_ref[...])
        dz_ref[...] = dz
        st_ref[0:1, :] += _rowsum(dx * xhat)
        st_ref[1:2, :] += _rowsum(dx)
        st_ref[2:3, :] += _rowsum(dz * br)
        db = (gate * dz).astype(MXU_DTYPE)
        for k in range(D_INNER // WBLK):
            sl = slice(k * WBLK, (k + 1) * WBLK)
            y = hf_ref[:, sl] + hb_ref[:, sl]
            gg = g_ref[:, sl]
            sg = _sigmoid(gg)
            silu = gg * sg
            gwo_acc[sl, :] += _dot_tn(y * silu, db)
            dp = _dot_nt(db, w_ref[sl, :])
            dy_ref[:, sl] = dp * silu
            dg_ref[:, sl] = (dp * y * (sg * (1.0 + gg * (1.0 - sg)))).astype(MXU_DTYPE)

        @pl.when(i == nt - 1)
        def _():
            _flush(gwo_acc, gwo_hbm, sem)

    wide = pl.BlockSpec((tm, D_INNER), lambda i: (i, 0))
    nar = pl.BlockSpec((tm, D_MODEL), lambda i: (i, 0))
    row = pl.BlockSpec((1, D_MODEL), lambda i: (0, 0))
    return _call_with_sides(
        body, sides, name=name,
        out_shape=[jax.ShapeDtypeStruct((t, D_MODEL), F32), jax.ShapeDtypeStruct((t, D_INNER), F32),
                   jax.ShapeDtypeStruct((t, D_INNER), MXU_DTYPE), jax.ShapeDtypeStruct((D_INNER, D_MODEL), F32),
                   jax.ShapeDtypeStruct((SUBLANES, D_MODEL), F32)],
        grid=(nt,),
        in_specs=[nar, nar, nar, row, wide, wide, wide, row,
                  pl.BlockSpec((D_INNER, D_MODEL), lambda i: (0, 0), pipeline_mode=pl.Buffered(1))],
        out_specs=[nar, wide, wide, ANY, pl.BlockSpec((SUBLANES, D_MODEL), lambda i: (0, 0))],
        scratch_shapes=[pltpu.VMEM((D_INNER, D_MODEL), F32), pltpu.SemaphoreType.DMA(())],
        compiler_params=_cparams(dimension_semantics=("arbitrary",)),
        args=[dx1, xt, br0, lg, hf, hb, g, gt, wo])


def _conv_bwd(duvf, duvb, u, conv_w, name):
    s = u.shape[0]
    tm = min(TM_LRU_FWD, s)
    cb = CB_LRU
    nt = s // tm

    def body(df_ref, dfp_ref, dfn_ref, db_ref, dbp_ref, dbn_ref, u_ref, cw_ref, du_ref, cst_ref):
        i = pl.program_id(1)

        @pl.when(i == 0)
        def _():
            cst_ref[...] = jnp.zeros_like(cst_ref)

        first, last = i == 0, i == nt - 1
        pz = jnp.where(first, 0.0, 1.0)
        nz = jnp.where(last, 0.0, 1.0)
        dout = df_ref[...] + db_ref[...]
        dm1, dp1, dp2 = _shifted(dout, (dfp_ref[...] + dbp_ref[...]) * pz, (dfn_ref[...] + dbn_ref[...]) * nz,
                                 [-1, 1, 2])
        cw = cw_ref[...]
        du_ref[...] = (dp2 * cw[0:1] + dp1 * cw[1:2] + dout * cw[2:3] + dm1 * cw[3:4]).astype(MXU_DTYPE)
        u_t = u_ref[...]
        cst_ref[0:1, :] += _rowsum(dp2 * u_t)
        cst_ref[1:2, :] += _rowsum(dp1 * u_t)
        cst_ref[2:3, :] += _rowsum(dout * u_t)
        cst_ref[3:4, :] += _rowsum(dm1 * u_t)
        cst_ref[4:5, :] += _rowsum(dout)

    tile, prev, nxt = _lru_specs(s, tm, cb, lambda i: i, nt)
    return pl.pallas_call(
        body, name=name,
        out_shape=[jax.ShapeDtypeStruct((s, D_INNER), MXU_DTYPE), jax.ShapeDtypeStruct((SUBLANES, D_INNER), F32)],
        grid=(D_INNER // cb, nt),
        in_specs=[tile, prev, nxt] * 2 + [tile, pl.BlockSpec((4, cb), lambda c, i: (0, c))],
        out_specs=[tile, pl.BlockSpec((SUBLANES, cb), lambda c, i: (0, c))],
        compiler_params=_cparams(dimension_semantics=("arbitrary", "arbitrary")),
    )(duvf, duvf, duvf, duvb, duvb, duvb, u, conv_w)


def _bin(du, dg, xin, dzin, sc, sh, wg, name, gw_init=None):
    t = xin.shape[0]
    tm = min(TM_MM, t)
    nt = t // tm
    has_g, has_dx, has_init = dg is not None, dzin is not None, gw_init is not None
    half = N_WBLK // 2
    n_blk = N_WBLK if has_g else half

    def body(*refs):
        refs = list(refs)
        du_ref = refs.pop(0)
        dg_ref = refs.pop(0) if has_g else None
        x_ref = refs.pop(0)
        dz_ref = refs.pop(0) if has_dx else None
        sc_ref, sh_ref, w_ref = refs.pop(0), refs.pop(0), refs.pop(0)
        init_hbm = refs.pop(0) if has_init else None
        dx_ref = refs.pop(0) if has_dx else None
        gw_hbm, st_ref, gw_acc, sem = refs
        i = pl.program_id(0)

        @pl.when(i == 0)
        def _():
            st_ref[...] = jnp.zeros_like(st_ref)
            first_zero = 0
            if has_init:
                _flush(init_hbm, gw_acc.at[pl.ds(0, half)], sem)
                first_zero = half
            for k in range(first_zero, n_blk):
                gw_acc[k] = jnp.zeros((D_MODEL, WBLK), F32)

        xv = x_ref[...]
        scale = 1.0 + sc_ref[...]
        h = (xv * scale + sh_ref[...]).astype(MXU_DTYPE)
        dh = None
        for k in range(n_blk):
            src = du_ref if k < half else dg_ref
            kk = k % half
            dk = src[:, kk * WBLK:(kk + 1) * WBLK]
            gw_acc[k] += _dot_tn(h, dk)
            contrib = _dot_nt(dk, w_ref[k])
            dh = contrib if dh is None else dh + contrib
        st_ref[0:1, :] += _rowsum(dh * xv)
        st_ref[1:2, :] += _rowsum(dh)
        if has_dx:
            dx_ref[...] = ALPHA * dz_ref[...] + dh * scale

        @pl.when(i == nt - 1)
        def _():
            _flush(gw_acc, gw_hbm, sem)

    wide = pl.BlockSpec((tm, D_INNER), lambda i: (i, 0))
    nar = pl.BlockSpec((tm, D_MODEL), lambda i: (i, 0))
    row = pl.BlockSpec((1, D_MODEL), lambda i: (0, 0))
    wspec = pl.BlockSpec((n_blk, D_MODEL, WBLK), lambda i: (0, 0, 0), pipeline_mode=pl.Buffered(1))
    in_specs = ([wide] + ([wide] if has_g else []) + [nar] + ([nar] if has_dx else []) + [row, row, wspec]
                + ([ANY] if has_init else []))
    args = ([du] + ([dg] if has_g else []) + [xin] + ([dzin] if has_dx else []) + [sc, sh, wg]
            + ([gw_init] if has_init else []))
    out_shape = ([jax.ShapeDtypeStruct((t, D_MODEL), F32)] if has_dx else []) + [
        jax.ShapeDtypeStruct((n_blk, D_MODEL, WBLK), F32), jax.ShapeDtypeStruct((SUBLANES, D_MODEL), F32)]
    out_specs = ([nar] if has_dx else []) + [ANY, pl.BlockSpec((SUBLANES, D_MODEL), lambda i: (0, 0))]
    return pl.pallas_call(
        body, name=name, out_shape=out_shape, grid=(nt,), in_specs=in_specs, out_specs=out_specs,
        scratch_shapes=[pltpu.VMEM((n_blk, D_MODEL, WBLK), F32), pltpu.SemaphoreType.DMA(())],
        compiler_params=_cparams(dimension_semantics=("arbitrary",)),
    )(*args)


def _blocks_by_device(a, axis):
    shape = a.shape
    a = a.reshape(shape[:axis] + (N_DEV, shape[axis] // N_DEV) + shape[axis + 1:])
    return jnp.moveaxis(a, axis, 0)


def kernel(x, c, ctx, c_ctx, w_mod, b_mod, w_in, w_out, ln_g, ln_b, conv_w, conv_b, lru_wa, lru_ba, lru_wx, lru_bx, lru_lam, pool_w, pool_scale, loss_target, m_c_ctx, m_w_mod, m_b_mod, m_w_in, m_w_out, m_ln_g, m_ln_b, m_conv_w, m_conv_b, m_lru_wa, m_lru_ba, m_lru_wx, m_lru_bx, m_lru_lam, m_pool_w, m_pool_scale, v_c_ctx, v_w_mod, v_b_mod, v_w_in, v_w_out, v_ln_g, v_ln_b, v_conv_w, v_conv_b, v_lru_wa, v_lru_ba, v_lru_wx, v_lru_bx, v_lru_lam, v_pool_w, v_pool_scale):
    xi, yi, ci = _my_pos()
    dev = 4 * xi + 2 * yi + ci
    xt, ctxt, tgt = x[0], ctx[0], loss_target[0]
    n_mod = w_mod.shape[2]

    small_shapes = [(D_MODEL,), conv_w.shape[1:], lru_ba.shape[1:], lru_bx.shape[1:], lru_lam.shape[1:],
                    pool_scale.shape[1:]]
    small = _to_rows([c[0], conv_w[0], lru_ba[0], lru_bx[0], lru_lam[0], pool_scale[0]], SUBLANES)
    small_all, = _all_gather([small], "gather_small")
    pieces = [_split_rows(small_all[k], small_shapes) for k in range(N_DEV)]
    c_all = jnp.stack([p[0] for p in pieces])
    conv_w_f = jnp.concatenate([p[1] for p in pieces], axis=-1)
    lru_ba_f = jnp.concatenate([p[2] for p in pieces], axis=-1)[:, None, :]
    lru_bx_f = jnp.concatenate([p[3] for p in pieces], axis=-1)[:, None, :]
    lru_lam_f = jnp.concatenate([p[4] for p in pieces], axis=-1)[:, None, :]
    pool_scale_f = jnp.concatenate([p[5] for p in pieces], axis=-1)[None, :]

    cond = jnp.concatenate([c_all, jnp.broadcast_to(c_ctx[None, :], (N_DEV, D_MODEL))], axis=0)
    b_my = lax.dynamic_slice(b_mod, (0, dev * n_mod), (2, n_mod))[:, None, :]
    mod_part = _mod_fwd(cond, w_mod, b_my, "mod_fwd")
    mod_all, = _all_gather([mod_part], "gather_mod")
    mod = jnp.transpose(mod_all, (1, 2, 0, 3)).reshape(2, 16, 3 * D_MODEL)
    mod_me = lax.dynamic_slice(mod, (0, dev, 0), (2, 1, 3 * D_MODEL))
    sh = [mod_me[i, :, 0:D_MODEL] for i in range(2)]
    sc = [mod_me[i, :, D_MODEL:2 * D_MODEL] for i in range(2)]
    gt = [mod_me[i, :, 2 * D_MODEL:] for i in range(2)]
    shc, scc = mod[0, 8:9, 0:D_MODEL], mod[0, 8:9, D_MODEL:2 * D_MODEL]

    wi0, = _all_gather([w_in[0].astype(MXU_DTYPE)], "gather_weights0")
    lg = [ln_g[i][None, :] for i in range(2)]
    lb = [ln_b[i][None, :] for i in range(2)]
    lru_p = dict(conv_w=conv_w_f, conv_b=conv_b, wa=lru_wa[0].astype(MXU_DTYPE), wx=lru_wx[0].astype(MXU_DTYPE),
                 ba=lru_ba_f, bx=lru_bx_f, lam=lru_lam_f)
    zero_state = jnp.zeros((1, D_INNER), F32)

    (u0, g0), (wo0,) = _in_proj(xt, sc[0], sh[0], wi0, "in_proj0", sides=[("gather", [w_out[0].astype(MXU_DTYPE)])])
    (uc, _), _ = _in_proj(ctxt, scc, shc, wi0, "in_proj0_ctx")
    (hcf, cf, uvc), _ = _lru_fwd(uc, zero_state, lru_p, 0, "lru_fwd_ctx_f", conv=True)
    (hcb, cbk), _ = _lru_fwd(uvc, zero_state, lru_p, 1, "lru_fwd_ctx_b", conv=False)
    (hf, _, uv0), (wi1,) = _lru_fwd(u0, cf, lru_p, 0, "lru_fwd_f", conv=True,
                                    sides=[("gather", [w_in[1].astype(MXU_DTYPE)])])
    (hb, _), (wo1, pool_w_g) = _lru_fwd(
        uv0, cbk, lru_p, 1, "lru_fwd_b", conv=False,
        sides=[("gather", [w_out[1].astype(MXU_DTYPE), pool_w[0].astype(MXU_DTYPE)])])
    w_in_l = [wi0, wi1]
    w_out_l = [wo0.reshape(D_INNER, D_MODEL), wo1.reshape(D_INNER, D_MODEL)]
    pool_w_f = jnp.transpose(pool_w_g, (1, 0, 2, 3)).reshape(len(POOL_WINDOWS), POOL_GROUP, POOL_GROUP)
    x1, br0 = _out0(hf, hb, g0, xt, gt[0], w_out_l[0], lg[0], lb[0], "out0")
    (u1, g1), _ = _in_proj(x1, sc[1], sh[1], w_in_l[1], "in_proj1")
    dmix = _pool_mix(u1, False, MXU_DTYPE, "pool_fwd")
    dz1, st1 = _out1(dmix, pool_w_f, pool_scale_f, g1, x1, gt[1], w_out_l[1], lg[1], lb[1], tgt, "out1")
    loss_me = jnp.full((1, LANES), (0.5 / D_MODEL) * jnp.sum(st1[3]), F32)

    core = jnp.reshape(ci, (1,)).astype(jnp.int32)
    wo_view = lambda a: a.reshape(N_DEV, D_INNER // N_DEV, D_MODEL)
    pw_view = lambda a: _blocks_by_device(a, 1).reshape(N_DEV, POOL_GROUP // N_DEV * len(POOL_WINDOWS), POOL_GROUP)
    dd, dg1, gwo1, gpw, gps = _bout1(dz1, dmix, g1, pool_w_f, pool_scale_f, gt[1], w_out_l[1], "bwd_out1")
    du1 = _pool_mix(dd, True, MXU_DTYPE, "pool_bwd")
    dx1, gwi1, stb1 = _bin(du1, dg1, x1, dz1, sc[1], sh[1], w_in_l[1], "bwd_in1")
    bufs1 = [gwi1, wo_view(gwo1), pw_view(gpw)]
    (dz0, dy0, dg0, gwo0, stl0), recv1 = _bout0(dx1, xt, br0, lg[0], hf, hb, g0, gt[0], w_out_l[0], "bwd_out0",
                                                sides=[("sibling", bufs1)])
    pairs1 = [_pair_sum(b, r, core, "reduce_pair_" + n)
              for b, r, n in zip(bufs1, recv1, ["w_in1", "w_out1", "pool_w"])]
    (duvf, gwa_f, gwx_f, gv_f, dh0f), (p_wi1, p_wo1, p_pw, recv_wo0) = _lru_bwd(
        uv0, dy0, hf, cf, zero_state, lru_p, 0, "lru_bwd_f", sides=[("chips", pairs1), ("sibling", [wo_view(gwo0)])])
    pair_wo0 = _pair_sum(wo_view(gwo0), recv_wo0, core, "reduce_pair_w_out0")
    (duvb, gwa_b, gwx_b, gv_b, dh0b), (p_wo0,) = _lru_bwd(
        uv0, dy0, hb, cbk, zero_state, lru_p, 1, "lru_bwd_b", sides=[("chips", [pair_wo0])])
    zero_dh = jnp.zeros_like(uc)
    (ducf, gwa_cf, gwx_cf, gv_cf, _), _ = _lru_bwd(uvc, zero_dh, hcf, zero_state, dh0f, lru_p, 0, "lru_bwd_ctx_f")
    (ducb, gwa_cb, gwx_cb, gv_cb, _), _ = _lru_bwd(uvc, zero_dh, hcb, zero_state, dh0b, lru_p, 1, "lru_bwd_ctx_b")
    du0, cst0 = _conv_bwd(duvf, duvb, u0, conv_w_f, "conv_bwd")
    duc, cstc = _conv_bwd(ducf, ducb, uc, conv_w_f, "conv_bwd_ctx")
    gwic, stc = _bin(duc, None, ctxt, None, scc, shc, w_in_l[0][:N_WBLK // 2], "bwd_in0_ctx")
    gx, gwi0, stb0 = _bin(du0, dg0, xt, dz0, sc[0], sh[0], w_in_l[0], "bwd_in0", gw_init=gwic)

    zero_row = jnp.zeros((1, D_MODEL), F32)
    dm_me = jnp.stack([
        jnp.concatenate([jnp.concatenate([stb0[1:2], stb0[0:1], stl0[2:3]], axis=1),
                         jnp.concatenate([stc[1:2], stc[0:1], zero_row], axis=1)], axis=0),
        jnp.concatenate([jnp.concatenate([stb1[1:2], stb1[0:1], st1[2:3]], axis=1),
                         jnp.zeros((1, 3 * D_MODEL), F32)], axis=0)])
    dm_g, loss_g = _all_gather([dm_me, loss_me], "gather_dmod")
    loss = jnp.sum(loss_g[:, 0, 0])
    dm_all = jnp.concatenate([jnp.transpose(dm_g[:, :, 0], (1, 0, 2)), jnp.transpose(dm_g[:, :, 1], (1, 0, 2))],
                             axis=1)
    dm_my = lax.dynamic_slice(dm_all, (0, 0, dev * n_mod), (2, 16, n_mod))
    g_w_mod, g_b_mod, gcc_part = _mod_bwd(cond, dm_all, dm_my, w_mod, "mod_bwd")
    g_b_mod = g_b_mod.reshape(b_mod.shape)

    gwa = jnp.stack([gwa_f + gwa_cf, gwa_b + gwa_cb])
    gwx = jnp.stack([gwx_f + gwx_cf, gwx_b + gwx_cb])
    gv = jnp.stack([gv_f + gv_cf, gv_b + gv_cb])
    cst = cst0 + cstc
    g_ln_g = jnp.stack([stl0[0], st1[0]])
    g_ln_b = jnp.stack([stl0[1], st1[1]])
    sharded = [
        _blocks_by_device(cst[0:4], 1),
        _blocks_by_device(gv[:, 0], 1), _blocks_by_device(gv[:, 1], 1), _blocks_by_device(gv[:, 2], 1),
        _blocks_by_device(gps[0], 0),
    ]
    replicated = [gwa.reshape(-1), gwx.reshape(-1), g_ln_g.reshape(-1), g_ln_b.reshape(-1), cst[4],
                  gcc_part.reshape(-1)]
    sh_sizes = [int(np.prod(a.shape[1:])) for a in sharded]
    rep_sizes = [a.shape[0] // N_DEV for a in replicated]
    n_flat = sum(sh_sizes) + sum(rep_sizes)
    rows = -(-n_flat // LANES)
    rows = -(-rows // FLAT_ROWS) * FLAT_ROWS
    misc = jnp.concatenate([a.reshape(N_DEV, -1) for a in sharded] +
                           [a.reshape(N_DEV, -1) for a in replicated], axis=1)
    misc = jnp.pad(misc, ((0, 0), (0, rows * LANES - n_flat))).reshape(N_DEV, rows, LANES)
    bufs = [gwi0, misc]
    recvs = _sibling_exchange(bufs, "reduce_sibling")
    pairs = [_pair_sum(b, r, core, "reduce_pair_" + n) for b, r, n in zip(bufs, recvs, ["w_in0", "misc"])]
    p_wi0, p_misc = _chip_exchange(pairs, "reduce_chips")
    g_flat = _sum4(p_misc, "reduce_sum_misc").reshape(-1)

    offs = np.cumsum([0] + sh_sizes + rep_sizes)
    n_sh = len(sh_sizes)
    sh_shapes = [conv_w.shape, lru_ba.shape, lru_bx.shape, lru_lam.shape, pool_scale.shape]
    g_sh = [g_flat[offs[k]:offs[k + 1]].reshape(sh_shapes[k]) for k in range(n_sh)]
    g_conv_w, g_lru_ba, g_lru_bx, g_lru_lam, g_pool_scale = g_sh
    rep_block = _to_rows([g_flat[offs[n_sh]:offs[-1]]], SUBLANES)
    rep_all, = _all_gather([rep_block], "gather_replicated")
    rep_flat = rep_all.reshape(N_DEV, -1)
    rep_full, off = [], 0
    for n in rep_sizes:
        rep_full.append(rep_flat[:, off:off + n].reshape(-1))
        off += n
    g_lru_wa = rep_full[0].reshape(lru_wa.shape)
    g_lru_wx = rep_full[1].reshape(lru_wx.shape)
    g_ln_g = rep_full[2].reshape(ln_g.shape)
    g_ln_b = rep_full[3].reshape(ln_b.shape)
    g_conv_b = rep_full[4].reshape(conv_b.shape)
    g_c_ctx = rep_full[5].reshape(c_ctx.shape)

    names = ["c_ctx", "w_mod", "b_mod", "w_in", "w_out", "ln_g", "ln_b", "conv_w", "conv_b", "lru_wa", "lru_ba",
             "lru_wx", "lru_bx", "lru_lam", "pool_w", "pool_scale"]
    weights = dict(c_ctx=c_ctx, w_mod=w_mod, b_mod=b_mod, w_in=w_in, w_out=w_out, ln_g=ln_g, ln_b=ln_b,
                   conv_w=conv_w, conv_b=conv_b, lru_wa=lru_wa, lru_ba=lru_ba, lru_wx=lru_wx, lru_bx=lru_bx,
                   lru_lam=lru_lam, pool_w=pool_w, pool_scale=pool_scale)
    mom_m = dict(c_ctx=m_c_ctx, w_mod=m_w_mod, b_mod=m_b_mod, w_in=m_w_in, w_out=m_w_out, ln_g=m_ln_g, ln_b=m_ln_b,
                 conv_w=m_conv_w, conv_b=m_conv_b, lru_wa=m_lru_wa, lru_ba=m_lru_ba, lru_wx=m_lru_wx,
                 lru_bx=m_lru_bx, lru_lam=m_lru_lam, pool_w=m_pool_w, pool_scale=m_pool_scale)
    mom_v = dict(c_ctx=v_c_ctx, w_mod=v_w_mod, b_mod=v_b_mod, w_in=v_w_in, w_out=v_w_out, ln_g=v_ln_g, ln_b=v_ln_b,
                 conv_w=v_conv_w, conv_b=v_conv_b, lru_wa=v_lru_wa, lru_ba=v_lru_ba, lru_wx=v_lru_wx,
                 lru_bx=v_lru_bx, lru_lam=v_lru_lam, pool_w=v_pool_w, pool_scale=v_pool_scale)
    grads = dict(c_ctx=g_c_ctx, w_mod=g_w_mod, b_mod=g_b_mod, ln_g=g_ln_g, ln_b=g_ln_b,
                 conv_w=g_conv_w, conv_b=g_conv_b, lru_wa=g_lru_wa, lru_ba=g_lru_ba, lru_wx=g_lru_wx,
                 lru_bx=g_lru_bx, lru_lam=g_lru_lam)
    grads["pool_scale"] = g_pool_scale
    delta, new_m, new_v = {}, {}, {}

    def update_parts(n, parts, view):
        res = _adamw_parts(weights[n].reshape(view), parts, mom_m[n].reshape(view), mom_v[n].reshape(view),
                           "adamw_" + n)
        grads[n], delta[n], new_m[n], new_v[n] = [r.reshape(weights[n].shape) for r in res]

    update_parts("w_in", [p_wi0, p_wi1], w_in.shape)
    update_parts("w_out", [p_wo0, p_wo1], w_out.shape)
    update_parts("pool_w", [p_pw], (1,) + p_pw.shape[1:])
    for n in ("w_mod", "lru_wa", "lru_wx"):
        shape = weights[n].shape
        view = (int(np.prod(shape[:-1])), shape[-1])
        res = _adamw(weights[n].reshape(view), grads[n].reshape(view), mom_m[n].reshape(view),
                     mom_v[n].reshape(view), "adamw_" + n)
        delta[n], new_m[n], new_v[n] = [r.reshape(shape) for r in res]

    small = [n for n in names if n not in delta]
    shapes = [weights[n].shape for n in small]
    flat = lambda d: _to_rows([d[n] for n in small], FLAT_ROWS)
    res = _adamw(flat(weights), flat(grads), flat(mom_m), flat(mom_v), "adamw_small")
    for d, r in zip((delta, new_m, new_v), res):
        d.update(zip(small, _split_rows(r, shapes)))

    return (loss, gx[None], *[grads[n] for n in names], *[delta[n] for n in names],
            *[new_m[n] for n in names], *[new_v[n] for n in names])
```

```python
import functools

import numpy as np
import jax
import jax.numpy as jnp
from jax import lax
from jax.experimental import pallas as pl
from jax.experimental.pallas import tpu as pltpu

F32 = jnp.float32
BF16 = jnp.bfloat16
MXU_DTYPE = BF16

D_MODEL = 1024
D_INNER = 2048
LRU_BLOCK = 128
GRID_W = 64
POOL_WINDOWS = (2, 4, 8, 16)
POOL_GROUP = 512
ALPHA = float(4 ** 0.25)
LN_EPS = 1e-5
LRU_C = 8.0
N_DEV = 8
N_WBLK = 8
WBLK = 512

ADAM_LR = 0.001
ADAM_B1 = 0.9
ADAM_B2 = 0.999
ADAM_EPS = 1e-08
ADAM_WD = 0.01
ADAM_STEP = 10

LANES = 128
SUBLANES = 8
V7X_VMEM_BYTES = 64 * 1024 * 1024
VMEM_LIMIT = V7X_VMEM_BYTES - 8 * 1024 * 1024
MESH = pl.DeviceIdType.MESH
ANY = pl.BlockSpec(memory_space=pl.ANY)

TM_MM = 512
TM_BWD = 256
TM_LRU = 1024
TM_LRU_FWD = 1024
CB_LRU = 512
N_SEG = 8
SCAN_UNROLL = 4
SCAN_ROW_T = 17
SCAN_ROW_J = 2
SQRT_FLOOR = 1e-30
FLAT_ROWS = 16
ELEMENTWISE_TILE_BYTES = 1 << 20
POOL_TOK = 256
WIRE_DTYPE = BF16


def _cparams(**kw):
    return pltpu.CompilerParams(vmem_limit_bytes=VMEM_LIMIT, **kw)


def _my_pos():
    return lax.axis_index("x"), lax.axis_index("y"), lax.axis_index("c")


def _dot(a, b):
    return jnp.dot(a.astype(MXU_DTYPE), b.astype(MXU_DTYPE), preferred_element_type=F32)


def _dot_tn(a, b):
    return lax.dot_general(a.astype(MXU_DTYPE), b.astype(MXU_DTYPE), (((0,), (0,)), ((), ())),
                           preferred_element_type=F32)


def _dot_nt(a, b):
    return lax.dot_general(a.astype(MXU_DTYPE), b.astype(MXU_DTYPE), (((1,), (1,)), ((), ())),
                           preferred_element_type=F32)


def _sigmoid(z):
    return 0.5 * jnp.tanh(0.5 * z) + 0.5


def _log_sigmoid(x):
    y = jnp.exp(-jnp.abs(x))
    u = 1.0 + y
    l1p = jnp.where(u == 1.0, y, jnp.log(u) * (y / jnp.where(u == 1.0, 1.0, u - 1.0)))
    return jnp.minimum(x, 0.0) - l1p


def _rowsum(v):
    return jnp.sum(v, axis=0, keepdims=True)


def _layer_norm_stats(z):
    mu = jnp.mean(z, axis=-1, keepdims=True)
    zc = z - mu
    var = jnp.mean(zc * zc, axis=-1, keepdims=True)
    rstd = lax.rsqrt(var + LN_EPS)
    return zc * rstd, rstd


def _layer_norm_bwd(dy, xhat, rstd, g):
    dxh = dy * g
    m1 = jnp.mean(dxh, axis=-1, keepdims=True)
    m2 = jnp.mean(dxh * xhat, axis=-1, keepdims=True)
    return rstd * (dxh - m1 - xhat * m2)


def _shifted(v, before8, after8, offsets):
    n = v.shape[0]
    ext = jnp.concatenate([before8, v, after8], axis=0)
    total = n + 2 * SUBLANES
    return [pltpu.roll(ext, (-k) % total, 0)[SUBLANES:SUBLANES + n] for k in offsets]


def _rows8(row):
    return jnp.broadcast_to(row, (SUBLANES, row.shape[1]))


def _shift_down(v, first_row):
    return _shifted(v, _rows8(first_row), _rows8(first_row), [-1])[0]


def _shift_up(v, last_row):
    return _shifted(v, _rows8(last_row), _rows8(last_row), [1])[0]


def _all_gather(blocks, name):
    n = len(blocks)

    def body(*refs):
        x_refs, out_refs = refs[:n], refs[n:2 * n]
        send_sems, recv_sems, local_sems = refs[2 * n:]
        x, y, c = _my_pos()
        me, sibling = (x, y, c), (x, y, 1 - c)
        chips = [(1 - x, y), (x, 1 - y), (1 - x, 1 - y)]

        def slot(a, px, py, pc):
            return out_refs[a].at[4 * px + 2 * py + pc]

        def copy(a, k, block, to, src=None):
            return pltpu.make_async_remote_copy(
                src_ref=slot(a, *block) if src is None else src, dst_ref=slot(a, *block),
                send_sem=send_sems.at[a, k], recv_sem=recv_sems.at[a, k], device_id=to, device_id_type=MESH)

        mine = [pltpu.make_async_copy(x_refs[a], slot(a, *me), local_sems.at[a]) for a in range(n)]
        for cp in mine:
            cp.start()
        first = []
        for a in range(n):
            first.append(copy(a, 0, me, sibling, src=x_refs[a]))
            first += [copy(a, 1 + j, me, (*chip, c), src=x_refs[a]) for j, chip in enumerate(chips)]
        for cp in first:
            cp.start()
        passed = []
        for j, chip in enumerate(chips):
            for a in range(n):
                copy(a, 1 + j, (*chip, c), me).wait_recv()
                fwd = copy(a, 4 + j, (*chip, c), sibling)
                fwd.start()
                passed.append(fwd)
        for a in range(n):
            copy(a, 0, sibling, me).wait_recv()
            for j, chip in enumerate(chips):
                copy(a, 4 + j, (*chip, 1 - c), me).wait_recv()
        for cp in first + passed:
            cp.wait_send()
        for cp in mine:
            cp.wait()

    outs = pl.pallas_call(
        body, name=name,
        out_shape=[jax.ShapeDtypeStruct((N_DEV,) + b.shape, b.dtype) for b in blocks],
        in_specs=[ANY] * n, out_specs=[ANY] * n,
        scratch_shapes=[pltpu.SemaphoreType.DMA((n, 7)), pltpu.SemaphoreType.DMA((n, 7)),
                        pltpu.SemaphoreType.DMA((n,))],
    )(*blocks)
    return list(outs)


def _sibling_exchange(bufs, name):
    n = len(bufs)

    def body(*refs):
        srcs, outs = refs[:n], refs[n:2 * n]
        send_sems, recv_sems = refs[2 * n:]
        x, y, c = _my_pos()
        copies = [pltpu.make_async_remote_copy(
            src_ref=srcs[a].at[2 * j + (1 - c)], dst_ref=outs[a].at[j], send_sem=send_sems.at[a, j],
            recv_sem=recv_sems.at[a, j], device_id=(x, y, 1 - c), device_id_type=MESH)
            for a in range(n) for j in range(4)]
        for cp in copies:
            cp.start()
        for cp in copies:
            cp.wait()

    outs = pl.pallas_call(
        body, name=name, out_shape=[jax.ShapeDtypeStruct((4,) + b.shape[1:], b.dtype) for b in bufs],
        in_specs=[ANY] * n, out_specs=[ANY] * n,
        scratch_shapes=[pltpu.SemaphoreType.DMA((n, 4)), pltpu.SemaphoreType.DMA((n, 4))],
    )(*bufs)
    return list(outs)


def _chip_exchange(parts, name):
    n = len(parts)

    def body(*refs):
        srcs, outs = refs[:n], refs[n:2 * n]
        send_sems, recv_sems, local_sems = refs[2 * n:]
        x, y, c = _my_pos()
        jme = 2 * x + y
        peers = [(1 - x, y), (x, 1 - y), (1 - x, 1 - y)]
        local = [pltpu.make_async_copy(srcs[a].at[jme], outs[a].at[jme], local_sems.at[a]) for a in range(n)]
        for cp in local:
            cp.start()

        def copy(a, k, px, py, dst_slot):
            return pltpu.make_async_remote_copy(
                src_ref=srcs[a].at[2 * px + py], dst_ref=outs[a].at[dst_slot], send_sem=send_sems.at[a, k],
                recv_sem=recv_sems.at[a, k], device_id=(px, py, c), device_id_type=MESH)

        sends = [copy(a, k, px, py, jme) for a in range(n) for k, (px, py) in enumerate(peers)]
        for cp in sends:
            cp.start()
        for a in range(n):
            for k, (px, py) in enumerate(peers):
                copy(a, k, px, py, 2 * px + py).wait_recv()
        for cp in sends:
            cp.wait_send()
        for cp in local:
            cp.wait()

    outs = pl.pallas_call(
        body, name=name, out_shape=[jax.ShapeDtypeStruct(p.shape, p.dtype) for p in parts],
        in_specs=[ANY] * n, out_specs=[ANY] * n,
        scratch_shapes=[pltpu.SemaphoreType.DMA((n, 3)), pltpu.SemaphoreType.DMA((n, 3)),
                        pltpu.SemaphoreType.DMA((n,))],
    )(*parts)
    return list(outs)


_SIDE_REMOTE = {"gather": 7, "sibling": 4, "chips": 3}
_FLIPS = [(0, 0, 1), (1, 0, 0), (0, 1, 0), (1, 1, 0), (1, 0, 1), (0, 1, 1), (1, 1, 1)]


def _side_plan(sides):
    inputs, out_shapes, scratch = [], [], []
    for kind, arrays in sides:
        n = len(arrays)
        for a in arrays:
            inputs.append(a)
            shape = {"gather": (N_DEV,) + a.shape, "sibling": (4,) + a.shape[1:], "chips": a.shape}[kind]
            out_shapes.append(jax.ShapeDtypeStruct(shape, a.dtype))
        scratch += [pltpu.SemaphoreType.DMA((n, _SIDE_REMOTE[kind])), pltpu.SemaphoreType.DMA((n, _SIDE_REMOTE[kind])),
                    pltpu.SemaphoreType.DMA((n,))]
    return inputs, out_shapes, scratch


def _side_copies(sides, in_refs, out_refs, sem_refs):
    x, y, c = _my_pos()
    starts, waits = [], []
    pos = 0
    for s, (kind, arrays) in enumerate(sides):
        send_sems, recv_sems, local_sems = sem_refs[3 * s:3 * s + 3]
        for a in range(len(arrays)):
            src, out = in_refs[pos], out_refs[pos]
            pos += 1

            def remote(k, src_ref, dst_ref, to):
                return pltpu.make_async_remote_copy(src_ref=src_ref, dst_ref=dst_ref, send_sem=send_sems.at[a, k],
                                                    recv_sem=recv_sems.at[a, k], device_id=to, device_id_type=MESH)

            def local(src_ref, dst_ref):
                cp = pltpu.make_async_copy(src_ref, dst_ref, local_sems.at[a])
                starts.append(cp.start)
                waits.append(cp.wait)

            if kind == "gather":
                me = 4 * x + 2 * y + c
                local(src, out.at[me])
                for k, (fx, fy, fc) in enumerate(_FLIPS):
                    px, py, pc = (1 - x if fx else x), (1 - y if fy else y), (1 - c if fc else c)
                    send = remote(k, src, out.at[me], (px, py, pc))
                    starts.append(send.start)
                    waits += [remote(k, src, out.at[4 * px + 2 * py + pc], (px, py, pc)).wait_recv, send.wait_send]
            elif kind == "sibling":
                for j in range(4):
                    cp = remote(j, src.at[2 * j + (1 - c)], out.at[j], (x, y, 1 - c))
                    starts.append(cp.start)
                    waits.append(cp.wait)
            else:
                jme = 2 * x + y
                local(src.at[jme], out.at[jme])
                for k, (px, py) in enumerate([(1 - x, y), (x, 1 - y), (1 - x, 1 - y)]):
                    send = remote(k, src.at[2 * px + py], out.at[jme], (px, py, c))
                    starts.append(send.start)
                    waits += [remote(k, src.at[2 * px + py], out.at[2 * px + py], (px, py, c)).wait_recv,
                              send.wait_send]
    return starts, waits


def _call_with_sides(body, sides, *, name, grid, in_specs, out_specs, out_shape, scratch_shapes, compiler_params, args):
    if not sides:
        res = pl.pallas_call(body, name=name, grid=grid, in_specs=in_specs, out_specs=out_specs, out_shape=out_shape,
                             scratch_shapes=scratch_shapes, compiler_params=compiler_params)(*args)
        return list(res), []
    s_in, s_out, s_scr = _side_plan(sides)
    n_in, n_out, n_scr, n_side = len(in_specs), len(out_specs), len(scratch_shapes), len(s_in)

    def wrapped(*refs):
        refs = list(refs)
        ins, side_in = refs[:n_in], refs[n_in:n_in + n_side]
        outs = refs[n_in + n_side:n_in + n_side + n_out]
        side_out = refs[n_in + n_side + n_out:n_in + 2 * n_side + n_out]
        rest = refs[n_in + 2 * n_side + n_out:]
        starts, waits = _side_copies(sides, side_in, side_out, rest[n_scr:])
        first = functools.reduce(jnp.logical_and, [pl.program_id(d) == 0 for d in range(len(grid))])
        last = functools.reduce(jnp.logical_and, [pl.program_id(d) == grid[d] - 1 for d in range(len(grid))])

        @pl.when(first)
        def _():
            for start in starts:
                start()

        body(*ins, *outs, *rest[:n_scr])

        @pl.when(last)
        def _():
            for wait in waits:
                wait()

    res = pl.pallas_call(
        wrapped, name=name, grid=grid, in_specs=list(in_specs) + [ANY] * n_side,
        out_specs=list(out_specs) + [ANY] * n_side, out_shape=list(out_shape) + s_out,
        scratch_shapes=list(scratch_shapes) + s_scr, compiler_params=compiler_params,
    )(*args, *s_in)
    return list(res[:n_out]), list(res[n_out:])


def _row_tile(r, l):
    t = min(r, max(16, ELEMENTWISE_TILE_BYTES // (4 * l) // 16 * 16))
    while r % t:
        t -= 16
    return t


def _pair_sum(buf, recv, core, name):
    _, r, l = buf.shape
    tr = _row_tile(r, l)

    def body(core_ref, a_ref, b_ref, o_ref):
        o_ref[...] = (a_ref[...] + b_ref[...]).astype(WIRE_DTYPE)

    return pl.pallas_call(
        body, name=name, out_shape=jax.ShapeDtypeStruct((4, r, l), WIRE_DTYPE),
        grid_spec=pltpu.PrefetchScalarGridSpec(
            num_scalar_prefetch=1, grid=(4, r // tr),
            in_specs=[pl.BlockSpec((None, tr, l), lambda j, i, cr: (2 * j + cr[0], i, 0)),
                      pl.BlockSpec((None, tr, l), lambda j, i, cr: (j, i, 0))],
            out_specs=pl.BlockSpec((None, tr, l), lambda j, i, cr: (j, i, 0))),
        compiler_params=_cparams(dimension_semantics=("arbitrary", "arbitrary")),
    )(core, buf, recv)


def _sum_parts(p_ref):
    return ((p_ref[0].astype(F32) + p_ref[1].astype(F32)) + (p_ref[2].astype(F32) + p_ref[3].astype(F32)))


def _sum4(parts, name):
    _, r, l = parts.shape
    tr = _row_tile(r, l)

    def body(p_ref, o_ref):
        o_ref[...] = _sum_parts(p_ref)

    return pl.pallas_call(
        body, name=name, out_shape=jax.ShapeDtypeStruct((r, l), F32), grid=(r // tr,),
        in_specs=[pl.BlockSpec((4, tr, l), lambda i: (0, i, 0))],
        out_specs=pl.BlockSpec((tr, l), lambda i: (i, 0)),
        compiler_params=_cparams(dimension_semantics=("arbitrary",)),
    )(parts)


def _adamw_update(w, gg, m, v):
    nm = ADAM_B1 * m + (1.0 - ADAM_B1) * gg
    nv = ADAM_B2 * v + (1.0 - ADAM_B2) * (gg * gg)
    m_hat = nm / (1.0 - ADAM_B1 ** ADAM_STEP)
    v_hat = nv / (1.0 - ADAM_B2 ** ADAM_STEP)
    return -ADAM_LR * (m_hat / (jnp.sqrt(v_hat) + ADAM_EPS) + ADAM_WD * w), nm, nv


def _adamw(w, g, m, v, name):
    r, l = w.shape
    tr = _row_tile(r, l)

    def body(w_ref, g_ref, m_ref, v_ref, d_ref, nm_ref, nv_ref):
        d_ref[...], nm_ref[...], nv_ref[...] = _adamw_update(w_ref[...], g_ref[...], m_ref[...], v_ref[...])

    spec = pl.BlockSpec((tr, l), lambda i: (i, 0))
    return pl.pallas_call(
        body, name=name, out_shape=[jax.ShapeDtypeStruct((r, l), F32)] * 3, grid=(r // tr,),
        in_specs=[spec] * 4, out_specs=[spec] * 3,
        compiler_params=_cparams(dimension_semantics=("arbitrary",)),
    )(w, g, m, v)


def _adamw_parts(w, parts, m, v, name):
    nl, r, l = w.shape
    tr = _row_tile(r, l)

    def body(*refs):
        w_ref, p_refs, (m_ref, v_ref, g_ref, d_ref, nm_ref, nv_ref) = refs[0], refs[1:1 + nl], refs[1 + nl:]
        layer = pl.program_id(0)
        gg = _sum_parts(p_refs[0])
        for q in range(1, nl):
            gg = jnp.where(layer == q, _sum_parts(p_refs[q]), gg)
        g_ref[...] = gg
        d_ref[...], nm_ref[...], nv_ref[...] = _adamw_update(w_ref[...], gg, m_ref[...], v_ref[...])

    spec = pl.BlockSpec((None, tr, l), lambda q, i: (q, i, 0))
    pspecs = [pl.BlockSpec((4, tr, l), lambda q, i, k=k: (0, jnp.where(q == k, i, 0), 0)) for k in range(nl)]
    return pl.pallas_call(
        body, name=name, out_shape=[jax.ShapeDtypeStruct((nl, r, l), F32)] * 4, grid=(nl, r // tr),
        in_specs=[spec] + pspecs + [spec, spec], out_specs=[spec] * 4,
        compiler_params=_cparams(dimension_semantics=("arbitrary", "arbitrary")),
    )(w, *parts, m, v)


def _to_rows(pieces, row_multiple):
    flat = jnp.concatenate([p.reshape(-1) for p in pieces])
    rows = -(-flat.shape[0] // LANES)
    rows = -(-rows // row_multiple) * row_multiple
    flat = jnp.pad(flat, (0, rows * LANES - flat.shape[0]))
    return flat.reshape(rows, LANES)


def _split_rows(rows, shapes):
    flat = rows.reshape(-1)
    out, off = [], 0
    for s in shapes:
        n = int(np.prod(s))
        out.append(flat[off:off + n].reshape(s))
        off += n
    return out


def _mod_fwd(cond, w_mod, b_my, name):
    nl, _, ncol = w_mod.shape

    def body(a_ref, w_ref, b_ref, o_ref):
        a = a_ref[...]
        s = a * _sigmoid(a)
        for i in range(nl):
            o_ref[i] = _dot(s, w_ref[i]) + b_ref[i]

    return pl.pallas_call(
        body, name=name, out_shape=jax.ShapeDtypeStruct((nl, 16, ncol), F32),
        compiler_params=_cparams(),
    )(cond, w_mod, b_my)


def _mod_bwd(cond, dm_all, dm_my, w_mod, name):
    nl, _, ncol = w_mod.shape

    def body(a_ref, dma_ref, dmm_ref, w_ref, gw_ref, gb_ref, gc_ref):
        a = a_ref[...]
        sg = _sigmoid(a)
        s = a * sg
        for i in range(nl):
            gw_ref[i] = _dot_tn(s, dmm_ref[i])
            gb_ref[i] = jnp.sum(dma_ref[i], axis=0, keepdims=True)
        back = _dot_nt(dmm_ref[0], w_ref[0])
        dsilu = sg * (1.0 + a * (1.0 - sg))
        gc_ref[...] = jnp.sum(back[8:16] * dsilu[8:16], axis=0, keepdims=True)

    return pl.pallas_call(
        body, name=name,
        out_shape=[jax.ShapeDtypeStruct((nl, D_MODEL, ncol), F32), jax.ShapeDtypeStruct((nl, 1, 3 * D_MODEL), F32),
                   jax.ShapeDtypeStruct((1, D_MODEL), F32)],
        compiler_params=_cparams(),
    )(cond, dm_all, dm_my, w_mod)


def _in_proj(xt, sc, sh, wg, name, sides=()):
    t = xt.shape[0]
    tm = min(TM_MM, t)

    def body(x_ref, sc_ref, sh_ref, w_ref, u_ref, g_ref):
        h = (x_ref[...] * (1.0 + sc_ref[...]) + sh_ref[...]).astype(MXU_DTYPE)
        for k in range(N_WBLK):
            o = jnp.dot(h, w_ref[k], preferred_element_type=F32)
            if k < N_WBLK // 2:
                u_ref[:, k * WBLK:(k + 1) * WBLK] = o
            else:
                kk = k - N_WBLK // 2
                g_ref[:, kk * WBLK:(kk + 1) * WBLK] = o

    row = pl.BlockSpec((1, D_MODEL), lambda i: (0, 0))
    return _call_with_sides(
        body, sides, name=name, out_shape=[jax.ShapeDtypeStruct((t, D_INNER), F32)] * 2, grid=(t // tm,),
        in_specs=[pl.BlockSpec((tm, D_MODEL), lambda i: (i, 0)), row, row,
                  pl.BlockSpec((N_WBLK, D_MODEL, WBLK), lambda i: (0, 0, 0), pipeline_mode=pl.Buffered(1))],
        out_specs=[pl.BlockSpec((tm, D_INNER), lambda i: (i, 0))] * 2, scratch_shapes=[],
        compiler_params=_cparams(dimension_semantics=("arbitrary",)), args=[xt, sc, sh, wg])


def _halo_maps(nt, tm, n_rows8, pos):
    per = tm // SUBLANES
    prev = lambda cb, i: (jnp.maximum(pos(i) * per - 1, 0), cb)
    nxt = lambda cb, i: (jnp.minimum((pos(i) + 1) * per, n_rows8 - 1), cb)
    return prev, nxt


def _conv_taps(u, prev8, next8, is_first, is_last):
    pz = jnp.where(is_first, 0.0, 1.0)
    nz = jnp.where(is_last, 0.0, 1.0)
    return _shifted(u, prev8 * pz, next8 * nz, [-2, -1, 1])


def _lru_gates(uv, wa_ref, wx_ref, ba, bx, cl, g):
    sl = slice(g * LANES, (g + 1) * LANES)
    uvg = uv[:, sl]
    r = _sigmoid(_dot(uvg, wa_ref[g]) + ba[:, sl])
    ii = _sigmoid(_dot(uvg, wx_ref[g]) + bx[:, sl])
    la = cl[:, sl] * r
    a = jnp.exp(la)
    q = jnp.tanh(-la) * (1.0 + a * a)
    rs = lax.rsqrt(jnp.maximum(q, SQRT_FLOOR))
    return uvg, r, ii, a, q * rs, rs


def _scan_rows(seg):
    return -(-(SCAN_ROW_T * (seg - 1) + SCAN_ROW_J * (N_SEG - 1) + 1) // SUBLANES) * SUBLANES


def _seg_chunk(j, c):
    return pl.ds(SCAN_ROW_T * SUBLANES * c + SCAN_ROW_J * j, SUBLANES, stride=SCAN_ROW_T)


def _seg_scatter(ref, g, seg, value):
    for j in range(N_SEG):
        for c in range(seg // SUBLANES):
            r0 = j * seg + SUBLANES * c
            ref[g, _seg_chunk(j, c), :] = value[r0:r0 + SUBLANES]


def _scan_tile(a_s, b_s, carry_ref, write_out, seg, reverse):
    n_g = a_s.shape[0]
    unroll = SCAN_UNROLL if seg % SCAN_UNROLL == 0 else 1

    n_trips = seg // unroll

    def steps(k, state):
        hs, cs = list(state[0]), list(state[1])
        base = ((n_trips - 1 - k) if reverse else k) * unroll
        for q in (range(unroll - 1, -1, -1) if reverse else range(unroll)):
            t = base + q
            rows = pl.ds(t * SCAN_ROW_T, N_SEG, stride=SCAN_ROW_J)
            for g in range(n_g):
                a = a_s[g, rows, :]
                b = b_s[g, rows, :]
                hs[g] = a * hs[g] + b
                cs[g] = a * cs[g]
                b_s[g, rows, :] = hs[g]
                a_s[g, rows, :] = cs[g]
        return tuple(hs), tuple(cs)

    zeros = tuple(jnp.zeros((N_SEG, LANES), F32) for _ in range(n_g))
    ones = tuple(jnp.ones((N_SEG, LANES), F32) for _ in range(n_g))
    h_fin, a_fin = lax.fori_loop(0, seg // unroll, steps, (zeros, ones))

    order = list(range(N_SEG - 1, -1, -1)) if reverse else list(range(N_SEG))
    for g in range(n_g):
        carry = carry_ref[:, g * LANES:(g + 1) * LANES]
        for j in order:
            for c in range(seg // SUBLANES):
                rows = _seg_chunk(j, c)
                write_out(j, c, g, b_s[g, rows, :] + a_s[g, rows, :] * carry)
            carry = a_fin[g][j:j + 1] * carry + h_fin[g][j:j + 1]
        carry_ref[:, g * LANES:(g + 1) * LANES] = carry


def _lru_specs(s, tm, cb, direction_pos, nt):
    n_rows8 = s // SUBLANES
    prev, nxt = _halo_maps(nt, tm, n_rows8, direction_pos)
    tile = pl.BlockSpec((tm, cb), lambda c, i: (direction_pos(i), c))
    return tile, pl.BlockSpec((SUBLANES, cb), prev), pl.BlockSpec((SUBLANES, cb), nxt)


def _lru_param_specs(cb, d):
    n_g = cb // LANES
    vec = pl.BlockSpec((1, cb), lambda c, i: (0, c))
    dvec = pl.BlockSpec((None, 1, cb), lambda c, i: (d, 0, c))
    wmat = pl.BlockSpec((None, n_g, LRU_BLOCK, LRU_BLOCK), lambda c, i: (d, c, 0, 0))
    return vec, dvec, wmat


def _lru_fwd(src, h0, p, d, name, conv, sides=()):
    s = src.shape[0]
    tm = min(TM_LRU_FWD, s)
    cb = CB_LRU
    n_g = cb // LANES
    nt = s // tm
    seg = tm // N_SEG
    pos = (lambda i: i) if d == 0 else (lambda i: nt - 1 - i)

    def body(*refs):
        refs = list(refs)
        u_ref = refs.pop(0)
        if conv:
            up_ref, un_ref, cw_ref, cbias_ref = [refs.pop(0) for _ in range(4)]
        wa_ref, wx_ref, ba_ref, bx_ref, lam_ref, h0_ref, h_ref, hc_ref = [refs.pop(0) for _ in range(8)]
        uv_ref = refs.pop(0) if conv else None
        a_s, b_s = refs
        i = pl.program_id(1)
        tp = pos(i)

        @pl.when(i == 0)
        def _():
            hc_ref[...] = h0_ref[...]

        if conv:
            u_t = u_ref[...]
            um2, um1, up1 = _conv_taps(u_t, up_ref[...], un_ref[...], tp == 0, tp == nt - 1)
            cw = cw_ref[...]
            uv_ref[...] = um2 * cw[0:1] + um1 * cw[1:2] + u_t * cw[2:3] + up1 * cw[3:4] + cbias_ref[...]
        src_ref = uv_ref if conv else u_ref
        cl = LRU_C * _log_sigmoid(lam_ref[...])
        ba, bx = ba_ref[...], bx_ref[...]
        for g in range(n_g):
            uvg, r, ii, a, sq, _ = _lru_gates(src_ref, wa_ref, wx_ref, ba, bx, cl, g)
            b = sq * (ii * uvg)
            _seg_scatter(a_s, g, seg, a)
            _seg_scatter(b_s, g, seg, b)

        def write_out(j, c, g, h):
            h_ref[pl.ds(j * seg + SUBLANES * c, SUBLANES), pl.ds(g * LANES, LANES)] = h

        _scan_tile(a_s, b_s, hc_ref, write_out, seg, reverse=(d == 1))

    tile, prev, nxt = _lru_specs(s, tm, cb, pos, nt)
    vec, dvec, wmat = _lru_param_specs(cb, d)
    wide = jax.ShapeDtypeStruct((s, D_INNER), F32)
    conv_specs = [prev, nxt, pl.BlockSpec((4, cb), lambda c, i: (0, c)), vec] if conv else []
    conv_args = [src, src, p["conv_w"], p["conv_b"]] if conv else []
    return _call_with_sides(
        body, sides, name=name,
        out_shape=[wide, jax.ShapeDtypeStruct((1, D_INNER), F32)] + ([wide] if conv else []),
        grid=(D_INNER // cb, nt),
        in_specs=[tile] + conv_specs + [wmat, wmat, dvec, dvec, dvec, vec],
        out_specs=[tile, vec] + ([tile] if conv else []),
        scratch_shapes=[pltpu.VMEM((n_g, _scan_rows(seg), LANES), F32)] * 2,
        compiler_params=_cparams(dimension_semantics=("arbitrary", "arbitrary")),
        args=[src, *conv_args, p["wa"], p["wx"], p["ba"], p["bx"], p["lam"], h0])


def _lru_bwd(uv, dh, h, h0, lam_in, p, d, name, sides=()):
    s = uv.shape[0]
    tm = min(TM_LRU, s)
    cb = CB_LRU
    n_g = cb // LANES
    nt = s // tm
    seg = tm // N_SEG
    pos = (lambda i: nt - 1 - i) if d == 0 else (lambda i: i)

    def body(uv_ref, dh_ref, h_ref, hh_ref, wa_ref, wx_ref, ba_ref, bx_ref,
             lam_ref, h0_ref, lin_ref, duv_ref, gwa_ref, gwx_ref, gv_ref, lc_ref, a_s, b_s, lp_s,
             r_s, i_s, q_s, rq_s, a_keep):
        i = pl.program_id(1)
        tp = pos(i)

        @pl.when(i == 0)
        def _():
            lc_ref[...] = lin_ref[...]
            gwa_ref[...] = jnp.zeros_like(gwa_ref)
            gwx_ref[...] = jnp.zeros_like(gwx_ref)
            gv_ref[...] = jnp.zeros_like(gv_ref)

        uv = uv_ref[...]
        lam = lam_ref[...]
        cl = LRU_C * _log_sigmoid(lam)
        ba, bx = ba_ref[...], bx_ref[...]
        dh_t = dh_ref[...]
        carry_in = lc_ref[...]
        for g in range(n_g):
            sl = slice(g * LANES, (g + 1) * LANES)
            _, r, ii, a, sq, rs = _lru_gates(uv, wa_ref, wx_ref, ba, bx, cl, g)
            r_s[:, sl], i_s[:, sl], q_s[:, sl], rq_s[:, sl], a_keep[:, sl] = r, ii, sq, rs, a
            b = a * dh_t[:, sl]
            _seg_scatter(a_s, g, seg, a)
            _seg_scatter(b_s, g, seg, b)

        def write_out(j, c, g, v):
            lp_s[pl.ds(j * seg + SUBLANES * c, SUBLANES), pl.ds(g * LANES, LANES)] = v

        _scan_tile(a_s, b_s, lc_ref, write_out, seg, reverse=(d == 0))

        h_t = h_ref[...]
        hh = hh_ref[...]
        if d == 0:
            edge = jnp.where(tp == 0, h0_ref[...], hh[7:8])
            h_prev = _shift_down(h_t, edge)
            lam_t = dh_t + _shift_up(lp_s[...], carry_in)
        else:
            edge = jnp.where(tp == nt - 1, h0_ref[...], hh[0:1])
            h_prev = _shift_up(h_t, edge)
            lam_t = dh_t + _shift_down(lp_s[...], carry_in)

        dsig = LRU_C * _sigmoid(-lam)
        for g in range(n_g):
            sl = slice(g * LANES, (g + 1) * LANES)
            uvg, r, ii, a, sq = uv[:, sl], r_s[:, sl], i_s[:, sl], a_keep[:, sl], q_s[:, sl]
            lt = lam_t[:, sl]
            ls = lt * sq
            dla = (lt * a) * (h_prev[:, sl] - (ii * uvg) * (a * rq_s[:, sl]))
            dzr = (dla * cl[:, sl]) * r * (1.0 - r)
            dzi = (ls * uvg) * ii * (1.0 - ii)
            duv_ref[:, sl] = ls * ii + _dot_nt(dzr, wa_ref[g]) + _dot_nt(dzi, wx_ref[g])
            gwa_ref[g] += _dot_tn(uvg, dzr)
            gwx_ref[g] += _dot_tn(uvg, dzi)
            gv_ref[0:1, sl] += _rowsum(dzr)
            gv_ref[1:2, sl] += _rowsum(dzi)
            gv_ref[2:3, sl] += _rowsum(dla * r) * dsig[:, sl]

    tile, prev, nxt = _lru_specs(s, tm, cb, pos, nt)
    vec, dvec, wmat = _lru_param_specs(cb, d)
    hh_spec = prev if d == 0 else nxt
    gw_spec = pl.BlockSpec((n_g, LRU_BLOCK, LRU_BLOCK), lambda c, i: (c, 0, 0))
    n_blk = D_INNER // LRU_BLOCK
    return _call_with_sides(
        body, sides, name=name,
        out_shape=[jax.ShapeDtypeStruct((s, D_INNER), F32),
                   jax.ShapeDtypeStruct((n_blk, LRU_BLOCK, LRU_BLOCK), F32),
                   jax.ShapeDtypeStruct((n_blk, LRU_BLOCK, LRU_BLOCK), F32),
                   jax.ShapeDtypeStruct((SUBLANES, D_INNER), F32),
                   jax.ShapeDtypeStruct((1, D_INNER), F32)],
        grid=(D_INNER // cb, nt),
        in_specs=[tile, tile, tile, hh_spec, wmat, wmat, dvec, dvec, dvec, vec, vec],
        out_specs=[tile, gw_spec, gw_spec, pl.BlockSpec((SUBLANES, cb), lambda c, i: (0, c)), vec],
        scratch_shapes=[pltpu.VMEM((n_g, _scan_rows(seg), LANES), F32)] * 2 + [pltpu.VMEM((tm, cb), F32)] * 6,
        compiler_params=_cparams(dimension_semantics=("arbitrary", "arbitrary")),
        args=[uv, dh, h, h, p["wa"], p["wx"], p["ba"], p["bx"], p["lam"], h0, lam_in])


def _out0(hf, hb, g, xt, gt, wo, lg, lb, name):
    t = xt.shape[0]
    tm = min(TM_MM, t)

    def body(hf_ref, hb_ref, g_ref, x_ref, gt_ref, w_ref, lg_ref, lb_ref, x1_ref, br_ref):
        br = None
        for k in range(D_INNER // WBLK):
            sl = slice(k * WBLK, (k + 1) * WBLK)
            gg = g_ref[:, sl]
            p = (hf_ref[:, sl] + hb_ref[:, sl]) * (gg * _sigmoid(gg))
            part = _dot(p, w_ref[sl, :])
            br = part if br is None else br + part
        z = ALPHA * x_ref[...] + gt_ref[...] * br
        xhat, _ = _layer_norm_stats(z)
        x1_ref[...] = xhat * lg_ref[...] + lb_ref[...]
        br_ref[...] = br

    wide = pl.BlockSpec((tm, D_INNER), lambda i: (i, 0))
    nar = pl.BlockSpec((tm, D_MODEL), lambda i: (i, 0))
    row = pl.BlockSpec((1, D_MODEL), lambda i: (0, 0))
    return pl.pallas_call(
        body, name=name, out_shape=[jax.ShapeDtypeStruct((t, D_MODEL), F32)] * 2, grid=(t // tm,),
        in_specs=[wide, wide, wide, nar, row,
                  pl.BlockSpec((D_INNER, D_MODEL), lambda i: (0, 0), pipeline_mode=pl.Buffered(1)), row, row],
        out_specs=[nar, nar],
        compiler_params=_cparams(dimension_semantics=("arbitrary",)),
    )(hf, hb, g, xt, gt, wo, lg, lb)


def _unrolled_loop(n, fn, unroll=4):
    while n % unroll:
        unroll //= 2

    def trip(k, carry):
        for q in range(unroll):
            fn(k * unroll + q)
        return carry
    lax.fori_loop(0, n // unroll, trip, 0)


def _window(n, w):
    t = np.arange(n)
    return np.clip(t - w // 2, 0, n), np.clip(t + w // 2, 0, n)


def _pool_tables(n_rows, transpose):
    boxes, inv_c, inv_r = [], [], []
    for w in POOL_WINDOWS:
        lo, hi = _window(GRID_W, w)
        m = np.zeros((GRID_W, GRID_W), np.float32)
        for r in range(GRID_W):
            m[r, lo[r]:hi[r]] = 1.0
        m = np.kron(np.eye(POOL_TOK // GRID_W, dtype=np.float32), m)
        boxes.append(m.T if transpose else m)
        inv_c.append(np.broadcast_to((1.0 / (hi - lo).astype(np.float32))[:, None], (GRID_W, LANES)))
        lo_r, hi_r = _window(n_rows, w)
        inv_r.append(1.0 / (hi_r - lo_r).astype(np.float32))
    return (jnp.asarray(np.stack(boxes), MXU_DTYPE), jnp.asarray(np.stack(inv_c), F32),
            jnp.asarray(np.stack(inv_r), F32))


def _pool_mix(xin, transpose, out_dtype, name):
    s = xin.shape[0]
    n_rows = s // GRID_W
    pad_t = SUBLANES * GRID_W
    rows_per_blk = POOL_TOK // GRID_W
    n_slab = D_INNER // LANES
    slabs_per_group = POOL_GROUP // LANES
    n_win = len(POOL_WINDOWS)
    boxes, inv_c, inv_r = _pool_tables(n_rows, transpose)

    def body(invr_ref, box_ref, invc_ref, x_ref, o_ref, pad_s):
        k = pl.program_id(0) // slabs_per_group
        pad_s[pl.ds(0, pad_t), :] = jnp.zeros((pad_t, LANES), F32)
        pad_s[pl.ds(pad_t + s, pad_t), :] = jnp.zeros((pad_t, LANES), F32)

        for kk, w in enumerate(POOL_WINDOWS):
            half = w // 2
            offsets = list(range(-(half - 1), half + 1)) if transpose else list(range(-half, half))

            @pl.when(k == kk)
            def _():
                inv_col = invc_ref[kk]

                def col_box(b):
                    st = pl.multiple_of(b * POOL_TOK, POOL_TOK)
                    xb = x_ref[pl.ds(st, POOL_TOK), :]
                    if transpose:
                        xb = xb * jnp.concatenate(
                            [inv_col * invr_ref[kk, b * rows_per_blk + q] for q in range(rows_per_blk)], axis=0)
                    hi = xb.astype(MXU_DTYPE)
                    lo = (xb - hi.astype(F32)).astype(MXU_DTYPE)
                    both = jnp.dot(box_ref[kk], jnp.concatenate([hi, lo], axis=1), preferred_element_type=F32)
                    pad_s[pl.ds(pad_t + st, POOL_TOK), :] = both[:, :LANES] + both[:, LANES:]
                _unrolled_loop(s // POOL_TOK, col_box)

                def row_box(r):
                    st = pl.multiple_of(r * GRID_W, GRID_W)
                    acc = pad_s[pl.ds(pad_t + st + offsets[0] * GRID_W, GRID_W), :]
                    for o in offsets[1:]:
                        acc = acc + pad_s[pl.ds(pad_t + st + o * GRID_W, GRID_W), :]
                    if not transpose:
                        acc = acc * (inv_col * invr_ref[kk, r])
                    o_ref[pl.ds(st, GRID_W), :] = (acc - x_ref[pl.ds(st, GRID_W), :]).astype(out_dtype)
                _unrolled_loop(n_rows, row_box)

    slab = pl.BlockSpec((s, LANES), lambda i: (0, i))
    return pl.pallas_call(
        body, name=name, out_shape=jax.ShapeDtypeStruct((s, D_INNER), out_dtype), grid=(n_slab,),
        in_specs=[pl.BlockSpec(memory_space=pltpu.SMEM),
                  pl.BlockSpec((n_win, POOL_TOK, POOL_TOK), lambda i: (0, 0, 0)),
                  pl.BlockSpec((n_win, GRID_W, LANES), lambda i: (0, 0, 0)), slab],
        out_specs=slab,
        scratch_shapes=[pltpu.VMEM((s + 2 * pad_t, LANES), F32)],
        compiler_params=_cparams(dimension_semantics=("arbitrary",)),
    )(inv_r, boxes, inv_c, xin)


def _out1(dmix, pw, ps, g, x1, gt, wo, lg, lb, tgt, name):
    t = x1.shape[0]
    tm = min(TM_MM, t)
    n_grp = len(POOL_WINDOWS)

    def body(d_ref, pw_ref, ps_ref, g_ref, x1_ref, gt_ref, w_ref, lg_ref, lb_ref, tgt_ref, dz_ref, st_ref):
        @pl.when(pl.program_id(0) == 0)
        def _():
            st_ref[...] = jnp.zeros_like(st_ref)

        br = jnp.zeros((tm, D_MODEL), F32)
        for k in range(n_grp):
            sl = slice(k * POOL_GROUP, (k + 1) * POOL_GROUP)
            y = jnp.dot(d_ref[:, sl], pw_ref[k], preferred_element_type=F32) * ps_ref[:, sl]
            gg = g_ref[:, sl]
            br = br + _dot(y * (gg * _sigmoid(gg)), w_ref[sl, :])
        z = ALPHA * x1_ref[...] + gt_ref[...] * br
        xhat, rstd = _layer_norm_stats(z)
        lg_v = lg_ref[...]
        err = xhat * lg_v + lb_ref[...] - tgt_ref[...]
        dy = err * (1.0 / D_MODEL)
        dz = _layer_norm_bwd(dy, xhat, rstd, lg_v)
        dz_ref[...] = dz
        st_ref[0:1, :] += _rowsum(dy * xhat)
        st_ref[1:2, :] += _rowsum(dy)
        st_ref[2:3, :] += _rowsum(dz * br)
        st_ref[3:4, :] += _rowsum(err * err)

    wide = pl.BlockSpec((tm, D_INNER), lambda i: (i, 0))
    nar = pl.BlockSpec((tm, D_MODEL), lambda i: (i, 0))
    row = pl.BlockSpec((1, D_MODEL), lambda i: (0, 0))
    return pl.pallas_call(
        body, name=name,
        out_shape=[jax.ShapeDtypeStruct((t, D_MODEL), F32), jax.ShapeDtypeStruct((SUBLANES, D_MODEL), F32)],
        grid=(t // tm,),
        in_specs=[wide, pl.BlockSpec((n_grp, POOL_GROUP, POOL_GROUP), lambda i: (0, 0, 0)),
                  pl.BlockSpec((1, D_INNER), lambda i: (0, 0)), wide, nar, row,
                  pl.BlockSpec((D_INNER, D_MODEL), lambda i: (0, 0), pipeline_mode=pl.Buffered(1)), row, row, nar],
        out_specs=[nar, pl.BlockSpec((SUBLANES, D_MODEL), lambda i: (0, 0))],
        compiler_params=_cparams(dimension_semantics=("arbitrary",)),
    )(dmix, pw, ps, g, x1, gt, wo, lg, lb, tgt)


def _flush(acc, out_hbm, sem):
    cp = pltpu.make_async_copy(acc, out_hbm, sem)
    cp.start()
    cp.wait()


def _bout1(dz, dmix, g, pw, ps, gt, wo, name):
    t = dz.shape[0]
    tm = min(TM_MM, t)
    nt = t // tm
    n_grp = len(POOL_WINDOWS)

    def body(dz_ref, d_ref, g_ref, pw_ref, ps_ref, gt_ref, w_ref, dd_ref, dg_ref, gwo_hbm, gpw_hbm, gps_ref,
             gwo_acc, gpw_acc, sems):
        i = pl.program_id(0)

        @pl.when(i == 0)
        def _():
            gwo_acc[...] = jnp.zeros_like(gwo_acc)
            gpw_acc[...] = jnp.zeros_like(gpw_acc)
            gps_ref[...] = jnp.zeros_like(gps_ref)

        db = (gt_ref[...] * dz_ref[...]).astype(MXU_DTYPE)
        for k in range(n_grp):
            sl = slice(k * POOL_GROUP, (k + 1) * POOL_GROUP)
            dk = d_ref[:, sl]
            po = jnp.dot(dk, pw_ref[k], preferred_element_type=F32)
            psk = ps_ref[:, sl]
            y = po * psk
            gg = g_ref[:, sl]
            sg = _sigmoid(gg)
            silu = gg * sg
            gwo_acc[sl, :] += _dot_tn(y * silu, db)
            dp = _dot_nt(db, w_ref[sl, :])
            dy = dp * silu
            dg_ref[:, sl] = (dp * y * (sg * (1.0 + gg * (1.0 - sg)))).astype(MXU_DTYPE)
            gps_ref[0:1, sl] += _rowsum(dy * po)
            dpo = (dy * psk).astype(MXU_DTYPE)
            gpw_acc[k] += _dot_tn(dk, dpo)
            dd_ref[:, sl] = _dot_nt(dpo, pw_ref[k])

        @pl.when(i == nt - 1)
        def _():
            _flush(gwo_acc, gwo_hbm, sems.at[0])
            _flush(gpw_acc, gpw_hbm, sems.at[1])

    wide = pl.BlockSpec((tm, D_INNER), lambda i: (i, 0))
    nar = pl.BlockSpec((tm, D_MODEL), lambda i: (i, 0))
    return pl.pallas_call(
        body, name=name,
        out_shape=[jax.ShapeDtypeStruct((t, D_INNER), F32), jax.ShapeDtypeStruct((t, D_INNER), MXU_DTYPE),
                   jax.ShapeDtypeStruct((D_INNER, D_MODEL), F32),
                   jax.ShapeDtypeStruct((n_grp, POOL_GROUP, POOL_GROUP), F32),
                   jax.ShapeDtypeStruct((SUBLANES, D_INNER), F32)],
        grid=(nt,),
        in_specs=[nar, wide, wide,
                  pl.BlockSpec((n_grp, POOL_GROUP, POOL_GROUP), lambda i: (0, 0, 0), pipeline_mode=pl.Buffered(1)),
                  pl.BlockSpec((1, D_INNER), lambda i: (0, 0)), pl.BlockSpec((1, D_MODEL), lambda i: (0, 0)),
                  pl.BlockSpec((D_INNER, D_MODEL), lambda i: (0, 0), pipeline_mode=pl.Buffered(1))],
        out_specs=[wide, wide, ANY, ANY, pl.BlockSpec((SUBLANES, D_INNER), lambda i: (0, 0))],
        scratch_shapes=[pltpu.VMEM((D_INNER, D_MODEL), F32), pltpu.VMEM((n_grp, POOL_GROUP, POOL_GROUP), F32),
                        pltpu.SemaphoreType.DMA((2,))],
        compiler_params=_cparams(dimension_semantics=("arbitrary",)),
    )(dz, dmix, g, pw, ps, gt, wo)


def _bout0(dx1, xt, br0, lg, hf, hb, g, gt, wo, name, sides=()):
    t = dx1.shape[0]
    tm = min(TM_BWD, t)
    nt = t // tm

    def body(dx_ref, x_ref, br_ref, lg_ref, hf_ref, hb_ref, g_ref, gt_ref, w_ref,
             dz_ref, dy_ref, dg_ref, gwo_hbm, st_ref, gwo_acc, sem):
        i = pl.program_id(0)

        @pl.when(i == 0)
        def _():
            gwo_acc[...] = jnp.zeros_like(gwo_acc)
            st_ref[...] = jnp.zeros_like(st_ref)

        dx = dx_ref[...]
        br = br_ref[...]
        gate = gt_ref[...]
        xhat, rstd = _layer_norm_stats(ALPHA * x_ref[...] + gate * br)
        dz = _layer_norm_bwd(dx, xhat, rstd, lg_ref[...])
        dz_ref[...] = dz
        st_ref[0:1, :] += _rowsum(dx * xhat)
        st_ref[1:2, :] += _rowsum(dx)
        st_ref[2:3, :] += _rowsum(dz * br)
        db = (gate * dz).astype(MXU_DTYPE)
        for k in range(D_INNER // WBLK):
            sl = slice(k * WBLK, (k + 1) * WBLK)
            y = hf_ref[:, sl] + hb_ref[:, sl]
            gg = g_ref[:, sl]
            sg = _sigmoid(gg)
            silu = gg * sg
            gwo_acc[sl, :] += _dot_tn(y * silu, db)
            dp = _dot_nt(db, w_ref[sl, :])
            dy_ref[:, sl] = dp * silu
            dg_ref[:, sl] = (dp * y * (sg * (1.0 + gg * (1.0 - sg)))).astype(MXU_DTYPE)

        @pl.when(i == nt - 1)
        def _():
            _flush(gwo_acc, gwo_hbm, sem)

    wide = pl.BlockSpec((tm, D_INNER), lambda i: (i, 0))
    nar = pl.BlockSpec((tm, D_MODEL), lambda i: (i, 0))
    row = pl.BlockSpec((1, D_MODEL), lambda i: (0, 0))
    return _call_with_sides(
        body, sides, name=name,
        out_shape=[jax.ShapeDtypeStruct((t, D_MODEL), F32), jax.ShapeDtypeStruct((t, D_INNER), F32),
                   jax.ShapeDtypeStruct((t, D_INNER), MXU_DTYPE), jax.ShapeDtypeStruct((D_INNER, D_MODEL), F32),
                   jax.ShapeDtypeStruct((SUBLANES, D_MODEL), F32)],
        grid=(nt,),
        in_specs=[nar, nar, nar, row, wide, wide, wide, row,
                  pl.BlockSpec((D_INNER, D_MODEL), lambda i: (0, 0), pipeline_mode=pl.Buffered(1))],
        out_specs=[nar, wide, wide, ANY, pl.BlockSpec((SUBLANES, D_MODEL), lambda i: (0, 0))],
        scratch_shapes=[pltpu.VMEM((D_INNER, D_MODEL), F32), pltpu.SemaphoreType.DMA(())],
        compiler_params=_cparams(dimension_semantics=("arbitrary",)),
        args=[dx1, xt, br0, lg, hf, hb, g, gt, wo])


def _conv_bwd(duvf, duvb, u, conv_w, name):
    s = u.shape[0]
    tm = min(TM_LRU_FWD, s)
    cb = CB_LRU
    nt = s // tm

    def body(df_ref, dfp_ref, dfn_ref, db_ref, dbp_ref, dbn_ref, u_ref, cw_ref, du_ref, cst_ref):
        i = pl.program_id(1)

        @pl.when(i == 0)
        def _():
            cst_ref[...] = jnp.zeros_like(cst_ref)

        first, last = i == 0, i == nt - 1
        pz = jnp.where(first, 0.0, 1.0)
        nz = jnp.where(last, 0.0, 1.0)
        dout = df_ref[...] + db_ref[...]
        dm1, dp1, dp2 = _shifted(dout, (dfp_ref[...] + dbp_ref[...]) * pz, (dfn_ref[...] + dbn_ref[...]) * nz,
                                 [-1, 1, 2])
        cw = cw_ref[...]
        du_ref[...] = (dp2 * cw[0:1] + dp1 * cw[1:2] + dout * cw[2:3] + dm1 * cw[3:4]).astype(MXU_DTYPE)
        u_t = u_ref[...]
        cst_ref[0:1, :] += _rowsum(dp2 * u_t)
        cst_ref[1:2, :] += _rowsum(dp1 * u_t)
        cst_ref[2:3, :] += _rowsum(dout * u_t)
        cst_ref[3:4, :] += _rowsum(dm1 * u_t)
        cst_ref[4:5, :] += _rowsum(dout)

    tile, prev, nxt = _lru_specs(s, tm, cb, lambda i: i, nt)
    return pl.pallas_call(
        body, name=name,
        out_shape=[jax.ShapeDtypeStruct((s, D_INNER), MXU_DTYPE), jax.ShapeDtypeStruct((SUBLANES, D_INNER), F32)],
        grid=(D_INNER // cb, nt),
        in_specs=[tile, prev, nxt] * 2 + [tile, pl.BlockSpec((4, cb), lambda c, i: (0, c))],
        out_specs=[tile, pl.BlockSpec((SUBLANES, cb), lambda c, i: (0, c))],
        compiler_params=_cparams(dimension_semantics=("arbitrary", "arbitrary")),
    )(duvf, duvf, duvf, duvb, duvb, duvb, u, conv_w)


def _bin(du, dg, xin, dzin, sc, sh, wg, name, gw_init=None):
    t = xin.shape[0]
    tm = min(TM_MM, t)
    nt = t // tm
    has_g, has_dx, has_init = dg is not None, dzin is not None, gw_init is not None
    half = N_WBLK // 2
    n_blk = N_WBLK if has_g else half

    def body(*refs):
        refs = list(refs)
        du_ref = refs.pop(0)
        dg_ref = refs.pop(0) if has_g else None
        x_ref = refs.pop(0)
        dz_ref = refs.pop(0) if has_dx else None
        sc_ref, sh_ref, w_ref = refs.pop(0), refs.pop(0), refs.pop(0)
        init_hbm = refs.pop(0) if has_init else None
        dx_ref = refs.pop(0) if has_dx else None
        gw_hbm, st_ref, gw_acc, sem = refs
        i = pl.program_id(0)

        @pl.when(i == 0)
        def _():
            st_ref[...] = jnp.zeros_like(st_ref)
            first_zero = 0
            if has_init:
                _flush(init_hbm, gw_acc.at[pl.ds(0, half)], sem)
                first_zero = half
            for k in range(first_zero, n_blk):
                gw_acc[k] = jnp.zeros((D_MODEL, WBLK), F32)

        xv = x_ref[...]
        scale = 1.0 + sc_ref[...]
        h = (xv * scale + sh_ref[...]).astype(MXU_DTYPE)
        dh = None
        for k in range(n_blk):
            src = du_ref if k < half else dg_ref
            kk = k % half
            dk = src[:, kk * WBLK:(kk + 1) * WBLK]
            gw_acc[k] += _dot_tn(h, dk)
            contrib = _dot_nt(dk, w_ref[k])
            dh = contrib if dh is None else dh + contrib
        st_ref[0:1, :] += _rowsum(dh * xv)
        st_ref[1:2, :] += _rowsum(dh)
        if has_dx:
            dx_ref[...] = ALPHA * dz_ref[...] + dh * scale

        @pl.when(i == nt - 1)
        def _():
            _flush(gw_acc, gw_hbm, sem)

    wide = pl.BlockSpec((tm, D_INNER), lambda i: (i, 0))
    nar = pl.BlockSpec((tm, D_MODEL), lambda i: (i, 0))
    row = pl.BlockSpec((1, D_MODEL), lambda i: (0, 0))
    wspec = pl.BlockSpec((n_blk, D_MODEL, WBLK), lambda i: (0, 0, 0), pipeline_mode=pl.Buffered(1))
    in_specs = ([wide] + ([wide] if has_g else []) + [nar] + ([nar] if has_dx else []) + [row, row, wspec]
                + ([ANY] if has_init else []))
    args = ([du] + ([dg] if has_g else []) + [xin] + ([dzin] if has_dx else []) + [sc, sh, wg]
            + ([gw_init] if has_init else []))
    out_shape = ([jax.ShapeDtypeStruct((t, D_MODEL), F32)] if has_dx else []) + [
        jax.ShapeDtypeStruct((n_blk, D_MODEL, WBLK), F32), jax.ShapeDtypeStruct((SUBLANES, D_MODEL), F32)]
    out_specs = ([nar] if has_dx else []) + [ANY, pl.BlockSpec((SUBLANES, D_MODEL), lambda i: (0, 0))]
    return pl.pallas_call(
        body, name=name, out_shape=out_shape, grid=(nt,), in_specs=in_specs, out_specs=out_specs,
        scratch_shapes=[pltpu.VMEM((n_blk, D_MODEL, WBLK), F32), pltpu.SemaphoreType.DMA(())],
        compiler_params=_cparams(dimension_semantics=("arbitrary",)),
    )(*args)


def _blocks_by_device(a, axis):
    shape = a.shape
    a = a.reshape(shape[:axis] + (N_DEV, shape[axis] // N_DEV) + shape[axis + 1:])
    return jnp.moveaxis(a, axis, 0)


def kernel(x, c, ctx, c_ctx, w_mod, b_mod, w_in, w_out, ln_g, ln_b, conv_w, conv_b, lru_wa, lru_ba, lru_wx, lru_bx, lru_lam, pool_w, pool_scale, loss_target, m_c_ctx, m_w_mod, m_b_mod, m_w_in, m_w_out, m_ln_g, m_ln_b, m_conv_w, m_conv_b, m_lru_wa, m_lru_ba, m_lru_wx, m_lru_bx, m_lru_lam, m_pool_w, m_pool_scale, v_c_ctx, v_w_mod, v_b_mod, v_w_in, v_w_out, v_ln_g, v_ln_b, v_conv_w, v_conv_b, v_lru_wa, v_lru_ba, v_lru_wx, v_lru_bx, v_lru_lam, v_pool_w, v_pool_scale):
    xi, yi, ci = _my_pos()
    dev = 4 * xi + 2 * yi + ci
    xt, ctxt, tgt = x[0], ctx[0], loss_target[0]
    n_mod = w_mod.shape[2]

    small_shapes = [(D_MODEL,), conv_w.shape[1:], lru_ba.shape[1:], lru_bx.shape[1:], lru_lam.shape[1:],
                    pool_scale.shape[1:]]
    small = _to_rows([c[0], conv_w[0], lru_ba[0], lru_bx[0], lru_lam[0], pool_scale[0]], SUBLANES)
    small_all, = _all_gather([small], "gather_small")
    pieces = [_split_rows(small_all[k], small_shapes) for k in range(N_DEV)]
    c_all = jnp.stack([p[0] for p in pieces])
    conv_w_f = jnp.concatenate([p[1] for p in pieces], axis=-1)
    lru_ba_f = jnp.concatenate([p[2] for p in pieces], axis=-1)[:, None, :]
    lru_bx_f = jnp.concatenate([p[3] for p in pieces], axis=-1)[:, None, :]
    lru_lam_f = jnp.concatenate([p[4] for p in pieces], axis=-1)[:, None, :]
    pool_scale_f = jnp.concatenate([p[5] for p in pieces], axis=-1)[None, :]

    cond = jnp.concatenate([c_all, jnp.broadcast_to(c_ctx[None, :], (N_DEV, D_MODEL))], axis=0)
    b_my = lax.dynamic_slice(b_mod, (0, dev * n_mod), (2, n_mod))[:, None, :]
    mod_part = _mod_fwd(cond, w_mod, b_my, "mod_fwd")
    mod_all, = _all_gather([mod_part], "gather_mod")
    mod = jnp.transpose(mod_all, (1, 2, 0, 3)).reshape(2, 16, 3 * D_MODEL)
    mod_me = lax.dynamic_slice(mod, (0, dev, 0), (2, 1, 3 * D_MODEL))
    sh = [mod_me[i, :, 0:D_MODEL] for i in range(2)]
    sc = [mod_me[i, :, D_MODEL:2 * D_MODEL] for i in range(2)]
    gt = [mod_me[i, :, 2 * D_MODEL:] for i in range(2)]
    shc, scc = mod[0, 8:9, 0:D_MODEL], mod[0, 8:9, D_MODEL:2 * D_MODEL]

    wi0, = _all_gather([w_in[0].astype(MXU_DTYPE)], "gather_weights0")
    lg = [ln_g[i][None, :] for i in range(2)]
    lb = [ln_b[i][None, :] for i in range(2)]
    lru_p = dict(conv_w=conv_w_f, conv_b=conv_b, wa=lru_wa[0].astype(MXU_DTYPE), wx=lru_wx[0].astype(MXU_DTYPE),
                 ba=lru_ba_f, bx=lru_bx_f, lam=lru_lam_f)
    zero_state = jnp.zeros((1, D_INNER), F32)

    (u0, g0), (wo0,) = _in_proj(xt, sc[0], sh[0], wi0, "in_proj0", sides=[("gather", [w_out[0].astype(MXU_DTYPE)])])
    (uc, _), _ = _in_proj(ctxt, scc, shc, wi0, "in_proj0_ctx")
    (hcf, cf, uvc), _ = _lru_fwd(uc, zero_state, lru_p, 0, "lru_fwd_ctx_f", conv=True)
    (hcb, cbk), _ = _lru_fwd(uvc, zero_state, lru_p, 1, "lru_fwd_ctx_b", conv=False)
    (hf, _, uv0), (wi1,) = _lru_fwd(u0, cf, lru_p, 0, "lru_fwd_f", conv=True,
                                    sides=[("gather", [w_in[1].astype(MXU_DTYPE)])])
    (hb, _), (wo1, pool_w_g) = _lru_fwd(
        uv0, cbk, lru_p, 1, "lru_fwd_b", conv=False,
        sides=[("gather", [w_out[1].astype(MXU_DTYPE), pool_w[0].astype(MXU_DTYPE)])])
    w_in_l = [wi0, wi1]
    w_out_l = [wo0.reshape(D_INNER, D_MODEL), wo1.reshape(D_INNER, D_MODEL)]
    pool_w_f = jnp.transpose(pool_w_g, (1, 0, 2, 3)).reshape(len(POOL_WINDOWS), POOL_GROUP, POOL_GROUP)
    x1, br0 = _out0(hf, hb, g0, xt, gt[0], w_out_l[0], lg[0], lb[0], "out0")
    (u1, g1), _ = _in_proj(x1, sc[1], sh[1], w_in_l[1], "in_proj1")
    dmix = _pool_mix(u1, False, MXU_DTYPE, "pool_fwd")
    dz1, st1 = _out1(dmix, pool_w_f, pool_scale_f, g1, x1, gt[1], w_out_l[1], lg[1], lb[1], tgt, "out1")
    loss_me = jnp.full((1, LANES), (0.5 / D_MODEL) * jnp.sum(st1[3]), F32)

    core = jnp.reshape(ci, (1,)).astype(jnp.int32)
    wo_view = lambda a: a.reshape(N_DEV, D_INNER // N_DEV, D_MODEL)
    pw_view = lambda a: _blocks_by_device(a, 1).reshape(N_DEV, POOL_GROUP // N_DEV * len(POOL_WINDOWS), POOL_GROUP)
    dd, dg1, gwo1, gpw, gps = _bout1(dz1, dmix, g1, pool_w_f, pool_scale_f, gt[1], w_out_l[1], "bwd_out1")
    du1 = _pool_mix(dd, True, MXU_DTYPE, "pool_bwd")
    dx1, gwi1, stb1 = _bin(du1, dg1, x1, dz1, sc[1], sh[1], w_in_l[1], "bwd_in1")
    bufs1 = [gwi1, wo_view(gwo1), pw_view(gpw)]
    (dz0, dy0, dg0, gwo0, stl0), recv1 = _bout0(dx1, xt, br0, lg[0], hf, hb, g0, gt[0], w_out_l[0], "bwd_out0",
                                                sides=[("sibling", bufs1)])
    pairs1 = [_pair_sum(b, r, core, "reduce_pair_" + n)
              for b, r, n in zip(bufs1, recv1, ["w_in1", "w_out1", "pool_w"])]
    (duvf, gwa_f, gwx_f, gv_f, dh0f), (p_wi1, p_wo1, p_pw, recv_wo0) = _lru_bwd(
        uv0, dy0, hf, cf, zero_state, lru_p, 0, "lru_bwd_f", sides=[("chips", pairs1), ("sibling", [wo_view(gwo0)])])
    pair_wo0 = _pair_sum(wo_view(gwo0), recv_wo0, core, "reduce_pair_w_out0")
    (duvb, gwa_b, gwx_b, gv_b, dh0b), (p_wo0,) = _lru_bwd(
        uv0, dy0, hb, cbk, zero_state, lru_p, 1, "lru_bwd_b", sides=[("chips", [pair_wo0])])
    zero_dh = jnp.zeros_like(uc)
    (ducf, gwa_cf, gwx_cf, gv_cf, _), _ = _lru_bwd(uvc, zero_dh, hcf, zero_state, dh0f, lru_p, 0, "lru_bwd_ctx_f")
    (ducb, gwa_cb, gwx_cb, gv_cb, _), _ = _lru_bwd(uvc, zero_dh, hcb, zero_state, dh0b, lru_p, 1, "lru_bwd_ctx_b")
    du0, cst0 = _conv_bwd(duvf, duvb, u0, conv_w_f, "conv_bwd")
    duc, cstc = _conv_bwd(ducf, ducb, uc, conv_w_f, "conv_bwd_ctx")
    gwic, stc = _bin(duc, None, ctxt, None, scc, shc, w_in_l[0][:N_WBLK // 2], "bwd_in0_ctx")
    gx, gwi0, stb0 = _bin(du0, dg0, xt, dz0, sc[0], sh[0], w_in_l[0], "bwd_in0", gw_init=gwic)

    zero_row = jnp.zeros((1, D_MODEL), F32)
    dm_me = jnp.stack([
        jnp.concatenate([jnp.concatenate([stb0[1:2], stb0[0:1], stl0[2:3]], axis=1),
                         jnp.concatenate([stc[1:2], stc[0:1], zero_row], axis=1)], axis=0),
        jnp.concatenate([jnp.concatenate([stb1[1:2], stb1[0:1], st1[2:3]], axis=1),
                         jnp.zeros((1, 3 * D_MODEL), F32)], axis=0)])
    dm_g, loss_g = _all_gather([dm_me, loss_me], "gather_dmod")
    loss = jnp.sum(loss_g[:, 0, 0])
    dm_all = jnp.concatenate([jnp.transpose(dm_g[:, :, 0], (1, 0, 2)), jnp.transpose(dm_g[:, :, 1], (1, 0, 2))],
                             axis=1)
    dm_my = lax.dynamic_slice(dm_all, (0, 0, dev * n_mod), (2, 16, n_mod))
    g_w_mod, g_b_mod, gcc_part = _mod_bwd(cond, dm_all, dm_my, w_mod, "mod_bwd")
    g_b_mod = g_b_mod.reshape(b_mod.shape)

    gwa = jnp.stack([gwa_f + gwa_cf, gwa_b + gwa_cb])
    gwx = jnp.stack([gwx_f + gwx_cf, gwx_b + gwx_cb])
    gv = jnp.stack([gv_f + gv_cf, gv_b + gv_cb])
    cst = cst0 + cstc
    g_ln_g = jnp.stack([stl0[0], st1[0]])
    g_ln_b = jnp.stack([stl0[1], st1[1]])
    sharded = [
        _blocks_by_device(cst[0:4], 1),
        _blocks_by_device(gv[:, 0], 1), _blocks_by_device(gv[:, 1], 1), _blocks_by_device(gv[:, 2], 1),
        _blocks_by_device(gps[0], 0),
    ]
    replicated = [gwa.reshape(-1), gwx.reshape(-1), g_ln_g.reshape(-1), g_ln_b.reshape(-1), cst[4],
                  gcc_part.reshape(-1)]
    sh_sizes = [int(np.prod(a.shape[1:])) for a in sharded]
    rep_sizes = [a.shape[0] // N_DEV for a in replicated]
    n_flat = sum(sh_sizes) + sum(rep_sizes)
    rows = -(-n_flat // LANES)
    rows = -(-rows // FLAT_ROWS) * FLAT_ROWS
    misc = jnp.concatenate([a.reshape(N_DEV, -1) for a in sharded] +
                           [a.reshape(N_DEV, -1) for a in replicated], axis=1)
    misc = jnp.pad(misc, ((0, 0), (0, rows * LANES - n_flat))).reshape(N_DEV, rows, LANES)
    bufs = [gwi0, misc]
    recvs = _sibling_exchange(bufs, "reduce_sibling")
    pairs = [_pair_sum(b, r, core, "reduce_pair_" + n) for b, r, n in zip(bufs, recvs, ["w_in0", "misc"])]
    p_wi0, p_misc = _chip_exchange(pairs, "reduce_chips")
    g_flat = _sum4(p_misc, "reduce_sum_misc").reshape(-1)

    offs = np.cumsum([0] + sh_sizes + rep_sizes)
    n_sh = len(sh_sizes)
    sh_shapes = [conv_w.shape, lru_ba.shape, lru_bx.shape, lru_lam.shape, pool_scale.shape]
    g_sh = [g_flat[offs[k]:offs[k + 1]].reshape(sh_shapes[k]) for k in range(n_sh)]
    g_conv_w, g_lru_ba, g_lru_bx, g_lru_lam, g_pool_scale = g_sh
    rep_block = _to_rows([g_flat[offs[n_sh]:offs[-1]]], SUBLANES)
    rep_all, = _all_gather([rep_block], "gather_replicated")
    rep_flat = rep_all.reshape(N_DEV, -1)
    rep_full, off = [], 0
    for n in rep_sizes:
        rep_full.append(rep_flat[:, off:off + n].reshape(-1))
        off += n
    g_lru_wa = rep_full[0].reshape(lru_wa.shape)
    g_lru_wx = rep_full[1].reshape(lru_wx.shape)
    g_ln_g = rep_full[2].reshape(ln_g.shape)
    g_ln_b = rep_full[3].reshape(ln_b.shape)
    g_conv_b = rep_full[4].reshape(conv_b.shape)
    g_c_ctx = rep_full[5].reshape(c_ctx.shape)

    names = ["c_ctx", "w_mod", "b_mod", "w_in", "w_out", "ln_g", "ln_b", "conv_w", "conv_b", "lru_wa", "lru_ba",
             "lru_wx", "lru_bx", "lru_lam", "pool_w", "pool_scale"]
    weights = dict(c_ctx=c_ctx, w_mod=w_mod, b_mod=b_mod, w_in=w_in, w_out=w_out, ln_g=ln_g, ln_b=ln_b,
                   conv_w=conv_w, conv_b=conv_b, lru_wa=lru_wa, lru_ba=lru_ba, lru_wx=lru_wx, lru_bx=lru_bx,
                   lru_lam=lru_lam, pool_w=pool_w, pool_scale=pool_scale)
    mom_m = dict(c_ctx=m_c_ctx, w_mod=m_w_mod, b_mod=m_b_mod, w_in=m_w_in, w_out=m_w_out, ln_g=m_ln_g, ln_b=m_ln_b,
                 conv_w=m_conv_w, conv_b=m_conv_b, lru_wa=m_lru_wa, lru_ba=m_lru_ba, lru_wx=m_lru_wx,
                 lru_bx=m_lru_bx, lru_lam=m_lru_lam, pool_w=m_pool_w, pool_scale=m_pool_scale)
    mom_v = dict(c_ctx=v_c_ctx, w_mod=v_w_mod, b_mod=v_b_mod, w_in=v_w_in, w_out=v_w_out, ln_g=v_ln_g, ln_b=v_ln_b,
                 conv_w=v_conv_w, conv_b=v_conv_b, lru_wa=v_lru_wa, lru_ba=v_lru_ba, lru_wx=v_lru_wx,
                 lru_bx=v_lru_bx, lru_lam=v_lru_lam, pool_w=v_pool_w, pool_scale=v_pool_scale)
    grads = dict(c_ctx=g_c_ctx, w_mod=g_w_mod, b_mod=g_b_mod, ln_g=g_ln_g, ln_b=g_ln_b,
                 conv_w=g_conv_w, conv_b=g_conv_b, lru_wa=g_lru_wa, lru_ba=g_lru_ba, lru_wx=g_lru_wx,
                 lru_bx=g_lru_bx, lru_lam=g_lru_lam)
    grads["pool_scale"] = g_pool_scale
    delta, new_m, new_v = {}, {}, {}

    def update_parts(n, parts, view):
        res = _adamw_parts(weights[n].reshape(view), parts, mom_m[n].reshape(view), mom_v[n].reshape(view),
                           "adamw_" + n)
        grads[n], delta[n], new_m[n], new_v[n] = [r.reshape(weights[n].shape) for r in res]

    update_parts("w_in", [p_wi0, p_wi1], w_in.shape)
    update_parts("w_out", [p_wo0, p_wo1], w_out.shape)
    update_parts("pool_w", [p_pw], (1,) + p_pw.shape[1:])
    for n in ("w_mod", "lru_wa", "lru_wx"):
        shape = weights[n].shape
        view = (int(np.prod(shape[:-1])), shape[-1])
        res = _adamw(weights[n].reshape(view), grads[n].reshape(view), mom_m[n].reshape(view),
                     mom_v[n].reshape(view), "adamw_" + n)
        delta[n], new_m[n], new_v[n] = [r.reshape(shape) for r in res]

    small = [n for n in names if n not in delta]
    shapes = [weights[n].shape for n in small]
    flat = lambda d: _to_rows([d[n] for n in small], FLAT_ROWS)
    res = _adamw(flat(weights), flat(grads), flat(mom_m), flat(mom_v), "adamw_small")
    for d, r in zip((delta, new_m, new_v), res):
        d.update(zip(small, _split_rows(r, shapes)))

    return (loss, gx[None], *[grads[n] for n in names], *[delta[n] for n in names],
            *[new_m[n] for n in names], *[new_v[n] for n in names])
```

```python
import functools

import numpy as np
import jax
import jax.numpy as jnp
from jax import lax
from jax.experimental import pallas as pl
from jax.experimental.pallas import tpu as pltpu

F32 = jnp.float32
BF16 = jnp.bfloat16
MXU_DTYPE = BF16

D_MODEL = 1024
D_INNER = 2048
LRU_BLOCK = 128
GRID_W = 64
POOL_WINDOWS = (2, 4, 8, 16)
POOL_GROUP = 512
ALPHA = float(4 ** 0.25)
LN_EPS = 1e-5
LRU_C = 8.0
N_DEV = 8
N_WBLK = 8
WBLK = 512

ADAM_LR = 0.001
ADAM_B1 = 0.9
ADAM_B2 = 0.999
ADAM_EPS = 1e-08
ADAM_WD = 0.01
ADAM_STEP = 10

LANES = 128
SUBLANES = 8
V7X_VMEM_BYTES = 64 * 1024 * 1024
VMEM_LIMIT = V7X_VMEM_BYTES - 8 * 1024 * 1024
MESH = pl.DeviceIdType.MESH
ANY = pl.BlockSpec(memory_space=pl.ANY)

TM_MM = 512
TM_BWD = 256
TM_LRU = 1024
TM_LRU_FWD = 1024
CB_LRU = 512
N_SEG = 8
SCAN_UNROLL = 4
SCAN_ROW_T = 17
SCAN_ROW_J = 2
SQRT_FLOOR = 1e-30
FLAT_ROWS = 16
ELEMENTWISE_TILE_BYTES = 1 << 20
POOL_TOK = 256
WIRE_DTYPE = BF16


def _cparams(**kw):
    return pltpu.CompilerParams(vmem_limit_bytes=VMEM_LIMIT, **kw)


def _my_pos():
    return lax.axis_index("x"), lax.axis_index("y"), lax.axis_index("c")


def _dot(a, b):
    return jnp.dot(a.astype(MXU_DTYPE), b.astype(MXU_DTYPE), preferred_element_type=F32)


def _dot_tn(a, b):
    return lax.dot_general(a.astype(MXU_DTYPE), b.astype(MXU_DTYPE), (((0,), (0,)), ((), ())),
                           preferred_element_type=F32)


def _dot_nt(a, b):
    return lax.dot_general(a.astype(MXU_DTYPE), b.astype(MXU_DTYPE), (((1,), (1,)), ((), ())),
                           preferred_element_type=F32)


def _sigmoid(z):
    return 0.5 * jnp.tanh(0.5 * z) + 0.5


def _log_sigmoid(x):
    y = jnp.exp(-jnp.abs(x))
    u = 1.0 + y
    l1p = jnp.where(u == 1.0, y, jnp.log(u) * (y / jnp.where(u == 1.0, 1.0, u - 1.0)))
    return jnp.minimum(x, 0.0) - l1p


def _rowsum(v):
    return jnp.sum(v, axis=0, keepdims=True)


def _layer_norm_stats(z):
    mu = jnp.mean(z, axis=-1, keepdims=True)
    zc = z - mu
    var = jnp.mean(zc * zc, axis=-1, keepdims=True)
    rstd = lax.rsqrt(var + LN_EPS)
    return zc * rstd, rstd


def _layer_norm_bwd(dy, xhat, rstd, g):
    dxh = dy * g
    m1 = jnp.mean(dxh, axis=-1, keepdims=True)
    m2 = jnp.mean(dxh * xhat, axis=-1, keepdims=True)
    return rstd * (dxh - m1 - xhat * m2)


def _shifted(v, before8, after8, offsets):
    n = v.shape[0]
    ext = jnp.concatenate([before8, v, after8], axis=0)
    total = n + 2 * SUBLANES
    return [pltpu.roll(ext, (-k) % total, 0)[SUBLANES:SUBLANES + n] for k in offsets]


def _rows8(row):
    return jnp.broadcast_to(row, (SUBLANES, row.shape[1]))


def _shift_down(v, first_row):
    return _shifted(v, _rows8(first_row), _rows8(first_row), [-1])[0]


def _shift_up(v, last_row):
    return _shifted(v, _rows8(last_row), _rows8(last_row), [1])[0]


def _all_gather(blocks, name):
    n = len(blocks)

    def body(*refs):
        x_refs, out_refs = refs[:n], refs[n:2 * n]
        send_sems, recv_sems, local_sems = refs[2 * n:]
        x, y, c = _my_pos()
        me, sibling = (x, y, c), (x, y, 1 - c)
        chips = [(1 - x, y), (x, 1 - y), (1 - x, 1 - y)]

        def slot(a, px, py, pc):
            return out_refs[a].at[4 * px + 2 * py + pc]

        def copy(a, k, block, to, src=None):
            return pltpu.make_async_remote_copy(
                src_ref=slot(a, *block) if src is None else src, dst_ref=slot(a, *block),
                send_sem=send_sems.at[a, k], recv_sem=recv_sems.at[a, k], device_id=to, device_id_type=MESH)

        mine = [pltpu.make_async_copy(x_refs[a], slot(a, *me), local_sems.at[a]) for a in range(n)]
        for cp in mine:
            cp.start()
        first = []
        for a in range(n):
            first.append(copy(a, 0, me, sibling, src=x_refs[a]))
            first += [copy(a, 1 + j, me, (*chip, c), src=x_refs[a]) for j, chip in enumerate(chips)]
        for cp in first:
            cp.start()
        passed = []
        for j, chip in enumerate(chips):
            for a in range(n):
                copy(a, 1 + j, (*chip, c), me).wait_recv()
                fwd = copy(a, 4 + j, (*chip, c), sibling)
                fwd.start()
                passed.append(fwd)
        for a in range(n):
            copy(a, 0, sibling, me).wait_recv()
            for j, chip in enumerate(chips):
                copy(a, 4 + j, (*chip, 1 - c), me).wait_recv()
        for cp in first + passed:
            cp.wait_send()
        for cp in mine:
            cp.wait()

    outs = pl.pallas_call(
        body, name=name,
        out_shape=[jax.ShapeDtypeStruct((N_DEV,) + b.shape, b.dtype) for b in blocks],
        in_specs=[ANY] * n, out_specs=[ANY] * n,
        scratch_shapes=[pltpu.SemaphoreType.DMA((n, 7)), pltpu.SemaphoreType.DMA((n, 7)),
                        pltpu.SemaphoreType.DMA((n,))],
    )(*blocks)
    return list(outs)


def _sibling_exchange(bufs, name):
    n = len(bufs)

    def body(*refs):
        srcs, outs = refs[:n], refs[n:2 * n]
        send_sems, recv_sems = refs[2 * n:]
        x, y, c = _my_pos()
        copies = [pltpu.make_async_remote_copy(
            src_ref=srcs[a].at[2 * j + (1 - c)], dst_ref=outs[a].at[j], send_sem=send_sems.at[a, j],
            recv_sem=recv_sems.at[a, j], device_id=(x, y, 1 - c), device_id_type=MESH)
            for a in range(n) for j in range(4)]
        for cp in copies:
            cp.start()
        for cp in copies:
            cp.wait()

    outs = pl.pallas_call(
        body, name=name, out_shape=[jax.ShapeDtypeStruct((4,) + b.shape[1:], b.dtype) for b in bufs],
        in_specs=[ANY] * n, out_specs=[ANY] * n,
        scratch_shapes=[pltpu.SemaphoreType.DMA((n, 4)), pltpu.SemaphoreType.DMA((n, 4))],
    )(*bufs)
    return list(outs)


def _chip_exchange(parts, name):
    n = len(parts)

    def body(*refs):
        srcs, outs = refs[:n], refs[n:2 * n]
        send_sems, recv_sems, local_sems = refs[2 * n:]
        x, y, c = _my_pos()
        jme = 2 * x + y
        peers = [(1 - x, y), (x, 1 - y), (1 - x, 1 - y)]
        local = [pltpu.make_async_copy(srcs[a].at[jme], outs[a].at[jme], local_sems.at[a]) for a in range(n)]
        for cp in local:
            cp.start()

        def copy(a, k, px, py, dst_slot):
            return pltpu.make_async_remote_copy(
                src_ref=srcs[a].at[2 * px + py], dst_ref=outs[a].at[dst_slot], send_sem=send_sems.at[a, k],
                recv_sem=recv_sems.at[a, k], device_id=(px, py, c), device_id_type=MESH)

        sends = [copy(a, k, px, py, jme) for a in range(n) for k, (px, py) in enumerate(peers)]
        for cp in sends:
            cp.start()
        for a in range(n):
            for k, (px, py) in enumerate(peers):
                copy(a, k, px, py, 2 * px + py).wait_recv()
        for cp in sends:
            cp.wait_send()
        for cp in local:
            cp.wait()

    outs = pl.pallas_call(
        body, name=name, out_shape=[jax.ShapeDtypeStruct(p.shape, p.dtype) for p in parts],
        in_specs=[ANY] * n, out_specs=[ANY] * n,
        scratch_shapes=[pltpu.SemaphoreType.DMA((n, 3)), pltpu.SemaphoreType.DMA((n, 3)),
                        pltpu.SemaphoreType.DMA((n,))],
    )(*parts)
    return list(outs)


_SIDE_REMOTE = {"gather": 7, "sibling": 4, "chips": 3}
_FLIPS = [(0, 0, 1), (1, 0, 0), (0, 1, 0), (1, 1, 0), (1, 0, 1), (0, 1, 1), (1, 1, 1)]


def _side_plan(sides):
    inputs, out_shapes, scratch = [], [], []
    for kind, arrays in sides:
        n = len(arrays)
        for a in arrays:
            inputs.append(a)
            shape = {"gather": (N_DEV,) + a.shape, "sibling": (4,) + a.shape[1:], "chips": a.shape}[kind]
            out_shapes.append(jax.ShapeDtypeStruct(shape, a.dtype))
        scratch += [pltpu.SemaphoreType.DMA((n, _SIDE_REMOTE[kind])), pltpu.SemaphoreType.DMA((n, _SIDE_REMOTE[kind])),
                    pltpu.SemaphoreType.DMA((n,))]
    return inputs, out_shapes, scratch


def _side_copies(sides, in_refs, out_refs, sem_refs):
    x, y, c = _my_pos()
    starts, waits = [], []
    pos = 0
    for s, (kind, arrays) in enumerate(sides):
        send_sems, recv_sems, local_sems = sem_refs[3 * s:3 * s + 3]
        for a in range(len(arrays)):
            src, out = in_refs[pos], out_refs[pos]
            pos += 1

            def remote(k, src_ref, dst_ref, to):
                return pltpu.make_async_remote_copy(src_ref=src_ref, dst_ref=dst_ref, send_sem=send_sems.at[a, k],
                                                    recv_sem=recv_sems.at[a, k], device_id=to, device_id_type=MESH)

            def local(src_ref, dst_ref):
                cp = pltpu.make_async_copy(src_ref, dst_ref, local_sems.at[a])
                starts.append(cp.start)
                waits.append(cp.wait)

            if kind == "gather":
                me = 4 * x + 2 * y + c
                local(src, out.at[me])
                for k, (fx, fy, fc) in enumerate(_FLIPS):
                    px, py, pc = (1 - x if fx else x), (1 - y if fy else y), (1 - c if fc else c)
                    send = remote(k, src, out.at[me], (px, py, pc))
                    starts.append(send.start)
                    waits += [remote(k, src, out.at[4 * px + 2 * py + pc], (px, py, pc)).wait_recv, send.wait_send]
            elif kind == "sibling":
                for j in range(4):
                    cp = remote(j, src.at[2 * j + (1 - c)], out.at[j], (x, y, 1 - c))
                    starts.append(cp.start)
                    waits.append(cp.wait)
            else:
                jme = 2 * x + y
                local(src.at[jme], out.at[jme])
                for k, (px, py) in enumerate([(1 - x, y), (x, 1 - y), (1 - x, 1 - y)]):
                    send = remote(k, src.at[2 * px + py], out.at[jme], (px, py, c))
                    starts.append(send.start)
                    waits += [remote(k, src.at[2 * px + py], out.at[2 * px + py], (px, py, c)).wait_recv,
                              send.wait_send]
    return starts, waits


def _call_with_sides(body, sides, *, name, grid, in_specs, out_specs, out_shape, scratch_shapes, compiler_params, args):
    if not sides:
        res = pl.pallas_call(body, name=name, grid=grid, in_specs=in_specs, out_specs=out_specs, out_shape=out_shape,
                             scratch_shapes=scratch_shapes, compiler_params=compiler_params)(*args)
        return list(res), []
    s_in, s_out, s_scr = _side_plan(sides)
    n_in, n_out, n_scr, n_side = len(in_specs), len(out_specs), len(scratch_shapes), len(s_in)

    def wrapped(*refs):
        refs = list(refs)
        ins, side_in = refs[:n_in], refs[n_in:n_in + n_side]
        outs = refs[n_in + n_side:n_in + n_side + n_out]
        side_out = refs[n_in + n_side + n_out:n_in + 2 * n_side + n_out]
        rest = refs[n_in + 2 * n_side + n_out:]
        starts, waits = _side_copies(sides, side_in, side_out, rest[n_scr:])
        first = functools.reduce(jnp.logical_and, [pl.program_id(d) == 0 for d in range(len(grid))])
        last = functools.reduce(jnp.logical_and, [pl.program_id(d) == grid[d] - 1 for d in range(len(grid))])

        @pl.when(first)
        def _():
            for start in starts:
                start()

        body(*ins, *outs, *rest[:n_scr])

        @pl.when(last)
        def _():
            for wait in waits:
                wait()

    res = pl.pallas_call(
        wrapped, name=name, grid=grid, in_specs=list(in_specs) + [ANY] * n_side,
        out_specs=list(out_specs) + [ANY] * n_side, out_shape=list(out_shape) + s_out,
        scratch_shapes=list(scratch_shapes) + s_scr, compiler_params=compiler_params,
    )(*args, *s_in)
    return list(res[:n_out]), list(res[n_out:])


def _row_tile(r, l):
    t = min(r, max(16, ELEMENTWISE_TILE_BYTES // (4 * l) // 16 * 16))
    while r % t:
        t -= 16
    return t


def _pair_sum(buf, recv, core, name):
    _, r, l = buf.shape
    tr = _row_tile(r, l)

    def body(core_ref, a_ref, b_ref, o_ref):
        o_ref[...] = (a_ref[...] + b_ref[...]).astype(WIRE_DTYPE)

    return pl.pallas_call(
        body, name=name, out_shape=jax.ShapeDtypeStruct((4, r, l), WIRE_DTYPE),
        grid_spec=pltpu.PrefetchScalarGridSpec(
            num_scalar_prefetch=1, grid=(4, r // tr),
            in_specs=[pl.BlockSpec((None, tr, l), lambda j, i, cr: (2 * j + cr[0], i, 0)),
                      pl.BlockSpec((None, tr, l), lambda j, i, cr: (j, i, 0))],
            out_specs=pl.BlockSpec((None, tr, l), lambda j, i, cr: (j, i, 0))),
        compiler_params=_cparams(dimension_semantics=("arbitrary", "arbitrary")),
    )(core, buf, recv)


def _sum_parts(p_ref):
    return ((p_ref[0].astype(F32) + p_ref[1].astype(F32)) + (p_ref[2].astype(F32) + p_ref[3].astype(F32)))


def _sum4(parts, name):
    _, r, l = parts.shape
    tr = _row_tile(r, l)

    def body(p_ref, o_ref):
        o_ref[...] = _sum_parts(p_ref)

    return pl.pallas_call(
        body, name=name, out_shape=jax.ShapeDtypeStruct((r, l), F32), grid=(r // tr,),
        in_specs=[pl.BlockSpec((4, tr, l), lambda i: (0, i, 0))],
        out_specs=pl.BlockSpec((tr, l), lambda i: (i, 0)),
        compiler_params=_cparams(dimension_semantics=("arbitrary",)),
    )(parts)


def _adamw_update(w, gg, m, v):
    nm = ADAM_B1 * m + (1.0 - ADAM_B1) * gg
    nv = ADAM_B2 * v + (1.0 - ADAM_B2) * (gg * gg)
    m_hat = nm / (1.0 - ADAM_B1 ** ADAM_STEP)
    v_hat = nv / (1.0 - ADAM_B2 ** ADAM_STEP)
    return -ADAM_LR * (m_hat / (jnp.sqrt(v_hat) + ADAM_EPS) + ADAM_WD * w), nm, nv


def _adamw(w, g, m, v, name):
    r, l = w.shape
    tr = _row_tile(r, l)

    def body(w_ref, g_ref, m_ref, v_ref, d_ref, nm_ref, nv_ref):
        d_ref[...], nm_ref[...], nv_ref[...] = _adamw_update(w_ref[...], g_ref[...], m_ref[...], v_ref[...])

    spec = pl.BlockSpec((tr, l), lambda i: (i, 0))
    return pl.pallas_call(
        body, name=name, out_shape=[jax.ShapeDtypeStruct((r, l), F32)] * 3, grid=(r // tr,),
        in_specs=[spec] * 4, out_specs=[spec] * 3,
        compiler_params=_cparams(dimension_semantics=("arbitrary",)),
    )(w, g, m, v)


def _adamw_parts(w, parts, m, v, name):
    nl, r, l = w.shape
    tr = _row_tile(r, l)

    def body(*refs):
        w_ref, p_refs, (m_ref, v_ref, g_ref, d_ref, nm_ref, nv_ref) = refs[0], refs[1:1 + nl], refs[1 + nl:]
        layer = pl.program_id(0)
        gg = _sum_parts(p_refs[0])
        for q in range(1, nl):
            gg = jnp.where(layer == q, _sum_parts(p_refs[q]), gg)
        g_ref[...] = gg
        d_ref[...], nm_ref[...], nv_ref[...] = _adamw_update(w_ref[...], gg, m_ref[...], v_ref[...])

    spec = pl.BlockSpec((None, tr, l), lambda q, i: (q, i, 0))
    pspecs = [pl.BlockSpec((4, tr, l), lambda q, i, k=k: (0, jnp.where(q == k, i, 0), 0)) for k in range(nl)]
    return pl.pallas_call(
        body, name=name, out_shape=[jax.ShapeDtypeStruct((nl, r, l), F32)] * 4, grid=(nl, r // tr),
        in_specs=[spec] + pspecs + [spec, spec], out_specs=[spec] * 4,
        compiler_params=_cparams(dimension_semantics=("arbitrary", "arbitrary")),
    )(w, *parts, m, v)


def _to_rows(pieces, row_multiple):
    flat = jnp.concatenate([p.reshape(-1) for p in pieces])
    rows = -(-flat.shape[0] // LANES)
    rows = -(-rows // row_multiple) * row_multiple
    flat = jnp.pad(flat, (0, rows * LANES - flat.shape[0]))
    return flat.reshape(rows, LANES)


def _split_rows(rows, shapes):
    flat = rows.reshape(-1)
    out, off = [], 0
    for s in shapes:
        n = int(np.prod(s))
        out.append(flat[off:off + n].reshape(s))
        off += n
    return out


def _mod_fwd(cond, w_mod, b_my, name):
    nl, _, ncol = w_mod.shape

    def body(a_ref, w_ref, b_ref, o_ref):
        a = a_ref[...]
        s = a * _sigmoid(a)
        for i in range(nl):
            o_ref[i] = _dot(s, w_ref[i]) + b_ref[i]

    return pl.pallas_call(
        body, name=name, out_shape=jax.ShapeDtypeStruct((nl, 16, ncol), F32),
        compiler_params=_cparams(),
    )(cond, w_mod, b_my)


def _mod_bwd(cond, dm_all, dm_my, w_mod, name):
    nl, _, ncol = w_mod.shape

    def body(a_ref, dma_ref, dmm_ref, w_ref, gw_ref, gb_ref, gc_ref):
        a = a_ref[...]
        sg = _sigmoid(a)
        s = a * sg
        for i in range(nl):
            gw_ref[i] = _dot_tn(s, dmm_ref[i])
            gb_ref[i] = jnp.sum(dma_ref[i], axis=0, keepdims=True)
        back = _dot_nt(dmm_ref[0], w_ref[0])
        dsilu = sg * (1.0 + a * (1.0 - sg))
        gc_ref[...] = jnp.sum(back[8:16] * dsilu[8:16], axis=0, keepdims=True)

    return pl.pallas_call(
        body, name=name,
        out_shape=[jax.ShapeDtypeStruct((nl, D_MODEL, ncol), F32), jax.ShapeDtypeStruct((nl, 1, 3 * D_MODEL), F32),
                   jax.ShapeDtypeStruct((1, D_MODEL), F32)],
        compiler_params=_cparams(),
    )(cond, dm_all, dm_my, w_mod)


def _in_proj(xt, sc, sh, wg, name, sides=()):
    t = xt.shape[0]
    tm = min(TM_MM, t)

    def body(x_ref, sc_ref, sh_ref, w_ref, u_ref, g_ref):
        h = (x_ref[...] * (1.0 + sc_ref[...]) + sh_ref[...]).astype(MXU_DTYPE)
        for k in range(N_WBLK):
            o = jnp.dot(h, w_ref[k], preferred_element_type=F32)
            if k < N_WBLK // 2:
                u_ref[:, k * WBLK:(k + 1) * WBLK] = o
            else:
                kk = k - N_WBLK // 2
                g_ref[:, kk * WBLK:(kk + 1) * WBLK] = o

    row = pl.BlockSpec((1, D_MODEL), lambda i: (0, 0))
    return _call_with_sides(
        body, sides, name=name, out_shape=[jax.ShapeDtypeStruct((t, D_INNER), F32)] * 2, grid=(t // tm,),
        in_specs=[pl.BlockSpec((tm, D_MODEL), lambda i: (i, 0)), row, row,
                  pl.BlockSpec((N_WBLK, D_MODEL, WBLK), lambda i: (0, 0, 0), pipeline_mode=pl.Buffered(1))],
        out_specs=[pl.BlockSpec((tm, D_INNER), lambda i: (i, 0))] * 2, scratch_shapes=[],
        compiler_params=_cparams(dimension_semantics=("arbitrary",)), args=[xt, sc, sh, wg])


def _halo_maps(nt, tm, n_rows8, pos):
    per = tm // SUBLANES
    prev = lambda cb, i: (jnp.maximum(pos(i) * per - 1, 0), cb)
    nxt = lambda cb, i: (jnp.minimum((pos(i) + 1) * per, n_rows8 - 1), cb)
    return prev, nxt


def _conv_taps(u, prev8, next8, is_first, is_last):
    pz = jnp.where(is_first, 0.0, 1.0)
    nz = jnp.where(is_last, 0.0, 1.0)
    return _shifted(u, prev8 * pz, next8 * nz, [-2, -1, 1])


def _lru_gates(uv, wa_ref, wx_ref, ba, bx, cl, g):
    sl = slice(g * LANES, (g + 1) * LANES)
    uvg = uv[:, sl]
    r = _sigmoid(_dot(uvg, wa_ref[g]) + ba[:, sl])
    ii = _sigmoid(_dot(uvg, wx_ref[g]) + bx[:, sl])
    la = cl[:, sl] * r
    a = jnp.exp(la)
    q = jnp.tanh(-la) * (1.0 + a * a)
    rs = lax.rsqrt(jnp.maximum(q, SQRT_FLOOR))
    return uvg, r, ii, a, q * rs, rs


def _scan_rows(seg):
    return -(-(SCAN_ROW_T * (seg - 1) + SCAN_ROW_J * (N_SEG - 1) + 1) // SUBLANES) * SUBLANES


def _seg_chunk(j, c):
    return pl.ds(SCAN_ROW_T * SUBLANES * c + SCAN_ROW_J * j, SUBLANES, stride=SCAN_ROW_T)


def _seg_scatter(ref, g, seg, value):
    for j in range(N_SEG):
        for c in range(seg // SUBLANES):
            r0 = j * seg + SUBLANES * c
            ref[g, _seg_chunk(j, c), :] = value[r0:r0 + SUBLANES]


def _scan_tile(a_s, b_s, carry_ref, write_out, seg, reverse):
    n_g = a_s.shape[0]
    unroll = SCAN_UNROLL if seg % SCAN_UNROLL == 0 else 1

    n_trips = seg // unroll

    def steps(k, state):
        hs, cs = list(state[0]), list(state[1])
        base = ((n_trips - 1 - k) if reverse else k) * unroll
        for q in (range(unroll - 1, -1, -1) if reverse else range(unroll)):
            t = base + q
            rows = pl.ds(t * SCAN_ROW_T, N_SEG, stride=SCAN_ROW_J)
            for g in range(n_g):
                a = a_s[g, rows, :]
                b = b_s[g, rows, :]
                hs[g] = a * hs[g] + b
                cs[g] = a * cs[g]
                b_s[g, rows, :] = hs[g]
                a_s[g, rows, :] = cs[g]
        return tuple(hs), tuple(cs)

    zeros = tuple(jnp.zeros((N_SEG, LANES), F32) for _ in range(n_g))
    ones = tuple(jnp.ones((N_SEG, LANES), F32) for _ in range(n_g))
    h_fin, a_fin = lax.fori_loop(0, seg // unroll, steps, (zeros, ones))

    order = list(range(N_SEG - 1, -1, -1)) if reverse else list(range(N_SEG))
    for g in range(n_g):
        carry = carry_ref[:, g * LANES:(g + 1) * LANES]
        for j in order:
            for c in range(seg // SUBLANES):
                rows = _seg_chunk(j, c)
                write_out(j, c, g, b_s[g, rows, :] + a_s[g, rows, :] * carry)
            carry = a_fin[g][j:j + 1] * carry + h_fin[g][j:j + 1]
        carry_ref[:, g * LANES:(g + 1) * LANES] = carry


def _lru_specs(s, tm, cb, direction_pos, nt):
    n_rows8 = s // SUBLANES
    prev, nxt = _halo_maps(nt, tm, n_rows8, direction_pos)
    tile = pl.BlockSpec((tm, cb), lambda c, i: (direction_pos(i), c))
    return tile, pl.BlockSpec((SUBLANES, cb), prev), pl.BlockSpec((SUBLANES, cb), nxt)


def _lru_param_specs(cb, d):
    n_g = cb // LANES
    vec = pl.BlockSpec((1, cb), lambda c, i: (0, c))
    dvec = pl.BlockSpec((None, 1, cb), lambda c, i: (d, 0, c))
    wmat = pl.BlockSpec((None, n_g, LRU_BLOCK, LRU_BLOCK), lambda c, i: (d, c, 0, 0))
    return vec, dvec, wmat


def _lru_fwd(src, h0, p, d, name, conv, sides=()):
    s = src.shape[0]
    tm = min(TM_LRU_FWD, s)
    cb = CB_LRU
    n_g = cb // LANES
    nt = s // tm
    seg = tm // N_SEG
    pos = (lambda i: i) if d == 0 else (lambda i: nt - 1 - i)

    def body(*refs):
        refs = list(refs)
        u_ref = refs.pop(0)
        if conv:
            up_ref, un_ref, cw_ref, cbias_ref = [refs.pop(0) for _ in range(4)]
        wa_ref, wx_ref, ba_ref, bx_ref, lam_ref, h0_ref, h_ref, hc_ref = [refs.pop(0) for _ in range(8)]
        uv_ref = refs.pop(0) if conv else None
        a_s, b_s = refs
        i = pl.program_id(1)
        tp = pos(i)

        @pl.when(i == 0)
        def _():
            hc_ref[...] = h0_ref[...]

        if conv:
            u_t = u_ref[...]
            um2, um1, up1 = _conv_taps(u_t, up_ref[...], un_ref[...], tp == 0, tp == nt - 1)
            cw = cw_ref[...]
            uv_ref[...] = um2 * cw[0:1] + um1 * cw[1:2] + u_t * cw[2:3] + up1 * cw[3:4] + cbias_ref[...]
        src_ref = uv_ref if conv else u_ref
        cl = LRU_C * _log_sigmoid(lam_ref[...])
        ba, bx = ba_ref[...], bx_ref[...]
        for g in range(n_g):
            uvg, r, ii, a, sq, _ = _lru_gates(src_ref, wa_ref, wx_ref, ba, bx, cl, g)
            b = sq * (ii * uvg)
            _seg_scatter(a_s, g, seg, a)
            _seg_scatter(b_s, g, seg, b)

        def write_out(j, c, g, h):
            h_ref[pl.ds(j * seg + SUBLANES * c, SUBLANES), pl.ds(g * LANES, LANES)] = h

        _scan_tile(a_s, b_s, hc_ref, write_out, seg, reverse=(d == 1))

    tile, prev, nxt = _lru_specs(s, tm, cb, pos, nt)
    vec, dvec, wmat = _lru_param_specs(cb, d)
    wide = jax.ShapeDtypeStruct((s, D_INNER), F32)
    conv_specs = [prev, nxt, pl.BlockSpec((4, cb), lambda c, i: (0, c)), vec] if conv else []
    conv_args = [src, src, p["conv_w"], p["conv_b"]] if conv else []
    return _call_with_sides(
        body, sides, name=name,
        out_shape=[wide, jax.ShapeDtypeStruct((1, D_INNER), F32)] + ([wide] if conv else []),
        grid=(D_INNER // cb, nt),
        in_specs=[tile] + conv_specs + [wmat, wmat, dvec, dvec, dvec, vec],
        out_specs=[tile, vec] + ([tile] if conv else []),
        scratch_shapes=[pltpu.VMEM((n_g, _scan_rows(seg), LANES), F32)] * 2,
        compiler_params=_cparams(dimension_semantics=("arbitrary", "arbitrary")),
        args=[src, *conv_args, p["wa"], p["wx"], p["ba"], p["bx"], p["lam"], h0])


def _lru_bwd(uv, dh, h, h0, lam_in, p, d, name, sides=()):
    s = uv.shape[0]
    tm = min(TM_LRU, s)
    cb = CB_LRU
    n_g = cb // LANES
    nt = s // tm
    seg = tm // N_SEG
    pos = (lambda i: nt - 1 - i) if d == 0 else (lambda i: i)

    def body(uv_ref, dh_ref, h_ref, hh_ref, wa_ref, wx_ref, ba_ref, bx_ref,
             lam_ref, h0_ref, lin_ref, duv_ref, gwa_ref, gwx_ref, gv_ref, lc_ref, a_s, b_s, lp_s,
             r_s, i_s, q_s, rq_s, a_keep):
        i = pl.program_id(1)
        tp = pos(i)

        @pl.when(i == 0)
        def _():
            lc_ref[...] = lin_ref[...]
            gwa_ref[...] = jnp.zeros_like(gwa_ref)
            gwx_ref[...] = jnp.zeros_like(gwx_ref)
            gv_ref[...] = jnp.zeros_like(gv_ref)

        uv = uv_ref[...]
        lam = lam_ref[...]
        cl = LRU_C * _log_sigmoid(lam)
        ba, bx = ba_ref[...], bx_ref[...]
        dh_t = dh_ref[...]
        carry_in = lc_ref[...]
        for g in range(n_g):
            sl = slice(g * LANES, (g + 1) * LANES)
            _, r, ii, a, sq, rs = _lru_gates(uv, wa_ref, wx_ref, ba, bx, cl, g)
            r_s[:, sl], i_s[:, sl], q_s[:, sl], rq_s[:, sl], a_keep[:, sl] = r, ii, sq, rs, a
            b = a * dh_t[:, sl]
            _seg_scatter(a_s, g, seg, a)
            _seg_scatter(b_s, g, seg, b)

        def write_out(j, c, g, v):
            lp_s[pl.ds(j * seg + SUBLANES * c, SUBLANES), pl.ds(g * LANES, LANES)] = v

        _scan_tile(a_s, b_s, lc_ref, write_out, seg, reverse=(d == 0))

        h_t = h_ref[...]
        hh = hh_ref[...]
        if d == 0:
            edge = jnp.where(tp == 0, h0_ref[...], hh[7:8])
            h_prev = _shift_down(h_t, edge)
            lam_t = dh_t + _shift_up(lp_s[...], carry_in)
        else:
            edge = jnp.where(tp == nt - 1, h0_ref[...], hh[0:1])
            h_prev = _shift_up(h_t, edge)
            lam_t = dh_t + _shift_down(lp_s[...], carry_in)

        dsig = LRU_C * _sigmoid(-lam)
        for g in range(n_g):
            sl = slice(g * LANES, (g + 1) * LANES)
            uvg, r, ii, a, sq = uv[:, sl], r_s[:, sl], i_s[:, sl], a_keep[:, sl], q_s[:, sl]
            lt = lam_t[:, sl]
            ls = lt * sq
            dla = (lt * a) * (h_prev[:, sl] - (ii * uvg) * (a * rq_s[:, sl]))
            dzr = (dla * cl[:, sl]) * r * (1.0 - r)
            dzi = (ls * uvg) * ii * (1.0 - ii)
            duv_ref[:, sl] = ls * ii + _dot_nt(dzr, wa_ref[g]) + _dot_nt(dzi, wx_ref[g])
            gwa_ref[g] += _dot_tn(uvg, dzr)
            gwx_ref[g] += _dot_tn(uvg, dzi)
            gv_ref[0:1, sl] += _rowsum(dzr)
            gv_ref[1:2, sl] += _rowsum(dzi)
            gv_ref[2:3, sl] += _rowsum(dla * r) * dsig[:, sl]

    tile, prev, nxt = _lru_specs(s, tm, cb, pos, nt)
    vec, dvec, wmat = _lru_param_specs(cb, d)
    hh_spec = prev if d == 0 else nxt
    gw_spec = pl.BlockSpec((n_g, LRU_BLOCK, LRU_BLOCK), lambda c, i: (c, 0, 0))
    n_blk = D_INNER // LRU_BLOCK
    return _call_with_sides(
        body, sides, name=name,
        out_shape=[jax.ShapeDtypeStruct((s, D_INNER), F32),
                   jax.ShapeDtypeStruct((n_blk, LRU_BLOCK, LRU_BLOCK), F32),
                   jax.ShapeDtypeStruct((n_blk, LRU_BLOCK, LRU_BLOCK), F32),
                   jax.ShapeDtypeStruct((SUBLANES, D_INNER), F32),
                   jax.ShapeDtypeStruct((1, D_INNER), F32)],
        grid=(D_INNER // cb, nt),
        in_specs=[tile, tile, tile, hh_spec, wmat, wmat, dvec, dvec, dvec, vec, vec],
        out_specs=[tile, gw_spec, gw_spec, pl.BlockSpec((SUBLANES, cb), lambda c, i: (0, c)), vec],
        scratch_shapes=[pltpu.VMEM((n_g, _scan_rows(seg), LANES), F32)] * 2 + [pltpu.VMEM((tm, cb), F32)] * 6,
        compiler_params=_cparams(dimension_semantics=("arbitrary", "arbitrary")),
        args=[uv, dh, h, h, p["wa"], p["wx"], p["ba"], p["bx"], p["lam"], h0, lam_in])


def _out0(hf, hb, g, xt, gt, wo, lg, lb, name):
    t = xt.shape[0]
    tm = min(TM_MM, t)

    def body(hf_ref, hb_ref, g_ref, x_ref, gt_ref, w_ref, lg_ref, lb_ref, x1_ref, br_ref):
        br = None
        for k in range(D_INNER // WBLK):
            sl = slice(k * WBLK, (k + 1) * WBLK)
            gg = g_ref[:, sl]
            p = (hf_ref[:, sl] + hb_ref[:, sl]) * (gg * _sigmoid(gg))
            part = _dot(p, w_ref[sl, :])
            br = part if br is None else br + part
        z = ALPHA * x_ref[...] + gt_ref[...] * br
        xhat, _ = _layer_norm_stats(z)
        x1_ref[...] = xhat * lg_ref[...] + lb_ref[...]
        br_ref[...] = br

    wide = pl.BlockSpec((tm, D_INNER), lambda i: (i, 0))
    nar = pl.BlockSpec((tm, D_MODEL), lambda i: (i, 0))
    row = pl.BlockSpec((1, D_MODEL), lambda i: (0, 0))
    return pl.pallas_call(
        body, name=name, out_shape=[jax.ShapeDtypeStruct((t, D_MODEL), F32)] * 2, grid=(t // tm,),
        in_specs=[wide, wide, wide, nar, row,
                  pl.BlockSpec((D_INNER, D_MODEL), lambda i: (0, 0), pipeline_mode=pl.Buffered(1)), row, row],
        out_specs=[nar, nar],
        compiler_params=_cparams(dimension_semantics=("arbitrary",)),
    )(hf, hb, g, xt, gt, wo, lg, lb)


def _unrolled_loop(n, fn, unroll=4):
    while n % unroll:
        unroll //= 2

    def trip(k, carry):
        for q in range(unroll):
            fn(k * unroll + q)
        return carry
    lax.fori_loop(0, n // unroll, trip, 0)


def _window(n, w):
    t = np.arange(n)
    return np.clip(t - w // 2, 0, n), np.clip(t + w // 2, 0, n)


def _pool_tables(n_rows, transpose):
    boxes, inv_c, inv_r = [], [], []
    for w in POOL_WINDOWS:
        lo, hi = _window(GRID_W, w)
        m = np.zeros((GRID_W, GRID_W), np.float32)
        for r in range(GRID_W):
            m[r, lo[r]:hi[r]] = 1.0
        m = np.kron(np.eye(POOL_TOK // GRID_W, dtype=np.float32), m)
        boxes.append(m.T if transpose else m)
        inv_c.append(np.broadcast_to((1.0 / (hi - lo).astype(np.float32))[:, None], (GRID_W, LANES)))
        lo_r, hi_r = _window(n_rows, w)
        inv_r.append(1.0 / (hi_r - lo_r).astype(np.float32))
    return (jnp.asarray(np.stack(boxes), MXU_DTYPE), jnp.asarray(np.stack(inv_c), F32),
            jnp.asarray(np.stack(inv_r), F32))


def _pool_mix(xin, transpose, out_dtype, name):
    s = xin.shape[0]
    n_rows = s // GRID_W
    pad_t = SUBLANES * GRID_W
    rows_per_blk = POOL_TOK // GRID_W
    n_slab = D_INNER // LANES
    slabs_per_group = POOL_GROUP // LANES
    n_win = len(POOL_WINDOWS)
    boxes, inv_c, inv_r = _pool_tables(n_rows, transpose)

    def body(invr_ref, box_ref, invc_ref, x_ref, o_ref, pad_s):
        k = pl.program_id(0) // slabs_per_group
        pad_s[pl.ds(0, pad_t), :] = jnp.zeros((pad_t, LANES), F32)
        pad_s[pl.ds(pad_t + s, pad_t), :] = jnp.zeros((pad_t, LANES), F32)

        for kk, w in enumerate(POOL_WINDOWS):
            half = w // 2
            offsets = list(range(-(half - 1), half + 1)) if transpose else list(range(-half, half))

            @pl.when(k == kk)
            def _():
                inv_col = invc_ref[kk]

                def col_box(b):
                    st = pl.multiple_of(b * POOL_TOK, POOL_TOK)
                    xb = x_ref[pl.ds(st, POOL_TOK), :]
                    if transpose:
                        xb = xb * jnp.concatenate(
                            [inv_col * invr_ref[kk, b * rows_per_blk + q] for q in range(rows_per_blk)], axis=0)
                    hi = xb.astype(MXU_DTYPE)
                    lo = (xb - hi.astype(F32)).astype(MXU_DTYPE)
                    both = jnp.dot(box_ref[kk], jnp.concatenate([hi, lo], axis=1), preferred_element_type=F32)
                    pad_s[pl.ds(pad_t + st, POOL_TOK), :] = both[:, :LANES] + both[:, LANES:]
                _unrolled_loop(s // POOL_TOK, col_box)

                def row_box(r):
                    st = pl.multiple_of(r * GRID_W, GRID_W)
                    acc = pad_s[pl.ds(pad_t + st + offsets[0] * GRID_W, GRID_W), :]
                    for o in offsets[1:]:
                        acc = acc + pad_s[pl.ds(pad_t + st + o * GRID_W, GRID_W), :]
                    if not transpose:
                        acc = acc * (inv_col * invr_ref[kk, r])
                    o_ref[pl.ds(st, GRID_W), :] = (acc - x_ref[pl.ds(st, GRID_W), :]).astype(out_dtype)
                _unrolled_loop(n_rows, row_box)

    slab = pl.BlockSpec((s, LANES), lambda i: (0, i))
    return pl.pallas_call(
        body, name=name, out_shape=jax.ShapeDtypeStruct((s, D_INNER), out_dtype), grid=(n_slab,),
        in_specs=[pl.BlockSpec(memory_space=pltpu.SMEM),
                  pl.BlockSpec((n_win, POOL_TOK, POOL_TOK), lambda i: (0, 0, 0)),
                  pl.BlockSpec((n_win, GRID_W, LANES), lambda i: (0, 0, 0)), slab],
        out_specs=slab,
        scratch_shapes=[pltpu.VMEM((s + 2 * pad_t, LANES), F32)],
        compiler_params=_cparams(dimension_semantics=("arbitrary",)),
    )(inv_r, boxes, inv_c, xin)


def _out1(dmix, pw, ps, g, x1, gt, wo, lg, lb, tgt, name):
    t = x1.shape[0]
    tm = min(TM_MM, t)
    n_grp = len(POOL_WINDOWS)

    def body(d_ref, pw_ref, ps_ref, g_ref, x1_ref, gt_ref, w_ref, lg_ref, lb_ref, tgt_ref, dz_ref, st_ref):
        @pl.when(pl.program_id(0) == 0)
        def _():
            st_ref[...] = jnp.zeros_like(st_ref)

        br = jnp.zeros((tm, D_MODEL), F32)
        for k in range(n_grp):
            sl = slice(k * POOL_GROUP, (k + 1) * POOL_GROUP)
            y = jnp.dot(d_ref[:, sl], pw_ref[k], preferred_element_type=F32) * ps_ref[:, sl]
            gg = g_ref[:, sl]
            br = br + _dot(y * (gg * _sigmoid(gg)), w_ref[sl, :])
        z = ALPHA * x1_ref[...] + gt_ref[...] * br
        xhat, rstd = _layer_norm_stats(z)
        lg_v = lg_ref[...]
        err = xhat * lg_v + lb_ref[...] - tgt_ref[...]
        dy = err * (1.0 / D_MODEL)
        dz = _layer_norm_bwd(dy, xhat, rstd, lg_v)
        dz_ref[...] = dz
        st_ref[0:1, :] += _rowsum(dy * xhat)
        st_ref[1:2, :] += _rowsum(dy)
        st_ref[2:3, :] += _rowsum(dz * br)
        st_ref[3:4, :] += _rowsum(err * err)

    wide = pl.BlockSpec((tm, D_INNER), lambda i: (i, 0))
    nar = pl.BlockSpec((tm, D_MODEL), lambda i: (i, 0))
    row = pl.BlockSpec((1, D_MODEL), lambda i: (0, 0))
    return pl.pallas_call(
        body, name=name,
        out_shape=[jax.ShapeDtypeStruct((t, D_MODEL), F32), jax.ShapeDtypeStruct((SUBLANES, D_MODEL), F32)],
        grid=(t // tm,),
        in_specs=[wide, pl.BlockSpec((n_grp, POOL_GROUP, POOL_GROUP), lambda i: (0, 0, 0)),
                  pl.BlockSpec((1, D_INNER), lambda i: (0, 0)), wide, nar, row,
                  pl.BlockSpec((D_INNER, D_MODEL), lambda i: (0, 0), pipeline_mode=pl.Buffered(1)), row, row, nar],
        out_specs=[nar, pl.BlockSpec((SUBLANES, D_MODEL), lambda i: (0, 0))],
        compiler_params=_cparams(dimension_semantics=("arbitrary",)),
    )(dmix, pw, ps, g, x1, gt, wo, lg, lb, tgt)


def _flush(acc, out_hbm, sem):
    cp = pltpu.make_async_copy(acc, out_hbm, sem)
    cp.start()
    cp.wait()


def _bout1(dz, dmix, g, pw, ps, gt, wo, name):
    t = dz.shape[0]
    tm = min(TM_MM, t)
    nt = t // tm
    n_grp = len(POOL_WINDOWS)

    def body(dz_ref, d_ref, g_ref, pw_ref, ps_ref, gt_ref, w_ref, dd_ref, dg_ref, gwo_hbm, gpw_hbm, gps_ref,
             gwo_acc, gpw_acc, sems):
        i = pl.program_id(0)

        @pl.when(i == 0)
        def _():
            gwo_acc[...] = jnp.zeros_like(gwo_acc)
            gpw_acc[...] = jnp.zeros_like(gpw_acc)
            gps_ref[...] = jnp.zeros_like(gps_ref)

        db = (gt_ref[...] * dz_ref[...]).astype(MXU_DTYPE)
        for k in range(n_grp):
            sl = slice(k * POOL_GROUP, (k + 1) * POOL_GROUP)
            dk = d_ref[:, sl]
            po = jnp.dot(dk, pw_ref[k], preferred_element_type=F32)
            psk = ps_ref[:, sl]
            y = po * psk
            gg = g_ref[:, sl]
            sg = _sigmoid(gg)
            silu = gg * sg
            gwo_acc[sl, :] += _dot_tn(y * silu, db)
            dp = _dot_nt(db, w_ref[sl, :])
            dy = dp * silu
            dg_ref[:, sl] = (dp * y * (sg * (1.0 + gg * (1.0 - sg)))).astype(MXU_DTYPE)
            gps_ref[0:1, sl] += _rowsum(dy * po)
            dpo = (dy * psk).astype(MXU_DTYPE)
            gpw_acc[k] += _dot_tn(dk, dpo)
            dd_ref[:, sl] = _dot_nt(dpo, pw_ref[k])

        @pl.when(i == nt - 1)
        def _():
            _flush(gwo_acc, gwo_hbm, sems.at[0])
            _flush(gpw_acc, gpw_hbm, sems.at[1])

    wide = pl.BlockSpec((tm, D_INNER), lambda i: (i, 0))
    nar = pl.BlockSpec((tm, D_MODEL), lambda i: (i, 0))
    return pl.pallas_call(
        body, name=name,
        out_shape=[jax.ShapeDtypeStruct((t, D_INNER), F32), jax.ShapeDtypeStruct((t, D_INNER), MXU_DTYPE),
                   jax.ShapeDtypeStruct((D_INNER, D_MODEL), F32),
                   jax.ShapeDtypeStruct((n_grp, POOL_GROUP, POOL_GROUP), F32),
                   jax.ShapeDtypeStruct((SUBLANES, D_INNER), F32)],
        grid=(nt,),
        in_specs=[nar, wide, wide,
                  pl.BlockSpec((n_grp, POOL_GROUP, POOL_GROUP), lambda i: (0, 0, 0), pipeline_mode=pl.Buffered(1)),
                  pl.BlockSpec((1, D_INNER), lambda i: (0, 0)), pl.BlockSpec((1, D_MODEL), lambda i: (0, 0)),
                  pl.BlockSpec((D_INNER, D_MODEL), lambda i: (0, 0), pipeline_mode=pl.Buffered(1))],
        out_specs=[wide, wide, ANY, ANY, pl.BlockSpec((SUBLANES, D_INNER), lambda i: (0, 0))],
        scratch_shapes=[pltpu.VMEM((D_INNER, D_MODEL), F32), pltpu.VMEM((n_grp, POOL_GROUP, POOL_GROUP), F32),
                        pltpu.SemaphoreType.DMA((2,))],
        compiler_params=_cparams(dimension_semantics=("arbitrary",)),
    )(dz, dmix, g, pw, ps, gt, wo)


def _bout0(dx1, xt, br0, lg, hf, hb, g, gt, wo, name, sides=()):
    t = dx1.shape[0]
    tm = min(TM_BWD, t)
    nt = t // tm

    def body(dx_ref, x_ref, br_ref, lg_ref, hf_ref, hb_ref, g_ref, gt_ref, w_ref,
             dz_ref, dy_ref, dg_ref, gwo_hbm, st_ref, gwo_acc, sem):
        i = pl.program_id(0)

        @pl.when(i == 0)
        def _():
            gwo_acc[...] = jnp.zeros_like(gwo_acc)
            st_ref[...] = jnp.zeros_like(st_ref)

        dx = dx_ref[...]
        br = br_ref[...]
        gate = gt_ref[...]
        xhat, rstd = _layer_norm_stats(ALPHA * x_ref[...] + gate * br)
        dz = _layer_norm_bwd(dx, xhat, rstd, lg_ref[...])
        dz_ref[...] = dz
        st_ref[0:1, :] += _rowsum(dx * xhat)
        st_ref[1:2, :] += _rowsum(dx)
        st_ref[2:3, :] += _rowsum(dz * br)
        db = (gate * dz).astype(MXU_DTYPE)
        for k in range(D_INNER // WBLK):
            sl = slice(k * WBLK, (k + 1) * WBLK)
            y = hf_ref[:, sl] + hb_ref[:, sl]
            gg = g_ref[:, sl]
            sg = _sigmoid(gg)
            silu = gg * sg
            gwo_acc[sl, :] += _dot_tn(y * silu, db)
            dp = _dot_nt(db, w_ref[sl, :])
            dy_ref[:, sl] = dp * silu
            dg_ref[:, sl] = (dp * y * (sg * (1.0 + gg * (1.0 - sg)))).astype(MXU_DTYPE)

        @pl.when(i == nt - 1)
        def _():
            _flush(gwo_acc, gwo_hbm, sem)

    wide = pl.BlockSpec((tm, D_INNER), lambda i: (i, 0))
    nar = pl.BlockSpec((tm, D_MODEL), lambda i: (i, 0))
    row = pl.BlockSpec((1, D_MODEL), lambda i: (0, 0))
    return _call_with_sides(
        body, sides, name=name,
        out_shape=[jax.ShapeDtypeStruct((t, D_MODEL), F32), jax.ShapeDtypeStruct((t, D_INNER), F32),
                   jax.ShapeDtypeStruct((t, D_INNER), MXU_DTYPE), jax.ShapeDtypeStruct((D_INNER, D_MODEL), F32),
                   jax.ShapeDtypeStruct((SUBLANES, D_MODEL), F32)],
        grid=(nt,),
        in_specs=[nar, nar, nar, row, wide, wide, wide, row,
                  pl.BlockSpec((D_INNER, D_MODEL), lambda i: (0, 0), pipeline_mode=pl.Buffered(1))],
        out_specs=[nar, wide, wide, ANY, pl.BlockSpec((SUBLANES, D_MODEL), lambda i: (0, 0))],
        scratch_shapes=[pltpu.VMEM((D_INNER, D_MODEL), F32), pltpu.SemaphoreType.DMA(())],
        compiler_params=_cparams(dimension_semantics=("arbitrary",)),
        args=[dx1, xt, br0, lg, hf, hb, g, gt, wo])


def _conv_bwd(duvf, duvb, u, conv_w, name, sides=()):
    s = u.shape[0]
    tm = min(TM_LRU_FWD, s)
    cb = CB_LRU
    nt = s // tm

    def body(df_ref, dfp_ref, dfn_ref, db_ref, dbp_ref, dbn_ref, u_ref, cw_ref, du_ref, cst_ref):
        i = pl.program_id(1)

        @pl.when(i == 0)
        def _():
            cst_ref[...] = jnp.zeros_like(cst_ref)

        first, last = i == 0, i == nt - 1
        pz = jnp.where(first, 0.0, 1.0)
        nz = jnp.where(last, 0.0, 1.0)
        dout = df_ref[...] + db_ref[...]
        dm1, dp1, dp2 = _shifted(dout, (dfp_ref[...] + dbp_ref[...]) * pz, (dfn_ref[...] + dbn_ref[...]) * nz,
                                 [-1, 1, 2])
        cw = cw_ref[...]
        du_ref[...] = (dp2 * cw[0:1] + dp1 * cw[1:2] + dout * cw[2:3] + dm1 * cw[3:4]).astype(MXU_DTYPE)
        u_t = u_ref[...]
        cst_ref[0:1, :] += _rowsum(dp2 * u_t)
        cst_ref[1:2, :] += _rowsum(dp1 * u_t)
        cst_ref[2:3, :] += _rowsum(dout * u_t)
        cst_ref[3:4, :] += _rowsum(dm1 * u_t)
        cst_ref[4:5, :] += _rowsum(dout)

    tile, prev, nxt = _lru_specs(s, tm, cb, lambda i: i, nt)
    return _call_with_sides(
        body, sides, name=name,
        out_shape=[jax.ShapeDtypeStruct((s, D_INNER), MXU_DTYPE), jax.ShapeDtypeStruct((SUBLANES, D_INNER), F32)],
        grid=(D_INNER // cb, nt),
        in_specs=[tile, prev, nxt] * 2 + [tile, pl.BlockSpec((4, cb), lambda c, i: (0, c))],
        out_specs=[tile, pl.BlockSpec((SUBLANES, cb), lambda c, i: (0, c))], scratch_shapes=[],
        compiler_params=_cparams(dimension_semantics=("arbitrary", "arbitrary")),
        args=[duvf, duvf, duvf, duvb, duvb, duvb, u, conv_w])


def _bin(du, dg, xin, dzin, sc, sh, wg, name, gw_init=None, sides=()):
    t = xin.shape[0]
    tm = min(TM_MM, t)
    nt = t // tm
    has_g, has_dx, has_init = dg is not None, dzin is not None, gw_init is not None
    half = N_WBLK // 2
    n_blk = N_WBLK if has_g else half

    def body(*refs):
        refs = list(refs)
        du_ref = refs.pop(0)
        dg_ref = refs.pop(0) if has_g else None
        x_ref = refs.pop(0)
        dz_ref = refs.pop(0) if has_dx else None
        sc_ref, sh_ref, w_ref = refs.pop(0), refs.pop(0), refs.pop(0)
        init_hbm = refs.pop(0) if has_init else None
        dx_ref = refs.pop(0) if has_dx else None
        gw_hbm, st_ref, gw_acc, sem = refs
        i = pl.program_id(0)

        @pl.when(i == 0)
        def _():
            st_ref[...] = jnp.zeros_like(st_ref)
            first_zero = 0
            if has_init:
                _flush(init_hbm, gw_acc.at[pl.ds(0, half)], sem)
                first_zero = half
            for k in range(first_zero, n_blk):
                gw_acc[k] = jnp.zeros((D_MODEL, WBLK), F32)

        xv = x_ref[...]
        scale = 1.0 + sc_ref[...]
        h = (xv * scale + sh_ref[...]).astype(MXU_DTYPE)
        dh = None
        for k in range(n_blk):
            src = du_ref if k < half else dg_ref
            kk = k % half
            dk = src[:, kk * WBLK:(kk + 1) * WBLK]
            gw_acc[k] += _dot_tn(h, dk)
            contrib = _dot_nt(dk, w_ref[k])
            dh = contrib if dh is None else dh + contrib
        st_ref[0:1, :] += _rowsum(dh * xv)
        st_ref[1:2, :] += _rowsum(dh)
        if has_dx:
            dx_ref[...] = ALPHA * dz_ref[...] + dh * scale

        @pl.when(i == nt - 1)
        def _():
            _flush(gw_acc, gw_hbm, sem)

    wide = pl.BlockSpec((tm, D_INNER), lambda i: (i, 0))
    nar = pl.BlockSpec((tm, D_MODEL), lambda i: (i, 0))
    row = pl.BlockSpec((1, D_MODEL), lambda i: (0, 0))
    wspec = pl.BlockSpec((n_blk, D_MODEL, WBLK), lambda i: (0, 0, 0), pipeline_mode=pl.Buffered(1))
    in_specs = ([wide] + ([wide] if has_g else []) + [nar] + ([nar] if has_dx else []) + [row, row, wspec]
                + ([ANY] if has_init else []))
    args = ([du] + ([dg] if has_g else []) + [xin] + ([dzin] if has_dx else []) + [sc, sh, wg]
            + ([gw_init] if has_init else []))
    out_shape = ([jax.ShapeDtypeStruct((t, D_MODEL), F32)] if has_dx else []) + [
        jax.ShapeDtypeStruct((n_blk, D_MODEL, WBLK), F32), jax.ShapeDtypeStruct((SUBLANES, D_MODEL), F32)]
    out_specs = ([nar] if has_dx else []) + [ANY, pl.BlockSpec((SUBLANES, D_MODEL), lambda i: (0, 0))]
    return _call_with_sides(
        body, sides, name=name, out_shape=out_shape, grid=(nt,), in_specs=in_specs, out_specs=out_specs,
        scratch_shapes=[pltpu.VMEM((n_blk, D_MODEL, WBLK), F32), pltpu.SemaphoreType.DMA(())],
        compiler_params=_cparams(dimension_semantics=("arbitrary",)), args=args)


def _blocks_by_device(a, axis):
    shape = a.shape
    a = a.reshape(shape[:axis] + (N_DEV, shape[axis] // N_DEV) + shape[axis + 1:])
    return jnp.moveaxis(a, axis, 0)


def kernel(x, c, ctx, c_ctx, w_mod, b_mod, w_in, w_out, ln_g, ln_b, conv_w, conv_b, lru_wa, lru_ba, lru_wx, lru_bx, lru_lam, pool_w, pool_scale, loss_target, m_c_ctx, m_w_mod, m_b_mod, m_w_in, m_w_out, m_ln_g, m_ln_b, m_conv_w, m_conv_b, m_lru_wa, m_lru_ba, m_lru_wx, m_lru_bx, m_lru_lam, m_pool_w, m_pool_scale, v_c_ctx, v_w_mod, v_b_mod, v_w_in, v_w_out, v_ln_g, v_ln_b, v_conv_w, v_conv_b, v_lru_wa, v_lru_ba, v_lru_wx, v_lru_bx, v_lru_lam, v_pool_w, v_pool_scale):
    xi, yi, ci = _my_pos()
    dev = 4 * xi + 2 * yi + ci
    xt, ctxt, tgt = x[0], ctx[0], loss_target[0]
    n_mod = w_mod.shape[2]

    small_shapes = [(D_MODEL,), conv_w.shape[1:], lru_ba.shape[1:], lru_bx.shape[1:], lru_lam.shape[1:],
                    pool_scale.shape[1:]]
    small = _to_rows([c[0], conv_w[0], lru_ba[0], lru_bx[0], lru_lam[0], pool_scale[0]], SUBLANES)
    small_all, = _all_gather([small], "gather_small")
    pieces = [_split_rows(small_all[k], small_shapes) for k in range(N_DEV)]
    c_all = jnp.stack([p[0] for p in pieces])
    conv_w_f = jnp.concatenate([p[1] for p in pieces], axis=-1)
    lru_ba_f = jnp.concatenate([p[2] for p in pieces], axis=-1)[:, None, :]
    lru_bx_f = jnp.concatenate([p[3] for p in pieces], axis=-1)[:, None, :]
    lru_lam_f = jnp.concatenate([p[4] for p in pieces], axis=-1)[:, None, :]
    pool_scale_f = jnp.concatenate([p[5] for p in pieces], axis=-1)[None, :]

    cond = jnp.concatenate([c_all, jnp.broadcast_to(c_ctx[None, :], (N_DEV, D_MODEL))], axis=0)
    b_my = lax.dynamic_slice(b_mod, (0, dev * n_mod), (2, n_mod))[:, None, :]
    mod_part = _mod_fwd(cond, w_mod, b_my, "mod_fwd")
    mod_all, = _all_gather([mod_part], "gather_mod")
    mod = jnp.transpose(mod_all, (1, 2, 0, 3)).reshape(2, 16, 3 * D_MODEL)
    mod_me = lax.dynamic_slice(mod, (0, dev, 0), (2, 1, 3 * D_MODEL))
    sh = [mod_me[i, :, 0:D_MODEL] for i in range(2)]
    sc = [mod_me[i, :, D_MODEL:2 * D_MODEL] for i in range(2)]
    gt = [mod_me[i, :, 2 * D_MODEL:] for i in range(2)]
    shc, scc = mod[0, 8:9, 0:D_MODEL], mod[0, 8:9, D_MODEL:2 * D_MODEL]

    wi0, = _all_gather([w_in[0].astype(MXU_DTYPE)], "gather_weights0")
    lg = [ln_g[i][None, :] for i in range(2)]
    lb = [ln_b[i][None, :] for i in range(2)]
    lru_p = dict(conv_w=conv_w_f, conv_b=conv_b, wa=lru_wa[0].astype(MXU_DTYPE), wx=lru_wx[0].astype(MXU_DTYPE),
                 ba=lru_ba_f, bx=lru_bx_f, lam=lru_lam_f)
    zero_state = jnp.zeros((1, D_INNER), F32)

    (u0, g0), (wo0,) = _in_proj(xt, sc[0], sh[0], wi0, "in_proj0", sides=[("gather", [w_out[0].astype(MXU_DTYPE)])])
    (uc, _), _ = _in_proj(ctxt, scc, shc, wi0, "in_proj0_ctx")
    (hcf, cf, uvc), _ = _lru_fwd(uc, zero_state, lru_p, 0, "lru_fwd_ctx_f", conv=True)
    (hcb, cbk), _ = _lru_fwd(uvc, zero_state, lru_p, 1, "lru_fwd_ctx_b", conv=False)
    (hf, _, uv0), (wi1,) = _lru_fwd(u0, cf, lru_p, 0, "lru_fwd_f", conv=True,
                                    sides=[("gather", [w_in[1].astype(MXU_DTYPE)])])
    (hb, _), (wo1, pool_w_g) = _lru_fwd(
        uv0, cbk, lru_p, 1, "lru_fwd_b", conv=False,
        sides=[("gather", [w_out[1].astype(MXU_DTYPE), pool_w[0].astype(MXU_DTYPE)])])
    w_in_l = [wi0, wi1]
    w_out_l = [wo0.reshape(D_INNER, D_MODEL), wo1.reshape(D_INNER, D_MODEL)]
    pool_w_f = jnp.transpose(pool_w_g, (1, 0, 2, 3)).reshape(len(POOL_WINDOWS), POOL_GROUP, POOL_GROUP)
    x1, br0 = _out0(hf, hb, g0, xt, gt[0], w_out_l[0], lg[0], lb[0], "out0")
    (u1, g1), _ = _in_proj(x1, sc[1], sh[1], w_in_l[1], "in_proj1")
    dmix = _pool_mix(u1, False, MXU_DTYPE, "pool_fwd")
    dz1, st1 = _out1(dmix, pool_w_f, pool_scale_f, g1, x1, gt[1], w_out_l[1], lg[1], lb[1], tgt, "out1")
    loss_me = jnp.full((1, LANES), (0.5 / D_MODEL) * jnp.sum(st1[3]), F32)

    core = jnp.reshape(ci, (1,)).astype(jnp.int32)
    wo_view = lambda a: a.reshape(N_DEV, D_INNER // N_DEV, D_MODEL)
    pw_view = lambda a: _blocks_by_device(a, 1).reshape(N_DEV, POOL_GROUP // N_DEV * len(POOL_WINDOWS), POOL_GROUP)
    dd, dg1, gwo1, gpw, gps = _bout1(dz1, dmix, g1, pool_w_f, pool_scale_f, gt[1], w_out_l[1], "bwd_out1")
    du1 = _pool_mix(dd, True, MXU_DTYPE, "pool_bwd")
    (dx1, gwi1, stb1), _ = _bin(du1, dg1, x1, dz1, sc[1], sh[1], w_in_l[1], "bwd_in1")
    bufs1 = [gwi1, wo_view(gwo1), pw_view(gpw)]
    (dz0, dy0, dg0, gwo0, stl0), recv1 = _bout0(dx1, xt, br0, lg[0], hf, hb, g0, gt[0], w_out_l[0], "bwd_out0",
                                                sides=[("sibling", bufs1)])
    pairs1 = [_pair_sum(b, r, core, "reduce_pair_" + n)
              for b, r, n in zip(bufs1, recv1, ["w_in1", "w_out1", "pool_w"])]
    (duvf, gwa_f, gwx_f, gv_f, dh0f), (p_wi1, p_wo1, p_pw, recv_wo0) = _lru_bwd(
        uv0, dy0, hf, cf, zero_state, lru_p, 0, "lru_bwd_f", sides=[("chips", pairs1), ("sibling", [wo_view(gwo0)])])
    pair_wo0 = _pair_sum(wo_view(gwo0), recv_wo0, core, "reduce_pair_w_out0")
    (duvb, gwa_b, gwx_b, gv_b, dh0b), (p_wo0,) = _lru_bwd(
        uv0, dy0, hb, cbk, zero_state, lru_p, 1, "lru_bwd_b", sides=[("chips", [pair_wo0])])
    zero_dh = jnp.zeros_like(uc)
    (ducf, gwa_cf, gwx_cf, gv_cf, _), _ = _lru_bwd(uvc, zero_dh, hcf, zero_state, dh0f, lru_p, 0, "lru_bwd_ctx_f")
    (ducb, gwa_cb, gwx_cb, gv_cb, _), _ = _lru_bwd(uvc, zero_dh, hcb, zero_state, dh0b, lru_p, 1, "lru_bwd_ctx_b")

    def pack(sharded, replicated):
        sh_sizes = [int(np.prod(a.shape[1:])) for a in sharded]
        rep_sizes = [a.shape[0] // N_DEV for a in replicated]
        n_flat = sum(sh_sizes) + sum(rep_sizes)
        rows = -(-(-(-n_flat // LANES)) // FLAT_ROWS) * FLAT_ROWS
        buf = jnp.concatenate([a.reshape(N_DEV, -1) for a in sharded + replicated], axis=1)
        return jnp.pad(buf, ((0, 0), (0, rows * LANES - n_flat))).reshape(N_DEV, rows, LANES), sh_sizes, rep_sizes

    def unpack(reduced, sh_sizes, rep_sizes, sh_shapes):
        flat = reduced.reshape(-1)
        offs = np.cumsum([0] + sh_sizes)
        mine = [flat[offs[k]:offs[k + 1]].reshape(s) for k, s in enumerate(sh_shapes)]
        return mine, _to_rows([flat[offs[-1]:offs[-1] + sum(rep_sizes)]], SUBLANES)

    def spread(rep_all, rep_sizes, shapes):
        flat = rep_all.reshape(N_DEV, -1)
        offs = np.cumsum([0] + rep_sizes)
        return [flat[:, offs[k]:offs[k + 1]].reshape(s) for k, s in enumerate(shapes)]

    gwa = jnp.stack([gwa_f + gwa_cf, gwa_b + gwa_cb])
    gwx = jnp.stack([gwx_f + gwx_cf, gwx_b + gwx_cb])
    gv = jnp.stack([gv_f + gv_cf, gv_b + gv_cb])
    early, e_sh, e_rep = pack(
        [_blocks_by_device(gv[:, 0], 1), _blocks_by_device(gv[:, 1], 1), _blocks_by_device(gv[:, 2], 1),
         _blocks_by_device(gps[0], 0)],
        [gwa.reshape(-1), gwx.reshape(-1), jnp.stack([stl0[0], st1[0]]).reshape(-1),
         jnp.stack([stl0[1], st1[1]]).reshape(-1)])
    recv_e, = _sibling_exchange([early], "reduce_sibling_early")
    pair_e = _pair_sum(early, recv_e, core, "reduce_pair_early")
    (du0, cst0), (p_early,) = _conv_bwd(duvf, duvb, u0, conv_w_f, "conv_bwd", sides=[("chips", [pair_e])])
    (duc, cstc), _ = _conv_bwd(ducf, ducb, uc, conv_w_f, "conv_bwd_ctx")
    (g_lru_ba, g_lru_bx, g_lru_lam, g_pool_scale), rep_early = unpack(
        _sum4(p_early, "reduce_sum_early"), e_sh, e_rep,
        [lru_ba.shape, lru_bx.shape, lru_lam.shape, pool_scale.shape])
    (gwic, stc), _ = _bin(duc, None, ctxt, None, scc, shc, w_in_l[0][:N_WBLK // 2], "bwd_in0_ctx")
    (gx, gwi0, stb0), (rep_early_all,) = _bin(du0, dg0, xt, dz0, sc[0], sh[0], w_in_l[0], "bwd_in0", gw_init=gwic,
                                             sides=[("gather", [rep_early])])
    g_lru_wa, g_lru_wx, g_ln_g, g_ln_b = spread(rep_early_all, e_rep,
                                                [lru_wa.shape, lru_wx.shape, ln_g.shape, ln_b.shape])

    zero_row = jnp.zeros((1, D_MODEL), F32)
    dm_me = jnp.stack([
        jnp.concatenate([jnp.concatenate([stb0[1:2], stb0[0:1], stl0[2:3]], axis=1),
                         jnp.concatenate([stc[1:2], stc[0:1], zero_row], axis=1)], axis=0),
        jnp.concatenate([jnp.concatenate([stb1[1:2], stb1[0:1], st1[2:3]], axis=1),
                         jnp.zeros((1, 3 * D_MODEL), F32)], axis=0)])
    dm_g, loss_g = _all_gather([dm_me, loss_me], "gather_dmod")
    loss = jnp.sum(loss_g[:, 0, 0])
    dm_all = jnp.concatenate([jnp.transpose(dm_g[:, :, 0], (1, 0, 2)), jnp.transpose(dm_g[:, :, 1], (1, 0, 2))],
                             axis=1)
    dm_my = lax.dynamic_slice(dm_all, (0, 0, dev * n_mod), (2, 16, n_mod))
    g_w_mod, g_b_mod, gcc_part = _mod_bwd(cond, dm_all, dm_my, w_mod, "mod_bwd")
    g_b_mod = g_b_mod.reshape(b_mod.shape)

    cst = cst0 + cstc
    late, l_sh, l_rep = pack([_blocks_by_device(cst[0:4], 1)], [cst[4], gcc_part.reshape(-1)])
    bufs = [gwi0, late]
    recvs = _sibling_exchange(bufs, "reduce_sibling")
    pairs = [_pair_sum(b, r, core, "reduce_pair_" + n) for b, r, n in zip(bufs, recvs, ["w_in0", "late"])]
    p_wi0, p_late = _chip_exchange(pairs, "reduce_chips")
    (g_conv_w,), rep_late = unpack(_sum4(p_late, "reduce_sum_late"), l_sh, l_rep, [conv_w.shape])
    rep_late_all, = _all_gather([rep_late], "gather_replicated")
    g_conv_b, g_c_ctx = spread(rep_late_all, l_rep, [conv_b.shape, c_ctx.shape])

    names = ["c_ctx", "w_mod", "b_mod", "w_in", "w_out", "ln_g", "ln_b", "conv_w", "conv_b", "lru_wa", "lru_ba",
             "lru_wx", "lru_bx", "lru_lam", "pool_w", "pool_scale"]
    weights = dict(c_ctx=c_ctx, w_mod=w_mod, b_mod=b_mod, w_in=w_in, w_out=w_out, ln_g=ln_g, ln_b=ln_b,
                   conv_w=conv_w, conv_b=conv_b, lru_wa=lru_wa, lru_ba=lru_ba, lru_wx=lru_wx, lru_bx=lru_bx,
                   lru_lam=lru_lam, pool_w=pool_w, pool_scale=pool_scale)
    mom_m = dict(c_ctx=m_c_ctx, w_mod=m_w_mod, b_mod=m_b_mod, w_in=m_w_in, w_out=m_w_out, ln_g=m_ln_g, ln_b=m_ln_b,
                 conv_w=m_conv_w, conv_b=m_conv_b, lru_wa=m_lru_wa, lru_ba=m_lru_ba, lru_wx=m_lru_wx,
                 lru_bx=m_lru_bx, lru_lam=m_lru_lam, pool_w=m_pool_w, pool_scale=m_pool_scale)
    mom_v = dict(c_ctx=v_c_ctx, w_mod=v_w_mod, b_mod=v_b_mod, w_in=v_w_in, w_out=v_w_out, ln_g=v_ln_g, ln_b=v_ln_b,
                 conv_w=v_conv_w, conv_b=v_conv_b, lru_wa=v_lru_wa, lru_ba=v_lru_ba, lru_wx=v_lru_wx,
                 lru_bx=v_lru_bx, lru_lam=v_lru_lam, pool_w=v_pool_w, pool_scale=v_pool_scale)
    grads = dict(c_ctx=g_c_ctx, w_mod=g_w_mod, b_mod=g_b_mod, ln_g=g_ln_g, ln_b=g_ln_b,
                 conv_w=g_conv_w, conv_b=g_conv_b, lru_wa=g_lru_wa, lru_ba=g_lru_ba, lru_wx=g_lru_wx,
                 lru_bx=g_lru_bx, lru_lam=g_lru_lam)
    grads["pool_scale"] = g_pool_scale
    delta, new_m, new_v = {}, {}, {}

    def update_parts(n, parts, view):
        res = _adamw_parts(weights[n].reshape(view), parts, mom_m[n].reshape(view), mom_v[n].reshape(view),
                           "adamw_" + n)
        grads[n], delta[n], new_m[n], new_v[n] = [r.reshape(weights[n].shape) for r in res]

    update_parts("w_in", [p_wi0, p_wi1], w_in.shape)
    update_parts("w_out", [p_wo0, p_wo1], w_out.shape)
    update_parts("pool_w", [p_pw], (1,) + p_pw.shape[1:])
    for n in ("w_mod", "lru_wa", "lru_wx"):
        shape = weights[n].shape
        view = (int(np.prod(shape[:-1])), shape[-1])
        res = _adamw(weights[n].reshape(view), grads[n].reshape(view), mom_m[n].reshape(view),
                     mom_v[n].reshape(view), "adamw_" + n)
        delta[n], new_m[n], new_v[n] = [r.reshape(shape) for r in res]

    small = [n for n in names if n not in delta]
    shapes = [weights[n].shape for n in small]
    flat = lambda d: _to_rows([d[n] for n in small], FLAT_ROWS)
    res = _adamw(flat(weights), flat(grads), flat(mom_m), flat(mom_v), "adamw_small")
    for d, r in zip((delta, new_m, new_v), res):
        d.update(zip(small, _split_rows(r, shapes)))

    return (loss, gx[None], *[grads[n] for n in names], *[delta[n] for n in names],
            *[new_m[n] for n in names], *[new_v[n] for n in names])
```

```python
import functools

import numpy as np
import jax
import jax.numpy as jnp
from jax import lax
from jax.experimental import pallas as pl
from jax.experimental.pallas import tpu as pltpu

F32 = jnp.float32
BF16 = jnp.bfloat16
MXU_DTYPE = BF16

D_MODEL = 1024
D_INNER = 2048
LRU_BLOCK = 128
GRID_W = 64
POOL_WINDOWS = (2, 4, 8, 16)
POOL_GROUP = 512
ALPHA = float(4 ** 0.25)
LN_EPS = 1e-5
LRU_C = 8.0
N_DEV = 8
N_WBLK = 8
WBLK = 512

ADAM_LR = 0.001
ADAM_B1 = 0.9
ADAM_B2 = 0.999
ADAM_EPS = 1e-08
ADAM_WD = 0.01
ADAM_STEP = 10

LANES = 128
SUBLANES = 8
V7X_VMEM_BYTES = 64 * 1024 * 1024
VMEM_LIMIT = V7X_VMEM_BYTES - 8 * 1024 * 1024
MESH = pl.DeviceIdType.MESH
ANY = pl.BlockSpec(memory_space=pl.ANY)

TM_MM = 512
TM_BWD = 256
TM_LRU = 1024
TM_LRU_FWD = 1024
CB_LRU = 512
N_SEG = 8
SCAN_UNROLL = 4
SCAN_ROW_T = 17
SCAN_ROW_J = 2
SQRT_FLOOR = 1e-30
FLAT_ROWS = 16
ELEMENTWISE_TILE_BYTES = 1 << 20
POOL_TOK = 256
WIRE_DTYPE = BF16
ACT_DTYPE = BF16


def _cparams(**kw):
    return pltpu.CompilerParams(vmem_limit_bytes=VMEM_LIMIT, **kw)


def _my_pos():
    return lax.axis_index("x"), lax.axis_index("y"), lax.axis_index("c")


def _dot(a, b):
    return jnp.dot(a.astype(MXU_DTYPE), b.astype(MXU_DTYPE), preferred_element_type=F32)


def _dot_tn(a, b):
    return lax.dot_general(a.astype(MXU_DTYPE), b.astype(MXU_DTYPE), (((0,), (0,)), ((), ())),
                           preferred_element_type=F32)


def _dot_nt(a, b):
    return lax.dot_general(a.astype(MXU_DTYPE), b.astype(MXU_DTYPE), (((1,), (1,)), ((), ())),
                           preferred_element_type=F32)


def _sigmoid(z):
    return 0.5 * jnp.tanh(0.5 * z) + 0.5


def _log_sigmoid(x):
    y = jnp.exp(-jnp.abs(x))
    u = 1.0 + y
    l1p = jnp.where(u == 1.0, y, jnp.log(u) * (y / jnp.where(u == 1.0, 1.0, u - 1.0)))
    return jnp.minimum(x, 0.0) - l1p


def _rowsum(v):
    return jnp.sum(v, axis=0, keepdims=True)


def _layer_norm_stats(z):
    mu = jnp.mean(z, axis=-1, keepdims=True)
    zc = z - mu
    var = jnp.mean(zc * zc, axis=-1, keepdims=True)
    rstd = lax.rsqrt(var + LN_EPS)
    return zc * rstd, rstd


def _layer_norm_bwd(dy, xhat, rstd, g):
    dxh = dy * g
    m1 = jnp.mean(dxh, axis=-1, keepdims=True)
    m2 = jnp.mean(dxh * xhat, axis=-1, keepdims=True)
    return rstd * (dxh - m1 - xhat * m2)


def _shifted(v, before8, after8, offsets):
    n = v.shape[0]
    ext = jnp.concatenate([before8, v, after8], axis=0)
    total = n + 2 * SUBLANES
    return [pltpu.roll(ext, (-k) % total, 0)[SUBLANES:SUBLANES + n] for k in offsets]


def _rows8(row):
    return jnp.broadcast_to(row, (SUBLANES, row.shape[1]))


def _shift_down(v, first_row):
    return _shifted(v, _rows8(first_row), _rows8(first_row), [-1])[0]


def _shift_up(v, last_row):
    return _shifted(v, _rows8(last_row), _rows8(last_row), [1])[0]


def _all_gather(blocks, name):
    n = len(blocks)

    def body(*refs):
        x_refs, out_refs = refs[:n], refs[n:2 * n]
        send_sems, recv_sems, local_sems = refs[2 * n:]
        x, y, c = _my_pos()
        me, sibling = (x, y, c), (x, y, 1 - c)
        chips = [(1 - x, y), (x, 1 - y), (1 - x, 1 - y)]

        def slot(a, px, py, pc):
            return out_refs[a].at[4 * px + 2 * py + pc]

        def copy(a, k, block, to, src=None):
            return pltpu.make_async_remote_copy(
                src_ref=slot(a, *block) if src is None else src, dst_ref=slot(a, *block),
                send_sem=send_sems.at[a, k], recv_sem=recv_sems.at[a, k], device_id=to, device_id_type=MESH)

        mine = [pltpu.make_async_copy(x_refs[a], slot(a, *me), local_sems.at[a]) for a in range(n)]
        for cp in mine:
            cp.start()
        first = []
        for a in range(n):
            first.append(copy(a, 0, me, sibling, src=x_refs[a]))
            first += [copy(a, 1 + j, me, (*chip, c), src=x_refs[a]) for j, chip in enumerate(chips)]
        for cp in first:
            cp.start()
        passed = []
        for j, chip in enumerate(chips):
            for a in range(n):
                copy(a, 1 + j, (*chip, c), me).wait_recv()
                fwd = copy(a, 4 + j, (*chip, c), sibling)
                fwd.start()
                passed.append(fwd)
        for a in range(n):
            copy(a, 0, sibling, me).wait_recv()
            for j, chip in enumerate(chips):
                copy(a, 4 + j, (*chip, 1 - c), me).wait_recv()
        for cp in first + passed:
            cp.wait_send()
        for cp in mine:
            cp.wait()

    outs = pl.pallas_call(
        body, name=name,
        out_shape=[jax.ShapeDtypeStruct((N_DEV,) + b.shape, b.dtype) for b in blocks],
        in_specs=[ANY] * n, out_specs=[ANY] * n,
        scratch_shapes=[pltpu.SemaphoreType.DMA((n, 7)), pltpu.SemaphoreType.DMA((n, 7)),
                        pltpu.SemaphoreType.DMA((n,))],
    )(*blocks)
    return list(outs)


def _sibling_exchange(bufs, name):
    n = len(bufs)

    def body(*refs):
        srcs, outs = refs[:n], refs[n:2 * n]
        send_sems, recv_sems = refs[2 * n:]
        x, y, c = _my_pos()
        copies = [pltpu.make_async_remote_copy(
            src_ref=srcs[a].at[2 * j + (1 - c)], dst_ref=outs[a].at[j], send_sem=send_sems.at[a, j],
            recv_sem=recv_sems.at[a, j], device_id=(x, y, 1 - c), device_id_type=MESH)
            for a in range(n) for j in range(4)]
        for cp in copies:
            cp.start()
        for cp in copies:
            cp.wait()

    outs = pl.pallas_call(
        body, name=name, out_shape=[jax.ShapeDtypeStruct((4,) + b.shape[1:], b.dtype) for b in bufs],
        in_specs=[ANY] * n, out_specs=[ANY] * n,
        scratch_shapes=[pltpu.SemaphoreType.DMA((n, 4)), pltpu.SemaphoreType.DMA((n, 4))],
    )(*bufs)
    return list(outs)


def _chip_exchange(parts, name):
    n = len(parts)

    def body(*refs):
        srcs, outs = refs[:n], refs[n:2 * n]
        send_sems, recv_sems, local_sems = refs[2 * n:]
        x, y, c = _my_pos()
        jme = 2 * x + y
        peers = [(1 - x, y), (x, 1 - y), (1 - x, 1 - y)]
        local = [pltpu.make_async_copy(srcs[a].at[jme], outs[a].at[jme], local_sems.at[a]) for a in range(n)]
        for cp in local:
            cp.start()

        def copy(a, k, px, py, dst_slot):
            return pltpu.make_async_remote_copy(
                src_ref=srcs[a].at[2 * px + py], dst_ref=outs[a].at[dst_slot], send_sem=send_sems.at[a, k],
                recv_sem=recv_sems.at[a, k], device_id=(px, py, c), device_id_type=MESH)

        sends = [copy(a, k, px, py, jme) for a in range(n) for k, (px, py) in enumerate(peers)]
        for cp in sends:
            cp.start()
        for a in range(n):
            for k, (px, py) in enumerate(peers):
                copy(a, k, px, py, 2 * px + py).wait_recv()
        for cp in sends:
            cp.wait_send()
        for cp in local:
            cp.wait()

    outs = pl.pallas_call(
        body, name=name, out_shape=[jax.ShapeDtypeStruct(p.shape, p.dtype) for p in parts],
        in_specs=[ANY] * n, out_specs=[ANY] * n,
        scratch_shapes=[pltpu.SemaphoreType.DMA((n, 3)), pltpu.SemaphoreType.DMA((n, 3)),
                        pltpu.SemaphoreType.DMA((n,))],
    )(*parts)
    return list(outs)


_SIDE_REMOTE = {"gather": 7, "sibling": 4, "chips": 3}
_FLIPS = [(0, 0, 1), (1, 0, 0), (0, 1, 0), (1, 1, 0), (1, 0, 1), (0, 1, 1), (1, 1, 1)]


def _side_plan(sides):
    inputs, out_shapes, scratch = [], [], []
    for kind, arrays in sides:
        n = len(arrays)
        for a in arrays:
            inputs.append(a)
            shape = {"gather": (N_DEV,) + a.shape, "sibling": (4,) + a.shape[1:], "chips": a.shape}[kind]
            out_shapes.append(jax.ShapeDtypeStruct(shape, a.dtype))
        scratch += [pltpu.SemaphoreType.DMA((n, _SIDE_REMOTE[kind])), pltpu.SemaphoreType.DMA((n, _SIDE_REMOTE[kind])),
                    pltpu.SemaphoreType.DMA((n,))]
    return inputs, out_shapes, scratch


def _side_copies(sides, in_refs, out_refs, sem_refs):
    x, y, c = _my_pos()
    starts, waits = [], []
    pos = 0
    for s, (kind, arrays) in enumerate(sides):
        send_sems, recv_sems, local_sems = sem_refs[3 * s:3 * s + 3]
        for a in range(len(arrays)):
            src, out = in_refs[pos], out_refs[pos]
            pos += 1

            def remote(k, src_ref, dst_ref, to):
                return pltpu.make_async_remote_copy(src_ref=src_ref, dst_ref=dst_ref, send_sem=send_sems.at[a, k],
                                                    recv_sem=recv_sems.at[a, k], device_id=to, device_id_type=MESH)

            def local(src_ref, dst_ref):
                cp = pltpu.make_async_copy(src_ref, dst_ref, local_sems.at[a])
                starts.append(cp.start)
                waits.append(cp.wait)

            if kind == "gather":
                me = 4 * x + 2 * y + c
                local(src, out.at[me])
                for k, (fx, fy, fc) in enumerate(_FLIPS):
                    px, py, pc = (1 - x if fx else x), (1 - y if fy else y), (1 - c if fc else c)
                    send = remote(k, src, out.at[me], (px, py, pc))
                    starts.append(send.start)
                    waits += [remote(k, src, out.at[4 * px + 2 * py + pc], (px, py, pc)).wait_recv, send.wait_send]
            elif kind == "sibling":
                for j in range(4):
                    cp = remote(j, src.at[2 * j + (1 - c)], out.at[j], (x, y, 1 - c))
                    starts.append(cp.start)
                    waits.append(cp.wait)
            else:
                jme = 2 * x + y
                local(src.at[jme], out.at[jme])
                for k, (px, py) in enumerate([(1 - x, y), (x, 1 - y), (1 - x, 1 - y)]):
                    send = remote(k, src.at[2 * px + py], out.at[jme], (px, py, c))
                    starts.append(send.start)
                    waits += [remote(k, src.at[2 * px + py], out.at[2 * px + py], (px, py, c)).wait_recv,
                              send.wait_send]
    return starts, waits


def _call_with_sides(body, sides, *, name, grid, in_specs, out_specs, out_shape, scratch_shapes, compiler_params, args):
    if not sides:
        res = pl.pallas_call(body, name=name, grid=grid, in_specs=in_specs, out_specs=out_specs, out_shape=out_shape,
                             scratch_shapes=scratch_shapes, compiler_params=compiler_params)(*args)
        return list(res), []
    s_in, s_out, s_scr = _side_plan(sides)
    n_in, n_out, n_scr, n_side = len(in_specs), len(out_specs), len(scratch_shapes), len(s_in)

    def wrapped(*refs):
        refs = list(refs)
        ins, side_in = refs[:n_in], refs[n_in:n_in + n_side]
        outs = refs[n_in + n_side:n_in + n_side + n_out]
        side_out = refs[n_in + n_side + n_out:n_in + 2 * n_side + n_out]
        rest = refs[n_in + 2 * n_side + n_out:]
        starts, waits = _side_copies(sides, side_in, side_out, rest[n_scr:])
        first = functools.reduce(jnp.logical_and, [pl.program_id(d) == 0 for d in range(len(grid))])
        last = functools.reduce(jnp.logical_and, [pl.program_id(d) == grid[d] - 1 for d in range(len(grid))])

        @pl.when(first)
        def _():
            for start in starts:
                start()

        body(*ins, *outs, *rest[:n_scr])

        @pl.when(last)
        def _():
            for wait in waits:
                wait()

    res = pl.pallas_call(
        wrapped, name=name, grid=grid, in_specs=list(in_specs) + [ANY] * n_side,
        out_specs=list(out_specs) + [ANY] * n_side, out_shape=list(out_shape) + s_out,
        scratch_shapes=list(scratch_shapes) + s_scr, compiler_params=compiler_params,
    )(*args, *s_in)
    return list(res[:n_out]), list(res[n_out:])


def _row_tile(r, l):
    t = min(r, max(16, ELEMENTWISE_TILE_BYTES // (4 * l) // 16 * 16))
    while r % t:
        t -= 16
    return t


def _pair_sum(buf, recv, core, name):
    _, r, l = buf.shape
    tr = _row_tile(r, l)

    def body(core_ref, a_ref, b_ref, o_ref):
        o_ref[...] = (a_ref[...] + b_ref[...]).astype(WIRE_DTYPE)

    return pl.pallas_call(
        body, name=name, out_shape=jax.ShapeDtypeStruct((4, r, l), WIRE_DTYPE),
        grid_spec=pltpu.PrefetchScalarGridSpec(
            num_scalar_prefetch=1, grid=(4, r // tr),
            in_specs=[pl.BlockSpec((None, tr, l), lambda j, i, cr: (2 * j + cr[0], i, 0)),
                      pl.BlockSpec((None, tr, l), lambda j, i, cr: (j, i, 0))],
            out_specs=pl.BlockSpec((None, tr, l), lambda j, i, cr: (j, i, 0))),
        compiler_params=_cparams(dimension_semantics=("arbitrary", "arbitrary")),
    )(core, buf, recv)


def _sum_parts(p_ref):
    return ((p_ref[0].astype(F32) + p_ref[1].astype(F32)) + (p_ref[2].astype(F32) + p_ref[3].astype(F32)))


def _sum4(parts, name):
    _, r, l = parts.shape
    tr = _row_tile(r, l)

    def body(p_ref, o_ref):
        o_ref[...] = _sum_parts(p_ref)

    return pl.pallas_call(
        body, name=name, out_shape=jax.ShapeDtypeStruct((r, l), F32), grid=(r // tr,),
        in_specs=[pl.BlockSpec((4, tr, l), lambda i: (0, i, 0))],
        out_specs=pl.BlockSpec((tr, l), lambda i: (i, 0)),
        compiler_params=_cparams(dimension_semantics=("arbitrary",)),
    )(parts)


def _adamw_update(w, gg, m, v):
    nm = ADAM_B1 * m + (1.0 - ADAM_B1) * gg
    nv = ADAM_B2 * v + (1.0 - ADAM_B2) * (gg * gg)
    m_hat = nm / (1.0 - ADAM_B1 ** ADAM_STEP)
    v_hat = nv / (1.0 - ADAM_B2 ** ADAM_STEP)
    return -ADAM_LR * (m_hat / (jnp.sqrt(v_hat) + ADAM_EPS) + ADAM_WD * w), nm, nv


def _adamw(w, g, m, v, name):
    r, l = w.shape
    tr = _row_tile(r, l)

    def body(w_ref, g_ref, m_ref, v_ref, d_ref, nm_ref, nv_ref):
        d_ref[...], nm_ref[...], nv_ref[...] = _adamw_update(w_ref[...], g_ref[...], m_ref[...], v_ref[...])

    spec = pl.BlockSpec((tr, l), lambda i: (i, 0))
    return pl.pallas_call(
        body, name=name, out_shape=[jax.ShapeDtypeStruct((r, l), F32)] * 3, grid=(r // tr,),
        in_specs=[spec] * 4, out_specs=[spec] * 3,
        compiler_params=_cparams(dimension_semantics=("arbitrary",)),
    )(w, g, m, v)


def _adamw_parts(w, parts, m, v, name):
    nl, r, l = w.shape
    tr = _row_tile(r, l)

    def body(*refs):
        w_ref, p_refs, (m_ref, v_ref, g_ref, d_ref, nm_ref, nv_ref) = refs[0], refs[1:1 + nl], refs[1 + nl:]
        layer = pl.program_id(0)
        gg = _sum_parts(p_refs[0])
        for q in range(1, nl):
            gg = jnp.where(layer == q, _sum_parts(p_refs[q]), gg)
        g_ref[...] = gg
        d_ref[...], nm_ref[...], nv_ref[...] = _adamw_update(w_ref[...], gg, m_ref[...], v_ref[...])

    spec = pl.BlockSpec((None, tr, l), lambda q, i: (q, i, 0))
    pspecs = [pl.BlockSpec((4, tr, l), lambda q, i, k=k: (0, jnp.where(q == k, i, 0), 0)) for k in range(nl)]
    return pl.pallas_call(
        body, name=name, out_shape=[jax.ShapeDtypeStruct((nl, r, l), F32)] * 4, grid=(nl, r // tr),
        in_specs=[spec] + pspecs + [spec, spec], out_specs=[spec] * 4,
        compiler_params=_cparams(dimension_semantics=("arbitrary", "arbitrary")),
    )(w, *parts, m, v)


def _to_rows(pieces, row_multiple):
    flat = jnp.concatenate([p.reshape(-1) for p in pieces])
    rows = -(-flat.shape[0] // LANES)
    rows = -(-rows // row_multiple) * row_multiple
    flat = jnp.pad(flat, (0, rows * LANES - flat.shape[0]))
    return flat.reshape(rows, LANES)


def _split_rows(rows, shapes):
    flat = rows.reshape(-1)
    out, off = [], 0
    for s in shapes:
        n = int(np.prod(s))
        out.append(flat[off:off + n].reshape(s))
        off += n
    return out


def _mod_fwd(cond, w_mod, b_my, name):
    nl, _, ncol = w_mod.shape

    def body(a_ref, w_ref, b_ref, o_ref):
        a = a_ref[...]
        s = a * _sigmoid(a)
        for i in range(nl):
            o_ref[i] = _dot(s, w_ref[i]) + b_ref[i]

    return pl.pallas_call(
        body, name=name, out_shape=jax.ShapeDtypeStruct((nl, 16, ncol), F32),
        compiler_params=_cparams(),
    )(cond, w_mod, b_my)


def _mod_bwd(cond, dm_all, dm_my, w_mod, name):
    nl, _, ncol = w_mod.shape

    def body(a_ref, dma_ref, dmm_ref, w_ref, gw_ref, gb_ref, gc_ref):
        a = a_ref[...]
        sg = _sigmoid(a)
        s = a * sg
        for i in range(nl):
            gw_ref[i] = _dot_tn(s, dmm_ref[i])
            gb_ref[i] = jnp.sum(dma_ref[i], axis=0, keepdims=True)
        back = _dot_nt(dmm_ref[0], w_ref[0])
        dsilu = sg * (1.0 + a * (1.0 - sg))
        gc_ref[...] = jnp.sum(back[8:16] * dsilu[8:16], axis=0, keepdims=True)

    return pl.pallas_call(
        body, name=name,
        out_shape=[jax.ShapeDtypeStruct((nl, D_MODEL, ncol), F32), jax.ShapeDtypeStruct((nl, 1, 3 * D_MODEL), F32),
                   jax.ShapeDtypeStruct((1, D_MODEL), F32)],
        compiler_params=_cparams(),
    )(cond, dm_all, dm_my, w_mod)


def _in_proj(xt, sc, sh, wg, name, sides=()):
    t = xt.shape[0]
    tm = min(TM_MM, t)

    def body(x_ref, sc_ref, sh_ref, w_ref, u_ref, g_ref):
        h = (x_ref[...] * (1.0 + sc_ref[...]) + sh_ref[...]).astype(MXU_DTYPE)
        for k in range(N_WBLK):
            o = jnp.dot(h, w_ref[k], preferred_element_type=F32)
            if k < N_WBLK // 2:
                u_ref[:, k * WBLK:(k + 1) * WBLK] = o
            else:
                kk = k - N_WBLK // 2
                g_ref[:, kk * WBLK:(kk + 1) * WBLK] = o.astype(ACT_DTYPE)

    row = pl.BlockSpec((1, D_MODEL), lambda i: (0, 0))
    return _call_with_sides(
        body, sides, name=name,
        out_shape=[jax.ShapeDtypeStruct((t, D_INNER), F32), jax.ShapeDtypeStruct((t, D_INNER), ACT_DTYPE)],
        grid=(t // tm,),
        in_specs=[pl.BlockSpec((tm, D_MODEL), lambda i: (i, 0)), row, row,
                  pl.BlockSpec((N_WBLK, D_MODEL, WBLK), lambda i: (0, 0, 0), pipeline_mode=pl.Buffered(1))],
        out_specs=[pl.BlockSpec((tm, D_INNER), lambda i: (i, 0))] * 2, scratch_shapes=[],
        compiler_params=_cparams(dimension_semantics=("arbitrary",)), args=[xt, sc, sh, wg])


def _halo_maps(nt, tm, n_rows8, pos):
    per = tm // SUBLANES
    prev = lambda cb, i: (jnp.maximum(pos(i) * per - 1, 0), cb)
    nxt = lambda cb, i: (jnp.minimum((pos(i) + 1) * per, n_rows8 - 1), cb)
    return prev, nxt


def _conv_taps(u, prev8, next8, is_first, is_last):
    pz = jnp.where(is_first, 0.0, 1.0)
    nz = jnp.where(is_last, 0.0, 1.0)
    return _shifted(u, prev8 * pz, next8 * nz, [-2, -1, 1])


def _lru_gates(uv, wa_ref, wx_ref, ba, bx, cl, g):
    sl = slice(g * LANES, (g + 1) * LANES)
    uvg = uv[:, sl]
    r = _sigmoid(_dot(uvg, wa_ref[g]) + ba[:, sl])
    ii = _sigmoid(_dot(uvg, wx_ref[g]) + bx[:, sl])
    la = cl[:, sl] * r
    a = jnp.exp(la)
    q = jnp.tanh(-la) * (1.0 + a * a)
    rs = lax.rsqrt(jnp.maximum(q, SQRT_FLOOR))
    return uvg, r, ii, a, q * rs, rs


def _scan_rows(seg):
    return -(-(SCAN_ROW_T * (seg - 1) + SCAN_ROW_J * (N_SEG - 1) + 1) // SUBLANES) * SUBLANES


def _seg_chunk(j, c):
    return pl.ds(SCAN_ROW_T * SUBLANES * c + SCAN_ROW_J * j, SUBLANES, stride=SCAN_ROW_T)


def _seg_scatter(ref, g, seg, value):
    for j in range(N_SEG):
        for c in range(seg // SUBLANES):
            r0 = j * seg + SUBLANES * c
            ref[g, _seg_chunk(j, c), :] = value[r0:r0 + SUBLANES]


def _scan_tile(a_s, b_s, carry_ref, write_out, seg, reverse):
    n_g = a_s.shape[0]
    unroll = SCAN_UNROLL if seg % SCAN_UNROLL == 0 else 1

    n_trips = seg // unroll

    def steps(k, state):
        hs, cs = list(state[0]), list(state[1])
        base = ((n_trips - 1 - k) if reverse else k) * unroll
        for q in (range(unroll - 1, -1, -1) if reverse else range(unroll)):
            t = base + q
            rows = pl.ds(t * SCAN_ROW_T, N_SEG, stride=SCAN_ROW_J)
            for g in range(n_g):
                a = a_s[g, rows, :]
                b = b_s[g, rows, :]
                hs[g] = a * hs[g] + b
                cs[g] = a * cs[g]
                b_s[g, rows, :] = hs[g]
                a_s[g, rows, :] = cs[g]
        return tuple(hs), tuple(cs)

    zeros = tuple(jnp.zeros((N_SEG, LANES), F32) for _ in range(n_g))
    ones = tuple(jnp.ones((N_SEG, LANES), F32) for _ in range(n_g))
    h_fin, a_fin = lax.fori_loop(0, seg // unroll, steps, (zeros, ones))

    order = list(range(N_SEG - 1, -1, -1)) if reverse else list(range(N_SEG))
    for g in range(n_g):
        carry = carry_ref[:, g * LANES:(g + 1) * LANES]
        for j in order:
            for c in range(seg // SUBLANES):
                rows = _seg_chunk(j, c)
                write_out(j, c, g, b_s[g, rows, :] + a_s[g, rows, :] * carry)
            carry = a_fin[g][j:j + 1] * carry + h_fin[g][j:j + 1]
        carry_ref[:, g * LANES:(g + 1) * LANES] = carry


def _lru_specs(s, tm, cb, direction_pos, nt):
    n_rows8 = s // SUBLANES
    prev, nxt = _halo_maps(nt, tm, n_rows8, direction_pos)
    tile = pl.BlockSpec((tm, cb), lambda c, i: (direction_pos(i), c))
    return tile, pl.BlockSpec((SUBLANES, cb), prev), pl.BlockSpec((SUBLANES, cb), nxt)


def _lru_param_specs(cb, d):
    n_g = cb // LANES
    vec = pl.BlockSpec((1, cb), lambda c, i: (0, c))
    dvec = pl.BlockSpec((None, 1, cb), lambda c, i: (d, 0, c))
    wmat = pl.BlockSpec((None, n_g, LRU_BLOCK, LRU_BLOCK), lambda c, i: (d, c, 0, 0))
    return vec, dvec, wmat


def _lru_fwd(src, h0, p, d, name, conv, sides=()):
    s = src.shape[0]
    tm = min(TM_LRU_FWD, s)
    cb = CB_LRU
    n_g = cb // LANES
    nt = s // tm
    seg = tm // N_SEG
    pos = (lambda i: i) if d == 0 else (lambda i: nt - 1 - i)

    def body(*refs):
        refs = list(refs)
        u_ref = refs.pop(0)
        if conv:
            up_ref, un_ref, cw_ref, cbias_ref = [refs.pop(0) for _ in range(4)]
        wa_ref, wx_ref, ba_ref, bx_ref, lam_ref, h0_ref, h_ref, hc_ref = [refs.pop(0) for _ in range(8)]
        uv_ref = refs.pop(0) if conv else None
        a_s, b_s = refs
        i = pl.program_id(1)
        tp = pos(i)

        @pl.when(i == 0)
        def _():
            hc_ref[...] = h0_ref[...]

        if conv:
            u_t = u_ref[...]
            um2, um1, up1 = _conv_taps(u_t, up_ref[...], un_ref[...], tp == 0, tp == nt - 1)
            cw = cw_ref[...]
            uv_ref[...] = um2 * cw[0:1] + um1 * cw[1:2] + u_t * cw[2:3] + up1 * cw[3:4] + cbias_ref[...]
        src_ref = uv_ref if conv else u_ref
        cl = LRU_C * _log_sigmoid(lam_ref[...])
        ba, bx = ba_ref[...], bx_ref[...]
        for g in range(n_g):
            uvg, r, ii, a, sq, _ = _lru_gates(src_ref, wa_ref, wx_ref, ba, bx, cl, g)
            b = sq * (ii * uvg)
            _seg_scatter(a_s, g, seg, a)
            _seg_scatter(b_s, g, seg, b)

        def write_out(j, c, g, h):
            h_ref[pl.ds(j * seg + SUBLANES * c, SUBLANES), pl.ds(g * LANES, LANES)] = h

        _scan_tile(a_s, b_s, hc_ref, write_out, seg, reverse=(d == 1))

    tile, prev, nxt = _lru_specs(s, tm, cb, pos, nt)
    vec, dvec, wmat = _lru_param_specs(cb, d)
    wide = jax.ShapeDtypeStruct((s, D_INNER), F32)
    conv_specs = [prev, nxt, pl.BlockSpec((4, cb), lambda c, i: (0, c)), vec] if conv else []
    conv_args = [src, src, p["conv_w"], p["conv_b"]] if conv else []
    return _call_with_sides(
        body, sides, name=name,
        out_shape=[wide, jax.ShapeDtypeStruct((1, D_INNER), F32)] + ([wide] if conv else []),
        grid=(D_INNER // cb, nt),
        in_specs=[tile] + conv_specs + [wmat, wmat, dvec, dvec, dvec, vec],
        out_specs=[tile, vec] + ([tile] if conv else []),
        scratch_shapes=[pltpu.VMEM((n_g, _scan_rows(seg), LANES), F32)] * 2,
        compiler_params=_cparams(dimension_semantics=("arbitrary", "arbitrary")),
        args=[src, *conv_args, p["wa"], p["wx"], p["ba"], p["bx"], p["lam"], h0])


def _lru_bwd(uv, dh, h, h0, lam_in, p, d, name, sides=()):
    s = uv.shape[0]
    tm = min(TM_LRU, s)
    cb = CB_LRU
    n_g = cb // LANES
    nt = s // tm
    seg = tm // N_SEG
    pos = (lambda i: nt - 1 - i) if d == 0 else (lambda i: i)

    def body(uv_ref, dh_ref, h_ref, hh_ref, wa_ref, wx_ref, ba_ref, bx_ref,
             lam_ref, h0_ref, lin_ref, duv_ref, gwa_ref, gwx_ref, gv_ref, lc_ref, a_s, b_s, lp_s,
             r_s, i_s, q_s, rq_s, a_keep):
        i = pl.program_id(1)
        tp = pos(i)

        @pl.when(i == 0)
        def _():
            lc_ref[...] = lin_ref[...]
            gwa_ref[...] = jnp.zeros_like(gwa_ref)
            gwx_ref[...] = jnp.zeros_like(gwx_ref)
            gv_ref[...] = jnp.zeros_like(gv_ref)

        uv = uv_ref[...]
        lam = lam_ref[...]
        cl = LRU_C * _log_sigmoid(lam)
        ba, bx = ba_ref[...], bx_ref[...]
        dh_t = dh_ref[...].astype(F32)
        carry_in = lc_ref[...]
        for g in range(n_g):
            sl = slice(g * LANES, (g + 1) * LANES)
            _, r, ii, a, sq, rs = _lru_gates(uv, wa_ref, wx_ref, ba, bx, cl, g)
            r_s[:, sl], i_s[:, sl], q_s[:, sl], rq_s[:, sl], a_keep[:, sl] = r, ii, sq, rs, a
            b = a * dh_t[:, sl]
            _seg_scatter(a_s, g, seg, a)
            _seg_scatter(b_s, g, seg, b)

        def write_out(j, c, g, v):
            lp_s[pl.ds(j * seg + SUBLANES * c, SUBLANES), pl.ds(g * LANES, LANES)] = v

        _scan_tile(a_s, b_s, lc_ref, write_out, seg, reverse=(d == 0))

        h_t = h_ref[...]
        hh = hh_ref[...]
        if d == 0:
            edge = jnp.where(tp == 0, h0_ref[...], hh[7:8])
            h_prev = _shift_down(h_t, edge)
            lam_t = dh_t + _shift_up(lp_s[...], carry_in)
        else:
            edge = jnp.where(tp == nt - 1, h0_ref[...], hh[0:1])
            h_prev = _shift_up(h_t, edge)
            lam_t = dh_t + _shift_down(lp_s[...], carry_in)

        dsig = LRU_C * _sigmoid(-lam)
        for g in range(n_g):
            sl = slice(g * LANES, (g + 1) * LANES)
            uvg, r, ii, a, sq = uv[:, sl], r_s[:, sl], i_s[:, sl], a_keep[:, sl], q_s[:, sl]
            lt = lam_t[:, sl]
            ls = lt * sq
            dla = (lt * a) * (h_prev[:, sl] - (ii * uvg) * (a * rq_s[:, sl]))
            dzr = (dla * cl[:, sl]) * r * (1.0 - r)
            dzi = (ls * uvg) * ii * (1.0 - ii)
            duv_ref[:, sl] = ls * ii + _dot_nt(dzr, wa_ref[g]) + _dot_nt(dzi, wx_ref[g])
            gwa_ref[g] += _dot_tn(uvg, dzr)
            gwx_ref[g] += _dot_tn(uvg, dzi)
            gv_ref[0:1, sl] += _rowsum(dzr)
            gv_ref[1:2, sl] += _rowsum(dzi)
            gv_ref[2:3, sl] += _rowsum(dla * r) * dsig[:, sl]

    tile, prev, nxt = _lru_specs(s, tm, cb, pos, nt)
    vec, dvec, wmat = _lru_param_specs(cb, d)
    hh_spec = prev if d == 0 else nxt
    gw_spec = pl.BlockSpec((n_g, LRU_BLOCK, LRU_BLOCK), lambda c, i: (c, 0, 0))
    n_blk = D_INNER // LRU_BLOCK
    return _call_with_sides(
        body, sides, name=name,
        out_shape=[jax.ShapeDtypeStruct((s, D_INNER), F32),
                   jax.ShapeDtypeStruct((n_blk, LRU_BLOCK, LRU_BLOCK), F32),
                   jax.ShapeDtypeStruct((n_blk, LRU_BLOCK, LRU_BLOCK), F32),
                   jax.ShapeDtypeStruct((SUBLANES, D_INNER), F32),
                   jax.ShapeDtypeStruct((1, D_INNER), F32)],
        grid=(D_INNER // cb, nt),
        in_specs=[tile, tile, tile, hh_spec, wmat, wmat, dvec, dvec, dvec, vec, vec],
        out_specs=[tile, gw_spec, gw_spec, pl.BlockSpec((SUBLANES, cb), lambda c, i: (0, c)), vec],
        scratch_shapes=[pltpu.VMEM((n_g, _scan_rows(seg), LANES), F32)] * 2 + [pltpu.VMEM((tm, cb), F32)] * 6,
        compiler_params=_cparams(dimension_semantics=("arbitrary", "arbitrary")),
        args=[uv, dh, h, h, p["wa"], p["wx"], p["ba"], p["bx"], p["lam"], h0, lam_in])


def _out0(hf, hb, g, xt, gt, wo, lg, lb, name):
    t = xt.shape[0]
    tm = min(TM_MM, t)

    def body(hf_ref, hb_ref, g_ref, x_ref, gt_ref, w_ref, lg_ref, lb_ref, x1_ref, br_ref):
        br = None
        for k in range(D_INNER // WBLK):
            sl = slice(k * WBLK, (k + 1) * WBLK)
            gg = g_ref[:, sl].astype(F32)
            p = (hf_ref[:, sl] + hb_ref[:, sl]) * (gg * _sigmoid(gg))
            part = _dot(p, w_ref[sl, :])
            br = part if br is None else br + part
        z = ALPHA * x_ref[...] + gt_ref[...] * br
        xhat, _ = _layer_norm_stats(z)
        x1_ref[...] = xhat * lg_ref[...] + lb_ref[...]
        br_ref[...] = br

    wide = pl.BlockSpec((tm, D_INNER), lambda i: (i, 0))
    nar = pl.BlockSpec((tm, D_MODEL), lambda i: (i, 0))
    row = pl.BlockSpec((1, D_MODEL), lambda i: (0, 0))
    return pl.pallas_call(
        body, name=name, out_shape=[jax.ShapeDtypeStruct((t, D_MODEL), F32)] * 2, grid=(t // tm,),
        in_specs=[wide, wide, wide, nar, row,
                  pl.BlockSpec((D_INNER, D_MODEL), lambda i: (0, 0), pipeline_mode=pl.Buffered(1)), row, row],
        out_specs=[nar, nar],
        compiler_params=_cparams(dimension_semantics=("arbitrary",)),
    )(hf, hb, g, xt, gt, wo, lg, lb)


def _unrolled_loop(n, fn, unroll=4):
    while n % unroll:
        unroll //= 2

    def trip(k, carry):
        for q in range(unroll):
            fn(k * unroll + q)
        return carry
    lax.fori_loop(0, n // unroll, trip, 0)


def _window(n, w):
    t = np.arange(n)
    return np.clip(t - w // 2, 0, n), np.clip(t + w // 2, 0, n)


def _pool_tables(n_rows, transpose):
    boxes, inv_c, inv_r = [], [], []
    for w in POOL_WINDOWS:
        lo, hi = _window(GRID_W, w)
        m = np.zeros((GRID_W, GRID_W), np.float32)
        for r in range(GRID_W):
            m[r, lo[r]:hi[r]] = 1.0
        m = np.kron(np.eye(POOL_TOK // GRID_W, dtype=np.float32), m)
        boxes.append(m.T if transpose else m)
        inv_c.append(np.broadcast_to((1.0 / (hi - lo).astype(np.float32))[:, None], (GRID_W, LANES)))
        lo_r, hi_r = _window(n_rows, w)
        inv_r.append(1.0 / (hi_r - lo_r).astype(np.float32))
    return (jnp.asarray(np.stack(boxes), MXU_DTYPE), jnp.asarray(np.stack(inv_c), F32),
            jnp.asarray(np.stack(inv_r), F32))


def _pool_mix(xin, transpose, out_dtype, name):
    s = xin.shape[0]
    n_rows = s // GRID_W
    pad_t = SUBLANES * GRID_W
    rows_per_blk = POOL_TOK // GRID_W
    n_slab = D_INNER // LANES
    slabs_per_group = POOL_GROUP // LANES
    n_win = len(POOL_WINDOWS)
    boxes, inv_c, inv_r = _pool_tables(n_rows, transpose)

    def body(invr_ref, box_ref, invc_ref, x_ref, o_ref, pad_s):
        k = pl.program_id(0) // slabs_per_group
        pad_s[pl.ds(0, pad_t), :] = jnp.zeros((pad_t, LANES), F32)
        pad_s[pl.ds(pad_t + s, pad_t), :] = jnp.zeros((pad_t, LANES), F32)

        for kk, w in enumerate(POOL_WINDOWS):
            half = w // 2
            offsets = list(range(-(half - 1), half + 1)) if transpose else list(range(-half, half))

            @pl.when(k == kk)
            def _():
                inv_col = invc_ref[kk]

                def col_box(b):
                    st = pl.multiple_of(b * POOL_TOK, POOL_TOK)
                    xb = x_ref[pl.ds(st, POOL_TOK), :]
                    if transpose:
                        xb = xb * jnp.concatenate(
                            [inv_col * invr_ref[kk, b * rows_per_blk + q] for q in range(rows_per_blk)], axis=0)
                    hi = xb.astype(MXU_DTYPE)
                    lo = (xb - hi.astype(F32)).astype(MXU_DTYPE)
                    both = jnp.dot(box_ref[kk], jnp.concatenate([hi, lo], axis=1), preferred_element_type=F32)
                    pad_s[pl.ds(pad_t + st, POOL_TOK), :] = both[:, :LANES] + both[:, LANES:]
                _unrolled_loop(s // POOL_TOK, col_box)

                def row_box(r):
                    st = pl.multiple_of(r * GRID_W, GRID_W)
                    acc = pad_s[pl.ds(pad_t + st + offsets[0] * GRID_W, GRID_W), :]
                    for o in offsets[1:]:
                        acc = acc + pad_s[pl.ds(pad_t + st + o * GRID_W, GRID_W), :]
                    if not transpose:
                        acc = acc * (inv_col * invr_ref[kk, r])
                    o_ref[pl.ds(st, GRID_W), :] = (acc - x_ref[pl.ds(st, GRID_W), :]).astype(out_dtype)
                _unrolled_loop(n_rows, row_box)

    slab = pl.BlockSpec((s, LANES), lambda i: (0, i))
    return pl.pallas_call(
        body, name=name, out_shape=jax.ShapeDtypeStruct((s, D_INNER), out_dtype), grid=(n_slab,),
        in_specs=[pl.BlockSpec(memory_space=pltpu.SMEM),
                  pl.BlockSpec((n_win, POOL_TOK, POOL_TOK), lambda i: (0, 0, 0)),
                  pl.BlockSpec((n_win, GRID_W, LANES), lambda i: (0, 0, 0)), slab],
        out_specs=slab,
        scratch_shapes=[pltpu.VMEM((s + 2 * pad_t, LANES), F32)],
        compiler_params=_cparams(dimension_semantics=("arbitrary",)),
    )(inv_r, boxes, inv_c, xin)


def _out1(dmix, pw, ps, g, x1, gt, wo, lg, lb, tgt, name):
    t = x1.shape[0]
    tm = min(TM_MM, t)
    n_grp = len(POOL_WINDOWS)

    def body(d_ref, pw_ref, ps_ref, g_ref, x1_ref, gt_ref, w_ref, lg_ref, lb_ref, tgt_ref, dz_ref, st_ref):
        @pl.when(pl.program_id(0) == 0)
        def _():
            st_ref[...] = jnp.zeros_like(st_ref)

        br = jnp.zeros((tm, D_MODEL), F32)
        for k in range(n_grp):
            sl = slice(k * POOL_GROUP, (k + 1) * POOL_GROUP)
            y = jnp.dot(d_ref[:, sl], pw_ref[k], preferred_element_type=F32) * ps_ref[:, sl]
            gg = g_ref[:, sl].astype(F32)
            br = br + _dot(y * (gg * _sigmoid(gg)), w_ref[sl, :])
        z = ALPHA * x1_ref[...] + gt_ref[...] * br
        xhat, rstd = _layer_norm_stats(z)
        lg_v = lg_ref[...]
        err = xhat * lg_v + lb_ref[...] - tgt_ref[...]
        dy = err * (1.0 / D_MODEL)
        dz = _layer_norm_bwd(dy, xhat, rstd, lg_v)
        dz_ref[...] = dz
        st_ref[0:1, :] += _rowsum(dy * xhat)
        st_ref[1:2, :] += _rowsum(dy)
        st_ref[2:3, :] += _rowsum(dz * br)
        st_ref[3:4, :] += _rowsum(err * err)

    wide = pl.BlockSpec((tm, D_INNER), lambda i: (i, 0))
    nar = pl.BlockSpec((tm, D_MODEL), lambda i: (i, 0))
    row = pl.BlockSpec((1, D_MODEL), lambda i: (0, 0))
    return pl.pallas_call(
        body, name=name,
        out_shape=[jax.ShapeDtypeStruct((t, D_MODEL), F32), jax.ShapeDtypeStruct((SUBLANES, D_MODEL), F32)],
        grid=(t // tm,),
        in_specs=[wide, pl.BlockSpec((n_grp, POOL_GROUP, POOL_GROUP), lambda i: (0, 0, 0)),
                  pl.BlockSpec((1, D_INNER), lambda i: (0, 0)), wide, nar, row,
                  pl.BlockSpec((D_INNER, D_MODEL), lambda i: (0, 0), pipeline_mode=pl.Buffered(1)), row, row, nar],
        out_specs=[nar, pl.BlockSpec((SUBLANES, D_MODEL), lambda i: (0, 0))],
        compiler_params=_cparams(dimension_semantics=("arbitrary",)),
    )(dmix, pw, ps, g, x1, gt, wo, lg, lb, tgt)


def _flush(acc, out_hbm, sem):
    cp = pltpu.make_async_copy(acc, out_hbm, sem)
    cp.start()
    cp.wait()


def _bout1(dz, dmix, g, pw, ps, gt, wo, name):
    t = dz.shape[0]
    tm = min(TM_MM, t)
    nt = t // tm
    n_grp = len(POOL_WINDOWS)

    def body(dz_ref, d_ref, g_ref, pw_ref, ps_ref, gt_ref, w_ref, dd_ref, dg_ref, gwo_hbm, gpw_hbm, gps_ref,
             gwo_acc, gpw_acc, sems):
        i = pl.program_id(0)

        @pl.when(i == 0)
        def _():
            gwo_acc[...] = jnp.zeros_like(gwo_acc)
            gpw_acc[...] = jnp.zeros_like(gpw_acc)
            gps_ref[...] = jnp.zeros_like(gps_ref)

        db = (gt_ref[...] * dz_ref[...]).astype(MXU_DTYPE)
        for k in range(n_grp):
            sl = slice(k * POOL_GROUP, (k + 1) * POOL_GROUP)
            dk = d_ref[:, sl]
            po = jnp.dot(dk, pw_ref[k], preferred_element_type=F32)
            psk = ps_ref[:, sl]
            y = po * psk
            gg = g_ref[:, sl].astype(F32)
            sg = _sigmoid(gg)
            silu = gg * sg
            gwo_acc[sl, :] += _dot_tn(y * silu, db)
            dp = _dot_nt(db, w_ref[sl, :])
            dy = dp * silu
            dg_ref[:, sl] = (dp * y * (sg * (1.0 + gg * (1.0 - sg)))).astype(MXU_DTYPE)
            gps_ref[0:1, sl] += _rowsum(dy * po)
            dpo = (dy * psk).astype(MXU_DTYPE)
            gpw_acc[k] += _dot_tn(dk, dpo)
            dd_ref[:, sl] = _dot_nt(dpo, pw_ref[k])

        @pl.when(i == nt - 1)
        def _():
            _flush(gwo_acc, gwo_hbm, sems.at[0])
            _flush(gpw_acc, gpw_hbm, sems.at[1])

    wide = pl.BlockSpec((tm, D_INNER), lambda i: (i, 0))
    nar = pl.BlockSpec((tm, D_MODEL), lambda i: (i, 0))
    return pl.pallas_call(
        body, name=name,
        out_shape=[jax.ShapeDtypeStruct((t, D_INNER), F32), jax.ShapeDtypeStruct((t, D_INNER), MXU_DTYPE),
                   jax.ShapeDtypeStruct((D_INNER, D_MODEL), F32),
                   jax.ShapeDtypeStruct((n_grp, POOL_GROUP, POOL_GROUP), F32),
                   jax.ShapeDtypeStruct((SUBLANES, D_INNER), F32)],
        grid=(nt,),
        in_specs=[nar, wide, wide,
                  pl.BlockSpec((n_grp, POOL_GROUP, POOL_GROUP), lambda i: (0, 0, 0), pipeline_mode=pl.Buffered(1)),
                  pl.BlockSpec((1, D_INNER), lambda i: (0, 0)), pl.BlockSpec((1, D_MODEL), lambda i: (0, 0)),
                  pl.BlockSpec((D_INNER, D_MODEL), lambda i: (0, 0), pipeline_mode=pl.Buffered(1))],
        out_specs=[wide, wide, ANY, ANY, pl.BlockSpec((SUBLANES, D_INNER), lambda i: (0, 0))],
        scratch_shapes=[pltpu.VMEM((D_INNER, D_MODEL), F32), pltpu.VMEM((n_grp, POOL_GROUP, POOL_GROUP), F32),
                        pltpu.SemaphoreType.DMA((2,))],
        compiler_params=_cparams(dimension_semantics=("arbitrary",)),
    )(dz, dmix, g, pw, ps, gt, wo)


def _bout0(dx1, xt, br0, lg, hf, hb, g, gt, wo, name, sides=()):
    t = dx1.shape[0]
    tm = min(TM_BWD, t)
    nt = t // tm

    def body(dx_ref, x_ref, br_ref, lg_ref, hf_ref, hb_ref, g_ref, gt_ref, w_ref,
             dz_ref, dy_ref, dg_ref, gwo_hbm, st_ref, gwo_acc, sem):
        i = pl.program_id(0)

        @pl.when(i == 0)
        def _():
            gwo_acc[...] = jnp.zeros_like(gwo_acc)
            st_ref[...] = jnp.zeros_like(st_ref)

        dx = dx_ref[...]
        br = br_ref[...]
        gate = gt_ref[...]
        xhat, rstd = _layer_norm_stats(ALPHA * x_ref[...] + gate * br)
        dz = _layer_norm_bwd(dx, xhat, rstd, lg_ref[...])
        dz_ref[...] = dz
        st_ref[0:1, :] += _rowsum(dx * xhat)
        st_ref[1:2, :] += _rowsum(dx)
        st_ref[2:3, :] += _rowsum(dz * br)
        db = (gate * dz).astype(MXU_DTYPE)
        for k in range(D_INNER // WBLK):
            sl = slice(k * WBLK, (k + 1) * WBLK)
            y = hf_ref[:, sl] + hb_ref[:, sl]
            gg = g_ref[:, sl].astype(F32)
            sg = _sigmoid(gg)
            silu = gg * sg
            gwo_acc[sl, :] += _dot_tn(y * silu, db)
            dp = _dot_nt(db, w_ref[sl, :])
            dy_ref[:, sl] = (dp * silu).astype(ACT_DTYPE)
            dg_ref[:, sl] = (dp * y * (sg * (1.0 + gg * (1.0 - sg)))).astype(MXU_DTYPE)

        @pl.when(i == nt - 1)
        def _():
            _flush(gwo_acc, gwo_hbm, sem)

    wide = pl.BlockSpec((tm, D_INNER), lambda i: (i, 0))
    nar = pl.BlockSpec((tm, D_MODEL), lambda i: (i, 0))
    row = pl.BlockSpec((1, D_MODEL), lambda i: (0, 0))
    return _call_with_sides(
        body, sides, name=name,
        out_shape=[jax.ShapeDtypeStruct((t, D_MODEL), F32), jax.ShapeDtypeStruct((t, D_INNER), ACT_DTYPE),
                   jax.ShapeDtypeStruct((t, D_INNER), MXU_DTYPE), jax.ShapeDtypeStruct((D_INNER, D_MODEL), F32),
                   jax.ShapeDtypeStruct((SUBLANES, D_MODEL), F32)],
        grid=(nt,),
        in_specs=[nar, nar, nar, row, wide, wide, wide, row,
                  pl.BlockSpec((D_INNER, D_MODEL), lambda i: (0, 0), pipeline_mode=pl.Buffered(1))],
        out_specs=[nar, wide, wide, ANY, pl.BlockSpec((SUBLANES, D_MODEL), lambda i: (0, 0))],
        scratch_shapes=[pltpu.VMEM((D_INNER, D_MODEL), F32), pltpu.SemaphoreType.DMA(())],
        compiler_params=_cparams(dimension_semantics=("arbitrary",)),
        args=[dx1, xt, br0, lg, hf, hb, g, gt, wo])


def _conv_bwd(duvf, duvb, u, conv_w, name, sides=()):
    s = u.shape[0]
    tm = min(TM_LRU_FWD, s)
    cb = CB_LRU
    nt = s // tm

    def body(df_ref, dfp_ref, dfn_ref, db_ref, dbp_ref, dbn_ref, u_ref, cw_ref, du_ref, cst_ref):
        i = pl.program_id(1)

        @pl.when(i == 0)
        def _():
            cst_ref[...] = jnp.zeros_like(cst_ref)

        first, last = i == 0, i == nt - 1
        pz = jnp.where(first, 0.0, 1.0)
        nz = jnp.where(last, 0.0, 1.0)
        dout = df_ref[...] + db_ref[...]
        dm1, dp1, dp2 = _shifted(dout, (dfp_ref[...] + dbp_ref[...]) * pz, (dfn_ref[...] + dbn_ref[...]) * nz,
                                 [-1, 1, 2])
        cw = cw_ref[...]
        du_ref[...] = (dp2 * cw[0:1] + dp1 * cw[1:2] + dout * cw[2:3] + dm1 * cw[3:4]).astype(MXU_DTYPE)
        u_t = u_ref[...]
        cst_ref[0:1, :] += _rowsum(dp2 * u_t)
        cst_ref[1:2, :] += _rowsum(dp1 * u_t)
        cst_ref[2:3, :] += _rowsum(dout * u_t)
        cst_ref[3:4, :] += _rowsum(dm1 * u_t)
        cst_ref[4:5, :] += _rowsum(dout)

    tile, prev, nxt = _lru_specs(s, tm, cb, lambda i: i, nt)
    return _call_with_sides(
        body, sides, name=name,
        out_shape=[jax.ShapeDtypeStruct((s, D_INNER), MXU_DTYPE), jax.ShapeDtypeStruct((SUBLANES, D_INNER), F32)],
        grid=(D_INNER // cb, nt),
        in_specs=[tile, prev, nxt] * 2 + [tile, pl.BlockSpec((4, cb), lambda c, i: (0, c))],
        out_specs=[tile, pl.BlockSpec((SUBLANES, cb), lambda c, i: (0, c))], scratch_shapes=[],
        compiler_params=_cparams(dimension_semantics=("arbitrary", "arbitrary")),
        args=[duvf, duvf, duvf, duvb, duvb, duvb, u, conv_w])


def _bin(du, dg, xin, dzin, sc, sh, wg, name, gw_init=None, sides=()):
    t = xin.shape[0]
    tm = min(TM_MM, t)
    nt = t // tm
    has_g, has_dx, has_init = dg is not None, dzin is not None, gw_init is not None
    half = N_WBLK // 2
    n_blk = N_WBLK if has_g else half

    def body(*refs):
        refs = list(refs)
        du_ref = refs.pop(0)
        dg_ref = refs.pop(0) if has_g else None
        x_ref = refs.pop(0)
        dz_ref = refs.pop(0) if has_dx else None
        sc_ref, sh_ref, w_ref = refs.pop(0), refs.pop(0), refs.pop(0)
        init_hbm = refs.pop(0) if has_init else None
        dx_ref = refs.pop(0) if has_dx else None
        gw_hbm, st_ref, gw_acc, sem = refs
        i = pl.program_id(0)

        @pl.when(i == 0)
        def _():
            st_ref[...] = jnp.zeros_like(st_ref)
            first_zero = 0
            if has_init:
                _flush(init_hbm, gw_acc.at[pl.ds(0, half)], sem)
                first_zero = half
            for k in range(first_zero, n_blk):
                gw_acc[k] = jnp.zeros((D_MODEL, WBLK), F32)

        xv = x_ref[...]
        scale = 1.0 + sc_ref[...]
        h = (xv * scale + sh_ref[...]).astype(MXU_DTYPE)
        dh = None
        for k in range(n_blk):
            src = du_ref if k < half else dg_ref
            kk = k % half
            dk = src[:, kk * WBLK:(kk + 1) * WBLK]
            gw_acc[k] += _dot_tn(h, dk)
            contrib = _dot_nt(dk, w_ref[k])
            dh = contrib if dh is None else dh + contrib
        st_ref[0:1, :] += _rowsum(dh * xv)
        st_ref[1:2, :] += _rowsum(dh)
        if has_dx:
            dx_ref[...] = ALPHA * dz_ref[...] + dh * scale

        @pl.when(i == nt - 1)
        def _():
            _flush(gw_acc, gw_hbm, sem)

    wide = pl.BlockSpec((tm, D_INNER), lambda i: (i, 0))
    nar = pl.BlockSpec((tm, D_MODEL), lambda i: (i, 0))
    row = pl.BlockSpec((1, D_MODEL), lambda i: (0, 0))
    wspec = pl.BlockSpec((n_blk, D_MODEL, WBLK), lambda i: (0, 0, 0), pipeline_mode=pl.Buffered(1))
    in_specs = ([wide] + ([wide] if has_g else []) + [nar] + ([nar] if has_dx else []) + [row, row, wspec]
                + ([ANY] if has_init else []))
    args = ([du] + ([dg] if has_g else []) + [xin] + ([dzin] if has_dx else []) + [sc, sh, wg]
            + ([gw_init] if has_init else []))
    out_shape = ([jax.ShapeDtypeStruct((t, D_MODEL), F32)] if has_dx else []) + [
        jax.ShapeDtypeStruct((n_blk, D_MODEL, WBLK), F32), jax.ShapeDtypeStruct((SUBLANES, D_MODEL), F32)]
    out_specs = ([nar] if has_dx else []) + [ANY, pl.BlockSpec((SUBLANES, D_MODEL), lambda i: (0, 0))]
    return _call_with_sides(
        body, sides, name=name, out_shape=out_shape, grid=(nt,), in_specs=in_specs, out_specs=out_specs,
        scratch_shapes=[pltpu.VMEM((n_blk, D_MODEL, WBLK), F32), pltpu.SemaphoreType.DMA(())],
        compiler_params=_cparams(dimension_semantics=("arbitrary",)), args=args)


def _blocks_by_device(a, axis):
    shape = a.shape
    a = a.reshape(shape[:axis] + (N_DEV, shape[axis] // N_DEV) + shape[axis + 1:])
    return jnp.moveaxis(a, axis, 0)


def kernel(x, c, ctx, c_ctx, w_mod, b_mod, w_in, w_out, ln_g, ln_b, conv_w, conv_b, lru_wa, lru_ba, lru_wx, lru_bx, lru_lam, pool_w, pool_scale, loss_target, m_c_ctx, m_w_mod, m_b_mod, m_w_in, m_w_out, m_ln_g, m_ln_b, m_conv_w, m_conv_b, m_lru_wa, m_lru_ba, m_lru_wx, m_lru_bx, m_lru_lam, m_pool_w, m_pool_scale, v_c_ctx, v_w_mod, v_b_mod, v_w_in, v_w_out, v_ln_g, v_ln_b, v_conv_w, v_conv_b, v_lru_wa, v_lru_ba, v_lru_wx, v_lru_bx, v_lru_lam, v_pool_w, v_pool_scale):
    xi, yi, ci = _my_pos()
    dev = 4 * xi + 2 * yi + ci
    xt, ctxt, tgt = x[0], ctx[0], loss_target[0]
    n_mod = w_mod.shape[2]

    small_shapes = [(D_MODEL,), conv_w.shape[1:], lru_ba.shape[1:], lru_bx.shape[1:], lru_lam.shape[1:],
                    pool_scale.shape[1:]]
    small = _to_rows([c[0], conv_w[0], lru_ba[0], lru_bx[0], lru_lam[0], pool_scale[0]], SUBLANES)
    small_all, wi0 = _all_gather([small, w_in[0].astype(MXU_DTYPE)], "gather_first")
    pieces = [_split_rows(small_all[k], small_shapes) for k in range(N_DEV)]
    c_all = jnp.stack([p[0] for p in pieces])
    conv_w_f = jnp.concatenate([p[1] for p in pieces], axis=-1)
    lru_ba_f = jnp.concatenate([p[2] for p in pieces], axis=-1)[:, None, :]
    lru_bx_f = jnp.concatenate([p[3] for p in pieces], axis=-1)[:, None, :]
    lru_lam_f = jnp.concatenate([p[4] for p in pieces], axis=-1)[:, None, :]
    pool_scale_f = jnp.concatenate([p[5] for p in pieces], axis=-1)[None, :]

    cond = jnp.concatenate([c_all, jnp.broadcast_to(c_ctx[None, :], (N_DEV, D_MODEL))], axis=0)
    b_my = lax.dynamic_slice(b_mod, (0, dev * n_mod), (2, n_mod))[:, None, :]
    mod_part = _mod_fwd(cond, w_mod, b_my, "mod_fwd")
    mod_all, = _all_gather([mod_part], "gather_mod")
    mod = jnp.transpose(mod_all, (1, 2, 0, 3)).reshape(2, 16, 3 * D_MODEL)
    mod_me = lax.dynamic_slice(mod, (0, dev, 0), (2, 1, 3 * D_MODEL))
    sh = [mod_me[i, :, 0:D_MODEL] for i in range(2)]
    sc = [mod_me[i, :, D_MODEL:2 * D_MODEL] for i in range(2)]
    gt = [mod_me[i, :, 2 * D_MODEL:] for i in range(2)]
    shc, scc = mod[0, 8:9, 0:D_MODEL], mod[0, 8:9, D_MODEL:2 * D_MODEL]

    lg = [ln_g[i][None, :] for i in range(2)]
    lb = [ln_b[i][None, :] for i in range(2)]
    lru_p = dict(conv_w=conv_w_f, conv_b=conv_b, wa=lru_wa[0].astype(MXU_DTYPE), wx=lru_wx[0].astype(MXU_DTYPE),
                 ba=lru_ba_f, bx=lru_bx_f, lam=lru_lam_f)
    zero_state = jnp.zeros((1, D_INNER), F32)

    (u0, g0), (wo0,) = _in_proj(xt, sc[0], sh[0], wi0, "in_proj0", sides=[("gather", [w_out[0].astype(MXU_DTYPE)])])
    (uc, _), _ = _in_proj(ctxt, scc, shc, wi0, "in_proj0_ctx")
    (hcf, cf, uvc), _ = _lru_fwd(uc, zero_state, lru_p, 0, "lru_fwd_ctx_f", conv=True)
    (hcb, cbk), _ = _lru_fwd(uvc, zero_state, lru_p, 1, "lru_fwd_ctx_b", conv=False)
    (hf, _, uv0), (wi1,) = _lru_fwd(u0, cf, lru_p, 0, "lru_fwd_f", conv=True,
                                    sides=[("gather", [w_in[1].astype(MXU_DTYPE)])])
    (hb, _), (wo1, pool_w_g) = _lru_fwd(
        uv0, cbk, lru_p, 1, "lru_fwd_b", conv=False,
        sides=[("gather", [w_out[1].astype(MXU_DTYPE), pool_w[0].astype(MXU_DTYPE)])])
    w_in_l = [wi0, wi1]
    w_out_l = [wo0.reshape(D_INNER, D_MODEL), wo1.reshape(D_INNER, D_MODEL)]
    pool_w_f = jnp.transpose(pool_w_g, (1, 0, 2, 3)).reshape(len(POOL_WINDOWS), POOL_GROUP, POOL_GROUP)
    x1, br0 = _out0(hf, hb, g0, xt, gt[0], w_out_l[0], lg[0], lb[0], "out0")
    (u1, g1), _ = _in_proj(x1, sc[1], sh[1], w_in_l[1], "in_proj1")
    dmix = _pool_mix(u1, False, MXU_DTYPE, "pool_fwd")
    dz1, st1 = _out1(dmix, pool_w_f, pool_scale_f, g1, x1, gt[1], w_out_l[1], lg[1], lb[1], tgt, "out1")
    loss_me = jnp.full((1, LANES), (0.5 / D_MODEL) * jnp.sum(st1[3]), F32)

    core = jnp.reshape(ci, (1,)).astype(jnp.int32)
    wo_view = lambda a: a.reshape(N_DEV, D_INNER // N_DEV, D_MODEL)
    pw_view = lambda a: _blocks_by_device(a, 1).reshape(N_DEV, POOL_GROUP // N_DEV * len(POOL_WINDOWS), POOL_GROUP)
    dd, dg1, gwo1, gpw, gps = _bout1(dz1, dmix, g1, pool_w_f, pool_scale_f, gt[1], w_out_l[1], "bwd_out1")
    du1 = _pool_mix(dd, True, MXU_DTYPE, "pool_bwd")
    (dx1, gwi1, stb1), _ = _bin(du1, dg1, x1, dz1, sc[1], sh[1], w_in_l[1], "bwd_in1")
    bufs1 = [gwi1, wo_view(gwo1), pw_view(gpw)]
    (dz0, dy0, dg0, gwo0, stl0), recv1 = _bout0(dx1, xt, br0, lg[0], hf, hb, g0, gt[0], w_out_l[0], "bwd_out0",
                                                sides=[("sibling", bufs1)])
    pairs1 = [_pair_sum(b, r, core, "reduce_pair_" + n)
              for b, r, n in zip(bufs1, recv1, ["w_in1", "w_out1", "pool_w"])]
    (duvf, gwa_f, gwx_f, gv_f, dh0f), (p_wi1, p_wo1, p_pw, recv_wo0) = _lru_bwd(
        uv0, dy0, hf, cf, zero_state, lru_p, 0, "lru_bwd_f", sides=[("chips", pairs1), ("sibling", [wo_view(gwo0)])])
    pair_wo0 = _pair_sum(wo_view(gwo0), recv_wo0, core, "reduce_pair_w_out0")
    (duvb, gwa_b, gwx_b, gv_b, dh0b), (p_wo0,) = _lru_bwd(
        uv0, dy0, hb, cbk, zero_state, lru_p, 1, "lru_bwd_b", sides=[("chips", [pair_wo0])])
    zero_dh = jnp.zeros(uc.shape, ACT_DTYPE)
    (ducf, gwa_cf, gwx_cf, gv_cf, _), _ = _lru_bwd(uvc, zero_dh, hcf, zero_state, dh0f, lru_p, 0, "lru_bwd_ctx_f")
    (ducb, gwa_cb, gwx_cb, gv_cb, _), _ = _lru_bwd(uvc, zero_dh, hcb, zero_state, dh0b, lru_p, 1, "lru_bwd_ctx_b")

    def pack(sharded, replicated):
        sh_sizes = [int(np.prod(a.shape[1:])) for a in sharded]
        rep_sizes = [a.shape[0] // N_DEV for a in replicated]
        n_flat = sum(sh_sizes) + sum(rep_sizes)
        rows = -(-(-(-n_flat // LANES)) // FLAT_ROWS) * FLAT_ROWS
        buf = jnp.concatenate([a.reshape(N_DEV, -1) for a in sharded + replicated], axis=1)
        return jnp.pad(buf, ((0, 0), (0, rows * LANES - n_flat))).reshape(N_DEV, rows, LANES), sh_sizes, rep_sizes

    def unpack(reduced, sh_sizes, rep_sizes, sh_shapes):
        flat = reduced.reshape(-1)
        offs = np.cumsum([0] + sh_sizes)
        mine = [flat[offs[k]:offs[k + 1]].reshape(s) for k, s in enumerate(sh_shapes)]
        return mine, _to_rows([flat[offs[-1]:offs[-1] + sum(rep_sizes)]], SUBLANES)

    def spread(rep_all, rep_sizes, shapes):
        flat = rep_all.reshape(N_DEV, -1)
        offs = np.cumsum([0] + rep_sizes)
        return [flat[:, offs[k]:offs[k + 1]].reshape(s) for k, s in enumerate(shapes)]

    (du0, cst0), _ = _conv_bwd(duvf, duvb, u0, conv_w_f, "conv_bwd")
    (duc, cstc), _ = _conv_bwd(ducf, ducb, uc, conv_w_f, "conv_bwd_ctx")
    (gwic, stc), _ = _bin(duc, None, ctxt, None, scc, shc, w_in_l[0][:N_WBLK // 2], "bwd_in0_ctx")
    (gx, gwi0, stb0), _ = _bin(du0, dg0, xt, dz0, sc[0], sh[0], w_in_l[0], "bwd_in0", gw_init=gwic)

    zero_row = jnp.zeros((1, D_MODEL), F32)
    dm_me = jnp.stack([
        jnp.concatenate([jnp.concatenate([stb0[1:2], stb0[0:1], stl0[2:3]], axis=1),
                         jnp.concatenate([stc[1:2], stc[0:1], zero_row], axis=1)], axis=0),
        jnp.concatenate([jnp.concatenate([stb1[1:2], stb1[0:1], st1[2:3]], axis=1),
                         jnp.zeros((1, 3 * D_MODEL), F32)], axis=0)])
    dm_g, loss_g = _all_gather([dm_me, loss_me], "gather_dmod")
    loss = jnp.sum(loss_g[:, 0, 0])
    dm_all = jnp.concatenate([jnp.transpose(dm_g[:, :, 0], (1, 0, 2)), jnp.transpose(dm_g[:, :, 1], (1, 0, 2))],
                             axis=1)
    dm_my = lax.dynamic_slice(dm_all, (0, 0, dev * n_mod), (2, 16, n_mod))
    g_w_mod, g_b_mod, gcc_part = _mod_bwd(cond, dm_all, dm_my, w_mod, "mod_bwd")
    g_b_mod = g_b_mod.reshape(b_mod.shape)

    gwa = jnp.stack([gwa_f + gwa_cf, gwa_b + gwa_cb])
    gwx = jnp.stack([gwx_f + gwx_cf, gwx_b + gwx_cb])
    gv = jnp.stack([gv_f + gv_cf, gv_b + gv_cb])
    cst = cst0 + cstc
    misc, m_sh, m_rep = pack(
        [_blocks_by_device(cst[0:4], 1), _blocks_by_device(gv[:, 0], 1), _blocks_by_device(gv[:, 1], 1),
         _blocks_by_device(gv[:, 2], 1), _blocks_by_device(gps[0], 0)],
        [gwa.reshape(-1), gwx.reshape(-1), jnp.stack([stl0[0], st1[0]]).reshape(-1),
         jnp.stack([stl0[1], st1[1]]).reshape(-1), cst[4], gcc_part.reshape(-1)])
    bufs = [gwi0, misc]
    recvs = _sibling_exchange(bufs, "reduce_sibling")
    pairs = [_pair_sum(b, r, core, "reduce_pair_" + n) for b, r, n in zip(bufs, recvs, ["w_in0", "misc"])]
    p_wi0, p_misc = _chip_exchange(pairs, "reduce_chips")
    (g_conv_w, g_lru_ba, g_lru_bx, g_lru_lam, g_pool_scale), rep_mine = unpack(
        _sum4(p_misc, "reduce_sum_misc"), m_sh, m_rep,
        [conv_w.shape, lru_ba.shape, lru_bx.shape, lru_lam.shape, pool_scale.shape])
    rep_all, = _all_gather([rep_mine], "gather_replicated")
    g_lru_wa, g_lru_wx, g_ln_g, g_ln_b, g_conv_b, g_c_ctx = spread(
        rep_all, m_rep, [lru_wa.shape, lru_wx.shape, ln_g.shape, ln_b.shape, conv_b.shape, c_ctx.shape])

    names = ["c_ctx", "w_mod", "b_mod", "w_in", "w_out", "ln_g", "ln_b", "conv_w", "conv_b", "lru_wa", "lru_ba",
             "lru_wx", "lru_bx", "lru_lam", "pool_w", "pool_scale"]
    weights = dict(c_ctx=c_ctx, w_mod=w_mod, b_mod=b_mod, w_in=w_in, w_out=w_out, ln_g=ln_g, ln_b=ln_b,
                   conv_w=conv_w, conv_b=conv_b, lru_wa=lru_wa, lru_ba=lru_ba, lru_wx=lru_wx, lru_bx=lru_bx,
                   lru_lam=lru_lam, pool_w=pool_w, pool_scale=pool_scale)
    mom_m = dict(c_ctx=m_c_ctx, w_mod=m_w_mod, b_mod=m_b_mod, w_in=m_w_in, w_out=m_w_out, ln_g=m_ln_g, ln_b=m_ln_b,
                 conv_w=m_conv_w, conv_b=m_conv_b, lru_wa=m_lru_wa, lru_ba=m_lru_ba, lru_wx=m_lru_wx,
                 lru_bx=m_lru_bx, lru_lam=m_lru_lam, pool_w=m_pool_w, pool_scale=m_pool_scale)
    mom_v = dict(c_ctx=v_c_ctx, w_mod=v_w_mod, b_mod=v_b_mod, w_in=v_w_in, w_out=v_w_out, ln_g=v_ln_g, ln_b=v_ln_b,
                 conv_w=v_conv_w, conv_b=v_conv_b, lru_wa=v_lru_wa, lru_ba=v_lru_ba, lru_wx=v_lru_wx,
                 lru_bx=v_lru_bx, lru_lam=v_lru_lam, pool_w=v_pool_w, pool_scale=v_pool_scale)
    grads = dict(c_ctx=g_c_ctx, w_mod=g_w_mod, b_mod=g_b_mod, ln_g=g_ln_g, ln_b=g_ln_b,
                 conv_w=g_conv_w, conv_b=g_conv_b, lru_wa=g_lru_wa, lru_ba=g_lru_ba, lru_wx=g_lru_wx,
                 lru_bx=g_lru_bx, lru_lam=g_lru_lam)
    grads["pool_scale"] = g_pool_scale
    delta, new_m, new_v = {}, {}, {}

    def update_parts(n, parts, view):
        res = _adamw_parts(weights[n].reshape(view), parts, mom_m[n].reshape(view), mom_v[n].reshape(view),
                           "adamw_" + n)
        grads[n], delta[n], new_m[n], new_v[n] = [r.reshape(weights[n].shape) for r in res]

    update_parts("w_in", [p_wi0, p_wi1], w_in.shape)
    update_parts("w_out", [p_wo0, p_wo1], w_out.shape)
    update_parts("pool_w", [p_pw], (1,) + p_pw.shape[1:])
    for n in ("w_mod", "lru_wa", "lru_wx"):
        shape = weights[n].shape
        view = (int(np.prod(shape[:-1])), shape[-1])
        res = _adamw(weights[n].reshape(view), grads[n].reshape(view), mom_m[n].reshape(view),
                     mom_v[n].reshape(view), "adamw_" + n)
        delta[n], new_m[n], new_v[n] = [r.reshape(shape) for r in res]

    small = [n for n in names if n not in delta]
    shapes = [weights[n].shape for n in small]
    flat = lambda d: _to_rows([d[n] for n in small], FLAT_ROWS)
    res = _adamw(flat(weights), flat(grads), flat(mom_m), flat(mom_v), "adamw_small")
    for d, r in zip((delta, new_m, new_v), res):
        d.update(zip(small, _split_rows(r, shapes)))

    return (loss, gx[None], *[grads[n] for n in names], *[delta[n] for n in names],
            *[new_m[n] for n in names], *[new_v[n] for n in names])
```

```python
import functools

import numpy as np
import jax
import jax.numpy as jnp
from jax import lax
from jax.experimental import pallas as pl
from jax.experimental.pallas import tpu as pltpu

F32 = jnp.float32
BF16 = jnp.bfloat16
MXU_DTYPE = BF16

D_MODEL = 1024
D_INNER = 2048
LRU_BLOCK = 128
GRID_W = 64
POOL_WINDOWS = (2, 4, 8, 16)
POOL_GROUP = 512
ALPHA = float(4 ** 0.25)
LN_EPS = 1e-5
LRU_C = 8.0
N_DEV = 8
N_WBLK = 8
WBLK = 512

ADAM_LR = 0.001
ADAM_B1 = 0.9
ADAM_B2 = 0.999
ADAM_EPS = 1e-08
ADAM_WD = 0.01
ADAM_STEP = 10

LANES = 128
SUBLANES = 8
V7X_VMEM_BYTES = 64 * 1024 * 1024
VMEM_LIMIT = V7X_VMEM_BYTES - 8 * 1024 * 1024
MESH = pl.DeviceIdType.MESH
ANY = pl.BlockSpec(memory_space=pl.ANY)

TM_MM = 512
TM_BWD = 256
TM_LRU = 1024
TM_LRU_FWD = 1024
CB_LRU = 512
N_SEG = 8
SCAN_UNROLL = 4
SCAN_ROW_T = 17
SCAN_ROW_J = 2
SQRT_FLOOR = 1e-30
FLAT_ROWS = 16
ELEMENTWISE_TILE_BYTES = 1 << 20
POOL_TOK = 256
WIRE_DTYPE = BF16
ACT_DTYPE = BF16
H_HALO = 16


def _cparams(**kw):
    return pltpu.CompilerParams(vmem_limit_bytes=VMEM_LIMIT, **kw)


def _my_pos():
    return lax.axis_index("x"), lax.axis_index("y"), lax.axis_index("c")


def _dot(a, b):
    return jnp.dot(a.astype(MXU_DTYPE), b.astype(MXU_DTYPE), preferred_element_type=F32)


def _dot_tn(a, b):
    return lax.dot_general(a.astype(MXU_DTYPE), b.astype(MXU_DTYPE), (((0,), (0,)), ((), ())),
                           preferred_element_type=F32)


def _dot_nt(a, b):
    return lax.dot_general(a.astype(MXU_DTYPE), b.astype(MXU_DTYPE), (((1,), (1,)), ((), ())),
                           preferred_element_type=F32)


def _sigmoid(z):
    return 0.5 * jnp.tanh(0.5 * z) + 0.5


def _log_sigmoid(x):
    y = jnp.exp(-jnp.abs(x))
    u = 1.0 + y
    l1p = jnp.where(u == 1.0, y, jnp.log(u) * (y / jnp.where(u == 1.0, 1.0, u - 1.0)))
    return jnp.minimum(x, 0.0) - l1p


def _rowsum(v):
    return jnp.sum(v, axis=0, keepdims=True)


def _layer_norm_stats(z):
    mu = jnp.mean(z, axis=-1, keepdims=True)
    zc = z - mu
    var = jnp.mean(zc * zc, axis=-1, keepdims=True)
    rstd = lax.rsqrt(var + LN_EPS)
    return zc * rstd, rstd


def _layer_norm_bwd(dy, xhat, rstd, g):
    dxh = dy * g
    m1 = jnp.mean(dxh, axis=-1, keepdims=True)
    m2 = jnp.mean(dxh * xhat, axis=-1, keepdims=True)
    return rstd * (dxh - m1 - xhat * m2)


def _shifted(v, before8, after8, offsets):
    n = v.shape[0]
    ext = jnp.concatenate([before8, v, after8], axis=0)
    total = n + 2 * SUBLANES
    return [pltpu.roll(ext, (-k) % total, 0)[SUBLANES:SUBLANES + n] for k in offsets]


def _rows8(row):
    return jnp.broadcast_to(row, (SUBLANES, row.shape[1]))


def _shift_down(v, first_row):
    return _shifted(v, _rows8(first_row), _rows8(first_row), [-1])[0]


def _shift_up(v, last_row):
    return _shifted(v, _rows8(last_row), _rows8(last_row), [1])[0]


def _all_gather(blocks, name):
    n = len(blocks)

    def body(*refs):
        x_refs, out_refs = refs[:n], refs[n:2 * n]
        send_sems, recv_sems, local_sems = refs[2 * n:]
        x, y, c = _my_pos()
        me, sibling = (x, y, c), (x, y, 1 - c)
        chips = [(1 - x, y), (x, 1 - y), (1 - x, 1 - y)]

        def slot(a, px, py, pc):
            return out_refs[a].at[4 * px + 2 * py + pc]

        def copy(a, k, block, to, src=None):
            return pltpu.make_async_remote_copy(
                src_ref=slot(a, *block) if src is None else src, dst_ref=slot(a, *block),
                send_sem=send_sems.at[a, k], recv_sem=recv_sems.at[a, k], device_id=to, device_id_type=MESH)

        mine = [pltpu.make_async_copy(x_refs[a], slot(a, *me), local_sems.at[a]) for a in range(n)]
        for cp in mine:
            cp.start()
        first = []
        for a in range(n):
            first.append(copy(a, 0, me, sibling, src=x_refs[a]))
            first += [copy(a, 1 + j, me, (*chip, c), src=x_refs[a]) for j, chip in enumerate(chips)]
        for cp in first:
            cp.start()
        passed = []
        for j, chip in enumerate(chips):
            for a in range(n):
                copy(a, 1 + j, (*chip, c), me).wait_recv()
                fwd = copy(a, 4 + j, (*chip, c), sibling)
                fwd.start()
                passed.append(fwd)
        for a in range(n):
            copy(a, 0, sibling, me).wait_recv()
            for j, chip in enumerate(chips):
                copy(a, 4 + j, (*chip, 1 - c), me).wait_recv()
        for cp in first + passed:
            cp.wait_send()
        for cp in mine:
            cp.wait()

    outs = pl.pallas_call(
        body, name=name,
        out_shape=[jax.ShapeDtypeStruct((N_DEV,) + b.shape, b.dtype) for b in blocks],
        in_specs=[ANY] * n, out_specs=[ANY] * n,
        scratch_shapes=[pltpu.SemaphoreType.DMA((n, 7)), pltpu.SemaphoreType.DMA((n, 7)),
                        pltpu.SemaphoreType.DMA((n,))],
    )(*blocks)
    return list(outs)


def _sibling_exchange(bufs, name):
    n = len(bufs)

    def body(*refs):
        srcs, outs = refs[:n], refs[n:2 * n]
        send_sems, recv_sems = refs[2 * n:]
        x, y, c = _my_pos()
        copies = [pltpu.make_async_remote_copy(
            src_ref=srcs[a].at[2 * j + (1 - c)], dst_ref=outs[a].at[j], send_sem=send_sems.at[a, j],
            recv_sem=recv_sems.at[a, j], device_id=(x, y, 1 - c), device_id_type=MESH)
            for a in range(n) for j in range(4)]
        for cp in copies:
            cp.start()
        for cp in copies:
            cp.wait()

    outs = pl.pallas_call(
        body, name=name, out_shape=[jax.ShapeDtypeStruct((4,) + b.shape[1:], b.dtype) for b in bufs],
        in_specs=[ANY] * n, out_specs=[ANY] * n,
        scratch_shapes=[pltpu.SemaphoreType.DMA((n, 4)), pltpu.SemaphoreType.DMA((n, 4))],
    )(*bufs)
    return list(outs)


def _chip_exchange(parts, name):
    n = len(parts)

    def body(*refs):
        srcs, outs = refs[:n], refs[n:2 * n]
        send_sems, recv_sems, local_sems = refs[2 * n:]
        x, y, c = _my_pos()
        jme = 2 * x + y
        peers = [(1 - x, y), (x, 1 - y), (1 - x, 1 - y)]
        local = [pltpu.make_async_copy(srcs[a].at[jme], outs[a].at[jme], local_sems.at[a]) for a in range(n)]
        for cp in local:
            cp.start()

        def copy(a, k, px, py, dst_slot):
            return pltpu.make_async_remote_copy(
                src_ref=srcs[a].at[2 * px + py], dst_ref=outs[a].at[dst_slot], send_sem=send_sems.at[a, k],
                recv_sem=recv_sems.at[a, k], device_id=(px, py, c), device_id_type=MESH)

        sends = [copy(a, k, px, py, jme) for a in range(n) for k, (px, py) in enumerate(peers)]
        for cp in sends:
            cp.start()
        for a in range(n):
            for k, (px, py) in enumerate(peers):
                copy(a, k, px, py, 2 * px + py).wait_recv()
        for cp in sends:
            cp.wait_send()
        for cp in local:
            cp.wait()

    outs = pl.pallas_call(
        body, name=name, out_shape=[jax.ShapeDtypeStruct(p.shape, p.dtype) for p in parts],
        in_specs=[ANY] * n, out_specs=[ANY] * n,
        scratch_shapes=[pltpu.SemaphoreType.DMA((n, 3)), pltpu.SemaphoreType.DMA((n, 3)),
                        pltpu.SemaphoreType.DMA((n,))],
    )(*parts)
    return list(outs)


_SIDE_REMOTE = {"gather": 7, "sibling": 4, "chips": 3}
_FLIPS = [(0, 0, 1), (1, 0, 0), (0, 1, 0), (1, 1, 0), (1, 0, 1), (0, 1, 1), (1, 1, 1)]


def _side_plan(sides):
    inputs, out_shapes, scratch = [], [], []
    for kind, arrays in sides:
        n = len(arrays)
        for a in arrays:
            inputs.append(a)
            shape = {"gather": (N_DEV,) + a.shape, "sibling": (4,) + a.shape[1:], "chips": a.shape}[kind]
            out_shapes.append(jax.ShapeDtypeStruct(shape, a.dtype))
        scratch += [pltpu.SemaphoreType.DMA((n, _SIDE_REMOTE[kind])), pltpu.SemaphoreType.DMA((n, _SIDE_REMOTE[kind])),
                    pltpu.SemaphoreType.DMA((n,))]
    return inputs, out_shapes, scratch


def _side_copies(sides, in_refs, out_refs, sem_refs):
    x, y, c = _my_pos()
    starts, waits = [], []
    pos = 0
    for s, (kind, arrays) in enumerate(sides):
        send_sems, recv_sems, local_sems = sem_refs[3 * s:3 * s + 3]
        for a in range(len(arrays)):
            src, out = in_refs[pos], out_refs[pos]
            pos += 1

            def remote(k, src_ref, dst_ref, to):
                return pltpu.make_async_remote_copy(src_ref=src_ref, dst_ref=dst_ref, send_sem=send_sems.at[a, k],
                                                    recv_sem=recv_sems.at[a, k], device_id=to, device_id_type=MESH)

            def local(src_ref, dst_ref):
                cp = pltpu.make_async_copy(src_ref, dst_ref, local_sems.at[a])
                starts.append(cp.start)
                waits.append(cp.wait)

            if kind == "gather":
                me = 4 * x + 2 * y + c
                local(src, out.at[me])
                for k, (fx, fy, fc) in enumerate(_FLIPS):
                    px, py, pc = (1 - x if fx else x), (1 - y if fy else y), (1 - c if fc else c)
                    send = remote(k, src, out.at[me], (px, py, pc))
                    starts.append(send.start)
                    waits += [remote(k, src, out.at[4 * px + 2 * py + pc], (px, py, pc)).wait_recv, send.wait_send]
            elif kind == "sibling":
                for j in range(4):
                    cp = remote(j, src.at[2 * j + (1 - c)], out.at[j], (x, y, 1 - c))
                    starts.append(cp.start)
                    waits.append(cp.wait)
            else:
                jme = 2 * x + y
                local(src.at[jme], out.at[jme])
                for k, (px, py) in enumerate([(1 - x, y), (x, 1 - y), (1 - x, 1 - y)]):
                    send = remote(k, src.at[2 * px + py], out.at[jme], (px, py, c))
                    starts.append(send.start)
                    waits += [remote(k, src.at[2 * px + py], out.at[2 * px + py], (px, py, c)).wait_recv,
                              send.wait_send]
    return starts, waits


def _call_with_sides(body, sides, *, name, grid, in_specs, out_specs, out_shape, scratch_shapes, compiler_params, args):
    if not sides:
        res = pl.pallas_call(body, name=name, grid=grid, in_specs=in_specs, out_specs=out_specs, out_shape=out_shape,
                             scratch_shapes=scratch_shapes, compiler_params=compiler_params)(*args)
        return list(res), []
    s_in, s_out, s_scr = _side_plan(sides)
    n_in, n_out, n_scr, n_side = len(in_specs), len(out_specs), len(scratch_shapes), len(s_in)

    def wrapped(*refs):
        refs = list(refs)
        ins, side_in = refs[:n_in], refs[n_in:n_in + n_side]
        outs = refs[n_in + n_side:n_in + n_side + n_out]
        side_out = refs[n_in + n_side + n_out:n_in + 2 * n_side + n_out]
        rest = refs[n_in + 2 * n_side + n_out:]
        starts, waits = _side_copies(sides, side_in, side_out, rest[n_scr:])
        first = functools.reduce(jnp.logical_and, [pl.program_id(d) == 0 for d in range(len(grid))])
        last = functools.reduce(jnp.logical_and, [pl.program_id(d) == grid[d] - 1 for d in range(len(grid))])

        @pl.when(first)
        def _():
            for start in starts:
                start()

        body(*ins, *outs, *rest[:n_scr])

        @pl.when(last)
        def _():
            for wait in waits:
                wait()

    res = pl.pallas_call(
        wrapped, name=name, grid=grid, in_specs=list(in_specs) + [ANY] * n_side,
        out_specs=list(out_specs) + [ANY] * n_side, out_shape=list(out_shape) + s_out,
        scratch_shapes=list(scratch_shapes) + s_scr, compiler_params=compiler_params,
    )(*args, *s_in)
    return list(res[:n_out]), list(res[n_out:])


def _row_tile(r, l):
    t = min(r, max(16, ELEMENTWISE_TILE_BYTES // (4 * l) // 16 * 16))
    while r % t:
        t -= 16
    return t


def _pair_sum(buf, recv, core, name):
    _, r, l = buf.shape
    tr = _row_tile(r, l)

    def body(core_ref, a_ref, b_ref, o_ref):
        o_ref[...] = (a_ref[...] + b_ref[...]).astype(WIRE_DTYPE)

    return pl.pallas_call(
        body, name=name, out_shape=jax.ShapeDtypeStruct((4, r, l), WIRE_DTYPE),
        grid_spec=pltpu.PrefetchScalarGridSpec(
            num_scalar_prefetch=1, grid=(4, r // tr),
            in_specs=[pl.BlockSpec((None, tr, l), lambda j, i, cr: (2 * j + cr[0], i, 0)),
                      pl.BlockSpec((None, tr, l), lambda j, i, cr: (j, i, 0))],
            out_specs=pl.BlockSpec((None, tr, l), lambda j, i, cr: (j, i, 0))),
        compiler_params=_cparams(dimension_semantics=("arbitrary", "arbitrary")),
    )(core, buf, recv)


def _sum_parts(p_ref):
    return ((p_ref[0].astype(F32) + p_ref[1].astype(F32)) + (p_ref[2].astype(F32) + p_ref[3].astype(F32)))


def _sum4(parts, name):
    _, r, l = parts.shape
    tr = _row_tile(r, l)

    def body(p_ref, o_ref):
        o_ref[...] = _sum_parts(p_ref)

    return pl.pallas_call(
        body, name=name, out_shape=jax.ShapeDtypeStruct((r, l), F32), grid=(r // tr,),
        in_specs=[pl.BlockSpec((4, tr, l), lambda i: (0, i, 0))],
        out_specs=pl.BlockSpec((tr, l), lambda i: (i, 0)),
        compiler_params=_cparams(dimension_semantics=("arbitrary",)),
    )(parts)


def _adamw_update(w, gg, m, v):
    nm = ADAM_B1 * m + (1.0 - ADAM_B1) * gg
    nv = ADAM_B2 * v + (1.0 - ADAM_B2) * (gg * gg)
    m_hat = nm / (1.0 - ADAM_B1 ** ADAM_STEP)
    v_hat = nv / (1.0 - ADAM_B2 ** ADAM_STEP)
    return -ADAM_LR * (m_hat / (jnp.sqrt(v_hat) + ADAM_EPS) + ADAM_WD * w), nm, nv


def _adamw(w, g, m, v, name):
    r, l = w.shape
    tr = _row_tile(r, l)

    def body(w_ref, g_ref, m_ref, v_ref, d_ref, nm_ref, nv_ref):
        d_ref[...], nm_ref[...], nv_ref[...] = _adamw_update(w_ref[...], g_ref[...], m_ref[...], v_ref[...])

    spec = pl.BlockSpec((tr, l), lambda i: (i, 0))
    return pl.pallas_call(
        body, name=name, out_shape=[jax.ShapeDtypeStruct((r, l), F32)] * 3, grid=(r // tr,),
        in_specs=[spec] * 4, out_specs=[spec] * 3,
        compiler_params=_cparams(dimension_semantics=("arbitrary",)),
    )(w, g, m, v)


def _adamw_parts(w, parts, m, v, name):
    nl, r, l = w.shape
    tr = _row_tile(r, l)

    def body(*refs):
        w_ref, p_refs, (m_ref, v_ref, g_ref, d_ref, nm_ref, nv_ref) = refs[0], refs[1:1 + nl], refs[1 + nl:]
        layer = pl.program_id(0)
        gg = _sum_parts(p_refs[0])
        for q in range(1, nl):
            gg = jnp.where(layer == q, _sum_parts(p_refs[q]), gg)
        g_ref[...] = gg
        d_ref[...], nm_ref[...], nv_ref[...] = _adamw_update(w_ref[...], gg, m_ref[...], v_ref[...])

    spec = pl.BlockSpec((None, tr, l), lambda q, i: (q, i, 0))
    pspecs = [pl.BlockSpec((4, tr, l), lambda q, i, k=k: (0, jnp.where(q == k, i, 0), 0)) for k in range(nl)]
    return pl.pallas_call(
        body, name=name, out_shape=[jax.ShapeDtypeStruct((nl, r, l), F32)] * 4, grid=(nl, r // tr),
        in_specs=[spec] + pspecs + [spec, spec], out_specs=[spec] * 4,
        compiler_params=_cparams(dimension_semantics=("arbitrary", "arbitrary")),
    )(w, *parts, m, v)


def _to_rows(pieces, row_multiple):
    flat = jnp.concatenate([p.reshape(-1) for p in pieces])
    rows = -(-flat.shape[0] // LANES)
    rows = -(-rows // row_multiple) * row_multiple
    flat = jnp.pad(flat, (0, rows * LANES - flat.shape[0]))
    return flat.reshape(rows, LANES)


def _split_rows(rows, shapes):
    flat = rows.reshape(-1)
    out, off = [], 0
    for s in shapes:
        n = int(np.prod(s))
        out.append(flat[off:off + n].reshape(s))
        off += n
    return out


def _mod_fwd(cond, w_mod, b_my, name):
    nl, _, ncol = w_mod.shape

    def body(a_ref, w_ref, b_ref, o_ref):
        a = a_ref[...]
        s = a * _sigmoid(a)
        for i in range(nl):
            o_ref[i] = _dot(s, w_ref[i]) + b_ref[i]

    return pl.pallas_call(
        body, name=name, out_shape=jax.ShapeDtypeStruct((nl, 16, ncol), F32),
        compiler_params=_cparams(),
    )(cond, w_mod, b_my)


def _mod_bwd(cond, dm_all, dm_my, w_mod, name):
    nl, _, ncol = w_mod.shape

    def body(a_ref, dma_ref, dmm_ref, w_ref, gw_ref, gb_ref, gc_ref):
        a = a_ref[...]
        sg = _sigmoid(a)
        s = a * sg
        for i in range(nl):
            gw_ref[i] = _dot_tn(s, dmm_ref[i])
            gb_ref[i] = jnp.sum(dma_ref[i], axis=0, keepdims=True)
        back = _dot_nt(dmm_ref[0], w_ref[0])
        dsilu = sg * (1.0 + a * (1.0 - sg))
        gc_ref[...] = jnp.sum(back[8:16] * dsilu[8:16], axis=0, keepdims=True)

    return pl.pallas_call(
        body, name=name,
        out_shape=[jax.ShapeDtypeStruct((nl, D_MODEL, ncol), F32), jax.ShapeDtypeStruct((nl, 1, 3 * D_MODEL), F32),
                   jax.ShapeDtypeStruct((1, D_MODEL), F32)],
        compiler_params=_cparams(),
    )(cond, dm_all, dm_my, w_mod)


def _in_proj(xt, sc, sh, wg, name, sides=()):
    t = xt.shape[0]
    tm = min(TM_MM, t)

    def body(x_ref, sc_ref, sh_ref, w_ref, u_ref, g_ref):
        h = (x_ref[...] * (1.0 + sc_ref[...]) + sh_ref[...]).astype(MXU_DTYPE)
        for k in range(N_WBLK):
            o = jnp.dot(h, w_ref[k], preferred_element_type=F32)
            if k < N_WBLK // 2:
                u_ref[:, k * WBLK:(k + 1) * WBLK] = o
            else:
                kk = k - N_WBLK // 2
                g_ref[:, kk * WBLK:(kk + 1) * WBLK] = o.astype(ACT_DTYPE)

    row = pl.BlockSpec((1, D_MODEL), lambda i: (0, 0))
    return _call_with_sides(
        body, sides, name=name,
        out_shape=[jax.ShapeDtypeStruct((t, D_INNER), F32), jax.ShapeDtypeStruct((t, D_INNER), ACT_DTYPE)],
        grid=(t // tm,),
        in_specs=[pl.BlockSpec((tm, D_MODEL), lambda i: (i, 0)), row, row,
                  pl.BlockSpec((N_WBLK, D_MODEL, WBLK), lambda i: (0, 0, 0), pipeline_mode=pl.Buffered(1))],
        out_specs=[pl.BlockSpec((tm, D_INNER), lambda i: (i, 0))] * 2, scratch_shapes=[],
        compiler_params=_cparams(dimension_semantics=("arbitrary",)), args=[xt, sc, sh, wg])


def _halo_maps(nt, tm, n_blocks, pos, rows=SUBLANES):
    per = tm // rows
    prev = lambda cb, i: (jnp.maximum(pos(i) * per - 1, 0), cb)
    nxt = lambda cb, i: (jnp.minimum((pos(i) + 1) * per, n_blocks - 1), cb)
    return prev, nxt


def _conv_taps(u, prev8, next8, is_first, is_last):
    pz = jnp.where(is_first, 0.0, 1.0)
    nz = jnp.where(is_last, 0.0, 1.0)
    return _shifted(u, prev8 * pz, next8 * nz, [-2, -1, 1])


def _lru_gates(uv, wa_ref, wx_ref, ba, bx, cl, g):
    sl = slice(g * LANES, (g + 1) * LANES)
    uvg = uv[:, sl]
    r = _sigmoid(_dot(uvg, wa_ref[g]) + ba[:, sl])
    ii = _sigmoid(_dot(uvg, wx_ref[g]) + bx[:, sl])
    la = cl[:, sl] * r
    a = jnp.exp(la)
    q = jnp.tanh(-la) * (1.0 + a * a)
    rs = lax.rsqrt(jnp.maximum(q, SQRT_FLOOR))
    return uvg, r, ii, a, q * rs, rs


def _scan_rows(seg):
    return -(-(SCAN_ROW_T * (seg - 1) + SCAN_ROW_J * (N_SEG - 1) + 1) // SUBLANES) * SUBLANES


def _seg_chunk(j, c):
    return pl.ds(SCAN_ROW_T * SUBLANES * c + SCAN_ROW_J * j, SUBLANES, stride=SCAN_ROW_T)


def _seg_scatter(ref, g, seg, value):
    for j in range(N_SEG):
        for c in range(seg // SUBLANES):
            r0 = j * seg + SUBLANES * c
            ref[g, _seg_chunk(j, c), :] = value[r0:r0 + SUBLANES]


def _scan_tile(a_s, b_s, carry_ref, write_out, seg, reverse, chunks_per_write=1):
    n_g = a_s.shape[0]
    unroll = SCAN_UNROLL if seg % SCAN_UNROLL == 0 else 1

    n_trips = seg // unroll

    def steps(k, state):
        hs, cs = list(state[0]), list(state[1])
        base = ((n_trips - 1 - k) if reverse else k) * unroll
        for q in (range(unroll - 1, -1, -1) if reverse else range(unroll)):
            t = base + q
            rows = pl.ds(t * SCAN_ROW_T, N_SEG, stride=SCAN_ROW_J)
            for g in range(n_g):
                a = a_s[g, rows, :]
                b = b_s[g, rows, :]
                hs[g] = a * hs[g] + b
                cs[g] = a * cs[g]
                b_s[g, rows, :] = hs[g]
                a_s[g, rows, :] = cs[g]
        return tuple(hs), tuple(cs)

    zeros = tuple(jnp.zeros((N_SEG, LANES), F32) for _ in range(n_g))
    ones = tuple(jnp.ones((N_SEG, LANES), F32) for _ in range(n_g))
    h_fin, a_fin = lax.fori_loop(0, seg // unroll, steps, (zeros, ones))

    order = list(range(N_SEG - 1, -1, -1)) if reverse else list(range(N_SEG))
    for g in range(n_g):
        carry = carry_ref[:, g * LANES:(g + 1) * LANES]
        for j in order:
            for c0 in range(0, seg // SUBLANES, chunks_per_write):
                parts = [b_s[g, _seg_chunk(j, c), :] + a_s[g, _seg_chunk(j, c), :] * carry
                         for c in range(c0, c0 + chunks_per_write)]
                write_out(j, c0, g, parts[0] if chunks_per_write == 1 else jnp.concatenate(parts, axis=0))
            carry = a_fin[g][j:j + 1] * carry + h_fin[g][j:j + 1]
        carry_ref[:, g * LANES:(g + 1) * LANES] = carry


def _lru_specs(s, tm, cb, direction_pos, nt):
    n_rows8 = s // SUBLANES
    prev, nxt = _halo_maps(nt, tm, n_rows8, direction_pos)
    tile = pl.BlockSpec((tm, cb), lambda c, i: (direction_pos(i), c))
    return tile, pl.BlockSpec((SUBLANES, cb), prev), pl.BlockSpec((SUBLANES, cb), nxt)


def _lru_param_specs(cb, d):
    n_g = cb // LANES
    vec = pl.BlockSpec((1, cb), lambda c, i: (0, c))
    dvec = pl.BlockSpec((None, 1, cb), lambda c, i: (d, 0, c))
    wmat = pl.BlockSpec((None, n_g, LRU_BLOCK, LRU_BLOCK), lambda c, i: (d, c, 0, 0))
    return vec, dvec, wmat


def _lru_fwd(src, h0, p, d, name, conv, sides=()):
    s = src.shape[0]
    tm = min(TM_LRU_FWD, s)
    cb = CB_LRU
    n_g = cb // LANES
    nt = s // tm
    seg = tm // N_SEG
    pos = (lambda i: i) if d == 0 else (lambda i: nt - 1 - i)

    def body(*refs):
        refs = list(refs)
        u_ref = refs.pop(0)
        if conv:
            up_ref, un_ref, cw_ref, cbias_ref = [refs.pop(0) for _ in range(4)]
        wa_ref, wx_ref, ba_ref, bx_ref, lam_ref, h0_ref, h_ref, hc_ref = [refs.pop(0) for _ in range(8)]
        uv_ref = refs.pop(0) if conv else None
        a_s, b_s = refs
        i = pl.program_id(1)
        tp = pos(i)

        @pl.when(i == 0)
        def _():
            hc_ref[...] = h0_ref[...]

        if conv:
            u_t = u_ref[...]
            um2, um1, up1 = _conv_taps(u_t, up_ref[...], un_ref[...], tp == 0, tp == nt - 1)
            cw = cw_ref[...]
            uv_ref[...] = um2 * cw[0:1] + um1 * cw[1:2] + u_t * cw[2:3] + up1 * cw[3:4] + cbias_ref[...]
        src_ref = uv_ref if conv else u_ref
        cl = LRU_C * _log_sigmoid(lam_ref[...])
        ba, bx = ba_ref[...], bx_ref[...]
        for g in range(n_g):
            uvg, r, ii, a, sq, _ = _lru_gates(src_ref, wa_ref, wx_ref, ba, bx, cl, g)
            b = sq * (ii * uvg)
            _seg_scatter(a_s, g, seg, a)
            _seg_scatter(b_s, g, seg, b)

        per_write = 2 if (seg // SUBLANES) % 2 == 0 else 1

        def write_out(j, c, g, h):
            h_ref[pl.ds(j * seg + SUBLANES * c, SUBLANES * per_write), pl.ds(g * LANES, LANES)] = h.astype(ACT_DTYPE)

        _scan_tile(a_s, b_s, hc_ref, write_out, seg, reverse=(d == 1), chunks_per_write=per_write)

    tile, prev, nxt = _lru_specs(s, tm, cb, pos, nt)
    vec, dvec, wmat = _lru_param_specs(cb, d)
    wide = jax.ShapeDtypeStruct((s, D_INNER), F32)
    conv_specs = [prev, nxt, pl.BlockSpec((4, cb), lambda c, i: (0, c)), vec] if conv else []
    conv_args = [src, src, p["conv_w"], p["conv_b"]] if conv else []
    return _call_with_sides(
        body, sides, name=name,
        out_shape=[jax.ShapeDtypeStruct((s, D_INNER), ACT_DTYPE), jax.ShapeDtypeStruct((1, D_INNER), F32)]
        + ([wide] if conv else []),
        grid=(D_INNER // cb, nt),
        in_specs=[tile] + conv_specs + [wmat, wmat, dvec, dvec, dvec, vec],
        out_specs=[tile, vec] + ([tile] if conv else []),
        scratch_shapes=[pltpu.VMEM((n_g, _scan_rows(seg), LANES), F32)] * 2,
        compiler_params=_cparams(dimension_semantics=("arbitrary", "arbitrary")),
        args=[src, *conv_args, p["wa"], p["wx"], p["ba"], p["bx"], p["lam"], h0])


def _lru_bwd(uv, dh, h, h0, lam_in, p, d, name, sides=()):
    s = uv.shape[0]
    tm = min(TM_LRU, s)
    cb = CB_LRU
    n_g = cb // LANES
    nt = s // tm
    seg = tm // N_SEG
    pos = (lambda i: nt - 1 - i) if d == 0 else (lambda i: i)

    def body(uv_ref, dh_ref, h_ref, hh_ref, wa_ref, wx_ref, ba_ref, bx_ref,
             lam_ref, h0_ref, lin_ref, duv_ref, gwa_ref, gwx_ref, gv_ref, lc_ref, a_s, b_s, lp_s,
             r_s, i_s, q_s, rq_s, a_keep):
        i = pl.program_id(1)
        tp = pos(i)

        @pl.when(i == 0)
        def _():
            lc_ref[...] = lin_ref[...]
            gwa_ref[...] = jnp.zeros_like(gwa_ref)
            gwx_ref[...] = jnp.zeros_like(gwx_ref)
            gv_ref[...] = jnp.zeros_like(gv_ref)

        uv = uv_ref[...]
        lam = lam_ref[...]
        cl = LRU_C * _log_sigmoid(lam)
        ba, bx = ba_ref[...], bx_ref[...]
        dh_t = dh_ref[...].astype(F32)
        carry_in = lc_ref[...]
        for g in range(n_g):
            sl = slice(g * LANES, (g + 1) * LANES)
            _, r, ii, a, sq, rs = _lru_gates(uv, wa_ref, wx_ref, ba, bx, cl, g)
            r_s[:, sl], i_s[:, sl], q_s[:, sl], rq_s[:, sl], a_keep[:, sl] = r, ii, sq, rs, a
            b = a * dh_t[:, sl]
            _seg_scatter(a_s, g, seg, a)
            _seg_scatter(b_s, g, seg, b)

        def write_out(j, c, g, v):
            lp_s[pl.ds(j * seg + SUBLANES * c, SUBLANES), pl.ds(g * LANES, LANES)] = v

        _scan_tile(a_s, b_s, lc_ref, write_out, seg, reverse=(d == 0))

        h_t = h_ref[...].astype(F32)
        hh = hh_ref[...].astype(F32)
        if d == 0:
            edge = jnp.where(tp == 0, h0_ref[...], hh[H_HALO - 1:H_HALO])
            h_prev = _shift_down(h_t, edge)
            lam_t = dh_t + _shift_up(lp_s[...], carry_in)
        else:
            edge = jnp.where(tp == nt - 1, h0_ref[...], hh[0:1])
            h_prev = _shift_up(h_t, edge)
            lam_t = dh_t + _shift_down(lp_s[...], carry_in)

        dsig = LRU_C * _sigmoid(-lam)
        for g in range(n_g):
            sl = slice(g * LANES, (g + 1) * LANES)
            uvg, r, ii, a, sq = uv[:, sl], r_s[:, sl], i_s[:, sl], a_keep[:, sl], q_s[:, sl]
            lt = lam_t[:, sl]
            ls = lt * sq
            dla = (lt * a) * (h_prev[:, sl] - (ii * uvg) * (a * rq_s[:, sl]))
            dzr = (dla * cl[:, sl]) * r * (1.0 - r)
            dzi = (ls * uvg) * ii * (1.0 - ii)
            duv_ref[:, sl] = ls * ii + _dot_nt(dzr, wa_ref[g]) + _dot_nt(dzi, wx_ref[g])
            gwa_ref[g] += _dot_tn(uvg, dzr)
            gwx_ref[g] += _dot_tn(uvg, dzi)
            gv_ref[0:1, sl] += _rowsum(dzr)
            gv_ref[1:2, sl] += _rowsum(dzi)
            gv_ref[2:3, sl] += _rowsum(dla * r) * dsig[:, sl]

    tile, _, _ = _lru_specs(s, tm, cb, pos, nt)
    vec, dvec, wmat = _lru_param_specs(cb, d)
    h_prev_map, h_next_map = _halo_maps(nt, tm, s // H_HALO, pos, rows=H_HALO)
    hh_spec = pl.BlockSpec((H_HALO, cb), h_prev_map if d == 0 else h_next_map)
    gw_spec = pl.BlockSpec((n_g, LRU_BLOCK, LRU_BLOCK), lambda c, i: (c, 0, 0))
    n_blk = D_INNER // LRU_BLOCK
    return _call_with_sides(
        body, sides, name=name,
        out_shape=[jax.ShapeDtypeStruct((s, D_INNER), F32),
                   jax.ShapeDtypeStruct((n_blk, LRU_BLOCK, LRU_BLOCK), F32),
                   jax.ShapeDtypeStruct((n_blk, LRU_BLOCK, LRU_BLOCK), F32),
                   jax.ShapeDtypeStruct((SUBLANES, D_INNER), F32),
                   jax.ShapeDtypeStruct((1, D_INNER), F32)],
        grid=(D_INNER // cb, nt),
        in_specs=[tile, tile, tile, hh_spec, wmat, wmat, dvec, dvec, dvec, vec, vec],
        out_specs=[tile, gw_spec, gw_spec, pl.BlockSpec((SUBLANES, cb), lambda c, i: (0, c)), vec],
        scratch_shapes=[pltpu.VMEM((n_g, _scan_rows(seg), LANES), F32)] * 2 + [pltpu.VMEM((tm, cb), F32)] * 6,
        compiler_params=_cparams(dimension_semantics=("arbitrary", "arbitrary")),
        args=[uv, dh, h, h, p["wa"], p["wx"], p["ba"], p["bx"], p["lam"], h0, lam_in])


def _out0(hf, hb, g, xt, gt, wo, lg, lb, name):
    t = xt.shape[0]
    tm = min(TM_MM, t)

    def body(hf_ref, hb_ref, g_ref, x_ref, gt_ref, w_ref, lg_ref, lb_ref, x1_ref, br_ref):
        br = None
        for k in range(D_INNER // WBLK):
            sl = slice(k * WBLK, (k + 1) * WBLK)
            gg = g_ref[:, sl].astype(F32)
            p = (hf_ref[:, sl].astype(F32) + hb_ref[:, sl].astype(F32)) * (gg * _sigmoid(gg))
            part = _dot(p, w_ref[sl, :])
            br = part if br is None else br + part
        z = ALPHA * x_ref[...] + gt_ref[...] * br
        xhat, _ = _layer_norm_stats(z)
        x1_ref[...] = xhat * lg_ref[...] + lb_ref[...]
        br_ref[...] = br.astype(ACT_DTYPE)

    wide = pl.BlockSpec((tm, D_INNER), lambda i: (i, 0))
    nar = pl.BlockSpec((tm, D_MODEL), lambda i: (i, 0))
    row = pl.BlockSpec((1, D_MODEL), lambda i: (0, 0))
    return pl.pallas_call(
        body, name=name,
        out_shape=[jax.ShapeDtypeStruct((t, D_MODEL), F32), jax.ShapeDtypeStruct((t, D_MODEL), ACT_DTYPE)],
        grid=(t // tm,),
        in_specs=[wide, wide, wide, nar, row,
                  pl.BlockSpec((D_INNER, D_MODEL), lambda i: (0, 0), pipeline_mode=pl.Buffered(1)), row, row],
        out_specs=[nar, nar],
        compiler_params=_cparams(dimension_semantics=("arbitrary",)),
    )(hf, hb, g, xt, gt, wo, lg, lb)


def _unrolled_loop(n, fn, unroll=4):
    while n % unroll:
        unroll //= 2

    def trip(k, carry):
        for q in range(unroll):
            fn(k * unroll + q)
        return carry
    lax.fori_loop(0, n // unroll, trip, 0)


def _window(n, w):
    t = np.arange(n)
    return np.clip(t - w // 2, 0, n), np.clip(t + w // 2, 0, n)


def _pool_tables(n_rows, transpose):
    boxes, inv_c, inv_r = [], [], []
    for w in POOL_WINDOWS:
        lo, hi = _window(GRID_W, w)
        m = np.zeros((GRID_W, GRID_W), np.float32)
        for r in range(GRID_W):
            m[r, lo[r]:hi[r]] = 1.0
        m = np.kron(np.eye(POOL_TOK // GRID_W, dtype=np.float32), m)
        boxes.append(m.T if transpose else m)
        inv_c.append(np.broadcast_to((1.0 / (hi - lo).astype(np.float32))[:, None], (GRID_W, LANES)))
        lo_r, hi_r = _window(n_rows, w)
        inv_r.append(1.0 / (hi_r - lo_r).astype(np.float32))
    return (jnp.asarray(np.stack(boxes), MXU_DTYPE), jnp.asarray(np.stack(inv_c), F32),
            jnp.asarray(np.stack(inv_r), F32))


def _pool_mix(xin, transpose, out_dtype, name):
    s = xin.shape[0]
    n_rows = s // GRID_W
    pad_t = SUBLANES * GRID_W
    rows_per_blk = POOL_TOK // GRID_W
    n_slab = D_INNER // LANES
    slabs_per_group = POOL_GROUP // LANES
    n_win = len(POOL_WINDOWS)
    boxes, inv_c, inv_r = _pool_tables(n_rows, transpose)

    def body(invr_ref, box_ref, invc_ref, x_ref, o_ref, pad_s):
        k = pl.program_id(0) // slabs_per_group
        pad_s[pl.ds(0, pad_t), :] = jnp.zeros((pad_t, LANES), F32)
        pad_s[pl.ds(pad_t + s, pad_t), :] = jnp.zeros((pad_t, LANES), F32)

        for kk, w in enumerate(POOL_WINDOWS):
            half = w // 2
            offsets = list(range(-(half - 1), half + 1)) if transpose else list(range(-half, half))

            @pl.when(k == kk)
            def _():
                inv_col = invc_ref[kk]

                def col_box(b):
                    st = pl.multiple_of(b * POOL_TOK, POOL_TOK)
                    xb = x_ref[pl.ds(st, POOL_TOK), :]
                    if transpose:
                        xb = xb * jnp.concatenate(
                            [inv_col * invr_ref[kk, b * rows_per_blk + q] for q in range(rows_per_blk)], axis=0)
                    hi = xb.astype(MXU_DTYPE)
                    lo = (xb - hi.astype(F32)).astype(MXU_DTYPE)
                    both = jnp.dot(box_ref[kk], jnp.concatenate([hi, lo], axis=1), preferred_element_type=F32)
                    pad_s[pl.ds(pad_t + st, POOL_TOK), :] = both[:, :LANES] + both[:, LANES:]
                _unrolled_loop(s // POOL_TOK, col_box)

                def row_box(r):
                    st = pl.multiple_of(r * GRID_W, GRID_W)
                    acc = pad_s[pl.ds(pad_t + st + offsets[0] * GRID_W, GRID_W), :]
                    for o in offsets[1:]:
                        acc = acc + pad_s[pl.ds(pad_t + st + o * GRID_W, GRID_W), :]
                    if not transpose:
                        acc = acc * (inv_col * invr_ref[kk, r])
                    o_ref[pl.ds(st, GRID_W), :] = (acc - x_ref[pl.ds(st, GRID_W), :]).astype(out_dtype)
                _unrolled_loop(n_rows, row_box)

    slab = pl.BlockSpec((s, LANES), lambda i: (0, i))
    return pl.pallas_call(
        body, name=name, out_shape=jax.ShapeDtypeStruct((s, D_INNER), out_dtype), grid=(n_slab,),
        in_specs=[pl.BlockSpec(memory_space=pltpu.SMEM),
                  pl.BlockSpec((n_win, POOL_TOK, POOL_TOK), lambda i: (0, 0, 0)),
                  pl.BlockSpec((n_win, GRID_W, LANES), lambda i: (0, 0, 0)), slab],
        out_specs=slab,
        scratch_shapes=[pltpu.VMEM((s + 2 * pad_t, LANES), F32)],
        compiler_params=_cparams(dimension_semantics=("arbitrary",)),
    )(inv_r, boxes, inv_c, xin)


def _out1(dmix, pw, ps, g, x1, gt, wo, lg, lb, tgt, name):
    t = x1.shape[0]
    tm = min(TM_MM, t)
    n_grp = len(POOL_WINDOWS)

    def body(d_ref, pw_ref, ps_ref, g_ref, x1_ref, gt_ref, w_ref, lg_ref, lb_ref, tgt_ref, dz_ref, st_ref):
        @pl.when(pl.program_id(0) == 0)
        def _():
            st_ref[...] = jnp.zeros_like(st_ref)

        br = jnp.zeros((tm, D_MODEL), F32)
        for k in range(n_grp):
            sl = slice(k * POOL_GROUP, (k + 1) * POOL_GROUP)
            y = jnp.dot(d_ref[:, sl], pw_ref[k], preferred_element_type=F32) * ps_ref[:, sl]
            gg = g_ref[:, sl].astype(F32)
            br = br + _dot(y * (gg * _sigmoid(gg)), w_ref[sl, :])
        z = ALPHA * x1_ref[...] + gt_ref[...] * br
        xhat, rstd = _layer_norm_stats(z)
        lg_v = lg_ref[...]
        err = xhat * lg_v + lb_ref[...] - tgt_ref[...]
        dy = err * (1.0 / D_MODEL)
        dz = _layer_norm_bwd(dy, xhat, rstd, lg_v)
        dz_ref[...] = dz
        st_ref[0:1, :] += _rowsum(dy * xhat)
        st_ref[1:2, :] += _rowsum(dy)
        st_ref[2:3, :] += _rowsum(dz * br)
        st_ref[3:4, :] += _rowsum(err * err)

    wide = pl.BlockSpec((tm, D_INNER), lambda i: (i, 0))
    nar = pl.BlockSpec((tm, D_MODEL), lambda i: (i, 0))
    row = pl.BlockSpec((1, D_MODEL), lambda i: (0, 0))
    return pl.pallas_call(
        body, name=name,
        out_shape=[jax.ShapeDtypeStruct((t, D_MODEL), F32), jax.ShapeDtypeStruct((SUBLANES, D_MODEL), F32)],
        grid=(t // tm,),
        in_specs=[wide, pl.BlockSpec((n_grp, POOL_GROUP, POOL_GROUP), lambda i: (0, 0, 0)),
                  pl.BlockSpec((1, D_INNER), lambda i: (0, 0)), wide, nar, row,
                  pl.BlockSpec((D_INNER, D_MODEL), lambda i: (0, 0), pipeline_mode=pl.Buffered(1)), row, row, nar],
        out_specs=[nar, pl.BlockSpec((SUBLANES, D_MODEL), lambda i: (0, 0))],
        compiler_params=_cparams(dimension_semantics=("arbitrary",)),
    )(dmix, pw, ps, g, x1, gt, wo, lg, lb, tgt)


def _flush(acc, out_hbm, sem):
    cp = pltpu.make_async_copy(acc, out_hbm, sem)
    cp.start()
    cp.wait()


def _bout1(dz, dmix, g, pw, ps, gt, wo, name):
    t = dz.shape[0]
    tm = min(TM_MM, t)
    nt = t // tm
    n_grp = len(POOL_WINDOWS)

    def body(dz_ref, d_ref, g_ref, pw_ref, ps_ref, gt_ref, w_ref, dd_ref, dg_ref, gwo_hbm, gpw_hbm, gps_ref,
             gwo_acc, gpw_acc, sems):
        i = pl.program_id(0)

        @pl.when(i == 0)
        def _():
            gwo_acc[...] = jnp.zeros_like(gwo_acc)
            gpw_acc[...] = jnp.zeros_like(gpw_acc)
            gps_ref[...] = jnp.zeros_like(gps_ref)

        db = (gt_ref[...] * dz_ref[...]).astype(MXU_DTYPE)
        for k in range(n_grp):
            sl = slice(k * POOL_GROUP, (k + 1) * POOL_GROUP)
            dk = d_ref[:, sl]
            po = jnp.dot(dk, pw_ref[k], preferred_element_type=F32)
            psk = ps_ref[:, sl]
            y = po * psk
            gg = g_ref[:, sl].astype(F32)
            sg = _sigmoid(gg)
            silu = gg * sg
            gwo_acc[sl, :] += _dot_tn(y * silu, db)
            dp = _dot_nt(db, w_ref[sl, :])
            dy = dp * silu
            dg_ref[:, sl] = (dp * y * (sg * (1.0 + gg * (1.0 - sg)))).astype(MXU_DTYPE)
            gps_ref[0:1, sl] += _rowsum(dy * po)
            dpo = (dy * psk).astype(MXU_DTYPE)
            gpw_acc[k] += _dot_tn(dk, dpo)
            dd_ref[:, sl] = _dot_nt(dpo, pw_ref[k])

        @pl.when(i == nt - 1)
        def _():
            _flush(gwo_acc, gwo_hbm, sems.at[0])
            _flush(gpw_acc, gpw_hbm, sems.at[1])

    wide = pl.BlockSpec((tm, D_INNER), lambda i: (i, 0))
    nar = pl.BlockSpec((tm, D_MODEL), lambda i: (i, 0))
    return pl.pallas_call(
        body, name=name,
        out_shape=[jax.ShapeDtypeStruct((t, D_INNER), F32), jax.ShapeDtypeStruct((t, D_INNER), MXU_DTYPE),
                   jax.ShapeDtypeStruct((D_INNER, D_MODEL), F32),
                   jax.ShapeDtypeStruct((n_grp, POOL_GROUP, POOL_GROUP), F32),
                   jax.ShapeDtypeStruct((SUBLANES, D_INNER), F32)],
        grid=(nt,),
        in_specs=[nar, wide, wide,
                  pl.BlockSpec((n_grp, POOL_GROUP, POOL_GROUP), lambda i: (0, 0, 0), pipeline_mode=pl.Buffered(1)),
                  pl.BlockSpec((1, D_INNER), lambda i: (0, 0)), pl.BlockSpec((1, D_MODEL), lambda i: (0, 0)),
                  pl.BlockSpec((D_INNER, D_MODEL), lambda i: (0, 0), pipeline_mode=pl.Buffered(1))],
        out_specs=[wide, wide, ANY, ANY, pl.BlockSpec((SUBLANES, D_INNER), lambda i: (0, 0))],
        scratch_shapes=[pltpu.VMEM((D_INNER, D_MODEL), F32), pltpu.VMEM((n_grp, POOL_GROUP, POOL_GROUP), F32),
                        pltpu.SemaphoreType.DMA((2,))],
        compiler_params=_cparams(dimension_semantics=("arbitrary",)),
    )(dz, dmix, g, pw, ps, gt, wo)


def _bout0(dx1, xt, br0, lg, hf, hb, g, gt, wo, name, sides=()):
    t = dx1.shape[0]
    tm = min(TM_MM, t)
    nt = t // tm

    def body(dx_ref, x_ref, br_ref, lg_ref, hf_ref, hb_ref, g_ref, gt_ref, w_ref,
             dz_ref, dy_ref, dg_ref, gwo_hbm, st_ref, gwo_acc, sem):
        i = pl.program_id(0)

        @pl.when(i == 0)
        def _():
            gwo_acc[...] = jnp.zeros_like(gwo_acc)
            st_ref[...] = jnp.zeros_like(st_ref)

        dx = dx_ref[...]
        br = br_ref[...].astype(F32)
        gate = gt_ref[...]
        xhat, rstd = _layer_norm_stats(ALPHA * x_ref[...] + gate * br)
        dz = _layer_norm_bwd(dx, xhat, rstd, lg_ref[...])
        dz_ref[...] = dz
        st_ref[0:1, :] += _rowsum(dx * xhat)
        st_ref[1:2, :] += _rowsum(dx)
        st_ref[2:3, :] += _rowsum(dz * br)
        db = (gate * dz).astype(MXU_DTYPE)
        for k in range(D_INNER // WBLK):
            sl = slice(k * WBLK, (k + 1) * WBLK)
            y = hf_ref[:, sl].astype(F32) + hb_ref[:, sl].astype(F32)
            gg = g_ref[:, sl].astype(F32)
            sg = _sigmoid(gg)
            silu = gg * sg
            gwo_acc[sl, :] += _dot_tn(y * silu, db)
            dp = _dot_nt(db, w_ref[sl, :])
            dy_ref[:, sl] = (dp * silu).astype(ACT_DTYPE)
            dg_ref[:, sl] = (dp * y * (sg * (1.0 + gg * (1.0 - sg)))).astype(MXU_DTYPE)

        @pl.when(i == nt - 1)
        def _():
            _flush(gwo_acc, gwo_hbm, sem)

    wide = pl.BlockSpec((tm, D_INNER), lambda i: (i, 0))
    nar = pl.BlockSpec((tm, D_MODEL), lambda i: (i, 0))
    row = pl.BlockSpec((1, D_MODEL), lambda i: (0, 0))
    return _call_with_sides(
        body, sides, name=name,
        out_shape=[jax.ShapeDtypeStruct((t, D_MODEL), F32), jax.ShapeDtypeStruct((t, D_INNER), ACT_DTYPE),
                   jax.ShapeDtypeStruct((t, D_INNER), MXU_DTYPE), jax.ShapeDtypeStruct((D_INNER, D_MODEL), F32),
                   jax.ShapeDtypeStruct((SUBLANES, D_MODEL), F32)],
        grid=(nt,),
        in_specs=[nar, nar, nar, row, wide, wide, wide, row,
                  pl.BlockSpec((D_INNER, D_MODEL), lambda i: (0, 0), pipeline_mode=pl.Buffered(1))],
        out_specs=[nar, wide, wide, ANY, pl.BlockSpec((SUBLANES, D_MODEL), lambda i: (0, 0))],
        scratch_shapes=[pltpu.VMEM((D_INNER, D_MODEL), F32), pltpu.SemaphoreType.DMA(())],
        compiler_params=_cparams(dimension_semantics=("arbitrary",)),
        args=[dx1, xt, br0, lg, hf, hb, g, gt, wo])


def _conv_bwd(duvf, duvb, u, conv_w, name, sides=()):
    s = u.shape[0]
    tm = min(TM_LRU_FWD, s)
    cb = CB_LRU
    nt = s // tm

    def body(df_ref, dfp_ref, dfn_ref, db_ref, dbp_ref, dbn_ref, u_ref, cw_ref, du_ref, cst_ref):
        i = pl.program_id(1)

        @pl.when(i == 0)
        def _():
            cst_ref[...] = jnp.zeros_like(cst_ref)

        first, last = i == 0, i == nt - 1
        pz = jnp.where(first, 0.0, 1.0)
        nz = jnp.where(last, 0.0, 1.0)
        dout = df_ref[...] + db_ref[...]
        dm1, dp1, dp2 = _shifted(dout, (dfp_ref[...] + dbp_ref[...]) * pz, (dfn_ref[...] + dbn_ref[...]) * nz,
                                 [-1, 1, 2])
        cw = cw_ref[...]
        du_ref[...] = (dp2 * cw[0:1] + dp1 * cw[1:2] + dout * cw[2:3] + dm1 * cw[3:4]).astype(MXU_DTYPE)
        u_t = u_ref[...]
        cst_ref[0:1, :] += _rowsum(dp2 * u_t)
        cst_ref[1:2, :] += _rowsum(dp1 * u_t)
        cst_ref[2:3, :] += _rowsum(dout * u_t)
        cst_ref[3:4, :] += _rowsum(dm1 * u_t)
        cst_ref[4:5, :] += _rowsum(dout)

    tile, prev, nxt = _lru_specs(s, tm, cb, lambda i: i, nt)
    return _call_with_sides(
        body, sides, name=name,
        out_shape=[jax.ShapeDtypeStruct((s, D_INNER), MXU_DTYPE), jax.ShapeDtypeStruct((SUBLANES, D_INNER), F32)],
        grid=(D_INNER // cb, nt),
        in_specs=[tile, prev, nxt] * 2 + [tile, pl.BlockSpec((4, cb), lambda c, i: (0, c))],
        out_specs=[tile, pl.BlockSpec((SUBLANES, cb), lambda c, i: (0, c))], scratch_shapes=[],
        compiler_params=_cparams(dimension_semantics=("arbitrary", "arbitrary")),
        args=[duvf, duvf, duvf, duvb, duvb, duvb, u, conv_w])


def _bin(du, dg, xin, dzin, sc, sh, wg, name, gw_init=None, sides=()):
    t = xin.shape[0]
    tm = min(TM_MM, t)
    nt = t // tm
    has_g, has_dx, has_init = dg is not None, dzin is not None, gw_init is not None
    half = N_WBLK // 2
    n_blk = N_WBLK if has_g else half

    def body(*refs):
        refs = list(refs)
        du_ref = refs.pop(0)
        dg_ref = refs.pop(0) if has_g else None
        x_ref = refs.pop(0)
        dz_ref = refs.pop(0) if has_dx else None
        sc_ref, sh_ref, w_ref = refs.pop(0), refs.pop(0), refs.pop(0)
        init_hbm = refs.pop(0) if has_init else None
        dx_ref = refs.pop(0) if has_dx else None
        gw_hbm, st_ref, gw_acc, sem = refs
        i = pl.program_id(0)

        @pl.when(i == 0)
        def _():
            st_ref[...] = jnp.zeros_like(st_ref)
            first_zero = 0
            if has_init:
                _flush(init_hbm, gw_acc.at[pl.ds(0, half)], sem)
                first_zero = half
            for k in range(first_zero, n_blk):
                gw_acc[k] = jnp.zeros((D_MODEL, WBLK), F32)

        xv = x_ref[...]
        scale = 1.0 + sc_ref[...]
        h = (xv * scale + sh_ref[...]).astype(MXU_DTYPE)
        dh = None
        for k in range(n_blk):
            src = du_ref if k < half else dg_ref
            kk = k % half
            dk = src[:, kk * WBLK:(kk + 1) * WBLK]
            gw_acc[k] += _dot_tn(h, dk)
            contrib = _dot_nt(dk, w_ref[k])
            dh = contrib if dh is None else dh + contrib
        st_ref[0:1, :] += _rowsum(dh * xv)
        st_ref[1:2, :] += _rowsum(dh)
        if has_dx:
            dx_ref[...] = ALPHA * dz_ref[...] + dh * scale

        @pl.when(i == nt - 1)
        def _():
            _flush(gw_acc, gw_hbm, sem)

    wide = pl.BlockSpec((tm, D_INNER), lambda i: (i, 0))
    nar = pl.BlockSpec((tm, D_MODEL), lambda i: (i, 0))
    row = pl.BlockSpec((1, D_MODEL), lambda i: (0, 0))
    wspec = pl.BlockSpec((n_blk, D_MODEL, WBLK), lambda i: (0, 0, 0), pipeline_mode=pl.Buffered(1))
    in_specs = ([wide] + ([wide] if has_g else []) + [nar] + ([nar] if has_dx else []) + [row, row, wspec]
                + ([ANY] if has_init else []))
    args = ([du] + ([dg] if has_g else []) + [xin] + ([dzin] if has_dx else []) + [sc, sh, wg]
            + ([gw_init] if has_init else []))
    out_shape = ([jax.ShapeDtypeStruct((t, D_MODEL), F32)] if has_dx else []) + [
        jax.ShapeDtypeStruct((n_blk, D_MODEL, WBLK), F32), jax.ShapeDtypeStruct((SUBLANES, D_MODEL), F32)]
    out_specs = ([nar] if has_dx else []) + [ANY, pl.BlockSpec((SUBLANES, D_MODEL), lambda i: (0, 0))]
    return _call_with_sides(
        body, sides, name=name, out_shape=out_shape, grid=(nt,), in_specs=in_specs, out_specs=out_specs,
        scratch_shapes=[pltpu.VMEM((n_blk, D_MODEL, WBLK), F32), pltpu.SemaphoreType.DMA(())],
        compiler_params=_cparams(dimension_semantics=("arbitrary",)), args=args)


def _blocks_by_device(a, axis):
    shape = a.shape
    a = a.reshape(shape[:axis] + (N_DEV, shape[axis] // N_DEV) + shape[axis + 1:])
    return jnp.moveaxis(a, axis, 0)


def kernel(x, c, ctx, c_ctx, w_mod, b_mod, w_in, w_out, ln_g, ln_b, conv_w, conv_b, lru_wa, lru_ba, lru_wx, lru_bx, lru_lam, pool_w, pool_scale, loss_target, m_c_ctx, m_w_mod, m_b_mod, m_w_in, m_w_out, m_ln_g, m_ln_b, m_conv_w, m_conv_b, m_lru_wa, m_lru_ba, m_lru_wx, m_lru_bx, m_lru_lam, m_pool_w, m_pool_scale, v_c_ctx, v_w_mod, v_b_mod, v_w_in, v_w_out, v_ln_g, v_ln_b, v_conv_w, v_conv_b, v_lru_wa, v_lru_ba, v_lru_wx, v_lru_bx, v_lru_lam, v_pool_w, v_pool_scale):
    xi, yi, ci = _my_pos()
    dev = 4 * xi + 2 * yi + ci
    xt, ctxt, tgt = x[0], ctx[0], loss_target[0]
    n_mod = w_mod.shape[2]

    small_shapes = [(D_MODEL,), conv_w.shape[1:], lru_ba.shape[1:], lru_bx.shape[1:], lru_lam.shape[1:],
                    pool_scale.shape[1:]]
    small = _to_rows([c[0], conv_w[0], lru_ba[0], lru_bx[0], lru_lam[0], pool_scale[0]], SUBLANES)
    small_all, wi0 = _all_gather([small, w_in[0].astype(MXU_DTYPE)], "gather_first")
    pieces = [_split_rows(small_all[k], small_shapes) for k in range(N_DEV)]
    c_all = jnp.stack([p[0] for p in pieces])
    conv_w_f = jnp.concatenate([p[1] for p in pieces], axis=-1)
    lru_ba_f = jnp.concatenate([p[2] for p in pieces], axis=-1)[:, None, :]
    lru_bx_f = jnp.concatenate([p[3] for p in pieces], axis=-1)[:, None, :]
    lru_lam_f = jnp.concatenate([p[4] for p in pieces], axis=-1)[:, None, :]
    pool_scale_f = jnp.concatenate([p[5] for p in pieces], axis=-1)[None, :]

    cond = jnp.concatenate([c_all, jnp.broadcast_to(c_ctx[None, :], (N_DEV, D_MODEL))], axis=0)
    b_my = lax.dynamic_slice(b_mod, (0, dev * n_mod), (2, n_mod))[:, None, :]
    mod_part = _mod_fwd(cond, w_mod, b_my, "mod_fwd")
    mod_all, = _all_gather([mod_part], "gather_mod")
    mod = jnp.transpose(mod_all, (1, 2, 0, 3)).reshape(2, 16, 3 * D_MODEL)
    mod_me = lax.dynamic_slice(mod, (0, dev, 0), (2, 1, 3 * D_MODEL))
    sh = [mod_me[i, :, 0:D_MODEL] for i in range(2)]
    sc = [mod_me[i, :, D_MODEL:2 * D_MODEL] for i in range(2)]
    gt = [mod_me[i, :, 2 * D_MODEL:] for i in range(2)]
    shc, scc = mod[0, 8:9, 0:D_MODEL], mod[0, 8:9, D_MODEL:2 * D_MODEL]

    lg = [ln_g[i][None, :] for i in range(2)]
    lb = [ln_b[i][None, :] for i in range(2)]
    lru_p = dict(conv_w=conv_w_f, conv_b=conv_b, wa=lru_wa[0].astype(MXU_DTYPE), wx=lru_wx[0].astype(MXU_DTYPE),
                 ba=lru_ba_f, bx=lru_bx_f, lam=lru_lam_f)
    zero_state = jnp.zeros((1, D_INNER), F32)

    (u0, g0), (wo0,) = _in_proj(xt, sc[0], sh[0], wi0, "in_proj0", sides=[("gather", [w_out[0].astype(MXU_DTYPE)])])
    (uc, _), _ = _in_proj(ctxt, scc, shc, wi0, "in_proj0_ctx")
    (hcf, cf, uvc), _ = _lru_fwd(uc, zero_state, lru_p, 0, "lru_fwd_ctx_f", conv=True)
    (hcb, cbk), _ = _lru_fwd(uvc, zero_state, lru_p, 1, "lru_fwd_ctx_b", conv=False)
    (hf, _, uv0), (wi1,) = _lru_fwd(u0, cf, lru_p, 0, "lru_fwd_f", conv=True,
                                    sides=[("gather", [w_in[1].astype(MXU_DTYPE)])])
    (hb, _), (wo1, pool_w_g) = _lru_fwd(
        uv0, cbk, lru_p, 1, "lru_fwd_b", conv=False,
        sides=[("gather", [w_out[1].astype(MXU_DTYPE), pool_w[0].astype(MXU_DTYPE)])])
    w_in_l = [wi0, wi1]
    w_out_l = [wo0.reshape(D_INNER, D_MODEL), wo1.reshape(D_INNER, D_MODEL)]
    pool_w_f = jnp.transpose(pool_w_g, (1, 0, 2, 3)).reshape(len(POOL_WINDOWS), POOL_GROUP, POOL_GROUP)
    x1, br0 = _out0(hf, hb, g0, xt, gt[0], w_out_l[0], lg[0], lb[0], "out0")
    (u1, g1), _ = _in_proj(x1, sc[1], sh[1], w_in_l[1], "in_proj1")
    dmix = _pool_mix(u1, False, MXU_DTYPE, "pool_fwd")
    dz1, st1 = _out1(dmix, pool_w_f, pool_scale_f, g1, x1, gt[1], w_out_l[1], lg[1], lb[1], tgt, "out1")
    loss_me = jnp.full((1, LANES), (0.5 / D_MODEL) * jnp.sum(st1[3]), F32)

    core = jnp.reshape(ci, (1,)).astype(jnp.int32)
    wo_view = lambda a: a.reshape(N_DEV, D_INNER // N_DEV, D_MODEL)
    pw_view = lambda a: _blocks_by_device(a, 1).reshape(N_DEV, POOL_GROUP // N_DEV * len(POOL_WINDOWS), POOL_GROUP)
    dd, dg1, gwo1, gpw, gps = _bout1(dz1, dmix, g1, pool_w_f, pool_scale_f, gt[1], w_out_l[1], "bwd_out1")
    du1 = _pool_mix(dd, True, MXU_DTYPE, "pool_bwd")
    (dx1, gwi1, stb1), _ = _bin(du1, dg1, x1, dz1, sc[1], sh[1], w_in_l[1], "bwd_in1")
    bufs1 = [gwi1, wo_view(gwo1), pw_view(gpw)]
    (dz0, dy0, dg0, gwo0, stl0), recv1 = _bout0(dx1, xt, br0, lg[0], hf, hb, g0, gt[0], w_out_l[0], "bwd_out0",
                                                sides=[("sibling", bufs1)])
    pairs1 = [_pair_sum(b, r, core, "reduce_pair_" + n)
              for b, r, n in zip(bufs1, recv1, ["w_in1", "w_out1", "pool_w"])]
    (duvf, gwa_f, gwx_f, gv_f, dh0f), (p_wi1, p_wo1, p_pw, recv_wo0) = _lru_bwd(
        uv0, dy0, hf, cf, zero_state, lru_p, 0, "lru_bwd_f", sides=[("chips", pairs1), ("sibling", [wo_view(gwo0)])])
    pair_wo0 = _pair_sum(wo_view(gwo0), recv_wo0, core, "reduce_pair_w_out0")
    (duvb, gwa_b, gwx_b, gv_b, dh0b), (p_wo0,) = _lru_bwd(
        uv0, dy0, hb, cbk, zero_state, lru_p, 1, "lru_bwd_b", sides=[("chips", [pair_wo0])])
    zero_dh = jnp.zeros(uc.shape, ACT_DTYPE)
    (ducf, gwa_cf, gwx_cf, gv_cf, _), _ = _lru_bwd(uvc, zero_dh, hcf, zero_state, dh0f, lru_p, 0, "lru_bwd_ctx_f")
    (ducb, gwa_cb, gwx_cb, gv_cb, _), _ = _lru_bwd(uvc, zero_dh, hcb, zero_state, dh0b, lru_p, 1, "lru_bwd_ctx_b")

    def pack(sharded, replicated):
        sh_sizes = [int(np.prod(a.shape[1:])) for a in sharded]
        rep_sizes = [a.shape[0] // N_DEV for a in replicated]
        n_flat = sum(sh_sizes) + sum(rep_sizes)
        rows = -(-(-(-n_flat // LANES)) // FLAT_ROWS) * FLAT_ROWS
        buf = jnp.concatenate([a.reshape(N_DEV, -1) for a in sharded + replicated], axis=1)
        return jnp.pad(buf, ((0, 0), (0, rows * LANES - n_flat))).reshape(N_DEV, rows, LANES), sh_sizes, rep_sizes

    def unpack(reduced, sh_sizes, rep_sizes, sh_shapes):
        flat = reduced.reshape(-1)
        offs = np.cumsum([0] + sh_sizes)
        mine = [flat[offs[k]:offs[k + 1]].reshape(s) for k, s in enumerate(sh_shapes)]
        return mine, _to_rows([flat[offs[-1]:offs[-1] + sum(rep_sizes)]], SUBLANES)

    def spread(rep_all, rep_sizes, shapes):
        flat = rep_all.reshape(N_DEV, -1)
        offs = np.cumsum([0] + rep_sizes)
        return [flat[:, offs[k]:offs[k + 1]].reshape(s) for k, s in enumerate(shapes)]

    (du0, cst0), _ = _conv_bwd(duvf, duvb, u0, conv_w_f, "conv_bwd")
    (duc, cstc), _ = _conv_bwd(ducf, ducb, uc, conv_w_f, "conv_bwd_ctx")
    (gwic, stc), _ = _bin(duc, None, ctxt, None, scc, shc, w_in_l[0][:N_WBLK // 2], "bwd_in0_ctx")
    (gx, gwi0, stb0), _ = _bin(du0, dg0, xt, dz0, sc[0], sh[0], w_in_l[0], "bwd_in0", gw_init=gwic)

    zero_row = jnp.zeros((1, D_MODEL), F32)
    dm_me = jnp.stack([
        jnp.concatenate([jnp.concatenate([stb0[1:2], stb0[0:1], stl0[2:3]], axis=1),
                         jnp.concatenate([stc[1:2], stc[0:1], zero_row], axis=1)], axis=0),
        jnp.concatenate([jnp.concatenate([stb1[1:2], stb1[0:1], st1[2:3]], axis=1),
                         jnp.zeros((1, 3 * D_MODEL), F32)], axis=0)])
    dm_g, loss_g = _all_gather([dm_me, loss_me], "gather_dmod")
    loss = jnp.sum(loss_g[:, 0, 0])
    dm_all = jnp.concatenate([jnp.transpose(dm_g[:, :, 0], (1, 0, 2)), jnp.transpose(dm_g[:, :, 1], (1, 0, 2))],
                             axis=1)
    dm_my = lax.dynamic_slice(dm_all, (0, 0, dev * n_mod), (2, 16, n_mod))
    g_w_mod, g_b_mod, gcc_part = _mod_bwd(cond, dm_all, dm_my, w_mod, "mod_bwd")
    g_b_mod = g_b_mod.reshape(b_mod.shape)

    gwa = jnp.stack([gwa_f + gwa_cf, gwa_b + gwa_cb])
    gwx = jnp.stack([gwx_f + gwx_cf, gwx_b + gwx_cb])
    gv = jnp.stack([gv_f + gv_cf, gv_b + gv_cb])
    cst = cst0 + cstc
    misc, m_sh, m_rep = pack(
        [_blocks_by_device(cst[0:4], 1), _blocks_by_device(gv[:, 0], 1), _blocks_by_device(gv[:, 1], 1),
         _blocks_by_device(gv[:, 2], 1), _blocks_by_device(gps[0], 0)],
        [gwa.reshape(-1), gwx.reshape(-1), jnp.stack([stl0[0], st1[0]]).reshape(-1),
         jnp.stack([stl0[1], st1[1]]).reshape(-1), cst[4], gcc_part.reshape(-1)])
    bufs = [gwi0, misc]
    recvs = _sibling_exchange(bufs, "reduce_sibling")
    pairs = [_pair_sum(b, r, core, "reduce_pair_" + n) for b, r, n in zip(bufs, recvs, ["w_in0", "misc"])]
    p_wi0, p_misc = _chip_exchange(pairs, "reduce_chips")
    (g_conv_w, g_lru_ba, g_lru_bx, g_lru_lam, g_pool_scale), rep_mine = unpack(
        _sum4(p_misc, "reduce_sum_misc"), m_sh, m_rep,
        [conv_w.shape, lru_ba.shape, lru_bx.shape, lru_lam.shape, pool_scale.shape])
    rep_all, = _all_gather([rep_mine], "gather_replicated")
    g_lru_wa, g_lru_wx, g_ln_g, g_ln_b, g_conv_b, g_c_ctx = spread(
        rep_all, m_rep, [lru_wa.shape, lru_wx.shape, ln_g.shape, ln_b.shape, conv_b.shape, c_ctx.shape])

    names = ["c_ctx", "w_mod", "b_mod", "w_in", "w_out", "ln_g", "ln_b", "conv_w", "conv_b", "lru_wa", "lru_ba",
             "lru_wx", "lru_bx", "lru_lam", "pool_w", "pool_scale"]
    weights = dict(c_ctx=c_ctx, w_mod=w_mod, b_mod=b_mod, w_in=w_in, w_out=w_out, ln_g=ln_g, ln_b=ln_b,
                   conv_w=conv_w, conv_b=conv_b, lru_wa=lru_wa, lru_ba=lru_ba, lru_wx=lru_wx, lru_bx=lru_bx,
                   lru_lam=lru_lam, pool_w=pool_w, pool_scale=pool_scale)
    mom_m = dict(c_ctx=m_c_ctx, w_mod=m_w_mod, b_mod=m_b_mod, w_in=m_w_in, w_out=m_w_out, ln_g=m_ln_g, ln_b=m_ln_b,
                 conv_w=m_conv_w, conv_b=m_conv_b, lru_wa=m_lru_wa, lru_ba=m_lru_ba, lru_wx=m_lru_wx,
                 lru_bx=m_lru_bx, lru_lam=m_lru_lam, pool_w=m_pool_w, pool_scale=m_pool_scale)
    mom_v = dict(c_ctx=v_c_ctx, w_mod=v_w_mod, b_mod=v_b_mod, w_in=v_w_in, w_out=v_w_out, ln_g=v_ln_g, ln_b=v_ln_b,
                 conv_w=v_conv_w, conv_b=v_conv_b, lru_wa=v_lru_wa, lru_ba=v_lru_ba, lru_wx=v_lru_wx,
                 lru_bx=v_lru_bx, lru_lam=v_lru_lam, pool_w=v_pool_w, pool_scale=v_pool_scale)
    grads = dict(c_ctx=g_c_ctx, w_mod=g_w_mod, b_mod=g_b_mod, ln_g=g_ln_g, ln_b=g_ln_b,
                 conv_w=g_conv_w, conv_b=g_conv_b, lru_wa=g_lru_wa, lru_ba=g_lru_ba, lru_wx=g_lru_wx,
                 lru_bx=g_lru_bx, lru_lam=g_lru_lam)
    grads["pool_scale"] = g_pool_scale
    delta, new_m, new_v = {}, {}, {}

    def update_parts(n, parts, view):
        res = _adamw_parts(weights[n].reshape(view), parts, mom_m[n].reshape(view), mom_v[n].reshape(view),
                           "adamw_" + n)
        grads[n], delta[n], new_m[n], new_v[n] = [r.reshape(weights[n].shape) for r in res]

    update_parts("w_in", [p_wi0, p_wi1], w_in.shape)
    update_parts("w_out", [p_wo0, p_wo1], w_out.shape)
    update_parts("pool_w", [p_pw], (1,) + p_pw.shape[1:])
    for n in ("w_mod", "lru_wa", "lru_wx"):
        shape = weights[n].shape
        view = (int(np.prod(shape[:-1])), shape[-1])
        res = _adamw(weights[n].reshape(view), grads[n].reshape(view), mom_m[n].reshape(view),
                     mom_v[n].reshape(view), "adamw_" + n)
        delta[n], new_m[n], new_v[n] = [r.reshape(shape) for r in res]

    small = [n for n in names if n not in delta]
    shapes = [weights[n].shape for n in small]
    flat = lambda d: _to_rows([d[n] for n in small], FLAT_ROWS)
    res = _adamw(flat(weights), flat(grads), flat(mom_m), flat(mom_v), "adamw_small")
    for d, r in zip((delta, new_m, new_v), res):
        d.update(zip(small, _split_rows(r, shapes)))

    return (loss, gx[None], *[grads[n] for n in names], *[delta[n] for n in names],
            *[new_m[n] for n in names], *[new_v[n] for n in names])
```

```python
import functools

import numpy as np
import jax
import jax.numpy as jnp
from jax import lax
from jax.experimental import pallas as pl
from jax.experimental.pallas import tpu as pltpu

F32 = jnp.float32
BF16 = jnp.bfloat16
MXU_DTYPE = BF16

D_MODEL = 1024
D_INNER = 2048
LRU_BLOCK = 128
GRID_W = 64
POOL_WINDOWS = (2, 4, 8, 16)
POOL_GROUP = 512
ALPHA = float(4 ** 0.25)
LN_EPS = 1e-5
LRU_C = 8.0
N_DEV = 8
N_WBLK = 8
WBLK = 512

ADAM_LR = 0.001
ADAM_B1 = 0.9
ADAM_B2 = 0.999
ADAM_EPS = 1e-08
ADAM_WD = 0.01
ADAM_STEP = 10

LANES = 128
SUBLANES = 8
V7X_VMEM_BYTES = 64 * 1024 * 1024
VMEM_LIMIT = V7X_VMEM_BYTES - 8 * 1024 * 1024
MESH = pl.DeviceIdType.MESH
ANY = pl.BlockSpec(memory_space=pl.ANY)

TM_MM = 512
TM_LRU = 1024
CB_LRU = 512
N_SEG = 8
SCAN_UNROLL = 4
SCAN_ROW_T = 17
SCAN_ROW_J = 2
SQRT_FLOOR = 1e-30
FLAT_ROWS = 16
ELEMENTWISE_TILE_BYTES = 1 << 20
POOL_TOK = 256
WIRE_DTYPE = BF16
ACT_DTYPE = BF16
H_HALO = 16


def _cparams(**kw):
    return pltpu.CompilerParams(vmem_limit_bytes=VMEM_LIMIT, **kw)


def _my_pos():
    return lax.axis_index("x"), lax.axis_index("y"), lax.axis_index("c")


def _dot(a, b):
    return jnp.dot(a.astype(MXU_DTYPE), b.astype(MXU_DTYPE), preferred_element_type=F32)


def _dot_tn(a, b):
    return lax.dot_general(a.astype(MXU_DTYPE), b.astype(MXU_DTYPE), (((0,), (0,)), ((), ())),
                           preferred_element_type=F32)


def _dot_nt(a, b):
    return lax.dot_general(a.astype(MXU_DTYPE), b.astype(MXU_DTYPE), (((1,), (1,)), ((), ())),
                           preferred_element_type=F32)


def _sigmoid(z):
    return 0.5 * jnp.tanh(0.5 * z) + 0.5


def _log_sigmoid(x):
    y = jnp.exp(-jnp.abs(x))
    u = 1.0 + y
    l1p = jnp.where(u == 1.0, y, jnp.log(u) * (y / jnp.where(u == 1.0, 1.0, u - 1.0)))
    return jnp.minimum(x, 0.0) - l1p


def _rowsum(v):
    return jnp.sum(v, axis=0, keepdims=True)


def _layer_norm_stats(z):
    mu = jnp.mean(z, axis=-1, keepdims=True)
    zc = z - mu
    var = jnp.mean(zc * zc, axis=-1, keepdims=True)
    rstd = lax.rsqrt(var + LN_EPS)
    return zc * rstd, rstd


def _layer_norm_bwd(dy, xhat, rstd, g):
    dxh = dy * g
    m1 = jnp.mean(dxh, axis=-1, keepdims=True)
    m2 = jnp.mean(dxh * xhat, axis=-1, keepdims=True)
    return rstd * (dxh - m1 - xhat * m2)


def _shifted(v, before8, after8, offsets):
    n = v.shape[0]
    ext = jnp.concatenate([before8, v, after8], axis=0)
    total = n + 2 * SUBLANES
    return [pltpu.roll(ext, (-k) % total, 0)[SUBLANES:SUBLANES + n] for k in offsets]


def _rows8(row):
    return jnp.broadcast_to(row, (SUBLANES, row.shape[1]))


def _shift_down(v, first_row):
    return _shifted(v, _rows8(first_row), _rows8(first_row), [-1])[0]


def _shift_up(v, last_row):
    return _shifted(v, _rows8(last_row), _rows8(last_row), [1])[0]


def _all_gather(blocks, name):
    n = len(blocks)

    def body(*refs):
        x_refs, out_refs = refs[:n], refs[n:2 * n]
        send_sems, recv_sems, local_sems = refs[2 * n:]
        x, y, c = _my_pos()
        me, sibling = (x, y, c), (x, y, 1 - c)
        chips = [(1 - x, y), (x, 1 - y), (1 - x, 1 - y)]

        def slot(a, px, py, pc):
            return out_refs[a].at[4 * px + 2 * py + pc]

        def copy(a, k, block, to, src=None):
            return pltpu.make_async_remote_copy(
                src_ref=slot(a, *block) if src is None else src, dst_ref=slot(a, *block),
                send_sem=send_sems.at[a, k], recv_sem=recv_sems.at[a, k], device_id=to, device_id_type=MESH)

        mine = [pltpu.make_async_copy(x_refs[a], slot(a, *me), local_sems.at[a]) for a in range(n)]
        for cp in mine:
            cp.start()
        first = []
        for a in range(n):
            first.append(copy(a, 0, me, sibling, src=x_refs[a]))
            first += [copy(a, 1 + j, me, (*chip, c), src=x_refs[a]) for j, chip in enumerate(chips)]
        for cp in first:
            cp.start()
        passed = []
        for j, chip in enumerate(chips):
            for a in range(n):
                copy(a, 1 + j, (*chip, c), me).wait_recv()
                fwd = copy(a, 4 + j, (*chip, c), sibling)
                fwd.start()
                passed.append(fwd)
        for a in range(n):
            copy(a, 0, sibling, me).wait_recv()
            for j, chip in enumerate(chips):
                copy(a, 4 + j, (*chip, 1 - c), me).wait_recv()
        for cp in first + passed:
            cp.wait_send()
        for cp in mine:
            cp.wait()

    outs = pl.pallas_call(
        body, name=name,
        out_shape=[jax.ShapeDtypeStruct((N_DEV,) + b.shape, b.dtype) for b in blocks],
        in_specs=[ANY] * n, out_specs=[ANY] * n,
        scratch_shapes=[pltpu.SemaphoreType.DMA((n, 7)), pltpu.SemaphoreType.DMA((n, 7)),
                        pltpu.SemaphoreType.DMA((n,))],
    )(*blocks)
    return list(outs)


def _sibling_exchange(bufs, name):
    n = len(bufs)

    def body(*refs):
        srcs, outs = refs[:n], refs[n:2 * n]
        send_sems, recv_sems = refs[2 * n:]
        x, y, c = _my_pos()
        copies = [pltpu.make_async_remote_copy(
            src_ref=srcs[a].at[2 * j + (1 - c)], dst_ref=outs[a].at[j], send_sem=send_sems.at[a, j],
            recv_sem=recv_sems.at[a, j], device_id=(x, y, 1 - c), device_id_type=MESH)
            for a in range(n) for j in range(4)]
        for cp in copies:
            cp.start()
        for cp in copies:
            cp.wait()

    outs = pl.pallas_call(
        body, name=name, out_shape=[jax.ShapeDtypeStruct((4,) + b.shape[1:], b.dtype) for b in bufs],
        in_specs=[ANY] * n, out_specs=[ANY] * n,
        scratch_shapes=[pltpu.SemaphoreType.DMA((n, 4)), pltpu.SemaphoreType.DMA((n, 4))],
    )(*bufs)
    return list(outs)


def _chip_exchange(parts, name):
    n = len(parts)

    def body(*refs):
        srcs, outs = refs[:n], refs[n:2 * n]
        send_sems, recv_sems, local_sems = refs[2 * n:]
        x, y, c = _my_pos()
        jme = 2 * x + y
        peers = [(1 - x, y), (x, 1 - y), (1 - x, 1 - y)]
        local = [pltpu.make_async_copy(srcs[a].at[jme], outs[a].at[jme], local_sems.at[a]) for a in range(n)]
        for cp in local:
            cp.start()

        def copy(a, k, px, py, dst_slot):
            return pltpu.make_async_remote_copy(
                src_ref=srcs[a].at[2 * px + py], dst_ref=outs[a].at[dst_slot], send_sem=send_sems.at[a, k],
                recv_sem=recv_sems.at[a, k], device_id=(px, py, c), device_id_type=MESH)

        sends = [copy(a, k, px, py, jme) for a in range(n) for k, (px, py) in enumerate(peers)]
        for cp in sends:
            cp.start()
        for a in range(n):
            for k, (px, py) in enumerate(peers):
                copy(a, k, px, py, 2 * px + py).wait_recv()
        for cp in sends:
            cp.wait_send()
        for cp in local:
            cp.wait()

    outs = pl.pallas_call(
        body, name=name, out_shape=[jax.ShapeDtypeStruct(p.shape, p.dtype) for p in parts],
        in_specs=[ANY] * n, out_specs=[ANY] * n,
        scratch_shapes=[pltpu.SemaphoreType.DMA((n, 3)), pltpu.SemaphoreType.DMA((n, 3)),
                        pltpu.SemaphoreType.DMA((n,))],
    )(*parts)
    return list(outs)


_SIDE_REMOTE = {"gather": 7, "sibling": 4, "chips": 3}
_FLIPS = [(0, 0, 1), (1, 0, 0), (0, 1, 0), (1, 1, 0), (1, 0, 1), (0, 1, 1), (1, 1, 1)]


def _side_plan(sides):
    inputs, out_shapes, scratch = [], [], []
    for kind, arrays in sides:
        n = len(arrays)
        for a in arrays:
            inputs.append(a)
            shape = {"gather": (N_DEV,) + a.shape, "sibling": (4,) + a.shape[1:], "chips": a.shape}[kind]
            out_shapes.append(jax.ShapeDtypeStruct(shape, a.dtype))
        scratch += [pltpu.SemaphoreType.DMA((n, _SIDE_REMOTE[kind])), pltpu.SemaphoreType.DMA((n, _SIDE_REMOTE[kind])),
                    pltpu.SemaphoreType.DMA((n,))]
    return inputs, out_shapes, scratch


def _side_copies(sides, in_refs, out_refs, sem_refs):
    x, y, c = _my_pos()
    starts, waits = [], []
    pos = 0
    for s, (kind, arrays) in enumerate(sides):
        send_sems, recv_sems, local_sems = sem_refs[3 * s:3 * s + 3]
        for a in range(len(arrays)):
            src, out = in_refs[pos], out_refs[pos]
            pos += 1

            def remote(k, src_ref, dst_ref, to):
                return pltpu.make_async_remote_copy(src_ref=src_ref, dst_ref=dst_ref, send_sem=send_sems.at[a, k],
                                                    recv_sem=recv_sems.at[a, k], device_id=to, device_id_type=MESH)

            def local(src_ref, dst_ref):
                cp = pltpu.make_async_copy(src_ref, dst_ref, local_sems.at[a])
                starts.append(cp.start)
                waits.append(cp.wait)

            if kind == "gather":
                me = 4 * x + 2 * y + c
                local(src, out.at[me])
                for k, (fx, fy, fc) in enumerate(_FLIPS):
                    px, py, pc = (1 - x if fx else x), (1 - y if fy else y), (1 - c if fc else c)
                    send = remote(k, src, out.at[me], (px, py, pc))
                    starts.append(send.start)
                    waits += [remote(k, src, out.at[4 * px + 2 * py + pc], (px, py, pc)).wait_recv, send.wait_send]
            elif kind == "sibling":
                for j in range(4):
                    cp = remote(j, src.at[2 * j + (1 - c)], out.at[j], (x, y, 1 - c))
                    starts.append(cp.start)
                    waits.append(cp.wait)
            else:
                jme = 2 * x + y
                local(src.at[jme], out.at[jme])
                for k, (px, py) in enumerate([(1 - x, y), (x, 1 - y), (1 - x, 1 - y)]):
                    send = remote(k, src.at[2 * px + py], out.at[jme], (px, py, c))
                    starts.append(send.start)
                    waits += [remote(k, src.at[2 * px + py], out.at[2 * px + py], (px, py, c)).wait_recv,
                              send.wait_send]
    return starts, waits


def _call_with_sides(body, sides, *, name, grid, in_specs, out_specs, out_shape, scratch_shapes, compiler_params, args):
    if not sides:
        res = pl.pallas_call(body, name=name, grid=grid, in_specs=in_specs, out_specs=out_specs, out_shape=out_shape,
                             scratch_shapes=scratch_shapes, compiler_params=compiler_params)(*args)
        return list(res), []
    s_in, s_out, s_scr = _side_plan(sides)
    n_in, n_out, n_scr, n_side = len(in_specs), len(out_specs), len(scratch_shapes), len(s_in)

    def wrapped(*refs):
        refs = list(refs)
        ins, side_in = refs[:n_in], refs[n_in:n_in + n_side]
        outs = refs[n_in + n_side:n_in + n_side + n_out]
        side_out = refs[n_in + n_side + n_out:n_in + 2 * n_side + n_out]
        rest = refs[n_in + 2 * n_side + n_out:]
        starts, waits = _side_copies(sides, side_in, side_out, rest[n_scr:])
        first = functools.reduce(jnp.logical_and, [pl.program_id(d) == 0 for d in range(len(grid))])
        last = functools.reduce(jnp.logical_and, [pl.program_id(d) == grid[d] - 1 for d in range(len(grid))])

        @pl.when(first)
        def _():
            for start in starts:
                start()

        body(*ins, *outs, *rest[:n_scr])

        @pl.when(last)
        def _():
            for wait in waits:
                wait()

    res = pl.pallas_call(
        wrapped, name=name, grid=grid, in_specs=list(in_specs) + [ANY] * n_side,
        out_specs=list(out_specs) + [ANY] * n_side, out_shape=list(out_shape) + s_out,
        scratch_shapes=list(scratch_shapes) + s_scr, compiler_params=compiler_params,
    )(*args, *s_in)
    return list(res[:n_out]), list(res[n_out:])


def _row_tile(r, l):
    t = min(r, max(16, ELEMENTWISE_TILE_BYTES // (4 * l) // 16 * 16))
    while r % t:
        t -= 16
    return t


def _pair_sum(buf, recv, core, name):
    _, r, l = buf.shape
    tr = _row_tile(r, l)

    def body(core_ref, a_ref, b_ref, o_ref):
        o_ref[...] = (a_ref[...] + b_ref[...]).astype(WIRE_DTYPE)

    return pl.pallas_call(
        body, name=name, out_shape=jax.ShapeDtypeStruct((4, r, l), WIRE_DTYPE),
        grid_spec=pltpu.PrefetchScalarGridSpec(
            num_scalar_prefetch=1, grid=(4, r // tr),
            in_specs=[pl.BlockSpec((None, tr, l), lambda j, i, cr: (2 * j + cr[0], i, 0)),
                      pl.BlockSpec((None, tr, l), lambda j, i, cr: (j, i, 0))],
            out_specs=pl.BlockSpec((None, tr, l), lambda j, i, cr: (j, i, 0))),
        compiler_params=_cparams(dimension_semantics=("arbitrary", "arbitrary")),
    )(core, buf, recv)


def _sum_parts(p_ref):
    return ((p_ref[0].astype(F32) + p_ref[1].astype(F32)) + (p_ref[2].astype(F32) + p_ref[3].astype(F32)))


def _sum4(parts, name):
    _, r, l = parts.shape
    tr = _row_tile(r, l)

    def body(p_ref, o_ref):
        o_ref[...] = _sum_parts(p_ref)

    return pl.pallas_call(
        body, name=name, out_shape=jax.ShapeDtypeStruct((r, l), F32), grid=(r // tr,),
        in_specs=[pl.BlockSpec((4, tr, l), lambda i: (0, i, 0))],
        out_specs=pl.BlockSpec((tr, l), lambda i: (i, 0)),
        compiler_params=_cparams(dimension_semantics=("arbitrary",)),
    )(parts)


def _adamw_update(w, gg, m, v):
    nm = ADAM_B1 * m + (1.0 - ADAM_B1) * gg
    nv = ADAM_B2 * v + (1.0 - ADAM_B2) * (gg * gg)
    m_hat = nm / (1.0 - ADAM_B1 ** ADAM_STEP)
    v_hat = nv / (1.0 - ADAM_B2 ** ADAM_STEP)
    return -ADAM_LR * (m_hat / (jnp.sqrt(v_hat) + ADAM_EPS) + ADAM_WD * w), nm, nv


def _adamw(w, g, m, v, name):
    r, l = w.shape
    tr = _row_tile(r, l)

    def body(w_ref, g_ref, m_ref, v_ref, d_ref, nm_ref, nv_ref):
        d_ref[...], nm_ref[...], nv_ref[...] = _adamw_update(w_ref[...], g_ref[...], m_ref[...], v_ref[...])

    spec = pl.BlockSpec((tr, l), lambda i: (i, 0))
    return pl.pallas_call(
        body, name=name, out_shape=[jax.ShapeDtypeStruct((r, l), F32)] * 3, grid=(r // tr,),
        in_specs=[spec] * 4, out_specs=[spec] * 3,
        compiler_params=_cparams(dimension_semantics=("arbitrary",)),
    )(w, g, m, v)


def _adamw_parts(w, parts, m, v, name):
    nl, r, l = w.shape
    tr = _row_tile(r, l)

    def body(*refs):
        w_ref, p_refs, (m_ref, v_ref, g_ref, d_ref, nm_ref, nv_ref) = refs[0], refs[1:1 + nl], refs[1 + nl:]
        layer = pl.program_id(0)
        gg = _sum_parts(p_refs[0])
        for q in range(1, nl):
            gg = jnp.where(layer == q, _sum_parts(p_refs[q]), gg)
        g_ref[...] = gg
        d_ref[...], nm_ref[...], nv_ref[...] = _adamw_update(w_ref[...], gg, m_ref[...], v_ref[...])

    spec = pl.BlockSpec((None, tr, l), lambda q, i: (q, i, 0))
    pspecs = [pl.BlockSpec((4, tr, l), lambda q, i, k=k: (0, jnp.where(q == k, i, 0), 0)) for k in range(nl)]
    return pl.pallas_call(
        body, name=name, out_shape=[jax.ShapeDtypeStruct((nl, r, l), F32)] * 4, grid=(nl, r // tr),
        in_specs=[spec] + pspecs + [spec, spec], out_specs=[spec] * 4,
        compiler_params=_cparams(dimension_semantics=("arbitrary", "arbitrary")),
    )(w, *parts, m, v)


def _to_rows(pieces, row_multiple):
    flat = jnp.concatenate([p.reshape(-1) for p in pieces])
    rows = -(-flat.shape[0] // LANES)
    rows = -(-rows // row_multiple) * row_multiple
    flat = jnp.pad(flat, (0, rows * LANES - flat.shape[0]))
    return flat.reshape(rows, LANES)


def _split_rows(rows, shapes):
    flat = rows.reshape(-1)
    out, off = [], 0
    for s in shapes:
        n = int(np.prod(s))
        out.append(flat[off:off + n].reshape(s))
        off += n
    return out


def _mod_fwd(cond, w_mod, b_my, name):
    nl, _, ncol = w_mod.shape

    def body(a_ref, w_ref, b_ref, o_ref):
        a = a_ref[...]
        s = a * _sigmoid(a)
        for i in range(nl):
            o_ref[i] = _dot(s, w_ref[i]) + b_ref[i]

    return pl.pallas_call(
        body, name=name, out_shape=jax.ShapeDtypeStruct((nl, 16, ncol), F32),
        compiler_params=_cparams(),
    )(cond, w_mod, b_my)


def _mod_bwd(cond, dm_all, dm_my, w_mod, name):
    nl, _, ncol = w_mod.shape

    def body(a_ref, dma_ref, dmm_ref, w_ref, gw_ref, gb_ref, gc_ref):
        a = a_ref[...]
        sg = _sigmoid(a)
        s = a * sg
        for i in range(nl):
            gw_ref[i] = _dot_tn(s, dmm_ref[i])
            gb_ref[i] = jnp.sum(dma_ref[i], axis=0, keepdims=True)
        back = _dot_nt(dmm_ref[0], w_ref[0])
        dsilu = sg * (1.0 + a * (1.0 - sg))
        gc_ref[...] = jnp.sum(back[8:16] * dsilu[8:16], axis=0, keepdims=True)

    return pl.pallas_call(
        body, name=name,
        out_shape=[jax.ShapeDtypeStruct((nl, D_MODEL, ncol), F32), jax.ShapeDtypeStruct((nl, 1, 3 * D_MODEL), F32),
                   jax.ShapeDtypeStruct((1, D_MODEL), F32)],
        compiler_params=_cparams(),
    )(cond, dm_all, dm_my, w_mod)


def _in_proj(xt, sc, sh, wg, name, sides=(), u_dtype=F32):
    t = xt.shape[0]
    tm = min(TM_MM, t)

    def body(x_ref, sc_ref, sh_ref, w_ref, u_ref, g_ref):
        h = (x_ref[...] * (1.0 + sc_ref[...]) + sh_ref[...]).astype(MXU_DTYPE)
        for k in range(N_WBLK):
            o = jnp.dot(h, w_ref[k], preferred_element_type=F32)
            if k < N_WBLK // 2:
                u_ref[:, k * WBLK:(k + 1) * WBLK] = o.astype(u_dtype)
            else:
                kk = k - N_WBLK // 2
                g_ref[:, kk * WBLK:(kk + 1) * WBLK] = o.astype(ACT_DTYPE)

    row = pl.BlockSpec((1, D_MODEL), lambda i: (0, 0))
    return _call_with_sides(
        body, sides, name=name,
        out_shape=[jax.ShapeDtypeStruct((t, D_INNER), u_dtype), jax.ShapeDtypeStruct((t, D_INNER), ACT_DTYPE)],
        grid=(t // tm,),
        in_specs=[pl.BlockSpec((tm, D_MODEL), lambda i: (i, 0)), row, row,
                  pl.BlockSpec((N_WBLK, D_MODEL, WBLK), lambda i: (0, 0, 0), pipeline_mode=pl.Buffered(1))],
        out_specs=[pl.BlockSpec((tm, D_INNER), lambda i: (i, 0))] * 2, scratch_shapes=[],
        compiler_params=_cparams(dimension_semantics=("arbitrary",)), args=[xt, sc, sh, wg])


def _halo_maps(nt, tm, n_blocks, pos, rows=SUBLANES):
    per = tm // rows
    prev = lambda cb, i: (jnp.maximum(pos(i) * per - 1, 0), cb)
    nxt = lambda cb, i: (jnp.minimum((pos(i) + 1) * per, n_blocks - 1), cb)
    return prev, nxt


def _conv_taps(u, prev8, next8, is_first, is_last):
    pz = jnp.where(is_first, 0.0, 1.0)
    nz = jnp.where(is_last, 0.0, 1.0)
    return _shifted(u, prev8 * pz, next8 * nz, [-2, -1, 1])


def _lru_gates(uv, wa_ref, wx_ref, ba, bx, cl, g):
    sl = slice(g * LANES, (g + 1) * LANES)
    uvg = uv[:, sl]
    r = _sigmoid(_dot(uvg, wa_ref[g]) + ba[:, sl])
    ii = _sigmoid(_dot(uvg, wx_ref[g]) + bx[:, sl])
    la = cl[:, sl] * r
    a = jnp.exp(la)
    q = jnp.tanh(-la) * (1.0 + a * a)
    rs = lax.rsqrt(jnp.maximum(q, SQRT_FLOOR))
    return uvg, r, ii, a, q * rs, rs


def _scan_rows(seg):
    return -(-(SCAN_ROW_T * (seg - 1) + SCAN_ROW_J * (N_SEG - 1) + 1) // SUBLANES) * SUBLANES


def _seg_chunk(j, c):
    return pl.ds(SCAN_ROW_T * SUBLANES * c + SCAN_ROW_J * j, SUBLANES, stride=SCAN_ROW_T)


def _seg_scatter(ref, g, seg, value):
    for j in range(N_SEG):
        for c in range(seg // SUBLANES):
            r0 = j * seg + SUBLANES * c
            ref[g, _seg_chunk(j, c), :] = value[r0:r0 + SUBLANES]


def _scan_tile(a_s, b_s, carry_ref, write_out, seg, reverse, chunks_per_write=1):
    n_g = a_s.shape[0]
    unroll = SCAN_UNROLL if seg % SCAN_UNROLL == 0 else 1

    n_trips = seg // unroll

    def steps(k, state):
        hs, cs = list(state[0]), list(state[1])
        base = ((n_trips - 1 - k) if reverse else k) * unroll
        for q in (range(unroll - 1, -1, -1) if reverse else range(unroll)):
            t = base + q
            rows = pl.ds(t * SCAN_ROW_T, N_SEG, stride=SCAN_ROW_J)
            for g in range(n_g):
                a = a_s[g, rows, :]
                b = b_s[g, rows, :]
                hs[g] = a * hs[g] + b
                cs[g] = a * cs[g]
                b_s[g, rows, :] = hs[g]
                a_s[g, rows, :] = cs[g]
        return tuple(hs), tuple(cs)

    zeros = tuple(jnp.zeros((N_SEG, LANES), F32) for _ in range(n_g))
    ones = tuple(jnp.ones((N_SEG, LANES), F32) for _ in range(n_g))
    h_fin, a_fin = lax.fori_loop(0, seg // unroll, steps, (zeros, ones))

    order = list(range(N_SEG - 1, -1, -1)) if reverse else list(range(N_SEG))
    for g in range(n_g):
        carry = carry_ref[:, g * LANES:(g + 1) * LANES]
        for j in order:
            for c0 in range(0, seg // SUBLANES, chunks_per_write):
                parts = [b_s[g, _seg_chunk(j, c), :] + a_s[g, _seg_chunk(j, c), :] * carry
                         for c in range(c0, c0 + chunks_per_write)]
                write_out(j, c0, g, parts[0] if chunks_per_write == 1 else jnp.concatenate(parts, axis=0))
            carry = a_fin[g][j:j + 1] * carry + h_fin[g][j:j + 1]
        carry_ref[:, g * LANES:(g + 1) * LANES] = carry


def _lru_specs(s, tm, cb, direction_pos, nt):
    n_rows8 = s // SUBLANES
    prev, nxt = _halo_maps(nt, tm, n_rows8, direction_pos)
    tile = pl.BlockSpec((tm, cb), lambda c, i: (direction_pos(i), c))
    return tile, pl.BlockSpec((SUBLANES, cb), prev), pl.BlockSpec((SUBLANES, cb), nxt)


def _lru_param_specs(cb, d):
    n_g = cb // LANES
    vec = pl.BlockSpec((1, cb), lambda c, i: (0, c))
    dvec = pl.BlockSpec((None, 1, cb), lambda c, i: (d, 0, c))
    wmat = pl.BlockSpec((None, n_g, LRU_BLOCK, LRU_BLOCK), lambda c, i: (d, c, 0, 0))
    return vec, dvec, wmat


def _lru_fwd(src, h0, p, d, name, conv, sides=()):
    s = src.shape[0]
    tm = min(TM_LRU, s)
    cb = CB_LRU
    n_g = cb // LANES
    nt = s // tm
    seg = tm // N_SEG
    pos = (lambda i: i) if d == 0 else (lambda i: nt - 1 - i)

    def body(*refs):
        refs = list(refs)
        u_ref = refs.pop(0)
        if conv:
            up_ref, un_ref, cw_ref, cbias_ref = [refs.pop(0) for _ in range(4)]
        wa_ref, wx_ref, ba_ref, bx_ref, lam_ref, h0_ref, h_ref, hc_ref = [refs.pop(0) for _ in range(8)]
        uv_ref = refs.pop(0) if conv else None
        a_s, b_s = refs
        i = pl.program_id(1)
        tp = pos(i)

        @pl.when(i == 0)
        def _():
            hc_ref[...] = h0_ref[...]

        if conv:
            u_t = u_ref[...]
            um2, um1, up1 = _conv_taps(u_t, up_ref[...], un_ref[...], tp == 0, tp == nt - 1)
            cw = cw_ref[...]
            uv_ref[...] = um2 * cw[0:1] + um1 * cw[1:2] + u_t * cw[2:3] + up1 * cw[3:4] + cbias_ref[...]
        src_ref = uv_ref if conv else u_ref
        cl = LRU_C * _log_sigmoid(lam_ref[...])
        ba, bx = ba_ref[...], bx_ref[...]
        for g in range(n_g):
            uvg, r, ii, a, sq, _ = _lru_gates(src_ref, wa_ref, wx_ref, ba, bx, cl, g)
            b = sq * (ii * uvg)
            _seg_scatter(a_s, g, seg, a)
            _seg_scatter(b_s, g, seg, b)

        per_write = 2 if (seg // SUBLANES) % 2 == 0 else 1

        def write_out(j, c, g, h):
            h_ref[pl.ds(j * seg + SUBLANES * c, SUBLANES * per_write), pl.ds(g * LANES, LANES)] = h.astype(ACT_DTYPE)

        _scan_tile(a_s, b_s, hc_ref, write_out, seg, reverse=(d == 1), chunks_per_write=per_write)

    tile, prev, nxt = _lru_specs(s, tm, cb, pos, nt)
    vec, dvec, wmat = _lru_param_specs(cb, d)
    wide = jax.ShapeDtypeStruct((s, D_INNER), F32)
    conv_specs = [prev, nxt, pl.BlockSpec((4, cb), lambda c, i: (0, c)), vec] if conv else []
    conv_args = [src, src, p["conv_w"], p["conv_b"]] if conv else []
    return _call_with_sides(
        body, sides, name=name,
        out_shape=[jax.ShapeDtypeStruct((s, D_INNER), ACT_DTYPE), jax.ShapeDtypeStruct((1, D_INNER), F32)]
        + ([wide] if conv else []),
        grid=(D_INNER // cb, nt),
        in_specs=[tile] + conv_specs + [wmat, wmat, dvec, dvec, dvec, vec],
        out_specs=[tile, vec] + ([tile] if conv else []),
        scratch_shapes=[pltpu.VMEM((n_g, _scan_rows(seg), LANES), F32)] * 2,
        compiler_params=_cparams(dimension_semantics=("arbitrary", "arbitrary")),
        args=[src, *conv_args, p["wa"], p["wx"], p["ba"], p["bx"], p["lam"], h0])


def _lru_bwd(uv, dh, h, h0, lam_in, p, d, name, sides=()):
    s = uv.shape[0]
    tm = min(TM_LRU, s)
    cb = CB_LRU
    n_g = cb // LANES
    nt = s // tm
    seg = tm // N_SEG
    pos = (lambda i: nt - 1 - i) if d == 0 else (lambda i: i)

    def body(uv_ref, dh_ref, h_ref, hh_ref, wa_ref, wx_ref, ba_ref, bx_ref,
             lam_ref, h0_ref, lin_ref, duv_ref, gwa_ref, gwx_ref, gv_ref, lc_ref, a_s, b_s, lp_s,
             r_s, i_s, q_s, rq_s, a_keep):
        i = pl.program_id(1)
        tp = pos(i)

        @pl.when(i == 0)
        def _():
            lc_ref[...] = lin_ref[...]
            gwa_ref[...] = jnp.zeros_like(gwa_ref)
            gwx_ref[...] = jnp.zeros_like(gwx_ref)
            gv_ref[...] = jnp.zeros_like(gv_ref)

        uv = uv_ref[...]
        lam = lam_ref[...]
        cl = LRU_C * _log_sigmoid(lam)
        ba, bx = ba_ref[...], bx_ref[...]
        dh_t = dh_ref[...].astype(F32)
        carry_in = lc_ref[...]
        for g in range(n_g):
            sl = slice(g * LANES, (g + 1) * LANES)
            _, r, ii, a, sq, rs = _lru_gates(uv, wa_ref, wx_ref, ba, bx, cl, g)
            r_s[:, sl], i_s[:, sl], q_s[:, sl], rq_s[:, sl], a_keep[:, sl] = r, ii, sq, rs, a
            b = a * dh_t[:, sl]
            _seg_scatter(a_s, g, seg, a)
            _seg_scatter(b_s, g, seg, b)

        def write_out(j, c, g, v):
            lp_s[pl.ds(j * seg + SUBLANES * c, SUBLANES), pl.ds(g * LANES, LANES)] = v

        _scan_tile(a_s, b_s, lc_ref, write_out, seg, reverse=(d == 0))

        h_t = h_ref[...].astype(F32)
        hh = hh_ref[...].astype(F32)
        if d == 0:
            edge = jnp.where(tp == 0, h0_ref[...], hh[H_HALO - 1:H_HALO])
            h_prev = _shift_down(h_t, edge)
            lam_t = dh_t + _shift_up(lp_s[...], carry_in)
        else:
            edge = jnp.where(tp == nt - 1, h0_ref[...], hh[0:1])
            h_prev = _shift_up(h_t, edge)
            lam_t = dh_t + _shift_down(lp_s[...], carry_in)

        dsig = LRU_C * _sigmoid(-lam)
        for g in range(n_g):
            sl = slice(g * LANES, (g + 1) * LANES)
            uvg, r, ii, a, sq = uv[:, sl], r_s[:, sl], i_s[:, sl], a_keep[:, sl], q_s[:, sl]
            lt = lam_t[:, sl]
            ls = lt * sq
            dla = (lt * a) * (h_prev[:, sl] - (ii * uvg) * (a * rq_s[:, sl]))
            dzr = (dla * cl[:, sl]) * r * (1.0 - r)
            dzi = (ls * uvg) * ii * (1.0 - ii)
            duv_ref[:, sl] = ls * ii + _dot_nt(dzr, wa_ref[g]) + _dot_nt(dzi, wx_ref[g])
            gwa_ref[g] += _dot_tn(uvg, dzr)
            gwx_ref[g] += _dot_tn(uvg, dzi)
            gv_ref[0:1, sl] += _rowsum(dzr)
            gv_ref[1:2, sl] += _rowsum(dzi)
            gv_ref[2:3, sl] += _rowsum(dla * r) * dsig[:, sl]

    tile, _, _ = _lru_specs(s, tm, cb, pos, nt)
    vec, dvec, wmat = _lru_param_specs(cb, d)
    h_prev_map, h_next_map = _halo_maps(nt, tm, s // H_HALO, pos, rows=H_HALO)
    hh_spec = pl.BlockSpec((H_HALO, cb), h_prev_map if d == 0 else h_next_map)
    gw_spec = pl.BlockSpec((n_g, LRU_BLOCK, LRU_BLOCK), lambda c, i: (c, 0, 0))
    n_blk = D_INNER // LRU_BLOCK
    return _call_with_sides(
        body, sides, name=name,
        out_shape=[jax.ShapeDtypeStruct((s, D_INNER), F32),
                   jax.ShapeDtypeStruct((n_blk, LRU_BLOCK, LRU_BLOCK), F32),
                   jax.ShapeDtypeStruct((n_blk, LRU_BLOCK, LRU_BLOCK), F32),
                   jax.ShapeDtypeStruct((SUBLANES, D_INNER), F32),
                   jax.ShapeDtypeStruct((1, D_INNER), F32)],
        grid=(D_INNER // cb, nt),
        in_specs=[tile, tile, tile, hh_spec, wmat, wmat, dvec, dvec, dvec, vec, vec],
        out_specs=[tile, gw_spec, gw_spec, pl.BlockSpec((SUBLANES, cb), lambda c, i: (0, c)), vec],
        scratch_shapes=[pltpu.VMEM((n_g, _scan_rows(seg), LANES), F32)] * 2 + [pltpu.VMEM((tm, cb), F32)] * 6,
        compiler_params=_cparams(dimension_semantics=("arbitrary", "arbitrary")),
        args=[uv, dh, h, h, p["wa"], p["wx"], p["ba"], p["bx"], p["lam"], h0, lam_in])


def _out0(hf, hb, g, xt, gt, wo, lg, lb, name):
    t = xt.shape[0]
    tm = min(TM_MM, t)

    def body(hf_ref, hb_ref, g_ref, x_ref, gt_ref, w_ref, lg_ref, lb_ref, x1_ref, br_ref):
        br = None
        for k in range(D_INNER // WBLK):
            sl = slice(k * WBLK, (k + 1) * WBLK)
            gg = g_ref[:, sl].astype(F32)
            p = (hf_ref[:, sl].astype(F32) + hb_ref[:, sl].astype(F32)) * (gg * _sigmoid(gg))
            part = _dot(p, w_ref[sl, :])
            br = part if br is None else br + part
        z = ALPHA * x_ref[...] + gt_ref[...] * br
        xhat, _ = _layer_norm_stats(z)
        x1_ref[...] = xhat * lg_ref[...] + lb_ref[...]
        br_ref[...] = br.astype(ACT_DTYPE)

    wide = pl.BlockSpec((tm, D_INNER), lambda i: (i, 0))
    nar = pl.BlockSpec((tm, D_MODEL), lambda i: (i, 0))
    row = pl.BlockSpec((1, D_MODEL), lambda i: (0, 0))
    return pl.pallas_call(
        body, name=name,
        out_shape=[jax.ShapeDtypeStruct((t, D_MODEL), F32), jax.ShapeDtypeStruct((t, D_MODEL), ACT_DTYPE)],
        grid=(t // tm,),
        in_specs=[wide, wide, wide, nar, row,
                  pl.BlockSpec((D_INNER, D_MODEL), lambda i: (0, 0), pipeline_mode=pl.Buffered(1)), row, row],
        out_specs=[nar, nar],
        compiler_params=_cparams(dimension_semantics=("arbitrary",)),
    )(hf, hb, g, xt, gt, wo, lg, lb)


def _unrolled_loop(n, fn, unroll=4):
    while n % unroll:
        unroll //= 2

    def trip(k, carry):
        for q in range(unroll):
            fn(k * unroll + q)
        return carry
    lax.fori_loop(0, n // unroll, trip, 0)


def _window(n, w):
    t = np.arange(n)
    return np.clip(t - w // 2, 0, n), np.clip(t + w // 2, 0, n)


def _pool_tables(n_rows, transpose):
    boxes, inv_c, inv_r = [], [], []
    for w in POOL_WINDOWS:
        lo, hi = _window(GRID_W, w)
        m = np.zeros((GRID_W, GRID_W), np.float32)
        for r in range(GRID_W):
            m[r, lo[r]:hi[r]] = 1.0
        m = np.kron(np.eye(POOL_TOK // GRID_W, dtype=np.float32), m)
        boxes.append(m.T if transpose else m)
        inv_c.append(np.broadcast_to((1.0 / (hi - lo).astype(np.float32))[:, None], (GRID_W, LANES)))
        lo_r, hi_r = _window(n_rows, w)
        inv_r.append(1.0 / (hi_r - lo_r).astype(np.float32))
    return (jnp.asarray(np.stack(boxes), MXU_DTYPE), jnp.asarray(np.stack(inv_c), F32),
            jnp.asarray(np.stack(inv_r), F32))


def _pool_mix(xin, transpose, out_dtype, name):
    s = xin.shape[0]
    n_rows = s // GRID_W
    pad_t = SUBLANES * GRID_W
    rows_per_blk = POOL_TOK // GRID_W
    n_slab = D_INNER // LANES
    slabs_per_group = POOL_GROUP // LANES
    n_win = len(POOL_WINDOWS)
    boxes, inv_c, inv_r = _pool_tables(n_rows, transpose)
    exact_operand = (not transpose) and xin.dtype == MXU_DTYPE and MXU_DTYPE != F32

    def body(invr_ref, box_ref, invc_ref, x_ref, o_ref, pad_s):
        k = pl.program_id(0) // slabs_per_group
        pad_s[pl.ds(0, pad_t), :] = jnp.zeros((pad_t, LANES), F32)
        pad_s[pl.ds(pad_t + s, pad_t), :] = jnp.zeros((pad_t, LANES), F32)

        for kk, w in enumerate(POOL_WINDOWS):
            half = w // 2
            offsets = list(range(-(half - 1), half + 1)) if transpose else list(range(-half, half))

            @pl.when(k == kk)
            def _():
                inv_col = invc_ref[kk]

                def col_box(b):
                    st = pl.multiple_of(b * POOL_TOK, POOL_TOK)
                    xb = x_ref[pl.ds(st, POOL_TOK), :]
                    if exact_operand:
                        pad_s[pl.ds(pad_t + st, POOL_TOK), :] = jnp.dot(box_ref[kk], xb, preferred_element_type=F32)
                        return
                    xb = xb.astype(F32)
                    if transpose:
                        xb = xb * jnp.concatenate(
                            [inv_col * invr_ref[kk, b * rows_per_blk + q] for q in range(rows_per_blk)], axis=0)
                    hi = xb.astype(MXU_DTYPE)
                    lo = (xb - hi.astype(F32)).astype(MXU_DTYPE)
                    both = jnp.dot(box_ref[kk], jnp.concatenate([hi, lo], axis=1), preferred_element_type=F32)
                    pad_s[pl.ds(pad_t + st, POOL_TOK), :] = both[:, :LANES] + both[:, LANES:]
                _unrolled_loop(s // POOL_TOK, col_box)

                def row_box(r):
                    st = pl.multiple_of(r * GRID_W, GRID_W)
                    acc = pad_s[pl.ds(pad_t + st + offsets[0] * GRID_W, GRID_W), :]
                    for o in offsets[1:]:
                        acc = acc + pad_s[pl.ds(pad_t + st + o * GRID_W, GRID_W), :]
                    if not transpose:
                        acc = acc * (inv_col * invr_ref[kk, r])
                    o_ref[pl.ds(st, GRID_W), :] = (acc - x_ref[pl.ds(st, GRID_W), :].astype(F32)).astype(out_dtype)
                _unrolled_loop(n_rows, row_box)

    slab = pl.BlockSpec((s, LANES), lambda i: (0, i))
    return pl.pallas_call(
        body, name=name, out_shape=jax.ShapeDtypeStruct((s, D_INNER), out_dtype), grid=(n_slab,),
        in_specs=[pl.BlockSpec(memory_space=pltpu.SMEM),
                  pl.BlockSpec((n_win, POOL_TOK, POOL_TOK), lambda i: (0, 0, 0)),
                  pl.BlockSpec((n_win, GRID_W, LANES), lambda i: (0, 0, 0)), slab],
        out_specs=slab,
        scratch_shapes=[pltpu.VMEM((s + 2 * pad_t, LANES), F32)],
        compiler_params=_cparams(dimension_semantics=("arbitrary",)),
    )(inv_r, boxes, inv_c, xin)


def _out1(dmix, pw, ps, g, x1, gt, wo, lg, lb, tgt, name):
    t = x1.shape[0]
    tm = min(TM_MM, t)
    n_grp = len(POOL_WINDOWS)

    def body(d_ref, pw_ref, ps_ref, g_ref, x1_ref, gt_ref, w_ref, lg_ref, lb_ref, tgt_ref, dz_ref, st_ref):
        @pl.when(pl.program_id(0) == 0)
        def _():
            st_ref[...] = jnp.zeros_like(st_ref)

        br = jnp.zeros((tm, D_MODEL), F32)
        for k in range(n_grp):
            sl = slice(k * POOL_GROUP, (k + 1) * POOL_GROUP)
            y = jnp.dot(d_ref[:, sl], pw_ref[k], preferred_element_type=F32) * ps_ref[:, sl]
            gg = g_ref[:, sl].astype(F32)
            br = br + _dot(y * (gg * _sigmoid(gg)), w_ref[sl, :])
        z = ALPHA * x1_ref[...] + gt_ref[...] * br
        xhat, rstd = _layer_norm_stats(z)
        lg_v = lg_ref[...]
        err = xhat * lg_v + lb_ref[...] - tgt_ref[...]
        dy = err * (1.0 / D_MODEL)
        dz = _layer_norm_bwd(dy, xhat, rstd, lg_v)
        dz_ref[...] = dz
        st_ref[0:1, :] += _rowsum(dy * xhat)
        st_ref[1:2, :] += _rowsum(dy)
        st_ref[2:3, :] += _rowsum(dz * br)
        st_ref[3:4, :] += _rowsum(err * err)

    wide = pl.BlockSpec((tm, D_INNER), lambda i: (i, 0))
    nar = pl.BlockSpec((tm, D_MODEL), lambda i: (i, 0))
    row = pl.BlockSpec((1, D_MODEL), lambda i: (0, 0))
    return pl.pallas_call(
        body, name=name,
        out_shape=[jax.ShapeDtypeStruct((t, D_MODEL), F32), jax.ShapeDtypeStruct((SUBLANES, D_MODEL), F32)],
        grid=(t // tm,),
        in_specs=[wide, pl.BlockSpec((n_grp, POOL_GROUP, POOL_GROUP), lambda i: (0, 0, 0)),
                  pl.BlockSpec((1, D_INNER), lambda i: (0, 0)), wide, nar, row,
                  pl.BlockSpec((D_INNER, D_MODEL), lambda i: (0, 0), pipeline_mode=pl.Buffered(1)), row, row, nar],
        out_specs=[nar, pl.BlockSpec((SUBLANES, D_MODEL), lambda i: (0, 0))],
        compiler_params=_cparams(dimension_semantics=("arbitrary",)),
    )(dmix, pw, ps, g, x1, gt, wo, lg, lb, tgt)


def _flush(acc, out_hbm, sem):
    cp = pltpu.make_async_copy(acc, out_hbm, sem)
    cp.start()
    cp.wait()


def _bout1(dz, dmix, g, pw, ps, gt, wo, name):
    t = dz.shape[0]
    tm = min(TM_MM, t)
    nt = t // tm
    n_grp = len(POOL_WINDOWS)

    def body(dz_ref, d_ref, g_ref, pw_ref, ps_ref, gt_ref, w_ref, dd_ref, dg_ref, gwo_hbm, gpw_hbm, gps_ref,
             gwo_acc, gpw_acc, sems):
        i = pl.program_id(0)

        @pl.when(i == 0)
        def _():
            gwo_acc[...] = jnp.zeros_like(gwo_acc)
            gpw_acc[...] = jnp.zeros_like(gpw_acc)
            gps_ref[...] = jnp.zeros_like(gps_ref)

        db = (gt_ref[...] * dz_ref[...]).astype(MXU_DTYPE)
        for k in range(n_grp):
            sl = slice(k * POOL_GROUP, (k + 1) * POOL_GROUP)
            dk = d_ref[:, sl]
            po = jnp.dot(dk, pw_ref[k], preferred_element_type=F32)
            psk = ps_ref[:, sl]
            y = po * psk
            gg = g_ref[:, sl].astype(F32)
            sg = _sigmoid(gg)
            silu = gg * sg
            gwo_acc[sl, :] += _dot_tn(y * silu, db)
            dp = _dot_nt(db, w_ref[sl, :])
            dy = dp * silu
            dg_ref[:, sl] = (dp * y * (sg * (1.0 + gg * (1.0 - sg)))).astype(MXU_DTYPE)
            gps_ref[0:1, sl] += _rowsum(dy * po)
            dpo = (dy * psk).astype(MXU_DTYPE)
            gpw_acc[k] += _dot_tn(dk, dpo)
            dd_ref[:, sl] = _dot_nt(dpo, pw_ref[k])

        @pl.when(i == nt - 1)
        def _():
            _flush(gwo_acc, gwo_hbm, sems.at[0])
            _flush(gpw_acc, gpw_hbm, sems.at[1])

    wide = pl.BlockSpec((tm, D_INNER), lambda i: (i, 0))
    nar = pl.BlockSpec((tm, D_MODEL), lambda i: (i, 0))
    return pl.pallas_call(
        body, name=name,
        out_shape=[jax.ShapeDtypeStruct((t, D_INNER), F32), jax.ShapeDtypeStruct((t, D_INNER), MXU_DTYPE),
                   jax.ShapeDtypeStruct((D_INNER, D_MODEL), F32),
                   jax.ShapeDtypeStruct((n_grp, POOL_GROUP, POOL_GROUP), F32),
                   jax.ShapeDtypeStruct((SUBLANES, D_INNER), F32)],
        grid=(nt,),
        in_specs=[nar, wide, wide,
                  pl.BlockSpec((n_grp, POOL_GROUP, POOL_GROUP), lambda i: (0, 0, 0), pipeline_mode=pl.Buffered(1)),
                  pl.BlockSpec((1, D_INNER), lambda i: (0, 0)), pl.BlockSpec((1, D_MODEL), lambda i: (0, 0)),
                  pl.BlockSpec((D_INNER, D_MODEL), lambda i: (0, 0), pipeline_mode=pl.Buffered(1))],
        out_specs=[wide, wide, ANY, ANY, pl.BlockSpec((SUBLANES, D_INNER), lambda i: (0, 0))],
        scratch_shapes=[pltpu.VMEM((D_INNER, D_MODEL), F32), pltpu.VMEM((n_grp, POOL_GROUP, POOL_GROUP), F32),
                        pltpu.SemaphoreType.DMA((2,))],
        compiler_params=_cparams(dimension_semantics=("arbitrary",)),
    )(dz, dmix, g, pw, ps, gt, wo)


def _bout0(dx1, xt, br0, lg, hf, hb, g, gt, wo, name, sides=()):
    t = dx1.shape[0]
    tm = min(TM_MM, t)
    nt = t // tm

    def body(dx_ref, x_ref, br_ref, lg_ref, hf_ref, hb_ref, g_ref, gt_ref, w_ref,
             dz_ref, dy_ref, dg_ref, gwo_hbm, st_ref, gwo_acc, sem):
        i = pl.program_id(0)

        @pl.when(i == 0)
        def _():
            gwo_acc[...] = jnp.zeros_like(gwo_acc)
            st_ref[...] = jnp.zeros_like(st_ref)

        dx = dx_ref[...]
        br = br_ref[...].astype(F32)
        gate = gt_ref[...]
        xhat, rstd = _layer_norm_stats(ALPHA * x_ref[...] + gate * br)
        dz = _layer_norm_bwd(dx, xhat, rstd, lg_ref[...])
        dz_ref[...] = dz
        st_ref[0:1, :] += _rowsum(dx * xhat)
        st_ref[1:2, :] += _rowsum(dx)
        st_ref[2:3, :] += _rowsum(dz * br)
        db = (gate * dz).astype(MXU_DTYPE)
        for k in range(D_INNER // WBLK):
            sl = slice(k * WBLK, (k + 1) * WBLK)
            y = hf_ref[:, sl].astype(F32) + hb_ref[:, sl].astype(F32)
            gg = g_ref[:, sl].astype(F32)
            sg = _sigmoid(gg)
            silu = gg * sg
            gwo_acc[sl, :] += _dot_tn(y * silu, db)
            dp = _dot_nt(db, w_ref[sl, :])
            dy_ref[:, sl] = (dp * silu).astype(ACT_DTYPE)
            dg_ref[:, sl] = (dp * y * (sg * (1.0 + gg * (1.0 - sg)))).astype(MXU_DTYPE)

        @pl.when(i == nt - 1)
        def _():
            _flush(gwo_acc, gwo_hbm, sem)

    wide = pl.BlockSpec((tm, D_INNER), lambda i: (i, 0))
    nar = pl.BlockSpec((tm, D_MODEL), lambda i: (i, 0))
    row = pl.BlockSpec((1, D_MODEL), lambda i: (0, 0))
    return _call_with_sides(
        body, sides, name=name,
        out_shape=[jax.ShapeDtypeStruct((t, D_MODEL), F32), jax.ShapeDtypeStruct((t, D_INNER), ACT_DTYPE),
                   jax.ShapeDtypeStruct((t, D_INNER), MXU_DTYPE), jax.ShapeDtypeStruct((D_INNER, D_MODEL), F32),
                   jax.ShapeDtypeStruct((SUBLANES, D_MODEL), F32)],
        grid=(nt,),
        in_specs=[nar, nar, nar, row, wide, wide, wide, row,
                  pl.BlockSpec((D_INNER, D_MODEL), lambda i: (0, 0), pipeline_mode=pl.Buffered(1))],
        out_specs=[nar, wide, wide, ANY, pl.BlockSpec((SUBLANES, D_MODEL), lambda i: (0, 0))],
        scratch_shapes=[pltpu.VMEM((D_INNER, D_MODEL), F32), pltpu.SemaphoreType.DMA(())],
        compiler_params=_cparams(dimension_semantics=("arbitrary",)),
        args=[dx1, xt, br0, lg, hf, hb, g, gt, wo])


def _conv_bwd(duvf, duvb, u, conv_w, name, sides=()):
    s = u.shape[0]
    tm = min(TM_LRU, s)
    cb = CB_LRU
    nt = s // tm

    def body(df_ref, dfp_ref, dfn_ref, db_ref, dbp_ref, dbn_ref, u_ref, cw_ref, du_ref, cst_ref):
        i = pl.program_id(1)

        @pl.when(i == 0)
        def _():
            cst_ref[...] = jnp.zeros_like(cst_ref)

        first, last = i == 0, i == nt - 1
        pz = jnp.where(first, 0.0, 1.0)
        nz = jnp.where(last, 0.0, 1.0)
        dout = df_ref[...] + db_ref[...]
        dm1, dp1, dp2 = _shifted(dout, (dfp_ref[...] + dbp_ref[...]) * pz, (dfn_ref[...] + dbn_ref[...]) * nz,
                                 [-1, 1, 2])
        cw = cw_ref[...]
        du_ref[...] = (dp2 * cw[0:1] + dp1 * cw[1:2] + dout * cw[2:3] + dm1 * cw[3:4]).astype(MXU_DTYPE)
        u_t = u_ref[...]
        cst_ref[0:1, :] += _rowsum(dp2 * u_t)
        cst_ref[1:2, :] += _rowsum(dp1 * u_t)
        cst_ref[2:3, :] += _rowsum(dout * u_t)
        cst_ref[3:4, :] += _rowsum(dm1 * u_t)
        cst_ref[4:5, :] += _rowsum(dout)

    tile, prev, nxt = _lru_specs(s, tm, cb, lambda i: i, nt)
    return _call_with_sides(
        body, sides, name=name,
        out_shape=[jax.ShapeDtypeStruct((s, D_INNER), MXU_DTYPE), jax.ShapeDtypeStruct((SUBLANES, D_INNER), F32)],
        grid=(D_INNER // cb, nt),
        in_specs=[tile, prev, nxt] * 2 + [tile, pl.BlockSpec((4, cb), lambda c, i: (0, c))],
        out_specs=[tile, pl.BlockSpec((SUBLANES, cb), lambda c, i: (0, c))], scratch_shapes=[],
        compiler_params=_cparams(dimension_semantics=("arbitrary", "arbitrary")),
        args=[duvf, duvf, duvf, duvb, duvb, duvb, u, conv_w])


def _bin(du, dg, xin, dzin, sc, sh, wg, name, gw_init=None, sides=()):
    t = xin.shape[0]
    tm = min(TM_MM, t)
    nt = t // tm
    has_g, has_dx, has_init = dg is not None, dzin is not None, gw_init is not None
    half = N_WBLK // 2
    n_blk = N_WBLK if has_g else half

    def body(*refs):
        refs = list(refs)
        du_ref = refs.pop(0)
        dg_ref = refs.pop(0) if has_g else None
        x_ref = refs.pop(0)
        dz_ref = refs.pop(0) if has_dx else None
        sc_ref, sh_ref, w_ref = refs.pop(0), refs.pop(0), refs.pop(0)
        init_hbm = refs.pop(0) if has_init else None
        dx_ref = refs.pop(0) if has_dx else None
        gw_hbm, st_ref, gw_acc, sem = refs
        i = pl.program_id(0)

        @pl.when(i == 0)
        def _():
            st_ref[...] = jnp.zeros_like(st_ref)
            first_zero = 0
            if has_init:
                _flush(init_hbm, gw_acc.at[pl.ds(0, half)], sem)
                first_zero = half
            for k in range(first_zero, n_blk):
                gw_acc[k] = jnp.zeros((D_MODEL, WBLK), F32)

        xv = x_ref[...]
        scale = 1.0 + sc_ref[...]
        h = (xv * scale + sh_ref[...]).astype(MXU_DTYPE)
        dh = None
        for k in range(n_blk):
            src = du_ref if k < half else dg_ref
            kk = k % half
            dk = src[:, kk * WBLK:(kk + 1) * WBLK]
            gw_acc[k] += _dot_tn(h, dk)
            contrib = _dot_nt(dk, w_ref[k])
            dh = contrib if dh is None else dh + contrib
        st_ref[0:1, :] += _rowsum(dh * xv)
        st_ref[1:2, :] += _rowsum(dh)
        if has_dx:
            dx_ref[...] = ALPHA * dz_ref[...] + dh * scale

        @pl.when(i == nt - 1)
        def _():
            _flush(gw_acc, gw_hbm, sem)

    wide = pl.BlockSpec((tm, D_INNER), lambda i: (i, 0))
    nar = pl.BlockSpec((tm, D_MODEL), lambda i: (i, 0))
    row = pl.BlockSpec((1, D_MODEL), lambda i: (0, 0))
    wspec = pl.BlockSpec((n_blk, D_MODEL, WBLK), lambda i: (0, 0, 0), pipeline_mode=pl.Buffered(1))
    in_specs = ([wide] + ([wide] if has_g else []) + [nar] + ([nar] if has_dx else []) + [row, row, wspec]
                + ([ANY] if has_init else []))
    args = ([du] + ([dg] if has_g else []) + [xin] + ([dzin] if has_dx else []) + [sc, sh, wg]
            + ([gw_init] if has_init else []))
    out_shape = ([jax.ShapeDtypeStruct((t, D_MODEL), F32)] if has_dx else []) + [
        jax.ShapeDtypeStruct((n_blk, D_MODEL, WBLK), F32), jax.ShapeDtypeStruct((SUBLANES, D_MODEL), F32)]
    out_specs = ([nar] if has_dx else []) + [ANY, pl.BlockSpec((SUBLANES, D_MODEL), lambda i: (0, 0))]
    return _call_with_sides(
        body, sides, name=name, out_shape=out_shape, grid=(nt,), in_specs=in_specs, out_specs=out_specs,
        scratch_shapes=[pltpu.VMEM((n_blk, D_MODEL, WBLK), F32), pltpu.SemaphoreType.DMA(())],
        compiler_params=_cparams(dimension_semantics=("arbitrary",)), args=args)


def _blocks_by_device(a, axis):
    shape = a.shape
    a = a.reshape(shape[:axis] + (N_DEV, shape[axis] // N_DEV) + shape[axis + 1:])
    return jnp.moveaxis(a, axis, 0)


def kernel(x, c, ctx, c_ctx, w_mod, b_mod, w_in, w_out, ln_g, ln_b, conv_w, conv_b, lru_wa, lru_ba, lru_wx, lru_bx, lru_lam, pool_w, pool_scale, loss_target, m_c_ctx, m_w_mod, m_b_mod, m_w_in, m_w_out, m_ln_g, m_ln_b, m_conv_w, m_conv_b, m_lru_wa, m_lru_ba, m_lru_wx, m_lru_bx, m_lru_lam, m_pool_w, m_pool_scale, v_c_ctx, v_w_mod, v_b_mod, v_w_in, v_w_out, v_ln_g, v_ln_b, v_conv_w, v_conv_b, v_lru_wa, v_lru_ba, v_lru_wx, v_lru_bx, v_lru_lam, v_pool_w, v_pool_scale):
    xi, yi, ci = _my_pos()
    dev = 4 * xi + 2 * yi + ci
    xt, ctxt, tgt = x[0], ctx[0], loss_target[0]
    n_mod = w_mod.shape[2]

    small_shapes = [(D_MODEL,), conv_w.shape[1:], lru_ba.shape[1:], lru_bx.shape[1:], lru_lam.shape[1:],
                    pool_scale.shape[1:]]
    small = _to_rows([c[0], conv_w[0], lru_ba[0], lru_bx[0], lru_lam[0], pool_scale[0]], SUBLANES)
    small_all, wi0 = _all_gather([small, w_in[0].astype(MXU_DTYPE)], "gather_first")
    pieces = [_split_rows(small_all[k], small_shapes) for k in range(N_DEV)]
    c_all = jnp.stack([p[0] for p in pieces])
    conv_w_f = jnp.concatenate([p[1] for p in pieces], axis=-1)
    lru_ba_f = jnp.concatenate([p[2] for p in pieces], axis=-1)[:, None, :]
    lru_bx_f = jnp.concatenate([p[3] for p in pieces], axis=-1)[:, None, :]
    lru_lam_f = jnp.concatenate([p[4] for p in pieces], axis=-1)[:, None, :]
    pool_scale_f = jnp.concatenate([p[5] for p in pieces], axis=-1)[None, :]

    cond = jnp.concatenate([c_all, jnp.broadcast_to(c_ctx[None, :], (N_DEV, D_MODEL))], axis=0)
    b_my = lax.dynamic_slice(b_mod, (0, dev * n_mod), (2, n_mod))[:, None, :]
    mod_part = _mod_fwd(cond, w_mod, b_my, "mod_fwd")
    mod_all, = _all_gather([mod_part], "gather_mod")
    mod = jnp.transpose(mod_all, (1, 2, 0, 3)).reshape(2, 16, 3 * D_MODEL)
    mod_me = lax.dynamic_slice(mod, (0, dev, 0), (2, 1, 3 * D_MODEL))
    sh = [mod_me[i, :, 0:D_MODEL] for i in range(2)]
    sc = [mod_me[i, :, D_MODEL:2 * D_MODEL] for i in range(2)]
    gt = [mod_me[i, :, 2 * D_MODEL:] for i in range(2)]
    shc, scc = mod[0, 8:9, 0:D_MODEL], mod[0, 8:9, D_MODEL:2 * D_MODEL]

    lg = [ln_g[i][None, :] for i in range(2)]
    lb = [ln_b[i][None, :] for i in range(2)]
    lru_p = dict(conv_w=conv_w_f, conv_b=conv_b, wa=lru_wa[0].astype(MXU_DTYPE), wx=lru_wx[0].astype(MXU_DTYPE),
                 ba=lru_ba_f, bx=lru_bx_f, lam=lru_lam_f)
    zero_state = jnp.zeros((1, D_INNER), F32)

    (u0, g0), (wo0,) = _in_proj(xt, sc[0], sh[0], wi0, "in_proj0", sides=[("gather", [w_out[0].astype(MXU_DTYPE)])])
    (uc, _), _ = _in_proj(ctxt, scc, shc, wi0, "in_proj0_ctx")
    (hcf, cf, uvc), _ = _lru_fwd(uc, zero_state, lru_p, 0, "lru_fwd_ctx_f", conv=True)
    (hcb, cbk), _ = _lru_fwd(uvc, zero_state, lru_p, 1, "lru_fwd_ctx_b", conv=False)
    (hf, _, uv0), (wi1,) = _lru_fwd(u0, cf, lru_p, 0, "lru_fwd_f", conv=True,
                                    sides=[("gather", [w_in[1].astype(MXU_DTYPE)])])
    (hb, _), (wo1, pool_w_g) = _lru_fwd(
        uv0, cbk, lru_p, 1, "lru_fwd_b", conv=False,
        sides=[("gather", [w_out[1].astype(MXU_DTYPE), pool_w[0].astype(MXU_DTYPE)])])
    w_in_l = [wi0, wi1]
    w_out_l = [wo0.reshape(D_INNER, D_MODEL), wo1.reshape(D_INNER, D_MODEL)]
    pool_w_f = jnp.transpose(pool_w_g, (1, 0, 2, 3)).reshape(len(POOL_WINDOWS), POOL_GROUP, POOL_GROUP)
    x1, br0 = _out0(hf, hb, g0, xt, gt[0], w_out_l[0], lg[0], lb[0], "out0")
    (u1, g1), _ = _in_proj(x1, sc[1], sh[1], w_in_l[1], "in_proj1", u_dtype=ACT_DTYPE)
    dmix = _pool_mix(u1, False, MXU_DTYPE, "pool_fwd")
    dz1, st1 = _out1(dmix, pool_w_f, pool_scale_f, g1, x1, gt[1], w_out_l[1], lg[1], lb[1], tgt, "out1")
    loss_me = jnp.full((1, LANES), (0.5 / D_MODEL) * jnp.sum(st1[3]), F32)

    core = jnp.reshape(ci, (1,)).astype(jnp.int32)
    wo_view = lambda a: a.reshape(N_DEV, D_INNER // N_DEV, D_MODEL)
    pw_view = lambda a: _blocks_by_device(a, 1).reshape(N_DEV, POOL_GROUP // N_DEV * len(POOL_WINDOWS), POOL_GROUP)
    dd, dg1, gwo1, gpw, gps = _bout1(dz1, dmix, g1, pool_w_f, pool_scale_f, gt[1], w_out_l[1], "bwd_out1")
    du1 = _pool_mix(dd, True, MXU_DTYPE, "pool_bwd")
    (dx1, gwi1, stb1), _ = _bin(du1, dg1, x1, dz1, sc[1], sh[1], w_in_l[1], "bwd_in1")
    bufs1 = [gwi1, wo_view(gwo1), pw_view(gpw)]
    (dz0, dy0, dg0, gwo0, stl0), recv1 = _bout0(dx1, xt, br0, lg[0], hf, hb, g0, gt[0], w_out_l[0], "bwd_out0",
                                                sides=[("sibling", bufs1)])
    pairs1 = [_pair_sum(b, r, core, "reduce_pair_" + n)
              for b, r, n in zip(bufs1, recv1, ["w_in1", "w_out1", "pool_w"])]
    (duvf, gwa_f, gwx_f, gv_f, dh0f), (p_wi1, p_wo1, p_pw, recv_wo0) = _lru_bwd(
        uv0, dy0, hf, cf, zero_state, lru_p, 0, "lru_bwd_f", sides=[("chips", pairs1), ("sibling", [wo_view(gwo0)])])
    pair_wo0 = _pair_sum(wo_view(gwo0), recv_wo0, core, "reduce_pair_w_out0")
    (duvb, gwa_b, gwx_b, gv_b, dh0b), (p_wo0,) = _lru_bwd(
        uv0, dy0, hb, cbk, zero_state, lru_p, 1, "lru_bwd_b", sides=[("chips", [pair_wo0])])
    zero_dh = jnp.zeros(uc.shape, ACT_DTYPE)
    (ducf, gwa_cf, gwx_cf, gv_cf, _), _ = _lru_bwd(uvc, zero_dh, hcf, zero_state, dh0f, lru_p, 0, "lru_bwd_ctx_f")
    (ducb, gwa_cb, gwx_cb, gv_cb, _), _ = _lru_bwd(uvc, zero_dh, hcb, zero_state, dh0b, lru_p, 1, "lru_bwd_ctx_b")

    def pack(sharded, replicated):
        sh_sizes = [int(np.prod(a.shape[1:])) for a in sharded]
        rep_sizes = [a.shape[0] // N_DEV for a in replicated]
        n_flat = sum(sh_sizes) + sum(rep_sizes)
        rows = -(-(-(-n_flat // LANES)) // FLAT_ROWS) * FLAT_ROWS
        buf = jnp.concatenate([a.reshape(N_DEV, -1) for a in sharded + replicated], axis=1)
        return jnp.pad(buf, ((0, 0), (0, rows * LANES - n_flat))).reshape(N_DEV, rows, LANES), sh_sizes, rep_sizes

    def unpack(reduced, sh_sizes, rep_sizes, sh_shapes):
        flat = reduced.reshape(-1)
        offs = np.cumsum([0] + sh_sizes)
        mine = [flat[offs[k]:offs[k + 1]].reshape(s) for k, s in enumerate(sh_shapes)]
        return mine, _to_rows([flat[offs[-1]:offs[-1] + sum(rep_sizes)]], SUBLANES)

    def spread(rep_all, rep_sizes, shapes):
        flat = rep_all.reshape(N_DEV, -1)
        offs = np.cumsum([0] + rep_sizes)
        return [flat[:, offs[k]:offs[k + 1]].reshape(s) for k, s in enumerate(shapes)]

    (du0, cst0), _ = _conv_bwd(duvf, duvb, u0, conv_w_f, "conv_bwd")
    (duc, cstc), _ = _conv_bwd(ducf, ducb, uc, conv_w_f, "conv_bwd_ctx")
    (gwic, stc), _ = _bin(duc, None, ctxt, None, scc, shc, w_in_l[0][:N_WBLK // 2], "bwd_in0_ctx")
    (gx, gwi0, stb0), _ = _bin(du0, dg0, xt, dz0, sc[0], sh[0], w_in_l[0], "bwd_in0", gw_init=gwic)

    zero_row = jnp.zeros((1, D_MODEL), F32)
    dm_me = jnp.stack([
        jnp.concatenate([jnp.concatenate([stb0[1:2], stb0[0:1], stl0[2:3]], axis=1),
                         jnp.concatenate([stc[1:2], stc[0:1], zero_row], axis=1)], axis=0),
        jnp.concatenate([jnp.concatenate([stb1[1:2], stb1[0:1], st1[2:3]], axis=1),
                         jnp.zeros((1, 3 * D_MODEL), F32)], axis=0)])
    dm_g, loss_g = _all_gather([dm_me, loss_me], "gather_dmod")
    loss = jnp.sum(loss_g[:, 0, 0])
    dm_all = jnp.concatenate([jnp.transpose(dm_g[:, :, 0], (1, 0, 2)), jnp.transpose(dm_g[:, :, 1], (1, 0, 2))],
                             axis=1)
    dm_my = lax.dynamic_slice(dm_all, (0, 0, dev * n_mod), (2, 16, n_mod))
    g_w_mod, g_b_mod, gcc_part = _mod_bwd(cond, dm_all, dm_my, w_mod, "mod_bwd")
    g_b_mod = g_b_mod.reshape(b_mod.shape)

    gwa = jnp.stack([gwa_f + gwa_cf, gwa_b + gwa_cb])
    gwx = jnp.stack([gwx_f + gwx_cf, gwx_b + gwx_cb])
    gv = jnp.stack([gv_f + gv_cf, gv_b + gv_cb])
    cst = cst0 + cstc
    misc, m_sh, m_rep = pack(
        [_blocks_by_device(cst[0:4], 1), _blocks_by_device(gv[:, 0], 1), _blocks_by_device(gv[:, 1], 1),
         _blocks_by_device(gv[:, 2], 1), _blocks_by_device(gps[0], 0)],
        [gwa.reshape(-1), gwx.reshape(-1), jnp.stack([stl0[0], st1[0]]).reshape(-1),
         jnp.stack([stl0[1], st1[1]]).reshape(-1), cst[4], gcc_part.reshape(-1)])
    bufs = [gwi0, misc]
    recvs = _sibling_exchange(bufs, "reduce_sibling")
    pairs = [_pair_sum(b, r, core, "reduce_pair_" + n) for b, r, n in zip(bufs, recvs, ["w_in0", "misc"])]
    p_wi0, p_misc = _chip_exchange(pairs, "reduce_chips")
    (g_conv_w, g_lru_ba, g_lru_bx, g_lru_lam, g_pool_scale), rep_mine = unpack(
        _sum4(p_misc, "reduce_sum_misc"), m_sh, m_rep,
        [conv_w.shape, lru_ba.shape, lru_bx.shape, lru_lam.shape, pool_scale.shape])
    rep_all, = _all_gather([rep_mine], "gather_replicated")
    g_lru_wa, g_lru_wx, g_ln_g, g_ln_b, g_conv_b, g_c_ctx = spread(
        rep_all, m_rep, [lru_wa.shape, lru_wx.shape, ln_g.shape, ln_b.shape, conv_b.shape, c_ctx.shape])

    names = ["c_ctx", "w_mod", "b_mod", "w_in", "w_out", "ln_g", "ln_b", "conv_w", "conv_b", "lru_wa", "lru_ba",
             "lru_wx", "lru_bx", "lru_lam", "pool_w", "pool_scale"]
    weights = dict(c_ctx=c_ctx, w_mod=w_mod, b_mod=b_mod, w_in=w_in, w_out=w_out, ln_g=ln_g, ln_b=ln_b,
                   conv_w=conv_w, conv_b=conv_b, lru_wa=lru_wa, lru_ba=lru_ba, lru_wx=lru_wx, lru_bx=lru_bx,
                   lru_lam=lru_lam, pool_w=pool_w, pool_scale=pool_scale)
    mom_m = dict(c_ctx=m_c_ctx, w_mod=m_w_mod, b_mod=m_b_mod, w_in=m_w_in, w_out=m_w_out, ln_g=m_ln_g, ln_b=m_ln_b,
                 conv_w=m_conv_w, conv_b=m_conv_b, lru_wa=m_lru_wa, lru_ba=m_lru_ba, lru_wx=m_lru_wx,
                 lru_bx=m_lru_bx, lru_lam=m_lru_lam, pool_w=m_pool_w, pool_scale=m_pool_scale)
    mom_v = dict(c_ctx=v_c_ctx, w_mod=v_w_mod, b_mod=v_b_mod, w_in=v_w_in, w_out=v_w_out, ln_g=v_ln_g, ln_b=v_ln_b,
                 conv_w=v_conv_w, conv_b=v_conv_b, lru_wa=v_lru_wa, lru_ba=v_lru_ba, lru_wx=v_lru_wx,
                 lru_bx=v_lru_bx, lru_lam=v_lru_lam, pool_w=v_pool_w, pool_scale=v_pool_scale)
    grads = dict(c_ctx=g_c_ctx, w_mod=g_w_mod, b_mod=g_b_mod, ln_g=g_ln_g, ln_b=g_ln_b,
                 conv_w=g_conv_w, conv_b=g_conv_b, lru_wa=g_lru_wa, lru_ba=g_lru_ba, lru_wx=g_lru_wx,
                 lru_bx=g_lru_bx, lru_lam=g_lru_lam)
    grads["pool_scale"] = g_pool_scale
    delta, new_m, new_v = {}, {}, {}

    def update_parts(n, parts, view):
        res = _adamw_parts(weights[n].reshape(view), parts, mom_m[n].reshape(view), mom_v[n].reshape(view),
                           "adamw_" + n)
        grads[n], delta[n], new_m[n], new_v[n] = [r.reshape(weights[n].shape) for r in res]

    update_parts("w_in", [p_wi0, p_wi1], w_in.shape)
    update_parts("w_out", [p_wo0, p_wo1], w_out.shape)
    update_parts("pool_w", [p_pw], (1,) + p_pw.shape[1:])
    for n in ("w_mod", "lru_wa", "lru_wx"):
        shape = weights[n].shape
        view = (int(np.prod(shape[:-1])), shape[-1])
        res = _adamw(weights[n].reshape(view), grads[n].reshape(view), mom_m[n].reshape(view),
                     mom_v[n].reshape(view), "adamw_" + n)
        delta[n], new_m[n], new_v[n] = [r.reshape(shape) for r in res]

    small = [n for n in names if n not in delta]
    shapes = [weights[n].shape for n in small]
    flat = lambda d: _to_rows([d[n] for n in small], FLAT_ROWS)
    res = _adamw(flat(weights), flat(grads), flat(mom_m), flat(mom_v), "adamw_small")
    for d, r in zip((delta, new_m, new_v), res):
        d.update(zip(small, _split_rows(r, shapes)))

    return (loss, gx[None], *[grads[n] for n in names], *[delta[n] for n in names],
            *[new_m[n] for n in names], *[new_v[n] for n in names])
```

```python
import functools

import numpy as np
import jax
import jax.numpy as jnp
from jax import lax
from jax.experimental import pallas as pl
from jax.experimental.pallas import tpu as pltpu

F32 = jnp.float32
BF16 = jnp.bfloat16
MXU_DTYPE = BF16

D_MODEL = 1024
D_INNER = 2048
LRU_BLOCK = 128
GRID_W = 64
POOL_WINDOWS = (2, 4, 8, 16)
POOL_GROUP = 512
ALPHA = float(4 ** 0.25)
LN_EPS = 1e-5
LRU_C = 8.0
N_DEV = 8
N_WBLK = 8
WBLK = 512

ADAM_LR = 0.001
ADAM_B1 = 0.9
ADAM_B2 = 0.999
ADAM_EPS = 1e-08
ADAM_WD = 0.01
ADAM_STEP = 10

LANES = 128
SUBLANES = 8
V7X_VMEM_BYTES = 64 * 1024 * 1024
VMEM_COMPILER_RESERVE = 8 * 1024 * 1024
VMEM_LIMIT = V7X_VMEM_BYTES - VMEM_COMPILER_RESERVE
MESH = pl.DeviceIdType.MESH
ANY = pl.BlockSpec(memory_space=pl.ANY)

TM_MM = 512
TM_LRU = 1024
CB_LRU = 512
N_SEG = 8
SCAN_UNROLL = 4
SCAN_ROW_T = 17
SCAN_ROW_J = 2
SQRT_FLOOR = 1e-30
FLAT_ROWS = 16
ELEMENTWISE_TILE_BYTES = 1 << 20
POOL_TOK = 256
WIRE_DTYPE = BF16
ACT_DTYPE = BF16
H_HALO = 16


def _cparams(**kw):
    return pltpu.CompilerParams(vmem_limit_bytes=VMEM_LIMIT, **kw)


def _my_pos():
    return lax.axis_index("x"), lax.axis_index("y"), lax.axis_index("c")


def _dot(a, b):
    return jnp.dot(a.astype(MXU_DTYPE), b.astype(MXU_DTYPE), preferred_element_type=F32)


def _dot_tn(a, b):
    return lax.dot_general(a.astype(MXU_DTYPE), b.astype(MXU_DTYPE), (((0,), (0,)), ((), ())),
                           preferred_element_type=F32)


def _dot_nt(a, b):
    return lax.dot_general(a.astype(MXU_DTYPE), b.astype(MXU_DTYPE), (((1,), (1,)), ((), ())),
                           preferred_element_type=F32)


def _sigmoid(z):
    return 0.5 * jnp.tanh(0.5 * z) + 0.5


def _log_sigmoid(x):
    y = jnp.exp(-jnp.abs(x))
    u = 1.0 + y
    l1p = jnp.where(u == 1.0, y, jnp.log(u) * (y / jnp.where(u == 1.0, 1.0, u - 1.0)))
    return jnp.minimum(x, 0.0) - l1p


def _rowsum(v):
    return jnp.sum(v, axis=0, keepdims=True)


def _layer_norm_stats(z):
    mu = jnp.mean(z, axis=-1, keepdims=True)
    zc = z - mu
    var = jnp.mean(zc * zc, axis=-1, keepdims=True)
    rstd = lax.rsqrt(var + LN_EPS)
    return zc * rstd, rstd


def _layer_norm_bwd(dy, xhat, rstd, g):
    dxh = dy * g
    m1 = jnp.mean(dxh, axis=-1, keepdims=True)
    m2 = jnp.mean(dxh * xhat, axis=-1, keepdims=True)
    return rstd * (dxh - m1 - xhat * m2)


def _shifted(v, before8, after8, offsets):
    n = v.shape[0]
    ext = jnp.concatenate([before8, v, after8], axis=0)
    total = n + 2 * SUBLANES
    return [pltpu.roll(ext, (-k) % total, 0)[SUBLANES:SUBLANES + n] for k in offsets]


def _rows8(row):
    return jnp.broadcast_to(row, (SUBLANES, row.shape[1]))


def _shift_down(v, first_row):
    return _shifted(v, _rows8(first_row), _rows8(first_row), [-1])[0]


def _shift_up(v, last_row):
    return _shifted(v, _rows8(last_row), _rows8(last_row), [1])[0]


def _all_gather(blocks, name):
    n = len(blocks)

    def body(*refs):
        x_refs, out_refs = refs[:n], refs[n:2 * n]
        send_sems, recv_sems, local_sems = refs[2 * n:]
        x, y, c = _my_pos()
        me, sibling = (x, y, c), (x, y, 1 - c)
        chips = [(1 - x, y), (x, 1 - y), (1 - x, 1 - y)]

        def slot(a, px, py, pc):
            return out_refs[a].at[4 * px + 2 * py + pc]

        def copy(a, k, block, to, src=None):
            return pltpu.make_async_remote_copy(
                src_ref=slot(a, *block) if src is None else src, dst_ref=slot(a, *block),
                send_sem=send_sems.at[a, k], recv_sem=recv_sems.at[a, k], device_id=to, device_id_type=MESH)

        mine = [pltpu.make_async_copy(x_refs[a], slot(a, *me), local_sems.at[a]) for a in range(n)]
        for cp in mine:
            cp.start()
        first = []
        for a in range(n):
            first.append(copy(a, 0, me, sibling, src=x_refs[a]))
            first += [copy(a, 1 + j, me, (*chip, c), src=x_refs[a]) for j, chip in enumerate(chips)]
        for cp in first:
            cp.start()
        passed = []
        for j, chip in enumerate(chips):
            for a in range(n):
                copy(a, 1 + j, (*chip, c), me).wait_recv()
                fwd = copy(a, 4 + j, (*chip, c), sibling)
                fwd.start()
                passed.append(fwd)
        for a in range(n):
            copy(a, 0, sibling, me).wait_recv()
            for j, chip in enumerate(chips):
                copy(a, 4 + j, (*chip, 1 - c), me).wait_recv()
        for cp in first + passed:
            cp.wait_send()
        for cp in mine:
            cp.wait()

    outs = pl.pallas_call(
        body, name=name,
        out_shape=[jax.ShapeDtypeStruct((N_DEV,) + b.shape, b.dtype) for b in blocks],
        in_specs=[ANY] * n, out_specs=[ANY] * n,
        scratch_shapes=[pltpu.SemaphoreType.DMA((n, 7)), pltpu.SemaphoreType.DMA((n, 7)),
                        pltpu.SemaphoreType.DMA((n,))],
    )(*blocks)
    return list(outs)


def _sibling_exchange(bufs, name):
    n = len(bufs)

    def body(*refs):
        srcs, outs = refs[:n], refs[n:2 * n]
        send_sems, recv_sems = refs[2 * n:]
        x, y, c = _my_pos()
        copies = [pltpu.make_async_remote_copy(
            src_ref=srcs[a].at[2 * j + (1 - c)], dst_ref=outs[a].at[j], send_sem=send_sems.at[a, j],
            recv_sem=recv_sems.at[a, j], device_id=(x, y, 1 - c), device_id_type=MESH)
            for a in range(n) for j in range(4)]
        for cp in copies:
            cp.start()
        for cp in copies:
            cp.wait()

    outs = pl.pallas_call(
        body, name=name, out_shape=[jax.ShapeDtypeStruct((4,) + b.shape[1:], b.dtype) for b in bufs],
        in_specs=[ANY] * n, out_specs=[ANY] * n,
        scratch_shapes=[pltpu.SemaphoreType.DMA((n, 4)), pltpu.SemaphoreType.DMA((n, 4))],
    )(*bufs)
    return list(outs)


def _chip_exchange(parts, name):
    n = len(parts)

    def body(*refs):
        srcs, outs = refs[:n], refs[n:2 * n]
        send_sems, recv_sems, local_sems = refs[2 * n:]
        x, y, c = _my_pos()
        jme = 2 * x + y
        peers = [(1 - x, y), (x, 1 - y), (1 - x, 1 - y)]
        local = [pltpu.make_async_copy(srcs[a].at[jme], outs[a].at[jme], local_sems.at[a]) for a in range(n)]
        for cp in local:
            cp.start()

        def copy(a, k, px, py, dst_slot):
            return pltpu.make_async_remote_copy(
                src_ref=srcs[a].at[2 * px + py], dst_ref=outs[a].at[dst_slot], send_sem=send_sems.at[a, k],
                recv_sem=recv_sems.at[a, k], device_id=(px, py, c), device_id_type=MESH)

        sends = [copy(a, k, px, py, jme) for a in range(n) for k, (px, py) in enumerate(peers)]
        for cp in sends:
            cp.start()
        for a in range(n):
            for k, (px, py) in enumerate(peers):
                copy(a, k, px, py, 2 * px + py).wait_recv()
        for cp in sends:
            cp.wait_send()
        for cp in local:
            cp.wait()

    outs = pl.pallas_call(
        body, name=name, out_shape=[jax.ShapeDtypeStruct(p.shape, p.dtype) for p in parts],
        in_specs=[ANY] * n, out_specs=[ANY] * n,
        scratch_shapes=[pltpu.SemaphoreType.DMA((n, 3)), pltpu.SemaphoreType.DMA((n, 3)),
                        pltpu.SemaphoreType.DMA((n,))],
    )(*parts)
    return list(outs)


_SIDE_REMOTE = {"gather": 7, "sibling": 4, "chips": 3}
_FLIPS = [(0, 0, 1), (1, 0, 0), (0, 1, 0), (1, 1, 0), (1, 0, 1), (0, 1, 1), (1, 1, 1)]


def _side_plan(sides):
    inputs, out_shapes, scratch = [], [], []
    for kind, arrays in sides:
        n = len(arrays)
        for a in arrays:
            inputs.append(a)
            shape = {"gather": (N_DEV,) + a.shape, "sibling": (4,) + a.shape[1:], "chips": a.shape}[kind]
            out_shapes.append(jax.ShapeDtypeStruct(shape, a.dtype))
        scratch += [pltpu.SemaphoreType.DMA((n, _SIDE_REMOTE[kind])), pltpu.SemaphoreType.DMA((n, _SIDE_REMOTE[kind])),
                    pltpu.SemaphoreType.DMA((n,))]
    return inputs, out_shapes, scratch


def _side_copies(sides, in_refs, out_refs, sem_refs):
    x, y, c = _my_pos()
    starts, waits = [], []
    pos = 0
    for s, (kind, arrays) in enumerate(sides):
        send_sems, recv_sems, local_sems = sem_refs[3 * s:3 * s + 3]
        for a in range(len(arrays)):
            src, out = in_refs[pos], out_refs[pos]
            pos += 1

            def remote(k, src_ref, dst_ref, to):
                return pltpu.make_async_remote_copy(src_ref=src_ref, dst_ref=dst_ref, send_sem=send_sems.at[a, k],
                                                    recv_sem=recv_sems.at[a, k], device_id=to, device_id_type=MESH)

            def local(src_ref, dst_ref):
                cp = pltpu.make_async_copy(src_ref, dst_ref, local_sems.at[a])
                starts.append(cp.start)
                waits.append(cp.wait)

            if kind == "gather":
                me = 4 * x + 2 * y + c
                local(src, out.at[me])
                for k, (fx, fy, fc) in enumerate(_FLIPS):
                    px, py, pc = (1 - x if fx else x), (1 - y if fy else y), (1 - c if fc else c)
                    send = remote(k, src, out.at[me], (px, py, pc))
                    starts.append(send.start)
                    waits += [remote(k, src, out.at[4 * px + 2 * py + pc], (px, py, pc)).wait_recv, send.wait_send]
            elif kind == "sibling":
                for j in range(4):
                    cp = remote(j, src.at[2 * j + (1 - c)], out.at[j], (x, y, 1 - c))
                    starts.append(cp.start)
                    waits.append(cp.wait)
            else:
                jme = 2 * x + y
                local(src.at[jme], out.at[jme])
                for k, (px, py) in enumerate([(1 - x, y), (x, 1 - y), (1 - x, 1 - y)]):
                    send = remote(k, src.at[2 * px + py], out.at[jme], (px, py, c))
                    starts.append(send.start)
                    waits += [remote(k, src.at[2 * px + py], out.at[2 * px + py], (px, py, c)).wait_recv,
                              send.wait_send]
    return starts, waits


def _call_with_sides(body, sides, *, name, grid, in_specs, out_specs, out_shape, scratch_shapes, compiler_params, args):
    if not sides:
        res = pl.pallas_call(body, name=name, grid=grid, in_specs=in_specs, out_specs=out_specs, out_shape=out_shape,
                             scratch_shapes=scratch_shapes, compiler_params=compiler_params)(*args)
        return list(res), []
    s_in, s_out, s_scr = _side_plan(sides)
    n_in, n_out, n_scr, n_side = len(in_specs), len(out_specs), len(scratch_shapes), len(s_in)

    def wrapped(*refs):
        refs = list(refs)
        ins, side_in = refs[:n_in], refs[n_in:n_in + n_side]
        outs = refs[n_in + n_side:n_in + n_side + n_out]
        side_out = refs[n_in + n_side + n_out:n_in + 2 * n_side + n_out]
        rest = refs[n_in + 2 * n_side + n_out:]
        starts, waits = _side_copies(sides, side_in, side_out, rest[n_scr:])
        first = functools.reduce(jnp.logical_and, [pl.program_id(d) == 0 for d in range(len(grid))])
        last = functools.reduce(jnp.logical_and, [pl.program_id(d) == grid[d] - 1 for d in range(len(grid))])

        @pl.when(first)
        def _():
            for start in starts:
                start()

        body(*ins, *outs, *rest[:n_scr])

        @pl.when(last)
        def _():
            for wait in waits:
                wait()

    res = pl.pallas_call(
        wrapped, name=name, grid=grid, in_specs=list(in_specs) + [ANY] * n_side,
        out_specs=list(out_specs) + [ANY] * n_side, out_shape=list(out_shape) + s_out,
        scratch_shapes=list(scratch_shapes) + s_scr, compiler_params=compiler_params,
    )(*args, *s_in)
    return list(res[:n_out]), list(res[n_out:])


def _row_tile(r, l):
    t = min(r, max(16, ELEMENTWISE_TILE_BYTES // (4 * l) // 16 * 16))
    while r % t:
        t -= 16
    return t


def _pair_sum(buf, recv, core, name):
    _, r, l = buf.shape
    tr = _row_tile(r, l)

    def body(core_ref, a_ref, b_ref, o_ref):
        o_ref[...] = (a_ref[...] + b_ref[...]).astype(WIRE_DTYPE)

    return pl.pallas_call(
        body, name=name, out_shape=jax.ShapeDtypeStruct((4, r, l), WIRE_DTYPE),
        grid_spec=pltpu.PrefetchScalarGridSpec(
            num_scalar_prefetch=1, grid=(4, r // tr),
            in_specs=[pl.BlockSpec((None, tr, l), lambda j, i, cr: (2 * j + cr[0], i, 0)),
                      pl.BlockSpec((None, tr, l), lambda j, i, cr: (j, i, 0))],
            out_specs=pl.BlockSpec((None, tr, l), lambda j, i, cr: (j, i, 0))),
        compiler_params=_cparams(dimension_semantics=("arbitrary", "arbitrary")),
    )(core, buf, recv)


def _sum_parts(p_ref):
    return ((p_ref[0].astype(F32) + p_ref[1].astype(F32)) + (p_ref[2].astype(F32) + p_ref[3].astype(F32)))


def _sum4(parts, name):
    _, r, l = parts.shape
    tr = _row_tile(r, l)

    def body(p_ref, o_ref):
        o_ref[...] = _sum_parts(p_ref)

    return pl.pallas_call(
        body, name=name, out_shape=jax.ShapeDtypeStruct((r, l), F32), grid=(r // tr,),
        in_specs=[pl.BlockSpec((4, tr, l), lambda i: (0, i, 0))],
        out_specs=pl.BlockSpec((tr, l), lambda i: (i, 0)),
        compiler_params=_cparams(dimension_semantics=("arbitrary",)),
    )(parts)


def _adamw_update(w, gg, m, v):
    nm = ADAM_B1 * m + (1.0 - ADAM_B1) * gg
    nv = ADAM_B2 * v + (1.0 - ADAM_B2) * (gg * gg)
    m_hat = nm / (1.0 - ADAM_B1 ** ADAM_STEP)
    v_hat = nv / (1.0 - ADAM_B2 ** ADAM_STEP)
    return -ADAM_LR * (m_hat / (jnp.sqrt(v_hat) + ADAM_EPS) + ADAM_WD * w), nm, nv


def _adamw(w, g, m, v, name):
    r, l = w.shape
    tr = _row_tile(r, l)

    def body(w_ref, g_ref, m_ref, v_ref, d_ref, nm_ref, nv_ref):
        d_ref[...], nm_ref[...], nv_ref[...] = _adamw_update(w_ref[...], g_ref[...], m_ref[...], v_ref[...])

    spec = pl.BlockSpec((tr, l), lambda i: (i, 0))
    return pl.pallas_call(
        body, name=name, out_shape=[jax.ShapeDtypeStruct((r, l), F32)] * 3, grid=(r // tr,),
        in_specs=[spec] * 4, out_specs=[spec] * 3,
        compiler_params=_cparams(dimension_semantics=("arbitrary",)),
    )(w, g, m, v)


def _adamw_parts(w, parts, m, v, name):
    nl, r, l = w.shape
    tr = _row_tile(r, l)

    def body(*refs):
        w_ref, p_refs, (m_ref, v_ref, g_ref, d_ref, nm_ref, nv_ref) = refs[0], refs[1:1 + nl], refs[1 + nl:]
        layer = pl.program_id(0)
        gg = _sum_parts(p_refs[0])
        for q in range(1, nl):
            gg = jnp.where(layer == q, _sum_parts(p_refs[q]), gg)
        g_ref[...] = gg
        d_ref[...], nm_ref[...], nv_ref[...] = _adamw_update(w_ref[...], gg, m_ref[...], v_ref[...])

    spec = pl.BlockSpec((None, tr, l), lambda q, i: (q, i, 0))
    pspecs = [pl.BlockSpec((4, tr, l), lambda q, i, k=k: (0, jnp.where(q == k, i, 0), 0)) for k in range(nl)]
    return pl.pallas_call(
        body, name=name, out_shape=[jax.ShapeDtypeStruct((nl, r, l), F32)] * 4, grid=(nl, r // tr),
        in_specs=[spec] + pspecs + [spec, spec], out_specs=[spec] * 4,
        compiler_params=_cparams(dimension_semantics=("arbitrary", "arbitrary")),
    )(w, *parts, m, v)


def _to_rows(pieces, row_multiple):
    flat = jnp.concatenate([p.reshape(-1) for p in pieces])
    rows = -(-flat.shape[0] // LANES)
    rows = -(-rows // row_multiple) * row_multiple
    flat = jnp.pad(flat, (0, rows * LANES - flat.shape[0]))
    return flat.reshape(rows, LANES)


def _split_rows(rows, shapes):
    flat = rows.reshape(-1)
    out, off = [], 0
    for s in shapes:
        n = int(np.prod(s))
        out.append(flat[off:off + n].reshape(s))
        off += n
    return out


def _mod_fwd(cond, w_mod, b_my, name):
    nl, _, ncol = w_mod.shape

    def body(a_ref, w_ref, b_ref, o_ref):
        a = a_ref[...]
        s = a * _sigmoid(a)
        for i in range(nl):
            o_ref[i] = _dot(s, w_ref[i]) + b_ref[i]

    return pl.pallas_call(
        body, name=name, out_shape=jax.ShapeDtypeStruct((nl, 16, ncol), F32),
        compiler_params=_cparams(),
    )(cond, w_mod, b_my)


def _mod_bwd(cond, dm_all, dm_my, w_mod, name):
    nl, _, ncol = w_mod.shape

    def body(a_ref, dma_ref, dmm_ref, w_ref, gw_ref, gb_ref, gc_ref):
        a = a_ref[...]
        sg = _sigmoid(a)
        s = a * sg
        for i in range(nl):
            gw_ref[i] = _dot_tn(s, dmm_ref[i])
            gb_ref[i] = jnp.sum(dma_ref[i], axis=0, keepdims=True)
        back = _dot_nt(dmm_ref[0], w_ref[0])
        dsilu = sg * (1.0 + a * (1.0 - sg))
        gc_ref[...] = jnp.sum(back[8:16] * dsilu[8:16], axis=0, keepdims=True)

    return pl.pallas_call(
        body, name=name,
        out_shape=[jax.ShapeDtypeStruct((nl, D_MODEL, ncol), F32), jax.ShapeDtypeStruct((nl, 1, 3 * D_MODEL), F32),
                   jax.ShapeDtypeStruct((1, D_MODEL), F32)],
        compiler_params=_cparams(),
    )(cond, dm_all, dm_my, w_mod)


def _in_proj(xt, sc, sh, wg, name, sides=(), u_dtype=F32):
    t = xt.shape[0]
    tm = min(TM_MM, t)

    def body(x_ref, sc_ref, sh_ref, w_ref, u_ref, g_ref):
        h = (x_ref[...] * (1.0 + sc_ref[...]) + sh_ref[...]).astype(MXU_DTYPE)
        for k in range(N_WBLK):
            o = jnp.dot(h, w_ref[k], preferred_element_type=F32)
            if k < N_WBLK // 2:
                u_ref[:, k * WBLK:(k + 1) * WBLK] = o.astype(u_dtype)
            else:
                kk = k - N_WBLK // 2
                g_ref[:, kk * WBLK:(kk + 1) * WBLK] = o.astype(ACT_DTYPE)

    row = pl.BlockSpec((1, D_MODEL), lambda i: (0, 0))
    return _call_with_sides(
        body, sides, name=name,
        out_shape=[jax.ShapeDtypeStruct((t, D_INNER), u_dtype), jax.ShapeDtypeStruct((t, D_INNER), ACT_DTYPE)],
        grid=(t // tm,),
        in_specs=[pl.BlockSpec((tm, D_MODEL), lambda i: (i, 0)), row, row,
                  pl.BlockSpec((N_WBLK, D_MODEL, WBLK), lambda i: (0, 0, 0), pipeline_mode=pl.Buffered(1))],
        out_specs=[pl.BlockSpec((tm, D_INNER), lambda i: (i, 0))] * 2, scratch_shapes=[],
        compiler_params=_cparams(dimension_semantics=("arbitrary",)), args=[xt, sc, sh, wg])


def _halo_maps(nt, tm, n_blocks, pos, rows=SUBLANES):
    per = tm // rows
    prev = lambda cb, i: (jnp.maximum(pos(i) * per - 1, 0), cb)
    nxt = lambda cb, i: (jnp.minimum((pos(i) + 1) * per, n_blocks - 1), cb)
    return prev, nxt


def _conv_taps(u, prev8, next8, is_first, is_last):
    pz = jnp.where(is_first, 0.0, 1.0)
    nz = jnp.where(is_last, 0.0, 1.0)
    return _shifted(u, prev8 * pz, next8 * nz, [-2, -1, 1])


def _lru_gates(uv, wa_ref, wx_ref, ba, bx, cl, g):
    sl = slice(g * LANES, (g + 1) * LANES)
    uvg = uv[:, sl]
    r = _sigmoid(_dot(uvg, wa_ref[g]) + ba[:, sl])
    ii = _sigmoid(_dot(uvg, wx_ref[g]) + bx[:, sl])
    la = cl[:, sl] * r
    a = jnp.exp(la)
    q = jnp.tanh(-la) * (1.0 + a * a)
    rs = lax.rsqrt(jnp.maximum(q, SQRT_FLOOR))
    return uvg, r, ii, a, q * rs, rs


def _scan_rows(seg):
    return -(-(SCAN_ROW_T * (seg - 1) + SCAN_ROW_J * (N_SEG - 1) + 1) // SUBLANES) * SUBLANES


def _seg_chunk(j, c):
    return pl.ds(SCAN_ROW_T * SUBLANES * c + SCAN_ROW_J * j, SUBLANES, stride=SCAN_ROW_T)


def _seg_scatter(ref, g, seg, value):
    for j in range(N_SEG):
        for c in range(seg // SUBLANES):
            r0 = j * seg + SUBLANES * c
            ref[g, _seg_chunk(j, c), :] = value[r0:r0 + SUBLANES]


def _scan_tile(a_s, b_s, carry_ref, write_out, seg, reverse, chunks_per_write=1):
    n_g = a_s.shape[0]
    unroll = SCAN_UNROLL if seg % SCAN_UNROLL == 0 else 1

    n_trips = seg // unroll

    def steps(k, state):
        hs, cs = list(state[0]), list(state[1])
        base = ((n_trips - 1 - k) if reverse else k) * unroll
        for q in (range(unroll - 1, -1, -1) if reverse else range(unroll)):
            t = base + q
            rows = pl.ds(t * SCAN_ROW_T, N_SEG, stride=SCAN_ROW_J)
            for g in range(n_g):
                a = a_s[g, rows, :]
                b = b_s[g, rows, :]
                hs[g] = a * hs[g] + b
                cs[g] = a * cs[g]
                b_s[g, rows, :] = hs[g]
                a_s[g, rows, :] = cs[g]
        return tuple(hs), tuple(cs)

    zeros = tuple(jnp.zeros((N_SEG, LANES), F32) for _ in range(n_g))
    ones = tuple(jnp.ones((N_SEG, LANES), F32) for _ in range(n_g))
    h_fin, a_fin = lax.fori_loop(0, seg // unroll, steps, (zeros, ones))

    order = list(range(N_SEG - 1, -1, -1)) if reverse else list(range(N_SEG))
    for g in range(n_g):
        carry = carry_ref[:, g * LANES:(g + 1) * LANES]
        for j in order:
            for c0 in range(0, seg // SUBLANES, chunks_per_write):
                parts = [b_s[g, _seg_chunk(j, c), :] + a_s[g, _seg_chunk(j, c), :] * carry
                         for c in range(c0, c0 + chunks_per_write)]
                write_out(j, c0, g, parts[0] if chunks_per_write == 1 else jnp.concatenate(parts, axis=0))
            carry = a_fin[g][j:j + 1] * carry + h_fin[g][j:j + 1]
        carry_ref[:, g * LANES:(g + 1) * LANES] = carry


def _lru_specs(s, tm, cb, direction_pos, nt):
    n_rows8 = s // SUBLANES
    prev, nxt = _halo_maps(nt, tm, n_rows8, direction_pos)
    tile = pl.BlockSpec((tm, cb), lambda c, i: (direction_pos(i), c))
    return tile, pl.BlockSpec((SUBLANES, cb), prev), pl.BlockSpec((SUBLANES, cb), nxt)


def _lru_param_specs(cb, d):
    n_g = cb // LANES
    vec = pl.BlockSpec((1, cb), lambda c, i: (0, c))
    dvec = pl.BlockSpec((None, 1, cb), lambda c, i: (d, 0, c))
    wmat = pl.BlockSpec((None, n_g, LRU_BLOCK, LRU_BLOCK), lambda c, i: (d, c, 0, 0))
    return vec, dvec, wmat


def _lru_fwd(src, h0, p, d, name, conv, sides=()):
    s = src.shape[0]
    tm = min(TM_LRU, s)
    cb = CB_LRU
    n_g = cb // LANES
    nt = s // tm
    seg = tm // N_SEG
    pos = (lambda i: i) if d == 0 else (lambda i: nt - 1 - i)

    def body(*refs):
        refs = list(refs)
        u_ref = refs.pop(0)
        if conv:
            up_ref, un_ref, cw_ref, cbias_ref = [refs.pop(0) for _ in range(4)]
        wa_ref, wx_ref, ba_ref, bx_ref, lam_ref, h0_ref, h_ref, hc_ref = [refs.pop(0) for _ in range(8)]
        uv_ref = refs.pop(0) if conv else None
        a_s, b_s = refs
        i = pl.program_id(1)
        tp = pos(i)

        @pl.when(i == 0)
        def _():
            hc_ref[...] = h0_ref[...]

        if conv:
            u_t = u_ref[...]
            um2, um1, up1 = _conv_taps(u_t, up_ref[...], un_ref[...], tp == 0, tp == nt - 1)
            cw = cw_ref[...]
            uv_ref[...] = um2 * cw[0:1] + um1 * cw[1:2] + u_t * cw[2:3] + up1 * cw[3:4] + cbias_ref[...]
        src_ref = uv_ref if conv else u_ref
        cl = LRU_C * _log_sigmoid(lam_ref[...])
        ba, bx = ba_ref[...], bx_ref[...]
        for g in range(n_g):
            uvg, r, ii, a, sq, _ = _lru_gates(src_ref, wa_ref, wx_ref, ba, bx, cl, g)
            b = sq * (ii * uvg)
            _seg_scatter(a_s, g, seg, a)
            _seg_scatter(b_s, g, seg, b)

        per_write = 2 if (seg // SUBLANES) % 2 == 0 else 1

        def write_out(j, c, g, h):
            h_ref[pl.ds(j * seg + SUBLANES * c, SUBLANES * per_write), pl.ds(g * LANES, LANES)] = h.astype(ACT_DTYPE)

        _scan_tile(a_s, b_s, hc_ref, write_out, seg, reverse=(d == 1), chunks_per_write=per_write)

    tile, prev, nxt = _lru_specs(s, tm, cb, pos, nt)
    vec, dvec, wmat = _lru_param_specs(cb, d)
    wide = jax.ShapeDtypeStruct((s, D_INNER), F32)
    conv_specs = [prev, nxt, pl.BlockSpec((4, cb), lambda c, i: (0, c)), vec] if conv else []
    conv_args = [src, src, p["conv_w"], p["conv_b"]] if conv else []
    return _call_with_sides(
        body, sides, name=name,
        out_shape=[jax.ShapeDtypeStruct((s, D_INNER), ACT_DTYPE), jax.ShapeDtypeStruct((1, D_INNER), F32)]
        + ([wide] if conv else []),
        grid=(D_INNER // cb, nt),
        in_specs=[tile] + conv_specs + [wmat, wmat, dvec, dvec, dvec, vec],
        out_specs=[tile, vec] + ([tile] if conv else []),
        scratch_shapes=[pltpu.VMEM((n_g, _scan_rows(seg), LANES), F32)] * 2,
        compiler_params=_cparams(dimension_semantics=("arbitrary", "arbitrary")),
        args=[src, *conv_args, p["wa"], p["wx"], p["ba"], p["bx"], p["lam"], h0])


def _lru_bwd(uv, dh, h, h0, lam_in, p, d, name, sides=()):
    s = uv.shape[0]
    tm = min(TM_LRU, s)
    cb = CB_LRU
    n_g = cb // LANES
    nt = s // tm
    seg = tm // N_SEG
    pos = (lambda i: nt - 1 - i) if d == 0 else (lambda i: i)

    def body(uv_ref, dh_ref, h_ref, hh_ref, wa_ref, wx_ref, ba_ref, bx_ref,
             lam_ref, h0_ref, lin_ref, duv_ref, gwa_ref, gwx_ref, gv_ref, lc_ref, a_s, b_s, lp_s,
             r_s, i_s, q_s, rq_s, a_keep):
        i = pl.program_id(1)
        tp = pos(i)

        @pl.when(i == 0)
        def _():
            lc_ref[...] = lin_ref[...]
            gwa_ref[...] = jnp.zeros_like(gwa_ref)
            gwx_ref[...] = jnp.zeros_like(gwx_ref)
            gv_ref[...] = jnp.zeros_like(gv_ref)

        uv = uv_ref[...]
        lam = lam_ref[...]
        cl = LRU_C * _log_sigmoid(lam)
        ba, bx = ba_ref[...], bx_ref[...]
        dh_t = dh_ref[...].astype(F32)
        carry_in = lc_ref[...]
        for g in range(n_g):
            sl = slice(g * LANES, (g + 1) * LANES)
            _, r, ii, a, sq, rs = _lru_gates(uv, wa_ref, wx_ref, ba, bx, cl, g)
            r_s[:, sl], i_s[:, sl], q_s[:, sl], rq_s[:, sl], a_keep[:, sl] = r, ii, sq, rs, a
            b = a * dh_t[:, sl]
            _seg_scatter(a_s, g, seg, a)
            _seg_scatter(b_s, g, seg, b)

        def write_out(j, c, g, v):
            lp_s[pl.ds(j * seg + SUBLANES * c, SUBLANES), pl.ds(g * LANES, LANES)] = v

        _scan_tile(a_s, b_s, lc_ref, write_out, seg, reverse=(d == 0))

        h_t = h_ref[...].astype(F32)
        hh = hh_ref[...].astype(F32)
        if d == 0:
            edge = jnp.where(tp == 0, h0_ref[...], hh[H_HALO - 1:H_HALO])
            h_prev = _shift_down(h_t, edge)
            lam_t = dh_t + _shift_up(lp_s[...], carry_in)
        else:
            edge = jnp.where(tp == nt - 1, h0_ref[...], hh[0:1])
            h_prev = _shift_up(h_t, edge)
            lam_t = dh_t + _shift_down(lp_s[...], carry_in)

        dsig = LRU_C * _sigmoid(-lam)
        for g in range(n_g):
            sl = slice(g * LANES, (g + 1) * LANES)
            uvg, r, ii, a, sq = uv[:, sl], r_s[:, sl], i_s[:, sl], a_keep[:, sl], q_s[:, sl]
            lt = lam_t[:, sl]
            ls = lt * sq
            dla = (lt * a) * (h_prev[:, sl] - (ii * uvg) * (a * rq_s[:, sl]))
            dzr = (dla * cl[:, sl]) * r * (1.0 - r)
            dzi = (ls * uvg) * ii * (1.0 - ii)
            duv_ref[:, sl] = ls * ii + _dot_nt(dzr, wa_ref[g]) + _dot_nt(dzi, wx_ref[g])
            gwa_ref[g] += _dot_tn(uvg, dzr)
            gwx_ref[g] += _dot_tn(uvg, dzi)
            gv_ref[0:1, sl] += _rowsum(dzr)
            gv_ref[1:2, sl] += _rowsum(dzi)
            gv_ref[2:3, sl] += _rowsum(dla * r) * dsig[:, sl]

    tile, _, _ = _lru_specs(s, tm, cb, pos, nt)
    vec, dvec, wmat = _lru_param_specs(cb, d)
    h_prev_map, h_next_map = _halo_maps(nt, tm, s // H_HALO, pos, rows=H_HALO)
    hh_spec = pl.BlockSpec((H_HALO, cb), h_prev_map if d == 0 else h_next_map)
    gw_spec = pl.BlockSpec((n_g, LRU_BLOCK, LRU_BLOCK), lambda c, i: (c, 0, 0))
    n_blk = D_INNER // LRU_BLOCK
    return _call_with_sides(
        body, sides, name=name,
        out_shape=[jax.ShapeDtypeStruct((s, D_INNER), F32),
                   jax.ShapeDtypeStruct((n_blk, LRU_BLOCK, LRU_BLOCK), F32),
                   jax.ShapeDtypeStruct((n_blk, LRU_BLOCK, LRU_BLOCK), F32),
                   jax.ShapeDtypeStruct((SUBLANES, D_INNER), F32),
                   jax.ShapeDtypeStruct((1, D_INNER), F32)],
        grid=(D_INNER // cb, nt),
        in_specs=[tile, tile, tile, hh_spec, wmat, wmat, dvec, dvec, dvec, vec, vec],
        out_specs=[tile, gw_spec, gw_spec, pl.BlockSpec((SUBLANES, cb), lambda c, i: (0, c)), vec],
        scratch_shapes=[pltpu.VMEM((n_g, _scan_rows(seg), LANES), F32)] * 2 + [pltpu.VMEM((tm, cb), F32)] * 6,
        compiler_params=_cparams(dimension_semantics=("arbitrary", "arbitrary")),
        args=[uv, dh, h, h, p["wa"], p["wx"], p["ba"], p["bx"], p["lam"], h0, lam_in])


def _out0(hf, hb, g, xt, gt, wo, lg, lb, name):
    t = xt.shape[0]
    tm = min(TM_MM, t)

    def body(hf_ref, hb_ref, g_ref, x_ref, gt_ref, w_ref, lg_ref, lb_ref, x1_ref, br_ref):
        br = None
        for k in range(D_INNER // WBLK):
            sl = slice(k * WBLK, (k + 1) * WBLK)
            gg = g_ref[:, sl].astype(F32)
            p = (hf_ref[:, sl].astype(F32) + hb_ref[:, sl].astype(F32)) * (gg * _sigmoid(gg))
            part = _dot(p, w_ref[sl, :])
            br = part if br is None else br + part
        z = ALPHA * x_ref[...] + gt_ref[...] * br
        xhat, _ = _layer_norm_stats(z)
        x1_ref[...] = xhat * lg_ref[...] + lb_ref[...]
        br_ref[...] = br.astype(ACT_DTYPE)

    wide = pl.BlockSpec((tm, D_INNER), lambda i: (i, 0))
    nar = pl.BlockSpec((tm, D_MODEL), lambda i: (i, 0))
    row = pl.BlockSpec((1, D_MODEL), lambda i: (0, 0))
    return pl.pallas_call(
        body, name=name,
        out_shape=[jax.ShapeDtypeStruct((t, D_MODEL), F32), jax.ShapeDtypeStruct((t, D_MODEL), ACT_DTYPE)],
        grid=(t // tm,),
        in_specs=[wide, wide, wide, nar, row,
                  pl.BlockSpec((D_INNER, D_MODEL), lambda i: (0, 0), pipeline_mode=pl.Buffered(1)), row, row],
        out_specs=[nar, nar],
        compiler_params=_cparams(dimension_semantics=("arbitrary",)),
    )(hf, hb, g, xt, gt, wo, lg, lb)


def _unrolled_loop(n, fn, unroll=4):
    while n % unroll:
        unroll //= 2

    def trip(k, carry):
        for q in range(unroll):
            fn(k * unroll + q)
        return carry
    lax.fori_loop(0, n // unroll, trip, 0)


def _window(n, w):
    t = np.arange(n)
    return np.clip(t - w // 2, 0, n), np.clip(t + w // 2, 0, n)


def _pool_tables(n_rows, transpose):
    boxes, inv_c, inv_r = [], [], []
    for w in POOL_WINDOWS:
        lo, hi = _window(GRID_W, w)
        m = np.zeros((GRID_W, GRID_W), np.float32)
        for r in range(GRID_W):
            m[r, lo[r]:hi[r]] = 1.0
        m = np.kron(np.eye(POOL_TOK // GRID_W, dtype=np.float32), m)
        boxes.append(m.T if transpose else m)
        inv_c.append(np.broadcast_to((1.0 / (hi - lo).astype(np.float32))[:, None], (GRID_W, LANES)))
        lo_r, hi_r = _window(n_rows, w)
        inv_r.append(1.0 / (hi_r - lo_r).astype(np.float32))
    return (jnp.asarray(np.stack(boxes), MXU_DTYPE), jnp.asarray(np.stack(inv_c), F32),
            jnp.asarray(np.stack(inv_r), F32))


def _pool_mix(xin, transpose, out_dtype, name):
    s = xin.shape[0]
    n_rows = s // GRID_W
    pad_t = SUBLANES * GRID_W
    rows_per_blk = POOL_TOK // GRID_W
    n_slab = D_INNER // LANES
    slabs_per_group = POOL_GROUP // LANES
    n_win = len(POOL_WINDOWS)
    boxes, inv_c, inv_r = _pool_tables(n_rows, transpose)
    exact_operand = (not transpose) and xin.dtype == MXU_DTYPE and MXU_DTYPE != F32

    def body(invr_ref, box_ref, invc_ref, x_ref, o_ref, pad_s):
        k = pl.program_id(0) // slabs_per_group
        pad_s[pl.ds(0, pad_t), :] = jnp.zeros((pad_t, LANES), F32)
        pad_s[pl.ds(pad_t + s, pad_t), :] = jnp.zeros((pad_t, LANES), F32)

        for kk, w in enumerate(POOL_WINDOWS):
            half = w // 2
            offsets = list(range(-(half - 1), half + 1)) if transpose else list(range(-half, half))

            @pl.when(k == kk)
            def _():
                inv_col = invc_ref[kk]

                def col_box(b):
                    st = pl.multiple_of(b * POOL_TOK, POOL_TOK)
                    xb = x_ref[pl.ds(st, POOL_TOK), :]
                    if exact_operand:
                        pad_s[pl.ds(pad_t + st, POOL_TOK), :] = jnp.dot(box_ref[kk], xb, preferred_element_type=F32)
                        return
                    xb = xb.astype(F32)
                    if transpose:
                        xb = xb * jnp.concatenate(
                            [inv_col * invr_ref[kk, b * rows_per_blk + q] for q in range(rows_per_blk)], axis=0)
                    hi = xb.astype(MXU_DTYPE)
                    lo = (xb - hi.astype(F32)).astype(MXU_DTYPE)
                    both = jnp.dot(box_ref[kk], jnp.concatenate([hi, lo], axis=1), preferred_element_type=F32)
                    pad_s[pl.ds(pad_t + st, POOL_TOK), :] = both[:, :LANES] + both[:, LANES:]
                _unrolled_loop(s // POOL_TOK, col_box)

                def row_box(r):
                    st = pl.multiple_of(r * GRID_W, GRID_W)
                    acc = pad_s[pl.ds(pad_t + st + offsets[0] * GRID_W, GRID_W), :]
                    for o in offsets[1:]:
                        acc = acc + pad_s[pl.ds(pad_t + st + o * GRID_W, GRID_W), :]
                    if not transpose:
                        acc = acc * (inv_col * invr_ref[kk, r])
                    o_ref[pl.ds(st, GRID_W), :] = (acc - x_ref[pl.ds(st, GRID_W), :].astype(F32)).astype(out_dtype)
                _unrolled_loop(n_rows, row_box)

    slab = pl.BlockSpec((s, LANES), lambda i: (0, i))
    return pl.pallas_call(
        body, name=name, out_shape=jax.ShapeDtypeStruct((s, D_INNER), out_dtype), grid=(n_slab,),
        in_specs=[pl.BlockSpec(memory_space=pltpu.SMEM),
                  pl.BlockSpec((n_win, POOL_TOK, POOL_TOK), lambda i: (0, 0, 0)),
                  pl.BlockSpec((n_win, GRID_W, LANES), lambda i: (0, 0, 0)), slab],
        out_specs=slab,
        scratch_shapes=[pltpu.VMEM((s + 2 * pad_t, LANES), F32)],
        compiler_params=_cparams(dimension_semantics=("arbitrary",)),
    )(inv_r, boxes, inv_c, xin)


def _out1(dmix, pw, ps, g, x1, gt, wo, lg, lb, tgt, name):
    t = x1.shape[0]
    tm = min(TM_MM, t)
    n_grp = len(POOL_WINDOWS)

    def body(d_ref, pw_ref, ps_ref, g_ref, x1_ref, gt_ref, w_ref, lg_ref, lb_ref, tgt_ref, dz_ref, st_ref, po_ref):
        @pl.when(pl.program_id(0) == 0)
        def _():
            st_ref[...] = jnp.zeros_like(st_ref)

        br = jnp.zeros((tm, D_MODEL), F32)
        for k in range(n_grp):
            sl = slice(k * POOL_GROUP, (k + 1) * POOL_GROUP)
            po = jnp.dot(d_ref[:, sl], pw_ref[k], preferred_element_type=F32)
            po_ref[:, sl] = po.astype(ACT_DTYPE)
            y = po * ps_ref[:, sl]
            gg = g_ref[:, sl].astype(F32)
            br = br + _dot(y * (gg * _sigmoid(gg)), w_ref[sl, :])
        z = ALPHA * x1_ref[...] + gt_ref[...] * br
        xhat, rstd = _layer_norm_stats(z)
        lg_v = lg_ref[...]
        err = xhat * lg_v + lb_ref[...] - tgt_ref[...]
        dy = err * (1.0 / D_MODEL)
        dz = _layer_norm_bwd(dy, xhat, rstd, lg_v)
        dz_ref[...] = dz
        st_ref[0:1, :] += _rowsum(dy * xhat)
        st_ref[1:2, :] += _rowsum(dy)
        st_ref[2:3, :] += _rowsum(dz * br)
        st_ref[3:4, :] += _rowsum(err * err)

    wide = pl.BlockSpec((tm, D_INNER), lambda i: (i, 0))
    nar = pl.BlockSpec((tm, D_MODEL), lambda i: (i, 0))
    row = pl.BlockSpec((1, D_MODEL), lambda i: (0, 0))
    return pl.pallas_call(
        body, name=name,
        out_shape=[jax.ShapeDtypeStruct((t, D_MODEL), F32), jax.ShapeDtypeStruct((SUBLANES, D_MODEL), F32),
                   jax.ShapeDtypeStruct((t, D_INNER), ACT_DTYPE)],
        grid=(t // tm,),
        in_specs=[wide, pl.BlockSpec((n_grp, POOL_GROUP, POOL_GROUP), lambda i: (0, 0, 0)),
                  pl.BlockSpec((1, D_INNER), lambda i: (0, 0)), wide, nar, row,
                  pl.BlockSpec((D_INNER, D_MODEL), lambda i: (0, 0), pipeline_mode=pl.Buffered(1)), row, row, nar],
        out_specs=[nar, pl.BlockSpec((SUBLANES, D_MODEL), lambda i: (0, 0)), wide],
        compiler_params=_cparams(dimension_semantics=("arbitrary",)),
    )(dmix, pw, ps, g, x1, gt, wo, lg, lb, tgt)


def _flush(acc, out_hbm, sem):
    cp = pltpu.make_async_copy(acc, out_hbm, sem)
    cp.start()
    cp.wait()


def _bout1(dz, dmix, po, g, pw, ps, gt, wo, name):
    t = dz.shape[0]
    tm = min(TM_MM, t)
    nt = t // tm
    n_grp = len(POOL_WINDOWS)

    def body(dz_ref, d_ref, po_ref, g_ref, pw_ref, ps_ref, gt_ref, w_ref, dd_ref, dg_ref, gwo_hbm, gpw_hbm, gps_ref,
             gwo_acc, gpw_acc, sems):
        i = pl.program_id(0)

        @pl.when(i == 0)
        def _():
            gwo_acc[...] = jnp.zeros_like(gwo_acc)
            gpw_acc[...] = jnp.zeros_like(gpw_acc)
            gps_ref[...] = jnp.zeros_like(gps_ref)

        db = (gt_ref[...] * dz_ref[...]).astype(MXU_DTYPE)
        for k in range(n_grp):
            sl = slice(k * POOL_GROUP, (k + 1) * POOL_GROUP)
            dk = d_ref[:, sl]
            po = po_ref[:, sl].astype(F32)
            psk = ps_ref[:, sl]
            y = po * psk
            gg = g_ref[:, sl].astype(F32)
            sg = _sigmoid(gg)
            silu = gg * sg
            gwo_acc[sl, :] += _dot_tn(y * silu, db)
            dp = _dot_nt(db, w_ref[sl, :])
            dy = dp * silu
            dg_ref[:, sl] = (dp * y * (sg * (1.0 + gg * (1.0 - sg)))).astype(MXU_DTYPE)
            gps_ref[0:1, sl] += _rowsum(dy * po)
            dpo = (dy * psk).astype(MXU_DTYPE)
            gpw_acc[k] += _dot_tn(dk, dpo)
            dd_ref[:, sl] = _dot_nt(dpo, pw_ref[k])

        @pl.when(i == nt - 1)
        def _():
            _flush(gwo_acc, gwo_hbm, sems.at[0])
            _flush(gpw_acc, gpw_hbm, sems.at[1])

    wide = pl.BlockSpec((tm, D_INNER), lambda i: (i, 0))
    nar = pl.BlockSpec((tm, D_MODEL), lambda i: (i, 0))
    return pl.pallas_call(
        body, name=name,
        out_shape=[jax.ShapeDtypeStruct((t, D_INNER), F32), jax.ShapeDtypeStruct((t, D_INNER), MXU_DTYPE),
                   jax.ShapeDtypeStruct((D_INNER, D_MODEL), F32),
                   jax.ShapeDtypeStruct((n_grp, POOL_GROUP, POOL_GROUP), F32),
                   jax.ShapeDtypeStruct((SUBLANES, D_INNER), F32)],
        grid=(nt,),
        in_specs=[nar, wide, wide, wide,
                  pl.BlockSpec((n_grp, POOL_GROUP, POOL_GROUP), lambda i: (0, 0, 0), pipeline_mode=pl.Buffered(1)),
                  pl.BlockSpec((1, D_INNER), lambda i: (0, 0)), pl.BlockSpec((1, D_MODEL), lambda i: (0, 0)),
                  pl.BlockSpec((D_INNER, D_MODEL), lambda i: (0, 0), pipeline_mode=pl.Buffered(1))],
        out_specs=[wide, wide, ANY, ANY, pl.BlockSpec((SUBLANES, D_INNER), lambda i: (0, 0))],
        scratch_shapes=[pltpu.VMEM((D_INNER, D_MODEL), F32), pltpu.VMEM((n_grp, POOL_GROUP, POOL_GROUP), F32),
                        pltpu.SemaphoreType.DMA((2,))],
        compiler_params=_cparams(dimension_semantics=("arbitrary",)),
    )(dz, dmix, po, g, pw, ps, gt, wo)


def _bout0(dx1, xt, br0, lg, hf, hb, g, gt, wo, name, sides=()):
    t = dx1.shape[0]
    tm = min(TM_MM, t)
    nt = t // tm

    def body(dx_ref, x_ref, br_ref, lg_ref, hf_ref, hb_ref, g_ref, gt_ref, w_ref,
             dz_ref, dy_ref, dg_ref, gwo_hbm, st_ref, gwo_acc, sem):
        i = pl.program_id(0)

        @pl.when(i == 0)
        def _():
            gwo_acc[...] = jnp.zeros_like(gwo_acc)
            st_ref[...] = jnp.zeros_like(st_ref)

        dx = dx_ref[...]
        br = br_ref[...].astype(F32)
        gate = gt_ref[...]
        xhat, rstd = _layer_norm_stats(ALPHA * x_ref[...] + gate * br)
        dz = _layer_norm_bwd(dx, xhat, rstd, lg_ref[...])
        dz_ref[...] = dz
        st_ref[0:1, :] += _rowsum(dx * xhat)
        st_ref[1:2, :] += _rowsum(dx)
        st_ref[2:3, :] += _rowsum(dz * br)
        db = (gate * dz).astype(MXU_DTYPE)
        for k in range(D_INNER // WBLK):
            sl = slice(k * WBLK, (k + 1) * WBLK)
            y = hf_ref[:, sl].astype(F32) + hb_ref[:, sl].astype(F32)
            gg = g_ref[:, sl].astype(F32)
            sg = _sigmoid(gg)
            silu = gg * sg
            gwo_acc[sl, :] += _dot_tn(y * silu, db)
            dp = _dot_nt(db, w_ref[sl, :])
            dy_ref[:, sl] = (dp * silu).astype(ACT_DTYPE)
            dg_ref[:, sl] = (dp * y * (sg * (1.0 + gg * (1.0 - sg)))).astype(MXU_DTYPE)

        @pl.when(i == nt - 1)
        def _():
            _flush(gwo_acc, gwo_hbm, sem)

    wide = pl.BlockSpec((tm, D_INNER), lambda i: (i, 0))
    nar = pl.BlockSpec((tm, D_MODEL), lambda i: (i, 0))
    row = pl.BlockSpec((1, D_MODEL), lambda i: (0, 0))
    return _call_with_sides(
        body, sides, name=name,
        out_shape=[jax.ShapeDtypeStruct((t, D_MODEL), F32), jax.ShapeDtypeStruct((t, D_INNER), ACT_DTYPE),
                   jax.ShapeDtypeStruct((t, D_INNER), MXU_DTYPE), jax.ShapeDtypeStruct((D_INNER, D_MODEL), F32),
                   jax.ShapeDtypeStruct((SUBLANES, D_MODEL), F32)],
        grid=(nt,),
        in_specs=[nar, nar, nar, row, wide, wide, wide, row,
                  pl.BlockSpec((D_INNER, D_MODEL), lambda i: (0, 0), pipeline_mode=pl.Buffered(1))],
        out_specs=[nar, wide, wide, ANY, pl.BlockSpec((SUBLANES, D_MODEL), lambda i: (0, 0))],
        scratch_shapes=[pltpu.VMEM((D_INNER, D_MODEL), F32), pltpu.SemaphoreType.DMA(())],
        compiler_params=_cparams(dimension_semantics=("arbitrary",)),
        args=[dx1, xt, br0, lg, hf, hb, g, gt, wo])


def _conv_bwd(duvf, duvb, u, conv_w, name, sides=()):
    s = u.shape[0]
    tm = min(TM_LRU, s)
    cb = CB_LRU
    nt = s // tm

    def body(df_ref, dfp_ref, dfn_ref, db_ref, dbp_ref, dbn_ref, u_ref, cw_ref, du_ref, cst_ref):
        i = pl.program_id(1)

        @pl.when(i == 0)
        def _():
            cst_ref[...] = jnp.zeros_like(cst_ref)

        first, last = i == 0, i == nt - 1
        pz = jnp.where(first, 0.0, 1.0)
        nz = jnp.where(last, 0.0, 1.0)
        dout = df_ref[...] + db_ref[...]
        dm1, dp1, dp2 = _shifted(dout, (dfp_ref[...] + dbp_ref[...]) * pz, (dfn_ref[...] + dbn_ref[...]) * nz,
                                 [-1, 1, 2])
        cw = cw_ref[...]
        du_ref[...] = (dp2 * cw[0:1] + dp1 * cw[1:2] + dout * cw[2:3] + dm1 * cw[3:4]).astype(MXU_DTYPE)
        u_t = u_ref[...]
        cst_ref[0:1, :] += _rowsum(dp2 * u_t)
        cst_ref[1:2, :] += _rowsum(dp1 * u_t)
        cst_ref[2:3, :] += _rowsum(dout * u_t)
        cst_ref[3:4, :] += _rowsum(dm1 * u_t)
        cst_ref[4:5, :] += _rowsum(dout)

    tile, prev, nxt = _lru_specs(s, tm, cb, lambda i: i, nt)
    return _call_with_sides(
        body, sides, name=name,
        out_shape=[jax.ShapeDtypeStruct((s, D_INNER), MXU_DTYPE), jax.ShapeDtypeStruct((SUBLANES, D_INNER), F32)],
        grid=(D_INNER // cb, nt),
        in_specs=[tile, prev, nxt] * 2 + [tile, pl.BlockSpec((4, cb), lambda c, i: (0, c))],
        out_specs=[tile, pl.BlockSpec((SUBLANES, cb), lambda c, i: (0, c))], scratch_shapes=[],
        compiler_params=_cparams(dimension_semantics=("arbitrary", "arbitrary")),
        args=[duvf, duvf, duvf, duvb, duvb, duvb, u, conv_w])


def _bin(du, dg, xin, dzin, sc, sh, wg, name, gw_init=None, sides=()):
    t = xin.shape[0]
    tm = min(TM_MM, t)
    nt = t // tm
    has_g, has_dx, has_init = dg is not None, dzin is not None, gw_init is not None
    half = N_WBLK // 2
    n_blk = N_WBLK if has_g else half

    def body(*refs):
        refs = list(refs)
        du_ref = refs.pop(0)
        dg_ref = refs.pop(0) if has_g else None
        x_ref = refs.pop(0)
        dz_ref = refs.pop(0) if has_dx else None
        sc_ref, sh_ref, w_ref = refs.pop(0), refs.pop(0), refs.pop(0)
        init_hbm = refs.pop(0) if has_init else None
        dx_ref = refs.pop(0) if has_dx else None
        gw_hbm, st_ref, gw_acc, sem = refs
        i = pl.program_id(0)

        @pl.when(i == 0)
        def _():
            st_ref[...] = jnp.zeros_like(st_ref)
            first_zero = 0
            if has_init:
                _flush(init_hbm, gw_acc.at[pl.ds(0, half)], sem)
                first_zero = half
            for k in range(first_zero, n_blk):
                gw_acc[k] = jnp.zeros((D_MODEL, WBLK), F32)

        xv = x_ref[...]
        scale = 1.0 + sc_ref[...]
        h = (xv * scale + sh_ref[...]).astype(MXU_DTYPE)
        dh = None
        for k in range(n_blk):
            src = du_ref if k < half else dg_ref
            kk = k % half
            dk = src[:, kk * WBLK:(kk + 1) * WBLK]
            gw_acc[k] += _dot_tn(h, dk)
            contrib = _dot_nt(dk, w_ref[k])
            dh = contrib if dh is None else dh + contrib
        st_ref[0:1, :] += _rowsum(dh * xv)
        st_ref[1:2, :] += _rowsum(dh)
        if has_dx:
            dx_ref[...] = ALPHA * dz_ref[...] + dh * scale

        @pl.when(i == nt - 1)
        def _():
            _flush(gw_acc, gw_hbm, sem)

    wide = pl.BlockSpec((tm, D_INNER), lambda i: (i, 0))
    nar = pl.BlockSpec((tm, D_MODEL), lambda i: (i, 0))
    row = pl.BlockSpec((1, D_MODEL), lambda i: (0, 0))
    wspec = pl.BlockSpec((n_blk, D_MODEL, WBLK), lambda i: (0, 0, 0), pipeline_mode=pl.Buffered(1))
    in_specs = ([wide] + ([wide] if has_g else []) + [nar] + ([nar] if has_dx else []) + [row, row, wspec]
                + ([ANY] if has_init else []))
    args = ([du] + ([dg] if has_g else []) + [xin] + ([dzin] if has_dx else []) + [sc, sh, wg]
            + ([gw_init] if has_init else []))
    out_shape = ([jax.ShapeDtypeStruct((t, D_MODEL), F32)] if has_dx else []) + [
        jax.ShapeDtypeStruct((n_blk, D_MODEL, WBLK), F32), jax.ShapeDtypeStruct((SUBLANES, D_MODEL), F32)]
    out_specs = ([nar] if has_dx else []) + [ANY, pl.BlockSpec((SUBLANES, D_MODEL), lambda i: (0, 0))]
    return _call_with_sides(
        body, sides, name=name, out_shape=out_shape, grid=(nt,), in_specs=in_specs, out_specs=out_specs,
        scratch_shapes=[pltpu.VMEM((n_blk, D_MODEL, WBLK), F32), pltpu.SemaphoreType.DMA(())],
        compiler_params=_cparams(dimension_semantics=("arbitrary",)), args=args)


def _blocks_by_device(a, axis):
    shape = a.shape
    a = a.reshape(shape[:axis] + (N_DEV, shape[axis] // N_DEV) + shape[axis + 1:])
    return jnp.moveaxis(a, axis, 0)


def kernel(x, c, ctx, c_ctx, w_mod, b_mod, w_in, w_out, ln_g, ln_b, conv_w, conv_b, lru_wa, lru_ba, lru_wx, lru_bx, lru_lam, pool_w, pool_scale, loss_target, m_c_ctx, m_w_mod, m_b_mod, m_w_in, m_w_out, m_ln_g, m_ln_b, m_conv_w, m_conv_b, m_lru_wa, m_lru_ba, m_lru_wx, m_lru_bx, m_lru_lam, m_pool_w, m_pool_scale, v_c_ctx, v_w_mod, v_b_mod, v_w_in, v_w_out, v_ln_g, v_ln_b, v_conv_w, v_conv_b, v_lru_wa, v_lru_ba, v_lru_wx, v_lru_bx, v_lru_lam, v_pool_w, v_pool_scale):
    xi, yi, ci = _my_pos()
    dev = 4 * xi + 2 * yi + ci
    xt, ctxt, tgt = x[0], ctx[0], loss_target[0]
    n_mod = w_mod.shape[2]

    small_shapes = [(D_MODEL,), conv_w.shape[1:], lru_ba.shape[1:], lru_bx.shape[1:], lru_lam.shape[1:],
                    pool_scale.shape[1:]]
    small = _to_rows([c[0], conv_w[0], lru_ba[0], lru_bx[0], lru_lam[0], pool_scale[0]], SUBLANES)
    small_all, wi0 = _all_gather([small, w_in[0].astype(MXU_DTYPE)], "gather_first")
    pieces = [_split_rows(small_all[k], small_shapes) for k in range(N_DEV)]
    c_all = jnp.stack([p[0] for p in pieces])
    conv_w_f = jnp.concatenate([p[1] for p in pieces], axis=-1)
    lru_ba_f = jnp.concatenate([p[2] for p in pieces], axis=-1)[:, None, :]
    lru_bx_f = jnp.concatenate([p[3] for p in pieces], axis=-1)[:, None, :]
    lru_lam_f = jnp.concatenate([p[4] for p in pieces], axis=-1)[:, None, :]
    pool_scale_f = jnp.concatenate([p[5] for p in pieces], axis=-1)[None, :]

    cond = jnp.concatenate([c_all, jnp.broadcast_to(c_ctx[None, :], (N_DEV, D_MODEL))], axis=0)
    b_my = lax.dynamic_slice(b_mod, (0, dev * n_mod), (2, n_mod))[:, None, :]
    mod_part = _mod_fwd(cond, w_mod, b_my, "mod_fwd")
    mod_all, = _all_gather([mod_part], "gather_mod")
    mod = jnp.transpose(mod_all, (1, 2, 0, 3)).reshape(2, 16, 3 * D_MODEL)
    mod_me = lax.dynamic_slice(mod, (0, dev, 0), (2, 1, 3 * D_MODEL))
    sh = [mod_me[i, :, 0:D_MODEL] for i in range(2)]
    sc = [mod_me[i, :, D_MODEL:2 * D_MODEL] for i in range(2)]
    gt = [mod_me[i, :, 2 * D_MODEL:] for i in range(2)]
    shc, scc = mod[0, 8:9, 0:D_MODEL], mod[0, 8:9, D_MODEL:2 * D_MODEL]

    lg = [ln_g[i][None, :] for i in range(2)]
    lb = [ln_b[i][None, :] for i in range(2)]
    lru_p = dict(conv_w=conv_w_f, conv_b=conv_b, wa=lru_wa[0].astype(MXU_DTYPE), wx=lru_wx[0].astype(MXU_DTYPE),
                 ba=lru_ba_f, bx=lru_bx_f, lam=lru_lam_f)
    zero_state = jnp.zeros((1, D_INNER), F32)

    (u0, g0), (wo0,) = _in_proj(xt, sc[0], sh[0], wi0, "in_proj0", sides=[("gather", [w_out[0].astype(MXU_DTYPE)])])
    (uc, _), _ = _in_proj(ctxt, scc, shc, wi0, "in_proj0_ctx")
    (hcf, cf, uvc), _ = _lru_fwd(uc, zero_state, lru_p, 0, "lru_fwd_ctx_f", conv=True)
    (hcb, cbk), _ = _lru_fwd(uvc, zero_state, lru_p, 1, "lru_fwd_ctx_b", conv=False)
    (hf, _, uv0), (wi1,) = _lru_fwd(u0, cf, lru_p, 0, "lru_fwd_f", conv=True,
                                    sides=[("gather", [w_in[1].astype(MXU_DTYPE)])])
    (hb, _), (wo1, pool_w_g) = _lru_fwd(
        uv0, cbk, lru_p, 1, "lru_fwd_b", conv=False,
        sides=[("gather", [w_out[1].astype(MXU_DTYPE), pool_w[0].astype(MXU_DTYPE)])])
    w_in_l = [wi0, wi1]
    w_out_l = [wo0.reshape(D_INNER, D_MODEL), wo1.reshape(D_INNER, D_MODEL)]
    pool_w_f = jnp.transpose(pool_w_g, (1, 0, 2, 3)).reshape(len(POOL_WINDOWS), POOL_GROUP, POOL_GROUP)
    x1, br0 = _out0(hf, hb, g0, xt, gt[0], w_out_l[0], lg[0], lb[0], "out0")
    (u1, g1), _ = _in_proj(x1, sc[1], sh[1], w_in_l[1], "in_proj1", u_dtype=ACT_DTYPE)
    dmix = _pool_mix(u1, False, MXU_DTYPE, "pool_fwd")
    dz1, st1, po1 = _out1(dmix, pool_w_f, pool_scale_f, g1, x1, gt[1], w_out_l[1], lg[1], lb[1], tgt, "out1")
    loss_me = jnp.full((1, LANES), (0.5 / D_MODEL) * jnp.sum(st1[3]), F32)

    core = jnp.reshape(ci, (1,)).astype(jnp.int32)
    wo_view = lambda a: a.reshape(N_DEV, D_INNER // N_DEV, D_MODEL)
    pw_view = lambda a: _blocks_by_device(a, 1).reshape(N_DEV, POOL_GROUP // N_DEV * len(POOL_WINDOWS), POOL_GROUP)
    dd, dg1, gwo1, gpw, gps = _bout1(dz1, dmix, po1, g1, pool_w_f, pool_scale_f, gt[1], w_out_l[1], "bwd_out1")
    du1 = _pool_mix(dd, True, MXU_DTYPE, "pool_bwd")
    (dx1, gwi1, stb1), _ = _bin(du1, dg1, x1, dz1, sc[1], sh[1], w_in_l[1], "bwd_in1")
    bufs1 = [gwi1, wo_view(gwo1), pw_view(gpw)]
    (dz0, dy0, dg0, gwo0, stl0), recv1 = _bout0(dx1, xt, br0, lg[0], hf, hb, g0, gt[0], w_out_l[0], "bwd_out0",
                                                sides=[("sibling", bufs1)])
    pairs1 = [_pair_sum(b, r, core, "reduce_pair_" + n)
              for b, r, n in zip(bufs1, recv1, ["w_in1", "w_out1", "pool_w"])]
    (duvf, gwa_f, gwx_f, gv_f, dh0f), (p_wi1, p_wo1, p_pw, recv_wo0) = _lru_bwd(
        uv0, dy0, hf, cf, zero_state, lru_p, 0, "lru_bwd_f", sides=[("chips", pairs1), ("sibling", [wo_view(gwo0)])])
    pair_wo0 = _pair_sum(wo_view(gwo0), recv_wo0, core, "reduce_pair_w_out0")
    (duvb, gwa_b, gwx_b, gv_b, dh0b), (p_wo0,) = _lru_bwd(
        uv0, dy0, hb, cbk, zero_state, lru_p, 1, "lru_bwd_b", sides=[("chips", [pair_wo0])])
    zero_dh = jnp.zeros(uc.shape, ACT_DTYPE)
    (ducf, gwa_cf, gwx_cf, gv_cf, _), _ = _lru_bwd(uvc, zero_dh, hcf, zero_state, dh0f, lru_p, 0, "lru_bwd_ctx_f")
    (ducb, gwa_cb, gwx_cb, gv_cb, _), _ = _lru_bwd(uvc, zero_dh, hcb, zero_state, dh0b, lru_p, 1, "lru_bwd_ctx_b")

    def pack(sharded, replicated):
        sh_sizes = [int(np.prod(a.shape[1:])) for a in sharded]
        rep_sizes = [a.shape[0] // N_DEV for a in replicated]
        n_flat = sum(sh_sizes) + sum(rep_sizes)
        rows = -(-(-(-n_flat // LANES)) // FLAT_ROWS) * FLAT_ROWS
        buf = jnp.concatenate([a.reshape(N_DEV, -1) for a in sharded + replicated], axis=1)
        return jnp.pad(buf, ((0, 0), (0, rows * LANES - n_flat))).reshape(N_DEV, rows, LANES), sh_sizes, rep_sizes

    def unpack(reduced, sh_sizes, rep_sizes, sh_shapes):
        flat = reduced.reshape(-1)
        offs = np.cumsum([0] + sh_sizes)
        mine = [flat[offs[k]:offs[k + 1]].reshape(s) for k, s in enumerate(sh_shapes)]
        return mine, _to_rows([flat[offs[-1]:offs[-1] + sum(rep_sizes)]], SUBLANES)

    def spread(rep_all, rep_sizes, shapes):
        flat = rep_all.reshape(N_DEV, -1)
        offs = np.cumsum([0] + rep_sizes)
        return [flat[:, offs[k]:offs[k + 1]].reshape(s) for k, s in enumerate(shapes)]

    (du0, cst0), _ = _conv_bwd(duvf, duvb, u0, conv_w_f, "conv_bwd")
    (duc, cstc), _ = _conv_bwd(ducf, ducb, uc, conv_w_f, "conv_bwd_ctx")
    (gwic, stc), _ = _bin(duc, None, ctxt, None, scc, shc, w_in_l[0][:N_WBLK // 2], "bwd_in0_ctx")
    (gx, gwi0, stb0), _ = _bin(du0, dg0, xt, dz0, sc[0], sh[0], w_in_l[0], "bwd_in0", gw_init=gwic)

    zero_row = jnp.zeros((1, D_MODEL), F32)
    dm_me = jnp.stack([
        jnp.concatenate([jnp.concatenate([stb0[1:2], stb0[0:1], stl0[2:3]], axis=1),
                         jnp.concatenate([stc[1:2], stc[0:1], zero_row], axis=1)], axis=0),
        jnp.concatenate([jnp.concatenate([stb1[1:2], stb1[0:1], st1[2:3]], axis=1),
                         jnp.zeros((1, 3 * D_MODEL), F32)], axis=0)])
    dm_g, loss_g = _all_gather([dm_me, loss_me], "gather_dmod")
    loss = jnp.sum(loss_g[:, 0, 0])
    dm_all = jnp.concatenate([jnp.transpose(dm_g[:, :, 0], (1, 0, 2)), jnp.transpose(dm_g[:, :, 1], (1, 0, 2))],
                             axis=1)
    dm_my = lax.dynamic_slice(dm_all, (0, 0, dev * n_mod), (2, 16, n_mod))
    g_w_mod, g_b_mod, gcc_part = _mod_bwd(cond, dm_all, dm_my, w_mod, "mod_bwd")
    g_b_mod = g_b_mod.reshape(b_mod.shape)

    gwa = jnp.stack([gwa_f + gwa_cf, gwa_b + gwa_cb])
    gwx = jnp.stack([gwx_f + gwx_cf, gwx_b + gwx_cb])
    gv = jnp.stack([gv_f + gv_cf, gv_b + gv_cb])
    cst = cst0 + cstc
    misc, m_sh, m_rep = pack(
        [_blocks_by_device(cst[0:4], 1), _blocks_by_device(gv[:, 0], 1), _blocks_by_device(gv[:, 1], 1),
         _blocks_by_device(gv[:, 2], 1), _blocks_by_device(gps[0], 0)],
        [gwa.reshape(-1), gwx.reshape(-1), jnp.stack([stl0[0], st1[0]]).reshape(-1),
         jnp.stack([stl0[1], st1[1]]).reshape(-1), cst[4], gcc_part.reshape(-1)])
    bufs = [gwi0, misc]
    recvs = _sibling_exchange(bufs, "reduce_sibling")
    pairs = [_pair_sum(b, r, core, "reduce_pair_" + n) for b, r, n in zip(bufs, recvs, ["w_in0", "misc"])]
    p_wi0, p_misc = _chip_exchange(pairs, "reduce_chips")
    (g_conv_w, g_lru_ba, g_lru_bx, g_lru_lam, g_pool_scale), rep_mine = unpack(
        _sum4(p_misc, "reduce_sum_misc"), m_sh, m_rep,
        [conv_w.shape, lru_ba.shape, lru_bx.shape, lru_lam.shape, pool_scale.shape])
    rep_all, = _all_gather([rep_mine], "gather_replicated")
    g_lru_wa, g_lru_wx, g_ln_g, g_ln_b, g_conv_b, g_c_ctx = spread(
        rep_all, m_rep, [lru_wa.shape, lru_wx.shape, ln_g.shape, ln_b.shape, conv_b.shape, c_ctx.shape])

    names = ["c_ctx", "w_mod", "b_mod", "w_in", "w_out", "ln_g", "ln_b", "conv_w", "conv_b", "lru_wa", "lru_ba",
             "lru_wx", "lru_bx", "lru_lam", "pool_w", "pool_scale"]
    weights = dict(c_ctx=c_ctx, w_mod=w_mod, b_mod=b_mod, w_in=w_in, w_out=w_out, ln_g=ln_g, ln_b=ln_b,
                   conv_w=conv_w, conv_b=conv_b, lru_wa=lru_wa, lru_ba=lru_ba, lru_wx=lru_wx, lru_bx=lru_bx,
                   lru_lam=lru_lam, pool_w=pool_w, pool_scale=pool_scale)
    mom_m = dict(c_ctx=m_c_ctx, w_mod=m_w_mod, b_mod=m_b_mod, w_in=m_w_in, w_out=m_w_out, ln_g=m_ln_g, ln_b=m_ln_b,
                 conv_w=m_conv_w, conv_b=m_conv_b, lru_wa=m_lru_wa, lru_ba=m_lru_ba, lru_wx=m_lru_wx,
                 lru_bx=m_lru_bx, lru_lam=m_lru_lam, pool_w=m_pool_w, pool_scale=m_pool_scale)
    mom_v = dict(c_ctx=v_c_ctx, w_mod=v_w_mod, b_mod=v_b_mod, w_in=v_w_in, w_out=v_w_out, ln_g=v_ln_g, ln_b=v_ln_b,
                 conv_w=v_conv_w, conv_b=v_conv_b, lru_wa=v_lru_wa, lru_ba=v_lru_ba, lru_wx=v_lru_wx,
                 lru_bx=v_lru_bx, lru_lam=v_lru_lam, pool_w=v_pool_w, pool_scale=v_pool_scale)
    grads = dict(c_ctx=g_c_ctx, w_mod=g_w_mod, b_mod=g_b_mod, ln_g=g_ln_g, ln_b=g_ln_b,
                 conv_w=g_conv_w, conv_b=g_conv_b, lru_wa=g_lru_wa, lru_ba=g_lru_ba, lru_wx=g_lru_wx,
                 lru_bx=g_lru_bx, lru_lam=g_lru_lam)
    grads["pool_scale"] = g_pool_scale
    delta, new_m, new_v = {}, {}, {}

    def update_parts(n, parts, view):
        res = _adamw_parts(weights[n].reshape(view), parts, mom_m[n].reshape(view), mom_v[n].reshape(view),
                           "adamw_" + n)
        grads[n], delta[n], new_m[n], new_v[n] = [r.reshape(weights[n].shape) for r in res]

    update_parts("w_in", [p_wi0, p_wi1], w_in.shape)
    update_parts("w_out", [p_wo0, p_wo1], w_out.shape)
    update_parts("pool_w", [p_pw], (1,) + p_pw.shape[1:])
    for n in ("w_mod", "lru_wa", "lru_wx"):
        shape = weights[n].shape
        view = (int(np.prod(shape[:-1])), shape[-1])
        res = _adamw(weights[n].reshape(view), grads[n].reshape(view), mom_m[n].reshape(view),
                     mom_v[n].reshape(view), "adamw_" + n)
        delta[n], new_m[n], new_v[n] = [r.reshape(shape) for r in res]

    small = [n for n in names if n not in delta]
    shapes = [weights[n].shape for n in small]
    flat = lambda d: _to_rows([d[n] for n in small], FLAT_ROWS)
    res = _adamw(flat(weights), flat(grads), flat(mom_m), flat(mom_v), "adamw_small")
    for d, r in zip((delta, new_m, new_v), res):
        d.update(zip(small, _split_rows(r, shapes)))

    return (loss, gx[None], *[grads[n] for n in names], *[delta[n] for n in names],
            *[new_m[n] for n in names], *[new_v[n] for n in names])
```

```python
import functools

import numpy as np
import jax
import jax.numpy as jnp
from jax import lax
from jax.experimental import pallas as pl
from jax.experimental.pallas import tpu as pltpu

F32 = jnp.float32
BF16 = jnp.bfloat16
MXU_DTYPE = BF16

D_MODEL = 1024
D_INNER = 2048
LRU_BLOCK = 128
GRID_W = 64
POOL_WINDOWS = (2, 4, 8, 16)
POOL_GROUP = 512
ALPHA = float(4 ** 0.25)
LN_EPS = 1e-5
LRU_C = 8.0
N_DEV = 8
N_WBLK = 8
WBLK = 512

ADAM_LR = 0.001
ADAM_B1 = 0.9
ADAM_B2 = 0.999
ADAM_EPS = 1e-08
ADAM_WD = 0.01
ADAM_STEP = 10

LANES = 128
SUBLANES = 8
V7X_VMEM_BYTES = 64 * 1024 * 1024
VMEM_COMPILER_RESERVE = 8 * 1024 * 1024
VMEM_LIMIT = V7X_VMEM_BYTES - VMEM_COMPILER_RESERVE
MESH = pl.DeviceIdType.MESH
ANY = pl.BlockSpec(memory_space=pl.ANY)

TM_MM = 512
TM_LRU = 1024
CB_LRU = 512
N_SEG = 8
SCAN_UNROLL = 4
SCAN_ROW_T = 17
SCAN_ROW_J = 2
SQRT_FLOOR = 1e-30
FLAT_ROWS = 16
ELEMENTWISE_TILE_BYTES = 1 << 20
POOL_TOK = 256
WIRE_DTYPE = BF16
ACT_DTYPE = BF16
H_HALO = 16


def _cparams(**kw):
    return pltpu.CompilerParams(vmem_limit_bytes=VMEM_LIMIT, **kw)


def _my_pos():
    return lax.axis_index("x"), lax.axis_index("y"), lax.axis_index("c")


def _dot(a, b):
    return jnp.dot(a.astype(MXU_DTYPE), b.astype(MXU_DTYPE), preferred_element_type=F32)


def _dot_tn(a, b):
    return lax.dot_general(a.astype(MXU_DTYPE), b.astype(MXU_DTYPE), (((0,), (0,)), ((), ())),
                           preferred_element_type=F32)


def _dot_nt(a, b):
    return lax.dot_general(a.astype(MXU_DTYPE), b.astype(MXU_DTYPE), (((1,), (1,)), ((), ())),
                           preferred_element_type=F32)


def _sigmoid(z):
    return 0.5 * jnp.tanh(0.5 * z) + 0.5


def _log_sigmoid(x):
    y = jnp.exp(-jnp.abs(x))
    u = 1.0 + y
    l1p = jnp.where(u == 1.0, y, jnp.log(u) * (y / jnp.where(u == 1.0, 1.0, u - 1.0)))
    return jnp.minimum(x, 0.0) - l1p


def _rowsum(v):
    return jnp.sum(v, axis=0, keepdims=True)


def _layer_norm_stats(z):
    mu = jnp.mean(z, axis=-1, keepdims=True)
    zc = z - mu
    var = jnp.mean(zc * zc, axis=-1, keepdims=True)
    rstd = lax.rsqrt(var + LN_EPS)
    return zc * rstd, rstd


def _layer_norm_bwd(dy, xhat, rstd, g):
    dxh = dy * g
    m1 = jnp.mean(dxh, axis=-1, keepdims=True)
    m2 = jnp.mean(dxh * xhat, axis=-1, keepdims=True)
    return rstd * (dxh - m1 - xhat * m2)


def _shifted(v, before8, after8, offsets):
    n = v.shape[0]
    ext = jnp.concatenate([before8, v, after8], axis=0)
    total = n + 2 * SUBLANES
    return [pltpu.roll(ext, (-k) % total, 0)[SUBLANES:SUBLANES + n] for k in offsets]


def _rows8(row):
    return jnp.broadcast_to(row, (SUBLANES, row.shape[1]))


def _shift_down(v, first_row):
    return _shifted(v, _rows8(first_row), _rows8(first_row), [-1])[0]


def _shift_up(v, last_row):
    return _shifted(v, _rows8(last_row), _rows8(last_row), [1])[0]


def _all_gather(blocks, name):
    n = len(blocks)

    def body(*refs):
        x_refs, out_refs = refs[:n], refs[n:2 * n]
        send_sems, recv_sems, local_sems = refs[2 * n:]
        x, y, c = _my_pos()
        me, sibling = (x, y, c), (x, y, 1 - c)
        chips = [(1 - x, y), (x, 1 - y), (1 - x, 1 - y)]

        def slot(a, px, py, pc):
            return out_refs[a].at[4 * px + 2 * py + pc]

        def copy(a, k, block, to, src=None):
            return pltpu.make_async_remote_copy(
                src_ref=slot(a, *block) if src is None else src, dst_ref=slot(a, *block),
                send_sem=send_sems.at[a, k], recv_sem=recv_sems.at[a, k], device_id=to, device_id_type=MESH)

        mine = [pltpu.make_async_copy(x_refs[a], slot(a, *me), local_sems.at[a]) for a in range(n)]
        for cp in mine:
            cp.start()
        first = []
        for a in range(n):
            first.append(copy(a, 0, me, sibling, src=x_refs[a]))
            first += [copy(a, 1 + j, me, (*chip, c), src=x_refs[a]) for j, chip in enumerate(chips)]
        for cp in first:
            cp.start()
        passed = []
        for j, chip in enumerate(chips):
            for a in range(n):
                copy(a, 1 + j, (*chip, c), me).wait_recv()
                fwd = copy(a, 4 + j, (*chip, c), sibling)
                fwd.start()
                passed.append(fwd)
        for a in range(n):
            copy(a, 0, sibling, me).wait_recv()
            for j, chip in enumerate(chips):
                copy(a, 4 + j, (*chip, 1 - c), me).wait_recv()
        for cp in first + passed:
            cp.wait_send()
        for cp in mine:
            cp.wait()

    outs = pl.pallas_call(
        body, name=name,
        out_shape=[jax.ShapeDtypeStruct((N_DEV,) + b.shape, b.dtype) for b in blocks],
        in_specs=[ANY] * n, out_specs=[ANY] * n,
        scratch_shapes=[pltpu.SemaphoreType.DMA((n, 7)), pltpu.SemaphoreType.DMA((n, 7)),
                        pltpu.SemaphoreType.DMA((n,))],
    )(*blocks)
    return list(outs)


def _sibling_exchange(bufs, name):
    n = len(bufs)

    def body(*refs):
        srcs, outs = refs[:n], refs[n:2 * n]
        send_sems, recv_sems = refs[2 * n:]
        x, y, c = _my_pos()
        copies = [pltpu.make_async_remote_copy(
            src_ref=srcs[a].at[2 * j + (1 - c)], dst_ref=outs[a].at[j], send_sem=send_sems.at[a, j],
            recv_sem=recv_sems.at[a, j], device_id=(x, y, 1 - c), device_id_type=MESH)
            for a in range(n) for j in range(4)]
        for cp in copies:
            cp.start()
        for cp in copies:
            cp.wait()

    outs = pl.pallas_call(
        body, name=name, out_shape=[jax.ShapeDtypeStruct((4,) + b.shape[1:], b.dtype) for b in bufs],
        in_specs=[ANY] * n, out_specs=[ANY] * n,
        scratch_shapes=[pltpu.SemaphoreType.DMA((n, 4)), pltpu.SemaphoreType.DMA((n, 4))],
    )(*bufs)
    return list(outs)


def _chip_exchange(parts, name):
    n = len(parts)

    def body(*refs):
        srcs, outs = refs[:n], refs[n:2 * n]
        send_sems, recv_sems, local_sems = refs[2 * n:]
        x, y, c = _my_pos()
        jme = 2 * x + y
        peers = [(1 - x, y), (x, 1 - y), (1 - x, 1 - y)]
        local = [pltpu.make_async_copy(srcs[a].at[jme], outs[a].at[jme], local_sems.at[a]) for a in range(n)]
        for cp in local:
            cp.start()

        def copy(a, k, px, py, dst_slot):
            return pltpu.make_async_remote_copy(
                src_ref=srcs[a].at[2 * px + py], dst_ref=outs[a].at[dst_slot], send_sem=send_sems.at[a, k],
                recv_sem=recv_sems.at[a, k], device_id=(px, py, c), device_id_type=MESH)

        sends = [copy(a, k, px, py, jme) for a in range(n) for k, (px, py) in enumerate(peers)]
        for cp in sends:
            cp.start()
        for a in range(n):
            for k, (px, py) in enumerate(peers):
                copy(a, k, px, py, 2 * px + py).wait_recv()
        for cp in sends:
            cp.wait_send()
        for cp in local:
            cp.wait()

    outs = pl.pallas_call(
        body, name=name, out_shape=[jax.ShapeDtypeStruct(p.shape, p.dtype) for p in parts],
        in_specs=[ANY] * n, out_specs=[ANY] * n,
        scratch_shapes=[pltpu.SemaphoreType.DMA((n, 3)), pltpu.SemaphoreType.DMA((n, 3)),
                        pltpu.SemaphoreType.DMA((n,))],
    )(*parts)
    return list(outs)


_SIDE_REMOTE = {"gather": 7, "sibling": 4, "chips": 3}
_FLIPS = [(0, 0, 1), (1, 0, 0), (0, 1, 0), (1, 1, 0), (1, 0, 1), (0, 1, 1), (1, 1, 1)]


def _side_plan(sides):
    inputs, out_shapes, scratch = [], [], []
    for kind, arrays in sides:
        n = len(arrays)
        for a in arrays:
            inputs.append(a)
            shape = {"gather": (N_DEV,) + a.shape, "sibling": (4,) + a.shape[1:], "chips": a.shape}[kind]
            out_shapes.append(jax.ShapeDtypeStruct(shape, a.dtype))
        scratch += [pltpu.SemaphoreType.DMA((n, _SIDE_REMOTE[kind])), pltpu.SemaphoreType.DMA((n, _SIDE_REMOTE[kind])),
                    pltpu.SemaphoreType.DMA((n,))]
    return inputs, out_shapes, scratch


def _side_copies(sides, in_refs, out_refs, sem_refs):
    x, y, c = _my_pos()
    starts, waits = [], []
    pos = 0
    for s, (kind, arrays) in enumerate(sides):
        send_sems, recv_sems, local_sems = sem_refs[3 * s:3 * s + 3]
        for a in range(len(arrays)):
            src, out = in_refs[pos], out_refs[pos]
            pos += 1

            def remote(k, src_ref, dst_ref, to):
                return pltpu.make_async_remote_copy(src_ref=src_ref, dst_ref=dst_ref, send_sem=send_sems.at[a, k],
                                                    recv_sem=recv_sems.at[a, k], device_id=to, device_id_type=MESH)

            def local(src_ref, dst_ref):
                cp = pltpu.make_async_copy(src_ref, dst_ref, local_sems.at[a])
                starts.append(cp.start)
                waits.append(cp.wait)

            if kind == "gather":
                me = 4 * x + 2 * y + c
                local(src, out.at[me])
                for k, (fx, fy, fc) in enumerate(_FLIPS):
                    px, py, pc = (1 - x if fx else x), (1 - y if fy else y), (1 - c if fc else c)
                    send = remote(k, src, out.at[me], (px, py, pc))
                    starts.append(send.start)
                    waits += [remote(k, src, out.at[4 * px + 2 * py + pc], (px, py, pc)).wait_recv, send.wait_send]
            elif kind == "sibling":
                for j in range(4):
                    cp = remote(j, src.at[2 * j + (1 - c)], out.at[j], (x, y, 1 - c))
                    starts.append(cp.start)
                    waits.append(cp.wait)
            else:
                jme = 2 * x + y
                local(src.at[jme], out.at[jme])
                for k, (px, py) in enumerate([(1 - x, y), (x, 1 - y), (1 - x, 1 - y)]):
                    send = remote(k, src.at[2 * px + py], out.at[jme], (px, py, c))
                    starts.append(send.start)
                    waits += [remote(k, src.at[2 * px + py], out.at[2 * px + py], (px, py, c)).wait_recv,
                              send.wait_send]
    return starts, waits


def _call_with_sides(body, sides, *, name, grid, in_specs, out_specs, out_shape, scratch_shapes, compiler_params, args):
    if not sides:
        res = pl.pallas_call(body, name=name, grid=grid, in_specs=in_specs, out_specs=out_specs, out_shape=out_shape,
                             scratch_shapes=scratch_shapes, compiler_params=compiler_params)(*args)
        return list(res), []
    s_in, s_out, s_scr = _side_plan(sides)
    n_in, n_out, n_scr, n_side = len(in_specs), len(out_specs), len(scratch_shapes), len(s_in)

    def wrapped(*refs):
        refs = list(refs)
        ins, side_in = refs[:n_in], refs[n_in:n_in + n_side]
        outs = refs[n_in + n_side:n_in + n_side + n_out]
        side_out = refs[n_in + n_side + n_out:n_in + 2 * n_side + n_out]
        rest = refs[n_in + 2 * n_side + n_out:]
        starts, waits = _side_copies(sides, side_in, side_out, rest[n_scr:])
        first = functools.reduce(jnp.logical_and, [pl.program_id(d) == 0 for d in range(len(grid))])
        last = functools.reduce(jnp.logical_and, [pl.program_id(d) == grid[d] - 1 for d in range(len(grid))])

        @pl.when(first)
        def _():
            for start in starts:
                start()

        body(*ins, *outs, *rest[:n_scr])

        @pl.when(last)
        def _():
            for wait in waits:
                wait()

    res = pl.pallas_call(
        wrapped, name=name, grid=grid, in_specs=list(in_specs) + [ANY] * n_side,
        out_specs=list(out_specs) + [ANY] * n_side, out_shape=list(out_shape) + s_out,
        scratch_shapes=list(scratch_shapes) + s_scr, compiler_params=compiler_params,
    )(*args, *s_in)
    return list(res[:n_out]), list(res[n_out:])


def _row_tile(r, l):
    t = min(r, max(16, ELEMENTWISE_TILE_BYTES // (4 * l) // 16 * 16))
    while r % t:
        t -= 16
    return t


def _pair_sum(buf, recv, core, name):
    _, r, l = buf.shape
    tr = _row_tile(r, l)

    def body(core_ref, a_ref, b_ref, o_ref):
        o_ref[...] = (a_ref[...] + b_ref[...]).astype(WIRE_DTYPE)

    return pl.pallas_call(
        body, name=name, out_shape=jax.ShapeDtypeStruct((4, r, l), WIRE_DTYPE),
        grid_spec=pltpu.PrefetchScalarGridSpec(
            num_scalar_prefetch=1, grid=(4, r // tr),
            in_specs=[pl.BlockSpec((None, tr, l), lambda j, i, cr: (2 * j + cr[0], i, 0)),
                      pl.BlockSpec((None, tr, l), lambda j, i, cr: (j, i, 0))],
            out_specs=pl.BlockSpec((None, tr, l), lambda j, i, cr: (j, i, 0))),
        compiler_params=_cparams(dimension_semantics=("arbitrary", "arbitrary")),
    )(core, buf, recv)


def _sum_parts(p_ref):
    return ((p_ref[0].astype(F32) + p_ref[1].astype(F32)) + (p_ref[2].astype(F32) + p_ref[3].astype(F32)))


def _sum4(parts, name):
    _, r, l = parts.shape
    tr = _row_tile(r, l)

    def body(p_ref, o_ref):
        o_ref[...] = _sum_parts(p_ref)

    return pl.pallas_call(
        body, name=name, out_shape=jax.ShapeDtypeStruct((r, l), F32), grid=(r // tr,),
        in_specs=[pl.BlockSpec((4, tr, l), lambda i: (0, i, 0))],
        out_specs=pl.BlockSpec((tr, l), lambda i: (i, 0)),
        compiler_params=_cparams(dimension_semantics=("arbitrary",)),
    )(parts)


def _adamw_update(w, gg, m, v):
    nm = ADAM_B1 * m + (1.0 - ADAM_B1) * gg
    nv = ADAM_B2 * v + (1.0 - ADAM_B2) * (gg * gg)
    m_hat = nm / (1.0 - ADAM_B1 ** ADAM_STEP)
    v_hat = nv / (1.0 - ADAM_B2 ** ADAM_STEP)
    return -ADAM_LR * (m_hat / (jnp.sqrt(v_hat) + ADAM_EPS) + ADAM_WD * w), nm, nv


def _adamw(w, g, m, v, name):
    r, l = w.shape
    tr = _row_tile(r, l)

    def body(w_ref, g_ref, m_ref, v_ref, d_ref, nm_ref, nv_ref):
        d_ref[...], nm_ref[...], nv_ref[...] = _adamw_update(w_ref[...], g_ref[...], m_ref[...], v_ref[...])

    spec = pl.BlockSpec((tr, l), lambda i: (i, 0))
    return pl.pallas_call(
        body, name=name, out_shape=[jax.ShapeDtypeStruct((r, l), F32)] * 3, grid=(r // tr,),
        in_specs=[spec] * 4, out_specs=[spec] * 3,
        compiler_params=_cparams(dimension_semantics=("arbitrary",)),
    )(w, g, m, v)


def _adamw_parts(w, parts, m, v, name):
    nl, r, l = w.shape
    tr = _row_tile(r, l)

    def body(*refs):
        w_ref, p_refs, (m_ref, v_ref, g_ref, d_ref, nm_ref, nv_ref) = refs[0], refs[1:1 + nl], refs[1 + nl:]
        layer = pl.program_id(0)
        gg = _sum_parts(p_refs[0])
        for q in range(1, nl):
            gg = jnp.where(layer == q, _sum_parts(p_refs[q]), gg)
        g_ref[...] = gg
        d_ref[...], nm_ref[...], nv_ref[...] = _adamw_update(w_ref[...], gg, m_ref[...], v_ref[...])

    spec = pl.BlockSpec((None, tr, l), lambda q, i: (q, i, 0))
    pspecs = [pl.BlockSpec((4, tr, l), lambda q, i, k=k: (0, jnp.where(q == k, i, 0), 0)) for k in range(nl)]
    return pl.pallas_call(
        body, name=name, out_shape=[jax.ShapeDtypeStruct((nl, r, l), F32)] * 4, grid=(nl, r // tr),
        in_specs=[spec] + pspecs + [spec, spec], out_specs=[spec] * 4,
        compiler_params=_cparams(dimension_semantics=("arbitrary", "arbitrary")),
    )(w, *parts, m, v)


def _to_rows(pieces, row_multiple):
    flat = jnp.concatenate([p.reshape(-1) for p in pieces])
    rows = -(-flat.shape[0] // LANES)
    rows = -(-rows // row_multiple) * row_multiple
    flat = jnp.pad(flat, (0, rows * LANES - flat.shape[0]))
    return flat.reshape(rows, LANES)


def _split_rows(rows, shapes):
    flat = rows.reshape(-1)
    out, off = [], 0
    for s in shapes:
        n = int(np.prod(s))
        out.append(flat[off:off + n].reshape(s))
        off += n
    return out


def _mod_fwd(cond, w_mod, b_my, name):
    nl, _, ncol = w_mod.shape

    def body(a_ref, w_ref, b_ref, o_ref):
        a = a_ref[...]
        s = a * _sigmoid(a)
        for i in range(nl):
            o_ref[i] = _dot(s, w_ref[i]) + b_ref[i]

    return pl.pallas_call(
        body, name=name, out_shape=jax.ShapeDtypeStruct((nl, 16, ncol), F32),
        compiler_params=_cparams(),
    )(cond, w_mod, b_my)


def _mod_bwd(cond, dm_all, dm_my, w_mod, name):
    nl, _, ncol = w_mod.shape

    def body(a_ref, dma_ref, dmm_ref, w_ref, gw_ref, gb_ref, gc_ref):
        a = a_ref[...]
        sg = _sigmoid(a)
        s = a * sg
        for i in range(nl):
            gw_ref[i] = _dot_tn(s, dmm_ref[i])
            gb_ref[i] = jnp.sum(dma_ref[i], axis=0, keepdims=True)
        back = _dot_nt(dmm_ref[0], w_ref[0])
        dsilu = sg * (1.0 + a * (1.0 - sg))
        gc_ref[...] = jnp.sum(back[8:16] * dsilu[8:16], axis=0, keepdims=True)

    return pl.pallas_call(
        body, name=name,
        out_shape=[jax.ShapeDtypeStruct((nl, D_MODEL, ncol), F32), jax.ShapeDtypeStruct((nl, 1, 3 * D_MODEL), F32),
                   jax.ShapeDtypeStruct((1, D_MODEL), F32)],
        compiler_params=_cparams(),
    )(cond, dm_all, dm_my, w_mod)


def _in_proj(xt, sc, sh, wg, name, sides=(), u_dtype=F32):
    t = xt.shape[0]
    tm = min(TM_MM, t)

    def body(x_ref, sc_ref, sh_ref, w_ref, u_ref, g_ref):
        h = (x_ref[...] * (1.0 + sc_ref[...]) + sh_ref[...]).astype(MXU_DTYPE)
        for k in range(N_WBLK):
            o = jnp.dot(h, w_ref[k], preferred_element_type=F32)
            if k < N_WBLK // 2:
                u_ref[:, k * WBLK:(k + 1) * WBLK] = o.astype(u_dtype)
            else:
                kk = k - N_WBLK // 2
                g_ref[:, kk * WBLK:(kk + 1) * WBLK] = o.astype(ACT_DTYPE)

    row = pl.BlockSpec((1, D_MODEL), lambda i: (0, 0))
    return _call_with_sides(
        body, sides, name=name,
        out_shape=[jax.ShapeDtypeStruct((t, D_INNER), u_dtype), jax.ShapeDtypeStruct((t, D_INNER), ACT_DTYPE)],
        grid=(t // tm,),
        in_specs=[pl.BlockSpec((tm, D_MODEL), lambda i: (i, 0)), row, row,
                  pl.BlockSpec((N_WBLK, D_MODEL, WBLK), lambda i: (0, 0, 0), pipeline_mode=pl.Buffered(1))],
        out_specs=[pl.BlockSpec((tm, D_INNER), lambda i: (i, 0))] * 2, scratch_shapes=[],
        compiler_params=_cparams(dimension_semantics=("arbitrary",)), args=[xt, sc, sh, wg])


def _halo_maps(nt, tm, n_blocks, pos, rows=SUBLANES):
    per = tm // rows
    prev = lambda cb, i: (jnp.maximum(pos(i) * per - 1, 0), cb)
    nxt = lambda cb, i: (jnp.minimum((pos(i) + 1) * per, n_blocks - 1), cb)
    return prev, nxt


def _conv_taps(u, prev8, next8, is_first, is_last):
    pz = jnp.where(is_first, 0.0, 1.0)
    nz = jnp.where(is_last, 0.0, 1.0)
    return _shifted(u, prev8 * pz, next8 * nz, [-2, -1, 1])


def _lru_gates(uv, wa_ref, wx_ref, ba, bx, cl, g):
    sl = slice(g * LANES, (g + 1) * LANES)
    uvg = uv[:, sl]
    r = _sigmoid(_dot(uvg, wa_ref[g]) + ba[:, sl])
    ii = _sigmoid(_dot(uvg, wx_ref[g]) + bx[:, sl])
    la = cl[:, sl] * r
    a = jnp.exp(la)
    q = jnp.tanh(-la) * (1.0 + a * a)
    rs = lax.rsqrt(jnp.maximum(q, SQRT_FLOOR))
    return uvg, r, ii, a, q * rs, rs


def _scan_rows(seg):
    return -(-(SCAN_ROW_T * (seg - 1) + SCAN_ROW_J * (N_SEG - 1) + 1) // SUBLANES) * SUBLANES


def _seg_chunk(j, c):
    return pl.ds(SCAN_ROW_T * SUBLANES * c + SCAN_ROW_J * j, SUBLANES, stride=SCAN_ROW_T)


def _seg_scatter(ref, g, seg, value):
    for j in range(N_SEG):
        for c in range(seg // SUBLANES):
            r0 = j * seg + SUBLANES * c
            ref[g, _seg_chunk(j, c), :] = value[r0:r0 + SUBLANES]


def _scan_tile(a_s, b_s, carry_ref, write_out, seg, reverse, chunks_per_write=1):
    n_g = a_s.shape[0]
    unroll = SCAN_UNROLL if seg % SCAN_UNROLL == 0 else 1

    n_trips = seg // unroll

    def steps(k, state):
        hs, cs = list(state[0]), list(state[1])
        base = ((n_trips - 1 - k) if reverse else k) * unroll
        for q in (range(unroll - 1, -1, -1) if reverse else range(unroll)):
            t = base + q
            rows = pl.ds(t * SCAN_ROW_T, N_SEG, stride=SCAN_ROW_J)
            for g in range(n_g):
                a = a_s[g, rows, :]
                b = b_s[g, rows, :]
                hs[g] = a * hs[g] + b
                cs[g] = a * cs[g]
                b_s[g, rows, :] = hs[g]
                a_s[g, rows, :] = cs[g]
        return tuple(hs), tuple(cs)

    zeros = tuple(jnp.zeros((N_SEG, LANES), F32) for _ in range(n_g))
    ones = tuple(jnp.ones((N_SEG, LANES), F32) for _ in range(n_g))
    h_fin, a_fin = lax.fori_loop(0, seg // unroll, steps, (zeros, ones))

    order = list(range(N_SEG - 1, -1, -1)) if reverse else list(range(N_SEG))
    for g in range(n_g):
        carry = carry_ref[:, g * LANES:(g + 1) * LANES]
        for j in order:
            for c0 in range(0, seg // SUBLANES, chunks_per_write):
                parts = [b_s[g, _seg_chunk(j, c), :] + a_s[g, _seg_chunk(j, c), :] * carry
                         for c in range(c0, c0 + chunks_per_write)]
                write_out(j, c0, g, parts[0] if chunks_per_write == 1 else jnp.concatenate(parts, axis=0))
            carry = a_fin[g][j:j + 1] * carry + h_fin[g][j:j + 1]
        carry_ref[:, g * LANES:(g + 1) * LANES] = carry


def _lru_specs(s, tm, cb, direction_pos, nt):
    n_rows8 = s // SUBLANES
    prev, nxt = _halo_maps(nt, tm, n_rows8, direction_pos)
    tile = pl.BlockSpec((tm, cb), lambda c, i: (direction_pos(i), c))
    return tile, pl.BlockSpec((SUBLANES, cb), prev), pl.BlockSpec((SUBLANES, cb), nxt)


def _lru_param_specs(cb, d):
    n_g = cb // LANES
    vec = pl.BlockSpec((1, cb), lambda c, i: (0, c))
    dvec = pl.BlockSpec((None, 1, cb), lambda c, i: (d, 0, c))
    wmat = pl.BlockSpec((None, n_g, LRU_BLOCK, LRU_BLOCK), lambda c, i: (d, c, 0, 0))
    return vec, dvec, wmat


def _lru_fwd(src, h0, p, d, name, conv, sides=()):
    s = src.shape[0]
    tm = min(TM_LRU, s)
    cb = CB_LRU
    n_g = cb // LANES
    nt = s // tm
    seg = tm // N_SEG
    pos = (lambda i: i) if d == 0 else (lambda i: nt - 1 - i)

    def body(*refs):
        refs = list(refs)
        u_ref = refs.pop(0)
        if conv:
            up_ref, un_ref, cw_ref, cbias_ref = [refs.pop(0) for _ in range(4)]
        wa_ref, wx_ref, ba_ref, bx_ref, lam_ref, h0_ref, h_ref, hc_ref = [refs.pop(0) for _ in range(8)]
        uv_ref = refs.pop(0) if conv else None
        a_s, b_s = refs
        i = pl.program_id(1)
        tp = pos(i)

        @pl.when(i == 0)
        def _():
            hc_ref[...] = h0_ref[...]

        if conv:
            u_t = u_ref[...]
            um2, um1, up1 = _conv_taps(u_t, up_ref[...], un_ref[...], tp == 0, tp == nt - 1)
            cw = cw_ref[...]
            uv_ref[...] = um2 * cw[0:1] + um1 * cw[1:2] + u_t * cw[2:3] + up1 * cw[3:4] + cbias_ref[...]
        src_ref = uv_ref if conv else u_ref
        cl = LRU_C * _log_sigmoid(lam_ref[...])
        ba, bx = ba_ref[...], bx_ref[...]
        for g in range(n_g):
            uvg, r, ii, a, sq, _ = _lru_gates(src_ref, wa_ref, wx_ref, ba, bx, cl, g)
            b = sq * (ii * uvg)
            _seg_scatter(a_s, g, seg, a)
            _seg_scatter(b_s, g, seg, b)

        per_write = 2 if (seg // SUBLANES) % 2 == 0 else 1

        def write_out(j, c, g, h):
            h_ref[pl.ds(j * seg + SUBLANES * c, SUBLANES * per_write), pl.ds(g * LANES, LANES)] = h.astype(ACT_DTYPE)

        _scan_tile(a_s, b_s, hc_ref, write_out, seg, reverse=(d == 1), chunks_per_write=per_write)

    tile, prev, nxt = _lru_specs(s, tm, cb, pos, nt)
    vec, dvec, wmat = _lru_param_specs(cb, d)
    wide = jax.ShapeDtypeStruct((s, D_INNER), F32)
    conv_specs = [prev, nxt, pl.BlockSpec((4, cb), lambda c, i: (0, c)), vec] if conv else []
    conv_args = [src, src, p["conv_w"], p["conv_b"]] if conv else []
    return _call_with_sides(
        body, sides, name=name,
        out_shape=[jax.ShapeDtypeStruct((s, D_INNER), ACT_DTYPE), jax.ShapeDtypeStruct((1, D_INNER), F32)]
        + ([wide] if conv else []),
        grid=(D_INNER // cb, nt),
        in_specs=[tile] + conv_specs + [wmat, wmat, dvec, dvec, dvec, vec],
        out_specs=[tile, vec] + ([tile] if conv else []),
        scratch_shapes=[pltpu.VMEM((n_g, _scan_rows(seg), LANES), F32)] * 2,
        compiler_params=_cparams(dimension_semantics=("arbitrary", "arbitrary")),
        args=[src, *conv_args, p["wa"], p["wx"], p["ba"], p["bx"], p["lam"], h0])


def _lru_bwd(uv, dh, h, h0, lam_in, p, d, name, sides=()):
    s = uv.shape[0]
    tm = min(TM_LRU, s)
    cb = CB_LRU
    n_g = cb // LANES
    nt = s // tm
    seg = tm // N_SEG
    pos = (lambda i: nt - 1 - i) if d == 0 else (lambda i: i)

    def body(uv_ref, dh_ref, h_ref, hh_ref, wa_ref, wx_ref, ba_ref, bx_ref,
             lam_ref, h0_ref, lin_ref, duv_ref, gwa_ref, gwx_ref, gv_ref, lc_ref, a_s, b_s, lp_s,
             r_s, i_s, q_s, rq_s, a_keep):
        i = pl.program_id(1)
        tp = pos(i)

        @pl.when(i == 0)
        def _():
            lc_ref[...] = lin_ref[...]
            gwa_ref[...] = jnp.zeros_like(gwa_ref)
            gwx_ref[...] = jnp.zeros_like(gwx_ref)
            gv_ref[...] = jnp.zeros_like(gv_ref)

        uv = uv_ref[...]
        lam = lam_ref[...]
        cl = LRU_C * _log_sigmoid(lam)
        ba, bx = ba_ref[...], bx_ref[...]
        dh_t = dh_ref[...].astype(F32)
        carry_in = lc_ref[...]
        for g in range(n_g):
            sl = slice(g * LANES, (g + 1) * LANES)
            _, r, ii, a, sq, rs = _lru_gates(uv, wa_ref, wx_ref, ba, bx, cl, g)
            r_s[:, sl], i_s[:, sl], q_s[:, sl], rq_s[:, sl], a_keep[:, sl] = r, ii, sq, rs, a
            b = a * dh_t[:, sl]
            _seg_scatter(a_s, g, seg, a)
            _seg_scatter(b_s, g, seg, b)

        def write_out(j, c, g, v):
            lp_s[pl.ds(j * seg + SUBLANES * c, SUBLANES), pl.ds(g * LANES, LANES)] = v

        _scan_tile(a_s, b_s, lc_ref, write_out, seg, reverse=(d == 0))

        h_t = h_ref[...].astype(F32)
        hh = hh_ref[...].astype(F32)
        if d == 0:
            edge = jnp.where(tp == 0, h0_ref[...], hh[H_HALO - 1:H_HALO])
            h_prev = _shift_down(h_t, edge)
            lam_t = dh_t + _shift_up(lp_s[...], carry_in)
        else:
            edge = jnp.where(tp == nt - 1, h0_ref[...], hh[0:1])
            h_prev = _shift_up(h_t, edge)
            lam_t = dh_t + _shift_down(lp_s[...], carry_in)

        dsig = LRU_C * _sigmoid(-lam)
        for g in range(n_g):
            sl = slice(g * LANES, (g + 1) * LANES)
            uvg, r, ii, a, sq = uv[:, sl], r_s[:, sl], i_s[:, sl], a_keep[:, sl], q_s[:, sl]
            lt = lam_t[:, sl]
            ls = lt * sq
            dla = (lt * a) * (h_prev[:, sl] - (ii * uvg) * (a * rq_s[:, sl]))
            dzr = (dla * cl[:, sl]) * r * (1.0 - r)
            dzi = (ls * uvg) * ii * (1.0 - ii)
            duv_ref[:, sl] = ls * ii + _dot_nt(dzr, wa_ref[g]) + _dot_nt(dzi, wx_ref[g])
            gwa_ref[g] += _dot_tn(uvg, dzr)
            gwx_ref[g] += _dot_tn(uvg, dzi)
            gv_ref[0:1, sl] += _rowsum(dzr)
            gv_ref[1:2, sl] += _rowsum(dzi)
            gv_ref[2:3, sl] += _rowsum(dla * r) * dsig[:, sl]

    tile, _, _ = _lru_specs(s, tm, cb, pos, nt)
    vec, dvec, wmat = _lru_param_specs(cb, d)
    h_prev_map, h_next_map = _halo_maps(nt, tm, s // H_HALO, pos, rows=H_HALO)
    hh_spec = pl.BlockSpec((H_HALO, cb), h_prev_map if d == 0 else h_next_map)
    gw_spec = pl.BlockSpec((n_g, LRU_BLOCK, LRU_BLOCK), lambda c, i: (c, 0, 0))
    n_blk = D_INNER // LRU_BLOCK
    return _call_with_sides(
        body, sides, name=name,
        out_shape=[jax.ShapeDtypeStruct((s, D_INNER), F32),
                   jax.ShapeDtypeStruct((n_blk, LRU_BLOCK, LRU_BLOCK), F32),
                   jax.ShapeDtypeStruct((n_blk, LRU_BLOCK, LRU_BLOCK), F32),
                   jax.ShapeDtypeStruct((SUBLANES, D_INNER), F32),
                   jax.ShapeDtypeStruct((1, D_INNER), F32)],
        grid=(D_INNER // cb, nt),
        in_specs=[tile, tile, tile, hh_spec, wmat, wmat, dvec, dvec, dvec, vec, vec],
        out_specs=[tile, gw_spec, gw_spec, pl.BlockSpec((SUBLANES, cb), lambda c, i: (0, c)), vec],
        scratch_shapes=[pltpu.VMEM((n_g, _scan_rows(seg), LANES), F32)] * 2 + [pltpu.VMEM((tm, cb), F32)] * 6,
        compiler_params=_cparams(dimension_semantics=("arbitrary", "arbitrary")),
        args=[uv, dh, h, h, p["wa"], p["wx"], p["ba"], p["bx"], p["lam"], h0, lam_in])


def _out0(hf, hb, g, xt, gt, wo, lg, lb, name):
    t = xt.shape[0]
    tm = min(TM_MM, t)

    def body(hf_ref, hb_ref, g_ref, x_ref, gt_ref, w_ref, lg_ref, lb_ref, x1_ref, br_ref):
        br = None
        for k in range(D_INNER // WBLK):
            sl = slice(k * WBLK, (k + 1) * WBLK)
            gg = g_ref[:, sl].astype(F32)
            p = (hf_ref[:, sl].astype(F32) + hb_ref[:, sl].astype(F32)) * (gg * _sigmoid(gg))
            part = _dot(p, w_ref[sl, :])
            br = part if br is None else br + part
        z = ALPHA * x_ref[...] + gt_ref[...] * br
        xhat, _ = _layer_norm_stats(z)
        x1_ref[...] = xhat * lg_ref[...] + lb_ref[...]
        br_ref[...] = br.astype(ACT_DTYPE)

    wide = pl.BlockSpec((tm, D_INNER), lambda i: (i, 0))
    nar = pl.BlockSpec((tm, D_MODEL), lambda i: (i, 0))
    row = pl.BlockSpec((1, D_MODEL), lambda i: (0, 0))
    return pl.pallas_call(
        body, name=name,
        out_shape=[jax.ShapeDtypeStruct((t, D_MODEL), F32), jax.ShapeDtypeStruct((t, D_MODEL), ACT_DTYPE)],
        grid=(t // tm,),
        in_specs=[wide, wide, wide, nar, row,
                  pl.BlockSpec((D_INNER, D_MODEL), lambda i: (0, 0), pipeline_mode=pl.Buffered(1)), row, row],
        out_specs=[nar, nar],
        compiler_params=_cparams(dimension_semantics=("arbitrary",)),
    )(hf, hb, g, xt, gt, wo, lg, lb)


def _unrolled_loop(n, fn, unroll=4):
    while n % unroll:
        unroll //= 2

    def trip(k, carry):
        for q in range(unroll):
            fn(k * unroll + q)
        return carry
    lax.fori_loop(0, n // unroll, trip, 0)


def _window(n, w):
    t = np.arange(n)
    return np.clip(t - w // 2, 0, n), np.clip(t + w // 2, 0, n)


def _pool_tables(n_rows, transpose):
    boxes, inv_c, inv_r = [], [], []
    for w in POOL_WINDOWS:
        lo, hi = _window(GRID_W, w)
        m = np.zeros((GRID_W, GRID_W), np.float32)
        for r in range(GRID_W):
            m[r, lo[r]:hi[r]] = 1.0
        m = np.kron(np.eye(POOL_TOK // GRID_W, dtype=np.float32), m)
        boxes.append(m.T if transpose else m)
        inv_c.append(np.broadcast_to((1.0 / (hi - lo).astype(np.float32))[:, None], (GRID_W, LANES)))
        lo_r, hi_r = _window(n_rows, w)
        inv_r.append(1.0 / (hi_r - lo_r).astype(np.float32))
    return (jnp.asarray(np.stack(boxes), MXU_DTYPE), jnp.asarray(np.stack(inv_c), F32),
            jnp.asarray(np.stack(inv_r), F32))


def _pool_mix(xin, transpose, out_dtype, name):
    s = xin.shape[0]
    n_rows = s // GRID_W
    pad_t = SUBLANES * GRID_W
    rows_per_blk = POOL_TOK // GRID_W
    n_slab = D_INNER // LANES
    slabs_per_group = POOL_GROUP // LANES
    n_win = len(POOL_WINDOWS)
    boxes, inv_c, inv_r = _pool_tables(n_rows, transpose)
    exact_operand = (not transpose) and xin.dtype == MXU_DTYPE and MXU_DTYPE != F32

    def body(invr_ref, box_ref, invc_ref, x_ref, o_ref, pad_s):
        k = pl.program_id(0) // slabs_per_group
        pad_s[pl.ds(0, pad_t), :] = jnp.zeros((pad_t, LANES), F32)
        pad_s[pl.ds(pad_t + s, pad_t), :] = jnp.zeros((pad_t, LANES), F32)

        for kk, w in enumerate(POOL_WINDOWS):
            half = w // 2
            offsets = list(range(-(half - 1), half + 1)) if transpose else list(range(-half, half))

            @pl.when(k == kk)
            def _():
                inv_col = invc_ref[kk]

                def col_box(b):
                    st = pl.multiple_of(b * POOL_TOK, POOL_TOK)
                    xb = x_ref[pl.ds(st, POOL_TOK), :]
                    if exact_operand:
                        pad_s[pl.ds(pad_t + st, POOL_TOK), :] = jnp.dot(box_ref[kk], xb, preferred_element_type=F32)
                        return
                    xb = xb.astype(F32)
                    if transpose:
                        xb = xb * jnp.concatenate(
                            [inv_col * invr_ref[kk, b * rows_per_blk + q] for q in range(rows_per_blk)], axis=0)
                    hi = xb.astype(MXU_DTYPE)
                    lo = (xb - hi.astype(F32)).astype(MXU_DTYPE)
                    both = jnp.dot(box_ref[kk], jnp.concatenate([hi, lo], axis=1), preferred_element_type=F32)
                    pad_s[pl.ds(pad_t + st, POOL_TOK), :] = both[:, :LANES] + both[:, LANES:]
                _unrolled_loop(s // POOL_TOK, col_box)

                def row_box(r):
                    st = pl.multiple_of(r * GRID_W, GRID_W)
                    acc = pad_s[pl.ds(pad_t + st + offsets[0] * GRID_W, GRID_W), :]
                    for o in offsets[1:]:
                        acc = acc + pad_s[pl.ds(pad_t + st + o * GRID_W, GRID_W), :]
                    if not transpose:
                        acc = acc * (inv_col * invr_ref[kk, r])
                    o_ref[pl.ds(st, GRID_W), :] = (acc - x_ref[pl.ds(st, GRID_W), :].astype(F32)).astype(out_dtype)
                _unrolled_loop(n_rows, row_box)

    slab = pl.BlockSpec((s, LANES), lambda i: (0, i))
    return pl.pallas_call(
        body, name=name, out_shape=jax.ShapeDtypeStruct((s, D_INNER), out_dtype), grid=(n_slab,),
        in_specs=[pl.BlockSpec(memory_space=pltpu.SMEM),
                  pl.BlockSpec((n_win, POOL_TOK, POOL_TOK), lambda i: (0, 0, 0)),
                  pl.BlockSpec((n_win, GRID_W, LANES), lambda i: (0, 0, 0)), slab],
        out_specs=slab,
        scratch_shapes=[pltpu.VMEM((s + 2 * pad_t, LANES), F32)],
        compiler_params=_cparams(dimension_semantics=("arbitrary",)),
    )(inv_r, boxes, inv_c, xin)


def _out1(dmix, pw, ps, g, x1, gt, wo, lg, lb, tgt, name):
    t = x1.shape[0]
    tm = min(TM_MM, t)
    n_grp = len(POOL_WINDOWS)

    def body(d_ref, pw_ref, ps_ref, g_ref, x1_ref, gt_ref, w_ref, lg_ref, lb_ref, tgt_ref, dz_ref, st_ref, po_ref):
        @pl.when(pl.program_id(0) == 0)
        def _():
            st_ref[...] = jnp.zeros_like(st_ref)

        br = jnp.zeros((tm, D_MODEL), F32)
        for k in range(n_grp):
            sl = slice(k * POOL_GROUP, (k + 1) * POOL_GROUP)
            po = jnp.dot(d_ref[:, sl], pw_ref[k], preferred_element_type=F32)
            po_ref[:, sl] = po.astype(ACT_DTYPE)
            y = po * ps_ref[:, sl]
            gg = g_ref[:, sl].astype(F32)
            br = br + _dot(y * (gg * _sigmoid(gg)), w_ref[sl, :])
        z = ALPHA * x1_ref[...] + gt_ref[...] * br
        xhat, rstd = _layer_norm_stats(z)
        lg_v = lg_ref[...]
        err = xhat * lg_v + lb_ref[...] - tgt_ref[...]
        dy = err * (1.0 / D_MODEL)
        dz = _layer_norm_bwd(dy, xhat, rstd, lg_v)
        dz_ref[...] = dz
        st_ref[0:1, :] += _rowsum(dy * xhat)
        st_ref[1:2, :] += _rowsum(dy)
        st_ref[2:3, :] += _rowsum(dz * br)
        st_ref[3:4, :] += _rowsum(err * err)

    wide = pl.BlockSpec((tm, D_INNER), lambda i: (i, 0))
    nar = pl.BlockSpec((tm, D_MODEL), lambda i: (i, 0))
    row = pl.BlockSpec((1, D_MODEL), lambda i: (0, 0))
    return pl.pallas_call(
        body, name=name,
        out_shape=[jax.ShapeDtypeStruct((t, D_MODEL), F32), jax.ShapeDtypeStruct((SUBLANES, D_MODEL), F32),
                   jax.ShapeDtypeStruct((t, D_INNER), ACT_DTYPE)],
        grid=(t // tm,),
        in_specs=[wide, pl.BlockSpec((n_grp, POOL_GROUP, POOL_GROUP), lambda i: (0, 0, 0)),
                  pl.BlockSpec((1, D_INNER), lambda i: (0, 0)), wide, nar, row,
                  pl.BlockSpec((D_INNER, D_MODEL), lambda i: (0, 0), pipeline_mode=pl.Buffered(1)), row, row, nar],
        out_specs=[nar, pl.BlockSpec((SUBLANES, D_MODEL), lambda i: (0, 0)), wide],
        compiler_params=_cparams(dimension_semantics=("arbitrary",)),
    )(dmix, pw, ps, g, x1, gt, wo, lg, lb, tgt)


def _flush(acc, out_hbm, sem):
    cp = pltpu.make_async_copy(acc, out_hbm, sem)
    cp.start()
    cp.wait()


def _bout1(dz, dmix, po, g, pw, ps, gt, wo, name):
    t = dz.shape[0]
    tm = min(TM_MM, t)
    nt = t // tm
    n_grp = len(POOL_WINDOWS)

    def body(dz_ref, d_ref, po_ref, g_ref, pw_ref, ps_ref, gt_ref, w_ref, dd_ref, dg_ref, gwo_hbm, gpw_hbm, gps_ref,
             gwo_acc, gpw_acc, sems):
        i = pl.program_id(0)

        @pl.when(i == 0)
        def _():
            gwo_acc[...] = jnp.zeros_like(gwo_acc)
            gpw_acc[...] = jnp.zeros_like(gpw_acc)
            gps_ref[...] = jnp.zeros_like(gps_ref)

        db = (gt_ref[...] * dz_ref[...]).astype(MXU_DTYPE)
        for k in range(n_grp):
            sl = slice(k * POOL_GROUP, (k + 1) * POOL_GROUP)
            dk = d_ref[:, sl]
            po = po_ref[:, sl].astype(F32)
            psk = ps_ref[:, sl]
            y = po * psk
            gg = g_ref[:, sl].astype(F32)
            sg = _sigmoid(gg)
            silu = gg * sg
            gwo_acc[sl, :] += _dot_tn(y * silu, db)
            dp = _dot_nt(db, w_ref[sl, :])
            dy = dp * silu
            dg_ref[:, sl] = (dp * y * (sg * (1.0 + gg * (1.0 - sg)))).astype(MXU_DTYPE)
            gps_ref[0:1, sl] += _rowsum(dy * po)
            dpo = (dy * psk).astype(MXU_DTYPE)
            gpw_acc[k] += _dot_tn(dk, dpo)
            dd_ref[:, sl] = _dot_nt(dpo, pw_ref[k])

        @pl.when(i == nt - 1)
        def _():
            _flush(gwo_acc, gwo_hbm, sems.at[0])
            _flush(gpw_acc, gpw_hbm, sems.at[1])

    wide = pl.BlockSpec((tm, D_INNER), lambda i: (i, 0))
    nar = pl.BlockSpec((tm, D_MODEL), lambda i: (i, 0))
    return pl.pallas_call(
        body, name=name,
        out_shape=[jax.ShapeDtypeStruct((t, D_INNER), F32), jax.ShapeDtypeStruct((t, D_INNER), MXU_DTYPE),
                   jax.ShapeDtypeStruct((D_INNER, D_MODEL), F32),
                   jax.ShapeDtypeStruct((n_grp, POOL_GROUP, POOL_GROUP), F32),
                   jax.ShapeDtypeStruct((SUBLANES, D_INNER), F32)],
        grid=(nt,),
        in_specs=[nar, wide, wide, wide,
                  pl.BlockSpec((n_grp, POOL_GROUP, POOL_GROUP), lambda i: (0, 0, 0), pipeline_mode=pl.Buffered(1)),
                  pl.BlockSpec((1, D_INNER), lambda i: (0, 0)), pl.BlockSpec((1, D_MODEL), lambda i: (0, 0)),
                  pl.BlockSpec((D_INNER, D_MODEL), lambda i: (0, 0), pipeline_mode=pl.Buffered(1))],
        out_specs=[wide, wide, ANY, ANY, pl.BlockSpec((SUBLANES, D_INNER), lambda i: (0, 0))],
        scratch_shapes=[pltpu.VMEM((D_INNER, D_MODEL), F32), pltpu.VMEM((n_grp, POOL_GROUP, POOL_GROUP), F32),
                        pltpu.SemaphoreType.DMA((2,))],
        compiler_params=_cparams(dimension_semantics=("arbitrary",)),
    )(dz, dmix, po, g, pw, ps, gt, wo)


def _bout0(dx1, xt, br0, lg, hf, hb, g, gt, wo, name, sides=()):
    t = dx1.shape[0]
    tm = min(TM_MM, t)
    nt = t // tm

    def body(dx_ref, x_ref, br_ref, lg_ref, hf_ref, hb_ref, g_ref, gt_ref, w_ref,
             dz_ref, dy_ref, dg_ref, gwo_hbm, st_ref, gwo_acc, sem):
        i = pl.program_id(0)

        @pl.when(i == 0)
        def _():
            gwo_acc[...] = jnp.zeros_like(gwo_acc)
            st_ref[...] = jnp.zeros_like(st_ref)

        dx = dx_ref[...]
        br = br_ref[...].astype(F32)
        gate = gt_ref[...]
        xhat, rstd = _layer_norm_stats(ALPHA * x_ref[...] + gate * br)
        dz = _layer_norm_bwd(dx, xhat, rstd, lg_ref[...])
        dz_ref[...] = dz
        st_ref[0:1, :] += _rowsum(dx * xhat)
        st_ref[1:2, :] += _rowsum(dx)
        st_ref[2:3, :] += _rowsum(dz * br)
        db = (gate * dz).astype(MXU_DTYPE)
        for k in range(D_INNER // WBLK):
            sl = slice(k * WBLK, (k + 1) * WBLK)
            y = hf_ref[:, sl].astype(F32) + hb_ref[:, sl].astype(F32)
            gg = g_ref[:, sl].astype(F32)
            sg = _sigmoid(gg)
            silu = gg * sg
            gwo_acc[sl, :] += _dot_tn(y * silu, db)
            dp = _dot_nt(db, w_ref[sl, :])
            dy_ref[:, sl] = (dp * silu).astype(ACT_DTYPE)
            dg_ref[:, sl] = (dp * y * (sg * (1.0 + gg * (1.0 - sg)))).astype(MXU_DTYPE)

        @pl.when(i == nt - 1)
        def _():
            _flush(gwo_acc, gwo_hbm, sem)

    wide = pl.BlockSpec((tm, D_INNER), lambda i: (i, 0))
    nar = pl.BlockSpec((tm, D_MODEL), lambda i: (i, 0))
    row = pl.BlockSpec((1, D_MODEL), lambda i: (0, 0))
    return _call_with_sides(
        body, sides, name=name,
        out_shape=[jax.ShapeDtypeStruct((t, D_MODEL), F32), jax.ShapeDtypeStruct((t, D_INNER), ACT_DTYPE),
                   jax.ShapeDtypeStruct((t, D_INNER), MXU_DTYPE), jax.ShapeDtypeStruct((D_INNER, D_MODEL), F32),
                   jax.ShapeDtypeStruct((SUBLANES, D_MODEL), F32)],
        grid=(nt,),
        in_specs=[nar, nar, nar, row, wide, wide, wide, row,
                  pl.BlockSpec((D_INNER, D_MODEL), lambda i: (0, 0), pipeline_mode=pl.Buffered(1))],
        out_specs=[nar, wide, wide, ANY, pl.BlockSpec((SUBLANES, D_MODEL), lambda i: (0, 0))],
        scratch_shapes=[pltpu.VMEM((D_INNER, D_MODEL), F32), pltpu.SemaphoreType.DMA(())],
        compiler_params=_cparams(dimension_semantics=("arbitrary",)),
        args=[dx1, xt, br0, lg, hf, hb, g, gt, wo])


def _conv_bwd(duvf, duvb, u, conv_w, name, sides=()):
    s = u.shape[0]
    tm = min(TM_LRU, s)
    cb = CB_LRU
    nt = s // tm

    def body(df_ref, dfp_ref, dfn_ref, db_ref, dbp_ref, dbn_ref, u_ref, cw_ref, du_ref, cst_ref):
        i = pl.program_id(1)

        @pl.when(i == 0)
        def _():
            cst_ref[...] = jnp.zeros_like(cst_ref)

        first, last = i == 0, i == nt - 1
        pz = jnp.where(first, 0.0, 1.0)
        nz = jnp.where(last, 0.0, 1.0)
        dout = df_ref[...] + db_ref[...]
        dm1, dp1, dp2 = _shifted(dout, (dfp_ref[...] + dbp_ref[...]) * pz, (dfn_ref[...] + dbn_ref[...]) * nz,
                                 [-1, 1, 2])
        cw = cw_ref[...]
        du_ref[...] = (dp2 * cw[0:1] + dp1 * cw[1:2] + dout * cw[2:3] + dm1 * cw[3:4]).astype(MXU_DTYPE)
        u_t = u_ref[...]
        cst_ref[0:1, :] += _rowsum(dp2 * u_t)
        cst_ref[1:2, :] += _rowsum(dp1 * u_t)
        cst_ref[2:3, :] += _rowsum(dout * u_t)
        cst_ref[3:4, :] += _rowsum(dm1 * u_t)
        cst_ref[4:5, :] += _rowsum(dout)

    tile, prev, nxt = _lru_specs(s, tm, cb, lambda i: i, nt)
    return _call_with_sides(
        body, sides, name=name,
        out_shape=[jax.ShapeDtypeStruct((s, D_INNER), MXU_DTYPE), jax.ShapeDtypeStruct((SUBLANES, D_INNER), F32)],
        grid=(D_INNER // cb, nt),
        in_specs=[tile, prev, nxt] * 2 + [tile, pl.BlockSpec((4, cb), lambda c, i: (0, c))],
        out_specs=[tile, pl.BlockSpec((SUBLANES, cb), lambda c, i: (0, c))], scratch_shapes=[],
        compiler_params=_cparams(dimension_semantics=("arbitrary", "arbitrary")),
        args=[duvf, duvf, duvf, duvb, duvb, duvb, u, conv_w])


def _bin(du, dg, xin, dzin, sc, sh, wg, name, gw_init=None, sides=()):
    t = xin.shape[0]
    tm = min(TM_MM, t)
    nt = t // tm
    has_g, has_dx, has_init = dg is not None, dzin is not None, gw_init is not None
    half = N_WBLK // 2
    n_blk = N_WBLK if has_g else half

    def body(*refs):
        refs = list(refs)
        du_ref = refs.pop(0)
        dg_ref = refs.pop(0) if has_g else None
        x_ref = refs.pop(0)
        dz_ref = refs.pop(0) if has_dx else None
        sc_ref, sh_ref, w_ref = refs.pop(0), refs.pop(0), refs.pop(0)
        init_hbm = refs.pop(0) if has_init else None
        dx_ref = refs.pop(0) if has_dx else None
        gw_hbm, st_ref, gw_acc, sem = refs
        i = pl.program_id(0)

        @pl.when(i == 0)
        def _():
            st_ref[...] = jnp.zeros_like(st_ref)
            first_zero = 0
            if has_init:
                _flush(init_hbm, gw_acc.at[pl.ds(0, half)], sem)
                first_zero = half
            for k in range(first_zero, n_blk):
                gw_acc[k] = jnp.zeros((D_MODEL, WBLK), F32)

        xv = x_ref[...]
        scale = 1.0 + sc_ref[...]
        h = (xv * scale + sh_ref[...]).astype(MXU_DTYPE)
        dh = None
        for k in range(n_blk):
            src = du_ref if k < half else dg_ref
            kk = k % half
            dk = src[:, kk * WBLK:(kk + 1) * WBLK]
            gw_acc[k] += _dot_tn(h, dk)
            contrib = _dot_nt(dk, w_ref[k])
            dh = contrib if dh is None else dh + contrib
        st_ref[0:1, :] += _rowsum(dh * xv)
        st_ref[1:2, :] += _rowsum(dh)
        if has_dx:
            dx_ref[...] = ALPHA * dz_ref[...] + dh * scale

        @pl.when(i == nt - 1)
        def _():
            _flush(gw_acc, gw_hbm, sem)

    wide = pl.BlockSpec((tm, D_INNER), lambda i: (i, 0))
    nar = pl.BlockSpec((tm, D_MODEL), lambda i: (i, 0))
    row = pl.BlockSpec((1, D_MODEL), lambda i: (0, 0))
    wspec = pl.BlockSpec((n_blk, D_MODEL, WBLK), lambda i: (0, 0, 0), pipeline_mode=pl.Buffered(1))
    in_specs = ([wide] + ([wide] if has_g else []) + [nar] + ([nar] if has_dx else []) + [row, row, wspec]
                + ([ANY] if has_init else []))
    args = ([du] + ([dg] if has_g else []) + [xin] + ([dzin] if has_dx else []) + [sc, sh, wg]
            + ([gw_init] if has_init else []))
    out_shape = ([jax.ShapeDtypeStruct((t, D_MODEL), F32)] if has_dx else []) + [
        jax.ShapeDtypeStruct((n_blk, D_MODEL, WBLK), F32), jax.ShapeDtypeStruct((SUBLANES, D_MODEL), F32)]
    out_specs = ([nar] if has_dx else []) + [ANY, pl.BlockSpec((SUBLANES, D_MODEL), lambda i: (0, 0))]
    return _call_with_sides(
        body, sides, name=name, out_shape=out_shape, grid=(nt,), in_specs=in_specs, out_specs=out_specs,
        scratch_shapes=[pltpu.VMEM((n_blk, D_MODEL, WBLK), F32), pltpu.SemaphoreType.DMA(())],
        compiler_params=_cparams(dimension_semantics=("arbitrary",)), args=args)


def _blocks_by_device(a, axis):
    shape = a.shape
    a = a.reshape(shape[:axis] + (N_DEV, shape[axis] // N_DEV) + shape[axis + 1:])
    return jnp.moveaxis(a, axis, 0)


def kernel(x, c, ctx, c_ctx, w_mod, b_mod, w_in, w_out, ln_g, ln_b, conv_w, conv_b, lru_wa, lru_ba, lru_wx, lru_bx, lru_lam, pool_w, pool_scale, loss_target, m_c_ctx, m_w_mod, m_b_mod, m_w_in, m_w_out, m_ln_g, m_ln_b, m_conv_w, m_conv_b, m_lru_wa, m_lru_ba, m_lru_wx, m_lru_bx, m_lru_lam, m_pool_w, m_pool_scale, v_c_ctx, v_w_mod, v_b_mod, v_w_in, v_w_out, v_ln_g, v_ln_b, v_conv_w, v_conv_b, v_lru_wa, v_lru_ba, v_lru_wx, v_lru_bx, v_lru_lam, v_pool_w, v_pool_scale):
    xi, yi, ci = _my_pos()
    dev = 4 * xi + 2 * yi + ci
    xt, ctxt, tgt = x[0], ctx[0], loss_target[0]
    n_mod = w_mod.shape[2]

    small_shapes = [(D_MODEL,), conv_w.shape[1:], lru_ba.shape[1:], lru_bx.shape[1:], lru_lam.shape[1:],
                    pool_scale.shape[1:]]
    small = _to_rows([c[0], conv_w[0], lru_ba[0], lru_bx[0], lru_lam[0], pool_scale[0]], SUBLANES)
    small_all, wi0 = _all_gather([small, w_in[0].astype(MXU_DTYPE)], "gather_first")
    pieces = [_split_rows(small_all[k], small_shapes) for k in range(N_DEV)]
    c_all = jnp.stack([p[0] for p in pieces])
    conv_w_f = jnp.concatenate([p[1] for p in pieces], axis=-1)
    lru_ba_f = jnp.concatenate([p[2] for p in pieces], axis=-1)[:, None, :]
    lru_bx_f = jnp.concatenate([p[3] for p in pieces], axis=-1)[:, None, :]
    lru_lam_f = jnp.concatenate([p[4] for p in pieces], axis=-1)[:, None, :]
    pool_scale_f = jnp.concatenate([p[5] for p in pieces], axis=-1)[None, :]

    cond = jnp.concatenate([c_all, jnp.broadcast_to(c_ctx[None, :], (N_DEV, D_MODEL))], axis=0)
    b_my = lax.dynamic_slice(b_mod, (0, dev * n_mod), (2, n_mod))[:, None, :]
    mod_part = _mod_fwd(cond, w_mod, b_my, "mod_fwd")
    mod_all, = _all_gather([mod_part], "gather_mod")
    mod = jnp.transpose(mod_all, (1, 2, 0, 3)).reshape(2, 16, 3 * D_MODEL)
    mod_me = lax.dynamic_slice(mod, (0, dev, 0), (2, 1, 3 * D_MODEL))
    sh = [mod_me[i, :, 0:D_MODEL] for i in range(2)]
    sc = [mod_me[i, :, D_MODEL:2 * D_MODEL] for i in range(2)]
    gt = [mod_me[i, :, 2 * D_MODEL:] for i in range(2)]
    shc, scc = mod[0, 8:9, 0:D_MODEL], mod[0, 8:9, D_MODEL:2 * D_MODEL]

    lg = [ln_g[i][None, :] for i in range(2)]
    lb = [ln_b[i][None, :] for i in range(2)]
    lru_p = dict(conv_w=conv_w_f, conv_b=conv_b, wa=lru_wa[0].astype(MXU_DTYPE), wx=lru_wx[0].astype(MXU_DTYPE),
                 ba=lru_ba_f, bx=lru_bx_f, lam=lru_lam_f)
    zero_state = jnp.zeros((1, D_INNER), F32)

    (u0, g0), (wo0,) = _in_proj(xt, sc[0], sh[0], wi0, "in_proj0", sides=[("gather", [w_out[0].astype(MXU_DTYPE)])])
    (uc, _), _ = _in_proj(ctxt, scc, shc, wi0, "in_proj0_ctx")
    (hcf, cf, uvc), _ = _lru_fwd(uc, zero_state, lru_p, 0, "lru_fwd_ctx_f", conv=True)
    (hcb, cbk), _ = _lru_fwd(uvc, zero_state, lru_p, 1, "lru_fwd_ctx_b", conv=False)
    (hf, _, uv0), (wi1,) = _lru_fwd(u0, cf, lru_p, 0, "lru_fwd_f", conv=True,
                                    sides=[("gather", [w_in[1].astype(MXU_DTYPE)])])
    (hb, _), (wo1, pool_w_g) = _lru_fwd(
        uv0, cbk, lru_p, 1, "lru_fwd_b", conv=False,
        sides=[("gather", [w_out[1].astype(MXU_DTYPE), pool_w[0].astype(MXU_DTYPE)])])
    w_in_l = [wi0, wi1]
    w_out_l = [wo0.reshape(D_INNER, D_MODEL), wo1.reshape(D_INNER, D_MODEL)]
    pool_w_f = jnp.transpose(pool_w_g, (1, 0, 2, 3)).reshape(len(POOL_WINDOWS), POOL_GROUP, POOL_GROUP)
    x1, br0 = _out0(hf, hb, g0, xt, gt[0], w_out_l[0], lg[0], lb[0], "out0")
    (u1, g1), _ = _in_proj(x1, sc[1], sh[1], w_in_l[1], "in_proj1", u_dtype=ACT_DTYPE)
    dmix = _pool_mix(u1, False, MXU_DTYPE, "pool_fwd")
    dz1, st1, po1 = _out1(dmix, pool_w_f, pool_scale_f, g1, x1, gt[1], w_out_l[1], lg[1], lb[1], tgt, "out1")
    loss_me = jnp.full((1, LANES), (0.5 / D_MODEL) * jnp.sum(st1[3]), F32)

    core = jnp.reshape(ci, (1,)).astype(jnp.int32)
    wo_view = lambda a: a.reshape(N_DEV, D_INNER // N_DEV, D_MODEL)
    pw_view = lambda a: _blocks_by_device(a, 1).reshape(N_DEV, POOL_GROUP // N_DEV * len(POOL_WINDOWS), POOL_GROUP)
    dd, dg1, gwo1, gpw, gps = _bout1(dz1, dmix, po1, g1, pool_w_f, pool_scale_f, gt[1], w_out_l[1], "bwd_out1")
    du1 = _pool_mix(dd, True, MXU_DTYPE, "pool_bwd")
    (dx1, gwi1, stb1), _ = _bin(du1, dg1, x1, dz1, sc[1], sh[1], w_in_l[1], "bwd_in1")
    bufs1 = [gwi1, wo_view(gwo1), pw_view(gpw)]
    (dz0, dy0, dg0, gwo0, stl0), recv1 = _bout0(dx1, xt, br0, lg[0], hf, hb, g0, gt[0], w_out_l[0], "bwd_out0",
                                                sides=[("sibling", bufs1)])
    pairs1 = [_pair_sum(b, r, core, "reduce_pair_" + n)
              for b, r, n in zip(bufs1, recv1, ["w_in1", "w_out1", "pool_w"])]
    (duvf, gwa_f, gwx_f, gv_f, dh0f), (p_wi1, p_wo1, p_pw, recv_wo0) = _lru_bwd(
        uv0, dy0, hf, cf, zero_state, lru_p, 0, "lru_bwd_f", sides=[("chips", pairs1), ("sibling", [wo_view(gwo0)])])
    pair_wo0 = _pair_sum(wo_view(gwo0), recv_wo0, core, "reduce_pair_w_out0")
    (duvb, gwa_b, gwx_b, gv_b, dh0b), (p_wo0,) = _lru_bwd(
        uv0, dy0, hb, cbk, zero_state, lru_p, 1, "lru_bwd_b", sides=[("chips", [pair_wo0])])
    zero_dh = jnp.zeros(uc.shape, ACT_DTYPE)
    (ducf, gwa_cf, gwx_cf, gv_cf, _), _ = _lru_bwd(uvc, zero_dh, hcf, zero_state, dh0f, lru_p, 0, "lru_bwd_ctx_f")
    (ducb, gwa_cb, gwx_cb, gv_cb, _), _ = _lru_bwd(uvc, zero_dh, hcb, zero_state, dh0b, lru_p, 1, "lru_bwd_ctx_b")

    def pack(sharded, replicated):
        sh_sizes = [int(np.prod(a.shape[1:])) for a in sharded]
        rep_sizes = [a.shape[0] // N_DEV for a in replicated]
        n_flat = sum(sh_sizes) + sum(rep_sizes)
        rows = -(-(-(-n_flat // LANES)) // FLAT_ROWS) * FLAT_ROWS
        buf = jnp.concatenate([a.reshape(N_DEV, -1) for a in sharded + replicated], axis=1)
        return jnp.pad(buf, ((0, 0), (0, rows * LANES - n_flat))).reshape(N_DEV, rows, LANES), sh_sizes, rep_sizes

    def unpack(reduced, sh_sizes, rep_sizes, sh_shapes):
        flat = reduced.reshape(-1)
        offs = np.cumsum([0] + sh_sizes)
        mine = [flat[offs[k]:offs[k + 1]].reshape(s) for k, s in enumerate(sh_shapes)]
        return mine, _to_rows([flat[offs[-1]:offs[-1] + sum(rep_sizes)]], FLAT_ROWS)

    def spread(rep_all, rep_sizes, shapes):
        flat = rep_all.reshape(N_DEV, -1)
        offs = np.cumsum([0] + rep_sizes)
        return [flat[:, offs[k]:offs[k + 1]].reshape(s) for k, s in enumerate(shapes)]

    (du0, cst0), _ = _conv_bwd(duvf, duvb, u0, conv_w_f, "conv_bwd")
    (duc, cstc), _ = _conv_bwd(ducf, ducb, uc, conv_w_f, "conv_bwd_ctx")
    (gwic, stc), _ = _bin(duc, None, ctxt, None, scc, shc, w_in_l[0][:N_WBLK // 2], "bwd_in0_ctx")
    (gx, gwi0, stb0), _ = _bin(du0, dg0, xt, dz0, sc[0], sh[0], w_in_l[0], "bwd_in0", gw_init=gwic)

    zero_row = jnp.zeros((1, D_MODEL), F32)
    dm_me = jnp.stack([
        jnp.concatenate([jnp.concatenate([stb0[1:2], stb0[0:1], stl0[2:3]], axis=1),
                         jnp.concatenate([stc[1:2], stc[0:1], zero_row], axis=1)], axis=0),
        jnp.concatenate([jnp.concatenate([stb1[1:2], stb1[0:1], st1[2:3]], axis=1),
                         jnp.zeros((1, 3 * D_MODEL), F32)], axis=0)])
    dm_g, loss_g = _all_gather([dm_me, loss_me], "gather_dmod")
    loss = jnp.sum(loss_g[:, 0, 0])
    dm_all = jnp.concatenate([jnp.transpose(dm_g[:, :, 0], (1, 0, 2)), jnp.transpose(dm_g[:, :, 1], (1, 0, 2))],
                             axis=1)
    dm_my = lax.dynamic_slice(dm_all, (0, 0, dev * n_mod), (2, 16, n_mod))
    g_w_mod, g_b_mod, gcc_part = _mod_bwd(cond, dm_all, dm_my, w_mod, "mod_bwd")
    g_b_mod = g_b_mod.reshape(b_mod.shape)

    gwa = jnp.stack([gwa_f + gwa_cf, gwa_b + gwa_cb])
    gwx = jnp.stack([gwx_f + gwx_cf, gwx_b + gwx_cb])
    gv = jnp.stack([gv_f + gv_cf, gv_b + gv_cb])
    cst = cst0 + cstc
    misc, m_sh, m_rep = pack(
        [_blocks_by_device(cst[0:4], 1), _blocks_by_device(gv[:, 0], 1), _blocks_by_device(gv[:, 1], 1),
         _blocks_by_device(gv[:, 2], 1), _blocks_by_device(gps[0], 0)],
        [gwa.reshape(-1), gwx.reshape(-1), jnp.stack([stl0[0], st1[0]]).reshape(-1),
         jnp.stack([stl0[1], st1[1]]).reshape(-1), cst[4], gcc_part.reshape(-1)])
    bufs = [gwi0, misc]
    recvs = _sibling_exchange(bufs, "reduce_sibling")
    pairs = [_pair_sum(b, r, core, "reduce_pair_" + n) for b, r, n in zip(bufs, recvs, ["w_in0", "misc"])]
    p_wi0, p_misc = _chip_exchange(pairs, "reduce_chips")
    (g_conv_w, g_lru_ba, g_lru_bx, g_lru_lam, g_pool_scale), rep_mine = unpack(
        _sum4(p_misc, "reduce_sum_misc"), m_sh, m_rep,
        [conv_w.shape, lru_ba.shape, lru_bx.shape, lru_lam.shape, pool_scale.shape])
    rep_all, = _all_gather([rep_mine.astype(WIRE_DTYPE)], "gather_replicated")
    rep_all = rep_all.astype(F32)
    g_lru_wa, g_lru_wx, g_ln_g, g_ln_b, g_conv_b, g_c_ctx = spread(
        rep_all, m_rep, [lru_wa.shape, lru_wx.shape, ln_g.shape, ln_b.shape, conv_b.shape, c_ctx.shape])

    names = ["c_ctx", "w_mod", "b_mod", "w_in", "w_out", "ln_g", "ln_b", "conv_w", "conv_b", "lru_wa", "lru_ba",
             "lru_wx", "lru_bx", "lru_lam", "pool_w", "pool_scale"]
    weights = dict(c_ctx=c_ctx, w_mod=w_mod, b_mod=b_mod, w_in=w_in, w_out=w_out, ln_g=ln_g, ln_b=ln_b,
                   conv_w=conv_w, conv_b=conv_b, lru_wa=lru_wa, lru_ba=lru_ba, lru_wx=lru_wx, lru_bx=lru_bx,
                   lru_lam=lru_lam, pool_w=pool_w, pool_scale=pool_scale)
    mom_m = dict(c_ctx=m_c_ctx, w_mod=m_w_mod, b_mod=m_b_mod, w_in=m_w_in, w_out=m_w_out, ln_g=m_ln_g, ln_b=m_ln_b,
                 conv_w=m_conv_w, conv_b=m_conv_b, lru_wa=m_lru_wa, lru_ba=m_lru_ba, lru_wx=m_lru_wx,
                 lru_bx=m_lru_bx, lru_lam=m_lru_lam, pool_w=m_pool_w, pool_scale=m_pool_scale)
    mom_v = dict(c_ctx=v_c_ctx, w_mod=v_w_mod, b_mod=v_b_mod, w_in=v_w_in, w_out=v_w_out, ln_g=v_ln_g, ln_b=v_ln_b,
                 conv_w=v_conv_w, conv_b=v_conv_b, lru_wa=v_lru_wa, lru_ba=v_lru_ba, lru_wx=v_lru_wx,
                 lru_bx=v_lru_bx, lru_lam=v_lru_lam, pool_w=v_pool_w, pool_scale=v_pool_scale)
    grads = dict(c_ctx=g_c_ctx, w_mod=g_w_mod, b_mod=g_b_mod, ln_g=g_ln_g, ln_b=g_ln_b,
                 conv_w=g_conv_w, conv_b=g_conv_b, lru_wa=g_lru_wa, lru_ba=g_lru_ba, lru_wx=g_lru_wx,
                 lru_bx=g_lru_bx, lru_lam=g_lru_lam)
    grads["pool_scale"] = g_pool_scale
    delta, new_m, new_v = {}, {}, {}

    def update_parts(n, parts, view):
        res = _adamw_parts(weights[n].reshape(view), parts, mom_m[n].reshape(view), mom_v[n].reshape(view),
                           "adamw_" + n)
        grads[n], delta[n], new_m[n], new_v[n] = [r.reshape(weights[n].shape) for r in res]

    update_parts("w_in", [p_wi0, p_wi1], w_in.shape)
    update_parts("w_out", [p_wo0, p_wo1], w_out.shape)
    update_parts("pool_w", [p_pw], (1,) + p_pw.shape[1:])
    for n in ("w_mod", "lru_wa", "lru_wx"):
        shape = weights[n].shape
        view = (int(np.prod(shape[:-1])), shape[-1])
        res = _adamw(weights[n].reshape(view), grads[n].reshape(view), mom_m[n].reshape(view),
                     mom_v[n].reshape(view), "adamw_" + n)
        delta[n], new_m[n], new_v[n] = [r.reshape(shape) for r in res]

    small = [n for n in names if n not in delta]
    shapes = [weights[n].shape for n in small]
    flat = lambda d: _to_rows([d[n] for n in small], FLAT_ROWS)
    res = _adamw(flat(weights), flat(grads), flat(mom_m), flat(mom_v), "adamw_small")
    for d, r in zip((delta, new_m, new_v), res):
        d.update(zip(small, _split_rows(r, shapes)))

    return (loss, gx[None], *[grads[n] for n in names], *[delta[n] for n in names],
            *[new_m[n] for n in names], *[new_v[n] for n in names])
```

```python
import functools

import numpy as np
import jax
import jax.numpy as jnp
from jax import lax
from jax.experimental import pallas as pl
from jax.experimental.pallas import tpu as pltpu

F32 = jnp.float32
BF16 = jnp.bfloat16
MXU_DTYPE = BF16

D_MODEL = 1024
D_INNER = 2048
LRU_BLOCK = 128
GRID_W = 64
POOL_WINDOWS = (2, 4, 8, 16)
POOL_GROUP = 512
ALPHA = float(4 ** 0.25)
LN_EPS = 1e-5
LRU_C = 8.0
N_DEV = 8
N_WBLK = 8
WBLK = 512

ADAM_LR = 0.001
ADAM_B1 = 0.9
ADAM_B2 = 0.999
ADAM_EPS = 1e-08
ADAM_WD = 0.01
ADAM_STEP = 10

LANES = 128
SUBLANES = 8
V7X_VMEM_BYTES = 64 * 1024 * 1024
VMEM_COMPILER_RESERVE = 8 * 1024 * 1024
VMEM_LIMIT = V7X_VMEM_BYTES - VMEM_COMPILER_RESERVE
MESH = pl.DeviceIdType.MESH
ANY = pl.BlockSpec(memory_space=pl.ANY)

TM_MM = 512
TM_LRU = 1024
CB_LRU = 512
N_SEG = 8
SCAN_UNROLL = 4
SCAN_ROW_T = 17
SCAN_ROW_J = 2
SQRT_FLOOR = 1e-30
FLAT_ROWS = 16
ELEMENTWISE_TILE_BYTES = 1 << 20
POOL_TOK = 256
WIRE_DTYPE = BF16
ACT_DTYPE = BF16
H_HALO = 16


def _cparams(**kw):
    return pltpu.CompilerParams(vmem_limit_bytes=VMEM_LIMIT, **kw)


def _my_pos():
    return lax.axis_index("x"), lax.axis_index("y"), lax.axis_index("c")


def _dot(a, b):
    return jnp.dot(a.astype(MXU_DTYPE), b.astype(MXU_DTYPE), preferred_element_type=F32)


def _dot_tn(a, b):
    return lax.dot_general(a.astype(MXU_DTYPE), b.astype(MXU_DTYPE), (((0,), (0,)), ((), ())),
                           preferred_element_type=F32)


def _dot_nt(a, b):
    return lax.dot_general(a.astype(MXU_DTYPE), b.astype(MXU_DTYPE), (((1,), (1,)), ((), ())),
                           preferred_element_type=F32)


def _sigmoid(z):
    return 0.5 * jnp.tanh(0.5 * z) + 0.5


def _log_sigmoid(x):
    y = jnp.exp(-jnp.abs(x))
    u = 1.0 + y
    l1p = jnp.where(u == 1.0, y, jnp.log(u) * (y / jnp.where(u == 1.0, 1.0, u - 1.0)))
    return jnp.minimum(x, 0.0) - l1p


def _rowsum(v):
    return jnp.sum(v, axis=0, keepdims=True)


def _layer_norm_stats(z):
    mu = jnp.mean(z, axis=-1, keepdims=True)
    zc = z - mu
    var = jnp.mean(zc * zc, axis=-1, keepdims=True)
    rstd = lax.rsqrt(var + LN_EPS)
    return zc * rstd, rstd


def _layer_norm_bwd(dy, xhat, rstd, g):
    dxh = dy * g
    m1 = jnp.mean(dxh, axis=-1, keepdims=True)
    m2 = jnp.mean(dxh * xhat, axis=-1, keepdims=True)
    return rstd * (dxh - m1 - xhat * m2)


def _shifted(v, before8, after8, offsets):
    n = v.shape[0]
    ext = jnp.concatenate([before8, v, after8], axis=0)
    total = n + 2 * SUBLANES
    return [pltpu.roll(ext, (-k) % total, 0)[SUBLANES:SUBLANES + n] for k in offsets]


def _rows8(row):
    return jnp.broadcast_to(row, (SUBLANES, row.shape[1]))


def _shift_down(v, first_row):
    return _shifted(v, _rows8(first_row), _rows8(first_row), [-1])[0]


def _shift_up(v, last_row):
    return _shifted(v, _rows8(last_row), _rows8(last_row), [1])[0]


def _all_gather(blocks, name):
    n = len(blocks)

    def body(*refs):
        x_refs, out_refs = refs[:n], refs[n:2 * n]
        send_sems, recv_sems, local_sems = refs[2 * n:]
        x, y, c = _my_pos()
        me, sibling = (x, y, c), (x, y, 1 - c)
        chips = [(1 - x, y), (x, 1 - y), (1 - x, 1 - y)]

        def slot(a, px, py, pc):
            return out_refs[a].at[4 * px + 2 * py + pc]

        def copy(a, k, block, to, src=None):
            return pltpu.make_async_remote_copy(
                src_ref=slot(a, *block) if src is None else src, dst_ref=slot(a, *block),
                send_sem=send_sems.at[a, k], recv_sem=recv_sems.at[a, k], device_id=to, device_id_type=MESH)

        mine = [pltpu.make_async_copy(x_refs[a], slot(a, *me), local_sems.at[a]) for a in range(n)]
        for cp in mine:
            cp.start()
        first = []
        for a in range(n):
            first.append(copy(a, 0, me, sibling, src=x_refs[a]))
            first += [copy(a, 1 + j, me, (*chip, c), src=x_refs[a]) for j, chip in enumerate(chips)]
        for cp in first:
            cp.start()
        passed = []
        for j, chip in enumerate(chips):
            for a in range(n):
                copy(a, 1 + j, (*chip, c), me).wait_recv()
                fwd = copy(a, 4 + j, (*chip, c), sibling)
                fwd.start()
                passed.append(fwd)
        for a in range(n):
            copy(a, 0, sibling, me).wait_recv()
            for j, chip in enumerate(chips):
                copy(a, 4 + j, (*chip, 1 - c), me).wait_recv()
        for cp in first + passed:
            cp.wait_send()
        for cp in mine:
            cp.wait()

    outs = pl.pallas_call(
        body, name=name,
        out_shape=[jax.ShapeDtypeStruct((N_DEV,) + b.shape, b.dtype) for b in blocks],
        in_specs=[ANY] * n, out_specs=[ANY] * n,
        scratch_shapes=[pltpu.SemaphoreType.DMA((n, 7)), pltpu.SemaphoreType.DMA((n, 7)),
                        pltpu.SemaphoreType.DMA((n,))],
    )(*blocks)
    return list(outs)


def _sibling_exchange(bufs, name):
    n = len(bufs)

    def body(*refs):
        srcs, outs = refs[:n], refs[n:2 * n]
        send_sems, recv_sems = refs[2 * n:]
        x, y, c = _my_pos()
        copies = [pltpu.make_async_remote_copy(
            src_ref=srcs[a].at[2 * j + (1 - c)], dst_ref=outs[a].at[j], send_sem=send_sems.at[a, j],
            recv_sem=recv_sems.at[a, j], device_id=(x, y, 1 - c), device_id_type=MESH)
            for a in range(n) for j in range(4)]
        for cp in copies:
            cp.start()
        for cp in copies:
            cp.wait()

    outs = pl.pallas_call(
        body, name=name, out_shape=[jax.ShapeDtypeStruct((4,) + b.shape[1:], b.dtype) for b in bufs],
        in_specs=[ANY] * n, out_specs=[ANY] * n,
        scratch_shapes=[pltpu.SemaphoreType.DMA((n, 4)), pltpu.SemaphoreType.DMA((n, 4))],
    )(*bufs)
    return list(outs)


def _chip_exchange(parts, name):
    n = len(parts)

    def body(*refs):
        srcs, outs = refs[:n], refs[n:2 * n]
        send_sems, recv_sems, local_sems = refs[2 * n:]
        x, y, c = _my_pos()
        jme = 2 * x + y
        peers = [(1 - x, y), (x, 1 - y), (1 - x, 1 - y)]
        local = [pltpu.make_async_copy(srcs[a].at[jme], outs[a].at[jme], local_sems.at[a]) for a in range(n)]
        for cp in local:
            cp.start()

        def copy(a, k, px, py, dst_slot):
            return pltpu.make_async_remote_copy(
                src_ref=srcs[a].at[2 * px + py], dst_ref=outs[a].at[dst_slot], send_sem=send_sems.at[a, k],
                recv_sem=recv_sems.at[a, k], device_id=(px, py, c), device_id_type=MESH)

        sends = [copy(a, k, px, py, jme) for a in range(n) for k, (px, py) in enumerate(peers)]
        for cp in sends:
            cp.start()
        for a in range(n):
            for k, (px, py) in enumerate(peers):
                copy(a, k, px, py, 2 * px + py).wait_recv()
        for cp in sends:
            cp.wait_send()
        for cp in local:
            cp.wait()

    outs = pl.pallas_call(
        body, name=name, out_shape=[jax.ShapeDtypeStruct(p.shape, p.dtype) for p in parts],
        in_specs=[ANY] * n, out_specs=[ANY] * n,
        scratch_shapes=[pltpu.SemaphoreType.DMA((n, 3)), pltpu.SemaphoreType.DMA((n, 3)),
                        pltpu.SemaphoreType.DMA((n,))],
    )(*parts)
    return list(outs)


_SIDE_REMOTE = {"gather": 7, "sibling": 4, "chips": 3}
_FLIPS = [(0, 0, 1), (1, 0, 0), (0, 1, 0), (1, 1, 0), (1, 0, 1), (0, 1, 1), (1, 1, 1)]


def _side_plan(sides):
    inputs, out_shapes, scratch = [], [], []
    for kind, arrays in sides:
        n = len(arrays)
        for a in arrays:
            inputs.append(a)
            shape = {"gather": (N_DEV,) + a.shape, "sibling": (4,) + a.shape[1:], "chips": a.shape}[kind]
            out_shapes.append(jax.ShapeDtypeStruct(shape, a.dtype))
        scratch += [pltpu.SemaphoreType.DMA((n, _SIDE_REMOTE[kind])), pltpu.SemaphoreType.DMA((n, _SIDE_REMOTE[kind])),
                    pltpu.SemaphoreType.DMA((n,))]
    return inputs, out_shapes, scratch


def _side_copies(sides, in_refs, out_refs, sem_refs):
    x, y, c = _my_pos()
    starts, waits = [], []
    pos = 0
    for s, (kind, arrays) in enumerate(sides):
        send_sems, recv_sems, local_sems = sem_refs[3 * s:3 * s + 3]
        for a in range(len(arrays)):
            src, out = in_refs[pos], out_refs[pos]
            pos += 1

            def remote(k, src_ref, dst_ref, to):
                return pltpu.make_async_remote_copy(src_ref=src_ref, dst_ref=dst_ref, send_sem=send_sems.at[a, k],
                                                    recv_sem=recv_sems.at[a, k], device_id=to, device_id_type=MESH)

            def local(src_ref, dst_ref):
                cp = pltpu.make_async_copy(src_ref, dst_ref, local_sems.at[a])
                starts.append(cp.start)
                waits.append(cp.wait)

            if kind == "gather":
                me = 4 * x + 2 * y + c
                local(src, out.at[me])
                for k, (fx, fy, fc) in enumerate(_FLIPS):
                    px, py, pc = (1 - x if fx else x), (1 - y if fy else y), (1 - c if fc else c)
                    send = remote(k, src, out.at[me], (px, py, pc))
                    starts.append(send.start)
                    waits += [remote(k, src, out.at[4 * px + 2 * py + pc], (px, py, pc)).wait_recv, send.wait_send]
            elif kind == "sibling":
                for j in range(4):
                    cp = remote(j, src.at[2 * j + (1 - c)], out.at[j], (x, y, 1 - c))
                    starts.append(cp.start)
                    waits.append(cp.wait)
            else:
                jme = 2 * x + y
                local(src.at[jme], out.at[jme])
                for k, (px, py) in enumerate([(1 - x, y), (x, 1 - y), (1 - x, 1 - y)]):
                    send = remote(k, src.at[2 * px + py], out.at[jme], (px, py, c))
                    starts.append(send.start)
                    waits += [remote(k, src.at[2 * px + py], out.at[2 * px + py], (px, py, c)).wait_recv,
                              send.wait_send]
    return starts, waits


def _call_with_sides(body, sides, *, name, grid, in_specs, out_specs, out_shape, scratch_shapes, compiler_params, args):
    if not sides:
        res = pl.pallas_call(body, name=name, grid=grid, in_specs=in_specs, out_specs=out_specs, out_shape=out_shape,
                             scratch_shapes=scratch_shapes, compiler_params=compiler_params)(*args)
        return list(res), []
    s_in, s_out, s_scr = _side_plan(sides)
    n_in, n_out, n_scr, n_side = len(in_specs), len(out_specs), len(scratch_shapes), len(s_in)

    def wrapped(*refs):
        refs = list(refs)
        ins, side_in = refs[:n_in], refs[n_in:n_in + n_side]
        outs = refs[n_in + n_side:n_in + n_side + n_out]
        side_out = refs[n_in + n_side + n_out:n_in + 2 * n_side + n_out]
        rest = refs[n_in + 2 * n_side + n_out:]
        starts, waits = _side_copies(sides, side_in, side_out, rest[n_scr:])
        first = functools.reduce(jnp.logical_and, [pl.program_id(d) == 0 for d in range(len(grid))])
        last = functools.reduce(jnp.logical_and, [pl.program_id(d) == grid[d] - 1 for d in range(len(grid))])

        @pl.when(first)
        def _():
            for start in starts:
                start()

        body(*ins, *outs, *rest[:n_scr])

        @pl.when(last)
        def _():
            for wait in waits:
                wait()

    res = pl.pallas_call(
        wrapped, name=name, grid=grid, in_specs=list(in_specs) + [ANY] * n_side,
        out_specs=list(out_specs) + [ANY] * n_side, out_shape=list(out_shape) + s_out,
        scratch_shapes=list(scratch_shapes) + s_scr, compiler_params=compiler_params,
    )(*args, *s_in)
    return list(res[:n_out]), list(res[n_out:])


def _row_tile(r, l):
    t = min(r, max(16, ELEMENTWISE_TILE_BYTES // (4 * l) // 16 * 16))
    while r % t:
        t -= 16
    return t


def _pair_sum(buf, recv, core, name):
    _, r, l = buf.shape
    tr = _row_tile(r, l)

    def body(core_ref, a_ref, b_ref, o_ref):
        o_ref[...] = (a_ref[...] + b_ref[...]).astype(WIRE_DTYPE)

    return pl.pallas_call(
        body, name=name, out_shape=jax.ShapeDtypeStruct((4, r, l), WIRE_DTYPE),
        grid_spec=pltpu.PrefetchScalarGridSpec(
            num_scalar_prefetch=1, grid=(4, r // tr),
            in_specs=[pl.BlockSpec((None, tr, l), lambda j, i, cr: (2 * j + cr[0], i, 0)),
                      pl.BlockSpec((None, tr, l), lambda j, i, cr: (j, i, 0))],
            out_specs=pl.BlockSpec((None, tr, l), lambda j, i, cr: (j, i, 0))),
        compiler_params=_cparams(dimension_semantics=("arbitrary", "arbitrary")),
    )(core, buf, recv)


def _sum_parts(p_ref):
    return ((p_ref[0].astype(F32) + p_ref[1].astype(F32)) + (p_ref[2].astype(F32) + p_ref[3].astype(F32)))


def _sum4(parts, name):
    _, r, l = parts.shape
    tr = _row_tile(r, l)

    def body(p_ref, o_ref):
        o_ref[...] = _sum_parts(p_ref)

    return pl.pallas_call(
        body, name=name, out_shape=jax.ShapeDtypeStruct((r, l), F32), grid=(r // tr,),
        in_specs=[pl.BlockSpec((4, tr, l), lambda i: (0, i, 0))],
        out_specs=pl.BlockSpec((tr, l), lambda i: (i, 0)),
        compiler_params=_cparams(dimension_semantics=("arbitrary",)),
    )(parts)


def _adamw_update(w, gg, m, v):
    nm = ADAM_B1 * m + (1.0 - ADAM_B1) * gg
    nv = ADAM_B2 * v + (1.0 - ADAM_B2) * (gg * gg)
    m_hat = nm / (1.0 - ADAM_B1 ** ADAM_STEP)
    v_hat = nv / (1.0 - ADAM_B2 ** ADAM_STEP)
    return -ADAM_LR * (m_hat / (jnp.sqrt(v_hat) + ADAM_EPS) + ADAM_WD * w), nm, nv


def _adamw(w, g, m, v, name):
    r, l = w.shape
    tr = _row_tile(r, l)

    def body(w_ref, g_ref, m_ref, v_ref, d_ref, nm_ref, nv_ref):
        d_ref[...], nm_ref[...], nv_ref[...] = _adamw_update(w_ref[...], g_ref[...], m_ref[...], v_ref[...])

    spec = pl.BlockSpec((tr, l), lambda i: (i, 0))
    return pl.pallas_call(
        body, name=name, out_shape=[jax.ShapeDtypeStruct((r, l), F32)] * 3, grid=(r // tr,),
        in_specs=[spec] * 4, out_specs=[spec] * 3,
        compiler_params=_cparams(dimension_semantics=("arbitrary",)),
    )(w, g, m, v)


def _adamw_parts(w, parts, m, v, name):
    nl, r, l = w.shape
    tr = _row_tile(r, l)

    def body(*refs):
        w_ref, p_refs, (m_ref, v_ref, g_ref, d_ref, nm_ref, nv_ref) = refs[0], refs[1:1 + nl], refs[1 + nl:]
        layer = pl.program_id(0)
        gg = _sum_parts(p_refs[0])
        for q in range(1, nl):
            gg = jnp.where(layer == q, _sum_parts(p_refs[q]), gg)
        g_ref[...] = gg
        d_ref[...], nm_ref[...], nv_ref[...] = _adamw_update(w_ref[...], gg, m_ref[...], v_ref[...])

    spec = pl.BlockSpec((None, tr, l), lambda q, i: (q, i, 0))
    pspecs = [pl.BlockSpec((4, tr, l), lambda q, i, k=k: (0, jnp.where(q == k, i, 0), 0)) for k in range(nl)]
    return pl.pallas_call(
        body, name=name, out_shape=[jax.ShapeDtypeStruct((nl, r, l), F32)] * 4, grid=(nl, r // tr),
        in_specs=[spec] + pspecs + [spec, spec], out_specs=[spec] * 4,
        compiler_params=_cparams(dimension_semantics=("arbitrary", "arbitrary")),
    )(w, *parts, m, v)


def _to_rows(pieces, row_multiple):
    flat = jnp.concatenate([p.reshape(-1) for p in pieces])
    rows = -(-flat.shape[0] // LANES)
    rows = -(-rows // row_multiple) * row_multiple
    flat = jnp.pad(flat, (0, rows * LANES - flat.shape[0]))
    return flat.reshape(rows, LANES)


def _split_rows(rows, shapes):
    flat = rows.reshape(-1)
    out, off = [], 0
    for s in shapes:
        n = int(np.prod(s))
        out.append(flat[off:off + n].reshape(s))
        off += n
    return out


def _mod_fwd(cond, w_mod, b_my, name):
    nl, _, ncol = w_mod.shape

    def body(a_ref, w_ref, b_ref, o_ref):
        a = a_ref[...]
        s = a * _sigmoid(a)
        for i in range(nl):
            o_ref[i] = _dot(s, w_ref[i]) + b_ref[i]

    return pl.pallas_call(
        body, name=name, out_shape=jax.ShapeDtypeStruct((nl, 16, ncol), F32),
        compiler_params=_cparams(),
    )(cond, w_mod, b_my)


def _mod_bwd(cond, dm_all, dm_my, w_mod, name):
    nl, _, ncol = w_mod.shape

    def body(a_ref, dma_ref, dmm_ref, w_ref, gw_ref, gb_ref, gc_ref):
        a = a_ref[...]
        sg = _sigmoid(a)
        s = a * sg
        for i in range(nl):
            gw_ref[i] = _dot_tn(s, dmm_ref[i])
            gb_ref[i] = jnp.sum(dma_ref[i], axis=0, keepdims=True)
        back = _dot_nt(dmm_ref[0], w_ref[0])
        dsilu = sg * (1.0 + a * (1.0 - sg))
        gc_ref[...] = jnp.sum(back[8:16] * dsilu[8:16], axis=0, keepdims=True)

    return pl.pallas_call(
        body, name=name,
        out_shape=[jax.ShapeDtypeStruct((nl, D_MODEL, ncol), F32), jax.ShapeDtypeStruct((nl, 1, 3 * D_MODEL), F32),
                   jax.ShapeDtypeStruct((1, D_MODEL), F32)],
        compiler_params=_cparams(),
    )(cond, dm_all, dm_my, w_mod)


def _in_proj(xt, sc, sh, wg, name, sides=(), u_dtype=F32):
    t = xt.shape[0]
    tm = min(TM_MM, t)

    def body(x_ref, sc_ref, sh_ref, w_ref, u_ref, g_ref):
        h = (x_ref[...] * (1.0 + sc_ref[...]) + sh_ref[...]).astype(MXU_DTYPE)
        for k in range(N_WBLK):
            o = jnp.dot(h, w_ref[k], preferred_element_type=F32)
            if k < N_WBLK // 2:
                u_ref[:, k * WBLK:(k + 1) * WBLK] = o.astype(u_dtype)
            else:
                kk = k - N_WBLK // 2
                g_ref[:, kk * WBLK:(kk + 1) * WBLK] = o.astype(ACT_DTYPE)

    row = pl.BlockSpec((1, D_MODEL), lambda i: (0, 0))
    return _call_with_sides(
        body, sides, name=name,
        out_shape=[jax.ShapeDtypeStruct((t, D_INNER), u_dtype), jax.ShapeDtypeStruct((t, D_INNER), ACT_DTYPE)],
        grid=(t // tm,),
        in_specs=[pl.BlockSpec((tm, D_MODEL), lambda i: (i, 0)), row, row,
                  pl.BlockSpec((N_WBLK, D_MODEL, WBLK), lambda i: (0, 0, 0), pipeline_mode=pl.Buffered(1))],
        out_specs=[pl.BlockSpec((tm, D_INNER), lambda i: (i, 0))] * 2, scratch_shapes=[],
        compiler_params=_cparams(dimension_semantics=("arbitrary",)), args=[xt, sc, sh, wg])


def _halo_maps(nt, tm, n_blocks, pos, rows=SUBLANES):
    per = tm // rows
    prev = lambda cb, i: (jnp.maximum(pos(i) * per - 1, 0), cb)
    nxt = lambda cb, i: (jnp.minimum((pos(i) + 1) * per, n_blocks - 1), cb)
    return prev, nxt


def _conv_taps(u, prev8, next8, is_first, is_last):
    pz = jnp.where(is_first, 0.0, 1.0)
    nz = jnp.where(is_last, 0.0, 1.0)
    return _shifted(u, prev8 * pz, next8 * nz, [-2, -1, 1])


def _lru_gates(uv, wa_ref, wx_ref, ba, bx, cl, g):
    sl = slice(g * LANES, (g + 1) * LANES)
    uvg = uv[:, sl]
    r = _sigmoid(_dot(uvg, wa_ref[g]) + ba[:, sl])
    ii = _sigmoid(_dot(uvg, wx_ref[g]) + bx[:, sl])
    la = cl[:, sl] * r
    a = jnp.exp(la)
    q = jnp.tanh(-la) * (1.0 + a * a)
    rs = lax.rsqrt(jnp.maximum(q, SQRT_FLOOR))
    return uvg, r, ii, a, q * rs, rs


def _scan_rows(seg):
    return -(-(SCAN_ROW_T * (seg - 1) + SCAN_ROW_J * (N_SEG - 1) + 1) // SUBLANES) * SUBLANES


def _seg_chunk(j, c):
    return pl.ds(SCAN_ROW_T * SUBLANES * c + SCAN_ROW_J * j, SUBLANES, stride=SCAN_ROW_T)


def _seg_scatter(ref, g, seg, value):
    for j in range(N_SEG):
        for c in range(seg // SUBLANES):
            r0 = j * seg + SUBLANES * c
            ref[g, _seg_chunk(j, c), :] = value[r0:r0 + SUBLANES]


def _scan_tile(a_s, b_s, carry_ref, write_out, seg, reverse, chunks_per_write=1):
    n_g = a_s.shape[0]
    unroll = SCAN_UNROLL if seg % SCAN_UNROLL == 0 else 1

    n_trips = seg // unroll

    def steps(k, state):
        hs, cs = list(state[0]), list(state[1])
        base = ((n_trips - 1 - k) if reverse else k) * unroll
        for q in (range(unroll - 1, -1, -1) if reverse else range(unroll)):
            t = base + q
            rows = pl.ds(t * SCAN_ROW_T, N_SEG, stride=SCAN_ROW_J)
            for g in range(n_g):
                a = a_s[g, rows, :]
                b = b_s[g, rows, :]
                hs[g] = a * hs[g] + b
                cs[g] = a * cs[g]
                b_s[g, rows, :] = hs[g]
                a_s[g, rows, :] = cs[g]
        return tuple(hs), tuple(cs)

    zeros = tuple(jnp.zeros((N_SEG, LANES), F32) for _ in range(n_g))
    ones = tuple(jnp.ones((N_SEG, LANES), F32) for _ in range(n_g))
    h_fin, a_fin = lax.fori_loop(0, seg // unroll, steps, (zeros, ones))

    order = list(range(N_SEG - 1, -1, -1)) if reverse else list(range(N_SEG))
    for g in range(n_g):
        carry = carry_ref[:, g * LANES:(g + 1) * LANES]
        for j in order:
            for c0 in range(0, seg // SUBLANES, chunks_per_write):
                parts = [b_s[g, _seg_chunk(j, c), :] + a_s[g, _seg_chunk(j, c), :] * carry
                         for c in range(c0, c0 + chunks_per_write)]
                write_out(j, c0, g, parts[0] if chunks_per_write == 1 else jnp.concatenate(parts, axis=0))
            carry = a_fin[g][j:j + 1] * carry + h_fin[g][j:j + 1]
        carry_ref[:, g * LANES:(g + 1) * LANES] = carry


def _lru_specs(s, tm, cb, direction_pos, nt):
    n_rows8 = s // SUBLANES
    prev, nxt = _halo_maps(nt, tm, n_rows8, direction_pos)
    tile = pl.BlockSpec((tm, cb), lambda c, i: (direction_pos(i), c))
    return tile, pl.BlockSpec((SUBLANES, cb), prev), pl.BlockSpec((SUBLANES, cb), nxt)


def _lru_param_specs(cb, d):
    n_g = cb // LANES
    vec = pl.BlockSpec((1, cb), lambda c, i: (0, c))
    dvec = pl.BlockSpec((None, 1, cb), lambda c, i: (d, 0, c))
    wmat = pl.BlockSpec((None, n_g, LRU_BLOCK, LRU_BLOCK), lambda c, i: (d, c, 0, 0))
    return vec, dvec, wmat


def _lru_fwd(src, h0, p, d, name, conv, sides=()):
    s = src.shape[0]
    tm = min(TM_LRU, s)
    cb = CB_LRU
    n_g = cb // LANES
    nt = s // tm
    seg = tm // N_SEG
    pos = (lambda i: i) if d == 0 else (lambda i: nt - 1 - i)

    def body(*refs):
        refs = list(refs)
        u_ref = refs.pop(0)
        if conv:
            up_ref, un_ref, cw_ref, cbias_ref = [refs.pop(0) for _ in range(4)]
        wa_ref, wx_ref, ba_ref, bx_ref, lam_ref, h0_ref, h_ref, hc_ref = [refs.pop(0) for _ in range(8)]
        uv_ref = refs.pop(0) if conv else None
        a_s, b_s = refs
        i = pl.program_id(1)
        tp = pos(i)

        @pl.when(i == 0)
        def _():
            hc_ref[...] = h0_ref[...]

        if conv:
            u_t = u_ref[...]
            um2, um1, up1 = _conv_taps(u_t, up_ref[...], un_ref[...], tp == 0, tp == nt - 1)
            cw = cw_ref[...]
            uv_ref[...] = um2 * cw[0:1] + um1 * cw[1:2] + u_t * cw[2:3] + up1 * cw[3:4] + cbias_ref[...]
        src_ref = uv_ref if conv else u_ref
        cl = LRU_C * _log_sigmoid(lam_ref[...])
        ba, bx = ba_ref[...], bx_ref[...]
        for g in range(n_g):
            uvg, r, ii, a, sq, _ = _lru_gates(src_ref, wa_ref, wx_ref, ba, bx, cl, g)
            b = sq * (ii * uvg)
            _seg_scatter(a_s, g, seg, a)
            _seg_scatter(b_s, g, seg, b)

        per_write = 2 if (seg // SUBLANES) % 2 == 0 else 1

        def write_out(j, c, g, h):
            h_ref[pl.ds(j * seg + SUBLANES * c, SUBLANES * per_write), pl.ds(g * LANES, LANES)] = h.astype(ACT_DTYPE)

        _scan_tile(a_s, b_s, hc_ref, write_out, seg, reverse=(d == 1), chunks_per_write=per_write)

    tile, prev, nxt = _lru_specs(s, tm, cb, pos, nt)
    vec, dvec, wmat = _lru_param_specs(cb, d)
    wide = jax.ShapeDtypeStruct((s, D_INNER), F32)
    conv_specs = [prev, nxt, pl.BlockSpec((4, cb), lambda c, i: (0, c)), vec] if conv else []
    conv_args = [src, src, p["conv_w"], p["conv_b"]] if conv else []
    return _call_with_sides(
        body, sides, name=name,
        out_shape=[jax.ShapeDtypeStruct((s, D_INNER), ACT_DTYPE), jax.ShapeDtypeStruct((1, D_INNER), F32)]
        + ([wide] if conv else []),
        grid=(D_INNER // cb, nt),
        in_specs=[tile] + conv_specs + [wmat, wmat, dvec, dvec, dvec, vec],
        out_specs=[tile, vec] + ([tile] if conv else []),
        scratch_shapes=[pltpu.VMEM((n_g, _scan_rows(seg), LANES), F32)] * 2,
        compiler_params=_cparams(dimension_semantics=("arbitrary", "arbitrary")),
        args=[src, *conv_args, p["wa"], p["wx"], p["ba"], p["bx"], p["lam"], h0])


def _lru_bwd(uv, dh, h, h0, lam_in, p, d, name, sides=()):
    s = uv.shape[0]
    tm = min(TM_LRU, s)
    cb = CB_LRU
    n_g = cb // LANES
    nt = s // tm
    seg = tm // N_SEG
    pos = (lambda i: nt - 1 - i) if d == 0 else (lambda i: i)

    def body(uv_ref, dh_ref, h_ref, hh_ref, wa_ref, wx_ref, ba_ref, bx_ref,
             lam_ref, h0_ref, lin_ref, duv_ref, gwa_ref, gwx_ref, gv_ref, lc_ref, a_s, b_s, lp_s,
             r_s, i_s, q_s, rq_s, a_keep):
        i = pl.program_id(1)
        tp = pos(i)

        @pl.when(i == 0)
        def _():
            lc_ref[...] = lin_ref[...]
            gwa_ref[...] = jnp.zeros_like(gwa_ref)
            gwx_ref[...] = jnp.zeros_like(gwx_ref)
            gv_ref[...] = jnp.zeros_like(gv_ref)

        uv = uv_ref[...]
        lam = lam_ref[...]
        cl = LRU_C * _log_sigmoid(lam)
        ba, bx = ba_ref[...], bx_ref[...]
        dh_t = dh_ref[...].astype(F32)
        carry_in = lc_ref[...]
        for g in range(n_g):
            sl = slice(g * LANES, (g + 1) * LANES)
            _, r, ii, a, sq, rs = _lru_gates(uv, wa_ref, wx_ref, ba, bx, cl, g)
            r_s[:, sl], i_s[:, sl], q_s[:, sl], rq_s[:, sl], a_keep[:, sl] = r, ii, sq, rs, a
            b = a * dh_t[:, sl]
            _seg_scatter(a_s, g, seg, a)
            _seg_scatter(b_s, g, seg, b)

        def write_out(j, c, g, v):
            lp_s[pl.ds(j * seg + SUBLANES * c, SUBLANES), pl.ds(g * LANES, LANES)] = v

        _scan_tile(a_s, b_s, lc_ref, write_out, seg, reverse=(d == 0))

        h_t = h_ref[...].astype(F32)
        hh = hh_ref[...].astype(F32)
        if d == 0:
            edge = jnp.where(tp == 0, h0_ref[...], hh[H_HALO - 1:H_HALO])
            h_prev = _shift_down(h_t, edge)
            lam_t = dh_t + _shift_up(lp_s[...], carry_in)
        else:
            edge = jnp.where(tp == nt - 1, h0_ref[...], hh[0:1])
            h_prev = _shift_up(h_t, edge)
            lam_t = dh_t + _shift_down(lp_s[...], carry_in)

        dsig = LRU_C * _sigmoid(-lam)
        for g in range(n_g):
            sl = slice(g * LANES, (g + 1) * LANES)
            uvg, r, ii, a, sq = uv[:, sl], r_s[:, sl], i_s[:, sl], a_keep[:, sl], q_s[:, sl]
            lt = lam_t[:, sl]
            ls = lt * sq
            dla = (lt * a) * (h_prev[:, sl] - (ii * uvg) * (a * rq_s[:, sl]))
            dzr = (dla * cl[:, sl]) * r * (1.0 - r)
            dzi = (ls * uvg) * ii * (1.0 - ii)
            duv_ref[:, sl] = (ls * ii + _dot_nt(dzr, wa_ref[g]) + _dot_nt(dzi, wx_ref[g])).astype(ACT_DTYPE)
            gwa_ref[g] += _dot_tn(uvg, dzr)
            gwx_ref[g] += _dot_tn(uvg, dzi)
            gv_ref[0:1, sl] += _rowsum(dzr)
            gv_ref[1:2, sl] += _rowsum(dzi)
            gv_ref[2:3, sl] += _rowsum(dla * r) * dsig[:, sl]

    tile, _, _ = _lru_specs(s, tm, cb, pos, nt)
    vec, dvec, wmat = _lru_param_specs(cb, d)
    h_prev_map, h_next_map = _halo_maps(nt, tm, s // H_HALO, pos, rows=H_HALO)
    hh_spec = pl.BlockSpec((H_HALO, cb), h_prev_map if d == 0 else h_next_map)
    gw_spec = pl.BlockSpec((n_g, LRU_BLOCK, LRU_BLOCK), lambda c, i: (c, 0, 0))
    n_blk = D_INNER // LRU_BLOCK
    return _call_with_sides(
        body, sides, name=name,
        out_shape=[jax.ShapeDtypeStruct((s, D_INNER), ACT_DTYPE),
                   jax.ShapeDtypeStruct((n_blk, LRU_BLOCK, LRU_BLOCK), F32),
                   jax.ShapeDtypeStruct((n_blk, LRU_BLOCK, LRU_BLOCK), F32),
                   jax.ShapeDtypeStruct((SUBLANES, D_INNER), F32),
                   jax.ShapeDtypeStruct((1, D_INNER), F32)],
        grid=(D_INNER // cb, nt),
        in_specs=[tile, tile, tile, hh_spec, wmat, wmat, dvec, dvec, dvec, vec, vec],
        out_specs=[tile, gw_spec, gw_spec, pl.BlockSpec((SUBLANES, cb), lambda c, i: (0, c)), vec],
        scratch_shapes=[pltpu.VMEM((n_g, _scan_rows(seg), LANES), F32)] * 2 + [pltpu.VMEM((tm, cb), F32)] * 6,
        compiler_params=_cparams(dimension_semantics=("arbitrary", "arbitrary")),
        args=[uv, dh, h, h, p["wa"], p["wx"], p["ba"], p["bx"], p["lam"], h0, lam_in])


def _out0(hf, hb, g, xt, gt, wo, lg, lb, name):
    t = xt.shape[0]
    tm = min(TM_MM, t)

    def body(hf_ref, hb_ref, g_ref, x_ref, gt_ref, w_ref, lg_ref, lb_ref, x1_ref, br_ref):
        br = None
        for k in range(D_INNER // WBLK):
            sl = slice(k * WBLK, (k + 1) * WBLK)
            gg = g_ref[:, sl].astype(F32)
            p = (hf_ref[:, sl].astype(F32) + hb_ref[:, sl].astype(F32)) * (gg * _sigmoid(gg))
            part = _dot(p, w_ref[sl, :])
            br = part if br is None else br + part
        z = ALPHA * x_ref[...] + gt_ref[...] * br
        xhat, _ = _layer_norm_stats(z)
        x1_ref[...] = xhat * lg_ref[...] + lb_ref[...]
        br_ref[...] = br.astype(ACT_DTYPE)

    wide = pl.BlockSpec((tm, D_INNER), lambda i: (i, 0))
    nar = pl.BlockSpec((tm, D_MODEL), lambda i: (i, 0))
    row = pl.BlockSpec((1, D_MODEL), lambda i: (0, 0))
    return pl.pallas_call(
        body, name=name,
        out_shape=[jax.ShapeDtypeStruct((t, D_MODEL), F32), jax.ShapeDtypeStruct((t, D_MODEL), ACT_DTYPE)],
        grid=(t // tm,),
        in_specs=[wide, wide, wide, nar, row,
                  pl.BlockSpec((D_INNER, D_MODEL), lambda i: (0, 0), pipeline_mode=pl.Buffered(1)), row, row],
        out_specs=[nar, nar],
        compiler_params=_cparams(dimension_semantics=("arbitrary",)),
    )(hf, hb, g, xt, gt, wo, lg, lb)


def _unrolled_loop(n, fn, unroll=4):
    while n % unroll:
        unroll //= 2

    def trip(k, carry):
        for q in range(unroll):
            fn(k * unroll + q)
        return carry
    lax.fori_loop(0, n // unroll, trip, 0)


def _window(n, w):
    t = np.arange(n)
    return np.clip(t - w // 2, 0, n), np.clip(t + w // 2, 0, n)


def _pool_tables(n_rows, transpose):
    boxes, inv_c, inv_r = [], [], []
    for w in POOL_WINDOWS:
        lo, hi = _window(GRID_W, w)
        m = np.zeros((GRID_W, GRID_W), np.float32)
        for r in range(GRID_W):
            m[r, lo[r]:hi[r]] = 1.0
        m = np.kron(np.eye(POOL_TOK // GRID_W, dtype=np.float32), m)
        boxes.append(m.T if transpose else m)
        inv_c.append(np.broadcast_to((1.0 / (hi - lo).astype(np.float32))[:, None], (GRID_W, LANES)))
        lo_r, hi_r = _window(n_rows, w)
        inv_r.append(1.0 / (hi_r - lo_r).astype(np.float32))
    return (jnp.asarray(np.stack(boxes), MXU_DTYPE), jnp.asarray(np.stack(inv_c), F32),
            jnp.asarray(np.stack(inv_r), F32))


def _pool_mix(xin, transpose, out_dtype, name):
    s = xin.shape[0]
    n_rows = s // GRID_W
    pad_t = SUBLANES * GRID_W
    rows_per_blk = POOL_TOK // GRID_W
    n_slab = D_INNER // LANES
    slabs_per_group = POOL_GROUP // LANES
    n_win = len(POOL_WINDOWS)
    boxes, inv_c, inv_r = _pool_tables(n_rows, transpose)
    exact_operand = (not transpose) and xin.dtype == MXU_DTYPE and MXU_DTYPE != F32

    def body(invr_ref, box_ref, invc_ref, x_ref, o_ref, pad_s):
        k = pl.program_id(0) // slabs_per_group
        pad_s[pl.ds(0, pad_t), :] = jnp.zeros((pad_t, LANES), F32)
        pad_s[pl.ds(pad_t + s, pad_t), :] = jnp.zeros((pad_t, LANES), F32)

        for kk, w in enumerate(POOL_WINDOWS):
            half = w // 2
            offsets = list(range(-(half - 1), half + 1)) if transpose else list(range(-half, half))

            @pl.when(k == kk)
            def _():
                inv_col = invc_ref[kk]

                def col_box(b):
                    st = pl.multiple_of(b * POOL_TOK, POOL_TOK)
                    xb = x_ref[pl.ds(st, POOL_TOK), :]
                    if exact_operand:
                        pad_s[pl.ds(pad_t + st, POOL_TOK), :] = jnp.dot(box_ref[kk], xb, preferred_element_type=F32)
                        return
                    xb = xb.astype(F32)
                    if transpose:
                        xb = xb * jnp.concatenate(
                            [inv_col * invr_ref[kk, b * rows_per_blk + q] for q in range(rows_per_blk)], axis=0)
                    hi = xb.astype(MXU_DTYPE)
                    lo = (xb - hi.astype(F32)).astype(MXU_DTYPE)
                    both = jnp.dot(box_ref[kk], jnp.concatenate([hi, lo], axis=1), preferred_element_type=F32)
                    pad_s[pl.ds(pad_t + st, POOL_TOK), :] = both[:, :LANES] + both[:, LANES:]
                _unrolled_loop(s // POOL_TOK, col_box)

                def row_box(r):
                    st = pl.multiple_of(r * GRID_W, GRID_W)
                    acc = pad_s[pl.ds(pad_t + st + offsets[0] * GRID_W, GRID_W), :]
                    for o in offsets[1:]:
                        acc = acc + pad_s[pl.ds(pad_t + st + o * GRID_W, GRID_W), :]
                    if not transpose:
                        acc = acc * (inv_col * invr_ref[kk, r])
                    o_ref[pl.ds(st, GRID_W), :] = (acc - x_ref[pl.ds(st, GRID_W), :].astype(F32)).astype(out_dtype)
                _unrolled_loop(n_rows, row_box)

    slab = pl.BlockSpec((s, LANES), lambda i: (0, i))
    return pl.pallas_call(
        body, name=name, out_shape=jax.ShapeDtypeStruct((s, D_INNER), out_dtype), grid=(n_slab,),
        in_specs=[pl.BlockSpec(memory_space=pltpu.SMEM),
                  pl.BlockSpec((n_win, POOL_TOK, POOL_TOK), lambda i: (0, 0, 0)),
                  pl.BlockSpec((n_win, GRID_W, LANES), lambda i: (0, 0, 0)), slab],
        out_specs=slab,
        scratch_shapes=[pltpu.VMEM((s + 2 * pad_t, LANES), F32)],
        compiler_params=_cparams(dimension_semantics=("arbitrary",)),
    )(inv_r, boxes, inv_c, xin)


def _out1(dmix, pw, ps, g, x1, gt, wo, lg, lb, tgt, name):
    t = x1.shape[0]
    tm = min(TM_MM, t)
    n_grp = len(POOL_WINDOWS)

    def body(d_ref, pw_ref, ps_ref, g_ref, x1_ref, gt_ref, w_ref, lg_ref, lb_ref, tgt_ref, dz_ref, st_ref, po_ref):
        @pl.when(pl.program_id(0) == 0)
        def _():
            st_ref[...] = jnp.zeros_like(st_ref)

        br = jnp.zeros((tm, D_MODEL), F32)
        for k in range(n_grp):
            sl = slice(k * POOL_GROUP, (k + 1) * POOL_GROUP)
            po = jnp.dot(d_ref[:, sl], pw_ref[k], preferred_element_type=F32)
            po_ref[:, sl] = po.astype(ACT_DTYPE)
            y = po * ps_ref[:, sl]
            gg = g_ref[:, sl].astype(F32)
            br = br + _dot(y * (gg * _sigmoid(gg)), w_ref[sl, :])
        z = ALPHA * x1_ref[...] + gt_ref[...] * br
        xhat, rstd = _layer_norm_stats(z)
        lg_v = lg_ref[...]
        err = xhat * lg_v + lb_ref[...] - tgt_ref[...]
        dy = err * (1.0 / D_MODEL)
        dz = _layer_norm_bwd(dy, xhat, rstd, lg_v)
        dz_ref[...] = dz
        st_ref[0:1, :] += _rowsum(dy * xhat)
        st_ref[1:2, :] += _rowsum(dy)
        st_ref[2:3, :] += _rowsum(dz * br)
        st_ref[3:4, :] += _rowsum(err * err)

    wide = pl.BlockSpec((tm, D_INNER), lambda i: (i, 0))
    nar = pl.BlockSpec((tm, D_MODEL), lambda i: (i, 0))
    row = pl.BlockSpec((1, D_MODEL), lambda i: (0, 0))
    return pl.pallas_call(
        body, name=name,
        out_shape=[jax.ShapeDtypeStruct((t, D_MODEL), F32), jax.ShapeDtypeStruct((SUBLANES, D_MODEL), F32),
                   jax.ShapeDtypeStruct((t, D_INNER), ACT_DTYPE)],
        grid=(t // tm,),
        in_specs=[wide, pl.BlockSpec((n_grp, POOL_GROUP, POOL_GROUP), lambda i: (0, 0, 0)),
                  pl.BlockSpec((1, D_INNER), lambda i: (0, 0)), wide, nar, row,
                  pl.BlockSpec((D_INNER, D_MODEL), lambda i: (0, 0), pipeline_mode=pl.Buffered(1)), row, row, nar],
        out_specs=[nar, pl.BlockSpec((SUBLANES, D_MODEL), lambda i: (0, 0)), wide],
        compiler_params=_cparams(dimension_semantics=("arbitrary",)),
    )(dmix, pw, ps, g, x1, gt, wo, lg, lb, tgt)


def _flush(acc, out_hbm, sem):
    cp = pltpu.make_async_copy(acc, out_hbm, sem)
    cp.start()
    cp.wait()


def _bout1(dz, dmix, po, g, pw, ps, gt, wo, name):
    t = dz.shape[0]
    tm = min(TM_MM, t)
    nt = t // tm
    n_grp = len(POOL_WINDOWS)

    def body(dz_ref, d_ref, po_ref, g_ref, pw_ref, ps_ref, gt_ref, w_ref, dd_ref, dg_ref, gwo_hbm, gpw_hbm, gps_ref,
             gwo_acc, gpw_acc, sems):
        i = pl.program_id(0)

        @pl.when(i == 0)
        def _():
            gwo_acc[...] = jnp.zeros_like(gwo_acc)
            gpw_acc[...] = jnp.zeros_like(gpw_acc)
            gps_ref[...] = jnp.zeros_like(gps_ref)

        db = (gt_ref[...] * dz_ref[...]).astype(MXU_DTYPE)
        for k in range(n_grp):
            sl = slice(k * POOL_GROUP, (k + 1) * POOL_GROUP)
            dk = d_ref[:, sl]
            po = po_ref[:, sl].astype(F32)
            psk = ps_ref[:, sl]
            y = po * psk
            gg = g_ref[:, sl].astype(F32)
            sg = _sigmoid(gg)
            silu = gg * sg
            gwo_acc[sl, :] += _dot_tn(y * silu, db)
            dp = _dot_nt(db, w_ref[sl, :])
            dy = dp * silu
            dg_ref[:, sl] = (dp * y * (sg * (1.0 + gg * (1.0 - sg)))).astype(MXU_DTYPE)
            gps_ref[0:1, sl] += _rowsum(dy * po)
            dpo = (dy * psk).astype(MXU_DTYPE)
            gpw_acc[k] += _dot_tn(dk, dpo)
            dd_ref[:, sl] = _dot_nt(dpo, pw_ref[k])

        @pl.when(i == nt - 1)
        def _():
            _flush(gwo_acc, gwo_hbm, sems.at[0])
            _flush(gpw_acc, gpw_hbm, sems.at[1])

    wide = pl.BlockSpec((tm, D_INNER), lambda i: (i, 0))
    nar = pl.BlockSpec((tm, D_MODEL), lambda i: (i, 0))
    return pl.pallas_call(
        body, name=name,
        out_shape=[jax.ShapeDtypeStruct((t, D_INNER), F32), jax.ShapeDtypeStruct((t, D_INNER), MXU_DTYPE),
                   jax.ShapeDtypeStruct((D_INNER, D_MODEL), F32),
                   jax.ShapeDtypeStruct((n_grp, POOL_GROUP, POOL_GROUP), F32),
                   jax.ShapeDtypeStruct((SUBLANES, D_INNER), F32)],
        grid=(nt,),
        in_specs=[nar, wide, wide, wide,
                  pl.BlockSpec((n_grp, POOL_GROUP, POOL_GROUP), lambda i: (0, 0, 0), pipeline_mode=pl.Buffered(1)),
                  pl.BlockSpec((1, D_INNER), lambda i: (0, 0)), pl.BlockSpec((1, D_MODEL), lambda i: (0, 0)),
                  pl.BlockSpec((D_INNER, D_MODEL), lambda i: (0, 0), pipeline_mode=pl.Buffered(1))],
        out_specs=[wide, wide, ANY, ANY, pl.BlockSpec((SUBLANES, D_INNER), lambda i: (0, 0))],
        scratch_shapes=[pltpu.VMEM((D_INNER, D_MODEL), F32), pltpu.VMEM((n_grp, POOL_GROUP, POOL_GROUP), F32),
                        pltpu.SemaphoreType.DMA((2,))],
        compiler_params=_cparams(dimension_semantics=("arbitrary",)),
    )(dz, dmix, po, g, pw, ps, gt, wo)


def _bout0(dx1, xt, br0, lg, hf, hb, g, gt, wo, name, sides=()):
    t = dx1.shape[0]
    tm = min(TM_MM, t)
    nt = t // tm

    def body(dx_ref, x_ref, br_ref, lg_ref, hf_ref, hb_ref, g_ref, gt_ref, w_ref,
             dz_ref, dy_ref, dg_ref, gwo_hbm, st_ref, gwo_acc, sem):
        i = pl.program_id(0)

        @pl.when(i == 0)
        def _():
            gwo_acc[...] = jnp.zeros_like(gwo_acc)
            st_ref[...] = jnp.zeros_like(st_ref)

        dx = dx_ref[...]
        br = br_ref[...].astype(F32)
        gate = gt_ref[...]
        xhat, rstd = _layer_norm_stats(ALPHA * x_ref[...] + gate * br)
        dz = _layer_norm_bwd(dx, xhat, rstd, lg_ref[...])
        dz_ref[...] = dz
        st_ref[0:1, :] += _rowsum(dx * xhat)
        st_ref[1:2, :] += _rowsum(dx)
        st_ref[2:3, :] += _rowsum(dz * br)
        db = (gate * dz).astype(MXU_DTYPE)
        for k in range(D_INNER // WBLK):
            sl = slice(k * WBLK, (k + 1) * WBLK)
            y = hf_ref[:, sl].astype(F32) + hb_ref[:, sl].astype(F32)
            gg = g_ref[:, sl].astype(F32)
            sg = _sigmoid(gg)
            silu = gg * sg
            gwo_acc[sl, :] += _dot_tn(y * silu, db)
            dp = _dot_nt(db, w_ref[sl, :])
            dy_ref[:, sl] = (dp * silu).astype(ACT_DTYPE)
            dg_ref[:, sl] = (dp * y * (sg * (1.0 + gg * (1.0 - sg)))).astype(MXU_DTYPE)

        @pl.when(i == nt - 1)
        def _():
            _flush(gwo_acc, gwo_hbm, sem)

    wide = pl.BlockSpec((tm, D_INNER), lambda i: (i, 0))
    nar = pl.BlockSpec((tm, D_MODEL), lambda i: (i, 0))
    row = pl.BlockSpec((1, D_MODEL), lambda i: (0, 0))
    return _call_with_sides(
        body, sides, name=name,
        out_shape=[jax.ShapeDtypeStruct((t, D_MODEL), F32), jax.ShapeDtypeStruct((t, D_INNER), ACT_DTYPE),
                   jax.ShapeDtypeStruct((t, D_INNER), MXU_DTYPE), jax.ShapeDtypeStruct((D_INNER, D_MODEL), F32),
                   jax.ShapeDtypeStruct((SUBLANES, D_MODEL), F32)],
        grid=(nt,),
        in_specs=[nar, nar, nar, row, wide, wide, wide, row,
                  pl.BlockSpec((D_INNER, D_MODEL), lambda i: (0, 0), pipeline_mode=pl.Buffered(1))],
        out_specs=[nar, wide, wide, ANY, pl.BlockSpec((SUBLANES, D_MODEL), lambda i: (0, 0))],
        scratch_shapes=[pltpu.VMEM((D_INNER, D_MODEL), F32), pltpu.SemaphoreType.DMA(())],
        compiler_params=_cparams(dimension_semantics=("arbitrary",)),
        args=[dx1, xt, br0, lg, hf, hb, g, gt, wo])


def _conv_bwd(duvf, duvb, u, conv_w, name, sides=()):
    s = u.shape[0]
    tm = min(TM_LRU, s)
    cb = CB_LRU
    nt = s // tm

    def body(df_ref, dfp_ref, dfn_ref, db_ref, dbp_ref, dbn_ref, u_ref, cw_ref, du_ref, cst_ref):
        i = pl.program_id(1)

        @pl.when(i == 0)
        def _():
            cst_ref[...] = jnp.zeros_like(cst_ref)

        first, last = i == 0, i == nt - 1
        pz = jnp.where(first, 0.0, 1.0)
        nz = jnp.where(last, 0.0, 1.0)
        dout = df_ref[...].astype(F32) + db_ref[...].astype(F32)
        before = (dfp_ref[...].astype(F32) + dbp_ref[...].astype(F32))[H_HALO - SUBLANES:] * pz
        after = (dfn_ref[...].astype(F32) + dbn_ref[...].astype(F32))[:SUBLANES] * nz
        dm1, dp1, dp2 = _shifted(dout, before, after, [-1, 1, 2])
        cw = cw_ref[...]
        du_ref[...] = (dp2 * cw[0:1] + dp1 * cw[1:2] + dout * cw[2:3] + dm1 * cw[3:4]).astype(MXU_DTYPE)
        u_t = u_ref[...]
        cst_ref[0:1, :] += _rowsum(dp2 * u_t)
        cst_ref[1:2, :] += _rowsum(dp1 * u_t)
        cst_ref[2:3, :] += _rowsum(dout * u_t)
        cst_ref[3:4, :] += _rowsum(dm1 * u_t)
        cst_ref[4:5, :] += _rowsum(dout)

    tile, _, _ = _lru_specs(s, tm, cb, lambda i: i, nt)
    prev_map, next_map = _halo_maps(nt, tm, s // H_HALO, lambda i: i, rows=H_HALO)
    prev, nxt = pl.BlockSpec((H_HALO, cb), prev_map), pl.BlockSpec((H_HALO, cb), next_map)
    return _call_with_sides(
        body, sides, name=name,
        out_shape=[jax.ShapeDtypeStruct((s, D_INNER), MXU_DTYPE), jax.ShapeDtypeStruct((SUBLANES, D_INNER), F32)],
        grid=(D_INNER // cb, nt),
        in_specs=[tile, prev, nxt] * 2 + [tile, pl.BlockSpec((4, cb), lambda c, i: (0, c))],
        out_specs=[tile, pl.BlockSpec((SUBLANES, cb), lambda c, i: (0, c))], scratch_shapes=[],
        compiler_params=_cparams(dimension_semantics=("arbitrary", "arbitrary")),
        args=[duvf, duvf, duvf, duvb, duvb, duvb, u, conv_w])


def _bin(du, dg, xin, dzin, sc, sh, wg, name, gw_init=None, sides=()):
    t = xin.shape[0]
    tm = min(TM_MM, t)
    nt = t // tm
    has_g, has_dx, has_init = dg is not None, dzin is not None, gw_init is not None
    half = N_WBLK // 2
    n_blk = N_WBLK if has_g else half

    def body(*refs):
        refs = list(refs)
        du_ref = refs.pop(0)
        dg_ref = refs.pop(0) if has_g else None
        x_ref = refs.pop(0)
        dz_ref = refs.pop(0) if has_dx else None
        sc_ref, sh_ref, w_ref = refs.pop(0), refs.pop(0), refs.pop(0)
        init_hbm = refs.pop(0) if has_init else None
        dx_ref = refs.pop(0) if has_dx else None
        gw_hbm, st_ref, gw_acc, sem = refs
        i = pl.program_id(0)

        @pl.when(i == 0)
        def _():
            st_ref[...] = jnp.zeros_like(st_ref)
            first_zero = 0
            if has_init:
                _flush(init_hbm, gw_acc.at[pl.ds(0, half)], sem)
                first_zero = half
            for k in range(first_zero, n_blk):
                gw_acc[k] = jnp.zeros((D_MODEL, WBLK), F32)

        xv = x_ref[...]
        scale = 1.0 + sc_ref[...]
        h = (xv * scale + sh_ref[...]).astype(MXU_DTYPE)
        dh = None
        for k in range(n_blk):
            src = du_ref if k < half else dg_ref
            kk = k % half
            dk = src[:, kk * WBLK:(kk + 1) * WBLK]
            gw_acc[k] += _dot_tn(h, dk)
            contrib = _dot_nt(dk, w_ref[k])
            dh = contrib if dh is None else dh + contrib
        st_ref[0:1, :] += _rowsum(dh * xv)
        st_ref[1:2, :] += _rowsum(dh)
        if has_dx:
            dx_ref[...] = ALPHA * dz_ref[...] + dh * scale

        @pl.when(i == nt - 1)
        def _():
            _flush(gw_acc, gw_hbm, sem)

    wide = pl.BlockSpec((tm, D_INNER), lambda i: (i, 0))
    nar = pl.BlockSpec((tm, D_MODEL), lambda i: (i, 0))
    row = pl.BlockSpec((1, D_MODEL), lambda i: (0, 0))
    wspec = pl.BlockSpec((n_blk, D_MODEL, WBLK), lambda i: (0, 0, 0), pipeline_mode=pl.Buffered(1))
    in_specs = ([wide] + ([wide] if has_g else []) + [nar] + ([nar] if has_dx else []) + [row, row, wspec]
                + ([ANY] if has_init else []))
    args = ([du] + ([dg] if has_g else []) + [xin] + ([dzin] if has_dx else []) + [sc, sh, wg]
            + ([gw_init] if has_init else []))
    out_shape = ([jax.ShapeDtypeStruct((t, D_MODEL), F32)] if has_dx else []) + [
        jax.ShapeDtypeStruct((n_blk, D_MODEL, WBLK), F32), jax.ShapeDtypeStruct((SUBLANES, D_MODEL), F32)]
    out_specs = ([nar] if has_dx else []) + [ANY, pl.BlockSpec((SUBLANES, D_MODEL), lambda i: (0, 0))]
    return _call_with_sides(
        body, sides, name=name, out_shape=out_shape, grid=(nt,), in_specs=in_specs, out_specs=out_specs,
        scratch_shapes=[pltpu.VMEM((n_blk, D_MODEL, WBLK), F32), pltpu.SemaphoreType.DMA(())],
        compiler_params=_cparams(dimension_semantics=("arbitrary",)), args=args)


def _blocks_by_device(a, axis):
    shape = a.shape
    a = a.reshape(shape[:axis] + (N_DEV, shape[axis] // N_DEV) + shape[axis + 1:])
    return jnp.moveaxis(a, axis, 0)


def kernel(x, c, ctx, c_ctx, w_mod, b_mod, w_in, w_out, ln_g, ln_b, conv_w, conv_b, lru_wa, lru_ba, lru_wx, lru_bx, lru_lam, pool_w, pool_scale, loss_target, m_c_ctx, m_w_mod, m_b_mod, m_w_in, m_w_out, m_ln_g, m_ln_b, m_conv_w, m_conv_b, m_lru_wa, m_lru_ba, m_lru_wx, m_lru_bx, m_lru_lam, m_pool_w, m_pool_scale, v_c_ctx, v_w_mod, v_b_mod, v_w_in, v_w_out, v_ln_g, v_ln_b, v_conv_w, v_conv_b, v_lru_wa, v_lru_ba, v_lru_wx, v_lru_bx, v_lru_lam, v_pool_w, v_pool_scale):
    xi, yi, ci = _my_pos()
    dev = 4 * xi + 2 * yi + ci
    xt, ctxt, tgt = x[0], ctx[0], loss_target[0]
    n_mod = w_mod.shape[2]

    small_shapes = [(D_MODEL,), conv_w.shape[1:], lru_ba.shape[1:], lru_bx.shape[1:], lru_lam.shape[1:],
                    pool_scale.shape[1:]]
    small = _to_rows([c[0], conv_w[0], lru_ba[0], lru_bx[0], lru_lam[0], pool_scale[0]], SUBLANES)
    small_all, wi0 = _all_gather([small, w_in[0].astype(MXU_DTYPE)], "gather_first")
    pieces = [_split_rows(small_all[k], small_shapes) for k in range(N_DEV)]
    c_all = jnp.stack([p[0] for p in pieces])
    conv_w_f = jnp.concatenate([p[1] for p in pieces], axis=-1)
    lru_ba_f = jnp.concatenate([p[2] for p in pieces], axis=-1)[:, None, :]
    lru_bx_f = jnp.concatenate([p[3] for p in pieces], axis=-1)[:, None, :]
    lru_lam_f = jnp.concatenate([p[4] for p in pieces], axis=-1)[:, None, :]
    pool_scale_f = jnp.concatenate([p[5] for p in pieces], axis=-1)[None, :]

    cond = jnp.concatenate([c_all, jnp.broadcast_to(c_ctx[None, :], (N_DEV, D_MODEL))], axis=0)
    b_my = lax.dynamic_slice(b_mod, (0, dev * n_mod), (2, n_mod))[:, None, :]
    mod_part = _mod_fwd(cond, w_mod, b_my, "mod_fwd")
    mod_all, = _all_gather([mod_part], "gather_mod")
    mod = jnp.transpose(mod_all, (1, 2, 0, 3)).reshape(2, 16, 3 * D_MODEL)
    mod_me = lax.dynamic_slice(mod, (0, dev, 0), (2, 1, 3 * D_MODEL))
    sh = [mod_me[i, :, 0:D_MODEL] for i in range(2)]
    sc = [mod_me[i, :, D_MODEL:2 * D_MODEL] for i in range(2)]
    gt = [mod_me[i, :, 2 * D_MODEL:] for i in range(2)]
    shc, scc = mod[0, 8:9, 0:D_MODEL], mod[0, 8:9, D_MODEL:2 * D_MODEL]

    lg = [ln_g[i][None, :] for i in range(2)]
    lb = [ln_b[i][None, :] for i in range(2)]
    lru_p = dict(conv_w=conv_w_f, conv_b=conv_b, wa=lru_wa[0].astype(MXU_DTYPE), wx=lru_wx[0].astype(MXU_DTYPE),
                 ba=lru_ba_f, bx=lru_bx_f, lam=lru_lam_f)
    zero_state = jnp.zeros((1, D_INNER), F32)

    (u0, g0), (wo0,) = _in_proj(xt, sc[0], sh[0], wi0, "in_proj0", sides=[("gather", [w_out[0].astype(MXU_DTYPE)])])
    (uc, _), _ = _in_proj(ctxt, scc, shc, wi0, "in_proj0_ctx")
    (hcf, cf, uvc), _ = _lru_fwd(uc, zero_state, lru_p, 0, "lru_fwd_ctx_f", conv=True)
    (hcb, cbk), _ = _lru_fwd(uvc, zero_state, lru_p, 1, "lru_fwd_ctx_b", conv=False)
    (hf, _, uv0), (wi1,) = _lru_fwd(u0, cf, lru_p, 0, "lru_fwd_f", conv=True,
                                    sides=[("gather", [w_in[1].astype(MXU_DTYPE)])])
    (hb, _), (wo1, pool_w_g) = _lru_fwd(
        uv0, cbk, lru_p, 1, "lru_fwd_b", conv=False,
        sides=[("gather", [w_out[1].astype(MXU_DTYPE), pool_w[0].astype(MXU_DTYPE)])])
    w_in_l = [wi0, wi1]
    w_out_l = [wo0.reshape(D_INNER, D_MODEL), wo1.reshape(D_INNER, D_MODEL)]
    pool_w_f = jnp.transpose(pool_w_g, (1, 0, 2, 3)).reshape(len(POOL_WINDOWS), POOL_GROUP, POOL_GROUP)
    x1, br0 = _out0(hf, hb, g0, xt, gt[0], w_out_l[0], lg[0], lb[0], "out0")
    (u1, g1), _ = _in_proj(x1, sc[1], sh[1], w_in_l[1], "in_proj1", u_dtype=ACT_DTYPE)
    dmix = _pool_mix(u1, False, MXU_DTYPE, "pool_fwd")
    dz1, st1, po1 = _out1(dmix, pool_w_f, pool_scale_f, g1, x1, gt[1], w_out_l[1], lg[1], lb[1], tgt, "out1")
    loss_me = jnp.full((1, LANES), (0.5 / D_MODEL) * jnp.sum(st1[3]), F32)

    core = jnp.reshape(ci, (1,)).astype(jnp.int32)
    wo_view = lambda a: a.reshape(N_DEV, D_INNER // N_DEV, D_MODEL)
    pw_view = lambda a: _blocks_by_device(a, 1).reshape(N_DEV, POOL_GROUP // N_DEV * len(POOL_WINDOWS), POOL_GROUP)
    dd, dg1, gwo1, gpw, gps = _bout1(dz1, dmix, po1, g1, pool_w_f, pool_scale_f, gt[1], w_out_l[1], "bwd_out1")
    du1 = _pool_mix(dd, True, MXU_DTYPE, "pool_bwd")
    (dx1, gwi1, stb1), _ = _bin(du1, dg1, x1, dz1, sc[1], sh[1], w_in_l[1], "bwd_in1")
    bufs1 = [gwi1, wo_view(gwo1), pw_view(gpw)]
    (dz0, dy0, dg0, gwo0, stl0), recv1 = _bout0(dx1, xt, br0, lg[0], hf, hb, g0, gt[0], w_out_l[0], "bwd_out0",
                                                sides=[("sibling", bufs1)])
    pairs1 = [_pair_sum(b, r, core, "reduce_pair_" + n)
              for b, r, n in zip(bufs1, recv1, ["w_in1", "w_out1", "pool_w"])]
    (duvf, gwa_f, gwx_f, gv_f, dh0f), (p_wi1, p_wo1, p_pw, recv_wo0) = _lru_bwd(
        uv0, dy0, hf, cf, zero_state, lru_p, 0, "lru_bwd_f", sides=[("chips", pairs1), ("sibling", [wo_view(gwo0)])])
    pair_wo0 = _pair_sum(wo_view(gwo0), recv_wo0, core, "reduce_pair_w_out0")
    (duvb, gwa_b, gwx_b, gv_b, dh0b), (p_wo0,) = _lru_bwd(
        uv0, dy0, hb, cbk, zero_state, lru_p, 1, "lru_bwd_b", sides=[("chips", [pair_wo0])])
    zero_dh = jnp.zeros(uc.shape, ACT_DTYPE)
    (ducf, gwa_cf, gwx_cf, gv_cf, _), _ = _lru_bwd(uvc, zero_dh, hcf, zero_state, dh0f, lru_p, 0, "lru_bwd_ctx_f")
    (ducb, gwa_cb, gwx_cb, gv_cb, _), _ = _lru_bwd(uvc, zero_dh, hcb, zero_state, dh0b, lru_p, 1, "lru_bwd_ctx_b")

    def pack(sharded, replicated):
        sh_sizes = [int(np.prod(a.shape[1:])) for a in sharded]
        rep_sizes = [a.shape[0] // N_DEV for a in replicated]
        n_flat = sum(sh_sizes) + sum(rep_sizes)
        rows = -(-(-(-n_flat // LANES)) // FLAT_ROWS) * FLAT_ROWS
        buf = jnp.concatenate([a.reshape(N_DEV, -1) for a in sharded + replicated], axis=1)
        return jnp.pad(buf, ((0, 0), (0, rows * LANES - n_flat))).reshape(N_DEV, rows, LANES), sh_sizes, rep_sizes

    def unpack(reduced, sh_sizes, rep_sizes, sh_shapes):
        flat = reduced.reshape(-1)
        offs = np.cumsum([0] + sh_sizes)
        mine = [flat[offs[k]:offs[k + 1]].reshape(s) for k, s in enumerate(sh_shapes)]
        return mine, _to_rows([flat[offs[-1]:offs[-1] + sum(rep_sizes)]], FLAT_ROWS)

    def spread(rep_all, rep_sizes, shapes):
        flat = rep_all.reshape(N_DEV, -1)
        offs = np.cumsum([0] + rep_sizes)
        return [flat[:, offs[k]:offs[k + 1]].reshape(s) for k, s in enumerate(shapes)]

    (du0, cst0), _ = _conv_bwd(duvf, duvb, u0, conv_w_f, "conv_bwd")
    (duc, cstc), _ = _conv_bwd(ducf, ducb, uc, conv_w_f, "conv_bwd_ctx")
    (gwic, stc), _ = _bin(duc, None, ctxt, None, scc, shc, w_in_l[0][:N_WBLK // 2], "bwd_in0_ctx")
    (gx, gwi0, stb0), _ = _bin(du0, dg0, xt, dz0, sc[0], sh[0], w_in_l[0], "bwd_in0", gw_init=gwic)

    zero_row = jnp.zeros((1, D_MODEL), F32)
    dm_me = jnp.stack([
        jnp.concatenate([jnp.concatenate([stb0[1:2], stb0[0:1], stl0[2:3]], axis=1),
                         jnp.concatenate([stc[1:2], stc[0:1], zero_row], axis=1)], axis=0),
        jnp.concatenate([jnp.concatenate([stb1[1:2], stb1[0:1], st1[2:3]], axis=1),
                         jnp.zeros((1, 3 * D_MODEL), F32)], axis=0)])
    dm_g, loss_g = _all_gather([dm_me, loss_me], "gather_dmod")
    loss = jnp.sum(loss_g[:, 0, 0])
    dm_all = jnp.concatenate([jnp.transpose(dm_g[:, :, 0], (1, 0, 2)), jnp.transpose(dm_g[:, :, 1], (1, 0, 2))],
                             axis=1)
    dm_my = lax.dynamic_slice(dm_all, (0, 0, dev * n_mod), (2, 16, n_mod))
    g_w_mod, g_b_mod, gcc_part = _mod_bwd(cond, dm_all, dm_my, w_mod, "mod_bwd")
    g_b_mod = g_b_mod.reshape(b_mod.shape)

    gwa = jnp.stack([gwa_f + gwa_cf, gwa_b + gwa_cb])
    gwx = jnp.stack([gwx_f + gwx_cf, gwx_b + gwx_cb])
    gv = jnp.stack([gv_f + gv_cf, gv_b + gv_cb])
    cst = cst0 + cstc
    misc, m_sh, m_rep = pack(
        [_blocks_by_device(cst[0:4], 1), _blocks_by_device(gv[:, 0], 1), _blocks_by_device(gv[:, 1], 1),
         _blocks_by_device(gv[:, 2], 1), _blocks_by_device(gps[0], 0)],
        [gwa.reshape(-1), gwx.reshape(-1), jnp.stack([stl0[0], st1[0]]).reshape(-1),
         jnp.stack([stl0[1], st1[1]]).reshape(-1), cst[4], gcc_part.reshape(-1)])
    bufs = [gwi0, misc]
    recvs = _sibling_exchange(bufs, "reduce_sibling")
    pairs = [_pair_sum(b, r, core, "reduce_pair_" + n) for b, r, n in zip(bufs, recvs, ["w_in0", "misc"])]
    p_wi0, p_misc = _chip_exchange(pairs, "reduce_chips")
    (g_conv_w, g_lru_ba, g_lru_bx, g_lru_lam, g_pool_scale), rep_mine = unpack(
        _sum4(p_misc, "reduce_sum_misc"), m_sh, m_rep,
        [conv_w.shape, lru_ba.shape, lru_bx.shape, lru_lam.shape, pool_scale.shape])
    rep_all, = _all_gather([rep_mine.astype(WIRE_DTYPE)], "gather_replicated")
    rep_all = rep_all.astype(F32)
    g_lru_wa, g_lru_wx, g_ln_g, g_ln_b, g_conv_b, g_c_ctx = spread(
        rep_all, m_rep, [lru_wa.shape, lru_wx.shape, ln_g.shape, ln_b.shape, conv_b.shape, c_ctx.shape])

    names = ["c_ctx", "w_mod", "b_mod", "w_in", "w_out", "ln_g", "ln_b", "conv_w", "conv_b", "lru_wa", "lru_ba",
             "lru_wx", "lru_bx", "lru_lam", "pool_w", "pool_scale"]
    weights = dict(c_ctx=c_ctx, w_mod=w_mod, b_mod=b_mod, w_in=w_in, w_out=w_out, ln_g=ln_g, ln_b=ln_b,
                   conv_w=conv_w, conv_b=conv_b, lru_wa=lru_wa, lru_ba=lru_ba, lru_wx=lru_wx, lru_bx=lru_bx,
                   lru_lam=lru_lam, pool_w=pool_w, pool_scale=pool_scale)
    mom_m = dict(c_ctx=m_c_ctx, w_mod=m_w_mod, b_mod=m_b_mod, w_in=m_w_in, w_out=m_w_out, ln_g=m_ln_g, ln_b=m_ln_b,
                 conv_w=m_conv_w, conv_b=m_conv_b, lru_wa=m_lru_wa, lru_ba=m_lru_ba, lru_wx=m_lru_wx,
                 lru_bx=m_lru_bx, lru_lam=m_lru_lam, pool_w=m_pool_w, pool_scale=m_pool_scale)
    mom_v = dict(c_ctx=v_c_ctx, w_mod=v_w_mod, b_mod=v_b_mod, w_in=v_w_in, w_out=v_w_out, ln_g=v_ln_g, ln_b=v_ln_b,
                 conv_w=v_conv_w, conv_b=v_conv_b, lru_wa=v_lru_wa, lru_ba=v_lru_ba, lru_wx=v_lru_wx,
                 lru_bx=v_lru_bx, lru_lam=v_lru_lam, pool_w=v_pool_w, pool_scale=v_pool_scale)
    grads = dict(c_ctx=g_c_ctx, w_mod=g_w_mod, b_mod=g_b_mod, ln_g=g_ln_g, ln_b=g_ln_b,
                 conv_w=g_conv_w, conv_b=g_conv_b, lru_wa=g_lru_wa, lru_ba=g_lru_ba, lru_wx=g_lru_wx,
                 lru_bx=g_lru_bx, lru_lam=g_lru_lam)
    grads["pool_scale"] = g_pool_scale
    delta, new_m, new_v = {}, {}, {}

    def update_parts(n, parts, view):
        res = _adamw_parts(weights[n].reshape(view), parts, mom_m[n].reshape(view), mom_v[n].reshape(view),
                           "adamw_" + n)
        grads[n], delta[n], new_m[n], new_v[n] = [r.reshape(weights[n].shape) for r in res]

    update_parts("w_in", [p_wi0, p_wi1], w_in.shape)
    update_parts("w_out", [p_wo0, p_wo1], w_out.shape)
    update_parts("pool_w", [p_pw], (1,) + p_pw.shape[1:])
    for n in ("w_mod", "lru_wa", "lru_wx"):
        shape = weights[n].shape
        view = (int(np.prod(shape[:-1])), shape[-1])
        res = _adamw(weights[n].reshape(view), grads[n].reshape(view), mom_m[n].reshape(view),
                     mom_v[n].reshape(view), "adamw_" + n)
        delta[n], new_m[n], new_v[n] = [r.reshape(shape) for r in res]

    small = [n for n in names if n not in delta]
    shapes = [weights[n].shape for n in small]
    flat = lambda d: _to_rows([d[n] for n in small], FLAT_ROWS)
    res = _adamw(flat(weights), flat(grads), flat(mom_m), flat(mom_v), "adamw_small")
    for d, r in zip((delta, new_m, new_v), res):
        d.update(zip(small, _split_rows(r, shapes)))

    return (loss, gx[None], *[grads[n] for n in names], *[delta[n] for n in names],
            *[new_m[n] for n in names], *[new_v[n] for n in names])
```

```python
import functools

import numpy as np
import jax
import jax.numpy as jnp
from jax import lax
from jax.experimental import pallas as pl
from jax.experimental.pallas import tpu as pltpu

F32 = jnp.float32
BF16 = jnp.bfloat16
MXU_DTYPE = BF16

D_MODEL = 1024
D_INNER = 2048
LRU_BLOCK = 128
GRID_W = 64
POOL_WINDOWS = (2, 4, 8, 16)
POOL_GROUP = 512
ALPHA = float(4 ** 0.25)
LN_EPS = 1e-5
LRU_C = 8.0
N_DEV = 8
N_WBLK = 8
WBLK = 512

ADAM_LR = 0.001
ADAM_B1 = 0.9
ADAM_B2 = 0.999
ADAM_EPS = 1e-08
ADAM_WD = 0.01
ADAM_STEP = 10

LANES = 128
SUBLANES = 8
V7X_VMEM_BYTES = 64 * 1024 * 1024
VMEM_COMPILER_RESERVE = 8 * 1024 * 1024
VMEM_LIMIT = V7X_VMEM_BYTES - VMEM_COMPILER_RESERVE
MESH = pl.DeviceIdType.MESH
ANY = pl.BlockSpec(memory_space=pl.ANY)

TM_MM = 512
TM_LRU = 1024
CB_LRU = 512
N_SEG = 8
SCAN_UNROLL = 4
SCAN_ROW_T = 17
SCAN_ROW_J = 2
SQRT_FLOOR = 1e-30
FLAT_ROWS = 16
ELEMENTWISE_TILE_BYTES = 1 << 20
POOL_TOK = 256
WIRE_DTYPE = BF16
ACT_DTYPE = BF16
H_HALO = 16


def _cparams(**kw):
    return pltpu.CompilerParams(vmem_limit_bytes=VMEM_LIMIT, **kw)


def _my_pos():
    return lax.axis_index("x"), lax.axis_index("y"), lax.axis_index("c")


def _dot(a, b):
    return jnp.dot(a.astype(MXU_DTYPE), b.astype(MXU_DTYPE), preferred_element_type=F32)


def _dot_tn(a, b):
    return lax.dot_general(a.astype(MXU_DTYPE), b.astype(MXU_DTYPE), (((0,), (0,)), ((), ())),
                           preferred_element_type=F32)


def _dot_nt(a, b):
    return lax.dot_general(a.astype(MXU_DTYPE), b.astype(MXU_DTYPE), (((1,), (1,)), ((), ())),
                           preferred_element_type=F32)


def _sigmoid(z):
    return 0.5 * jnp.tanh(0.5 * z) + 0.5


def _log_sigmoid(x):
    y = jnp.exp(-jnp.abs(x))
    u = 1.0 + y
    l1p = jnp.where(u == 1.0, y, jnp.log(u) * (y / jnp.where(u == 1.0, 1.0, u - 1.0)))
    return jnp.minimum(x, 0.0) - l1p


def _rowsum(v):
    return jnp.sum(v, axis=0, keepdims=True)


def _layer_norm_stats(z):
    mu = jnp.mean(z, axis=-1, keepdims=True)
    zc = z - mu
    var = jnp.mean(zc * zc, axis=-1, keepdims=True)
    rstd = lax.rsqrt(var + LN_EPS)
    return zc * rstd, rstd


def _layer_norm_bwd(dy, xhat, rstd, g):
    dxh = dy * g
    m1 = jnp.mean(dxh, axis=-1, keepdims=True)
    m2 = jnp.mean(dxh * xhat, axis=-1, keepdims=True)
    return rstd * (dxh - m1 - xhat * m2)


def _shifted(v, before8, after8, offsets):
    n = v.shape[0]
    ext = jnp.concatenate([before8, v, after8], axis=0)
    total = n + 2 * SUBLANES
    return [pltpu.roll(ext, (-k) % total, 0)[SUBLANES:SUBLANES + n] for k in offsets]


def _rows8(row):
    return jnp.broadcast_to(row, (SUBLANES, row.shape[1]))


def _shift_down(v, first_row):
    return _shifted(v, _rows8(first_row), _rows8(first_row), [-1])[0]


def _shift_up(v, last_row):
    return _shifted(v, _rows8(last_row), _rows8(last_row), [1])[0]


def _all_gather(blocks, name):
    n = len(blocks)

    def body(*refs):
        x_refs, out_refs = refs[:n], refs[n:2 * n]
        send_sems, recv_sems, local_sems = refs[2 * n:]
        x, y, c = _my_pos()
        me, sibling = (x, y, c), (x, y, 1 - c)
        chips = [(1 - x, y), (x, 1 - y), (1 - x, 1 - y)]

        def slot(a, px, py, pc):
            return out_refs[a].at[4 * px + 2 * py + pc]

        def copy(a, k, block, to, src=None):
            return pltpu.make_async_remote_copy(
                src_ref=slot(a, *block) if src is None else src, dst_ref=slot(a, *block),
                send_sem=send_sems.at[a, k], recv_sem=recv_sems.at[a, k], device_id=to, device_id_type=MESH)

        mine = [pltpu.make_async_copy(x_refs[a], slot(a, *me), local_sems.at[a]) for a in range(n)]
        for cp in mine:
            cp.start()
        first = []
        for a in range(n):
            first.append(copy(a, 0, me, sibling, src=x_refs[a]))
            first += [copy(a, 1 + j, me, (*chip, c), src=x_refs[a]) for j, chip in enumerate(chips)]
        for cp in first:
            cp.start()
        passed = []
        for j, chip in enumerate(chips):
            for a in range(n):
                copy(a, 1 + j, (*chip, c), me).wait_recv()
                fwd = copy(a, 4 + j, (*chip, c), sibling)
                fwd.start()
                passed.append(fwd)
        for a in range(n):
            copy(a, 0, sibling, me).wait_recv()
            for j, chip in enumerate(chips):
                copy(a, 4 + j, (*chip, 1 - c), me).wait_recv()
        for cp in first + passed:
            cp.wait_send()
        for cp in mine:
            cp.wait()

    outs = pl.pallas_call(
        body, name=name,
        out_shape=[jax.ShapeDtypeStruct((N_DEV,) + b.shape, b.dtype) for b in blocks],
        in_specs=[ANY] * n, out_specs=[ANY] * n,
        scratch_shapes=[pltpu.SemaphoreType.DMA((n, 7)), pltpu.SemaphoreType.DMA((n, 7)),
                        pltpu.SemaphoreType.DMA((n,))],
    )(*blocks)
    return list(outs)


def _sibling_exchange(bufs, name):
    n = len(bufs)

    def body(*refs):
        srcs, outs = refs[:n], refs[n:2 * n]
        send_sems, recv_sems = refs[2 * n:]
        x, y, c = _my_pos()
        copies = [pltpu.make_async_remote_copy(
            src_ref=srcs[a].at[2 * j + (1 - c)], dst_ref=outs[a].at[j], send_sem=send_sems.at[a, j],
            recv_sem=recv_sems.at[a, j], device_id=(x, y, 1 - c), device_id_type=MESH)
            for a in range(n) for j in range(4)]
        for cp in copies:
            cp.start()
        for cp in copies:
            cp.wait()

    outs = pl.pallas_call(
        body, name=name, out_shape=[jax.ShapeDtypeStruct((4,) + b.shape[1:], b.dtype) for b in bufs],
        in_specs=[ANY] * n, out_specs=[ANY] * n,
        scratch_shapes=[pltpu.SemaphoreType.DMA((n, 4)), pltpu.SemaphoreType.DMA((n, 4))],
    )(*bufs)
    return list(outs)


def _chip_exchange(parts, name):
    n = len(parts)

    def body(*refs):
        srcs, outs = refs[:n], refs[n:2 * n]
        send_sems, recv_sems, local_sems = refs[2 * n:]
        x, y, c = _my_pos()
        jme = 2 * x + y
        peers = [(1 - x, y), (x, 1 - y), (1 - x, 1 - y)]
        local = [pltpu.make_async_copy(srcs[a].at[jme], outs[a].at[jme], local_sems.at[a]) for a in range(n)]
        for cp in local:
            cp.start()

        def copy(a, k, px, py, dst_slot):
            return pltpu.make_async_remote_copy(
                src_ref=srcs[a].at[2 * px + py], dst_ref=outs[a].at[dst_slot], send_sem=send_sems.at[a, k],
                recv_sem=recv_sems.at[a, k], device_id=(px, py, c), device_id_type=MESH)

        sends = [copy(a, k, px, py, jme) for a in range(n) for k, (px, py) in enumerate(peers)]
        for cp in sends:
            cp.start()
        for a in range(n):
            for k, (px, py) in enumerate(peers):
                copy(a, k, px, py, 2 * px + py).wait_recv()
        for cp in sends:
            cp.wait_send()
        for cp in local:
            cp.wait()

    outs = pl.pallas_call(
        body, name=name, out_shape=[jax.ShapeDtypeStruct(p.shape, p.dtype) for p in parts],
        in_specs=[ANY] * n, out_specs=[ANY] * n,
        scratch_shapes=[pltpu.SemaphoreType.DMA((n, 3)), pltpu.SemaphoreType.DMA((n, 3)),
                        pltpu.SemaphoreType.DMA((n,))],
    )(*parts)
    return list(outs)


_SIDE_REMOTE = {"gather": 7, "sibling": 4, "chips": 3}
_FLIPS = [(0, 0, 1), (1, 0, 0), (0, 1, 0), (1, 1, 0), (1, 0, 1), (0, 1, 1), (1, 1, 1)]


def _side_plan(sides):
    inputs, out_shapes, scratch = [], [], []
    for kind, arrays in sides:
        n = len(arrays)
        for a in arrays:
            inputs.append(a)
            shape = {"gather": (N_DEV,) + a.shape, "sibling": (4,) + a.shape[1:], "chips": a.shape}[kind]
            out_shapes.append(jax.ShapeDtypeStruct(shape, a.dtype))
        scratch += [pltpu.SemaphoreType.DMA((n, _SIDE_REMOTE[kind])), pltpu.SemaphoreType.DMA((n, _SIDE_REMOTE[kind])),
                    pltpu.SemaphoreType.DMA((n,))]
    return inputs, out_shapes, scratch


def _side_copies(sides, in_refs, out_refs, sem_refs):
    x, y, c = _my_pos()
    starts, waits = [], []
    pos = 0
    for s, (kind, arrays) in enumerate(sides):
        send_sems, recv_sems, local_sems = sem_refs[3 * s:3 * s + 3]
        for a in range(len(arrays)):
            src, out = in_refs[pos], out_refs[pos]
            pos += 1

            def remote(k, src_ref, dst_ref, to):
                return pltpu.make_async_remote_copy(src_ref=src_ref, dst_ref=dst_ref, send_sem=send_sems.at[a, k],
                                                    recv_sem=recv_sems.at[a, k], device_id=to, device_id_type=MESH)

            def local(src_ref, dst_ref):
                cp = pltpu.make_async_copy(src_ref, dst_ref, local_sems.at[a])
                starts.append(cp.start)
                waits.append(cp.wait)

            if kind == "gather":
                me = 4 * x + 2 * y + c
                local(src, out.at[me])
                for k, (fx, fy, fc) in enumerate(_FLIPS):
                    px, py, pc = (1 - x if fx else x), (1 - y if fy else y), (1 - c if fc else c)
                    send = remote(k, src, out.at[me], (px, py, pc))
                    starts.append(send.start)
                    waits += [remote(k, src, out.at[4 * px + 2 * py + pc], (px, py, pc)).wait_recv, send.wait_send]
            elif kind == "sibling":
                for j in range(4):
                    cp = remote(j, src.at[2 * j + (1 - c)], out.at[j], (x, y, 1 - c))
                    starts.append(cp.start)
                    waits.append(cp.wait)
            else:
                jme = 2 * x + y
                local(src.at[jme], out.at[jme])
                for k, (px, py) in enumerate([(1 - x, y), (x, 1 - y), (1 - x, 1 - y)]):
                    send = remote(k, src.at[2 * px + py], out.at[jme], (px, py, c))
                    starts.append(send.start)
                    waits += [remote(k, src.at[2 * px + py], out.at[2 * px + py], (px, py, c)).wait_recv,
                              send.wait_send]
    return starts, waits


def _call_with_sides(body, sides, *, name, grid, in_specs, out_specs, out_shape, scratch_shapes, compiler_params, args):
    if not sides:
        res = pl.pallas_call(body, name=name, grid=grid, in_specs=in_specs, out_specs=out_specs, out_shape=out_shape,
                             scratch_shapes=scratch_shapes, compiler_params=compiler_params)(*args)
        return list(res), []
    s_in, s_out, s_scr = _side_plan(sides)
    n_in, n_out, n_scr, n_side = len(in_specs), len(out_specs), len(scratch_shapes), len(s_in)

    def wrapped(*refs):
        refs = list(refs)
        ins, side_in = refs[:n_in], refs[n_in:n_in + n_side]
        outs = refs[n_in + n_side:n_in + n_side + n_out]
        side_out = refs[n_in + n_side + n_out:n_in + 2 * n_side + n_out]
        rest = refs[n_in + 2 * n_side + n_out:]
        starts, waits = _side_copies(sides, side_in, side_out, rest[n_scr:])
        first = functools.reduce(jnp.logical_and, [pl.program_id(d) == 0 for d in range(len(grid))])
        last = functools.reduce(jnp.logical_and, [pl.program_id(d) == grid[d] - 1 for d in range(len(grid))])

        @pl.when(first)
        def _():
            for start in starts:
                start()

        body(*ins, *outs, *rest[:n_scr])

        @pl.when(last)
        def _():
            for wait in waits:
                wait()

    res = pl.pallas_call(
        wrapped, name=name, grid=grid, in_specs=list(in_specs) + [ANY] * n_side,
        out_specs=list(out_specs) + [ANY] * n_side, out_shape=list(out_shape) + s_out,
        scratch_shapes=list(scratch_shapes) + s_scr, compiler_params=compiler_params,
    )(*args, *s_in)
    return list(res[:n_out]), list(res[n_out:])


def _row_tile(r, l):
    t = min(r, max(16, ELEMENTWISE_TILE_BYTES // (4 * l) // 16 * 16))
    while r % t:
        t -= 16
    return t


def _pair_sum(buf, recv, core, name):
    _, r, l = buf.shape
    tr = _row_tile(r, l)

    def body(core_ref, a_ref, b_ref, o_ref):
        o_ref[...] = (a_ref[...] + b_ref[...]).astype(WIRE_DTYPE)

    return pl.pallas_call(
        body, name=name, out_shape=jax.ShapeDtypeStruct((4, r, l), WIRE_DTYPE),
        grid_spec=pltpu.PrefetchScalarGridSpec(
            num_scalar_prefetch=1, grid=(4, r // tr),
            in_specs=[pl.BlockSpec((None, tr, l), lambda j, i, cr: (2 * j + cr[0], i, 0)),
                      pl.BlockSpec((None, tr, l), lambda j, i, cr: (j, i, 0))],
            out_specs=pl.BlockSpec((None, tr, l), lambda j, i, cr: (j, i, 0))),
        compiler_params=_cparams(dimension_semantics=("arbitrary", "arbitrary")),
    )(core, buf, recv)


def _sum_parts(p_ref):
    return ((p_ref[0].astype(F32) + p_ref[1].astype(F32)) + (p_ref[2].astype(F32) + p_ref[3].astype(F32)))


def _sum4(parts, name):
    _, r, l = parts.shape
    tr = _row_tile(r, l)

    def body(p_ref, o_ref):
        o_ref[...] = _sum_parts(p_ref)

    return pl.pallas_call(
        body, name=name, out_shape=jax.ShapeDtypeStruct((r, l), F32), grid=(r // tr,),
        in_specs=[pl.BlockSpec((4, tr, l), lambda i: (0, i, 0))],
        out_specs=pl.BlockSpec((tr, l), lambda i: (i, 0)),
        compiler_params=_cparams(dimension_semantics=("arbitrary",)),
    )(parts)


def _adamw_update(w, gg, m, v):
    nm = ADAM_B1 * m + (1.0 - ADAM_B1) * gg
    nv = ADAM_B2 * v + (1.0 - ADAM_B2) * (gg * gg)
    m_hat = nm / (1.0 - ADAM_B1 ** ADAM_STEP)
    v_hat = nv / (1.0 - ADAM_B2 ** ADAM_STEP)
    return -ADAM_LR * (m_hat / (jnp.sqrt(v_hat) + ADAM_EPS) + ADAM_WD * w), nm, nv


def _adamw(w, g, m, v, name):
    r, l = w.shape
    tr = _row_tile(r, l)

    def body(w_ref, g_ref, m_ref, v_ref, d_ref, nm_ref, nv_ref):
        d_ref[...], nm_ref[...], nv_ref[...] = _adamw_update(w_ref[...], g_ref[...], m_ref[...], v_ref[...])

    spec = pl.BlockSpec((tr, l), lambda i: (i, 0))
    return pl.pallas_call(
        body, name=name, out_shape=[jax.ShapeDtypeStruct((r, l), F32)] * 3, grid=(r // tr,),
        in_specs=[spec] * 4, out_specs=[spec] * 3,
        compiler_params=_cparams(dimension_semantics=("arbitrary",)),
    )(w, g, m, v)


def _adamw_parts(w, parts, m, v, name):
    nl, r, l = w.shape
    tr = _row_tile(r, l)

    def body(*refs):
        w_ref, p_refs, (m_ref, v_ref, g_ref, d_ref, nm_ref, nv_ref) = refs[0], refs[1:1 + nl], refs[1 + nl:]
        layer = pl.program_id(0)
        gg = _sum_parts(p_refs[0])
        for q in range(1, nl):
            gg = jnp.where(layer == q, _sum_parts(p_refs[q]), gg)
        g_ref[...] = gg
        d_ref[...], nm_ref[...], nv_ref[...] = _adamw_update(w_ref[...], gg, m_ref[...], v_ref[...])

    spec = pl.BlockSpec((None, tr, l), lambda q, i: (q, i, 0))
    pspecs = [pl.BlockSpec((4, tr, l), lambda q, i, k=k: (0, jnp.where(q == k, i, 0), 0)) for k in range(nl)]
    return pl.pallas_call(
        body, name=name, out_shape=[jax.ShapeDtypeStruct((nl, r, l), F32)] * 4, grid=(nl, r // tr),
        in_specs=[spec] + pspecs + [spec, spec], out_specs=[spec] * 4,
        compiler_params=_cparams(dimension_semantics=("arbitrary", "arbitrary")),
    )(w, *parts, m, v)


def _to_rows(pieces, row_multiple):
    flat = jnp.concatenate([p.reshape(-1) for p in pieces])
    rows = -(-flat.shape[0] // LANES)
    rows = -(-rows // row_multiple) * row_multiple
    flat = jnp.pad(flat, (0, rows * LANES - flat.shape[0]))
    return flat.reshape(rows, LANES)


def _split_rows(rows, shapes):
    flat = rows.reshape(-1)
    out, off = [], 0
    for s in shapes:
        n = int(np.prod(s))
        out.append(flat[off:off + n].reshape(s))
        off += n
    return out


def _mod_fwd(cond, w_mod, b_my, name):
    nl, _, ncol = w_mod.shape

    def body(a_ref, w_ref, b_ref, o_ref):
        a = a_ref[...]
        s = a * _sigmoid(a)
        for i in range(nl):
            o_ref[i] = _dot(s, w_ref[i]) + b_ref[i]

    return pl.pallas_call(
        body, name=name, out_shape=jax.ShapeDtypeStruct((nl, 16, ncol), F32),
        compiler_params=_cparams(),
    )(cond, w_mod, b_my)


def _mod_bwd(cond, dm_all, dm_my, w_mod, name):
    nl, _, ncol = w_mod.shape

    def body(a_ref, dma_ref, dmm_ref, w_ref, gw_ref, gb_ref, gc_ref):
        a = a_ref[...]
        sg = _sigmoid(a)
        s = a * sg
        for i in range(nl):
            gw_ref[i] = _dot_tn(s, dmm_ref[i])
            gb_ref[i] = jnp.sum(dma_ref[i], axis=0, keepdims=True)
        back = _dot_nt(dmm_ref[0], w_ref[0])
        dsilu = sg * (1.0 + a * (1.0 - sg))
        gc_ref[...] = jnp.sum(back[8:16] * dsilu[8:16], axis=0, keepdims=True)

    return pl.pallas_call(
        body, name=name,
        out_shape=[jax.ShapeDtypeStruct((nl, D_MODEL, ncol), F32), jax.ShapeDtypeStruct((nl, 1, 3 * D_MODEL), F32),
                   jax.ShapeDtypeStruct((1, D_MODEL), F32)],
        compiler_params=_cparams(),
    )(cond, dm_all, dm_my, w_mod)


def _in_proj(xt, sc, sh, wg, name, sides=(), u_dtype=F32):
    t = xt.shape[0]
    tm = min(TM_MM, t)

    def body(x_ref, sc_ref, sh_ref, w_ref, u_ref, g_ref):
        h = (x_ref[...] * (1.0 + sc_ref[...]) + sh_ref[...]).astype(MXU_DTYPE)
        for k in range(N_WBLK):
            o = jnp.dot(h, w_ref[k], preferred_element_type=F32)
            if k < N_WBLK // 2:
                u_ref[:, k * WBLK:(k + 1) * WBLK] = o.astype(u_dtype)
            else:
                kk = k - N_WBLK // 2
                g_ref[:, kk * WBLK:(kk + 1) * WBLK] = o.astype(ACT_DTYPE)

    row = pl.BlockSpec((1, D_MODEL), lambda i: (0, 0))
    return _call_with_sides(
        body, sides, name=name,
        out_shape=[jax.ShapeDtypeStruct((t, D_INNER), u_dtype), jax.ShapeDtypeStruct((t, D_INNER), ACT_DTYPE)],
        grid=(t // tm,),
        in_specs=[pl.BlockSpec((tm, D_MODEL), lambda i: (i, 0)), row, row,
                  pl.BlockSpec((N_WBLK, D_MODEL, WBLK), lambda i: (0, 0, 0), pipeline_mode=pl.Buffered(1))],
        out_specs=[pl.BlockSpec((tm, D_INNER), lambda i: (i, 0))] * 2, scratch_shapes=[],
        compiler_params=_cparams(dimension_semantics=("arbitrary",)), args=[xt, sc, sh, wg])


def _halo_maps(nt, tm, n_blocks, pos, rows=SUBLANES):
    per = tm // rows
    prev = lambda cb, i: (jnp.maximum(pos(i) * per - 1, 0), cb)
    nxt = lambda cb, i: (jnp.minimum((pos(i) + 1) * per, n_blocks - 1), cb)
    return prev, nxt


def _conv_taps(u, prev8, next8, is_first, is_last):
    pz = jnp.where(is_first, 0.0, 1.0)
    nz = jnp.where(is_last, 0.0, 1.0)
    return _shifted(u, prev8 * pz, next8 * nz, [-2, -1, 1])


def _lru_gates(uv, wa_ref, wx_ref, ba, bx, cl, g):
    sl = slice(g * LANES, (g + 1) * LANES)
    uvg = uv[:, sl]
    r = _sigmoid(_dot(uvg, wa_ref[g]) + ba[:, sl])
    ii = _sigmoid(_dot(uvg, wx_ref[g]) + bx[:, sl])
    la = cl[:, sl] * r
    a = jnp.exp(la)
    q = jnp.tanh(-la) * (1.0 + a * a)
    rs = lax.rsqrt(jnp.maximum(q, SQRT_FLOOR))
    return uvg, r, ii, a, q * rs, rs


def _scan_rows(seg):
    return -(-(SCAN_ROW_T * (seg - 1) + SCAN_ROW_J * (N_SEG - 1) + 1) // SUBLANES) * SUBLANES


def _seg_chunk(j, c):
    return pl.ds(SCAN_ROW_T * SUBLANES * c + SCAN_ROW_J * j, SUBLANES, stride=SCAN_ROW_T)


def _seg_scatter(ref, g, seg, value):
    for j in range(N_SEG):
        for c in range(seg // SUBLANES):
            r0 = j * seg + SUBLANES * c
            ref[g, _seg_chunk(j, c), :] = value[r0:r0 + SUBLANES]


def _scan_tile(a_s, b_s, carry_ref, write_out, seg, reverse, chunks_per_write=1):
    n_g = a_s.shape[0]
    unroll = SCAN_UNROLL if seg % SCAN_UNROLL == 0 else 1

    n_trips = seg // unroll

    def steps(k, state):
        hs, cs = list(state[0]), list(state[1])
        base = ((n_trips - 1 - k) if reverse else k) * unroll
        for q in (range(unroll - 1, -1, -1) if reverse else range(unroll)):
            t = base + q
            rows = pl.ds(t * SCAN_ROW_T, N_SEG, stride=SCAN_ROW_J)
            for g in range(n_g):
                a = a_s[g, rows, :]
                b = b_s[g, rows, :]
                hs[g] = a * hs[g] + b
                cs[g] = a * cs[g]
                b_s[g, rows, :] = hs[g]
                a_s[g, rows, :] = cs[g]
        return tuple(hs), tuple(cs)

    zeros = tuple(jnp.zeros((N_SEG, LANES), F32) for _ in range(n_g))
    ones = tuple(jnp.ones((N_SEG, LANES), F32) for _ in range(n_g))
    h_fin, a_fin = lax.fori_loop(0, seg // unroll, steps, (zeros, ones))

    order = list(range(N_SEG - 1, -1, -1)) if reverse else list(range(N_SEG))
    for g in range(n_g):
        carry = carry_ref[:, g * LANES:(g + 1) * LANES]
        for j in order:
            for c0 in range(0, seg // SUBLANES, chunks_per_write):
                parts = [b_s[g, _seg_chunk(j, c), :] + a_s[g, _seg_chunk(j, c), :] * carry
                         for c in range(c0, c0 + chunks_per_write)]
                write_out(j, c0, g, parts[0] if chunks_per_write == 1 else jnp.concatenate(parts, axis=0))
            carry = a_fin[g][j:j + 1] * carry + h_fin[g][j:j + 1]
        carry_ref[:, g * LANES:(g + 1) * LANES] = carry


def _lru_specs(s, tm, cb, direction_pos, nt):
    n_rows8 = s // SUBLANES
    prev, nxt = _halo_maps(nt, tm, n_rows8, direction_pos)
    tile = pl.BlockSpec((tm, cb), lambda c, i: (direction_pos(i), c))
    return tile, pl.BlockSpec((SUBLANES, cb), prev), pl.BlockSpec((SUBLANES, cb), nxt)


def _lru_param_specs(cb, d):
    n_g = cb // LANES
    vec = pl.BlockSpec((1, cb), lambda c, i: (0, c))
    dvec = pl.BlockSpec((None, 1, cb), lambda c, i: (d, 0, c))
    wmat = pl.BlockSpec((None, n_g, LRU_BLOCK, LRU_BLOCK), lambda c, i: (d, c, 0, 0))
    return vec, dvec, wmat


def _lru_fwd(src, h0, p, d, name, conv, sides=()):
    s = src.shape[0]
    tm = min(TM_LRU, s)
    cb = CB_LRU
    n_g = cb // LANES
    nt = s // tm
    seg = tm // N_SEG
    pos = (lambda i: i) if d == 0 else (lambda i: nt - 1 - i)

    def body(*refs):
        refs = list(refs)
        u_ref = refs.pop(0)
        if conv:
            up_ref, un_ref, cw_ref, cbias_ref = [refs.pop(0) for _ in range(4)]
        wa_ref, wx_ref, ba_ref, bx_ref, lam_ref, h0_ref, h_ref, hc_ref = [refs.pop(0) for _ in range(8)]
        uv_ref = refs.pop(0) if conv else None
        a_s, b_s = refs
        i = pl.program_id(1)
        tp = pos(i)

        @pl.when(i == 0)
        def _():
            hc_ref[...] = h0_ref[...]

        if conv:
            u_t = u_ref[...]
            um2, um1, up1 = _conv_taps(u_t, up_ref[...], un_ref[...], tp == 0, tp == nt - 1)
            cw = cw_ref[...]
            uv_ref[...] = um2 * cw[0:1] + um1 * cw[1:2] + u_t * cw[2:3] + up1 * cw[3:4] + cbias_ref[...]
        src_ref = uv_ref if conv else u_ref
        cl = LRU_C * _log_sigmoid(lam_ref[...])
        ba, bx = ba_ref[...], bx_ref[...]
        for g in range(n_g):
            uvg, r, ii, a, sq, _ = _lru_gates(src_ref, wa_ref, wx_ref, ba, bx, cl, g)
            b = sq * (ii * uvg)
            _seg_scatter(a_s, g, seg, a)
            _seg_scatter(b_s, g, seg, b)

        per_write = 2 if (seg // SUBLANES) % 2 == 0 else 1

        def write_out(j, c, g, h):
            h_ref[pl.ds(j * seg + SUBLANES * c, SUBLANES * per_write), pl.ds(g * LANES, LANES)] = h.astype(ACT_DTYPE)

        _scan_tile(a_s, b_s, hc_ref, write_out, seg, reverse=(d == 1), chunks_per_write=per_write)

    tile, prev, nxt = _lru_specs(s, tm, cb, pos, nt)
    vec, dvec, wmat = _lru_param_specs(cb, d)
    wide = jax.ShapeDtypeStruct((s, D_INNER), F32)
    conv_specs = [prev, nxt, pl.BlockSpec((4, cb), lambda c, i: (0, c)), vec] if conv else []
    conv_args = [src, src, p["conv_w"], p["conv_b"]] if conv else []
    return _call_with_sides(
        body, sides, name=name,
        out_shape=[jax.ShapeDtypeStruct((s, D_INNER), ACT_DTYPE), jax.ShapeDtypeStruct((1, D_INNER), F32)]
        + ([wide] if conv else []),
        grid=(D_INNER // cb, nt),
        in_specs=[tile] + conv_specs + [wmat, wmat, dvec, dvec, dvec, vec],
        out_specs=[tile, vec] + ([tile] if conv else []),
        scratch_shapes=[pltpu.VMEM((n_g, _scan_rows(seg), LANES), F32)] * 2,
        compiler_params=_cparams(dimension_semantics=("arbitrary", "arbitrary")),
        args=[src, *conv_args, p["wa"], p["wx"], p["ba"], p["bx"], p["lam"], h0])


def _lru_bwd(uv, dh, h, h0, lam_in, p, d, name, sides=()):
    s = uv.shape[0]
    tm = min(TM_LRU, s)
    cb = CB_LRU
    n_g = cb // LANES
    nt = s // tm
    seg = tm // N_SEG
    pos = (lambda i: nt - 1 - i) if d == 0 else (lambda i: i)

    def body(uv_ref, dh_ref, h_ref, hh_ref, wa_ref, wx_ref, ba_ref, bx_ref,
             lam_ref, h0_ref, lin_ref, duv_ref, gwa_ref, gwx_ref, gv_ref, lc_ref, a_s, b_s, lp_s,
             r_s, i_s, q_s, rq_s, a_keep):
        i = pl.program_id(1)
        tp = pos(i)

        @pl.when(i == 0)
        def _():
            lc_ref[...] = lin_ref[...]
            gwa_ref[...] = jnp.zeros_like(gwa_ref)
            gwx_ref[...] = jnp.zeros_like(gwx_ref)
            gv_ref[...] = jnp.zeros_like(gv_ref)

        uv = uv_ref[...]
        lam = lam_ref[...]
        cl = LRU_C * _log_sigmoid(lam)
        ba, bx = ba_ref[...], bx_ref[...]
        dh_t = dh_ref[...].astype(F32)
        carry_in = lc_ref[...]
        for g in range(n_g):
            sl = slice(g * LANES, (g + 1) * LANES)
            _, r, ii, a, sq, rs = _lru_gates(uv, wa_ref, wx_ref, ba, bx, cl, g)
            for ref, val in ((r_s, r), (i_s, ii), (q_s, sq), (rq_s, rs), (a_keep, a)):
                ref[:, sl] = val.astype(ACT_DTYPE)
            b = a * dh_t[:, sl]
            _seg_scatter(a_s, g, seg, a)
            _seg_scatter(b_s, g, seg, b)

        def write_out(j, c, g, v):
            lp_s[pl.ds(j * seg + SUBLANES * c, SUBLANES), pl.ds(g * LANES, LANES)] = v

        _scan_tile(a_s, b_s, lc_ref, write_out, seg, reverse=(d == 0))

        h_t = h_ref[...].astype(F32)
        hh = hh_ref[...].astype(F32)
        if d == 0:
            edge = jnp.where(tp == 0, h0_ref[...], hh[H_HALO - 1:H_HALO])
            h_prev = _shift_down(h_t, edge)
            lam_t = dh_t + _shift_up(lp_s[...], carry_in)
        else:
            edge = jnp.where(tp == nt - 1, h0_ref[...], hh[0:1])
            h_prev = _shift_up(h_t, edge)
            lam_t = dh_t + _shift_down(lp_s[...], carry_in)

        dsig = LRU_C * _sigmoid(-lam)
        for g in range(n_g):
            sl = slice(g * LANES, (g + 1) * LANES)
            uvg = uv[:, sl]
            r, ii, a, sq, rs = [ref[:, sl].astype(F32) for ref in (r_s, i_s, a_keep, q_s, rq_s)]
            lt = lam_t[:, sl]
            ls = lt * sq
            dla = (lt * a) * (h_prev[:, sl] - (ii * uvg) * (a * rs))
            dzr = (dla * cl[:, sl]) * r * (1.0 - r)
            dzi = (ls * uvg) * ii * (1.0 - ii)
            duv_ref[:, sl] = (ls * ii + _dot_nt(dzr, wa_ref[g]) + _dot_nt(dzi, wx_ref[g])).astype(ACT_DTYPE)
            gwa_ref[g] += _dot_tn(uvg, dzr)
            gwx_ref[g] += _dot_tn(uvg, dzi)
            gv_ref[0:1, sl] += _rowsum(dzr)
            gv_ref[1:2, sl] += _rowsum(dzi)
            gv_ref[2:3, sl] += _rowsum(dla * r) * dsig[:, sl]

    tile, _, _ = _lru_specs(s, tm, cb, pos, nt)
    vec, dvec, wmat = _lru_param_specs(cb, d)
    h_prev_map, h_next_map = _halo_maps(nt, tm, s // H_HALO, pos, rows=H_HALO)
    hh_spec = pl.BlockSpec((H_HALO, cb), h_prev_map if d == 0 else h_next_map)
    gw_spec = pl.BlockSpec((n_g, LRU_BLOCK, LRU_BLOCK), lambda c, i: (c, 0, 0))
    n_blk = D_INNER // LRU_BLOCK
    return _call_with_sides(
        body, sides, name=name,
        out_shape=[jax.ShapeDtypeStruct((s, D_INNER), ACT_DTYPE),
                   jax.ShapeDtypeStruct((n_blk, LRU_BLOCK, LRU_BLOCK), F32),
                   jax.ShapeDtypeStruct((n_blk, LRU_BLOCK, LRU_BLOCK), F32),
                   jax.ShapeDtypeStruct((SUBLANES, D_INNER), F32),
                   jax.ShapeDtypeStruct((1, D_INNER), F32)],
        grid=(D_INNER // cb, nt),
        in_specs=[tile, tile, tile, hh_spec, wmat, wmat, dvec, dvec, dvec, vec, vec],
        out_specs=[tile, gw_spec, gw_spec, pl.BlockSpec((SUBLANES, cb), lambda c, i: (0, c)), vec],
        scratch_shapes=[pltpu.VMEM((n_g, _scan_rows(seg), LANES), F32)] * 2 + [pltpu.VMEM((tm, cb), F32)]
        + [pltpu.VMEM((tm, cb), ACT_DTYPE)] * 5,
        compiler_params=_cparams(dimension_semantics=("arbitrary", "arbitrary")),
        args=[uv, dh, h, h, p["wa"], p["wx"], p["ba"], p["bx"], p["lam"], h0, lam_in])


def _out0(hf, hb, g, xt, gt, wo, lg, lb, name):
    t = xt.shape[0]
    tm = min(TM_MM, t)

    def body(hf_ref, hb_ref, g_ref, x_ref, gt_ref, w_ref, lg_ref, lb_ref, x1_ref, br_ref):
        br = None
        for k in range(D_INNER // WBLK):
            sl = slice(k * WBLK, (k + 1) * WBLK)
            gg = g_ref[:, sl].astype(F32)
            p = (hf_ref[:, sl].astype(F32) + hb_ref[:, sl].astype(F32)) * (gg * _sigmoid(gg))
            part = _dot(p, w_ref[sl, :])
            br = part if br is None else br + part
        z = ALPHA * x_ref[...] + gt_ref[...] * br
        xhat, _ = _layer_norm_stats(z)
        x1_ref[...] = xhat * lg_ref[...] + lb_ref[...]
        br_ref[...] = br.astype(ACT_DTYPE)

    wide = pl.BlockSpec((tm, D_INNER), lambda i: (i, 0))
    nar = pl.BlockSpec((tm, D_MODEL), lambda i: (i, 0))
    row = pl.BlockSpec((1, D_MODEL), lambda i: (0, 0))
    return pl.pallas_call(
        body, name=name,
        out_shape=[jax.ShapeDtypeStruct((t, D_MODEL), F32), jax.ShapeDtypeStruct((t, D_MODEL), ACT_DTYPE)],
        grid=(t // tm,),
        in_specs=[wide, wide, wide, nar, row,
                  pl.BlockSpec((D_INNER, D_MODEL), lambda i: (0, 0), pipeline_mode=pl.Buffered(1)), row, row],
        out_specs=[nar, nar],
        compiler_params=_cparams(dimension_semantics=("arbitrary",)),
    )(hf, hb, g, xt, gt, wo, lg, lb)


def _unrolled_loop(n, fn, unroll=4):
    while n % unroll:
        unroll //= 2

    def trip(k, carry):
        for q in range(unroll):
            fn(k * unroll + q)
        return carry
    lax.fori_loop(0, n // unroll, trip, 0)


def _window(n, w):
    t = np.arange(n)
    return np.clip(t - w // 2, 0, n), np.clip(t + w // 2, 0, n)


def _pool_tables(n_rows, transpose):
    boxes, inv_c, inv_r = [], [], []
    for w in POOL_WINDOWS:
        lo, hi = _window(GRID_W, w)
        m = np.zeros((GRID_W, GRID_W), np.float32)
        for r in range(GRID_W):
            m[r, lo[r]:hi[r]] = 1.0
        m = np.kron(np.eye(POOL_TOK // GRID_W, dtype=np.float32), m)
        boxes.append(m.T if transpose else m)
        inv_c.append(np.broadcast_to((1.0 / (hi - lo).astype(np.float32))[:, None], (GRID_W, LANES)))
        lo_r, hi_r = _window(n_rows, w)
        inv_r.append(1.0 / (hi_r - lo_r).astype(np.float32))
    return (jnp.asarray(np.stack(boxes), MXU_DTYPE), jnp.asarray(np.stack(inv_c), F32),
            jnp.asarray(np.stack(inv_r), F32))


def _pool_mix(xin, transpose, out_dtype, name):
    s = xin.shape[0]
    n_rows = s // GRID_W
    pad_t = SUBLANES * GRID_W
    rows_per_blk = POOL_TOK // GRID_W
    n_slab = D_INNER // LANES
    slabs_per_group = POOL_GROUP // LANES
    n_win = len(POOL_WINDOWS)
    boxes, inv_c, inv_r = _pool_tables(n_rows, transpose)
    exact_operand = (not transpose) and xin.dtype == MXU_DTYPE and MXU_DTYPE != F32

    def body(invr_ref, box_ref, invc_ref, x_ref, o_ref, pad_s):
        k = pl.program_id(0) // slabs_per_group
        pad_s[pl.ds(0, pad_t), :] = jnp.zeros((pad_t, LANES), F32)
        pad_s[pl.ds(pad_t + s, pad_t), :] = jnp.zeros((pad_t, LANES), F32)

        for kk, w in enumerate(POOL_WINDOWS):
            half = w // 2
            offsets = list(range(-(half - 1), half + 1)) if transpose else list(range(-half, half))

            @pl.when(k == kk)
            def _():
                inv_col = invc_ref[kk]

                def col_box(b):
                    st = pl.multiple_of(b * POOL_TOK, POOL_TOK)
                    xb = x_ref[pl.ds(st, POOL_TOK), :]
                    if exact_operand:
                        pad_s[pl.ds(pad_t + st, POOL_TOK), :] = jnp.dot(box_ref[kk], xb, preferred_element_type=F32)
                        return
                    xb = xb.astype(F32)
                    if transpose:
                        xb = xb * jnp.concatenate(
                            [inv_col * invr_ref[kk, b * rows_per_blk + q] for q in range(rows_per_blk)], axis=0)
                    hi = xb.astype(MXU_DTYPE)
                    lo = (xb - hi.astype(F32)).astype(MXU_DTYPE)
                    both = jnp.dot(box_ref[kk], jnp.concatenate([hi, lo], axis=1), preferred_element_type=F32)
                    pad_s[pl.ds(pad_t + st, POOL_TOK), :] = both[:, :LANES] + both[:, LANES:]
                _unrolled_loop(s // POOL_TOK, col_box)

                def row_box(r):
                    st = pl.multiple_of(r * GRID_W, GRID_W)
                    acc = pad_s[pl.ds(pad_t + st + offsets[0] * GRID_W, GRID_W), :]
                    for o in offsets[1:]:
                        acc = acc + pad_s[pl.ds(pad_t + st + o * GRID_W, GRID_W), :]
                    if not transpose:
                        acc = acc * (inv_col * invr_ref[kk, r])
                    o_ref[pl.ds(st, GRID_W), :] = (acc - x_ref[pl.ds(st, GRID_W), :].astype(F32)).astype(out_dtype)
                _unrolled_loop(n_rows, row_box)

    slab = pl.BlockSpec((s, LANES), lambda i: (0, i))
    return pl.pallas_call(
        body, name=name, out_shape=jax.ShapeDtypeStruct((s, D_INNER), out_dtype), grid=(n_slab,),
        in_specs=[pl.BlockSpec(memory_space=pltpu.SMEM),
                  pl.BlockSpec((n_win, POOL_TOK, POOL_TOK), lambda i: (0, 0, 0)),
                  pl.BlockSpec((n_win, GRID_W, LANES), lambda i: (0, 0, 0)), slab],
        out_specs=slab,
        scratch_shapes=[pltpu.VMEM((s + 2 * pad_t, LANES), F32)],
        compiler_params=_cparams(dimension_semantics=("arbitrary",)),
    )(inv_r, boxes, inv_c, xin)


def _out1(dmix, pw, ps, g, x1, gt, wo, lg, lb, tgt, name):
    t = x1.shape[0]
    tm = min(TM_MM, t)
    n_grp = len(POOL_WINDOWS)

    def body(d_ref, pw_ref, ps_ref, g_ref, x1_ref, gt_ref, w_ref, lg_ref, lb_ref, tgt_ref, dz_ref, st_ref, po_ref):
        @pl.when(pl.program_id(0) == 0)
        def _():
            st_ref[...] = jnp.zeros_like(st_ref)

        br = jnp.zeros((tm, D_MODEL), F32)
        for k in range(n_grp):
            sl = slice(k * POOL_GROUP, (k + 1) * POOL_GROUP)
            po = jnp.dot(d_ref[:, sl], pw_ref[k], preferred_element_type=F32)
            po_ref[:, sl] = po.astype(ACT_DTYPE)
            y = po * ps_ref[:, sl]
            gg = g_ref[:, sl].astype(F32)
            br = br + _dot(y * (gg * _sigmoid(gg)), w_ref[sl, :])
        z = ALPHA * x1_ref[...] + gt_ref[...] * br
        xhat, rstd = _layer_norm_stats(z)
        lg_v = lg_ref[...]
        err = xhat * lg_v + lb_ref[...] - tgt_ref[...]
        dy = err * (1.0 / D_MODEL)
        dz = _layer_norm_bwd(dy, xhat, rstd, lg_v)
        dz_ref[...] = dz
        st_ref[0:1, :] += _rowsum(dy * xhat)
        st_ref[1:2, :] += _rowsum(dy)
        st_ref[2:3, :] += _rowsum(dz * br)
        st_ref[3:4, :] += _rowsum(err * err)

    wide = pl.BlockSpec((tm, D_INNER), lambda i: (i, 0))
    nar = pl.BlockSpec((tm, D_MODEL), lambda i: (i, 0))
    row = pl.BlockSpec((1, D_MODEL), lambda i: (0, 0))
    return pl.pallas_call(
        body, name=name,
        out_shape=[jax.ShapeDtypeStruct((t, D_MODEL), F32), jax.ShapeDtypeStruct((SUBLANES, D_MODEL), F32),
                   jax.ShapeDtypeStruct((t, D_INNER), ACT_DTYPE)],
        grid=(t // tm,),
        in_specs=[wide, pl.BlockSpec((n_grp, POOL_GROUP, POOL_GROUP), lambda i: (0, 0, 0)),
                  pl.BlockSpec((1, D_INNER), lambda i: (0, 0)), wide, nar, row,
                  pl.BlockSpec((D_INNER, D_MODEL), lambda i: (0, 0), pipeline_mode=pl.Buffered(1)), row, row, nar],
        out_specs=[nar, pl.BlockSpec((SUBLANES, D_MODEL), lambda i: (0, 0)), wide],
        compiler_params=_cparams(dimension_semantics=("arbitrary",)),
    )(dmix, pw, ps, g, x1, gt, wo, lg, lb, tgt)


def _flush(acc, out_hbm, sem):
    cp = pltpu.make_async_copy(acc, out_hbm, sem)
    cp.start()
    cp.wait()


def _bout1(dz, dmix, po, g, pw, ps, gt, wo, name):
    t = dz.shape[0]
    tm = min(TM_MM, t)
    nt = t // tm
    n_grp = len(POOL_WINDOWS)

    def body(dz_ref, d_ref, po_ref, g_ref, pw_ref, ps_ref, gt_ref, w_ref, dd_ref, dg_ref, gwo_hbm, gpw_hbm, gps_ref,
             gwo_acc, gpw_acc, sems):
        i = pl.program_id(0)

        @pl.when(i == 0)
        def _():
            gwo_acc[...] = jnp.zeros_like(gwo_acc)
            gpw_acc[...] = jnp.zeros_like(gpw_acc)
            gps_ref[...] = jnp.zeros_like(gps_ref)

        db = (gt_ref[...] * dz_ref[...]).astype(MXU_DTYPE)
        for k in range(n_grp):
            sl = slice(k * POOL_GROUP, (k + 1) * POOL_GROUP)
            dk = d_ref[:, sl]
            po = po_ref[:, sl].astype(F32)
            psk = ps_ref[:, sl]
            y = po * psk
            gg = g_ref[:, sl].astype(F32)
            sg = _sigmoid(gg)
            silu = gg * sg
            gwo_acc[sl, :] += _dot_tn(y * silu, db)
            dp = _dot_nt(db, w_ref[sl, :])
            dy = dp * silu
            dg_ref[:, sl] = (dp * y * (sg * (1.0 + gg * (1.0 - sg)))).astype(MXU_DTYPE)
            gps_ref[0:1, sl] += _rowsum(dy * po)
            dpo = (dy * psk).astype(MXU_DTYPE)
            gpw_acc[k] += _dot_tn(dk, dpo)
            dd_ref[:, sl] = _dot_nt(dpo, pw_ref[k])

        @pl.when(i == nt - 1)
        def _():
            _flush(gwo_acc, gwo_hbm, sems.at[0])
            _flush(gpw_acc, gpw_hbm, sems.at[1])

    wide = pl.BlockSpec((tm, D_INNER), lambda i: (i, 0))
    nar = pl.BlockSpec((tm, D_MODEL), lambda i: (i, 0))
    return pl.pallas_call(
        body, name=name,
        out_shape=[jax.ShapeDtypeStruct((t, D_INNER), F32), jax.ShapeDtypeStruct((t, D_INNER), MXU_DTYPE),
                   jax.ShapeDtypeStruct((D_INNER, D_MODEL), F32),
                   jax.ShapeDtypeStruct((n_grp, POOL_GROUP, POOL_GROUP), F32),
                   jax.ShapeDtypeStruct((SUBLANES, D_INNER), F32)],
        grid=(nt,),
        in_specs=[nar, wide, wide, wide,
                  pl.BlockSpec((n_grp, POOL_GROUP, POOL_GROUP), lambda i: (0, 0, 0), pipeline_mode=pl.Buffered(1)),
                  pl.BlockSpec((1, D_INNER), lambda i: (0, 0)), pl.BlockSpec((1, D_MODEL), lambda i: (0, 0)),
                  pl.BlockSpec((D_INNER, D_MODEL), lambda i: (0, 0), pipeline_mode=pl.Buffered(1))],
        out_specs=[wide, wide, ANY, ANY, pl.BlockSpec((SUBLANES, D_INNER), lambda i: (0, 0))],
        scratch_shapes=[pltpu.VMEM((D_INNER, D_MODEL), F32), pltpu.VMEM((n_grp, POOL_GROUP, POOL_GROUP), F32),
                        pltpu.SemaphoreType.DMA((2,))],
        compiler_params=_cparams(dimension_semantics=("arbitrary",)),
    )(dz, dmix, po, g, pw, ps, gt, wo)


def _bout0(dx1, xt, br0, lg, hf, hb, g, gt, wo, name, sides=()):
    t = dx1.shape[0]
    tm = min(TM_MM, t)
    nt = t // tm

    def body(dx_ref, x_ref, br_ref, lg_ref, hf_ref, hb_ref, g_ref, gt_ref, w_ref,
             dz_ref, dy_ref, dg_ref, gwo_hbm, st_ref, gwo_acc, sem):
        i = pl.program_id(0)

        @pl.when(i == 0)
        def _():
            gwo_acc[...] = jnp.zeros_like(gwo_acc)
            st_ref[...] = jnp.zeros_like(st_ref)

        dx = dx_ref[...]
        br = br_ref[...].astype(F32)
        gate = gt_ref[...]
        xhat, rstd = _layer_norm_stats(ALPHA * x_ref[...] + gate * br)
        dz = _layer_norm_bwd(dx, xhat, rstd, lg_ref[...])
        dz_ref[...] = dz
        st_ref[0:1, :] += _rowsum(dx * xhat)
        st_ref[1:2, :] += _rowsum(dx)
        st_ref[2:3, :] += _rowsum(dz * br)
        db = (gate * dz).astype(MXU_DTYPE)
        for k in range(D_INNER // WBLK):
            sl = slice(k * WBLK, (k + 1) * WBLK)
            y = hf_ref[:, sl].astype(F32) + hb_ref[:, sl].astype(F32)
            gg = g_ref[:, sl].astype(F32)
            sg = _sigmoid(gg)
            silu = gg * sg
            gwo_acc[sl, :] += _dot_tn(y * silu, db)
            dp = _dot_nt(db, w_ref[sl, :])
            dy_ref[:, sl] = (dp * silu).astype(ACT_DTYPE)
            dg_ref[:, sl] = (dp * y * (sg * (1.0 + gg * (1.0 - sg)))).astype(MXU_DTYPE)

        @pl.when(i == nt - 1)
        def _():
            _flush(gwo_acc, gwo_hbm, sem)

    wide = pl.BlockSpec((tm, D_INNER), lambda i: (i, 0))
    nar = pl.BlockSpec((tm, D_MODEL), lambda i: (i, 0))
    row = pl.BlockSpec((1, D_MODEL), lambda i: (0, 0))
    return _call_with_sides(
        body, sides, name=name,
        out_shape=[jax.ShapeDtypeStruct((t, D_MODEL), F32), jax.ShapeDtypeStruct((t, D_INNER), ACT_DTYPE),
                   jax.ShapeDtypeStruct((t, D_INNER), MXU_DTYPE), jax.ShapeDtypeStruct((D_INNER, D_MODEL), F32),
                   jax.ShapeDtypeStruct((SUBLANES, D_MODEL), F32)],
        grid=(nt,),
        in_specs=[nar, nar, nar, row, wide, wide, wide, row,
                  pl.BlockSpec((D_INNER, D_MODEL), lambda i: (0, 0), pipeline_mode=pl.Buffered(1))],
        out_specs=[nar, wide, wide, ANY, pl.BlockSpec((SUBLANES, D_MODEL), lambda i: (0, 0))],
        scratch_shapes=[pltpu.VMEM((D_INNER, D_MODEL), F32), pltpu.SemaphoreType.DMA(())],
        compiler_params=_cparams(dimension_semantics=("arbitrary",)),
        args=[dx1, xt, br0, lg, hf, hb, g, gt, wo])


def _conv_bwd(duvf, duvb, u, conv_w, name, sides=()):
    s = u.shape[0]
    tm = min(TM_LRU, s)
    cb = CB_LRU
    nt = s // tm

    def body(df_ref, dfp_ref, dfn_ref, db_ref, dbp_ref, dbn_ref, u_ref, cw_ref, du_ref, cst_ref):
        i = pl.program_id(1)

        @pl.when(i == 0)
        def _():
            cst_ref[...] = jnp.zeros_like(cst_ref)

        first, last = i == 0, i == nt - 1
        pz = jnp.where(first, 0.0, 1.0)
        nz = jnp.where(last, 0.0, 1.0)
        dout = df_ref[...].astype(F32) + db_ref[...].astype(F32)
        before = (dfp_ref[...].astype(F32) + dbp_ref[...].astype(F32))[H_HALO - SUBLANES:] * pz
        after = (dfn_ref[...].astype(F32) + dbn_ref[...].astype(F32))[:SUBLANES] * nz
        dm1, dp1, dp2 = _shifted(dout, before, after, [-1, 1, 2])
        cw = cw_ref[...]
        du_ref[...] = (dp2 * cw[0:1] + dp1 * cw[1:2] + dout * cw[2:3] + dm1 * cw[3:4]).astype(MXU_DTYPE)
        u_t = u_ref[...]
        cst_ref[0:1, :] += _rowsum(dp2 * u_t)
        cst_ref[1:2, :] += _rowsum(dp1 * u_t)
        cst_ref[2:3, :] += _rowsum(dout * u_t)
        cst_ref[3:4, :] += _rowsum(dm1 * u_t)
        cst_ref[4:5, :] += _rowsum(dout)

    tile, _, _ = _lru_specs(s, tm, cb, lambda i: i, nt)
    prev_map, next_map = _halo_maps(nt, tm, s // H_HALO, lambda i: i, rows=H_HALO)
    prev, nxt = pl.BlockSpec((H_HALO, cb), prev_map), pl.BlockSpec((H_HALO, cb), next_map)
    return _call_with_sides(
        body, sides, name=name,
        out_shape=[jax.ShapeDtypeStruct((s, D_INNER), MXU_DTYPE), jax.ShapeDtypeStruct((SUBLANES, D_INNER), F32)],
        grid=(D_INNER // cb, nt),
        in_specs=[tile, prev, nxt] * 2 + [tile, pl.BlockSpec((4, cb), lambda c, i: (0, c))],
        out_specs=[tile, pl.BlockSpec((SUBLANES, cb), lambda c, i: (0, c))], scratch_shapes=[],
        compiler_params=_cparams(dimension_semantics=("arbitrary", "arbitrary")),
        args=[duvf, duvf, duvf, duvb, duvb, duvb, u, conv_w])


def _bin(du, dg, xin, dzin, sc, sh, wg, name, gw_init=None, sides=()):
    t = xin.shape[0]
    tm = min(TM_MM, t)
    nt = t // tm
    has_g, has_dx, has_init = dg is not None, dzin is not None, gw_init is not None
    half = N_WBLK // 2
    n_blk = N_WBLK if has_g else half

    def body(*refs):
        refs = list(refs)
        du_ref = refs.pop(0)
        dg_ref = refs.pop(0) if has_g else None
        x_ref = refs.pop(0)
        dz_ref = refs.pop(0) if has_dx else None
        sc_ref, sh_ref, w_ref = refs.pop(0), refs.pop(0), refs.pop(0)
        init_hbm = refs.pop(0) if has_init else None
        dx_ref = refs.pop(0) if has_dx else None
        gw_hbm, st_ref, gw_acc, sem = refs
        i = pl.program_id(0)

        @pl.when(i == 0)
        def _():
            st_ref[...] = jnp.zeros_like(st_ref)
            first_zero = 0
            if has_init:
                _flush(init_hbm, gw_acc.at[pl.ds(0, half)], sem)
                first_zero = half
            for k in range(first_zero, n_blk):
                gw_acc[k] = jnp.zeros((D_MODEL, WBLK), F32)

        xv = x_ref[...]
        scale = 1.0 + sc_ref[...]
        h = (xv * scale + sh_ref[...]).astype(MXU_DTYPE)
        dh = None
        for k in range(n_blk):
            src = du_ref if k < half else dg_ref
            kk = k % half
            dk = src[:, kk * WBLK:(kk + 1) * WBLK]
            gw_acc[k] += _dot_tn(h, dk)
            contrib = _dot_nt(dk, w_ref[k])
            dh = contrib if dh is None else dh + contrib
        st_ref[0:1, :] += _rowsum(dh * xv)
        st_ref[1:2, :] += _rowsum(dh)
        if has_dx:
            dx_ref[...] = ALPHA * dz_ref[...] + dh * scale

        @pl.when(i == nt - 1)
        def _():
            _flush(gw_acc, gw_hbm, sem)

    wide = pl.BlockSpec((tm, D_INNER), lambda i: (i, 0))
    nar = pl.BlockSpec((tm, D_MODEL), lambda i: (i, 0))
    row = pl.BlockSpec((1, D_MODEL), lambda i: (0, 0))
    wspec = pl.BlockSpec((n_blk, D_MODEL, WBLK), lambda i: (0, 0, 0), pipeline_mode=pl.Buffered(1))
    in_specs = ([wide] + ([wide] if has_g else []) + [nar] + ([nar] if has_dx else []) + [row, row, wspec]
                + ([ANY] if has_init else []))
    args = ([du] + ([dg] if has_g else []) + [xin] + ([dzin] if has_dx else []) + [sc, sh, wg]
            + ([gw_init] if has_init else []))
    out_shape = ([jax.ShapeDtypeStruct((t, D_MODEL), F32)] if has_dx else []) + [
        jax.ShapeDtypeStruct((n_blk, D_MODEL, WBLK), F32), jax.ShapeDtypeStruct((SUBLANES, D_MODEL), F32)]
    out_specs = ([nar] if has_dx else []) + [ANY, pl.BlockSpec((SUBLANES, D_MODEL), lambda i: (0, 0))]
    return _call_with_sides(
        body, sides, name=name, out_shape=out_shape, grid=(nt,), in_specs=in_specs, out_specs=out_specs,
        scratch_shapes=[pltpu.VMEM((n_blk, D_MODEL, WBLK), F32), pltpu.SemaphoreType.DMA(())],
        compiler_params=_cparams(dimension_semantics=("arbitrary",)), args=args)


def _blocks_by_device(a, axis):
    shape = a.shape
    a = a.reshape(shape[:axis] + (N_DEV, shape[axis] // N_DEV) + shape[axis + 1:])
    return jnp.moveaxis(a, axis, 0)


def kernel(x, c, ctx, c_ctx, w_mod, b_mod, w_in, w_out, ln_g, ln_b, conv_w, conv_b, lru_wa, lru_ba, lru_wx, lru_bx, lru_lam, pool_w, pool_scale, loss_target, m_c_ctx, m_w_mod, m_b_mod, m_w_in, m_w_out, m_ln_g, m_ln_b, m_conv_w, m_conv_b, m_lru_wa, m_lru_ba, m_lru_wx, m_lru_bx, m_lru_lam, m_pool_w, m_pool_scale, v_c_ctx, v_w_mod, v_b_mod, v_w_in, v_w_out, v_ln_g, v_ln_b, v_conv_w, v_conv_b, v_lru_wa, v_lru_ba, v_lru_wx, v_lru_bx, v_lru_lam, v_pool_w, v_pool_scale):
    xi, yi, ci = _my_pos()
    dev = 4 * xi + 2 * yi + ci
    xt, ctxt, tgt = x[0], ctx[0], loss_target[0]
    n_mod = w_mod.shape[2]

    small_shapes = [(D_MODEL,), conv_w.shape[1:], lru_ba.shape[1:], lru_bx.shape[1:], lru_lam.shape[1:],
                    pool_scale.shape[1:]]
    small = _to_rows([c[0], conv_w[0], lru_ba[0], lru_bx[0], lru_lam[0], pool_scale[0]], SUBLANES)
    small_all, wi0 = _all_gather([small, w_in[0].astype(MXU_DTYPE)], "gather_first")
    pieces = [_split_rows(small_all[k], small_shapes) for k in range(N_DEV)]
    c_all = jnp.stack([p[0] for p in pieces])
    conv_w_f = jnp.concatenate([p[1] for p in pieces], axis=-1)
    lru_ba_f = jnp.concatenate([p[2] for p in pieces], axis=-1)[:, None, :]
    lru_bx_f = jnp.concatenate([p[3] for p in pieces], axis=-1)[:, None, :]
    lru_lam_f = jnp.concatenate([p[4] for p in pieces], axis=-1)[:, None, :]
    pool_scale_f = jnp.concatenate([p[5] for p in pieces], axis=-1)[None, :]

    cond = jnp.concatenate([c_all, jnp.broadcast_to(c_ctx[None, :], (N_DEV, D_MODEL))], axis=0)
    b_my = lax.dynamic_slice(b_mod, (0, dev * n_mod), (2, n_mod))[:, None, :]
    mod_part = _mod_fwd(cond, w_mod, b_my, "mod_fwd")
    mod_all, = _all_gather([mod_part], "gather_mod")
    mod = jnp.transpose(mod_all, (1, 2, 0, 3)).reshape(2, 16, 3 * D_MODEL)
    mod_me = lax.dynamic_slice(mod, (0, dev, 0), (2, 1, 3 * D_MODEL))
    sh = [mod_me[i, :, 0:D_MODEL] for i in range(2)]
    sc = [mod_me[i, :, D_MODEL:2 * D_MODEL] for i in range(2)]
    gt = [mod_me[i, :, 2 * D_MODEL:] for i in range(2)]
    shc, scc = mod[0, 8:9, 0:D_MODEL], mod[0, 8:9, D_MODEL:2 * D_MODEL]

    lg = [ln_g[i][None, :] for i in range(2)]
    lb = [ln_b[i][None, :] for i in range(2)]
    lru_p = dict(conv_w=conv_w_f, conv_b=conv_b, wa=lru_wa[0].astype(MXU_DTYPE), wx=lru_wx[0].astype(MXU_DTYPE),
                 ba=lru_ba_f, bx=lru_bx_f, lam=lru_lam_f)
    zero_state = jnp.zeros((1, D_INNER), F32)

    (u0, g0), (wo0,) = _in_proj(xt, sc[0], sh[0], wi0, "in_proj0", sides=[("gather", [w_out[0].astype(MXU_DTYPE)])])
    (uc, _), _ = _in_proj(ctxt, scc, shc, wi0, "in_proj0_ctx")
    (hcf, cf, uvc), _ = _lru_fwd(uc, zero_state, lru_p, 0, "lru_fwd_ctx_f", conv=True)
    (hcb, cbk), _ = _lru_fwd(uvc, zero_state, lru_p, 1, "lru_fwd_ctx_b", conv=False)
    (hf, _, uv0), (wi1,) = _lru_fwd(u0, cf, lru_p, 0, "lru_fwd_f", conv=True,
                                    sides=[("gather", [w_in[1].astype(MXU_DTYPE)])])
    (hb, _), (wo1, pool_w_g) = _lru_fwd(
        uv0, cbk, lru_p, 1, "lru_fwd_b", conv=False,
        sides=[("gather", [w_out[1].astype(MXU_DTYPE), pool_w[0].astype(MXU_DTYPE)])])
    w_in_l = [wi0, wi1]
    w_out_l = [wo0.reshape(D_INNER, D_MODEL), wo1.reshape(D_INNER, D_MODEL)]
    pool_w_f = jnp.transpose(pool_w_g, (1, 0, 2, 3)).reshape(len(POOL_WINDOWS), POOL_GROUP, POOL_GROUP)
    x1, br0 = _out0(hf, hb, g0, xt, gt[0], w_out_l[0], lg[0], lb[0], "out0")
    (u1, g1), _ = _in_proj(x1, sc[1], sh[1], w_in_l[1], "in_proj1", u_dtype=ACT_DTYPE)
    dmix = _pool_mix(u1, False, MXU_DTYPE, "pool_fwd")
    dz1, st1, po1 = _out1(dmix, pool_w_f, pool_scale_f, g1, x1, gt[1], w_out_l[1], lg[1], lb[1], tgt, "out1")
    loss_me = jnp.full((1, LANES), (0.5 / D_MODEL) * jnp.sum(st1[3]), F32)

    core = jnp.reshape(ci, (1,)).astype(jnp.int32)
    wo_view = lambda a: a.reshape(N_DEV, D_INNER // N_DEV, D_MODEL)
    pw_view = lambda a: _blocks_by_device(a, 1).reshape(N_DEV, POOL_GROUP // N_DEV * len(POOL_WINDOWS), POOL_GROUP)
    dd, dg1, gwo1, gpw, gps = _bout1(dz1, dmix, po1, g1, pool_w_f, pool_scale_f, gt[1], w_out_l[1], "bwd_out1")
    du1 = _pool_mix(dd, True, MXU_DTYPE, "pool_bwd")
    (dx1, gwi1, stb1), _ = _bin(du1, dg1, x1, dz1, sc[1], sh[1], w_in_l[1], "bwd_in1")
    bufs1 = [gwi1, wo_view(gwo1), pw_view(gpw)]
    (dz0, dy0, dg0, gwo0, stl0), recv1 = _bout0(dx1, xt, br0, lg[0], hf, hb, g0, gt[0], w_out_l[0], "bwd_out0",
                                                sides=[("sibling", bufs1)])
    pairs1 = [_pair_sum(b, r, core, "reduce_pair_" + n)
              for b, r, n in zip(bufs1, recv1, ["w_in1", "w_out1", "pool_w"])]
    (duvf, gwa_f, gwx_f, gv_f, dh0f), (p_wi1, p_wo1, p_pw, recv_wo0) = _lru_bwd(
        uv0, dy0, hf, cf, zero_state, lru_p, 0, "lru_bwd_f", sides=[("chips", pairs1), ("sibling", [wo_view(gwo0)])])
    pair_wo0 = _pair_sum(wo_view(gwo0), recv_wo0, core, "reduce_pair_w_out0")
    (duvb, gwa_b, gwx_b, gv_b, dh0b), (p_wo0,) = _lru_bwd(
        uv0, dy0, hb, cbk, zero_state, lru_p, 1, "lru_bwd_b", sides=[("chips", [pair_wo0])])
    zero_dh = jnp.zeros(uc.shape, ACT_DTYPE)
    (ducf, gwa_cf, gwx_cf, gv_cf, _), _ = _lru_bwd(uvc, zero_dh, hcf, zero_state, dh0f, lru_p, 0, "lru_bwd_ctx_f")
    (ducb, gwa_cb, gwx_cb, gv_cb, _), _ = _lru_bwd(uvc, zero_dh, hcb, zero_state, dh0b, lru_p, 1, "lru_bwd_ctx_b")

    def pack(sharded, replicated):
        sh_sizes = [int(np.prod(a.shape[1:])) for a in sharded]
        rep_sizes = [a.shape[0] // N_DEV for a in replicated]
        n_flat = sum(sh_sizes) + sum(rep_sizes)
        rows = -(-(-(-n_flat // LANES)) // FLAT_ROWS) * FLAT_ROWS
        buf = jnp.concatenate([a.reshape(N_DEV, -1) for a in sharded + replicated], axis=1)
        return jnp.pad(buf, ((0, 0), (0, rows * LANES - n_flat))).reshape(N_DEV, rows, LANES), sh_sizes, rep_sizes

    def unpack(reduced, sh_sizes, rep_sizes, sh_shapes):
        flat = reduced.reshape(-1)
        offs = np.cumsum([0] + sh_sizes)
        mine = [flat[offs[k]:offs[k + 1]].reshape(s) for k, s in enumerate(sh_shapes)]
        return mine, _to_rows([flat[offs[-1]:offs[-1] + sum(rep_sizes)]], FLAT_ROWS)

    def spread(rep_all, rep_sizes, shapes):
        flat = rep_all.reshape(N_DEV, -1)
        offs = np.cumsum([0] + rep_sizes)
        return [flat[:, offs[k]:offs[k + 1]].reshape(s) for k, s in enumerate(shapes)]

    (du0, cst0), _ = _conv_bwd(duvf, duvb, u0, conv_w_f, "conv_bwd")
    (duc, cstc), _ = _conv_bwd(ducf, ducb, uc, conv_w_f, "conv_bwd_ctx")
    (gwic, stc), _ = _bin(duc, None, ctxt, None, scc, shc, w_in_l[0][:N_WBLK // 2], "bwd_in0_ctx")
    (gx, gwi0, stb0), _ = _bin(du0, dg0, xt, dz0, sc[0], sh[0], w_in_l[0], "bwd_in0", gw_init=gwic)

    zero_row = jnp.zeros((1, D_MODEL), F32)
    dm_me = jnp.stack([
        jnp.concatenate([jnp.concatenate([stb0[1:2], stb0[0:1], stl0[2:3]], axis=1),
                         jnp.concatenate([stc[1:2], stc[0:1], zero_row], axis=1)], axis=0),
        jnp.concatenate([jnp.concatenate([stb1[1:2], stb1[0:1], st1[2:3]], axis=1),
                         jnp.zeros((1, 3 * D_MODEL), F32)], axis=0)])
    dm_g, loss_g = _all_gather([dm_me, loss_me], "gather_dmod")
    loss = jnp.sum(loss_g[:, 0, 0])
    dm_all = jnp.concatenate([jnp.transpose(dm_g[:, :, 0], (1, 0, 2)), jnp.transpose(dm_g[:, :, 1], (1, 0, 2))],
                             axis=1)
    dm_my = lax.dynamic_slice(dm_all, (0, 0, dev * n_mod), (2, 16, n_mod))
    g_w_mod, g_b_mod, gcc_part = _mod_bwd(cond, dm_all, dm_my, w_mod, "mod_bwd")
    g_b_mod = g_b_mod.reshape(b_mod.shape)

    gwa = jnp.stack([gwa_f + gwa_cf, gwa_b + gwa_cb])
    gwx = jnp.stack([gwx_f + gwx_cf, gwx_b + gwx_cb])
    gv = jnp.stack([gv_f + gv_cf, gv_b + gv_cb])
    cst = cst0 + cstc
    misc, m_sh, m_rep = pack(
        [_blocks_by_device(cst[0:4], 1), _blocks_by_device(gv[:, 0], 1), _blocks_by_device(gv[:, 1], 1),
         _blocks_by_device(gv[:, 2], 1), _blocks_by_device(gps[0], 0)],
        [gwa.reshape(-1), gwx.reshape(-1), jnp.stack([stl0[0], st1[0]]).reshape(-1),
         jnp.stack([stl0[1], st1[1]]).reshape(-1), cst[4], gcc_part.reshape(-1)])
    bufs = [gwi0, misc]
    recvs = _sibling_exchange(bufs, "reduce_sibling")
    pairs = [_pair_sum(b, r, core, "reduce_pair_" + n) for b, r, n in zip(bufs, recvs, ["w_in0", "misc"])]
    p_wi0, p_misc = _chip_exchange(pairs, "reduce_chips")
    (g_conv_w, g_lru_ba, g_lru_bx, g_lru_lam, g_pool_scale), rep_mine = unpack(
        _sum4(p_misc, "reduce_sum_misc"), m_sh, m_rep,
        [conv_w.shape, lru_ba.shape, lru_bx.shape, lru_lam.shape, pool_scale.shape])
    rep_all, = _all_gather([rep_mine.astype(WIRE_DTYPE)], "gather_replicated")
    rep_all = rep_all.astype(F32)
    g_lru_wa, g_lru_wx, g_ln_g, g_ln_b, g_conv_b, g_c_ctx = spread(
        rep_all, m_rep, [lru_wa.shape, lru_wx.shape, ln_g.shape, ln_b.shape, conv_b.shape, c_ctx.shape])

    names = ["c_ctx", "w_mod", "b_mod", "w_in", "w_out", "ln_g", "ln_b", "conv_w", "conv_b", "lru_wa", "lru_ba",
             "lru_wx", "lru_bx", "lru_lam", "pool_w", "pool_scale"]
    weights = dict(c_ctx=c_ctx, w_mod=w_mod, b_mod=b_mod, w_in=w_in, w_out=w_out, ln_g=ln_g, ln_b=ln_b,
                   conv_w=conv_w, conv_b=conv_b, lru_wa=lru_wa, lru_ba=lru_ba, lru_wx=lru_wx, lru_bx=lru_bx,
                   lru_lam=lru_lam, pool_w=pool_w, pool_scale=pool_scale)
    mom_m = dict(c_ctx=m_c_ctx, w_mod=m_w_mod, b_mod=m_b_mod, w_in=m_w_in, w_out=m_w_out, ln_g=m_ln_g, ln_b=m_ln_b,
                 conv_w=m_conv_w, conv_b=m_conv_b, lru_wa=m_lru_wa, lru_ba=m_lru_ba, lru_wx=m_lru_wx,
                 lru_bx=m_lru_bx, lru_lam=m_lru_lam, pool_w=m_pool_w, pool_scale=m_pool_scale)
    mom_v = dict(c_ctx=v_c_ctx, w_mod=v_w_mod, b_mod=v_b_mod, w_in=v_w_in, w_out=v_w_out, ln_g=v_ln_g, ln_b=v_ln_b,
                 conv_w=v_conv_w, conv_b=v_conv_b, lru_wa=v_lru_wa, lru_ba=v_lru_ba, lru_wx=v_lru_wx,
                 lru_bx=v_lru_bx, lru_lam=v_lru_lam, pool_w=v_pool_w, pool_scale=v_pool_scale)
    grads = dict(c_ctx=g_c_ctx, w_mod=g_w_mod, b_mod=g_b_mod, ln_g=g_ln_g, ln_b=g_ln_b,
                 conv_w=g_conv_w, conv_b=g_conv_b, lru_wa=g_lru_wa, lru_ba=g_lru_ba, lru_wx=g_lru_wx,
                 lru_bx=g_lru_bx, lru_lam=g_lru_lam)
    grads["pool_scale"] = g_pool_scale
    delta, new_m, new_v = {}, {}, {}

    def update_parts(n, parts, view):
        res = _adamw_parts(weights[n].reshape(view), parts, mom_m[n].reshape(view), mom_v[n].reshape(view),
                           "adamw_" + n)
        grads[n], delta[n], new_m[n], new_v[n] = [r.reshape(weights[n].shape) for r in res]

    update_parts("w_in", [p_wi0, p_wi1], w_in.shape)
    update_parts("w_out", [p_wo0, p_wo1], w_out.shape)
    update_parts("pool_w", [p_pw], (1,) + p_pw.shape[1:])
    for n in ("w_mod", "lru_wa", "lru_wx"):
        shape = weights[n].shape
        view = (int(np.prod(shape[:-1])), shape[-1])
        res = _adamw(weights[n].reshape(view), grads[n].reshape(view), mom_m[n].reshape(view),
                     mom_v[n].reshape(view), "adamw_" + n)
        delta[n], new_m[n], new_v[n] = [r.reshape(shape) for r in res]

    small = [n for n in names if n not in delta]
    shapes = [weights[n].shape for n in small]
    flat = lambda d: _to_rows([d[n] for n in small], FLAT_ROWS)
    res = _adamw(flat(weights), flat(grads), flat(mom_m), flat(mom_v), "adamw_small")
    for d, r in zip((delta, new_m, new_v), res):
        d.update(zip(small, _split_rows(r, shapes)))

    return (loss, gx[None], *[grads[n] for n in names], *[delta[n] for n in names],
            *[new_m[n] for n in names], *[new_v[n] for n in names])
```

```python
import functools

import numpy as np
import jax
import jax.numpy as jnp
from jax import lax
from jax.experimental import pallas as pl
from jax.experimental.pallas import tpu as pltpu

F32 = jnp.float32
BF16 = jnp.bfloat16
MXU_DTYPE = BF16

D_MODEL = 1024
D_INNER = 2048
LRU_BLOCK = 128
GRID_W = 64
POOL_WINDOWS = (2, 4, 8, 16)
POOL_GROUP = 512
ALPHA = float(4 ** 0.25)
LN_EPS = 1e-5
LRU_C = 8.0
N_DEV = 8
N_WBLK = 8
WBLK = 512

ADAM_LR = 0.001
ADAM_B1 = 0.9
ADAM_B2 = 0.999
ADAM_EPS = 1e-08
ADAM_WD = 0.01
ADAM_STEP = 10

LANES = 128
SUBLANES = 8
V7X_VMEM_BYTES = 64 * 1024 * 1024
VMEM_COMPILER_RESERVE = 8 * 1024 * 1024
VMEM_LIMIT = V7X_VMEM_BYTES - VMEM_COMPILER_RESERVE
MESH = pl.DeviceIdType.MESH
ANY = pl.BlockSpec(memory_space=pl.ANY)

TM_MM = 512
TM_LRU = 1024
CB_LRU = 512
N_SEG = 8
SCAN_UNROLL = 4
SCAN_ROW_T = 17
SCAN_ROW_J = 2
SQRT_FLOOR = 1e-30
FLAT_ROWS = 16
ELEMENTWISE_TILE_BYTES = 1 << 20
POOL_TOK = 256
WIRE_DTYPE = BF16
ACT_DTYPE = BF16
H_HALO = 16


def _cparams(**kw):
    return pltpu.CompilerParams(vmem_limit_bytes=VMEM_LIMIT, **kw)


def _my_pos():
    return lax.axis_index("x"), lax.axis_index("y"), lax.axis_index("c")


def _dot(a, b):
    return jnp.dot(a.astype(MXU_DTYPE), b.astype(MXU_DTYPE), preferred_element_type=F32)


def _dot_tn(a, b):
    return lax.dot_general(a.astype(MXU_DTYPE), b.astype(MXU_DTYPE), (((0,), (0,)), ((), ())),
                           preferred_element_type=F32)


def _dot_nt(a, b):
    return lax.dot_general(a.astype(MXU_DTYPE), b.astype(MXU_DTYPE), (((1,), (1,)), ((), ())),
                           preferred_element_type=F32)


def _sigmoid(z):
    return 0.5 * jnp.tanh(0.5 * z) + 0.5


def _log_sigmoid(x):
    y = jnp.exp(-jnp.abs(x))
    u = 1.0 + y
    l1p = jnp.where(u == 1.0, y, jnp.log(u) * (y / jnp.where(u == 1.0, 1.0, u - 1.0)))
    return jnp.minimum(x, 0.0) - l1p


def _rowsum(v):
    return jnp.sum(v, axis=0, keepdims=True)


def _layer_norm_stats(z):
    mu = jnp.mean(z, axis=-1, keepdims=True)
    zc = z - mu
    var = jnp.mean(zc * zc, axis=-1, keepdims=True)
    rstd = lax.rsqrt(var + LN_EPS)
    return zc * rstd, rstd


def _layer_norm_bwd(dy, xhat, rstd, g):
    dxh = dy * g
    m1 = jnp.mean(dxh, axis=-1, keepdims=True)
    m2 = jnp.mean(dxh * xhat, axis=-1, keepdims=True)
    return rstd * (dxh - m1 - xhat * m2)


def _shifted(v, before8, after8, offsets):
    n = v.shape[0]
    ext = jnp.concatenate([before8, v, after8], axis=0)
    total = n + 2 * SUBLANES
    return [pltpu.roll(ext, (-k) % total, 0)[SUBLANES:SUBLANES + n] for k in offsets]


def _rows8(row):
    return jnp.broadcast_to(row, (SUBLANES, row.shape[1]))


def _shift_down(v, first_row):
    return _shifted(v, _rows8(first_row), _rows8(first_row), [-1])[0]


def _shift_up(v, last_row):
    return _shifted(v, _rows8(last_row), _rows8(last_row), [1])[0]


def _all_gather(blocks, name):
    n = len(blocks)

    def body(*refs):
        x_refs, out_refs = refs[:n], refs[n:2 * n]
        send_sems, recv_sems, local_sems = refs[2 * n:]
        x, y, c = _my_pos()
        me, sibling = (x, y, c), (x, y, 1 - c)
        chips = [(1 - x, y), (x, 1 - y), (1 - x, 1 - y)]

        def slot(a, px, py, pc):
            return out_refs[a].at[4 * px + 2 * py + pc]

        def copy(a, k, block, to, src=None):
            return pltpu.make_async_remote_copy(
                src_ref=slot(a, *block) if src is None else src, dst_ref=slot(a, *block),
                send_sem=send_sems.at[a, k], recv_sem=recv_sems.at[a, k], device_id=to, device_id_type=MESH)

        mine = [pltpu.make_async_copy(x_refs[a], slot(a, *me), local_sems.at[a]) for a in range(n)]
        for cp in mine:
            cp.start()
        first = []
        for a in range(n):
            first.append(copy(a, 0, me, sibling, src=x_refs[a]))
            first += [copy(a, 1 + j, me, (*chip, c), src=x_refs[a]) for j, chip in enumerate(chips)]
        for cp in first:
            cp.start()
        passed = []
        for j, chip in enumerate(chips):
            for a in range(n):
                copy(a, 1 + j, (*chip, c), me).wait_recv()
                fwd = copy(a, 4 + j, (*chip, c), sibling)
                fwd.start()
                passed.append(fwd)
        for a in range(n):
            copy(a, 0, sibling, me).wait_recv()
            for j, chip in enumerate(chips):
                copy(a, 4 + j, (*chip, 1 - c), me).wait_recv()
        for cp in first + passed:
            cp.wait_send()
        for cp in mine:
            cp.wait()

    outs = pl.pallas_call(
        body, name=name,
        out_shape=[jax.ShapeDtypeStruct((N_DEV,) + b.shape, b.dtype) for b in blocks],
        in_specs=[ANY] * n, out_specs=[ANY] * n,
        scratch_shapes=[pltpu.SemaphoreType.DMA((n, 7)), pltpu.SemaphoreType.DMA((n, 7)),
                        pltpu.SemaphoreType.DMA((n,))],
    )(*blocks)
    return list(outs)


def _sibling_exchange(bufs, name):
    n = len(bufs)

    def body(*refs):
        srcs, outs = refs[:n], refs[n:2 * n]
        send_sems, recv_sems = refs[2 * n:]
        x, y, c = _my_pos()
        copies = [pltpu.make_async_remote_copy(
            src_ref=srcs[a].at[2 * j + (1 - c)], dst_ref=outs[a].at[j], send_sem=send_sems.at[a, j],
            recv_sem=recv_sems.at[a, j], device_id=(x, y, 1 - c), device_id_type=MESH)
            for a in range(n) for j in range(4)]
        for cp in copies:
            cp.start()
        for cp in copies:
            cp.wait()

    outs = pl.pallas_call(
        body, name=name, out_shape=[jax.ShapeDtypeStruct((4,) + b.shape[1:], b.dtype) for b in bufs],
        in_specs=[ANY] * n, out_specs=[ANY] * n,
        scratch_shapes=[pltpu.SemaphoreType.DMA((n, 4)), pltpu.SemaphoreType.DMA((n, 4))],
    )(*bufs)
    return list(outs)


def _chip_exchange(parts, name):
    n = len(parts)

    def body(*refs):
        srcs, outs = refs[:n], refs[n:2 * n]
        send_sems, recv_sems, local_sems = refs[2 * n:]
        x, y, c = _my_pos()
        jme = 2 * x + y
        peers = [(1 - x, y), (x, 1 - y), (1 - x, 1 - y)]
        local = [pltpu.make_async_copy(srcs[a].at[jme], outs[a].at[jme], local_sems.at[a]) for a in range(n)]
        for cp in local:
            cp.start()

        def copy(a, k, px, py, dst_slot):
            return pltpu.make_async_remote_copy(
                src_ref=srcs[a].at[2 * px + py], dst_ref=outs[a].at[dst_slot], send_sem=send_sems.at[a, k],
                recv_sem=recv_sems.at[a, k], device_id=(px, py, c), device_id_type=MESH)

        sends = [copy(a, k, px, py, jme) for a in range(n) for k, (px, py) in enumerate(peers)]
        for cp in sends:
            cp.start()
        for a in range(n):
            for k, (px, py) in enumerate(peers):
                copy(a, k, px, py, 2 * px + py).wait_recv()
        for cp in sends:
            cp.wait_send()
        for cp in local:
            cp.wait()

    outs = pl.pallas_call(
        body, name=name, out_shape=[jax.ShapeDtypeStruct(p.shape, p.dtype) for p in parts],
        in_specs=[ANY] * n, out_specs=[ANY] * n,
        scratch_shapes=[pltpu.SemaphoreType.DMA((n, 3)), pltpu.SemaphoreType.DMA((n, 3)),
                        pltpu.SemaphoreType.DMA((n,))],
    )(*parts)
    return list(outs)


_SIDE_REMOTE = {"gather": 7, "sibling": 4, "chips": 3}
_FLIPS = [(0, 0, 1), (1, 0, 0), (0, 1, 0), (1, 1, 0), (1, 0, 1), (0, 1, 1), (1, 1, 1)]


def _side_plan(sides):
    inputs, out_shapes, scratch = [], [], []
    for kind, arrays in sides:
        n = len(arrays)
        for a in arrays:
            inputs.append(a)
            shape = {"gather": (N_DEV,) + a.shape, "sibling": (4,) + a.shape[1:], "chips": a.shape}[kind]
            out_shapes.append(jax.ShapeDtypeStruct(shape, a.dtype))
        scratch += [pltpu.SemaphoreType.DMA((n, _SIDE_REMOTE[kind])), pltpu.SemaphoreType.DMA((n, _SIDE_REMOTE[kind])),
                    pltpu.SemaphoreType.DMA((n,))]
    return inputs, out_shapes, scratch


def _side_copies(sides, in_refs, out_refs, sem_refs):
    x, y, c = _my_pos()
    starts, waits = [], []
    pos = 0
    for s, (kind, arrays) in enumerate(sides):
        send_sems, recv_sems, local_sems = sem_refs[3 * s:3 * s + 3]
        for a in range(len(arrays)):
            src, out = in_refs[pos], out_refs[pos]
            pos += 1

            def remote(k, src_ref, dst_ref, to):
                return pltpu.make_async_remote_copy(src_ref=src_ref, dst_ref=dst_ref, send_sem=send_sems.at[a, k],
                                                    recv_sem=recv_sems.at[a, k], device_id=to, device_id_type=MESH)

            def local(src_ref, dst_ref):
                cp = pltpu.make_async_copy(src_ref, dst_ref, local_sems.at[a])
                starts.append(cp.start)
                waits.append(cp.wait)

            if kind == "gather":
                me = 4 * x + 2 * y + c
                local(src, out.at[me])
                for k, (fx, fy, fc) in enumerate(_FLIPS):
                    px, py, pc = (1 - x if fx else x), (1 - y if fy else y), (1 - c if fc else c)
                    send = remote(k, src, out.at[me], (px, py, pc))
                    starts.append(send.start)
                    waits += [remote(k, src, out.at[4 * px + 2 * py + pc], (px, py, pc)).wait_recv, send.wait_send]
            elif kind == "sibling":
                for j in range(4):
                    cp = remote(j, src.at[2 * j + (1 - c)], out.at[j], (x, y, 1 - c))
                    starts.append(cp.start)
                    waits.append(cp.wait)
            else:
                jme = 2 * x + y
                local(src.at[jme], out.at[jme])
                for k, (px, py) in enumerate([(1 - x, y), (x, 1 - y), (1 - x, 1 - y)]):
                    send = remote(k, src.at[2 * px + py], out.at[jme], (px, py, c))
                    starts.append(send.start)
                    waits += [remote(k, src.at[2 * px + py], out.at[2 * px + py], (px, py, c)).wait_recv,
                              send.wait_send]
    return starts, waits


def _call_with_sides(body, sides, *, name, grid, in_specs, out_specs, out_shape, scratch_shapes, compiler_params, args):
    if not sides:
        res = pl.pallas_call(body, name=name, grid=grid, in_specs=in_specs, out_specs=out_specs, out_shape=out_shape,
                             scratch_shapes=scratch_shapes, compiler_params=compiler_params)(*args)
        return list(res), []
    s_in, s_out, s_scr = _side_plan(sides)
    n_in, n_out, n_scr, n_side = len(in_specs), len(out_specs), len(scratch_shapes), len(s_in)

    def wrapped(*refs):
        refs = list(refs)
        ins, side_in = refs[:n_in], refs[n_in:n_in + n_side]
        outs = refs[n_in + n_side:n_in + n_side + n_out]
        side_out = refs[n_in + n_side + n_out:n_in + 2 * n_side + n_out]
        rest = refs[n_in + 2 * n_side + n_out:]
        starts, waits = _side_copies(sides, side_in, side_out, rest[n_scr:])
        first = functools.reduce(jnp.logical_and, [pl.program_id(d) == 0 for d in range(len(grid))])
        last = functools.reduce(jnp.logical_and, [pl.program_id(d) == grid[d] - 1 for d in range(len(grid))])

        @pl.when(first)
        def _():
            for start in starts:
                start()

        body(*ins, *outs, *rest[:n_scr])

        @pl.when(last)
        def _():
            for wait in waits:
                wait()

    res = pl.pallas_call(
        wrapped, name=name, grid=grid, in_specs=list(in_specs) + [ANY] * n_side,
        out_specs=list(out_specs) + [ANY] * n_side, out_shape=list(out_shape) + s_out,
        scratch_shapes=list(scratch_shapes) + s_scr, compiler_params=compiler_params,
    )(*args, *s_in)
    return list(res[:n_out]), list(res[n_out:])


def _row_tile(r, l):
    t = min(r, max(16, ELEMENTWISE_TILE_BYTES // (4 * l) // 16 * 16))
    while r % t:
        t -= 16
    return t


def _pair_sum(buf, recv, core, name):
    _, r, l = buf.shape
    tr = _row_tile(r, l)

    def body(core_ref, a_ref, b_ref, o_ref):
        o_ref[...] = (a_ref[...] + b_ref[...]).astype(WIRE_DTYPE)

    return pl.pallas_call(
        body, name=name, out_shape=jax.ShapeDtypeStruct((4, r, l), WIRE_DTYPE),
        grid_spec=pltpu.PrefetchScalarGridSpec(
            num_scalar_prefetch=1, grid=(4, r // tr),
            in_specs=[pl.BlockSpec((None, tr, l), lambda j, i, cr: (2 * j + cr[0], i, 0)),
                      pl.BlockSpec((None, tr, l), lambda j, i, cr: (j, i, 0))],
            out_specs=pl.BlockSpec((None, tr, l), lambda j, i, cr: (j, i, 0))),
        compiler_params=_cparams(dimension_semantics=("arbitrary", "arbitrary")),
    )(core, buf, recv)


def _sum_parts(p_ref):
    return ((p_ref[0].astype(F32) + p_ref[1].astype(F32)) + (p_ref[2].astype(F32) + p_ref[3].astype(F32)))


def _sum4(parts, name):
    _, r, l = parts.shape
    tr = _row_tile(r, l)

    def body(p_ref, o_ref):
        o_ref[...] = _sum_parts(p_ref)

    return pl.pallas_call(
        body, name=name, out_shape=jax.ShapeDtypeStruct((r, l), F32), grid=(r // tr,),
        in_specs=[pl.BlockSpec((4, tr, l), lambda i: (0, i, 0))],
        out_specs=pl.BlockSpec((tr, l), lambda i: (i, 0)),
        compiler_params=_cparams(dimension_semantics=("arbitrary",)),
    )(parts)


def _adamw_update(w, gg, m, v):
    nm = ADAM_B1 * m + (1.0 - ADAM_B1) * gg
    nv = ADAM_B2 * v + (1.0 - ADAM_B2) * (gg * gg)
    m_hat = nm / (1.0 - ADAM_B1 ** ADAM_STEP)
    v_hat = nv / (1.0 - ADAM_B2 ** ADAM_STEP)
    return -ADAM_LR * (m_hat / (jnp.sqrt(v_hat) + ADAM_EPS) + ADAM_WD * w), nm, nv


def _adamw(w, g, m, v, name):
    r, l = w.shape
    tr = _row_tile(r, l)

    def body(w_ref, g_ref, m_ref, v_ref, d_ref, nm_ref, nv_ref):
        d_ref[...], nm_ref[...], nv_ref[...] = _adamw_update(w_ref[...], g_ref[...], m_ref[...], v_ref[...])

    spec = pl.BlockSpec((tr, l), lambda i: (i, 0))
    return pl.pallas_call(
        body, name=name, out_shape=[jax.ShapeDtypeStruct((r, l), F32)] * 3, grid=(r // tr,),
        in_specs=[spec] * 4, out_specs=[spec] * 3,
        compiler_params=_cparams(dimension_semantics=("arbitrary",)),
    )(w, g, m, v)


def _adamw_parts(w, parts, m, v, name):
    nl, r, l = w.shape
    tr = _row_tile(r, l)

    def body(*refs):
        w_ref, p_refs, (m_ref, v_ref, g_ref, d_ref, nm_ref, nv_ref) = refs[0], refs[1:1 + nl], refs[1 + nl:]
        layer = pl.program_id(0)
        gg = _sum_parts(p_refs[0])
        for q in range(1, nl):
            gg = jnp.where(layer == q, _sum_parts(p_refs[q]), gg)
        g_ref[...] = gg
        d_ref[...], nm_ref[...], nv_ref[...] = _adamw_update(w_ref[...], gg, m_ref[...], v_ref[...])

    spec = pl.BlockSpec((None, tr, l), lambda q, i: (q, i, 0))
    pspecs = [pl.BlockSpec((4, tr, l), lambda q, i, k=k: (0, jnp.where(q == k, i, 0), 0)) for k in range(nl)]
    return pl.pallas_call(
        body, name=name, out_shape=[jax.ShapeDtypeStruct((nl, r, l), F32)] * 4, grid=(nl, r // tr),
        in_specs=[spec] + pspecs + [spec, spec], out_specs=[spec] * 4,
        compiler_params=_cparams(dimension_semantics=("arbitrary", "arbitrary")),
    )(w, *parts, m, v)


def _to_rows(pieces, row_multiple):
    flat = jnp.concatenate([p.reshape(-1) for p in pieces])
    rows = -(-flat.shape[0] // LANES)
    rows = -(-rows // row_multiple) * row_multiple
    flat = jnp.pad(flat, (0, rows * LANES - flat.shape[0]))
    return flat.reshape(rows, LANES)


def _split_rows(rows, shapes):
    flat = rows.reshape(-1)
    out, off = [], 0
    for s in shapes:
        n = int(np.prod(s))
        out.append(flat[off:off + n].reshape(s))
        off += n
    return out


def _mod_fwd(cond, w_mod, b_my, name):
    nl, _, ncol = w_mod.shape

    def body(a_ref, w_ref, b_ref, o_ref):
        a = a_ref[...]
        s = a * _sigmoid(a)
        for i in range(nl):
            o_ref[i] = _dot(s, w_ref[i]) + b_ref[i]

    return pl.pallas_call(
        body, name=name, out_shape=jax.ShapeDtypeStruct((nl, 16, ncol), F32),
        compiler_params=_cparams(),
    )(cond, w_mod, b_my)


def _mod_bwd(cond, dm_all, dm_my, w_mod, name):
    nl, _, ncol = w_mod.shape

    def body(a_ref, dma_ref, dmm_ref, w_ref, gw_ref, gb_ref, gc_ref):
        a = a_ref[...]
        sg = _sigmoid(a)
        s = a * sg
        for i in range(nl):
            gw_ref[i] = _dot_tn(s, dmm_ref[i])
            gb_ref[i] = jnp.sum(dma_ref[i], axis=0, keepdims=True)
        back = _dot_nt(dmm_ref[0], w_ref[0])
        dsilu = sg * (1.0 + a * (1.0 - sg))
        gc_ref[...] = jnp.sum(back[8:16] * dsilu[8:16], axis=0, keepdims=True)

    return pl.pallas_call(
        body, name=name,
        out_shape=[jax.ShapeDtypeStruct((nl, D_MODEL, ncol), F32), jax.ShapeDtypeStruct((nl, 1, 3 * D_MODEL), F32),
                   jax.ShapeDtypeStruct((1, D_MODEL), F32)],
        compiler_params=_cparams(),
    )(cond, dm_all, dm_my, w_mod)


def _in_proj(xt, sc, sh, wg, name, sides=()):
    t = xt.shape[0]
    tm = min(TM_MM, t)

    def body(x_ref, sc_ref, sh_ref, w_ref, u_ref, g_ref):
        h = (x_ref[...] * (1.0 + sc_ref[...]) + sh_ref[...]).astype(MXU_DTYPE)
        for k in range(N_WBLK):
            o = jnp.dot(h, w_ref[k], preferred_element_type=F32)
            if k < N_WBLK // 2:
                u_ref[:, k * WBLK:(k + 1) * WBLK] = o
            else:
                kk = k - N_WBLK // 2
                g_ref[:, kk * WBLK:(kk + 1) * WBLK] = o.astype(ACT_DTYPE)

    row = pl.BlockSpec((1, D_MODEL), lambda i: (0, 0))
    return _call_with_sides(
        body, sides, name=name,
        out_shape=[jax.ShapeDtypeStruct((t, D_INNER), F32), jax.ShapeDtypeStruct((t, D_INNER), ACT_DTYPE)],
        grid=(t // tm,),
        in_specs=[pl.BlockSpec((tm, D_MODEL), lambda i: (i, 0)), row, row,
                  pl.BlockSpec((N_WBLK, D_MODEL, WBLK), lambda i: (0, 0, 0), pipeline_mode=pl.Buffered(1))],
        out_specs=[pl.BlockSpec((tm, D_INNER), lambda i: (i, 0))] * 2, scratch_shapes=[],
        compiler_params=_cparams(dimension_semantics=("arbitrary",)), args=[xt, sc, sh, wg])


def _halo_maps(nt, tm, n_blocks, pos, rows=SUBLANES):
    per = tm // rows
    prev = lambda cb, i: (jnp.maximum(pos(i) * per - 1, 0), cb)
    nxt = lambda cb, i: (jnp.minimum((pos(i) + 1) * per, n_blocks - 1), cb)
    return prev, nxt


def _conv_taps(u, prev8, next8, is_first, is_last):
    pz = jnp.where(is_first, 0.0, 1.0)
    nz = jnp.where(is_last, 0.0, 1.0)
    return _shifted(u, prev8 * pz, next8 * nz, [-2, -1, 1])


def _lru_gates(uv, wa_ref, wx_ref, ba, bx, cl, g):
    sl = slice(g * LANES, (g + 1) * LANES)
    uvg = uv[:, sl]
    r = _sigmoid(_dot(uvg, wa_ref[g]) + ba[:, sl])
    ii = _sigmoid(_dot(uvg, wx_ref[g]) + bx[:, sl])
    la = cl[:, sl] * r
    a = jnp.exp(la)
    q = jnp.tanh(-la) * (1.0 + a * a)
    rs = lax.rsqrt(jnp.maximum(q, SQRT_FLOOR))
    return uvg, r, ii, a, q * rs, rs


def _scan_rows(seg):
    return -(-(SCAN_ROW_T * (seg - 1) + SCAN_ROW_J * (N_SEG - 1) + 1) // SUBLANES) * SUBLANES


def _seg_chunk(j, c):
    return pl.ds(SCAN_ROW_T * SUBLANES * c + SCAN_ROW_J * j, SUBLANES, stride=SCAN_ROW_T)


def _seg_scatter(ref, g, seg, value):
    for j in range(N_SEG):
        for c in range(seg // SUBLANES):
            r0 = j * seg + SUBLANES * c
            ref[g, _seg_chunk(j, c), :] = value[r0:r0 + SUBLANES]


def _scan_tile(a_s, b_s, carry_ref, write_out, seg, reverse, chunks_per_write=1):
    n_g = a_s.shape[0]
    unroll = SCAN_UNROLL if seg % SCAN_UNROLL == 0 else 1

    n_trips = seg // unroll

    def steps(k, state):
        hs, cs = list(state[0]), list(state[1])
        base = ((n_trips - 1 - k) if reverse else k) * unroll
        for q in (range(unroll - 1, -1, -1) if reverse else range(unroll)):
            t = base + q
            rows = pl.ds(t * SCAN_ROW_T, N_SEG, stride=SCAN_ROW_J)
            for g in range(n_g):
                a = a_s[g, rows, :]
                b = b_s[g, rows, :]
                hs[g] = a * hs[g] + b
                cs[g] = a * cs[g]
                b_s[g, rows, :] = hs[g]
                a_s[g, rows, :] = cs[g]
        return tuple(hs), tuple(cs)

    zeros = tuple(jnp.zeros((N_SEG, LANES), F32) for _ in range(n_g))
    ones = tuple(jnp.ones((N_SEG, LANES), F32) for _ in range(n_g))
    h_fin, a_fin = lax.fori_loop(0, seg // unroll, steps, (zeros, ones))

    order = list(range(N_SEG - 1, -1, -1)) if reverse else list(range(N_SEG))
    for g in range(n_g):
        carry = carry_ref[:, g * LANES:(g + 1) * LANES]
        for j in order:
            for c0 in range(0, seg // SUBLANES, chunks_per_write):
                parts = [b_s[g, _seg_chunk(j, c), :] + a_s[g, _seg_chunk(j, c), :] * carry
                         for c in range(c0, c0 + chunks_per_write)]
                write_out(j, c0, g, parts[0] if chunks_per_write == 1 else jnp.concatenate(parts, axis=0))
            carry = a_fin[g][j:j + 1] * carry + h_fin[g][j:j + 1]
        carry_ref[:, g * LANES:(g + 1) * LANES] = carry


def _lru_specs(s, tm, cb, direction_pos, nt):
    n_rows8 = s // SUBLANES
    prev, nxt = _halo_maps(nt, tm, n_rows8, direction_pos)
    tile = pl.BlockSpec((tm, cb), lambda c, i: (direction_pos(i), c))
    return tile, pl.BlockSpec((SUBLANES, cb), prev), pl.BlockSpec((SUBLANES, cb), nxt)


def _lru_param_specs(cb, d):
    n_g = cb // LANES
    vec = pl.BlockSpec((1, cb), lambda c, i: (0, c))
    dvec = pl.BlockSpec((None, 1, cb), lambda c, i: (d, 0, c))
    wmat = pl.BlockSpec((None, n_g, LRU_BLOCK, LRU_BLOCK), lambda c, i: (d, c, 0, 0))
    return vec, dvec, wmat


def _lru_fwd(src, h0, p, d, name, conv, sides=()):
    s = src.shape[0]
    tm = min(TM_LRU, s)
    cb = CB_LRU
    n_g = cb // LANES
    nt = s // tm
    seg = tm // N_SEG
    pos = (lambda i: i) if d == 0 else (lambda i: nt - 1 - i)

    def body(*refs):
        refs = list(refs)
        u_ref = refs.pop(0)
        if conv:
            up_ref, un_ref, cw_ref, cbias_ref = [refs.pop(0) for _ in range(4)]
        wa_ref, wx_ref, ba_ref, bx_ref, lam_ref, h0_ref, h_ref, hc_ref = [refs.pop(0) for _ in range(8)]
        uv_ref = refs.pop(0) if conv else None
        a_s, b_s = refs
        i = pl.program_id(1)
        tp = pos(i)

        @pl.when(i == 0)
        def _():
            hc_ref[...] = h0_ref[...]

        if conv:
            u_t = u_ref[...]
            um2, um1, up1 = _conv_taps(u_t, up_ref[...], un_ref[...], tp == 0, tp == nt - 1)
            cw = cw_ref[...]
            uv_ref[...] = um2 * cw[0:1] + um1 * cw[1:2] + u_t * cw[2:3] + up1 * cw[3:4] + cbias_ref[...]
        src_ref = uv_ref if conv else u_ref
        cl = LRU_C * _log_sigmoid(lam_ref[...])
        ba, bx = ba_ref[...], bx_ref[...]
        for g in range(n_g):
            uvg, r, ii, a, sq, _ = _lru_gates(src_ref, wa_ref, wx_ref, ba, bx, cl, g)
            b = sq * (ii * uvg)
            _seg_scatter(a_s, g, seg, a)
            _seg_scatter(b_s, g, seg, b)

        per_write = 2 if (seg // SUBLANES) % 2 == 0 else 1

        def write_out(j, c, g, h):
            h_ref[pl.ds(j * seg + SUBLANES * c, SUBLANES * per_write), pl.ds(g * LANES, LANES)] = h.astype(ACT_DTYPE)

        _scan_tile(a_s, b_s, hc_ref, write_out, seg, reverse=(d == 1), chunks_per_write=per_write)

    tile, prev, nxt = _lru_specs(s, tm, cb, pos, nt)
    vec, dvec, wmat = _lru_param_specs(cb, d)
    wide = jax.ShapeDtypeStruct((s, D_INNER), F32)
    conv_specs = [prev, nxt, pl.BlockSpec((4, cb), lambda c, i: (0, c)), vec] if conv else []
    conv_args = [src, src, p["conv_w"], p["conv_b"]] if conv else []
    return _call_with_sides(
        body, sides, name=name,
        out_shape=[jax.ShapeDtypeStruct((s, D_INNER), ACT_DTYPE), jax.ShapeDtypeStruct((1, D_INNER), F32)]
        + ([wide] if conv else []),
        grid=(D_INNER // cb, nt),
        in_specs=[tile] + conv_specs + [wmat, wmat, dvec, dvec, dvec, vec],
        out_specs=[tile, vec] + ([tile] if conv else []),
        scratch_shapes=[pltpu.VMEM((n_g, _scan_rows(seg), LANES), F32)] * 2,
        compiler_params=_cparams(dimension_semantics=("arbitrary", "arbitrary")),
        args=[src, *conv_args, p["wa"], p["wx"], p["ba"], p["bx"], p["lam"], h0])


def _lru_bwd(uv, dh, h, h0, lam_in, p, d, name, sides=()):
    s = uv.shape[0]
    tm = min(TM_LRU, s)
    cb = CB_LRU
    n_g = cb // LANES
    nt = s // tm
    seg = tm // N_SEG
    pos = (lambda i: nt - 1 - i) if d == 0 else (lambda i: i)

    def body(uv_ref, dh_ref, h_ref, hh_ref, wa_ref, wx_ref, ba_ref, bx_ref,
             lam_ref, h0_ref, lin_ref, duv_ref, gwa_ref, gwx_ref, gv_ref, lc_ref, a_s, b_s, lp_s,
             r_s, i_s, q_s, rq_s, a_keep):
        i = pl.program_id(1)
        tp = pos(i)

        @pl.when(i == 0)
        def _():
            lc_ref[...] = lin_ref[...]
            gwa_ref[...] = jnp.zeros_like(gwa_ref)
            gwx_ref[...] = jnp.zeros_like(gwx_ref)
            gv_ref[...] = jnp.zeros_like(gv_ref)

        uv = uv_ref[...]
        lam = lam_ref[...]
        cl = LRU_C * _log_sigmoid(lam)
        ba, bx = ba_ref[...], bx_ref[...]
        dh_t = dh_ref[...].astype(F32)
        carry_in = lc_ref[...]
        for g in range(n_g):
            sl = slice(g * LANES, (g + 1) * LANES)
            _, r, ii, a, sq, rs = _lru_gates(uv, wa_ref, wx_ref, ba, bx, cl, g)
            for ref, val in ((r_s, r), (i_s, ii), (q_s, sq), (rq_s, rs), (a_keep, a)):
                ref[:, sl] = val.astype(ACT_DTYPE)
            b = a * dh_t[:, sl]
            _seg_scatter(a_s, g, seg, a)
            _seg_scatter(b_s, g, seg, b)

        def write_out(j, c, g, v):
            lp_s[pl.ds(j * seg + SUBLANES * c, SUBLANES), pl.ds(g * LANES, LANES)] = v

        _scan_tile(a_s, b_s, lc_ref, write_out, seg, reverse=(d == 0))

        h_t = h_ref[...].astype(F32)
        hh = hh_ref[...].astype(F32)
        if d == 0:
            edge = jnp.where(tp == 0, h0_ref[...], hh[H_HALO - 1:H_HALO])
            h_prev = _shift_down(h_t, edge)
            lam_t = dh_t + _shift_up(lp_s[...], carry_in)
        else:
            edge = jnp.where(tp == nt - 1, h0_ref[...], hh[0:1])
            h_prev = _shift_up(h_t, edge)
            lam_t = dh_t + _shift_down(lp_s[...], carry_in)

        dsig = LRU_C * _sigmoid(-lam)
        for g in range(n_g):
            sl = slice(g * LANES, (g + 1) * LANES)
            uvg = uv[:, sl]
            r, ii, a, sq, rs = [ref[:, sl].astype(F32) for ref in (r_s, i_s, a_keep, q_s, rq_s)]
            lt = lam_t[:, sl]
            ls = lt * sq
            dla = (lt * a) * (h_prev[:, sl] - (ii * uvg) * (a * rs))
            dzr = (dla * cl[:, sl]) * r * (1.0 - r)
            dzi = (ls * uvg) * ii * (1.0 - ii)
            duv_ref[:, sl] = (ls * ii + _dot_nt(dzr, wa_ref[g]) + _dot_nt(dzi, wx_ref[g])).astype(ACT_DTYPE)
            gwa_ref[g] += _dot_tn(uvg, dzr)
            gwx_ref[g] += _dot_tn(uvg, dzi)
            gv_ref[0:1, sl] += _rowsum(dzr)
            gv_ref[1:2, sl] += _rowsum(dzi)
            gv_ref[2:3, sl] += _rowsum(dla * r) * dsig[:, sl]

    tile, _, _ = _lru_specs(s, tm, cb, pos, nt)
    vec, dvec, wmat = _lru_param_specs(cb, d)
    h_prev_map, h_next_map = _halo_maps(nt, tm, s // H_HALO, pos, rows=H_HALO)
    hh_spec = pl.BlockSpec((H_HALO, cb), h_prev_map if d == 0 else h_next_map)
    gw_spec = pl.BlockSpec((n_g, LRU_BLOCK, LRU_BLOCK), lambda c, i: (c, 0, 0))
    n_blk = D_INNER // LRU_BLOCK
    return _call_with_sides(
        body, sides, name=name,
        out_shape=[jax.ShapeDtypeStruct((s, D_INNER), ACT_DTYPE),
                   jax.ShapeDtypeStruct((n_blk, LRU_BLOCK, LRU_BLOCK), F32),
                   jax.ShapeDtypeStruct((n_blk, LRU_BLOCK, LRU_BLOCK), F32),
                   jax.ShapeDtypeStruct((SUBLANES, D_INNER), F32),
                   jax.ShapeDtypeStruct((1, D_INNER), F32)],
        grid=(D_INNER // cb, nt),
        in_specs=[tile, tile, tile, hh_spec, wmat, wmat, dvec, dvec, dvec, vec, vec],
        out_specs=[tile, gw_spec, gw_spec, pl.BlockSpec((SUBLANES, cb), lambda c, i: (0, c)), vec],
        scratch_shapes=[pltpu.VMEM((n_g, _scan_rows(seg), LANES), F32)] * 2 + [pltpu.VMEM((tm, cb), F32)]
        + [pltpu.VMEM((tm, cb), ACT_DTYPE)] * 5,
        compiler_params=_cparams(dimension_semantics=("arbitrary", "arbitrary")),
        args=[uv, dh, h, h, p["wa"], p["wx"], p["ba"], p["bx"], p["lam"], h0, lam_in])


def _out0(hf, hb, g, xt, gt, wo, lg, lb, sc1, sh1, wg1, name):
    t = xt.shape[0]
    tm = min(TM_MM, t)

    def body(hf_ref, hb_ref, g_ref, x_ref, gt_ref, w_ref, lg_ref, lb_ref, sc1_ref, sh1_ref, w1_ref,
             x1_ref, br_ref, u1_ref, g1_ref):
        br = None
        for k in range(D_INNER // WBLK):
            sl = slice(k * WBLK, (k + 1) * WBLK)
            gg = g_ref[:, sl].astype(F32)
            p = (hf_ref[:, sl].astype(F32) + hb_ref[:, sl].astype(F32)) * (gg * _sigmoid(gg))
            part = _dot(p, w_ref[sl, :])
            br = part if br is None else br + part
        z = ALPHA * x_ref[...] + gt_ref[...] * br
        xhat, _ = _layer_norm_stats(z)
        x1 = xhat * lg_ref[...] + lb_ref[...]
        x1_ref[...] = x1
        br_ref[...] = br.astype(ACT_DTYPE)
        h1 = (x1 * (1.0 + sc1_ref[...]) + sh1_ref[...]).astype(MXU_DTYPE)
        for k in range(N_WBLK):
            o = jnp.dot(h1, w1_ref[k], preferred_element_type=F32).astype(ACT_DTYPE)
            if k < N_WBLK // 2:
                u1_ref[:, k * WBLK:(k + 1) * WBLK] = o
            else:
                kk = k - N_WBLK // 2
                g1_ref[:, kk * WBLK:(kk + 1) * WBLK] = o

    wide = pl.BlockSpec((tm, D_INNER), lambda i: (i, 0))
    nar = pl.BlockSpec((tm, D_MODEL), lambda i: (i, 0))
    row = pl.BlockSpec((1, D_MODEL), lambda i: (0, 0))
    return pl.pallas_call(
        body, name=name,
        out_shape=[jax.ShapeDtypeStruct((t, D_MODEL), F32), jax.ShapeDtypeStruct((t, D_MODEL), ACT_DTYPE),
                   jax.ShapeDtypeStruct((t, D_INNER), ACT_DTYPE), jax.ShapeDtypeStruct((t, D_INNER), ACT_DTYPE)],
        grid=(t // tm,),
        in_specs=[wide, wide, wide, nar, row,
                  pl.BlockSpec((D_INNER, D_MODEL), lambda i: (0, 0), pipeline_mode=pl.Buffered(1)), row, row, row, row,
                  pl.BlockSpec((N_WBLK, D_MODEL, WBLK), lambda i: (0, 0, 0), pipeline_mode=pl.Buffered(1))],
        out_specs=[nar, nar, wide, wide],
        compiler_params=_cparams(dimension_semantics=("arbitrary",)),
    )(hf, hb, g, xt, gt, wo, lg, lb, sc1, sh1, wg1)


def _unrolled_loop(n, fn, unroll=4):
    while n % unroll:
        unroll //= 2

    def trip(k, carry):
        for q in range(unroll):
            fn(k * unroll + q)
        return carry
    lax.fori_loop(0, n // unroll, trip, 0)


def _window(n, w):
    t = np.arange(n)
    return np.clip(t - w // 2, 0, n), np.clip(t + w // 2, 0, n)


def _pool_tables(n_rows, transpose):
    boxes, inv_c, inv_r = [], [], []
    for w in POOL_WINDOWS:
        lo, hi = _window(GRID_W, w)
        m = np.zeros((GRID_W, GRID_W), np.float32)
        for r in range(GRID_W):
            m[r, lo[r]:hi[r]] = 1.0
        m = np.kron(np.eye(POOL_TOK // GRID_W, dtype=np.float32), m)
        boxes.append(m.T if transpose else m)
        inv_c.append(np.broadcast_to((1.0 / (hi - lo).astype(np.float32))[:, None], (GRID_W, LANES)))
        lo_r, hi_r = _window(n_rows, w)
        inv_r.append(1.0 / (hi_r - lo_r).astype(np.float32))
    return (jnp.asarray(np.stack(boxes), MXU_DTYPE), jnp.asarray(np.stack(inv_c), F32),
            jnp.asarray(np.stack(inv_r), F32))


def _pool_mix(xin, transpose, out_dtype, name):
    s = xin.shape[0]
    n_rows = s // GRID_W
    pad_t = SUBLANES * GRID_W
    rows_per_blk = POOL_TOK // GRID_W
    n_slab = D_INNER // LANES
    slabs_per_group = POOL_GROUP // LANES
    n_win = len(POOL_WINDOWS)
    boxes, inv_c, inv_r = _pool_tables(n_rows, transpose)
    exact_operand = (not transpose) and xin.dtype == MXU_DTYPE and MXU_DTYPE != F32

    def body(invr_ref, box_ref, invc_ref, x_ref, o_ref, pad_s):
        k = pl.program_id(0) // slabs_per_group
        pad_s[pl.ds(0, pad_t), :] = jnp.zeros((pad_t, LANES), F32)
        pad_s[pl.ds(pad_t + s, pad_t), :] = jnp.zeros((pad_t, LANES), F32)

        for kk, w in enumerate(POOL_WINDOWS):
            half = w // 2
            offsets = list(range(-(half - 1), half + 1)) if transpose else list(range(-half, half))

            @pl.when(k == kk)
            def _():
                inv_col = invc_ref[kk]

                def col_box(b):
                    st = pl.multiple_of(b * POOL_TOK, POOL_TOK)
                    xb = x_ref[pl.ds(st, POOL_TOK), :]
                    if exact_operand:
                        pad_s[pl.ds(pad_t + st, POOL_TOK), :] = jnp.dot(box_ref[kk], xb, preferred_element_type=F32)
                        return
                    xb = xb.astype(F32)
                    if transpose:
                        xb = xb * jnp.concatenate(
                            [inv_col * invr_ref[kk, b * rows_per_blk + q] for q in range(rows_per_blk)], axis=0)
                    hi = xb.astype(MXU_DTYPE)
                    lo = (xb - hi.astype(F32)).astype(MXU_DTYPE)
                    both = jnp.dot(box_ref[kk], jnp.concatenate([hi, lo], axis=1), preferred_element_type=F32)
                    pad_s[pl.ds(pad_t + st, POOL_TOK), :] = both[:, :LANES] + both[:, LANES:]
                _unrolled_loop(s // POOL_TOK, col_box)

                def row_box(r):
                    st = pl.multiple_of(r * GRID_W, GRID_W)
                    acc = pad_s[pl.ds(pad_t + st + offsets[0] * GRID_W, GRID_W), :]
                    for o in offsets[1:]:
                        acc = acc + pad_s[pl.ds(pad_t + st + o * GRID_W, GRID_W), :]
                    if not transpose:
                        acc = acc * (inv_col * invr_ref[kk, r])
                    o_ref[pl.ds(st, GRID_W), :] = (acc - x_ref[pl.ds(st, GRID_W), :].astype(F32)).astype(out_dtype)
                _unrolled_loop(n_rows, row_box)

    slab = pl.BlockSpec((s, LANES), lambda i: (0, i))
    return pl.pallas_call(
        body, name=name, out_shape=jax.ShapeDtypeStruct((s, D_INNER), out_dtype), grid=(n_slab,),
        in_specs=[pl.BlockSpec(memory_space=pltpu.SMEM),
                  pl.BlockSpec((n_win, POOL_TOK, POOL_TOK), lambda i: (0, 0, 0)),
                  pl.BlockSpec((n_win, GRID_W, LANES), lambda i: (0, 0, 0)), slab],
        out_specs=slab,
        scratch_shapes=[pltpu.VMEM((s + 2 * pad_t, LANES), F32)],
        compiler_params=_cparams(dimension_semantics=("arbitrary",)),
    )(inv_r, boxes, inv_c, xin)


def _out1(dmix, pw, ps, g, x1, gt, wo, lg, lb, tgt, name):
    t = x1.shape[0]
    tm = min(TM_MM, t)
    n_grp = len(POOL_WINDOWS)

    def body(d_ref, pw_ref, ps_ref, g_ref, x1_ref, gt_ref, w_ref, lg_ref, lb_ref, tgt_ref, dz_ref, st_ref, po_ref):
        @pl.when(pl.program_id(0) == 0)
        def _():
            st_ref[...] = jnp.zeros_like(st_ref)

        br = jnp.zeros((tm, D_MODEL), F32)
        for k in range(n_grp):
            sl = slice(k * POOL_GROUP, (k + 1) * POOL_GROUP)
            po = jnp.dot(d_ref[:, sl], pw_ref[k], preferred_element_type=F32)
            po_ref[:, sl] = po.astype(ACT_DTYPE)
            y = po * ps_ref[:, sl]
            gg = g_ref[:, sl].astype(F32)
            br = br + _dot(y * (gg * _sigmoid(gg)), w_ref[sl, :])
        z = ALPHA * x1_ref[...] + gt_ref[...] * br
        xhat, rstd = _layer_norm_stats(z)
        lg_v = lg_ref[...]
        err = xhat * lg_v + lb_ref[...] - tgt_ref[...]
        dy = err * (1.0 / D_MODEL)
        dz = _layer_norm_bwd(dy, xhat, rstd, lg_v)
        dz_ref[...] = dz
        st_ref[0:1, :] += _rowsum(dy * xhat)
        st_ref[1:2, :] += _rowsum(dy)
        st_ref[2:3, :] += _rowsum(dz * br)
        st_ref[3:4, :] += _rowsum(err * err)

    wide = pl.BlockSpec((tm, D_INNER), lambda i: (i, 0))
    nar = pl.BlockSpec((tm, D_MODEL), lambda i: (i, 0))
    row = pl.BlockSpec((1, D_MODEL), lambda i: (0, 0))
    return pl.pallas_call(
        body, name=name,
        out_shape=[jax.ShapeDtypeStruct((t, D_MODEL), F32), jax.ShapeDtypeStruct((SUBLANES, D_MODEL), F32),
                   jax.ShapeDtypeStruct((t, D_INNER), ACT_DTYPE)],
        grid=(t // tm,),
        in_specs=[wide, pl.BlockSpec((n_grp, POOL_GROUP, POOL_GROUP), lambda i: (0, 0, 0)),
                  pl.BlockSpec((1, D_INNER), lambda i: (0, 0)), wide, nar, row,
                  pl.BlockSpec((D_INNER, D_MODEL), lambda i: (0, 0), pipeline_mode=pl.Buffered(1)), row, row, nar],
        out_specs=[nar, pl.BlockSpec((SUBLANES, D_MODEL), lambda i: (0, 0)), wide],
        compiler_params=_cparams(dimension_semantics=("arbitrary",)),
    )(dmix, pw, ps, g, x1, gt, wo, lg, lb, tgt)


def _flush(acc, out_hbm, sem):
    cp = pltpu.make_async_copy(acc, out_hbm, sem)
    cp.start()
    cp.wait()


def _bout1(dz, dmix, po, g, pw, ps, gt, wo, name):
    t = dz.shape[0]
    tm = min(TM_MM, t)
    nt = t // tm
    n_grp = len(POOL_WINDOWS)

    def body(dz_ref, d_ref, po_ref, g_ref, pw_ref, ps_ref, gt_ref, w_ref, dd_ref, dg_ref, gwo_hbm, gpw_hbm, gps_ref,
             gwo_acc, gpw_acc, sems):
        i = pl.program_id(0)

        @pl.when(i == 0)
        def _():
            gwo_acc[...] = jnp.zeros_like(gwo_acc)
            gpw_acc[...] = jnp.zeros_like(gpw_acc)
            gps_ref[...] = jnp.zeros_like(gps_ref)

        db = (gt_ref[...] * dz_ref[...]).astype(MXU_DTYPE)
        for k in range(n_grp):
            sl = slice(k * POOL_GROUP, (k + 1) * POOL_GROUP)
            dk = d_ref[:, sl]
            po = po_ref[:, sl].astype(F32)
            psk = ps_ref[:, sl]
            y = po * psk
            gg = g_ref[:, sl].astype(F32)
            sg = _sigmoid(gg)
            silu = gg * sg
            gwo_acc[sl, :] += _dot_tn(y * silu, db)
            dp = _dot_nt(db, w_ref[sl, :])
            dy = dp * silu
            dg_ref[:, sl] = (dp * y * (sg * (1.0 + gg * (1.0 - sg)))).astype(MXU_DTYPE)
            gps_ref[0:1, sl] += _rowsum(dy * po)
            dpo = (dy * psk).astype(MXU_DTYPE)
            gpw_acc[k] += _dot_tn(dk, dpo)
            dd_ref[:, sl] = _dot_nt(dpo, pw_ref[k])

        @pl.when(i == nt - 1)
        def _():
            _flush(gwo_acc, gwo_hbm, sems.at[0])
            _flush(gpw_acc, gpw_hbm, sems.at[1])

    wide = pl.BlockSpec((tm, D_INNER), lambda i: (i, 0))
    nar = pl.BlockSpec((tm, D_MODEL), lambda i: (i, 0))
    return pl.pallas_call(
        body, name=name,
        out_shape=[jax.ShapeDtypeStruct((t, D_INNER), F32), jax.ShapeDtypeStruct((t, D_INNER), MXU_DTYPE),
                   jax.ShapeDtypeStruct((D_INNER, D_MODEL), F32),
                   jax.ShapeDtypeStruct((n_grp, POOL_GROUP, POOL_GROUP), F32),
                   jax.ShapeDtypeStruct((SUBLANES, D_INNER), F32)],
        grid=(nt,),
        in_specs=[nar, wide, wide, wide,
                  pl.BlockSpec((n_grp, POOL_GROUP, POOL_GROUP), lambda i: (0, 0, 0), pipeline_mode=pl.Buffered(1)),
                  pl.BlockSpec((1, D_INNER), lambda i: (0, 0)), pl.BlockSpec((1, D_MODEL), lambda i: (0, 0)),
                  pl.BlockSpec((D_INNER, D_MODEL), lambda i: (0, 0), pipeline_mode=pl.Buffered(1))],
        out_specs=[wide, wide, ANY, ANY, pl.BlockSpec((SUBLANES, D_INNER), lambda i: (0, 0))],
        scratch_shapes=[pltpu.VMEM((D_INNER, D_MODEL), F32), pltpu.VMEM((n_grp, POOL_GROUP, POOL_GROUP), F32),
                        pltpu.SemaphoreType.DMA((2,))],
        compiler_params=_cparams(dimension_semantics=("arbitrary",)),
    )(dz, dmix, po, g, pw, ps, gt, wo)


def _bout0(dx1, xt, br0, lg, hf, hb, g, gt, wo, name, sides=()):
    t = dx1.shape[0]
    tm = min(TM_MM, t)
    nt = t // tm

    def body(dx_ref, x_ref, br_ref, lg_ref, hf_ref, hb_ref, g_ref, gt_ref, w_ref,
             dz_ref, dy_ref, dg_ref, gwo_hbm, st_ref, gwo_acc, sem):
        i = pl.program_id(0)

        @pl.when(i == 0)
        def _():
            gwo_acc[...] = jnp.zeros_like(gwo_acc)
            st_ref[...] = jnp.zeros_like(st_ref)

        dx = dx_ref[...]
        br = br_ref[...].astype(F32)
        gate = gt_ref[...]
        xhat, rstd = _layer_norm_stats(ALPHA * x_ref[...] + gate * br)
        dz = _layer_norm_bwd(dx, xhat, rstd, lg_ref[...])
        dz_ref[...] = dz
        st_ref[0:1, :] += _rowsum(dx * xhat)
        st_ref[1:2, :] += _rowsum(dx)
        st_ref[2:3, :] += _rowsum(dz * br)
        db = (gate * dz).astype(MXU_DTYPE)
        for k in range(D_INNER // WBLK):
            sl = slice(k * WBLK, (k + 1) * WBLK)
            y = hf_ref[:, sl].astype(F32) + hb_ref[:, sl].astype(F32)
            gg = g_ref[:, sl].astype(F32)
            sg = _sigmoid(gg)
            silu = gg * sg
            gwo_acc[sl, :] += _dot_tn(y * silu, db)
            dp = _dot_nt(db, w_ref[sl, :])
            dy_ref[:, sl] = (dp * silu).astype(ACT_DTYPE)
            dg_ref[:, sl] = (dp * y * (sg * (1.0 + gg * (1.0 - sg)))).astype(MXU_DTYPE)

        @pl.when(i == nt - 1)
        def _():
            _flush(gwo_acc, gwo_hbm, sem)

    wide = pl.BlockSpec((tm, D_INNER), lambda i: (i, 0))
    nar = pl.BlockSpec((tm, D_MODEL), lambda i: (i, 0))
    row = pl.BlockSpec((1, D_MODEL), lambda i: (0, 0))
    return _call_with_sides(
        body, sides, name=name,
        out_shape=[jax.ShapeDtypeStruct((t, D_MODEL), F32), jax.ShapeDtypeStruct((t, D_INNER), ACT_DTYPE),
                   jax.ShapeDtypeStruct((t, D_INNER), MXU_DTYPE), jax.ShapeDtypeStruct((D_INNER, D_MODEL), F32),
                   jax.ShapeDtypeStruct((SUBLANES, D_MODEL), F32)],
        grid=(nt,),
        in_specs=[nar, nar, nar, row, wide, wide, wide, row,
                  pl.BlockSpec((D_INNER, D_MODEL), lambda i: (0, 0), pipeline_mode=pl.Buffered(1))],
        out_specs=[nar, wide, wide, ANY, pl.BlockSpec((SUBLANES, D_MODEL), lambda i: (0, 0))],
        scratch_shapes=[pltpu.VMEM((D_INNER, D_MODEL), F32), pltpu.SemaphoreType.DMA(())],
        compiler_params=_cparams(dimension_semantics=("arbitrary",)),
        args=[dx1, xt, br0, lg, hf, hb, g, gt, wo])


def _conv_bwd(duvf, duvb, u, conv_w, name, sides=()):
    s = u.shape[0]
    tm = min(TM_LRU, s)
    cb = CB_LRU
    nt = s // tm

    def body(df_ref, dfp_ref, dfn_ref, db_ref, dbp_ref, dbn_ref, u_ref, cw_ref, du_ref, cst_ref):
        i = pl.program_id(1)

        @pl.when(i == 0)
        def _():
            cst_ref[...] = jnp.zeros_like(cst_ref)

        first, last = i == 0, i == nt - 1
        pz = jnp.where(first, 0.0, 1.0)
        nz = jnp.where(last, 0.0, 1.0)
        dout = df_ref[...].astype(F32) + db_ref[...].astype(F32)
        before = (dfp_ref[...].astype(F32) + dbp_ref[...].astype(F32))[H_HALO - SUBLANES:] * pz
        after = (dfn_ref[...].astype(F32) + dbn_ref[...].astype(F32))[:SUBLANES] * nz
        dm1, dp1, dp2 = _shifted(dout, before, after, [-1, 1, 2])
        cw = cw_ref[...]
        du_ref[...] = (dp2 * cw[0:1] + dp1 * cw[1:2] + dout * cw[2:3] + dm1 * cw[3:4]).astype(MXU_DTYPE)
        u_t = u_ref[...]
        cst_ref[0:1, :] += _rowsum(dp2 * u_t)
        cst_ref[1:2, :] += _rowsum(dp1 * u_t)
        cst_ref[2:3, :] += _rowsum(dout * u_t)
        cst_ref[3:4, :] += _rowsum(dm1 * u_t)
        cst_ref[4:5, :] += _rowsum(dout)

    tile, _, _ = _lru_specs(s, tm, cb, lambda i: i, nt)
    prev_map, next_map = _halo_maps(nt, tm, s // H_HALO, lambda i: i, rows=H_HALO)
    prev, nxt = pl.BlockSpec((H_HALO, cb), prev_map), pl.BlockSpec((H_HALO, cb), next_map)
    return _call_with_sides(
        body, sides, name=name,
        out_shape=[jax.ShapeDtypeStruct((s, D_INNER), MXU_DTYPE), jax.ShapeDtypeStruct((SUBLANES, D_INNER), F32)],
        grid=(D_INNER // cb, nt),
        in_specs=[tile, prev, nxt] * 2 + [tile, pl.BlockSpec((4, cb), lambda c, i: (0, c))],
        out_specs=[tile, pl.BlockSpec((SUBLANES, cb), lambda c, i: (0, c))], scratch_shapes=[],
        compiler_params=_cparams(dimension_semantics=("arbitrary", "arbitrary")),
        args=[duvf, duvf, duvf, duvb, duvb, duvb, u, conv_w])


def _bin(du, dg, xin, dzin, sc, sh, wg, name, gw_init=None, sides=()):
    t = xin.shape[0]
    tm = min(TM_MM, t)
    nt = t // tm
    has_g, has_dx, has_init = dg is not None, dzin is not None, gw_init is not None
    half = N_WBLK // 2
    n_blk = N_WBLK if has_g else half

    def body(*refs):
        refs = list(refs)
        du_ref = refs.pop(0)
        dg_ref = refs.pop(0) if has_g else None
        x_ref = refs.pop(0)
        dz_ref = refs.pop(0) if has_dx else None
        sc_ref, sh_ref, w_ref = refs.pop(0), refs.pop(0), refs.pop(0)
        init_hbm = refs.pop(0) if has_init else None
        dx_ref = refs.pop(0) if has_dx else None
        gw_hbm, st_ref, gw_acc, sem = refs
        i = pl.program_id(0)

        @pl.when(i == 0)
        def _():
            st_ref[...] = jnp.zeros_like(st_ref)
            first_zero = 0
            if has_init:
                _flush(init_hbm, gw_acc.at[pl.ds(0, half)], sem)
                first_zero = half
            for k in range(first_zero, n_blk):
                gw_acc[k] = jnp.zeros((D_MODEL, WBLK), F32)

        xv = x_ref[...]
        scale = 1.0 + sc_ref[...]
        h = (xv * scale + sh_ref[...]).astype(MXU_DTYPE)
        dh = None
        for k in range(n_blk):
            src = du_ref if k < half else dg_ref
            kk = k % half
            dk = src[:, kk * WBLK:(kk + 1) * WBLK]
            gw_acc[k] += _dot_tn(h, dk)
            contrib = _dot_nt(dk, w_ref[k])
            dh = contrib if dh is None else dh + contrib
        st_ref[0:1, :] += _rowsum(dh * xv)
        st_ref[1:2, :] += _rowsum(dh)
        if has_dx:
            dx_ref[...] = ALPHA * dz_ref[...] + dh * scale

        @pl.when(i == nt - 1)
        def _():
            _flush(gw_acc, gw_hbm, sem)

    wide = pl.BlockSpec((tm, D_INNER), lambda i: (i, 0))
    nar = pl.BlockSpec((tm, D_MODEL), lambda i: (i, 0))
    row = pl.BlockSpec((1, D_MODEL), lambda i: (0, 0))
    wspec = pl.BlockSpec((n_blk, D_MODEL, WBLK), lambda i: (0, 0, 0), pipeline_mode=pl.Buffered(1))
    in_specs = ([wide] + ([wide] if has_g else []) + [nar] + ([nar] if has_dx else []) + [row, row, wspec]
                + ([ANY] if has_init else []))
    args = ([du] + ([dg] if has_g else []) + [xin] + ([dzin] if has_dx else []) + [sc, sh, wg]
            + ([gw_init] if has_init else []))
    out_shape = ([jax.ShapeDtypeStruct((t, D_MODEL), F32)] if has_dx else []) + [
        jax.ShapeDtypeStruct((n_blk, D_MODEL, WBLK), F32), jax.ShapeDtypeStruct((SUBLANES, D_MODEL), F32)]
    out_specs = ([nar] if has_dx else []) + [ANY, pl.BlockSpec((SUBLANES, D_MODEL), lambda i: (0, 0))]
    return _call_with_sides(
        body, sides, name=name, out_shape=out_shape, grid=(nt,), in_specs=in_specs, out_specs=out_specs,
        scratch_shapes=[pltpu.VMEM((n_blk, D_MODEL, WBLK), F32), pltpu.SemaphoreType.DMA(())],
        compiler_params=_cparams(dimension_semantics=("arbitrary",)), args=args)


def _blocks_by_device(a, axis):
    shape = a.shape
    a = a.reshape(shape[:axis] + (N_DEV, shape[axis] // N_DEV) + shape[axis + 1:])
    return jnp.moveaxis(a, axis, 0)


def kernel(x, c, ctx, c_ctx, w_mod, b_mod, w_in, w_out, ln_g, ln_b, conv_w, conv_b, lru_wa, lru_ba, lru_wx, lru_bx, lru_lam, pool_w, pool_scale, loss_target, m_c_ctx, m_w_mod, m_b_mod, m_w_in, m_w_out, m_ln_g, m_ln_b, m_conv_w, m_conv_b, m_lru_wa, m_lru_ba, m_lru_wx, m_lru_bx, m_lru_lam, m_pool_w, m_pool_scale, v_c_ctx, v_w_mod, v_b_mod, v_w_in, v_w_out, v_ln_g, v_ln_b, v_conv_w, v_conv_b, v_lru_wa, v_lru_ba, v_lru_wx, v_lru_bx, v_lru_lam, v_pool_w, v_pool_scale):
    xi, yi, ci = _my_pos()
    dev = 4 * xi + 2 * yi + ci
    xt, ctxt, tgt = x[0], ctx[0], loss_target[0]
    n_mod = w_mod.shape[2]

    small_shapes = [(D_MODEL,), conv_w.shape[1:], lru_ba.shape[1:], lru_bx.shape[1:], lru_lam.shape[1:],
                    pool_scale.shape[1:]]
    small = _to_rows([c[0], conv_w[0], lru_ba[0], lru_bx[0], lru_lam[0], pool_scale[0]], SUBLANES)
    small_all, wi0 = _all_gather([small, w_in[0].astype(MXU_DTYPE)], "gather_first")
    pieces = [_split_rows(small_all[k], small_shapes) for k in range(N_DEV)]
    c_all = jnp.stack([p[0] for p in pieces])
    conv_w_f = jnp.concatenate([p[1] for p in pieces], axis=-1)
    lru_ba_f = jnp.concatenate([p[2] for p in pieces], axis=-1)[:, None, :]
    lru_bx_f = jnp.concatenate([p[3] for p in pieces], axis=-1)[:, None, :]
    lru_lam_f = jnp.concatenate([p[4] for p in pieces], axis=-1)[:, None, :]
    pool_scale_f = jnp.concatenate([p[5] for p in pieces], axis=-1)[None, :]

    cond = jnp.concatenate([c_all, jnp.broadcast_to(c_ctx[None, :], (N_DEV, D_MODEL))], axis=0)
    b_my = lax.dynamic_slice(b_mod, (0, dev * n_mod), (2, n_mod))[:, None, :]
    mod_part = _mod_fwd(cond, w_mod, b_my, "mod_fwd")
    mod_all, = _all_gather([mod_part], "gather_mod")
    mod = jnp.transpose(mod_all, (1, 2, 0, 3)).reshape(2, 16, 3 * D_MODEL)
    mod_me = lax.dynamic_slice(mod, (0, dev, 0), (2, 1, 3 * D_MODEL))
    sh = [mod_me[i, :, 0:D_MODEL] for i in range(2)]
    sc = [mod_me[i, :, D_MODEL:2 * D_MODEL] for i in range(2)]
    gt = [mod_me[i, :, 2 * D_MODEL:] for i in range(2)]
    shc, scc = mod[0, 8:9, 0:D_MODEL], mod[0, 8:9, D_MODEL:2 * D_MODEL]

    lg = [ln_g[i][None, :] for i in range(2)]
    lb = [ln_b[i][None, :] for i in range(2)]
    lru_p = dict(conv_w=conv_w_f, conv_b=conv_b, wa=lru_wa[0].astype(MXU_DTYPE), wx=lru_wx[0].astype(MXU_DTYPE),
                 ba=lru_ba_f, bx=lru_bx_f, lam=lru_lam_f)
    zero_state = jnp.zeros((1, D_INNER), F32)

    (u0, g0), (wo0,) = _in_proj(xt, sc[0], sh[0], wi0, "in_proj0", sides=[("gather", [w_out[0].astype(MXU_DTYPE)])])
    (uc, _), _ = _in_proj(ctxt, scc, shc, wi0, "in_proj0_ctx")
    (hcf, cf, uvc), _ = _lru_fwd(uc, zero_state, lru_p, 0, "lru_fwd_ctx_f", conv=True)
    (hcb, cbk), _ = _lru_fwd(uvc, zero_state, lru_p, 1, "lru_fwd_ctx_b", conv=False)
    (hf, _, uv0), (wi1,) = _lru_fwd(u0, cf, lru_p, 0, "lru_fwd_f", conv=True,
                                    sides=[("gather", [w_in[1].astype(MXU_DTYPE)])])
    (hb, _), (wo1, pool_w_g) = _lru_fwd(
        uv0, cbk, lru_p, 1, "lru_fwd_b", conv=False,
        sides=[("gather", [w_out[1].astype(MXU_DTYPE), pool_w[0].astype(MXU_DTYPE)])])
    w_in_l = [wi0, wi1]
    w_out_l = [wo0.reshape(D_INNER, D_MODEL), wo1.reshape(D_INNER, D_MODEL)]
    pool_w_f = jnp.transpose(pool_w_g, (1, 0, 2, 3)).reshape(len(POOL_WINDOWS), POOL_GROUP, POOL_GROUP)
    x1, br0, u1, g1 = _out0(hf, hb, g0, xt, gt[0], w_out_l[0], lg[0], lb[0], sc[1], sh[1], w_in_l[1], "out0_in1")
    dmix = _pool_mix(u1, False, MXU_DTYPE, "pool_fwd")
    dz1, st1, po1 = _out1(dmix, pool_w_f, pool_scale_f, g1, x1, gt[1], w_out_l[1], lg[1], lb[1], tgt, "out1")
    loss_me = jnp.full((1, LANES), (0.5 / D_MODEL) * jnp.sum(st1[3]), F32)

    core = jnp.reshape(ci, (1,)).astype(jnp.int32)
    wo_view = lambda a: a.reshape(N_DEV, D_INNER // N_DEV, D_MODEL)
    pw_view = lambda a: _blocks_by_device(a, 1).reshape(N_DEV, POOL_GROUP // N_DEV * len(POOL_WINDOWS), POOL_GROUP)
    dd, dg1, gwo1, gpw, gps = _bout1(dz1, dmix, po1, g1, pool_w_f, pool_scale_f, gt[1], w_out_l[1], "bwd_out1")
    du1 = _pool_mix(dd, True, MXU_DTYPE, "pool_bwd")
    (dx1, gwi1, stb1), _ = _bin(du1, dg1, x1, dz1, sc[1], sh[1], w_in_l[1], "bwd_in1")
    bufs1 = [gwi1, wo_view(gwo1), pw_view(gpw)]
    (dz0, dy0, dg0, gwo0, stl0), recv1 = _bout0(dx1, xt, br0, lg[0], hf, hb, g0, gt[0], w_out_l[0], "bwd_out0",
                                                sides=[("sibling", bufs1)])
    pairs1 = [_pair_sum(b, r, core, "reduce_pair_" + n)
              for b, r, n in zip(bufs1, recv1, ["w_in1", "w_out1", "pool_w"])]
    (duvf, gwa_f, gwx_f, gv_f, dh0f), (p_wi1, p_wo1, p_pw, recv_wo0) = _lru_bwd(
        uv0, dy0, hf, cf, zero_state, lru_p, 0, "lru_bwd_f", sides=[("chips", pairs1), ("sibling", [wo_view(gwo0)])])
    pair_wo0 = _pair_sum(wo_view(gwo0), recv_wo0, core, "reduce_pair_w_out0")
    (duvb, gwa_b, gwx_b, gv_b, dh0b), (p_wo0,) = _lru_bwd(
        uv0, dy0, hb, cbk, zero_state, lru_p, 1, "lru_bwd_b", sides=[("chips", [pair_wo0])])
    zero_dh = jnp.zeros(uc.shape, ACT_DTYPE)
    (ducf, gwa_cf, gwx_cf, gv_cf, _), _ = _lru_bwd(uvc, zero_dh, hcf, zero_state, dh0f, lru_p, 0, "lru_bwd_ctx_f")
    (ducb, gwa_cb, gwx_cb, gv_cb, _), _ = _lru_bwd(uvc, zero_dh, hcb, zero_state, dh0b, lru_p, 1, "lru_bwd_ctx_b")

    def pack(sharded, replicated):
        sh_sizes = [int(np.prod(a.shape[1:])) for a in sharded]
        rep_sizes = [a.shape[0] // N_DEV for a in replicated]
        n_flat = sum(sh_sizes) + sum(rep_sizes)
        rows = -(-(-(-n_flat // LANES)) // FLAT_ROWS) * FLAT_ROWS
        buf = jnp.concatenate([a.reshape(N_DEV, -1) for a in sharded + replicated], axis=1)
        return jnp.pad(buf, ((0, 0), (0, rows * LANES - n_flat))).reshape(N_DEV, rows, LANES), sh_sizes, rep_sizes

    def unpack(reduced, sh_sizes, rep_sizes, sh_shapes):
        flat = reduced.reshape(-1)
        offs = np.cumsum([0] + sh_sizes)
        mine = [flat[offs[k]:offs[k + 1]].reshape(s) for k, s in enumerate(sh_shapes)]
        return mine, _to_rows([flat[offs[-1]:offs[-1] + sum(rep_sizes)]], FLAT_ROWS)

    def spread(rep_all, rep_sizes, shapes):
        flat = rep_all.reshape(N_DEV, -1)
        offs = np.cumsum([0] + rep_sizes)
        return [flat[:, offs[k]:offs[k + 1]].reshape(s) for k, s in enumerate(shapes)]

    (du0, cst0), _ = _conv_bwd(duvf, duvb, u0, conv_w_f, "conv_bwd")
    (duc, cstc), _ = _conv_bwd(ducf, ducb, uc, conv_w_f, "conv_bwd_ctx")
    (gwic, stc), _ = _bin(duc, None, ctxt, None, scc, shc, w_in_l[0][:N_WBLK // 2], "bwd_in0_ctx")
    (gx, gwi0, stb0), _ = _bin(du0, dg0, xt, dz0, sc[0], sh[0], w_in_l[0], "bwd_in0", gw_init=gwic)

    zero_row = jnp.zeros((1, D_MODEL), F32)
    dm_me = jnp.stack([
        jnp.concatenate([jnp.concatenate([stb0[1:2], stb0[0:1], stl0[2:3]], axis=1),
                         jnp.concatenate([stc[1:2], stc[0:1], zero_row], axis=1)], axis=0),
        jnp.concatenate([jnp.concatenate([stb1[1:2], stb1[0:1], st1[2:3]], axis=1),
                         jnp.zeros((1, 3 * D_MODEL), F32)], axis=0)])
    dm_g, loss_g = _all_gather([dm_me, loss_me], "gather_dmod")
    loss = jnp.sum(loss_g[:, 0, 0])
    dm_all = jnp.concatenate([jnp.transpose(dm_g[:, :, 0], (1, 0, 2)), jnp.transpose(dm_g[:, :, 1], (1, 0, 2))],
                             axis=1)
    dm_my = lax.dynamic_slice(dm_all, (0, 0, dev * n_mod), (2, 16, n_mod))
    g_w_mod, g_b_mod, gcc_part = _mod_bwd(cond, dm_all, dm_my, w_mod, "mod_bwd")
    g_b_mod = g_b_mod.reshape(b_mod.shape)

    gwa = jnp.stack([gwa_f + gwa_cf, gwa_b + gwa_cb])
    gwx = jnp.stack([gwx_f + gwx_cf, gwx_b + gwx_cb])
    gv = jnp.stack([gv_f + gv_cf, gv_b + gv_cb])
    cst = cst0 + cstc
    misc, m_sh, m_rep = pack(
        [_blocks_by_device(cst[0:4], 1), _blocks_by_device(gv[:, 0], 1), _blocks_by_device(gv[:, 1], 1),
         _blocks_by_device(gv[:, 2], 1), _blocks_by_device(gps[0], 0)],
        [gwa.reshape(-1), gwx.reshape(-1), jnp.stack([stl0[0], st1[0]]).reshape(-1),
         jnp.stack([stl0[1], st1[1]]).reshape(-1), cst[4], gcc_part.reshape(-1)])
    bufs = [gwi0, misc]
    recvs = _sibling_exchange(bufs, "reduce_sibling")
    pairs = [_pair_sum(b, r, core, "reduce_pair_" + n) for b, r, n in zip(bufs, recvs, ["w_in0", "misc"])]
    p_wi0, p_misc = _chip_exchange(pairs, "reduce_chips")
    (g_conv_w, g_lru_ba, g_lru_bx, g_lru_lam, g_pool_scale), rep_mine = unpack(
        _sum4(p_misc, "reduce_sum_misc"), m_sh, m_rep,
        [conv_w.shape, lru_ba.shape, lru_bx.shape, lru_lam.shape, pool_scale.shape])
    rep_all, = _all_gather([rep_mine.astype(WIRE_DTYPE)], "gather_replicated")
    rep_all = rep_all.astype(F32)
    g_lru_wa, g_lru_wx, g_ln_g, g_ln_b, g_conv_b, g_c_ctx = spread(
        rep_all, m_rep, [lru_wa.shape, lru_wx.shape, ln_g.shape, ln_b.shape, conv_b.shape, c_ctx.shape])

    names = ["c_ctx", "w_mod", "b_mod", "w_in", "w_out", "ln_g", "ln_b", "conv_w", "conv_b", "lru_wa", "lru_ba",
             "lru_wx", "lru_bx", "lru_lam", "pool_w", "pool_scale"]
    weights = dict(c_ctx=c_ctx, w_mod=w_mod, b_mod=b_mod, w_in=w_in, w_out=w_out, ln_g=ln_g, ln_b=ln_b,
                   conv_w=conv_w, conv_b=conv_b, lru_wa=lru_wa, lru_ba=lru_ba, lru_wx=lru_wx, lru_bx=lru_bx,
                   lru_lam=lru_lam, pool_w=pool_w, pool_scale=pool_scale)
    mom_m = dict(c_ctx=m_c_ctx, w_mod=m_w_mod, b_mod=m_b_mod, w_in=m_w_in, w_out=m_w_out, ln_g=m_ln_g, ln_b=m_ln_b,
                 conv_w=m_conv_w, conv_b=m_conv_b, lru_wa=m_lru_wa, lru_ba=m_lru_ba, lru_wx=m_lru_wx,
                 lru_bx=m_lru_bx, lru_lam=m_lru_lam, pool_w=m_pool_w, pool_scale=m_pool_scale)
    mom_v = dict(c_ctx=v_c_ctx, w_mod=v_w_mod, b_mod=v_b_mod, w_in=v_w_in, w_out=v_w_out, ln_g=v_ln_g, ln_b=v_ln_b,
                 conv_w=v_conv_w, conv_b=v_conv_b, lru_wa=v_lru_wa, lru_ba=v_lru_ba, lru_wx=v_lru_wx,
                 lru_bx=v_lru_bx, lru_lam=v_lru_lam, pool_w=v_pool_w, pool_scale=v_pool_scale)
    grads = dict(c_ctx=g_c_ctx, w_mod=g_w_mod, b_mod=g_b_mod, ln_g=g_ln_g, ln_b=g_ln_b,
                 conv_w=g_conv_w, conv_b=g_conv_b, lru_wa=g_lru_wa, lru_ba=g_lru_ba, lru_wx=g_lru_wx,
                 lru_bx=g_lru_bx, lru_lam=g_lru_lam)
    grads["pool_scale"] = g_pool_scale
    delta, new_m, new_v = {}, {}, {}

    def update_parts(n, parts, view):
        res = _adamw_parts(weights[n].reshape(view), parts, mom_m[n].reshape(view), mom_v[n].reshape(view),
                           "adamw_" + n)
        grads[n], delta[n], new_m[n], new_v[n] = [r.reshape(weights[n].shape) for r in res]

    update_parts("w_in", [p_wi0, p_wi1], w_in.shape)
    update_parts("w_out", [p_wo0, p_wo1], w_out.shape)
    update_parts("pool_w", [p_pw], (1,) + p_pw.shape[1:])
    for n in ("w_mod", "lru_wa", "lru_wx"):
        shape = weights[n].shape
        view = (int(np.prod(shape[:-1])), shape[-1])
        res = _adamw(weights[n].reshape(view), grads[n].reshape(view), mom_m[n].reshape(view),
                     mom_v[n].reshape(view), "adamw_" + n)
        delta[n], new_m[n], new_v[n] = [r.reshape(shape) for r in res]

    small = [n for n in names if n not in delta]
    shapes = [weights[n].shape for n in small]
    flat = lambda d: _to_rows([d[n] for n in small], FLAT_ROWS)
    res = _adamw(flat(weights), flat(grads), flat(mom_m), flat(mom_v), "adamw_small")
    for d, r in zip((delta, new_m, new_v), res):
        d.update(zip(small, _split_rows(r, shapes)))

    return (loss, gx[None], *[grads[n] for n in names], *[delta[n] for n in names],
            *[new_m[n] for n in names], *[new_v[n] for n in names])
```

```python
import functools

import numpy as np
import jax
import jax.numpy as jnp
from jax import lax
from jax.experimental import pallas as pl
from jax.experimental.pallas import tpu as pltpu

F32 = jnp.float32
BF16 = jnp.bfloat16
MXU_DTYPE = BF16

D_MODEL = 1024
D_INNER = 2048
LRU_BLOCK = 128
GRID_W = 64
POOL_WINDOWS = (2, 4, 8, 16)
POOL_GROUP = 512
ALPHA = float(4 ** 0.25)
LN_EPS = 1e-5
LRU_C = 8.0
N_DEV = 8
N_WBLK = 8
WBLK = 512

ADAM_LR = 0.001
ADAM_B1 = 0.9
ADAM_B2 = 0.999
ADAM_EPS = 1e-08
ADAM_WD = 0.01
ADAM_STEP = 10

LANES = 128
SUBLANES = 8
V7X_VMEM_BYTES = 64 * 1024 * 1024
VMEM_COMPILER_RESERVE = 8 * 1024 * 1024
VMEM_LIMIT = V7X_VMEM_BYTES - VMEM_COMPILER_RESERVE
MESH = pl.DeviceIdType.MESH
ANY = pl.BlockSpec(memory_space=pl.ANY)

TM_MM = 512
TM_LRU = 1024
CB_LRU = 512
N_SEG = 8
SCAN_UNROLL = 4
SCAN_ROW_T = 17
SCAN_ROW_J = 2
SQRT_FLOOR = 1e-30
FLAT_ROWS = 16
ELEMENTWISE_TILE_BYTES = 1 << 20
POOL_TOK = 256
WIRE_DTYPE = BF16
ACT_DTYPE = BF16
H_HALO = 16


def _cparams(**kw):
    return pltpu.CompilerParams(vmem_limit_bytes=VMEM_LIMIT, **kw)


def _my_pos():
    return lax.axis_index("x"), lax.axis_index("y"), lax.axis_index("c")


def _dot(a, b):
    return jnp.dot(a.astype(MXU_DTYPE), b.astype(MXU_DTYPE), preferred_element_type=F32)


def _dot_tn(a, b):
    return lax.dot_general(a.astype(MXU_DTYPE), b.astype(MXU_DTYPE), (((0,), (0,)), ((), ())),
                           preferred_element_type=F32)


def _dot_nt(a, b):
    return lax.dot_general(a.astype(MXU_DTYPE), b.astype(MXU_DTYPE), (((1,), (1,)), ((), ())),
                           preferred_element_type=F32)


def _sigmoid(z):
    return 0.5 * jnp.tanh(0.5 * z) + 0.5


def _log_sigmoid(x):
    y = jnp.exp(-jnp.abs(x))
    u = 1.0 + y
    l1p = jnp.where(u == 1.0, y, jnp.log(u) * (y / jnp.where(u == 1.0, 1.0, u - 1.0)))
    return jnp.minimum(x, 0.0) - l1p


def _rowsum(v):
    return jnp.sum(v, axis=0, keepdims=True)


def _layer_norm_stats(z):
    mu = jnp.mean(z, axis=-1, keepdims=True)
    zc = z - mu
    var = jnp.mean(zc * zc, axis=-1, keepdims=True)
    rstd = lax.rsqrt(var + LN_EPS)
    return zc * rstd, rstd


def _layer_norm_bwd(dy, xhat, rstd, g):
    dxh = dy * g
    m1 = jnp.mean(dxh, axis=-1, keepdims=True)
    m2 = jnp.mean(dxh * xhat, axis=-1, keepdims=True)
    return rstd * (dxh - m1 - xhat * m2)


def _shifted(v, before8, after8, offsets):
    n = v.shape[0]
    ext = jnp.concatenate([before8, v, after8], axis=0)
    total = n + 2 * SUBLANES
    return [pltpu.roll(ext, (-k) % total, 0)[SUBLANES:SUBLANES + n] for k in offsets]


def _rows8(row):
    return jnp.broadcast_to(row, (SUBLANES, row.shape[1]))


def _shift_down(v, first_row):
    return _shifted(v, _rows8(first_row), _rows8(first_row), [-1])[0]


def _shift_up(v, last_row):
    return _shifted(v, _rows8(last_row), _rows8(last_row), [1])[0]


def _all_gather(blocks, name):
    n = len(blocks)

    def body(*refs):
        x_refs, out_refs = refs[:n], refs[n:2 * n]
        send_sems, recv_sems, local_sems = refs[2 * n:]
        x, y, c = _my_pos()
        me, sibling = (x, y, c), (x, y, 1 - c)
        chips = [(1 - x, y), (x, 1 - y), (1 - x, 1 - y)]

        def slot(a, px, py, pc):
            return out_refs[a].at[4 * px + 2 * py + pc]

        def copy(a, k, block, to, src=None):
            return pltpu.make_async_remote_copy(
                src_ref=slot(a, *block) if src is None else src, dst_ref=slot(a, *block),
                send_sem=send_sems.at[a, k], recv_sem=recv_sems.at[a, k], device_id=to, device_id_type=MESH)

        mine = [pltpu.make_async_copy(x_refs[a], slot(a, *me), local_sems.at[a]) for a in range(n)]
        for cp in mine:
            cp.start()
        first = []
        for a in range(n):
            first.append(copy(a, 0, me, sibling, src=x_refs[a]))
            first += [copy(a, 1 + j, me, (*chip, c), src=x_refs[a]) for j, chip in enumerate(chips)]
        for cp in first:
            cp.start()
        passed = []
        for j, chip in enumerate(chips):
            for a in range(n):
                copy(a, 1 + j, (*chip, c), me).wait_recv()
                fwd = copy(a, 4 + j, (*chip, c), sibling)
                fwd.start()
                passed.append(fwd)
        for a in range(n):
            copy(a, 0, sibling, me).wait_recv()
            for j, chip in enumerate(chips):
                copy(a, 4 + j, (*chip, 1 - c), me).wait_recv()
        for cp in first + passed:
            cp.wait_send()
        for cp in mine:
            cp.wait()

    outs = pl.pallas_call(
        body, name=name,
        out_shape=[jax.ShapeDtypeStruct((N_DEV,) + b.shape, b.dtype) for b in blocks],
        in_specs=[ANY] * n, out_specs=[ANY] * n,
        scratch_shapes=[pltpu.SemaphoreType.DMA((n, 7)), pltpu.SemaphoreType.DMA((n, 7)),
                        pltpu.SemaphoreType.DMA((n,))],
    )(*blocks)
    return list(outs)


def _sibling_exchange(bufs, name):
    n = len(bufs)

    def body(*refs):
        srcs, outs = refs[:n], refs[n:2 * n]
        send_sems, recv_sems = refs[2 * n:]
        x, y, c = _my_pos()
        copies = [pltpu.make_async_remote_copy(
            src_ref=srcs[a].at[2 * j + (1 - c)], dst_ref=outs[a].at[j], send_sem=send_sems.at[a, j],
            recv_sem=recv_sems.at[a, j], device_id=(x, y, 1 - c), device_id_type=MESH)
            for a in range(n) for j in range(4)]
        for cp in copies:
            cp.start()
        for cp in copies:
            cp.wait()

    outs = pl.pallas_call(
        body, name=name, out_shape=[jax.ShapeDtypeStruct((4,) + b.shape[1:], b.dtype) for b in bufs],
        in_specs=[ANY] * n, out_specs=[ANY] * n,
        scratch_shapes=[pltpu.SemaphoreType.DMA((n, 4)), pltpu.SemaphoreType.DMA((n, 4))],
    )(*bufs)
    return list(outs)


def _chip_exchange(parts, name):
    n = len(parts)

    def body(*refs):
        srcs, outs = refs[:n], refs[n:2 * n]
        send_sems, recv_sems, local_sems = refs[2 * n:]
        x, y, c = _my_pos()
        jme = 2 * x + y
        peers = [(1 - x, y), (x, 1 - y), (1 - x, 1 - y)]
        local = [pltpu.make_async_copy(srcs[a].at[jme], outs[a].at[jme], local_sems.at[a]) for a in range(n)]
        for cp in local:
            cp.start()

        def copy(a, k, px, py, dst_slot):
            return pltpu.make_async_remote_copy(
                src_ref=srcs[a].at[2 * px + py], dst_ref=outs[a].at[dst_slot], send_sem=send_sems.at[a, k],
                recv_sem=recv_sems.at[a, k], device_id=(px, py, c), device_id_type=MESH)

        sends = [copy(a, k, px, py, jme) for a in range(n) for k, (px, py) in enumerate(peers)]
        for cp in sends:
            cp.start()
        for a in range(n):
            for k, (px, py) in enumerate(peers):
                copy(a, k, px, py, 2 * px + py).wait_recv()
        for cp in sends:
            cp.wait_send()
        for cp in local:
            cp.wait()

    outs = pl.pallas_call(
        body, name=name, out_shape=[jax.ShapeDtypeStruct(p.shape, p.dtype) for p in parts],
        in_specs=[ANY] * n, out_specs=[ANY] * n,
        scratch_shapes=[pltpu.SemaphoreType.DMA((n, 3)), pltpu.SemaphoreType.DMA((n, 3)),
                        pltpu.SemaphoreType.DMA((n,))],
    )(*parts)
    return list(outs)


_SIDE_REMOTE = {"gather": 7, "sibling": 4, "chips": 3}
_FLIPS = [(0, 0, 1), (1, 0, 0), (0, 1, 0), (1, 1, 0), (1, 0, 1), (0, 1, 1), (1, 1, 1)]


def _side_plan(sides):
    inputs, out_shapes, scratch = [], [], []
    for kind, arrays in sides:
        n = len(arrays)
        for a in arrays:
            inputs.append(a)
            shape = {"gather": (N_DEV,) + a.shape, "sibling": (4,) + a.shape[1:], "chips": a.shape}[kind]
            out_shapes.append(jax.ShapeDtypeStruct(shape, a.dtype))
        scratch += [pltpu.SemaphoreType.DMA((n, _SIDE_REMOTE[kind])), pltpu.SemaphoreType.DMA((n, _SIDE_REMOTE[kind])),
                    pltpu.SemaphoreType.DMA((n,))]
    return inputs, out_shapes, scratch


def _side_copies(sides, in_refs, out_refs, sem_refs):
    x, y, c = _my_pos()
    starts, waits = [], []
    pos = 0
    for s, (kind, arrays) in enumerate(sides):
        send_sems, recv_sems, local_sems = sem_refs[3 * s:3 * s + 3]
        for a in range(len(arrays)):
            src, out = in_refs[pos], out_refs[pos]
            pos += 1

            def remote(k, src_ref, dst_ref, to):
                return pltpu.make_async_remote_copy(src_ref=src_ref, dst_ref=dst_ref, send_sem=send_sems.at[a, k],
                                                    recv_sem=recv_sems.at[a, k], device_id=to, device_id_type=MESH)

            def local(src_ref, dst_ref):
                cp = pltpu.make_async_copy(src_ref, dst_ref, local_sems.at[a])
                starts.append(cp.start)
                waits.append(cp.wait)

            if kind == "gather":
                me = 4 * x + 2 * y + c
                local(src, out.at[me])
                for k, (fx, fy, fc) in enumerate(_FLIPS):
                    px, py, pc = (1 - x if fx else x), (1 - y if fy else y), (1 - c if fc else c)
                    send = remote(k, src, out.at[me], (px, py, pc))
                    starts.append(send.start)
                    waits += [remote(k, src, out.at[4 * px + 2 * py + pc], (px, py, pc)).wait_recv, send.wait_send]
            elif kind == "sibling":
                for j in range(4):
                    cp = remote(j, src.at[2 * j + (1 - c)], out.at[j], (x, y, 1 - c))
                    starts.append(cp.start)
                    waits.append(cp.wait)
            else:
                jme = 2 * x + y
                local(src.at[jme], out.at[jme])
                for k, (px, py) in enumerate([(1 - x, y), (x, 1 - y), (1 - x, 1 - y)]):
                    send = remote(k, src.at[2 * px + py], out.at[jme], (px, py, c))
                    starts.append(send.start)
                    waits += [remote(k, src.at[2 * px + py], out.at[2 * px + py], (px, py, c)).wait_recv,
                              send.wait_send]
    return starts, waits


def _call_with_sides(body, sides, *, name, grid, in_specs, out_specs, out_shape, scratch_shapes, compiler_params, args):
    if not sides:
        res = pl.pallas_call(body, name=name, grid=grid, in_specs=in_specs, out_specs=out_specs, out_shape=out_shape,
                             scratch_shapes=scratch_shapes, compiler_params=compiler_params)(*args)
        return list(res), []
    s_in, s_out, s_scr = _side_plan(sides)
    n_in, n_out, n_scr, n_side = len(in_specs), len(out_specs), len(scratch_shapes), len(s_in)

    def wrapped(*refs):
        refs = list(refs)
        ins, side_in = refs[:n_in], refs[n_in:n_in + n_side]
        outs = refs[n_in + n_side:n_in + n_side + n_out]
        side_out = refs[n_in + n_side + n_out:n_in + 2 * n_side + n_out]
        rest = refs[n_in + 2 * n_side + n_out:]
        starts, waits = _side_copies(sides, side_in, side_out, rest[n_scr:])
        first = functools.reduce(jnp.logical_and, [pl.program_id(d) == 0 for d in range(len(grid))])
        last = functools.reduce(jnp.logical_and, [pl.program_id(d) == grid[d] - 1 for d in range(len(grid))])

        @pl.when(first)
        def _():
            for start in starts:
                start()

        body(*ins, *outs, *rest[:n_scr])

        @pl.when(last)
        def _():
            for wait in waits:
                wait()

    res = pl.pallas_call(
        wrapped, name=name, grid=grid, in_specs=list(in_specs) + [ANY] * n_side,
        out_specs=list(out_specs) + [ANY] * n_side, out_shape=list(out_shape) + s_out,
        scratch_shapes=list(scratch_shapes) + s_scr, compiler_params=compiler_params,
    )(*args, *s_in)
    return list(res[:n_out]), list(res[n_out:])


def _row_tile(r, l):
    t = min(r, max(16, ELEMENTWISE_TILE_BYTES // (4 * l) // 16 * 16))
    while r % t:
        t -= 16
    return t


def _pair_sum(buf, recv, core, name):
    _, r, l = buf.shape
    tr = _row_tile(r, l)

    def body(core_ref, a_ref, b_ref, o_ref):
        o_ref[...] = (a_ref[...] + b_ref[...]).astype(WIRE_DTYPE)

    return pl.pallas_call(
        body, name=name, out_shape=jax.ShapeDtypeStruct((4, r, l), WIRE_DTYPE),
        grid_spec=pltpu.PrefetchScalarGridSpec(
            num_scalar_prefetch=1, grid=(4, r // tr),
            in_specs=[pl.BlockSpec((None, tr, l), lambda j, i, cr: (2 * j + cr[0], i, 0)),
                      pl.BlockSpec((None, tr, l), lambda j, i, cr: (j, i, 0))],
            out_specs=pl.BlockSpec((None, tr, l), lambda j, i, cr: (j, i, 0))),
        compiler_params=_cparams(dimension_semantics=("arbitrary", "arbitrary")),
    )(core, buf, recv)


def _sum_parts(p_ref):
    return ((p_ref[0].astype(F32) + p_ref[1].astype(F32)) + (p_ref[2].astype(F32) + p_ref[3].astype(F32)))


def _sum4(parts, name):
    _, r, l = parts.shape
    tr = _row_tile(r, l)

    def body(p_ref, o_ref):
        o_ref[...] = _sum_parts(p_ref)

    return pl.pallas_call(
        body, name=name, out_shape=jax.ShapeDtypeStruct((r, l), F32), grid=(r // tr,),
        in_specs=[pl.BlockSpec((4, tr, l), lambda i: (0, i, 0))],
        out_specs=pl.BlockSpec((tr, l), lambda i: (i, 0)),
        compiler_params=_cparams(dimension_semantics=("arbitrary",)),
    )(parts)


def _adamw_update(w, gg, m, v):
    nm = ADAM_B1 * m + (1.0 - ADAM_B1) * gg
    nv = ADAM_B2 * v + (1.0 - ADAM_B2) * (gg * gg)
    m_hat = nm / (1.0 - ADAM_B1 ** ADAM_STEP)
    v_hat = nv / (1.0 - ADAM_B2 ** ADAM_STEP)
    return -ADAM_LR * (m_hat / (jnp.sqrt(v_hat) + ADAM_EPS) + ADAM_WD * w), nm, nv


def _adamw(w, g, m, v, name):
    r, l = w.shape
    tr = _row_tile(r, l)

    def body(w_ref, g_ref, m_ref, v_ref, d_ref, nm_ref, nv_ref):
        d_ref[...], nm_ref[...], nv_ref[...] = _adamw_update(w_ref[...], g_ref[...], m_ref[...], v_ref[...])

    spec = pl.BlockSpec((tr, l), lambda i: (i, 0))
    return pl.pallas_call(
        body, name=name, out_shape=[jax.ShapeDtypeStruct((r, l), F32)] * 3, grid=(r // tr,),
        in_specs=[spec] * 4, out_specs=[spec] * 3,
        compiler_params=_cparams(dimension_semantics=("arbitrary",)),
    )(w, g, m, v)


def _adamw_parts(w, parts, m, v, name):
    nl, r, l = w.shape
    tr = _row_tile(r, l)

    def body(*refs):
        w_ref, p_refs, (m_ref, v_ref, g_ref, d_ref, nm_ref, nv_ref) = refs[0], refs[1:1 + nl], refs[1 + nl:]
        layer = pl.program_id(0)
        gg = _sum_parts(p_refs[0])
        for q in range(1, nl):
            gg = jnp.where(layer == q, _sum_parts(p_refs[q]), gg)
        g_ref[...] = gg
        d_ref[...], nm_ref[...], nv_ref[...] = _adamw_update(w_ref[...], gg, m_ref[...], v_ref[...])

    spec = pl.BlockSpec((None, tr, l), lambda q, i: (q, i, 0))
    pspecs = [pl.BlockSpec((4, tr, l), lambda q, i, k=k: (0, jnp.where(q == k, i, 0), 0)) for k in range(nl)]
    return pl.pallas_call(
        body, name=name, out_shape=[jax.ShapeDtypeStruct((nl, r, l), F32)] * 4, grid=(nl, r // tr),
        in_specs=[spec] + pspecs + [spec, spec], out_specs=[spec] * 4,
        compiler_params=_cparams(dimension_semantics=("arbitrary", "arbitrary")),
    )(w, *parts, m, v)


def _to_rows(pieces, row_multiple):
    flat = jnp.concatenate([p.reshape(-1) for p in pieces])
    rows = -(-flat.shape[0] // LANES)
    rows = -(-rows // row_multiple) * row_multiple
    flat = jnp.pad(flat, (0, rows * LANES - flat.shape[0]))
    return flat.reshape(rows, LANES)


def _split_rows(rows, shapes):
    flat = rows.reshape(-1)
    out, off = [], 0
    for s in shapes:
        n = int(np.prod(s))
        out.append(flat[off:off + n].reshape(s))
        off += n
    return out


def _mod_fwd(cond, w_mod, b_my, name):
    nl, _, ncol = w_mod.shape

    def body(a_ref, w_ref, b_ref, o_ref):
        a = a_ref[...]
        s = a * _sigmoid(a)
        for i in range(nl):
            o_ref[i] = _dot(s, w_ref[i]) + b_ref[i]

    return pl.pallas_call(
        body, name=name, out_shape=jax.ShapeDtypeStruct((nl, 16, ncol), F32),
        compiler_params=_cparams(),
    )(cond, w_mod, b_my)


def _mod_bwd(cond, dm_all, dm_my, w_mod, name):
    nl, _, ncol = w_mod.shape

    def body(a_ref, dma_ref, dmm_ref, w_ref, gw_ref, gb_ref, gc_ref):
        a = a_ref[...]
        sg = _sigmoid(a)
        s = a * sg
        for i in range(nl):
            gw_ref[i] = _dot_tn(s, dmm_ref[i])
            gb_ref[i] = jnp.sum(dma_ref[i], axis=0, keepdims=True)
        back = _dot_nt(dmm_ref[0], w_ref[0])
        dsilu = sg * (1.0 + a * (1.0 - sg))
        gc_ref[...] = jnp.sum(back[8:16] * dsilu[8:16], axis=0, keepdims=True)

    return pl.pallas_call(
        body, name=name,
        out_shape=[jax.ShapeDtypeStruct((nl, D_MODEL, ncol), F32), jax.ShapeDtypeStruct((nl, 1, 3 * D_MODEL), F32),
                   jax.ShapeDtypeStruct((1, D_MODEL), F32)],
        compiler_params=_cparams(),
    )(cond, dm_all, dm_my, w_mod)


def _in_proj(xt, sc, sh, wg, name, sides=()):
    t = xt.shape[0]
    tm = min(TM_MM, t)

    def body(x_ref, sc_ref, sh_ref, w_ref, u_ref, g_ref):
        h = (x_ref[...] * (1.0 + sc_ref[...]) + sh_ref[...]).astype(MXU_DTYPE)
        for k in range(N_WBLK):
            o = jnp.dot(h, w_ref[k], preferred_element_type=F32)
            if k < N_WBLK // 2:
                u_ref[:, k * WBLK:(k + 1) * WBLK] = o
            else:
                kk = k - N_WBLK // 2
                g_ref[:, kk * WBLK:(kk + 1) * WBLK] = o.astype(ACT_DTYPE)

    row = pl.BlockSpec((1, D_MODEL), lambda i: (0, 0))
    return _call_with_sides(
        body, sides, name=name,
        out_shape=[jax.ShapeDtypeStruct((t, D_INNER), F32), jax.ShapeDtypeStruct((t, D_INNER), ACT_DTYPE)],
        grid=(t // tm,),
        in_specs=[pl.BlockSpec((tm, D_MODEL), lambda i: (i, 0)), row, row,
                  pl.BlockSpec((N_WBLK, D_MODEL, WBLK), lambda i: (0, 0, 0), pipeline_mode=pl.Buffered(1))],
        out_specs=[pl.BlockSpec((tm, D_INNER), lambda i: (i, 0))] * 2, scratch_shapes=[],
        compiler_params=_cparams(dimension_semantics=("arbitrary",)), args=[xt, sc, sh, wg])


def _halo_maps(nt, tm, n_blocks, pos, rows=SUBLANES):
    per = tm // rows
    prev = lambda cb, i: (jnp.maximum(pos(i) * per - 1, 0), cb)
    nxt = lambda cb, i: (jnp.minimum((pos(i) + 1) * per, n_blocks - 1), cb)
    return prev, nxt


def _conv_taps(u, prev8, next8, is_first, is_last):
    pz = jnp.where(is_first, 0.0, 1.0)
    nz = jnp.where(is_last, 0.0, 1.0)
    return _shifted(u, prev8 * pz, next8 * nz, [-2, -1, 1])


def _lru_gates(uv, wa_ref, wx_ref, ba, bx, cl, g):
    sl = slice(g * LANES, (g + 1) * LANES)
    uvg = uv[:, sl]
    r = 0.5 * jnp.tanh(_dot(uvg, wa_ref[g]) + ba[:, sl]) + 0.5
    ii = 0.5 * jnp.tanh(_dot(uvg, wx_ref[g]) + bx[:, sl]) + 0.5
    la = cl[:, sl] * r
    a = jnp.exp(la)
    q = jnp.tanh(-la) * (1.0 + a * a)
    rs = lax.rsqrt(jnp.maximum(q, SQRT_FLOOR))
    return uvg, r, ii, a, q * rs, rs


def _scan_rows(seg):
    return -(-(SCAN_ROW_T * (seg - 1) + SCAN_ROW_J * (N_SEG - 1) + 1) // SUBLANES) * SUBLANES


def _seg_chunk(j, c):
    return pl.ds(SCAN_ROW_T * SUBLANES * c + SCAN_ROW_J * j, SUBLANES, stride=SCAN_ROW_T)


def _seg_scatter(ref, g, seg, value):
    for j in range(N_SEG):
        for c in range(seg // SUBLANES):
            r0 = j * seg + SUBLANES * c
            ref[g, _seg_chunk(j, c), :] = value[r0:r0 + SUBLANES]


def _scan_tile(a_s, b_s, carry_ref, write_out, seg, reverse, chunks_per_write=1):
    n_g = a_s.shape[0]
    unroll = SCAN_UNROLL if seg % SCAN_UNROLL == 0 else 1

    n_trips = seg // unroll

    def steps(k, state):
        hs, cs = list(state[0]), list(state[1])
        base = ((n_trips - 1 - k) if reverse else k) * unroll
        for q in (range(unroll - 1, -1, -1) if reverse else range(unroll)):
            t = base + q
            rows = pl.ds(t * SCAN_ROW_T, N_SEG, stride=SCAN_ROW_J)
            for g in range(n_g):
                a = a_s[g, rows, :]
                b = b_s[g, rows, :]
                hs[g] = a * hs[g] + b
                cs[g] = a * cs[g]
                b_s[g, rows, :] = hs[g]
                a_s[g, rows, :] = cs[g]
        return tuple(hs), tuple(cs)

    zeros = tuple(jnp.zeros((N_SEG, LANES), F32) for _ in range(n_g))
    ones = tuple(jnp.ones((N_SEG, LANES), F32) for _ in range(n_g))
    h_fin, a_fin = lax.fori_loop(0, seg // unroll, steps, (zeros, ones))

    order = list(range(N_SEG - 1, -1, -1)) if reverse else list(range(N_SEG))
    for g in range(n_g):
        carry = carry_ref[:, g * LANES:(g + 1) * LANES]
        for j in order:
            for c0 in range(0, seg // SUBLANES, chunks_per_write):
                parts = [b_s[g, _seg_chunk(j, c), :] + a_s[g, _seg_chunk(j, c), :] * carry
                         for c in range(c0, c0 + chunks_per_write)]
                write_out(j, c0, g, parts[0] if chunks_per_write == 1 else jnp.concatenate(parts, axis=0))
            carry = a_fin[g][j:j + 1] * carry + h_fin[g][j:j + 1]
        carry_ref[:, g * LANES:(g + 1) * LANES] = carry


def _lru_specs(s, tm, cb, direction_pos, nt):
    n_rows8 = s // SUBLANES
    prev, nxt = _halo_maps(nt, tm, n_rows8, direction_pos)
    tile = pl.BlockSpec((tm, cb), lambda c, i: (direction_pos(i), c))
    return tile, pl.BlockSpec((SUBLANES, cb), prev), pl.BlockSpec((SUBLANES, cb), nxt)


def _lru_param_specs(cb, d):
    n_g = cb // LANES
    vec = pl.BlockSpec((1, cb), lambda c, i: (0, c))
    dvec = pl.BlockSpec((None, 1, cb), lambda c, i: (d, 0, c))
    wmat = pl.BlockSpec((None, n_g, LRU_BLOCK, LRU_BLOCK), lambda c, i: (d, c, 0, 0))
    return vec, dvec, wmat


def _lru_fwd(src, h0, p, d, name, conv, sides=()):
    s = src.shape[0]
    tm = min(TM_LRU, s)
    cb = CB_LRU
    n_g = cb // LANES
    nt = s // tm
    seg = tm // N_SEG
    pos = (lambda i: i) if d == 0 else (lambda i: nt - 1 - i)

    def body(*refs):
        refs = list(refs)
        u_ref = refs.pop(0)
        if conv:
            up_ref, un_ref, cw_ref, cbias_ref = [refs.pop(0) for _ in range(4)]
        wa_ref, wx_ref, ba_ref, bx_ref, lam_ref, h0_ref, h_ref, hc_ref = [refs.pop(0) for _ in range(8)]
        uv_ref = refs.pop(0) if conv else None
        a_s, b_s = refs
        i = pl.program_id(1)
        tp = pos(i)

        @pl.when(i == 0)
        def _():
            hc_ref[...] = h0_ref[...]

        if conv:
            u_t = u_ref[...]
            um2, um1, up1 = _conv_taps(u_t, up_ref[...], un_ref[...], tp == 0, tp == nt - 1)
            cw = cw_ref[...]
            uv_ref[...] = um2 * cw[0:1] + um1 * cw[1:2] + u_t * cw[2:3] + up1 * cw[3:4] + cbias_ref[...]
        src_ref = uv_ref if conv else u_ref
        cl = LRU_C * _log_sigmoid(lam_ref[...])
        ba, bx = ba_ref[...], bx_ref[...]
        for g in range(n_g):
            uvg, r, ii, a, sq, _ = _lru_gates(src_ref, wa_ref, wx_ref, ba, bx, cl, g)
            b = sq * (ii * uvg)
            _seg_scatter(a_s, g, seg, a)
            _seg_scatter(b_s, g, seg, b)

        per_write = 2 if (seg // SUBLANES) % 2 == 0 else 1

        def write_out(j, c, g, h):
            h_ref[pl.ds(j * seg + SUBLANES * c, SUBLANES * per_write), pl.ds(g * LANES, LANES)] = h.astype(ACT_DTYPE)

        _scan_tile(a_s, b_s, hc_ref, write_out, seg, reverse=(d == 1), chunks_per_write=per_write)

    tile, prev, nxt = _lru_specs(s, tm, cb, pos, nt)
    vec, dvec, wmat = _lru_param_specs(cb, d)
    wide = jax.ShapeDtypeStruct((s, D_INNER), F32)
    conv_specs = [prev, nxt, pl.BlockSpec((4, cb), lambda c, i: (0, c)), vec] if conv else []
    conv_args = [src, src, p["conv_w"], p["conv_b"]] if conv else []
    return _call_with_sides(
        body, sides, name=name,
        out_shape=[jax.ShapeDtypeStruct((s, D_INNER), ACT_DTYPE), jax.ShapeDtypeStruct((1, D_INNER), F32)]
        + ([wide] if conv else []),
        grid=(D_INNER // cb, nt),
        in_specs=[tile] + conv_specs + [wmat, wmat, dvec, dvec, dvec, vec],
        out_specs=[tile, vec] + ([tile] if conv else []),
        scratch_shapes=[pltpu.VMEM((n_g, _scan_rows(seg), LANES), F32)] * 2,
        compiler_params=_cparams(dimension_semantics=("arbitrary", "arbitrary")),
        args=[src, *conv_args, p["wa"], p["wx"], p["ba"], p["bx"], p["lam"], h0])


def _lru_bwd(uv, dh, h, h0, lam_in, p, d, name, sides=()):
    s = uv.shape[0]
    tm = min(TM_LRU, s)
    cb = CB_LRU
    n_g = cb // LANES
    nt = s // tm
    seg = tm // N_SEG
    pos = (lambda i: nt - 1 - i) if d == 0 else (lambda i: i)

    def body(uv_ref, dh_ref, h_ref, hh_ref, wa_ref, wx_ref, ba_ref, bx_ref,
             lam_ref, h0_ref, lin_ref, duv_ref, gwa_ref, gwx_ref, gv_ref, lc_ref, a_s, b_s, lp_s,
             r_s, i_s, q_s, rq_s, a_keep):
        i = pl.program_id(1)
        tp = pos(i)

        @pl.when(i == 0)
        def _():
            lc_ref[...] = lin_ref[...]
            gwa_ref[...] = jnp.zeros_like(gwa_ref)
            gwx_ref[...] = jnp.zeros_like(gwx_ref)
            gv_ref[...] = jnp.zeros_like(gv_ref)

        uv = uv_ref[...]
        lam = lam_ref[...]
        cl = LRU_C * _log_sigmoid(lam)
        ba, bx = ba_ref[...], bx_ref[...]
        dh_t = dh_ref[...].astype(F32)
        carry_in = lc_ref[...]
        for g in range(n_g):
            sl = slice(g * LANES, (g + 1) * LANES)
            _, r, ii, a, sq, rs = _lru_gates(uv, wa_ref, wx_ref, ba, bx, cl, g)
            for ref, val in ((r_s, r), (i_s, ii), (q_s, sq), (rq_s, rs), (a_keep, a)):
                ref[:, sl] = val.astype(ACT_DTYPE)
            b = a * dh_t[:, sl]
            _seg_scatter(a_s, g, seg, a)
            _seg_scatter(b_s, g, seg, b)

        def write_out(j, c, g, v):
            lp_s[pl.ds(j * seg + SUBLANES * c, SUBLANES), pl.ds(g * LANES, LANES)] = v

        _scan_tile(a_s, b_s, lc_ref, write_out, seg, reverse=(d == 0))

        h_t = h_ref[...].astype(F32)
        hh = hh_ref[...].astype(F32)
        if d == 0:
            edge = jnp.where(tp == 0, h0_ref[...], hh[H_HALO - 1:H_HALO])
            h_prev = _shift_down(h_t, edge)
            lam_t = dh_t + _shift_up(lp_s[...], carry_in)
        else:
            edge = jnp.where(tp == nt - 1, h0_ref[...], hh[0:1])
            h_prev = _shift_up(h_t, edge)
            lam_t = dh_t + _shift_down(lp_s[...], carry_in)

        dsig = LRU_C * _sigmoid(-lam)
        cl2 = cl + cl
        for g in range(n_g):
            sl = slice(g * LANES, (g + 1) * LANES)
            uvg = uv[:, sl]
            r, ii, a, sq, rs =[ref[:, sl].astype(F32) for ref in (r_s, i_s, a_keep, q_s, rq_s)]
            lt = lam_t[:, sl]
            ls = lt * sq
            dla = (lt * a) * (h_prev[:, sl] - (ii * uvg) * (a * rs))
            dzr = (dla * cl2[:, sl]) * r * (1.0 - r)
            dzi = ((ls + ls) * uvg) * ii * (1.0 - ii)
            duv_ref[:, sl] = (ls * ii + _dot_nt(dzr, wa_ref[g]) + _dot_nt(dzi, wx_ref[g])).astype(ACT_DTYPE)
            gwa_ref[g] += _dot_tn(uvg, dzr)
            gwx_ref[g] += _dot_tn(uvg, dzi)
            gv_ref[0:1, sl] += _rowsum(dzr)
            gv_ref[1:2, sl] += _rowsum(dzi)
            gv_ref[2:3, sl] += _rowsum(dla * r) * dsig[:, sl]

        @pl.when(i == nt - 1)
        def _():
            gwa_ref[...] = 0.5 * gwa_ref[...]
            gwx_ref[...] = 0.5 * gwx_ref[...]
            gv_ref[0:2, :] = 0.5 * gv_ref[0:2, :]

    tile, _, _ = _lru_specs(s, tm, cb, pos, nt)
    vec, dvec, wmat = _lru_param_specs(cb, d)
    h_prev_map, h_next_map = _halo_maps(nt, tm, s // H_HALO, pos, rows=H_HALO)
    hh_spec = pl.BlockSpec((H_HALO, cb), h_prev_map if d == 0 else h_next_map)
    gw_spec = pl.BlockSpec((n_g, LRU_BLOCK, LRU_BLOCK), lambda c, i: (c, 0, 0))
    n_blk = D_INNER // LRU_BLOCK
    return _call_with_sides(
        body, sides, name=name,
        out_shape=[jax.ShapeDtypeStruct((s, D_INNER), ACT_DTYPE),
                   jax.ShapeDtypeStruct((n_blk, LRU_BLOCK, LRU_BLOCK), F32),
                   jax.ShapeDtypeStruct((n_blk, LRU_BLOCK, LRU_BLOCK), F32),
                   jax.ShapeDtypeStruct((SUBLANES, D_INNER), F32),
                   jax.ShapeDtypeStruct((1, D_INNER), F32)],
        grid=(D_INNER // cb, nt),
        in_specs=[tile, tile, tile, hh_spec, wmat, wmat, dvec, dvec, dvec, vec, vec],
        out_specs=[tile, gw_spec, gw_spec, pl.BlockSpec((SUBLANES, cb), lambda c, i: (0, c)), vec],
        scratch_shapes=[pltpu.VMEM((n_g, _scan_rows(seg), LANES), F32)] * 2 + [pltpu.VMEM((tm, cb), F32)]
        + [pltpu.VMEM((tm, cb), ACT_DTYPE)] * 5,
        compiler_params=_cparams(dimension_semantics=("arbitrary", "arbitrary")),
        args=[uv, dh, h, h, p["wa"], p["wx"], p["ba"], p["bx"], p["lam"], h0, lam_in])


def _out0(hf, hb, g, xt, gt, wo, lg, lb, sc1, sh1, wg1, name):
    t = xt.shape[0]
    tm = min(TM_MM, t)

    def body(hf_ref, hb_ref, g_ref, x_ref, gt_ref, w_ref, lg_ref, lb_ref, sc1_ref, sh1_ref, w1_ref,
             x1_ref, br_ref, u1_ref, g1_ref):
        br = None
        for k in range(D_INNER // WBLK):
            sl = slice(k * WBLK, (k + 1) * WBLK)
            gg = g_ref[:, sl].astype(F32)
            p = (hf_ref[:, sl].astype(F32) + hb_ref[:, sl].astype(F32)) * (gg * _sigmoid(gg))
            part = _dot(p, w_ref[sl, :])
            br = part if br is None else br + part
        z = ALPHA * x_ref[...] + gt_ref[...] * br
        xhat, _ = _layer_norm_stats(z)
        x1 = xhat * lg_ref[...] + lb_ref[...]
        x1_ref[...] = x1
        br_ref[...] = br.astype(ACT_DTYPE)
        h1 = (x1 * (1.0 + sc1_ref[...]) + sh1_ref[...]).astype(MXU_DTYPE)
        for k in range(N_WBLK):
            o = jnp.dot(h1, w1_ref[k], preferred_element_type=F32).astype(ACT_DTYPE)
            if k < N_WBLK // 2:
                u1_ref[:, k * WBLK:(k + 1) * WBLK] = o
            else:
                kk = k - N_WBLK // 2
                g1_ref[:, kk * WBLK:(kk + 1) * WBLK] = o

    wide = pl.BlockSpec((tm, D_INNER), lambda i: (i, 0))
    nar = pl.BlockSpec((tm, D_MODEL), lambda i: (i, 0))
    row = pl.BlockSpec((1, D_MODEL), lambda i: (0, 0))
    return pl.pallas_call(
        body, name=name,
        out_shape=[jax.ShapeDtypeStruct((t, D_MODEL), F32), jax.ShapeDtypeStruct((t, D_MODEL), ACT_DTYPE),
                   jax.ShapeDtypeStruct((t, D_INNER), ACT_DTYPE), jax.ShapeDtypeStruct((t, D_INNER), ACT_DTYPE)],
        grid=(t // tm,),
        in_specs=[wide, wide, wide, nar, row,
                  pl.BlockSpec((D_INNER, D_MODEL), lambda i: (0, 0), pipeline_mode=pl.Buffered(1)), row, row, row, row,
                  pl.BlockSpec((N_WBLK, D_MODEL, WBLK), lambda i: (0, 0, 0), pipeline_mode=pl.Buffered(1))],
        out_specs=[nar, nar, wide, wide],
        compiler_params=_cparams(dimension_semantics=("arbitrary",)),
    )(hf, hb, g, xt, gt, wo, lg, lb, sc1, sh1, wg1)


def _unrolled_loop(n, fn, unroll=4):
    while n % unroll:
        unroll //= 2

    def trip(k, carry):
        for q in range(unroll):
            fn(k * unroll + q)
        return carry
    lax.fori_loop(0, n // unroll, trip, 0)


def _window(n, w):
    t = np.arange(n)
    return np.clip(t - w // 2, 0, n), np.clip(t + w // 2, 0, n)


def _pool_tables(n_rows, transpose):
    boxes, inv_c, inv_r = [], [], []
    for w in POOL_WINDOWS:
        lo, hi = _window(GRID_W, w)
        m = np.zeros((GRID_W, GRID_W), np.float32)
        for r in range(GRID_W):
            m[r, lo[r]:hi[r]] = 1.0
        m = np.kron(np.eye(POOL_TOK // GRID_W, dtype=np.float32), m)
        boxes.append(m.T if transpose else m)
        inv_c.append(np.broadcast_to((1.0 / (hi - lo).astype(np.float32))[:, None], (GRID_W, LANES)))
        lo_r, hi_r = _window(n_rows, w)
        inv_r.append(1.0 / (hi_r - lo_r).astype(np.float32))
    return (jnp.asarray(np.stack(boxes), MXU_DTYPE), jnp.asarray(np.stack(inv_c), F32),
            jnp.asarray(np.stack(inv_r), F32))


def _pool_mix(xin, transpose, out_dtype, name):
    s = xin.shape[0]
    n_rows = s // GRID_W
    pad_t = SUBLANES * GRID_W
    rows_per_blk = POOL_TOK // GRID_W
    n_slab = D_INNER // LANES
    slabs_per_group = POOL_GROUP // LANES
    n_win = len(POOL_WINDOWS)
    boxes, inv_c, inv_r = _pool_tables(n_rows, transpose)
    exact_operand = (not transpose) and xin.dtype == MXU_DTYPE and MXU_DTYPE != F32

    def body(invr_ref, box_ref, invc_ref, x_ref, o_ref, pad_s):
        k = pl.program_id(0) // slabs_per_group
        pad_s[pl.ds(0, pad_t), :] = jnp.zeros((pad_t, LANES), F32)
        pad_s[pl.ds(pad_t + s, pad_t), :] = jnp.zeros((pad_t, LANES), F32)

        for kk, w in enumerate(POOL_WINDOWS):
            half = w // 2
            offsets = list(range(-(half - 1), half + 1)) if transpose else list(range(-half, half))

            @pl.when(k == kk)
            def _():
                inv_col = invc_ref[kk]

                def col_box(b):
                    st = pl.multiple_of(b * POOL_TOK, POOL_TOK)
                    xb = x_ref[pl.ds(st, POOL_TOK), :]
                    if exact_operand:
                        pad_s[pl.ds(pad_t + st, POOL_TOK), :] = jnp.dot(box_ref[kk], xb, preferred_element_type=F32)
                        return
                    xb = xb.astype(F32)
                    if transpose:
                        xb = xb * jnp.concatenate(
                            [inv_col * invr_ref[kk, b * rows_per_blk + q] for q in range(rows_per_blk)], axis=0)
                    hi = xb.astype(MXU_DTYPE)
                    lo = (xb - hi.astype(F32)).astype(MXU_DTYPE)
                    both = jnp.dot(box_ref[kk], jnp.concatenate([hi, lo], axis=1), preferred_element_type=F32)
                    pad_s[pl.ds(pad_t + st, POOL_TOK), :] = both[:, :LANES] + both[:, LANES:]
                _unrolled_loop(s // POOL_TOK, col_box)

                def row_box(r):
                    st = pl.multiple_of(r * GRID_W, GRID_W)
                    acc = pad_s[pl.ds(pad_t + st + offsets[0] * GRID_W, GRID_W), :]
                    for o in offsets[1:]:
                        acc = acc + pad_s[pl.ds(pad_t + st + o * GRID_W, GRID_W), :]
                    if not transpose:
                        acc = acc * (inv_col * invr_ref[kk, r])
                    o_ref[pl.ds(st, GRID_W), :] = (acc - x_ref[pl.ds(st, GRID_W), :].astype(F32)).astype(out_dtype)
                _unrolled_loop(n_rows, row_box)

    slab = pl.BlockSpec((s, LANES), lambda i: (0, i))
    return pl.pallas_call(
        body, name=name, out_shape=jax.ShapeDtypeStruct((s, D_INNER), out_dtype), grid=(n_slab,),
        in_specs=[pl.BlockSpec(memory_space=pltpu.SMEM),
                  pl.BlockSpec((n_win, POOL_TOK, POOL_TOK), lambda i: (0, 0, 0)),
                  pl.BlockSpec((n_win, GRID_W, LANES), lambda i: (0, 0, 0)), slab],
        out_specs=slab,
        scratch_shapes=[pltpu.VMEM((s + 2 * pad_t, LANES), F32)],
        compiler_params=_cparams(dimension_semantics=("arbitrary",)),
    )(inv_r, boxes, inv_c, xin)


def _out1(dmix, pw, ps, g, x1, gt, wo, lg, lb, tgt, name):
    t = x1.shape[0]
    tm = min(TM_MM, t)
    n_grp = len(POOL_WINDOWS)

    def body(d_ref, pw_ref, ps_ref, g_ref, x1_ref, gt_ref, w_ref, lg_ref, lb_ref, tgt_ref, dz_ref, st_ref, po_ref):
        @pl.when(pl.program_id(0) == 0)
        def _():
            st_ref[...] = jnp.zeros_like(st_ref)

        br = jnp.zeros((tm, D_MODEL), F32)
        for k in range(n_grp):
            sl = slice(k * POOL_GROUP, (k + 1) * POOL_GROUP)
            po = jnp.dot(d_ref[:, sl], pw_ref[k], preferred_element_type=F32)
            po_ref[:, sl] = po.astype(ACT_DTYPE)
            y = po * ps_ref[:, sl]
            gg = g_ref[:, sl].astype(F32)
            br = br + _dot(y * (gg * _sigmoid(gg)), w_ref[sl, :])
        z = ALPHA * x1_ref[...] + gt_ref[...] * br
        xhat, rstd = _layer_norm_stats(z)
        lg_v = lg_ref[...]
        err = xhat * lg_v + lb_ref[...] - tgt_ref[...]
        dy = err * (1.0 / D_MODEL)
        dz = _layer_norm_bwd(dy, xhat, rstd, lg_v)
        dz_ref[...] = dz
        st_ref[0:1, :] += _rowsum(dy * xhat)
        st_ref[1:2, :] += _rowsum(dy)
        st_ref[2:3, :] += _rowsum(dz * br)
        st_ref[3:4, :] += _rowsum(err * err)

    wide = pl.BlockSpec((tm, D_INNER), lambda i: (i, 0))
    nar = pl.BlockSpec((tm, D_MODEL), lambda i: (i, 0))
    row = pl.BlockSpec((1, D_MODEL), lambda i: (0, 0))
    return pl.pallas_call(
        body, name=name,
        out_shape=[jax.ShapeDtypeStruct((t, D_MODEL), F32), jax.ShapeDtypeStruct((SUBLANES, D_MODEL), F32),
                   jax.ShapeDtypeStruct((t, D_INNER), ACT_DTYPE)],
        grid=(t // tm,),
        in_specs=[wide, pl.BlockSpec((n_grp, POOL_GROUP, POOL_GROUP), lambda i: (0, 0, 0)),
                  pl.BlockSpec((1, D_INNER), lambda i: (0, 0)), wide, nar, row,
                  pl.BlockSpec((D_INNER, D_MODEL), lambda i: (0, 0), pipeline_mode=pl.Buffered(1)), row, row, nar],
        out_specs=[nar, pl.BlockSpec((SUBLANES, D_MODEL), lambda i: (0, 0)), wide],
        compiler_params=_cparams(dimension_semantics=("arbitrary",)),
    )(dmix, pw, ps, g, x1, gt, wo, lg, lb, tgt)


def _flush(acc, out_hbm, sem):
    cp = pltpu.make_async_copy(acc, out_hbm, sem)
    cp.start()
    cp.wait()


def _bout1(dz, dmix, po, g, pw, ps, gt, wo, name):
    t = dz.shape[0]
    tm = min(TM_MM, t)
    nt = t // tm
    n_grp = len(POOL_WINDOWS)

    def body(dz_ref, d_ref, po_ref, g_ref, pw_ref, ps_ref, gt_ref, w_ref, dd_ref, dg_ref, gwo_hbm, gpw_hbm, gps_ref,
             gwo_acc, gpw_acc, sems):
        i = pl.program_id(0)

        @pl.when(i == 0)
        def _():
            gwo_acc[...] = jnp.zeros_like(gwo_acc)
            gpw_acc[...] = jnp.zeros_like(gpw_acc)
            gps_ref[...] = jnp.zeros_like(gps_ref)

        db = (gt_ref[...] * dz_ref[...]).astype(MXU_DTYPE)
        for k in range(n_grp):
            sl = slice(k * POOL_GROUP, (k + 1) * POOL_GROUP)
            dk = d_ref[:, sl]
            po = po_ref[:, sl].astype(F32)
            psk = ps_ref[:, sl]
            y = po * psk
            gg = g_ref[:, sl].astype(F32)
            sg = _sigmoid(gg)
            silu = gg * sg
            gwo_acc[sl, :] += _dot_tn(y * silu, db)
            dp = _dot_nt(db, w_ref[sl, :])
            dy = dp * silu
            dg_ref[:, sl] = (dp * y * (sg * (1.0 + gg * (1.0 - sg)))).astype(MXU_DTYPE)
            gps_ref[0:1, sl] += _rowsum(dy * po)
            dpo = (dy * psk).astype(MXU_DTYPE)
            gpw_acc[k] += _dot_tn(dk, dpo)
            dd_ref[:, sl] = _dot_nt(dpo, pw_ref[k])

        @pl.when(i == nt - 1)
        def _():
            _flush(gwo_acc, gwo_hbm, sems.at[0])
            _flush(gpw_acc, gpw_hbm, sems.at[1])

    wide = pl.BlockSpec((tm, D_INNER), lambda i: (i, 0))
    nar = pl.BlockSpec((tm, D_MODEL), lambda i: (i, 0))
    return pl.pallas_call(
        body, name=name,
        out_shape=[jax.ShapeDtypeStruct((t, D_INNER), F32), jax.ShapeDtypeStruct((t, D_INNER), MXU_DTYPE),
                   jax.ShapeDtypeStruct((D_INNER, D_MODEL), F32),
                   jax.ShapeDtypeStruct((n_grp, POOL_GROUP, POOL_GROUP), F32),
                   jax.ShapeDtypeStruct((SUBLANES, D_INNER), F32)],
        grid=(nt,),
        in_specs=[nar, wide, wide, wide,
                  pl.BlockSpec((n_grp, POOL_GROUP, POOL_GROUP), lambda i: (0, 0, 0), pipeline_mode=pl.Buffered(1)),
                  pl.BlockSpec((1, D_INNER), lambda i: (0, 0)), pl.BlockSpec((1, D_MODEL), lambda i: (0, 0)),
                  pl.BlockSpec((D_INNER, D_MODEL), lambda i: (0, 0), pipeline_mode=pl.Buffered(1))],
        out_specs=[wide, wide, ANY, ANY, pl.BlockSpec((SUBLANES, D_INNER), lambda i: (0, 0))],
        scratch_shapes=[pltpu.VMEM((D_INNER, D_MODEL), F32), pltpu.VMEM((n_grp, POOL_GROUP, POOL_GROUP), F32),
                        pltpu.SemaphoreType.DMA((2,))],
        compiler_params=_cparams(dimension_semantics=("arbitrary",)),
    )(dz, dmix, po, g, pw, ps, gt, wo)


def _bout0(dx1, xt, br0, lg, hf, hb, g, gt, wo, name, sides=()):
    t = dx1.shape[0]
    tm = min(TM_MM, t)
    nt = t // tm

    def body(dx_ref, x_ref, br_ref, lg_ref, hf_ref, hb_ref, g_ref, gt_ref, w_ref,
             dz_ref, dy_ref, dg_ref, gwo_hbm, st_ref, gwo_acc, sem):
        i = pl.program_id(0)

        @pl.when(i == 0)
        def _():
            gwo_acc[...] = jnp.zeros_like(gwo_acc)
            st_ref[...] = jnp.zeros_like(st_ref)

        dx = dx_ref[...]
        br = br_ref[...].astype(F32)
        gate = gt_ref[...]
        xhat, rstd = _layer_norm_stats(ALPHA * x_ref[...] + gate * br)
        dz = _layer_norm_bwd(dx, xhat, rstd, lg_ref[...])
        dz_ref[...] = dz
        st_ref[0:1, :] += _rowsum(dx * xhat)
        st_ref[1:2, :] += _rowsum(dx)
        st_ref[2:3, :] += _rowsum(dz * br)
        db = (gate * dz).astype(MXU_DTYPE)
        for k in range(D_INNER // WBLK):
            sl = slice(k * WBLK, (k + 1) * WBLK)
            y = hf_ref[:, sl].astype(F32) + hb_ref[:, sl].astype(F32)
            gg = g_ref[:, sl].astype(F32)
            sg = _sigmoid(gg)
            silu = gg * sg
            gwo_acc[sl, :] += _dot_tn(y * silu, db)
            dp = _dot_nt(db, w_ref[sl, :])
            dy_ref[:, sl] = (dp * silu).astype(ACT_DTYPE)
            dg_ref[:, sl] = (dp * y * (sg * (1.0 + gg * (1.0 - sg)))).astype(MXU_DTYPE)

        @pl.when(i == nt - 1)
        def _():
            _flush(gwo_acc, gwo_hbm, sem)

    wide = pl.BlockSpec((tm, D_INNER), lambda i: (i, 0))
    nar = pl.BlockSpec((tm, D_MODEL), lambda i: (i, 0))
    row = pl.BlockSpec((1, D_MODEL), lambda i: (0, 0))
    return _call_with_sides(
        body, sides, name=name,
        out_shape=[jax.ShapeDtypeStruct((t, D_MODEL), F32), jax.ShapeDtypeStruct((t, D_INNER), ACT_DTYPE),
                   jax.ShapeDtypeStruct((t, D_INNER), MXU_DTYPE), jax.ShapeDtypeStruct((D_INNER, D_MODEL), F32),
                   jax.ShapeDtypeStruct((SUBLANES, D_MODEL), F32)],
        grid=(nt,),
        in_specs=[nar, nar, nar, row, wide, wide, wide, row,
                  pl.BlockSpec((D_INNER, D_MODEL), lambda i: (0, 0), pipeline_mode=pl.Buffered(1))],
        out_specs=[nar, wide, wide, ANY, pl.BlockSpec((SUBLANES, D_MODEL), lambda i: (0, 0))],
        scratch_shapes=[pltpu.VMEM((D_INNER, D_MODEL), F32), pltpu.SemaphoreType.DMA(())],
        compiler_params=_cparams(dimension_semantics=("arbitrary",)),
        args=[dx1, xt, br0, lg, hf, hb, g, gt, wo])


def _conv_bwd(duvf, duvb, u, conv_w, name, sides=()):
    s = u.shape[0]
    tm = min(TM_LRU, s)
    cb = CB_LRU
    nt = s // tm

    def body(df_ref, dfp_ref, dfn_ref, db_ref, dbp_ref, dbn_ref, u_ref, cw_ref, du_ref, cst_ref):
        i = pl.program_id(1)

        @pl.when(i == 0)
        def _():
            cst_ref[...] = jnp.zeros_like(cst_ref)

        first, last = i == 0, i == nt - 1
        pz = jnp.where(first, 0.0, 1.0)
        nz = jnp.where(last, 0.0, 1.0)
        dout = df_ref[...].astype(F32) + db_ref[...].astype(F32)
        before = (dfp_ref[...].astype(F32) + dbp_ref[...].astype(F32))[H_HALO - SUBLANES:] * pz
        after = (dfn_ref[...].astype(F32) + dbn_ref[...].astype(F32))[:SUBLANES] * nz
        dm1, dp1, dp2 = _shifted(dout, before, after, [-1, 1, 2])
        cw = cw_ref[...]
        du_ref[...] = (dp2 * cw[0:1] + dp1 * cw[1:2] + dout * cw[2:3] + dm1 * cw[3:4]).astype(MXU_DTYPE)
        u_t = u_ref[...]
        cst_ref[0:1, :] += _rowsum(dp2 * u_t)
        cst_ref[1:2, :] += _rowsum(dp1 * u_t)
        cst_ref[2:3, :] += _rowsum(dout * u_t)
        cst_ref[3:4, :] += _rowsum(dm1 * u_t)
        cst_ref[4:5, :] += _rowsum(dout)

    tile, _, _ = _lru_specs(s, tm, cb, lambda i: i, nt)
    prev_map, next_map = _halo_maps(nt, tm, s // H_HALO, lambda i: i, rows=H_HALO)
    prev, nxt = pl.BlockSpec((H_HALO, cb), prev_map), pl.BlockSpec((H_HALO, cb), next_map)
    return _call_with_sides(
        body, sides, name=name,
        out_shape=[jax.ShapeDtypeStruct((s, D_INNER), MXU_DTYPE), jax.ShapeDtypeStruct((SUBLANES, D_INNER), F32)],
        grid=(D_INNER // cb, nt),
        in_specs=[tile, prev, nxt] * 2 + [tile, pl.BlockSpec((4, cb), lambda c, i: (0, c))],
        out_specs=[tile, pl.BlockSpec((SUBLANES, cb), lambda c, i: (0, c))], scratch_shapes=[],
        compiler_params=_cparams(dimension_semantics=("arbitrary", "arbitrary")),
        args=[duvf, duvf, duvf, duvb, duvb, duvb, u, conv_w])


def _bin(du, dg, xin, dzin, sc, sh, wg, name, gw_init=None, sides=()):
    t = xin.shape[0]
    tm = min(TM_MM, t)
    nt = t // tm
    has_g, has_dx, has_init = dg is not None, dzin is not None, gw_init is not None
    half = N_WBLK // 2
    n_blk = N_WBLK if has_g else half

    def body(*refs):
        refs = list(refs)
        du_ref = refs.pop(0)
        dg_ref = refs.pop(0) if has_g else None
        x_ref = refs.pop(0)
        dz_ref = refs.pop(0) if has_dx else None
        sc_ref, sh_ref, w_ref = refs.pop(0), refs.pop(0), refs.pop(0)
        init_hbm = refs.pop(0) if has_init else None
        dx_ref = refs.pop(0) if has_dx else None
        gw_hbm, st_ref, gw_acc, sem = refs
        i = pl.program_id(0)

        @pl.when(i == 0)
        def _():
            st_ref[...] = jnp.zeros_like(st_ref)
            first_zero = 0
            if has_init:
                _flush(init_hbm, gw_acc.at[pl.ds(0, half)], sem)
                first_zero = half
            for k in range(first_zero, n_blk):
                gw_acc[k] = jnp.zeros((D_MODEL, WBLK), F32)

        xv = x_ref[...]
        scale = 1.0 + sc_ref[...]
        h = (xv * scale + sh_ref[...]).astype(MXU_DTYPE)
        dh = None
        for k in range(n_blk):
            src = du_ref if k < half else dg_ref
            kk = k % half
            dk = src[:, kk * WBLK:(kk + 1) * WBLK]
            gw_acc[k] += _dot_tn(h, dk)
            contrib = _dot_nt(dk, w_ref[k])
            dh = contrib if dh is None else dh + contrib
        st_ref[0:1, :] += _rowsum(dh * xv)
        st_ref[1:2, :] += _rowsum(dh)
        if has_dx:
            dx_ref[...] = ALPHA * dz_ref[...] + dh * scale

        @pl.when(i == nt - 1)
        def _():
            _flush(gw_acc, gw_hbm, sem)

    wide = pl.BlockSpec((tm, D_INNER), lambda i: (i, 0))
    nar = pl.BlockSpec((tm, D_MODEL), lambda i: (i, 0))
    row = pl.BlockSpec((1, D_MODEL), lambda i: (0, 0))
    wspec = pl.BlockSpec((n_blk, D_MODEL, WBLK), lambda i: (0, 0, 0), pipeline_mode=pl.Buffered(1))
    in_specs = ([wide] + ([wide] if has_g else []) + [nar] + ([nar] if has_dx else []) + [row, row, wspec]
                + ([ANY] if has_init else []))
    args = ([du] + ([dg] if has_g else []) + [xin] + ([dzin] if has_dx else []) + [sc, sh, wg]
            + ([gw_init] if has_init else []))
    out_shape = ([jax.ShapeDtypeStruct((t, D_MODEL), F32)] if has_dx else []) + [
        jax.ShapeDtypeStruct((n_blk, D_MODEL, WBLK), F32), jax.ShapeDtypeStruct((SUBLANES, D_MODEL), F32)]
    out_specs = ([nar] if has_dx else []) + [ANY, pl.BlockSpec((SUBLANES, D_MODEL), lambda i: (0, 0))]
    return _call_with_sides(
        body, sides, name=name, out_shape=out_shape, grid=(nt,), in_specs=in_specs, out_specs=out_specs,
        scratch_shapes=[pltpu.VMEM((n_blk, D_MODEL, WBLK), F32), pltpu.SemaphoreType.DMA(())],
        compiler_params=_cparams(dimension_semantics=("arbitrary",)), args=args)


def _blocks_by_device(a, axis):
    shape = a.shape
    a = a.reshape(shape[:axis] + (N_DEV, shape[axis] // N_DEV) + shape[axis + 1:])
    return jnp.moveaxis(a, axis, 0)


def kernel(x, c, ctx, c_ctx, w_mod, b_mod, w_in, w_out, ln_g, ln_b, conv_w, conv_b, lru_wa, lru_ba, lru_wx, lru_bx, lru_lam, pool_w, pool_scale, loss_target, m_c_ctx, m_w_mod, m_b_mod, m_w_in, m_w_out, m_ln_g, m_ln_b, m_conv_w, m_conv_b, m_lru_wa, m_lru_ba, m_lru_wx, m_lru_bx, m_lru_lam, m_pool_w, m_pool_scale, v_c_ctx, v_w_mod, v_b_mod, v_w_in, v_w_out, v_ln_g, v_ln_b, v_conv_w, v_conv_b, v_lru_wa, v_lru_ba, v_lru_wx, v_lru_bx, v_lru_lam, v_pool_w, v_pool_scale):
    xi, yi, ci = _my_pos()
    dev = 4 * xi + 2 * yi + ci
    xt, ctxt, tgt = x[0], ctx[0], loss_target[0]
    n_mod = w_mod.shape[2]

    small_shapes = [(D_MODEL,), conv_w.shape[1:], lru_ba.shape[1:], lru_bx.shape[1:], lru_lam.shape[1:],
                    pool_scale.shape[1:]]
    small = _to_rows([c[0], conv_w[0], lru_ba[0], lru_bx[0], lru_lam[0], pool_scale[0]], SUBLANES)
    small_all, wi0 = _all_gather([small, w_in[0].astype(MXU_DTYPE)], "gather_first")
    pieces = [_split_rows(small_all[k], small_shapes) for k in range(N_DEV)]
    c_all = jnp.stack([p[0] for p in pieces])
    conv_w_f = jnp.concatenate([p[1] for p in pieces], axis=-1)
    lru_ba_f = jnp.concatenate([p[2] for p in pieces], axis=-1)[:, None, :]
    lru_bx_f = jnp.concatenate([p[3] for p in pieces], axis=-1)[:, None, :]
    lru_lam_f = jnp.concatenate([p[4] for p in pieces], axis=-1)[:, None, :]
    pool_scale_f = jnp.concatenate([p[5] for p in pieces], axis=-1)[None, :]

    cond = jnp.concatenate([c_all, jnp.broadcast_to(c_ctx[None, :], (N_DEV, D_MODEL))], axis=0)
    b_my = lax.dynamic_slice(b_mod, (0, dev * n_mod), (2, n_mod))[:, None, :]
    mod_part = _mod_fwd(cond, w_mod, b_my, "mod_fwd")
    mod_all, = _all_gather([mod_part], "gather_mod")
    mod = jnp.transpose(mod_all, (1, 2, 0, 3)).reshape(2, 16, 3 * D_MODEL)
    mod_me = lax.dynamic_slice(mod, (0, dev, 0), (2, 1, 3 * D_MODEL))
    sh = [mod_me[i, :, 0:D_MODEL] for i in range(2)]
    sc = [mod_me[i, :, D_MODEL:2 * D_MODEL] for i in range(2)]
    gt = [mod_me[i, :, 2 * D_MODEL:] for i in range(2)]
    shc, scc = mod[0, 8:9, 0:D_MODEL], mod[0, 8:9, D_MODEL:2 * D_MODEL]

    lg = [ln_g[i][None, :] for i in range(2)]
    lb = [ln_b[i][None, :] for i in range(2)]
    lru_p = dict(conv_w=conv_w_f, conv_b=conv_b, wa=(0.5 * lru_wa[0]).astype(MXU_DTYPE),
                 wx=(0.5 * lru_wx[0]).astype(MXU_DTYPE), ba=0.5 * lru_ba_f, bx=0.5 * lru_bx_f, lam=lru_lam_f)
    zero_state = jnp.zeros((1, D_INNER), F32)

    (u0, g0), (wo0,) = _in_proj(xt, sc[0], sh[0], wi0, "in_proj0", sides=[("gather", [w_out[0].astype(MXU_DTYPE)])])
    (uc, _), _ = _in_proj(ctxt, scc, shc, wi0, "in_proj0_ctx")
    (hcf, cf, uvc), _ = _lru_fwd(uc, zero_state, lru_p, 0, "lru_fwd_ctx_f", conv=True)
    (hcb, cbk), _ = _lru_fwd(uvc, zero_state, lru_p, 1, "lru_fwd_ctx_b", conv=False)
    (hf, _, uv0), (wi1,) = _lru_fwd(u0, cf, lru_p, 0, "lru_fwd_f", conv=True,
                                    sides=[("gather", [w_in[1].astype(MXU_DTYPE)])])
    (hb, _), (wo1, pool_w_g) = _lru_fwd(
        uv0, cbk, lru_p, 1, "lru_fwd_b", conv=False,
        sides=[("gather", [w_out[1].astype(MXU_DTYPE), pool_w[0].astype(MXU_DTYPE)])])
    w_in_l = [wi0, wi1]
    w_out_l = [wo0.reshape(D_INNER, D_MODEL), wo1.reshape(D_INNER, D_MODEL)]
    pool_w_f = jnp.transpose(pool_w_g, (1, 0, 2, 3)).reshape(len(POOL_WINDOWS), POOL_GROUP, POOL_GROUP)
    x1, br0, u1, g1 = _out0(hf, hb, g0, xt, gt[0], w_out_l[0], lg[0], lb[0], sc[1], sh[1], w_in_l[1], "out0_in1")
    dmix = _pool_mix(u1, False, MXU_DTYPE, "pool_fwd")
    dz1, st1, po1 = _out1(dmix, pool_w_f, pool_scale_f, g1, x1, gt[1], w_out_l[1], lg[1], lb[1], tgt, "out1")
    loss_me = jnp.full((1, LANES), (0.5 / D_MODEL) * jnp.sum(st1[3]), F32)

    core = jnp.reshape(ci, (1,)).astype(jnp.int32)
    wo_view = lambda a: a.reshape(N_DEV, D_INNER // N_DEV, D_MODEL)
    pw_view = lambda a: _blocks_by_device(a, 1).reshape(N_DEV, POOL_GROUP // N_DEV * len(POOL_WINDOWS), POOL_GROUP)
    dd, dg1, gwo1, gpw, gps = _bout1(dz1, dmix, po1, g1, pool_w_f, pool_scale_f, gt[1], w_out_l[1], "bwd_out1")
    du1 = _pool_mix(dd, True, MXU_DTYPE, "pool_bwd")
    (dx1, gwi1, stb1), _ = _bin(du1, dg1, x1, dz1, sc[1], sh[1], w_in_l[1], "bwd_in1")
    bufs1 = [gwi1, wo_view(gwo1), pw_view(gpw)]
    (dz0, dy0, dg0, gwo0, stl0), recv1 = _bout0(dx1, xt, br0, lg[0], hf, hb, g0, gt[0], w_out_l[0], "bwd_out0",
                                                sides=[("sibling", bufs1)])
    pairs1 = [_pair_sum(b, r, core, "reduce_pair_" + n)
              for b, r, n in zip(bufs1, recv1, ["w_in1", "w_out1", "pool_w"])]
    (duvf, gwa_f, gwx_f, gv_f, dh0f), (p_wi1, p_wo1, p_pw, recv_wo0) = _lru_bwd(
        uv0, dy0, hf, cf, zero_state, lru_p, 0, "lru_bwd_f", sides=[("chips", pairs1), ("sibling", [wo_view(gwo0)])])
    pair_wo0 = _pair_sum(wo_view(gwo0), recv_wo0, core, "reduce_pair_w_out0")
    (duvb, gwa_b, gwx_b, gv_b, dh0b), (p_wo0,) = _lru_bwd(
        uv0, dy0, hb, cbk, zero_state, lru_p, 1, "lru_bwd_b", sides=[("chips", [pair_wo0])])
    zero_dh = jnp.zeros(uc.shape, ACT_DTYPE)
    (ducf, gwa_cf, gwx_cf, gv_cf, _), _ = _lru_bwd(uvc, zero_dh, hcf, zero_state, dh0f, lru_p, 0, "lru_bwd_ctx_f")
    (ducb, gwa_cb, gwx_cb, gv_cb, _), _ = _lru_bwd(uvc, zero_dh, hcb, zero_state, dh0b, lru_p, 1, "lru_bwd_ctx_b")

    def pack(sharded, replicated):
        sh_sizes = [int(np.prod(a.shape[1:])) for a in sharded]
        rep_sizes = [a.shape[0] // N_DEV for a in replicated]
        n_flat = sum(sh_sizes) + sum(rep_sizes)
        rows = -(-(-(-n_flat // LANES)) // FLAT_ROWS) * FLAT_ROWS
        buf = jnp.concatenate([a.reshape(N_DEV, -1) for a in sharded + replicated], axis=1)
        return jnp.pad(buf, ((0, 0), (0, rows * LANES - n_flat))).reshape(N_DEV, rows, LANES), sh_sizes, rep_sizes

    def unpack(reduced, sh_sizes, rep_sizes, sh_shapes):
        flat = reduced.reshape(-1)
        offs = np.cumsum([0] + sh_sizes)
        mine = [flat[offs[k]:offs[k + 1]].reshape(s) for k, s in enumerate(sh_shapes)]
        return mine, _to_rows([flat[offs[-1]:offs[-1] + sum(rep_sizes)]], FLAT_ROWS)

    def spread(rep_all, rep_sizes, shapes):
        flat = rep_all.reshape(N_DEV, -1)
        offs = np.cumsum([0] + rep_sizes)
        return [flat[:, offs[k]:offs[k + 1]].reshape(s) for k, s in enumerate(shapes)]

    (du0, cst0), _ = _conv_bwd(duvf, duvb, u0, conv_w_f, "conv_bwd")
    (duc, cstc), _ = _conv_bwd(ducf, ducb, uc, conv_w_f, "conv_bwd_ctx")
    (gwic, stc), _ = _bin(duc, None, ctxt, None, scc, shc, w_in_l[0][:N_WBLK // 2], "bwd_in0_ctx")
    (gx, gwi0, stb0), _ = _bin(du0, dg0, xt, dz0, sc[0], sh[0], w_in_l[0], "bwd_in0", gw_init=gwic)

    zero_row = jnp.zeros((1, D_MODEL), F32)
    dm_me = jnp.stack([
        jnp.concatenate([jnp.concatenate([stb0[1:2], stb0[0:1], stl0[2:3]], axis=1),
                         jnp.concatenate([stc[1:2], stc[0:1], zero_row], axis=1)], axis=0),
        jnp.concatenate([jnp.concatenate([stb1[1:2], stb1[0:1], st1[2:3]], axis=1),
                         jnp.zeros((1, 3 * D_MODEL), F32)], axis=0)])
    dm_g, loss_g = _all_gather([dm_me, loss_me], "gather_dmod")
    loss = jnp.sum(loss_g[:, 0, 0])
    dm_all = jnp.concatenate([jnp.transpose(dm_g[:, :, 0], (1, 0, 2)), jnp.transpose(dm_g[:, :, 1], (1, 0, 2))],
                             axis=1)
    dm_my = lax.dynamic_slice(dm_all, (0, 0, dev * n_mod), (2, 16, n_mod))
    g_w_mod, g_b_mod, gcc_part = _mod_bwd(cond, dm_all, dm_my, w_mod, "mod_bwd")
    g_b_mod = g_b_mod.reshape(b_mod.shape)

    gwa = jnp.stack([gwa_f + gwa_cf, gwa_b + gwa_cb])
    gwx = jnp.stack([gwx_f + gwx_cf, gwx_b + gwx_cb])
    gv = jnp.stack([gv_f + gv_cf, gv_b + gv_cb])
    cst = cst0 + cstc
    misc, m_sh, m_rep = pack(
        [_blocks_by_device(cst[0:4], 1), _blocks_by_device(gv[:, 0], 1), _blocks_by_device(gv[:, 1], 1),
         _blocks_by_device(gv[:, 2], 1), _blocks_by_device(gps[0], 0)],
        [gwa.reshape(-1), gwx.reshape(-1), jnp.stack([stl0[0], st1[0]]).reshape(-1),
         jnp.stack([stl0[1], st1[1]]).reshape(-1), cst[4], gcc_part.reshape(-1)])
    bufs = [gwi0, misc]
    recvs = _sibling_exchange(bufs, "reduce_sibling")
    pairs = [_pair_sum(b, r, core, "reduce_pair_" + n) for b, r, n in zip(bufs, recvs, ["w_in0", "misc"])]
    p_wi0, p_misc = _chip_exchange(pairs, "reduce_chips")
    (g_conv_w, g_lru_ba, g_lru_bx, g_lru_lam, g_pool_scale), rep_mine = unpack(
        _sum4(p_misc, "reduce_sum_misc"), m_sh, m_rep,
        [conv_w.shape, lru_ba.shape, lru_bx.shape, lru_lam.shape, pool_scale.shape])
    rep_all, = _all_gather([rep_mine.astype(WIRE_DTYPE)], "gather_replicated")
    rep_all = rep_all.astype(F32)
    g_lru_wa, g_lru_wx, g_ln_g, g_ln_b, g_conv_b, g_c_ctx = spread(
        rep_all, m_rep, [lru_wa.shape, lru_wx.shape, ln_g.shape, ln_b.shape, conv_b.shape, c_ctx.shape])

    names = ["c_ctx", "w_mod", "b_mod", "w_in", "w_out", "ln_g", "ln_b", "conv_w", "conv_b", "lru_wa", "lru_ba",
             "lru_wx", "lru_bx", "lru_lam", "pool_w", "pool_scale"]
    weights = dict(c_ctx=c_ctx, w_mod=w_mod, b_mod=b_mod, w_in=w_in, w_out=w_out, ln_g=ln_g, ln_b=ln_b,
                   conv_w=conv_w, conv_b=conv_b, lru_wa=lru_wa, lru_ba=lru_ba, lru_wx=lru_wx, lru_bx=lru_bx,
                   lru_lam=lru_lam, pool_w=pool_w, pool_scale=pool_scale)
    mom_m = dict(c_ctx=m_c_ctx, w_mod=m_w_mod, b_mod=m_b_mod, w_in=m_w_in, w_out=m_w_out, ln_g=m_ln_g, ln_b=m_ln_b,
                 conv_w=m_conv_w, conv_b=m_conv_b, lru_wa=m_lru_wa, lru_ba=m_lru_ba, lru_wx=m_lru_wx,
                 lru_bx=m_lru_bx, lru_lam=m_lru_lam, pool_w=m_pool_w, pool_scale=m_pool_scale)
    mom_v = dict(c_ctx=v_c_ctx, w_mod=v_w_mod, b_mod=v_b_mod, w_in=v_w_in, w_out=v_w_out, ln_g=v_ln_g, ln_b=v_ln_b,
                 conv_w=v_conv_w, conv_b=v_conv_b, lru_wa=v_lru_wa, lru_ba=v_lru_ba, lru_wx=v_lru_wx,
                 lru_bx=v_lru_bx, lru_lam=v_lru_lam, pool_w=v_pool_w, pool_scale=v_pool_scale)
    grads = dict(c_ctx=g_c_ctx, w_mod=g_w_mod, b_mod=g_b_mod, ln_g=g_ln_g, ln_b=g_ln_b,
                 conv_w=g_conv_w, conv_b=g_conv_b, lru_wa=g_lru_wa, lru_ba=g_lru_ba, lru_wx=g_lru_wx,
                 lru_bx=g_lru_bx, lru_lam=g_lru_lam)
    grads["pool_scale"] = g_pool_scale
    delta, new_m, new_v = {}, {}, {}

    def update_parts(n, parts, view):
        res = _adamw_parts(weights[n].reshape(view), parts, mom_m[n].reshape(view), mom_v[n].reshape(view),
                           "adamw_" + n)
        grads[n], delta[n], new_m[n], new_v[n] = [r.reshape(weights[n].shape) for r in res]

    update_parts("w_in", [p_wi0, p_wi1], w_in.shape)
    update_parts("w_out", [p_wo0, p_wo1], w_out.shape)
    update_parts("pool_w", [p_pw], (1,) + p_pw.shape[1:])
    for n in ("w_mod", "lru_wa", "lru_wx"):
        shape = weights[n].shape
        view = (int(np.prod(shape[:-1])), shape[-1])
        res = _adamw(weights[n].reshape(view), grads[n].reshape(view), mom_m[n].reshape(view),
                     mom_v[n].reshape(view), "adamw_" + n)
        delta[n], new_m[n], new_v[n] = [r.reshape(shape) for r in res]

    small = [n for n in names if n not in delta]
    shapes = [weights[n].shape for n in small]
    flat = lambda d: _to_rows([d[n] for n in small], FLAT_ROWS)
    res = _adamw(flat(weights), flat(grads), flat(mom_m), flat(mom_v), "adamw_small")
    for d, r in zip((delta, new_m, new_v), res):
        d.update(zip(small, _split_rows(r, shapes)))

    return (loss, gx[None], *[grads[n] for n in names], *[delta[n] for n in names],
            *[new_m[n] for n in names], *[new_v[n] for n in names])
```

```python
import functools

import numpy as np
import jax
import jax.numpy as jnp
from jax import lax
from jax.experimental import pallas as pl
from jax.experimental.pallas import tpu as pltpu

F32 = jnp.float32
BF16 = jnp.bfloat16
MXU_DTYPE = BF16

D_MODEL = 1024
D_INNER = 2048
LRU_BLOCK = 128
GRID_W = 64
POOL_WINDOWS = (2, 4, 8, 16)
POOL_GROUP = 512
ALPHA = float(4 ** 0.25)
LN_EPS = 1e-5
LRU_C = 8.0
N_DEV = 8
N_WBLK = 8
WBLK = 512

ADAM_LR = 0.001
ADAM_B1 = 0.9
ADAM_B2 = 0.999
ADAM_EPS = 1e-08
ADAM_WD = 0.01
ADAM_STEP = 10

LANES = 128
SUBLANES = 8
V7X_VMEM_BYTES = 64 * 1024 * 1024
VMEM_COMPILER_RESERVE = 8 * 1024 * 1024
VMEM_LIMIT = V7X_VMEM_BYTES - VMEM_COMPILER_RESERVE
MESH = pl.DeviceIdType.MESH
ANY = pl.BlockSpec(memory_space=pl.ANY)

TM_MM = 512
TM_LRU = 1024
CB_LRU = 512
N_SEG = 8
SCAN_UNROLL = 4
SCAN_ROW_T = 17
SCAN_ROW_J = 2
SQRT_FLOOR = 1e-30
FLAT_ROWS = 16
ELEMENTWISE_TILE_BYTES = 1 << 20
POOL_TOK = 256
WIRE_DTYPE = BF16
ACT_DTYPE = BF16
H_HALO = 16


def _cparams(**kw):
    return pltpu.CompilerParams(vmem_limit_bytes=VMEM_LIMIT, **kw)


def _my_pos():
    return lax.axis_index("x"), lax.axis_index("y"), lax.axis_index("c")


def _dot(a, b):
    return jnp.dot(a.astype(MXU_DTYPE), b.astype(MXU_DTYPE), preferred_element_type=F32)


def _dot_tn(a, b):
    return lax.dot_general(a.astype(MXU_DTYPE), b.astype(MXU_DTYPE), (((0,), (0,)), ((), ())),
                           preferred_element_type=F32)


def _dot_nt(a, b):
    return lax.dot_general(a.astype(MXU_DTYPE), b.astype(MXU_DTYPE), (((1,), (1,)), ((), ())),
                           preferred_element_type=F32)


def _sigmoid(z):
    return 0.5 * jnp.tanh(0.5 * z) + 0.5


def _log_sigmoid(x):
    y = jnp.exp(-jnp.abs(x))
    u = 1.0 + y
    l1p = jnp.where(u == 1.0, y, jnp.log(u) * (y / jnp.where(u == 1.0, 1.0, u - 1.0)))
    return jnp.minimum(x, 0.0) - l1p


def _rowsum(v):
    return jnp.sum(v, axis=0, keepdims=True)


def _layer_norm_stats(z):
    mu = jnp.mean(z, axis=-1, keepdims=True)
    zc = z - mu
    var = jnp.mean(zc * zc, axis=-1, keepdims=True)
    rstd = lax.rsqrt(var + LN_EPS)
    return zc * rstd, rstd


def _layer_norm_bwd(dy, xhat, rstd, g):
    dxh = dy * g
    m1 = jnp.mean(dxh, axis=-1, keepdims=True)
    m2 = jnp.mean(dxh * xhat, axis=-1, keepdims=True)
    return rstd * (dxh - m1 - xhat * m2)


def _shifted(v, before8, after8, offsets):
    n = v.shape[0]
    ext = jnp.concatenate([before8, v, after8], axis=0)
    total = n + 2 * SUBLANES
    return [pltpu.roll(ext, (-k) % total, 0)[SUBLANES:SUBLANES + n] for k in offsets]


def _rows8(row):
    return jnp.broadcast_to(row, (SUBLANES, row.shape[1]))


def _shift_down(v, first_row):
    return _shifted(v, _rows8(first_row), _rows8(first_row), [-1])[0]


def _shift_up(v, last_row):
    return _shifted(v, _rows8(last_row), _rows8(last_row), [1])[0]


def _all_gather(blocks, name):
    n = len(blocks)

    def body(*refs):
        x_refs, out_refs = refs[:n], refs[n:2 * n]
        send_sems, recv_sems, local_sems = refs[2 * n:]
        x, y, c = _my_pos()
        me, sibling = (x, y, c), (x, y, 1 - c)
        chips = [(1 - x, y), (x, 1 - y), (1 - x, 1 - y)]

        def slot(a, px, py, pc):
            return out_refs[a].at[4 * px + 2 * py + pc]

        def copy(a, k, block, to, src=None):
            return pltpu.make_async_remote_copy(
                src_ref=slot(a, *block) if src is None else src, dst_ref=slot(a, *block),
                send_sem=send_sems.at[a, k], recv_sem=recv_sems.at[a, k], device_id=to, device_id_type=MESH)

        mine = [pltpu.make_async_copy(x_refs[a], slot(a, *me), local_sems.at[a]) for a in range(n)]
        for cp in mine:
            cp.start()
        first = []
        for a in range(n):
            first.append(copy(a, 0, me, sibling, src=x_refs[a]))
            first += [copy(a, 1 + j, me, (*chip, c), src=x_refs[a]) for j, chip in enumerate(chips)]
        for cp in first:
            cp.start()
        passed = []
        for j, chip in enumerate(chips):
            for a in range(n):
                copy(a, 1 + j, (*chip, c), me).wait_recv()
                fwd = copy(a, 4 + j, (*chip, c), sibling)
                fwd.start()
                passed.append(fwd)
        for a in range(n):
            copy(a, 0, sibling, me).wait_recv()
            for j, chip in enumerate(chips):
                copy(a, 4 + j, (*chip, 1 - c), me).wait_recv()
        for cp in first + passed:
            cp.wait_send()
        for cp in mine:
            cp.wait()

    outs = pl.pallas_call(
        body, name=name,
        out_shape=[jax.ShapeDtypeStruct((N_DEV,) + b.shape, b.dtype) for b in blocks],
        in_specs=[ANY] * n, out_specs=[ANY] * n,
        scratch_shapes=[pltpu.SemaphoreType.DMA((n, 7)), pltpu.SemaphoreType.DMA((n, 7)),
                        pltpu.SemaphoreType.DMA((n,))],
    )(*blocks)
    return list(outs)


def _sibling_exchange(bufs, name):
    n = len(bufs)

    def body(*refs):
        srcs, outs = refs[:n], refs[n:2 * n]
        send_sems, recv_sems = refs[2 * n:]
        x, y, c = _my_pos()
        copies = [pltpu.make_async_remote_copy(
            src_ref=srcs[a].at[2 * j + (1 - c)], dst_ref=outs[a].at[j], send_sem=send_sems.at[a, j],
            recv_sem=recv_sems.at[a, j], device_id=(x, y, 1 - c), device_id_type=MESH)
            for a in range(n) for j in range(4)]
        for cp in copies:
            cp.start()
        for cp in copies:
            cp.wait()

    outs = pl.pallas_call(
        body, name=name, out_shape=[jax.ShapeDtypeStruct((4,) + b.shape[1:], b.dtype) for b in bufs],
        in_specs=[ANY] * n, out_specs=[ANY] * n,
        scratch_shapes=[pltpu.SemaphoreType.DMA((n, 4)), pltpu.SemaphoreType.DMA((n, 4))],
    )(*bufs)
    return list(outs)


def _chip_exchange(parts, name):
    n = len(parts)

    def body(*refs):
        srcs, outs = refs[:n], refs[n:2 * n]
        send_sems, recv_sems, local_sems = refs[2 * n:]
        x, y, c = _my_pos()
        jme = 2 * x + y
        peers = [(1 - x, y), (x, 1 - y), (1 - x, 1 - y)]
        local = [pltpu.make_async_copy(srcs[a].at[jme], outs[a].at[jme], local_sems.at[a]) for a in range(n)]
        for cp in local:
            cp.start()

        def copy(a, k, px, py, dst_slot):
            return pltpu.make_async_remote_copy(
                src_ref=srcs[a].at[2 * px + py], dst_ref=outs[a].at[dst_slot], send_sem=send_sems.at[a, k],
                recv_sem=recv_sems.at[a, k], device_id=(px, py, c), device_id_type=MESH)

        sends = [copy(a, k, px, py, jme) for a in range(n) for k, (px, py) in enumerate(peers)]
        for cp in sends:
            cp.start()
        for a in range(n):
            for k, (px, py) in enumerate(peers):
                copy(a, k, px, py, 2 * px + py).wait_recv()
        for cp in sends:
            cp.wait_send()
        for cp in local:
            cp.wait()

    outs = pl.pallas_call(
        body, name=name, out_shape=[jax.ShapeDtypeStruct(p.shape, p.dtype) for p in parts],
        in_specs=[ANY] * n, out_specs=[ANY] * n,
        scratch_shapes=[pltpu.SemaphoreType.DMA((n, 3)), pltpu.SemaphoreType.DMA((n, 3)),
                        pltpu.SemaphoreType.DMA((n,))],
    )(*parts)
    return list(outs)


_SIDE_REMOTE = {"gather": 7, "sibling": 4, "chips": 3}
_FLIPS = [(0, 0, 1), (1, 0, 0), (0, 1, 0), (1, 1, 0), (1, 0, 1), (0, 1, 1), (1, 1, 1)]


def _side_plan(sides):
    inputs, out_shapes, scratch = [], [], []
    for kind, arrays in sides:
        n = len(arrays)
        for a in arrays:
            inputs.append(a)
            shape = {"gather": (N_DEV,) + a.shape, "sibling": (4,) + a.shape[1:], "chips": a.shape}[kind]
            out_shapes.append(jax.ShapeDtypeStruct(shape, a.dtype))
        scratch += [pltpu.SemaphoreType.DMA((n, _SIDE_REMOTE[kind])), pltpu.SemaphoreType.DMA((n, _SIDE_REMOTE[kind])),
                    pltpu.SemaphoreType.DMA((n,))]
    return inputs, out_shapes, scratch


def _side_copies(sides, in_refs, out_refs, sem_refs):
    x, y, c = _my_pos()
    starts, waits = [], []
    pos = 0
    for s, (kind, arrays) in enumerate(sides):
        send_sems, recv_sems, local_sems = sem_refs[3 * s:3 * s + 3]
        for a in range(len(arrays)):
            src, out = in_refs[pos], out_refs[pos]
            pos += 1

            def remote(k, src_ref, dst_ref, to):
                return pltpu.make_async_remote_copy(src_ref=src_ref, dst_ref=dst_ref, send_sem=send_sems.at[a, k],
                                                    recv_sem=recv_sems.at[a, k], device_id=to, device_id_type=MESH)

            def local(src_ref, dst_ref):
                cp = pltpu.make_async_copy(src_ref, dst_ref, local_sems.at[a])
                starts.append(cp.start)
                waits.append(cp.wait)

            if kind == "gather":
                me = 4 * x + 2 * y + c
                local(src, out.at[me])
                for k, (fx, fy, fc) in enumerate(_FLIPS):
                    px, py, pc = (1 - x if fx else x), (1 - y if fy else y), (1 - c if fc else c)
                    send = remote(k, src, out.at[me], (px, py, pc))
                    starts.append(send.start)
                    waits += [remote(k, src, out.at[4 * px + 2 * py + pc], (px, py, pc)).wait_recv, send.wait_send]
            elif kind == "sibling":
                for j in range(4):
                    cp = remote(j, src.at[2 * j + (1 - c)], out.at[j], (x, y, 1 - c))
                    starts.append(cp.start)
                    waits.append(cp.wait)
            else:
                jme = 2 * x + y
                local(src.at[jme], out.at[jme])
                for k, (px, py) in enumerate([(1 - x, y), (x, 1 - y), (1 - x, 1 - y)]):
                    send = remote(k, src.at[2 * px + py], out.at[jme], (px, py, c))
                    starts.append(send.start)
                    waits += [remote(k, src.at[2 * px + py], out.at[2 * px + py], (px, py, c)).wait_recv,
                              send.wait_send]
    return starts, waits


def _call_with_sides(body, sides, *, name, grid, in_specs, out_specs, out_shape, scratch_shapes, compiler_params, args):
    if not sides:
        res = pl.pallas_call(body, name=name, grid=grid, in_specs=in_specs, out_specs=out_specs, out_shape=out_shape,
                             scratch_shapes=scratch_shapes, compiler_params=compiler_params)(*args)
        return list(res), []
    s_in, s_out, s_scr = _side_plan(sides)
    n_in, n_out, n_scr, n_side = len(in_specs), len(out_specs), len(scratch_shapes), len(s_in)

    def wrapped(*refs):
        refs = list(refs)
        ins, side_in = refs[:n_in], refs[n_in:n_in + n_side]
        outs = refs[n_in + n_side:n_in + n_side + n_out]
        side_out = refs[n_in + n_side + n_out:n_in + 2 * n_side + n_out]
        rest = refs[n_in + 2 * n_side + n_out:]
        starts, waits = _side_copies(sides, side_in, side_out, rest[n_scr:])
        first = functools.reduce(jnp.logical_and, [pl.program_id(d) == 0 for d in range(len(grid))])
        last = functools.reduce(jnp.logical_and, [pl.program_id(d) == grid[d] - 1 for d in range(len(grid))])

        @pl.when(first)
        def _():
            for start in starts:
                start()

        body(*ins, *outs, *rest[:n_scr])

        @pl.when(last)
        def _():
            for wait in waits:
                wait()

    res = pl.pallas_call(
        wrapped, name=name, grid=grid, in_specs=list(in_specs) + [ANY] * n_side,
        out_specs=list(out_specs) + [ANY] * n_side, out_shape=list(out_shape) + s_out,
        scratch_shapes=list(scratch_shapes) + s_scr, compiler_params=compiler_params,
    )(*args, *s_in)
    return list(res[:n_out]), list(res[n_out:])


def _row_tile(r, l):
    t = min(r, max(16, ELEMENTWISE_TILE_BYTES // (4 * l) // 16 * 16))
    while r % t:
        t -= 16
    return t


def _pair_sum(buf, recv, core, name):
    _, r, l = buf.shape
    tr = _row_tile(r, l)

    def body(core_ref, a_ref, b_ref, o_ref):
        o_ref[...] = (a_ref[...] + b_ref[...]).astype(WIRE_DTYPE)

    return pl.pallas_call(
        body, name=name, out_shape=jax.ShapeDtypeStruct((4, r, l), WIRE_DTYPE),
        grid_spec=pltpu.PrefetchScalarGridSpec(
            num_scalar_prefetch=1, grid=(4, r // tr),
            in_specs=[pl.BlockSpec((None, tr, l), lambda j, i, cr: (2 * j + cr[0], i, 0)),
                      pl.BlockSpec((None, tr, l), lambda j, i, cr: (j, i, 0))],
            out_specs=pl.BlockSpec((None, tr, l), lambda j, i, cr: (j, i, 0))),
        compiler_params=_cparams(dimension_semantics=("arbitrary", "arbitrary")),
    )(core, buf, recv)


def _sum_parts(p_ref):
    return ((p_ref[0].astype(F32) + p_ref[1].astype(F32)) + (p_ref[2].astype(F32) + p_ref[3].astype(F32)))


def _sum4(parts, name):
    _, r, l = parts.shape
    tr = _row_tile(r, l)

    def body(p_ref, o_ref):
        o_ref[...] = _sum_parts(p_ref)

    return pl.pallas_call(
        body, name=name, out_shape=jax.ShapeDtypeStruct((r, l), F32), grid=(r // tr,),
        in_specs=[pl.BlockSpec((4, tr, l), lambda i: (0, i, 0))],
        out_specs=pl.BlockSpec((tr, l), lambda i: (i, 0)),
        compiler_params=_cparams(dimension_semantics=("arbitrary",)),
    )(parts)


def _adamw_update(w, gg, m, v):
    nm = ADAM_B1 * m + (1.0 - ADAM_B1) * gg
    nv = ADAM_B2 * v + (1.0 - ADAM_B2) * (gg * gg)
    m_hat = nm / (1.0 - ADAM_B1 ** ADAM_STEP)
    v_hat = nv / (1.0 - ADAM_B2 ** ADAM_STEP)
    return -ADAM_LR * (m_hat / (jnp.sqrt(v_hat) + ADAM_EPS) + ADAM_WD * w), nm, nv


def _adamw(w, g, m, v, name):
    r, l = w.shape
    tr = _row_tile(r, l)

    def body(w_ref, g_ref, m_ref, v_ref, d_ref, nm_ref, nv_ref):
        d_ref[...], nm_ref[...], nv_ref[...] = _adamw_update(w_ref[...], g_ref[...], m_ref[...], v_ref[...])

    spec = pl.BlockSpec((tr, l), lambda i: (i, 0))
    return pl.pallas_call(
        body, name=name, out_shape=[jax.ShapeDtypeStruct((r, l), F32)] * 3, grid=(r // tr,),
        in_specs=[spec] * 4, out_specs=[spec] * 3,
        compiler_params=_cparams(dimension_semantics=("arbitrary",)),
    )(w, g, m, v)


def _adamw_parts(w, parts, m, v, name):
    nl, r, l = w.shape
    tr = _row_tile(r, l)

    def body(*refs):
        w_ref, p_refs, (m_ref, v_ref, g_ref, d_ref, nm_ref, nv_ref) = refs[0], refs[1:1 + nl], refs[1 + nl:]
        layer = pl.program_id(0)
        gg = _sum_parts(p_refs[0])
        for q in range(1, nl):
            gg = jnp.where(layer == q, _sum_parts(p_refs[q]), gg)
        g_ref[...] = gg
        d_ref[...], nm_ref[...], nv_ref[...] = _adamw_update(w_ref[...], gg, m_ref[...], v_ref[...])

    spec = pl.BlockSpec((None, tr, l), lambda q, i: (q, i, 0))
    pspecs = [pl.BlockSpec((4, tr, l), lambda q, i, k=k: (0, jnp.where(q == k, i, 0), 0)) for k in range(nl)]
    return pl.pallas_call(
        body, name=name, out_shape=[jax.ShapeDtypeStruct((nl, r, l), F32)] * 4, grid=(nl, r // tr),
        in_specs=[spec] + pspecs + [spec, spec], out_specs=[spec] * 4,
        compiler_params=_cparams(dimension_semantics=("arbitrary", "arbitrary")),
    )(w, *parts, m, v)


def _to_rows(pieces, row_multiple):
    flat = jnp.concatenate([p.reshape(-1) for p in pieces])
    rows = -(-flat.shape[0] // LANES)
    rows = -(-rows // row_multiple) * row_multiple
    flat = jnp.pad(flat, (0, rows * LANES - flat.shape[0]))
    return flat.reshape(rows, LANES)


def _split_rows(rows, shapes):
    flat = rows.reshape(-1)
    out, off = [], 0
    for s in shapes:
        n = int(np.prod(s))
        out.append(flat[off:off + n].reshape(s))
        off += n
    return out


def _mod_fwd(cond, w_mod, b_my, name):
    nl, _, ncol = w_mod.shape

    def body(a_ref, w_ref, b_ref, o_ref):
        a = a_ref[...]
        s = a * _sigmoid(a)
        for i in range(nl):
            o_ref[i] = _dot(s, w_ref[i]) + b_ref[i]

    return pl.pallas_call(
        body, name=name, out_shape=jax.ShapeDtypeStruct((nl, 16, ncol), F32),
        compiler_params=_cparams(),
    )(cond, w_mod, b_my)


def _mod_bwd(cond, dm_all, dm_my, w_mod, name):
    nl, _, ncol = w_mod.shape

    def body(a_ref, dma_ref, dmm_ref, w_ref, gw_ref, gb_ref, gc_ref):
        a = a_ref[...]
        sg = _sigmoid(a)
        s = a * sg
        for i in range(nl):
            gw_ref[i] = _dot_tn(s, dmm_ref[i])
            gb_ref[i] = jnp.sum(dma_ref[i], axis=0, keepdims=True)
        back = _dot_nt(dmm_ref[0], w_ref[0])
        dsilu = sg * (1.0 + a * (1.0 - sg))
        gc_ref[...] = jnp.sum(back[8:16] * dsilu[8:16], axis=0, keepdims=True)

    return pl.pallas_call(
        body, name=name,
        out_shape=[jax.ShapeDtypeStruct((nl, D_MODEL, ncol), F32), jax.ShapeDtypeStruct((nl, 1, 3 * D_MODEL), F32),
                   jax.ShapeDtypeStruct((1, D_MODEL), F32)],
        compiler_params=_cparams(),
    )(cond, dm_all, dm_my, w_mod)


def _in_proj(xt, sc, sh, wg, name, sides=()):
    t = xt.shape[0]
    tm = min(TM_MM, t)

    def body(x_ref, sc_ref, sh_ref, w_ref, u_ref, g_ref):
        h = (x_ref[...] * (1.0 + sc_ref[...]) + sh_ref[...]).astype(MXU_DTYPE)
        for k in range(N_WBLK):
            o = jnp.dot(h, w_ref[k], preferred_element_type=F32)
            if k < N_WBLK // 2:
                u_ref[:, k * WBLK:(k + 1) * WBLK] = o
            else:
                kk = k - N_WBLK // 2
                g_ref[:, kk * WBLK:(kk + 1) * WBLK] = o.astype(ACT_DTYPE)

    row = pl.BlockSpec((1, D_MODEL), lambda i: (0, 0))
    return _call_with_sides(
        body, sides, name=name,
        out_shape=[jax.ShapeDtypeStruct((t, D_INNER), F32), jax.ShapeDtypeStruct((t, D_INNER), ACT_DTYPE)],
        grid=(t // tm,),
        in_specs=[pl.BlockSpec((tm, D_MODEL), lambda i: (i, 0)), row, row,
                  pl.BlockSpec((N_WBLK, D_MODEL, WBLK), lambda i: (0, 0, 0), pipeline_mode=pl.Buffered(1))],
        out_specs=[pl.BlockSpec((tm, D_INNER), lambda i: (i, 0))] * 2, scratch_shapes=[],
        compiler_params=_cparams(dimension_semantics=("arbitrary",)), args=[xt, sc, sh, wg])


def _halo_maps(nt, tm, n_blocks, pos, rows=SUBLANES):
    per = tm // rows
    prev = lambda cb, i: (jnp.maximum(pos(i) * per - 1, 0), cb)
    nxt = lambda cb, i: (jnp.minimum((pos(i) + 1) * per, n_blocks - 1), cb)
    return prev, nxt


def _conv_taps(u, prev8, next8, is_first, is_last):
    pz = jnp.where(is_first, 0.0, 1.0)
    nz = jnp.where(is_last, 0.0, 1.0)
    return _shifted(u, prev8 * pz, next8 * nz, [-2, -1, 1])


def _lru_gates(uv, wa_ref, wx_ref, ba, bx, cl, g):
    sl = slice(g * LANES, (g + 1) * LANES)
    uvg = uv[:, sl]
    r = 0.5 * jnp.tanh(_dot(uvg, wa_ref[g]) + ba[:, sl]) + 0.5
    ii = 0.5 * jnp.tanh(_dot(uvg, wx_ref[g]) + bx[:, sl]) + 0.5
    la = cl[:, sl] * r
    a = jnp.exp(la)
    q = jnp.tanh(-la) * (1.0 + a * a)
    rs = lax.rsqrt(jnp.maximum(q, SQRT_FLOOR))
    return uvg, r, ii, a, q * rs, rs


def _scan_rows(seg):
    return -(-(SCAN_ROW_T * (seg - 1) + SCAN_ROW_J * (N_SEG - 1) + 1) // SUBLANES) * SUBLANES


def _seg_chunk(j, c):
    return pl.ds(SCAN_ROW_T * SUBLANES * c + SCAN_ROW_J * j, SUBLANES, stride=SCAN_ROW_T)


def _seg_scatter(ref, g, seg, value):
    for j in range(N_SEG):
        for c in range(seg // SUBLANES):
            r0 = j * seg + SUBLANES * c
            ref[g, _seg_chunk(j, c), :] = value[r0:r0 + SUBLANES]


def _scan_tile(a_s, b_s, carry_ref, write_out, seg, reverse, chunks_per_write=1):
    n_g = a_s.shape[0]
    unroll = SCAN_UNROLL if seg % SCAN_UNROLL == 0 else 1

    n_trips = seg // unroll

    def steps(k, state):
        hs, cs = list(state[0]), list(state[1])
        base = ((n_trips - 1 - k) if reverse else k) * unroll
        for q in (range(unroll - 1, -1, -1) if reverse else range(unroll)):
            t = base + q
            rows = pl.ds(t * SCAN_ROW_T, N_SEG, stride=SCAN_ROW_J)
            for g in range(n_g):
                a = a_s[g, rows, :]
                b = b_s[g, rows, :]
                hs[g] = a * hs[g] + b
                cs[g] = a * cs[g]
                b_s[g, rows, :] = hs[g]
                a_s[g, rows, :] = cs[g]
        return tuple(hs), tuple(cs)

    zeros = tuple(jnp.zeros((N_SEG, LANES), F32) for _ in range(n_g))
    ones = tuple(jnp.ones((N_SEG, LANES), F32) for _ in range(n_g))
    h_fin, a_fin = lax.fori_loop(0, seg // unroll, steps, (zeros, ones))

    order = list(range(N_SEG - 1, -1, -1)) if reverse else list(range(N_SEG))
    for g in range(n_g):
        carry = carry_ref[:, g * LANES:(g + 1) * LANES]
        for j in order:
            for c0 in range(0, seg // SUBLANES, chunks_per_write):
                parts = [b_s[g, _seg_chunk(j, c), :] + a_s[g, _seg_chunk(j, c), :] * carry
                         for c in range(c0, c0 + chunks_per_write)]
                write_out(j, c0, g, parts[0] if chunks_per_write == 1 else jnp.concatenate(parts, axis=0))
            carry = a_fin[g][j:j + 1] * carry + h_fin[g][j:j + 1]
        carry_ref[:, g * LANES:(g + 1) * LANES] = carry


def _lru_specs(s, tm, cb, direction_pos, nt):
    n_rows8 = s // SUBLANES
    prev, nxt = _halo_maps(nt, tm, n_rows8, direction_pos)
    tile = pl.BlockSpec((tm, cb), lambda c, i: (direction_pos(i), c))
    return tile, pl.BlockSpec((SUBLANES, cb), prev), pl.BlockSpec((SUBLANES, cb), nxt)


def _lru_param_specs(cb, d):
    n_g = cb // LANES
    vec = pl.BlockSpec((1, cb), lambda c, i: (0, c))
    dvec = pl.BlockSpec((None, 1, cb), lambda c, i: (d, 0, c))
    wmat = pl.BlockSpec((None, n_g, LRU_BLOCK, LRU_BLOCK), lambda c, i: (d, c, 0, 0))
    return vec, dvec, wmat


def _lru_fwd(src, h0, p, d, name, conv, sides=()):
    s = src.shape[0]
    tm = min(TM_LRU, s)
    cb = CB_LRU
    n_g = cb // LANES
    nt = s // tm
    seg = tm // N_SEG
    pos = (lambda i: i) if d == 0 else (lambda i: nt - 1 - i)

    def body(*refs):
        refs = list(refs)
        u_ref = refs.pop(0)
        if conv:
            up_ref, un_ref, cw_ref, cbias_ref = [refs.pop(0) for _ in range(4)]
        wa_ref, wx_ref, ba_ref, bx_ref, lam_ref, h0_ref, h_ref, hc_ref = [refs.pop(0) for _ in range(8)]
        uv_ref = refs.pop(0) if conv else None
        a_s, b_s = refs
        i = pl.program_id(1)
        tp = pos(i)

        @pl.when(i == 0)
        def _():
            hc_ref[...] = h0_ref[...]

        if conv:
            u_t = u_ref[...]
            um2, um1, up1 = _conv_taps(u_t, up_ref[...], un_ref[...], tp == 0, tp == nt - 1)
            cw = cw_ref[...]
            uv_ref[...] = um2 * cw[0:1] + um1 * cw[1:2] + u_t * cw[2:3] + up1 * cw[3:4] + cbias_ref[...]
        src_ref = uv_ref if conv else u_ref
        cl = LRU_C * _log_sigmoid(lam_ref[...])
        ba, bx = ba_ref[...], bx_ref[...]
        for g in range(n_g):
            uvg, r, ii, a, sq, _ = _lru_gates(src_ref, wa_ref, wx_ref, ba, bx, cl, g)
            b = sq * (ii * uvg)
            _seg_scatter(a_s, g, seg, a)
            _seg_scatter(b_s, g, seg, b)

        per_write = 2 if (seg // SUBLANES) % 2 == 0 else 1

        def write_out(j, c, g, h):
            h_ref[pl.ds(j * seg + SUBLANES * c, SUBLANES * per_write), pl.ds(g * LANES, LANES)] = h.astype(ACT_DTYPE)

        _scan_tile(a_s, b_s, hc_ref, write_out, seg, reverse=(d == 1), chunks_per_write=per_write)

    tile, prev, nxt = _lru_specs(s, tm, cb, pos, nt)
    vec, dvec, wmat = _lru_param_specs(cb, d)
    wide = jax.ShapeDtypeStruct((s, D_INNER), F32)
    conv_specs = [prev, nxt, pl.BlockSpec((4, cb), lambda c, i: (0, c)), vec] if conv else []
    conv_args = [src, src, p["conv_w"], p["conv_b"]] if conv else []
    return _call_with_sides(
        body, sides, name=name,
        out_shape=[jax.ShapeDtypeStruct((s, D_INNER), ACT_DTYPE), jax.ShapeDtypeStruct((1, D_INNER), F32)]
        + ([wide] if conv else []),
        grid=(D_INNER // cb, nt),
        in_specs=[tile] + conv_specs + [wmat, wmat, dvec, dvec, dvec, vec],
        out_specs=[tile, vec] + ([tile] if conv else []),
        scratch_shapes=[pltpu.VMEM((n_g, _scan_rows(seg), LANES), F32)] * 2,
        compiler_params=_cparams(dimension_semantics=("arbitrary", "arbitrary")),
        args=[src, *conv_args, p["wa"], p["wx"], p["ba"], p["bx"], p["lam"], h0])


def _lru_bwd(uv, dh, h, h0, lam_in, p, d, name, sides=()):
    s = uv.shape[0]
    tm = min(TM_LRU, s)
    cb = CB_LRU
    n_g = cb // LANES
    nt = s // tm
    seg = tm // N_SEG
    pos = (lambda i: nt - 1 - i) if d == 0 else (lambda i: i)

    def body(uv_ref, dh_ref, h_ref, hh_ref, wa_ref, wx_ref, ba_ref, bx_ref,
             lam_ref, h0_ref, lin_ref, duv_ref, gwa_ref, gwx_ref, gv_ref, lc_ref, a_s, b_s, lp_s,
             r_s, i_s, q_s, rq_s, a_keep):
        i = pl.program_id(1)
        tp = pos(i)

        @pl.when(i == 0)
        def _():
            lc_ref[...] = lin_ref[...]
            gwa_ref[...] = jnp.zeros_like(gwa_ref)
            gwx_ref[...] = jnp.zeros_like(gwx_ref)
            gv_ref[...] = jnp.zeros_like(gv_ref)

        uv = uv_ref[...]
        lam = lam_ref[...]
        cl = LRU_C * _log_sigmoid(lam)
        ba, bx = ba_ref[...], bx_ref[...]
        dh_t = dh_ref[...].astype(F32)
        carry_in = lc_ref[...]
        for g in range(n_g):
            sl = slice(g * LANES, (g + 1) * LANES)
            _, r, ii, a, sq, rs = _lru_gates(uv, wa_ref, wx_ref, ba, bx, cl, g)
            for ref, val in ((r_s, r), (i_s, ii), (q_s, sq), (rq_s, rs), (a_keep, a)):
                ref[:, sl] = val.astype(ACT_DTYPE)
            b = a * dh_t[:, sl]
            _seg_scatter(a_s, g, seg, a)
            _seg_scatter(b_s, g, seg, b)

        def write_out(j, c, g, v):
            lp_s[pl.ds(j * seg + SUBLANES * c, SUBLANES), pl.ds(g * LANES, LANES)] = v

        _scan_tile(a_s, b_s, lc_ref, write_out, seg, reverse=(d == 0))

        h_t = h_ref[...].astype(F32)
        hh = hh_ref[...].astype(F32)
        if d == 0:
            edge = jnp.where(tp == 0, h0_ref[...], hh[H_HALO - 1:H_HALO])
            h_prev = _shift_down(h_t, edge)
            lam_t = dh_t + _shift_up(lp_s[...], carry_in)
        else:
            edge = jnp.where(tp == nt - 1, h0_ref[...], hh[0:1])
            h_prev = _shift_up(h_t, edge)
            lam_t = dh_t + _shift_down(lp_s[...], carry_in)

        dsig = LRU_C * _sigmoid(-lam)
        cl2 = cl + cl
        for g in range(n_g):
            sl = slice(g * LANES, (g + 1) * LANES)
            uvg = uv[:, sl]
            r, ii, a, sq, rs =[ref[:, sl].astype(F32) for ref in (r_s, i_s, a_keep, q_s, rq_s)]
            lt = lam_t[:, sl]
            ls = lt * sq
            dla = (lt * a) * (h_prev[:, sl] - (ii * uvg) * (a * rs))
            dzr = (dla * cl2[:, sl]) * r * (1.0 - r)
            dzi = ((ls + ls) * uvg) * ii * (1.0 - ii)
            duv_ref[:, sl] = (ls * ii + _dot_nt(dzr, wa_ref[g]) + _dot_nt(dzi, wx_ref[g])).astype(ACT_DTYPE)
            gwa_ref[g] += _dot_tn(uvg, dzr)
            gwx_ref[g] += _dot_tn(uvg, dzi)
            gv_ref[0:1, sl] += _rowsum(dzr)
            gv_ref[1:2, sl] += _rowsum(dzi)
            gv_ref[2:3, sl] += _rowsum(dla * r) * dsig[:, sl]

        @pl.when(i == nt - 1)
        def _():
            gwa_ref[...] = 0.5 * gwa_ref[...]
            gwx_ref[...] = 0.5 * gwx_ref[...]
            gv_ref[0:2, :] = 0.5 * gv_ref[0:2, :]

    tile, _, _ = _lru_specs(s, tm, cb, pos, nt)
    vec, dvec, wmat = _lru_param_specs(cb, d)
    h_prev_map, h_next_map = _halo_maps(nt, tm, s // H_HALO, pos, rows=H_HALO)
    hh_spec = pl.BlockSpec((H_HALO, cb), h_prev_map if d == 0 else h_next_map)
    gw_spec = pl.BlockSpec((n_g, LRU_BLOCK, LRU_BLOCK), lambda c, i: (c, 0, 0))
    n_blk = D_INNER // LRU_BLOCK
    return _call_with_sides(
        body, sides, name=name,
        out_shape=[jax.ShapeDtypeStruct((s, D_INNER), ACT_DTYPE),
                   jax.ShapeDtypeStruct((n_blk, LRU_BLOCK, LRU_BLOCK), F32),
                   jax.ShapeDtypeStruct((n_blk, LRU_BLOCK, LRU_BLOCK), F32),
                   jax.ShapeDtypeStruct((SUBLANES, D_INNER), F32),
                   jax.ShapeDtypeStruct((1, D_INNER), F32)],
        grid=(D_INNER // cb, nt),
        in_specs=[tile, tile, tile, hh_spec, wmat, wmat, dvec, dvec, dvec, vec, vec],
        out_specs=[tile, gw_spec, gw_spec, pl.BlockSpec((SUBLANES, cb), lambda c, i: (0, c)), vec],
        scratch_shapes=[pltpu.VMEM((n_g, _scan_rows(seg), LANES), F32)] * 2 + [pltpu.VMEM((tm, cb), F32)]
        + [pltpu.VMEM((tm, cb), ACT_DTYPE)] * 5,
        compiler_params=_cparams(dimension_semantics=("arbitrary", "arbitrary")),
        args=[uv, dh, h, h, p["wa"], p["wx"], p["ba"], p["bx"], p["lam"], h0, lam_in])


def _out0(hf, hb, g, xt, gt, wo, lg, lb, sc1, sh1, wg1, name):
    t = xt.shape[0]
    tm = min(TM_MM, t)

    def body(hf_ref, hb_ref, g_ref, x_ref, gt_ref, w_ref, lg_ref, lb_ref, sc1_ref, sh1_ref, w1_ref,
             x1_ref, br_ref, u1_ref, g1_ref):
        br = None
        for k in range(D_INNER // WBLK):
            sl = slice(k * WBLK, (k + 1) * WBLK)
            gg = g_ref[:, sl].astype(F32)
            p = (hf_ref[:, sl].astype(F32) + hb_ref[:, sl].astype(F32)) * (gg * _sigmoid(gg))
            part = _dot(p, w_ref[sl, :])
            br = part if br is None else br + part
        z = ALPHA * x_ref[...] + gt_ref[...] * br
        xhat, _ = _layer_norm_stats(z)
        x1 = xhat * lg_ref[...] + lb_ref[...]
        x1_ref[...] = x1
        br_ref[...] = br.astype(ACT_DTYPE)
        h1 = (x1 * (1.0 + sc1_ref[...]) + sh1_ref[...]).astype(MXU_DTYPE)
        for k in range(N_WBLK):
            o = jnp.dot(h1, w1_ref[k], preferred_element_type=F32).astype(ACT_DTYPE)
            if k < N_WBLK // 2:
                u1_ref[:, k * WBLK:(k + 1) * WBLK] = o
            else:
                kk = k - N_WBLK // 2
                g1_ref[:, kk * WBLK:(kk + 1) * WBLK] = o

    wide = pl.BlockSpec((tm, D_INNER), lambda i: (i, 0))
    nar = pl.BlockSpec((tm, D_MODEL), lambda i: (i, 0))
    row = pl.BlockSpec((1, D_MODEL), lambda i: (0, 0))
    return pl.pallas_call(
        body, name=name,
        out_shape=[jax.ShapeDtypeStruct((t, D_MODEL), F32), jax.ShapeDtypeStruct((t, D_MODEL), ACT_DTYPE),
                   jax.ShapeDtypeStruct((t, D_INNER), ACT_DTYPE), jax.ShapeDtypeStruct((t, D_INNER), ACT_DTYPE)],
        grid=(t // tm,),
        in_specs=[wide, wide, wide, nar, row,
                  pl.BlockSpec((D_INNER, D_MODEL), lambda i: (0, 0), pipeline_mode=pl.Buffered(1)), row, row, row, row,
                  pl.BlockSpec((N_WBLK, D_MODEL, WBLK), lambda i: (0, 0, 0), pipeline_mode=pl.Buffered(1))],
        out_specs=[nar, nar, wide, wide],
        compiler_params=_cparams(dimension_semantics=("arbitrary",)),
    )(hf, hb, g, xt, gt, wo, lg, lb, sc1, sh1, wg1)


def _unrolled_loop(n, fn, unroll=4):
    while n % unroll:
        unroll //= 2

    def trip(k, carry):
        for q in range(unroll):
            fn(k * unroll + q)
        return carry
    lax.fori_loop(0, n // unroll, trip, 0)


def _window(n, w):
    t = np.arange(n)
    return np.clip(t - w // 2, 0, n), np.clip(t + w // 2, 0, n)


def _pool_tables(n_rows, transpose):
    boxes, inv_c, inv_r = [], [], []
    for w in POOL_WINDOWS:
        lo, hi = _window(GRID_W, w)
        m = np.zeros((GRID_W, GRID_W), np.float32)
        for r in range(GRID_W):
            m[r, lo[r]:hi[r]] = 1.0
        m = np.kron(np.eye(POOL_TOK // GRID_W, dtype=np.float32), m)
        boxes.append(m.T if transpose else m)
        inv_c.append(np.broadcast_to((1.0 / (hi - lo).astype(np.float32))[:, None], (GRID_W, LANES)))
        lo_r, hi_r = _window(n_rows, w)
        inv_r.append(1.0 / (hi_r - lo_r).astype(np.float32))
    return (jnp.asarray(np.stack(boxes), MXU_DTYPE), jnp.asarray(np.stack(inv_c), F32),
            jnp.asarray(np.stack(inv_r), F32))


def _pool_mix(xin, transpose, out_dtype, name):
    s = xin.shape[0]
    n_rows = s // GRID_W
    pad_t = SUBLANES * GRID_W
    rows_per_blk = POOL_TOK // GRID_W
    n_slab = D_INNER // LANES
    slabs_per_group = POOL_GROUP // LANES
    n_win = len(POOL_WINDOWS)
    boxes, inv_c, inv_r = _pool_tables(n_rows, transpose)
    exact_operand = (not transpose) and xin.dtype == MXU_DTYPE and MXU_DTYPE != F32

    def body(invr_ref, box_ref, invc_ref, x_ref, o_ref, pad_s):
        k = pl.program_id(0) // slabs_per_group
        pad_s[pl.ds(0, pad_t), :] = jnp.zeros((pad_t, LANES), F32)
        pad_s[pl.ds(pad_t + s, pad_t), :] = jnp.zeros((pad_t, LANES), F32)

        for kk, w in enumerate(POOL_WINDOWS):
            half = w // 2
            offsets = list(range(-(half - 1), half + 1)) if transpose else list(range(-half, half))

            @pl.when(k == kk)
            def _():
                inv_col = invc_ref[kk]

                def col_box(b):
                    st = pl.multiple_of(b * POOL_TOK, POOL_TOK)
                    xb = x_ref[pl.ds(st, POOL_TOK), :]
                    if exact_operand:
                        pad_s[pl.ds(pad_t + st, POOL_TOK), :] = jnp.dot(box_ref[kk], xb, preferred_element_type=F32)
                        return
                    xb = xb.astype(F32)
                    if transpose:
                        xb = xb * jnp.concatenate(
                            [inv_col * invr_ref[kk, b * rows_per_blk + q] for q in range(rows_per_blk)], axis=0)
                    hi = xb.astype(MXU_DTYPE)
                    lo = (xb - hi.astype(F32)).astype(MXU_DTYPE)
                    both = jnp.dot(box_ref[kk], jnp.concatenate([hi, lo], axis=1), preferred_element_type=F32)
                    pad_s[pl.ds(pad_t + st, POOL_TOK), :] = both[:, :LANES] + both[:, LANES:]
                _unrolled_loop(s // POOL_TOK, col_box)

                def row_box(r):
                    st = pl.multiple_of(r * GRID_W, GRID_W)
                    acc = pad_s[pl.ds(pad_t + st + offsets[0] * GRID_W, GRID_W), :]
                    for o in offsets[1:]:
                        acc = acc + pad_s[pl.ds(pad_t + st + o * GRID_W, GRID_W), :]
                    if not transpose:
                        acc = acc * (inv_col * invr_ref[kk, r])
                    o_ref[pl.ds(st, GRID_W), :] = (acc - x_ref[pl.ds(st, GRID_W), :].astype(F32)).astype(out_dtype)
                _unrolled_loop(n_rows, row_box)

    slab = pl.BlockSpec((s, LANES), lambda i: (0, i))
    return pl.pallas_call(
        body, name=name, out_shape=jax.ShapeDtypeStruct((s, D_INNER), out_dtype), grid=(n_slab,),
        in_specs=[pl.BlockSpec(memory_space=pltpu.SMEM),
                  pl.BlockSpec((n_win, POOL_TOK, POOL_TOK), lambda i: (0, 0, 0)),
                  pl.BlockSpec((n_win, GRID_W, LANES), lambda i: (0, 0, 0)), slab],
        out_specs=slab,
        scratch_shapes=[pltpu.VMEM((s + 2 * pad_t, LANES), F32)],
        compiler_params=_cparams(dimension_semantics=("arbitrary",)),
    )(inv_r, boxes, inv_c, xin)


def _out1(dmix, pw, ps, g, x1, gt, wo, lg, lb, tgt, name):
    t = x1.shape[0]
    tm = min(TM_MM, t)
    n_grp = len(POOL_WINDOWS)

    def body(d_ref, pw_ref, ps_ref, g_ref, x1_ref, gt_ref, w_ref, lg_ref, lb_ref, tgt_ref, dz_ref, st_ref, po_ref):
        @pl.when(pl.program_id(0) == 0)
        def _():
            st_ref[...] = jnp.zeros_like(st_ref)

        br = jnp.zeros((tm, D_MODEL), F32)
        for k in range(n_grp):
            sl = slice(k * POOL_GROUP, (k + 1) * POOL_GROUP)
            po = jnp.dot(d_ref[:, sl], pw_ref[k], preferred_element_type=F32)
            po_ref[:, sl] = po.astype(ACT_DTYPE)
            y = po * ps_ref[:, sl]
            gg = g_ref[:, sl].astype(F32)
            br = br + _dot(y * (gg * _sigmoid(gg)), w_ref[sl, :])
        z = ALPHA * x1_ref[...] + gt_ref[...] * br
        xhat, rstd = _layer_norm_stats(z)
        lg_v = lg_ref[...]
        err = xhat * lg_v + lb_ref[...] - tgt_ref[...]
        dy = err * (1.0 / D_MODEL)
        dz = _layer_norm_bwd(dy, xhat, rstd, lg_v)
        dz_ref[...] = dz.astype(ACT_DTYPE)
        st_ref[0:1, :] += _rowsum(dy * xhat)
        st_ref[1:2, :] += _rowsum(dy)
        st_ref[2:3, :] += _rowsum(dz * br)
        st_ref[3:4, :] += _rowsum(err * err)

    wide = pl.BlockSpec((tm, D_INNER), lambda i: (i, 0))
    nar = pl.BlockSpec((tm, D_MODEL), lambda i: (i, 0))
    row = pl.BlockSpec((1, D_MODEL), lambda i: (0, 0))
    return pl.pallas_call(
        body, name=name,
        out_shape=[jax.ShapeDtypeStruct((t, D_MODEL), ACT_DTYPE), jax.ShapeDtypeStruct((SUBLANES, D_MODEL), F32),
                   jax.ShapeDtypeStruct((t, D_INNER), ACT_DTYPE)],
        grid=(t // tm,),
        in_specs=[wide, pl.BlockSpec((n_grp, POOL_GROUP, POOL_GROUP), lambda i: (0, 0, 0)),
                  pl.BlockSpec((1, D_INNER), lambda i: (0, 0)), wide, nar, row,
                  pl.BlockSpec((D_INNER, D_MODEL), lambda i: (0, 0), pipeline_mode=pl.Buffered(1)), row, row, nar],
        out_specs=[nar, pl.BlockSpec((SUBLANES, D_MODEL), lambda i: (0, 0)), wide],
        compiler_params=_cparams(dimension_semantics=("arbitrary",)),
    )(dmix, pw, ps, g, x1, gt, wo, lg, lb, tgt)


def _flush(acc, out_hbm, sem):
    cp = pltpu.make_async_copy(acc, out_hbm, sem)
    cp.start()
    cp.wait()


def _bout1(dz, dmix, po, g, pw, ps, gt, wo, name):
    t = dz.shape[0]
    tm = min(TM_MM, t)
    nt = t // tm
    n_grp = len(POOL_WINDOWS)

    def body(dz_ref, d_ref, po_ref, g_ref, pw_ref, ps_ref, gt_ref, w_ref, dd_ref, dg_ref, gwo_hbm, gpw_hbm, gps_ref,
             gwo_acc, gpw_acc, sems):
        i = pl.program_id(0)

        @pl.when(i == 0)
        def _():
            gwo_acc[...] = jnp.zeros_like(gwo_acc)
            gpw_acc[...] = jnp.zeros_like(gpw_acc)
            gps_ref[...] = jnp.zeros_like(gps_ref)

        db = (gt_ref[...] * dz_ref[...].astype(F32)).astype(MXU_DTYPE)
        for k in range(n_grp):
            sl = slice(k * POOL_GROUP, (k + 1) * POOL_GROUP)
            dk = d_ref[:, sl]
            po = po_ref[:, sl].astype(F32)
            psk = ps_ref[:, sl]
            y = po * psk
            gg = g_ref[:, sl].astype(F32)
            sg = _sigmoid(gg)
            silu = gg * sg
            gwo_acc[sl, :] += _dot_tn(y * silu, db)
            dp = _dot_nt(db, w_ref[sl, :])
            dy = dp * silu
            dg_ref[:, sl] = (dp * y * (sg * (1.0 + gg * (1.0 - sg)))).astype(MXU_DTYPE)
            gps_ref[0:1, sl] += _rowsum(dy * po)
            dpo = (dy * psk).astype(MXU_DTYPE)
            gpw_acc[k] += _dot_tn(dk, dpo)
            dd_ref[:, sl] = _dot_nt(dpo, pw_ref[k])

        @pl.when(i == nt - 1)
        def _():
            _flush(gwo_acc, gwo_hbm, sems.at[0])
            _flush(gpw_acc, gpw_hbm, sems.at[1])

    wide = pl.BlockSpec((tm, D_INNER), lambda i: (i, 0))
    nar = pl.BlockSpec((tm, D_MODEL), lambda i: (i, 0))
    return pl.pallas_call(
        body, name=name,
        out_shape=[jax.ShapeDtypeStruct((t, D_INNER), F32), jax.ShapeDtypeStruct((t, D_INNER), MXU_DTYPE),
                   jax.ShapeDtypeStruct((D_INNER, D_MODEL), F32),
                   jax.ShapeDtypeStruct((n_grp, POOL_GROUP, POOL_GROUP), F32),
                   jax.ShapeDtypeStruct((SUBLANES, D_INNER), F32)],
        grid=(nt,),
        in_specs=[nar, wide, wide, wide,
                  pl.BlockSpec((n_grp, POOL_GROUP, POOL_GROUP), lambda i: (0, 0, 0), pipeline_mode=pl.Buffered(1)),
                  pl.BlockSpec((1, D_INNER), lambda i: (0, 0)), pl.BlockSpec((1, D_MODEL), lambda i: (0, 0)),
                  pl.BlockSpec((D_INNER, D_MODEL), lambda i: (0, 0), pipeline_mode=pl.Buffered(1))],
        out_specs=[wide, wide, ANY, ANY, pl.BlockSpec((SUBLANES, D_INNER), lambda i: (0, 0))],
        scratch_shapes=[pltpu.VMEM((D_INNER, D_MODEL), F32), pltpu.VMEM((n_grp, POOL_GROUP, POOL_GROUP), F32),
                        pltpu.SemaphoreType.DMA((2,))],
        compiler_params=_cparams(dimension_semantics=("arbitrary",)),
    )(dz, dmix, po, g, pw, ps, gt, wo)


def _bout0(dx1, xt, br0, lg, hf, hb, g, gt, wo, name, sides=()):
    t = dx1.shape[0]
    tm = min(TM_MM, t)
    nt = t // tm

    def body(dx_ref, x_ref, br_ref, lg_ref, hf_ref, hb_ref, g_ref, gt_ref, w_ref,
             dz_ref, dy_ref, dg_ref, gwo_hbm, st_ref, gwo_acc, sem):
        i = pl.program_id(0)

        @pl.when(i == 0)
        def _():
            gwo_acc[...] = jnp.zeros_like(gwo_acc)
            st_ref[...] = jnp.zeros_like(st_ref)

        dx = dx_ref[...].astype(F32)
        br = br_ref[...].astype(F32)
        gate = gt_ref[...]
        xhat, rstd = _layer_norm_stats(ALPHA * x_ref[...] + gate * br)
        dz = _layer_norm_bwd(dx, xhat, rstd, lg_ref[...])
        dz_ref[...] = dz.astype(ACT_DTYPE)
        st_ref[0:1, :] += _rowsum(dx * xhat)
        st_ref[1:2, :] += _rowsum(dx)
        st_ref[2:3, :] += _rowsum(dz * br)
        db = (gate * dz).astype(MXU_DTYPE)
        for k in range(D_INNER // WBLK):
            sl = slice(k * WBLK, (k + 1) * WBLK)
            y = hf_ref[:, sl].astype(F32) + hb_ref[:, sl].astype(F32)
            gg = g_ref[:, sl].astype(F32)
            sg = _sigmoid(gg)
            silu = gg * sg
            gwo_acc[sl, :] += _dot_tn(y * silu, db)
            dp = _dot_nt(db, w_ref[sl, :])
            dy_ref[:, sl] = (dp * silu).astype(ACT_DTYPE)
            dg_ref[:, sl] = (dp * y * (sg * (1.0 + gg * (1.0 - sg)))).astype(MXU_DTYPE)

        @pl.when(i == nt - 1)
        def _():
            _flush(gwo_acc, gwo_hbm, sem)

    wide = pl.BlockSpec((tm, D_INNER), lambda i: (i, 0))
    nar = pl.BlockSpec((tm, D_MODEL), lambda i: (i, 0))
    row = pl.BlockSpec((1, D_MODEL), lambda i: (0, 0))
    return _call_with_sides(
        body, sides, name=name,
        out_shape=[jax.ShapeDtypeStruct((t, D_MODEL), ACT_DTYPE), jax.ShapeDtypeStruct((t, D_INNER), ACT_DTYPE),
                   jax.ShapeDtypeStruct((t, D_INNER), MXU_DTYPE), jax.ShapeDtypeStruct((D_INNER, D_MODEL), F32),
                   jax.ShapeDtypeStruct((SUBLANES, D_MODEL), F32)],
        grid=(nt,),
        in_specs=[nar, nar, nar, row, wide, wide, wide, row,
                  pl.BlockSpec((D_INNER, D_MODEL), lambda i: (0, 0), pipeline_mode=pl.Buffered(1))],
        out_specs=[nar, wide, wide, ANY, pl.BlockSpec((SUBLANES, D_MODEL), lambda i: (0, 0))],
        scratch_shapes=[pltpu.VMEM((D_INNER, D_MODEL), F32), pltpu.SemaphoreType.DMA(())],
        compiler_params=_cparams(dimension_semantics=("arbitrary",)),
        args=[dx1, xt, br0, lg, hf, hb, g, gt, wo])


def _conv_bwd(duvf, duvb, u, conv_w, name, sides=()):
    s = u.shape[0]
    tm = min(TM_LRU, s)
    cb = CB_LRU
    nt = s // tm

    def body(df_ref, dfp_ref, dfn_ref, db_ref, dbp_ref, dbn_ref, u_ref, cw_ref, du_ref, cst_ref):
        i = pl.program_id(1)

        @pl.when(i == 0)
        def _():
            cst_ref[...] = jnp.zeros_like(cst_ref)

        first, last = i == 0, i == nt - 1
        pz = jnp.where(first, 0.0, 1.0)
        nz = jnp.where(last, 0.0, 1.0)
        dout = df_ref[...].astype(F32) + db_ref[...].astype(F32)
        before = (dfp_ref[...].astype(F32) + dbp_ref[...].astype(F32))[H_HALO - SUBLANES:] * pz
        after = (dfn_ref[...].astype(F32) + dbn_ref[...].astype(F32))[:SUBLANES] * nz
        dm1, dp1, dp2 = _shifted(dout, before, after, [-1, 1, 2])
        cw = cw_ref[...]
        du_ref[...] = (dp2 * cw[0:1] + dp1 * cw[1:2] + dout * cw[2:3] + dm1 * cw[3:4]).astype(MXU_DTYPE)
        u_t = u_ref[...]
        cst_ref[0:1, :] += _rowsum(dp2 * u_t)
        cst_ref[1:2, :] += _rowsum(dp1 * u_t)
        cst_ref[2:3, :] += _rowsum(dout * u_t)
        cst_ref[3:4, :] += _rowsum(dm1 * u_t)
        cst_ref[4:5, :] += _rowsum(dout)

    tile, _, _ = _lru_specs(s, tm, cb, lambda i: i, nt)
    prev_map, next_map = _halo_maps(nt, tm, s // H_HALO, lambda i: i, rows=H_HALO)
    prev, nxt = pl.BlockSpec((H_HALO, cb), prev_map), pl.BlockSpec((H_HALO, cb), next_map)
    return _call_with_sides(
        body, sides, name=name,
        out_shape=[jax.ShapeDtypeStruct((s, D_INNER), MXU_DTYPE), jax.ShapeDtypeStruct((SUBLANES, D_INNER), F32)],
        grid=(D_INNER // cb, nt),
        in_specs=[tile, prev, nxt] * 2 + [tile, pl.BlockSpec((4, cb), lambda c, i: (0, c))],
        out_specs=[tile, pl.BlockSpec((SUBLANES, cb), lambda c, i: (0, c))], scratch_shapes=[],
        compiler_params=_cparams(dimension_semantics=("arbitrary", "arbitrary")),
        args=[duvf, duvf, duvf, duvb, duvb, duvb, u, conv_w])


def _bin(du, dg, xin, dzin, sc, sh, wg, name, gw_init=None, sides=(), dx_dtype=F32):
    t = xin.shape[0]
    tm = min(TM_MM, t)
    nt = t // tm
    has_g, has_dx, has_init = dg is not None, dzin is not None, gw_init is not None
    half = N_WBLK // 2
    n_blk = N_WBLK if has_g else half

    def body(*refs):
        refs = list(refs)
        du_ref = refs.pop(0)
        dg_ref = refs.pop(0) if has_g else None
        x_ref = refs.pop(0)
        dz_ref = refs.pop(0) if has_dx else None
        sc_ref, sh_ref, w_ref = refs.pop(0), refs.pop(0), refs.pop(0)
        init_hbm = refs.pop(0) if has_init else None
        dx_ref = refs.pop(0) if has_dx else None
        gw_hbm, st_ref, gw_acc, sem = refs
        i = pl.program_id(0)

        @pl.when(i == 0)
        def _():
            st_ref[...] = jnp.zeros_like(st_ref)
            first_zero = 0
            if has_init:
                _flush(init_hbm, gw_acc.at[pl.ds(0, half)], sem)
                first_zero = half
            for k in range(first_zero, n_blk):
                gw_acc[k] = jnp.zeros((D_MODEL, WBLK), F32)

        xv = x_ref[...]
        scale = 1.0 + sc_ref[...]
        h = (xv * scale + sh_ref[...]).astype(MXU_DTYPE)
        dh = None
        for k in range(n_blk):
            src = du_ref if k < half else dg_ref
            kk = k % half
            dk = src[:, kk * WBLK:(kk + 1) * WBLK]
            gw_acc[k] += _dot_tn(h, dk)
            contrib = _dot_nt(dk, w_ref[k])
            dh = contrib if dh is None else dh + contrib
        st_ref[0:1, :] += _rowsum(dh * xv)
        st_ref[1:2, :] += _rowsum(dh)
        if has_dx:
            dx_ref[...] = (ALPHA * dz_ref[...].astype(F32) + dh * scale).astype(dx_dtype)

        @pl.when(i == nt - 1)
        def _():
            _flush(gw_acc, gw_hbm, sem)

    wide = pl.BlockSpec((tm, D_INNER), lambda i: (i, 0))
    nar = pl.BlockSpec((tm, D_MODEL), lambda i: (i, 0))
    row = pl.BlockSpec((1, D_MODEL), lambda i: (0, 0))
    wspec = pl.BlockSpec((n_blk, D_MODEL, WBLK), lambda i: (0, 0, 0), pipeline_mode=pl.Buffered(1))
    in_specs = ([wide] + ([wide] if has_g else []) + [nar] + ([nar] if has_dx else []) + [row, row, wspec]
                + ([ANY] if has_init else []))
    args = ([du] + ([dg] if has_g else []) + [xin] + ([dzin] if has_dx else []) + [sc, sh, wg]
            + ([gw_init] if has_init else []))
    out_shape = ([jax.ShapeDtypeStruct((t, D_MODEL), dx_dtype)] if has_dx else []) + [
        jax.ShapeDtypeStruct((n_blk, D_MODEL, WBLK), F32), jax.ShapeDtypeStruct((SUBLANES, D_MODEL), F32)]
    out_specs = ([nar] if has_dx else []) + [ANY, pl.BlockSpec((SUBLANES, D_MODEL), lambda i: (0, 0))]
    return _call_with_sides(
        body, sides, name=name, out_shape=out_shape, grid=(nt,), in_specs=in_specs, out_specs=out_specs,
        scratch_shapes=[pltpu.VMEM((n_blk, D_MODEL, WBLK), F32), pltpu.SemaphoreType.DMA(())],
        compiler_params=_cparams(dimension_semantics=("arbitrary",)), args=args)


def _blocks_by_device(a, axis):
    shape = a.shape
    a = a.reshape(shape[:axis] + (N_DEV, shape[axis] // N_DEV) + shape[axis + 1:])
    return jnp.moveaxis(a, axis, 0)


def kernel(x, c, ctx, c_ctx, w_mod, b_mod, w_in, w_out, ln_g, ln_b, conv_w, conv_b, lru_wa, lru_ba, lru_wx, lru_bx, lru_lam, pool_w, pool_scale, loss_target, m_c_ctx, m_w_mod, m_b_mod, m_w_in, m_w_out, m_ln_g, m_ln_b, m_conv_w, m_conv_b, m_lru_wa, m_lru_ba, m_lru_wx, m_lru_bx, m_lru_lam, m_pool_w, m_pool_scale, v_c_ctx, v_w_mod, v_b_mod, v_w_in, v_w_out, v_ln_g, v_ln_b, v_conv_w, v_conv_b, v_lru_wa, v_lru_ba, v_lru_wx, v_lru_bx, v_lru_lam, v_pool_w, v_pool_scale):
    xi, yi, ci = _my_pos()
    dev = 4 * xi + 2 * yi + ci
    xt, ctxt, tgt = x[0], ctx[0], loss_target[0]
    n_mod = w_mod.shape[2]

    small_shapes = [(D_MODEL,), conv_w.shape[1:], lru_ba.shape[1:], lru_bx.shape[1:], lru_lam.shape[1:],
                    pool_scale.shape[1:]]
    small = _to_rows([c[0], conv_w[0], lru_ba[0], lru_bx[0], lru_lam[0], pool_scale[0]], SUBLANES)
    small_all, wi0 = _all_gather([small, w_in[0].astype(MXU_DTYPE)], "gather_first")
    pieces = [_split_rows(small_all[k], small_shapes) for k in range(N_DEV)]
    c_all = jnp.stack([p[0] for p in pieces])
    conv_w_f = jnp.concatenate([p[1] for p in pieces], axis=-1)
    lru_ba_f = jnp.concatenate([p[2] for p in pieces], axis=-1)[:, None, :]
    lru_bx_f = jnp.concatenate([p[3] for p in pieces], axis=-1)[:, None, :]
    lru_lam_f = jnp.concatenate([p[4] for p in pieces], axis=-1)[:, None, :]
    pool_scale_f = jnp.concatenate([p[5] for p in pieces], axis=-1)[None, :]

    cond = jnp.concatenate([c_all, jnp.broadcast_to(c_ctx[None, :], (N_DEV, D_MODEL))], axis=0)
    b_my = lax.dynamic_slice(b_mod, (0, dev * n_mod), (2, n_mod))[:, None, :]
    mod_part = _mod_fwd(cond, w_mod, b_my, "mod_fwd")
    mod_all, = _all_gather([mod_part], "gather_mod")
    mod = jnp.transpose(mod_all, (1, 2, 0, 3)).reshape(2, 16, 3 * D_MODEL)
    mod_me = lax.dynamic_slice(mod, (0, dev, 0), (2, 1, 3 * D_MODEL))
    sh = [mod_me[i, :, 0:D_MODEL] for i in range(2)]
    sc = [mod_me[i, :, D_MODEL:2 * D_MODEL] for i in range(2)]
    gt = [mod_me[i, :, 2 * D_MODEL:] for i in range(2)]
    shc, scc = mod[0, 8:9, 0:D_MODEL], mod[0, 8:9, D_MODEL:2 * D_MODEL]

    lg = [ln_g[i][None, :] for i in range(2)]
    lb = [ln_b[i][None, :] for i in range(2)]
    lru_p = dict(conv_w=conv_w_f, conv_b=conv_b, wa=(0.5 * lru_wa[0]).astype(MXU_DTYPE),
                 wx=(0.5 * lru_wx[0]).astype(MXU_DTYPE), ba=0.5 * lru_ba_f, bx=0.5 * lru_bx_f, lam=lru_lam_f)
    zero_state = jnp.zeros((1, D_INNER), F32)

    (u0, g0), (wo0,) = _in_proj(xt, sc[0], sh[0], wi0, "in_proj0", sides=[("gather", [w_out[0].astype(MXU_DTYPE)])])
    (uc, _), _ = _in_proj(ctxt, scc, shc, wi0, "in_proj0_ctx")
    (hcf, cf, uvc), _ = _lru_fwd(uc, zero_state, lru_p, 0, "lru_fwd_ctx_f", conv=True)
    (hcb, cbk), _ = _lru_fwd(uvc, zero_state, lru_p, 1, "lru_fwd_ctx_b", conv=False)
    (hf, _, uv0), (wi1,) = _lru_fwd(u0, cf, lru_p, 0, "lru_fwd_f", conv=True,
                                    sides=[("gather", [w_in[1].astype(MXU_DTYPE)])])
    (hb, _), (wo1, pool_w_g) = _lru_fwd(
        uv0, cbk, lru_p, 1, "lru_fwd_b", conv=False,
        sides=[("gather", [w_out[1].astype(MXU_DTYPE), pool_w[0].astype(MXU_DTYPE)])])
    w_in_l = [wi0, wi1]
    w_out_l = [wo0.reshape(D_INNER, D_MODEL), wo1.reshape(D_INNER, D_MODEL)]
    pool_w_f = jnp.transpose(pool_w_g, (1, 0, 2, 3)).reshape(len(POOL_WINDOWS), POOL_GROUP, POOL_GROUP)
    x1, br0, u1, g1 = _out0(hf, hb, g0, xt, gt[0], w_out_l[0], lg[0], lb[0], sc[1], sh[1], w_in_l[1], "out0_in1")
    dmix = _pool_mix(u1, False, MXU_DTYPE, "pool_fwd")
    dz1, st1, po1 = _out1(dmix, pool_w_f, pool_scale_f, g1, x1, gt[1], w_out_l[1], lg[1], lb[1], tgt, "out1")
    loss_me = jnp.full((1, LANES), (0.5 / D_MODEL) * jnp.sum(st1[3]), F32)

    core = jnp.reshape(ci, (1,)).astype(jnp.int32)
    wo_view = lambda a: a.reshape(N_DEV, D_INNER // N_DEV, D_MODEL)
    pw_view = lambda a: _blocks_by_device(a, 1).reshape(N_DEV, POOL_GROUP // N_DEV * len(POOL_WINDOWS), POOL_GROUP)
    dd, dg1, gwo1, gpw, gps = _bout1(dz1, dmix, po1, g1, pool_w_f, pool_scale_f, gt[1], w_out_l[1], "bwd_out1")
    du1 = _pool_mix(dd, True, MXU_DTYPE, "pool_bwd")
    (dx1, gwi1, stb1), _ = _bin(du1, dg1, x1, dz1, sc[1], sh[1], w_in_l[1], "bwd_in1", dx_dtype=ACT_DTYPE)
    bufs1 = [gwi1, wo_view(gwo1), pw_view(gpw)]
    (dz0, dy0, dg0, gwo0, stl0), recv1 = _bout0(dx1, xt, br0, lg[0], hf, hb, g0, gt[0], w_out_l[0], "bwd_out0",
                                                sides=[("sibling", bufs1)])
    pairs1 = [_pair_sum(b, r, core, "reduce_pair_" + n)
              for b, r, n in zip(bufs1, recv1, ["w_in1", "w_out1", "pool_w"])]
    (duvf, gwa_f, gwx_f, gv_f, dh0f), (p_wi1, p_wo1, p_pw, recv_wo0) = _lru_bwd(
        uv0, dy0, hf, cf, zero_state, lru_p, 0, "lru_bwd_f", sides=[("chips", pairs1), ("sibling", [wo_view(gwo0)])])
    pair_wo0 = _pair_sum(wo_view(gwo0), recv_wo0, core, "reduce_pair_w_out0")
    (duvb, gwa_b, gwx_b, gv_b, dh0b), (p_wo0,) = _lru_bwd(
        uv0, dy0, hb, cbk, zero_state, lru_p, 1, "lru_bwd_b", sides=[("chips", [pair_wo0])])
    zero_dh = jnp.zeros(uc.shape, ACT_DTYPE)
    (ducf, gwa_cf, gwx_cf, gv_cf, _), _ = _lru_bwd(uvc, zero_dh, hcf, zero_state, dh0f, lru_p, 0, "lru_bwd_ctx_f")
    (ducb, gwa_cb, gwx_cb, gv_cb, _), _ = _lru_bwd(uvc, zero_dh, hcb, zero_state, dh0b, lru_p, 1, "lru_bwd_ctx_b")

    def pack(sharded, replicated):
        sh_sizes = [int(np.prod(a.shape[1:])) for a in sharded]
        rep_sizes = [a.shape[0] // N_DEV for a in replicated]
        n_flat = sum(sh_sizes) + sum(rep_sizes)
        rows = -(-(-(-n_flat // LANES)) // FLAT_ROWS) * FLAT_ROWS
        buf = jnp.concatenate([a.reshape(N_DEV, -1) for a in sharded + replicated], axis=1)
        return jnp.pad(buf, ((0, 0), (0, rows * LANES - n_flat))).reshape(N_DEV, rows, LANES), sh_sizes, rep_sizes

    def unpack(reduced, sh_sizes, rep_sizes, sh_shapes):
        flat = reduced.reshape(-1)
        offs = np.cumsum([0] + sh_sizes)
        mine = [flat[offs[k]:offs[k + 1]].reshape(s) for k, s in enumerate(sh_shapes)]
        return mine, _to_rows([flat[offs[-1]:offs[-1] + sum(rep_sizes)]], FLAT_ROWS)

    def spread(rep_all, rep_sizes, shapes):
        flat = rep_all.reshape(N_DEV, -1)
        offs = np.cumsum([0] + rep_sizes)
        return [flat[:, offs[k]:offs[k + 1]].reshape(s) for k, s in enumerate(shapes)]

    (du0, cst0), _ = _conv_bwd(duvf, duvb, u0, conv_w_f, "conv_bwd")
    (duc, cstc), _ = _conv_bwd(ducf, ducb, uc, conv_w_f, "conv_bwd_ctx")
    (gwic, stc), _ = _bin(duc, None, ctxt, None, scc, shc, w_in_l[0][:N_WBLK // 2], "bwd_in0_ctx")
    (gx, gwi0, stb0), _ = _bin(du0, dg0, xt, dz0, sc[0], sh[0], w_in_l[0], "bwd_in0", gw_init=gwic)

    zero_row = jnp.zeros((1, D_MODEL), F32)
    dm_me = jnp.stack([
        jnp.concatenate([jnp.concatenate([stb0[1:2], stb0[0:1], stl0[2:3]], axis=1),
                         jnp.concatenate([stc[1:2], stc[0:1], zero_row], axis=1)], axis=0),
        jnp.concatenate([jnp.concatenate([stb1[1:2], stb1[0:1], st1[2:3]], axis=1),
                         jnp.zeros((1, 3 * D_MODEL), F32)], axis=0)])
    dm_g, loss_g = _all_gather([dm_me, loss_me], "gather_dmod")
    loss = jnp.sum(loss_g[:, 0, 0])
    dm_all = jnp.concatenate([jnp.transpose(dm_g[:, :, 0], (1, 0, 2)), jnp.transpose(dm_g[:, :, 1], (1, 0, 2))],
                             axis=1)
    dm_my = lax.dynamic_slice(dm_all, (0, 0, dev * n_mod), (2, 16, n_mod))
    g_w_mod, g_b_mod, gcc_part = _mod_bwd(cond, dm_all, dm_my, w_mod, "mod_bwd")
    g_b_mod = g_b_mod.reshape(b_mod.shape)

    gwa = jnp.stack([gwa_f + gwa_cf, gwa_b + gwa_cb])
    gwx = jnp.stack([gwx_f + gwx_cf, gwx_b + gwx_cb])
    gv = jnp.stack([gv_f + gv_cf, gv_b + gv_cb])
    cst = cst0 + cstc
    misc, m_sh, m_rep = pack(
        [_blocks_by_device(cst[0:4], 1), _blocks_by_device(gv[:, 0], 1), _blocks_by_device(gv[:, 1], 1),
         _blocks_by_device(gv[:, 2], 1), _blocks_by_device(gps[0], 0)],
        [gwa.reshape(-1), gwx.reshape(-1), jnp.stack([stl0[0], st1[0]]).reshape(-1),
         jnp.stack([stl0[1], st1[1]]).reshape(-1), cst[4], gcc_part.reshape(-1)])
    bufs = [gwi0, misc]
    recvs = _sibling_exchange(bufs, "reduce_sibling")
    pairs = [_pair_sum(b, r, core, "reduce_pair_" + n) for b, r, n in zip(bufs, recvs, ["w_in0", "misc"])]
    p_wi0, p_misc = _chip_exchange(pairs, "reduce_chips")
    (g_conv_w, g_lru_ba, g_lru_bx, g_lru_lam, g_pool_scale), rep_mine = unpack(
        _sum4(p_misc, "reduce_sum_misc"), m_sh, m_rep,
        [conv_w.shape, lru_ba.shape, lru_bx.shape, lru_lam.shape, pool_scale.shape])
    rep_all, = _all_gather([rep_mine.astype(WIRE_DTYPE)], "gather_replicated")
    rep_all = rep_all.astype(F32)
    g_lru_wa, g_lru_wx, g_ln_g, g_ln_b, g_conv_b, g_c_ctx = spread(
        rep_all, m_rep, [lru_wa.shape, lru_wx.shape, ln_g.shape, ln_b.shape, conv_b.shape, c_ctx.shape])

    names = ["c_ctx", "w_mod", "b_mod", "w_in", "w_out", "ln_g", "ln_b", "conv_w", "conv_b", "lru_wa", "lru_ba",
             "lru_wx", "lru_bx", "lru_lam", "pool_w", "pool_scale"]
    weights = dict(c_ctx=c_ctx, w_mod=w_mod, b_mod=b_mod, w_in=w_in, w_out=w_out, ln_g=ln_g, ln_b=ln_b,
                   conv_w=conv_w, conv_b=conv_b, lru_wa=lru_wa, lru_ba=lru_ba, lru_wx=lru_wx, lru_bx=lru_bx,
                   lru_lam=lru_lam, pool_w=pool_w, pool_scale=pool_scale)
    mom_m = dict(c_ctx=m_c_ctx, w_mod=m_w_mod, b_mod=m_b_mod, w_in=m_w_in, w_out=m_w_out, ln_g=m_ln_g, ln_b=m_ln_b,
                 conv_w=m_conv_w, conv_b=m_conv_b, lru_wa=m_lru_wa, lru_ba=m_lru_ba, lru_wx=m_lru_wx,
                 lru_bx=m_lru_bx, lru_lam=m_lru_lam, pool_w=m_pool_w, pool_scale=m_pool_scale)
    mom_v = dict(c_ctx=v_c_ctx, w_mod=v_w_mod, b_mod=v_b_mod, w_in=v_w_in, w_out=v_w_out, ln_g=v_ln_g, ln_b=v_ln_b,
                 conv_w=v_conv_w, conv_b=v_conv_b, lru_wa=v_lru_wa, lru_ba=v_lru_ba, lru_wx=v_lru_wx,
                 lru_bx=v_lru_bx, lru_lam=v_lru_lam, pool_w=v_pool_w, pool_scale=v_pool_scale)
    grads = dict(c_ctx=g_c_ctx, w_mod=g_w_mod, b_mod=g_b_mod, ln_g=g_ln_g, ln_b=g_ln_b,
                 conv_w=g_conv_w, conv_b=g_conv_b, lru_wa=g_lru_wa, lru_ba=g_lru_ba, lru_wx=g_lru_wx,
                 lru_bx=g_lru_bx, lru_lam=g_lru_lam)
    grads["pool_scale"] = g_pool_scale
    delta, new_m, new_v = {}, {}, {}

    def update_parts(n, parts, view):
        res = _adamw_parts(weights[n].reshape(view), parts, mom_m[n].reshape(view), mom_v[n].reshape(view),
                           "adamw_" + n)
        grads[n], delta[n], new_m[n], new_v[n] = [r.reshape(weights[n].shape) for r in res]

    update_parts("w_in", [p_wi0, p_wi1], w_in.shape)
    update_parts("w_out", [p_wo0, p_wo1], w_out.shape)
    update_parts("pool_w", [p_pw], (1,) + p_pw.shape[1:])
    for n in ("w_mod", "lru_wa", "lru_wx"):
        shape = weights[n].shape
        view = (int(np.prod(shape[:-1])), shape[-1])
        res = _adamw(weights[n].reshape(view), grads[n].reshape(view), mom_m[n].reshape(view),
                     mom_v[n].reshape(view), "adamw_" + n)
        delta[n], new_m[n], new_v[n] = [r.reshape(shape) for r in res]

    small = [n for n in names if n not in delta]
    shapes = [weights[n].shape for n in small]
    flat = lambda d: _to_rows([d[n] for n in small], FLAT_ROWS)
    res = _adamw(flat(weights), flat(grads), flat(mom_m), flat(mom_v), "adamw_small")
    for d, r in zip((delta, new_m, new_v), res):
        d.update(zip(small, _split_rows(r, shapes)))

    return (loss, gx[None], *[grads[n] for n in names], *[delta[n] for n in names],
            *[new_m[n] for n in names], *[new_v[n] for n in names])
```

```python
import functools

import numpy as np
import jax
import jax.numpy as jnp
from jax import lax
from jax.experimental import pallas as pl
from jax.experimental.pallas import tpu as pltpu

F32 = jnp.float32
BF16 = jnp.bfloat16
MXU_DTYPE = BF16

D_MODEL = 1024
D_INNER = 2048
LRU_BLOCK = 128
GRID_W = 64
POOL_WINDOWS = (2, 4, 8, 16)
POOL_GROUP = 512
ALPHA = float(4 ** 0.25)
LN_EPS = 1e-5
LRU_C = 8.0
N_DEV = 8
N_WBLK = 8
WBLK = 512

ADAM_LR = 0.001
ADAM_B1 = 0.9
ADAM_B2 = 0.999
ADAM_EPS = 1e-08
ADAM_WD = 0.01
ADAM_STEP = 10

LANES = 128
SUBLANES = 8
V7X_VMEM_BYTES = 64 * 1024 * 1024
VMEM_COMPILER_RESERVE = 8 * 1024 * 1024
VMEM_LIMIT = V7X_VMEM_BYTES - VMEM_COMPILER_RESERVE
MESH = pl.DeviceIdType.MESH
ANY = pl.BlockSpec(memory_space=pl.ANY)

TM_MM = 512
TM_LRU = 1024
CB_LRU = 512
CB_LRU_FWD = 1024
N_SEG = 8
SCAN_UNROLL = 4
SCAN_ROW_T = 17
SCAN_ROW_J = 2
SQRT_FLOOR = 1e-30
FLAT_ROWS = 16
ELEMENTWISE_TILE_BYTES = 1 << 20
POOL_TOK = 256
WIRE_DTYPE = BF16
ACT_DTYPE = BF16
H_HALO = 16


def _cparams(**kw):
    return pltpu.CompilerParams(vmem_limit_bytes=VMEM_LIMIT, **kw)


def _my_pos():
    return lax.axis_index("x"), lax.axis_index("y"), lax.axis_index("c")


def _dot(a, b):
    return jnp.dot(a.astype(MXU_DTYPE), b.astype(MXU_DTYPE), preferred_element_type=F32)


def _dot_tn(a, b):
    return lax.dot_general(a.astype(MXU_DTYPE), b.astype(MXU_DTYPE), (((0,), (0,)), ((), ())),
                           preferred_element_type=F32)


def _dot_nt(a, b):
    return lax.dot_general(a.astype(MXU_DTYPE), b.astype(MXU_DTYPE), (((1,), (1,)), ((), ())),
                           preferred_element_type=F32)


def _sigmoid(z):
    return 0.5 * jnp.tanh(0.5 * z) + 0.5


def _log_sigmoid(x):
    y = jnp.exp(-jnp.abs(x))
    u = 1.0 + y
    l1p = jnp.where(u == 1.0, y, jnp.log(u) * (y / jnp.where(u == 1.0, 1.0, u - 1.0)))
    return jnp.minimum(x, 0.0) - l1p


def _rowsum(v):
    return jnp.sum(v, axis=0, keepdims=True)


def _layer_norm_stats(z):
    mu = jnp.mean(z, axis=-1, keepdims=True)
    zc = z - mu
    var = jnp.mean(zc * zc, axis=-1, keepdims=True)
    rstd = lax.rsqrt(var + LN_EPS)
    return zc * rstd, rstd


def _layer_norm_bwd(dy, xhat, rstd, g):
    dxh = dy * g
    m1 = jnp.mean(dxh, axis=-1, keepdims=True)
    m2 = jnp.mean(dxh * xhat, axis=-1, keepdims=True)
    return rstd * (dxh - m1 - xhat * m2)


def _shifted(v, before8, after8, offsets):
    n = v.shape[0]
    ext = jnp.concatenate([before8, v, after8], axis=0)
    total = n + 2 * SUBLANES
    return [pltpu.roll(ext, (-k) % total, 0)[SUBLANES:SUBLANES + n] for k in offsets]


def _rows8(row):
    return jnp.broadcast_to(row, (SUBLANES, row.shape[1]))


def _shift_down(v, first_row):
    return _shifted(v, _rows8(first_row), _rows8(first_row), [-1])[0]


def _shift_up(v, last_row):
    return _shifted(v, _rows8(last_row), _rows8(last_row), [1])[0]


def _all_gather(blocks, name):
    n = len(blocks)

    def body(*refs):
        x_refs, out_refs = refs[:n], refs[n:2 * n]
        send_sems, recv_sems, local_sems = refs[2 * n:]
        x, y, c = _my_pos()
        me, sibling = (x, y, c), (x, y, 1 - c)
        chips = [(1 - x, y), (x, 1 - y), (1 - x, 1 - y)]

        def slot(a, px, py, pc):
            return out_refs[a].at[4 * px + 2 * py + pc]

        def copy(a, k, block, to, src=None):
            return pltpu.make_async_remote_copy(
                src_ref=slot(a, *block) if src is None else src, dst_ref=slot(a, *block),
                send_sem=send_sems.at[a, k], recv_sem=recv_sems.at[a, k], device_id=to, device_id_type=MESH)

        mine = [pltpu.make_async_copy(x_refs[a], slot(a, *me), local_sems.at[a]) for a in range(n)]
        for cp in mine:
            cp.start()
        first = []
        for a in range(n):
            first.append(copy(a, 0, me, sibling, src=x_refs[a]))
            first += [copy(a, 1 + j, me, (*chip, c), src=x_refs[a]) for j, chip in enumerate(chips)]
        for cp in first:
            cp.start()
        passed = []
        for j, chip in enumerate(chips):
            for a in range(n):
                copy(a, 1 + j, (*chip, c), me).wait_recv()
                fwd = copy(a, 4 + j, (*chip, c), sibling)
                fwd.start()
                passed.append(fwd)
        for a in range(n):
            copy(a, 0, sibling, me).wait_recv()
            for j, chip in enumerate(chips):
                copy(a, 4 + j, (*chip, 1 - c), me).wait_recv()
        for cp in first + passed:
            cp.wait_send()
        for cp in mine:
            cp.wait()

    outs = pl.pallas_call(
        body, name=name,
        out_shape=[jax.ShapeDtypeStruct((N_DEV,) + b.shape, b.dtype) for b in blocks],
        in_specs=[ANY] * n, out_specs=[ANY] * n,
        scratch_shapes=[pltpu.SemaphoreType.DMA((n, 7)), pltpu.SemaphoreType.DMA((n, 7)),
                        pltpu.SemaphoreType.DMA((n,))],
    )(*blocks)
    return list(outs)


def _sibling_exchange(bufs, name):
    n = len(bufs)

    def body(*refs):
        srcs, outs = refs[:n], refs[n:2 * n]
        send_sems, recv_sems = refs[2 * n:]
        x, y, c = _my_pos()
        copies = [pltpu.make_async_remote_copy(
            src_ref=srcs[a].at[2 * j + (1 - c)], dst_ref=outs[a].at[j], send_sem=send_sems.at[a, j],
            recv_sem=recv_sems.at[a, j], device_id=(x, y, 1 - c), device_id_type=MESH)
            for a in range(n) for j in range(4)]
        for cp in copies:
            cp.start()
        for cp in copies:
            cp.wait()

    outs = pl.pallas_call(
        body, name=name, out_shape=[jax.ShapeDtypeStruct((4,) + b.shape[1:], b.dtype) for b in bufs],
        in_specs=[ANY] * n, out_specs=[ANY] * n,
        scratch_shapes=[pltpu.SemaphoreType.DMA((n, 4)), pltpu.SemaphoreType.DMA((n, 4))],
    )(*bufs)
    return list(outs)


def _chip_exchange(parts, name):
    n = len(parts)

    def body(*refs):
        srcs, outs = refs[:n], refs[n:2 * n]
        send_sems, recv_sems, local_sems = refs[2 * n:]
        x, y, c = _my_pos()
        jme = 2 * x + y
        peers = [(1 - x, y), (x, 1 - y), (1 - x, 1 - y)]
        local = [pltpu.make_async_copy(srcs[a].at[jme], outs[a].at[jme], local_sems.at[a]) for a in range(n)]
        for cp in local:
            cp.start()

        def copy(a, k, px, py, dst_slot):
            return pltpu.make_async_remote_copy(
                src_ref=srcs[a].at[2 * px + py], dst_ref=outs[a].at[dst_slot], send_sem=send_sems.at[a, k],
                recv_sem=recv_sems.at[a, k], device_id=(px, py, c), device_id_type=MESH)

        sends = [copy(a, k, px, py, jme) for a in range(n) for k, (px, py) in enumerate(peers)]
        for cp in sends:
            cp.start()
        for a in range(n):
            for k, (px, py) in enumerate(peers):
                copy(a, k, px, py, 2 * px + py).wait_recv()
        for cp in sends:
            cp.wait_send()
        for cp in local:
            cp.wait()

    outs = pl.pallas_call(
        body, name=name, out_shape=[jax.ShapeDtypeStruct(p.shape, p.dtype) for p in parts],
        in_specs=[ANY] * n, out_specs=[ANY] * n,
        scratch_shapes=[pltpu.SemaphoreType.DMA((n, 3)), pltpu.SemaphoreType.DMA((n, 3)),
                        pltpu.SemaphoreType.DMA((n,))],
    )(*parts)
    return list(outs)


_SIDE_REMOTE = {"gather": 7, "sibling": 4, "chips": 3}
_FLIPS = [(0, 0, 1), (1, 0, 0), (0, 1, 0), (1, 1, 0), (1, 0, 1), (0, 1, 1), (1, 1, 1)]


def _side_plan(sides):
    inputs, out_shapes, scratch = [], [], []
    for kind, arrays in sides:
        n = len(arrays)
        for a in arrays:
            inputs.append(a)
            shape = {"gather": (N_DEV,) + a.shape, "sibling": (4,) + a.shape[1:], "chips": a.shape}[kind]
            out_shapes.append(jax.ShapeDtypeStruct(shape, a.dtype))
        scratch += [pltpu.SemaphoreType.DMA((n, _SIDE_REMOTE[kind])), pltpu.SemaphoreType.DMA((n, _SIDE_REMOTE[kind])),
                    pltpu.SemaphoreType.DMA((n,))]
    return inputs, out_shapes, scratch


def _side_copies(sides, in_refs, out_refs, sem_refs):
    x, y, c = _my_pos()
    starts, waits = [], []
    pos = 0
    for s, (kind, arrays) in enumerate(sides):
        send_sems, recv_sems, local_sems = sem_refs[3 * s:3 * s + 3]
        for a in range(len(arrays)):
            src, out = in_refs[pos], out_refs[pos]
            pos += 1

            def remote(k, src_ref, dst_ref, to):
                return pltpu.make_async_remote_copy(src_ref=src_ref, dst_ref=dst_ref, send_sem=send_sems.at[a, k],
                                                    recv_sem=recv_sems.at[a, k], device_id=to, device_id_type=MESH)

            def local(src_ref, dst_ref):
                cp = pltpu.make_async_copy(src_ref, dst_ref, local_sems.at[a])
                starts.append(cp.start)
                waits.append(cp.wait)

            if kind == "gather":
                me = 4 * x + 2 * y + c
                local(src, out.at[me])
                for k, (fx, fy, fc) in enumerate(_FLIPS):
                    px, py, pc = (1 - x if fx else x), (1 - y if fy else y), (1 - c if fc else c)
                    send = remote(k, src, out.at[me], (px, py, pc))
                    starts.append(send.start)
                    waits += [remote(k, src, out.at[4 * px + 2 * py + pc], (px, py, pc)).wait_recv, send.wait_send]
            elif kind == "sibling":
                for j in range(4):
                    cp = remote(j, src.at[2 * j + (1 - c)], out.at[j], (x, y, 1 - c))
                    starts.append(cp.start)
                    waits.append(cp.wait)
            else:
                jme = 2 * x + y
                local(src.at[jme], out.at[jme])
                for k, (px, py) in enumerate([(1 - x, y), (x, 1 - y), (1 - x, 1 - y)]):
                    send = remote(k, src.at[2 * px + py], out.at[jme], (px, py, c))
                    starts.append(send.start)
                    waits += [remote(k, src.at[2 * px + py], out.at[2 * px + py], (px, py, c)).wait_recv,
                              send.wait_send]
    return starts, waits


def _call_with_sides(body, sides, *, name, grid, in_specs, out_specs, out_shape, scratch_shapes, compiler_params, args):
    if not sides:
        res = pl.pallas_call(body, name=name, grid=grid, in_specs=in_specs, out_specs=out_specs, out_shape=out_shape,
                             scratch_shapes=scratch_shapes, compiler_params=compiler_params)(*args)
        return list(res), []
    s_in, s_out, s_scr = _side_plan(sides)
    n_in, n_out, n_scr, n_side = len(in_specs), len(out_specs), len(scratch_shapes), len(s_in)

    def wrapped(*refs):
        refs = list(refs)
        ins, side_in = refs[:n_in], refs[n_in:n_in + n_side]
        outs = refs[n_in + n_side:n_in + n_side + n_out]
        side_out = refs[n_in + n_side + n_out:n_in + 2 * n_side + n_out]
        rest = refs[n_in + 2 * n_side + n_out:]
        starts, waits = _side_copies(sides, side_in, side_out, rest[n_scr:])
        first = functools.reduce(jnp.logical_and, [pl.program_id(d) == 0 for d in range(len(grid))])
        last = functools.reduce(jnp.logical_and, [pl.program_id(d) == grid[d] - 1 for d in range(len(grid))])

        @pl.when(first)
        def _():
            for start in starts:
                start()

        body(*ins, *outs, *rest[:n_scr])

        @pl.when(last)
        def _():
            for wait in waits:
                wait()

    res = pl.pallas_call(
        wrapped, name=name, grid=grid, in_specs=list(in_specs) + [ANY] * n_side,
        out_specs=list(out_specs) + [ANY] * n_side, out_shape=list(out_shape) + s_out,
        scratch_shapes=list(scratch_shapes) + s_scr, compiler_params=compiler_params,
    )(*args, *s_in)
    return list(res[:n_out]), list(res[n_out:])


def _row_tile(r, l):
    t = min(r, max(16, ELEMENTWISE_TILE_BYTES // (4 * l) // 16 * 16))
    while r % t:
        t -= 16
    return t


def _pair_sum(buf, recv, core, name):
    _, r, l = buf.shape
    tr = _row_tile(r, l)

    def body(core_ref, a_ref, b_ref, o_ref):
        o_ref[...] = (a_ref[...] + b_ref[...]).astype(WIRE_DTYPE)

    return pl.pallas_call(
        body, name=name, out_shape=jax.ShapeDtypeStruct((4, r, l), WIRE_DTYPE),
        grid_spec=pltpu.PrefetchScalarGridSpec(
            num_scalar_prefetch=1, grid=(4, r // tr),
            in_specs=[pl.BlockSpec((None, tr, l), lambda j, i, cr: (2 * j + cr[0], i, 0)),
                      pl.BlockSpec((None, tr, l), lambda j, i, cr: (j, i, 0))],
            out_specs=pl.BlockSpec((None, tr, l), lambda j, i, cr: (j, i, 0))),
        compiler_params=_cparams(dimension_semantics=("arbitrary", "arbitrary")),
    )(core, buf, recv)


def _sum_parts(p_ref):
    return ((p_ref[0].astype(F32) + p_ref[1].astype(F32)) + (p_ref[2].astype(F32) + p_ref[3].astype(F32)))


def _sum4(parts, name):
    _, r, l = parts.shape
    tr = _row_tile(r, l)

    def body(p_ref, o_ref):
        o_ref[...] = _sum_parts(p_ref)

    return pl.pallas_call(
        body, name=name, out_shape=jax.ShapeDtypeStruct((r, l), F32), grid=(r // tr,),
        in_specs=[pl.BlockSpec((4, tr, l), lambda i: (0, i, 0))],
        out_specs=pl.BlockSpec((tr, l), lambda i: (i, 0)),
        compiler_params=_cparams(dimension_semantics=("arbitrary",)),
    )(parts)


def _adamw_update(w, gg, m, v):
    nm = ADAM_B1 * m + (1.0 - ADAM_B1) * gg
    nv = ADAM_B2 * v + (1.0 - ADAM_B2) * (gg * gg)
    m_hat = nm / (1.0 - ADAM_B1 ** ADAM_STEP)
    v_hat = nv / (1.0 - ADAM_B2 ** ADAM_STEP)
    return -ADAM_LR * (m_hat / (jnp.sqrt(v_hat) + ADAM_EPS) + ADAM_WD * w), nm, nv


def _adamw(w, g, m, v, name):
    r, l = w.shape
    tr = _row_tile(r, l)

    def body(w_ref, g_ref, m_ref, v_ref, d_ref, nm_ref, nv_ref):
        d_ref[...], nm_ref[...], nv_ref[...] = _adamw_update(w_ref[...], g_ref[...], m_ref[...], v_ref[...])

    spec = pl.BlockSpec((tr, l), lambda i: (i, 0))
    return pl.pallas_call(
        body, name=name, out_shape=[jax.ShapeDtypeStruct((r, l), F32)] * 3, grid=(r // tr,),
        in_specs=[spec] * 4, out_specs=[spec] * 3,
        compiler_params=_cparams(dimension_semantics=("arbitrary",)),
    )(w, g, m, v)


def _adamw_parts(w, parts, m, v, name):
    nl, r, l = w.shape
    tr = _row_tile(r, l)

    def body(*refs):
        w_ref, p_refs, (m_ref, v_ref, g_ref, d_ref, nm_ref, nv_ref) = refs[0], refs[1:1 + nl], refs[1 + nl:]
        layer = pl.program_id(0)
        gg = _sum_parts(p_refs[0])
        for q in range(1, nl):
            gg = jnp.where(layer == q, _sum_parts(p_refs[q]), gg)
        g_ref[...] = gg
        d_ref[...], nm_ref[...], nv_ref[...] = _adamw_update(w_ref[...], gg, m_ref[...], v_ref[...])

    spec = pl.BlockSpec((None, tr, l), lambda q, i: (q, i, 0))
    pspecs = [pl.BlockSpec((4, tr, l), lambda q, i, k=k: (0, jnp.where(q == k, i, 0), 0)) for k in range(nl)]
    return pl.pallas_call(
        body, name=name, out_shape=[jax.ShapeDtypeStruct((nl, r, l), F32)] * 4, grid=(nl, r // tr),
        in_specs=[spec] + pspecs + [spec, spec], out_specs=[spec] * 4,
        compiler_params=_cparams(dimension_semantics=("arbitrary", "arbitrary")),
    )(w, *parts, m, v)


def _to_rows(pieces, row_multiple):
    flat = jnp.concatenate([p.reshape(-1) for p in pieces])
    rows = -(-flat.shape[0] // LANES)
    rows = -(-rows // row_multiple) * row_multiple
    flat = jnp.pad(flat, (0, rows * LANES - flat.shape[0]))
    return flat.reshape(rows, LANES)


def _split_rows(rows, shapes):
    flat = rows.reshape(-1)
    out, off = [], 0
    for s in shapes:
        n = int(np.prod(s))
        out.append(flat[off:off + n].reshape(s))
        off += n
    return out


def _mod_fwd(cond, w_mod, b_my, name):
    nl, _, ncol = w_mod.shape

    def body(a_ref, w_ref, b_ref, o_ref):
        a = a_ref[...]
        s = a * _sigmoid(a)
        for i in range(nl):
            o_ref[i] = _dot(s, w_ref[i]) + b_ref[i]

    return pl.pallas_call(
        body, name=name, out_shape=jax.ShapeDtypeStruct((nl, 16, ncol), F32),
        compiler_params=_cparams(),
    )(cond, w_mod, b_my)


def _mod_bwd(cond, dm_all, dm_my, w_mod, name):
    nl, _, ncol = w_mod.shape

    def body(a_ref, dma_ref, dmm_ref, w_ref, gw_ref, gb_ref, gc_ref):
        a = a_ref[...]
        sg = _sigmoid(a)
        s = a * sg
        for i in range(nl):
            gw_ref[i] = _dot_tn(s, dmm_ref[i])
            gb_ref[i] = jnp.sum(dma_ref[i], axis=0, keepdims=True)
        back = _dot_nt(dmm_ref[0], w_ref[0])
        dsilu = sg * (1.0 + a * (1.0 - sg))
        gc_ref[...] = jnp.sum(back[8:16] * dsilu[8:16], axis=0, keepdims=True)

    return pl.pallas_call(
        body, name=name,
        out_shape=[jax.ShapeDtypeStruct((nl, D_MODEL, ncol), F32), jax.ShapeDtypeStruct((nl, 1, 3 * D_MODEL), F32),
                   jax.ShapeDtypeStruct((1, D_MODEL), F32)],
        compiler_params=_cparams(),
    )(cond, dm_all, dm_my, w_mod)


def _in_proj(xt, sc, sh, wg, name, sides=()):
    t = xt.shape[0]
    tm = min(TM_MM, t)

    def body(x_ref, sc_ref, sh_ref, w_ref, u_ref, g_ref):
        h = (x_ref[...] * (1.0 + sc_ref[...]) + sh_ref[...]).astype(MXU_DTYPE)
        for k in range(N_WBLK):
            o = jnp.dot(h, w_ref[k], preferred_element_type=F32)
            if k < N_WBLK // 2:
                u_ref[:, k * WBLK:(k + 1) * WBLK] = o
            else:
                kk = k - N_WBLK // 2
                g_ref[:, kk * WBLK:(kk + 1) * WBLK] = o.astype(ACT_DTYPE)

    row = pl.BlockSpec((1, D_MODEL), lambda i: (0, 0))
    return _call_with_sides(
        body, sides, name=name,
        out_shape=[jax.ShapeDtypeStruct((t, D_INNER), F32), jax.ShapeDtypeStruct((t, D_INNER), ACT_DTYPE)],
        grid=(t // tm,),
        in_specs=[pl.BlockSpec((tm, D_MODEL), lambda i: (i, 0)), row, row,
                  pl.BlockSpec((N_WBLK, D_MODEL, WBLK), lambda i: (0, 0, 0), pipeline_mode=pl.Buffered(1))],
        out_specs=[pl.BlockSpec((tm, D_INNER), lambda i: (i, 0))] * 2, scratch_shapes=[],
        compiler_params=_cparams(dimension_semantics=("arbitrary",)), args=[xt, sc, sh, wg])


def _halo_maps(nt, tm, n_blocks, pos, rows=SUBLANES):
    per = tm // rows
    prev = lambda cb, i: (jnp.maximum(pos(i) * per - 1, 0), cb)
    nxt = lambda cb, i: (jnp.minimum((pos(i) + 1) * per, n_blocks - 1), cb)
    return prev, nxt


def _conv_taps(u, prev8, next8, is_first, is_last):
    pz = jnp.where(is_first, 0.0, 1.0)
    nz = jnp.where(is_last, 0.0, 1.0)
    return _shifted(u, prev8 * pz, next8 * nz, [-2, -1, 1])


def _lru_gates(uv, wa_ref, wx_ref, ba, bx, cl, g):
    sl = slice(g * LANES, (g + 1) * LANES)
    uvg = uv[:, sl]
    r = 0.5 * jnp.tanh(_dot(uvg, wa_ref[g]) + ba[:, sl]) + 0.5
    ii = 0.5 * jnp.tanh(_dot(uvg, wx_ref[g]) + bx[:, sl]) + 0.5
    la = cl[:, sl] * r
    a = jnp.exp(la)
    q = jnp.tanh(-la) * (1.0 + a * a)
    rs = lax.rsqrt(jnp.maximum(q, SQRT_FLOOR))
    return uvg, r, ii, a, q * rs, rs


def _scan_rows(seg):
    return -(-(SCAN_ROW_T * (seg - 1) + SCAN_ROW_J * (N_SEG - 1) + 1) // SUBLANES) * SUBLANES


def _seg_chunk(j, c):
    return pl.ds(SCAN_ROW_T * SUBLANES * c + SCAN_ROW_J * j, SUBLANES, stride=SCAN_ROW_T)


def _seg_scatter(ref, g, seg, value):
    for j in range(N_SEG):
        for c in range(seg // SUBLANES):
            r0 = j * seg + SUBLANES * c
            ref[g, _seg_chunk(j, c), :] = value[r0:r0 + SUBLANES]


def _scan_tile(a_s, b_s, carry_ref, write_out, seg, reverse, chunks_per_write=1):
    n_g = a_s.shape[0]
    unroll = SCAN_UNROLL if seg % SCAN_UNROLL == 0 else 1

    n_trips = seg // unroll

    def steps(k, state):
        hs, cs = list(state[0]), list(state[1])
        base = ((n_trips - 1 - k) if reverse else k) * unroll
        for q in (range(unroll - 1, -1, -1) if reverse else range(unroll)):
            t = base + q
            rows = pl.ds(t * SCAN_ROW_T, N_SEG, stride=SCAN_ROW_J)
            for g in range(n_g):
                a = a_s[g, rows, :]
                b = b_s[g, rows, :]
                hs[g] = a * hs[g] + b
                cs[g] = a * cs[g]
                b_s[g, rows, :] = hs[g]
                a_s[g, rows, :] = cs[g]
        return tuple(hs), tuple(cs)

    zeros = tuple(jnp.zeros((N_SEG, LANES), F32) for _ in range(n_g))
    ones = tuple(jnp.ones((N_SEG, LANES), F32) for _ in range(n_g))
    h_fin, a_fin = lax.fori_loop(0, seg // unroll, steps, (zeros, ones))

    order = list(range(N_SEG - 1, -1, -1)) if reverse else list(range(N_SEG))
    for g in range(n_g):
        carry = carry_ref[:, g * LANES:(g + 1) * LANES]
        for j in order:
            for c0 in range(0, seg // SUBLANES, chunks_per_write):
                parts = [b_s[g, _seg_chunk(j, c), :] + a_s[g, _seg_chunk(j, c), :] * carry
                         for c in range(c0, c0 + chunks_per_write)]
                write_out(j, c0, g, parts[0] if chunks_per_write == 1 else jnp.concatenate(parts, axis=0))
            carry = a_fin[g][j:j + 1] * carry + h_fin[g][j:j + 1]
        carry_ref[:, g * LANES:(g + 1) * LANES] = carry


def _lru_specs(s, tm, cb, direction_pos, nt):
    n_rows8 = s // SUBLANES
    prev, nxt = _halo_maps(nt, tm, n_rows8, direction_pos)
    tile = pl.BlockSpec((tm, cb), lambda c, i: (direction_pos(i), c))
    return tile, pl.BlockSpec((SUBLANES, cb), prev), pl.BlockSpec((SUBLANES, cb), nxt)


def _lru_param_specs(cb, d):
    n_g = cb // LANES
    vec = pl.BlockSpec((1, cb), lambda c, i: (0, c))
    dvec = pl.BlockSpec((None, 1, cb), lambda c, i: (d, 0, c))
    wmat = pl.BlockSpec((None, n_g, LRU_BLOCK, LRU_BLOCK), lambda c, i: (d, c, 0, 0))
    return vec, dvec, wmat


def _lru_fwd(src, h0, p, d, name, conv, sides=()):
    s = src.shape[0]
    tm = min(TM_LRU, s)
    cb = CB_LRU_FWD
    n_g = cb // LANES
    nt = s // tm
    seg = tm // N_SEG
    pos = (lambda i: i) if d == 0 else (lambda i: nt - 1 - i)

    def body(*refs):
        refs = list(refs)
        u_ref = refs.pop(0)
        if conv:
            up_ref, un_ref, cw_ref, cbias_ref = [refs.pop(0) for _ in range(4)]
        wa_ref, wx_ref, ba_ref, bx_ref, lam_ref, h0_ref, h_ref, hc_ref = [refs.pop(0) for _ in range(8)]
        uv_ref = refs.pop(0) if conv else None
        a_s, b_s = refs
        i = pl.program_id(1)
        tp = pos(i)

        @pl.when(i == 0)
        def _():
            hc_ref[...] = h0_ref[...]

        if conv:
            u_t = u_ref[...]
            um2, um1, up1 = _conv_taps(u_t, up_ref[...], un_ref[...], tp == 0, tp == nt - 1)
            cw = cw_ref[...]
            uv_ref[...] = um2 * cw[0:1] + um1 * cw[1:2] + u_t * cw[2:3] + up1 * cw[3:4] + cbias_ref[...]
        src_ref = uv_ref if conv else u_ref
        cl = LRU_C * _log_sigmoid(lam_ref[...])
        ba, bx = ba_ref[...], bx_ref[...]
        for g in range(n_g):
            uvg, r, ii, a, sq, _ = _lru_gates(src_ref, wa_ref, wx_ref, ba, bx, cl, g)
            b = sq * (ii * uvg)
            _seg_scatter(a_s, g, seg, a)
            _seg_scatter(b_s, g, seg, b)

        per_write = 2 if (seg // SUBLANES) % 2 == 0 else 1

        def write_out(j, c, g, h):
            h_ref[pl.ds(j * seg + SUBLANES * c, SUBLANES * per_write), pl.ds(g * LANES, LANES)] = h.astype(ACT_DTYPE)

        _scan_tile(a_s, b_s, hc_ref, write_out, seg, reverse=(d == 1), chunks_per_write=per_write)

    tile, prev, nxt = _lru_specs(s, tm, cb, pos, nt)
    vec, dvec, wmat = _lru_param_specs(cb, d)
    wide = jax.ShapeDtypeStruct((s, D_INNER), F32)
    conv_specs = [prev, nxt, pl.BlockSpec((4, cb), lambda c, i: (0, c)), vec] if conv else []
    conv_args = [src, src, p["conv_w"], p["conv_b"]] if conv else []
    return _call_with_sides(
        body, sides, name=name,
        out_shape=[jax.ShapeDtypeStruct((s, D_INNER), ACT_DTYPE), jax.ShapeDtypeStruct((1, D_INNER), F32)]
        + ([wide] if conv else []),
        grid=(D_INNER // cb, nt),
        in_specs=[tile] + conv_specs + [wmat, wmat, dvec, dvec, dvec, vec],
        out_specs=[tile, vec] + ([tile] if conv else []),
        scratch_shapes=[pltpu.VMEM((n_g, _scan_rows(seg), LANES), F32)] * 2,
        compiler_params=_cparams(dimension_semantics=("arbitrary", "arbitrary")),
        args=[src, *conv_args, p["wa"], p["wx"], p["ba"], p["bx"], p["lam"], h0])


def _lru_bwd(uv, dh, h, h0, lam_in, p, d, name, sides=()):
    s = uv.shape[0]
    tm = min(TM_LRU, s)
    cb = CB_LRU
    n_g = cb // LANES
    nt = s // tm
    seg = tm // N_SEG
    pos = (lambda i: nt - 1 - i) if d == 0 else (lambda i: i)

    def body(uv_ref, dh_ref, h_ref, hh_ref, wa_ref, wx_ref, ba_ref, bx_ref,
             lam_ref, h0_ref, lin_ref, duv_ref, gwa_ref, gwx_ref, gv_ref, lc_ref, a_s, b_s, lp_s,
             r_s, i_s, q_s, rq_s, a_keep):
        i = pl.program_id(1)
        tp = pos(i)

        @pl.when(i == 0)
        def _():
            lc_ref[...] = lin_ref[...]
            gwa_ref[...] = jnp.zeros_like(gwa_ref)
            gwx_ref[...] = jnp.zeros_like(gwx_ref)
            gv_ref[...] = jnp.zeros_like(gv_ref)

        uv = uv_ref[...]
        lam = lam_ref[...]
        cl = LRU_C * _log_sigmoid(lam)
        ba, bx = ba_ref[...], bx_ref[...]
        dh_t = dh_ref[...].astype(F32)
        carry_in = lc_ref[...]
        for g in range(n_g):
            sl = slice(g * LANES, (g + 1) * LANES)
            _, r, ii, a, sq, rs = _lru_gates(uv, wa_ref, wx_ref, ba, bx, cl, g)
            for ref, val in ((r_s, r), (i_s, ii), (q_s, sq), (rq_s, rs), (a_keep, a)):
                ref[:, sl] = val.astype(ACT_DTYPE)
            b = a * dh_t[:, sl]
            _seg_scatter(a_s, g, seg, a)
            _seg_scatter(b_s, g, seg, b)

        def write_out(j, c, g, v):
            lp_s[pl.ds(j * seg + SUBLANES * c, SUBLANES), pl.ds(g * LANES, LANES)] = v

        _scan_tile(a_s, b_s, lc_ref, write_out, seg, reverse=(d == 0))

        h_t = h_ref[...].astype(F32)
        hh = hh_ref[...].astype(F32)
        if d == 0:
            edge = jnp.where(tp == 0, h0_ref[...], hh[H_HALO - 1:H_HALO])
            h_prev = _shift_down(h_t, edge)
            lam_t = dh_t + _shift_up(lp_s[...], carry_in)
        else:
            edge = jnp.where(tp == nt - 1, h0_ref[...], hh[0:1])
            h_prev = _shift_up(h_t, edge)
            lam_t = dh_t + _shift_down(lp_s[...], carry_in)

        dsig = LRU_C * _sigmoid(-lam)
        cl2 = cl + cl
        for g in range(n_g):
            sl = slice(g * LANES, (g + 1) * LANES)
            uvg = uv[:, sl]
            r, ii, a, sq, rs =[ref[:, sl].astype(F32) for ref in (r_s, i_s, a_keep, q_s, rq_s)]
            lt = lam_t[:, sl]
            ls = lt * sq
            dla = (lt * a) * (h_prev[:, sl] - (ii * uvg) * (a * rs))
            dzr = (dla * cl2[:, sl]) * r * (1.0 - r)
            dzi = ((ls + ls) * uvg) * ii * (1.0 - ii)
            duv_ref[:, sl] = (ls * ii + _dot_nt(dzr, wa_ref[g]) + _dot_nt(dzi, wx_ref[g])).astype(ACT_DTYPE)
            gwa_ref[g] += _dot_tn(uvg, dzr)
            gwx_ref[g] += _dot_tn(uvg, dzi)
            gv_ref[0:1, sl] += _rowsum(dzr)
            gv_ref[1:2, sl] += _rowsum(dzi)
            gv_ref[2:3, sl] += _rowsum(dla * r) * dsig[:, sl]

        @pl.when(i == nt - 1)
        def _():
            gwa_ref[...] = 0.5 * gwa_ref[...]
            gwx_ref[...] = 0.5 * gwx_ref[...]
            gv_ref[0:2, :] = 0.5 * gv_ref[0:2, :]

    tile, _, _ = _lru_specs(s, tm, cb, pos, nt)
    vec, dvec, wmat = _lru_param_specs(cb, d)
    h_prev_map, h_next_map = _halo_maps(nt, tm, s // H_HALO, pos, rows=H_HALO)
    hh_spec = pl.BlockSpec((H_HALO, cb), h_prev_map if d == 0 else h_next_map)
    gw_spec = pl.BlockSpec((n_g, LRU_BLOCK, LRU_BLOCK), lambda c, i: (c, 0, 0))
    n_blk = D_INNER // LRU_BLOCK
    return _call_with_sides(
        body, sides, name=name,
        out_shape=[jax.ShapeDtypeStruct((s, D_INNER), ACT_DTYPE),
                   jax.ShapeDtypeStruct((n_blk, LRU_BLOCK, LRU_BLOCK), F32),
                   jax.ShapeDtypeStruct((n_blk, LRU_BLOCK, LRU_BLOCK), F32),
                   jax.ShapeDtypeStruct((SUBLANES, D_INNER), F32),
                   jax.ShapeDtypeStruct((1, D_INNER), F32)],
        grid=(D_INNER // cb, nt),
        in_specs=[tile, tile, tile, hh_spec, wmat, wmat, dvec, dvec, dvec, vec, vec],
        out_specs=[tile, gw_spec, gw_spec, pl.BlockSpec((SUBLANES, cb), lambda c, i: (0, c)), vec],
        scratch_shapes=[pltpu.VMEM((n_g, _scan_rows(seg), LANES), F32)] * 2 + [pltpu.VMEM((tm, cb), F32)]
        + [pltpu.VMEM((tm, cb), ACT_DTYPE)] * 5,
        compiler_params=_cparams(dimension_semantics=("arbitrary", "arbitrary")),
        args=[uv, dh, h, h, p["wa"], p["wx"], p["ba"], p["bx"], p["lam"], h0, lam_in])


def _out0(hf, hb, g, xt, gt, wo, lg, lb, sc1, sh1, wg1, name):
    t = xt.shape[0]
    tm = min(TM_MM, t)

    def body(hf_ref, hb_ref, g_ref, x_ref, gt_ref, w_ref, lg_ref, lb_ref, sc1_ref, sh1_ref, w1_ref,
             x1_ref, br_ref, u1_ref, g1_ref):
        br = None
        for k in range(D_INNER // WBLK):
            sl = slice(k * WBLK, (k + 1) * WBLK)
            gg = g_ref[:, sl].astype(F32)
            p = (hf_ref[:, sl].astype(F32) + hb_ref[:, sl].astype(F32)) * (gg * _sigmoid(gg))
            part = _dot(p, w_ref[sl, :])
            br = part if br is None else br + part
        z = ALPHA * x_ref[...] + gt_ref[...] * br
        xhat, _ = _layer_norm_stats(z)
        x1 = xhat * lg_ref[...] + lb_ref[...]
        x1_ref[...] = x1
        br_ref[...] = br.astype(ACT_DTYPE)
        h1 = (x1 * (1.0 + sc1_ref[...]) + sh1_ref[...]).astype(MXU_DTYPE)
        for k in range(N_WBLK):
            o = jnp.dot(h1, w1_ref[k], preferred_element_type=F32).astype(ACT_DTYPE)
            if k < N_WBLK // 2:
                u1_ref[:, k * WBLK:(k + 1) * WBLK] = o
            else:
                kk = k - N_WBLK // 2
                g1_ref[:, kk * WBLK:(kk + 1) * WBLK] = o

    wide = pl.BlockSpec((tm, D_INNER), lambda i: (i, 0))
    nar = pl.BlockSpec((tm, D_MODEL), lambda i: (i, 0))
    row = pl.BlockSpec((1, D_MODEL), lambda i: (0, 0))
    return pl.pallas_call(
        body, name=name,
        out_shape=[jax.ShapeDtypeStruct((t, D_MODEL), F32), jax.ShapeDtypeStruct((t, D_MODEL), ACT_DTYPE),
                   jax.ShapeDtypeStruct((t, D_INNER), ACT_DTYPE), jax.ShapeDtypeStruct((t, D_INNER), ACT_DTYPE)],
        grid=(t // tm,),
        in_specs=[wide, wide, wide, nar, row,
                  pl.BlockSpec((D_INNER, D_MODEL), lambda i: (0, 0), pipeline_mode=pl.Buffered(1)), row, row, row, row,
                  pl.BlockSpec((N_WBLK, D_MODEL, WBLK), lambda i: (0, 0, 0), pipeline_mode=pl.Buffered(1))],
        out_specs=[nar, nar, wide, wide],
        compiler_params=_cparams(dimension_semantics=("arbitrary",)),
    )(hf, hb, g, xt, gt, wo, lg, lb, sc1, sh1, wg1)


def _unrolled_loop(n, fn, unroll=4):
    while n % unroll:
        unroll //= 2

    def trip(k, carry):
        for q in range(unroll):
            fn(k * unroll + q)
        return carry
    lax.fori_loop(0, n // unroll, trip, 0)


def _window(n, w):
    t = np.arange(n)
    return np.clip(t - w // 2, 0, n), np.clip(t + w // 2, 0, n)


def _pool_tables(n_rows, transpose):
    boxes, inv_c, inv_r = [], [], []
    for w in POOL_WINDOWS:
        lo, hi = _window(GRID_W, w)
        m = np.zeros((GRID_W, GRID_W), np.float32)
        for r in range(GRID_W):
            m[r, lo[r]:hi[r]] = 1.0
        m = np.kron(np.eye(POOL_TOK // GRID_W, dtype=np.float32), m)
        boxes.append(m.T if transpose else m)
        inv_c.append(np.broadcast_to((1.0 / (hi - lo).astype(np.float32))[:, None], (GRID_W, LANES)))
        lo_r, hi_r = _window(n_rows, w)
        inv_r.append(1.0 / (hi_r - lo_r).astype(np.float32))
    return (jnp.asarray(np.stack(boxes), MXU_DTYPE), jnp.asarray(np.stack(inv_c), F32),
            jnp.asarray(np.stack(inv_r), F32))


def _pool_mix(xin, transpose, out_dtype, name):
    s = xin.shape[0]
    n_rows = s // GRID_W
    pad_t = SUBLANES * GRID_W
    rows_per_blk = POOL_TOK // GRID_W
    n_slab = D_INNER // LANES
    slabs_per_group = POOL_GROUP // LANES
    n_win = len(POOL_WINDOWS)
    boxes, inv_c, inv_r = _pool_tables(n_rows, transpose)
    exact_operand = (not transpose) and xin.dtype == MXU_DTYPE and MXU_DTYPE != F32

    def body(invr_ref, box_ref, invc_ref, x_ref, o_ref, pad_s):
        k = pl.program_id(0) // slabs_per_group
        pad_s[pl.ds(0, pad_t), :] = jnp.zeros((pad_t, LANES), F32)
        pad_s[pl.ds(pad_t + s, pad_t), :] = jnp.zeros((pad_t, LANES), F32)

        for kk, w in enumerate(POOL_WINDOWS):
            half = w // 2
            offsets = list(range(-(half - 1), half + 1)) if transpose else list(range(-half, half))

            @pl.when(k == kk)
            def _():
                inv_col = invc_ref[kk]

                def col_box(b):
                    st = pl.multiple_of(b * POOL_TOK, POOL_TOK)
                    xb = x_ref[pl.ds(st, POOL_TOK), :]
                    if exact_operand:
                        pad_s[pl.ds(pad_t + st, POOL_TOK), :] = jnp.dot(box_ref[kk], xb, preferred_element_type=F32)
                        return
                    xb = xb.astype(F32)
                    if transpose:
                        xb = xb * jnp.concatenate(
                            [inv_col * invr_ref[kk, b * rows_per_blk + q] for q in range(rows_per_blk)], axis=0)
                    hi = xb.astype(MXU_DTYPE)
                    lo = (xb - hi.astype(F32)).astype(MXU_DTYPE)
                    both = jnp.dot(box_ref[kk], jnp.concatenate([hi, lo], axis=1), preferred_element_type=F32)
                    pad_s[pl.ds(pad_t + st, POOL_TOK), :] = both[:, :LANES] + both[:, LANES:]
                _unrolled_loop(s // POOL_TOK, col_box)

                def row_box(r):
                    st = pl.multiple_of(r * GRID_W, GRID_W)
                    acc = pad_s[pl.ds(pad_t + st + offsets[0] * GRID_W, GRID_W), :]
                    for o in offsets[1:]:
                        acc = acc + pad_s[pl.ds(pad_t + st + o * GRID_W, GRID_W), :]
                    if not transpose:
                        acc = acc * (inv_col * invr_ref[kk, r])
                    o_ref[pl.ds(st, GRID_W), :] = (acc - x_ref[pl.ds(st, GRID_W), :].astype(F32)).astype(out_dtype)
                _unrolled_loop(n_rows, row_box)

    slab = pl.BlockSpec((s, LANES), lambda i: (0, i))
    return pl.pallas_call(
        body, name=name, out_shape=jax.ShapeDtypeStruct((s, D_INNER), out_dtype), grid=(n_slab,),
        in_specs=[pl.BlockSpec(memory_space=pltpu.SMEM),
                  pl.BlockSpec((n_win, POOL_TOK, POOL_TOK), lambda i: (0, 0, 0)),
                  pl.BlockSpec((n_win, GRID_W, LANES), lambda i: (0, 0, 0)), slab],
        out_specs=slab,
        scratch_shapes=[pltpu.VMEM((s + 2 * pad_t, LANES), F32)],
        compiler_params=_cparams(dimension_semantics=("arbitrary",)),
    )(inv_r, boxes, inv_c, xin)


def _out1(dmix, pw, ps, g, x1, gt, wo, lg, lb, tgt, name):
    t = x1.shape[0]
    tm = min(TM_MM, t)
    n_grp = len(POOL_WINDOWS)

    def body(d_ref, pw_ref, ps_ref, g_ref, x1_ref, gt_ref, w_ref, lg_ref, lb_ref, tgt_ref, dz_ref, st_ref, po_ref):
        @pl.when(pl.program_id(0) == 0)
        def _():
            st_ref[...] = jnp.zeros_like(st_ref)

        br = jnp.zeros((tm, D_MODEL), F32)
        for k in range(n_grp):
            sl = slice(k * POOL_GROUP, (k + 1) * POOL_GROUP)
            po = jnp.dot(d_ref[:, sl], pw_ref[k], preferred_element_type=F32)
            po_ref[:, sl] = po.astype(ACT_DTYPE)
            y = po * ps_ref[:, sl]
            gg = g_ref[:, sl].astype(F32)
            br = br + _dot(y * (gg * _sigmoid(gg)), w_ref[sl, :])
        z = ALPHA * x1_ref[...] + gt_ref[...] * br
        xhat, rstd = _layer_norm_stats(z)
        lg_v = lg_ref[...]
        err = xhat * lg_v + lb_ref[...] - tgt_ref[...]
        dy = err * (1.0 / D_MODEL)
        dz = _layer_norm_bwd(dy, xhat, rstd, lg_v)
        dz_ref[...] = dz
        st_ref[0:1, :] += _rowsum(dy * xhat)
        st_ref[1:2, :] += _rowsum(dy)
        st_ref[2:3, :] += _rowsum(dz * br)
        st_ref[3:4, :] += _rowsum(err * err)

    wide = pl.BlockSpec((tm, D_INNER), lambda i: (i, 0))
    nar = pl.BlockSpec((tm, D_MODEL), lambda i: (i, 0))
    row = pl.BlockSpec((1, D_MODEL), lambda i: (0, 0))
    return pl.pallas_call(
        body, name=name,
        out_shape=[jax.ShapeDtypeStruct((t, D_MODEL), F32), jax.ShapeDtypeStruct((SUBLANES, D_MODEL), F32),
                   jax.ShapeDtypeStruct((t, D_INNER), ACT_DTYPE)],
        grid=(t // tm,),
        in_specs=[wide, pl.BlockSpec((n_grp, POOL_GROUP, POOL_GROUP), lambda i: (0, 0, 0)),
                  pl.BlockSpec((1, D_INNER), lambda i: (0, 0)), wide, nar, row,
                  pl.BlockSpec((D_INNER, D_MODEL), lambda i: (0, 0), pipeline_mode=pl.Buffered(1)), row, row, nar],
        out_specs=[nar, pl.BlockSpec((SUBLANES, D_MODEL), lambda i: (0, 0)), wide],
        compiler_params=_cparams(dimension_semantics=("arbitrary",)),
    )(dmix, pw, ps, g, x1, gt, wo, lg, lb, tgt)


def _flush(acc, out_hbm, sem):
    cp = pltpu.make_async_copy(acc, out_hbm, sem)
    cp.start()
    cp.wait()


def _bout1(dz, dmix, po, g, pw, ps, gt, wo, name):
    t = dz.shape[0]
    tm = min(TM_MM, t)
    nt = t // tm
    n_grp = len(POOL_WINDOWS)

    def body(dz_ref, d_ref, po_ref, g_ref, pw_ref, ps_ref, gt_ref, w_ref, dd_ref, dg_ref, gwo_hbm, gpw_hbm, gps_ref,
             gwo_acc, gpw_acc, sems):
        i = pl.program_id(0)

        @pl.when(i == 0)
        def _():
            gwo_acc[...] = jnp.zeros_like(gwo_acc)
            gpw_acc[...] = jnp.zeros_like(gpw_acc)
            gps_ref[...] = jnp.zeros_like(gps_ref)

        db = (gt_ref[...] * dz_ref[...]).astype(MXU_DTYPE)
        for k in range(n_grp):
            sl = slice(k * POOL_GROUP, (k + 1) * POOL_GROUP)
            dk = d_ref[:, sl]
            po = po_ref[:, sl].astype(F32)
            psk = ps_ref[:, sl]
            y = po * psk
            gg = g_ref[:, sl].astype(F32)
            sg = _sigmoid(gg)
            silu = gg * sg
            gwo_acc[sl, :] += _dot_tn(y * silu, db)
            dp = _dot_nt(db, w_ref[sl, :])
            dy = dp * silu
            dg_ref[:, sl] = (dp * y * (sg * (1.0 + gg * (1.0 - sg)))).astype(MXU_DTYPE)
            gps_ref[0:1, sl] += _rowsum(dy * po)
            dpo = (dy * psk).astype(MXU_DTYPE)
            gpw_acc[k] += _dot_tn(dk, dpo)
            dd_ref[:, sl] = _dot_nt(dpo, pw_ref[k])

        @pl.when(i == nt - 1)
        def _():
            _flush(gwo_acc, gwo_hbm, sems.at[0])
            _flush(gpw_acc, gpw_hbm, sems.at[1])

    wide = pl.BlockSpec((tm, D_INNER), lambda i: (i, 0))
    nar = pl.BlockSpec((tm, D_MODEL), lambda i: (i, 0))
    return pl.pallas_call(
        body, name=name,
        out_shape=[jax.ShapeDtypeStruct((t, D_INNER), F32), jax.ShapeDtypeStruct((t, D_INNER), MXU_DTYPE),
                   jax.ShapeDtypeStruct((D_INNER, D_MODEL), F32),
                   jax.ShapeDtypeStruct((n_grp, POOL_GROUP, POOL_GROUP), F32),
                   jax.ShapeDtypeStruct((SUBLANES, D_INNER), F32)],
        grid=(nt,),
        in_specs=[nar, wide, wide, wide,
                  pl.BlockSpec((n_grp, POOL_GROUP, POOL_GROUP), lambda i: (0, 0, 0), pipeline_mode=pl.Buffered(1)),
                  pl.BlockSpec((1, D_INNER), lambda i: (0, 0)), pl.BlockSpec((1, D_MODEL), lambda i: (0, 0)),
                  pl.BlockSpec((D_INNER, D_MODEL), lambda i: (0, 0), pipeline_mode=pl.Buffered(1))],
        out_specs=[wide, wide, ANY, ANY, pl.BlockSpec((SUBLANES, D_INNER), lambda i: (0, 0))],
        scratch_shapes=[pltpu.VMEM((D_INNER, D_MODEL), F32), pltpu.VMEM((n_grp, POOL_GROUP, POOL_GROUP), F32),
                        pltpu.SemaphoreType.DMA((2,))],
        compiler_params=_cparams(dimension_semantics=("arbitrary",)),
    )(dz, dmix, po, g, pw, ps, gt, wo)


def _bout0(dx1, xt, br0, lg, hf, hb, g, gt, wo, name, sides=()):
    t = dx1.shape[0]
    tm = min(TM_MM, t)
    nt = t // tm

    def body(dx_ref, x_ref, br_ref, lg_ref, hf_ref, hb_ref, g_ref, gt_ref, w_ref,
             dz_ref, dy_ref, dg_ref, gwo_hbm, st_ref, gwo_acc, sem):
        i = pl.program_id(0)

        @pl.when(i == 0)
        def _():
            gwo_acc[...] = jnp.zeros_like(gwo_acc)
            st_ref[...] = jnp.zeros_like(st_ref)

        dx = dx_ref[...]
        br = br_ref[...].astype(F32)
        gate = gt_ref[...]
        xhat, rstd = _layer_norm_stats(ALPHA * x_ref[...] + gate * br)
        dz = _layer_norm_bwd(dx, xhat, rstd, lg_ref[...])
        dz_ref[...] = dz
        st_ref[0:1, :] += _rowsum(dx * xhat)
        st_ref[1:2, :] += _rowsum(dx)
        st_ref[2:3, :] += _rowsum(dz * br)
        db = (gate * dz).astype(MXU_DTYPE)
        for k in range(D_INNER // WBLK):
            sl = slice(k * WBLK, (k + 1) * WBLK)
            y = hf_ref[:, sl].astype(F32) + hb_ref[:, sl].astype(F32)
            gg = g_ref[:, sl].astype(F32)
            sg = _sigmoid(gg)
            silu = gg * sg
            gwo_acc[sl, :] += _dot_tn(y * silu, db)
            dp = _dot_nt(db, w_ref[sl, :])
            dy_ref[:, sl] = (dp * silu).astype(ACT_DTYPE)
            dg_ref[:, sl] = (dp * y * (sg * (1.0 + gg * (1.0 - sg)))).astype(MXU_DTYPE)

        @pl.when(i == nt - 1)
        def _():
            _flush(gwo_acc, gwo_hbm, sem)

    wide = pl.BlockSpec((tm, D_INNER), lambda i: (i, 0))
    nar = pl.BlockSpec((tm, D_MODEL), lambda i: (i, 0))
    row = pl.BlockSpec((1, D_MODEL), lambda i: (0, 0))
    return _call_with_sides(
        body, sides, name=name,
        out_shape=[jax.ShapeDtypeStruct((t, D_MODEL), F32), jax.ShapeDtypeStruct((t, D_INNER), ACT_DTYPE),
                   jax.ShapeDtypeStruct((t, D_INNER), MXU_DTYPE), jax.ShapeDtypeStruct((D_INNER, D_MODEL), F32),
                   jax.ShapeDtypeStruct((SUBLANES, D_MODEL), F32)],
        grid=(nt,),
        in_specs=[nar, nar, nar, row, wide, wide, wide, row,
                  pl.BlockSpec((D_INNER, D_MODEL), lambda i: (0, 0), pipeline_mode=pl.Buffered(1))],
        out_specs=[nar, wide, wide, ANY, pl.BlockSpec((SUBLANES, D_MODEL), lambda i: (0, 0))],
        scratch_shapes=[pltpu.VMEM((D_INNER, D_MODEL), F32), pltpu.SemaphoreType.DMA(())],
        compiler_params=_cparams(dimension_semantics=("arbitrary",)),
        args=[dx1, xt, br0, lg, hf, hb, g, gt, wo])


def _conv_bwd(duvf, duvb, u, conv_w, name, sides=()):
    s = u.shape[0]
    tm = min(TM_LRU, s)
    cb = CB_LRU
    nt = s // tm

    def body(df_ref, dfp_ref, dfn_ref, db_ref, dbp_ref, dbn_ref, u_ref, cw_ref, du_ref, cst_ref):
        i = pl.program_id(1)

        @pl.when(i == 0)
        def _():
            cst_ref[...] = jnp.zeros_like(cst_ref)

        first, last = i == 0, i == nt - 1
        pz = jnp.where(first, 0.0, 1.0)
        nz = jnp.where(last, 0.0, 1.0)
        dout = df_ref[...].astype(F32) + db_ref[...].astype(F32)
        before = (dfp_ref[...].astype(F32) + dbp_ref[...].astype(F32))[H_HALO - SUBLANES:] * pz
        after = (dfn_ref[...].astype(F32) + dbn_ref[...].astype(F32))[:SUBLANES] * nz
        dm1, dp1, dp2 = _shifted(dout, before, after, [-1, 1, 2])
        cw = cw_ref[...]
        du_ref[...] = (dp2 * cw[0:1] + dp1 * cw[1:2] + dout * cw[2:3] + dm1 * cw[3:4]).astype(MXU_DTYPE)
        u_t = u_ref[...]
        cst_ref[0:1, :] += _rowsum(dp2 * u_t)
        cst_ref[1:2, :] += _rowsum(dp1 * u_t)
        cst_ref[2:3, :] += _rowsum(dout * u_t)
        cst_ref[3:4, :] += _rowsum(dm1 * u_t)
        cst_ref[4:5, :] += _rowsum(dout)

    tile, _, _ = _lru_specs(s, tm, cb, lambda i: i, nt)
    prev_map, next_map = _halo_maps(nt, tm, s // H_HALO, lambda i: i, rows=H_HALO)
    prev, nxt = pl.BlockSpec((H_HALO, cb), prev_map), pl.BlockSpec((H_HALO, cb), next_map)
    return _call_with_sides(
        body, sides, name=name,
        out_shape=[jax.ShapeDtypeStruct((s, D_INNER), MXU_DTYPE), jax.ShapeDtypeStruct((SUBLANES, D_INNER), F32)],
        grid=(D_INNER // cb, nt),
        in_specs=[tile, prev, nxt] * 2 + [tile, pl.BlockSpec((4, cb), lambda c, i: (0, c))],
        out_specs=[tile, pl.BlockSpec((SUBLANES, cb), lambda c, i: (0, c))], scratch_shapes=[],
        compiler_params=_cparams(dimension_semantics=("arbitrary", "arbitrary")),
        args=[duvf, duvf, duvf, duvb, duvb, duvb, u, conv_w])


def _bin(du, dg, xin, dzin, sc, sh, wg, name, gw_init=None, sides=()):
    t = xin.shape[0]
    tm = min(TM_MM, t)
    nt = t // tm
    has_g, has_dx, has_init = dg is not None, dzin is not None, gw_init is not None
    half = N_WBLK // 2
    n_blk = N_WBLK if has_g else half

    def body(*refs):
        refs = list(refs)
        du_ref = refs.pop(0)
        dg_ref = refs.pop(0) if has_g else None
        x_ref = refs.pop(0)
        dz_ref = refs.pop(0) if has_dx else None
        sc_ref, sh_ref, w_ref = refs.pop(0), refs.pop(0), refs.pop(0)
        init_hbm = refs.pop(0) if has_init else None
        dx_ref = refs.pop(0) if has_dx else None
        gw_hbm, st_ref, gw_acc, sem = refs
        i = pl.program_id(0)

        @pl.when(i == 0)
        def _():
            st_ref[...] = jnp.zeros_like(st_ref)
            first_zero = 0
            if has_init:
                _flush(init_hbm, gw_acc.at[pl.ds(0, half)], sem)
                first_zero = half
            for k in range(first_zero, n_blk):
                gw_acc[k] = jnp.zeros((D_MODEL, WBLK), F32)

        xv = x_ref[...]
        scale = 1.0 + sc_ref[...]
        h = (xv * scale + sh_ref[...]).astype(MXU_DTYPE)
        dh = None
        for k in range(n_blk):
            src = du_ref if k < half else dg_ref
            kk = k % half
            dk = src[:, kk * WBLK:(kk + 1) * WBLK]
            gw_acc[k] += _dot_tn(h, dk)
            contrib = _dot_nt(dk, w_ref[k])
            dh = contrib if dh is None else dh + contrib
        st_ref[0:1, :] += _rowsum(dh * xv)
        st_ref[1:2, :] += _rowsum(dh)
        if has_dx:
            dx_ref[...] = ALPHA * dz_ref[...] + dh * scale

        @pl.when(i == nt - 1)
        def _():
            _flush(gw_acc, gw_hbm, sem)

    wide = pl.BlockSpec((tm, D_INNER), lambda i: (i, 0))
    nar = pl.BlockSpec((tm, D_MODEL), lambda i: (i, 0))
    row = pl.BlockSpec((1, D_MODEL), lambda i: (0, 0))
    wspec = pl.BlockSpec((n_blk, D_MODEL, WBLK), lambda i: (0, 0, 0), pipeline_mode=pl.Buffered(1))
    in_specs = ([wide] + ([wide] if has_g else []) + [nar] + ([nar] if has_dx else []) + [row, row, wspec]
                + ([ANY] if has_init else []))
    args = ([du] + ([dg] if has_g else []) + [xin] + ([dzin] if has_dx else []) + [sc, sh, wg]
            + ([gw_init] if has_init else []))
    out_shape = ([jax.ShapeDtypeStruct((t, D_MODEL), F32)] if has_dx else []) + [
        jax.ShapeDtypeStruct((n_blk, D_MODEL, WBLK), F32), jax.ShapeDtypeStruct((SUBLANES, D_MODEL), F32)]
    out_specs = ([nar] if has_dx else []) + [ANY, pl.BlockSpec((SUBLANES, D_MODEL), lambda i: (0, 0))]
    return _call_with_sides(
        body, sides, name=name, out_shape=out_shape, grid=(nt,), in_specs=in_specs, out_specs=out_specs,
        scratch_shapes=[pltpu.VMEM((n_blk, D_MODEL, WBLK), F32), pltpu.SemaphoreType.DMA(())],
        compiler_params=_cparams(dimension_semantics=("arbitrary",)), args=args)


def _blocks_by_device(a, axis):
    shape = a.shape
    a = a.reshape(shape[:axis] + (N_DEV, shape[axis] // N_DEV) + shape[axis + 1:])
    return jnp.moveaxis(a, axis, 0)


def kernel(x, c, ctx, c_ctx, w_mod, b_mod, w_in, w_out, ln_g, ln_b, conv_w, conv_b, lru_wa, lru_ba, lru_wx, lru_bx, lru_lam, pool_w, pool_scale, loss_target, m_c_ctx, m_w_mod, m_b_mod, m_w_in, m_w_out, m_ln_g, m_ln_b, m_conv_w, m_conv_b, m_lru_wa, m_lru_ba, m_lru_wx, m_lru_bx, m_lru_lam, m_pool_w, m_pool_scale, v_c_ctx, v_w_mod, v_b_mod, v_w_in, v_w_out, v_ln_g, v_ln_b, v_conv_w, v_conv_b, v_lru_wa, v_lru_ba, v_lru_wx, v_lru_bx, v_lru_lam, v_pool_w, v_pool_scale):
    xi, yi, ci = _my_pos()
    dev = 4 * xi + 2 * yi + ci
    xt, ctxt, tgt = x[0], ctx[0], loss_target[0]
    n_mod = w_mod.shape[2]

    small_shapes = [(D_MODEL,), conv_w.shape[1:], lru_ba.shape[1:], lru_bx.shape[1:], lru_lam.shape[1:],
                    pool_scale.shape[1:]]
    small = _to_rows([c[0], conv_w[0], lru_ba[0], lru_bx[0], lru_lam[0], pool_scale[0]], SUBLANES)
    small_all, wi0 = _all_gather([small, w_in[0].astype(MXU_DTYPE)], "gather_first")
    pieces = [_split_rows(small_all[k], small_shapes) for k in range(N_DEV)]
    c_all = jnp.stack([p[0] for p in pieces])
    conv_w_f = jnp.concatenate([p[1] for p in pieces], axis=-1)
    lru_ba_f = jnp.concatenate([p[2] for p in pieces], axis=-1)[:, None, :]
    lru_bx_f = jnp.concatenate([p[3] for p in pieces], axis=-1)[:, None, :]
    lru_lam_f = jnp.concatenate([p[4] for p in pieces], axis=-1)[:, None, :]
    pool_scale_f = jnp.concatenate([p[5] for p in pieces], axis=-1)[None, :]

    cond = jnp.concatenate([c_all, jnp.broadcast_to(c_ctx[None, :], (N_DEV, D_MODEL))], axis=0)
    b_my = lax.dynamic_slice(b_mod, (0, dev * n_mod), (2, n_mod))[:, None, :]
    mod_part = _mod_fwd(cond, w_mod, b_my, "mod_fwd")
    mod_all, = _all_gather([mod_part], "gather_mod")
    mod = jnp.transpose(mod_all, (1, 2, 0, 3)).reshape(2, 16, 3 * D_MODEL)
    mod_me = lax.dynamic_slice(mod, (0, dev, 0), (2, 1, 3 * D_MODEL))
    sh = [mod_me[i, :, 0:D_MODEL] for i in range(2)]
    sc = [mod_me[i, :, D_MODEL:2 * D_MODEL] for i in range(2)]
    gt = [mod_me[i, :, 2 * D_MODEL:] for i in range(2)]
    shc, scc = mod[0, 8:9, 0:D_MODEL], mod[0, 8:9, D_MODEL:2 * D_MODEL]

    lg = [ln_g[i][None, :] for i in range(2)]
    lb = [ln_b[i][None, :] for i in range(2)]
    lru_p = dict(conv_w=conv_w_f, conv_b=conv_b, wa=(0.5 * lru_wa[0]).astype(MXU_DTYPE),
                 wx=(0.5 * lru_wx[0]).astype(MXU_DTYPE), ba=0.5 * lru_ba_f, bx=0.5 * lru_bx_f, lam=lru_lam_f)
    zero_state = jnp.zeros((1, D_INNER), F32)

    (u0, g0), (wo0,) = _in_proj(xt, sc[0], sh[0], wi0, "in_proj0", sides=[("gather", [w_out[0].astype(MXU_DTYPE)])])
    (uc, _), _ = _in_proj(ctxt, scc, shc, wi0, "in_proj0_ctx")
    (hcf, cf, uvc), _ = _lru_fwd(uc, zero_state, lru_p, 0, "lru_fwd_ctx_f", conv=True)
    (hcb, cbk), _ = _lru_fwd(uvc, zero_state, lru_p, 1, "lru_fwd_ctx_b", conv=False)
    (hf, _, uv0), (wi1,) = _lru_fwd(u0, cf, lru_p, 0, "lru_fwd_f", conv=True,
                                    sides=[("gather", [w_in[1].astype(MXU_DTYPE)])])
    (hb, _), (wo1, pool_w_g) = _lru_fwd(
        uv0, cbk, lru_p, 1, "lru_fwd_b", conv=False,
        sides=[("gather", [w_out[1].astype(MXU_DTYPE), pool_w[0].astype(MXU_DTYPE)])])
    w_in_l = [wi0, wi1]
    w_out_l = [wo0.reshape(D_INNER, D_MODEL), wo1.reshape(D_INNER, D_MODEL)]
    pool_w_f = jnp.transpose(pool_w_g, (1, 0, 2, 3)).reshape(len(POOL_WINDOWS), POOL_GROUP, POOL_GROUP)
    x1, br0, u1, g1 = _out0(hf, hb, g0, xt, gt[0], w_out_l[0], lg[0], lb[0], sc[1], sh[1], w_in_l[1], "out0_in1")
    dmix = _pool_mix(u1, False, MXU_DTYPE, "pool_fwd")
    dz1, st1, po1 = _out1(dmix, pool_w_f, pool_scale_f, g1, x1, gt[1], w_out_l[1], lg[1], lb[1], tgt, "out1")
    loss_me = jnp.full((1, LANES), (0.5 / D_MODEL) * jnp.sum(st1[3]), F32)

    core = jnp.reshape(ci, (1,)).astype(jnp.int32)
    wo_view = lambda a: a.reshape(N_DEV, D_INNER // N_DEV, D_MODEL)
    pw_view = lambda a: _blocks_by_device(a, 1).reshape(N_DEV, POOL_GROUP // N_DEV * len(POOL_WINDOWS), POOL_GROUP)
    dd, dg1, gwo1, gpw, gps = _bout1(dz1, dmix, po1, g1, pool_w_f, pool_scale_f, gt[1], w_out_l[1], "bwd_out1")
    du1 = _pool_mix(dd, True, MXU_DTYPE, "pool_bwd")
    (dx1, gwi1, stb1), _ = _bin(du1, dg1, x1, dz1, sc[1], sh[1], w_in_l[1], "bwd_in1")
    bufs1 = [gwi1, wo_view(gwo1), pw_view(gpw)]
    (dz0, dy0, dg0, gwo0, stl0), recv1 = _bout0(dx1, xt, br0, lg[0], hf, hb, g0, gt[0], w_out_l[0], "bwd_out0",
                                                sides=[("sibling", bufs1)])
    pairs1 = [_pair_sum(b, r, core, "reduce_pair_" + n)
              for b, r, n in zip(bufs1, recv1, ["w_in1", "w_out1", "pool_w"])]
    (duvf, gwa_f, gwx_f, gv_f, dh0f), (p_wi1, p_wo1, p_pw, recv_wo0) = _lru_bwd(
        uv0, dy0, hf, cf, zero_state, lru_p, 0, "lru_bwd_f", sides=[("chips", pairs1), ("sibling", [wo_view(gwo0)])])
    pair_wo0 = _pair_sum(wo_view(gwo0), recv_wo0, core, "reduce_pair_w_out0")
    (duvb, gwa_b, gwx_b, gv_b, dh0b), (p_wo0,) = _lru_bwd(
        uv0, dy0, hb, cbk, zero_state, lru_p, 1, "lru_bwd_b", sides=[("chips", [pair_wo0])])
    zero_dh = jnp.zeros(uc.shape, ACT_DTYPE)
    (ducf, gwa_cf, gwx_cf, gv_cf, _), _ = _lru_bwd(uvc, zero_dh, hcf, zero_state, dh0f, lru_p, 0, "lru_bwd_ctx_f")
    (ducb, gwa_cb, gwx_cb, gv_cb, _), _ = _lru_bwd(uvc, zero_dh, hcb, zero_state, dh0b, lru_p, 1, "lru_bwd_ctx_b")

    def pack(sharded, replicated):
        sh_sizes = [int(np.prod(a.shape[1:])) for a in sharded]
        rep_sizes = [a.shape[0] // N_DEV for a in replicated]
        n_flat = sum(sh_sizes) + sum(rep_sizes)
        rows = -(-(-(-n_flat // LANES)) // FLAT_ROWS) * FLAT_ROWS
        buf = jnp.concatenate([a.reshape(N_DEV, -1) for a in sharded + replicated], axis=1)
        return jnp.pad(buf, ((0, 0), (0, rows * LANES - n_flat))).reshape(N_DEV, rows, LANES), sh_sizes, rep_sizes

    def unpack(reduced, sh_sizes, rep_sizes, sh_shapes):
        flat = reduced.reshape(-1)
        offs = np.cumsum([0] + sh_sizes)
        mine = [flat[offs[k]:offs[k + 1]].reshape(s) for k, s in enumerate(sh_shapes)]
        return mine, _to_rows([flat[offs[-1]:offs[-1] + sum(rep_sizes)]], FLAT_ROWS)

    def spread(rep_all, rep_sizes, shapes):
        flat = rep_all.reshape(N_DEV, -1)
        offs = np.cumsum([0] + rep_sizes)
        return [flat[:, offs[k]:offs[k + 1]].reshape(s) for k, s in enumerate(shapes)]

    (du0, cst0), _ = _conv_bwd(duvf, duvb, u0, conv_w_f, "conv_bwd")
    (duc, cstc), _ = _conv_bwd(ducf, ducb, uc, conv_w_f, "conv_bwd_ctx")
    (gwic, stc), _ = _bin(duc, None, ctxt, None, scc, shc, w_in_l[0][:N_WBLK // 2], "bwd_in0_ctx")
    (gx, gwi0, stb0), _ = _bin(du0, dg0, xt, dz0, sc[0], sh[0], w_in_l[0], "bwd_in0", gw_init=gwic)

    zero_row = jnp.zeros((1, D_MODEL), F32)
    dm_me = jnp.stack([
        jnp.concatenate([jnp.concatenate([stb0[1:2], stb0[0:1], stl0[2:3]], axis=1),
                         jnp.concatenate([stc[1:2], stc[0:1], zero_row], axis=1)], axis=0),
        jnp.concatenate([jnp.concatenate([stb1[1:2], stb1[0:1], st1[2:3]], axis=1),
                         jnp.zeros((1, 3 * D_MODEL), F32)], axis=0)])
    dm_g, loss_g = _all_gather([dm_me, loss_me], "gather_dmod")
    loss = jnp.sum(loss_g[:, 0, 0])
    dm_all = jnp.concatenate([jnp.transpose(dm_g[:, :, 0], (1, 0, 2)), jnp.transpose(dm_g[:, :, 1], (1, 0, 2))],
                             axis=1)
    dm_my = lax.dynamic_slice(dm_all, (0, 0, dev * n_mod), (2, 16, n_mod))
    g_w_mod, g_b_mod, gcc_part = _mod_bwd(cond, dm_all, dm_my, w_mod, "mod_bwd")
    g_b_mod = g_b_mod.reshape(b_mod.shape)

    gwa = jnp.stack([gwa_f + gwa_cf, gwa_b + gwa_cb])
    gwx = jnp.stack([gwx_f + gwx_cf, gwx_b + gwx_cb])
    gv = jnp.stack([gv_f + gv_cf, gv_b + gv_cb])
    cst = cst0 + cstc
    misc, m_sh, m_rep = pack(
        [_blocks_by_device(cst[0:4], 1), _blocks_by_device(gv[:, 0], 1), _blocks_by_device(gv[:, 1], 1),
         _blocks_by_device(gv[:, 2], 1), _blocks_by_device(gps[0], 0)],
        [gwa.reshape(-1), gwx.reshape(-1), jnp.stack([stl0[0], st1[0]]).reshape(-1),
         jnp.stack([stl0[1], st1[1]]).reshape(-1), cst[4], gcc_part.reshape(-1)])
    bufs = [gwi0, misc]
    recvs = _sibling_exchange(bufs, "reduce_sibling")
    pairs = [_pair_sum(b, r, core, "reduce_pair_" + n) for b, r, n in zip(bufs, recvs, ["w_in0", "misc"])]
    p_wi0, p_misc = _chip_exchange(pairs, "reduce_chips")
    (g_conv_w, g_lru_ba, g_lru_bx, g_lru_lam, g_pool_scale), rep_mine = unpack(
        _sum4(p_misc, "reduce_sum_misc"), m_sh, m_rep,
        [conv_w.shape, lru_ba.shape, lru_bx.shape, lru_lam.shape, pool_scale.shape])
    rep_all, = _all_gather([rep_mine.astype(WIRE_DTYPE)], "gather_replicated")
    rep_all = rep_all.astype(F32)
    g_lru_wa, g_lru_wx, g_ln_g, g_ln_b, g_conv_b, g_c_ctx = spread(
        rep_all, m_rep, [lru_wa.shape, lru_wx.shape, ln_g.shape, ln_b.shape, conv_b.shape, c_ctx.shape])

    names = ["c_ctx", "w_mod", "b_mod", "w_in", "w_out", "ln_g", "ln_b", "conv_w", "conv_b", "lru_wa", "lru_ba",
             "lru_wx", "lru_bx", "lru_lam", "pool_w", "pool_scale"]
    weights = dict(c_ctx=c_ctx, w_mod=w_mod, b_mod=b_mod, w_in=w_in, w_out=w_out, ln_g=ln_g, ln_b=ln_b,
                   conv_w=conv_w, conv_b=conv_b, lru_wa=lru_wa, lru_ba=lru_ba, lru_wx=lru_wx, lru_bx=lru_bx,
                   lru_lam=lru_lam, pool_w=pool_w, pool_scale=pool_scale)
    mom_m = dict(c_ctx=m_c_ctx, w_mod=m_w_mod, b_mod=m_b_mod, w_in=m_w_in, w_out=m_w_out, ln_g=m_ln_g, ln_b=m_ln_b,
                 conv_w=m_conv_w, conv_b=m_conv_b, lru_wa=m_lru_wa, lru_ba=m_lru_ba, lru_wx=m_lru_wx,
                 lru_bx=m_lru_bx, lru_lam=m_lru_lam, pool_w=m_pool_w, pool_scale=m_pool_scale)
    mom_v = dict(c_ctx=v_c_ctx, w_mod=v_w_mod, b_mod=v_b_mod, w_in=v_w_in, w_out=v_w_out, ln_g=v_ln_g, ln_b=v_ln_b,
                 conv_w=v_conv_w, conv_b=v_conv_b, lru_wa=v_lru_wa, lru_ba=v_lru_ba, lru_wx=v_lru_wx,
                 lru_bx=v_lru_bx, lru_lam=v_lru_lam, pool_w=v_pool_w, pool_scale=v_pool_scale)
    grads = dict(c_ctx=g_c_ctx, w_mod=g_w_mod, b_mod=g_b_mod, ln_g=g_ln_g, ln_b=g_ln_b,
                 conv_w=g_conv_w, conv_b=g_conv_b, lru_wa=g_lru_wa, lru_ba=g_lru_ba, lru_wx=g_lru_wx,
                 lru_bx=g_lru_bx, lru_lam=g_lru_lam)
    grads["pool_scale"] = g_pool_scale
    delta, new_m, new_v = {}, {}, {}

    def update_parts(n, parts, view):
        res = _adamw_parts(weights[n].reshape(view), parts, mom_m[n].reshape(view), mom_v[n].reshape(view),
                           "adamw_" + n)
        grads[n], delta[n], new_m[n], new_v[n] = [r.reshape(weights[n].shape) for r in res]

    update_parts("w_in", [p_wi0, p_wi1], w_in.shape)
    update_parts("w_out", [p_wo0, p_wo1], w_out.shape)
    update_parts("pool_w", [p_pw], (1,) + p_pw.shape[1:])
    for n in ("w_mod", "lru_wa", "lru_wx"):
        shape = weights[n].shape
        view = (int(np.prod(shape[:-1])), shape[-1])
        res = _adamw(weights[n].reshape(view), grads[n].reshape(view), mom_m[n].reshape(view),
                     mom_v[n].reshape(view), "adamw_" + n)
        delta[n], new_m[n], new_v[n] = [r.reshape(shape) for r in res]

    small = [n for n in names if n not in delta]
    shapes = [weights[n].shape for n in small]
    flat = lambda d: _to_rows([d[n] for n in small], FLAT_ROWS)
    res = _adamw(flat(weights), flat(grads), flat(mom_m), flat(mom_v), "adamw_small")
    for d, r in zip((delta, new_m, new_v), res):
        d.update(zip(small, _split_rows(r, shapes)))

    return (loss, gx[None], *[grads[n] for n in names], *[delta[n] for n in names],
            *[new_m[n] for n in names], *[new_v[n] for n in names])
```

```python
import functools

import numpy as np
import jax
import jax.numpy as jnp
from jax import lax
from jax.experimental import pallas as pl
from jax.experimental.pallas import tpu as pltpu

F32 = jnp.float32
BF16 = jnp.bfloat16
MXU_DTYPE = BF16

D_MODEL = 1024
D_INNER = 2048
LRU_BLOCK = 128
GRID_W = 64
POOL_WINDOWS = (2, 4, 8, 16)
POOL_GROUP = 512
ALPHA = float(4 ** 0.25)
LN_EPS = 1e-5
LRU_C = 8.0
N_DEV = 8
N_WBLK = 8
WBLK = 512

ADAM_LR = 0.001
ADAM_B1 = 0.9
ADAM_B2 = 0.999
ADAM_EPS = 1e-08
ADAM_WD = 0.01
ADAM_STEP = 10

LANES = 128
SUBLANES = 8
V7X_VMEM_BYTES = 64 * 1024 * 1024
VMEM_COMPILER_RESERVE = 8 * 1024 * 1024
VMEM_LIMIT = V7X_VMEM_BYTES - VMEM_COMPILER_RESERVE
MESH = pl.DeviceIdType.MESH
ANY = pl.BlockSpec(memory_space=pl.ANY)

TM_MM = 512
TM_LRU = 1024
CB_LRU = 512
CB_LRU_FWD = 1024
N_SEG = 8
SCAN_UNROLL = 4
SCAN_ROW_T = 17
SCAN_ROW_J = 2
SQRT_FLOOR = 1e-30
FLAT_ROWS = 16
ELEMENTWISE_TILE_BYTES = 1 << 20
POOL_TOK = 256
WIRE_DTYPE = BF16
ACT_DTYPE = BF16
H_HALO = 16


def _cparams(**kw):
    return pltpu.CompilerParams(vmem_limit_bytes=VMEM_LIMIT, **kw)


def _my_pos():
    return lax.axis_index("x"), lax.axis_index("y"), lax.axis_index("c")


def _dot(a, b):
    return jnp.dot(a.astype(MXU_DTYPE), b.astype(MXU_DTYPE), preferred_element_type=F32)


def _dot_tn(a, b):
    return lax.dot_general(a.astype(MXU_DTYPE), b.astype(MXU_DTYPE), (((0,), (0,)), ((), ())),
                           preferred_element_type=F32)


def _dot_nt(a, b):
    return lax.dot_general(a.astype(MXU_DTYPE), b.astype(MXU_DTYPE), (((1,), (1,)), ((), ())),
                           preferred_element_type=F32)


def _sigmoid(z):
    return 0.5 * jnp.tanh(0.5 * z) + 0.5


def _log_sigmoid(x):
    y = jnp.exp(-jnp.abs(x))
    u = 1.0 + y
    l1p = jnp.where(u == 1.0, y, jnp.log(u) * (y / jnp.where(u == 1.0, 1.0, u - 1.0)))
    return jnp.minimum(x, 0.0) - l1p


def _rowsum(v):
    return jnp.sum(v, axis=0, keepdims=True)


def _layer_norm_stats(z):
    mu = jnp.mean(z, axis=-1, keepdims=True)
    zc = z - mu
    var = jnp.mean(zc * zc, axis=-1, keepdims=True)
    rstd = lax.rsqrt(var + LN_EPS)
    return zc * rstd, rstd


def _layer_norm_bwd(dy, xhat, rstd, g):
    dxh = dy * g
    m1 = jnp.mean(dxh, axis=-1, keepdims=True)
    m2 = jnp.mean(dxh * xhat, axis=-1, keepdims=True)
    return rstd * (dxh - m1 - xhat * m2)


def _shifted(v, before8, after8, offsets):
    n = v.shape[0]
    ext = jnp.concatenate([before8, v, after8], axis=0)
    total = n + 2 * SUBLANES
    return [pltpu.roll(ext, (-k) % total, 0)[SUBLANES:SUBLANES + n] for k in offsets]


def _rows8(row):
    return jnp.broadcast_to(row, (SUBLANES, row.shape[1]))


def _shift_down(v, first_row):
    return _shifted(v, _rows8(first_row), _rows8(first_row), [-1])[0]


def _shift_up(v, last_row):
    return _shifted(v, _rows8(last_row), _rows8(last_row), [1])[0]


def _all_gather(blocks, name):
    n = len(blocks)

    def body(*refs):
        x_refs, out_refs = refs[:n], refs[n:2 * n]
        send_sems, recv_sems, local_sems = refs[2 * n:]
        x, y, c = _my_pos()
        me, sibling = (x, y, c), (x, y, 1 - c)
        chips = [(1 - x, y), (x, 1 - y), (1 - x, 1 - y)]

        def slot(a, px, py, pc):
            return out_refs[a].at[4 * px + 2 * py + pc]

        def copy(a, k, block, to, src=None):
            return pltpu.make_async_remote_copy(
                src_ref=slot(a, *block) if src is None else src, dst_ref=slot(a, *block),
                send_sem=send_sems.at[a, k], recv_sem=recv_sems.at[a, k], device_id=to, device_id_type=MESH)

        mine = [pltpu.make_async_copy(x_refs[a], slot(a, *me), local_sems.at[a]) for a in range(n)]
        for cp in mine:
            cp.start()
        first = []
        for a in range(n):
            first.append(copy(a, 0, me, sibling, src=x_refs[a]))
            first += [copy(a, 1 + j, me, (*chip, c), src=x_refs[a]) for j, chip in enumerate(chips)]
        for cp in first:
            cp.start()
        passed = []
        for j, chip in enumerate(chips):
            for a in range(n):
                copy(a, 1 + j, (*chip, c), me).wait_recv()
                fwd = copy(a, 4 + j, (*chip, c), sibling)
                fwd.start()
                passed.append(fwd)
        for a in range(n):
            copy(a, 0, sibling, me).wait_recv()
            for j, chip in enumerate(chips):
                copy(a, 4 + j, (*chip, 1 - c), me).wait_recv()
        for cp in first + passed:
            cp.wait_send()
        for cp in mine:
            cp.wait()

    outs = pl.pallas_call(
        body, name=name,
        out_shape=[jax.ShapeDtypeStruct((N_DEV,) + b.shape, b.dtype) for b in blocks],
        in_specs=[ANY] * n, out_specs=[ANY] * n,
        scratch_shapes=[pltpu.SemaphoreType.DMA((n, 7)), pltpu.SemaphoreType.DMA((n, 7)),
                        pltpu.SemaphoreType.DMA((n,))],
    )(*blocks)
    return list(outs)


def _sibling_exchange(bufs, name):
    n = len(bufs)

    def body(*refs):
        srcs, outs = refs[:n], refs[n:2 * n]
        send_sems, recv_sems = refs[2 * n:]
        x, y, c = _my_pos()
        copies = [pltpu.make_async_remote_copy(
            src_ref=srcs[a].at[2 * j + (1 - c)], dst_ref=outs[a].at[j], send_sem=send_sems.at[a, j],
            recv_sem=recv_sems.at[a, j], device_id=(x, y, 1 - c), device_id_type=MESH)
            for a in range(n) for j in range(4)]
        for cp in copies:
            cp.start()
        for cp in copies:
            cp.wait()

    outs = pl.pallas_call(
        body, name=name, out_shape=[jax.ShapeDtypeStruct((4,) + b.shape[1:], b.dtype) for b in bufs],
        in_specs=[ANY] * n, out_specs=[ANY] * n,
        scratch_shapes=[pltpu.SemaphoreType.DMA((n, 4)), pltpu.SemaphoreType.DMA((n, 4))],
    )(*bufs)
    return list(outs)


def _chip_exchange(parts, name):
    n = len(parts)

    def body(*refs):
        srcs, outs = refs[:n], refs[n:2 * n]
        send_sems, recv_sems, local_sems = refs[2 * n:]
        x, y, c = _my_pos()
        jme = 2 * x + y
        peers = [(1 - x, y), (x, 1 - y), (1 - x, 1 - y)]
        local = [pltpu.make_async_copy(srcs[a].at[jme], outs[a].at[jme], local_sems.at[a]) for a in range(n)]
        for cp in local:
            cp.start()

        def copy(a, k, px, py, dst_slot):
            return pltpu.make_async_remote_copy(
                src_ref=srcs[a].at[2 * px + py], dst_ref=outs[a].at[dst_slot], send_sem=send_sems.at[a, k],
                recv_sem=recv_sems.at[a, k], device_id=(px, py, c), device_id_type=MESH)

        sends = [copy(a, k, px, py, jme) for a in range(n) for k, (px, py) in enumerate(peers)]
        for cp in sends:
            cp.start()
        for a in range(n):
            for k, (px, py) in enumerate(peers):
                copy(a, k, px, py, 2 * px + py).wait_recv()
        for cp in sends:
            cp.wait_send()
        for cp in local:
            cp.wait()

    outs = pl.pallas_call(
        body, name=name, out_shape=[jax.ShapeDtypeStruct(p.shape, p.dtype) for p in parts],
        in_specs=[ANY] * n, out_specs=[ANY] * n,
        scratch_shapes=[pltpu.SemaphoreType.DMA((n, 3)), pltpu.SemaphoreType.DMA((n, 3)),
                        pltpu.SemaphoreType.DMA((n,))],
    )(*parts)
    return list(outs)


_SIDE_REMOTE = {"gather": 7, "sibling": 4, "chips": 3}
_FLIPS = [(0, 0, 1), (1, 0, 0), (0, 1, 0), (1, 1, 0), (1, 0, 1), (0, 1, 1), (1, 1, 1)]


def _side_plan(sides):
    inputs, out_shapes, scratch = [], [], []
    for kind, arrays in sides:
        n = len(arrays)
        for a in arrays:
            inputs.append(a)
            shape = {"gather": (N_DEV,) + a.shape, "sibling": (4,) + a.shape[1:], "chips": a.shape}[kind]
            out_shapes.append(jax.ShapeDtypeStruct(shape, a.dtype))
        scratch += [pltpu.SemaphoreType.DMA((n, _SIDE_REMOTE[kind])), pltpu.SemaphoreType.DMA((n, _SIDE_REMOTE[kind])),
                    pltpu.SemaphoreType.DMA((n,))]
    return inputs, out_shapes, scratch


def _side_copies(sides, in_refs, out_refs, sem_refs):
    x, y, c = _my_pos()
    starts, waits = [], []
    pos = 0
    for s, (kind, arrays) in enumerate(sides):
        send_sems, recv_sems, local_sems = sem_refs[3 * s:3 * s + 3]
        for a in range(len(arrays)):
            src, out = in_refs[pos], out_refs[pos]
            pos += 1

            def remote(k, src_ref, dst_ref, to):
                return pltpu.make_async_remote_copy(src_ref=src_ref, dst_ref=dst_ref, send_sem=send_sems.at[a, k],
                                                    recv_sem=recv_sems.at[a, k], device_id=to, device_id_type=MESH)

            def local(src_ref, dst_ref):
                cp = pltpu.make_async_copy(src_ref, dst_ref, local_sems.at[a])
                starts.append(cp.start)
                waits.append(cp.wait)

            if kind == "gather":
                me = 4 * x + 2 * y + c
                local(src, out.at[me])
                for k, (fx, fy, fc) in enumerate(_FLIPS):
                    px, py, pc = (1 - x if fx else x), (1 - y if fy else y), (1 - c if fc else c)
                    send = remote(k, src, out.at[me], (px, py, pc))
                    starts.append(send.start)
                    waits += [remote(k, src, out.at[4 * px + 2 * py + pc], (px, py, pc)).wait_recv, send.wait_send]
            elif kind == "sibling":
                for j in range(4):
                    cp = remote(j, src.at[2 * j + (1 - c)], out.at[j], (x, y, 1 - c))
                    starts.append(cp.start)
                    waits.append(cp.wait)
            else:
                jme = 2 * x + y
                local(src.at[jme], out.at[jme])
                for k, (px, py) in enumerate([(1 - x, y), (x, 1 - y), (1 - x, 1 - y)]):
                    send = remote(k, src.at[2 * px + py], out.at[jme], (px, py, c))
                    starts.append(send.start)
                    waits += [remote(k, src.at[2 * px + py], out.at[2 * px + py], (px, py, c)).wait_recv,
                              send.wait_send]
    return starts, waits


def _call_with_sides(body, sides, *, name, grid, in_specs, out_specs, out_shape, scratch_shapes, compiler_params, args):
    if not sides:
        res = pl.pallas_call(body, name=name, grid=grid, in_specs=in_specs, out_specs=out_specs, out_shape=out_shape,
                             scratch_shapes=scratch_shapes, compiler_params=compiler_params)(*args)
        return list(res), []
    s_in, s_out, s_scr = _side_plan(sides)
    n_in, n_out, n_scr, n_side = len(in_specs), len(out_specs), len(scratch_shapes), len(s_in)

    def wrapped(*refs):
        refs = list(refs)
        ins, side_in = refs[:n_in], refs[n_in:n_in + n_side]
        outs = refs[n_in + n_side:n_in + n_side + n_out]
        side_out = refs[n_in + n_side + n_out:n_in + 2 * n_side + n_out]
        rest = refs[n_in + 2 * n_side + n_out:]
        starts, waits = _side_copies(sides, side_in, side_out, rest[n_scr:])
        first = functools.reduce(jnp.logical_and, [pl.program_id(d) == 0 for d in range(len(grid))])
        last = functools.reduce(jnp.logical_and, [pl.program_id(d) == grid[d] - 1 for d in range(len(grid))])

        @pl.when(first)
        def _():
            for start in starts:
                start()

        body(*ins, *outs, *rest[:n_scr])

        @pl.when(last)
        def _():
            for wait in waits:
                wait()

    res = pl.pallas_call(
        wrapped, name=name, grid=grid, in_specs=list(in_specs) + [ANY] * n_side,
        out_specs=list(out_specs) + [ANY] * n_side, out_shape=list(out_shape) + s_out,
        scratch_shapes=list(scratch_shapes) + s_scr, compiler_params=compiler_params,
    )(*args, *s_in)
    return list(res[:n_out]), list(res[n_out:])


def _row_tile(r, l):
    t = min(r, max(16, ELEMENTWISE_TILE_BYTES // (4 * l) // 16 * 16))
    while r % t:
        t -= 16
    return t


def _pair_sum(buf, recv, core, name):
    _, r, l = buf.shape
    tr = _row_tile(r, l)

    def body(core_ref, a_ref, b_ref, o_ref):
        o_ref[...] = (a_ref[...] + b_ref[...]).astype(WIRE_DTYPE)

    return pl.pallas_call(
        body, name=name, out_shape=jax.ShapeDtypeStruct((4, r, l), WIRE_DTYPE),
        grid_spec=pltpu.PrefetchScalarGridSpec(
            num_scalar_prefetch=1, grid=(4, r // tr),
            in_specs=[pl.BlockSpec((None, tr, l), lambda j, i, cr: (2 * j + cr[0], i, 0)),
                      pl.BlockSpec((None, tr, l), lambda j, i, cr: (j, i, 0))],
            out_specs=pl.BlockSpec((None, tr, l), lambda j, i, cr: (j, i, 0))),
        compiler_params=_cparams(dimension_semantics=("arbitrary", "arbitrary")),
    )(core, buf, recv)


def _sum_parts(p_ref):
    return ((p_ref[0].astype(F32) + p_ref[1].astype(F32)) + (p_ref[2].astype(F32) + p_ref[3].astype(F32)))


def _sum4(parts, name):
    _, r, l = parts.shape
    tr = _row_tile(r, l)

    def body(p_ref, o_ref):
        o_ref[...] = _sum_parts(p_ref)

    return pl.pallas_call(
        body, name=name, out_shape=jax.ShapeDtypeStruct((r, l), F32), grid=(r // tr,),
        in_specs=[pl.BlockSpec((4, tr, l), lambda i: (0, i, 0))],
        out_specs=pl.BlockSpec((tr, l), lambda i: (i, 0)),
        compiler_params=_cparams(dimension_semantics=("arbitrary",)),
    )(parts)


def _adamw_update(w, gg, m, v):
    nm = ADAM_B1 * m + (1.0 - ADAM_B1) * gg
    nv = ADAM_B2 * v + (1.0 - ADAM_B2) * (gg * gg)
    m_hat = nm / (1.0 - ADAM_B1 ** ADAM_STEP)
    v_hat = nv / (1.0 - ADAM_B2 ** ADAM_STEP)
    return -ADAM_LR * (m_hat / (jnp.sqrt(v_hat) + ADAM_EPS) + ADAM_WD * w), nm, nv


def _adamw(w, g, m, v, name):
    r, l = w.shape
    tr = _row_tile(r, l)

    def body(w_ref, g_ref, m_ref, v_ref, d_ref, nm_ref, nv_ref):
        d_ref[...], nm_ref[...], nv_ref[...] = _adamw_update(w_ref[...], g_ref[...], m_ref[...], v_ref[...])

    spec = pl.BlockSpec((tr, l), lambda i: (i, 0))
    return pl.pallas_call(
        body, name=name, out_shape=[jax.ShapeDtypeStruct((r, l), F32)] * 3, grid=(r // tr,),
        in_specs=[spec] * 4, out_specs=[spec] * 3,
        compiler_params=_cparams(dimension_semantics=("arbitrary",)),
    )(w, g, m, v)


def _adamw_parts(w, parts, m, v, name):
    nl, r, l = w.shape
    tr = _row_tile(r, l)

    def body(*refs):
        w_ref, p_refs, (m_ref, v_ref, g_ref, d_ref, nm_ref, nv_ref) = refs[0], refs[1:1 + nl], refs[1 + nl:]
        layer = pl.program_id(0)
        gg = _sum_parts(p_refs[0])
        for q in range(1, nl):
            gg = jnp.where(layer == q, _sum_parts(p_refs[q]), gg)
        g_ref[...] = gg
        d_ref[...], nm_ref[...], nv_ref[...] = _adamw_update(w_ref[...], gg, m_ref[...], v_ref[...])

    spec = pl.BlockSpec((None, tr, l), lambda q, i: (q, i, 0))
    pspecs = [pl.BlockSpec((4, tr, l), lambda q, i, k=k: (0, jnp.where(q == k, i, 0), 0)) for k in range(nl)]
    return pl.pallas_call(
        body, name=name, out_shape=[jax.ShapeDtypeStruct((nl, r, l), F32)] * 4, grid=(nl, r // tr),
        in_specs=[spec] + pspecs + [spec, spec], out_specs=[spec] * 4,
        compiler_params=_cparams(dimension_semantics=("arbitrary", "arbitrary")),
    )(w, *parts, m, v)


def _to_rows(pieces, row_multiple):
    flat = jnp.concatenate([p.reshape(-1) for p in pieces])
    rows = -(-flat.shape[0] // LANES)
    rows = -(-rows // row_multiple) * row_multiple
    flat = jnp.pad(flat, (0, rows * LANES - flat.shape[0]))
    return flat.reshape(rows, LANES)


def _split_rows(rows, shapes):
    flat = rows.reshape(-1)
    out, off = [], 0
    for s in shapes:
        n = int(np.prod(s))
        out.append(flat[off:off + n].reshape(s))
        off += n
    return out


def _mod_fwd(cond, w_mod, b_my, name):
    nl, _, ncol = w_mod.shape

    def body(a_ref, w_ref, b_ref, o_ref):
        a = a_ref[...]
        s = a * _sigmoid(a)
        for i in range(nl):
            o_ref[i] = _dot(s, w_ref[i]) + b_ref[i]

    return pl.pallas_call(
        body, name=name, out_shape=jax.ShapeDtypeStruct((nl, 16, ncol), F32),
        compiler_params=_cparams(),
    )(cond, w_mod, b_my)


def _mod_bwd(cond, dm_all, dm_my, w_mod, name):
    nl, _, ncol = w_mod.shape

    def body(a_ref, dma_ref, dmm_ref, w_ref, gw_ref, gb_ref, gc_ref):
        a = a_ref[...]
        sg = _sigmoid(a)
        s = a * sg
        for i in range(nl):
            gw_ref[i] = _dot_tn(s, dmm_ref[i])
            gb_ref[i] = jnp.sum(dma_ref[i], axis=0, keepdims=True)
        back = _dot_nt(dmm_ref[0], w_ref[0])
        dsilu = sg * (1.0 + a * (1.0 - sg))
        gc_ref[...] = jnp.sum(back[8:16] * dsilu[8:16], axis=0, keepdims=True)

    return pl.pallas_call(
        body, name=name,
        out_shape=[jax.ShapeDtypeStruct((nl, D_MODEL, ncol), F32), jax.ShapeDtypeStruct((nl, 1, 3 * D_MODEL), F32),
                   jax.ShapeDtypeStruct((1, D_MODEL), F32)],
        compiler_params=_cparams(),
    )(cond, dm_all, dm_my, w_mod)


def _in_proj(xt, sc, sh, wg, name, sides=()):
    t = xt.shape[0]
    tm = min(TM_MM, t)

    def body(x_ref, sc_ref, sh_ref, w_ref, u_ref, g_ref):
        h = (x_ref[...] * (1.0 + sc_ref[...]) + sh_ref[...]).astype(MXU_DTYPE)
        for k in range(N_WBLK):
            o = jnp.dot(h, w_ref[k], preferred_element_type=F32)
            if k < N_WBLK // 2:
                u_ref[:, k * WBLK:(k + 1) * WBLK] = o
            else:
                kk = k - N_WBLK // 2
                g_ref[:, kk * WBLK:(kk + 1) * WBLK] = o.astype(ACT_DTYPE)

    row = pl.BlockSpec((1, D_MODEL), lambda i: (0, 0))
    return _call_with_sides(
        body, sides, name=name,
        out_shape=[jax.ShapeDtypeStruct((t, D_INNER), F32), jax.ShapeDtypeStruct((t, D_INNER), ACT_DTYPE)],
        grid=(t // tm,),
        in_specs=[pl.BlockSpec((tm, D_MODEL), lambda i: (i, 0)), row, row,
                  pl.BlockSpec((N_WBLK, D_MODEL, WBLK), lambda i: (0, 0, 0), pipeline_mode=pl.Buffered(1))],
        out_specs=[pl.BlockSpec((tm, D_INNER), lambda i: (i, 0))] * 2, scratch_shapes=[],
        compiler_params=_cparams(dimension_semantics=("arbitrary",)), args=[xt, sc, sh, wg])


def _halo_maps(nt, tm, n_blocks, pos, rows=SUBLANES):
    per = tm // rows
    prev = lambda cb, i: (jnp.maximum(pos(i) * per - 1, 0), cb)
    nxt = lambda cb, i: (jnp.minimum((pos(i) + 1) * per, n_blocks - 1), cb)
    return prev, nxt


def _conv_taps(u, prev8, next8, is_first, is_last):
    pz = jnp.where(is_first, 0.0, 1.0)
    nz = jnp.where(is_last, 0.0, 1.0)
    return _shifted(u, prev8 * pz, next8 * nz, [-2, -1, 1])


def _lru_gates(uv, wa_ref, wx_ref, ba, bx, cl, g):
    sl = slice(g * LANES, (g + 1) * LANES)
    uvg = uv[:, sl]
    r = 0.5 * jnp.tanh(_dot(uvg, wa_ref[g]) + ba[:, sl]) + 0.5
    ii = 0.5 * jnp.tanh(_dot(uvg, wx_ref[g]) + bx[:, sl]) + 0.5
    la = cl[:, sl] * r
    a = jnp.exp(la)
    q = jnp.tanh(-la) * (1.0 + a * a)
    rs = lax.rsqrt(jnp.maximum(q, SQRT_FLOOR))
    return uvg, r, ii, a, q * rs, rs


def _scan_rows(seg):
    return -(-(SCAN_ROW_T * (seg - 1) + SCAN_ROW_J * (N_SEG - 1) + 1) // SUBLANES) * SUBLANES


def _seg_chunk(j, c):
    return pl.ds(SCAN_ROW_T * SUBLANES * c + SCAN_ROW_J * j, SUBLANES, stride=SCAN_ROW_T)


def _seg_scatter(ref, g, seg, value):
    for j in range(N_SEG):
        for c in range(seg // SUBLANES):
            r0 = j * seg + SUBLANES * c
            ref[g, _seg_chunk(j, c), :] = value[r0:r0 + SUBLANES]


def _scan_tile(a_s, b_s, carry_ref, write_out, seg, reverse, chunks_per_write=1):
    n_g = a_s.shape[0]
    unroll = SCAN_UNROLL if seg % SCAN_UNROLL == 0 else 1

    n_trips = seg // unroll

    def steps(k, state):
        hs, cs = list(state[0]), list(state[1])
        base = ((n_trips - 1 - k) if reverse else k) * unroll
        for q in (range(unroll - 1, -1, -1) if reverse else range(unroll)):
            t = base + q
            rows = pl.ds(t * SCAN_ROW_T, N_SEG, stride=SCAN_ROW_J)
            for g in range(n_g):
                a = a_s[g, rows, :]
                b = b_s[g, rows, :]
                hs[g] = a * hs[g] + b
                cs[g] = a * cs[g]
                b_s[g, rows, :] = hs[g]
                a_s[g, rows, :] = cs[g]
        return tuple(hs), tuple(cs)

    zeros = tuple(jnp.zeros((N_SEG, LANES), F32) for _ in range(n_g))
    ones = tuple(jnp.ones((N_SEG, LANES), F32) for _ in range(n_g))
    h_fin, a_fin = lax.fori_loop(0, seg // unroll, steps, (zeros, ones))

    order = list(range(N_SEG - 1, -1, -1)) if reverse else list(range(N_SEG))
    for g in range(n_g):
        carry = carry_ref[:, g * LANES:(g + 1) * LANES]
        for j in order:
            for c0 in range(0, seg // SUBLANES, chunks_per_write):
                parts = [b_s[g, _seg_chunk(j, c), :] + a_s[g, _seg_chunk(j, c), :] * carry
                         for c in range(c0, c0 + chunks_per_write)]
                write_out(j, c0, g, parts[0] if chunks_per_write == 1 else jnp.concatenate(parts, axis=0))
            carry = a_fin[g][j:j + 1] * carry + h_fin[g][j:j + 1]
        carry_ref[:, g * LANES:(g + 1) * LANES] = carry


def _lru_specs(s, tm, cb, direction_pos, nt):
    n_rows8 = s // SUBLANES
    prev, nxt = _halo_maps(nt, tm, n_rows8, direction_pos)
    tile = pl.BlockSpec((tm, cb), lambda c, i: (direction_pos(i), c))
    return tile, pl.BlockSpec((SUBLANES, cb), prev), pl.BlockSpec((SUBLANES, cb), nxt)


def _lru_param_specs(cb, d):
    n_g = cb // LANES
    vec = pl.BlockSpec((1, cb), lambda c, i: (0, c))
    dvec = pl.BlockSpec((None, 1, cb), lambda c, i: (d, 0, c))
    wmat = pl.BlockSpec((None, n_g, LRU_BLOCK, LRU_BLOCK), lambda c, i: (d, c, 0, 0))
    return vec, dvec, wmat


def _lru_fwd(src, h0, p, d, name, conv, sides=()):
    s = src.shape[0]
    tm = min(TM_LRU, s)
    cb = CB_LRU_FWD
    n_g = cb // LANES
    nt = s // tm
    seg = tm // N_SEG
    pos = (lambda i: i) if d == 0 else (lambda i: nt - 1 - i)

    def body(*refs):
        refs = list(refs)
        u_ref = refs.pop(0)
        if conv:
            up_ref, un_ref, cw_ref, cbias_ref = [refs.pop(0) for _ in range(4)]
        wa_ref, wx_ref, ba_ref, bx_ref, lam_ref, h0_ref, h_ref, hc_ref = [refs.pop(0) for _ in range(8)]
        uv_ref = refs.pop(0) if conv else None
        a_s, b_s = refs
        i = pl.program_id(1)
        tp = pos(i)

        @pl.when(i == 0)
        def _():
            hc_ref[...] = h0_ref[...]

        if conv:
            u_t = u_ref[...]
            um2, um1, up1 = _conv_taps(u_t, up_ref[...], un_ref[...], tp == 0, tp == nt - 1)
            cw = cw_ref[...]
            uv_ref[...] = um2 * cw[0:1] + um1 * cw[1:2] + u_t * cw[2:3] + up1 * cw[3:4] + cbias_ref[...]
        src_ref = uv_ref if conv else u_ref
        cl = LRU_C * _log_sigmoid(lam_ref[...])
        ba, bx = ba_ref[...], bx_ref[...]
        for g in range(n_g):
            uvg, r, ii, a, sq, _ = _lru_gates(src_ref, wa_ref, wx_ref, ba, bx, cl, g)
            b = sq * (ii * uvg)
            _seg_scatter(a_s, g, seg, a)
            _seg_scatter(b_s, g, seg, b)

        per_write = 2 if (seg // SUBLANES) % 2 == 0 else 1

        def write_out(j, c, g, h):
            h_ref[pl.ds(j * seg + SUBLANES * c, SUBLANES * per_write), pl.ds(g * LANES, LANES)] = h.astype(ACT_DTYPE)

        _scan_tile(a_s, b_s, hc_ref, write_out, seg, reverse=(d == 1), chunks_per_write=per_write)

    tile, prev, nxt = _lru_specs(s, tm, cb, pos, nt)
    vec, dvec, wmat = _lru_param_specs(cb, d)
    wide = jax.ShapeDtypeStruct((s, D_INNER), F32)
    conv_specs = [prev, nxt, pl.BlockSpec((4, cb), lambda c, i: (0, c)), vec] if conv else []
    conv_args = [src, src, p["conv_w"], p["conv_b"]] if conv else []
    return _call_with_sides(
        body, sides, name=name,
        out_shape=[jax.ShapeDtypeStruct((s, D_INNER), ACT_DTYPE), jax.ShapeDtypeStruct((1, D_INNER), F32)]
        + ([wide] if conv else []),
        grid=(D_INNER // cb, nt),
        in_specs=[tile] + conv_specs + [wmat, wmat, dvec, dvec, dvec, vec],
        out_specs=[tile, vec] + ([tile] if conv else []),
        scratch_shapes=[pltpu.VMEM((n_g, _scan_rows(seg), LANES), F32)] * 2,
        compiler_params=_cparams(dimension_semantics=("arbitrary", "arbitrary")),
        args=[src, *conv_args, p["wa"], p["wx"], p["ba"], p["bx"], p["lam"], h0])


def _lru_bwd(uv, dh, h, h0, lam_in, p, d, name, sides=()):
    s = uv.shape[0]
    tm = min(TM_LRU, s)
    cb = CB_LRU
    n_g = cb // LANES
    nt = s // tm
    seg = tm // N_SEG
    pos = (lambda i: nt - 1 - i) if d == 0 else (lambda i: i)

    def body(uv_ref, dh_ref, h_ref, hh_ref, wa_ref, wx_ref, ba_ref, bx_ref,
             lam_ref, h0_ref, lin_ref, duv_ref, gwa_ref, gwx_ref, gv_ref, lc_ref, a_s, b_s, lp_s,
             r_s, i_s, q_s, rq_s, a_keep):
        i = pl.program_id(1)
        tp = pos(i)

        @pl.when(i == 0)
        def _():
            lc_ref[...] = lin_ref[...]
            gwa_ref[...] = jnp.zeros_like(gwa_ref)
            gwx_ref[...] = jnp.zeros_like(gwx_ref)
            gv_ref[...] = jnp.zeros_like(gv_ref)

        uv = uv_ref[...]
        lam = lam_ref[...]
        cl = LRU_C * _log_sigmoid(lam)
        ba, bx = ba_ref[...], bx_ref[...]
        dh_t = dh_ref[...].astype(F32)
        carry_in = lc_ref[...]
        for g in range(n_g):
            sl = slice(g * LANES, (g + 1) * LANES)
            _, r, ii, a, sq, rs = _lru_gates(uv, wa_ref, wx_ref, ba, bx, cl, g)
            for ref, val in ((r_s, r), (i_s, ii), (q_s, sq), (rq_s, rs), (a_keep, a)):
                ref[:, sl] = val.astype(ACT_DTYPE)
            b = a * dh_t[:, sl]
            _seg_scatter(a_s, g, seg, a)
            _seg_scatter(b_s, g, seg, b)

        def write_out(j, c, g, v):
            lp_s[pl.ds(j * seg + SUBLANES * c, SUBLANES), pl.ds(g * LANES, LANES)] = v

        _scan_tile(a_s, b_s, lc_ref, write_out, seg, reverse=(d == 0))

        h_t = h_ref[...].astype(F32)
        hh = hh_ref[...].astype(F32)
        if d == 0:
            edge = jnp.where(tp == 0, h0_ref[...], hh[H_HALO - 1:H_HALO])
            h_prev = _shift_down(h_t, edge)
            lam_t = dh_t + _shift_up(lp_s[...], carry_in)
        else:
            edge = jnp.where(tp == nt - 1, h0_ref[...], hh[0:1])
            h_prev = _shift_up(h_t, edge)
            lam_t = dh_t + _shift_down(lp_s[...], carry_in)

        dsig = LRU_C * _sigmoid(-lam)
        cl2 = cl + cl
        for g in range(n_g):
            sl = slice(g * LANES, (g + 1) * LANES)
            uvg = uv[:, sl]
            r, ii, a, sq, rs =[ref[:, sl].astype(F32) for ref in (r_s, i_s, a_keep, q_s, rq_s)]
            lt = lam_t[:, sl]
            ls = lt * sq
            dla = (lt * a) * (h_prev[:, sl] - (ii * uvg) * (a * rs))
            dzr = (dla * cl2[:, sl]) * r * (1.0 - r)
            dzi = ((ls + ls) * uvg) * ii * (1.0 - ii)
            duv_ref[:, sl] = (ls * ii + _dot_nt(dzr, wa_ref[g]) + _dot_nt(dzi, wx_ref[g])).astype(ACT_DTYPE)
            gwa_ref[g] += _dot_tn(uvg, dzr)
            gwx_ref[g] += _dot_tn(uvg, dzi)
            gv_ref[0:1, sl] += _rowsum(dzr)
            gv_ref[1:2, sl] += _rowsum(dzi)
            gv_ref[2:3, sl] += _rowsum(dla * r) * dsig[:, sl]

        @pl.when(i == nt - 1)
        def _():
            gwa_ref[...] = 0.5 * gwa_ref[...]
            gwx_ref[...] = 0.5 * gwx_ref[...]
            gv_ref[0:2, :] = 0.5 * gv_ref[0:2, :]

    tile, _, _ = _lru_specs(s, tm, cb, pos, nt)
    vec, dvec, wmat = _lru_param_specs(cb, d)
    h_prev_map, h_next_map = _halo_maps(nt, tm, s // H_HALO, pos, rows=H_HALO)
    hh_spec = pl.BlockSpec((H_HALO, cb), h_prev_map if d == 0 else h_next_map)
    gw_spec = pl.BlockSpec((n_g, LRU_BLOCK, LRU_BLOCK), lambda c, i: (c, 0, 0))
    n_blk = D_INNER // LRU_BLOCK
    return _call_with_sides(
        body, sides, name=name,
        out_shape=[jax.ShapeDtypeStruct((s, D_INNER), ACT_DTYPE),
                   jax.ShapeDtypeStruct((n_blk, LRU_BLOCK, LRU_BLOCK), F32),
                   jax.ShapeDtypeStruct((n_blk, LRU_BLOCK, LRU_BLOCK), F32),
                   jax.ShapeDtypeStruct((SUBLANES, D_INNER), F32),
                   jax.ShapeDtypeStruct((1, D_INNER), F32)],
        grid=(D_INNER // cb, nt),
        in_specs=[tile, tile, tile, hh_spec, wmat, wmat, dvec, dvec, dvec, vec, vec],
        out_specs=[tile, gw_spec, gw_spec, pl.BlockSpec((SUBLANES, cb), lambda c, i: (0, c)), vec],
        scratch_shapes=[pltpu.VMEM((n_g, _scan_rows(seg), LANES), F32)] * 2 + [pltpu.VMEM((tm, cb), F32)]
        + [pltpu.VMEM((tm, cb), ACT_DTYPE)] * 5,
        compiler_params=_cparams(dimension_semantics=("arbitrary", "arbitrary")),
        args=[uv, dh, h, h, p["wa"], p["wx"], p["ba"], p["bx"], p["lam"], h0, lam_in])


def _out0(hf, hb, g, xt, gt, wo, lg, lb, sc1, sh1, wg1, name):
    t = xt.shape[0]
    tm = min(TM_MM, t)

    def body(hf_ref, hb_ref, g_ref, x_ref, gt_ref, w_ref, lg_ref, lb_ref, sc1_ref, sh1_ref, w1_ref,
             x1_ref, br_ref, u1_ref, g1_ref):
        br = None
        for k in range(D_INNER // WBLK):
            sl = slice(k * WBLK, (k + 1) * WBLK)
            gg = g_ref[:, sl].astype(F32)
            p = (hf_ref[:, sl].astype(F32) + hb_ref[:, sl].astype(F32)) * (gg * _sigmoid(gg))
            part = _dot(p, w_ref[sl, :])
            br = part if br is None else br + part
        z = ALPHA * x_ref[...] + gt_ref[...] * br
        xhat, _ = _layer_norm_stats(z)
        x1 = xhat * lg_ref[...] + lb_ref[...]
        x1_ref[...] = x1
        br_ref[...] = br.astype(ACT_DTYPE)
        h1 = (x1 * (1.0 + sc1_ref[...]) + sh1_ref[...]).astype(MXU_DTYPE)
        for k in range(N_WBLK):
            o = jnp.dot(h1, w1_ref[k], preferred_element_type=F32).astype(ACT_DTYPE)
            if k < N_WBLK // 2:
                u1_ref[:, k * WBLK:(k + 1) * WBLK] = o
            else:
                kk = k - N_WBLK // 2
                g1_ref[:, kk * WBLK:(kk + 1) * WBLK] = o

    wide = pl.BlockSpec((tm, D_INNER), lambda i: (i, 0))
    nar = pl.BlockSpec((tm, D_MODEL), lambda i: (i, 0))
    row = pl.BlockSpec((1, D_MODEL), lambda i: (0, 0))
    return pl.pallas_call(
        body, name=name,
        out_shape=[jax.ShapeDtypeStruct((t, D_MODEL), F32), jax.ShapeDtypeStruct((t, D_MODEL), ACT_DTYPE),
                   jax.ShapeDtypeStruct((t, D_INNER), ACT_DTYPE), jax.ShapeDtypeStruct((t, D_INNER), ACT_DTYPE)],
        grid=(t // tm,),
        in_specs=[wide, wide, wide, nar, row,
                  pl.BlockSpec((D_INNER, D_MODEL), lambda i: (0, 0), pipeline_mode=pl.Buffered(1)), row, row, row, row,
                  pl.BlockSpec((N_WBLK, D_MODEL, WBLK), lambda i: (0, 0, 0), pipeline_mode=pl.Buffered(1))],
        out_specs=[nar, nar, wide, wide],
        compiler_params=_cparams(dimension_semantics=("arbitrary",)),
    )(hf, hb, g, xt, gt, wo, lg, lb, sc1, sh1, wg1)


def _unrolled_loop(n, fn, unroll=4):
    while n % unroll:
        unroll //= 2

    def trip(k, carry):
        for q in range(unroll):
            fn(k * unroll + q)
        return carry
    lax.fori_loop(0, n // unroll, trip, 0)


def _window(n, w):
    t = np.arange(n)
    return np.clip(t - w // 2, 0, n), np.clip(t + w // 2, 0, n)


def _pool_tables(n_rows, transpose):
    boxes, inv_c, inv_r = [], [], []
    for w in POOL_WINDOWS:
        lo, hi = _window(GRID_W, w)
        m = np.zeros((GRID_W, GRID_W), np.float32)
        for r in range(GRID_W):
            m[r, lo[r]:hi[r]] = 1.0
        m = np.kron(np.eye(POOL_TOK // GRID_W, dtype=np.float32), m)
        boxes.append(m.T if transpose else m)
        inv_c.append(np.broadcast_to((1.0 / (hi - lo).astype(np.float32))[:, None], (GRID_W, LANES)))
        lo_r, hi_r = _window(n_rows, w)
        inv_r.append(1.0 / (hi_r - lo_r).astype(np.float32))
    return (jnp.asarray(np.stack(boxes), MXU_DTYPE), jnp.asarray(np.stack(inv_c), F32),
            jnp.asarray(np.stack(inv_r), F32))


def _pool_mix(xin, transpose, out_dtype, name):
    s = xin.shape[0]
    n_rows = s // GRID_W
    pad_t = SUBLANES * GRID_W
    rows_per_blk = POOL_TOK // GRID_W
    n_slab = D_INNER // LANES
    slabs_per_group = POOL_GROUP // LANES
    n_win = len(POOL_WINDOWS)
    boxes, inv_c, inv_r = _pool_tables(n_rows, transpose)
    exact_operand = (not transpose) and xin.dtype == MXU_DTYPE and MXU_DTYPE != F32

    def body(invr_ref, box_ref, invc_ref, x_ref, o_ref, pad_s):
        k = pl.program_id(0) // slabs_per_group
        pad_s[pl.ds(0, pad_t), :] = jnp.zeros((pad_t, LANES), F32)
        pad_s[pl.ds(pad_t + s, pad_t), :] = jnp.zeros((pad_t, LANES), F32)

        for kk, w in enumerate(POOL_WINDOWS):
            half = w // 2
            offsets = list(range(-(half - 1), half + 1)) if transpose else list(range(-half, half))

            @pl.when(k == kk)
            def _():
                inv_col = invc_ref[kk]

                def col_box(b):
                    st = pl.multiple_of(b * POOL_TOK, POOL_TOK)
                    xb = x_ref[pl.ds(st, POOL_TOK), :]
                    if exact_operand:
                        pad_s[pl.ds(pad_t + st, POOL_TOK), :] = jnp.dot(box_ref[kk], xb, preferred_element_type=F32)
                        return
                    xb = xb.astype(F32)
                    if transpose:
                        xb = xb * jnp.concatenate(
                            [inv_col * invr_ref[kk, b * rows_per_blk + q] for q in range(rows_per_blk)], axis=0)
                    hi = xb.astype(MXU_DTYPE)
                    lo = (xb - hi.astype(F32)).astype(MXU_DTYPE)
                    both = jnp.dot(box_ref[kk], jnp.concatenate([hi, lo], axis=1), preferred_element_type=F32)
                    pad_s[pl.ds(pad_t + st, POOL_TOK), :] = both[:, :LANES] + both[:, LANES:]
                _unrolled_loop(s // POOL_TOK, col_box)

                def row_box(r):
                    st = pl.multiple_of(r * GRID_W, GRID_W)
                    acc = pad_s[pl.ds(pad_t + st + offsets[0] * GRID_W, GRID_W), :]
                    for o in offsets[1:]:
                        acc = acc + pad_s[pl.ds(pad_t + st + o * GRID_W, GRID_W), :]
                    if not transpose:
                        acc = acc * (inv_col * invr_ref[kk, r])
                    o_ref[pl.ds(st, GRID_W), :] = (acc - x_ref[pl.ds(st, GRID_W), :].astype(F32)).astype(out_dtype)
                _unrolled_loop(n_rows, row_box)

    slab = pl.BlockSpec((s, LANES), lambda i: (0, i))
    return pl.pallas_call(
        body, name=name, out_shape=jax.ShapeDtypeStruct((s, D_INNER), out_dtype), grid=(n_slab,),
        in_specs=[pl.BlockSpec(memory_space=pltpu.SMEM),
                  pl.BlockSpec((n_win, POOL_TOK, POOL_TOK), lambda i: (0, 0, 0)),
                  pl.BlockSpec((n_win, GRID_W, LANES), lambda i: (0, 0, 0)), slab],
        out_specs=slab,
        scratch_shapes=[pltpu.VMEM((s + 2 * pad_t, LANES), F32)],
        compiler_params=_cparams(dimension_semantics=("arbitrary",)),
    )(inv_r, boxes, inv_c, xin)


def _out1(dmix, pw, ps, g, x1, gt, wo, lg, lb, tgt, name):
    t = x1.shape[0]
    tm = min(TM_MM, t)
    n_grp = len(POOL_WINDOWS)

    def body(d_ref, pw_ref, ps_ref, g_ref, x1_ref, gt_ref, w_ref, lg_ref, lb_ref, tgt_ref, dz_ref, st_ref, po_ref):
        @pl.when(pl.program_id(0) == 0)
        def _():
            st_ref[...] = jnp.zeros_like(st_ref)

        br = jnp.zeros((tm, D_MODEL), F32)
        for k in range(n_grp):
            sl = slice(k * POOL_GROUP, (k + 1) * POOL_GROUP)
            po = jnp.dot(d_ref[:, sl], pw_ref[k], preferred_element_type=F32)
            po_ref[:, sl] = po.astype(ACT_DTYPE)
            y = po * ps_ref[:, sl]
            gg = g_ref[:, sl].astype(F32)
            br = br + _dot(y * (gg * _sigmoid(gg)), w_ref[sl, :])
        z = ALPHA * x1_ref[...] + gt_ref[...] * br
        xhat, rstd = _layer_norm_stats(z)
        lg_v = lg_ref[...]
        err = xhat * lg_v + lb_ref[...] - tgt_ref[...]
        dy = err * (1.0 / D_MODEL)
        dz = _layer_norm_bwd(dy, xhat, rstd, lg_v)
        dz_ref[...] = dz
        st_ref[0:1, :] += _rowsum(dy * xhat)
        st_ref[1:2, :] += _rowsum(dy)
        st_ref[2:3, :] += _rowsum(dz * br)
        st_ref[3:4, :] += _rowsum(err * err)

    wide = pl.BlockSpec((tm, D_INNER), lambda i: (i, 0))
    nar = pl.BlockSpec((tm, D_MODEL), lambda i: (i, 0))
    row = pl.BlockSpec((1, D_MODEL), lambda i: (0, 0))
    return pl.pallas_call(
        body, name=name,
        out_shape=[jax.ShapeDtypeStruct((t, D_MODEL), F32), jax.ShapeDtypeStruct((SUBLANES, D_MODEL), F32),
                   jax.ShapeDtypeStruct((t, D_INNER), ACT_DTYPE)],
        grid=(t // tm,),
        in_specs=[wide, pl.BlockSpec((n_grp, POOL_GROUP, POOL_GROUP), lambda i: (0, 0, 0)),
                  pl.BlockSpec((1, D_INNER), lambda i: (0, 0)), wide, nar, row,
                  pl.BlockSpec((D_INNER, D_MODEL), lambda i: (0, 0), pipeline_mode=pl.Buffered(1)), row, row, nar],
        out_specs=[nar, pl.BlockSpec((SUBLANES, D_MODEL), lambda i: (0, 0)), wide],
        compiler_params=_cparams(dimension_semantics=("arbitrary",)),
    )(dmix, pw, ps, g, x1, gt, wo, lg, lb, tgt)


def _flush(acc, out_hbm, sem):
    cp = pltpu.make_async_copy(acc, out_hbm, sem)
    cp.start()
    cp.wait()


def _bout1(dz, dmix, po, g, pw, ps, gt, wo, name):
    t = dz.shape[0]
    tm = min(TM_MM, t)
    nt = t // tm
    n_grp = len(POOL_WINDOWS)

    def body(dz_ref, d_ref, po_ref, g_ref, pw_ref, ps_ref, gt_ref, w_ref, dd_ref, dg_ref, gwo_hbm, gpw_hbm, gps_ref,
             gwo_acc, gpw_acc, sems):
        i = pl.program_id(0)

        @pl.when(i == 0)
        def _():
            gwo_acc[...] = jnp.zeros_like(gwo_acc)
            gpw_acc[...] = jnp.zeros_like(gpw_acc)
            gps_ref[...] = jnp.zeros_like(gps_ref)

        db = (gt_ref[...] * dz_ref[...]).astype(MXU_DTYPE)
        for k in range(n_grp):
            sl = slice(k * POOL_GROUP, (k + 1) * POOL_GROUP)
            dk = d_ref[:, sl]
            po = po_ref[:, sl].astype(F32)
            psk = ps_ref[:, sl]
            y = po * psk
            gg = g_ref[:, sl].astype(F32)
            sg = _sigmoid(gg)
            silu = gg * sg
            gwo_acc[sl, :] += _dot_tn(y * silu, db)
            dp = _dot_nt(db, w_ref[sl, :])
            dy = dp * silu
            dg_ref[:, sl] = (dp * y * (sg * (1.0 + gg * (1.0 - sg)))).astype(MXU_DTYPE)
            gps_ref[0:1, sl] += _rowsum(dy * po)
            dpo = (dy * psk).astype(MXU_DTYPE)
            gpw_acc[k] += _dot_tn(dk, dpo)
            dd_ref[:, sl] = _dot_nt(dpo, pw_ref[k])

        @pl.when(i == nt - 1)
        def _():
            _flush(gwo_acc, gwo_hbm, sems.at[0])
            _flush(gpw_acc, gpw_hbm, sems.at[1])

    wide = pl.BlockSpec((tm, D_INNER), lambda i: (i, 0))
    nar = pl.BlockSpec((tm, D_MODEL), lambda i: (i, 0))
    return pl.pallas_call(
        body, name=name,
        out_shape=[jax.ShapeDtypeStruct((t, D_INNER), F32), jax.ShapeDtypeStruct((t, D_INNER), MXU_DTYPE),
                   jax.ShapeDtypeStruct((D_INNER, D_MODEL), F32),
                   jax.ShapeDtypeStruct((n_grp, POOL_GROUP, POOL_GROUP), F32),
                   jax.ShapeDtypeStruct((SUBLANES, D_INNER), F32)],
        grid=(nt,),
        in_specs=[nar, wide, wide, wide,
                  pl.BlockSpec((n_grp, POOL_GROUP, POOL_GROUP), lambda i: (0, 0, 0), pipeline_mode=pl.Buffered(1)),
                  pl.BlockSpec((1, D_INNER), lambda i: (0, 0)), pl.BlockSpec((1, D_MODEL), lambda i: (0, 0)),
                  pl.BlockSpec((D_INNER, D_MODEL), lambda i: (0, 0), pipeline_mode=pl.Buffered(1))],
        out_specs=[wide, wide, ANY, ANY, pl.BlockSpec((SUBLANES, D_INNER), lambda i: (0, 0))],
        scratch_shapes=[pltpu.VMEM((D_INNER, D_MODEL), F32), pltpu.VMEM((n_grp, POOL_GROUP, POOL_GROUP), F32),
                        pltpu.SemaphoreType.DMA((2,))],
        compiler_params=_cparams(dimension_semantics=("arbitrary",)),
    )(dz, dmix, po, g, pw, ps, gt, wo)


def _bout0(dx1, xt, br0, lg, hf, hb, g, gt, wo, name, sides=()):
    t = dx1.shape[0]
    tm = min(TM_MM, t)
    nt = t // tm

    def body(dx_ref, x_ref, br_ref, lg_ref, hf_ref, hb_ref, g_ref, gt_ref, w_ref,
             dz_ref, dy_ref, dg_ref, gwo_hbm, st_ref, gwo_acc, sem):
        i = pl.program_id(0)

        @pl.when(i == 0)
        def _():
            gwo_acc[...] = jnp.zeros_like(gwo_acc)
            st_ref[...] = jnp.zeros_like(st_ref)

        dx = dx_ref[...]
        br = br_ref[...].astype(F32)
        gate = gt_ref[...]
        xhat, rstd = _layer_norm_stats(ALPHA * x_ref[...] + gate * br)
        dz = _layer_norm_bwd(dx, xhat, rstd, lg_ref[...])
        dz_ref[...] = dz
        st_ref[0:1, :] += _rowsum(dx * xhat)
        st_ref[1:2, :] += _rowsum(dx)
        st_ref[2:3, :] += _rowsum(dz * br)
        db = (gate * dz).astype(MXU_DTYPE)
        for k in range(D_INNER // WBLK):
            sl = slice(k * WBLK, (k + 1) * WBLK)
            y = hf_ref[:, sl].astype(F32) + hb_ref[:, sl].astype(F32)
            gg = g_ref[:, sl].astype(F32)
            sg = _sigmoid(gg)
            silu = gg * sg
            gwo_acc[sl, :] += _dot_tn(y * silu, db)
            dp = _dot_nt(db, w_ref[sl, :])
            dy_ref[:, sl] = (dp * silu).astype(ACT_DTYPE)
            dg_ref[:, sl] = (dp * y * (sg * (1.0 + gg * (1.0 - sg)))).astype(MXU_DTYPE)

        @pl.when(i == nt - 1)
        def _():
            _flush(gwo_acc, gwo_hbm, sem)

    wide = pl.BlockSpec((tm, D_INNER), lambda i: (i, 0))
    nar = pl.BlockSpec((tm, D_MODEL), lambda i: (i, 0))
    row = pl.BlockSpec((1, D_MODEL), lambda i: (0, 0))
    return _call_with_sides(
        body, sides, name=name,
        out_shape=[jax.ShapeDtypeStruct((t, D_MODEL), F32), jax.ShapeDtypeStruct((t, D_INNER), ACT_DTYPE),
                   jax.ShapeDtypeStruct((t, D_INNER), MXU_DTYPE), jax.ShapeDtypeStruct((D_INNER, D_MODEL), F32),
                   jax.ShapeDtypeStruct((SUBLANES, D_MODEL), F32)],
        grid=(nt,),
        in_specs=[nar, nar, nar, row, wide, wide, wide, row,
                  pl.BlockSpec((D_INNER, D_MODEL), lambda i: (0, 0), pipeline_mode=pl.Buffered(1))],
        out_specs=[nar, wide, wide, ANY, pl.BlockSpec((SUBLANES, D_MODEL), lambda i: (0, 0))],
        scratch_shapes=[pltpu.VMEM((D_INNER, D_MODEL), F32), pltpu.SemaphoreType.DMA(())],
        compiler_params=_cparams(dimension_semantics=("arbitrary",)),
        args=[dx1, xt, br0, lg, hf, hb, g, gt, wo])


def _conv_bwd(duvf, duvb, u, conv_w, name, sides=()):
    s = u.shape[0]
    tm = min(TM_LRU, s)
    cb = CB_LRU_FWD
    nt = s // tm

    def body(df_ref, dfp_ref, dfn_ref, db_ref, dbp_ref, dbn_ref, u_ref, cw_ref, du_ref, cst_ref):
        i = pl.program_id(1)

        @pl.when(i == 0)
        def _():
            cst_ref[...] = jnp.zeros_like(cst_ref)

        first, last = i == 0, i == nt - 1
        pz = jnp.where(first, 0.0, 1.0)
        nz = jnp.where(last, 0.0, 1.0)
        dout = df_ref[...].astype(F32) + db_ref[...].astype(F32)
        before = (dfp_ref[...].astype(F32) + dbp_ref[...].astype(F32))[H_HALO - SUBLANES:] * pz
        after = (dfn_ref[...].astype(F32) + dbn_ref[...].astype(F32))[:SUBLANES] * nz
        dm1, dp1, dp2 = _shifted(dout, before, after, [-1, 1, 2])
        cw = cw_ref[...]
        du_ref[...] = (dp2 * cw[0:1] + dp1 * cw[1:2] + dout * cw[2:3] + dm1 * cw[3:4]).astype(MXU_DTYPE)
        u_t = u_ref[...]
        cst_ref[0:1, :] += _rowsum(dp2 * u_t)
        cst_ref[1:2, :] += _rowsum(dp1 * u_t)
        cst_ref[2:3, :] += _rowsum(dout * u_t)
        cst_ref[3:4, :] += _rowsum(dm1 * u_t)
        cst_ref[4:5, :] += _rowsum(dout)

    tile, _, _ = _lru_specs(s, tm, cb, lambda i: i, nt)
    prev_map, next_map = _halo_maps(nt, tm, s // H_HALO, lambda i: i, rows=H_HALO)
    prev, nxt = pl.BlockSpec((H_HALO, cb), prev_map), pl.BlockSpec((H_HALO, cb), next_map)
    return _call_with_sides(
        body, sides, name=name,
        out_shape=[jax.ShapeDtypeStruct((s, D_INNER), MXU_DTYPE), jax.ShapeDtypeStruct((SUBLANES, D_INNER), F32)],
        grid=(D_INNER // cb, nt),
        in_specs=[tile, prev, nxt] * 2 + [tile, pl.BlockSpec((4, cb), lambda c, i: (0, c))],
        out_specs=[tile, pl.BlockSpec((SUBLANES, cb), lambda c, i: (0, c))], scratch_shapes=[],
        compiler_params=_cparams(dimension_semantics=("arbitrary", "arbitrary")),
        args=[duvf, duvf, duvf, duvb, duvb, duvb, u, conv_w])


def _bin(du, dg, xin, dzin, sc, sh, wg, name, gw_init=None, sides=()):
    t = xin.shape[0]
    tm = min(TM_MM, t)
    nt = t // tm
    has_g, has_dx, has_init = dg is not None, dzin is not None, gw_init is not None
    half = N_WBLK // 2
    n_blk = N_WBLK if has_g else half

    def body(*refs):
        refs = list(refs)
        du_ref = refs.pop(0)
        dg_ref = refs.pop(0) if has_g else None
        x_ref = refs.pop(0)
        dz_ref = refs.pop(0) if has_dx else None
        sc_ref, sh_ref, w_ref = refs.pop(0), refs.pop(0), refs.pop(0)
        init_hbm = refs.pop(0) if has_init else None
        dx_ref = refs.pop(0) if has_dx else None
        gw_hbm, st_ref, gw_acc, sem = refs
        i = pl.program_id(0)

        @pl.when(i == 0)
        def _():
            st_ref[...] = jnp.zeros_like(st_ref)
            first_zero = 0
            if has_init:
                _flush(init_hbm, gw_acc.at[pl.ds(0, half)], sem)
                first_zero = half
            for k in range(first_zero, n_blk):
                gw_acc[k] = jnp.zeros((D_MODEL, WBLK), F32)

        xv = x_ref[...]
        scale = 1.0 + sc_ref[...]
        h = (xv * scale + sh_ref[...]).astype(MXU_DTYPE)
        dh = None
        for k in range(n_blk):
            src = du_ref if k < half else dg_ref
            kk = k % half
            dk = src[:, kk * WBLK:(kk + 1) * WBLK]
            gw_acc[k] += _dot_tn(h, dk)
            contrib = _dot_nt(dk, w_ref[k])
            dh = contrib if dh is None else dh + contrib
        st_ref[0:1, :] += _rowsum(dh * xv)
        st_ref[1:2, :] += _rowsum(dh)
        if has_dx:
            dx_ref[...] = ALPHA * dz_ref[...] + dh * scale

        @pl.when(i == nt - 1)
        def _():
            _flush(gw_acc, gw_hbm, sem)

    wide = pl.BlockSpec((tm, D_INNER), lambda i: (i, 0))
    nar = pl.BlockSpec((tm, D_MODEL), lambda i: (i, 0))
    row = pl.BlockSpec((1, D_MODEL), lambda i: (0, 0))
    wspec = pl.BlockSpec((n_blk, D_MODEL, WBLK), lambda i: (0, 0, 0), pipeline_mode=pl.Buffered(1))
    in_specs = ([wide] + ([wide] if has_g else []) + [nar] + ([nar] if has_dx else []) + [row, row, wspec]
                + ([ANY] if has_init else []))
    args = ([du] + ([dg] if has_g else []) + [xin] + ([dzin] if has_dx else []) + [sc, sh, wg]
            + ([gw_init] if has_init else []))
    out_shape = ([jax.ShapeDtypeStruct((t, D_MODEL), F32)] if has_dx else []) + [
        jax.ShapeDtypeStruct((n_blk, D_MODEL, WBLK), F32), jax.ShapeDtypeStruct((SUBLANES, D_MODEL), F32)]
    out_specs = ([nar] if has_dx else []) + [ANY, pl.BlockSpec((SUBLANES, D_MODEL), lambda i: (0, 0))]
    return _call_with_sides(
        body, sides, name=name, out_shape=out_shape, grid=(nt,), in_specs=in_specs, out_specs=out_specs,
        scratch_shapes=[pltpu.VMEM((n_blk, D_MODEL, WBLK), F32), pltpu.SemaphoreType.DMA(())],
        compiler_params=_cparams(dimension_semantics=("arbitrary",)), args=args)


def _blocks_by_device(a, axis):
    shape = a.shape
    a = a.reshape(shape[:axis] + (N_DEV, shape[axis] // N_DEV) + shape[axis + 1:])
    return jnp.moveaxis(a, axis, 0)


def kernel(x, c, ctx, c_ctx, w_mod, b_mod, w_in, w_out, ln_g, ln_b, conv_w, conv_b, lru_wa, lru_ba, lru_wx, lru_bx, lru_lam, pool_w, pool_scale, loss_target, m_c_ctx, m_w_mod, m_b_mod, m_w_in, m_w_out, m_ln_g, m_ln_b, m_conv_w, m_conv_b, m_lru_wa, m_lru_ba, m_lru_wx, m_lru_bx, m_lru_lam, m_pool_w, m_pool_scale, v_c_ctx, v_w_mod, v_b_mod, v_w_in, v_w_out, v_ln_g, v_ln_b, v_conv_w, v_conv_b, v_lru_wa, v_lru_ba, v_lru_wx, v_lru_bx, v_lru_lam, v_pool_w, v_pool_scale):
    xi, yi, ci = _my_pos()
    dev = 4 * xi + 2 * yi + ci
    xt, ctxt, tgt = x[0], ctx[0], loss_target[0]
    n_mod = w_mod.shape[2]

    small_shapes = [(D_MODEL,), conv_w.shape[1:], lru_ba.shape[1:], lru_bx.shape[1:], lru_lam.shape[1:],
                    pool_scale.shape[1:]]
    small = _to_rows([c[0], conv_w[0], lru_ba[0], lru_bx[0], lru_lam[0], pool_scale[0]], SUBLANES)
    small_all, wi0 = _all_gather([small, w_in[0].astype(MXU_DTYPE)], "gather_first")
    pieces = [_split_rows(small_all[k], small_shapes) for k in range(N_DEV)]
    c_all = jnp.stack([p[0] for p in pieces])
    conv_w_f = jnp.concatenate([p[1] for p in pieces], axis=-1)
    lru_ba_f = jnp.concatenate([p[2] for p in pieces], axis=-1)[:, None, :]
    lru_bx_f = jnp.concatenate([p[3] for p in pieces], axis=-1)[:, None, :]
    lru_lam_f = jnp.concatenate([p[4] for p in pieces], axis=-1)[:, None, :]
    pool_scale_f = jnp.concatenate([p[5] for p in pieces], axis=-1)[None, :]

    cond = jnp.concatenate([c_all, jnp.broadcast_to(c_ctx[None, :], (N_DEV, D_MODEL))], axis=0)
    b_my = lax.dynamic_slice(b_mod, (0, dev * n_mod), (2, n_mod))[:, None, :]
    mod_part = _mod_fwd(cond, w_mod, b_my, "mod_fwd")
    mod_all, = _all_gather([mod_part], "gather_mod")
    mod = jnp.transpose(mod_all, (1, 2, 0, 3)).reshape(2, 16, 3 * D_MODEL)
    mod_me = lax.dynamic_slice(mod, (0, dev, 0), (2, 1, 3 * D_MODEL))
    sh = [mod_me[i, :, 0:D_MODEL] for i in range(2)]
    sc = [mod_me[i, :, D_MODEL:2 * D_MODEL] for i in range(2)]
    gt = [mod_me[i, :, 2 * D_MODEL:] for i in range(2)]
    shc, scc = mod[0, 8:9, 0:D_MODEL], mod[0, 8:9, D_MODEL:2 * D_MODEL]

    lg = [ln_g[i][None, :] for i in range(2)]
    lb = [ln_b[i][None, :] for i in range(2)]
    lru_p = dict(conv_w=conv_w_f, conv_b=conv_b, wa=(0.5 * lru_wa[0]).astype(MXU_DTYPE),
                 wx=(0.5 * lru_wx[0]).astype(MXU_DTYPE), ba=0.5 * lru_ba_f, bx=0.5 * lru_bx_f, lam=lru_lam_f)
    zero_state = jnp.zeros((1, D_INNER), F32)

    (u0, g0), (wo0,) = _in_proj(xt, sc[0], sh[0], wi0, "in_proj0", sides=[("gather", [w_out[0].astype(MXU_DTYPE)])])
    (uc, _), _ = _in_proj(ctxt, scc, shc, wi0, "in_proj0_ctx")
    (hcf, cf, uvc), _ = _lru_fwd(uc, zero_state, lru_p, 0, "lru_fwd_ctx_f", conv=True)
    (hcb, cbk), _ = _lru_fwd(uvc, zero_state, lru_p, 1, "lru_fwd_ctx_b", conv=False)
    (hf, _, uv0), (wi1,) = _lru_fwd(u0, cf, lru_p, 0, "lru_fwd_f", conv=True,
                                    sides=[("gather", [w_in[1].astype(MXU_DTYPE)])])
    (hb, _), (wo1, pool_w_g) = _lru_fwd(
        uv0, cbk, lru_p, 1, "lru_fwd_b", conv=False,
        sides=[("gather", [w_out[1].astype(MXU_DTYPE), pool_w[0].astype(MXU_DTYPE)])])
    w_in_l = [wi0, wi1]
    w_out_l = [wo0.reshape(D_INNER, D_MODEL), wo1.reshape(D_INNER, D_MODEL)]
    pool_w_f = jnp.transpose(pool_w_g, (1, 0, 2, 3)).reshape(len(POOL_WINDOWS), POOL_GROUP, POOL_GROUP)
    x1, br0, u1, g1 = _out0(hf, hb, g0, xt, gt[0], w_out_l[0], lg[0], lb[0], sc[1], sh[1], w_in_l[1], "out0_in1")
    dmix = _pool_mix(u1, False, MXU_DTYPE, "pool_fwd")
    dz1, st1, po1 = _out1(dmix, pool_w_f, pool_scale_f, g1, x1, gt[1], w_out_l[1], lg[1], lb[1], tgt, "out1")
    loss_me = jnp.full((1, LANES), (0.5 / D_MODEL) * jnp.sum(st1[3]), F32)

    core = jnp.reshape(ci, (1,)).astype(jnp.int32)
    wo_view = lambda a: a.reshape(N_DEV, D_INNER // N_DEV, D_MODEL)
    pw_view = lambda a: _blocks_by_device(a, 1).reshape(N_DEV, POOL_GROUP // N_DEV * len(POOL_WINDOWS), POOL_GROUP)
    dd, dg1, gwo1, gpw, gps = _bout1(dz1, dmix, po1, g1, pool_w_f, pool_scale_f, gt[1], w_out_l[1], "bwd_out1")
    du1 = _pool_mix(dd, True, MXU_DTYPE, "pool_bwd")
    (dx1, gwi1, stb1), _ = _bin(du1, dg1, x1, dz1, sc[1], sh[1], w_in_l[1], "bwd_in1")
    bufs1 = [gwi1, wo_view(gwo1), pw_view(gpw)]
    (dz0, dy0, dg0, gwo0, stl0), recv1 = _bout0(dx1, xt, br0, lg[0], hf, hb, g0, gt[0], w_out_l[0], "bwd_out0",
                                                sides=[("sibling", bufs1)])
    pairs1 = [_pair_sum(b, r, core, "reduce_pair_" + n)
              for b, r, n in zip(bufs1, recv1, ["w_in1", "w_out1", "pool_w"])]
    (duvf, gwa_f, gwx_f, gv_f, dh0f), (p_wi1, p_wo1, p_pw, recv_wo0) = _lru_bwd(
        uv0, dy0, hf, cf, zero_state, lru_p, 0, "lru_bwd_f", sides=[("chips", pairs1), ("sibling", [wo_view(gwo0)])])
    pair_wo0 = _pair_sum(wo_view(gwo0), recv_wo0, core, "reduce_pair_w_out0")
    (duvb, gwa_b, gwx_b, gv_b, dh0b), (p_wo0,) = _lru_bwd(
        uv0, dy0, hb, cbk, zero_state, lru_p, 1, "lru_bwd_b", sides=[("chips", [pair_wo0])])
    zero_dh = jnp.zeros(uc.shape, ACT_DTYPE)
    (ducf, gwa_cf, gwx_cf, gv_cf, _), _ = _lru_bwd(uvc, zero_dh, hcf, zero_state, dh0f, lru_p, 0, "lru_bwd_ctx_f")
    (ducb, gwa_cb, gwx_cb, gv_cb, _), _ = _lru_bwd(uvc, zero_dh, hcb, zero_state, dh0b, lru_p, 1, "lru_bwd_ctx_b")

    def pack(sharded, replicated):
        sh_sizes = [int(np.prod(a.shape[1:])) for a in sharded]
        rep_sizes = [a.shape[0] // N_DEV for a in replicated]
        n_flat = sum(sh_sizes) + sum(rep_sizes)
        rows = -(-(-(-n_flat // LANES)) // FLAT_ROWS) * FLAT_ROWS
        buf = jnp.concatenate([a.reshape(N_DEV, -1) for a in sharded + replicated], axis=1)
        return jnp.pad(buf, ((0, 0), (0, rows * LANES - n_flat))).reshape(N_DEV, rows, LANES), sh_sizes, rep_sizes

    def unpack(reduced, sh_sizes, rep_sizes, sh_shapes):
        flat = reduced.reshape(-1)
        offs = np.cumsum([0] + sh_sizes)
        mine = [flat[offs[k]:offs[k + 1]].reshape(s) for k, s in enumerate(sh_shapes)]
        return mine, _to_rows([flat[offs[-1]:offs[-1] + sum(rep_sizes)]], FLAT_ROWS)

    def spread(rep_all, rep_sizes, shapes):
        flat = rep_all.reshape(N_DEV, -1)
        offs = np.cumsum([0] + rep_sizes)
        return [flat[:, offs[k]:offs[k + 1]].reshape(s) for k, s in enumerate(shapes)]

    (du0, cst0), _ = _conv_bwd(duvf, duvb, u0, conv_w_f, "conv_bwd")
    (duc, cstc), _ = _conv_bwd(ducf, ducb, uc, conv_w_f, "conv_bwd_ctx")
    (gwic, stc), _ = _bin(duc, None, ctxt, None, scc, shc, w_in_l[0][:N_WBLK // 2], "bwd_in0_ctx")
    (gx, gwi0, stb0), _ = _bin(du0, dg0, xt, dz0, sc[0], sh[0], w_in_l[0], "bwd_in0", gw_init=gwic)

    zero_row = jnp.zeros((1, D_MODEL), F32)
    dm_me = jnp.stack([
        jnp.concatenate([jnp.concatenate([stb0[1:2], stb0[0:1], stl0[2:3]], axis=1),
                         jnp.concatenate([stc[1:2], stc[0:1], zero_row], axis=1)], axis=0),
        jnp.concatenate([jnp.concatenate([stb1[1:2], stb1[0:1], st1[2:3]], axis=1),
                         jnp.zeros((1, 3 * D_MODEL), F32)], axis=0)])
    dm_g, loss_g = _all_gather([dm_me, loss_me], "gather_dmod")
    loss = jnp.sum(loss_g[:, 0, 0])
    dm_all = jnp.concatenate([jnp.transpose(dm_g[:, :, 0], (1, 0, 2)), jnp.transpose(dm_g[:, :, 1], (1, 0, 2))],
                             axis=1)
    dm_my = lax.dynamic_slice(dm_all, (0, 0, dev * n_mod), (2, 16, n_mod))
    g_w_mod, g_b_mod, gcc_part = _mod_bwd(cond, dm_all, dm_my, w_mod, "mod_bwd")
    g_b_mod = g_b_mod.reshape(b_mod.shape)

    gwa = jnp.stack([gwa_f + gwa_cf, gwa_b + gwa_cb])
    gwx = jnp.stack([gwx_f + gwx_cf, gwx_b + gwx_cb])
    gv = jnp.stack([gv_f + gv_cf, gv_b + gv_cb])
    cst = cst0 + cstc
    misc, m_sh, m_rep = pack(
        [_blocks_by_device(cst[0:4], 1), _blocks_by_device(gv[:, 0], 1), _blocks_by_device(gv[:, 1], 1),
         _blocks_by_device(gv[:, 2], 1), _blocks_by_device(gps[0], 0)],
        [gwa.reshape(-1), gwx.reshape(-1), jnp.stack([stl0[0], st1[0]]).reshape(-1),
         jnp.stack([stl0[1], st1[1]]).reshape(-1), cst[4], gcc_part.reshape(-1)])
    bufs = [gwi0, misc]
    recvs = _sibling_exchange(bufs, "reduce_sibling")
    pairs = [_pair_sum(b, r, core, "reduce_pair_" + n) for b, r, n in zip(bufs, recvs, ["w_in0", "misc"])]
    p_wi0, p_misc = _chip_exchange(pairs, "reduce_chips")
    (g_conv_w, g_lru_ba, g_lru_bx, g_lru_lam, g_pool_scale), rep_mine = unpack(
        _sum4(p_misc, "reduce_sum_misc"), m_sh, m_rep,
        [conv_w.shape, lru_ba.shape, lru_bx.shape, lru_lam.shape, pool_scale.shape])
    rep_all, = _all_gather([rep_mine.astype(WIRE_DTYPE)], "gather_replicated")
    rep_all = rep_all.astype(F32)
    g_lru_wa, g_lru_wx, g_ln_g, g_ln_b, g_conv_b, g_c_ctx = spread(
        rep_all, m_rep, [lru_wa.shape, lru_wx.shape, ln_g.shape, ln_b.shape, conv_b.shape, c_ctx.shape])

    names = ["c_ctx", "w_mod", "b_mod", "w_in", "w_out", "ln_g", "ln_b", "conv_w", "conv_b", "lru_wa", "lru_ba",
             "lru_wx", "lru_bx", "lru_lam", "pool_w", "pool_scale"]
    weights = dict(c_ctx=c_ctx, w_mod=w_mod, b_mod=b_mod, w_in=w_in, w_out=w_out, ln_g=ln_g, ln_b=ln_b,
                   conv_w=conv_w, conv_b=conv_b, lru_wa=lru_wa, lru_ba=lru_ba, lru_wx=lru_wx, lru_bx=lru_bx,
                   lru_lam=lru_lam, pool_w=pool_w, pool_scale=pool_scale)
    mom_m = dict(c_ctx=m_c_ctx, w_mod=m_w_mod, b_mod=m_b_mod, w_in=m_w_in, w_out=m_w_out, ln_g=m_ln_g, ln_b=m_ln_b,
                 conv_w=m_conv_w, conv_b=m_conv_b, lru_wa=m_lru_wa, lru_ba=m_lru_ba, lru_wx=m_lru_wx,
                 lru_bx=m_lru_bx, lru_lam=m_lru_lam, pool_w=m_pool_w, pool_scale=m_pool_scale)
    mom_v = dict(c_ctx=v_c_ctx, w_mod=v_w_mod, b_mod=v_b_mod, w_in=v_w_in, w_out=v_w_out, ln_g=v_ln_g, ln_b=v_ln_b,
                 conv_w=v_conv_w, conv_b=v_conv_b, lru_wa=v_lru_wa, lru_ba=v_lru_ba, lru_wx=v_lru_wx,
                 lru_bx=v_lru_bx, lru_lam=v_lru_lam, pool_w=v_pool_w, pool_scale=v_pool_scale)
    grads = dict(c_ctx=g_c_ctx, w_mod=g_w_mod, b_mod=g_b_mod, ln_g=g_ln_g, ln_b=g_ln_b,
                 conv_w=g_conv_w, conv_b=g_conv_b, lru_wa=g_lru_wa, lru_ba=g_lru_ba, lru_wx=g_lru_wx,
                 lru_bx=g_lru_bx, lru_lam=g_lru_lam)
    grads["pool_scale"] = g_pool_scale
    delta, new_m, new_v = {}, {}, {}

    def update_parts(n, parts, view):
        res = _adamw_parts(weights[n].reshape(view), parts, mom_m[n].reshape(view), mom_v[n].reshape(view),
                           "adamw_" + n)
        grads[n], delta[n], new_m[n], new_v[n] = [r.reshape(weights[n].shape) for r in res]

    update_parts("w_in", [p_wi0, p_wi1], w_in.shape)
    update_parts("w_out", [p_wo0, p_wo1], w_out.shape)
    update_parts("pool_w", [p_pw], (1,) + p_pw.shape[1:])
    for n in ("w_mod", "lru_wa", "lru_wx"):
        shape = weights[n].shape
        view = (int(np.prod(shape[:-1])), shape[-1])
        res = _adamw(weights[n].reshape(view), grads[n].reshape(view), mom_m[n].reshape(view),
                     mom_v[n].reshape(view), "adamw_" + n)
        delta[n], new_m[n], new_v[n] = [r.reshape(shape) for r in res]

    small = [n for n in names if n not in delta]
    shapes = [weights[n].shape for n in small]
    flat = lambda d: _to_rows([d[n] for n in small], FLAT_ROWS)
    res = _adamw(flat(weights), flat(grads), flat(mom_m), flat(mom_v), "adamw_small")
    for d, r in zip((delta, new_m, new_v), res):
        d.update(zip(small, _split_rows(r, shapes)))

    return (loss, gx[None], *[grads[n] for n in names], *[delta[n] for n in names],
            *[new_m[n] for n in names], *[new_v[n] for n in names])
```

```python
import functools

import numpy as np
import jax
import jax.numpy as jnp
from jax import lax
from jax.experimental import pallas as pl
from jax.experimental.pallas import tpu as pltpu

F32 = jnp.float32
BF16 = jnp.bfloat16
MXU_DTYPE = BF16

D_MODEL = 1024
D_INNER = 2048
LRU_BLOCK = 128
GRID_W = 64
POOL_WINDOWS = (2, 4, 8, 16)
POOL_GROUP = 512
ALPHA = float(4 ** 0.25)
LN_EPS = 1e-5
LRU_C = 8.0
N_DEV = 8
N_WBLK = 8
WBLK = 512

ADAM_LR = 0.001
ADAM_B1 = 0.9
ADAM_B2 = 0.999
ADAM_EPS = 1e-08
ADAM_WD = 0.01
ADAM_STEP = 10

LANES = 128
SUBLANES = 8
V7X_VMEM_BYTES = 64 * 1024 * 1024
VMEM_COMPILER_RESERVE = 8 * 1024 * 1024
VMEM_LIMIT = V7X_VMEM_BYTES - VMEM_COMPILER_RESERVE
MESH = pl.DeviceIdType.MESH
ANY = pl.BlockSpec(memory_space=pl.ANY)

TM_MM = 512
TM_LRU = 1024
CB_LRU = 512
CB_LRU_FWD = 1024
N_SEG = 8
SCAN_UNROLL = 4
SCAN_ROW_T = 17
SCAN_ROW_J = 2
SQRT_FLOOR = 1e-30
FLAT_ROWS = 16
ELEMENTWISE_TILE_BYTES = 1 << 20
POOL_TOK = 256
WIRE_DTYPE = BF16
ACT_DTYPE = BF16
H_HALO = 16


def _cparams(**kw):
    return pltpu.CompilerParams(vmem_limit_bytes=VMEM_LIMIT, **kw)


def _my_pos():
    return lax.axis_index("x"), lax.axis_index("y"), lax.axis_index("c")


def _dot(a, b):
    return jnp.dot(a.astype(MXU_DTYPE), b.astype(MXU_DTYPE), preferred_element_type=F32)


def _dot_tn(a, b):
    return lax.dot_general(a.astype(MXU_DTYPE), b.astype(MXU_DTYPE), (((0,), (0,)), ((), ())),
                           preferred_element_type=F32)


def _dot_nt(a, b):
    return lax.dot_general(a.astype(MXU_DTYPE), b.astype(MXU_DTYPE), (((1,), (1,)), ((), ())),
                           preferred_element_type=F32)


def _sigmoid(z):
    return 0.5 * jnp.tanh(0.5 * z) + 0.5


def _log_sigmoid(x):
    y = jnp.exp(-jnp.abs(x))
    u = 1.0 + y
    l1p = jnp.where(u == 1.0, y, jnp.log(u) * (y / jnp.where(u == 1.0, 1.0, u - 1.0)))
    return jnp.minimum(x, 0.0) - l1p


def _rowsum(v):
    return jnp.sum(v, axis=0, keepdims=True)


def _layer_norm_stats(z):
    mu = jnp.mean(z, axis=-1, keepdims=True)
    zc = z - mu
    var = jnp.mean(zc * zc, axis=-1, keepdims=True)
    rstd = lax.rsqrt(var + LN_EPS)
    return zc * rstd, rstd


def _layer_norm_bwd(dy, xhat, rstd, g):
    dxh = dy * g
    m1 = jnp.mean(dxh, axis=-1, keepdims=True)
    m2 = jnp.mean(dxh * xhat, axis=-1, keepdims=True)
    return rstd * (dxh - m1 - xhat * m2)


def _shifted(v, before8, after8, offsets):
    n = v.shape[0]
    ext = jnp.concatenate([before8, v, after8], axis=0)
    total = n + 2 * SUBLANES
    return [pltpu.roll(ext, (-k) % total, 0)[SUBLANES:SUBLANES + n] for k in offsets]


def _rows8(row):
    return jnp.broadcast_to(row, (SUBLANES, row.shape[1]))


def _shift_down(v, first_row):
    return _shifted(v, _rows8(first_row), _rows8(first_row), [-1])[0]


def _shift_up(v, last_row):
    return _shifted(v, _rows8(last_row), _rows8(last_row), [1])[0]


def _all_gather(blocks, name):
    n = len(blocks)

    def body(*refs):
        x_refs, out_refs = refs[:n], refs[n:2 * n]
        send_sems, recv_sems, local_sems = refs[2 * n:]
        x, y, c = _my_pos()
        me, sibling = (x, y, c), (x, y, 1 - c)
        chips = [(1 - x, y), (x, 1 - y), (1 - x, 1 - y)]

        def slot(a, px, py, pc):
            return out_refs[a].at[4 * px + 2 * py + pc]

        def copy(a, k, block, to, src=None):
            return pltpu.make_async_remote_copy(
                src_ref=slot(a, *block) if src is None else src, dst_ref=slot(a, *block),
                send_sem=send_sems.at[a, k], recv_sem=recv_sems.at[a, k], device_id=to, device_id_type=MESH)

        mine = [pltpu.make_async_copy(x_refs[a], slot(a, *me), local_sems.at[a]) for a in range(n)]
        for cp in mine:
            cp.start()
        first = []
        for a in range(n):
            first.append(copy(a, 0, me, sibling, src=x_refs[a]))
            first += [copy(a, 1 + j, me, (*chip, c), src=x_refs[a]) for j, chip in enumerate(chips)]
        for cp in first:
            cp.start()
        passed = []
        for j, chip in enumerate(chips):
            for a in range(n):
                copy(a, 1 + j, (*chip, c), me).wait_recv()
                fwd = copy(a, 4 + j, (*chip, c), sibling)
                fwd.start()
                passed.append(fwd)
        for a in range(n):
            copy(a, 0, sibling, me).wait_recv()
            for j, chip in enumerate(chips):
                copy(a, 4 + j, (*chip, 1 - c), me).wait_recv()
        for cp in first + passed:
            cp.wait_send()
        for cp in mine:
            cp.wait()

    outs = pl.pallas_call(
        body, name=name,
        out_shape=[jax.ShapeDtypeStruct((N_DEV,) + b.shape, b.dtype) for b in blocks],
        in_specs=[ANY] * n, out_specs=[ANY] * n,
        scratch_shapes=[pltpu.SemaphoreType.DMA((n, 7)), pltpu.SemaphoreType.DMA((n, 7)),
                        pltpu.SemaphoreType.DMA((n,))],
    )(*blocks)
    return list(outs)


def _sibling_exchange(bufs, name):
    n = len(bufs)

    def body(*refs):
        srcs, outs = refs[:n], refs[n:2 * n]
        send_sems, recv_sems = refs[2 * n:]
        x, y, c = _my_pos()
        copies = [pltpu.make_async_remote_copy(
            src_ref=srcs[a].at[2 * j + (1 - c)], dst_ref=outs[a].at[j], send_sem=send_sems.at[a, j],
            recv_sem=recv_sems.at[a, j], device_id=(x, y, 1 - c), device_id_type=MESH)
            for a in range(n) for j in range(4)]
        for cp in copies:
            cp.start()
        for cp in copies:
            cp.wait()

    outs = pl.pallas_call(
        body, name=name, out_shape=[jax.ShapeDtypeStruct((4,) + b.shape[1:], b.dtype) for b in bufs],
        in_specs=[ANY] * n, out_specs=[ANY] * n,
        scratch_shapes=[pltpu.SemaphoreType.DMA((n, 4)), pltpu.SemaphoreType.DMA((n, 4))],
    )(*bufs)
    return list(outs)


def _chip_exchange(parts, name):
    n = len(parts)

    def body(*refs):
        srcs, outs = refs[:n], refs[n:2 * n]
        send_sems, recv_sems, local_sems = refs[2 * n:]
        x, y, c = _my_pos()
        jme = 2 * x + y
        peers = [(1 - x, y), (x, 1 - y), (1 - x, 1 - y)]
        local = [pltpu.make_async_copy(srcs[a].at[jme], outs[a].at[jme], local_sems.at[a]) for a in range(n)]
        for cp in local:
            cp.start()

        def copy(a, k, px, py, dst_slot):
            return pltpu.make_async_remote_copy(
                src_ref=srcs[a].at[2 * px + py], dst_ref=outs[a].at[dst_slot], send_sem=send_sems.at[a, k],
                recv_sem=recv_sems.at[a, k], device_id=(px, py, c), device_id_type=MESH)

        sends = [copy(a, k, px, py, jme) for a in range(n) for k, (px, py) in enumerate(peers)]
        for cp in sends:
            cp.start()
        for a in range(n):
            for k, (px, py) in enumerate(peers):
                copy(a, k, px, py, 2 * px + py).wait_recv()
        for cp in sends:
            cp.wait_send()
        for cp in local:
            cp.wait()

    outs = pl.pallas_call(
        body, name=name, out_shape=[jax.ShapeDtypeStruct(p.shape, p.dtype) for p in parts],
        in_specs=[ANY] * n, out_specs=[ANY] * n,
        scratch_shapes=[pltpu.SemaphoreType.DMA((n, 3)), pltpu.SemaphoreType.DMA((n, 3)),
                        pltpu.SemaphoreType.DMA((n,))],
    )(*parts)
    return list(outs)


_SIDE_REMOTE = {"gather": 7, "sibling": 4, "chips": 3}
_FLIPS = [(0, 0, 1), (1, 0, 0), (0, 1, 0), (1, 1, 0), (1, 0, 1), (0, 1, 1), (1, 1, 1)]


def _side_plan(sides):
    inputs, out_shapes, scratch = [], [], []
    for kind, arrays in sides:
        n = len(arrays)
        for a in arrays:
            inputs.append(a)
            shape = {"gather": (N_DEV,) + a.shape, "sibling": (4,) + a.shape[1:], "chips": a.shape}[kind]
            out_shapes.append(jax.ShapeDtypeStruct(shape, a.dtype))
        scratch += [pltpu.SemaphoreType.DMA((n, _SIDE_REMOTE[kind])), pltpu.SemaphoreType.DMA((n, _SIDE_REMOTE[kind])),
                    pltpu.SemaphoreType.DMA((n,))]
    return inputs, out_shapes, scratch


def _side_copies(sides, in_refs, out_refs, sem_refs):
    x, y, c = _my_pos()
    starts, waits = [], []
    pos = 0
    for s, (kind, arrays) in enumerate(sides):
        send_sems, recv_sems, local_sems = sem_refs[3 * s:3 * s + 3]
        for a in range(len(arrays)):
            src, out = in_refs[pos], out_refs[pos]
            pos += 1

            def remote(k, src_ref, dst_ref, to):
                return pltpu.make_async_remote_copy(src_ref=src_ref, dst_ref=dst_ref, send_sem=send_sems.at[a, k],
                                                    recv_sem=recv_sems.at[a, k], device_id=to, device_id_type=MESH)

            def local(src_ref, dst_ref):
                cp = pltpu.make_async_copy(src_ref, dst_ref, local_sems.at[a])
                starts.append(cp.start)
                waits.append(cp.wait)

            if kind == "gather":
                me = 4 * x + 2 * y + c
                local(src, out.at[me])
                for k, (fx, fy, fc) in enumerate(_FLIPS):
                    px, py, pc = (1 - x if fx else x), (1 - y if fy else y), (1 - c if fc else c)
                    send = remote(k, src, out.at[me], (px, py, pc))
                    starts.append(send.start)
                    waits += [remote(k, src, out.at[4 * px + 2 * py + pc], (px, py, pc)).wait_recv, send.wait_send]
            elif kind == "sibling":
                for j in range(4):
                    cp = remote(j, src.at[2 * j + (1 - c)], out.at[j], (x, y, 1 - c))
                    starts.append(cp.start)
                    waits.append(cp.wait)
            else:
                jme = 2 * x + y
                local(src.at[jme], out.at[jme])
                for k, (px, py) in enumerate([(1 - x, y), (x, 1 - y), (1 - x, 1 - y)]):
                    send = remote(k, src.at[2 * px + py], out.at[jme], (px, py, c))
                    starts.append(send.start)
                    waits += [remote(k, src.at[2 * px + py], out.at[2 * px + py], (px, py, c)).wait_recv,
                              send.wait_send]
    return starts, waits


def _call_with_sides(body, sides, *, name, grid, in_specs, out_specs, out_shape, scratch_shapes, compiler_params, args):
    if not sides:
        res = pl.pallas_call(body, name=name, grid=grid, in_specs=in_specs, out_specs=out_specs, out_shape=out_shape,
                             scratch_shapes=scratch_shapes, compiler_params=compiler_params)(*args)
        return list(res), []
    s_in, s_out, s_scr = _side_plan(sides)
    n_in, n_out, n_scr, n_side = len(in_specs), len(out_specs), len(scratch_shapes), len(s_in)

    def wrapped(*refs):
        refs = list(refs)
        ins, side_in = refs[:n_in], refs[n_in:n_in + n_side]
        outs = refs[n_in + n_side:n_in + n_side + n_out]
        side_out = refs[n_in + n_side + n_out:n_in + 2 * n_side + n_out]
        rest = refs[n_in + 2 * n_side + n_out:]
        starts, waits = _side_copies(sides, side_in, side_out, rest[n_scr:])
        first = functools.reduce(jnp.logical_and, [pl.program_id(d) == 0 for d in range(len(grid))])
        last = functools.reduce(jnp.logical_and, [pl.program_id(d) == grid[d] - 1 for d in range(len(grid))])

        @pl.when(first)
        def _():
            for start in starts:
                start()

        body(*ins, *outs, *rest[:n_scr])

        @pl.when(last)
        def _():
            for wait in waits:
                wait()

    res = pl.pallas_call(
        wrapped, name=name, grid=grid, in_specs=list(in_specs) + [ANY] * n_side,
        out_specs=list(out_specs) + [ANY] * n_side, out_shape=list(out_shape) + s_out,
        scratch_shapes=list(scratch_shapes) + s_scr, compiler_params=compiler_params,
    )(*args, *s_in)
    return list(res[:n_out]), list(res[n_out:])


def _row_tile(r, l):
    t = min(r, max(16, ELEMENTWISE_TILE_BYTES // (4 * l) // 16 * 16))
    while r % t:
        t -= 16
    return t


def _pair_sum(buf, recv, core, name):
    _, r, l = buf.shape
    tr = _row_tile(r, l)

    def body(core_ref, a_ref, b_ref, o_ref):
        o_ref[...] = (a_ref[...] + b_ref[...]).astype(WIRE_DTYPE)

    return pl.pallas_call(
        body, name=name, out_shape=jax.ShapeDtypeStruct((4, r, l), WIRE_DTYPE),
        grid_spec=pltpu.PrefetchScalarGridSpec(
            num_scalar_prefetch=1, grid=(4, r // tr),
            in_specs=[pl.BlockSpec((None, tr, l), lambda j, i, cr: (2 * j + cr[0], i, 0)),
                      pl.BlockSpec((None, tr, l), lambda j, i, cr: (j, i, 0))],
            out_specs=pl.BlockSpec((None, tr, l), lambda j, i, cr: (j, i, 0))),
        compiler_params=_cparams(dimension_semantics=("arbitrary", "arbitrary")),
    )(core, buf, recv)


def _sum_parts(p_ref):
    return ((p_ref[0].astype(F32) + p_ref[1].astype(F32)) + (p_ref[2].astype(F32) + p_ref[3].astype(F32)))


def _sum4(parts, name):
    _, r, l = parts.shape
    tr = _row_tile(r, l)

    def body(p_ref, o_ref):
        o_ref[...] = _sum_parts(p_ref)

    return pl.pallas_call(
        body, name=name, out_shape=jax.ShapeDtypeStruct((r, l), F32), grid=(r // tr,),
        in_specs=[pl.BlockSpec((4, tr, l), lambda i: (0, i, 0))],
        out_specs=pl.BlockSpec((tr, l), lambda i: (i, 0)),
        compiler_params=_cparams(dimension_semantics=("arbitrary",)),
    )(parts)


def _adamw_update(w, gg, m, v):
    nm = ADAM_B1 * m + (1.0 - ADAM_B1) * gg
    nv = ADAM_B2 * v + (1.0 - ADAM_B2) * (gg * gg)
    m_hat = nm / (1.0 - ADAM_B1 ** ADAM_STEP)
    v_hat = nv / (1.0 - ADAM_B2 ** ADAM_STEP)
    return -ADAM_LR * (m_hat / (jnp.sqrt(v_hat) + ADAM_EPS) + ADAM_WD * w), nm, nv


def _adamw(w, g, m, v, name):
    r, l = w.shape
    tr = _row_tile(r, l)

    def body(w_ref, g_ref, m_ref, v_ref, d_ref, nm_ref, nv_ref):
        d_ref[...], nm_ref[...], nv_ref[...] = _adamw_update(w_ref[...], g_ref[...], m_ref[...], v_ref[...])

    spec = pl.BlockSpec((tr, l), lambda i: (i, 0))
    return pl.pallas_call(
        body, name=name, out_shape=[jax.ShapeDtypeStruct((r, l), F32)] * 3, grid=(r // tr,),
        in_specs=[spec] * 4, out_specs=[spec] * 3,
        compiler_params=_cparams(dimension_semantics=("arbitrary",)),
    )(w, g, m, v)


def _adamw_parts(w, parts, m, v, name):
    nl, r, l = w.shape
    tr = _row_tile(r, l)

    def body(*refs):
        w_ref, p_refs, (m_ref, v_ref, g_ref, d_ref, nm_ref, nv_ref) = refs[0], refs[1:1 + nl], refs[1 + nl:]
        layer = pl.program_id(0)
        gg = _sum_parts(p_refs[0])
        for q in range(1, nl):
            gg = jnp.where(layer == q, _sum_parts(p_refs[q]), gg)
        g_ref[...] = gg
        d_ref[...], nm_ref[...], nv_ref[...] = _adamw_update(w_ref[...], gg, m_ref[...], v_ref[...])

    spec = pl.BlockSpec((None, tr, l), lambda q, i: (q, i, 0))
    pspecs = [pl.BlockSpec((4, tr, l), lambda q, i, k=k: (0, jnp.where(q == k, i, 0), 0)) for k in range(nl)]
    return pl.pallas_call(
        body, name=name, out_shape=[jax.ShapeDtypeStruct((nl, r, l), F32)] * 4, grid=(nl, r // tr),
        in_specs=[spec] + pspecs + [spec, spec], out_specs=[spec] * 4,
        compiler_params=_cparams(dimension_semantics=("arbitrary", "arbitrary")),
    )(w, *parts, m, v)


def _to_rows(pieces, row_multiple):
    flat = jnp.concatenate([p.reshape(-1) for p in pieces])
    rows = -(-flat.shape[0] // LANES)
    rows = -(-rows // row_multiple) * row_multiple
    flat = jnp.pad(flat, (0, rows * LANES - flat.shape[0]))
    return flat.reshape(rows, LANES)


def _split_rows(rows, shapes):
    flat = rows.reshape(-1)
    out, off = [], 0
    for s in shapes:
        n = int(np.prod(s))
        out.append(flat[off:off + n].reshape(s))
        off += n
    return out


def _mod_fwd(cond, w_mod, b_my, name):
    nl, _, ncol = w_mod.shape

    def body(a_ref, w_ref, b_ref, o_ref):
        a = a_ref[...]
        s = a * _sigmoid(a)
        for i in range(nl):
            o_ref[i] = _dot(s, w_ref[i]) + b_ref[i]

    return pl.pallas_call(
        body, name=name, out_shape=jax.ShapeDtypeStruct((nl, 16, ncol), F32),
        compiler_params=_cparams(),
    )(cond, w_mod, b_my)


def _mod_bwd(cond, dm_all, dm_my, w_mod, name):
    nl, _, ncol = w_mod.shape

    def body(a_ref, dma_ref, dmm_ref, w_ref, gw_ref, gb_ref, gc_ref):
        a = a_ref[...]
        sg = _sigmoid(a)
        s = a * sg
        for i in range(nl):
            gw_ref[i] = _dot_tn(s, dmm_ref[i])
            gb_ref[i] = jnp.sum(dma_ref[i], axis=0, keepdims=True)
        back = _dot_nt(dmm_ref[0], w_ref[0])
        dsilu = sg * (1.0 + a * (1.0 - sg))
        gc_ref[...] = jnp.sum(back[8:16] * dsilu[8:16], axis=0, keepdims=True)

    return pl.pallas_call(
        body, name=name,
        out_shape=[jax.ShapeDtypeStruct((nl, D_MODEL, ncol), F32), jax.ShapeDtypeStruct((nl, 1, 3 * D_MODEL), F32),
                   jax.ShapeDtypeStruct((1, D_MODEL), F32)],
        compiler_params=_cparams(),
    )(cond, dm_all, dm_my, w_mod)


def _in_proj(xt, sc, sh, wg, name, sides=()):
    t = xt.shape[0]
    tm = min(TM_MM, t)

    def body(x_ref, sc_ref, sh_ref, w_ref, u_ref, g_ref):
        h = (x_ref[...] * (1.0 + sc_ref[...]) + sh_ref[...]).astype(MXU_DTYPE)
        for k in range(N_WBLK):
            o = jnp.dot(h, w_ref[k], preferred_element_type=F32)
            if k < N_WBLK // 2:
                u_ref[:, k * WBLK:(k + 1) * WBLK] = o
            else:
                kk = k - N_WBLK // 2
                g_ref[:, kk * WBLK:(kk + 1) * WBLK] = o.astype(ACT_DTYPE)

    row = pl.BlockSpec((1, D_MODEL), lambda i: (0, 0))
    return _call_with_sides(
        body, sides, name=name,
        out_shape=[jax.ShapeDtypeStruct((t, D_INNER), F32), jax.ShapeDtypeStruct((t, D_INNER), ACT_DTYPE)],
        grid=(t // tm,),
        in_specs=[pl.BlockSpec((tm, D_MODEL), lambda i: (i, 0)), row, row,
                  pl.BlockSpec((N_WBLK, D_MODEL, WBLK), lambda i: (0, 0, 0), pipeline_mode=pl.Buffered(1))],
        out_specs=[pl.BlockSpec((tm, D_INNER), lambda i: (i, 0))] * 2, scratch_shapes=[],
        compiler_params=_cparams(dimension_semantics=("arbitrary",)), args=[xt, sc, sh, wg])


def _halo_maps(nt, tm, n_blocks, pos, rows=SUBLANES):
    per = tm // rows
    prev = lambda cb, i: (jnp.maximum(pos(i) * per - 1, 0), cb)
    nxt = lambda cb, i: (jnp.minimum((pos(i) + 1) * per, n_blocks - 1), cb)
    return prev, nxt


def _conv_taps(u, prev8, next8, is_first, is_last):
    pz = jnp.where(is_first, 0.0, 1.0)
    nz = jnp.where(is_last, 0.0, 1.0)
    return _shifted(u, prev8 * pz, next8 * nz, [-2, -1, 1])


def _lru_gates(uv, wa_ref, wx_ref, ba, bx, cl, g):
    sl = slice(g * LANES, (g + 1) * LANES)
    uvg = uv[:, sl]
    r = 0.5 * jnp.tanh(_dot(uvg, wa_ref[g]) + ba[:, sl]) + 0.5
    ii = 0.5 * jnp.tanh(_dot(uvg, wx_ref[g]) + bx[:, sl]) + 0.5
    la = cl[:, sl] * r
    a = jnp.exp(la)
    q = jnp.tanh(-la) * (1.0 + a * a)
    rs = lax.rsqrt(jnp.maximum(q, SQRT_FLOOR))
    return uvg, r, ii, a, q * rs, rs


def _scan_rows(seg):
    return -(-(SCAN_ROW_T * (seg - 1) + SCAN_ROW_J * (N_SEG - 1) + 1) // SUBLANES) * SUBLANES


def _seg_chunk(j, c):
    return pl.ds(SCAN_ROW_T * SUBLANES * c + SCAN_ROW_J * j, SUBLANES, stride=SCAN_ROW_T)


def _seg_scatter(ref, g, seg, value):
    for j in range(N_SEG):
        for c in range(seg // SUBLANES):
            r0 = j * seg + SUBLANES * c
            ref[g, _seg_chunk(j, c), :] = value[r0:r0 + SUBLANES]


def _scan_tile(a_s, b_s, carry_ref, write_out, seg, reverse, chunks_per_write=1):
    n_g = a_s.shape[0]
    unroll = SCAN_UNROLL if seg % SCAN_UNROLL == 0 else 1

    n_trips = seg // unroll

    def steps(k, state):
        hs, cs = list(state[0]), list(state[1])
        base = ((n_trips - 1 - k) if reverse else k) * unroll
        for q in (range(unroll - 1, -1, -1) if reverse else range(unroll)):
            t = base + q
            rows = pl.ds(t * SCAN_ROW_T, N_SEG, stride=SCAN_ROW_J)
            for g in range(n_g):
                a = a_s[g, rows, :]
                b = b_s[g, rows, :]
                hs[g] = a * hs[g] + b
                cs[g] = a * cs[g]
                b_s[g, rows, :] = hs[g]
                a_s[g, rows, :] = cs[g]
        return tuple(hs), tuple(cs)

    zeros = tuple(jnp.zeros((N_SEG, LANES), F32) for _ in range(n_g))
    ones = tuple(jnp.ones((N_SEG, LANES), F32) for _ in range(n_g))
    h_fin, a_fin = lax.fori_loop(0, seg // unroll, steps, (zeros, ones))

    order = list(range(N_SEG - 1, -1, -1)) if reverse else list(range(N_SEG))
    for g in range(n_g):
        carry = carry_ref[:, g * LANES:(g + 1) * LANES]
        for j in order:
            for c0 in range(0, seg // SUBLANES, chunks_per_write):
                parts = [b_s[g, _seg_chunk(j, c), :] + a_s[g, _seg_chunk(j, c), :] * carry
                         for c in range(c0, c0 + chunks_per_write)]
                write_out(j, c0, g, parts[0] if chunks_per_write == 1 else jnp.concatenate(parts, axis=0))
            carry = a_fin[g][j:j + 1] * carry + h_fin[g][j:j + 1]
        carry_ref[:, g * LANES:(g + 1) * LANES] = carry


def _lru_specs(s, tm, cb, direction_pos, nt):
    n_rows8 = s // SUBLANES
    prev, nxt = _halo_maps(nt, tm, n_rows8, direction_pos)
    tile = pl.BlockSpec((tm, cb), lambda c, i: (direction_pos(i), c))
    return tile, pl.BlockSpec((SUBLANES, cb), prev), pl.BlockSpec((SUBLANES, cb), nxt)


def _lru_param_specs(cb, d):
    n_g = cb // LANES
    vec = pl.BlockSpec((1, cb), lambda c, i: (0, c))
    dvec = pl.BlockSpec((None, 1, cb), lambda c, i: (d, 0, c))
    wmat = pl.BlockSpec((None, n_g, LRU_BLOCK, LRU_BLOCK), lambda c, i: (d, c, 0, 0))
    return vec, dvec, wmat


def _lru_fwd(src, h0, p, d, name, conv, sides=()):
    s = src.shape[0]
    tm = min(TM_LRU, s)
    cb = CB_LRU_FWD
    n_g = cb // LANES
    nt = s // tm
    seg = tm // N_SEG
    pos = (lambda i: i) if d == 0 else (lambda i: nt - 1 - i)

    def body(*refs):
        refs = list(refs)
        u_ref = refs.pop(0)
        if conv:
            up_ref, un_ref, cw_ref, cbias_ref = [refs.pop(0) for _ in range(4)]
        wa_ref, wx_ref, ba_ref, bx_ref, lam_ref, h0_ref, h_ref, hc_ref = [refs.pop(0) for _ in range(8)]
        uv_ref = refs.pop(0) if conv else None
        a_s, b_s = refs
        i = pl.program_id(1)
        tp = pos(i)

        @pl.when(i == 0)
        def _():
            hc_ref[...] = h0_ref[...]

        if conv:
            u_t = u_ref[...]
            um2, um1, up1 = _conv_taps(u_t, up_ref[...], un_ref[...], tp == 0, tp == nt - 1)
            cw = cw_ref[...]
            uv_ref[...] = um2 * cw[0:1] + um1 * cw[1:2] + u_t * cw[2:3] + up1 * cw[3:4] + cbias_ref[...]
        src_ref = uv_ref if conv else u_ref
        cl = LRU_C * _log_sigmoid(lam_ref[...])
        ba, bx = ba_ref[...], bx_ref[...]
        for g in range(n_g):
            uvg, r, ii, a, sq, _ = _lru_gates(src_ref, wa_ref, wx_ref, ba, bx, cl, g)
            b = sq * (ii * uvg)
            _seg_scatter(a_s, g, seg, a)
            _seg_scatter(b_s, g, seg, b)

        per_write = 2 if (seg // SUBLANES) % 2 == 0 else 1

        def write_out(j, c, g, h):
            h_ref[pl.ds(j * seg + SUBLANES * c, SUBLANES * per_write), pl.ds(g * LANES, LANES)] = h.astype(ACT_DTYPE)

        _scan_tile(a_s, b_s, hc_ref, write_out, seg, reverse=(d == 1), chunks_per_write=per_write)

    tile, prev, nxt = _lru_specs(s, tm, cb, pos, nt)
    vec, dvec, wmat = _lru_param_specs(cb, d)
    wide = jax.ShapeDtypeStruct((s, D_INNER), F32)
    conv_specs = [prev, nxt, pl.BlockSpec((4, cb), lambda c, i: (0, c)), vec] if conv else []
    conv_args = [src, src, p["conv_w"], p["conv_b"]] if conv else []
    return _call_with_sides(
        body, sides, name=name,
        out_shape=[jax.ShapeDtypeStruct((s, D_INNER), ACT_DTYPE), jax.ShapeDtypeStruct((1, D_INNER), F32)]
        + ([wide] if conv else []),
        grid=(D_INNER // cb, nt),
        in_specs=[tile] + conv_specs + [wmat, wmat, dvec, dvec, dvec, vec],
        out_specs=[tile, vec] + ([tile] if conv else []),
        scratch_shapes=[pltpu.VMEM((n_g, _scan_rows(seg), LANES), F32)] * 2,
        compiler_params=_cparams(dimension_semantics=("arbitrary", "arbitrary")),
        args=[src, *conv_args, p["wa"], p["wx"], p["ba"], p["bx"], p["lam"], h0])


def _lru_bwd(uv, dh, h, h0, lam_in, p, d, name, sides=()):
    s = uv.shape[0]
    tm = min(TM_LRU, s)
    cb = CB_LRU
    n_g = cb // LANES
    nt = s // tm
    seg = tm // N_SEG
    pos = (lambda i: nt - 1 - i) if d == 0 else (lambda i: i)

    def body(uv_ref, dh_ref, h_ref, hh_ref, wa_ref, wx_ref, ba_ref, bx_ref,
             lam_ref, h0_ref, lin_ref, duv_ref, gwa_ref, gwx_ref, gv_ref, lc_ref, a_s, b_s, lp_s,
             r_s, i_s, q_s, rq_s, a_keep):
        i = pl.program_id(1)
        tp = pos(i)

        @pl.when(i == 0)
        def _():
            lc_ref[...] = lin_ref[...]
            gwa_ref[...] = jnp.zeros_like(gwa_ref)
            gwx_ref[...] = jnp.zeros_like(gwx_ref)
            gv_ref[...] = jnp.zeros_like(gv_ref)

        uv = uv_ref[...]
        lam = lam_ref[...]
        cl = LRU_C * _log_sigmoid(lam)
        ba, bx = ba_ref[...], bx_ref[...]
        dh_t = dh_ref[...].astype(F32)
        carry_in = lc_ref[...]
        for g in range(n_g):
            sl = slice(g * LANES, (g + 1) * LANES)
            _, r, ii, a, sq, rs = _lru_gates(uv, wa_ref, wx_ref, ba, bx, cl, g)
            for ref, val in ((r_s, r), (i_s, ii), (q_s, sq), (rq_s, rs), (a_keep, a)):
                ref[:, sl] = val.astype(ACT_DTYPE)
            b = a * dh_t[:, sl]
            _seg_scatter(a_s, g, seg, a)
            _seg_scatter(b_s, g, seg, b)

        def write_out(j, c, g, v):
            lp_s[pl.ds(j * seg + SUBLANES * c, SUBLANES), pl.ds(g * LANES, LANES)] = v

        _scan_tile(a_s, b_s, lc_ref, write_out, seg, reverse=(d == 0))

        h_t = h_ref[...].astype(F32)
        hh = hh_ref[...].astype(F32)
        if d == 0:
            edge = jnp.where(tp == 0, h0_ref[...], hh[H_HALO - 1:H_HALO])
            h_prev = _shift_down(h_t, edge)
            lam_t = dh_t + _shift_up(lp_s[...], carry_in)
        else:
            edge = jnp.where(tp == nt - 1, h0_ref[...], hh[0:1])
            h_prev = _shift_up(h_t, edge)
            lam_t = dh_t + _shift_down(lp_s[...], carry_in)

        dsig = LRU_C * _sigmoid(-lam)
        cl2 = cl + cl
        for g in range(n_g):
            sl = slice(g * LANES, (g + 1) * LANES)
            uvg = uv[:, sl]
            r, ii, a, sq, rs =[ref[:, sl].astype(F32) for ref in (r_s, i_s, a_keep, q_s, rq_s)]
            lt = lam_t[:, sl]
            ls = lt * sq
            dla = (lt * a) * (h_prev[:, sl] - (ii * uvg) * (a * rs))
            dzr = (dla * cl2[:, sl]) * r * (1.0 - r)
            dzi = ((ls + ls) * uvg) * ii * (1.0 - ii)
            duv_ref[:, sl] = (ls * ii + _dot_nt(dzr, wa_ref[g]) + _dot_nt(dzi, wx_ref[g])).astype(ACT_DTYPE)
            gwa_ref[g] += _dot_tn(uvg, dzr)
            gwx_ref[g] += _dot_tn(uvg, dzi)
            gv_ref[0:1, sl] += _rowsum(dzr)
            gv_ref[1:2, sl] += _rowsum(dzi)
            gv_ref[2:3, sl] += _rowsum(dla * r) * dsig[:, sl]

        @pl.when(i == nt - 1)
        def _():
            gwa_ref[...] = 0.5 * gwa_ref[...]
            gwx_ref[...] = 0.5 * gwx_ref[...]
            gv_ref[0:2, :] = 0.5 * gv_ref[0:2, :]

    tile, _, _ = _lru_specs(s, tm, cb, pos, nt)
    vec, dvec, wmat = _lru_param_specs(cb, d)
    h_prev_map, h_next_map = _halo_maps(nt, tm, s // H_HALO, pos, rows=H_HALO)
    hh_spec = pl.BlockSpec((H_HALO, cb), h_prev_map if d == 0 else h_next_map)
    gw_spec = pl.BlockSpec((n_g, LRU_BLOCK, LRU_BLOCK), lambda c, i: (c, 0, 0))
    n_blk = D_INNER // LRU_BLOCK
    return _call_with_sides(
        body, sides, name=name,
        out_shape=[jax.ShapeDtypeStruct((s, D_INNER), ACT_DTYPE),
                   jax.ShapeDtypeStruct((n_blk, LRU_BLOCK, LRU_BLOCK), F32),
                   jax.ShapeDtypeStruct((n_blk, LRU_BLOCK, LRU_BLOCK), F32),
                   jax.ShapeDtypeStruct((SUBLANES, D_INNER), F32),
                   jax.ShapeDtypeStruct((1, D_INNER), F32)],
        grid=(D_INNER // cb, nt),
        in_specs=[tile, tile, tile, hh_spec, wmat, wmat, dvec, dvec, dvec, vec, vec],
        out_specs=[tile, gw_spec, gw_spec, pl.BlockSpec((SUBLANES, cb), lambda c, i: (0, c)), vec],
        scratch_shapes=[pltpu.VMEM((n_g, _scan_rows(seg), LANES), F32)] * 2 + [pltpu.VMEM((tm, cb), F32)]
        + [pltpu.VMEM((tm, cb), ACT_DTYPE)] * 5,
        compiler_params=_cparams(dimension_semantics=("arbitrary", "arbitrary")),
        args=[uv, dh, h, h, p["wa"], p["wx"], p["ba"], p["bx"], p["lam"], h0, lam_in])


def _out0(hf, hb, g, xt, gt, wo, lg, lb, sc1, sh1, wg1, name):
    t = xt.shape[0]
    tm = min(TM_MM, t)

    def body(hf_ref, hb_ref, g_ref, x_ref, gt_ref, w_ref, lg_ref, lb_ref, sc1_ref, sh1_ref, w1_ref,
             x1_ref, br_ref, u1_ref, g1_ref):
        br = None
        for k in range(D_INNER // WBLK):
            sl = slice(k * WBLK, (k + 1) * WBLK)
            gg = g_ref[:, sl].astype(F32)
            p = (hf_ref[:, sl].astype(F32) + hb_ref[:, sl].astype(F32)) * (gg * _sigmoid(gg))
            part = _dot(p, w_ref[sl, :])
            br = part if br is None else br + part
        z = ALPHA * x_ref[...] + gt_ref[...] * br
        xhat, _ = _layer_norm_stats(z)
        x1 = xhat * lg_ref[...] + lb_ref[...]
        x1_ref[...] = x1
        br_ref[...] = br.astype(ACT_DTYPE)
        h1 = (x1 * (1.0 + sc1_ref[...]) + sh1_ref[...]).astype(MXU_DTYPE)
        for k in range(N_WBLK):
            o = jnp.dot(h1, w1_ref[k], preferred_element_type=F32).astype(ACT_DTYPE)
            if k < N_WBLK // 2:
                u1_ref[:, k * WBLK:(k + 1) * WBLK] = o
            else:
                kk = k - N_WBLK // 2
                g1_ref[:, kk * WBLK:(kk + 1) * WBLK] = o

    wide = pl.BlockSpec((tm, D_INNER), lambda i: (i, 0))
    nar = pl.BlockSpec((tm, D_MODEL), lambda i: (i, 0))
    row = pl.BlockSpec((1, D_MODEL), lambda i: (0, 0))
    return pl.pallas_call(
        body, name=name,
        out_shape=[jax.ShapeDtypeStruct((t, D_MODEL), F32), jax.ShapeDtypeStruct((t, D_MODEL), ACT_DTYPE),
                   jax.ShapeDtypeStruct((t, D_INNER), ACT_DTYPE), jax.ShapeDtypeStruct((t, D_INNER), ACT_DTYPE)],
        grid=(t // tm,),
        in_specs=[wide, wide, wide, nar, row,
                  pl.BlockSpec((D_INNER, D_MODEL), lambda i: (0, 0), pipeline_mode=pl.Buffered(1)), row, row, row, row,
                  pl.BlockSpec((N_WBLK, D_MODEL, WBLK), lambda i: (0, 0, 0), pipeline_mode=pl.Buffered(1))],
        out_specs=[nar, nar, wide, wide],
        compiler_params=_cparams(dimension_semantics=("arbitrary",)),
    )(hf, hb, g, xt, gt, wo, lg, lb, sc1, sh1, wg1)


def _unrolled_loop(n, fn, unroll=4):
    while n % unroll:
        unroll //= 2

    def trip(k, carry):
        for q in range(unroll):
            fn(k * unroll + q)
        return carry
    lax.fori_loop(0, n // unroll, trip, 0)


def _window(n, w):
    t = np.arange(n)
    return np.clip(t - w // 2, 0, n), np.clip(t + w // 2, 0, n)


def _pool_tables(n_rows, transpose):
    boxes, inv_c, inv_r = [], [], []
    for w in POOL_WINDOWS:
        lo, hi = _window(GRID_W, w)
        m = np.zeros((GRID_W, GRID_W), np.float32)
        for r in range(GRID_W):
            m[r, lo[r]:hi[r]] = 1.0
        m = np.kron(np.eye(POOL_TOK // GRID_W, dtype=np.float32), m)
        boxes.append(m.T if transpose else m)
        inv_c.append(np.broadcast_to((1.0 / (hi - lo).astype(np.float32))[:, None], (GRID_W, LANES)))
        lo_r, hi_r = _window(n_rows, w)
        inv_r.append(1.0 / (hi_r - lo_r).astype(np.float32))
    return (jnp.asarray(np.stack(boxes), MXU_DTYPE), jnp.asarray(np.stack(inv_c), F32),
            jnp.asarray(np.stack(inv_r), F32))


def _pool_mix(xin, transpose, out_dtype, name):
    s = xin.shape[0]
    n_rows = s // GRID_W
    pad_t = SUBLANES * GRID_W
    rows_per_blk = POOL_TOK // GRID_W
    n_slab = D_INNER // LANES
    slabs_per_group = POOL_GROUP // LANES
    n_win = len(POOL_WINDOWS)
    boxes, inv_c, inv_r = _pool_tables(n_rows, transpose)
    exact_operand = (not transpose) and xin.dtype == MXU_DTYPE and MXU_DTYPE != F32

    def body(invr_ref, box_ref, invc_ref, x_ref, o_ref, pad_s):
        k = pl.program_id(0) // slabs_per_group
        pad_s[pl.ds(0, pad_t), :] = jnp.zeros((pad_t, LANES), F32)
        pad_s[pl.ds(pad_t + s, pad_t), :] = jnp.zeros((pad_t, LANES), F32)

        for kk, w in enumerate(POOL_WINDOWS):
            half = w // 2
            offsets = list(range(-(half - 1), half + 1)) if transpose else list(range(-half, half))

            @pl.when(k == kk)
            def _():
                inv_col = invc_ref[kk]

                def col_box(b):
                    st = pl.multiple_of(b * POOL_TOK, POOL_TOK)
                    xb = x_ref[pl.ds(st, POOL_TOK), :]
                    if exact_operand:
                        pad_s[pl.ds(pad_t + st, POOL_TOK), :] = jnp.dot(box_ref[kk], xb, preferred_element_type=F32)
                        return
                    xb = xb.astype(F32)
                    if transpose:
                        xb = xb * jnp.concatenate(
                            [inv_col * invr_ref[kk, b * rows_per_blk + q] for q in range(rows_per_blk)], axis=0)
                    hi = xb.astype(MXU_DTYPE)
                    lo = (xb - hi.astype(F32)).astype(MXU_DTYPE)
                    both = jnp.dot(box_ref[kk], jnp.concatenate([hi, lo], axis=1), preferred_element_type=F32)
                    pad_s[pl.ds(pad_t + st, POOL_TOK), :] = both[:, :LANES] + both[:, LANES:]
                _unrolled_loop(s // POOL_TOK, col_box, unroll=8)

                def row_box(r):
                    st = pl.multiple_of(r * GRID_W, GRID_W)
                    acc = pad_s[pl.ds(pad_t + st + offsets[0] * GRID_W, GRID_W), :]
                    for o in offsets[1:]:
                        acc = acc + pad_s[pl.ds(pad_t + st + o * GRID_W, GRID_W), :]
                    if not transpose:
                        acc = acc * (inv_col * invr_ref[kk, r])
                    o_ref[pl.ds(st, GRID_W), :] = (acc - x_ref[pl.ds(st, GRID_W), :].astype(F32)).astype(out_dtype)
                _unrolled_loop(n_rows, row_box)

    slab = pl.BlockSpec((s, LANES), lambda i: (0, i))
    return pl.pallas_call(
        body, name=name, out_shape=jax.ShapeDtypeStruct((s, D_INNER), out_dtype), grid=(n_slab,),
        in_specs=[pl.BlockSpec(memory_space=pltpu.SMEM),
                  pl.BlockSpec((n_win, POOL_TOK, POOL_TOK), lambda i: (0, 0, 0)),
                  pl.BlockSpec((n_win, GRID_W, LANES), lambda i: (0, 0, 0)), slab],
        out_specs=slab,
        scratch_shapes=[pltpu.VMEM((s + 2 * pad_t, LANES), F32)],
        compiler_params=_cparams(dimension_semantics=("arbitrary",)),
    )(inv_r, boxes, inv_c, xin)


def _out1(dmix, pw, ps, g, x1, gt, wo, lg, lb, tgt, name):
    t = x1.shape[0]
    tm = min(TM_MM, t)
    n_grp = len(POOL_WINDOWS)

    def body(d_ref, pw_ref, ps_ref, g_ref, x1_ref, gt_ref, w_ref, lg_ref, lb_ref, tgt_ref, dz_ref, st_ref, po_ref):
        @pl.when(pl.program_id(0) == 0)
        def _():
            st_ref[...] = jnp.zeros_like(st_ref)

        br = jnp.zeros((tm, D_MODEL), F32)
        for k in range(n_grp):
            sl = slice(k * POOL_GROUP, (k + 1) * POOL_GROUP)
            po = jnp.dot(d_ref[:, sl], pw_ref[k], preferred_element_type=F32)
            po_ref[:, sl] = po.astype(ACT_DTYPE)
            y = po * ps_ref[:, sl]
            gg = g_ref[:, sl].astype(F32)
            br = br + _dot(y * (gg * _sigmoid(gg)), w_ref[sl, :])
        z = ALPHA * x1_ref[...] + gt_ref[...] * br
        xhat, rstd = _layer_norm_stats(z)
        lg_v = lg_ref[...]
        err = xhat * lg_v + lb_ref[...] - tgt_ref[...]
        dy = err * (1.0 / D_MODEL)
        dz = _layer_norm_bwd(dy, xhat, rstd, lg_v)
        dz_ref[...] = dz
        st_ref[0:1, :] += _rowsum(dy * xhat)
        st_ref[1:2, :] += _rowsum(dy)
        st_ref[2:3, :] += _rowsum(dz * br)
        st_ref[3:4, :] += _rowsum(err * err)

    wide = pl.BlockSpec((tm, D_INNER), lambda i: (i, 0))
    nar = pl.BlockSpec((tm, D_MODEL), lambda i: (i, 0))
    row = pl.BlockSpec((1, D_MODEL), lambda i: (0, 0))
    return pl.pallas_call(
        body, name=name,
        out_shape=[jax.ShapeDtypeStruct((t, D_MODEL), F32), jax.ShapeDtypeStruct((SUBLANES, D_MODEL), F32),
                   jax.ShapeDtypeStruct((t, D_INNER), ACT_DTYPE)],
        grid=(t // tm,),
        in_specs=[wide, pl.BlockSpec((n_grp, POOL_GROUP, POOL_GROUP), lambda i: (0, 0, 0)),
                  pl.BlockSpec((1, D_INNER), lambda i: (0, 0)), wide, nar, row,
                  pl.BlockSpec((D_INNER, D_MODEL), lambda i: (0, 0), pipeline_mode=pl.Buffered(1)), row, row, nar],
        out_specs=[nar, pl.BlockSpec((SUBLANES, D_MODEL), lambda i: (0, 0)), wide],
        compiler_params=_cparams(dimension_semantics=("arbitrary",)),
    )(dmix, pw, ps, g, x1, gt, wo, lg, lb, tgt)


def _flush(acc, out_hbm, sem):
    cp = pltpu.make_async_copy(acc, out_hbm, sem)
    cp.start()
    cp.wait()


def _bout1(dz, dmix, po, g, pw, ps, gt, wo, name):
    t = dz.shape[0]
    tm = min(TM_MM, t)
    nt = t // tm
    n_grp = len(POOL_WINDOWS)

    def body(dz_ref, d_ref, po_ref, g_ref, pw_ref, ps_ref, gt_ref, w_ref, dd_ref, dg_ref, gwo_hbm, gpw_hbm, gps_ref,
             gwo_acc, gpw_acc, sems):
        i = pl.program_id(0)

        @pl.when(i == 0)
        def _():
            gwo_acc[...] = jnp.zeros_like(gwo_acc)
            gpw_acc[...] = jnp.zeros_like(gpw_acc)
            gps_ref[...] = jnp.zeros_like(gps_ref)

        db = (gt_ref[...] * dz_ref[...]).astype(MXU_DTYPE)
        for k in range(n_grp):
            sl = slice(k * POOL_GROUP, (k + 1) * POOL_GROUP)
            dk = d_ref[:, sl]
            po = po_ref[:, sl].astype(F32)
            psk = ps_ref[:, sl]
            y = po * psk
            gg = g_ref[:, sl].astype(F32)
            sg = _sigmoid(gg)
            silu = gg * sg
            gwo_acc[sl, :] += _dot_tn(y * silu, db)
            dp = _dot_nt(db, w_ref[sl, :])
            dy = dp * silu
            dg_ref[:, sl] = (dp * y * (sg * (1.0 + gg * (1.0 - sg)))).astype(MXU_DTYPE)
            gps_ref[0:1, sl] += _rowsum(dy * po)
            dpo = (dy * psk).astype(MXU_DTYPE)
            gpw_acc[k] += _dot_tn(dk, dpo)
            dd_ref[:, sl] = _dot_nt(dpo, pw_ref[k])

        @pl.when(i == nt - 1)
        def _():
            _flush(gwo_acc, gwo_hbm, sems.at[0])
            _flush(gpw_acc, gpw_hbm, sems.at[1])

    wide = pl.BlockSpec((tm, D_INNER), lambda i: (i, 0))
    nar = pl.BlockSpec((tm, D_MODEL), lambda i: (i, 0))
    return pl.pallas_call(
        body, name=name,
        out_shape=[jax.ShapeDtypeStruct((t, D_INNER), F32), jax.ShapeDtypeStruct((t, D_INNER), MXU_DTYPE),
                   jax.ShapeDtypeStruct((D_INNER, D_MODEL), F32),
                   jax.ShapeDtypeStruct((n_grp, POOL_GROUP, POOL_GROUP), F32),
                   jax.ShapeDtypeStruct((SUBLANES, D_INNER), F32)],
        grid=(nt,),
        in_specs=[nar, wide, wide, wide,
                  pl.BlockSpec((n_grp, POOL_GROUP, POOL_GROUP), lambda i: (0, 0, 0), pipeline_mode=pl.Buffered(1)),
                  pl.BlockSpec((1, D_INNER), lambda i: (0, 0)), pl.BlockSpec((1, D_MODEL), lambda i: (0, 0)),
                  pl.BlockSpec((D_INNER, D_MODEL), lambda i: (0, 0), pipeline_mode=pl.Buffered(1))],
        out_specs=[wide, wide, ANY, ANY, pl.BlockSpec((SUBLANES, D_INNER), lambda i: (0, 0))],
        scratch_shapes=[pltpu.VMEM((D_INNER, D_MODEL), F32), pltpu.VMEM((n_grp, POOL_GROUP, POOL_GROUP), F32),
                        pltpu.SemaphoreType.DMA((2,))],
        compiler_params=_cparams(dimension_semantics=("arbitrary",)),
    )(dz, dmix, po, g, pw, ps, gt, wo)


def _bout0(dx1, xt, br0, lg, hf, hb, g, gt, wo, name, sides=()):
    t = dx1.shape[0]
    tm = min(TM_MM, t)
    nt = t // tm

    def body(dx_ref, x_ref, br_ref, lg_ref, hf_ref, hb_ref, g_ref, gt_ref, w_ref,
             dz_ref, dy_ref, dg_ref, gwo_hbm, st_ref, gwo_acc, sem):
        i = pl.program_id(0)

        @pl.when(i == 0)
        def _():
            gwo_acc[...] = jnp.zeros_like(gwo_acc)
            st_ref[...] = jnp.zeros_like(st_ref)

        dx = dx_ref[...]
        br = br_ref[...].astype(F32)
        gate = gt_ref[...]
        xhat, rstd = _layer_norm_stats(ALPHA * x_ref[...] + gate * br)
        dz = _layer_norm_bwd(dx, xhat, rstd, lg_ref[...])
        dz_ref[...] = dz
        st_ref[0:1, :] += _rowsum(dx * xhat)
        st_ref[1:2, :] += _rowsum(dx)
        st_ref[2:3, :] += _rowsum(dz * br)
        db = (gate * dz).astype(MXU_DTYPE)
        for k in range(D_INNER // WBLK):
            sl = slice(k * WBLK, (k + 1) * WBLK)
            y = hf_ref[:, sl].astype(F32) + hb_ref[:, sl].astype(F32)
            gg = g_ref[:, sl].astype(F32)
            sg = _sigmoid(gg)
            silu = gg * sg
            gwo_acc[sl, :] += _dot_tn(y * silu, db)
            dp = _dot_nt(db, w_ref[sl, :])
            dy_ref[:, sl] = (dp * silu).astype(ACT_DTYPE)
            dg_ref[:, sl] = (dp * y * (sg * (1.0 + gg * (1.0 - sg)))).astype(MXU_DTYPE)

        @pl.when(i == nt - 1)
        def _():
            _flush(gwo_acc, gwo_hbm, sem)

    wide = pl.BlockSpec((tm, D_INNER), lambda i: (i, 0))
    nar = pl.BlockSpec((tm, D_MODEL), lambda i: (i, 0))
    row = pl.BlockSpec((1, D_MODEL), lambda i: (0, 0))
    return _call_with_sides(
        body, sides, name=name,
        out_shape=[jax.ShapeDtypeStruct((t, D_MODEL), F32), jax.ShapeDtypeStruct((t, D_INNER), ACT_DTYPE),
                   jax.ShapeDtypeStruct((t, D_INNER), MXU_DTYPE), jax.ShapeDtypeStruct((D_INNER, D_MODEL), F32),
                   jax.ShapeDtypeStruct((SUBLANES, D_MODEL), F32)],
        grid=(nt,),
        in_specs=[nar, nar, nar, row, wide, wide, wide, row,
                  pl.BlockSpec((D_INNER, D_MODEL), lambda i: (0, 0), pipeline_mode=pl.Buffered(1))],
        out_specs=[nar, wide, wide, ANY, pl.BlockSpec((SUBLANES, D_MODEL), lambda i: (0, 0))],
        scratch_shapes=[pltpu.VMEM((D_INNER, D_MODEL), F32), pltpu.SemaphoreType.DMA(())],
        compiler_params=_cparams(dimension_semantics=("arbitrary",)),
        args=[dx1, xt, br0, lg, hf, hb, g, gt, wo])


def _conv_bwd(duvf, duvb, u, conv_w, name, sides=()):
    s = u.shape[0]
    tm = min(TM_LRU, s)
    cb = CB_LRU_FWD
    nt = s // tm

    def body(df_ref, dfp_ref, dfn_ref, db_ref, dbp_ref, dbn_ref, u_ref, cw_ref, du_ref, cst_ref):
        i = pl.program_id(1)

        @pl.when(i == 0)
        def _():
            cst_ref[...] = jnp.zeros_like(cst_ref)

        first, last = i == 0, i == nt - 1
        pz = jnp.where(first, 0.0, 1.0)
        nz = jnp.where(last, 0.0, 1.0)
        dout = df_ref[...].astype(F32) + db_ref[...].astype(F32)
        before = (dfp_ref[...].astype(F32) + dbp_ref[...].astype(F32))[H_HALO - SUBLANES:] * pz
        after = (dfn_ref[...].astype(F32) + dbn_ref[...].astype(F32))[:SUBLANES] * nz
        dm1, dp1, dp2 = _shifted(dout, before, after, [-1, 1, 2])
        cw = cw_ref[...]
        du_ref[...] = (dp2 * cw[0:1] + dp1 * cw[1:2] + dout * cw[2:3] + dm1 * cw[3:4]).astype(MXU_DTYPE)
        u_t = u_ref[...]
        cst_ref[0:1, :] += _rowsum(dp2 * u_t)
        cst_ref[1:2, :] += _rowsum(dp1 * u_t)
        cst_ref[2:3, :] += _rowsum(dout * u_t)
        cst_ref[3:4, :] += _rowsum(dm1 * u_t)
        cst_ref[4:5, :] += _rowsum(dout)

    tile, _, _ = _lru_specs(s, tm, cb, lambda i: i, nt)
    prev_map, next_map = _halo_maps(nt, tm, s // H_HALO, lambda i: i, rows=H_HALO)
    prev, nxt = pl.BlockSpec((H_HALO, cb), prev_map), pl.BlockSpec((H_HALO, cb), next_map)
    return _call_with_sides(
        body, sides, name=name,
        out_shape=[jax.ShapeDtypeStruct((s, D_INNER), MXU_DTYPE), jax.ShapeDtypeStruct((SUBLANES, D_INNER), F32)],
        grid=(D_INNER // cb, nt),
        in_specs=[tile, prev, nxt] * 2 + [tile, pl.BlockSpec((4, cb), lambda c, i: (0, c))],
        out_specs=[tile, pl.BlockSpec((SUBLANES, cb), lambda c, i: (0, c))], scratch_shapes=[],
        compiler_params=_cparams(dimension_semantics=("arbitrary", "arbitrary")),
        args=[duvf, duvf, duvf, duvb, duvb, duvb, u, conv_w])


def _bin(du, dg, xin, dzin, sc, sh, wg, name, gw_init=None, sides=()):
    t = xin.shape[0]
    tm = min(TM_MM, t)
    nt = t // tm
    has_g, has_dx, has_init = dg is not None, dzin is not None, gw_init is not None
    half = N_WBLK // 2
    n_blk = N_WBLK if has_g else half

    def body(*refs):
        refs = list(refs)
        du_ref = refs.pop(0)
        dg_ref = refs.pop(0) if has_g else None
        x_ref = refs.pop(0)
        dz_ref = refs.pop(0) if has_dx else None
        sc_ref, sh_ref, w_ref = refs.pop(0), refs.pop(0), refs.pop(0)
        init_hbm = refs.pop(0) if has_init else None
        dx_ref = refs.pop(0) if has_dx else None
        gw_hbm, st_ref, gw_acc, sem = refs
        i = pl.program_id(0)

        @pl.when(i == 0)
        def _():
            st_ref[...] = jnp.zeros_like(st_ref)
            first_zero = 0
            if has_init:
                _flush(init_hbm, gw_acc.at[pl.ds(0, half)], sem)
                first_zero = half
            for k in range(first_zero, n_blk):
                gw_acc[k] = jnp.zeros((D_MODEL, WBLK), F32)

        xv = x_ref[...]
        scale = 1.0 + sc_ref[...]
        h = (xv * scale + sh_ref[...]).astype(MXU_DTYPE)
        dh = None
        for k in range(n_blk):
            src = du_ref if k < half else dg_ref
            kk = k % half
            dk = src[:, kk * WBLK:(kk + 1) * WBLK]
            gw_acc[k] += _dot_tn(h, dk)
            contrib = _dot_nt(dk, w_ref[k])
            dh = contrib if dh is None else dh + contrib
        st_ref[0:1, :] += _rowsum(dh * xv)
        st_ref[1:2, :] += _rowsum(dh)
        if has_dx:
            dx_ref[...] = ALPHA * dz_ref[...] + dh * scale

        @pl.when(i == nt - 1)
        def _():
            _flush(gw_acc, gw_hbm, sem)

    wide = pl.BlockSpec((tm, D_INNER), lambda i: (i, 0))
    nar = pl.BlockSpec((tm, D_MODEL), lambda i: (i, 0))
    row = pl.BlockSpec((1, D_MODEL), lambda i: (0, 0))
    wspec = pl.BlockSpec((n_blk, D_MODEL, WBLK), lambda i: (0, 0, 0), pipeline_mode=pl.Buffered(1))
    in_specs = ([wide] + ([wide] if has_g else []) + [nar] + ([nar] if has_dx else []) + [row, row, wspec]
                + ([ANY] if has_init else []))
    args = ([du] + ([dg] if has_g else []) + [xin] + ([dzin] if has_dx else []) + [sc, sh, wg]
            + ([gw_init] if has_init else []))
    out_shape = ([jax.ShapeDtypeStruct((t, D_MODEL), F32)] if has_dx else []) + [
        jax.ShapeDtypeStruct((n_blk, D_MODEL, WBLK), F32), jax.ShapeDtypeStruct((SUBLANES, D_MODEL), F32)]
    out_specs = ([nar] if has_dx else []) + [ANY, pl.BlockSpec((SUBLANES, D_MODEL), lambda i: (0, 0))]
    return _call_with_sides(
        body, sides, name=name, out_shape=out_shape, grid=(nt,), in_specs=in_specs, out_specs=out_specs,
        scratch_shapes=[pltpu.VMEM((n_blk, D_MODEL, WBLK), F32), pltpu.SemaphoreType.DMA(())],
        compiler_params=_cparams(dimension_semantics=("arbitrary",)), args=args)


def _blocks_by_device(a, axis):
    shape = a.shape
    a = a.reshape(shape[:axis] + (N_DEV, shape[axis] // N_DEV) + shape[axis + 1:])
    return jnp.moveaxis(a, axis, 0)


def kernel(x, c, ctx, c_ctx, w_mod, b_mod, w_in, w_out, ln_g, ln_b, conv_w, conv_b, lru_wa, lru_ba, lru_wx, lru_bx, lru_lam, pool_w, pool_scale, loss_target, m_c_ctx, m_w_mod, m_b_mod, m_w_in, m_w_out, m_ln_g, m_ln_b, m_conv_w, m_conv_b, m_lru_wa, m_lru_ba, m_lru_wx, m_lru_bx, m_lru_lam, m_pool_w, m_pool_scale, v_c_ctx, v_w_mod, v_b_mod, v_w_in, v_w_out, v_ln_g, v_ln_b, v_conv_w, v_conv_b, v_lru_wa, v_lru_ba, v_lru_wx, v_lru_bx, v_lru_lam, v_pool_w, v_pool_scale):
    xi, yi, ci = _my_pos()
    dev = 4 * xi + 2 * yi + ci
    xt, ctxt, tgt = x[0], ctx[0], loss_target[0]
    n_mod = w_mod.shape[2]

    small_shapes = [(D_MODEL,), conv_w.shape[1:], lru_ba.shape[1:], lru_bx.shape[1:], lru_lam.shape[1:],
                    pool_scale.shape[1:]]
    small = _to_rows([c[0], conv_w[0], lru_ba[0], lru_bx[0], lru_lam[0], pool_scale[0]], SUBLANES)
    small_all, wi0 = _all_gather([small, w_in[0].astype(MXU_DTYPE)], "gather_first")
    pieces = [_split_rows(small_all[k], small_shapes) for k in range(N_DEV)]
    c_all = jnp.stack([p[0] for p in pieces])
    conv_w_f = jnp.concatenate([p[1] for p in pieces], axis=-1)
    lru_ba_f = jnp.concatenate([p[2] for p in pieces], axis=-1)[:, None, :]
    lru_bx_f = jnp.concatenate([p[3] for p in pieces], axis=-1)[:, None, :]
    lru_lam_f = jnp.concatenate([p[4] for p in pieces], axis=-1)[:, None, :]
    pool_scale_f = jnp.concatenate([p[5] for p in pieces], axis=-1)[None, :]

    cond = jnp.concatenate([c_all, jnp.broadcast_to(c_ctx[None, :], (N_DEV, D_MODEL))], axis=0)
    b_my = lax.dynamic_slice(b_mod, (0, dev * n_mod), (2, n_mod))[:, None, :]
    mod_part = _mod_fwd(cond, w_mod, b_my, "mod_fwd")
    mod_all, = _all_gather([mod_part], "gather_mod")
    mod = jnp.transpose(mod_all, (1, 2, 0, 3)).reshape(2, 16, 3 * D_MODEL)
    mod_me = lax.dynamic_slice(mod, (0, dev, 0), (2, 1, 3 * D_MODEL))
    sh = [mod_me[i, :, 0:D_MODEL] for i in range(2)]
    sc = [mod_me[i, :, D_MODEL:2 * D_MODEL] for i in range(2)]
    gt = [mod_me[i, :, 2 * D_MODEL:] for i in range(2)]
    shc, scc = mod[0, 8:9, 0:D_MODEL], mod[0, 8:9, D_MODEL:2 * D_MODEL]

    lg = [ln_g[i][None, :] for i in range(2)]
    lb = [ln_b[i][None, :] for i in range(2)]
    lru_p = dict(conv_w=conv_w_f, conv_b=conv_b, wa=(0.5 * lru_wa[0]).astype(MXU_DTYPE),
                 wx=(0.5 * lru_wx[0]).astype(MXU_DTYPE), ba=0.5 * lru_ba_f, bx=0.5 * lru_bx_f, lam=lru_lam_f)
    zero_state = jnp.zeros((1, D_INNER), F32)

    (u0, g0), (wo0,) = _in_proj(xt, sc[0], sh[0], wi0, "in_proj0", sides=[("gather", [w_out[0].astype(MXU_DTYPE)])])
    (uc, _), _ = _in_proj(ctxt, scc, shc, wi0, "in_proj0_ctx")
    (hcf, cf, uvc), _ = _lru_fwd(uc, zero_state, lru_p, 0, "lru_fwd_ctx_f", conv=True)
    (hcb, cbk), _ = _lru_fwd(uvc, zero_state, lru_p, 1, "lru_fwd_ctx_b", conv=False)
    (hf, _, uv0), (wi1,) = _lru_fwd(u0, cf, lru_p, 0, "lru_fwd_f", conv=True,
                                    sides=[("gather", [w_in[1].astype(MXU_DTYPE)])])
    (hb, _), (wo1, pool_w_g) = _lru_fwd(
        uv0, cbk, lru_p, 1, "lru_fwd_b", conv=False,
        sides=[("gather", [w_out[1].astype(MXU_DTYPE), pool_w[0].astype(MXU_DTYPE)])])
    w_in_l = [wi0, wi1]
    w_out_l = [wo0.reshape(D_INNER, D_MODEL), wo1.reshape(D_INNER, D_MODEL)]
    pool_w_f = jnp.transpose(pool_w_g, (1, 0, 2, 3)).reshape(len(POOL_WINDOWS), POOL_GROUP, POOL_GROUP)
    x1, br0, u1, g1 = _out0(hf, hb, g0, xt, gt[0], w_out_l[0], lg[0], lb[0], sc[1], sh[1], w_in_l[1], "out0_in1")
    dmix = _pool_mix(u1, False, MXU_DTYPE, "pool_fwd")
    dz1, st1, po1 = _out1(dmix, pool_w_f, pool_scale_f, g1, x1, gt[1], w_out_l[1], lg[1], lb[1], tgt, "out1")
    loss_me = jnp.full((1, LANES), (0.5 / D_MODEL) * jnp.sum(st1[3]), F32)

    core = jnp.reshape(ci, (1,)).astype(jnp.int32)
    wo_view = lambda a: a.reshape(N_DEV, D_INNER // N_DEV, D_MODEL)
    pw_view = lambda a: _blocks_by_device(a, 1).reshape(N_DEV, POOL_GROUP // N_DEV * len(POOL_WINDOWS), POOL_GROUP)
    dd, dg1, gwo1, gpw, gps = _bout1(dz1, dmix, po1, g1, pool_w_f, pool_scale_f, gt[1], w_out_l[1], "bwd_out1")
    du1 = _pool_mix(dd, True, MXU_DTYPE, "pool_bwd")
    (dx1, gwi1, stb1), _ = _bin(du1, dg1, x1, dz1, sc[1], sh[1], w_in_l[1], "bwd_in1")
    bufs1 = [gwi1, wo_view(gwo1), pw_view(gpw)]
    (dz0, dy0, dg0, gwo0, stl0), recv1 = _bout0(dx1, xt, br0, lg[0], hf, hb, g0, gt[0], w_out_l[0], "bwd_out0",
                                                sides=[("sibling", bufs1)])
    pairs1 = [_pair_sum(b, r, core, "reduce_pair_" + n)
              for b, r, n in zip(bufs1, recv1, ["w_in1", "w_out1", "pool_w"])]
    (duvf, gwa_f, gwx_f, gv_f, dh0f), (p_wi1, p_wo1, p_pw, recv_wo0) = _lru_bwd(
        uv0, dy0, hf, cf, zero_state, lru_p, 0, "lru_bwd_f", sides=[("chips", pairs1), ("sibling", [wo_view(gwo0)])])
    pair_wo0 = _pair_sum(wo_view(gwo0), recv_wo0, core, "reduce_pair_w_out0")
    (duvb, gwa_b, gwx_b, gv_b, dh0b), (p_wo0,) = _lru_bwd(
        uv0, dy0, hb, cbk, zero_state, lru_p, 1, "lru_bwd_b", sides=[("chips", [pair_wo0])])
    zero_dh = jnp.zeros(uc.shape, ACT_DTYPE)
    (ducf, gwa_cf, gwx_cf, gv_cf, _), _ = _lru_bwd(uvc, zero_dh, hcf, zero_state, dh0f, lru_p, 0, "lru_bwd_ctx_f")
    (ducb, gwa_cb, gwx_cb, gv_cb, _), _ = _lru_bwd(uvc, zero_dh, hcb, zero_state, dh0b, lru_p, 1, "lru_bwd_ctx_b")

    def pack(sharded, replicated):
        sh_sizes = [int(np.prod(a.shape[1:])) for a in sharded]
        rep_sizes = [a.shape[0] // N_DEV for a in replicated]
        n_flat = sum(sh_sizes) + sum(rep_sizes)
        rows = -(-(-(-n_flat // LANES)) // FLAT_ROWS) * FLAT_ROWS
        buf = jnp.concatenate([a.reshape(N_DEV, -1) for a in sharded + replicated], axis=1)
        return jnp.pad(buf, ((0, 0), (0, rows * LANES - n_flat))).reshape(N_DEV, rows, LANES), sh_sizes, rep_sizes

    def unpack(reduced, sh_sizes, rep_sizes, sh_shapes):
        flat = reduced.reshape(-1)
        offs = np.cumsum([0] + sh_sizes)
        mine = [flat[offs[k]:offs[k + 1]].reshape(s) for k, s in enumerate(sh_shapes)]
        return mine, _to_rows([flat[offs[-1]:offs[-1] + sum(rep_sizes)]], FLAT_ROWS)

    def spread(rep_all, rep_sizes, shapes):
        flat = rep_all.reshape(N_DEV, -1)
        offs = np.cumsum([0] + rep_sizes)
        return [flat[:, offs[k]:offs[k + 1]].reshape(s) for k, s in enumerate(shapes)]

    (du0, cst0), _ = _conv_bwd(duvf, duvb, u0, conv_w_f, "conv_bwd")
    (duc, cstc), _ = _conv_bwd(ducf, ducb, uc, conv_w_f, "conv_bwd_ctx")
    (gwic, stc), _ = _bin(duc, None, ctxt, None, scc, shc, w_in_l[0][:N_WBLK // 2], "bwd_in0_ctx")
    (gx, gwi0, stb0), _ = _bin(du0, dg0, xt, dz0, sc[0], sh[0], w_in_l[0], "bwd_in0", gw_init=gwic)

    zero_row = jnp.zeros((1, D_MODEL), F32)
    dm_me = jnp.stack([
        jnp.concatenate([jnp.concatenate([stb0[1:2], stb0[0:1], stl0[2:3]], axis=1),
                         jnp.concatenate([stc[1:2], stc[0:1], zero_row], axis=1)], axis=0),
        jnp.concatenate([jnp.concatenate([stb1[1:2], stb1[0:1], st1[2:3]], axis=1),
                         jnp.zeros((1, 3 * D_MODEL), F32)], axis=0)])
    dm_g, loss_g = _all_gather([dm_me, loss_me], "gather_dmod")
    loss = jnp.sum(loss_g[:, 0, 0])
    dm_all = jnp.concatenate([jnp.transpose(dm_g[:, :, 0], (1, 0, 2)), jnp.transpose(dm_g[:, :, 1], (1, 0, 2))],
                             axis=1)
    dm_my = lax.dynamic_slice(dm_all, (0, 0, dev * n_mod), (2, 16, n_mod))
    g_w_mod, g_b_mod, gcc_part = _mod_bwd(cond, dm_all, dm_my, w_mod, "mod_bwd")
    g_b_mod = g_b_mod.reshape(b_mod.shape)

    gwa = jnp.stack([gwa_f + gwa_cf, gwa_b + gwa_cb])
    gwx = jnp.stack([gwx_f + gwx_cf, gwx_b + gwx_cb])
    gv = jnp.stack([gv_f + gv_cf, gv_b + gv_cb])
    cst = cst0 + cstc
    misc, m_sh, m_rep = pack(
        [_blocks_by_device(cst[0:4], 1), _blocks_by_device(gv[:, 0], 1), _blocks_by_device(gv[:, 1], 1),
         _blocks_by_device(gv[:, 2], 1), _blocks_by_device(gps[0], 0)],
        [gwa.reshape(-1), gwx.reshape(-1), jnp.stack([stl0[0], st1[0]]).reshape(-1),
         jnp.stack([stl0[1], st1[1]]).reshape(-1), cst[4], gcc_part.reshape(-1)])
    bufs = [gwi0, misc]
    recvs = _sibling_exchange(bufs, "reduce_sibling")
    pairs = [_pair_sum(b, r, core, "reduce_pair_" + n) for b, r, n in zip(bufs, recvs, ["w_in0", "misc"])]
    p_wi0, p_misc = _chip_exchange(pairs, "reduce_chips")
    (g_conv_w, g_lru_ba, g_lru_bx, g_lru_lam, g_pool_scale), rep_mine = unpack(
        _sum4(p_misc, "reduce_sum_misc"), m_sh, m_rep,
        [conv_w.shape, lru_ba.shape, lru_bx.shape, lru_lam.shape, pool_scale.shape])
    rep_all, = _all_gather([rep_mine.astype(WIRE_DTYPE)], "gather_replicated")
    rep_all = rep_all.astype(F32)
    g_lru_wa, g_lru_wx, g_ln_g, g_ln_b, g_conv_b, g_c_ctx = spread(
        rep_all, m_rep, [lru_wa.shape, lru_wx.shape, ln_g.shape, ln_b.shape, conv_b.shape, c_ctx.shape])

    names = ["c_ctx", "w_mod", "b_mod", "w_in", "w_out", "ln_g", "ln_b", "conv_w", "conv_b", "lru_wa", "lru_ba",
             "lru_wx", "lru_bx", "lru_lam", "pool_w", "pool_scale"]
    weights = dict(c_ctx=c_ctx, w_mod=w_mod, b_mod=b_mod, w_in=w_in, w_out=w_out, ln_g=ln_g, ln_b=ln_b,
                   conv_w=conv_w, conv_b=conv_b, lru_wa=lru_wa, lru_ba=lru_ba, lru_wx=lru_wx, lru_bx=lru_bx,
                   lru_lam=lru_lam, pool_w=pool_w, pool_scale=pool_scale)
    mom_m = dict(c_ctx=m_c_ctx, w_mod=m_w_mod, b_mod=m_b_mod, w_in=m_w_in, w_out=m_w_out, ln_g=m_ln_g, ln_b=m_ln_b,
                 conv_w=m_conv_w, conv_b=m_conv_b, lru_wa=m_lru_wa, lru_ba=m_lru_ba, lru_wx=m_lru_wx,
                 lru_bx=m_lru_bx, lru_lam=m_lru_lam, pool_w=m_pool_w, pool_scale=m_pool_scale)
    mom_v = dict(c_ctx=v_c_ctx, w_mod=v_w_mod, b_mod=v_b_mod, w_in=v_w_in, w_out=v_w_out, ln_g=v_ln_g, ln_b=v_ln_b,
                 conv_w=v_conv_w, conv_b=v_conv_b, lru_wa=v_lru_wa, lru_ba=v_lru_ba, lru_wx=v_lru_wx,
                 lru_bx=v_lru_bx, lru_lam=v_lru_lam, pool_w=v_pool_w, pool_scale=v_pool_scale)
    grads = dict(c_ctx=g_c_ctx, w_mod=g_w_mod, b_mod=g_b_mod, ln_g=g_ln_g, ln_b=g_ln_b,
                 conv_w=g_conv_w, conv_b=g_conv_b, lru_wa=g_lru_wa, lru_ba=g_lru_ba, lru_wx=g_lru_wx,
                 lru_bx=g_lru_bx, lru_lam=g_lru_lam)
    grads["pool_scale"] = g_pool_scale
    delta, new_m, new_v = {}, {}, {}

    def update_parts(n, parts, view):
        res = _adamw_parts(weights[n].reshape(view), parts, mom_m[n].reshape(view), mom_v[n].reshape(view),
                           "adamw_" + n)
        grads[n], delta[n], new_m[n], new_v[n] = [r.reshape(weights[n].shape) for r in res]

    update_parts("w_in", [p_wi0, p_wi1], w_in.shape)
    update_parts("w_out", [p_wo0, p_wo1], w_out.shape)
    update_parts("pool_w", [p_pw], (1,) + p_pw.shape[1:])
    for n in ("w_mod", "lru_wa", "lru_wx"):
        shape = weights[n].shape
        view = (int(np.prod(shape[:-1])), shape[-1])
        res = _adamw(weights[n].reshape(view), grads[n].reshape(view), mom_m[n].reshape(view),
                     mom_v[n].reshape(view), "adamw_" + n)
        delta[n], new_m[n], new_v[n] = [r.reshape(shape) for r in res]

    small = [n for n in names if n not in delta]
    shapes = [weights[n].shape for n in small]
    flat = lambda d: _to_rows([d[n] for n in small], FLAT_ROWS)
    res = _adamw(flat(weights), flat(grads), flat(mom_m), flat(mom_v), "adamw_small")
    for d, r in zip((delta, new_m, new_v), res):
        d.update(zip(small, _split_rows(r, shapes)))

    return (loss, gx[None], *[grads[n] for n in names], *[delta[n] for n in names],
            *[new_m[n] for n in names], *[new_v[n] for n in names])
```

```python
import functools

import numpy as np
import jax
import jax.numpy as jnp
from jax import lax
from jax.experimental import pallas as pl
from jax.experimental.pallas import tpu as pltpu

F32 = jnp.float32
BF16 = jnp.bfloat16
MXU_DTYPE = BF16

D_MODEL = 1024
D_INNER = 2048
LRU_BLOCK = 128
GRID_W = 64
POOL_WINDOWS = (2, 4, 8, 16)
POOL_GROUP = 512
ALPHA = float(4 ** 0.25)
LN_EPS = 1e-5
LRU_C = 8.0
N_DEV = 8
N_WBLK = 8
WBLK = 512

ADAM_LR = 0.001
ADAM_B1 = 0.9
ADAM_B2 = 0.999
ADAM_EPS = 1e-08
ADAM_WD = 0.01
ADAM_STEP = 10

LANES = 128
SUBLANES = 8
V7X_VMEM_BYTES = 64 * 1024 * 1024
VMEM_COMPILER_RESERVE = 8 * 1024 * 1024
VMEM_LIMIT = V7X_VMEM_BYTES - VMEM_COMPILER_RESERVE
MESH = pl.DeviceIdType.MESH
ANY = pl.BlockSpec(memory_space=pl.ANY)

TM_MM = 512
TM_LRU = 1024
CB_LRU = 512
CB_LRU_FWD = 1024
N_SEG = 8
SCAN_UNROLL = 4
SCAN_ROW_T = 17
SCAN_ROW_J = 2
SQRT_FLOOR = 1e-30
FLAT_ROWS = 16
ELEMENTWISE_TILE_BYTES = 1 << 20
POOL_TOK = 256
WIRE_DTYPE = BF16
ACT_DTYPE = BF16
H_HALO = 16


def _cparams(**kw):
    return pltpu.CompilerParams(vmem_limit_bytes=VMEM_LIMIT, **kw)


def _my_pos():
    return lax.axis_index("x"), lax.axis_index("y"), lax.axis_index("c")


def _dot(a, b):
    return jnp.dot(a.astype(MXU_DTYPE), b.astype(MXU_DTYPE), preferred_element_type=F32)


def _dot_tn(a, b):
    return lax.dot_general(a.astype(MXU_DTYPE), b.astype(MXU_DTYPE), (((0,), (0,)), ((), ())),
                           preferred_element_type=F32)


def _dot_nt(a, b):
    return lax.dot_general(a.astype(MXU_DTYPE), b.astype(MXU_DTYPE), (((1,), (1,)), ((), ())),
                           preferred_element_type=F32)


def _sigmoid(z):
    return 0.5 * jnp.tanh(0.5 * z) + 0.5


def _log_sigmoid(x):
    y = jnp.exp(-jnp.abs(x))
    u = 1.0 + y
    l1p = jnp.where(u == 1.0, y, jnp.log(u) * (y / jnp.where(u == 1.0, 1.0, u - 1.0)))
    return jnp.minimum(x, 0.0) - l1p


def _rowsum(v):
    return jnp.sum(v, axis=0, keepdims=True)


def _layer_norm_stats(z):
    mu = jnp.mean(z, axis=-1, keepdims=True)
    zc = z - mu
    var = jnp.mean(zc * zc, axis=-1, keepdims=True)
    rstd = lax.rsqrt(var + LN_EPS)
    return zc * rstd, rstd


def _layer_norm_bwd(dy, xhat, rstd, g):
    dxh = dy * g
    m1 = jnp.mean(dxh, axis=-1, keepdims=True)
    m2 = jnp.mean(dxh * xhat, axis=-1, keepdims=True)
    return rstd * (dxh - m1 - xhat * m2)


def _shifted(v, before8, after8, offsets):
    n = v.shape[0]
    ext = jnp.concatenate([before8, v, after8], axis=0)
    total = n + 2 * SUBLANES
    return [pltpu.roll(ext, (-k) % total, 0)[SUBLANES:SUBLANES + n] for k in offsets]


def _rows8(row):
    return jnp.broadcast_to(row, (SUBLANES, row.shape[1]))


def _shift_down(v, first_row):
    return _shifted(v, _rows8(first_row), _rows8(first_row), [-1])[0]


def _shift_up(v, last_row):
    return _shifted(v, _rows8(last_row), _rows8(last_row), [1])[0]


def _all_gather(blocks, name):
    n = len(blocks)

    def body(*refs):
        x_refs, out_refs = refs[:n], refs[n:2 * n]
        send_sems, recv_sems, local_sems = refs[2 * n:]
        x, y, c = _my_pos()
        me, sibling = (x, y, c), (x, y, 1 - c)
        chips = [(1 - x, y), (x, 1 - y), (1 - x, 1 - y)]

        def slot(a, px, py, pc):
            return out_refs[a].at[4 * px + 2 * py + pc]

        def copy(a, k, block, to, src=None):
            return pltpu.make_async_remote_copy(
                src_ref=slot(a, *block) if src is None else src, dst_ref=slot(a, *block),
                send_sem=send_sems.at[a, k], recv_sem=recv_sems.at[a, k], device_id=to, device_id_type=MESH)

        mine = [pltpu.make_async_copy(x_refs[a], slot(a, *me), local_sems.at[a]) for a in range(n)]
        for cp in mine:
            cp.start()
        first = []
        for a in range(n):
            first.append(copy(a, 0, me, sibling, src=x_refs[a]))
            first += [copy(a, 1 + j, me, (*chip, c), src=x_refs[a]) for j, chip in enumerate(chips)]
        for cp in first:
            cp.start()
        passed = []
        for j, chip in enumerate(chips):
            for a in range(n):
                copy(a, 1 + j, (*chip, c), me).wait_recv()
                fwd = copy(a, 4 + j, (*chip, c), sibling)
                fwd.start()
                passed.append(fwd)
        for a in range(n):
            copy(a, 0, sibling, me).wait_recv()
            for j, chip in enumerate(chips):
                copy(a, 4 + j, (*chip, 1 - c), me).wait_recv()
        for cp in first + passed:
            cp.wait_send()
        for cp in mine:
            cp.wait()

    outs = pl.pallas_call(
        body, name=name,
        out_shape=[jax.ShapeDtypeStruct((N_DEV,) + b.shape, b.dtype) for b in blocks],
        in_specs=[ANY] * n, out_specs=[ANY] * n,
        scratch_shapes=[pltpu.SemaphoreType.DMA((n, 7)), pltpu.SemaphoreType.DMA((n, 7)),
                        pltpu.SemaphoreType.DMA((n,))],
    )(*blocks)
    return list(outs)


def _sibling_exchange(bufs, name):
    n = len(bufs)

    def body(*refs):
        srcs, outs = refs[:n], refs[n:2 * n]
        send_sems, recv_sems = refs[2 * n:]
        x, y, c = _my_pos()
        copies = [pltpu.make_async_remote_copy(
            src_ref=srcs[a].at[2 * j + (1 - c)], dst_ref=outs[a].at[j], send_sem=send_sems.at[a, j],
            recv_sem=recv_sems.at[a, j], device_id=(x, y, 1 - c), device_id_type=MESH)
            for a in range(n) for j in range(4)]
        for cp in copies:
            cp.start()
        for cp in copies:
            cp.wait()

    outs = pl.pallas_call(
        body, name=name, out_shape=[jax.ShapeDtypeStruct((4,) + b.shape[1:], b.dtype) for b in bufs],
        in_specs=[ANY] * n, out_specs=[ANY] * n,
        scratch_shapes=[pltpu.SemaphoreType.DMA((n, 4)), pltpu.SemaphoreType.DMA((n, 4))],
    )(*bufs)
    return list(outs)


def _chip_exchange(parts, name):
    n = len(parts)

    def body(*refs):
        srcs, outs = refs[:n], refs[n:2 * n]
        send_sems, recv_sems, local_sems = refs[2 * n:]
        x, y, c = _my_pos()
        jme = 2 * x + y
        peers = [(1 - x, y), (x, 1 - y), (1 - x, 1 - y)]
        local = [pltpu.make_async_copy(srcs[a].at[jme], outs[a].at[jme], local_sems.at[a]) for a in range(n)]
        for cp in local:
            cp.start()

        def copy(a, k, px, py, dst_slot):
            return pltpu.make_async_remote_copy(
                src_ref=srcs[a].at[2 * px + py], dst_ref=outs[a].at[dst_slot], send_sem=send_sems.at[a, k],
                recv_sem=recv_sems.at[a, k], device_id=(px, py, c), device_id_type=MESH)

        sends = [copy(a, k, px, py, jme) for a in range(n) for k, (px, py) in enumerate(peers)]
        for cp in sends:
            cp.start()
        for a in range(n):
            for k, (px, py) in enumerate(peers):
                copy(a, k, px, py, 2 * px + py).wait_recv()
        for cp in sends:
            cp.wait_send()
        for cp in local:
            cp.wait()

    outs = pl.pallas_call(
        body, name=name, out_shape=[jax.ShapeDtypeStruct(p.shape, p.dtype) for p in parts],
        in_specs=[ANY] * n, out_specs=[ANY] * n,
        scratch_shapes=[pltpu.SemaphoreType.DMA((n, 3)), pltpu.SemaphoreType.DMA((n, 3)),
                        pltpu.SemaphoreType.DMA((n,))],
    )(*parts)
    return list(outs)


_SIDE_REMOTE = {"gather": 7, "sibling": 4, "chips": 3}
_FLIPS = [(0, 0, 1), (1, 0, 0), (0, 1, 0), (1, 1, 0), (1, 0, 1), (0, 1, 1), (1, 1, 1)]


def _side_plan(sides):
    inputs, out_shapes, scratch = [], [], []
    for kind, arrays in sides:
        n = len(arrays)
        for a in arrays:
            inputs.append(a)
            shape = {"gather": (N_DEV,) + a.shape, "sibling": (4,) + a.shape[1:], "chips": a.shape}[kind]
            out_shapes.append(jax.ShapeDtypeStruct(shape, a.dtype))
        scratch += [pltpu.SemaphoreType.DMA((n, _SIDE_REMOTE[kind])), pltpu.SemaphoreType.DMA((n, _SIDE_REMOTE[kind])),
                    pltpu.SemaphoreType.DMA((n,))]
    return inputs, out_shapes, scratch


def _side_copies(sides, in_refs, out_refs, sem_refs):
    x, y, c = _my_pos()
    starts, waits = [], []
    pos = 0
    for s, (kind, arrays) in enumerate(sides):
        send_sems, recv_sems, local_sems = sem_refs[3 * s:3 * s + 3]
        for a in range(len(arrays)):
            src, out = in_refs[pos], out_refs[pos]
            pos += 1

            def remote(k, src_ref, dst_ref, to):
                return pltpu.make_async_remote_copy(src_ref=src_ref, dst_ref=dst_ref, send_sem=send_sems.at[a, k],
                                                    recv_sem=recv_sems.at[a, k], device_id=to, device_id_type=MESH)

            def local(src_ref, dst_ref):
                cp = pltpu.make_async_copy(src_ref, dst_ref, local_sems.at[a])
                starts.append(cp.start)
                waits.append(cp.wait)

            if kind == "gather":
                me = 4 * x + 2 * y + c
                local(src, out.at[me])
                for k, (fx, fy, fc) in enumerate(_FLIPS):
                    px, py, pc = (1 - x if fx else x), (1 - y if fy else y), (1 - c if fc else c)
                    send = remote(k, src, out.at[me], (px, py, pc))
                    starts.append(send.start)
                    waits += [remote(k, src, out.at[4 * px + 2 * py + pc], (px, py, pc)).wait_recv, send.wait_send]
            elif kind == "sibling":
                for j in range(4):
                    cp = remote(j, src.at[2 * j + (1 - c)], out.at[j], (x, y, 1 - c))
                    starts.append(cp.start)
                    waits.append(cp.wait)
            else:
                jme = 2 * x + y
                local(src.at[jme], out.at[jme])
                for k, (px, py) in enumerate([(1 - x, y), (x, 1 - y), (1 - x, 1 - y)]):
                    send = remote(k, src.at[2 * px + py], out.at[jme], (px, py, c))
                    starts.append(send.start)
                    waits += [remote(k, src.at[2 * px + py], out.at[2 * px + py], (px, py, c)).wait_recv,
                              send.wait_send]
    return starts, waits


def _call_with_sides(body, sides, *, name, grid, in_specs, out_specs, out_shape, scratch_shapes, compiler_params, args):
    if not sides:
        res = pl.pallas_call(body, name=name, grid=grid, in_specs=in_specs, out_specs=out_specs, out_shape=out_shape,
                             scratch_shapes=scratch_shapes, compiler_params=compiler_params)(*args)
        return list(res), []
    s_in, s_out, s_scr = _side_plan(sides)
    n_in, n_out, n_scr, n_side = len(in_specs), len(out_specs), len(scratch_shapes), len(s_in)

    def wrapped(*refs):
        refs = list(refs)
        ins, side_in = refs[:n_in], refs[n_in:n_in + n_side]
        outs = refs[n_in + n_side:n_in + n_side + n_out]
        side_out = refs[n_in + n_side + n_out:n_in + 2 * n_side + n_out]
        rest = refs[n_in + 2 * n_side + n_out:]
        starts, waits = _side_copies(sides, side_in, side_out, rest[n_scr:])
        first = functools.reduce(jnp.logical_and, [pl.program_id(d) == 0 for d in range(len(grid))])
        last = functools.reduce(jnp.logical_and, [pl.program_id(d) == grid[d] - 1 for d in range(len(grid))])

        @pl.when(first)
        def _():
            for start in starts:
                start()

        body(*ins, *outs, *rest[:n_scr])

        @pl.when(last)
        def _():
            for wait in waits:
                wait()

    res = pl.pallas_call(
        wrapped, name=name, grid=grid, in_specs=list(in_specs) + [ANY] * n_side,
        out_specs=list(out_specs) + [ANY] * n_side, out_shape=list(out_shape) + s_out,
        scratch_shapes=list(scratch_shapes) + s_scr, compiler_params=compiler_params,
    )(*args, *s_in)
    return list(res[:n_out]), list(res[n_out:])


def _row_tile(r, l):
    t = min(r, max(16, ELEMENTWISE_TILE_BYTES // (4 * l) // 16 * 16))
    while r % t:
        t -= 16
    return t


def _pair_sum(buf, recv, core, name):
    _, r, l = buf.shape
    tr = _row_tile(r, l)

    def body(core_ref, a_ref, b_ref, o_ref):
        o_ref[...] = (a_ref[...] + b_ref[...]).astype(WIRE_DTYPE)

    return pl.pallas_call(
        body, name=name, out_shape=jax.ShapeDtypeStruct((4, r, l), WIRE_DTYPE),
        grid_spec=pltpu.PrefetchScalarGridSpec(
            num_scalar_prefetch=1, grid=(4, r // tr),
            in_specs=[pl.BlockSpec((None, tr, l), lambda j, i, cr: (2 * j + cr[0], i, 0)),
                      pl.BlockSpec((None, tr, l), lambda j, i, cr: (j, i, 0))],
            out_specs=pl.BlockSpec((None, tr, l), lambda j, i, cr: (j, i, 0))),
        compiler_params=_cparams(dimension_semantics=("arbitrary", "arbitrary")),
    )(core, buf, recv)


def _sum_parts(p_ref):
    return ((p_ref[0].astype(F32) + p_ref[1].astype(F32)) + (p_ref[2].astype(F32) + p_ref[3].astype(F32)))


def _sum4(parts, name):
    _, r, l = parts.shape
    tr = _row_tile(r, l)

    def body(p_ref, o_ref):
        o_ref[...] = _sum_parts(p_ref)

    return pl.pallas_call(
        body, name=name, out_shape=jax.ShapeDtypeStruct((r, l), F32), grid=(r // tr,),
        in_specs=[pl.BlockSpec((4, tr, l), lambda i: (0, i, 0))],
        out_specs=pl.BlockSpec((tr, l), lambda i: (i, 0)),
        compiler_params=_cparams(dimension_semantics=("arbitrary",)),
    )(parts)


def _adamw_update(w, gg, m, v):
    nm = ADAM_B1 * m + (1.0 - ADAM_B1) * gg
    nv = ADAM_B2 * v + (1.0 - ADAM_B2) * (gg * gg)
    m_hat = nm / (1.0 - ADAM_B1 ** ADAM_STEP)
    v_hat = nv / (1.0 - ADAM_B2 ** ADAM_STEP)
    return -ADAM_LR * (m_hat / (jnp.sqrt(v_hat) + ADAM_EPS) + ADAM_WD * w), nm, nv


def _adamw(w, g, m, v, name):
    r, l = w.shape
    tr = _row_tile(r, l)

    def body(w_ref, g_ref, m_ref, v_ref, d_ref, nm_ref, nv_ref):
        d_ref[...], nm_ref[...], nv_ref[...] = _adamw_update(w_ref[...], g_ref[...], m_ref[...], v_ref[...])

    spec = pl.BlockSpec((tr, l), lambda i: (i, 0))
    return pl.pallas_call(
        body, name=name, out_shape=[jax.ShapeDtypeStruct((r, l), F32)] * 3, grid=(r // tr,),
        in_specs=[spec] * 4, out_specs=[spec] * 3,
        compiler_params=_cparams(dimension_semantics=("arbitrary",)),
    )(w, g, m, v)


def _adamw_parts(w, parts, m, v, name):
    nl, r, l = w.shape
    tr = _row_tile(r, l)

    def body(*refs):
        w_ref, p_refs, (m_ref, v_ref, g_ref, d_ref, nm_ref, nv_ref) = refs[0], refs[1:1 + nl], refs[1 + nl:]
        layer = pl.program_id(0)
        gg = _sum_parts(p_refs[0])
        for q in range(1, nl):
            gg = jnp.where(layer == q, _sum_parts(p_refs[q]), gg)
        g_ref[...] = gg
        d_ref[...], nm_ref[...], nv_ref[...] = _adamw_update(w_ref[...], gg, m_ref[...], v_ref[...])

    spec = pl.BlockSpec((None, tr, l), lambda q, i: (q, i, 0))
    pspecs = [pl.BlockSpec((4, tr, l), lambda q, i, k=k: (0, jnp.where(q == k, i, 0), 0)) for k in range(nl)]
    return pl.pallas_call(
        body, name=name, out_shape=[jax.ShapeDtypeStruct((nl, r, l), F32)] * 4, grid=(nl, r // tr),
        in_specs=[spec] + pspecs + [spec, spec], out_specs=[spec] * 4,
        compiler_params=_cparams(dimension_semantics=("arbitrary", "arbitrary")),
    )(w, *parts, m, v)


def _to_rows(pieces, row_multiple):
    flat = jnp.concatenate([p.reshape(-1) for p in pieces])
    rows = -(-flat.shape[0] // LANES)
    rows = -(-rows // row_multiple) * row_multiple
    flat = jnp.pad(flat, (0, rows * LANES - flat.shape[0]))
    return flat.reshape(rows, LANES)


def _split_rows(rows, shapes):
    flat = rows.reshape(-1)
    out, off = [], 0
    for s in shapes:
        n = int(np.prod(s))
        out.append(flat[off:off + n].reshape(s))
        off += n
    return out


def _mod_fwd(cond, w_mod, b_my, name):
    nl, _, ncol = w_mod.shape

    def body(a_ref, w_ref, b_ref, o_ref):
        a = a_ref[...]
        s = a * _sigmoid(a)
        for i in range(nl):
            o_ref[i] = _dot(s, w_ref[i]) + b_ref[i]

    return pl.pallas_call(
        body, name=name, out_shape=jax.ShapeDtypeStruct((nl, 16, ncol), F32),
        compiler_params=_cparams(),
    )(cond, w_mod, b_my)


def _mod_bwd(cond, dm_all, dm_my, w_mod, name):
    nl, _, ncol = w_mod.shape

    def body(a_ref, dma_ref, dmm_ref, w_ref, gw_ref, gb_ref, gc_ref):
        a = a_ref[...]
        sg = _sigmoid(a)
        s = a * sg
        for i in range(nl):
            gw_ref[i] = _dot_tn(s, dmm_ref[i])
            gb_ref[i] = jnp.sum(dma_ref[i], axis=0, keepdims=True)
        back = _dot_nt(dmm_ref[0], w_ref[0])
        dsilu = sg * (1.0 + a * (1.0 - sg))
        gc_ref[...] = jnp.sum(back[8:16] * dsilu[8:16], axis=0, keepdims=True)

    return pl.pallas_call(
        body, name=name,
        out_shape=[jax.ShapeDtypeStruct((nl, D_MODEL, ncol), F32), jax.ShapeDtypeStruct((nl, 1, 3 * D_MODEL), F32),
                   jax.ShapeDtypeStruct((1, D_MODEL), F32)],
        compiler_params=_cparams(),
    )(cond, dm_all, dm_my, w_mod)


def _in_proj(xt, sc, sh, wg, name, sides=()):
    t = xt.shape[0]
    tm = min(TM_MM, t)

    def body(x_ref, sc_ref, sh_ref, w_ref, u_ref, g_ref):
        h = (x_ref[...] * (1.0 + sc_ref[...]) + sh_ref[...]).astype(MXU_DTYPE)
        for k in range(N_WBLK):
            o = jnp.dot(h, w_ref[k], preferred_element_type=F32)
            if k < N_WBLK // 2:
                u_ref[:, k * WBLK:(k + 1) * WBLK] = o
            else:
                kk = k - N_WBLK // 2
                g_ref[:, kk * WBLK:(kk + 1) * WBLK] = o.astype(ACT_DTYPE)

    row = pl.BlockSpec((1, D_MODEL), lambda i: (0, 0))
    return _call_with_sides(
        body, sides, name=name,
        out_shape=[jax.ShapeDtypeStruct((t, D_INNER), F32), jax.ShapeDtypeStruct((t, D_INNER), ACT_DTYPE)],
        grid=(t // tm,),
        in_specs=[pl.BlockSpec((tm, D_MODEL), lambda i: (i, 0)), row, row,
                  pl.BlockSpec((N_WBLK, D_MODEL, WBLK), lambda i: (0, 0, 0), pipeline_mode=pl.Buffered(1))],
        out_specs=[pl.BlockSpec((tm, D_INNER), lambda i: (i, 0))] * 2, scratch_shapes=[],
        compiler_params=_cparams(dimension_semantics=("arbitrary",)), args=[xt, sc, sh, wg])


def _halo_maps(nt, tm, n_blocks, pos, rows=SUBLANES):
    per = tm // rows
    prev = lambda cb, i: (jnp.maximum(pos(i) * per - 1, 0), cb)
    nxt = lambda cb, i: (jnp.minimum((pos(i) + 1) * per, n_blocks - 1), cb)
    return prev, nxt


def _conv_taps(u, prev8, next8, is_first, is_last):
    pz = jnp.where(is_first, 0.0, 1.0)
    nz = jnp.where(is_last, 0.0, 1.0)
    return _shifted(u, prev8 * pz, next8 * nz, [-2, -1, 1])


def _lru_gates(uv, wa_ref, wx_ref, ba, bx, cl, g):
    sl = slice(g * LANES, (g + 1) * LANES)
    uvg = uv[:, sl]
    r = 0.5 * jnp.tanh(_dot(uvg, wa_ref[g]) + ba[:, sl]) + 0.5
    ii = 0.5 * jnp.tanh(_dot(uvg, wx_ref[g]) + bx[:, sl]) + 0.5
    la = cl[:, sl] * r
    a = jnp.exp(la)
    q = jnp.tanh(-la) * (1.0 + a * a)
    rs = lax.rsqrt(jnp.maximum(q, SQRT_FLOOR))
    return uvg, r, ii, a, q * rs, rs


def _scan_rows(seg):
    return -(-(SCAN_ROW_T * (seg - 1) + SCAN_ROW_J * (N_SEG - 1) + 1) // SUBLANES) * SUBLANES


def _seg_chunk(j, c):
    return pl.ds(SCAN_ROW_T * SUBLANES * c + SCAN_ROW_J * j, SUBLANES, stride=SCAN_ROW_T)


def _seg_scatter(ref, g, seg, value):
    for j in range(N_SEG):
        for c in range(seg // SUBLANES):
            r0 = j * seg + SUBLANES * c
            ref[g, _seg_chunk(j, c), :] = value[r0:r0 + SUBLANES]


def _scan_tile(a_s, b_s, carry_ref, write_out, seg, reverse, chunks_per_write=1):
    n_g = a_s.shape[0]
    unroll = SCAN_UNROLL if seg % SCAN_UNROLL == 0 else 1

    n_trips = seg // unroll

    def steps(k, state):
        hs, cs = list(state[0]), list(state[1])
        base = ((n_trips - 1 - k) if reverse else k) * unroll
        for q in (range(unroll - 1, -1, -1) if reverse else range(unroll)):
            t = base + q
            rows = pl.ds(t * SCAN_ROW_T, N_SEG, stride=SCAN_ROW_J)
            for g in range(n_g):
                a = a_s[g, rows, :]
                b = b_s[g, rows, :]
                hs[g] = a * hs[g] + b
                cs[g] = a * cs[g]
                b_s[g, rows, :] = hs[g]
                a_s[g, rows, :] = cs[g]
        return tuple(hs), tuple(cs)

    zeros = tuple(jnp.zeros((N_SEG, LANES), F32) for _ in range(n_g))
    ones = tuple(jnp.ones((N_SEG, LANES), F32) for _ in range(n_g))
    h_fin, a_fin = lax.fori_loop(0, seg // unroll, steps, (zeros, ones))

    order = list(range(N_SEG - 1, -1, -1)) if reverse else list(range(N_SEG))
    for g in range(n_g):
        carry = carry_ref[:, g * LANES:(g + 1) * LANES]
        for j in order:
            for c0 in range(0, seg // SUBLANES, chunks_per_write):
                parts = [b_s[g, _seg_chunk(j, c), :] + a_s[g, _seg_chunk(j, c), :] * carry
                         for c in range(c0, c0 + chunks_per_write)]
                write_out(j, c0, g, parts[0] if chunks_per_write == 1 else jnp.concatenate(parts, axis=0))
            carry = a_fin[g][j:j + 1] * carry + h_fin[g][j:j + 1]
        carry_ref[:, g * LANES:(g + 1) * LANES] = carry


def _lru_specs(s, tm, cb, direction_pos, nt):
    n_rows8 = s // SUBLANES
    prev, nxt = _halo_maps(nt, tm, n_rows8, direction_pos)
    tile = pl.BlockSpec((tm, cb), lambda c, i: (direction_pos(i), c))
    return tile, pl.BlockSpec((SUBLANES, cb), prev), pl.BlockSpec((SUBLANES, cb), nxt)


def _lru_param_specs(cb, d):
    n_g = cb // LANES
    vec = pl.BlockSpec((1, cb), lambda c, i: (0, c))
    dvec = pl.BlockSpec((None, 1, cb), lambda c, i: (d, 0, c))
    wmat = pl.BlockSpec((None, n_g, LRU_BLOCK, LRU_BLOCK), lambda c, i: (d, c, 0, 0))
    return vec, dvec, wmat


def _lru_fwd(src, h0, p, d, name, conv, sides=()):
    s = src.shape[0]
    tm = min(TM_LRU, s)
    cb = CB_LRU_FWD
    n_g = cb // LANES
    nt = s // tm
    seg = tm // N_SEG
    pos = (lambda i: i) if d == 0 else (lambda i: nt - 1 - i)

    def body(*refs):
        refs = list(refs)
        u_ref = refs.pop(0)
        if conv:
            up_ref, un_ref, cw_ref, cbias_ref = [refs.pop(0) for _ in range(4)]
        wa_ref, wx_ref, ba_ref, bx_ref, lam_ref, h0_ref, h_ref, hc_ref = [refs.pop(0) for _ in range(8)]
        uv_ref = refs.pop(0) if conv else None
        a_s, b_s = refs
        i = pl.program_id(1)
        tp = pos(i)

        @pl.when(i == 0)
        def _():
            hc_ref[...] = h0_ref[...]

        if conv:
            u_t = u_ref[...]
            um2, um1, up1 = _conv_taps(u_t, up_ref[...], un_ref[...], tp == 0, tp == nt - 1)
            cw = cw_ref[...]
            uv_ref[...] = um2 * cw[0:1] + um1 * cw[1:2] + u_t * cw[2:3] + up1 * cw[3:4] + cbias_ref[...]
        src_ref = uv_ref if conv else u_ref
        cl = LRU_C * _log_sigmoid(lam_ref[...])
        ba, bx = ba_ref[...], bx_ref[...]
        for g in range(n_g):
            uvg, r, ii, a, sq, _ = _lru_gates(src_ref, wa_ref, wx_ref, ba, bx, cl, g)
            b = sq * (ii * uvg)
            _seg_scatter(a_s, g, seg, a)
            _seg_scatter(b_s, g, seg, b)

        per_write = 2 if (seg // SUBLANES) % 2 == 0 else 1

        def write_out(j, c, g, h):
            h_ref[pl.ds(j * seg + SUBLANES * c, SUBLANES * per_write), pl.ds(g * LANES, LANES)] = h.astype(ACT_DTYPE)

        _scan_tile(a_s, b_s, hc_ref, write_out, seg, reverse=(d == 1), chunks_per_write=per_write)

    tile, prev, nxt = _lru_specs(s, tm, cb, pos, nt)
    vec, dvec, wmat = _lru_param_specs(cb, d)
    wide = jax.ShapeDtypeStruct((s, D_INNER), F32)
    conv_specs = [prev, nxt, pl.BlockSpec((4, cb), lambda c, i: (0, c)), vec] if conv else []
    conv_args = [src, src, p["conv_w"], p["conv_b"]] if conv else []
    return _call_with_sides(
        body, sides, name=name,
        out_shape=[jax.ShapeDtypeStruct((s, D_INNER), ACT_DTYPE), jax.ShapeDtypeStruct((1, D_INNER), F32)]
        + ([wide] if conv else []),
        grid=(D_INNER // cb, nt),
        in_specs=[tile] + conv_specs + [wmat, wmat, dvec, dvec, dvec, vec],
        out_specs=[tile, vec] + ([tile] if conv else []),
        scratch_shapes=[pltpu.VMEM((n_g, _scan_rows(seg), LANES), F32)] * 2,
        compiler_params=_cparams(dimension_semantics=("arbitrary", "arbitrary")),
        args=[src, *conv_args, p["wa"], p["wx"], p["ba"], p["bx"], p["lam"], h0])


def _lru_bwd(uv, dh, h, h0, lam_in, p, d, name, sides=()):
    s = uv.shape[0]
    tm = min(TM_LRU, s)
    cb = CB_LRU
    n_g = cb // LANES
    nt = s // tm
    seg = tm // N_SEG
    pos = (lambda i: nt - 1 - i) if d == 0 else (lambda i: i)

    def body(uv_ref, dh_ref, h_ref, hh_ref, wa_ref, wx_ref, ba_ref, bx_ref,
             lam_ref, h0_ref, lin_ref, duv_ref, gwa_ref, gwx_ref, gv_ref, lc_ref, a_s, b_s, lp_s,
             r_s, i_s, q_s, rq_s, a_keep):
        i = pl.program_id(1)
        tp = pos(i)

        @pl.when(i == 0)
        def _():
            lc_ref[...] = lin_ref[...]
            gwa_ref[...] = jnp.zeros_like(gwa_ref)
            gwx_ref[...] = jnp.zeros_like(gwx_ref)
            gv_ref[...] = jnp.zeros_like(gv_ref)

        uv = uv_ref[...]
        lam = lam_ref[...]
        cl = LRU_C * _log_sigmoid(lam)
        ba, bx = ba_ref[...], bx_ref[...]
        dh_t = dh_ref[...].astype(F32)
        carry_in = lc_ref[...]
        for g in range(n_g):
            sl = slice(g * LANES, (g + 1) * LANES)
            _, r, ii, a, sq, rs = _lru_gates(uv, wa_ref, wx_ref, ba, bx, cl, g)
            for ref, val in ((r_s, r), (i_s, ii), (q_s, sq), (rq_s, rs), (a_keep, a)):
                ref[:, sl] = val.astype(ACT_DTYPE)
            b = a * dh_t[:, sl]
            _seg_scatter(a_s, g, seg, a)
            _seg_scatter(b_s, g, seg, b)

        def write_out(j, c, g, v):
            lp_s[pl.ds(j * seg + SUBLANES * c, SUBLANES), pl.ds(g * LANES, LANES)] = v

        _scan_tile(a_s, b_s, lc_ref, write_out, seg, reverse=(d == 0))

        h_t = h_ref[...].astype(F32)
        hh = hh_ref[...].astype(F32)
        if d == 0:
            edge = jnp.where(tp == 0, h0_ref[...], hh[H_HALO - 1:H_HALO])
            h_prev = _shift_down(h_t, edge)
            lam_t = dh_t + _shift_up(lp_s[...], carry_in)
        else:
            edge = jnp.where(tp == nt - 1, h0_ref[...], hh[0:1])
            h_prev = _shift_up(h_t, edge)
            lam_t = dh_t + _shift_down(lp_s[...], carry_in)

        dsig = LRU_C * _sigmoid(-lam)
        cl2 = cl + cl
        for g in range(n_g):
            sl = slice(g * LANES, (g + 1) * LANES)
            uvg = uv[:, sl]
            r, ii, a, sq, rs =[ref[:, sl].astype(F32) for ref in (r_s, i_s, a_keep, q_s, rq_s)]
            lt = lam_t[:, sl]
            ls = lt * sq
            dla = (lt * a) * (h_prev[:, sl] - (ii * uvg) * (a * rs))
            dzr = (dla * cl2[:, sl]) * r * (1.0 - r)
            dzi = ((ls + ls) * uvg) * ii * (1.0 - ii)
            duv_ref[:, sl] = (ls * ii + _dot_nt(dzr, wa_ref[g]) + _dot_nt(dzi, wx_ref[g])).astype(ACT_DTYPE)
            gwa_ref[g] += _dot_tn(uvg, dzr)
            gwx_ref[g] += _dot_tn(uvg, dzi)
            gv_ref[0:1, sl] += _rowsum(dzr)
            gv_ref[1:2, sl] += _rowsum(dzi)
            gv_ref[2:3, sl] += _rowsum(dla * r) * dsig[:, sl]

        @pl.when(i == nt - 1)
        def _():
            gwa_ref[...] = 0.5 * gwa_ref[...]
            gwx_ref[...] = 0.5 * gwx_ref[...]
            gv_ref[0:2, :] = 0.5 * gv_ref[0:2, :]

    tile, _, _ = _lru_specs(s, tm, cb, pos, nt)
    vec, dvec, wmat = _lru_param_specs(cb, d)
    h_prev_map, h_next_map = _halo_maps(nt, tm, s // H_HALO, pos, rows=H_HALO)
    hh_spec = pl.BlockSpec((H_HALO, cb), h_prev_map if d == 0 else h_next_map)
    gw_spec = pl.BlockSpec((n_g, LRU_BLOCK, LRU_BLOCK), lambda c, i: (c, 0, 0))
    n_blk = D_INNER // LRU_BLOCK
    return _call_with_sides(
        body, sides, name=name,
        out_shape=[jax.ShapeDtypeStruct((s, D_INNER), ACT_DTYPE),
                   jax.ShapeDtypeStruct((n_blk, LRU_BLOCK, LRU_BLOCK), F32),
                   jax.ShapeDtypeStruct((n_blk, LRU_BLOCK, LRU_BLOCK), F32),
                   jax.ShapeDtypeStruct((SUBLANES, D_INNER), F32),
                   jax.ShapeDtypeStruct((1, D_INNER), F32)],
        grid=(D_INNER // cb, nt),
        in_specs=[tile, tile, tile, hh_spec, wmat, wmat, dvec, dvec, dvec, vec, vec],
        out_specs=[tile, gw_spec, gw_spec, pl.BlockSpec((SUBLANES, cb), lambda c, i: (0, c)), vec],
        scratch_shapes=[pltpu.VMEM((n_g, _scan_rows(seg), LANES), F32)] * 2 + [pltpu.VMEM((tm, cb), F32)]
        + [pltpu.VMEM((tm, cb), ACT_DTYPE)] * 5,
        compiler_params=_cparams(dimension_semantics=("arbitrary", "arbitrary")),
        args=[uv, dh, h, h, p["wa"], p["wx"], p["ba"], p["bx"], p["lam"], h0, lam_in])


def _out0(hf, hb, g, xt, gt, wo, lg, lb, sc1, sh1, wg1, name):
    t = xt.shape[0]
    tm = min(TM_MM, t)

    def body(hf_ref, hb_ref, g_ref, x_ref, gt_ref, w_ref, lg_ref, lb_ref, sc1_ref, sh1_ref, w1_ref,
             x1_ref, br_ref, u1_ref, g1_ref):
        br = None
        for k in range(D_INNER // WBLK):
            sl = slice(k * WBLK, (k + 1) * WBLK)
            gg = g_ref[:, sl].astype(F32)
            p = (hf_ref[:, sl].astype(F32) + hb_ref[:, sl].astype(F32)) * (gg * _sigmoid(gg))
            part = _dot(p, w_ref[sl, :])
            br = part if br is None else br + part
        z = ALPHA * x_ref[...] + gt_ref[...] * br
        xhat, _ = _layer_norm_stats(z)
        x1 = xhat * lg_ref[...] + lb_ref[...]
        x1_ref[...] = x1
        br_ref[...] = br.astype(ACT_DTYPE)
        h1 = (x1 * (1.0 + sc1_ref[...]) + sh1_ref[...]).astype(MXU_DTYPE)
        for k in range(N_WBLK):
            o = jnp.dot(h1, w1_ref[k], preferred_element_type=F32).astype(ACT_DTYPE)
            if k < N_WBLK // 2:
                u1_ref[:, k * WBLK:(k + 1) * WBLK] = o
            else:
                kk = k - N_WBLK // 2
                g1_ref[:, kk * WBLK:(kk + 1) * WBLK] = o

    wide = pl.BlockSpec((tm, D_INNER), lambda i: (i, 0))
    nar = pl.BlockSpec((tm, D_MODEL), lambda i: (i, 0))
    row = pl.BlockSpec((1, D_MODEL), lambda i: (0, 0))
    return pl.pallas_call(
        body, name=name,
        out_shape=[jax.ShapeDtypeStruct((t, D_MODEL), F32), jax.ShapeDtypeStruct((t, D_MODEL), ACT_DTYPE),
                   jax.ShapeDtypeStruct((t, D_INNER), ACT_DTYPE), jax.ShapeDtypeStruct((t, D_INNER), ACT_DTYPE)],
        grid=(t // tm,),
        in_specs=[wide, wide, wide, nar, row,
                  pl.BlockSpec((D_INNER, D_MODEL), lambda i: (0, 0), pipeline_mode=pl.Buffered(1)), row, row, row, row,
                  pl.BlockSpec((N_WBLK, D_MODEL, WBLK), lambda i: (0, 0, 0), pipeline_mode=pl.Buffered(1))],
        out_specs=[nar, nar, wide, wide],
        compiler_params=_cparams(dimension_semantics=("arbitrary",)),
    )(hf, hb, g, xt, gt, wo, lg, lb, sc1, sh1, wg1)


def _unrolled_loop(n, fn, unroll=4):
    while n % unroll:
        unroll //= 2

    def trip(k, carry):
        for q in range(unroll):
            fn(k * unroll + q)
        return carry
    lax.fori_loop(0, n // unroll, trip, 0)


def _window(n, w):
    t = np.arange(n)
    return np.clip(t - w // 2, 0, n), np.clip(t + w // 2, 0, n)


def _pool_tables(n_rows, transpose):
    boxes, inv_c, inv_r = [], [], []
    for w in POOL_WINDOWS:
        lo, hi = _window(GRID_W, w)
        m = np.zeros((GRID_W, GRID_W), np.float32)
        for r in range(GRID_W):
            m[r, lo[r]:hi[r]] = 1.0
        m = np.kron(np.eye(POOL_TOK // GRID_W, dtype=np.float32), m)
        boxes.append(m.T if transpose else m)
        inv_c.append(np.broadcast_to((1.0 / (hi - lo).astype(np.float32))[:, None], (GRID_W, LANES)))
        lo_r, hi_r = _window(n_rows, w)
        inv_r.append(1.0 / (hi_r - lo_r).astype(np.float32))
    return (jnp.asarray(np.stack(boxes), MXU_DTYPE), jnp.asarray(np.stack(inv_c), F32),
            jnp.asarray(np.stack(inv_r), F32))


def _pool_mix(xin, transpose, out_dtype, name):
    s = xin.shape[0]
    n_rows = s // GRID_W
    pad_t = SUBLANES * GRID_W
    rows_per_blk = POOL_TOK // GRID_W
    n_slab = D_INNER // LANES
    slabs_per_group = POOL_GROUP // LANES
    n_win = len(POOL_WINDOWS)
    boxes, inv_c, inv_r = _pool_tables(n_rows, transpose)
    exact_operand = (not transpose) and xin.dtype == MXU_DTYPE and MXU_DTYPE != F32

    def body(invr_ref, box_ref, invc_ref, x_ref, o_ref, pad_s):
        k = pl.program_id(0) // slabs_per_group
        pad_s[pl.ds(0, pad_t), :] = jnp.zeros((pad_t, LANES), F32)
        pad_s[pl.ds(pad_t + s, pad_t), :] = jnp.zeros((pad_t, LANES), F32)

        for kk, w in enumerate(POOL_WINDOWS):
            half = w // 2
            offsets = list(range(-(half - 1), half + 1)) if transpose else list(range(-half, half))

            @pl.when(k == kk)
            def _():
                inv_col = invc_ref[kk]

                def col_box(b):
                    st = pl.multiple_of(b * POOL_TOK, POOL_TOK)
                    xb = x_ref[pl.ds(st, POOL_TOK), :]
                    if exact_operand:
                        pad_s[pl.ds(pad_t + st, POOL_TOK), :] = jnp.dot(box_ref[kk], xb, preferred_element_type=F32)
                        return
                    xb = xb.astype(F32)
                    if transpose:
                        xb = xb * jnp.concatenate(
                            [inv_col * invr_ref[kk, b * rows_per_blk + q] for q in range(rows_per_blk)], axis=0)
                    hi = xb.astype(MXU_DTYPE)
                    lo = (xb - hi.astype(F32)).astype(MXU_DTYPE)
                    both = jnp.dot(box_ref[kk], jnp.concatenate([hi, lo], axis=1), preferred_element_type=F32)
                    pad_s[pl.ds(pad_t + st, POOL_TOK), :] = both[:, :LANES] + both[:, LANES:]
                _unrolled_loop(s // POOL_TOK, col_box, unroll=16)

                def row_box(r):
                    st = pl.multiple_of(r * GRID_W, GRID_W)
                    acc = pad_s[pl.ds(pad_t + st + offsets[0] * GRID_W, GRID_W), :]
                    for o in offsets[1:]:
                        acc = acc + pad_s[pl.ds(pad_t + st + o * GRID_W, GRID_W), :]
                    if not transpose:
                        acc = acc * (inv_col * invr_ref[kk, r])
                    o_ref[pl.ds(st, GRID_W), :] = (acc - x_ref[pl.ds(st, GRID_W), :].astype(F32)).astype(out_dtype)
                _unrolled_loop(n_rows, row_box)

    slab = pl.BlockSpec((s, LANES), lambda i: (0, i))
    return pl.pallas_call(
        body, name=name, out_shape=jax.ShapeDtypeStruct((s, D_INNER), out_dtype), grid=(n_slab,),
        in_specs=[pl.BlockSpec(memory_space=pltpu.SMEM),
                  pl.BlockSpec((n_win, POOL_TOK, POOL_TOK), lambda i: (0, 0, 0)),
                  pl.BlockSpec((n_win, GRID_W, LANES), lambda i: (0, 0, 0)), slab],
        out_specs=slab,
        scratch_shapes=[pltpu.VMEM((s + 2 * pad_t, LANES), F32)],
        compiler_params=_cparams(dimension_semantics=("arbitrary",)),
    )(inv_r, boxes, inv_c, xin)


def _out1(dmix, pw, ps, g, x1, gt, wo, lg, lb, tgt, name):
    t = x1.shape[0]
    tm = min(TM_MM, t)
    n_grp = len(POOL_WINDOWS)

    def body(d_ref, pw_ref, ps_ref, g_ref, x1_ref, gt_ref, w_ref, lg_ref, lb_ref, tgt_ref, dz_ref, st_ref, po_ref):
        @pl.when(pl.program_id(0) == 0)
        def _():
            st_ref[...] = jnp.zeros_like(st_ref)

        br = jnp.zeros((tm, D_MODEL), F32)
        for k in range(n_grp):
            sl = slice(k * POOL_GROUP, (k + 1) * POOL_GROUP)
            po = jnp.dot(d_ref[:, sl], pw_ref[k], preferred_element_type=F32)
            po_ref[:, sl] = po.astype(ACT_DTYPE)
            y = po * ps_ref[:, sl]
            gg = g_ref[:, sl].astype(F32)
            br = br + _dot(y * (gg * _sigmoid(gg)), w_ref[sl, :])
        z = ALPHA * x1_ref[...] + gt_ref[...] * br
        xhat, rstd = _layer_norm_stats(z)
        lg_v = lg_ref[...]
        err = xhat * lg_v + lb_ref[...] - tgt_ref[...]
        dy = err * (1.0 / D_MODEL)
        dz = _layer_norm_bwd(dy, xhat, rstd, lg_v)
        dz_ref[...] = dz
        st_ref[0:1, :] += _rowsum(dy * xhat)
        st_ref[1:2, :] += _rowsum(dy)
        st_ref[2:3, :] += _rowsum(dz * br)
        st_ref[3:4, :] += _rowsum(err * err)

    wide = pl.BlockSpec((tm, D_INNER), lambda i: (i, 0))
    nar = pl.BlockSpec((tm, D_MODEL), lambda i: (i, 0))
    row = pl.BlockSpec((1, D_MODEL), lambda i: (0, 0))
    return pl.pallas_call(
        body, name=name,
        out_shape=[jax.ShapeDtypeStruct((t, D_MODEL), F32), jax.ShapeDtypeStruct((SUBLANES, D_MODEL), F32),
                   jax.ShapeDtypeStruct((t, D_INNER), ACT_DTYPE)],
        grid=(t // tm,),
        in_specs=[wide, pl.BlockSpec((n_grp, POOL_GROUP, POOL_GROUP), lambda i: (0, 0, 0)),
                  pl.BlockSpec((1, D_INNER), lambda i: (0, 0)), wide, nar, row,
                  pl.BlockSpec((D_INNER, D_MODEL), lambda i: (0, 0), pipeline_mode=pl.Buffered(1)), row, row, nar],
        out_specs=[nar, pl.BlockSpec((SUBLANES, D_MODEL), lambda i: (0, 0)), wide],
        compiler_params=_cparams(dimension_semantics=("arbitrary",)),
    )(dmix, pw, ps, g, x1, gt, wo, lg, lb, tgt)


def _flush(acc, out_hbm, sem):
    cp = pltpu.make_async_copy(acc, out_hbm, sem)
    cp.start()
    cp.wait()


def _bout1(dz, dmix, po, g, pw, ps, gt, wo, name):
    t = dz.shape[0]
    tm = min(TM_MM, t)
    nt = t // tm
    n_grp = len(POOL_WINDOWS)

    def body(dz_ref, d_ref, po_ref, g_ref, pw_ref, ps_ref, gt_ref, w_ref, dd_ref, dg_ref, gwo_hbm, gpw_hbm, gps_ref,
             gwo_acc, gpw_acc, sems):
        i = pl.program_id(0)

        @pl.when(i == 0)
        def _():
            gwo_acc[...] = jnp.zeros_like(gwo_acc)
            gpw_acc[...] = jnp.zeros_like(gpw_acc)
            gps_ref[...] = jnp.zeros_like(gps_ref)

        db = (gt_ref[...] * dz_ref[...]).astype(MXU_DTYPE)
        for k in range(n_grp):
            sl = slice(k * POOL_GROUP, (k + 1) * POOL_GROUP)
            dk = d_ref[:, sl]
            po = po_ref[:, sl].astype(F32)
            psk = ps_ref[:, sl]
            y = po * psk
            gg = g_ref[:, sl].astype(F32)
            sg = _sigmoid(gg)
            silu = gg * sg
            gwo_acc[sl, :] += _dot_tn(y * silu, db)
            dp = _dot_nt(db, w_ref[sl, :])
            dy = dp * silu
            dg_ref[:, sl] = (dp * y * (sg * (1.0 + gg * (1.0 - sg)))).astype(MXU_DTYPE)
            gps_ref[0:1, sl] += _rowsum(dy * po)
            dpo = (dy * psk).astype(MXU_DTYPE)
            gpw_acc[k] += _dot_tn(dk, dpo)
            dd_ref[:, sl] = _dot_nt(dpo, pw_ref[k])

        @pl.when(i == nt - 1)
        def _():
            _flush(gwo_acc, gwo_hbm, sems.at[0])
            _flush(gpw_acc, gpw_hbm, sems.at[1])

    wide = pl.BlockSpec((tm, D_INNER), lambda i: (i, 0))
    nar = pl.BlockSpec((tm, D_MODEL), lambda i: (i, 0))
    return pl.pallas_call(
        body, name=name,
        out_shape=[jax.ShapeDtypeStruct((t, D_INNER), F32), jax.ShapeDtypeStruct((t, D_INNER), MXU_DTYPE),
                   jax.ShapeDtypeStruct((D_INNER, D_MODEL), F32),
                   jax.ShapeDtypeStruct((n_grp, POOL_GROUP, POOL_GROUP), F32),
                   jax.ShapeDtypeStruct((SUBLANES, D_INNER), F32)],
        grid=(nt,),
        in_specs=[nar, wide, wide, wide,
                  pl.BlockSpec((n_grp, POOL_GROUP, POOL_GROUP), lambda i: (0, 0, 0), pipeline_mode=pl.Buffered(1)),
                  pl.BlockSpec((1, D_INNER), lambda i: (0, 0)), pl.BlockSpec((1, D_MODEL), lambda i: (0, 0)),
                  pl.BlockSpec((D_INNER, D_MODEL), lambda i: (0, 0), pipeline_mode=pl.Buffered(1))],
        out_specs=[wide, wide, ANY, ANY, pl.BlockSpec((SUBLANES, D_INNER), lambda i: (0, 0))],
        scratch_shapes=[pltpu.VMEM((D_INNER, D_MODEL), F32), pltpu.VMEM((n_grp, POOL_GROUP, POOL_GROUP), F32),
                        pltpu.SemaphoreType.DMA((2,))],
        compiler_params=_cparams(dimension_semantics=("arbitrary",)),
    )(dz, dmix, po, g, pw, ps, gt, wo)


def _bout0(dx1, xt, br0, lg, hf, hb, g, gt, wo, name, sides=()):
    t = dx1.shape[0]
    tm = min(TM_MM, t)
    nt = t // tm

    def body(dx_ref, x_ref, br_ref, lg_ref, hf_ref, hb_ref, g_ref, gt_ref, w_ref,
             dz_ref, dy_ref, dg_ref, gwo_hbm, st_ref, gwo_acc, sem):
        i = pl.program_id(0)

        @pl.when(i == 0)
        def _():
            gwo_acc[...] = jnp.zeros_like(gwo_acc)
            st_ref[...] = jnp.zeros_like(st_ref)

        dx = dx_ref[...]
        br = br_ref[...].astype(F32)
        gate = gt_ref[...]
        xhat, rstd = _layer_norm_stats(ALPHA * x_ref[...] + gate * br)
        dz = _layer_norm_bwd(dx, xhat, rstd, lg_ref[...])
        dz_ref[...] = dz
        st_ref[0:1, :] += _rowsum(dx * xhat)
        st_ref[1:2, :] += _rowsum(dx)
        st_ref[2:3, :] += _rowsum(dz * br)
        db = (gate * dz).astype(MXU_DTYPE)
        for k in range(D_INNER // WBLK):
            sl = slice(k * WBLK, (k + 1) * WBLK)
            y = hf_ref[:, sl].astype(F32) + hb_ref[:, sl].astype(F32)
            gg = g_ref[:, sl].astype(F32)
            sg = _sigmoid(gg)
            silu = gg * sg
            gwo_acc[sl, :] += _dot_tn(y * silu, db)
            dp = _dot_nt(db, w_ref[sl, :])
            dy_ref[:, sl] = (dp * silu).astype(ACT_DTYPE)
            dg_ref[:, sl] = (dp * y * (sg * (1.0 + gg * (1.0 - sg)))).astype(MXU_DTYPE)

        @pl.when(i == nt - 1)
        def _():
            _flush(gwo_acc, gwo_hbm, sem)

    wide = pl.BlockSpec((tm, D_INNER), lambda i: (i, 0))
    nar = pl.BlockSpec((tm, D_MODEL), lambda i: (i, 0))
    row = pl.BlockSpec((1, D_MODEL), lambda i: (0, 0))
    return _call_with_sides(
        body, sides, name=name,
        out_shape=[jax.ShapeDtypeStruct((t, D_MODEL), F32), jax.ShapeDtypeStruct((t, D_INNER), ACT_DTYPE),
                   jax.ShapeDtypeStruct((t, D_INNER), MXU_DTYPE), jax.ShapeDtypeStruct((D_INNER, D_MODEL), F32),
                   jax.ShapeDtypeStruct((SUBLANES, D_MODEL), F32)],
        grid=(nt,),
        in_specs=[nar, nar, nar, row, wide, wide, wide, row,
                  pl.BlockSpec((D_INNER, D_MODEL), lambda i: (0, 0), pipeline_mode=pl.Buffered(1))],
        out_specs=[nar, wide, wide, ANY, pl.BlockSpec((SUBLANES, D_MODEL), lambda i: (0, 0))],
        scratch_shapes=[pltpu.VMEM((D_INNER, D_MODEL), F32), pltpu.SemaphoreType.DMA(())],
        compiler_params=_cparams(dimension_semantics=("arbitrary",)),
        args=[dx1, xt, br0, lg, hf, hb, g, gt, wo])


def _conv_bwd(duvf, duvb, u, conv_w, name, sides=()):
    s = u.shape[0]
    tm = min(TM_LRU, s)
    cb = CB_LRU_FWD
    nt = s // tm

    def body(df_ref, dfp_ref, dfn_ref, db_ref, dbp_ref, dbn_ref, u_ref, cw_ref, du_ref, cst_ref):
        i = pl.program_id(1)

        @pl.when(i == 0)
        def _():
            cst_ref[...] = jnp.zeros_like(cst_ref)

        first, last = i == 0, i == nt - 1
        pz = jnp.where(first, 0.0, 1.0)
        nz = jnp.where(last, 0.0, 1.0)
        dout = df_ref[...].astype(F32) + db_ref[...].astype(F32)
        before = (dfp_ref[...].astype(F32) + dbp_ref[...].astype(F32))[H_HALO - SUBLANES:] * pz
        after = (dfn_ref[...].astype(F32) + dbn_ref[...].astype(F32))[:SUBLANES] * nz
        dm1, dp1, dp2 = _shifted(dout, before, after, [-1, 1, 2])
        cw = cw_ref[...]
        du_ref[...] = (dp2 * cw[0:1] + dp1 * cw[1:2] + dout * cw[2:3] + dm1 * cw[3:4]).astype(MXU_DTYPE)
        u_t = u_ref[...]
        cst_ref[0:1, :] += _rowsum(dp2 * u_t)
        cst_ref[1:2, :] += _rowsum(dp1 * u_t)
        cst_ref[2:3, :] += _rowsum(dout * u_t)
        cst_ref[3:4, :] += _rowsum(dm1 * u_t)
        cst_ref[4:5, :] += _rowsum(dout)

    tile, _, _ = _lru_specs(s, tm, cb, lambda i: i, nt)
    prev_map, next_map = _halo_maps(nt, tm, s // H_HALO, lambda i: i, rows=H_HALO)
    prev, nxt = pl.BlockSpec((H_HALO, cb), prev_map), pl.BlockSpec((H_HALO, cb), next_map)
    return _call_with_sides(
        body, sides, name=name,
        out_shape=[jax.ShapeDtypeStruct((s, D_INNER), MXU_DTYPE), jax.ShapeDtypeStruct((SUBLANES, D_INNER), F32)],
        grid=(D_INNER // cb, nt),
        in_specs=[tile, prev, nxt] * 2 + [tile, pl.BlockSpec((4, cb), lambda c, i: (0, c))],
        out_specs=[tile, pl.BlockSpec((SUBLANES, cb), lambda c, i: (0, c))], scratch_shapes=[],
        compiler_params=_cparams(dimension_semantics=("arbitrary", "arbitrary")),
        args=[duvf, duvf, duvf, duvb, duvb, duvb, u, conv_w])


def _bin(du, dg, xin, dzin, sc, sh, wg, name, gw_init=None, sides=()):
    t = xin.shape[0]
    tm = min(TM_MM, t)
    nt = t // tm
    has_g, has_dx, has_init = dg is not None, dzin is not None, gw_init is not None
    half = N_WBLK // 2
    n_blk = N_WBLK if has_g else half

    def body(*refs):
        refs = list(refs)
        du_ref = refs.pop(0)
        dg_ref = refs.pop(0) if has_g else None
        x_ref = refs.pop(0)
        dz_ref = refs.pop(0) if has_dx else None
        sc_ref, sh_ref, w_ref = refs.pop(0), refs.pop(0), refs.pop(0)
        init_hbm = refs.pop(0) if has_init else None
        dx_ref = refs.pop(0) if has_dx else None
        gw_hbm, st_ref, gw_acc, sem = refs
        i = pl.program_id(0)

        @pl.when(i == 0)
        def _():
            st_ref[...] = jnp.zeros_like(st_ref)
            first_zero = 0
            if has_init:
                _flush(init_hbm, gw_acc.at[pl.ds(0, half)], sem)
                first_zero = half
            for k in range(first_zero, n_blk):
                gw_acc[k] = jnp.zeros((D_MODEL, WBLK), F32)

        xv = x_ref[...]
        scale = 1.0 + sc_ref[...]
        h = (xv * scale + sh_ref[...]).astype(MXU_DTYPE)
        dh = None
        for k in range(n_blk):
            src = du_ref if k < half else dg_ref
            kk = k % half
            dk = src[:, kk * WBLK:(kk + 1) * WBLK]
            gw_acc[k] += _dot_tn(h, dk)
            contrib = _dot_nt(dk, w_ref[k])
            dh = contrib if dh is None else dh + contrib
        st_ref[0:1, :] += _rowsum(dh * xv)
        st_ref[1:2, :] += _rowsum(dh)
        if has_dx:
            dx_ref[...] = ALPHA * dz_ref[...] + dh * scale

        @pl.when(i == nt - 1)
        def _():
            _flush(gw_acc, gw_hbm, sem)

    wide = pl.BlockSpec((tm, D_INNER), lambda i: (i, 0))
    nar = pl.BlockSpec((tm, D_MODEL), lambda i: (i, 0))
    row = pl.BlockSpec((1, D_MODEL), lambda i: (0, 0))
    wspec = pl.BlockSpec((n_blk, D_MODEL, WBLK), lambda i: (0, 0, 0), pipeline_mode=pl.Buffered(1))
    in_specs = ([wide] + ([wide] if has_g else []) + [nar] + ([nar] if has_dx else []) + [row, row, wspec]
                + ([ANY] if has_init else []))
    args = ([du] + ([dg] if has_g else []) + [xin] + ([dzin] if has_dx else []) + [sc, sh, wg]
            + ([gw_init] if has_init else []))
    out_shape = ([jax.ShapeDtypeStruct((t, D_MODEL), F32)] if has_dx else []) + [
        jax.ShapeDtypeStruct((n_blk, D_MODEL, WBLK), F32), jax.ShapeDtypeStruct((SUBLANES, D_MODEL), F32)]
    out_specs = ([nar] if has_dx else []) + [ANY, pl.BlockSpec((SUBLANES, D_MODEL), lambda i: (0, 0))]
    return _call_with_sides(
        body, sides, name=name, out_shape=out_shape, grid=(nt,), in_specs=in_specs, out_specs=out_specs,
        scratch_shapes=[pltpu.VMEM((n_blk, D_MODEL, WBLK), F32), pltpu.SemaphoreType.DMA(())],
        compiler_params=_cparams(dimension_semantics=("arbitrary",)), args=args)


def _blocks_by_device(a, axis):
    shape = a.shape
    a = a.reshape(shape[:axis] + (N_DEV, shape[axis] // N_DEV) + shape[axis + 1:])
    return jnp.moveaxis(a, axis, 0)


def kernel(x, c, ctx, c_ctx, w_mod, b_mod, w_in, w_out, ln_g, ln_b, conv_w, conv_b, lru_wa, lru_ba, lru_wx, lru_bx, lru_lam, pool_w, pool_scale, loss_target, m_c_ctx, m_w_mod, m_b_mod, m_w_in, m_w_out, m_ln_g, m_ln_b, m_conv_w, m_conv_b, m_lru_wa, m_lru_ba, m_lru_wx, m_lru_bx, m_lru_lam, m_pool_w, m_pool_scale, v_c_ctx, v_w_mod, v_b_mod, v_w_in, v_w_out, v_ln_g, v_ln_b, v_conv_w, v_conv_b, v_lru_wa, v_lru_ba, v_lru_wx, v_lru_bx, v_lru_lam, v_pool_w, v_pool_scale):
    xi, yi, ci = _my_pos()
    dev = 4 * xi + 2 * yi + ci
    xt, ctxt, tgt = x[0], ctx[0], loss_target[0]
    n_mod = w_mod.shape[2]

    small_shapes = [(D_MODEL,), conv_w.shape[1:], lru_ba.shape[1:], lru_bx.shape[1:], lru_lam.shape[1:],
                    pool_scale.shape[1:]]
    small = _to_rows([c[0], conv_w[0], lru_ba[0], lru_bx[0], lru_lam[0], pool_scale[0]], SUBLANES)
    small_all, wi0 = _all_gather([small, w_in[0].astype(MXU_DTYPE)], "gather_first")
    pieces = [_split_rows(small_all[k], small_shapes) for k in range(N_DEV)]
    c_all = jnp.stack([p[0] for p in pieces])
    conv_w_f = jnp.concatenate([p[1] for p in pieces], axis=-1)
    lru_ba_f = jnp.concatenate([p[2] for p in pieces], axis=-1)[:, None, :]
    lru_bx_f = jnp.concatenate([p[3] for p in pieces], axis=-1)[:, None, :]
    lru_lam_f = jnp.concatenate([p[4] for p in pieces], axis=-1)[:, None, :]
    pool_scale_f = jnp.concatenate([p[5] for p in pieces], axis=-1)[None, :]

    cond = jnp.concatenate([c_all, jnp.broadcast_to(c_ctx[None, :], (N_DEV, D_MODEL))], axis=0)
    b_my = lax.dynamic_slice(b_mod, (0, dev * n_mod), (2, n_mod))[:, None, :]
    mod_part = _mod_fwd(cond, w_mod, b_my, "mod_fwd")
    mod_all, = _all_gather([mod_part], "gather_mod")
    mod = jnp.transpose(mod_all, (1, 2, 0, 3)).reshape(2, 16, 3 * D_MODEL)
    mod_me = lax.dynamic_slice(mod, (0, dev, 0), (2, 1, 3 * D_MODEL))
    sh = [mod_me[i, :, 0:D_MODEL] for i in range(2)]
    sc = [mod_me[i, :, D_MODEL:2 * D_MODEL] for i in range(2)]
    gt = [mod_me[i, :, 2 * D_MODEL:] for i in range(2)]
    shc, scc = mod[0, 8:9, 0:D_MODEL], mod[0, 8:9, D_MODEL:2 * D_MODEL]

    lg = [ln_g[i][None, :] for i in range(2)]
    lb = [ln_b[i][None, :] for i in range(2)]
    lru_p = dict(conv_w=conv_w_f, conv_b=conv_b, wa=(0.5 * lru_wa[0]).astype(MXU_DTYPE),
                 wx=(0.5 * lru_wx[0]).astype(MXU_DTYPE), ba=0.5 * lru_ba_f, bx=0.5 * lru_bx_f, lam=lru_lam_f)
    zero_state = jnp.zeros((1, D_INNER), F32)

    (u0, g0), (wo0,) = _in_proj(xt, sc[0], sh[0], wi0, "in_proj0", sides=[("gather", [w_out[0].astype(MXU_DTYPE)])])
    (uc, _), _ = _in_proj(ctxt, scc, shc, wi0, "in_proj0_ctx")
    (hcf, cf, uvc), _ = _lru_fwd(uc, zero_state, lru_p, 0, "lru_fwd_ctx_f", conv=True)
    (hcb, cbk), _ = _lru_fwd(uvc, zero_state, lru_p, 1, "lru_fwd_ctx_b", conv=False)
    (hf, _, uv0), (wi1,) = _lru_fwd(u0, cf, lru_p, 0, "lru_fwd_f", conv=True,
                                    sides=[("gather", [w_in[1].astype(MXU_DTYPE)])])
    (hb, _), (wo1, pool_w_g) = _lru_fwd(
        uv0, cbk, lru_p, 1, "lru_fwd_b", conv=False,
        sides=[("gather", [w_out[1].astype(MXU_DTYPE), pool_w[0].astype(MXU_DTYPE)])])
    w_in_l = [wi0, wi1]
    w_out_l = [wo0.reshape(D_INNER, D_MODEL), wo1.reshape(D_INNER, D_MODEL)]
    pool_w_f = jnp.transpose(pool_w_g, (1, 0, 2, 3)).reshape(len(POOL_WINDOWS), POOL_GROUP, POOL_GROUP)
    x1, br0, u1, g1 = _out0(hf, hb, g0, xt, gt[0], w_out_l[0], lg[0], lb[0], sc[1], sh[1], w_in_l[1], "out0_in1")
    dmix = _pool_mix(u1, False, MXU_DTYPE, "pool_fwd")
    dz1, st1, po1 = _out1(dmix, pool_w_f, pool_scale_f, g1, x1, gt[1], w_out_l[1], lg[1], lb[1], tgt, "out1")
    loss_me = jnp.full((1, LANES), (0.5 / D_MODEL) * jnp.sum(st1[3]), F32)

    core = jnp.reshape(ci, (1,)).astype(jnp.int32)
    wo_view = lambda a: a.reshape(N_DEV, D_INNER // N_DEV, D_MODEL)
    pw_view = lambda a: _blocks_by_device(a, 1).reshape(N_DEV, POOL_GROUP // N_DEV * len(POOL_WINDOWS), POOL_GROUP)
    dd, dg1, gwo1, gpw, gps = _bout1(dz1, dmix, po1, g1, pool_w_f, pool_scale_f, gt[1], w_out_l[1], "bwd_out1")
    du1 = _pool_mix(dd, True, MXU_DTYPE, "pool_bwd")
    (dx1, gwi1, stb1), _ = _bin(du1, dg1, x1, dz1, sc[1], sh[1], w_in_l[1], "bwd_in1")
    bufs1 = [gwi1, wo_view(gwo1), pw_view(gpw)]
    (dz0, dy0, dg0, gwo0, stl0), recv1 = _bout0(dx1, xt, br0, lg[0], hf, hb, g0, gt[0], w_out_l[0], "bwd_out0",
                                                sides=[("sibling", bufs1)])
    pairs1 = [_pair_sum(b, r, core, "reduce_pair_" + n)
              for b, r, n in zip(bufs1, recv1, ["w_in1", "w_out1", "pool_w"])]
    (duvf, gwa_f, gwx_f, gv_f, dh0f), (p_wi1, p_wo1, p_pw, recv_wo0) = _lru_bwd(
        uv0, dy0, hf, cf, zero_state, lru_p, 0, "lru_bwd_f", sides=[("chips", pairs1), ("sibling", [wo_view(gwo0)])])
    pair_wo0 = _pair_sum(wo_view(gwo0), recv_wo0, core, "reduce_pair_w_out0")
    (duvb, gwa_b, gwx_b, gv_b, dh0b), (p_wo0,) = _lru_bwd(
        uv0, dy0, hb, cbk, zero_state, lru_p, 1, "lru_bwd_b", sides=[("chips", [pair_wo0])])
    zero_dh = jnp.zeros(uc.shape, ACT_DTYPE)
    (ducf, gwa_cf, gwx_cf, gv_cf, _), _ = _lru_bwd(uvc, zero_dh, hcf, zero_state, dh0f, lru_p, 0, "lru_bwd_ctx_f")
    (ducb, gwa_cb, gwx_cb, gv_cb, _), _ = _lru_bwd(uvc, zero_dh, hcb, zero_state, dh0b, lru_p, 1, "lru_bwd_ctx_b")

    def pack(sharded, replicated):
        sh_sizes = [int(np.prod(a.shape[1:])) for a in sharded]
        rep_sizes = [a.shape[0] // N_DEV for a in replicated]
        n_flat = sum(sh_sizes) + sum(rep_sizes)
        rows = -(-(-(-n_flat // LANES)) // FLAT_ROWS) * FLAT_ROWS
        buf = jnp.concatenate([a.reshape(N_DEV, -1) for a in sharded + replicated], axis=1)
        return jnp.pad(buf, ((0, 0), (0, rows * LANES - n_flat))).reshape(N_DEV, rows, LANES), sh_sizes, rep_sizes

    def unpack(reduced, sh_sizes, rep_sizes, sh_shapes):
        flat = reduced.reshape(-1)
        offs = np.cumsum([0] + sh_sizes)
        mine = [flat[offs[k]:offs[k + 1]].reshape(s) for k, s in enumerate(sh_shapes)]
        return mine, _to_rows([flat[offs[-1]:offs[-1] + sum(rep_sizes)]], FLAT_ROWS)

    def spread(rep_all, rep_sizes, shapes):
        flat = rep_all.reshape(N_DEV, -1)
        offs = np.cumsum([0] + rep_sizes)
        return [flat[:, offs[k]:offs[k + 1]].reshape(s) for k, s in enumerate(shapes)]

    (du0, cst0), _ = _conv_bwd(duvf, duvb, u0, conv_w_f, "conv_bwd")
    (duc, cstc), _ = _conv_bwd(ducf, ducb, uc, conv_w_f, "conv_bwd_ctx")
    (gwic, stc), _ = _bin(duc, None, ctxt, None, scc, shc, w_in_l[0][:N_WBLK // 2], "bwd_in0_ctx")
    (gx, gwi0, stb0), _ = _bin(du0, dg0, xt, dz0, sc[0], sh[0], w_in_l[0], "bwd_in0", gw_init=gwic)

    zero_row = jnp.zeros((1, D_MODEL), F32)
    dm_me = jnp.stack([
        jnp.concatenate([jnp.concatenate([stb0[1:2], stb0[0:1], stl0[2:3]], axis=1),
                         jnp.concatenate([stc[1:2], stc[0:1], zero_row], axis=1)], axis=0),
        jnp.concatenate([jnp.concatenate([stb1[1:2], stb1[0:1], st1[2:3]], axis=1),
                         jnp.zeros((1, 3 * D_MODEL), F32)], axis=0)])
    dm_g, loss_g = _all_gather([dm_me, loss_me], "gather_dmod")
    loss = jnp.sum(loss_g[:, 0, 0])
    dm_all = jnp.concatenate([jnp.transpose(dm_g[:, :, 0], (1, 0, 2)), jnp.transpose(dm_g[:, :, 1], (1, 0, 2))],
                             axis=1)
    dm_my = lax.dynamic_slice(dm_all, (0, 0, dev * n_mod), (2, 16, n_mod))
    g_w_mod, g_b_mod, gcc_part = _mod_bwd(cond, dm_all, dm_my, w_mod, "mod_bwd")
    g_b_mod = g_b_mod.reshape(b_mod.shape)

    gwa = jnp.stack([gwa_f + gwa_cf, gwa_b + gwa_cb])
    gwx = jnp.stack([gwx_f + gwx_cf, gwx_b + gwx_cb])
    gv = jnp.stack([gv_f + gv_cf, gv_b + gv_cb])
    cst = cst0 + cstc
    misc, m_sh, m_rep = pack(
        [_blocks_by_device(cst[0:4], 1), _blocks_by_device(gv[:, 0], 1), _blocks_by_device(gv[:, 1], 1),
         _blocks_by_device(gv[:, 2], 1), _blocks_by_device(gps[0], 0)],
        [gwa.reshape(-1), gwx.reshape(-1), jnp.stack([stl0[0], st1[0]]).reshape(-1),
         jnp.stack([stl0[1], st1[1]]).reshape(-1), cst[4], gcc_part.reshape(-1)])
    bufs = [gwi0, misc]
    recvs = _sibling_exchange(bufs, "reduce_sibling")
    pairs = [_pair_sum(b, r, core, "reduce_pair_" + n) for b, r, n in zip(bufs, recvs, ["w_in0", "misc"])]
    p_wi0, p_misc = _chip_exchange(pairs, "reduce_chips")
    (g_conv_w, g_lru_ba, g_lru_bx, g_lru_lam, g_pool_scale), rep_mine = unpack(
        _sum4(p_misc, "reduce_sum_misc"), m_sh, m_rep,
        [conv_w.shape, lru_ba.shape, lru_bx.shape, lru_lam.shape, pool_scale.shape])
    rep_all, = _all_gather([rep_mine.astype(WIRE_DTYPE)], "gather_replicated")
    rep_all = rep_all.astype(F32)
    g_lru_wa, g_lru_wx, g_ln_g, g_ln_b, g_conv_b, g_c_ctx = spread(
        rep_all, m_rep, [lru_wa.shape, lru_wx.shape, ln_g.shape, ln_b.shape, conv_b.shape, c_ctx.shape])

    names = ["c_ctx", "w_mod", "b_mod", "w_in", "w_out", "ln_g", "ln_b", "conv_w", "conv_b", "lru_wa", "lru_ba",
             "lru_wx", "lru_bx", "lru_lam", "pool_w", "pool_scale"]
    weights = dict(c_ctx=c_ctx, w_mod=w_mod, b_mod=b_mod, w_in=w_in, w_out=w_out, ln_g=ln_g, ln_b=ln_b,
                   conv_w=conv_w, conv_b=conv_b, lru_wa=lru_wa, lru_ba=lru_ba, lru_wx=lru_wx, lru_bx=lru_bx,
                   lru_lam=lru_lam, pool_w=pool_w, pool_scale=pool_scale)
    mom_m = dict(c_ctx=m_c_ctx, w_mod=m_w_mod, b_mod=m_b_mod, w_in=m_w_in, w_out=m_w_out, ln_g=m_ln_g, ln_b=m_ln_b,
                 conv_w=m_conv_w, conv_b=m_conv_b, lru_wa=m_lru_wa, lru_ba=m_lru_ba, lru_wx=m_lru_wx,
                 lru_bx=m_lru_bx, lru_lam=m_lru_lam, pool_w=m_pool_w, pool_scale=m_pool_scale)
    mom_v = dict(c_ctx=v_c_ctx, w_mod=v_w_mod, b_mod=v_b_mod, w_in=v_w_in, w_out=v_w_out, ln_g=v_ln_g, ln_b=v_ln_b,
                 conv_w=v_conv_w, conv_b=v_conv_b, lru_wa=v_lru_wa, lru_ba=v_lru_ba, lru_wx=v_lru_wx,
                 lru_bx=v_lru_bx, lru_lam=v_lru_lam, pool_w=v_pool_w, pool_scale=v_pool_scale)
    grads = dict(c_ctx=g_c_ctx, w_mod=g_w_mod, b_mod=g_b_mod, ln_g=g_ln_g, ln_b=g_ln_b,
                 conv_w=g_conv_w, conv_b=g_conv_b, lru_wa=g_lru_wa, lru_ba=g_lru_ba, lru_wx=g_lru_wx,
                 lru_bx=g_lru_bx, lru_lam=g_lru_lam)
    grads["pool_scale"] = g_pool_scale
    delta, new_m, new_v = {}, {}, {}

    def update_parts(n, parts, view):
        res = _adamw_parts(weights[n].reshape(view), parts, mom_m[n].reshape(view), mom_v[n].reshape(view),
                           "adamw_" + n)
        grads[n], delta[n], new_m[n], new_v[n] = [r.reshape(weights[n].shape) for r in res]

    update_parts("w_in", [p_wi0, p_wi1], w_in.shape)
    update_parts("w_out", [p_wo0, p_wo1], w_out.shape)
    update_parts("pool_w", [p_pw], (1,) + p_pw.shape[1:])
    for n in ("w_mod", "lru_wa", "lru_wx"):
        shape = weights[n].shape
        view = (int(np.prod(shape[:-1])), shape[-1])
        res = _adamw(weights[n].reshape(view), grads[n].reshape(view), mom_m[n].reshape(view),
                     mom_v[n].reshape(view), "adamw_" + n)
        delta[n], new_m[n], new_v[n] = [r.reshape(shape) for r in res]

    small = [n for n in names if n not in delta]
    shapes = [weights[n].shape for n in small]
    flat = lambda d: _to_rows([d[n] for n in small], FLAT_ROWS)
    res = _adamw(flat(weights), flat(grads), flat(mom_m), flat(mom_v), "adamw_small")
    for d, r in zip((delta, new_m, new_v), res):
        d.update(zip(small, _split_rows(r, shapes)))

    return (loss, gx[None], *[grads[n] for n in names], *[delta[n] for n in names],
            *[new_m[n] for n in names], *[new_v[n] for n in names])
```
